```python
import jax, jax.numpy as jnp
from jax import lax
import numpy as np

D_MODEL = 2048
BATCH = 8
SEQ = 2048
DEPTH = 1

D_MIX = D_MODEL
D_CONV = D_MIX // 2
D_ATTN = D_MIX - D_CONV
HEAD_DIM = 64
N_HEADS = D_ATTN // HEAD_DIM
CONV_WIDTH = 3
DILATED_BRANCHES = ((128, 1), (512, 4), (2048, 16))
N_PROJ = 4 * D_CONV + 4 * D_ATTN
EPS = 1e-6
NEG_INF = -1e30

kernel_name = "hybrid_conv_dilated_attn_block"


def rms_norm(x, gain):
    xf = x.astype(jnp.float32)
    y = xf * lax.rsqrt(jnp.mean(xf * xf, axis=-1, keepdims=True) + EPS)
    return (y * gain.astype(jnp.float32)).astype(x.dtype)


def alibi_slopes(n_heads):
    return 2.0 ** (-8.0 * jnp.arange(1, n_heads + 1, dtype=jnp.float32) / n_heads)


def short_conv_centred(u, w, b):
    ch = u.shape[-1]
    pad = CONV_WIDTH // 2
    y = lax.conv_general_dilated(
        u, w[:, None, :].astype(u.dtype), window_strides=(1,), padding=((pad, pad),),
        dimension_numbers=("NWC", "WIO", "NWC"), feature_group_count=ch)
    return y + b.astype(u.dtype)


def dilated_branch(q, k, v, slopes, window, dilation):
    bsz, n_heads, seq, hd = q.shape
    r = dilation
    n = (window // 2) // r
    L = seq // r
    nb = -(-L // n)
    Lp = nb * n

    def to_res(t):
        return t.reshape(bsz, n_heads, L, r, hd).transpose(0, 1, 3, 2, 4)

    qb = jnp.pad(to_res(q), ((0, 0), (0, 0), (0, 0), (0, Lp - L), (0, 0)))
    qb = qb.reshape(bsz, n_heads, r, nb, n, hd)
    kv_pad = ((0, 0), (0, 0), (0, 0), (n, Lp - L + n), (0, 0))
    kb = jnp.pad(to_res(k), kv_pad).reshape(bsz, n_heads, r, nb + 2, n, hd)
    vb = jnp.pad(to_res(v), kv_pad).reshape(bsz, n_heads, r, nb + 2, n, hd)
    shifts = (slice(0, nb), slice(1, nb + 1), slice(2, nb + 2))

    s = jnp.concatenate(
        [jnp.einsum("bhrnqd,bhrnkd->bhrnqk", qb, kb[:, :, :, sl]) for sl in shifts],
        axis=-1).astype(jnp.float32) * (hd ** -0.5)

    qi = jnp.arange(n)
    ki = jnp.arange(3 * n)
    blk = jnp.arange(nb)
    off = ki[None, :] - n - qi[:, None]
    key_idx = blk[:, None] * n + ki[None, :] - n
    valid = (jnp.abs(off) <= n)[None] & ((key_idx >= 0) & (key_idx < L))[:, None, :]
    dist = (r * jnp.abs(off)).astype(jnp.float32)
    bias = -slopes[:, None, None] * dist[None]
    s = jnp.where(valid[None, None, None], s + bias[None, :, None, None], NEG_INF)

    m = jnp.max(s, axis=-1, keepdims=True)
    p = jnp.exp(s - m)
    den = jnp.sum(p, axis=-1, keepdims=True)
    o = jnp.einsum("bhrnqk,bhrnkd->bhrnqd", p[..., 0:n], vb[:, :, :, shifts[0]].astype(jnp.float32))
    o = o + jnp.einsum("bhrnqk,bhrnkd->bhrnqd", p[..., n:2 * n], vb[:, :, :, shifts[1]].astype(jnp.float32))
    o = o + jnp.einsum("bhrnqk,bhrnkd->bhrnqd", p[..., 2 * n:], vb[:, :, :, shifts[2]].astype(jnp.float32))
    o = o / den
    lse = (m + jnp.log(den))[..., 0]

    o = o.reshape(bsz, n_heads, r, Lp, hd)[:, :, :, :L]
    o = o.transpose(0, 1, 3, 2, 4).reshape(bsz, n_heads, seq, hd)
    lse = lse.reshape(bsz, n_heads, r, Lp)[..., :L]
    lse = lse.transpose(0, 1, 3, 2).reshape(bsz, n_heads, seq)
    return o, lse


def dilated_mixture_attention(q, k, v):
    slopes = alibi_slopes(q.shape[1])
    outs, lses = [], []
    for window, dilation in DILATED_BRANCHES:
        o, lse = dilated_branch(q, k, v, slopes, window, dilation)
        outs.append(o)
        lses.append(lse)
    alpha = jax.nn.softmax(jnp.stack(lses, axis=0), axis=0)
    return jnp.sum(alpha[..., None] * jnp.stack(outs, axis=0), axis=0)


def hybrid_layer(x, mod, g_pre, w_in, conv_w, conv_b, g_conv, g_attn, w_out, g_post):
    bsz, seq, _ = x.shape
    shift, scale, gate = jnp.split(mod, 3, axis=-1)
    h = rms_norm(x, g_pre) * (1.0 + scale[:, None, :]) + shift[:, None, :]

    proj = jnp.einsum("bsd,dn->bsn", h, w_in)
    cuts = np.cumsum([D_CONV, D_CONV, D_CONV, D_CONV, D_ATTN, D_ATTN, D_ATTN])
    u, b_gate, c_gate, z_c, q, k, v, z_a = jnp.split(proj, cuts, axis=-1)

    y_c = b_gate * short_conv_centred(c_gate * u, conv_w, conv_b)
    y_c = rms_norm(y_c, g_conv) * jax.nn.silu(z_c)

    def heads(t):
        return t.reshape(bsz, seq, N_HEADS, HEAD_DIM).transpose(0, 2, 1, 3)
    o = dilated_mixture_attention(heads(q), heads(k), heads(v))
    y_a = o.transpose(0, 2, 1, 3).reshape(bsz, seq, D_ATTN).astype(x.dtype)
    y_a = rms_norm(y_a, g_attn) * jax.nn.silu(z_a)

    y = jnp.einsum("bsn,nd->bsd", jnp.concatenate([y_c, y_a], axis=-1), w_out)
    return x + gate[:, None, :] * rms_norm(y, g_post)


def _fwd_setup_inputs(seed: int = 0) -> dict:
    key = jax.random.key(seed)
    ks = jax.random.split(key, 14)
    f32 = jnp.float32
    x = jax.random.normal(ks[0], (BATCH, SEQ, D_MODEL), f32)
    c = jax.random.normal(ks[1], (BATCH, D_MODEL), f32)
    w_ada = jax.random.normal(ks[2], (DEPTH, D_MODEL, 3 * D_MODEL), f32) * D_MODEL ** -0.5
    b_ada = 0.02 * jax.random.normal(ks[3], (DEPTH, 3 * D_MODEL), f32)
    g_pre = 1.0 + 0.05 * jax.random.normal(ks[4], (DEPTH, D_MODEL), f32)
    w_in = jax.random.normal(ks[5], (DEPTH, D_MODEL, N_PROJ), f32) * D_MODEL ** -0.5
    conv_w = jax.random.normal(ks[6], (DEPTH, CONV_WIDTH, D_CONV), f32) * CONV_WIDTH ** -0.5
    conv_b = 0.02 * jax.random.normal(ks[7], (DEPTH, D_CONV), f32)
    g_conv = 1.0 + 0.05 * jax.random.normal(ks[8], (DEPTH, D_CONV), f32)
    g_attn = 1.0 + 0.05 * jax.random.normal(ks[9], (DEPTH, D_ATTN), f32)
    w_out = jax.random.normal(ks[10], (DEPTH, D_MIX, D_MODEL), f32) * D_MIX ** -0.5
    g_post = 1.0 + 0.05 * jax.random.normal(ks[11], (DEPTH, D_MODEL), f32)
    return {"x": x, "c": c, "w_ada": w_ada, "b_ada": b_ada, "g_pre": g_pre,
            "w_in": w_in, "conv_w": conv_w, "conv_b": conv_b, "g_conv": g_conv,
            "g_attn": g_attn, "w_out": w_out, "g_post": g_post}


def _fwd_reference(x, c, w_ada, b_ada, g_pre, w_in, conv_w, conv_b, g_conv, g_attn, w_out, g_post):
    c_act = jax.nn.silu(c)
    for layer in range(DEPTH):
        mod = jnp.einsum("bd,dn->bn", c_act, w_ada[layer]) + b_ada[layer]
        x = hybrid_layer(x, mod, g_pre[layer], w_in[layer], conv_w[layer], conv_b[layer],
                         g_conv[layer], g_attn[layer], w_out[layer], g_post[layer])
    return x


import jax as _jax
import jax.numpy as _jnp

TWIN_FORMAT = 'train_step'
FWD_PARAMS = ['x', 'c', 'w_ada', 'b_ada', 'g_pre', 'w_in', 'conv_w', 'conv_b', 'g_conv', 'g_attn', 'w_out', 'g_post']
TWIN_WEIGHTS = ['w_ada', 'b_ada', 'g_pre', 'w_in', 'conv_w', 'conv_b', 'g_conv', 'g_attn', 'w_out', 'g_post']
TWIN_DIFF_INPUT = 'x'
TWIN_INPUTS = ['x', 'c', 'w_ada', 'b_ada', 'g_pre', 'w_in', 'conv_w', 'conv_b', 'g_conv', 'g_attn', 'w_out', 'g_post', 'loss_target', 'm_w_ada', 'm_b_ada', 'm_g_pre', 'm_w_in', 'm_conv_w', 'm_conv_b', 'm_g_conv', 'm_g_attn', 'm_w_out', 'm_g_post', 'v_w_ada', 'v_b_ada', 'v_g_pre', 'v_w_in', 'v_conv_w', 'v_conv_b', 'v_g_conv', 'v_g_attn', 'v_w_out', 'v_g_post']
TWIN_OUTPUTS = ['loss', 'grad_x', 'grad_w_ada', 'grad_b_ada', 'grad_g_pre', 'grad_w_in', 'grad_conv_w', 'grad_conv_b', 'grad_g_conv', 'grad_g_attn', 'grad_w_out', 'grad_g_post', 'delta_w_ada', 'delta_b_ada', 'delta_g_pre', 'delta_w_in', 'delta_conv_w', 'delta_conv_b', 'delta_g_conv', 'delta_g_attn', 'delta_w_out', 'delta_g_post', 'new_m_w_ada', 'new_m_b_ada', 'new_m_g_pre', 'new_m_w_in', 'new_m_conv_w', 'new_m_conv_b', 'new_m_g_conv', 'new_m_g_attn', 'new_m_w_out', 'new_m_g_post', 'new_v_w_ada', 'new_v_b_ada', 'new_v_g_pre', 'new_v_w_in', 'new_v_conv_w', 'new_v_conv_b', 'new_v_g_conv', 'new_v_g_attn', 'new_v_w_out', 'new_v_g_post']
TWIN_LEAF_KINDS = {'loss': 'loss', 'grad_x': 'grad_x', 'grad_w_ada': 'grad_w', 'grad_b_ada': 'grad_w', 'grad_g_pre': 'grad_w', 'grad_w_in': 'grad_w', 'grad_conv_w': 'grad_w', 'grad_conv_b': 'grad_w', 'grad_g_conv': 'grad_w', 'grad_g_attn': 'grad_w', 'grad_w_out': 'grad_w', 'grad_g_post': 'grad_w', 'delta_w_ada': 'delta_w', 'delta_b_ada': 'delta_w', 'delta_g_pre': 'delta_w', 'delta_w_in': 'delta_w', 'delta_conv_w': 'delta_w', 'delta_conv_b': 'delta_w', 'delta_g_conv': 'delta_w', 'delta_g_attn': 'delta_w', 'delta_w_out': 'delta_w', 'delta_g_post': 'delta_w', 'new_m_w_ada': 'new_m', 'new_m_b_ada': 'new_m', 'new_m_g_pre': 'new_m', 'new_m_w_in': 'new_m', 'new_m_conv_w': 'new_m', 'new_m_conv_b': 'new_m', 'new_m_g_conv': 'new_m', 'new_m_g_attn': 'new_m', 'new_m_w_out': 'new_m', 'new_m_g_post': 'new_m', 'new_v_w_ada': 'new_v', 'new_v_b_ada': 'new_v', 'new_v_g_pre': 'new_v', 'new_v_w_in': 'new_v', 'new_v_conv_w': 'new_v', 'new_v_conv_b': 'new_v', 'new_v_g_conv': 'new_v', 'new_v_g_attn': 'new_v', 'new_v_w_out': 'new_v', 'new_v_g_post': 'new_v'}


def _forward(args):
    return _fwd_reference(*[args[k] for k in FWD_PARAMS])


def _output_shape():
    out = _jax.eval_shape(lambda: _forward(_fwd_setup_inputs(0)))
    return out.shape, out.dtype

N_MICROBATCH = 1
ADAM_LR = 0.001
ADAM_B1 = 0.9
ADAM_B2 = 0.999
ADAM_EPS = 1e-08
ADAM_WD = 0.01
ADAM_STEP = 10
PER_EXAMPLE_BATCH_AXIS = {'x': 0, 'c': 0, 'loss_target': 0}
SHARED_INPUTS = []
_WEIGHT_DTYPES = {'w_ada': _jnp.float32, 'b_ada': _jnp.float32, 'g_pre': _jnp.float32, 'w_in': _jnp.float32, 'conv_w': _jnp.float32, 'conv_b': _jnp.float32, 'g_conv': _jnp.float32, 'g_attn': _jnp.float32, 'w_out': _jnp.float32, 'g_post': _jnp.float32}
MOMENT_SCALE = {'w_ada': 8.684978e-01, 'b_ada': 1.636497e+00, 'g_pre': 1.056158e-01, 'w_in': 1.128592e-01, 'conv_w': 7.510561e-02, 'conv_b': 8.058122e-02, 'g_conv': 1.012003e-01, 'g_attn': 2.793621e-01, 'w_out': 1.918789e-01, 'g_post': 3.536228e+00}


def _to_microbatches(a, axis):
    t = _jnp.moveaxis(a, axis, 0)
    t = t.reshape((N_MICROBATCH, t.shape[0] // N_MICROBATCH) + t.shape[1:])
    return _jnp.moveaxis(t, 1, axis + 1)


def setup_inputs(seed: int = 0) -> dict:
    inp = _fwd_setup_inputs(seed)
    key = _jax.random.fold_in(_jax.random.key(seed), 7919)
    shape, _ = _output_shape()
    out = dict(inp)
    out["loss_target"] = _jax.random.normal(_jax.random.fold_in(key, 0), shape, _jnp.float32)
    for i, name in enumerate(TWIN_WEIGHTS):
        w = inp[name].astype(_jnp.float32)
        if MOMENT_SCALE is None:
            s = _jnp.sqrt(_jnp.mean(_jnp.square(w)) + 1e-30)
        else:
            s = MOMENT_SCALE[name]
        km, kv = _jax.random.split(_jax.random.fold_in(key, i + 1))
        out[name] = w
        out["m_" + name] = s * _jax.random.normal(km, w.shape, _jnp.float32)
        out["v_" + name] = (s * s) * _jax.random.uniform(kv, w.shape, _jnp.float32, 0.5, 1.5)
    if N_MICROBATCH > 1:
        for name, axis in PER_EXAMPLE_BATCH_AXIS.items():
            out[name] = _to_microbatches(out[name], axis)
    return {'x': out['x'], 'c': out['c'], 'w_ada': out['w_ada'], 'b_ada': out['b_ada'], 'g_pre': out['g_pre'], 'w_in': out['w_in'], 'conv_w': out['conv_w'], 'conv_b': out['conv_b'], 'g_conv': out['g_conv'], 'g_attn': out['g_attn'], 'w_out': out['w_out'], 'g_post': out['g_post'], 'loss_target': out['loss_target'], 'm_w_ada': out['m_w_ada'], 'm_b_ada': out['m_b_ada'], 'm_g_pre': out['m_g_pre'], 'm_w_in': out['m_w_in'], 'm_conv_w': out['m_conv_w'], 'm_conv_b': out['m_conv_b'], 'm_g_conv': out['m_g_conv'], 'm_g_attn': out['m_g_attn'], 'm_w_out': out['m_w_out'], 'm_g_post': out['m_g_post'], 'v_w_ada': out['v_w_ada'], 'v_b_ada': out['v_b_ada'], 'v_g_pre': out['v_g_pre'], 'v_w_in': out['v_w_in'], 'v_conv_w': out['v_conv_w'], 'v_conv_b': out['v_conv_b'], 'v_g_conv': out['v_g_conv'], 'v_g_attn': out['v_g_attn'], 'v_w_out': out['v_w_out'], 'v_g_post': out['v_g_post']}


def _loss(weights, diff, rest, loss_target):
    with _jax.named_scope("forward"):
        args = {**rest, TWIN_DIFF_INPUT: diff, **{k: w.astype(_WEIGHT_DTYPES[k]) for k, w in weights.items()}}
        y = _forward(args)
    with _jax.named_scope("loss_head"):
        err = _jnp.square(y.astype(_jnp.float32) - loss_target)
        return 0.5 * _jnp.sum(_jnp.mean(err, axis=-1)) if err.ndim else 0.5 * err


def _adamw(w, g, m, v):
    m = ADAM_B1 * m + (1.0 - ADAM_B1) * g
    v = ADAM_B2 * v + (1.0 - ADAM_B2) * _jnp.square(g)
    m_hat = m / (1.0 - ADAM_B1 ** ADAM_STEP)
    v_hat = v / (1.0 - ADAM_B2 ** ADAM_STEP)
    delta = -ADAM_LR * (m_hat / (_jnp.sqrt(v_hat) + ADAM_EPS) + ADAM_WD * w)
    return delta, m, v


def reference(x, c, w_ada, b_ada, g_pre, w_in, conv_w, conv_b, g_conv, g_attn, w_out, g_post, loss_target, m_w_ada, m_b_ada, m_g_pre, m_w_in, m_conv_w, m_conv_b, m_g_conv, m_g_attn, m_w_out, m_g_post, v_w_ada, v_b_ada, v_g_pre, v_w_in, v_conv_w, v_conv_b, v_g_conv, v_g_attn, v_w_out, v_g_post):
    given = dict(x=x, c=c, w_ada=w_ada, b_ada=b_ada, g_pre=g_pre, w_in=w_in, conv_w=conv_w, conv_b=conv_b, g_conv=g_conv, g_attn=g_attn, w_out=w_out, g_post=g_post, loss_target=loss_target, m_w_ada=m_w_ada, m_b_ada=m_b_ada, m_g_pre=m_g_pre, m_w_in=m_w_in, m_conv_w=m_conv_w, m_conv_b=m_conv_b, m_g_conv=m_g_conv, m_g_attn=m_g_attn, m_w_out=m_w_out, m_g_post=m_g_post, v_w_ada=v_w_ada, v_b_ada=v_b_ada, v_g_pre=v_g_pre, v_w_in=v_w_in, v_conv_w=v_conv_w, v_conv_b=v_conv_b, v_g_conv=v_g_conv, v_g_attn=v_g_attn, v_w_out=v_w_out, v_g_post=v_g_post)
    weights = {n: given[n] for n in TWIN_WEIGHTS}
    shared = {n: given[n] for n in SHARED_INPUTS}
    per_example = {n: given[n] for n in ['x', 'c']}
    grad_fn = _jax.value_and_grad(_loss, argnums=(0, 1))

    def one_microbatch(ex, loss_target):
        ex = dict(ex)
        diff = ex.pop(TWIN_DIFF_INPUT)
        return grad_fn(weights, diff, {**shared, **ex}, loss_target)

    if N_MICROBATCH == 1:
        loss, (grad_w, grad_x) = one_microbatch(per_example, given["loss_target"])
    else:
        def body(carry, xs):
            loss_sum, grad_sum = carry
            l_k, (gw_k, gx_k) = one_microbatch(xs[0], xs[1])
            with _jax.named_scope("update"):
                return (loss_sum + l_k, _jax.tree.map(_jnp.add, grad_sum, gw_k)), gx_k

        init = (_jnp.zeros((), _jnp.float32), _jax.tree.map(_jnp.zeros_like, weights))
        (loss, grad_w), grad_x = _jax.lax.scan(body, init, (per_example, given["loss_target"]))
    with _jax.named_scope("update"):
        delta_w, new_m, new_v = {}, {}, {}
        for n in TWIN_WEIGHTS:
            delta_w[n], new_m[n], new_v[n] = _adamw(weights[n], grad_w[n], given["m_" + n], given["v_" + n])
    return (loss, grad_x, *[grad_w[n] for n in TWIN_WEIGHTS], *[delta_w[n] for n in TWIN_WEIGHTS],
            *[new_m[n] for n in TWIN_WEIGHTS], *[new_v[n] for n in TWIN_WEIGHTS])
```

```python
import functools

import jax
import jax.numpy as jnp
from jax import lax
from jax.experimental import pallas as pl
from jax.experimental.pallas import tpu as pltpu

f32 = jnp.float32
bf16 = jnp.bfloat16

NDEV = 8
HEAD_DIM = 64
PAIR = 2 * HEAD_DIM
BRANCHES = ((128, 1), (512, 4), (2048, 16))
HALF_WIN = 64
EPS = 1e-6
NEG_INF = -1e30
ADAM_LR, ADAM_B1, ADAM_B2, ADAM_EPS, ADAM_WD, ADAM_STEP = 0.001, 0.9, 0.999, 1e-08, 0.01, 10
MESH = pl.DeviceIdType.MESH
VMEM_LIMIT = 56 * 1024 * 1024
HBM_SPEC = pl.BlockSpec(memory_space=pltpu.HBM)


def _params(*sem):
    return pltpu.CompilerParams(dimension_semantics=sem or None, vmem_limit_bytes=VMEM_LIMIT)


def _silu(z):
    return z * jax.nn.sigmoid(z)


def _dsilu(z):
    s = jax.nn.sigmoid(z)
    return s * (1.0 + z * (1.0 - s))


def _my_place():
    x, y, c = lax.axis_index("x"), lax.axis_index("y"), lax.axis_index("c")
    return x, y, c, 4 * x + 2 * y + c


def _peer(x, y, c, k):
    px, py, pc = x ^ (k >> 2 & 1), y ^ (k >> 1 & 1), c ^ (k & 1)
    return (px, py, pc), 4 * px + 2 * py + pc


def _all_gather(arrays, name):
    n = len(arrays)

    def body(*refs):
        srcs, dsts = refs[:n], refs[n:2 * n]
        send_sems, recv_sems, local_sems = refs[2 * n:]
        x, y, c, me = _my_place()
        locals_, sends = [], []
        for t in range(n):
            own = pltpu.make_async_copy(srcs[t], dsts[t].at[me], local_sems.at[t])
            own.start()
            locals_.append(own)
            for k in range(1, NDEV):
                peer, pidx = _peer(x, y, c, k)
                cp = pltpu.make_async_remote_copy(
                    src_ref=srcs[t], dst_ref=dsts[t].at[me], send_sem=send_sems.at[t, k],
                    recv_sem=recv_sems.at[t, k], device_id=peer, device_id_type=MESH)
                cp.start()
                sends.append(cp)
        for t in range(n):
            for k in range(1, NDEV):
                peer, pidx = _peer(x, y, c, k)
                pltpu.make_async_remote_copy(
                    src_ref=srcs[t], dst_ref=dsts[t].at[pidx], send_sem=send_sems.at[t, k],
                    recv_sem=recv_sems.at[t, k], device_id=peer, device_id_type=MESH).wait_recv()
        for cp in sends:
            cp.wait_send()
        for cp in locals_:
            cp.wait()

    return pl.pallas_call(
        body, name=name,
        out_shape=tuple(jax.ShapeDtypeStruct((NDEV,) + a.shape, a.dtype) for a in arrays),
        in_specs=[HBM_SPEC] * n, out_specs=tuple([HBM_SPEC] * n),
        scratch_shapes=[pltpu.SemaphoreType.DMA((n, NDEV)), pltpu.SemaphoreType.DMA((n, NDEV)),
                        pltpu.SemaphoreType.DMA((n,))],
    )(*arrays)


def _all_to_all(arrays, name):
    n = len(arrays)

    def body(*refs):
        srcs, dsts = refs[:n], refs[n:2 * n]
        send_sems, recv_sems, local_sems = refs[2 * n:]
        x, y, c, me = _my_place()
        locals_, sends = [], []
        for t in range(n):
            own = pltpu.make_async_copy(srcs[t].at[me], dsts[t].at[me], local_sems.at[t])
            own.start()
            locals_.append(own)
            for k in range(1, NDEV):
                peer, pidx = _peer(x, y, c, k)
                cp = pltpu.make_async_remote_copy(
                    src_ref=srcs[t].at[pidx], dst_ref=dsts[t].at[me], send_sem=send_sems.at[t, k],
                    recv_sem=recv_sems.at[t, k], device_id=peer, device_id_type=MESH)
                cp.start()
                sends.append(cp)
        for t in range(n):
            for k in range(1, NDEV):
                peer, pidx = _peer(x, y, c, k)
                pltpu.make_async_remote_copy(
                    src_ref=srcs[t].at[pidx], dst_ref=dsts[t].at[pidx], send_sem=send_sems.at[t, k],
                    recv_sem=recv_sems.at[t, k], device_id=peer, device_id_type=MESH).wait_recv()
        for cp in sends:
            cp.wait_send()
        for cp in locals_:
            cp.wait()

    return pl.pallas_call(
        body, name=name,
        out_shape=tuple(jax.ShapeDtypeStruct(a.shape, a.dtype) for a in arrays),
        in_specs=[HBM_SPEC] * n, out_specs=tuple([HBM_SPEC] * n),
        scratch_shapes=[pltpu.SemaphoreType.DMA((n, NDEV)), pltpu.SemaphoreType.DMA((n, NDEV)),
                        pltpu.SemaphoreType.DMA((n,))],
    )(*arrays)


def _matmul(a, b, *, name, out_dtype, ta=False, tb=False, b_slots=False, out_slots=False,
            tm=512, tn=512, tk=512):
    M, K = (a.shape[1], a.shape[0]) if ta else a.shape
    if b_slots:
        slab = b.shape[2]
        N = b.shape[1] if tb else NDEV * slab
        assert (K if tb else N) == NDEV * slab
    else:
        N = b.shape[0] if tb else b.shape[1]
    tm, tn, tk = min(tm, M), min(tn, N), min(tk, K)
    if b_slots:
        if tb:
            tk = min(tk, slab)
        else:
            tn = min(tn, slab)
    if out_slots:
        tn = min(tn, N // NDEV)
    nm, nn, nk = M // tm, N // tn, K // tk
    assert (nm * tm, nn * tn, nk * tk) == (M, N, K), (name, M, N, K, tm, tn, tk)

    a_spec = pl.BlockSpec((tk, tm), lambda i, j, k: (k, i)) if ta else pl.BlockSpec((tm, tk), lambda i, j, k: (i, k))
    if b_slots and tb:
        per = slab // tk
        b_spec = pl.BlockSpec((None, tn, tk), lambda i, j, k: (k // per, j, k % per))
    elif b_slots:
        per = slab // tn
        b_spec = pl.BlockSpec((None, tk, tn), lambda i, j, k: (j // per, k, j % per))
    elif tb:
        b_spec = pl.BlockSpec((tn, tk), lambda i, j, k: (j, k))
    else:
        b_spec = pl.BlockSpec((tk, tn), lambda i, j, k: (k, j))
    if out_slots:
        per_o = (N // NDEV) // tn
        o_spec = pl.BlockSpec((None, tm, tn), lambda i, j, k: (j // per_o, i, j % per_o))
        out_shape = jax.ShapeDtypeStruct((NDEV, M, N // NDEV), out_dtype)
    else:
        o_spec = pl.BlockSpec((tm, tn), lambda i, j, k: (i, j))
        out_shape = jax.ShapeDtypeStruct((M, N), out_dtype)
    dims = (((0 if ta else 1,), (1 if tb else 0,)), ((), ()))

    def body(a_ref, b_ref, o_ref, acc_ref):
        k = pl.program_id(2)

        @pl.when(k == 0)
        def _():
            acc_ref[...] = jnp.zeros_like(acc_ref)

        acc_ref[...] += lax.dot_general(a_ref[...], b_ref[...], dims, preferred_element_type=f32)

        @pl.when(k == nk - 1)
        def _():
            o_ref[...] = acc_ref[...].astype(out_dtype)

    return pl.pallas_call(
        body, name=name, out_shape=out_shape, grid=(nm, nn, nk),
        in_specs=[a_spec, b_spec], out_specs=o_spec,
        scratch_shapes=[pltpu.VMEM((tm, tn), f32)],
        compiler_params=_params("parallel", "parallel", "arbitrary"),
    )(a, b)


def _silu_rows(c):
    D = c.shape[1]

    def body(c_ref, o_ref):
        o_ref[...] = jnp.broadcast_to(_silu(c_ref[...]), (8, D))

    return pl.pallas_call(body, name="silu_c", out_shape=jax.ShapeDtypeStruct((8, D), f32))(c)


def _ada_cols(c_all, w_ada, b_cols):
    D, W = w_ada.shape

    def body(c_ref, w_ref, b_ref, o_ref):
        mod = lax.dot_general(c_ref[...], w_ref[...], (((1,), (0,)), ((), ())), preferred_element_type=f32,
                              precision=lax.Precision.HIGHEST) + b_ref[...]
        for j in range(NDEV):
            o_ref[j] = jnp.broadcast_to(mod[j:j + 1, :], (8, W))

    return pl.pallas_call(body, name="ada_cols", out_shape=jax.ShapeDtypeStruct((NDEV, 8, W), f32),
                          compiler_params=_params())(c_all, w_ada, b_cols)


def _prenorm(x, scale, shift, g_pre, tr=256):
    S, D = x.shape
    tr = min(tr, S)

    def body(x_ref, sc_ref, sh_ref, g_ref, h_ref):
        xv = x_ref[...]
        r = lax.rsqrt(jnp.mean(xv * xv, axis=-1, keepdims=True) + EPS)
        h_ref[...] = ((xv * r) * g_ref[...] * (1.0 + sc_ref[...]) + sh_ref[...]).astype(bf16)

    row = pl.BlockSpec((tr, D), lambda i: (i, 0))
    vec = pl.BlockSpec((1, D), lambda i: (0, 0))
    return pl.pallas_call(body, name="prenorm", out_shape=jax.ShapeDtypeStruct((S, D), bf16), grid=(S // tr,),
                          in_specs=[row, vec, vec, vec], out_specs=row, compiler_params=_params("parallel"))(
                              x, scale, shift, g_pre)


def _ext_rows(i, tr, S):
    g = lax.broadcasted_iota(jnp.int32, (tr + 16, 1), 0) + (i * tr - 8)
    return (g >= 0) & (g < S)


def _halo_specs(tr, S, C, col):
    nb8 = S // 8
    main = pl.BlockSpec((tr, C), lambda i: (i, col))
    prev = pl.BlockSpec((8, C), lambda i: (jnp.maximum(i * (tr // 8) - 1, 0), col))
    nxt = pl.BlockSpec((8, C), lambda i: (jnp.minimum((i + 1) * (tr // 8), nb8 - 1), col))
    return prev, main, nxt


def _conv_fwd(proj, conv_w, conv_b, g_conv, tr=256):
    S, C = proj.shape[0], proj.shape[1] // 8
    tr = min(tr, S)

    def body(up, um, un, cp, cm, cn, bg_ref, zc_ref, w_ref, cb_ref, g_ref, o_ref):
        i = pl.program_id(0)
        exists = _ext_rows(i, tr, S)
        u = jnp.concatenate([up[...], um[...], un[...]], axis=0)
        cg = jnp.concatenate([cp[...], cm[...], cn[...]], axis=0)
        t = jnp.where(exists, cg * u, 0.0)
        t_before = pltpu.roll(t, 1, 0)[8:tr + 8]
        t_after = pltpu.roll(t, tr + 15, 0)[8:tr + 8]
        w = w_ref[...]
        cv = w[0:1] * t_before + w[1:2] * t[8:tr + 8] + w[2:3] * t_after + cb_ref[...]
        yc = bg_ref[...] * cv
        rc = lax.rsqrt(jnp.mean(yc * yc, axis=-1, keepdims=True) + EPS)
        o_ref[...] = ((yc * rc) * g_ref[...] * _silu(zc_ref[...])).astype(bf16)

    u_specs = _halo_specs(tr, S, C, 0)
    c_specs = _halo_specs(tr, S, C, 2)
    vec = pl.BlockSpec((1, C), lambda i: (0, 0))
    return pl.pallas_call(
        body, name="conv_fwd", out_shape=jax.ShapeDtypeStruct((S, 2 * C), bf16), grid=(S // tr,),
        in_specs=[*u_specs, *c_specs, pl.BlockSpec((tr, C), lambda i: (i, 1)), pl.BlockSpec((tr, C), lambda i: (i, 3)),
                  pl.BlockSpec((8, C), lambda i: (0, 0)), vec, vec],
        out_specs=pl.BlockSpec((tr, C), lambda i: (i, 0)), compiler_params=_params("parallel"),
    )(proj, proj, proj, proj, proj, proj, proj, proj, conv_w, conv_b, g_conv)


def _branch_geometry(S, r):
    L = S // r
    nq = min(128, L)
    nk = min(nq + 2 * HALF_WIN, L)
    return L, nq, nk, L // nq


def _block_rows(idx, r, L, nq, nk, nblk):
    if r == 1:
        rho, qb = 0, idx
    else:
        rho, qb = idx // nblk, idx % nblk
    i0 = qb * nq
    ws = jnp.clip(i0 - HALF_WIN, 0, L - nk)
    if r == 1:
        qrows = pl.ds(pl.multiple_of(i0, 8), nq)
        krows = pl.ds(pl.multiple_of(ws, 8), nk)
    else:
        qrows = pl.ds(rho + r * i0, nq, stride=r)
        krows = pl.ds(rho + r * ws, nk, stride=r)
    return qrows, krows, i0 - ws


def _band(nq, nk, off, r):
    d = jnp.abs(lax.broadcasted_iota(jnp.int32, (nq, nk), 0) - lax.broadcasted_iota(jnp.int32, (nq, nk), 1) + off)
    return d <= HALF_WIN, d.astype(f32) * float(r)


def _head_slopes(n_heads):
    slopes = 2.0 ** (-8.0 * jnp.arange(1, n_heads + 1, dtype=f32) / n_heads)
    return jnp.broadcast_to(jnp.repeat(slopes.reshape(n_heads // 2, 2), HEAD_DIM, axis=1)[:, None, :],
                            (n_heads // 2, 8, PAIR))


def _attn_fwd(proj, slopes):
    S, C = proj.shape[0], proj.shape[1] // 8
    npair = C // PAIR

    def body(q_ref, k_ref, v_ref, sl_ref, o_ref, lse_ref, m_scr, l_scr, a_scr):
        lane = lax.broadcasted_iota(jnp.int32, (1, PAIR), 1)
        first = lane < HEAD_DIM
        sl = sl_ref[...]
        slope = (sl[0:1, 0:1], sl[0:1, HEAD_DIM:HEAD_DIM + 1])

        for b, (_, r) in enumerate(BRANCHES):
            L, nq, nk, nblk = _branch_geometry(S, r)

            def step(idx, carry, b=b, r=r, L=L, nq=nq, nk=nk, nblk=nblk):
                qrows, krows, off = _block_rows(idx, r, L, nq, nk, nblk)
                q2 = q_ref[qrows, :]
                k2 = k_ref[krows, :].astype(bf16)
                v2 = v_ref[krows, :].astype(bf16)
                valid, dist = _band(nq, nk, off, r)
                ms, ls, accs = [], [], []
                for hh in range(2):
                    qh = jnp.where(first if hh == 0 else ~first, q2, 0.0).astype(bf16)
                    s = lax.dot_general(qh, k2, (((1,), (1,)), ((), ())), preferred_element_type=f32) * HEAD_DIM ** -0.5
                    s = jnp.where(valid, s - slope[hh] * dist, NEG_INF)
                    m = jnp.max(s, axis=-1, keepdims=True)
                    p = jnp.exp(s - m)
                    ms.append(m)
                    ls.append(jnp.sum(p, axis=-1, keepdims=True))
                    accs.append(jnp.dot(p.astype(bf16), v2, preferred_element_type=f32))
                m_scr[b, qrows, :] = jnp.where(first, ms[0], ms[1])
                l_scr[b, qrows, :] = jnp.where(first, ls[0], ls[1])
                a_scr[b, qrows, :] = jnp.where(first, accs[0], accs[1])
                return carry

            lax.fori_loop(0, S // nq, step, 0)

        ch = min(256, S)

        def merge(i, carry):
            rows = pl.ds(pl.multiple_of(i * ch, 8), ch)
            m = jnp.maximum(jnp.maximum(m_scr[0, rows, :], m_scr[1, rows, :]), m_scr[2, rows, :])
            l = jnp.zeros((ch, PAIR), f32)
            acc = jnp.zeros((ch, PAIR), f32)
            for b in range(3):
                w = jnp.exp(m_scr[b, rows, :] - m)
                l = l + w * l_scr[b, rows, :]
                acc = acc + w * a_scr[b, rows, :]
            o_ref[rows, :] = acc / l
            lse_ref[rows, :] = m + jnp.log(l)
            return carry

        lax.fori_loop(0, S // ch, merge, 0)

    blk = lambda part: pl.BlockSpec((S, PAIR), lambda p: (0, part * npair + p))
    out = pl.BlockSpec((S, PAIR), lambda p: (0, p))
    return pl.pallas_call(
        body, name="attn_fwd",
        out_shape=(jax.ShapeDtypeStruct((S, C), f32), jax.ShapeDtypeStruct((S, C), f32)), grid=(npair,),
        in_specs=[blk(4), blk(5), blk(6), pl.BlockSpec((None, 8, PAIR), lambda p: (p, 0, 0))],
        out_specs=(out, out),
        scratch_shapes=[pltpu.VMEM((3, S, PAIR), f32)] * 3, compiler_params=_params("parallel"),
    )(proj, proj, proj, slopes)


def _attn_post(ycat, o, proj, g_attn, tr=256):
    S, C = o.shape
    tr = min(tr, S)

    def body(y_ref, o_ref, z_ref, g_ref, out_ref):
        del y_ref
        ov = o_ref[...]
        ra = lax.rsqrt(jnp.mean(ov * ov, axis=-1, keepdims=True) + EPS)
        out_ref[...] = ((ov * ra) * g_ref[...] * _silu(z_ref[...])).astype(bf16)

    return pl.pallas_call(
        body, name="attn_post", out_shape=jax.ShapeDtypeStruct(ycat.shape, ycat.dtype), grid=(S // tr,),
        in_specs=[HBM_SPEC, pl.BlockSpec((tr, C), lambda i: (i, 0)), pl.BlockSpec((tr, C), lambda i: (i, 7)),
                  pl.BlockSpec((1, C), lambda i: (0, 0))],
        out_specs=pl.BlockSpec((tr, C), lambda i: (i, 1)), input_output_aliases={0: 0},
        compiler_params=_params("arbitrary"),
    )(ycat, o, proj, g_attn)


def _sandwich(y, x, target, gate, g_post, tr=256):
    S, D = y.shape
    tr = min(tr, S)

    def body(y_ref, x_ref, t_ref, gate_ref, g_ref, dy_ref, dout_ref, sums_ref):
        i = pl.program_id(0)
        yv = y_ref[...]
        rp = lax.rsqrt(jnp.mean(yv * yv, axis=-1, keepdims=True) + EPS)
        yhat = yv * rp
        yn = yhat * g_ref[...]
        err = (x_ref[...] + gate_ref[...] * yn) - t_ref[...]
        dout = err * (1.0 / D)
        dout_ref[...] = dout
        dyn = dout * gate_ref[...]
        w = dyn * g_ref[...]
        dy_ref[...] = (rp * (w - yhat * jnp.mean(w * yhat, axis=-1, keepdims=True))).astype(bf16)
        loss = 0.5 * jnp.sum(jnp.mean(err * err, axis=-1, keepdims=True), axis=0, keepdims=True)
        row = lax.broadcasted_iota(jnp.int32, (8, D), 0)
        upd = jnp.where(row == 0, jnp.sum(dout * yn, axis=0, keepdims=True),
                        jnp.where(row == 1, jnp.sum(dyn * yhat, axis=0, keepdims=True),
                                  jnp.where(row == 2, loss, 0.0)))

        @pl.when(i == 0)
        def _():
            sums_ref[...] = upd

        @pl.when(i > 0)
        def _():
            sums_ref[...] += upd

    row = pl.BlockSpec((tr, D), lambda i: (i, 0))
    vec = pl.BlockSpec((1, D), lambda i: (0, 0))
    return pl.pallas_call(
        body, name="sandwich",
        out_shape=(jax.ShapeDtypeStruct((S, D), bf16), jax.ShapeDtypeStruct((S, D), f32), jax.ShapeDtypeStruct((8, D), f32)),
        grid=(S // tr,), in_specs=[row, row, row, vec, vec],
        out_specs=(row, row, pl.BlockSpec((8, D), lambda i: (0, 0))), compiler_params=_params("arbitrary"),
    )(y, x, target, gate, g_post)


def _conv_bwd(proj, dycat, conv_w, conv_b, g_conv, tr=256):
    S, C = proj.shape[0], proj.shape[1] // 8
    tr = min(tr, S)
    n = tr + 16

    def body(*refs):
        ins, (w_ref, cb_ref, g_ref, dp_ref, sums_ref) = refs[:15], refs[15:]
        i = pl.program_id(0)
        exists = _ext_rows(i, tr, S)
        u, bg, cg, zc, dyn = (jnp.concatenate([ins[3 * t][...], ins[3 * t + 1][...], ins[3 * t + 2][...]], axis=0)
                              for t in range(5))
        w = w_ref[...]
        t = jnp.where(exists, cg * u, 0.0)
        t_before, t_after = pltpu.roll(t, 1, 0), pltpu.roll(t, n - 1, 0)
        cv = w[0:1] * t_before + w[1:2] * t + w[2:3] * t_after + cb_ref[...]
        yc = bg * cv
        rc = lax.rsqrt(jnp.mean(yc * yc, axis=-1, keepdims=True) + EPS)
        yhat = yc * rc
        sz = _silu(zc)
        wgt = dyn * g_ref[...] * sz
        dyc = rc * (wgt - yhat * jnp.mean(wgt * yhat, axis=-1, keepdims=True))
        dcv = jnp.where(exists, dyc * bg, 0.0)
        dt = w[0:1] * pltpu.roll(dcv, n - 1, 0) + w[1:2] * dcv + w[2:3] * pltpu.roll(dcv, 1, 0)
        mid = slice(8, tr + 8)
        dp_ref[:, 0:C] = (dt * cg)[mid].astype(bf16)
        dp_ref[:, C:2 * C] = (dyc * cv)[mid].astype(bf16)
        dp_ref[:, 2 * C:3 * C] = (dt * u)[mid].astype(bf16)
        dp_ref[:, 3 * C:4 * C] = (dyn * yhat * g_ref[...] * _dsilu(zc))[mid].astype(bf16)
        colsum = lambda v: jnp.sum(v[mid], axis=0, keepdims=True)
        parts = [colsum(dyn * yhat * sz), colsum(dcv), colsum(dcv * t_before), colsum(dcv * t), colsum(dcv * t_after)]
        row = lax.broadcasted_iota(jnp.int32, (8, C), 0)
        upd = jnp.zeros((8, C), f32)
        for j, pj in enumerate(parts):
            upd = jnp.where(row == j, pj, upd)

        @pl.when(i == 0)
        def _():
            sums_ref[...] = upd

        @pl.when(i > 0)
        def _():
            sums_ref[...] += upd

    specs = []
    for col in range(4):
        specs += _halo_specs(tr, S, C, col)
    specs += _halo_specs(tr, S, C, 0)
    vec = pl.BlockSpec((1, C), lambda i: (0, 0))
    return pl.pallas_call(
        body, name="conv_bwd",
        out_shape=(jax.ShapeDtypeStruct((S, 4 * C), bf16), jax.ShapeDtypeStruct((8, C), f32)), grid=(S // tr,),
        in_specs=[*specs, pl.BlockSpec((8, C), lambda i: (0, 0)), vec, vec],
        out_specs=(pl.BlockSpec((tr, 4 * C), lambda i: (i, 0)), pl.BlockSpec((8, C), lambda i: (0, 0))),
        compiler_params=_params("arbitrary"),
    )(*([proj] * 12), dycat, dycat, dycat, conv_w, conv_b, g_conv)


def _attn_post_bwd(o, proj, dycat, g_attn, tr=256):
    S, C = o.shape
    tr = min(tr, S)

    def body(o_ref, z_ref, dy_ref, g_ref, do_ref, dz_ref, sums_ref):
        i = pl.program_id(0)
        ov, zv, dyn = o_ref[...], z_ref[...], dy_ref[...]
        ra = lax.rsqrt(jnp.mean(ov * ov, axis=-1, keepdims=True) + EPS)
        ohat = ov * ra
        sz = _silu(zv)
        wgt = dyn * g_ref[...] * sz
        do_ref[...] = ra * (wgt - ohat * jnp.mean(wgt * ohat, axis=-1, keepdims=True))
        dz_ref[...] = (dyn * ohat * g_ref[...] * _dsilu(zv)).astype(bf16)
        row = lax.broadcasted_iota(jnp.int32, (8, C), 0)
        upd = jnp.where(row == 0, jnp.sum(dyn * ohat * sz, axis=0, keepdims=True), 0.0)

        @pl.when(i == 0)
        def _():
            sums_ref[...] = upd

        @pl.when(i > 0)
        def _():
            sums_ref[...] += upd

    return pl.pallas_call(
        body, name="attn_post_bwd",
        out_shape=(jax.ShapeDtypeStruct((S, C), f32), jax.ShapeDtypeStruct((S, C), bf16), jax.ShapeDtypeStruct((8, C), f32)),
        grid=(S // tr,),
        in_specs=[pl.BlockSpec((tr, C), lambda i: (i, 0)), pl.BlockSpec((tr, C), lambda i: (i, 7)),
                  pl.BlockSpec((tr, C), lambda i: (i, 1)), pl.BlockSpec((1, C), lambda i: (0, 0))],
        out_specs=(pl.BlockSpec((tr, C), lambda i: (i, 0)), pl.BlockSpec((tr, C), lambda i: (i, 0)),
                   pl.BlockSpec((8, C), lambda i: (0, 0))),
        compiler_params=_params("arbitrary"),
    )(o, proj, dycat, g_attn)


def _attn_bwd(proj, o, do, lse, slopes):
    S, C = o.shape
    npair = C // PAIR

    def body(q_ref, k_ref, v_ref, o_ref, do_ref, lse_ref, sl_ref, dq_ref, dk_ref, dv_ref, dq_scr, dk_scr, dv_scr, dl_scr):
        lane = lax.broadcasted_iota(jnp.int32, (1, PAIR), 1)
        first = lane < HEAD_DIM
        sl = sl_ref[...]
        slope = (sl[0:1, 0:1], sl[0:1, HEAD_DIM:HEAD_DIM + 1])
        ch = min(256, S)

        def prep(i, carry):
            rows = pl.ds(pl.multiple_of(i * ch, 8), ch)
            prod = do_ref[rows, :] * o_ref[rows, :]
            d0 = jnp.sum(jnp.where(first, prod, 0.0), axis=-1, keepdims=True)
            d1 = jnp.sum(jnp.where(first, 0.0, prod), axis=-1, keepdims=True)
            dl_scr[rows, :] = jnp.where(first, d0, d1)
            zero = jnp.zeros((ch, PAIR), f32)
            dq_scr[rows, :] = zero
            dk_scr[rows, :] = zero
            dv_scr[rows, :] = zero
            return carry

        lax.fori_loop(0, S // ch, prep, 0)

        for _, r in BRANCHES:
            L, nq, nk, nblk = _branch_geometry(S, r)

            def step(idx, carry, r=r, L=L, nq=nq, nk=nk, nblk=nblk):
                qrows, krows, off = _block_rows(idx, r, L, nq, nk, nblk)
                q2 = q_ref[qrows, :]
                k2 = k_ref[krows, :].astype(bf16)
                v2 = v_ref[krows, :].astype(bf16)
                do2 = do_ref[qrows, :]
                lse2 = lse_ref[qrows, :]
                dl2 = dl_scr[qrows, :]
                valid, dist = _band(nq, nk, off, r)
                dq2 = jnp.zeros((nq, PAIR), f32)
                dk2 = jnp.zeros((nk, PAIR), f32)
                dv2 = jnp.zeros((nk, PAIR), f32)
                for hh in range(2):
                    mine = first if hh == 0 else ~first
                    lo = hh * HEAD_DIM
                    qh = jnp.where(mine, q2, 0.0).astype(bf16)
                    doh = jnp.where(mine, do2, 0.0).astype(bf16)
                    s = lax.dot_general(qh, k2, (((1,), (1,)), ((), ())), preferred_element_type=f32) * HEAD_DIM ** -0.5
                    s = jnp.where(valid, s - slope[hh] * dist, NEG_INF)
                    p = jnp.exp(s - lse2[:, lo:lo + 1])
                    dv2 = dv2 + lax.dot_general(p.astype(bf16), doh, (((0,), (0,)), ((), ())), preferred_element_type=f32)
                    dp = lax.dot_general(doh, v2, (((1,), (1,)), ((), ())), preferred_element_type=f32)
                    ds = (p * (dp - dl2[:, lo:lo + 1]) * HEAD_DIM ** -0.5).astype(bf16)
                    dq2 = dq2 + jnp.where(mine, jnp.dot(ds, k2, preferred_element_type=f32), 0.0)
                    dk2 = dk2 + lax.dot_general(ds, qh, (((0,), (0,)), ((), ())), preferred_element_type=f32)
                dq_scr[qrows, :] = dq_scr[qrows, :] + dq2
                dk_scr[krows, :] = dk_scr[krows, :] + dk2
                dv_scr[krows, :] = dv_scr[krows, :] + dv2
                return carry

            lax.fori_loop(0, S // nq, step, 0)

        dq_ref[...] = dq_scr[...].astype(bf16)
        dk_ref[...] = dk_scr[...].astype(bf16)
        dv_ref[...] = dv_scr[...].astype(bf16)

    blk = lambda part: pl.BlockSpec((S, PAIR), lambda p: (0, part * npair + p))
    own = pl.BlockSpec((S, PAIR), lambda p: (0, p))
    grad = jax.ShapeDtypeStruct((S, C), bf16)
    return pl.pallas_call(
        body, name="attn_bwd", out_shape=(grad, grad, grad), grid=(npair,),
        in_specs=[blk(4), blk(5), blk(6), own, own, own, pl.BlockSpec((None, 8, PAIR), lambda p: (p, 0, 0))],
        out_specs=(own, own, own), scratch_shapes=[pltpu.VMEM((S, PAIR), f32)] * 4,
        compiler_params=_params("parallel"),
    )(proj, proj, proj, o, do, lse, slopes)


def _prenorm_bwd(dh, x, dout, scale, g_pre, tr=256):
    S, D = x.shape
    tr = min(tr, S)

    def body(dh_ref, x_ref, dout_ref, sc_ref, g_ref, gx_ref, sums_ref):
        i = pl.program_id(0)
        xv, dhv = x_ref[...], dh_ref[...]
        r = lax.rsqrt(jnp.mean(xv * xv, axis=-1, keepdims=True) + EPS)
        xn = xv * r
        dxn = dhv * (g_ref[...] * (1.0 + sc_ref[...]))
        gx_ref[...] = dout_ref[...] + r * (dxn - xn * jnp.mean(dxn * xn, axis=-1, keepdims=True))
        dhx = dhv * xn
        row = lax.broadcasted_iota(jnp.int32, (8, D), 0)
        upd = jnp.where(row == 0, jnp.sum(dhv, axis=0, keepdims=True),
                        jnp.where(row == 1, jnp.sum(dhx, axis=0, keepdims=True) * g_ref[...],
                                  jnp.where(row == 2, jnp.sum(dhx, axis=0, keepdims=True) * (1.0 + sc_ref[...]), 0.0)))

        @pl.when(i == 0)
        def _():
            sums_ref[...] = upd

        @pl.when(i > 0)
        def _():
            sums_ref[...] += upd

    row = pl.BlockSpec((tr, D), lambda i: (i, 0))
    vec = pl.BlockSpec((1, D), lambda i: (0, 0))
    return pl.pallas_call(
        body, name="prenorm_bwd",
        out_shape=(jax.ShapeDtypeStruct((S, D), f32), jax.ShapeDtypeStruct((8, D), f32)), grid=(S // tr,),
        in_specs=[row, row, row, vec, vec], out_specs=(row, pl.BlockSpec((8, D), lambda i: (0, 0))),
        compiler_params=_params("arbitrary"),
    )(dh, x, dout, scale, g_pre)


def _adamw(w, g, m, v):
    m = ADAM_B1 * m + (1.0 - ADAM_B1) * g
    v = ADAM_B2 * v + (1.0 - ADAM_B2) * (g * g)
    m_hat = m / (1.0 - ADAM_B1 ** ADAM_STEP)
    v_hat = v / (1.0 - ADAM_B2 ** ADAM_STEP)
    delta = -ADAM_LR * (m_hat / (jnp.sqrt(v_hat) + ADAM_EPS) + ADAM_WD * w)
    return delta, m, v


def _sum_rows(parts):
    P = parts.shape[1]

    def body(p_ref, o_ref):
        acc = p_ref[0:1, :]
        for j in range(1, NDEV):
            acc = acc + p_ref[j:j + 1, :]
        o_ref[...] = jnp.broadcast_to(acc, (8, P))

    return pl.pallas_call(body, name="sum_small", out_shape=jax.ShapeDtypeStruct((8, P), f32),
                          compiler_params=_params())(parts)


def _adamw_small(w, g, m, v):
    def body(w_ref, g_ref, m_ref, v_ref, d_ref, nm_ref, nv_ref):
        d_ref[...], nm_ref[...], nv_ref[...] = _adamw(w_ref[...], g_ref[...], m_ref[...], v_ref[...])

    out = jax.ShapeDtypeStruct(w.shape, f32)
    return pl.pallas_call(body, name="adamw_small", out_shape=(out, out, out), compiler_params=_params())(w, g, m, v)


def _adamw_sharded(parts, w, m, v, name, tr=128):
    R, Cc = w.shape
    tr = min(tr, R)

    def body(p_ref, w_ref, m_ref, v_ref, g_ref, d_ref, nm_ref, nv_ref):
        g = p_ref[0].astype(f32)
        for j in range(1, NDEV):
            g = g + p_ref[j].astype(f32)
        g_ref[...] = g
        d_ref[...], nm_ref[...], nv_ref[...] = _adamw(w_ref[...], g, m_ref[...], v_ref[...])

    row = pl.BlockSpec((tr, Cc), lambda i: (i, 0))
    out = jax.ShapeDtypeStruct((R, Cc), f32)
    return pl.pallas_call(
        body, name=name, out_shape=(out, out, out, out), grid=(R // tr,),
        in_specs=[pl.BlockSpec((NDEV, tr, Cc), lambda i: (0, i, 0)), row, row, row], out_specs=(row, row, row, row),
        compiler_params=_params("parallel"),
    )(parts, w, m, v)


def _adamw_ada(c_t, dmod_cols, w, m, v, tr=256):
    D, W = w.shape
    tr = min(tr, D)

    def body(c_ref, dm_ref, w_ref, m_ref, v_ref, g_ref, d_ref, nm_ref, nv_ref):
        cv, dm = c_ref[...], dm_ref[...]
        g = cv[:, 0:1] * dm[0:1, :]
        for b in range(1, NDEV):
            g = g + cv[:, b:b + 1] * dm[b:b + 1, :]
        g_ref[...] = g
        d_ref[...], nm_ref[...], nv_ref[...] = _adamw(w_ref[...], g, m_ref[...], v_ref[...])

    row = pl.BlockSpec((tr, W), lambda i: (i, 0))
    out = jax.ShapeDtypeStruct((D, W), f32)
    return pl.pallas_call(
        body, name="adamw_ada", out_shape=(out, out, out, out), grid=(D // tr,),
        in_specs=[pl.BlockSpec((tr, NDEV), lambda i: (i, 0)), pl.BlockSpec((NDEV, W), lambda i: (0, 0)), row, row, row],
        out_specs=(row, row, row, row), compiler_params=_params("parallel"),
    )(c_t, dmod_cols, w, m, v)


def kernel(x, c, w_ada, b_ada, g_pre, w_in, conv_w, conv_b, g_conv, g_attn, w_out, g_post, loss_target, m_w_ada, m_b_ada, m_g_pre, m_w_in, m_conv_w, m_conv_b, m_g_conv, m_g_attn, m_w_out, m_g_post, v_w_ada, v_b_ada, v_g_pre, v_w_in, v_conv_w, v_conv_b, v_g_conv, v_g_attn, v_w_out, v_g_post):
    S, D = x.shape[1], x.shape[2]
    C = D // 2
    W = w_ada.shape[2]
    CW = conv_w.shape[2]
    me = 4 * lax.axis_index("x") + 2 * lax.axis_index("y") + lax.axis_index("c")
    x2, tgt = x[0], loss_target[0]
    w_ada2, w_in2, w_out2 = w_ada[0], w_in[0], w_out[0]

    (c_rows,) = _all_gather([_silu_rows(c)], "gather_c")
    c_all = c_rows[:, 0, :]
    b_cols = lax.dynamic_slice_in_dim(b_ada, me * W, W, axis=1)
    (mod_slabs,) = _all_to_all([_ada_cols(c_all, w_ada2, b_cols)], "scatter_mod")
    mod = mod_slabs[:, 0, :].reshape(1, 3 * D)
    shift, scale, gate = mod[:, :D], mod[:, D:2 * D], mod[:, 2 * D:]

    cw_slab = jnp.zeros((8, CW), f32).at[:3].set(conv_w[0])
    win_g, wout_g, cw_g = _all_gather([w_in2.astype(bf16), w_out2.astype(bf16), cw_slab], "gather_weights")
    wout_full = wout_g.reshape(D, D)
    conv_w_full = jnp.transpose(cw_g, (1, 0, 2)).reshape(8, C)

    h = _prenorm(x2, scale, shift, g_pre)
    proj = _matmul(h, win_g, name="in_proj", out_dtype=f32, b_slots=True, tm=1024, tn=1024, tk=512)
    slopes = _head_slopes(C // HEAD_DIM)
    ycat = _conv_fwd(proj, conv_w_full, conv_b, g_conv)
    o, lse = _attn_fwd(proj, slopes)
    ycat = _attn_post(ycat, o, proj, g_attn)
    y = _matmul(ycat, wout_full, name="out_proj", out_dtype=f32, tm=1024, tn=1024, tk=512)
    dy, dout, post_sums = _sandwich(y, x2, tgt, gate, g_post)

    dycat = _matmul(dy, wout_full, name="out_proj_dx", out_dtype=f32, tb=True, tm=1024, tn=1024, tk=512)
    gw_out = _matmul(ycat, dy, name="out_proj_dw", out_dtype=bf16, ta=True, tm=512, tn=1024, tk=512)
    dpc, conv_sums = _conv_bwd(proj, dycat, conv_w_full, conv_b, g_conv)
    do, dza, attn_sums = _attn_post_bwd(o, proj, dycat, g_attn)
    dq, dk, dv = _attn_bwd(proj, o, do, lse, slopes)
    dproj = jnp.concatenate([dpc, dq, dk, dv, dza], axis=1)
    dh = _matmul(dproj, win_g, name="in_proj_dx", out_dtype=f32, tb=True, b_slots=True, tm=1024, tn=1024, tk=512)
    gw_in = _matmul(h, dproj, name="in_proj_dw", out_dtype=bf16, ta=True, out_slots=True, tm=1024, tn=1024, tk=512)
    grad_x, pre_sums = _prenorm_bwd(dh, x2, dout, scale, g_pre)

    gw_in_parts, gw_out_parts = _all_to_all([gw_in, gw_out.reshape(NDEV, D // NDEV, D)], "scatter_grads")
    small = jnp.concatenate([pre_sums[0:1], pre_sums[1:2], post_sums[0:1],
                             pre_sums[2:3], post_sums[1:2],
                             conv_sums[2:3], conv_sums[3:4], conv_sums[4:5],
                             conv_sums[1:2], conv_sums[0:1], attn_sums[0:1]], axis=1)
    (small_all,) = _all_gather([jnp.broadcast_to(small, (8, 8 * D))], "gather_small")
    small_all = small_all[:, 0, :]
    tot = _sum_rows(small_all)[0:1]
    loss = lax.psum(post_sums[2, 0], ("x", "y", "c"))

    g_b_ada = tot[:, :3 * D]
    g_g_pre, g_g_post = tot[:, 3 * D:4 * D], tot[:, 4 * D:5 * D]
    g_conv_w_full = tot[:, 5 * D:5 * D + 3 * C].reshape(3, C)
    g_conv_w = lax.dynamic_slice_in_dim(g_conv_w_full, me * CW, CW, axis=1)[None]
    g_conv_b, g_g_conv, g_g_attn = (tot[:, 5 * D + (3 + t) * C:5 * D + (4 + t) * C] for t in range(3))

    g_w_in, d_w_in, nm_w_in, nv_w_in = _adamw_sharded(gw_in_parts, w_in2, m_w_in[0], v_w_in[0], "adamw_w_in")
    g_w_out, d_w_out, nm_w_out, nv_w_out = _adamw_sharded(gw_out_parts, w_out2, m_w_out[0], v_w_out[0], "adamw_w_out")
    dmod_cols = lax.dynamic_slice_in_dim(small_all[:, :3 * D], me * W, W, axis=1)
    g_w_ada, d_w_ada, nm_w_ada, nv_w_ada = _adamw_ada(c_all.T, dmod_cols, w_ada2, m_w_ada[0], v_w_ada[0])

    pack = lambda *vs: jnp.concatenate([a.reshape(1, -1) for a in vs], axis=1)
    smalls = [(b_ada, g_b_ada, m_b_ada, v_b_ada), (g_pre, g_g_pre, m_g_pre, v_g_pre),
              (conv_w, g_conv_w, m_conv_w, v_conv_w), (conv_b, g_conv_b, m_conv_b, v_conv_b),
              (g_conv, g_g_conv, m_g_conv, v_g_conv), (g_attn, g_g_attn, m_g_attn, v_g_attn),
              (g_post, g_g_post, m_g_post, v_g_post)]
    packed = [pack(*[s[t] for s in smalls]) for t in range(4)]
    npad = -packed[0].shape[1] % 128
    packed = [jnp.pad(p, ((0, 0), (0, npad)), constant_values=1.0) for p in packed]
    d_s, nm_s, nv_s = _adamw_small(*packed)

    def unpack(vec):
        out, at = [], 0
        for s in smalls:
            n = s[0].size
            out.append(vec[:, at:at + n].reshape(s[0].shape))
            at += n
        return out

    d_b_ada, d_g_pre, d_conv_w, d_conv_b, d_g_conv, d_g_attn, d_g_post = unpack(d_s)
    nm_b_ada, nm_g_pre, nm_conv_w, nm_conv_b, nm_g_conv, nm_g_attn, nm_g_post = unpack(nm_s)
    nv_b_ada, nv_g_pre, nv_conv_w, nv_conv_b, nv_g_conv, nv_g_attn, nv_g_post = unpack(nv_s)

    return (loss, grad_x[None],
            g_w_ada[None], g_b_ada, g_g_pre, g_w_in[None], g_conv_w, g_conv_b, g_g_conv, g_g_attn, g_w_out[None], g_g_post,
            d_w_ada[None], d_b_ada, d_g_pre, d_w_in[None], d_conv_w, d_conv_b, d_g_conv, d_g_attn, d_w_out[None], d_g_post,
            nm_w_ada[None], nm_b_ada, nm_g_pre, nm_w_in[None], nm_conv_w, nm_conv_b, nm_g_conv, nm_g_attn, nm_w_out[None], nm_g_post,
            nv_w_ada[None], nv_b_ada, nv_g_pre, nv_w_in[None], nv_conv_w, nv_conv_b, nv_g_conv, nv_g_attn, nv_w_out[None], nv_g_post)
```

```python
import functools

import jax
import jax.numpy as jnp
from jax import lax
from jax.experimental import pallas as pl
from jax.experimental.pallas import tpu as pltpu

f32 = jnp.float32
bf16 = jnp.bfloat16

NDEV = 8
HEAD_DIM = 64
PAIR = 2 * HEAD_DIM
BRANCHES = ((128, 1), (512, 4), (2048, 16))
HALF_WIN = 64
EPS = 1e-6
NEG_INF = -1e30
ADAM_LR, ADAM_B1, ADAM_B2, ADAM_EPS, ADAM_WD, ADAM_STEP = 0.001, 0.9, 0.999, 1e-08, 0.01, 10
MESH = pl.DeviceIdType.MESH
VMEM_LIMIT = 56 * 1024 * 1024
HBM_SPEC = pl.BlockSpec(memory_space=pltpu.HBM)
ANY_SPEC = pl.BlockSpec(memory_space=pl.ANY)
SEM_SPEC = pl.BlockSpec(memory_space=pltpu.SEMAPHORE)


def _params(*sem):
    return pltpu.CompilerParams(dimension_semantics=sem or None, vmem_limit_bytes=VMEM_LIMIT)


def _silu(z):
    return z * jax.nn.sigmoid(z)


def _dsilu(z):
    s = jax.nn.sigmoid(z)
    return s * (1.0 + z * (1.0 - s))


def _my_place():
    x, y, c = lax.axis_index("x"), lax.axis_index("y"), lax.axis_index("c")
    return x, y, c, 4 * x + 2 * y + c


def _peer(x, y, c, k):
    px, py, pc = x ^ (k >> 2 & 1), y ^ (k >> 1 & 1), c ^ (k & 1)
    return (px, py, pc), 4 * px + 2 * py + pc


def _all_gather(arrays, name):
    n = len(arrays)

    def body(*refs):
        srcs, dsts = refs[:n], refs[n:2 * n]
        send_sems, recv_sems, local_sems = refs[2 * n:]
        x, y, c, me = _my_place()
        locals_, sends = [], []
        for t in range(n):
            own = pltpu.make_async_copy(srcs[t], dsts[t].at[me], local_sems.at[t])
            own.start()
            locals_.append(own)
            for k in range(1, NDEV):
                peer, pidx = _peer(x, y, c, k)
                cp = pltpu.make_async_remote_copy(
                    src_ref=srcs[t], dst_ref=dsts[t].at[me], send_sem=send_sems.at[t, k],
                    recv_sem=recv_sems.at[t, k], device_id=peer, device_id_type=MESH)
                cp.start()
                sends.append(cp)
        for t in range(n):
            for k in range(1, NDEV):
                peer, pidx = _peer(x, y, c, k)
                pltpu.make_async_remote_copy(
                    src_ref=srcs[t], dst_ref=dsts[t].at[pidx], send_sem=send_sems.at[t, k],
                    recv_sem=recv_sems.at[t, k], device_id=peer, device_id_type=MESH).wait_recv()
        for cp in sends:
            cp.wait_send()
        for cp in locals_:
            cp.wait()

    return pl.pallas_call(
        body, name=name,
        out_shape=tuple(jax.ShapeDtypeStruct((NDEV,) + a.shape, a.dtype) for a in arrays),
        in_specs=[HBM_SPEC] * n, out_specs=tuple([HBM_SPEC] * n),
        scratch_shapes=[pltpu.SemaphoreType.DMA((n, NDEV)), pltpu.SemaphoreType.DMA((n, NDEV)),
                        pltpu.SemaphoreType.DMA((n,))],
    )(*arrays)


def _all_to_all(arrays, name):
    n = len(arrays)

    def body(*refs):
        srcs, dsts = refs[:n], refs[n:2 * n]
        send_sems, recv_sems, local_sems = refs[2 * n:]
        x, y, c, me = _my_place()
        locals_, sends = [], []
        for t in range(n):
            own = pltpu.make_async_copy(srcs[t].at[me], dsts[t].at[me], local_sems.at[t])
            own.start()
            locals_.append(own)
            for k in range(1, NDEV):
                peer, pidx = _peer(x, y, c, k)
                cp = pltpu.make_async_remote_copy(
                    src_ref=srcs[t].at[pidx], dst_ref=dsts[t].at[me], send_sem=send_sems.at[t, k],
                    recv_sem=recv_sems.at[t, k], device_id=peer, device_id_type=MESH)
                cp.start()
                sends.append(cp)
        for t in range(n):
            for k in range(1, NDEV):
                peer, pidx = _peer(x, y, c, k)
                pltpu.make_async_remote_copy(
                    src_ref=srcs[t].at[pidx], dst_ref=dsts[t].at[pidx], send_sem=send_sems.at[t, k],
                    recv_sem=recv_sems.at[t, k], device_id=peer, device_id_type=MESH).wait_recv()
        for cp in sends:
            cp.wait_send()
        for cp in locals_:
            cp.wait()

    return pl.pallas_call(
        body, name=name,
        out_shape=tuple(jax.ShapeDtypeStruct(a.shape, a.dtype) for a in arrays),
        in_specs=[HBM_SPEC] * n, out_specs=tuple([HBM_SPEC] * n),
        scratch_shapes=[pltpu.SemaphoreType.DMA((n, NDEV)), pltpu.SemaphoreType.DMA((n, NDEV)),
                        pltpu.SemaphoreType.DMA((n,))],
    )(*arrays)


def _comm_call(name, arrays, sems, new_sems, body, after=(), token=False):
    na, ns, nn, nf = len(arrays), len(sems), len(new_sems), len(after)

    def kern(*refs):
        ins, outs = refs[:na + ns + nf], refs[na + ns + nf:]
        body(ins[:na], ins[na:na + ns], outs[:nn])
        if token:
            outs[nn + na][...] = jnp.zeros((8, 128), f32)

    out_shape = ([pltpu.SemaphoreType.DMA(s) for s in new_sems] + [pltpu.HBM(a.shape, a.dtype) for a in arrays]
                 + ([jax.ShapeDtypeStruct((8, 128), f32)] if token else []))
    out_specs = [SEM_SPEC] * nn + [HBM_SPEC] * na + ([pl.BlockSpec(memory_space=pltpu.VMEM)] if token else [])
    res = pl.pallas_call(
        kern, name=name, out_shape=tuple(out_shape),
        in_specs=[HBM_SPEC] * na + [SEM_SPEC] * ns + [ANY_SPEC] * nf, out_specs=tuple(out_specs),
        input_output_aliases={t: nn + t for t in range(na)},
        compiler_params=pltpu.CompilerParams(has_side_effects=pltpu.SideEffectType.DATAFLOW_SIDE_EFFECTING),
    )(*[pltpu.with_memory_space_constraint(a, pltpu.HBM) for a in arrays], *sems, *after)
    return list(res[:nn]), list(res[nn:nn + na]), (res[nn + na] if token else None)


def _remote(src, dst, send_sem, recv_sem, device):
    return pltpu.make_async_remote_copy(src_ref=src, dst_ref=dst, send_sem=send_sem, recv_sem=recv_sem,
                                        device_id=device, device_id_type=MESH)


SAME_CORE = (2, 4, 6)
VIA_SIBLING = (3, 5, 7)


def _weights_start(lands):
    n = len(lands)

    def body(a, s, new):
        x, y, c, me = _my_place()
        for t, land in enumerate(a):
            send, recv = new[t], new[n + t]
            for k in (1,) + SAME_CORE:
                peer, _ = _peer(x, y, c, k)
                _remote(land.at[me], land.at[me], send.at[k], recv.at[k], peer).start()

    sems, lands, token = _comm_call("weights_start", lands, [], [(NDEV,)] * (2 * n), body, token=True)
    return sems[:n], sems[n:], lands, token


def _weights_forward(name, land, recv, after):
    def body(a, s, new):
        (land,), (recv,), (fsend, frecv) = a, s, new
        x, y, c, me = _my_place()
        sibling, _ = _peer(x, y, c, 1)
        for k in SAME_CORE:
            peer, slot = _peer(x, y, c, k)
            _remote(land.at[slot], land.at[slot], fsend.at[k], recv.at[k], peer).wait_recv()
            _remote(land.at[slot], land.at[slot], fsend.at[k], frecv.at[k ^ 1], sibling).start()

    return _comm_call(name, [land], [recv], [(NDEV,), (NDEV,)], body, after=after)


def _weights_wait(name, land, send, recv, fsend, frecv, after):
    def body(a, s, new):
        (land,), (send, recv, fsend, frecv) = a, s
        x, y, c, me = _my_place()
        sibling, sib_slot = _peer(x, y, c, 1)
        _remote(land.at[sib_slot], land.at[sib_slot], send.at[1], recv.at[1], sibling).wait_recv()
        for k in VIA_SIBLING:
            _, slot = _peer(x, y, c, k)
            _remote(land.at[slot], land.at[slot], fsend.at[k ^ 1], frecv.at[k], sibling).wait_recv()
        for k in (1,) + SAME_CORE:
            peer, _ = _peer(x, y, c, k)
            _remote(land.at[me], land.at[me], send.at[k], recv.at[k], peer).wait_send()
        for k in SAME_CORE:
            _, slot = _peer(x, y, c, k)
            _remote(land.at[slot], land.at[slot], fsend.at[k], frecv.at[k ^ 1], sibling).wait_send()

    return _comm_call(name, [land], [send, recv, fsend, frecv], [], body, after=after)[1][0]


def _device(j):
    return (j >> 2 & 1, j >> 1 & 1, j & 1)


def _grads_start(name, src, land, shards):
    def body(a, s, new):
        (src, land), (send, recv) = a, new
        x, y, c, me = _my_place()
        for gi, j in enumerate(shards):
            @pl.when(me != j)
            def _():
                _remote(src.at[gi], land.at[me], send.at[j], recv.at[me], _device(j)).start()

    return _comm_call(name, [src, land], [], [(NDEV,), (NDEV,)], body, token=True)


def _grads_wait(name, src, land, send, recv, shards, after):
    def body(a, s, new):
        (src, land), (send, recv) = a, s
        x, y, c, me = _my_place()
        mine = (me >= shards[0]) & (me <= shards[-1])
        for gi, j in enumerate(shards):
            @pl.when(me != j)
            def _():
                _remote(src.at[gi], land.at[me], send.at[j], recv.at[me], _device(j)).wait_send()
        for i in range(NDEV):
            @pl.when(mine & (me != i))
            def _():
                _remote(src.at[0], land.at[i], send.at[i], recv.at[i], _device(i)).wait_recv()

    return _comm_call(name, [src, land], [send, recv], [], body, after=after)[1]


def _matmul(a, b, *, name, out_dtype, ta=False, tb=False, b_slots=False, out_slots=0, b_cols=None,
            tm=512, tn=512, tk=512, dep=None):
    M, K = (a.shape[1], a.shape[0]) if ta else a.shape
    col0 = 0
    if b_slots:
        slab = b.shape[2]
        N = b.shape[1] if tb else NDEV * slab
        assert (K if tb else N) == NDEV * slab
    elif b_cols is not None:
        assert not tb
        col0, N = b_cols
    else:
        N = b.shape[0] if tb else b.shape[1]
    tm, tn, tk = min(tm, M), min(tn, N), min(tk, K)
    if b_slots:
        if tb:
            tk = min(tk, slab)
        else:
            tn = min(tn, slab)
    if out_slots:
        tn = min(tn, N // out_slots)
    nm, nn, nk = M // tm, N // tn, K // tk
    assert (nm * tm, nn * tn, nk * tk) == (M, N, K) and col0 % tn == 0, (name, M, N, K, tm, tn, tk)
    j0 = col0 // tn

    a_spec = pl.BlockSpec((tk, tm), lambda i, j, k: (k, i)) if ta else pl.BlockSpec((tm, tk), lambda i, j, k: (i, k))
    if b_slots and tb:
        per = slab // tk
        b_spec = pl.BlockSpec((None, tn, tk), lambda i, j, k: (k // per, j, k % per))
    elif b_slots:
        per = slab // tn
        b_spec = pl.BlockSpec((None, tk, tn), lambda i, j, k: (j // per, k, j % per))
    elif tb:
        b_spec = pl.BlockSpec((tn, tk), lambda i, j, k: (j, k))
    else:
        b_spec = pl.BlockSpec((tk, tn), lambda i, j, k: (k, j + j0))
    if out_slots:
        per_o = (N // out_slots) // tn
        o_spec = pl.BlockSpec((None, tm, tn), lambda i, j, k: (j // per_o, i, j % per_o))
        out_shape = jax.ShapeDtypeStruct((out_slots, M, N // out_slots), out_dtype)
    else:
        o_spec = pl.BlockSpec((tm, tn), lambda i, j, k: (i, j))
        out_shape = jax.ShapeDtypeStruct((M, N), out_dtype)
    dims = (((0 if ta else 1,), (1 if tb else 0,)), ((), ()))
    deps = [] if dep is None else [dep]

    def body(a_ref, b_ref, *rest):
        o_ref, acc_ref = rest[len(deps):]
        k = pl.program_id(2)

        @pl.when(k == 0)
        def _():
            acc_ref[...] = jnp.zeros_like(acc_ref)

        acc_ref[...] += lax.dot_general(a_ref[...], b_ref[...], dims, preferred_element_type=f32)

        @pl.when(k == nk - 1)
        def _():
            o_ref[...] = acc_ref[...].astype(out_dtype)

    return pl.pallas_call(
        body, name=name, out_shape=out_shape, grid=(nm, nn, nk),
        in_specs=[a_spec, b_spec] + [ANY_SPEC] * len(deps), out_specs=o_spec,
        scratch_shapes=[pltpu.VMEM((tm, tn), f32)],
        compiler_params=_params("parallel", "parallel", "arbitrary"),
    )(a, b, *deps)


def _silu_rows(c, dep):
    D = c.shape[1]

    def body(c_ref, dep_ref, o_ref):
        o_ref[...] = jnp.broadcast_to(_silu(c_ref[...]), (8, D))

    vmem = pl.BlockSpec(memory_space=pltpu.VMEM)
    return pl.pallas_call(body, name="silu_c", out_shape=jax.ShapeDtypeStruct((8, D), f32),
                          in_specs=[vmem, ANY_SPEC], out_specs=vmem)(c, dep)


def _ada_cols(c_all, w_ada, b_cols):
    D, W = w_ada.shape

    def body(c_ref, w_ref, b_ref, o_ref):
        mod = lax.dot_general(c_ref[...], w_ref[...], (((1,), (0,)), ((), ())), preferred_element_type=f32,
                              precision=lax.Precision.HIGHEST) + b_ref[...]
        for j in range(NDEV):
            o_ref[j] = jnp.broadcast_to(mod[j:j + 1, :], (8, W))

    return pl.pallas_call(body, name="ada_cols", out_shape=jax.ShapeDtypeStruct((NDEV, 8, W), f32),
                          compiler_params=_params())(c_all, w_ada, b_cols)


def _prenorm(x, scale, shift, g_pre, tr=256):
    S, D = x.shape
    tr = min(tr, S)

    def body(x_ref, sc_ref, sh_ref, g_ref, h_ref):
        xv = x_ref[...]
        r = lax.rsqrt(jnp.mean(xv * xv, axis=-1, keepdims=True) + EPS)
        h_ref[...] = ((xv * r) * g_ref[...] * (1.0 + sc_ref[...]) + sh_ref[...]).astype(bf16)

    row = pl.BlockSpec((tr, D), lambda i: (i, 0))
    vec = pl.BlockSpec((1, D), lambda i: (0, 0))
    return pl.pallas_call(body, name="prenorm", out_shape=jax.ShapeDtypeStruct((S, D), bf16), grid=(S // tr,),
                          in_specs=[row, vec, vec, vec], out_specs=row, compiler_params=_params("parallel"))(
                              x, scale, shift, g_pre)


def _ext_rows(i, tr, S):
    g = lax.broadcasted_iota(jnp.int32, (tr + 16, 1), 0) + (i * tr - 8)
    return (g >= 0) & (g < S)


def _halo_specs(tr, S, C, col):
    nb8 = S // 8
    main = pl.BlockSpec((tr, C), lambda i: (i, col))
    prev = pl.BlockSpec((8, C), lambda i: (jnp.maximum(i * (tr // 8) - 1, 0), col))
    nxt = pl.BlockSpec((8, C), lambda i: (jnp.minimum((i + 1) * (tr // 8), nb8 - 1), col))
    return prev, main, nxt


def _conv_fwd(proj, conv_w, conv_b, g_conv, tr=256):
    S, C = proj.shape[0], proj.shape[1] // 8
    tr = min(tr, S)

    def body(up, um, un, cp, cm, cn, bg_ref, zc_ref, w_ref, cb_ref, g_ref, o_ref):
        i = pl.program_id(0)
        exists = _ext_rows(i, tr, S)
        u = jnp.concatenate([up[...], um[...], un[...]], axis=0)
        cg = jnp.concatenate([cp[...], cm[...], cn[...]], axis=0)
        t = jnp.where(exists, cg * u, 0.0)
        t_before = pltpu.roll(t, 1, 0)[8:tr + 8]
        t_after = pltpu.roll(t, tr + 15, 0)[8:tr + 8]
        w = w_ref[...]
        cv = w[0:1] * t_before + w[1:2] * t[8:tr + 8] + w[2:3] * t_after + cb_ref[...]
        yc = bg_ref[...] * cv
        rc = lax.rsqrt(jnp.mean(yc * yc, axis=-1, keepdims=True) + EPS)
        o_ref[...] = ((yc * rc) * g_ref[...] * _silu(zc_ref[...])).astype(bf16)

    u_specs = _halo_specs(tr, S, C, 0)
    c_specs = _halo_specs(tr, S, C, 2)
    vec = pl.BlockSpec((1, C), lambda i: (0, 0))
    return pl.pallas_call(
        body, name="conv_fwd", out_shape=jax.ShapeDtypeStruct((S, 2 * C), bf16), grid=(S // tr,),
        in_specs=[*u_specs, *c_specs, pl.BlockSpec((tr, C), lambda i: (i, 1)), pl.BlockSpec((tr, C), lambda i: (i, 3)),
                  pl.BlockSpec((8, C), lambda i: (0, 0)), vec, vec],
        out_specs=pl.BlockSpec((tr, C), lambda i: (i, 0)), compiler_params=_params("parallel"),
    )(proj, proj, proj, proj, proj, proj, proj, proj, conv_w, conv_b, g_conv)


def _branch_geometry(S, r):
    L = S // r
    nq = min(128, L)
    nk = min(nq + 2 * HALF_WIN, L)
    return L, nq, nk, L // nq


def _block_rows(idx, r, L, nq, nk, nblk):
    if r == 1:
        rho, qb = 0, idx
    else:
        rho, qb = idx // nblk, idx % nblk
    i0 = qb * nq
    ws = jnp.clip(i0 - HALF_WIN, 0, L - nk)
    if r == 1:
        qrows = pl.ds(pl.multiple_of(i0, 8), nq)
        krows = pl.ds(pl.multiple_of(ws, 8), nk)
    else:
        qrows = pl.ds(rho + r * i0, nq, stride=r)
        krows = pl.ds(rho + r * ws, nk, stride=r)
    return qrows, krows, i0 - ws


def _band(nq, nk, off, r):
    d = jnp.abs(lax.broadcasted_iota(jnp.int32, (nq, nk), 0) - lax.broadcasted_iota(jnp.int32, (nq, nk), 1) + off)
    return d <= HALF_WIN, d.astype(f32) * float(r)


def _head_slopes(n_heads):
    slopes = 2.0 ** (-8.0 * jnp.arange(1, n_heads + 1, dtype=f32) / n_heads)
    return jnp.broadcast_to(jnp.repeat(slopes.reshape(n_heads // 2, 2), HEAD_DIM, axis=1)[:, None, :],
                            (n_heads // 2, 8, PAIR))


def _attn_fwd(proj, slopes):
    S, C = proj.shape[0], proj.shape[1] // 8
    npair = C // PAIR

    def body(q_ref, k_ref, v_ref, sl_ref, o_ref, lse_ref, m_scr, l_scr, a_scr):
        lane = lax.broadcasted_iota(jnp.int32, (1, PAIR), 1)
        first = lane < HEAD_DIM
        sl = sl_ref[...]
        slope = (sl[0:1, 0:1], sl[0:1, HEAD_DIM:HEAD_DIM + 1])

        for b, (_, r) in enumerate(BRANCHES):
            L, nq, nk, nblk = _branch_geometry(S, r)

            def step(idx, carry, b=b, r=r, L=L, nq=nq, nk=nk, nblk=nblk):
                qrows, krows, off = _block_rows(idx, r, L, nq, nk, nblk)
                q2 = q_ref[qrows, :]
                k2 = k_ref[krows, :].astype(bf16)
                v2 = v_ref[krows, :].astype(bf16)
                valid, dist = _band(nq, nk, off, r)
                ms, ls, accs = [], [], []
                for hh in range(2):
                    qh = jnp.where(first if hh == 0 else ~first, q2, 0.0).astype(bf16)
                    s = lax.dot_general(qh, k2, (((1,), (1,)), ((), ())), preferred_element_type=f32) * HEAD_DIM ** -0.5
                    s = jnp.where(valid, s - slope[hh] * dist, NEG_INF)
                    m = jnp.max(s, axis=-1, keepdims=True)
                    p = jnp.exp(s - m)
                    ms.append(m)
                    ls.append(jnp.sum(p, axis=-1, keepdims=True))
                    accs.append(jnp.dot(p.astype(bf16), v2, preferred_element_type=f32))
                m_scr[b, qrows, :] = jnp.where(first, ms[0], ms[1])
                l_scr[b, qrows, :] = jnp.where(first, ls[0], ls[1])
                a_scr[b, qrows, :] = jnp.where(first, accs[0], accs[1])
                return carry

            lax.fori_loop(0, S // nq, step, 0)

        ch = min(256, S)

        def merge(i, carry):
            rows = pl.ds(pl.multiple_of(i * ch, 8), ch)
            m = jnp.maximum(jnp.maximum(m_scr[0, rows, :], m_scr[1, rows, :]), m_scr[2, rows, :])
            l = jnp.zeros((ch, PAIR), f32)
            acc = jnp.zeros((ch, PAIR), f32)
            for b in range(3):
                w = jnp.exp(m_scr[b, rows, :] - m)
                l = l + w * l_scr[b, rows, :]
                acc = acc + w * a_scr[b, rows, :]
            o_ref[rows, :] = acc / l
            lse_ref[rows, :] = m + jnp.log(l)
            return carry

        lax.fori_loop(0, S // ch, merge, 0)

    blk = lambda part: pl.BlockSpec((S, PAIR), lambda p: (0, part * npair + p))
    out = pl.BlockSpec((S, PAIR), lambda p: (0, p))
    return pl.pallas_call(
        body, name="attn_fwd",
        out_shape=(jax.ShapeDtypeStruct((S, C), f32), jax.ShapeDtypeStruct((S, C), f32)), grid=(npair,),
        in_specs=[blk(4), blk(5), blk(6), pl.BlockSpec((None, 8, PAIR), lambda p: (p, 0, 0))],
        out_specs=(out, out),
        scratch_shapes=[pltpu.VMEM((3, S, PAIR), f32)] * 3, compiler_params=_params("parallel"),
    )(proj, proj, proj, slopes)


def _attn_post(ycat, o, proj, g_attn, tr=256):
    S, C = o.shape
    tr = min(tr, S)

    def body(y_ref, o_ref, z_ref, g_ref, out_ref):
        del y_ref
        ov = o_ref[...]
        ra = lax.rsqrt(jnp.mean(ov * ov, axis=-1, keepdims=True) + EPS)
        out_ref[...] = ((ov * ra) * g_ref[...] * _silu(z_ref[...])).astype(bf16)

    return pl.pallas_call(
        body, name="attn_post", out_shape=jax.ShapeDtypeStruct(ycat.shape, ycat.dtype), grid=(S // tr,),
        in_specs=[HBM_SPEC, pl.BlockSpec((tr, C), lambda i: (i, 0)), pl.BlockSpec((tr, C), lambda i: (i, 7)),
                  pl.BlockSpec((1, C), lambda i: (0, 0))],
        out_specs=pl.BlockSpec((tr, C), lambda i: (i, 1)), input_output_aliases={0: 0},
        compiler_params=_params("arbitrary"),
    )(ycat, o, proj, g_attn)


def _sandwich(y, x, target, gate, g_post, tr=256):
    S, D = y.shape
    tr = min(tr, S)

    def body(y_ref, x_ref, t_ref, gate_ref, g_ref, dy_ref, dout_ref, sums_ref):
        i = pl.program_id(0)
        yv = y_ref[...]
        rp = lax.rsqrt(jnp.mean(yv * yv, axis=-1, keepdims=True) + EPS)
        yhat = yv * rp
        yn = yhat * g_ref[...]
        err = (x_ref[...] + gate_ref[...] * yn) - t_ref[...]
        dout = err * (1.0 / D)
        dout_ref[...] = dout
        dyn = dout * gate_ref[...]
        w = dyn * g_ref[...]
        dy_ref[...] = (rp * (w - yhat * jnp.mean(w * yhat, axis=-1, keepdims=True))).astype(bf16)
        loss = 0.5 * jnp.sum(jnp.mean(err * err, axis=-1, keepdims=True), axis=0, keepdims=True)
        row = lax.broadcasted_iota(jnp.int32, (8, D), 0)
        upd = jnp.where(row == 0, jnp.sum(dout * yn, axis=0, keepdims=True),
                        jnp.where(row == 1, jnp.sum(dyn * yhat, axis=0, keepdims=True),
                                  jnp.where(row == 2, loss, 0.0)))

        @pl.when(i == 0)
        def _():
            sums_ref[...] = upd

        @pl.when(i > 0)
        def _():
            sums_ref[...] += upd

    row = pl.BlockSpec((tr, D), lambda i: (i, 0))
    vec = pl.BlockSpec((1, D), lambda i: (0, 0))
    return pl.pallas_call(
        body, name="sandwich",
        out_shape=(jax.ShapeDtypeStruct((S, D), bf16), jax.ShapeDtypeStruct((S, D), f32), jax.ShapeDtypeStruct((8, D), f32)),
        grid=(S // tr,), in_specs=[row, row, row, vec, vec],
        out_specs=(row, row, pl.BlockSpec((8, D), lambda i: (0, 0))), compiler_params=_params("arbitrary"),
    )(y, x, target, gate, g_post)


def _conv_bwd(proj, dycat, conv_w, conv_b, g_conv, dep, tr=256):
    S, C = proj.shape[0], proj.shape[1] // 8
    tr = min(tr, S)
    n = tr + 16

    def body(*refs):
        ins, (w_ref, cb_ref, g_ref, _, dp_ref, sums_ref) = refs[:15], refs[15:]
        i = pl.program_id(0)
        exists = _ext_rows(i, tr, S)
        u, bg, cg, zc, dyn = (jnp.concatenate([ins[3 * t][...], ins[3 * t + 1][...], ins[3 * t + 2][...]], axis=0)
                              for t in range(5))
        w = w_ref[...]
        t = jnp.where(exists, cg * u, 0.0)
        t_before, t_after = pltpu.roll(t, 1, 0), pltpu.roll(t, n - 1, 0)
        cv = w[0:1] * t_before + w[1:2] * t + w[2:3] * t_after + cb_ref[...]
        yc = bg * cv
        rc = lax.rsqrt(jnp.mean(yc * yc, axis=-1, keepdims=True) + EPS)
        yhat = yc * rc
        sz = _silu(zc)
        wgt = dyn * g_ref[...] * sz
        dyc = rc * (wgt - yhat * jnp.mean(wgt * yhat, axis=-1, keepdims=True))
        dcv = jnp.where(exists, dyc * bg, 0.0)
        dt = w[0:1] * pltpu.roll(dcv, n - 1, 0) + w[1:2] * dcv + w[2:3] * pltpu.roll(dcv, 1, 0)
        mid = slice(8, tr + 8)
        dp_ref[:, 0:C] = (dt * cg)[mid].astype(bf16)
        dp_ref[:, C:2 * C] = (dyc * cv)[mid].astype(bf16)
        dp_ref[:, 2 * C:3 * C] = (dt * u)[mid].astype(bf16)
        dp_ref[:, 3 * C:4 * C] = (dyn * yhat * g_ref[...] * _dsilu(zc))[mid].astype(bf16)
        colsum = lambda v: jnp.sum(v[mid], axis=0, keepdims=True)
        parts = [colsum(dyn * yhat * sz), colsum(dcv), colsum(dcv * t_before), colsum(dcv * t), colsum(dcv * t_after)]
        row = lax.broadcasted_iota(jnp.int32, (8, C), 0)
        upd = jnp.zeros((8, C), f32)
        for j, pj in enumerate(parts):
            upd = jnp.where(row == j, pj, upd)

        @pl.when(i == 0)
        def _():
            sums_ref[...] = upd

        @pl.when(i > 0)
        def _():
            sums_ref[...] += upd

    specs = []
    for col in range(4):
        specs += _halo_specs(tr, S, C, col)
    specs += _halo_specs(tr, S, C, 0)
    vec = pl.BlockSpec((1, C), lambda i: (0, 0))
    return pl.pallas_call(
        body, name="conv_bwd",
        out_shape=(jax.ShapeDtypeStruct((S, 4 * C), bf16), jax.ShapeDtypeStruct((8, C), f32)), grid=(S // tr,),
        in_specs=[*specs, pl.BlockSpec((8, C), lambda i: (0, 0)), vec, vec, ANY_SPEC],
        out_specs=(pl.BlockSpec((tr, 4 * C), lambda i: (i, 0)), pl.BlockSpec((8, C), lambda i: (0, 0))),
        compiler_params=_params("arbitrary"),
    )(*([proj] * 12), dycat, dycat, dycat, conv_w, conv_b, g_conv, dep)


def _attn_post_bwd(o, proj, dycat, g_attn, dep, tr=256):
    S, C = o.shape
    tr = min(tr, S)

    def body(o_ref, z_ref, dy_ref, g_ref, dep_ref, do_ref, dz_ref, sums_ref):
        i = pl.program_id(0)
        ov, zv, dyn = o_ref[...], z_ref[...], dy_ref[...]
        ra = lax.rsqrt(jnp.mean(ov * ov, axis=-1, keepdims=True) + EPS)
        ohat = ov * ra
        sz = _silu(zv)
        wgt = dyn * g_ref[...] * sz
        do_ref[...] = ra * (wgt - ohat * jnp.mean(wgt * ohat, axis=-1, keepdims=True))
        dz_ref[...] = (dyn * ohat * g_ref[...] * _dsilu(zv)).astype(bf16)
        row = lax.broadcasted_iota(jnp.int32, (8, C), 0)
        upd = jnp.where(row == 0, jnp.sum(dyn * ohat * sz, axis=0, keepdims=True), 0.0)

        @pl.when(i == 0)
        def _():
            sums_ref[...] = upd

        @pl.when(i > 0)
        def _():
            sums_ref[...] += upd

    return pl.pallas_call(
        body, name="attn_post_bwd",
        out_shape=(jax.ShapeDtypeStruct((S, C), f32), jax.ShapeDtypeStruct((S, C), bf16), jax.ShapeDtypeStruct((8, C), f32)),
        grid=(S // tr,),
        in_specs=[pl.BlockSpec((tr, C), lambda i: (i, 0)), pl.BlockSpec((tr, C), lambda i: (i, 7)),
                  pl.BlockSpec((tr, C), lambda i: (i, 1)), pl.BlockSpec((1, C), lambda i: (0, 0)), ANY_SPEC],
        out_specs=(pl.BlockSpec((tr, C), lambda i: (i, 0)), pl.BlockSpec((tr, C), lambda i: (i, 0)),
                   pl.BlockSpec((8, C), lambda i: (0, 0))),
        compiler_params=_params("arbitrary"),
    )(o, proj, dycat, g_attn, dep)


def _attn_bwd(proj, o, do, lse, slopes):
    S, C = o.shape
    npair = C // PAIR

    def body(q_ref, k_ref, v_ref, o_ref, do_ref, lse_ref, sl_ref, dq_ref, dk_ref, dv_ref, dq_scr, dk_scr, dv_scr, dl_scr):
        lane = lax.broadcasted_iota(jnp.int32, (1, PAIR), 1)
        first = lane < HEAD_DIM
        sl = sl_ref[...]
        slope = (sl[0:1, 0:1], sl[0:1, HEAD_DIM:HEAD_DIM + 1])
        ch = min(256, S)

        def prep(i, carry):
            rows = pl.ds(pl.multiple_of(i * ch, 8), ch)
            prod = do_ref[rows, :] * o_ref[rows, :]
            d0 = jnp.sum(jnp.where(first, prod, 0.0), axis=-1, keepdims=True)
            d1 = jnp.sum(jnp.where(first, 0.0, prod), axis=-1, keepdims=True)
            dl_scr[rows, :] = jnp.where(first, d0, d1)
            zero = jnp.zeros((ch, PAIR), f32)
            dq_scr[rows, :] = zero
            dk_scr[rows, :] = zero
            dv_scr[rows, :] = zero
            return carry

        lax.fori_loop(0, S // ch, prep, 0)

        for _, r in BRANCHES:
            L, nq, nk, nblk = _branch_geometry(S, r)

            def step(idx, carry, r=r, L=L, nq=nq, nk=nk, nblk=nblk):
                qrows, krows, off = _block_rows(idx, r, L, nq, nk, nblk)
                q2 = q_ref[qrows, :]
                k2 = k_ref[krows, :].astype(bf16)
                v2 = v_ref[krows, :].astype(bf16)
                do2 = do_ref[qrows, :]
                lse2 = lse_ref[qrows, :]
                dl2 = dl_scr[qrows, :]
                valid, dist = _band(nq, nk, off, r)
                dq2 = jnp.zeros((nq, PAIR), f32)
                dk2 = jnp.zeros((nk, PAIR), f32)
                dv2 = jnp.zeros((nk, PAIR), f32)
                for hh in range(2):
                    mine = first if hh == 0 else ~first
                    lo = hh * HEAD_DIM
                    qh = jnp.where(mine, q2, 0.0).astype(bf16)
                    doh = jnp.where(mine, do2, 0.0).astype(bf16)
                    s = lax.dot_general(qh, k2, (((1,), (1,)), ((), ())), preferred_element_type=f32) * HEAD_DIM ** -0.5
                    s = jnp.where(valid, s - slope[hh] * dist, NEG_INF)
                    p = jnp.exp(s - lse2[:, lo:lo + 1])
                    dv2 = dv2 + lax.dot_general(p.astype(bf16), doh, (((0,), (0,)), ((), ())), preferred_element_type=f32)
                    dp = lax.dot_general(doh, v2, (((1,), (1,)), ((), ())), preferred_element_type=f32)
                    ds = (p * (dp - dl2[:, lo:lo + 1]) * HEAD_DIM ** -0.5).astype(bf16)
                    dq2 = dq2 + jnp.where(mine, jnp.dot(ds, k2, preferred_element_type=f32), 0.0)
                    dk2 = dk2 + lax.dot_general(ds, qh, (((0,), (0,)), ((), ())), preferred_element_type=f32)
                dq_scr[qrows, :] = dq_scr[qrows, :] + dq2
                dk_scr[krows, :] = dk_scr[krows, :] + dk2
                dv_scr[krows, :] = dv_scr[krows, :] + dv2
                return carry

            lax.fori_loop(0, S // nq, step, 0)

        dq_ref[...] = dq_scr[...].astype(bf16)
        dk_ref[...] = dk_scr[...].astype(bf16)
        dv_ref[...] = dv_scr[...].astype(bf16)

    blk = lambda part: pl.BlockSpec((S, PAIR), lambda p: (0, part * npair + p))
    own = pl.BlockSpec((S, PAIR), lambda p: (0, p))
    grad = jax.ShapeDtypeStruct((S, C), bf16)
    return pl.pallas_call(
        body, name="attn_bwd", out_shape=(grad, grad, grad), grid=(npair,),
        in_specs=[blk(4), blk(5), blk(6), own, own, own, pl.BlockSpec((None, 8, PAIR), lambda p: (p, 0, 0))],
        out_specs=(own, own, own), scratch_shapes=[pltpu.VMEM((S, PAIR), f32)] * 4,
        compiler_params=_params("parallel"),
    )(proj, proj, proj, o, do, lse, slopes)


def _prenorm_bwd(dh, x, dout, scale, g_pre, tr=256):
    S, D = x.shape
    tr = min(tr, S)

    def body(dh_ref, x_ref, dout_ref, sc_ref, g_ref, gx_ref, sums_ref):
        i = pl.program_id(0)
        xv, dhv = x_ref[...], dh_ref[...]
        r = lax.rsqrt(jnp.mean(xv * xv, axis=-1, keepdims=True) + EPS)
        xn = xv * r
        dxn = dhv * (g_ref[...] * (1.0 + sc_ref[...]))
        gx_ref[...] = dout_ref[...] + r * (dxn - xn * jnp.mean(dxn * xn, axis=-1, keepdims=True))
        dhx = dhv * xn
        row = lax.broadcasted_iota(jnp.int32, (8, D), 0)
        upd = jnp.where(row == 0, jnp.sum(dhv, axis=0, keepdims=True),
                        jnp.where(row == 1, jnp.sum(dhx, axis=0, keepdims=True) * g_ref[...],
                                  jnp.where(row == 2, jnp.sum(dhx, axis=0, keepdims=True) * (1.0 + sc_ref[...]), 0.0)))

        @pl.when(i == 0)
        def _():
            sums_ref[...] = upd

        @pl.when(i > 0)
        def _():
            sums_ref[...] += upd

    row = pl.BlockSpec((tr, D), lambda i: (i, 0))
    vec = pl.BlockSpec((1, D), lambda i: (0, 0))
    return pl.pallas_call(
        body, name="prenorm_bwd",
        out_shape=(jax.ShapeDtypeStruct((S, D), f32), jax.ShapeDtypeStruct((8, D), f32)), grid=(S // tr,),
        in_specs=[row, row, row, vec, vec], out_specs=(row, pl.BlockSpec((8, D), lambda i: (0, 0))),
        compiler_params=_params("arbitrary"),
    )(dh, x, dout, scale, g_pre)


def _adamw(w, g, m, v):
    m = ADAM_B1 * m + (1.0 - ADAM_B1) * g
    v = ADAM_B2 * v + (1.0 - ADAM_B2) * (g * g)
    m_hat = m / (1.0 - ADAM_B1 ** ADAM_STEP)
    v_hat = v / (1.0 - ADAM_B2 ** ADAM_STEP)
    delta = -ADAM_LR * (m_hat / (jnp.sqrt(v_hat) + ADAM_EPS) + ADAM_WD * w)
    return delta, m, v


def _sum_rows(parts):
    P = parts.shape[1]

    def body(p_ref, o_ref):
        acc = p_ref[0:1, :]
        for j in range(1, NDEV):
            acc = acc + p_ref[j:j + 1, :]
        o_ref[...] = jnp.broadcast_to(acc, (8, P))

    return pl.pallas_call(body, name="sum_small", out_shape=jax.ShapeDtypeStruct((8, P), f32),
                          compiler_params=_params())(parts)


def _adamw_small(w, g, m, v):
    def body(w_ref, g_ref, m_ref, v_ref, d_ref, nm_ref, nv_ref):
        d_ref[...], nm_ref[...], nv_ref[...] = _adamw(w_ref[...], g_ref[...], m_ref[...], v_ref[...])

    out = jax.ShapeDtypeStruct(w.shape, f32)
    return pl.pallas_call(body, name="adamw_small", out_shape=(out, out, out), compiler_params=_params())(w, g, m, v)


def _adamw_sharded(parts, own, w, m, v, name, tr=128):
    R, Cc = w.shape
    tr = min(tr, R)

    def body(p_ref, own_ref, w_ref, m_ref, v_ref, g_ref, d_ref, nm_ref, nv_ref):
        g = own_ref[...].astype(f32)
        for j in range(NDEV):
            g = g + p_ref[j].astype(f32)
        g_ref[...] = g
        d_ref[...], nm_ref[...], nv_ref[...] = _adamw(w_ref[...], g, m_ref[...], v_ref[...])

    row = pl.BlockSpec((tr, Cc), lambda i: (i, 0))
    out = jax.ShapeDtypeStruct((R, Cc), f32)
    return pl.pallas_call(
        body, name=name, out_shape=(out, out, out, out), grid=(R // tr,),
        in_specs=[pl.BlockSpec((NDEV, tr, Cc), lambda i: (0, i, 0)), row, row, row, row],
        out_specs=(row, row, row, row), compiler_params=_params("parallel"),
    )(parts, own, w, m, v)


def _adamw_ada(c_t, dmod_cols, w, m, v, tr=256):
    D, W = w.shape
    tr = min(tr, D)

    def body(c_ref, dm_ref, w_ref, m_ref, v_ref, g_ref, d_ref, nm_ref, nv_ref):
        cv, dm = c_ref[...], dm_ref[...]
        g = cv[:, 0:1] * dm[0:1, :]
        for b in range(1, NDEV):
            g = g + cv[:, b:b + 1] * dm[b:b + 1, :]
        g_ref[...] = g
        d_ref[...], nm_ref[...], nv_ref[...] = _adamw(w_ref[...], g, m_ref[...], v_ref[...])

    row = pl.BlockSpec((tr, W), lambda i: (i, 0))
    out = jax.ShapeDtypeStruct((D, W), f32)
    return pl.pallas_call(
        body, name="adamw_ada", out_shape=(out, out, out, out), grid=(D // tr,),
        in_specs=[pl.BlockSpec((tr, NDEV), lambda i: (i, 0)), pl.BlockSpec((NDEV, W), lambda i: (0, 0)), row, row, row],
        out_specs=(row, row, row, row), compiler_params=_params("parallel"),
    )(c_t, dmod_cols, w, m, v)


def kernel(x, c, w_ada, b_ada, g_pre, w_in, conv_w, conv_b, g_conv, g_attn, w_out, g_post, loss_target, m_w_ada, m_b_ada, m_g_pre, m_w_in, m_conv_w, m_conv_b, m_g_conv, m_g_attn, m_w_out, m_g_post, v_w_ada, v_b_ada, v_g_pre, v_w_in, v_conv_w, v_conv_b, v_g_conv, v_g_attn, v_w_out, v_g_post):
    S, D = x.shape[1], x.shape[2]
    C = D // 2
    W = w_ada.shape[2]
    CW = conv_w.shape[2]
    me = 4 * lax.axis_index("x") + 2 * lax.axis_index("y") + lax.axis_index("c")
    x2, tgt = x[0], loss_target[0]
    w_ada2, w_in2, w_out2 = w_ada[0], w_in[0], w_out[0]

    R = D // NDEV
    land_i = lax.dynamic_update_slice(lax.empty((NDEV, D, C), bf16), w_in2.astype(bf16)[None], (me, 0, 0))
    land_o = lax.dynamic_update_slice(lax.empty((NDEV, R, D), bf16), w_out2.astype(bf16)[None], (me, 0, 0))
    (wi_send, wo_send), (wi_recv, wo_recv), (land_i, land_o), w_token = _weights_start([land_i, land_o])

    cw_slab = jnp.zeros((8, CW), f32).at[:3].set(conv_w[0])
    c_rows, cw_g = _all_gather([_silu_rows(c, w_token), cw_slab], "gather_c")
    c_all = c_rows[:, 0, :]
    conv_w_full = jnp.transpose(cw_g, (1, 0, 2)).reshape(8, C)
    b_cols = lax.dynamic_slice_in_dim(b_ada, me * W, W, axis=1)
    (mod_slabs,) = _all_to_all([_ada_cols(c_all, w_ada2, b_cols)], "scatter_mod")
    mod = mod_slabs[:, 0, :].reshape(1, 3 * D)
    shift, scale, gate = mod[:, :D], mod[:, D:2 * D], mod[:, 2 * D:]

    h = _prenorm(x2, scale, shift, g_pre)
    (fi_send, fi_recv), (land_i,), _ = _weights_forward("w_in_forward", land_i, wi_recv, after=[h])
    win_g = _weights_wait("w_in_wait", land_i, wi_send, wi_recv, fi_send, fi_recv, after=[h])
    proj = _matmul(h, win_g, name="in_proj", out_dtype=f32, b_slots=True, tm=1024, tn=1024, tk=512)
    (fo_send, fo_recv), (land_o,), _ = _weights_forward("w_out_forward", land_o, wo_recv, after=[proj])
    slopes = _head_slopes(C // HEAD_DIM)
    ycat = _conv_fwd(proj, conv_w_full, conv_b, g_conv)
    o, lse = _attn_fwd(proj, slopes)
    ycat = _attn_post(ycat, o, proj, g_attn)
    wout_g = _weights_wait("w_out_wait", land_o, wo_send, wo_recv, fo_send, fo_recv, after=[ycat])
    wout_full = wout_g.reshape(D, D)
    y = _matmul(ycat, wout_full, name="out_proj", out_dtype=f32, tm=1024, tn=1024, tk=512)
    dy, dout, post_sums = _sandwich(y, x2, tgt, gate, g_post)

    gw_out = _matmul(ycat, dy, name="out_proj_dw", out_dtype=bf16, ta=True, tm=512, tn=1024, tk=512).reshape(NDEV, R, D)
    (go_send, go_recv), (gw_out, land_go), go_token = _grads_start(
        "g_out_start", gw_out, jnp.zeros((NDEV, R, D), bf16), tuple(range(NDEV)))
    dycat = _matmul(dy, wout_full, name="out_proj_dx", out_dtype=f32, tb=True, tm=1024, tn=1024, tk=512, dep=go_token)
    dpc, conv_sums = _conv_bwd(proj, dycat, conv_w_full, conv_b, g_conv, go_token)
    gw_c = _matmul(h, dpc, name="in_proj_dw_conv", out_dtype=bf16, ta=True, out_slots=4, tm=1024, tn=1024, tk=512)
    (gc_send, gc_recv), (gw_c, land_gi), gc_token = _grads_start(
        "g_conv_start", gw_c, jnp.zeros((NDEV, D, C), bf16), (0, 1, 2, 3))
    do, dza, attn_sums = _attn_post_bwd(o, proj, dycat, g_attn, gc_token)
    dq, dk, dv = _attn_bwd(proj, o, do, lse, slopes)
    dproj = jnp.concatenate([dpc, dq, dk, dv, dza], axis=1)
    gw_a = _matmul(h, dproj, name="in_proj_dw_attn", out_dtype=bf16, ta=True, out_slots=4, b_cols=(4 * C, 4 * C),
                   tm=1024, tn=1024, tk=512)
    (ga_send, ga_recv), (gw_a, land_gi), ga_token = _grads_start("g_attn_start", gw_a, land_gi, (4, 5, 6, 7))
    dh = _matmul(dproj, win_g, name="in_proj_dx", out_dtype=f32, tb=True, b_slots=True, tm=1024, tn=1024, tk=512,
                 dep=ga_token)
    grad_x, pre_sums = _prenorm_bwd(dh, x2, dout, scale, g_pre)

    small = jnp.concatenate([pre_sums[0:1], pre_sums[1:2], post_sums[0:1],
                             pre_sums[2:3], post_sums[1:2],
                             conv_sums[2:3], conv_sums[3:4], conv_sums[4:5],
                             conv_sums[1:2], conv_sums[0:1], attn_sums[0:1]], axis=1)
    (small_all,) = _all_gather([jnp.broadcast_to(small, (8, 8 * D))], "gather_small")
    small_all = small_all[:, 0, :]
    tot = _sum_rows(small_all)[0:1]
    loss = lax.psum(post_sums[2, 0], ("x", "y", "c"))

    g_b_ada = tot[:, :3 * D]
    g_g_pre, g_g_post = tot[:, 3 * D:4 * D], tot[:, 4 * D:5 * D]
    g_conv_w_full = tot[:, 5 * D:5 * D + 3 * C].reshape(3, C)
    g_conv_w = lax.dynamic_slice_in_dim(g_conv_w_full, me * CW, CW, axis=1)[None]
    g_conv_b, g_g_conv, g_g_attn = (tot[:, 5 * D + (3 + t) * C:5 * D + (4 + t) * C] for t in range(3))

    dmod_cols = lax.dynamic_slice_in_dim(small_all[:, :3 * D], me * W, W, axis=1)
    g_w_ada, d_w_ada, nm_w_ada, nv_w_ada = _adamw_ada(c_all.T, dmod_cols, w_ada2, m_w_ada[0], v_w_ada[0])

    gw_out, land_go = _grads_wait("g_out_wait", gw_out, land_go, go_send, go_recv, tuple(range(NDEV)), after=[g_w_ada])
    own_out = lax.dynamic_index_in_dim(gw_out, me, 0, keepdims=False)
    g_w_out, d_w_out, nm_w_out, nv_w_out = _adamw_sharded(land_go, own_out, w_out2, m_w_out[0], v_w_out[0], "adamw_w_out")
    gw_c, land_gi = _grads_wait("g_conv_wait", gw_c, land_gi, gc_send, gc_recv, (0, 1, 2, 3), after=[g_w_out])
    gw_a, land_gi = _grads_wait("g_attn_wait", gw_a, land_gi, ga_send, ga_recv, (4, 5, 6, 7), after=[g_w_out])
    own_in = jnp.where(me < 4, lax.dynamic_index_in_dim(gw_c, me % 4, 0, keepdims=False),
                       lax.dynamic_index_in_dim(gw_a, me % 4, 0, keepdims=False))
    g_w_in, d_w_in, nm_w_in, nv_w_in = _adamw_sharded(land_gi, own_in, w_in2, m_w_in[0], v_w_in[0], "adamw_w_in")

    pack = lambda *vs: jnp.concatenate([a.reshape(1, -1) for a in vs], axis=1)
    smalls = [(b_ada, g_b_ada, m_b_ada, v_b_ada), (g_pre, g_g_pre, m_g_pre, v_g_pre),
              (conv_w, g_conv_w, m_conv_w, v_conv_w), (conv_b, g_conv_b, m_conv_b, v_conv_b),
              (g_conv, g_g_conv, m_g_conv, v_g_conv), (g_attn, g_g_attn, m_g_attn, v_g_attn),
              (g_post, g_g_post, m_g_post, v_g_post)]
    packed = [pack(*[s[t] for s in smalls]) for t in range(4)]
    npad = -packed[0].shape[1] % 128
    packed = [jnp.pad(p, ((0, 0), (0, npad)), constant_values=1.0) for p in packed]
    d_s, nm_s, nv_s = _adamw_small(*packed)

    def unpack(vec):
        out, at = [], 0
        for s in smalls:
            n = s[0].size
            out.append(vec[:, at:at + n].reshape(s[0].shape))
            at += n
        return out

    d_b_ada, d_g_pre, d_conv_w, d_conv_b, d_g_conv, d_g_attn, d_g_post = unpack(d_s)
    nm_b_ada, nm_g_pre, nm_conv_w, nm_conv_b, nm_g_conv, nm_g_attn, nm_g_post = unpack(nm_s)
    nv_b_ada, nv_g_pre, nv_conv_w, nv_conv_b, nv_g_conv, nv_g_attn, nv_g_post = unpack(nv_s)

    return (loss, grad_x[None],
            g_w_ada[None], g_b_ada, g_g_pre, g_w_in[None], g_conv_w, g_conv_b, g_g_conv, g_g_attn, g_w_out[None], g_g_post,
            d_w_ada[None], d_b_ada, d_g_pre, d_w_in[None], d_conv_w, d_conv_b, d_g_conv, d_g_attn, d_w_out[None], d_g_post,
            nm_w_ada[None], nm_b_ada, nm_g_pre, nm_w_in[None], nm_conv_w, nm_conv_b, nm_g_conv, nm_g_attn, nm_w_out[None], nm_g_post,
            nv_w_ada[None], nv_b_ada, nv_g_pre, nv_w_in[None], nv_conv_w, nv_conv_b, nv_g_conv, nv_g_attn, nv_w_out[None], nv_g_post)
```

```python
import functools

import jax
import jax.numpy as jnp
from jax import lax
from jax.experimental import pallas as pl
from jax.experimental.pallas import tpu as pltpu

f32 = jnp.float32
bf16 = jnp.bfloat16

NDEV = 8
HEAD_DIM = 64
PAIR = 2 * HEAD_DIM
BRANCHES = ((128, 1), (512, 4), (2048, 16))
HALF_WIN = 64
EPS = 1e-6
NEG_INF = -1e30
ADAM_LR, ADAM_B1, ADAM_B2, ADAM_EPS, ADAM_WD, ADAM_STEP = 0.001, 0.9, 0.999, 1e-08, 0.01, 10
MESH = pl.DeviceIdType.MESH
VMEM_LIMIT = 56 * 1024 * 1024
HBM_SPEC = pl.BlockSpec(memory_space=pltpu.HBM)
ANY_SPEC = pl.BlockSpec(memory_space=pl.ANY)
SEM_SPEC = pl.BlockSpec(memory_space=pltpu.SEMAPHORE)


def _params(*sem):
    return pltpu.CompilerParams(dimension_semantics=sem or None, vmem_limit_bytes=VMEM_LIMIT)


def _silu(z):
    return z * jax.nn.sigmoid(z)


def _dsilu(z):
    s = jax.nn.sigmoid(z)
    return s * (1.0 + z * (1.0 - s))


def _my_place():
    x, y, c = lax.axis_index("x"), lax.axis_index("y"), lax.axis_index("c")
    return x, y, c, 4 * x + 2 * y + c


def _peer(x, y, c, k):
    px, py, pc = x ^ (k >> 2 & 1), y ^ (k >> 1 & 1), c ^ (k & 1)
    return (px, py, pc), 4 * px + 2 * py + pc


def _all_gather(arrays, name):
    n = len(arrays)

    def body(*refs):
        srcs, dsts = refs[:n], refs[n:2 * n]
        send_sems, recv_sems, local_sems = refs[2 * n:]
        x, y, c, me = _my_place()
        locals_, sends = [], []
        for t in range(n):
            own = pltpu.make_async_copy(srcs[t], dsts[t].at[me], local_sems.at[t])
            own.start()
            locals_.append(own)
            for k in range(1, NDEV):
                peer, pidx = _peer(x, y, c, k)
                cp = pltpu.make_async_remote_copy(
                    src_ref=srcs[t], dst_ref=dsts[t].at[me], send_sem=send_sems.at[t, k],
                    recv_sem=recv_sems.at[t, k], device_id=peer, device_id_type=MESH)
                cp.start()
                sends.append(cp)
        for t in range(n):
            for k in range(1, NDEV):
                peer, pidx = _peer(x, y, c, k)
                pltpu.make_async_remote_copy(
                    src_ref=srcs[t], dst_ref=dsts[t].at[pidx], send_sem=send_sems.at[t, k],
                    recv_sem=recv_sems.at[t, k], device_id=peer, device_id_type=MESH).wait_recv()
        for cp in sends:
            cp.wait_send()
        for cp in locals_:
            cp.wait()

    return pl.pallas_call(
        body, name=name,
        out_shape=tuple(jax.ShapeDtypeStruct((NDEV,) + a.shape, a.dtype) for a in arrays),
        in_specs=[HBM_SPEC] * n, out_specs=tuple([HBM_SPEC] * n),
        scratch_shapes=[pltpu.SemaphoreType.DMA((n, NDEV)), pltpu.SemaphoreType.DMA((n, NDEV)),
                        pltpu.SemaphoreType.DMA((n,))],
    )(*arrays)


def _all_to_all(arrays, name):
    n = len(arrays)

    def body(*refs):
        srcs, dsts = refs[:n], refs[n:2 * n]
        send_sems, recv_sems, local_sems = refs[2 * n:]
        x, y, c, me = _my_place()
        locals_, sends = [], []
        for t in range(n):
            own = pltpu.make_async_copy(srcs[t].at[me], dsts[t].at[me], local_sems.at[t])
            own.start()
            locals_.append(own)
            for k in range(1, NDEV):
                peer, pidx = _peer(x, y, c, k)
                cp = pltpu.make_async_remote_copy(
                    src_ref=srcs[t].at[pidx], dst_ref=dsts[t].at[me], send_sem=send_sems.at[t, k],
                    recv_sem=recv_sems.at[t, k], device_id=peer, device_id_type=MESH)
                cp.start()
                sends.append(cp)
        for t in range(n):
            for k in range(1, NDEV):
                peer, pidx = _peer(x, y, c, k)
                pltpu.make_async_remote_copy(
                    src_ref=srcs[t].at[pidx], dst_ref=dsts[t].at[pidx], send_sem=send_sems.at[t, k],
                    recv_sem=recv_sems.at[t, k], device_id=peer, device_id_type=MESH).wait_recv()
        for cp in sends:
            cp.wait_send()
        for cp in locals_:
            cp.wait()

    return pl.pallas_call(
        body, name=name,
        out_shape=tuple(jax.ShapeDtypeStruct(a.shape, a.dtype) for a in arrays),
        in_specs=[HBM_SPEC] * n, out_specs=tuple([HBM_SPEC] * n),
        scratch_shapes=[pltpu.SemaphoreType.DMA((n, NDEV)), pltpu.SemaphoreType.DMA((n, NDEV)),
                        pltpu.SemaphoreType.DMA((n,))],
    )(*arrays)


def _comm_call(name, arrays, sems, new_sems, body, after=(), token=False):
    na, ns, nn, nf = len(arrays), len(sems), len(new_sems), len(after)

    def kern(*refs):
        ins, outs = refs[:na + ns + nf], refs[na + ns + nf:]
        body(ins[:na], ins[na:na + ns], outs[:nn])
        if token:
            outs[nn + na][...] = jnp.zeros((8, 128), f32)

    out_shape = ([pltpu.SemaphoreType.DMA(s) for s in new_sems] + [pltpu.HBM(a.shape, a.dtype) for a in arrays]
                 + ([jax.ShapeDtypeStruct((8, 128), f32)] if token else []))
    out_specs = [SEM_SPEC] * nn + [HBM_SPEC] * na + ([pl.BlockSpec(memory_space=pltpu.VMEM)] if token else [])
    res = pl.pallas_call(
        kern, name=name, out_shape=tuple(out_shape),
        in_specs=[HBM_SPEC] * na + [SEM_SPEC] * ns + [ANY_SPEC] * nf, out_specs=tuple(out_specs),
        input_output_aliases={t: nn + t for t in range(na)},
        compiler_params=pltpu.CompilerParams(has_side_effects=pltpu.SideEffectType.DATAFLOW_SIDE_EFFECTING),
    )(*[pltpu.with_memory_space_constraint(a, pltpu.HBM) for a in arrays], *sems, *after)
    return list(res[:nn]), list(res[nn:nn + na]), (res[nn + na] if token else None)


def _remote(src, dst, send_sem, recv_sem, device):
    return pltpu.make_async_remote_copy(src_ref=src, dst_ref=dst, send_sem=send_sem, recv_sem=recv_sem,
                                        device_id=device, device_id_type=MESH)


SAME_CORE = (2, 4, 6)
VIA_SIBLING = (3, 5, 7)


def _weights_start(lands, after):
    n = len(lands)

    def body(a, s, new):
        x, y, c, me = _my_place()
        for t, land in enumerate(a):
            send, recv = new[t], new[n + t]
            for k in (1,) + SAME_CORE:
                peer, _ = _peer(x, y, c, k)
                _remote(land.at[me], land.at[me], send.at[k], recv.at[k], peer).start()

    sems, lands, token = _comm_call("weights_start", lands, [], [(NDEV,)] * (2 * n), body, after=after, token=True)
    return sems[:n], sems[n:], lands, token


def _weights_forward(name, land, recv, after):
    def body(a, s, new):
        (land,), (recv,), (fsend, frecv) = a, s, new
        x, y, c, me = _my_place()
        sibling, _ = _peer(x, y, c, 1)
        for k in SAME_CORE:
            peer, slot = _peer(x, y, c, k)
            _remote(land.at[slot], land.at[slot], fsend.at[k], recv.at[k], peer).wait_recv()
            _remote(land.at[slot], land.at[slot], fsend.at[k], frecv.at[k ^ 1], sibling).start()

    return _comm_call(name, [land], [recv], [(NDEV,), (NDEV,)], body, after=after)


def _weights_wait(name, land, send, recv, fsend, frecv, after):
    def body(a, s, new):
        (land,), (send, recv, fsend, frecv) = a, s
        x, y, c, me = _my_place()
        sibling, sib_slot = _peer(x, y, c, 1)
        _remote(land.at[sib_slot], land.at[sib_slot], send.at[1], recv.at[1], sibling).wait_recv()
        for k in VIA_SIBLING:
            _, slot = _peer(x, y, c, k)
            _remote(land.at[slot], land.at[slot], fsend.at[k ^ 1], frecv.at[k], sibling).wait_recv()
        for k in (1,) + SAME_CORE:
            peer, _ = _peer(x, y, c, k)
            _remote(land.at[me], land.at[me], send.at[k], recv.at[k], peer).wait_send()
        for k in SAME_CORE:
            _, slot = _peer(x, y, c, k)
            _remote(land.at[slot], land.at[slot], fsend.at[k], frecv.at[k ^ 1], sibling).wait_send()

    return _comm_call(name, [land], [send, recv, fsend, frecv], [], body, after=after)[1][0]


NCHIP = NDEV // 2


def _pair_start(name, src):
    npair = src.shape[0] // 2

    def body(a, s, new):
        (src, pair), (send, recv) = a, new
        x, y, c, me = _my_place()
        sibling, _ = _peer(x, y, c, 1)
        for i in range(npair):
            _remote(src.at[2 * i + 1 - c], pair.at[i], send.at[i], recv.at[i], sibling).start()

    pair = lax.empty((npair,) + src.shape[1:], src.dtype)
    (send, recv), (src, pair), token = _comm_call(name, [src, pair], [], [(npair,), (npair,)], body, token=True)
    return send, recv, src, pair, token


def _pair_wait(name, src, pair, send, recv, after):
    npair = pair.shape[0]

    def body(a, s, new):
        (src, pair), (send, recv) = a, s
        x, y, c, me = _my_place()
        sibling, _ = _peer(x, y, c, 1)
        for i in range(npair):
            cp = _remote(src.at[2 * i + 1 - c], pair.at[i], send.at[i], recv.at[i], sibling)
            cp.wait_recv()
            cp.wait_send()

    return _comm_call(name, [src, pair], [send, recv], [], body, after=after)[1]


def _pair_sum(name, src, pair, core, tr=256):
    npair, R, Cc = pair.shape
    tr = min(tr, R)

    def body(core_ref, a_ref, b_ref, o_ref):
        o_ref[...] = (a_ref[...].astype(f32) + b_ref[...].astype(f32)).astype(o_ref.dtype)

    grid_spec = pltpu.PrefetchScalarGridSpec(
        num_scalar_prefetch=1, grid=(npair, R // tr),
        in_specs=[pl.BlockSpec((None, tr, Cc), lambda i, r, core: (2 * i + core[0], r, 0)),
                  pl.BlockSpec((None, tr, Cc), lambda i, r, core: (i, r, 0))],
        out_specs=pl.BlockSpec((None, tr, Cc), lambda i, r, core: (i, r, 0)))
    return pl.pallas_call(body, name=name, out_shape=jax.ShapeDtypeStruct(pair.shape, pair.dtype),
                          grid_spec=grid_spec, compiler_params=_params("parallel", "parallel"))(core, src, pair)


def _owner_chip(first, i):
    q = first // 2 + i
    return q >> 1 & 1, q & 1


def _chip_start(name, sums, land, first):
    npair = sums.shape[0]

    def body(a, s, new):
        (sums, land), (send, recv) = a, new
        x, y, c, me = _my_place()
        for i in range(npair):
            ox, oy = _owner_chip(first, i)

            @pl.when((x != ox) | (y != oy))
            def _():
                _remote(sums.at[i], land.at[2 * x + y], send.at[i], recv.at[2 * x + y], (ox, oy, c)).start()

    (send, recv), (sums, land), token = _comm_call(name, [sums, land], [], [(npair,), (NCHIP,)], body, token=True)
    return send, recv, sums, land, token


def _chip_wait(name, sums, land, send, recv, first, after):
    npair = sums.shape[0]

    def body(a, s, new):
        (sums, land), (send, recv) = a, s
        x, y, c, me = _my_place()
        mine = (me >= first) & (me < first + 2 * npair)
        for i in range(npair):
            ox, oy = _owner_chip(first, i)

            @pl.when((x != ox) | (y != oy))
            def _():
                _remote(sums.at[i], land.at[2 * x + y], send.at[i], recv.at[2 * x + y], (ox, oy, c)).wait_send()
        for q in range(NCHIP):
            @pl.when(mine & (2 * x + y != q))
            def _():
                _remote(sums.at[0], land.at[q], send.at[0], recv.at[q], (q >> 1, q & 1, c)).wait_recv()

    return _comm_call(name, [sums, land], [send, recv], [], body, after=after)[1]


def _matmul(a, b, *, name, out_dtype, ta=False, tb=False, b_slots=False, out_slots=0, b_cols=None,
            tm=512, tn=512, tk=512, dep=None):
    M, K = (a.shape[1], a.shape[0]) if ta else a.shape
    col0 = 0
    if b_slots:
        slab = b.shape[2]
        N = b.shape[1] if tb else NDEV * slab
        assert (K if tb else N) == NDEV * slab
    elif b_cols is not None:
        assert not tb
        col0, N = b_cols
    else:
        N = b.shape[0] if tb else b.shape[1]
    tm, tn, tk = min(tm, M), min(tn, N), min(tk, K)
    if b_slots:
        if tb:
            tk = min(tk, slab)
        else:
            tn = min(tn, slab)
    if out_slots:
        tn = min(tn, N // out_slots)
    nm, nn, nk = M // tm, N // tn, K // tk
    assert (nm * tm, nn * tn, nk * tk) == (M, N, K) and col0 % tn == 0, (name, M, N, K, tm, tn, tk)
    j0 = col0 // tn

    a_spec = pl.BlockSpec((tk, tm), lambda i, j, k: (k, i)) if ta else pl.BlockSpec((tm, tk), lambda i, j, k: (i, k))
    if b_slots and tb:
        per = slab // tk
        b_spec = pl.BlockSpec((None, tn, tk), lambda i, j, k: (k // per, j, k % per))
    elif b_slots:
        per = slab // tn
        b_spec = pl.BlockSpec((None, tk, tn), lambda i, j, k: (j // per, k, j % per))
    elif tb:
        b_spec = pl.BlockSpec((tn, tk), lambda i, j, k: (j, k))
    else:
        b_spec = pl.BlockSpec((tk, tn), lambda i, j, k: (k, j + j0))
    if out_slots:
        per_o = (N // out_slots) // tn
        o_spec = pl.BlockSpec((None, tm, tn), lambda i, j, k: (j // per_o, i, j % per_o))
        out_shape = jax.ShapeDtypeStruct((out_slots, M, N // out_slots), out_dtype)
    else:
        o_spec = pl.BlockSpec((tm, tn), lambda i, j, k: (i, j))
        out_shape = jax.ShapeDtypeStruct((M, N), out_dtype)
    dims = (((0 if ta else 1,), (1 if tb else 0,)), ((), ()))
    deps = [] if dep is None else [dep]

    def body(a_ref, b_ref, *rest):
        o_ref, acc_ref = rest[len(deps):]
        k = pl.program_id(2)

        @pl.when(k == 0)
        def _():
            acc_ref[...] = jnp.zeros_like(acc_ref)

        acc_ref[...] += lax.dot_general(a_ref[...], b_ref[...], dims, preferred_element_type=f32)

        @pl.when(k == nk - 1)
        def _():
            o_ref[...] = acc_ref[...].astype(out_dtype)

    return pl.pallas_call(
        body, name=name, out_shape=out_shape, grid=(nm, nn, nk),
        in_specs=[a_spec, b_spec] + [ANY_SPEC] * len(deps), out_specs=o_spec,
        scratch_shapes=[pltpu.VMEM((tm, tn), f32)],
        compiler_params=_params("parallel", "parallel", "arbitrary"),
    )(a, b, *deps)


def _silu_rows(c):
    D = c.shape[1]

    def body(c_ref, o_ref):
        o_ref[...] = jnp.broadcast_to(_silu(c_ref[...]), (8, D))

    return pl.pallas_call(body, name="silu_c", out_shape=jax.ShapeDtypeStruct((8, D), f32))(c)


def _ada_cols(c_all, w_ada, b_cols):
    D, W = w_ada.shape

    def body(c_ref, w_ref, b_ref, o_ref):
        mod = lax.dot_general(c_ref[...], w_ref[...], (((1,), (0,)), ((), ())), preferred_element_type=f32,
                              precision=lax.Precision.HIGHEST) + b_ref[...]
        for j in range(NDEV):
            o_ref[j] = jnp.broadcast_to(mod[j:j + 1, :], (8, W))

    return pl.pallas_call(body, name="ada_cols", out_shape=jax.ShapeDtypeStruct((NDEV, 8, W), f32),
                          compiler_params=_params())(c_all, w_ada, b_cols)


def _prenorm(x, scale, shift, g_pre, dep, tr=256):
    S, D = x.shape
    tr = min(tr, S)

    def body(x_ref, sc_ref, sh_ref, g_ref, dep_ref, h_ref):
        xv = x_ref[...]
        r = lax.rsqrt(jnp.mean(xv * xv, axis=-1, keepdims=True) + EPS)
        h_ref[...] = ((xv * r) * g_ref[...] * (1.0 + sc_ref[...]) + sh_ref[...]).astype(bf16)

    row = pl.BlockSpec((tr, D), lambda i: (i, 0))
    vec = pl.BlockSpec((1, D), lambda i: (0, 0))
    return pl.pallas_call(body, name="prenorm", out_shape=jax.ShapeDtypeStruct((S, D), bf16), grid=(S // tr,),
                          in_specs=[row, vec, vec, vec, ANY_SPEC], out_specs=row, compiler_params=_params("parallel"))(
                              x, scale, shift, g_pre, dep)


def _ext_rows(i, tr, S):
    g = lax.broadcasted_iota(jnp.int32, (tr + 16, 1), 0) + (i * tr - 8)
    return (g >= 0) & (g < S)


def _halo_specs(tr, S, C, col):
    nb8 = S // 8
    main = pl.BlockSpec((tr, C), lambda i: (i, col))
    prev = pl.BlockSpec((8, C), lambda i: (jnp.maximum(i * (tr // 8) - 1, 0), col))
    nxt = pl.BlockSpec((8, C), lambda i: (jnp.minimum((i + 1) * (tr // 8), nb8 - 1), col))
    return prev, main, nxt


def _conv_fwd(proj, conv_w, conv_b, g_conv, tr=256):
    S, C = proj.shape[0], proj.shape[1] // 8
    tr = min(tr, S)

    def body(up, um, un, cp, cm, cn, bg_ref, zc_ref, w_ref, cb_ref, g_ref, o_ref):
        i = pl.program_id(0)
        exists = _ext_rows(i, tr, S)
        u = jnp.concatenate([up[...], um[...], un[...]], axis=0)
        cg = jnp.concatenate([cp[...], cm[...], cn[...]], axis=0)
        t = jnp.where(exists, cg * u, 0.0)
        t_before = pltpu.roll(t, 1, 0)[8:tr + 8]
        t_after = pltpu.roll(t, tr + 15, 0)[8:tr + 8]
        w = w_ref[...]
        cv = w[0:1] * t_before + w[1:2] * t[8:tr + 8] + w[2:3] * t_after + cb_ref[...]
        yc = bg_ref[...] * cv
        rc = lax.rsqrt(jnp.mean(yc * yc, axis=-1, keepdims=True) + EPS)
        o_ref[...] = ((yc * rc) * g_ref[...] * _silu(zc_ref[...])).astype(bf16)

    u_specs = _halo_specs(tr, S, C, 0)
    c_specs = _halo_specs(tr, S, C, 2)
    vec = pl.BlockSpec((1, C), lambda i: (0, 0))
    return pl.pallas_call(
        body, name="conv_fwd", out_shape=jax.ShapeDtypeStruct((S, 2 * C), bf16), grid=(S // tr,),
        in_specs=[*u_specs, *c_specs, pl.BlockSpec((tr, C), lambda i: (i, 1)), pl.BlockSpec((tr, C), lambda i: (i, 3)),
                  pl.BlockSpec((8, C), lambda i: (0, 0)), vec, vec],
        out_specs=pl.BlockSpec((tr, C), lambda i: (i, 0)), compiler_params=_params("parallel"),
    )(proj, proj, proj, proj, proj, proj, proj, proj, conv_w, conv_b, g_conv)


def _branch_geometry(S, r):
    L = S // r
    nq = min(128, L)
    nk = min(nq + 2 * HALF_WIN, L)
    return L, nq, nk, L // nq


def _block_rows(idx, r, L, nq, nk, nblk):
    if r == 1:
        rho, qb = 0, idx
    else:
        rho, qb = idx // nblk, idx % nblk
    i0 = qb * nq
    ws = jnp.clip(i0 - HALF_WIN, 0, L - nk)
    if r == 1:
        qrows = pl.ds(pl.multiple_of(i0, 8), nq)
        krows = pl.ds(pl.multiple_of(ws, 8), nk)
    else:
        qrows = pl.ds(rho + r * i0, nq, stride=r)
        krows = pl.ds(rho + r * ws, nk, stride=r)
    return qrows, krows, i0 - ws


def _band(nq, nk, off, r):
    d = jnp.abs(lax.broadcasted_iota(jnp.int32, (nq, nk), 0) - lax.broadcasted_iota(jnp.int32, (nq, nk), 1) + off)
    return d <= HALF_WIN, d.astype(f32) * float(r)


def _head_slopes(n_heads):
    slopes = 2.0 ** (-8.0 * jnp.arange(1, n_heads + 1, dtype=f32) / n_heads)
    return jnp.broadcast_to(jnp.repeat(slopes.reshape(n_heads // 2, 2), HEAD_DIM, axis=1)[:, None, :],
                            (n_heads // 2, 8, PAIR))


def _attn_fwd(proj, slopes):
    S, C = proj.shape[0], proj.shape[1] // 8
    npair = C // PAIR

    def body(q_ref, k_ref, v_ref, sl_ref, o_ref, lse_ref, m_scr, l_scr, a_scr):
        lane = lax.broadcasted_iota(jnp.int32, (1, PAIR), 1)
        first = lane < HEAD_DIM
        sl = sl_ref[...]
        slope = (sl[0:1, 0:1], sl[0:1, HEAD_DIM:HEAD_DIM + 1])

        for b, (_, r) in enumerate(BRANCHES):
            L, nq, nk, nblk = _branch_geometry(S, r)

            def step(idx, carry, b=b, r=r, L=L, nq=nq, nk=nk, nblk=nblk):
                qrows, krows, off = _block_rows(idx, r, L, nq, nk, nblk)
                q2 = q_ref[qrows, :]
                k2 = k_ref[krows, :].astype(bf16)
                v2 = v_ref[krows, :].astype(bf16)
                valid, dist = _band(nq, nk, off, r)
                ms, ls, accs = [], [], []
                for hh in range(2):
                    qh = jnp.where(first if hh == 0 else ~first, q2, 0.0).astype(bf16)
                    s = lax.dot_general(qh, k2, (((1,), (1,)), ((), ())), preferred_element_type=f32) * HEAD_DIM ** -0.5
                    s = jnp.where(valid, s - slope[hh] * dist, NEG_INF)
                    m = jnp.max(s, axis=-1, keepdims=True)
                    p = jnp.exp(s - m)
                    ms.append(m)
                    ls.append(jnp.sum(p, axis=-1, keepdims=True))
                    accs.append(jnp.dot(p.astype(bf16), v2, preferred_element_type=f32))
                m_scr[b, qrows, :] = jnp.where(first, ms[0], ms[1])
                l_scr[b, qrows, :] = jnp.where(first, ls[0], ls[1])
                a_scr[b, qrows, :] = jnp.where(first, accs[0], accs[1])
                return carry

            lax.fori_loop(0, S // nq, step, 0)

        ch = min(256, S)

        def merge(i, carry):
            rows = pl.ds(pl.multiple_of(i * ch, 8), ch)
            m = jnp.maximum(jnp.maximum(m_scr[0, rows, :], m_scr[1, rows, :]), m_scr[2, rows, :])
            l = jnp.zeros((ch, PAIR), f32)
            acc = jnp.zeros((ch, PAIR), f32)
            for b in range(3):
                w = jnp.exp(m_scr[b, rows, :] - m)
                l = l + w * l_scr[b, rows, :]
                acc = acc + w * a_scr[b, rows, :]
            o_ref[rows, :] = acc / l
            lse_ref[rows, :] = m + jnp.log(l)
            return carry

        lax.fori_loop(0, S // ch, merge, 0)

    blk = lambda part: pl.BlockSpec((S, PAIR), lambda p: (0, part * npair + p))
    out = pl.BlockSpec((S, PAIR), lambda p: (0, p))
    return pl.pallas_call(
        body, name="attn_fwd",
        out_shape=(jax.ShapeDtypeStruct((S, C), f32), jax.ShapeDtypeStruct((S, C), f32)), grid=(npair,),
        in_specs=[blk(4), blk(5), blk(6), pl.BlockSpec((None, 8, PAIR), lambda p: (p, 0, 0))],
        out_specs=(out, out),
        scratch_shapes=[pltpu.VMEM((3, S, PAIR), f32)] * 3, compiler_params=_params("parallel"),
    )(proj, proj, proj, slopes)


def _attn_post(ycat, o, proj, g_attn, tr=256):
    S, C = o.shape
    tr = min(tr, S)

    def body(y_ref, o_ref, z_ref, g_ref, out_ref):
        del y_ref
        ov = o_ref[...]
        ra = lax.rsqrt(jnp.mean(ov * ov, axis=-1, keepdims=True) + EPS)
        out_ref[...] = ((ov * ra) * g_ref[...] * _silu(z_ref[...])).astype(bf16)

    return pl.pallas_call(
        body, name="attn_post", out_shape=jax.ShapeDtypeStruct(ycat.shape, ycat.dtype), grid=(S // tr,),
        in_specs=[HBM_SPEC, pl.BlockSpec((tr, C), lambda i: (i, 0)), pl.BlockSpec((tr, C), lambda i: (i, 7)),
                  pl.BlockSpec((1, C), lambda i: (0, 0))],
        out_specs=pl.BlockSpec((tr, C), lambda i: (i, 1)), input_output_aliases={0: 0},
        compiler_params=_params("arbitrary"),
    )(ycat, o, proj, g_attn)


def _sandwich(y, x, target, gate, g_post, tr=256):
    S, D = y.shape
    tr = min(tr, S)

    def body(y_ref, x_ref, t_ref, gate_ref, g_ref, dy_ref, dout_ref, sums_ref):
        i = pl.program_id(0)
        yv = y_ref[...]
        rp = lax.rsqrt(jnp.mean(yv * yv, axis=-1, keepdims=True) + EPS)
        yhat = yv * rp
        yn = yhat * g_ref[...]
        err = (x_ref[...] + gate_ref[...] * yn) - t_ref[...]
        dout = err * (1.0 / D)
        dout_ref[...] = dout
        dyn = dout * gate_ref[...]
        w = dyn * g_ref[...]
        dy_ref[...] = (rp * (w - yhat * jnp.mean(w * yhat, axis=-1, keepdims=True))).astype(bf16)
        loss = 0.5 * jnp.sum(jnp.mean(err * err, axis=-1, keepdims=True), axis=0, keepdims=True)
        row = lax.broadcasted_iota(jnp.int32, (8, D), 0)
        upd = jnp.where(row == 0, jnp.sum(dout * yn, axis=0, keepdims=True),
                        jnp.where(row == 1, jnp.sum(dyn * yhat, axis=0, keepdims=True),
                                  jnp.where(row == 2, loss, 0.0)))

        @pl.when(i == 0)
        def _():
            sums_ref[...] = upd

        @pl.when(i > 0)
        def _():
            sums_ref[...] += upd

    row = pl.BlockSpec((tr, D), lambda i: (i, 0))
    vec = pl.BlockSpec((1, D), lambda i: (0, 0))
    return pl.pallas_call(
        body, name="sandwich",
        out_shape=(jax.ShapeDtypeStruct((S, D), bf16), jax.ShapeDtypeStruct((S, D), f32), jax.ShapeDtypeStruct((8, D), f32)),
        grid=(S // tr,), in_specs=[row, row, row, vec, vec],
        out_specs=(row, row, pl.BlockSpec((8, D), lambda i: (0, 0))), compiler_params=_params("arbitrary"),
    )(y, x, target, gate, g_post)


def _conv_bwd(proj, dycat, conv_w, conv_b, g_conv, dep, tr=256):
    S, C = proj.shape[0], proj.shape[1] // 8
    tr = min(tr, S)
    n = tr + 16

    def body(*refs):
        ins, (w_ref, cb_ref, g_ref, _, dp_ref, sums_ref) = refs[:15], refs[15:]
        i = pl.program_id(0)
        exists = _ext_rows(i, tr, S)
        u, bg, cg, zc, dyn = (jnp.concatenate([ins[3 * t][...], ins[3 * t + 1][...], ins[3 * t + 2][...]], axis=0)
                              for t in range(5))
        w = w_ref[...]
        t = jnp.where(exists, cg * u, 0.0)
        t_before, t_after = pltpu.roll(t, 1, 0), pltpu.roll(t, n - 1, 0)
        cv = w[0:1] * t_before + w[1:2] * t + w[2:3] * t_after + cb_ref[...]
        yc = bg * cv
        rc = lax.rsqrt(jnp.mean(yc * yc, axis=-1, keepdims=True) + EPS)
        yhat = yc * rc
        sz = _silu(zc)
        wgt = dyn * g_ref[...] * sz
        dyc = rc * (wgt - yhat * jnp.mean(wgt * yhat, axis=-1, keepdims=True))
        dcv = jnp.where(exists, dyc * bg, 0.0)
        dt = w[0:1] * pltpu.roll(dcv, n - 1, 0) + w[1:2] * dcv + w[2:3] * pltpu.roll(dcv, 1, 0)
        mid = slice(8, tr + 8)
        dp_ref[:, 0:C] = (dt * cg)[mid].astype(bf16)
        dp_ref[:, C:2 * C] = (dyc * cv)[mid].astype(bf16)
        dp_ref[:, 2 * C:3 * C] = (dt * u)[mid].astype(bf16)
        dp_ref[:, 3 * C:4 * C] = (dyn * yhat * g_ref[...] * _dsilu(zc))[mid].astype(bf16)
        colsum = lambda v: jnp.sum(v[mid], axis=0, keepdims=True)
        parts = [colsum(dyn * yhat * sz), colsum(dcv), colsum(dcv * t_before), colsum(dcv * t), colsum(dcv * t_after)]
        row = lax.broadcasted_iota(jnp.int32, (8, C), 0)
        upd = jnp.zeros((8, C), f32)
        for j, pj in enumerate(parts):
            upd = jnp.where(row == j, pj, upd)

        @pl.when(i == 0)
        def _():
            sums_ref[...] = upd

        @pl.when(i > 0)
        def _():
            sums_ref[...] += upd

    specs = []
    for col in range(4):
        specs += _halo_specs(tr, S, C, col)
    specs += _halo_specs(tr, S, C, 0)
    vec = pl.BlockSpec((1, C), lambda i: (0, 0))
    return pl.pallas_call(
        body, name="conv_bwd",
        out_shape=(jax.ShapeDtypeStruct((S, 4 * C), bf16), jax.ShapeDtypeStruct((8, C), f32)), grid=(S // tr,),
        in_specs=[*specs, pl.BlockSpec((8, C), lambda i: (0, 0)), vec, vec, ANY_SPEC],
        out_specs=(pl.BlockSpec((tr, 4 * C), lambda i: (i, 0)), pl.BlockSpec((8, C), lambda i: (0, 0))),
        compiler_params=_params("arbitrary"),
    )(*([proj] * 12), dycat, dycat, dycat, conv_w, conv_b, g_conv, dep)


def _attn_post_bwd(o, proj, dycat, g_attn, dep, tr=256):
    S, C = o.shape
    tr = min(tr, S)

    def body(o_ref, z_ref, dy_ref, g_ref, dep_ref, do_ref, dz_ref, sums_ref):
        i = pl.program_id(0)
        ov, zv, dyn = o_ref[...], z_ref[...], dy_ref[...]
        ra = lax.rsqrt(jnp.mean(ov * ov, axis=-1, keepdims=True) + EPS)
        ohat = ov * ra
        sz = _silu(zv)
        wgt = dyn * g_ref[...] * sz
        do_ref[...] = ra * (wgt - ohat * jnp.mean(wgt * ohat, axis=-1, keepdims=True))
        dz_ref[...] = (dyn * ohat * g_ref[...] * _dsilu(zv)).astype(bf16)
        row = lax.broadcasted_iota(jnp.int32, (8, C), 0)
        upd = jnp.where(row == 0, jnp.sum(dyn * ohat * sz, axis=0, keepdims=True), 0.0)

        @pl.when(i == 0)
        def _():
            sums_ref[...] = upd

        @pl.when(i > 0)
        def _():
            sums_ref[...] += upd

    return pl.pallas_call(
        body, name="attn_post_bwd",
        out_shape=(jax.ShapeDtypeStruct((S, C), f32), jax.ShapeDtypeStruct((S, C), bf16), jax.ShapeDtypeStruct((8, C), f32)),
        grid=(S // tr,),
        in_specs=[pl.BlockSpec((tr, C), lambda i: (i, 0)), pl.BlockSpec((tr, C), lambda i: (i, 7)),
                  pl.BlockSpec((tr, C), lambda i: (i, 1)), pl.BlockSpec((1, C), lambda i: (0, 0)), ANY_SPEC],
        out_specs=(pl.BlockSpec((tr, C), lambda i: (i, 0)), pl.BlockSpec((tr, C), lambda i: (i, 0)),
                   pl.BlockSpec((8, C), lambda i: (0, 0))),
        compiler_params=_params("arbitrary"),
    )(o, proj, dycat, g_attn, dep)


def _attn_bwd(proj, o, do, lse, slopes, dep):
    S, C = o.shape
    npair = C // PAIR

    def body(q_ref, k_ref, v_ref, o_ref, do_ref, lse_ref, sl_ref, dep_ref, dq_ref, dk_ref, dv_ref,
             dq_scr, dk_scr, dv_scr, dl_scr):
        lane = lax.broadcasted_iota(jnp.int32, (1, PAIR), 1)
        first = lane < HEAD_DIM
        sl = sl_ref[...]
        slope = (sl[0:1, 0:1], sl[0:1, HEAD_DIM:HEAD_DIM + 1])
        ch = min(256, S)

        def prep(i, carry):
            rows = pl.ds(pl.multiple_of(i * ch, 8), ch)
            prod = do_ref[rows, :] * o_ref[rows, :]
            d0 = jnp.sum(jnp.where(first, prod, 0.0), axis=-1, keepdims=True)
            d1 = jnp.sum(jnp.where(first, 0.0, prod), axis=-1, keepdims=True)
            dl_scr[rows, :] = jnp.where(first, d0, d1)
            zero = jnp.zeros((ch, PAIR), f32)
            dq_scr[rows, :] = zero
            dk_scr[rows, :] = zero
            dv_scr[rows, :] = zero
            return carry

        lax.fori_loop(0, S // ch, prep, 0)

        for _, r in BRANCHES:
            L, nq, nk, nblk = _branch_geometry(S, r)

            def step(idx, carry, r=r, L=L, nq=nq, nk=nk, nblk=nblk):
                qrows, krows, off = _block_rows(idx, r, L, nq, nk, nblk)
                q2 = q_ref[qrows, :]
                k2 = k_ref[krows, :].astype(bf16)
                v2 = v_ref[krows, :].astype(bf16)
                do2 = do_ref[qrows, :]
                lse2 = lse_ref[qrows, :]
                dl2 = dl_scr[qrows, :]
                valid, dist = _band(nq, nk, off, r)
                dq2 = jnp.zeros((nq, PAIR), f32)
                dk2 = jnp.zeros((nk, PAIR), f32)
                dv2 = jnp.zeros((nk, PAIR), f32)
                for hh in range(2):
                    mine = first if hh == 0 else ~first
                    lo = hh * HEAD_DIM
                    qh = jnp.where(mine, q2, 0.0).astype(bf16)
                    doh = jnp.where(mine, do2, 0.0).astype(bf16)
                    s = lax.dot_general(qh, k2, (((1,), (1,)), ((), ())), preferred_element_type=f32) * HEAD_DIM ** -0.5
                    s = jnp.where(valid, s - slope[hh] * dist, NEG_INF)
                    p = jnp.exp(s - lse2[:, lo:lo + 1])
                    dv2 = dv2 + lax.dot_general(p.astype(bf16), doh, (((0,), (0,)), ((), ())), preferred_element_type=f32)
                    dp = lax.dot_general(doh, v2, (((1,), (1,)), ((), ())), preferred_element_type=f32)
                    ds = (p * (dp - dl2[:, lo:lo + 1]) * HEAD_DIM ** -0.5).astype(bf16)
                    dq2 = dq2 + jnp.where(mine, jnp.dot(ds, k2, preferred_element_type=f32), 0.0)
                    dk2 = dk2 + lax.dot_general(ds, qh, (((0,), (0,)), ((), ())), preferred_element_type=f32)
                dq_scr[qrows, :] = dq_scr[qrows, :] + dq2
                dk_scr[krows, :] = dk_scr[krows, :] + dk2
                dv_scr[krows, :] = dv_scr[krows, :] + dv2
                return carry

            lax.fori_loop(0, S // nq, step, 0)

        dq_ref[...] = dq_scr[...].astype(bf16)
        dk_ref[...] = dk_scr[...].astype(bf16)
        dv_ref[...] = dv_scr[...].astype(bf16)

    blk = lambda part: pl.BlockSpec((S, PAIR), lambda p: (0, part * npair + p))
    own = pl.BlockSpec((S, PAIR), lambda p: (0, p))
    grad = jax.ShapeDtypeStruct((S, C), bf16)
    return pl.pallas_call(
        body, name="attn_bwd", out_shape=(grad, grad, grad), grid=(npair,),
        in_specs=[blk(4), blk(5), blk(6), own, own, own, pl.BlockSpec((None, 8, PAIR), lambda p: (p, 0, 0)), ANY_SPEC],
        out_specs=(own, own, own), scratch_shapes=[pltpu.VMEM((S, PAIR), f32)] * 4,
        compiler_params=_params("parallel"),
    )(proj, proj, proj, o, do, lse, slopes, dep)


def _prenorm_bwd(dh, x, dout, scale, g_pre, tr=256):
    S, D = x.shape
    tr = min(tr, S)

    def body(dh_ref, x_ref, dout_ref, sc_ref, g_ref, gx_ref, sums_ref):
        i = pl.program_id(0)
        xv, dhv = x_ref[...], dh_ref[...]
        r = lax.rsqrt(jnp.mean(xv * xv, axis=-1, keepdims=True) + EPS)
        xn = xv * r
        dxn = dhv * (g_ref[...] * (1.0 + sc_ref[...]))
        gx_ref[...] = dout_ref[...] + r * (dxn - xn * jnp.mean(dxn * xn, axis=-1, keepdims=True))
        dhx = dhv * xn
        row = lax.broadcasted_iota(jnp.int32, (8, D), 0)
        upd = jnp.where(row == 0, jnp.sum(dhv, axis=0, keepdims=True),
                        jnp.where(row == 1, jnp.sum(dhx, axis=0, keepdims=True) * g_ref[...],
                                  jnp.where(row == 2, jnp.sum(dhx, axis=0, keepdims=True) * (1.0 + sc_ref[...]), 0.0)))

        @pl.when(i == 0)
        def _():
            sums_ref[...] = upd

        @pl.when(i > 0)
        def _():
            sums_ref[...] += upd

    row = pl.BlockSpec((tr, D), lambda i: (i, 0))
    vec = pl.BlockSpec((1, D), lambda i: (0, 0))
    return pl.pallas_call(
        body, name="prenorm_bwd",
        out_shape=(jax.ShapeDtypeStruct((S, D), f32), jax.ShapeDtypeStruct((8, D), f32)), grid=(S // tr,),
        in_specs=[row, row, row, vec, vec], out_specs=(row, pl.BlockSpec((8, D), lambda i: (0, 0))),
        compiler_params=_params("arbitrary"),
    )(dh, x, dout, scale, g_pre)


def _adamw(w, g, m, v):
    m = ADAM_B1 * m + (1.0 - ADAM_B1) * g
    v = ADAM_B2 * v + (1.0 - ADAM_B2) * (g * g)
    m_hat = m / (1.0 - ADAM_B1 ** ADAM_STEP)
    v_hat = v / (1.0 - ADAM_B2 ** ADAM_STEP)
    delta = -ADAM_LR * (m_hat / (jnp.sqrt(v_hat) + ADAM_EPS) + ADAM_WD * w)
    return delta, m, v


def _sum_rows(parts):
    P = parts.shape[1]

    def body(p_ref, o_ref):
        acc = p_ref[0:1, :]
        for j in range(1, NDEV):
            acc = acc + p_ref[j:j + 1, :]
        o_ref[...] = jnp.broadcast_to(acc, (8, P))

    return pl.pallas_call(body, name="sum_small", out_shape=jax.ShapeDtypeStruct((8, P), f32),
                          compiler_params=_params())(parts)


def _adamw_small(w, g, m, v):
    def body(w_ref, g_ref, m_ref, v_ref, d_ref, nm_ref, nv_ref):
        d_ref[...], nm_ref[...], nv_ref[...] = _adamw(w_ref[...], g_ref[...], m_ref[...], v_ref[...])

    out = jax.ShapeDtypeStruct(w.shape, f32)
    return pl.pallas_call(body, name="adamw_small", out_shape=(out, out, out), compiler_params=_params())(w, g, m, v)


def _adamw_sharded(parts, own, w, m, v, name, tr=128):
    R, Cc = w.shape
    tr = min(tr, R)
    n = parts.shape[0]

    def body(p_ref, own_ref, w_ref, m_ref, v_ref, g_ref, d_ref, nm_ref, nv_ref):
        g = own_ref[...].astype(f32)
        for j in range(n):
            g = g + p_ref[j].astype(f32)
        g_ref[...] = g
        d_ref[...], nm_ref[...], nv_ref[...] = _adamw(w_ref[...], g, m_ref[...], v_ref[...])

    row = pl.BlockSpec((tr, Cc), lambda i: (i, 0))
    out = jax.ShapeDtypeStruct((R, Cc), f32)
    return pl.pallas_call(
        body, name=name, out_shape=(out, out, out, out), grid=(R // tr,),
        in_specs=[pl.BlockSpec((n, tr, Cc), lambda i: (0, i, 0)), row, row, row, row],
        out_specs=(row, row, row, row), compiler_params=_params("parallel"),
    )(parts, own, w, m, v)


def _adamw_ada(c_t, dmod_cols, w, m, v, tr=256):
    D, W = w.shape
    tr = min(tr, D)

    def body(c_ref, dm_ref, w_ref, m_ref, v_ref, g_ref, d_ref, nm_ref, nv_ref):
        cv, dm = c_ref[...], dm_ref[...]
        g = cv[:, 0:1] * dm[0:1, :]
        for b in range(1, NDEV):
            g = g + cv[:, b:b + 1] * dm[b:b + 1, :]
        g_ref[...] = g
        d_ref[...], nm_ref[...], nv_ref[...] = _adamw(w_ref[...], g, m_ref[...], v_ref[...])

    row = pl.BlockSpec((tr, W), lambda i: (i, 0))
    out = jax.ShapeDtypeStruct((D, W), f32)
    return pl.pallas_call(
        body, name="adamw_ada", out_shape=(out, out, out, out), grid=(D // tr,),
        in_specs=[pl.BlockSpec((tr, NDEV), lambda i: (i, 0)), pl.BlockSpec((NDEV, W), lambda i: (0, 0)), row, row, row],
        out_specs=(row, row, row, row), compiler_params=_params("parallel"),
    )(c_t, dmod_cols, w, m, v)


def kernel(x, c, w_ada, b_ada, g_pre, w_in, conv_w, conv_b, g_conv, g_attn, w_out, g_post, loss_target, m_w_ada, m_b_ada, m_g_pre, m_w_in, m_conv_w, m_conv_b, m_g_conv, m_g_attn, m_w_out, m_g_post, v_w_ada, v_b_ada, v_g_pre, v_w_in, v_conv_w, v_conv_b, v_g_conv, v_g_attn, v_w_out, v_g_post):
    S, D = x.shape[1], x.shape[2]
    C = D // 2
    W = w_ada.shape[2]
    CW = conv_w.shape[2]
    me = 4 * lax.axis_index("x") + 2 * lax.axis_index("y") + lax.axis_index("c")
    x2, tgt = x[0], loss_target[0]
    w_ada2, w_in2, w_out2 = w_ada[0], w_in[0], w_out[0]

    R = D // NDEV
    core = lax.axis_index("c").astype(jnp.int32).reshape(1)

    cw_slab = jnp.zeros((8, CW), f32).at[:3].set(conv_w[0])
    c_rows, cw_g = _all_gather([_silu_rows(c), cw_slab], "gather_c")
    c_all = c_rows[:, 0, :]
    conv_w_full = jnp.transpose(cw_g, (1, 0, 2)).reshape(8, C)
    b_cols = lax.dynamic_slice_in_dim(b_ada, me * W, W, axis=1)
    (mod_slabs,) = _all_to_all([_ada_cols(c_all, w_ada2, b_cols)], "scatter_mod")
    mod = mod_slabs[:, 0, :].reshape(1, 3 * D)
    shift, scale, gate = mod[:, :D], mod[:, D:2 * D], mod[:, 2 * D:]

    land_i = lax.dynamic_update_slice(lax.empty((NDEV, D, C), bf16), w_in2.astype(bf16)[None], (me, 0, 0))
    land_o = lax.dynamic_update_slice(lax.empty((NDEV, R, D), bf16), w_out2.astype(bf16)[None], (me, 0, 0))
    (wi_send, wo_send), (wi_recv, wo_recv), (land_i, land_o), w_token = _weights_start([land_i, land_o], [mod_slabs])

    h = _prenorm(x2, scale, shift, g_pre, w_token)
    (fi_send, fi_recv), (land_i,), _ = _weights_forward("w_in_forward", land_i, wi_recv, after=[h])
    win_g = _weights_wait("w_in_wait", land_i, wi_send, wi_recv, fi_send, fi_recv, after=[h])
    proj = _matmul(h, win_g, name="in_proj", out_dtype=f32, b_slots=True, tm=1024, tn=1024, tk=512)
    (fo_send, fo_recv), (land_o,), _ = _weights_forward("w_out_forward", land_o, wo_recv, after=[proj])
    slopes = _head_slopes(C // HEAD_DIM)
    ycat = _conv_fwd(proj, conv_w_full, conv_b, g_conv)
    o, lse = _attn_fwd(proj, slopes)
    ycat = _attn_post(ycat, o, proj, g_attn)
    wout_g = _weights_wait("w_out_wait", land_o, wo_send, wo_recv, fo_send, fo_recv, after=[ycat])
    wout_full = wout_g.reshape(D, D)
    y = _matmul(ycat, wout_full, name="out_proj", out_dtype=f32, tm=1024, tn=1024, tk=512)
    dy, dout, post_sums = _sandwich(y, x2, tgt, gate, g_post)

    gw_out = _matmul(ycat, dy, name="out_proj_dw", out_dtype=bf16, ta=True, tm=512, tn=1024, tk=512).reshape(NDEV, R, D)
    po_send, po_recv, gw_out, pair_o, po_token = _pair_start("g_out_pair_start", gw_out)
    dycat = _matmul(dy, wout_full, name="out_proj_dx", out_dtype=f32, tb=True, tm=1024, tn=1024, tk=512, dep=po_token)
    gw_out, pair_o = _pair_wait("g_out_pair_wait", gw_out, pair_o, po_send, po_recv, after=[dycat])
    sum_o = _pair_sum("g_out_pair_sum", gw_out, pair_o, core)
    co_send, co_recv, sum_o, land_go, co_token = _chip_start(
        "g_out_chip_start", sum_o, jnp.zeros((NCHIP, R, D), bf16), 0)
    dpc, conv_sums = _conv_bwd(proj, dycat, conv_w_full, conv_b, g_conv, co_token)
    gw_c = _matmul(h, dpc, name="in_proj_dw_conv", out_dtype=bf16, ta=True, out_slots=4, tm=1024, tn=1024, tk=512)
    pc_send, pc_recv, gw_c, pair_c, pc_token = _pair_start("g_conv_pair_start", gw_c)
    do, dza, attn_sums = _attn_post_bwd(o, proj, dycat, g_attn, pc_token)
    gw_c, pair_c = _pair_wait("g_conv_pair_wait", gw_c, pair_c, pc_send, pc_recv, after=[do])
    sum_c = _pair_sum("g_conv_pair_sum", gw_c, pair_c, core)
    cc_send, cc_recv, sum_c, land_gi, cc_token = _chip_start(
        "g_conv_chip_start", sum_c, jnp.zeros((NCHIP, D, C), bf16), 0)
    dq, dk, dv = _attn_bwd(proj, o, do, lse, slopes, cc_token)
    dproj = jnp.concatenate([dpc, dq, dk, dv, dza], axis=1)
    gw_a = _matmul(h, dproj, name="in_proj_dw_attn", out_dtype=bf16, ta=True, out_slots=4, b_cols=(4 * C, 4 * C),
                   tm=1024, tn=1024, tk=512)
    pa_send, pa_recv, gw_a, pair_a, pa_token = _pair_start("g_attn_pair_start", gw_a)
    gw_a, pair_a = _pair_wait("g_attn_pair_wait", gw_a, pair_a, pa_send, pa_recv, after=[pa_token])
    sum_a = _pair_sum("g_attn_pair_sum", gw_a, pair_a, core)
    ca_send, ca_recv, sum_a, land_gi, ca_token = _chip_start("g_attn_chip_start", sum_a, land_gi, 4)
    dh = _matmul(dproj, win_g, name="in_proj_dx", out_dtype=f32, tb=True, b_slots=True, tm=1024, tn=1024, tk=512,
                 dep=ca_token)
    grad_x, pre_sums = _prenorm_bwd(dh, x2, dout, scale, g_pre)

    small = jnp.concatenate([pre_sums[0:1], pre_sums[1:2], post_sums[0:1],
                             pre_sums[2:3], post_sums[1:2],
                             conv_sums[2:3], conv_sums[3:4], conv_sums[4:5],
                             conv_sums[1:2], conv_sums[0:1], attn_sums[0:1]], axis=1)
    (small_all,) = _all_gather([jnp.broadcast_to(small, (8, 8 * D))], "gather_small")
    small_all = small_all[:, 0, :]
    tot = _sum_rows(small_all)[0:1]
    loss = lax.psum(post_sums[2, 0], ("x", "y", "c"))

    g_b_ada = tot[:, :3 * D]
    g_g_pre, g_g_post = tot[:, 3 * D:4 * D], tot[:, 4 * D:5 * D]
    g_conv_w_full = tot[:, 5 * D:5 * D + 3 * C].reshape(3, C)
    g_conv_w = lax.dynamic_slice_in_dim(g_conv_w_full, me * CW, CW, axis=1)[None]
    g_conv_b, g_g_conv, g_g_attn = (tot[:, 5 * D + (3 + t) * C:5 * D + (4 + t) * C] for t in range(3))

    dmod_cols = lax.dynamic_slice_in_dim(small_all[:, :3 * D], me * W, W, axis=1)
    g_w_ada, d_w_ada, nm_w_ada, nv_w_ada = _adamw_ada(c_all.T, dmod_cols, w_ada2, m_w_ada[0], v_w_ada[0])

    sum_o, land_go = _chip_wait("g_out_chip_wait", sum_o, land_go, co_send, co_recv, 0, after=[g_w_ada])
    own_out = lax.dynamic_index_in_dim(sum_o, me // 2, 0, keepdims=False)
    g_w_out, d_w_out, nm_w_out, nv_w_out = _adamw_sharded(land_go, own_out, w_out2, m_w_out[0], v_w_out[0], "adamw_w_out")
    sum_c, land_gi = _chip_wait("g_conv_chip_wait", sum_c, land_gi, cc_send, cc_recv, 0, after=[g_w_out])
    sum_a, land_gi = _chip_wait("g_attn_chip_wait", sum_a, land_gi, ca_send, ca_recv, 4, after=[g_w_out])
    own_in = jnp.where(me < 4, lax.dynamic_index_in_dim(sum_c, (me % 4) // 2, 0, keepdims=False),
                       lax.dynamic_index_in_dim(sum_a, (me % 4) // 2, 0, keepdims=False))
    g_w_in, d_w_in, nm_w_in, nv_w_in = _adamw_sharded(land_gi, own_in, w_in2, m_w_in[0], v_w_in[0], "adamw_w_in")

    pack = lambda *vs: jnp.concatenate([a.reshape(1, -1) for a in vs], axis=1)
    smalls = [(b_ada, g_b_ada, m_b_ada, v_b_ada), (g_pre, g_g_pre, m_g_pre, v_g_pre),
              (conv_w, g_conv_w, m_conv_w, v_conv_w), (conv_b, g_conv_b, m_conv_b, v_conv_b),
              (g_conv, g_g_conv, m_g_conv, v_g_conv), (g_attn, g_g_attn, m_g_attn, v_g_attn),
              (g_post, g_g_post, m_g_post, v_g_post)]
    packed = [pack(*[s[t] for s in smalls]) for t in range(4)]
    npad = -packed[0].shape[1] % 128
    packed = [jnp.pad(p, ((0, 0), (0, npad)), constant_values=1.0) for p in packed]
    d_s, nm_s, nv_s = _adamw_small(*packed)

    def unpack(vec):
        out, at = [], 0
        for s in smalls:
            n = s[0].size
            out.append(vec[:, at:at + n].reshape(s[0].shape))
            at += n
        return out

    d_b_ada, d_g_pre, d_conv_w, d_conv_b, d_g_conv, d_g_attn, d_g_post = unpack(d_s)
    nm_b_ada, nm_g_pre, nm_conv_w, nm_conv_b, nm_g_conv, nm_g_attn, nm_g_post = unpack(nm_s)
    nv_b_ada, nv_g_pre, nv_conv_w, nv_conv_b, nv_g_conv, nv_g_attn, nv_g_post = unpack(nv_s)

    return (loss, grad_x[None],
            g_w_ada[None], g_b_ada, g_g_pre, g_w_in[None], g_conv_w, g_conv_b, g_g_conv, g_g_attn, g_w_out[None], g_g_post,
            d_w_ada[None], d_b_ada, d_g_pre, d_w_in[None], d_conv_w, d_conv_b, d_g_conv, d_g_attn, d_w_out[None], d_g_post,
            nm_w_ada[None], nm_b_ada, nm_g_pre, nm_w_in[None], nm_conv_w, nm_conv_b, nm_g_conv, nm_g_attn, nm_w_out[None], nm_g_post,
            nv_w_ada[None], nv_b_ada, nv_g_pre, nv_w_in[None], nv_conv_w, nv_conv_b, nv_g_conv, nv_g_attn, nv_w_out[None], nv_g_post)
```

```python
import functools

import jax
import jax.numpy as jnp
from jax import lax
from jax.experimental import pallas as pl
from jax.experimental.pallas import tpu as pltpu

f32 = jnp.float32
bf16 = jnp.bfloat16

NDEV = 8
HEAD_DIM = 64
PAIR = 2 * HEAD_DIM
BRANCHES = ((128, 1), (512, 4), (2048, 16))
HALF_WIN = 64
EPS = 1e-6
NEG_INF = -1e30
ADAM_LR, ADAM_B1, ADAM_B2, ADAM_EPS, ADAM_WD, ADAM_STEP = 0.001, 0.9, 0.999, 1e-08, 0.01, 10
MESH = pl.DeviceIdType.MESH
VMEM_LIMIT = 56 * 1024 * 1024
HBM_SPEC = pl.BlockSpec(memory_space=pltpu.HBM)
ANY_SPEC = pl.BlockSpec(memory_space=pl.ANY)
SEM_SPEC = pl.BlockSpec(memory_space=pltpu.SEMAPHORE)


def _params(*sem):
    return pltpu.CompilerParams(dimension_semantics=sem or None, vmem_limit_bytes=VMEM_LIMIT)


def _silu(z):
    return z * jax.nn.sigmoid(z)


def _dsilu(z):
    s = jax.nn.sigmoid(z)
    return s * (1.0 + z * (1.0 - s))


def _my_place():
    x, y, c = lax.axis_index("x"), lax.axis_index("y"), lax.axis_index("c")
    return x, y, c, 4 * x + 2 * y + c


def _peer(x, y, c, k):
    px, py, pc = x ^ (k >> 2 & 1), y ^ (k >> 1 & 1), c ^ (k & 1)
    return (px, py, pc), 4 * px + 2 * py + pc


def _all_gather(arrays, name):
    n = len(arrays)

    def body(*refs):
        srcs, dsts = refs[:n], refs[n:2 * n]
        send_sems, recv_sems, local_sems = refs[2 * n:]
        x, y, c, me = _my_place()
        locals_, sends = [], []
        for t in range(n):
            own = pltpu.make_async_copy(srcs[t], dsts[t].at[me], local_sems.at[t])
            own.start()
            locals_.append(own)
            for k in range(1, NDEV):
                peer, pidx = _peer(x, y, c, k)
                cp = pltpu.make_async_remote_copy(
                    src_ref=srcs[t], dst_ref=dsts[t].at[me], send_sem=send_sems.at[t, k],
                    recv_sem=recv_sems.at[t, k], device_id=peer, device_id_type=MESH)
                cp.start()
                sends.append(cp)
        for t in range(n):
            for k in range(1, NDEV):
                peer, pidx = _peer(x, y, c, k)
                pltpu.make_async_remote_copy(
                    src_ref=srcs[t], dst_ref=dsts[t].at[pidx], send_sem=send_sems.at[t, k],
                    recv_sem=recv_sems.at[t, k], device_id=peer, device_id_type=MESH).wait_recv()
        for cp in sends:
            cp.wait_send()
        for cp in locals_:
            cp.wait()

    return pl.pallas_call(
        body, name=name,
        out_shape=tuple(jax.ShapeDtypeStruct((NDEV,) + a.shape, a.dtype) for a in arrays),
        in_specs=[HBM_SPEC] * n, out_specs=tuple([HBM_SPEC] * n),
        scratch_shapes=[pltpu.SemaphoreType.DMA((n, NDEV)), pltpu.SemaphoreType.DMA((n, NDEV)),
                        pltpu.SemaphoreType.DMA((n,))],
    )(*arrays)


def _all_to_all(arrays, name):
    n = len(arrays)

    def body(*refs):
        srcs, dsts = refs[:n], refs[n:2 * n]
        send_sems, recv_sems, local_sems = refs[2 * n:]
        x, y, c, me = _my_place()
        locals_, sends = [], []
        for t in range(n):
            own = pltpu.make_async_copy(srcs[t].at[me], dsts[t].at[me], local_sems.at[t])
            own.start()
            locals_.append(own)
            for k in range(1, NDEV):
                peer, pidx = _peer(x, y, c, k)
                cp = pltpu.make_async_remote_copy(
                    src_ref=srcs[t].at[pidx], dst_ref=dsts[t].at[me], send_sem=send_sems.at[t, k],
                    recv_sem=recv_sems.at[t, k], device_id=peer, device_id_type=MESH)
                cp.start()
                sends.append(cp)
        for t in range(n):
            for k in range(1, NDEV):
                peer, pidx = _peer(x, y, c, k)
                pltpu.make_async_remote_copy(
                    src_ref=srcs[t].at[pidx], dst_ref=dsts[t].at[pidx], send_sem=send_sems.at[t, k],
                    recv_sem=recv_sems.at[t, k], device_id=peer, device_id_type=MESH).wait_recv()
        for cp in sends:
            cp.wait_send()
        for cp in locals_:
            cp.wait()

    return pl.pallas_call(
        body, name=name,
        out_shape=tuple(jax.ShapeDtypeStruct(a.shape, a.dtype) for a in arrays),
        in_specs=[HBM_SPEC] * n, out_specs=tuple([HBM_SPEC] * n),
        scratch_shapes=[pltpu.SemaphoreType.DMA((n, NDEV)), pltpu.SemaphoreType.DMA((n, NDEV)),
                        pltpu.SemaphoreType.DMA((n,))],
    )(*arrays)


def _comm_call(name, arrays, sems, new_sems, body, after=(), token=False):
    na, ns, nn, nf = len(arrays), len(sems), len(new_sems), len(after)

    def kern(*refs):
        ins, outs = refs[:na + ns + nf], refs[na + ns + nf:]
        body(ins[:na], ins[na:na + ns], outs[:nn])
        if token:
            outs[nn + na][...] = jnp.zeros((8, 128), f32)

    out_shape = ([pltpu.SemaphoreType.DMA(s) for s in new_sems] + [pltpu.HBM(a.shape, a.dtype) for a in arrays]
                 + ([jax.ShapeDtypeStruct((8, 128), f32)] if token else []))
    out_specs = [SEM_SPEC] * nn + [HBM_SPEC] * na + ([pl.BlockSpec(memory_space=pltpu.VMEM)] if token else [])
    res = pl.pallas_call(
        kern, name=name, out_shape=tuple(out_shape),
        in_specs=[HBM_SPEC] * na + [SEM_SPEC] * ns + [ANY_SPEC] * nf, out_specs=tuple(out_specs),
        input_output_aliases={t: nn + t for t in range(na)},
        compiler_params=pltpu.CompilerParams(has_side_effects=pltpu.SideEffectType.DATAFLOW_SIDE_EFFECTING),
    )(*[pltpu.with_memory_space_constraint(a, pltpu.HBM) for a in arrays], *sems, *after)
    return list(res[:nn]), list(res[nn:nn + na]), (res[nn + na] if token else None)


def _remote(src, dst, send_sem, recv_sem, device):
    return pltpu.make_async_remote_copy(src_ref=src, dst_ref=dst, send_sem=send_sem, recv_sem=recv_sem,
                                        device_id=device, device_id_type=MESH)


SAME_CORE = (2, 4, 6)
VIA_SIBLING = (3, 5, 7)


def _weights_start(lands, after):
    n = len(lands)

    def body(a, s, new):
        x, y, c, me = _my_place()
        for t, land in enumerate(a):
            send, recv = new[t], new[n + t]
            for k in (1,) + SAME_CORE:
                peer, _ = _peer(x, y, c, k)
                _remote(land.at[me], land.at[me], send.at[k], recv.at[k], peer).start()

    sems, lands, token = _comm_call("weights_start", lands, [], [(NDEV,)] * (2 * n), body, after=after, token=True)
    return sems[:n], sems[n:], lands, token


def _weights_forward(name, land, recv, after):
    def body(a, s, new):
        (land,), (recv,), (fsend, frecv) = a, s, new
        x, y, c, me = _my_place()
        sibling, _ = _peer(x, y, c, 1)
        for k in SAME_CORE:
            peer, slot = _peer(x, y, c, k)
            _remote(land.at[slot], land.at[slot], fsend.at[k], recv.at[k], peer).wait_recv()
            _remote(land.at[slot], land.at[slot], fsend.at[k], frecv.at[k ^ 1], sibling).start()

    return _comm_call(name, [land], [recv], [(NDEV,), (NDEV,)], body, after=after)


def _weights_wait(name, land, send, recv, fsend, frecv, after):
    def body(a, s, new):
        (land,), (send, recv, fsend, frecv) = a, s
        x, y, c, me = _my_place()
        sibling, sib_slot = _peer(x, y, c, 1)
        _remote(land.at[sib_slot], land.at[sib_slot], send.at[1], recv.at[1], sibling).wait_recv()
        for k in VIA_SIBLING:
            _, slot = _peer(x, y, c, k)
            _remote(land.at[slot], land.at[slot], fsend.at[k ^ 1], frecv.at[k], sibling).wait_recv()
        for k in (1,) + SAME_CORE:
            peer, _ = _peer(x, y, c, k)
            _remote(land.at[me], land.at[me], send.at[k], recv.at[k], peer).wait_send()
        for k in SAME_CORE:
            _, slot = _peer(x, y, c, k)
            _remote(land.at[slot], land.at[slot], fsend.at[k], frecv.at[k ^ 1], sibling).wait_send()

    return _comm_call(name, [land], [send, recv, fsend, frecv], [], body, after=after)[1][0]


NCHIP = NDEV // 2


def _pair_start(name, src):
    npair = src.shape[0] // 2

    def body(a, s, new):
        (src, pair), (send, recv) = a, new
        x, y, c, me = _my_place()
        sibling, _ = _peer(x, y, c, 1)
        for i in range(npair):
            _remote(src.at[2 * i + 1 - c], pair.at[i], send.at[i], recv.at[i], sibling).start()

    pair = lax.empty((npair,) + src.shape[1:], src.dtype)
    (send, recv), (src, pair), token = _comm_call(name, [src, pair], [], [(npair,), (npair,)], body, token=True)
    return send, recv, src, pair, token


def _pair_wait(name, src, pair, send, recv, after):
    npair = pair.shape[0]

    def body(a, s, new):
        (src, pair), (send, recv) = a, s
        x, y, c, me = _my_place()
        sibling, _ = _peer(x, y, c, 1)
        for i in range(npair):
            cp = _remote(src.at[2 * i + 1 - c], pair.at[i], send.at[i], recv.at[i], sibling)
            cp.wait_recv()
            cp.wait_send()

    return _comm_call(name, [src, pair], [send, recv], [], body, after=after)[1]


def _pair_sum(name, src, pair, core, tr=256):
    npair, R, Cc = pair.shape
    tr = min(tr, R)

    def body(core_ref, a_ref, b_ref, o_ref):
        o_ref[...] = (a_ref[...].astype(f32) + b_ref[...].astype(f32)).astype(o_ref.dtype)

    grid_spec = pltpu.PrefetchScalarGridSpec(
        num_scalar_prefetch=1, grid=(npair, R // tr),
        in_specs=[pl.BlockSpec((None, tr, Cc), lambda i, r, core: (2 * i + core[0], r, 0)),
                  pl.BlockSpec((None, tr, Cc), lambda i, r, core: (i, r, 0))],
        out_specs=pl.BlockSpec((None, tr, Cc), lambda i, r, core: (i, r, 0)))
    return pl.pallas_call(body, name=name, out_shape=jax.ShapeDtypeStruct(pair.shape, pair.dtype),
                          grid_spec=grid_spec, compiler_params=_params("parallel", "parallel"))(core, src, pair)


def _owner_chip(first, i):
    q = first // 2 + i
    return q >> 1 & 1, q & 1


def _chip_start(name, sums, land, first):
    npair = sums.shape[0]

    def body(a, s, new):
        (sums, land), (send, recv) = a, new
        x, y, c, me = _my_place()
        for i in range(npair):
            ox, oy = _owner_chip(first, i)

            @pl.when((x != ox) | (y != oy))
            def _():
                _remote(sums.at[i], land.at[2 * x + y], send.at[i], recv.at[2 * x + y], (ox, oy, c)).start()

    (send, recv), (sums, land), token = _comm_call(name, [sums, land], [], [(npair,), (NCHIP,)], body, token=True)
    return send, recv, sums, land, token


def _chip_wait(name, sums, land, send, recv, first, after):
    npair = sums.shape[0]

    def body(a, s, new):
        (sums, land), (send, recv) = a, s
        x, y, c, me = _my_place()
        mine = (me >= first) & (me < first + 2 * npair)
        for i in range(npair):
            ox, oy = _owner_chip(first, i)

            @pl.when((x != ox) | (y != oy))
            def _():
                _remote(sums.at[i], land.at[2 * x + y], send.at[i], recv.at[2 * x + y], (ox, oy, c)).wait_send()
        for q in range(NCHIP):
            @pl.when(mine & (2 * x + y != q))
            def _():
                _remote(sums.at[0], land.at[q], send.at[0], recv.at[q], (q >> 1, q & 1, c)).wait_recv()

    return _comm_call(name, [sums, land], [send, recv], [], body, after=after)[1]


def _matmul(a, b, *, name, out_dtype, ta=False, tb=False, b_slots=False, out_slots=0, b_cols=None,
            tm=1024, tn=1024, tk=2048, dep=None):
    M, K = (a.shape[1], a.shape[0]) if ta else a.shape
    col0 = 0
    if b_slots:
        slab = b.shape[2]
        N = b.shape[1] if tb else NDEV * slab
        assert (K if tb else N) == NDEV * slab
    elif b_cols is not None:
        assert not tb
        col0, N = b_cols
    else:
        N = b.shape[0] if tb else b.shape[1]
    tm, tn, tk = min(tm, M), min(tn, N), min(tk, K)
    if b_slots:
        if tb:
            tk = min(tk, slab)
        else:
            tn = min(tn, slab)
    if out_slots:
        tn = min(tn, N // out_slots)
    nm, nn, nk = M // tm, N // tn, K // tk
    assert (nm * tm, nn * tn, nk * tk) == (M, N, K) and col0 % tn == 0, (name, M, N, K, tm, tn, tk)
    j0 = col0 // tn

    a_spec = pl.BlockSpec((tk, tm), lambda i, j, k: (k, i)) if ta else pl.BlockSpec((tm, tk), lambda i, j, k: (i, k))
    if b_slots and tb:
        per = slab // tk
        b_spec = pl.BlockSpec((None, tn, tk), lambda i, j, k: (k // per, j, k % per))
    elif b_slots:
        per = slab // tn
        b_spec = pl.BlockSpec((None, tk, tn), lambda i, j, k: (j // per, k, j % per))
    elif tb:
        b_spec = pl.BlockSpec((tn, tk), lambda i, j, k: (j, k))
    else:
        b_spec = pl.BlockSpec((tk, tn), lambda i, j, k: (k, j + j0))
    if out_slots:
        per_o = (N // out_slots) // tn
        o_spec = pl.BlockSpec((None, tm, tn), lambda i, j, k: (j // per_o, i, j % per_o))
        out_shape = jax.ShapeDtypeStruct((out_slots, M, N // out_slots), out_dtype)
    else:
        o_spec = pl.BlockSpec((tm, tn), lambda i, j, k: (i, j))
        out_shape = jax.ShapeDtypeStruct((M, N), out_dtype)
    dims = (((0 if ta else 1,), (1 if tb else 0,)), ((), ()))
    deps = [] if dep is None else [dep]

    def body(a_ref, b_ref, *rest):
        o_ref = rest[len(deps)]
        prod = lax.dot_general(a_ref[...], b_ref[...], dims, preferred_element_type=f32)
        if nk == 1:
            o_ref[...] = prod.astype(out_dtype)
            return
        acc_ref = rest[len(deps) + 1]
        k = pl.program_id(2)

        @pl.when(k == 0)
        def _():
            acc_ref[...] = prod

        @pl.when((k > 0) & (k < nk - 1))
        def _():
            acc_ref[...] += prod

        @pl.when(k == nk - 1)
        def _():
            o_ref[...] = (acc_ref[...] + prod).astype(out_dtype)

    return pl.pallas_call(
        body, name=name, out_shape=out_shape, grid=(nm, nn, nk),
        in_specs=[a_spec, b_spec] + [ANY_SPEC] * len(deps), out_specs=o_spec,
        scratch_shapes=[pltpu.VMEM((tm, tn), f32)] if nk > 1 else [],
        compiler_params=_params("parallel", "parallel", "arbitrary"),
    )(a, b, *deps)


def _matmul_slabs_t(a, b, *, name, tm=512, tn=512, dep=None):
    M, K = a.shape
    n_slab, N, slab = b.shape
    assert K == n_slab * slab
    tm, tn = min(tm, M), min(tn, N)
    deps = [] if dep is None else [dep]

    def body(a_ref, b_ref, *rest):
        o_ref = rest[len(deps)]
        acc = None
        for s in range(n_slab):
            prod = lax.dot_general(a_ref[:, s * slab:(s + 1) * slab], b_ref[s], (((1,), (1,)), ((), ())),
                                   preferred_element_type=f32)
            acc = prod if acc is None else acc + prod
        o_ref[...] = acc

    return pl.pallas_call(
        body, name=name, out_shape=jax.ShapeDtypeStruct((M, N), f32), grid=(M // tm, N // tn),
        in_specs=[pl.BlockSpec((tm, K), lambda i, j: (i, 0)), pl.BlockSpec((n_slab, tn, slab), lambda i, j: (0, j, 0))]
        + [ANY_SPEC] * len(deps),
        out_specs=pl.BlockSpec((tm, tn), lambda i, j: (i, j)), compiler_params=_params("parallel", "parallel"),
    )(a, b, *deps)


def _silu_rows(c):
    D = c.shape[1]

    def body(c_ref, o_ref):
        o_ref[...] = jnp.broadcast_to(_silu(c_ref[...]), (8, D))

    return pl.pallas_call(body, name="silu_c", out_shape=jax.ShapeDtypeStruct((8, D), f32))(c)


def _ada_cols(c_all, w_ada, b_cols):
    D, W = w_ada.shape

    def body(c_ref, w_ref, b_ref, o_ref):
        mod = lax.dot_general(c_ref[...], w_ref[...], (((1,), (0,)), ((), ())), preferred_element_type=f32,
                              precision=lax.Precision.HIGHEST) + b_ref[...]
        for j in range(NDEV):
            o_ref[j] = jnp.broadcast_to(mod[j:j + 1, :], (8, W))

    return pl.pallas_call(body, name="ada_cols", out_shape=jax.ShapeDtypeStruct((NDEV, 8, W), f32),
                          compiler_params=_params())(c_all, w_ada, b_cols)


def _prenorm(x, scale, shift, g_pre, dep, tr=256):
    S, D = x.shape
    tr = min(tr, S)

    def body(x_ref, sc_ref, sh_ref, g_ref, dep_ref, h_ref):
        xv = x_ref[...]
        r = lax.rsqrt(jnp.mean(xv * xv, axis=-1, keepdims=True) + EPS)
        h_ref[...] = ((xv * r) * g_ref[...] * (1.0 + sc_ref[...]) + sh_ref[...]).astype(bf16)

    row = pl.BlockSpec((tr, D), lambda i: (i, 0))
    vec = pl.BlockSpec((1, D), lambda i: (0, 0))
    return pl.pallas_call(body, name="prenorm", out_shape=jax.ShapeDtypeStruct((S, D), bf16), grid=(S // tr,),
                          in_specs=[row, vec, vec, vec, ANY_SPEC], out_specs=row, compiler_params=_params("parallel"))(
                              x, scale, shift, g_pre, dep)


def _ext_rows(i, tr, S):
    g = lax.broadcasted_iota(jnp.int32, (tr + 16, 1), 0) + (i * tr - 8)
    return (g >= 0) & (g < S)


def _halo_specs(tr, S, C, col):
    nb8 = S // 8
    main = pl.BlockSpec((tr, C), lambda i: (i, col))
    prev = pl.BlockSpec((8, C), lambda i: (jnp.maximum(i * (tr // 8) - 1, 0), col))
    nxt = pl.BlockSpec((8, C), lambda i: (jnp.minimum((i + 1) * (tr // 8), nb8 - 1), col))
    return prev, main, nxt


def _conv_fwd(proj, conv_w, conv_b, g_conv, tr=256):
    S, C = proj.shape[0], proj.shape[1] // 8
    tr = min(tr, S)

    def body(up, um, un, cp, cm, cn, bg_ref, zc_ref, w_ref, cb_ref, g_ref, o_ref):
        i = pl.program_id(0)
        exists = _ext_rows(i, tr, S)
        u = jnp.concatenate([up[...], um[...], un[...]], axis=0)
        cg = jnp.concatenate([cp[...], cm[...], cn[...]], axis=0)
        t = jnp.where(exists, cg * u, 0.0)
        t_before = pltpu.roll(t, 1, 0)[8:tr + 8]
        t_after = pltpu.roll(t, tr + 15, 0)[8:tr + 8]
        w = w_ref[...]
        cv = w[0:1] * t_before + w[1:2] * t[8:tr + 8] + w[2:3] * t_after + cb_ref[...]
        yc = bg_ref[...] * cv
        rc = lax.rsqrt(jnp.mean(yc * yc, axis=-1, keepdims=True) + EPS)
        o_ref[...] = ((yc * rc) * g_ref[...] * _silu(zc_ref[...])).astype(bf16)

    u_specs = _halo_specs(tr, S, C, 0)
    c_specs = _halo_specs(tr, S, C, 2)
    vec = pl.BlockSpec((1, C), lambda i: (0, 0))
    return pl.pallas_call(
        body, name="conv_fwd", out_shape=jax.ShapeDtypeStruct((S, 2 * C), bf16), grid=(S // tr,),
        in_specs=[*u_specs, *c_specs, pl.BlockSpec((tr, C), lambda i: (i, 1)), pl.BlockSpec((tr, C), lambda i: (i, 3)),
                  pl.BlockSpec((8, C), lambda i: (0, 0)), vec, vec],
        out_specs=pl.BlockSpec((tr, C), lambda i: (i, 0)), compiler_params=_params("parallel"),
    )(proj, proj, proj, proj, proj, proj, proj, proj, conv_w, conv_b, g_conv)


def _branch_geometry(S, r):
    L = S // r
    nq = min(128, L)
    nk = min(nq + 2 * HALF_WIN, L)
    return L, nq, nk, L // nq


def _block_rows(idx, r, L, nq, nk, nblk):
    if r == 1:
        rho, qb = 0, idx
    else:
        rho, qb = idx // nblk, idx % nblk
    i0 = qb * nq
    ws = jnp.clip(i0 - HALF_WIN, 0, L - nk)
    if r == 1:
        qrows = pl.ds(pl.multiple_of(i0, 8), nq)
        krows = pl.ds(pl.multiple_of(ws, 8), nk)
    else:
        qrows = pl.ds(rho + r * i0, nq, stride=r)
        krows = pl.ds(rho + r * ws, nk, stride=r)
    return qrows, krows, i0 - ws


N_CASES = 3
SCALE = HEAD_DIM ** -0.5
ATTN_UNROLL = 2


def _bias_shape(S):
    return (len(BRANCHES) * N_CASES * 2, min(128, S), min(128 + 2 * HALF_WIN, S))


def _bias_index(b, case, head):
    return (b * N_CASES + case) * 2 + head


def _fill_bias(bias_scr, sl_ref, S):
    sl = sl_ref[...]
    slope = (sl[0:1, 0:1], sl[0:1, HEAD_DIM:HEAD_DIM + 1])
    for b, (_, r) in enumerate(BRANCHES):
        L, nq, nk, nblk = _branch_geometry(S, r)
        assert nblk == 1 or L >= nq + 2 * HALF_WIN
        rel = lax.broadcasted_iota(jnp.int32, (nq, nk), 0) - lax.broadcasted_iota(jnp.int32, (nq, nk), 1)
        for case in range(N_CASES):
            d = jnp.abs(rel + case * HALF_WIN)
            dist = d.astype(f32) * float(r)
            for head in range(2):
                bias_scr[_bias_index(b, case, head), 0:nq, 0:nk] = jnp.where(d <= HALF_WIN, -slope[head] * dist, NEG_INF)


def _head_slopes(n_heads):
    slopes = 2.0 ** (-8.0 * jnp.arange(1, n_heads + 1, dtype=f32) / n_heads)
    return jnp.broadcast_to(jnp.repeat(slopes.reshape(n_heads // 2, 2), HEAD_DIM, axis=1)[:, None, :],
                            (n_heads // 2, 8, PAIR))


def _attn_fwd(proj, slopes):
    S, C = proj.shape[0], proj.shape[1] // 8
    npair = C // PAIR

    def body(q_ref, k_ref, v_ref, sl_ref, o_ref, lse_ref, m_scr, l_scr, a_scr, bias_scr):
        lane = lax.broadcasted_iota(jnp.int32, (1, PAIR), 1)
        first = lane < HEAD_DIM
        _fill_bias(bias_scr, sl_ref, S)

        for b, (_, r) in enumerate(BRANCHES):
            L, nq, nk, nblk = _branch_geometry(S, r)

            def step(idx, carry, b=b, r=r, L=L, nq=nq, nk=nk, nblk=nblk):
                qrows, krows, off = _block_rows(idx, r, L, nq, nk, nblk)
                case = off // HALF_WIN
                q2 = q_ref[qrows, :] * SCALE
                k2 = k_ref[krows, :].astype(bf16)
                v2 = v_ref[krows, :].astype(bf16)
                ms, accs = [], []
                for hh in range(2):
                    mine = first if hh == 0 else ~first
                    qh = jnp.where(mine, q2, 0.0).astype(bf16)
                    s = lax.dot_general(qh, k2, (((1,), (1,)), ((), ())), preferred_element_type=f32)
                    s = s + bias_scr[_bias_index(b, case, hh), 0:nq, 0:nk]
                    m = jnp.max(s, axis=-1, keepdims=True)
                    p = jnp.exp(s - m).astype(bf16)
                    vh = jnp.where(mine, v2, jnp.ones_like(v2))
                    ms.append(m)
                    accs.append(jnp.dot(p, vh, preferred_element_type=f32))
                m_scr[b, qrows, :] = jnp.where(first, ms[0], ms[1])
                a_scr[b, qrows, :] = jnp.where(first, accs[0], accs[1])
                l_scr[b, qrows, :] = jnp.where(first, accs[1], accs[0])
                return carry

            lax.fori_loop(0, S // nq, step, 0, unroll=min(ATTN_UNROLL, S // nq))

        ch = min(256, S)

        def merge(i, carry):
            rows = pl.ds(pl.multiple_of(i * ch, 8), ch)
            m = jnp.maximum(jnp.maximum(m_scr[0, rows, :], m_scr[1, rows, :]), m_scr[2, rows, :])
            l = jnp.zeros((ch, PAIR), f32)
            acc = jnp.zeros((ch, PAIR), f32)
            for b in range(3):
                w = jnp.exp(m_scr[b, rows, :] - m)
                l = l + w * pltpu.roll(l_scr[b, rows, :], HEAD_DIM, 1)
                acc = acc + w * a_scr[b, rows, :]
            o_ref[rows, :] = acc / l
            lse_ref[rows, :] = m + jnp.log(l)
            return carry

        lax.fori_loop(0, S // ch, merge, 0)

    blk = lambda part: pl.BlockSpec((S, PAIR), lambda p: (0, part * npair + p))
    out = pl.BlockSpec((S, PAIR), lambda p: (0, p))
    return pl.pallas_call(
        body, name="attn_fwd",
        out_shape=(jax.ShapeDtypeStruct((S, C), f32), jax.ShapeDtypeStruct((S, C), f32)), grid=(npair,),
        in_specs=[blk(4), blk(5), blk(6), pl.BlockSpec((None, 8, PAIR), lambda p: (p, 0, 0))],
        out_specs=(out, out),
        scratch_shapes=[pltpu.VMEM((3, S, PAIR), f32)] * 3 + [pltpu.VMEM(_bias_shape(S), f32)],
        compiler_params=_params("parallel"),
    )(proj, proj, proj, slopes)


def _attn_post(ycat, o, proj, g_attn, tr=256):
    S, C = o.shape
    tr = min(tr, S)

    def body(y_ref, o_ref, z_ref, g_ref, out_ref):
        del y_ref
        ov = o_ref[...]
        ra = lax.rsqrt(jnp.mean(ov * ov, axis=-1, keepdims=True) + EPS)
        out_ref[...] = ((ov * ra) * g_ref[...] * _silu(z_ref[...])).astype(bf16)

    return pl.pallas_call(
        body, name="attn_post", out_shape=jax.ShapeDtypeStruct(ycat.shape, ycat.dtype), grid=(S // tr,),
        in_specs=[HBM_SPEC, pl.BlockSpec((tr, C), lambda i: (i, 0)), pl.BlockSpec((tr, C), lambda i: (i, 7)),
                  pl.BlockSpec((1, C), lambda i: (0, 0))],
        out_specs=pl.BlockSpec((tr, C), lambda i: (i, 1)), input_output_aliases={0: 0},
        compiler_params=_params("arbitrary"),
    )(ycat, o, proj, g_attn)


def _sandwich(y, x, target, gate, g_post, tr=256):
    S, D = y.shape
    tr = min(tr, S)

    def body(y_ref, x_ref, t_ref, gate_ref, g_ref, dy_ref, dout_ref, sums_ref):
        i = pl.program_id(0)
        yv = y_ref[...]
        rp = lax.rsqrt(jnp.mean(yv * yv, axis=-1, keepdims=True) + EPS)
        yhat = yv * rp
        yn = yhat * g_ref[...]
        err = (x_ref[...] + gate_ref[...] * yn) - t_ref[...]
        dout = err * (1.0 / D)
        dout_ref[...] = dout
        dyn = dout * gate_ref[...]
        w = dyn * g_ref[...]
        dy_ref[...] = (rp * (w - yhat * jnp.mean(w * yhat, axis=-1, keepdims=True))).astype(bf16)
        loss = 0.5 * jnp.sum(jnp.mean(err * err, axis=-1, keepdims=True), axis=0, keepdims=True)
        row = lax.broadcasted_iota(jnp.int32, (8, D), 0)
        upd = jnp.where(row == 0, jnp.sum(dout * yn, axis=0, keepdims=True),
                        jnp.where(row == 1, jnp.sum(dyn * yhat, axis=0, keepdims=True),
                                  jnp.where(row == 2, loss, 0.0)))

        @pl.when(i == 0)
        def _():
            sums_ref[...] = upd

        @pl.when(i > 0)
        def _():
            sums_ref[...] += upd

    row = pl.BlockSpec((tr, D), lambda i: (i, 0))
    vec = pl.BlockSpec((1, D), lambda i: (0, 0))
    return pl.pallas_call(
        body, name="sandwich",
        out_shape=(jax.ShapeDtypeStruct((S, D), bf16), jax.ShapeDtypeStruct((S, D), f32), jax.ShapeDtypeStruct((8, D), f32)),
        grid=(S // tr,), in_specs=[row, row, row, vec, vec],
        out_specs=(row, row, pl.BlockSpec((8, D), lambda i: (0, 0))), compiler_params=_params("arbitrary"),
    )(y, x, target, gate, g_post)


def _conv_bwd(proj, dycat, conv_w, conv_b, g_conv, dep, tr=256):
    S, C = proj.shape[0], proj.shape[1] // 8
    tr = min(tr, S)
    n = tr + 16

    def body(*refs):
        ins, (w_ref, cb_ref, g_ref, _, dp_ref, sums_ref) = refs[:15], refs[15:]
        i = pl.program_id(0)
        exists = _ext_rows(i, tr, S)
        u, bg, cg, zc, dyn = (jnp.concatenate([ins[3 * t][...], ins[3 * t + 1][...], ins[3 * t + 2][...]], axis=0)
                              for t in range(5))
        w = w_ref[...]
        t = jnp.where(exists, cg * u, 0.0)
        t_before, t_after = pltpu.roll(t, 1, 0), pltpu.roll(t, n - 1, 0)
        cv = w[0:1] * t_before + w[1:2] * t + w[2:3] * t_after + cb_ref[...]
        yc = bg * cv
        rc = lax.rsqrt(jnp.mean(yc * yc, axis=-1, keepdims=True) + EPS)
        yhat = yc * rc
        sz = _silu(zc)
        wgt = dyn * g_ref[...] * sz
        dyc = rc * (wgt - yhat * jnp.mean(wgt * yhat, axis=-1, keepdims=True))
        dcv = jnp.where(exists, dyc * bg, 0.0)
        dt = w[0:1] * pltpu.roll(dcv, n - 1, 0) + w[1:2] * dcv + w[2:3] * pltpu.roll(dcv, 1, 0)
        mid = slice(8, tr + 8)
        dp_ref[:, 0:C] = (dt * cg)[mid].astype(bf16)
        dp_ref[:, C:2 * C] = (dyc * cv)[mid].astype(bf16)
        dp_ref[:, 2 * C:3 * C] = (dt * u)[mid].astype(bf16)
        dp_ref[:, 3 * C:4 * C] = (dyn * yhat * g_ref[...] * _dsilu(zc))[mid].astype(bf16)
        colsum = lambda v: jnp.sum(v[mid], axis=0, keepdims=True)
        parts = [colsum(dyn * yhat * sz), colsum(dcv), colsum(dcv * t_before), colsum(dcv * t), colsum(dcv * t_after)]
        row = lax.broadcasted_iota(jnp.int32, (8, C), 0)
        upd = jnp.zeros((8, C), f32)
        for j, pj in enumerate(parts):
            upd = jnp.where(row == j, pj, upd)

        @pl.when(i == 0)
        def _():
            sums_ref[...] = upd

        @pl.when(i > 0)
        def _():
            sums_ref[...] += upd

    specs = []
    for col in range(4):
        specs += _halo_specs(tr, S, C, col)
    specs += _halo_specs(tr, S, C, 0)
    vec = pl.BlockSpec((1, C), lambda i: (0, 0))
    return pl.pallas_call(
        body, name="conv_bwd",
        out_shape=(jax.ShapeDtypeStruct((S, 4 * C), bf16), jax.ShapeDtypeStruct((8, C), f32)), grid=(S // tr,),
        in_specs=[*specs, pl.BlockSpec((8, C), lambda i: (0, 0)), vec, vec, ANY_SPEC],
        out_specs=(pl.BlockSpec((tr, 4 * C), lambda i: (i, 0)), pl.BlockSpec((8, C), lambda i: (0, 0))),
        compiler_params=_params("arbitrary"),
    )(*([proj] * 12), dycat, dycat, dycat, conv_w, conv_b, g_conv, dep)


def _attn_post_bwd(o, proj, dycat, g_attn, dep, tr=256):
    S, C = o.shape
    tr = min(tr, S)

    def body(o_ref, z_ref, dy_ref, g_ref, dep_ref, do_ref, dz_ref, sums_ref):
        i = pl.program_id(0)
        ov, zv, dyn = o_ref[...], z_ref[...], dy_ref[...]
        ra = lax.rsqrt(jnp.mean(ov * ov, axis=-1, keepdims=True) + EPS)
        ohat = ov * ra
        sz = _silu(zv)
        wgt = dyn * g_ref[...] * sz
        do_ref[...] = ra * (wgt - ohat * jnp.mean(wgt * ohat, axis=-1, keepdims=True))
        dz_ref[...] = (dyn * ohat * g_ref[...] * _dsilu(zv)).astype(bf16)
        row = lax.broadcasted_iota(jnp.int32, (8, C), 0)
        upd = jnp.where(row == 0, jnp.sum(dyn * ohat * sz, axis=0, keepdims=True), 0.0)

        @pl.when(i == 0)
        def _():
            sums_ref[...] = upd

        @pl.when(i > 0)
        def _():
            sums_ref[...] += upd

    return pl.pallas_call(
        body, name="attn_post_bwd",
        out_shape=(jax.ShapeDtypeStruct((S, C), f32), jax.ShapeDtypeStruct((S, C), bf16), jax.ShapeDtypeStruct((8, C), f32)),
        grid=(S // tr,),
        in_specs=[pl.BlockSpec((tr, C), lambda i: (i, 0)), pl.BlockSpec((tr, C), lambda i: (i, 7)),
                  pl.BlockSpec((tr, C), lambda i: (i, 1)), pl.BlockSpec((1, C), lambda i: (0, 0)), ANY_SPEC],
        out_specs=(pl.BlockSpec((tr, C), lambda i: (i, 0)), pl.BlockSpec((tr, C), lambda i: (i, 0)),
                   pl.BlockSpec((8, C), lambda i: (0, 0))),
        compiler_params=_params("arbitrary"),
    )(o, proj, dycat, g_attn, dep)


def _attn_bwd(proj, o, do, lse, slopes, dep):
    S, C = o.shape
    npair = C // PAIR

    def body(q_ref, k_ref, v_ref, o_ref, do_ref, lse_ref, sl_ref, dep_ref, dq_ref, dk_ref, dv_ref,
             dq_scr, dk_scr, dv_scr, dl_scr, bias_scr):
        lane = lax.broadcasted_iota(jnp.int32, (1, PAIR), 1)
        first = lane < HEAD_DIM
        _fill_bias(bias_scr, sl_ref, S)
        ch = min(256, S)

        def prep(i, carry):
            rows = pl.ds(pl.multiple_of(i * ch, 8), ch)
            prod = do_ref[rows, :] * o_ref[rows, :]
            d0 = jnp.sum(jnp.where(first, prod, 0.0), axis=-1, keepdims=True)
            d1 = jnp.sum(jnp.where(first, 0.0, prod), axis=-1, keepdims=True)
            dl_scr[rows, :] = jnp.where(first, d0, d1)
            zero = jnp.zeros((ch, PAIR), f32)
            dq_scr[rows, :] = zero
            dk_scr[rows, :] = zero
            dv_scr[rows, :] = zero
            return carry

        lax.fori_loop(0, S // ch, prep, 0)

        for b, (_, r) in enumerate(BRANCHES):
            L, nq, nk, nblk = _branch_geometry(S, r)

            def step(idx, carry, b=b, r=r, L=L, nq=nq, nk=nk, nblk=nblk):
                qrows, krows, off = _block_rows(idx, r, L, nq, nk, nblk)
                case = off // HALF_WIN
                q2 = q_ref[qrows, :] * SCALE
                k2 = k_ref[krows, :].astype(bf16)
                v2 = v_ref[krows, :].astype(bf16)
                do2 = do_ref[qrows, :]
                lse2 = lse_ref[qrows, :]
                dl2 = dl_scr[qrows, :]
                dq2 = jnp.zeros((nq, PAIR), f32)
                dk2 = jnp.zeros((nk, PAIR), f32)
                dv2 = jnp.zeros((nk, PAIR), f32)
                for hh in range(2):
                    mine = first if hh == 0 else ~first
                    lo = hh * HEAD_DIM
                    qh = jnp.where(mine, q2, 0.0).astype(bf16)
                    doh = jnp.where(mine, do2, 0.0).astype(bf16)
                    s = lax.dot_general(qh, k2, (((1,), (1,)), ((), ())), preferred_element_type=f32)
                    s = s + bias_scr[_bias_index(b, case, hh), 0:nq, 0:nk]
                    p = jnp.exp(s - lse2[:, lo:lo + 1])
                    dv2 = dv2 + lax.dot_general(p.astype(bf16), doh, (((0,), (0,)), ((), ())), preferred_element_type=f32)
                    dp = lax.dot_general(doh, v2, (((1,), (1,)), ((), ())), preferred_element_type=f32)
                    ds = (p * (dp - dl2[:, lo:lo + 1])).astype(bf16)
                    dq2 = dq2 + jnp.where(mine, jnp.dot(ds, k2, preferred_element_type=f32), 0.0)
                    dk2 = dk2 + lax.dot_general(ds, qh, (((0,), (0,)), ((), ())), preferred_element_type=f32)
                dq_scr[qrows, :] = dq_scr[qrows, :] + dq2
                dk_scr[krows, :] = dk_scr[krows, :] + dk2
                dv_scr[krows, :] = dv_scr[krows, :] + dv2
                return carry

            lax.fori_loop(0, S // nq, step, 0, unroll=min(ATTN_UNROLL, S // nq))

        dq_ref[...] = (dq_scr[...] * SCALE).astype(bf16)
        dk_ref[...] = dk_scr[...].astype(bf16)
        dv_ref[...] = dv_scr[...].astype(bf16)

    blk = lambda part: pl.BlockSpec((S, PAIR), lambda p: (0, part * npair + p))
    own = pl.BlockSpec((S, PAIR), lambda p: (0, p))
    grad = jax.ShapeDtypeStruct((S, C), bf16)
    return pl.pallas_call(
        body, name="attn_bwd", out_shape=(grad, grad, grad), grid=(npair,),
        in_specs=[blk(4), blk(5), blk(6), own, own, own, pl.BlockSpec((None, 8, PAIR), lambda p: (p, 0, 0)), ANY_SPEC],
        out_specs=(own, own, own),
        scratch_shapes=[pltpu.VMEM((S, PAIR), f32)] * 4 + [pltpu.VMEM(_bias_shape(S), f32)],
        compiler_params=_params("parallel"),
    )(proj, proj, proj, o, do, lse, slopes, dep)


def _prenorm_bwd(dh, x, dout, scale, g_pre, tr=256):
    S, D = x.shape
    tr = min(tr, S)

    def body(dh_ref, x_ref, dout_ref, sc_ref, g_ref, gx_ref, sums_ref):
        i = pl.program_id(0)
        xv, dhv = x_ref[...], dh_ref[...]
        r = lax.rsqrt(jnp.mean(xv * xv, axis=-1, keepdims=True) + EPS)
        xn = xv * r
        dxn = dhv * (g_ref[...] * (1.0 + sc_ref[...]))
        gx_ref[...] = dout_ref[...] + r * (dxn - xn * jnp.mean(dxn * xn, axis=-1, keepdims=True))
        dhx = dhv * xn
        row = lax.broadcasted_iota(jnp.int32, (8, D), 0)
        upd = jnp.where(row == 0, jnp.sum(dhv, axis=0, keepdims=True),
                        jnp.where(row == 1, jnp.sum(dhx, axis=0, keepdims=True) * g_ref[...],
                                  jnp.where(row == 2, jnp.sum(dhx, axis=0, keepdims=True) * (1.0 + sc_ref[...]), 0.0)))

        @pl.when(i == 0)
        def _():
            sums_ref[...] = upd

        @pl.when(i > 0)
        def _():
            sums_ref[...] += upd

    row = pl.BlockSpec((tr, D), lambda i: (i, 0))
    vec = pl.BlockSpec((1, D), lambda i: (0, 0))
    return pl.pallas_call(
        body, name="prenorm_bwd",
        out_shape=(jax.ShapeDtypeStruct((S, D), f32), jax.ShapeDtypeStruct((8, D), f32)), grid=(S // tr,),
        in_specs=[row, row, row, vec, vec], out_specs=(row, pl.BlockSpec((8, D), lambda i: (0, 0))),
        compiler_params=_params("arbitrary"),
    )(dh, x, dout, scale, g_pre)


def _adamw(w, g, m, v):
    m = ADAM_B1 * m + (1.0 - ADAM_B1) * g
    v = ADAM_B2 * v + (1.0 - ADAM_B2) * (g * g)
    m_hat = m / (1.0 - ADAM_B1 ** ADAM_STEP)
    v_hat = v / (1.0 - ADAM_B2 ** ADAM_STEP)
    delta = -ADAM_LR * (m_hat / (jnp.sqrt(v_hat) + ADAM_EPS) + ADAM_WD * w)
    return delta, m, v


def _sum_rows(parts):
    P = parts.shape[1]

    def body(p_ref, o_ref):
        acc = p_ref[0:1, :]
        for j in range(1, NDEV):
            acc = acc + p_ref[j:j + 1, :]
        o_ref[...] = jnp.broadcast_to(acc, (8, P))

    return pl.pallas_call(body, name="sum_small", out_shape=jax.ShapeDtypeStruct((8, P), f32),
                          compiler_params=_params())(parts)


def _adamw_small(w, g, m, v):
    def body(w_ref, g_ref, m_ref, v_ref, d_ref, nm_ref, nv_ref):
        d_ref[...], nm_ref[...], nv_ref[...] = _adamw(w_ref[...], g_ref[...], m_ref[...], v_ref[...])

    out = jax.ShapeDtypeStruct(w.shape, f32)
    return pl.pallas_call(body, name="adamw_small", out_shape=(out, out, out), compiler_params=_params())(w, g, m, v)


def _adamw_sharded(parts, own, w, m, v, name, tr=128):
    R, Cc = w.shape
    tr = min(tr, R)
    n = parts.shape[0]

    def body(p_ref, own_ref, w_ref, m_ref, v_ref, g_ref, d_ref, nm_ref, nv_ref):
        g = own_ref[...].astype(f32)
        for j in range(n):
            g = g + p_ref[j].astype(f32)
        g_ref[...] = g
        d_ref[...], nm_ref[...], nv_ref[...] = _adamw(w_ref[...], g, m_ref[...], v_ref[...])

    row = pl.BlockSpec((tr, Cc), lambda i: (i, 0))
    out = jax.ShapeDtypeStruct((R, Cc), f32)
    return pl.pallas_call(
        body, name=name, out_shape=(out, out, out, out), grid=(R // tr,),
        in_specs=[pl.BlockSpec((n, tr, Cc), lambda i: (0, i, 0)), row, row, row, row],
        out_specs=(row, row, row, row), compiler_params=_params("parallel"),
    )(parts, own, w, m, v)


def _adamw_ada(c_t, dmod_cols, w, m, v, tr=256):
    D, W = w.shape
    tr = min(tr, D)

    def body(c_ref, dm_ref, w_ref, m_ref, v_ref, g_ref, d_ref, nm_ref, nv_ref):
        cv, dm = c_ref[...], dm_ref[...]
        g = cv[:, 0:1] * dm[0:1, :]
        for b in range(1, NDEV):
            g = g + cv[:, b:b + 1] * dm[b:b + 1, :]
        g_ref[...] = g
        d_ref[...], nm_ref[...], nv_ref[...] = _adamw(w_ref[...], g, m_ref[...], v_ref[...])

    row = pl.BlockSpec((tr, W), lambda i: (i, 0))
    out = jax.ShapeDtypeStruct((D, W), f32)
    return pl.pallas_call(
        body, name="adamw_ada", out_shape=(out, out, out, out), grid=(D // tr,),
        in_specs=[pl.BlockSpec((tr, NDEV), lambda i: (i, 0)), pl.BlockSpec((NDEV, W), lambda i: (0, 0)), row, row, row],
        out_specs=(row, row, row, row), compiler_params=_params("parallel"),
    )(c_t, dmod_cols, w, m, v)


def kernel(x, c, w_ada, b_ada, g_pre, w_in, conv_w, conv_b, g_conv, g_attn, w_out, g_post, loss_target, m_w_ada, m_b_ada, m_g_pre, m_w_in, m_conv_w, m_conv_b, m_g_conv, m_g_attn, m_w_out, m_g_post, v_w_ada, v_b_ada, v_g_pre, v_w_in, v_conv_w, v_conv_b, v_g_conv, v_g_attn, v_w_out, v_g_post):
    S, D = x.shape[1], x.shape[2]
    C = D // 2
    W = w_ada.shape[2]
    CW = conv_w.shape[2]
    me = 4 * lax.axis_index("x") + 2 * lax.axis_index("y") + lax.axis_index("c")
    x2, tgt = x[0], loss_target[0]
    w_ada2, w_in2, w_out2 = w_ada[0], w_in[0], w_out[0]

    R = D // NDEV
    core = lax.axis_index("c").astype(jnp.int32).reshape(1)

    cw_slab = jnp.zeros((8, CW), f32).at[:3].set(conv_w[0])
    c_rows, cw_g = _all_gather([_silu_rows(c), cw_slab], "gather_c")
    c_all = c_rows[:, 0, :]
    conv_w_full = jnp.transpose(cw_g, (1, 0, 2)).reshape(8, C)
    b_cols = lax.dynamic_slice_in_dim(b_ada, me * W, W, axis=1)
    (mod_slabs,) = _all_to_all([_ada_cols(c_all, w_ada2, b_cols)], "scatter_mod")
    mod = mod_slabs[:, 0, :].reshape(1, 3 * D)
    shift, scale, gate = mod[:, :D], mod[:, D:2 * D], mod[:, 2 * D:]

    land_i = lax.dynamic_update_slice(lax.empty((NDEV, D, C), bf16), w_in2.astype(bf16)[None], (me, 0, 0))
    land_o = lax.dynamic_update_slice(lax.empty((NDEV, R, D), bf16), w_out2.astype(bf16)[None], (me, 0, 0))
    (wi_send, wo_send), (wi_recv, wo_recv), (land_i, land_o), w_token = _weights_start([land_i, land_o], [mod_slabs])

    h = _prenorm(x2, scale, shift, g_pre, w_token)
    (fi_send, fi_recv), (land_i,), _ = _weights_forward("w_in_forward", land_i, wi_recv, after=[h])
    win_g = _weights_wait("w_in_wait", land_i, wi_send, wi_recv, fi_send, fi_recv, after=[h])
    proj = _matmul(h, win_g, name="in_proj", out_dtype=f32, b_slots=True)
    (fo_send, fo_recv), (land_o,), _ = _weights_forward("w_out_forward", land_o, wo_recv, after=[proj])
    slopes = _head_slopes(C // HEAD_DIM)
    ycat = _conv_fwd(proj, conv_w_full, conv_b, g_conv)
    o, lse = _attn_fwd(proj, slopes)
    ycat = _attn_post(ycat, o, proj, g_attn)
    wout_g = _weights_wait("w_out_wait", land_o, wo_send, wo_recv, fo_send, fo_recv, after=[ycat])
    wout_full = wout_g.reshape(D, D)
    y = _matmul(ycat, wout_full, name="out_proj", out_dtype=f32)
    dy, dout, post_sums = _sandwich(y, x2, tgt, gate, g_post)

    gw_out = _matmul(ycat, dy, name="out_proj_dw", out_dtype=bf16, ta=True).reshape(NDEV, R, D)
    po_send, po_recv, gw_out, pair_o, po_token = _pair_start("g_out_pair_start", gw_out)
    dycat = _matmul(dy, wout_full, name="out_proj_dx", out_dtype=f32, tb=True, dep=po_token)
    gw_out, pair_o = _pair_wait("g_out_pair_wait", gw_out, pair_o, po_send, po_recv, after=[dycat])
    sum_o = _pair_sum("g_out_pair_sum", gw_out, pair_o, core)
    co_send, co_recv, sum_o, land_go, co_token = _chip_start(
        "g_out_chip_start", sum_o, jnp.zeros((NCHIP, R, D), bf16), 0)
    dpc, conv_sums = _conv_bwd(proj, dycat, conv_w_full, conv_b, g_conv, co_token)
    gw_c = _matmul(h, dpc, name="in_proj_dw_conv", out_dtype=bf16, ta=True, out_slots=4)
    pc_send, pc_recv, gw_c, pair_c, pc_token = _pair_start("g_conv_pair_start", gw_c)
    do, dza, attn_sums = _attn_post_bwd(o, proj, dycat, g_attn, pc_token)
    gw_c, pair_c = _pair_wait("g_conv_pair_wait", gw_c, pair_c, pc_send, pc_recv, after=[do])
    sum_c = _pair_sum("g_conv_pair_sum", gw_c, pair_c, core)
    cc_send, cc_recv, sum_c, land_gi, cc_token = _chip_start(
        "g_conv_chip_start", sum_c, jnp.zeros((NCHIP, D, C), bf16), 0)
    dq, dk, dv = _attn_bwd(proj, o, do, lse, slopes, cc_token)
    dproj = jnp.concatenate([dpc, dq, dk, dv, dza], axis=1)
    gw_a = _matmul(h, dproj, name="in_proj_dw_attn", out_dtype=bf16, ta=True, out_slots=4, b_cols=(4 * C, 4 * C))
    pa_send, pa_recv, gw_a, pair_a, pa_token = _pair_start("g_attn_pair_start", gw_a)
    gw_a, pair_a = _pair_wait("g_attn_pair_wait", gw_a, pair_a, pa_send, pa_recv, after=[pa_token])
    sum_a = _pair_sum("g_attn_pair_sum", gw_a, pair_a, core)
    ca_send, ca_recv, sum_a, land_gi, ca_token = _chip_start("g_attn_chip_start", sum_a, land_gi, 4)
    dh = _matmul_slabs_t(dproj, win_g, name="in_proj_dx", dep=ca_token)
    grad_x, pre_sums = _prenorm_bwd(dh, x2, dout, scale, g_pre)

    small = jnp.concatenate([pre_sums[0:1], pre_sums[1:2], post_sums[0:1],
                             pre_sums[2:3], post_sums[1:2],
                             conv_sums[2:3], conv_sums[3:4], conv_sums[4:5],
                             conv_sums[1:2], conv_sums[0:1], attn_sums[0:1]], axis=1)
    (small_all,) = _all_gather([jnp.broadcast_to(small, (8, 8 * D))], "gather_small")
    small_all = small_all[:, 0, :]
    tot = _sum_rows(small_all)[0:1]
    loss = lax.psum(post_sums[2, 0], ("x", "y", "c"))

    g_b_ada = tot[:, :3 * D]
    g_g_pre, g_g_post = tot[:, 3 * D:4 * D], tot[:, 4 * D:5 * D]
    g_conv_w_full = tot[:, 5 * D:5 * D + 3 * C].reshape(3, C)
    g_conv_w = lax.dynamic_slice_in_dim(g_conv_w_full, me * CW, CW, axis=1)[None]
    g_conv_b, g_g_conv, g_g_attn = (tot[:, 5 * D + (3 + t) * C:5 * D + (4 + t) * C] for t in range(3))

    dmod_cols = lax.dynamic_slice_in_dim(small_all[:, :3 * D], me * W, W, axis=1)
    g_w_ada, d_w_ada, nm_w_ada, nv_w_ada = _adamw_ada(c_all.T, dmod_cols, w_ada2, m_w_ada[0], v_w_ada[0])

    sum_o, land_go = _chip_wait("g_out_chip_wait", sum_o, land_go, co_send, co_recv, 0, after=[g_w_ada])
    own_out = lax.dynamic_index_in_dim(sum_o, me // 2, 0, keepdims=False)
    g_w_out, d_w_out, nm_w_out, nv_w_out = _adamw_sharded(land_go, own_out, w_out2, m_w_out[0], v_w_out[0], "adamw_w_out")
    sum_c, land_gi = _chip_wait("g_conv_chip_wait", sum_c, land_gi, cc_send, cc_recv, 0, after=[g_w_out])
    sum_a, land_gi = _chip_wait("g_attn_chip_wait", sum_a, land_gi, ca_send, ca_recv, 4, after=[g_w_out])
    own_in = jnp.where(me < 4, lax.dynamic_index_in_dim(sum_c, (me % 4) // 2, 0, keepdims=False),
                       lax.dynamic_index_in_dim(sum_a, (me % 4) // 2, 0, keepdims=False))
    g_w_in, d_w_in, nm_w_in, nv_w_in = _adamw_sharded(land_gi, own_in, w_in2, m_w_in[0], v_w_in[0], "adamw_w_in")

    pack = lambda *vs: jnp.concatenate([a.reshape(1, -1) for a in vs], axis=1)
    smalls = [(b_ada, g_b_ada, m_b_ada, v_b_ada), (g_pre, g_g_pre, m_g_pre, v_g_pre),
              (conv_w, g_conv_w, m_conv_w, v_conv_w), (conv_b, g_conv_b, m_conv_b, v_conv_b),
              (g_conv, g_g_conv, m_g_conv, v_g_conv), (g_attn, g_g_attn, m_g_attn, v_g_attn),
              (g_post, g_g_post, m_g_post, v_g_post)]
    packed = [pack(*[s[t] for s in smalls]) for t in range(4)]
    npad = -packed[0].shape[1] % 128
    packed = [jnp.pad(p, ((0, 0), (0, npad)), constant_values=1.0) for p in packed]
    d_s, nm_s, nv_s = _adamw_small(*packed)

    def unpack(vec):
        out, at = [], 0
        for s in smalls:
            n = s[0].size
            out.append(vec[:, at:at + n].reshape(s[0].shape))
            at += n
        return out

    d_b_ada, d_g_pre, d_conv_w, d_conv_b, d_g_conv, d_g_attn, d_g_post = unpack(d_s)
    nm_b_ada, nm_g_pre, nm_conv_w, nm_conv_b, nm_g_conv, nm_g_attn, nm_g_post = unpack(nm_s)
    nv_b_ada, nv_g_pre, nv_conv_w, nv_conv_b, nv_g_conv, nv_g_attn, nv_g_post = unpack(nv_s)

    return (loss, grad_x[None],
            g_w_ada[None], g_b_ada, g_g_pre, g_w_in[None], g_conv_w, g_conv_b, g_g_conv, g_g_attn, g_w_out[None], g_g_post,
            d_w_ada[None], d_b_ada, d_g_pre, d_w_in[None], d_conv_w, d_conv_b, d_g_conv, d_g_attn, d_w_out[None], d_g_post,
            nm_w_ada[None], nm_b_ada, nm_g_pre, nm_w_in[None], nm_conv_w, nm_conv_b, nm_g_conv, nm_g_attn, nm_w_out[None], nm_g_post,
            nv_w_ada[None], nv_b_ada, nv_g_pre, nv_w_in[None], nv_conv_w, nv_conv_b, nv_g_conv, nv_g_attn, nv_w_out[None], nv_g_post)
```

```python
import functools

import jax
import jax.numpy as jnp
from jax import lax
from jax.experimental import pallas as pl
from jax.experimental.pallas import tpu as pltpu

f32 = jnp.float32
bf16 = jnp.bfloat16

NDEV = 8
HEAD_DIM = 64
PAIR = 2 * HEAD_DIM
BRANCHES = ((128, 1), (512, 4), (2048, 16))
HALF_WIN = 64
EPS = 1e-6
NEG_INF = -1e30
ADAM_LR, ADAM_B1, ADAM_B2, ADAM_EPS, ADAM_WD, ADAM_STEP = 0.001, 0.9, 0.999, 1e-08, 0.01, 10
MESH = pl.DeviceIdType.MESH
VMEM_LIMIT = 56 * 1024 * 1024
HBM_SPEC = pl.BlockSpec(memory_space=pltpu.HBM)
ANY_SPEC = pl.BlockSpec(memory_space=pl.ANY)
SEM_SPEC = pl.BlockSpec(memory_space=pltpu.SEMAPHORE)


def _params(*sem):
    return pltpu.CompilerParams(dimension_semantics=sem or None, vmem_limit_bytes=VMEM_LIMIT)


def _silu(z):
    return z * jax.nn.sigmoid(z)


def _dsilu(z):
    s = jax.nn.sigmoid(z)
    return s * (1.0 + z * (1.0 - s))


def _my_place():
    x, y, c = lax.axis_index("x"), lax.axis_index("y"), lax.axis_index("c")
    return x, y, c, 4 * x + 2 * y + c


def _peer(x, y, c, k):
    px, py, pc = x ^ (k >> 2 & 1), y ^ (k >> 1 & 1), c ^ (k & 1)
    return (px, py, pc), 4 * px + 2 * py + pc


def _all_gather(arrays, name):
    n = len(arrays)

    def body(*refs):
        srcs, dsts = refs[:n], refs[n:2 * n]
        send_sems, recv_sems, local_sems = refs[2 * n:]
        x, y, c, me = _my_place()
        locals_, sends = [], []
        for t in range(n):
            own = pltpu.make_async_copy(srcs[t], dsts[t].at[me], local_sems.at[t])
            own.start()
            locals_.append(own)
            for k in range(1, NDEV):
                peer, pidx = _peer(x, y, c, k)
                cp = pltpu.make_async_remote_copy(
                    src_ref=srcs[t], dst_ref=dsts[t].at[me], send_sem=send_sems.at[t, k],
                    recv_sem=recv_sems.at[t, k], device_id=peer, device_id_type=MESH)
                cp.start()
                sends.append(cp)
        for t in range(n):
            for k in range(1, NDEV):
                peer, pidx = _peer(x, y, c, k)
                pltpu.make_async_remote_copy(
                    src_ref=srcs[t], dst_ref=dsts[t].at[pidx], send_sem=send_sems.at[t, k],
                    recv_sem=recv_sems.at[t, k], device_id=peer, device_id_type=MESH).wait_recv()
        for cp in sends:
            cp.wait_send()
        for cp in locals_:
            cp.wait()

    return pl.pallas_call(
        body, name=name,
        out_shape=tuple(jax.ShapeDtypeStruct((NDEV,) + a.shape, a.dtype) for a in arrays),
        in_specs=[HBM_SPEC] * n, out_specs=tuple([HBM_SPEC] * n),
        scratch_shapes=[pltpu.SemaphoreType.DMA((n, NDEV)), pltpu.SemaphoreType.DMA((n, NDEV)),
                        pltpu.SemaphoreType.DMA((n,))],
    )(*arrays)


def _all_to_all(arrays, name):
    n = len(arrays)

    def body(*refs):
        srcs, dsts = refs[:n], refs[n:2 * n]
        send_sems, recv_sems, local_sems = refs[2 * n:]
        x, y, c, me = _my_place()
        locals_, sends = [], []
        for t in range(n):
            own = pltpu.make_async_copy(srcs[t].at[me], dsts[t].at[me], local_sems.at[t])
            own.start()
            locals_.append(own)
            for k in range(1, NDEV):
                peer, pidx = _peer(x, y, c, k)
                cp = pltpu.make_async_remote_copy(
                    src_ref=srcs[t].at[pidx], dst_ref=dsts[t].at[me], send_sem=send_sems.at[t, k],
                    recv_sem=recv_sems.at[t, k], device_id=peer, device_id_type=MESH)
                cp.start()
                sends.append(cp)
        for t in range(n):
            for k in range(1, NDEV):
                peer, pidx = _peer(x, y, c, k)
                pltpu.make_async_remote_copy(
                    src_ref=srcs[t].at[pidx], dst_ref=dsts[t].at[pidx], send_sem=send_sems.at[t, k],
                    recv_sem=recv_sems.at[t, k], device_id=peer, device_id_type=MESH).wait_recv()
        for cp in sends:
            cp.wait_send()
        for cp in locals_:
            cp.wait()

    return pl.pallas_call(
        body, name=name,
        out_shape=tuple(jax.ShapeDtypeStruct(a.shape, a.dtype) for a in arrays),
        in_specs=[HBM_SPEC] * n, out_specs=tuple([HBM_SPEC] * n),
        scratch_shapes=[pltpu.SemaphoreType.DMA((n, NDEV)), pltpu.SemaphoreType.DMA((n, NDEV)),
                        pltpu.SemaphoreType.DMA((n,))],
    )(*arrays)


def _comm_call(name, arrays, sems, new_sems, body, after=(), token=False):
    na, ns, nn, nf = len(arrays), len(sems), len(new_sems), len(after)

    def kern(*refs):
        ins, outs = refs[:na + ns + nf], refs[na + ns + nf:]
        body(ins[:na], ins[na:na + ns], outs[:nn])
        if token:
            outs[nn + na][...] = jnp.zeros((8, 128), f32)

    out_shape = ([pltpu.SemaphoreType.DMA(s) for s in new_sems] + [pltpu.HBM(a.shape, a.dtype) for a in arrays]
                 + ([jax.ShapeDtypeStruct((8, 128), f32)] if token else []))
    out_specs = [SEM_SPEC] * nn + [HBM_SPEC] * na + ([pl.BlockSpec(memory_space=pltpu.VMEM)] if token else [])
    res = pl.pallas_call(
        kern, name=name, out_shape=tuple(out_shape),
        in_specs=[HBM_SPEC] * na + [SEM_SPEC] * ns + [ANY_SPEC] * nf, out_specs=tuple(out_specs),
        input_output_aliases={t: nn + t for t in range(na)},
        compiler_params=pltpu.CompilerParams(has_side_effects=pltpu.SideEffectType.DATAFLOW_SIDE_EFFECTING),
    )(*[pltpu.with_memory_space_constraint(a, pltpu.HBM) for a in arrays], *sems, *after)
    return list(res[:nn]), list(res[nn:nn + na]), (res[nn + na] if token else None)


def _remote(src, dst, send_sem, recv_sem, device):
    return pltpu.make_async_remote_copy(src_ref=src, dst_ref=dst, send_sem=send_sem, recv_sem=recv_sem,
                                        device_id=device, device_id_type=MESH)


SAME_CORE = (2, 4, 6)
VIA_SIBLING = (3, 5, 7)


def _weights_start(lands, after):
    n = len(lands)

    def body(a, s, new):
        x, y, c, me = _my_place()
        for t, land in enumerate(a):
            send, recv = new[t], new[n + t]
            for k in (1,) + SAME_CORE:
                peer, _ = _peer(x, y, c, k)
                _remote(land.at[me], land.at[me], send.at[k], recv.at[k], peer).start()

    sems, lands, token = _comm_call("weights_start", lands, [], [(NDEV,)] * (2 * n), body, after=after, token=True)
    return sems[:n], sems[n:], lands, token


def _weights_forward(name, land, recv, after):
    def body(a, s, new):
        (land,), (recv,), (fsend, frecv) = a, s, new
        x, y, c, me = _my_place()
        sibling, _ = _peer(x, y, c, 1)
        for k in SAME_CORE:
            peer, slot = _peer(x, y, c, k)
            _remote(land.at[slot], land.at[slot], fsend.at[k], recv.at[k], peer).wait_recv()
            _remote(land.at[slot], land.at[slot], fsend.at[k], frecv.at[k ^ 1], sibling).start()

    return _comm_call(name, [land], [recv], [(NDEV,), (NDEV,)], body, after=after)


def _weights_wait(name, land, send, recv, fsend, frecv, after):
    def body(a, s, new):
        (land,), (send, recv, fsend, frecv) = a, s
        x, y, c, me = _my_place()
        sibling, sib_slot = _peer(x, y, c, 1)
        _remote(land.at[sib_slot], land.at[sib_slot], send.at[1], recv.at[1], sibling).wait_recv()
        for k in VIA_SIBLING:
            _, slot = _peer(x, y, c, k)
            _remote(land.at[slot], land.at[slot], fsend.at[k ^ 1], frecv.at[k], sibling).wait_recv()
        for k in (1,) + SAME_CORE:
            peer, _ = _peer(x, y, c, k)
            _remote(land.at[me], land.at[me], send.at[k], recv.at[k], peer).wait_send()
        for k in SAME_CORE:
            _, slot = _peer(x, y, c, k)
            _remote(land.at[slot], land.at[slot], fsend.at[k], frecv.at[k ^ 1], sibling).wait_send()

    return _comm_call(name, [land], [send, recv, fsend, frecv], [], body, after=after)[1][0]


NCHIP = NDEV // 2


def _pair_start(name, src):
    npair = src.shape[0] // 2

    def body(a, s, new):
        (src, pair), (send, recv) = a, new
        x, y, c, me = _my_place()
        sibling, _ = _peer(x, y, c, 1)
        for i in range(npair):
            _remote(src.at[2 * i + 1 - c], pair.at[i], send.at[i], recv.at[i], sibling).start()

    pair = lax.empty((npair,) + src.shape[1:], src.dtype)
    (send, recv), (src, pair), token = _comm_call(name, [src, pair], [], [(npair,), (npair,)], body, token=True)
    return send, recv, src, pair, token


def _pair_wait(name, src, pair, send, recv, after):
    npair = pair.shape[0]

    def body(a, s, new):
        (src, pair), (send, recv) = a, s
        x, y, c, me = _my_place()
        sibling, _ = _peer(x, y, c, 1)
        for i in range(npair):
            cp = _remote(src.at[2 * i + 1 - c], pair.at[i], send.at[i], recv.at[i], sibling)
            cp.wait_recv()
            cp.wait_send()

    return _comm_call(name, [src, pair], [send, recv], [], body, after=after)[1]


def _pair_sum(name, src, pair, core, tr=256):
    npair, R, Cc = pair.shape
    tr = min(tr, R)

    def body(core_ref, a_ref, b_ref, o_ref):
        o_ref[...] = (a_ref[...].astype(f32) + b_ref[...].astype(f32)).astype(o_ref.dtype)

    grid_spec = pltpu.PrefetchScalarGridSpec(
        num_scalar_prefetch=1, grid=(npair, R // tr),
        in_specs=[pl.BlockSpec((None, tr, Cc), lambda i, r, core: (2 * i + core[0], r, 0)),
                  pl.BlockSpec((None, tr, Cc), lambda i, r, core: (i, r, 0))],
        out_specs=pl.BlockSpec((None, tr, Cc), lambda i, r, core: (i, r, 0)))
    return pl.pallas_call(body, name=name, out_shape=jax.ShapeDtypeStruct(pair.shape, pair.dtype),
                          grid_spec=grid_spec, compiler_params=_params("parallel", "parallel"))(core, src, pair)


def _owner_chip(first, i):
    q = first // 2 + i
    return q >> 1 & 1, q & 1


def _chip_start(name, sums, land, first, rows=None, after=()):
    npair = sums.shape[0]
    rows = pl.ds(*(rows or (0, sums.shape[1])))

    def body(a, s, new):
        (sums, land), (send, recv) = a, new
        x, y, c, me = _my_place()
        for i in range(npair):
            ox, oy = _owner_chip(first, i)

            @pl.when((x != ox) | (y != oy))
            def _():
                _remote(sums.at[i, rows], land.at[2 * x + y, rows], send.at[i], recv.at[2 * x + y], (ox, oy, c)).start()

    (send, recv), (sums, land), token = _comm_call(name, [sums, land], [], [(npair,), (NCHIP,)], body, after=after,
                                                   token=True)
    return send, recv, sums, land, token


def _chip_wait(name, sums, land, send, recv, first, after, rows=None):
    npair = sums.shape[0]
    rows = pl.ds(*(rows or (0, sums.shape[1])))

    def body(a, s, new):
        (sums, land), (send, recv) = a, s
        x, y, c, me = _my_place()
        mine = (me >= first) & (me < first + 2 * npair)
        for i in range(npair):
            ox, oy = _owner_chip(first, i)

            @pl.when((x != ox) | (y != oy))
            def _():
                _remote(sums.at[i, rows], land.at[2 * x + y, rows], send.at[i], recv.at[2 * x + y], (ox, oy, c)).wait_send()
        for q in range(NCHIP):
            @pl.when(mine & (2 * x + y != q))
            def _():
                _remote(sums.at[0, rows], land.at[q, rows], send.at[0], recv.at[q], (q >> 1, q & 1, c)).wait_recv()

    return _comm_call(name, [sums, land], [send, recv], [], body, after=after)[1]


def _matmul(a, b, *, name, out_dtype, ta=False, tb=False, b_slots=False, out_slots=0, b_cols=None,
            tm=1024, tn=1024, tk=2048, dep=None):
    M, K = (a.shape[1], a.shape[0]) if ta else a.shape
    col0 = 0
    if b_slots:
        slab = b.shape[2]
        N = b.shape[1] if tb else NDEV * slab
        assert (K if tb else N) == NDEV * slab
    elif b_cols is not None:
        assert not tb
        col0, N = b_cols
    else:
        N = b.shape[0] if tb else b.shape[1]
    tm, tn, tk = min(tm, M), min(tn, N), min(tk, K)
    if b_slots:
        if tb:
            tk = min(tk, slab)
        else:
            tn = min(tn, slab)
    if out_slots:
        tn = min(tn, N // out_slots)
    nm, nn, nk = M // tm, N // tn, K // tk
    assert (nm * tm, nn * tn, nk * tk) == (M, N, K) and col0 % tn == 0, (name, M, N, K, tm, tn, tk)
    j0 = col0 // tn

    a_spec = pl.BlockSpec((tk, tm), lambda i, j, k: (k, i)) if ta else pl.BlockSpec((tm, tk), lambda i, j, k: (i, k))
    if b_slots and tb:
        per = slab // tk
        b_spec = pl.BlockSpec((None, tn, tk), lambda i, j, k: (k // per, j, k % per))
    elif b_slots:
        per = slab // tn
        b_spec = pl.BlockSpec((None, tk, tn), lambda i, j, k: (j // per, k, j % per))
    elif tb:
        b_spec = pl.BlockSpec((tn, tk), lambda i, j, k: (j, k))
    else:
        b_spec = pl.BlockSpec((tk, tn), lambda i, j, k: (k, j + j0))
    if out_slots:
        per_o = (N // out_slots) // tn
        o_spec = pl.BlockSpec((None, tm, tn), lambda i, j, k: (j // per_o, i, j % per_o))
        out_shape = jax.ShapeDtypeStruct((out_slots, M, N // out_slots), out_dtype)
    else:
        o_spec = pl.BlockSpec((tm, tn), lambda i, j, k: (i, j))
        out_shape = jax.ShapeDtypeStruct((M, N), out_dtype)
    dims = (((0 if ta else 1,), (1 if tb else 0,)), ((), ()))
    deps = [] if dep is None else [dep]

    def body(a_ref, b_ref, *rest):
        o_ref = rest[len(deps)]
        prod = lax.dot_general(a_ref[...], b_ref[...], dims, preferred_element_type=f32)
        if nk == 1:
            o_ref[...] = prod.astype(out_dtype)
            return
        acc_ref = rest[len(deps) + 1]
        k = pl.program_id(2)

        @pl.when(k == 0)
        def _():
            acc_ref[...] = prod

        @pl.when((k > 0) & (k < nk - 1))
        def _():
            acc_ref[...] += prod

        @pl.when(k == nk - 1)
        def _():
            o_ref[...] = (acc_ref[...] + prod).astype(out_dtype)

    return pl.pallas_call(
        body, name=name, out_shape=out_shape, grid=(nm, nn, nk),
        in_specs=[a_spec, b_spec] + [ANY_SPEC] * len(deps), out_specs=o_spec,
        scratch_shapes=[pltpu.VMEM((tm, tn), f32)] if nk > 1 else [],
        compiler_params=_params("parallel", "parallel", "arbitrary"),
    )(a, b, *deps)


def _matmul_slabs_t(a, b, *, name, tm=512, tn=512, dep=None):
    M, K = a.shape
    n_slab, N, slab = b.shape
    assert K == n_slab * slab
    tm, tn = min(tm, M), min(tn, N)
    deps = [] if dep is None else [dep]

    def body(a_ref, b_ref, *rest):
        o_ref = rest[len(deps)]
        acc = None
        for s in range(n_slab):
            prod = lax.dot_general(a_ref[:, s * slab:(s + 1) * slab], b_ref[s], (((1,), (1,)), ((), ())),
                                   preferred_element_type=f32)
            acc = prod if acc is None else acc + prod
        o_ref[...] = acc

    return pl.pallas_call(
        body, name=name, out_shape=jax.ShapeDtypeStruct((M, N), f32), grid=(M // tm, N // tn),
        in_specs=[pl.BlockSpec((tm, K), lambda i, j: (i, 0)), pl.BlockSpec((n_slab, tn, slab), lambda i, j: (0, j, 0))]
        + [ANY_SPEC] * len(deps),
        out_specs=pl.BlockSpec((tm, tn), lambda i, j: (i, j)), compiler_params=_params("parallel", "parallel"),
    )(a, b, *deps)


def _silu_rows(c):
    D = c.shape[1]

    def body(c_ref, o_ref):
        o_ref[...] = jnp.broadcast_to(_silu(c_ref[...]), (8, D))

    return pl.pallas_call(body, name="silu_c", out_shape=jax.ShapeDtypeStruct((8, D), f32))(c)


def _ada_cols(c_all, w_ada, b_cols):
    D, W = w_ada.shape

    def body(c_ref, w_ref, b_ref, o_ref):
        mod = lax.dot_general(c_ref[...], w_ref[...], (((1,), (0,)), ((), ())), preferred_element_type=f32,
                              precision=lax.Precision.HIGHEST) + b_ref[...]
        for j in range(NDEV):
            o_ref[j] = jnp.broadcast_to(mod[j:j + 1, :], (8, W))

    return pl.pallas_call(body, name="ada_cols", out_shape=jax.ShapeDtypeStruct((NDEV, 8, W), f32),
                          compiler_params=_params())(c_all, w_ada, b_cols)


def _prenorm(x, scale, shift, g_pre, dep, tr=256):
    S, D = x.shape
    tr = min(tr, S)

    def body(x_ref, sc_ref, sh_ref, g_ref, dep_ref, h_ref):
        xv = x_ref[...]
        r = lax.rsqrt(jnp.mean(xv * xv, axis=-1, keepdims=True) + EPS)
        h_ref[...] = ((xv * r) * g_ref[...] * (1.0 + sc_ref[...]) + sh_ref[...]).astype(bf16)

    row = pl.BlockSpec((tr, D), lambda i: (i, 0))
    vec = pl.BlockSpec((1, D), lambda i: (0, 0))
    return pl.pallas_call(body, name="prenorm", out_shape=jax.ShapeDtypeStruct((S, D), bf16), grid=(S // tr,),
                          in_specs=[row, vec, vec, vec, ANY_SPEC], out_specs=row, compiler_params=_params("parallel"))(
                              x, scale, shift, g_pre, dep)


def _ext_rows(i, tr, S):
    g = lax.broadcasted_iota(jnp.int32, (tr + 16, 1), 0) + (i * tr - 8)
    return (g >= 0) & (g < S)


def _halo_specs(tr, S, C, col):
    nb8 = S // 8
    main = pl.BlockSpec((tr, C), lambda i: (i, col))
    prev = pl.BlockSpec((8, C), lambda i: (jnp.maximum(i * (tr // 8) - 1, 0), col))
    nxt = pl.BlockSpec((8, C), lambda i: (jnp.minimum((i + 1) * (tr // 8), nb8 - 1), col))
    return prev, main, nxt


def _conv_fwd(proj, conv_w, conv_b, g_conv, tr=256):
    S, C = proj.shape[0], proj.shape[1] // 8
    tr = min(tr, S)

    def body(up, um, un, cp, cm, cn, bg_ref, zc_ref, w_ref, cb_ref, g_ref, o_ref):
        i = pl.program_id(0)
        exists = _ext_rows(i, tr, S)
        u = jnp.concatenate([up[...], um[...], un[...]], axis=0)
        cg = jnp.concatenate([cp[...], cm[...], cn[...]], axis=0)
        t = jnp.where(exists, cg * u, 0.0)
        t_before = pltpu.roll(t, 1, 0)[8:tr + 8]
        t_after = pltpu.roll(t, tr + 15, 0)[8:tr + 8]
        w = w_ref[...]
        cv = w[0:1] * t_before + w[1:2] * t[8:tr + 8] + w[2:3] * t_after + cb_ref[...]
        yc = bg_ref[...] * cv
        rc = lax.rsqrt(jnp.mean(yc * yc, axis=-1, keepdims=True) + EPS)
        o_ref[...] = ((yc * rc) * g_ref[...] * _silu(zc_ref[...])).astype(bf16)

    u_specs = _halo_specs(tr, S, C, 0)
    c_specs = _halo_specs(tr, S, C, 2)
    vec = pl.BlockSpec((1, C), lambda i: (0, 0))
    return pl.pallas_call(
        body, name="conv_fwd", out_shape=jax.ShapeDtypeStruct((S, 2 * C), bf16), grid=(S // tr,),
        in_specs=[*u_specs, *c_specs, pl.BlockSpec((tr, C), lambda i: (i, 1)), pl.BlockSpec((tr, C), lambda i: (i, 3)),
                  pl.BlockSpec((8, C), lambda i: (0, 0)), vec, vec],
        out_specs=pl.BlockSpec((tr, C), lambda i: (i, 0)), compiler_params=_params("parallel"),
    )(proj, proj, proj, proj, proj, proj, proj, proj, conv_w, conv_b, g_conv)


def _branch_geometry(S, r):
    L = S // r
    nq = min(128, L)
    nk = min(nq + 2 * HALF_WIN, L)
    return L, nq, nk, L // nq


def _block_rows(idx, r, L, nq, nk, nblk):
    if r == 1:
        rho, qb = 0, idx
    else:
        rho, qb = idx // nblk, idx % nblk
    i0 = qb * nq
    ws = jnp.clip(i0 - HALF_WIN, 0, L - nk)
    if r == 1:
        qrows = pl.ds(pl.multiple_of(i0, 8), nq)
        krows = pl.ds(pl.multiple_of(ws, 8), nk)
    else:
        qrows = pl.ds(rho + r * i0, nq, stride=r)
        krows = pl.ds(rho + r * ws, nk, stride=r)
    return qrows, krows, i0 - ws


N_CASES = 3
SCALE = HEAD_DIM ** -0.5
ATTN_UNROLL = 8


def _bias_shape(S):
    return (len(BRANCHES) * N_CASES * 2, min(128, S), min(128 + 2 * HALF_WIN, S))


def _bias_index(b, case, head):
    return (b * N_CASES + case) * 2 + head


def _fill_bias(bias_scr, sl_ref, S):
    sl = sl_ref[...]
    slope = (sl[0:1, 0:1], sl[0:1, HEAD_DIM:HEAD_DIM + 1])
    for b, (_, r) in enumerate(BRANCHES):
        L, nq, nk, nblk = _branch_geometry(S, r)
        assert nblk == 1 or L >= nq + 2 * HALF_WIN
        rel = lax.broadcasted_iota(jnp.int32, (nq, nk), 0) - lax.broadcasted_iota(jnp.int32, (nq, nk), 1)
        for case in range(N_CASES):
            d = jnp.abs(rel + case * HALF_WIN)
            dist = d.astype(f32) * float(r)
            for head in range(2):
                bias_scr[_bias_index(b, case, head), 0:nq, 0:nk] = jnp.where(d <= HALF_WIN, -slope[head] * dist, NEG_INF)


def _head_slopes(n_heads):
    slopes = 2.0 ** (-8.0 * jnp.arange(1, n_heads + 1, dtype=f32) / n_heads)
    return jnp.broadcast_to(jnp.repeat(slopes.reshape(n_heads // 2, 2), HEAD_DIM, axis=1)[:, None, :],
                            (n_heads // 2, 8, PAIR))


def _attn_fwd(proj, slopes):
    S, C = proj.shape[0], proj.shape[1] // 8
    npair = C // PAIR

    def body(q_ref, k_ref, v_ref, sl_ref, o_ref, lse_ref, m_scr, l_scr, a_scr, bias_scr):
        lane = lax.broadcasted_iota(jnp.int32, (1, PAIR), 1)
        first = lane < HEAD_DIM
        _fill_bias(bias_scr, sl_ref, S)

        for b, (_, r) in enumerate(BRANCHES):
            L, nq, nk, nblk = _branch_geometry(S, r)

            def step(idx, carry, b=b, r=r, L=L, nq=nq, nk=nk, nblk=nblk):
                qrows, krows, off = _block_rows(idx, r, L, nq, nk, nblk)
                case = off // HALF_WIN
                q2 = q_ref[qrows, :] * SCALE
                k2 = k_ref[krows, :].astype(bf16)
                v2 = v_ref[krows, :].astype(bf16)
                ms, accs = [], []
                for hh in range(2):
                    mine = first if hh == 0 else ~first
                    qh = jnp.where(mine, q2, 0.0).astype(bf16)
                    s = lax.dot_general(qh, k2, (((1,), (1,)), ((), ())), preferred_element_type=f32)
                    s = s + bias_scr[_bias_index(b, case, hh), 0:nq, 0:nk]
                    m = jnp.max(s, axis=-1, keepdims=True)
                    p = jnp.exp(s - m).astype(bf16)
                    vh = jnp.where(mine, v2, jnp.ones_like(v2))
                    ms.append(m)
                    accs.append(jnp.dot(p, vh, preferred_element_type=f32))
                m_scr[b, qrows, :] = jnp.where(first, ms[0], ms[1])
                a_scr[b, qrows, :] = jnp.where(first, accs[0], accs[1])
                l_scr[b, qrows, :] = jnp.where(first, accs[1], accs[0])
                return carry

            lax.fori_loop(0, S // nq, step, 0, unroll=min(ATTN_UNROLL, S // nq))

        ch = min(256, S)

        def merge(i, carry):
            rows = pl.ds(pl.multiple_of(i * ch, 8), ch)
            m = jnp.maximum(jnp.maximum(m_scr[0, rows, :], m_scr[1, rows, :]), m_scr[2, rows, :])
            l = jnp.zeros((ch, PAIR), f32)
            acc = jnp.zeros((ch, PAIR), f32)
            for b in range(3):
                w = jnp.exp(m_scr[b, rows, :] - m)
                l = l + w * pltpu.roll(l_scr[b, rows, :], HEAD_DIM, 1)
                acc = acc + w * a_scr[b, rows, :]
            o_ref[rows, :] = acc / l
            lse_ref[rows, :] = m + jnp.log(l)
            return carry

        lax.fori_loop(0, S // ch, merge, 0)

    blk = lambda part: pl.BlockSpec((S, PAIR), lambda p: (0, part * npair + p))
    out = pl.BlockSpec((S, PAIR), lambda p: (0, p))
    return pl.pallas_call(
        body, name="attn_fwd",
        out_shape=(jax.ShapeDtypeStruct((S, C), f32), jax.ShapeDtypeStruct((S, C), f32)), grid=(npair,),
        in_specs=[blk(4), blk(5), blk(6), pl.BlockSpec((None, 8, PAIR), lambda p: (p, 0, 0))],
        out_specs=(out, out),
        scratch_shapes=[pltpu.VMEM((3, S, PAIR), f32)] * 3 + [pltpu.VMEM(_bias_shape(S), f32)],
        compiler_params=_params("parallel"),
    )(proj, proj, proj, slopes)


def _attn_post(ycat, o, proj, g_attn, tr=256):
    S, C = o.shape
    tr = min(tr, S)

    def body(y_ref, o_ref, z_ref, g_ref, out_ref):
        del y_ref
        ov = o_ref[...]
        ra = lax.rsqrt(jnp.mean(ov * ov, axis=-1, keepdims=True) + EPS)
        out_ref[...] = ((ov * ra) * g_ref[...] * _silu(z_ref[...])).astype(bf16)

    return pl.pallas_call(
        body, name="attn_post", out_shape=jax.ShapeDtypeStruct(ycat.shape, ycat.dtype), grid=(S // tr,),
        in_specs=[HBM_SPEC, pl.BlockSpec((tr, C), lambda i: (i, 0)), pl.BlockSpec((tr, C), lambda i: (i, 7)),
                  pl.BlockSpec((1, C), lambda i: (0, 0))],
        out_specs=pl.BlockSpec((tr, C), lambda i: (i, 1)), input_output_aliases={0: 0},
        compiler_params=_params("arbitrary"),
    )(ycat, o, proj, g_attn)


def _sandwich(y, x, target, gate, g_post, tr=256):
    S, D = y.shape
    tr = min(tr, S)

    def body(y_ref, x_ref, t_ref, gate_ref, g_ref, dy_ref, dout_ref, sums_ref):
        i = pl.program_id(0)
        yv = y_ref[...]
        rp = lax.rsqrt(jnp.mean(yv * yv, axis=-1, keepdims=True) + EPS)
        yhat = yv * rp
        yn = yhat * g_ref[...]
        err = (x_ref[...] + gate_ref[...] * yn) - t_ref[...]
        dout = err * (1.0 / D)
        dout_ref[...] = dout
        dyn = dout * gate_ref[...]
        w = dyn * g_ref[...]
        dy_ref[...] = (rp * (w - yhat * jnp.mean(w * yhat, axis=-1, keepdims=True))).astype(bf16)
        loss = 0.5 * jnp.sum(jnp.mean(err * err, axis=-1, keepdims=True), axis=0, keepdims=True)
        row = lax.broadcasted_iota(jnp.int32, (8, D), 0)
        upd = jnp.where(row == 0, jnp.sum(dout * yn, axis=0, keepdims=True),
                        jnp.where(row == 1, jnp.sum(dyn * yhat, axis=0, keepdims=True),
                                  jnp.where(row == 2, loss, 0.0)))

        @pl.when(i == 0)
        def _():
            sums_ref[...] = upd

        @pl.when(i > 0)
        def _():
            sums_ref[...] += upd

    row = pl.BlockSpec((tr, D), lambda i: (i, 0))
    vec = pl.BlockSpec((1, D), lambda i: (0, 0))
    return pl.pallas_call(
        body, name="sandwich",
        out_shape=(jax.ShapeDtypeStruct((S, D), bf16), jax.ShapeDtypeStruct((S, D), f32), jax.ShapeDtypeStruct((8, D), f32)),
        grid=(S // tr,), in_specs=[row, row, row, vec, vec],
        out_specs=(row, row, pl.BlockSpec((8, D), lambda i: (0, 0))), compiler_params=_params("arbitrary"),
    )(y, x, target, gate, g_post)


def _conv_bwd(proj, dycat, conv_w, conv_b, g_conv, dep, tr=256):
    S, C = proj.shape[0], proj.shape[1] // 8
    tr = min(tr, S)
    n = tr + 16

    def body(*refs):
        ins, (w_ref, cb_ref, g_ref, _, dp_ref, sums_ref) = refs[:15], refs[15:]
        i = pl.program_id(0)
        exists = _ext_rows(i, tr, S)
        u, bg, cg, zc, dyn = (jnp.concatenate([ins[3 * t][...], ins[3 * t + 1][...], ins[3 * t + 2][...]], axis=0)
                              for t in range(5))
        w = w_ref[...]
        t = jnp.where(exists, cg * u, 0.0)
        t_before, t_after = pltpu.roll(t, 1, 0), pltpu.roll(t, n - 1, 0)
        cv = w[0:1] * t_before + w[1:2] * t + w[2:3] * t_after + cb_ref[...]
        yc = bg * cv
        rc = lax.rsqrt(jnp.mean(yc * yc, axis=-1, keepdims=True) + EPS)
        yhat = yc * rc
        sz = _silu(zc)
        wgt = dyn * g_ref[...] * sz
        dyc = rc * (wgt - yhat * jnp.mean(wgt * yhat, axis=-1, keepdims=True))
        dcv = jnp.where(exists, dyc * bg, 0.0)
        dt = w[0:1] * pltpu.roll(dcv, n - 1, 0) + w[1:2] * dcv + w[2:3] * pltpu.roll(dcv, 1, 0)
        mid = slice(8, tr + 8)
        dp_ref[:, 0:C] = (dt * cg)[mid].astype(bf16)
        dp_ref[:, C:2 * C] = (dyc * cv)[mid].astype(bf16)
        dp_ref[:, 2 * C:3 * C] = (dt * u)[mid].astype(bf16)
        dp_ref[:, 3 * C:4 * C] = (dyn * yhat * g_ref[...] * _dsilu(zc))[mid].astype(bf16)
        colsum = lambda v: jnp.sum(v[mid], axis=0, keepdims=True)
        parts = [colsum(dyn * yhat * sz), colsum(dcv), colsum(dcv * t_before), colsum(dcv * t), colsum(dcv * t_after)]
        row = lax.broadcasted_iota(jnp.int32, (8, C), 0)
        upd = jnp.zeros((8, C), f32)
        for j, pj in enumerate(parts):
            upd = jnp.where(row == j, pj, upd)

        @pl.when(i == 0)
        def _():
            sums_ref[...] = upd

        @pl.when(i > 0)
        def _():
            sums_ref[...] += upd

    specs = []
    for col in range(4):
        specs += _halo_specs(tr, S, C, col)
    specs += _halo_specs(tr, S, C, 0)
    vec = pl.BlockSpec((1, C), lambda i: (0, 0))
    return pl.pallas_call(
        body, name="conv_bwd",
        out_shape=(jax.ShapeDtypeStruct((S, 4 * C), bf16), jax.ShapeDtypeStruct((8, C), f32)), grid=(S // tr,),
        in_specs=[*specs, pl.BlockSpec((8, C), lambda i: (0, 0)), vec, vec, ANY_SPEC],
        out_specs=(pl.BlockSpec((tr, 4 * C), lambda i: (i, 0)), pl.BlockSpec((8, C), lambda i: (0, 0))),
        compiler_params=_params("arbitrary"),
    )(*([proj] * 12), dycat, dycat, dycat, conv_w, conv_b, g_conv, dep)


def _attn_post_bwd(o, proj, dycat, g_attn, dep, tr=256):
    S, C = o.shape
    tr = min(tr, S)

    def body(o_ref, z_ref, dy_ref, g_ref, dep_ref, do_ref, dz_ref, sums_ref):
        i = pl.program_id(0)
        ov, zv, dyn = o_ref[...], z_ref[...], dy_ref[...]
        ra = lax.rsqrt(jnp.mean(ov * ov, axis=-1, keepdims=True) + EPS)
        ohat = ov * ra
        sz = _silu(zv)
        wgt = dyn * g_ref[...] * sz
        do_ref[...] = ra * (wgt - ohat * jnp.mean(wgt * ohat, axis=-1, keepdims=True))
        dz_ref[...] = (dyn * ohat * g_ref[...] * _dsilu(zv)).astype(bf16)
        row = lax.broadcasted_iota(jnp.int32, (8, C), 0)
        upd = jnp.where(row == 0, jnp.sum(dyn * ohat * sz, axis=0, keepdims=True), 0.0)

        @pl.when(i == 0)
        def _():
            sums_ref[...] = upd

        @pl.when(i > 0)
        def _():
            sums_ref[...] += upd

    return pl.pallas_call(
        body, name="attn_post_bwd",
        out_shape=(jax.ShapeDtypeStruct((S, C), f32), jax.ShapeDtypeStruct((S, C), bf16), jax.ShapeDtypeStruct((8, C), f32)),
        grid=(S // tr,),
        in_specs=[pl.BlockSpec((tr, C), lambda i: (i, 0)), pl.BlockSpec((tr, C), lambda i: (i, 7)),
                  pl.BlockSpec((tr, C), lambda i: (i, 1)), pl.BlockSpec((1, C), lambda i: (0, 0)), ANY_SPEC],
        out_specs=(pl.BlockSpec((tr, C), lambda i: (i, 0)), pl.BlockSpec((tr, C), lambda i: (i, 0)),
                   pl.BlockSpec((8, C), lambda i: (0, 0))),
        compiler_params=_params("arbitrary"),
    )(o, proj, dycat, g_attn, dep)


def _attn_bwd(proj, o, do, lse, slopes, dep):
    S, C = o.shape
    npair = C // PAIR

    def body(q_ref, k_ref, v_ref, o_ref, do_ref, lse_ref, sl_ref, dep_ref, dq_ref, dk_ref, dv_ref,
             dq_scr, dk_scr, dv_scr, dl_scr, bias_scr):
        lane = lax.broadcasted_iota(jnp.int32, (1, PAIR), 1)
        first = lane < HEAD_DIM
        _fill_bias(bias_scr, sl_ref, S)
        ch = min(256, S)

        def prep(i, carry):
            rows = pl.ds(pl.multiple_of(i * ch, 8), ch)
            prod = do_ref[rows, :] * o_ref[rows, :]
            d0 = jnp.sum(jnp.where(first, prod, 0.0), axis=-1, keepdims=True)
            d1 = jnp.sum(jnp.where(first, 0.0, prod), axis=-1, keepdims=True)
            dl_scr[rows, :] = jnp.where(first, d0, d1)
            zero = jnp.zeros((ch, PAIR), f32)
            dq_scr[rows, :] = zero
            dk_scr[rows, :] = zero
            dv_scr[rows, :] = zero
            return carry

        lax.fori_loop(0, S // ch, prep, 0)

        for b, (_, r) in enumerate(BRANCHES):
            L, nq, nk, nblk = _branch_geometry(S, r)

            def step(idx, carry, b=b, r=r, L=L, nq=nq, nk=nk, nblk=nblk):
                qrows, krows, off = _block_rows(idx, r, L, nq, nk, nblk)
                case = off // HALF_WIN
                q2 = q_ref[qrows, :] * SCALE
                k2 = k_ref[krows, :].astype(bf16)
                v2 = v_ref[krows, :].astype(bf16)
                do2 = do_ref[qrows, :]
                lse2 = lse_ref[qrows, :]
                dl2 = dl_scr[qrows, :]
                dq2 = jnp.zeros((nq, PAIR), f32)
                dk2 = jnp.zeros((nk, PAIR), f32)
                dv2 = jnp.zeros((nk, PAIR), f32)
                for hh in range(2):
                    mine = first if hh == 0 else ~first
                    lo = hh * HEAD_DIM
                    qh = jnp.where(mine, q2, 0.0).astype(bf16)
                    doh = jnp.where(mine, do2, 0.0).astype(bf16)
                    s = lax.dot_general(qh, k2, (((1,), (1,)), ((), ())), preferred_element_type=f32)
                    s = s + bias_scr[_bias_index(b, case, hh), 0:nq, 0:nk]
                    p = jnp.exp(s - lse2[:, lo:lo + 1])
                    dv2 = dv2 + lax.dot_general(p.astype(bf16), doh, (((0,), (0,)), ((), ())), preferred_element_type=f32)
                    dp = lax.dot_general(doh, v2, (((1,), (1,)), ((), ())), preferred_element_type=f32)
                    ds = (p * (dp - dl2[:, lo:lo + 1])).astype(bf16)
                    dq2 = dq2 + jnp.where(mine, jnp.dot(ds, k2, preferred_element_type=f32), 0.0)
                    dk2 = dk2 + lax.dot_general(ds, qh, (((0,), (0,)), ((), ())), preferred_element_type=f32)
                dq_scr[qrows, :] = dq_scr[qrows, :] + dq2
                dk_scr[krows, :] = dk_scr[krows, :] + dk2
                dv_scr[krows, :] = dv_scr[krows, :] + dv2
                return carry

            lax.fori_loop(0, S // nq, step, 0, unroll=min(ATTN_UNROLL, S // nq))

        dq_ref[...] = (dq_scr[...] * SCALE).astype(bf16)
        dk_ref[...] = dk_scr[...].astype(bf16)
        dv_ref[...] = dv_scr[...].astype(bf16)

    blk = lambda part: pl.BlockSpec((S, PAIR), lambda p: (0, part * npair + p))
    own = pl.BlockSpec((S, PAIR), lambda p: (0, p))
    grad = jax.ShapeDtypeStruct((S, C), bf16)
    return pl.pallas_call(
        body, name="attn_bwd", out_shape=(grad, grad, grad), grid=(npair,),
        in_specs=[blk(4), blk(5), blk(6), own, own, own, pl.BlockSpec((None, 8, PAIR), lambda p: (p, 0, 0)), ANY_SPEC],
        out_specs=(own, own, own),
        scratch_shapes=[pltpu.VMEM((S, PAIR), f32)] * 4 + [pltpu.VMEM(_bias_shape(S), f32)],
        compiler_params=_params("parallel"),
    )(proj, proj, proj, o, do, lse, slopes, dep)


def _prenorm_bwd(dh, x, dout, scale, g_pre, tr=256):
    S, D = x.shape
    tr = min(tr, S)

    def body(dh_ref, x_ref, dout_ref, sc_ref, g_ref, gx_ref, sums_ref):
        i = pl.program_id(0)
        xv, dhv = x_ref[...], dh_ref[...]
        r = lax.rsqrt(jnp.mean(xv * xv, axis=-1, keepdims=True) + EPS)
        xn = xv * r
        dxn = dhv * (g_ref[...] * (1.0 + sc_ref[...]))
        gx_ref[...] = dout_ref[...] + r * (dxn - xn * jnp.mean(dxn * xn, axis=-1, keepdims=True))
        dhx = dhv * xn
        row = lax.broadcasted_iota(jnp.int32, (8, D), 0)
        upd = jnp.where(row == 0, jnp.sum(dhv, axis=0, keepdims=True),
                        jnp.where(row == 1, jnp.sum(dhx, axis=0, keepdims=True) * g_ref[...],
                                  jnp.where(row == 2, jnp.sum(dhx, axis=0, keepdims=True) * (1.0 + sc_ref[...]), 0.0)))

        @pl.when(i == 0)
        def _():
            sums_ref[...] = upd

        @pl.when(i > 0)
        def _():
            sums_ref[...] += upd

    row = pl.BlockSpec((tr, D), lambda i: (i, 0))
    vec = pl.BlockSpec((1, D), lambda i: (0, 0))
    return pl.pallas_call(
        body, name="prenorm_bwd",
        out_shape=(jax.ShapeDtypeStruct((S, D), f32), jax.ShapeDtypeStruct((8, D), f32)), grid=(S // tr,),
        in_specs=[row, row, row, vec, vec], out_specs=(row, pl.BlockSpec((8, D), lambda i: (0, 0))),
        compiler_params=_params("arbitrary"),
    )(dh, x, dout, scale, g_pre)


def _adamw(w, g, m, v):
    m = ADAM_B1 * m + (1.0 - ADAM_B1) * g
    v = ADAM_B2 * v + (1.0 - ADAM_B2) * (g * g)
    m_hat = m / (1.0 - ADAM_B1 ** ADAM_STEP)
    v_hat = v / (1.0 - ADAM_B2 ** ADAM_STEP)
    delta = -ADAM_LR * (m_hat / (jnp.sqrt(v_hat) + ADAM_EPS) + ADAM_WD * w)
    return delta, m, v


def _sum_rows(parts, dep):
    P = parts.shape[1]

    def body(p_ref, dep_ref, o_ref):
        acc = p_ref[0:1, :]
        for j in range(1, NDEV):
            acc = acc + p_ref[j:j + 1, :]
        o_ref[...] = jnp.broadcast_to(acc, (8, P))

    vmem = pl.BlockSpec(memory_space=pltpu.VMEM)
    return pl.pallas_call(body, name="sum_small", out_shape=jax.ShapeDtypeStruct((8, P), f32),
                          in_specs=[vmem, ANY_SPEC], out_specs=vmem, compiler_params=_params())(parts, dep)


def _adamw_small(w, g, m, v):
    def body(w_ref, g_ref, m_ref, v_ref, d_ref, nm_ref, nv_ref):
        d_ref[...], nm_ref[...], nv_ref[...] = _adamw(w_ref[...], g_ref[...], m_ref[...], v_ref[...])

    out = jax.ShapeDtypeStruct(w.shape, f32)
    return pl.pallas_call(body, name="adamw_small", out_shape=(out, out, out), compiler_params=_params())(w, g, m, v)


def _adamw_sharded(parts, own, w, m, v, name, tr=128):
    R, Cc = w.shape
    tr = min(tr, R)
    n = parts.shape[0]

    def body(p_ref, own_ref, w_ref, m_ref, v_ref, g_ref, d_ref, nm_ref, nv_ref):
        g = own_ref[...].astype(f32)
        for j in range(n):
            g = g + p_ref[j].astype(f32)
        g_ref[...] = g
        d_ref[...], nm_ref[...], nv_ref[...] = _adamw(w_ref[...], g, m_ref[...], v_ref[...])

    row = pl.BlockSpec((tr, Cc), lambda i: (i, 0))
    out = jax.ShapeDtypeStruct((R, Cc), f32)
    return pl.pallas_call(
        body, name=name, out_shape=(out, out, out, out), grid=(R // tr,),
        in_specs=[pl.BlockSpec((n, tr, Cc), lambda i: (0, i, 0)), row, row, row, row],
        out_specs=(row, row, row, row), compiler_params=_params("parallel"),
    )(parts, own, w, m, v)


def _adamw_ada(c_t, dmod_cols, w, m, v, dep, tr=256):
    D, W = w.shape
    tr = min(tr, D)

    def body(c_ref, dm_ref, w_ref, m_ref, v_ref, dep_ref, g_ref, d_ref, nm_ref, nv_ref):
        cv, dm = c_ref[...], dm_ref[...]
        g = cv[:, 0:1] * dm[0:1, :]
        for b in range(1, NDEV):
            g = g + cv[:, b:b + 1] * dm[b:b + 1, :]
        g_ref[...] = g
        d_ref[...], nm_ref[...], nv_ref[...] = _adamw(w_ref[...], g, m_ref[...], v_ref[...])

    row = pl.BlockSpec((tr, W), lambda i: (i, 0))
    out = jax.ShapeDtypeStruct((D, W), f32)
    return pl.pallas_call(
        body, name="adamw_ada", out_shape=(out, out, out, out), grid=(D // tr,),
        in_specs=[pl.BlockSpec((tr, NDEV), lambda i: (i, 0)), pl.BlockSpec((NDEV, W), lambda i: (0, 0)), row, row, row,
                  ANY_SPEC],
        out_specs=(row, row, row, row), compiler_params=_params("parallel"),
    )(c_t, dmod_cols, w, m, v, dep)


def kernel(x, c, w_ada, b_ada, g_pre, w_in, conv_w, conv_b, g_conv, g_attn, w_out, g_post, loss_target, m_w_ada, m_b_ada, m_g_pre, m_w_in, m_conv_w, m_conv_b, m_g_conv, m_g_attn, m_w_out, m_g_post, v_w_ada, v_b_ada, v_g_pre, v_w_in, v_conv_w, v_conv_b, v_g_conv, v_g_attn, v_w_out, v_g_post):
    S, D = x.shape[1], x.shape[2]
    C = D // 2
    W = w_ada.shape[2]
    CW = conv_w.shape[2]
    me = 4 * lax.axis_index("x") + 2 * lax.axis_index("y") + lax.axis_index("c")
    x2, tgt = x[0], loss_target[0]
    w_ada2, w_in2, w_out2 = w_ada[0], w_in[0], w_out[0]

    R = D // NDEV
    core = lax.axis_index("c").astype(jnp.int32).reshape(1)

    cw_slab = jnp.zeros((8, CW), f32).at[:3].set(conv_w[0])
    c_rows, cw_g = _all_gather([_silu_rows(c), cw_slab], "gather_c")
    c_all = c_rows[:, 0, :]
    conv_w_full = jnp.transpose(cw_g, (1, 0, 2)).reshape(8, C)
    b_cols = lax.dynamic_slice_in_dim(b_ada, me * W, W, axis=1)
    (mod_slabs,) = _all_to_all([_ada_cols(c_all, w_ada2, b_cols)], "scatter_mod")
    mod = mod_slabs[:, 0, :].reshape(1, 3 * D)
    shift, scale, gate = mod[:, :D], mod[:, D:2 * D], mod[:, 2 * D:]

    land_i = lax.dynamic_update_slice(lax.empty((NDEV, D, C), bf16), w_in2.astype(bf16)[None], (me, 0, 0))
    land_o = lax.dynamic_update_slice(lax.empty((NDEV, R, D), bf16), w_out2.astype(bf16)[None], (me, 0, 0))
    (wi_send, wo_send), (wi_recv, wo_recv), (land_i, land_o), w_token = _weights_start([land_i, land_o], [mod_slabs])

    h = _prenorm(x2, scale, shift, g_pre, w_token)
    (fi_send, fi_recv), (land_i,), _ = _weights_forward("w_in_forward", land_i, wi_recv, after=[h])
    win_g = _weights_wait("w_in_wait", land_i, wi_send, wi_recv, fi_send, fi_recv, after=[h])
    proj = _matmul(h, win_g, name="in_proj", out_dtype=f32, b_slots=True)
    (fo_send, fo_recv), (land_o,), _ = _weights_forward("w_out_forward", land_o, wo_recv, after=[proj])
    slopes = _head_slopes(C // HEAD_DIM)
    ycat = _conv_fwd(proj, conv_w_full, conv_b, g_conv)
    o, lse = _attn_fwd(proj, slopes)
    ycat = _attn_post(ycat, o, proj, g_attn)
    wout_g = _weights_wait("w_out_wait", land_o, wo_send, wo_recv, fo_send, fo_recv, after=[ycat])
    wout_full = wout_g.reshape(D, D)
    y = _matmul(ycat, wout_full, name="out_proj", out_dtype=f32)
    dy, dout, post_sums = _sandwich(y, x2, tgt, gate, g_post)

    gw_out = _matmul(ycat, dy, name="out_proj_dw", out_dtype=bf16, ta=True).reshape(NDEV, R, D)
    po_send, po_recv, gw_out, pair_o, po_token = _pair_start("g_out_pair_start", gw_out)
    dycat = _matmul(dy, wout_full, name="out_proj_dx", out_dtype=f32, tb=True, dep=po_token)
    gw_out, pair_o = _pair_wait("g_out_pair_wait", gw_out, pair_o, po_send, po_recv, after=[dycat])
    sum_o = _pair_sum("g_out_pair_sum", gw_out, pair_o, core)
    co_send, co_recv, sum_o, land_go, co_token = _chip_start(
        "g_out_chip_start", sum_o, jnp.zeros((NCHIP, R, D), bf16), 0)
    dpc, conv_sums = _conv_bwd(proj, dycat, conv_w_full, conv_b, g_conv, co_token)
    gw_c = _matmul(h, dpc, name="in_proj_dw_conv", out_dtype=bf16, ta=True, out_slots=4)
    pc_send, pc_recv, gw_c, pair_c, pc_token = _pair_start("g_conv_pair_start", gw_c)
    do, dza, attn_sums = _attn_post_bwd(o, proj, dycat, g_attn, pc_token)
    gw_c, pair_c = _pair_wait("g_conv_pair_wait", gw_c, pair_c, pc_send, pc_recv, after=[do])
    sum_c = _pair_sum("g_conv_pair_sum", gw_c, pair_c, core)
    cc_send, cc_recv, sum_c, land_gi, cc_token = _chip_start(
        "g_conv_chip_start", sum_c, jnp.zeros((NCHIP, D, C), bf16), 0)
    dq, dk, dv = _attn_bwd(proj, o, do, lse, slopes, cc_token)
    dproj = jnp.concatenate([dpc, dq, dk, dv, dza], axis=1)
    gw_a = _matmul(h, dproj, name="in_proj_dw_attn", out_dtype=bf16, ta=True, out_slots=4, b_cols=(4 * C, 4 * C))
    pa_send, pa_recv, gw_a, pair_a, pa_token = _pair_start("g_attn_pair_start", gw_a)
    gw_a, pair_a = _pair_wait("g_attn_pair_wait", gw_a, pair_a, pa_send, pa_recv, after=[pa_token])
    sum_a = _pair_sum("g_attn_pair_sum", gw_a, pair_a, core)
    half = D // 2
    ca_send, ca_recv, sum_a, land_gi, ca_token = _chip_start("g_attn_chip_start_a", sum_a, land_gi, 4, (0, half))
    dh = _matmul_slabs_t(dproj, win_g, name="in_proj_dx", dep=ca_token)
    grad_x, pre_sums = _prenorm_bwd(dh, x2, dout, scale, g_pre)

    small = jnp.concatenate([pre_sums[0:1], pre_sums[1:2], post_sums[0:1],
                             pre_sums[2:3], post_sums[1:2],
                             conv_sums[2:3], conv_sums[3:4], conv_sums[4:5],
                             conv_sums[1:2], conv_sums[0:1], attn_sums[0:1]], axis=1)
    (small_all,) = _all_gather([jnp.broadcast_to(small, (8, 8 * D))], "gather_small")
    cb_send, cb_recv, sum_a, land_gi, cb_token = _chip_start("g_attn_chip_start_b", sum_a, land_gi, 4, (half, half),
                                                             after=[small_all])
    small_all = small_all[:, 0, :]
    tot = _sum_rows(small_all, cb_token)[0:1]
    loss = lax.psum(post_sums[2, 0], ("x", "y", "c"))

    g_b_ada = tot[:, :3 * D]
    g_g_pre, g_g_post = tot[:, 3 * D:4 * D], tot[:, 4 * D:5 * D]
    g_conv_w_full = tot[:, 5 * D:5 * D + 3 * C].reshape(3, C)
    g_conv_w = lax.dynamic_slice_in_dim(g_conv_w_full, me * CW, CW, axis=1)[None]
    g_conv_b, g_g_conv, g_g_attn = (tot[:, 5 * D + (3 + t) * C:5 * D + (4 + t) * C] for t in range(3))

    dmod_cols = lax.dynamic_slice_in_dim(small_all[:, :3 * D], me * W, W, axis=1)
    g_w_ada, d_w_ada, nm_w_ada, nv_w_ada = _adamw_ada(c_all.T, dmod_cols, w_ada2, m_w_ada[0], v_w_ada[0], cb_token)

    sum_o, land_go = _chip_wait("g_out_chip_wait", sum_o, land_go, co_send, co_recv, 0, after=[g_w_ada])
    own_out = lax.dynamic_index_in_dim(sum_o, me // 2, 0, keepdims=False)
    g_w_out, d_w_out, nm_w_out, nv_w_out = _adamw_sharded(land_go, own_out, w_out2, m_w_out[0], v_w_out[0], "adamw_w_out")
    sum_c, land_gi = _chip_wait("g_conv_chip_wait", sum_c, land_gi, cc_send, cc_recv, 0, after=[g_w_out])
    sum_a, land_gi = _chip_wait("g_attn_chip_wait_a", sum_a, land_gi, ca_send, ca_recv, 4, [g_w_out], (0, half))
    sum_a, land_gi = _chip_wait("g_attn_chip_wait_b", sum_a, land_gi, cb_send, cb_recv, 4, [g_w_out], (half, half))
    own_in = jnp.where(me < 4, lax.dynamic_index_in_dim(sum_c, (me % 4) // 2, 0, keepdims=False),
                       lax.dynamic_index_in_dim(sum_a, (me % 4) // 2, 0, keepdims=False))
    g_w_in, d_w_in, nm_w_in, nv_w_in = _adamw_sharded(land_gi, own_in, w_in2, m_w_in[0], v_w_in[0], "adamw_w_in")

    pack = lambda *vs: jnp.concatenate([a.reshape(1, -1) for a in vs], axis=1)
    smalls = [(b_ada, g_b_ada, m_b_ada, v_b_ada), (g_pre, g_g_pre, m_g_pre, v_g_pre),
              (conv_w, g_conv_w, m_conv_w, v_conv_w), (conv_b, g_conv_b, m_conv_b, v_conv_b),
              (g_conv, g_g_conv, m_g_conv, v_g_conv), (g_attn, g_g_attn, m_g_attn, v_g_attn),
              (g_post, g_g_post, m_g_post, v_g_post)]
    packed = [pack(*[s[t] for s in smalls]) for t in range(4)]
    npad = -packed[0].shape[1] % 128
    packed = [jnp.pad(p, ((0, 0), (0, npad)), constant_values=1.0) for p in packed]
    d_s, nm_s, nv_s = _adamw_small(*packed)

    def unpack(vec):
        out, at = [], 0
        for s in smalls:
            n = s[0].size
            out.append(vec[:, at:at + n].reshape(s[0].shape))
            at += n
        return out

    d_b_ada, d_g_pre, d_conv_w, d_conv_b, d_g_conv, d_g_attn, d_g_post = unpack(d_s)
    nm_b_ada, nm_g_pre, nm_conv_w, nm_conv_b, nm_g_conv, nm_g_attn, nm_g_post = unpack(nm_s)
    nv_b_ada, nv_g_pre, nv_conv_w, nv_conv_b, nv_g_conv, nv_g_attn, nv_g_post = unpack(nv_s)

    return (loss, grad_x[None],
            g_w_ada[None], g_b_ada, g_g_pre, g_w_in[None], g_conv_w, g_conv_b, g_g_conv, g_g_attn, g_w_out[None], g_g_post,
            d_w_ada[None], d_b_ada, d_g_pre, d_w_in[None], d_conv_w, d_conv_b, d_g_conv, d_g_attn, d_w_out[None], d_g_post,
            nm_w_ada[None], nm_b_ada, nm_g_pre, nm_w_in[None], nm_conv_w, nm_conv_b, nm_g_conv, nm_g_attn, nm_w_out[None], nm_g_post,
            nv_w_ada[None], nv_b_ada, nv_g_pre, nv_w_in[None], nv_conv_w, nv_conv_b, nv_g_conv, nv_g_attn, nv_w_out[None], nv_g_post)
```

```python
import functools

import jax
import jax.numpy as jnp
from jax import lax
from jax.experimental import pallas as pl
from jax.experimental.pallas import tpu as pltpu

f32 = jnp.float32
bf16 = jnp.bfloat16

NDEV = 8
HEAD_DIM = 64
PAIR = 2 * HEAD_DIM
BRANCHES = ((128, 1), (512, 4), (2048, 16))
HALF_WIN = 64
EPS = 1e-6
NEG_INF = -1e30
ADAM_LR, ADAM_B1, ADAM_B2, ADAM_EPS, ADAM_WD, ADAM_STEP = 0.001, 0.9, 0.999, 1e-08, 0.01, 10
MESH = pl.DeviceIdType.MESH
VMEM_LIMIT = 56 * 1024 * 1024
HBM_SPEC = pl.BlockSpec(memory_space=pltpu.HBM)
ANY_SPEC = pl.BlockSpec(memory_space=pl.ANY)
SEM_SPEC = pl.BlockSpec(memory_space=pltpu.SEMAPHORE)


def _params(*sem):
    return pltpu.CompilerParams(dimension_semantics=sem or None, vmem_limit_bytes=VMEM_LIMIT)


def _silu(z):
    return z * jax.nn.sigmoid(z)


def _dsilu(z):
    s = jax.nn.sigmoid(z)
    return s * (1.0 + z * (1.0 - s))


def _my_place():
    x, y, c = lax.axis_index("x"), lax.axis_index("y"), lax.axis_index("c")
    return x, y, c, 4 * x + 2 * y + c


def _peer(x, y, c, k):
    px, py, pc = x ^ (k >> 2 & 1), y ^ (k >> 1 & 1), c ^ (k & 1)
    return (px, py, pc), 4 * px + 2 * py + pc


def _all_gather(arrays, name):
    n = len(arrays)

    def body(*refs):
        srcs, dsts = refs[:n], refs[n:2 * n]
        send_sems, recv_sems, local_sems = refs[2 * n:]
        x, y, c, me = _my_place()
        locals_, sends = [], []
        for t in range(n):
            own = pltpu.make_async_copy(srcs[t], dsts[t].at[me], local_sems.at[t])
            own.start()
            locals_.append(own)
            for k in range(1, NDEV):
                peer, pidx = _peer(x, y, c, k)
                cp = pltpu.make_async_remote_copy(
                    src_ref=srcs[t], dst_ref=dsts[t].at[me], send_sem=send_sems.at[t, k],
                    recv_sem=recv_sems.at[t, k], device_id=peer, device_id_type=MESH)
                cp.start()
                sends.append(cp)
        for t in range(n):
            for k in range(1, NDEV):
                peer, pidx = _peer(x, y, c, k)
                pltpu.make_async_remote_copy(
                    src_ref=srcs[t], dst_ref=dsts[t].at[pidx], send_sem=send_sems.at[t, k],
                    recv_sem=recv_sems.at[t, k], device_id=peer, device_id_type=MESH).wait_recv()
        for cp in sends:
            cp.wait_send()
        for cp in locals_:
            cp.wait()

    return pl.pallas_call(
        body, name=name,
        out_shape=tuple(jax.ShapeDtypeStruct((NDEV,) + a.shape, a.dtype) for a in arrays),
        in_specs=[HBM_SPEC] * n, out_specs=tuple([HBM_SPEC] * n),
        scratch_shapes=[pltpu.SemaphoreType.DMA((n, NDEV)), pltpu.SemaphoreType.DMA((n, NDEV)),
                        pltpu.SemaphoreType.DMA((n,))],
    )(*arrays)


def _all_to_all(arrays, name):
    n = len(arrays)

    def body(*refs):
        srcs, dsts = refs[:n], refs[n:2 * n]
        send_sems, recv_sems, local_sems = refs[2 * n:]
        x, y, c, me = _my_place()
        locals_, sends = [], []
        for t in range(n):
            own = pltpu.make_async_copy(srcs[t].at[me], dsts[t].at[me], local_sems.at[t])
            own.start()
            locals_.append(own)
            for k in range(1, NDEV):
                peer, pidx = _peer(x, y, c, k)
                cp = pltpu.make_async_remote_copy(
                    src_ref=srcs[t].at[pidx], dst_ref=dsts[t].at[me], send_sem=send_sems.at[t, k],
                    recv_sem=recv_sems.at[t, k], device_id=peer, device_id_type=MESH)
                cp.start()
                sends.append(cp)
        for t in range(n):
            for k in range(1, NDEV):
                peer, pidx = _peer(x, y, c, k)
                pltpu.make_async_remote_copy(
                    src_ref=srcs[t].at[pidx], dst_ref=dsts[t].at[pidx], send_sem=send_sems.at[t, k],
                    recv_sem=recv_sems.at[t, k], device_id=peer, device_id_type=MESH).wait_recv()
        for cp in sends:
            cp.wait_send()
        for cp in locals_:
            cp.wait()

    return pl.pallas_call(
        body, name=name,
        out_shape=tuple(jax.ShapeDtypeStruct(a.shape, a.dtype) for a in arrays),
        in_specs=[HBM_SPEC] * n, out_specs=tuple([HBM_SPEC] * n),
        scratch_shapes=[pltpu.SemaphoreType.DMA((n, NDEV)), pltpu.SemaphoreType.DMA((n, NDEV)),
                        pltpu.SemaphoreType.DMA((n,))],
    )(*arrays)


def _comm_call(name, arrays, sems, new_sems, body, after=(), token=False):
    na, ns, nn, nf = len(arrays), len(sems), len(new_sems), len(after)

    def kern(*refs):
        ins, outs = refs[:na + ns + nf], refs[na + ns + nf:]
        body(ins[:na], ins[na:na + ns], outs[:nn])
        if token:
            outs[nn + na][...] = jnp.zeros((8, 128), f32)

    out_shape = ([pltpu.SemaphoreType.DMA(s) for s in new_sems] + [pltpu.HBM(a.shape, a.dtype) for a in arrays]
                 + ([jax.ShapeDtypeStruct((8, 128), f32)] if token else []))
    out_specs = [SEM_SPEC] * nn + [HBM_SPEC] * na + ([pl.BlockSpec(memory_space=pltpu.VMEM)] if token else [])
    res = pl.pallas_call(
        kern, name=name, out_shape=tuple(out_shape),
        in_specs=[HBM_SPEC] * na + [SEM_SPEC] * ns + [ANY_SPEC] * nf, out_specs=tuple(out_specs),
        input_output_aliases={t: nn + t for t in range(na)},
        compiler_params=pltpu.CompilerParams(has_side_effects=pltpu.SideEffectType.DATAFLOW_SIDE_EFFECTING),
    )(*[pltpu.with_memory_space_constraint(a, pltpu.HBM) for a in arrays], *sems, *after)
    return list(res[:nn]), list(res[nn:nn + na]), (res[nn + na] if token else None)


def _remote(src, dst, send_sem, recv_sem, device):
    return pltpu.make_async_remote_copy(src_ref=src, dst_ref=dst, send_sem=send_sem, recv_sem=recv_sem,
                                        device_id=device, device_id_type=MESH)


SAME_CORE = (2, 4, 6)
VIA_SIBLING = (3, 5, 7)


def _weights_start(lands, after):
    n = len(lands)

    def body(a, s, new):
        x, y, c, me = _my_place()
        for t, land in enumerate(a):
            send, recv = new[t], new[n + t]
            for k in (1,) + SAME_CORE:
                peer, _ = _peer(x, y, c, k)
                _remote(land.at[me], land.at[me], send.at[k], recv.at[k], peer).start()

    sems, lands, token = _comm_call("weights_start", lands, [], [(NDEV,)] * (2 * n), body, after=after, token=True)
    return sems[:n], sems[n:], lands, token


def _weights_forward(name, land, recv, after):
    def body(a, s, new):
        (land,), (recv,), (fsend, frecv) = a, s, new
        x, y, c, me = _my_place()
        sibling, _ = _peer(x, y, c, 1)
        for k in SAME_CORE:
            peer, slot = _peer(x, y, c, k)
            _remote(land.at[slot], land.at[slot], fsend.at[k], recv.at[k], peer).wait_recv()
            _remote(land.at[slot], land.at[slot], fsend.at[k], frecv.at[k ^ 1], sibling).start()

    return _comm_call(name, [land], [recv], [(NDEV,), (NDEV,)], body, after=after)


def _weights_wait(name, land, send, recv, fsend, frecv, after):
    def body(a, s, new):
        (land,), (send, recv, fsend, frecv) = a, s
        x, y, c, me = _my_place()
        sibling, sib_slot = _peer(x, y, c, 1)
        _remote(land.at[sib_slot], land.at[sib_slot], send.at[1], recv.at[1], sibling).wait_recv()
        for k in VIA_SIBLING:
            _, slot = _peer(x, y, c, k)
            _remote(land.at[slot], land.at[slot], fsend.at[k ^ 1], frecv.at[k], sibling).wait_recv()
        for k in (1,) + SAME_CORE:
            peer, _ = _peer(x, y, c, k)
            _remote(land.at[me], land.at[me], send.at[k], recv.at[k], peer).wait_send()
        for k in SAME_CORE:
            _, slot = _peer(x, y, c, k)
            _remote(land.at[slot], land.at[slot], fsend.at[k], frecv.at[k ^ 1], sibling).wait_send()

    return _comm_call(name, [land], [send, recv, fsend, frecv], [], body, after=after)[1][0]


NCHIP = NDEV // 2


def _pair_start(name, src):
    npair = src.shape[0] // 2

    def body(a, s, new):
        (src, pair), (send, recv) = a, new
        x, y, c, me = _my_place()
        sibling, _ = _peer(x, y, c, 1)
        for i in range(npair):
            _remote(src.at[2 * i + 1 - c], pair.at[i], send.at[i], recv.at[i], sibling).start()

    pair = lax.empty((npair,) + src.shape[1:], src.dtype)
    (send, recv), (src, pair), token = _comm_call(name, [src, pair], [], [(npair,), (npair,)], body, token=True)
    return send, recv, src, pair, token


def _pair_wait(name, src, pair, send, recv, after):
    npair = pair.shape[0]

    def body(a, s, new):
        (src, pair), (send, recv) = a, s
        x, y, c, me = _my_place()
        sibling, _ = _peer(x, y, c, 1)
        for i in range(npair):
            cp = _remote(src.at[2 * i + 1 - c], pair.at[i], send.at[i], recv.at[i], sibling)
            cp.wait_recv()
            cp.wait_send()

    return _comm_call(name, [src, pair], [send, recv], [], body, after=after)[1]


def _pair_sum(name, src, pair, core, tr=256):
    npair, R, Cc = pair.shape
    tr = min(tr, R)

    def body(core_ref, a_ref, b_ref, o_ref):
        o_ref[...] = (a_ref[...].astype(f32) + b_ref[...].astype(f32)).astype(o_ref.dtype)

    grid_spec = pltpu.PrefetchScalarGridSpec(
        num_scalar_prefetch=1, grid=(npair, R // tr),
        in_specs=[pl.BlockSpec((None, tr, Cc), lambda i, r, core: (2 * i + core[0], r, 0)),
                  pl.BlockSpec((None, tr, Cc), lambda i, r, core: (i, r, 0))],
        out_specs=pl.BlockSpec((None, tr, Cc), lambda i, r, core: (i, r, 0)))
    return pl.pallas_call(body, name=name, out_shape=jax.ShapeDtypeStruct(pair.shape, pair.dtype),
                          grid_spec=grid_spec, compiler_params=_params("parallel", "parallel"))(core, src, pair)


def _owner_chip(first, i):
    q = first // 2 + i
    return q >> 1 & 1, q & 1


def _chip_start(name, sums, land, first, rows=None, after=()):
    npair = sums.shape[0]
    rows = pl.ds(*(rows or (0, sums.shape[1])))

    def body(a, s, new):
        (sums, land), (send, recv) = a, new
        x, y, c, me = _my_place()
        for i in range(npair):
            ox, oy = _owner_chip(first, i)

            @pl.when((x != ox) | (y != oy))
            def _():
                _remote(sums.at[i, rows], land.at[2 * x + y, rows], send.at[i], recv.at[2 * x + y], (ox, oy, c)).start()

    (send, recv), (sums, land), token = _comm_call(name, [sums, land], [], [(npair,), (NCHIP,)], body, after=after,
                                                   token=True)
    return send, recv, sums, land, token


def _chip_wait(name, sums, land, send, recv, first, after, rows=None):
    npair = sums.shape[0]
    rows = pl.ds(*(rows or (0, sums.shape[1])))

    def body(a, s, new):
        (sums, land), (send, recv) = a, s
        x, y, c, me = _my_place()
        mine = (me >= first) & (me < first + 2 * npair)
        for i in range(npair):
            ox, oy = _owner_chip(first, i)

            @pl.when((x != ox) | (y != oy))
            def _():
                _remote(sums.at[i, rows], land.at[2 * x + y, rows], send.at[i], recv.at[2 * x + y], (ox, oy, c)).wait_send()
        for q in range(NCHIP):
            @pl.when(mine & (2 * x + y != q))
            def _():
                _remote(sums.at[0, rows], land.at[q, rows], send.at[0], recv.at[q], (q >> 1, q & 1, c)).wait_recv()

    return _comm_call(name, [sums, land], [send, recv], [], body, after=after)[1]


def _matmul(a, b, *, name, out_dtype, ta=False, tb=False, b_slots=False, out_slots=0, b_cols=None,
            tm=1024, tn=1024, tk=2048, dep=None):
    M, K = (a.shape[1], a.shape[0]) if ta else a.shape
    col0 = 0
    if b_slots:
        slab = b.shape[2]
        N = b.shape[1] if tb else NDEV * slab
        assert (K if tb else N) == NDEV * slab
    elif b_cols is not None:
        assert not tb
        col0, N = b_cols
    else:
        N = b.shape[0] if tb else b.shape[1]
    tm, tn, tk = min(tm, M), min(tn, N), min(tk, K)
    if b_slots:
        if tb:
            tk = min(tk, slab)
        else:
            tn = min(tn, slab)
    if out_slots:
        tn = min(tn, N // out_slots)
    nm, nn, nk = M // tm, N // tn, K // tk
    assert (nm * tm, nn * tn, nk * tk) == (M, N, K) and col0 % tn == 0, (name, M, N, K, tm, tn, tk)
    j0 = col0 // tn

    a_spec = pl.BlockSpec((tk, tm), lambda i, j, k: (k, i)) if ta else pl.BlockSpec((tm, tk), lambda i, j, k: (i, k))
    if b_slots and tb:
        per = slab // tk
        b_spec = pl.BlockSpec((None, tn, tk), lambda i, j, k: (k // per, j, k % per))
    elif b_slots:
        per = slab // tn
        b_spec = pl.BlockSpec((None, tk, tn), lambda i, j, k: (j // per, k, j % per))
    elif tb:
        b_spec = pl.BlockSpec((tn, tk), lambda i, j, k: (j, k))
    else:
        b_spec = pl.BlockSpec((tk, tn), lambda i, j, k: (k, j + j0))
    if out_slots:
        per_o = (N // out_slots) // tn
        o_spec = pl.BlockSpec((None, tm, tn), lambda i, j, k: (j // per_o, i, j % per_o))
        out_shape = jax.ShapeDtypeStruct((out_slots, M, N // out_slots), out_dtype)
    else:
        o_spec = pl.BlockSpec((tm, tn), lambda i, j, k: (i, j))
        out_shape = jax.ShapeDtypeStruct((M, N), out_dtype)
    dims = (((0 if ta else 1,), (1 if tb else 0,)), ((), ()))
    deps = [] if dep is None else [dep]

    def body(a_ref, b_ref, *rest):
        o_ref = rest[len(deps)]
        prod = lax.dot_general(a_ref[...], b_ref[...], dims, preferred_element_type=f32)
        if nk == 1:
            o_ref[...] = prod.astype(out_dtype)
            return
        acc_ref = rest[len(deps) + 1]
        k = pl.program_id(2)

        @pl.when(k == 0)
        def _():
            acc_ref[...] = prod

        @pl.when((k > 0) & (k < nk - 1))
        def _():
            acc_ref[...] += prod

        @pl.when(k == nk - 1)
        def _():
            o_ref[...] = (acc_ref[...] + prod).astype(out_dtype)

    return pl.pallas_call(
        body, name=name, out_shape=out_shape, grid=(nm, nn, nk),
        in_specs=[a_spec, b_spec] + [ANY_SPEC] * len(deps), out_specs=o_spec,
        scratch_shapes=[pltpu.VMEM((tm, tn), f32)] if nk > 1 else [],
        compiler_params=_params("parallel", "parallel", "arbitrary"),
    )(a, b, *deps)


def _matmul_slabs_t(a, b, *, name, tm=512, tn=512, dep=None):
    M, K = a.shape
    n_slab, N, slab = b.shape
    assert K == n_slab * slab
    tm, tn = min(tm, M), min(tn, N)
    deps = [] if dep is None else [dep]

    def body(a_ref, b_ref, *rest):
        o_ref = rest[len(deps)]
        acc = None
        for s in range(n_slab):
            prod = lax.dot_general(a_ref[:, s * slab:(s + 1) * slab], b_ref[s], (((1,), (1,)), ((), ())),
                                   preferred_element_type=f32)
            acc = prod if acc is None else acc + prod
        o_ref[...] = acc

    return pl.pallas_call(
        body, name=name, out_shape=jax.ShapeDtypeStruct((M, N), f32), grid=(M // tm, N // tn),
        in_specs=[pl.BlockSpec((tm, K), lambda i, j: (i, 0)), pl.BlockSpec((n_slab, tn, slab), lambda i, j: (0, j, 0))]
        + [ANY_SPEC] * len(deps),
        out_specs=pl.BlockSpec((tm, tn), lambda i, j: (i, j)), compiler_params=_params("parallel", "parallel"),
    )(a, b, *deps)


def _silu_block(c):
    def body(c_ref, o_ref):
        o_ref[...] = _silu(c_ref[...])

    return pl.pallas_call(body, name="silu_c", out_shape=jax.ShapeDtypeStruct(c.shape, f32))(c)


def _ada_cols(c_all, w_ada, b_cols):
    D, W = w_ada.shape

    def body(c_ref, w_ref, b_ref, o_ref):
        mod = lax.dot_general(c_ref[...], w_ref[...], (((1,), (0,)), ((), ())), preferred_element_type=f32,
                              precision=lax.Precision.HIGHEST) + b_ref[...]
        for j in range(NDEV):
            o_ref[j] = jnp.broadcast_to(mod[j:j + 1, :], (8, W))

    return pl.pallas_call(body, name="ada_cols", out_shape=jax.ShapeDtypeStruct((NDEV, 8, W), f32),
                          compiler_params=_params())(c_all, w_ada, b_cols)


def _prenorm(x, scale, shift, g_pre, dep, tr=256):
    S, D = x.shape
    tr = min(tr, S)

    def body(x_ref, sc_ref, sh_ref, g_ref, dep_ref, h_ref):
        xv = x_ref[...]
        r = lax.rsqrt(jnp.mean(xv * xv, axis=-1, keepdims=True) + EPS)
        h_ref[...] = ((xv * r) * g_ref[...] * (1.0 + sc_ref[...]) + sh_ref[...]).astype(bf16)

    row = pl.BlockSpec((tr, D), lambda i: (i, 0))
    vec = pl.BlockSpec((1, D), lambda i: (0, 0))
    return pl.pallas_call(body, name="prenorm", out_shape=jax.ShapeDtypeStruct((S, D), bf16), grid=(S // tr,),
                          in_specs=[row, vec, vec, vec, ANY_SPEC], out_specs=row, compiler_params=_params("parallel"))(
                              x, scale, shift, g_pre, dep)


def _ext_rows(i, tr, S):
    g = lax.broadcasted_iota(jnp.int32, (tr + 16, 1), 0) + (i * tr - 8)
    return (g >= 0) & (g < S)


def _halo_specs(tr, S, C, col):
    nb8 = S // 8
    main = pl.BlockSpec((tr, C), lambda i: (i, col))
    prev = pl.BlockSpec((8, C), lambda i: (jnp.maximum(i * (tr // 8) - 1, 0), col))
    nxt = pl.BlockSpec((8, C), lambda i: (jnp.minimum((i + 1) * (tr // 8), nb8 - 1), col))
    return prev, main, nxt


def _conv_fwd(proj, conv_w, conv_b, g_conv, tr=256):
    S, C = proj.shape[0], proj.shape[1] // 8
    tr = min(tr, S)

    def body(up, um, un, cp, cm, cn, bg_ref, zc_ref, w_ref, cb_ref, g_ref, o_ref):
        i = pl.program_id(0)
        exists = _ext_rows(i, tr, S)
        u = jnp.concatenate([up[...], um[...], un[...]], axis=0)
        cg = jnp.concatenate([cp[...], cm[...], cn[...]], axis=0)
        t = jnp.where(exists, cg * u, 0.0)
        t_before = pltpu.roll(t, 1, 0)[8:tr + 8]
        t_after = pltpu.roll(t, tr + 15, 0)[8:tr + 8]
        w = w_ref[...]
        cv = w[0:1] * t_before + w[1:2] * t[8:tr + 8] + w[2:3] * t_after + cb_ref[...]
        yc = bg_ref[...] * cv
        rc = lax.rsqrt(jnp.mean(yc * yc, axis=-1, keepdims=True) + EPS)
        o_ref[...] = ((yc * rc) * g_ref[...] * _silu(zc_ref[...])).astype(bf16)

    u_specs = _halo_specs(tr, S, C, 0)
    c_specs = _halo_specs(tr, S, C, 2)
    vec = pl.BlockSpec((1, C), lambda i: (0, 0))
    return pl.pallas_call(
        body, name="conv_fwd", out_shape=jax.ShapeDtypeStruct((S, 2 * C), bf16), grid=(S // tr,),
        in_specs=[*u_specs, *c_specs, pl.BlockSpec((tr, C), lambda i: (i, 1)), pl.BlockSpec((tr, C), lambda i: (i, 3)),
                  pl.BlockSpec((8, C), lambda i: (0, 0)), vec, vec],
        out_specs=pl.BlockSpec((tr, C), lambda i: (i, 0)), compiler_params=_params("parallel"),
    )(proj, proj, proj, proj, proj, proj, proj, proj, conv_w, conv_b, g_conv)


def _branch_geometry(S, r):
    L = S // r
    nq = min(128, L)
    nk = min(nq + 2 * HALF_WIN, L)
    return L, nq, nk, L // nq


def _block_rows(idx, r, L, nq, nk, nblk):
    if r == 1:
        rho, qb = 0, idx
    else:
        rho, qb = idx // nblk, idx % nblk
    i0 = qb * nq
    ws = jnp.clip(i0 - HALF_WIN, 0, L - nk)
    if r == 1:
        qrows = pl.ds(pl.multiple_of(i0, 8), nq)
        krows = pl.ds(pl.multiple_of(ws, 8), nk)
    else:
        qrows = pl.ds(rho + r * i0, nq, stride=r)
        krows = pl.ds(rho + r * ws, nk, stride=r)
    return qrows, krows, i0 - ws


N_CASES = 3
SCALE = HEAD_DIM ** -0.5
ATTN_UNROLL = 8


def _bias_shape(S):
    return (len(BRANCHES) * N_CASES * 2, min(128, S), min(128 + 2 * HALF_WIN, S))


def _bias_index(b, case, head):
    return (b * N_CASES + case) * 2 + head


def _fill_bias(bias_scr, sl_ref, S):
    sl = sl_ref[...]
    slope = (sl[0:1, 0:1], sl[0:1, HEAD_DIM:HEAD_DIM + 1])
    for b, (_, r) in enumerate(BRANCHES):
        L, nq, nk, nblk = _branch_geometry(S, r)
        assert nblk == 1 or L >= nq + 2 * HALF_WIN
        rel = lax.broadcasted_iota(jnp.int32, (nq, nk), 0) - lax.broadcasted_iota(jnp.int32, (nq, nk), 1)
        for case in range(N_CASES):
            d = jnp.abs(rel + case * HALF_WIN)
            dist = d.astype(f32) * float(r)
            for head in range(2):
                bias_scr[_bias_index(b, case, head), 0:nq, 0:nk] = jnp.where(d <= HALF_WIN, -slope[head] * dist, NEG_INF)


def _head_slopes(n_heads):
    slopes = 2.0 ** (-8.0 * jnp.arange(1, n_heads + 1, dtype=f32) / n_heads)
    return jnp.broadcast_to(jnp.repeat(slopes.reshape(n_heads // 2, 2), HEAD_DIM, axis=1)[:, None, :],
                            (n_heads // 2, 8, PAIR))


def _attn_fwd(proj, slopes):
    S, C = proj.shape[0], proj.shape[1] // 8
    npair = C // PAIR

    def body(q_ref, k_ref, v_ref, sl_ref, o_ref, lse_ref, m_scr, l_scr, a_scr, bias_scr):
        lane = lax.broadcasted_iota(jnp.int32, (1, PAIR), 1)
        first = lane < HEAD_DIM
        _fill_bias(bias_scr, sl_ref, S)

        for b, (_, r) in enumerate(BRANCHES):
            L, nq, nk, nblk = _branch_geometry(S, r)

            def step(idx, carry, b=b, r=r, L=L, nq=nq, nk=nk, nblk=nblk):
                qrows, krows, off = _block_rows(idx, r, L, nq, nk, nblk)
                case = off // HALF_WIN
                q2 = q_ref[qrows, :] * SCALE
                k2 = k_ref[krows, :].astype(bf16)
                v2 = v_ref[krows, :].astype(bf16)
                ms, accs = [], []
                for hh in range(2):
                    mine = first if hh == 0 else ~first
                    qh = jnp.where(mine, q2, 0.0).astype(bf16)
                    s = lax.dot_general(qh, k2, (((1,), (1,)), ((), ())), preferred_element_type=f32)
                    s = s + bias_scr[_bias_index(b, case, hh), 0:nq, 0:nk]
                    m = jnp.max(s, axis=-1, keepdims=True)
                    p = jnp.exp(s - m).astype(bf16)
                    vh = jnp.where(mine, v2, jnp.ones_like(v2))
                    ms.append(m)
                    accs.append(jnp.dot(p, vh, preferred_element_type=f32))
                m_scr[b, qrows, :] = jnp.where(first, ms[0], ms[1])
                a_scr[b, qrows, :] = jnp.where(first, accs[0], accs[1])
                l_scr[b, qrows, :] = jnp.where(first, accs[1], accs[0])
                return carry

            lax.fori_loop(0, S // nq, step, 0, unroll=min(ATTN_UNROLL, S // nq))

        ch = min(256, S)

        def merge(i, carry):
            rows = pl.ds(pl.multiple_of(i * ch, 8), ch)
            m = jnp.maximum(jnp.maximum(m_scr[0, rows, :], m_scr[1, rows, :]), m_scr[2, rows, :])
            l = jnp.zeros((ch, PAIR), f32)
            acc = jnp.zeros((ch, PAIR), f32)
            for b in range(3):
                w = jnp.exp(m_scr[b, rows, :] - m)
                l = l + w * pltpu.roll(l_scr[b, rows, :], HEAD_DIM, 1)
                acc = acc + w * a_scr[b, rows, :]
            o_ref[rows, :] = acc / l
            lse_ref[rows, :] = m + jnp.log(l)
            return carry

        lax.fori_loop(0, S // ch, merge, 0)

    blk = lambda part: pl.BlockSpec((S, PAIR), lambda p: (0, part * npair + p))
    out = pl.BlockSpec((S, PAIR), lambda p: (0, p))
    return pl.pallas_call(
        body, name="attn_fwd",
        out_shape=(jax.ShapeDtypeStruct((S, C), f32), jax.ShapeDtypeStruct((S, C), f32)), grid=(npair,),
        in_specs=[blk(4), blk(5), blk(6), pl.BlockSpec((None, 8, PAIR), lambda p: (p, 0, 0))],
        out_specs=(out, out),
        scratch_shapes=[pltpu.VMEM((3, S, PAIR), f32)] * 3 + [pltpu.VMEM(_bias_shape(S), f32)],
        compiler_params=_params("parallel"),
    )(proj, proj, proj, slopes)


def _attn_post(ycat, o, proj, g_attn, tr=256):
    S, C = o.shape
    tr = min(tr, S)

    def body(y_ref, o_ref, z_ref, g_ref, out_ref):
        del y_ref
        ov = o_ref[...]
        ra = lax.rsqrt(jnp.mean(ov * ov, axis=-1, keepdims=True) + EPS)
        out_ref[...] = ((ov * ra) * g_ref[...] * _silu(z_ref[...])).astype(bf16)

    return pl.pallas_call(
        body, name="attn_post", out_shape=jax.ShapeDtypeStruct(ycat.shape, ycat.dtype), grid=(S // tr,),
        in_specs=[HBM_SPEC, pl.BlockSpec((tr, C), lambda i: (i, 0)), pl.BlockSpec((tr, C), lambda i: (i, 7)),
                  pl.BlockSpec((1, C), lambda i: (0, 0))],
        out_specs=pl.BlockSpec((tr, C), lambda i: (i, 1)), input_output_aliases={0: 0},
        compiler_params=_params("arbitrary"),
    )(ycat, o, proj, g_attn)


def _sandwich(y, x, target, gate, g_post, tr=256):
    S, D = y.shape
    tr = min(tr, S)

    def body(y_ref, x_ref, t_ref, gate_ref, g_ref, dy_ref, dout_ref, sums_ref):
        i = pl.program_id(0)
        yv = y_ref[...]
        rp = lax.rsqrt(jnp.mean(yv * yv, axis=-1, keepdims=True) + EPS)
        yhat = yv * rp
        yn = yhat * g_ref[...]
        err = (x_ref[...] + gate_ref[...] * yn) - t_ref[...]
        dout = err * (1.0 / D)
        dout_ref[...] = dout
        dyn = dout * gate_ref[...]
        w = dyn * g_ref[...]
        dy_ref[...] = (rp * (w - yhat * jnp.mean(w * yhat, axis=-1, keepdims=True))).astype(bf16)
        loss = 0.5 * jnp.sum(jnp.mean(err * err, axis=-1, keepdims=True), axis=0, keepdims=True)
        row = lax.broadcasted_iota(jnp.int32, (8, D), 0)
        upd = jnp.where(row == 0, jnp.sum(dout * yn, axis=0, keepdims=True),
                        jnp.where(row == 1, jnp.sum(dyn * yhat, axis=0, keepdims=True),
                                  jnp.where(row == 2, loss, 0.0)))

        @pl.when(i == 0)
        def _():
            sums_ref[...] = upd

        @pl.when(i > 0)
        def _():
            sums_ref[...] += upd

    row = pl.BlockSpec((tr, D), lambda i: (i, 0))
    vec = pl.BlockSpec((1, D), lambda i: (0, 0))
    return pl.pallas_call(
        body, name="sandwich",
        out_shape=(jax.ShapeDtypeStruct((S, D), bf16), jax.ShapeDtypeStruct((S, D), f32), jax.ShapeDtypeStruct((8, D), f32)),
        grid=(S // tr,), in_specs=[row, row, row, vec, vec],
        out_specs=(row, row, pl.BlockSpec((8, D), lambda i: (0, 0))), compiler_params=_params("arbitrary"),
    )(y, x, target, gate, g_post)


def _conv_bwd(proj, dycat, conv_w, conv_b, g_conv, dep, tr=256):
    S, C = proj.shape[0], proj.shape[1] // 8
    tr = min(tr, S)
    n = tr + 16

    def body(*refs):
        ins, (w_ref, cb_ref, g_ref, _, dp_ref, sums_ref) = refs[:15], refs[15:]
        i = pl.program_id(0)
        exists = _ext_rows(i, tr, S)
        u, bg, cg, zc, dyn = (jnp.concatenate([ins[3 * t][...], ins[3 * t + 1][...], ins[3 * t + 2][...]], axis=0)
                              for t in range(5))
        w = w_ref[...]
        t = jnp.where(exists, cg * u, 0.0)
        t_before, t_after = pltpu.roll(t, 1, 0), pltpu.roll(t, n - 1, 0)
        cv = w[0:1] * t_before + w[1:2] * t + w[2:3] * t_after + cb_ref[...]
        yc = bg * cv
        rc = lax.rsqrt(jnp.mean(yc * yc, axis=-1, keepdims=True) + EPS)
        yhat = yc * rc
        sz = _silu(zc)
        wgt = dyn * g_ref[...] * sz
        dyc = rc * (wgt - yhat * jnp.mean(wgt * yhat, axis=-1, keepdims=True))
        dcv = jnp.where(exists, dyc * bg, 0.0)
        dt = w[0:1] * pltpu.roll(dcv, n - 1, 0) + w[1:2] * dcv + w[2:3] * pltpu.roll(dcv, 1, 0)
        mid = slice(8, tr + 8)
        dp_ref[:, 0:C] = (dt * cg)[mid].astype(bf16)
        dp_ref[:, C:2 * C] = (dyc * cv)[mid].astype(bf16)
        dp_ref[:, 2 * C:3 * C] = (dt * u)[mid].astype(bf16)
        dp_ref[:, 3 * C:4 * C] = (dyn * yhat * g_ref[...] * _dsilu(zc))[mid].astype(bf16)
        colsum = lambda v: jnp.sum(v[mid], axis=0, keepdims=True)
        parts = [colsum(dyn * yhat * sz), colsum(dcv), colsum(dcv * t_before), colsum(dcv * t), colsum(dcv * t_after)]
        row = lax.broadcasted_iota(jnp.int32, (8, C), 0)
        upd = jnp.zeros((8, C), f32)
        for j, pj in enumerate(parts):
            upd = jnp.where(row == j, pj, upd)

        @pl.when(i == 0)
        def _():
            sums_ref[...] = upd

        @pl.when(i > 0)
        def _():
            sums_ref[...] += upd

    specs = []
    for col in range(4):
        specs += _halo_specs(tr, S, C, col)
    specs += _halo_specs(tr, S, C, 0)
    vec = pl.BlockSpec((1, C), lambda i: (0, 0))
    return pl.pallas_call(
        body, name="conv_bwd",
        out_shape=(jax.ShapeDtypeStruct((S, 4 * C), bf16), jax.ShapeDtypeStruct((8, C), f32)), grid=(S // tr,),
        in_specs=[*specs, pl.BlockSpec((8, C), lambda i: (0, 0)), vec, vec, ANY_SPEC],
        out_specs=(pl.BlockSpec((tr, 4 * C), lambda i: (i, 0)), pl.BlockSpec((8, C), lambda i: (0, 0))),
        compiler_params=_params("arbitrary"),
    )(*([proj] * 12), dycat, dycat, dycat, conv_w, conv_b, g_conv, dep)


def _attn_post_bwd(o, proj, dycat, g_attn, dep, tr=256):
    S, C = o.shape
    tr = min(tr, S)

    def body(o_ref, z_ref, dy_ref, g_ref, dep_ref, do_ref, dz_ref, sums_ref):
        i = pl.program_id(0)
        ov, zv, dyn = o_ref[...], z_ref[...], dy_ref[...]
        ra = lax.rsqrt(jnp.mean(ov * ov, axis=-1, keepdims=True) + EPS)
        ohat = ov * ra
        sz = _silu(zv)
        wgt = dyn * g_ref[...] * sz
        do_ref[...] = ra * (wgt - ohat * jnp.mean(wgt * ohat, axis=-1, keepdims=True))
        dz_ref[...] = (dyn * ohat * g_ref[...] * _dsilu(zv)).astype(bf16)
        row = lax.broadcasted_iota(jnp.int32, (8, C), 0)
        upd = jnp.where(row == 0, jnp.sum(dyn * ohat * sz, axis=0, keepdims=True), 0.0)

        @pl.when(i == 0)
        def _():
            sums_ref[...] = upd

        @pl.when(i > 0)
        def _():
            sums_ref[...] += upd

    return pl.pallas_call(
        body, name="attn_post_bwd",
        out_shape=(jax.ShapeDtypeStruct((S, C), f32), jax.ShapeDtypeStruct((S, C), bf16), jax.ShapeDtypeStruct((8, C), f32)),
        grid=(S // tr,),
        in_specs=[pl.BlockSpec((tr, C), lambda i: (i, 0)), pl.BlockSpec((tr, C), lambda i: (i, 7)),
                  pl.BlockSpec((tr, C), lambda i: (i, 1)), pl.BlockSpec((1, C), lambda i: (0, 0)), ANY_SPEC],
        out_specs=(pl.BlockSpec((tr, C), lambda i: (i, 0)), pl.BlockSpec((tr, C), lambda i: (i, 0)),
                   pl.BlockSpec((8, C), lambda i: (0, 0))),
        compiler_params=_params("arbitrary"),
    )(o, proj, dycat, g_attn, dep)


def _attn_bwd(proj, o, do, lse, slopes, dep):
    S, C = o.shape
    npair = C // PAIR

    def body(q_ref, k_ref, v_ref, o_ref, do_ref, lse_ref, sl_ref, dep_ref, dq_ref, dk_ref, dv_ref,
             dq_scr, dk_scr, dv_scr, dl_scr, bias_scr):
        lane = lax.broadcasted_iota(jnp.int32, (1, PAIR), 1)
        first = lane < HEAD_DIM
        _fill_bias(bias_scr, sl_ref, S)
        ch = min(256, S)

        def prep(i, carry):
            rows = pl.ds(pl.multiple_of(i * ch, 8), ch)
            prod = do_ref[rows, :] * o_ref[rows, :]
            d0 = jnp.sum(jnp.where(first, prod, 0.0), axis=-1, keepdims=True)
            d1 = jnp.sum(jnp.where(first, 0.0, prod), axis=-1, keepdims=True)
            dl_scr[rows, :] = jnp.where(first, d0, d1)
            zero = jnp.zeros((ch, PAIR), f32)
            dq_scr[rows, :] = zero
            dk_scr[rows, :] = zero
            dv_scr[rows, :] = zero
            return carry

        lax.fori_loop(0, S // ch, prep, 0)

        for b, (_, r) in enumerate(BRANCHES):
            L, nq, nk, nblk = _branch_geometry(S, r)

            def step(idx, carry, b=b, r=r, L=L, nq=nq, nk=nk, nblk=nblk):
                qrows, krows, off = _block_rows(idx, r, L, nq, nk, nblk)
                case = off // HALF_WIN
                q2 = q_ref[qrows, :] * SCALE
                k2 = k_ref[krows, :].astype(bf16)
                v2 = v_ref[krows, :].astype(bf16)
                do2 = do_ref[qrows, :]
                lse2 = lse_ref[qrows, :]
                dl2 = dl_scr[qrows, :]
                dq2 = jnp.zeros((nq, PAIR), f32)
                dk2 = jnp.zeros((nk, PAIR), f32)
                dv2 = jnp.zeros((nk, PAIR), f32)
                for hh in range(2):
                    mine = first if hh == 0 else ~first
                    lo = hh * HEAD_DIM
                    qh = jnp.where(mine, q2, 0.0).astype(bf16)
                    doh = jnp.where(mine, do2, 0.0).astype(bf16)
                    s = lax.dot_general(qh, k2, (((1,), (1,)), ((), ())), preferred_element_type=f32)
                    s = s + bias_scr[_bias_index(b, case, hh), 0:nq, 0:nk]
                    p = jnp.exp(s - lse2[:, lo:lo + 1])
                    dv2 = dv2 + lax.dot_general(p.astype(bf16), doh, (((0,), (0,)), ((), ())), preferred_element_type=f32)
                    dp = lax.dot_general(doh, v2, (((1,), (1,)), ((), ())), preferred_element_type=f32)
                    ds = (p * (dp - dl2[:, lo:lo + 1])).astype(bf16)
                    dq2 = dq2 + jnp.where(mine, jnp.dot(ds, k2, preferred_element_type=f32), 0.0)
                    dk2 = dk2 + lax.dot_general(ds, qh, (((0,), (0,)), ((), ())), preferred_element_type=f32)
                dq_scr[qrows, :] = dq_scr[qrows, :] + dq2
                dk_scr[krows, :] = dk_scr[krows, :] + dk2
                dv_scr[krows, :] = dv_scr[krows, :] + dv2
                return carry

            lax.fori_loop(0, S // nq, step, 0, unroll=min(ATTN_UNROLL, S // nq))

        dq_ref[...] = (dq_scr[...] * SCALE).astype(bf16)
        dk_ref[...] = dk_scr[...].astype(bf16)
        dv_ref[...] = dv_scr[...].astype(bf16)

    blk = lambda part: pl.BlockSpec((S, PAIR), lambda p: (0, part * npair + p))
    own = pl.BlockSpec((S, PAIR), lambda p: (0, p))
    grad = jax.ShapeDtypeStruct((S, C), bf16)
    return pl.pallas_call(
        body, name="attn_bwd", out_shape=(grad, grad, grad), grid=(npair,),
        in_specs=[blk(4), blk(5), blk(6), own, own, own, pl.BlockSpec((None, 8, PAIR), lambda p: (p, 0, 0)), ANY_SPEC],
        out_specs=(own, own, own),
        scratch_shapes=[pltpu.VMEM((S, PAIR), f32)] * 4 + [pltpu.VMEM(_bias_shape(S), f32)],
        compiler_params=_params("parallel"),
    )(proj, proj, proj, o, do, lse, slopes, dep)


def _prenorm_bwd(dh, x, dout, scale, g_pre, tr=256):
    S, D = x.shape
    tr = min(tr, S)

    def body(dh_ref, x_ref, dout_ref, sc_ref, g_ref, gx_ref, sums_ref):
        i = pl.program_id(0)
        xv, dhv = x_ref[...], dh_ref[...]
        r = lax.rsqrt(jnp.mean(xv * xv, axis=-1, keepdims=True) + EPS)
        xn = xv * r
        dxn = dhv * (g_ref[...] * (1.0 + sc_ref[...]))
        gx_ref[...] = dout_ref[...] + r * (dxn - xn * jnp.mean(dxn * xn, axis=-1, keepdims=True))
        dhx = dhv * xn
        row = lax.broadcasted_iota(jnp.int32, (8, D), 0)
        upd = jnp.where(row == 0, jnp.sum(dhv, axis=0, keepdims=True),
                        jnp.where(row == 1, jnp.sum(dhx, axis=0, keepdims=True) * g_ref[...],
                                  jnp.where(row == 2, jnp.sum(dhx, axis=0, keepdims=True) * (1.0 + sc_ref[...]), 0.0)))

        @pl.when(i == 0)
        def _():
            sums_ref[...] = upd

        @pl.when(i > 0)
        def _():
            sums_ref[...] += upd

    row = pl.BlockSpec((tr, D), lambda i: (i, 0))
    vec = pl.BlockSpec((1, D), lambda i: (0, 0))
    return pl.pallas_call(
        body, name="prenorm_bwd",
        out_shape=(jax.ShapeDtypeStruct((S, D), f32), jax.ShapeDtypeStruct((8, D), f32)), grid=(S // tr,),
        in_specs=[row, row, row, vec, vec], out_specs=(row, pl.BlockSpec((8, D), lambda i: (0, 0))),
        compiler_params=_params("arbitrary"),
    )(dh, x, dout, scale, g_pre)


def _adamw(w, g, m, v):
    m = ADAM_B1 * m + (1.0 - ADAM_B1) * g
    v = ADAM_B2 * v + (1.0 - ADAM_B2) * (g * g)
    m_hat = m / (1.0 - ADAM_B1 ** ADAM_STEP)
    v_hat = v / (1.0 - ADAM_B2 ** ADAM_STEP)
    delta = -ADAM_LR * (m_hat / (jnp.sqrt(v_hat) + ADAM_EPS) + ADAM_WD * w)
    return delta, m, v


def _sum_rows(parts, dep):
    P = parts.shape[1]

    def body(p_ref, dep_ref, o_ref):
        acc = p_ref[0:1, :]
        for j in range(1, NDEV):
            acc = acc + p_ref[j:j + 1, :]
        o_ref[...] = jnp.broadcast_to(acc, (8, P))

    vmem = pl.BlockSpec(memory_space=pltpu.VMEM)
    return pl.pallas_call(body, name="sum_small", out_shape=jax.ShapeDtypeStruct((8, P), f32),
                          in_specs=[vmem, ANY_SPEC], out_specs=vmem, compiler_params=_params())(parts, dep)


def _adamw_small(w, g, m, v):
    def body(w_ref, g_ref, m_ref, v_ref, d_ref, nm_ref, nv_ref):
        d_ref[...], nm_ref[...], nv_ref[...] = _adamw(w_ref[...], g_ref[...], m_ref[...], v_ref[...])

    out = jax.ShapeDtypeStruct(w.shape, f32)
    return pl.pallas_call(body, name="adamw_small", out_shape=(out, out, out), compiler_params=_params())(w, g, m, v)


def _adamw_sharded(parts, own, w, m, v, name, tr=128):
    R, Cc = w.shape
    tr = min(tr, R)
    n = parts.shape[0]

    def body(p_ref, own_ref, w_ref, m_ref, v_ref, g_ref, d_ref, nm_ref, nv_ref):
        g = own_ref[...].astype(f32)
        for j in range(n):
            g = g + p_ref[j].astype(f32)
        g_ref[...] = g
        d_ref[...], nm_ref[...], nv_ref[...] = _adamw(w_ref[...], g, m_ref[...], v_ref[...])

    row = pl.BlockSpec((tr, Cc), lambda i: (i, 0))
    out = jax.ShapeDtypeStruct((R, Cc), f32)
    return pl.pallas_call(
        body, name=name, out_shape=(out, out, out, out), grid=(R // tr,),
        in_specs=[pl.BlockSpec((n, tr, Cc), lambda i: (0, i, 0)), row, row, row, row],
        out_specs=(row, row, row, row), compiler_params=_params("parallel"),
    )(parts, own, w, m, v)


def _adamw_ada(c_t, dmod_cols, w, m, v, dep, tr=256):
    D, W = w.shape
    tr = min(tr, D)

    def body(c_ref, dm_ref, w_ref, m_ref, v_ref, dep_ref, g_ref, d_ref, nm_ref, nv_ref):
        cv, dm = c_ref[...], dm_ref[...]
        g = cv[:, 0:1] * dm[0:1, :]
        for b in range(1, NDEV):
            g = g + cv[:, b:b + 1] * dm[b:b + 1, :]
        g_ref[...] = g
        d_ref[...], nm_ref[...], nv_ref[...] = _adamw(w_ref[...], g, m_ref[...], v_ref[...])

    row = pl.BlockSpec((tr, W), lambda i: (i, 0))
    out = jax.ShapeDtypeStruct((D, W), f32)
    return pl.pallas_call(
        body, name="adamw_ada", out_shape=(out, out, out, out), grid=(D // tr,),
        in_specs=[pl.BlockSpec((tr, NDEV), lambda i: (i, 0)), pl.BlockSpec((NDEV, W), lambda i: (0, 0)), row, row, row,
                  ANY_SPEC],
        out_specs=(row, row, row, row), compiler_params=_params("parallel"),
    )(c_t, dmod_cols, w, m, v, dep)


def kernel(x, c, w_ada, b_ada, g_pre, w_in, conv_w, conv_b, g_conv, g_attn, w_out, g_post, loss_target, m_w_ada, m_b_ada, m_g_pre, m_w_in, m_conv_w, m_conv_b, m_g_conv, m_g_attn, m_w_out, m_g_post, v_w_ada, v_b_ada, v_g_pre, v_w_in, v_conv_w, v_conv_b, v_g_conv, v_g_attn, v_w_out, v_g_post):
    S, D = x.shape[1], x.shape[2]
    C = D // 2
    W = w_ada.shape[2]
    CW = conv_w.shape[2]
    me = 4 * lax.axis_index("x") + 2 * lax.axis_index("y") + lax.axis_index("c")
    x2, tgt = x[0], loss_target[0]
    w_ada2, w_in2, w_out2 = w_ada[0], w_in[0], w_out[0]

    R = D // NDEV
    core = lax.axis_index("c").astype(jnp.int32).reshape(1)

    cw_slab = jnp.zeros((8, CW), f32).at[:3].set(conv_w[0])
    c_blocks, cw_g = _all_gather([_silu_block(c.reshape(D // 128, 128)), cw_slab], "gather_c")
    c_all = c_blocks.reshape(NDEV, D)
    conv_w_full = jnp.transpose(cw_g, (1, 0, 2)).reshape(8, C)
    b_cols = lax.dynamic_slice_in_dim(b_ada, me * W, W, axis=1)
    (mod_slabs,) = _all_to_all([_ada_cols(c_all, w_ada2, b_cols)], "scatter_mod")
    mod = mod_slabs[:, 0, :].reshape(1, 3 * D)
    shift, scale, gate = mod[:, :D], mod[:, D:2 * D], mod[:, 2 * D:]

    land_i = lax.dynamic_update_slice(lax.empty((NDEV, D, C), bf16), w_in2.astype(bf16)[None], (me, 0, 0))
    land_o = lax.dynamic_update_slice(lax.empty((NDEV, R, D), bf16), w_out2.astype(bf16)[None], (me, 0, 0))
    (wi_send, wo_send), (wi_recv, wo_recv), (land_i, land_o), w_token = _weights_start([land_i, land_o], [mod_slabs])

    h = _prenorm(x2, scale, shift, g_pre, w_token)
    (fi_send, fi_recv), (land_i,), _ = _weights_forward("w_in_forward", land_i, wi_recv, after=[h])
    win_g = _weights_wait("w_in_wait", land_i, wi_send, wi_recv, fi_send, fi_recv, after=[h])
    proj = _matmul(h, win_g, name="in_proj", out_dtype=f32, b_slots=True)
    (fo_send, fo_recv), (land_o,), _ = _weights_forward("w_out_forward", land_o, wo_recv, after=[proj])
    slopes = _head_slopes(C // HEAD_DIM)
    ycat = _conv_fwd(proj, conv_w_full, conv_b, g_conv)
    o, lse = _attn_fwd(proj, slopes)
    ycat = _attn_post(ycat, o, proj, g_attn)
    wout_g = _weights_wait("w_out_wait", land_o, wo_send, wo_recv, fo_send, fo_recv, after=[ycat])
    wout_full = wout_g.reshape(D, D)
    y = _matmul(ycat, wout_full, name="out_proj", out_dtype=f32)
    dy, dout, post_sums = _sandwich(y, x2, tgt, gate, g_post)

    loss = lax.psum(post_sums[2, 0], ("x", "y", "c"))
    gw_out = _matmul(ycat, dy, name="out_proj_dw", out_dtype=bf16, ta=True, dep=loss.reshape(1, 1)).reshape(NDEV, R, D)
    po_send, po_recv, gw_out, pair_o, po_token = _pair_start("g_out_pair_start", gw_out)
    dycat = _matmul(dy, wout_full, name="out_proj_dx", out_dtype=f32, tb=True, dep=po_token)
    gw_out, pair_o = _pair_wait("g_out_pair_wait", gw_out, pair_o, po_send, po_recv, after=[dycat])
    sum_o = _pair_sum("g_out_pair_sum", gw_out, pair_o, core)
    co_send, co_recv, sum_o, land_go, co_token = _chip_start(
        "g_out_chip_start", sum_o, jnp.zeros((NCHIP, R, D), bf16), 0)
    dpc, conv_sums = _conv_bwd(proj, dycat, conv_w_full, conv_b, g_conv, co_token)
    gw_c = _matmul(h, dpc, name="in_proj_dw_conv", out_dtype=bf16, ta=True, out_slots=4)
    pc_send, pc_recv, gw_c, pair_c, pc_token = _pair_start("g_conv_pair_start", gw_c)
    do, dza, attn_sums = _attn_post_bwd(o, proj, dycat, g_attn, pc_token)
    gw_c, pair_c = _pair_wait("g_conv_pair_wait", gw_c, pair_c, pc_send, pc_recv, after=[do])
    sum_c = _pair_sum("g_conv_pair_sum", gw_c, pair_c, core)
    cc_send, cc_recv, sum_c, land_gi, cc_token = _chip_start(
        "g_conv_chip_start", sum_c, jnp.zeros((NCHIP, D, C), bf16), 0)
    dq, dk, dv = _attn_bwd(proj, o, do, lse, slopes, cc_token)
    dproj = jnp.concatenate([dpc, dq, dk, dv, dza], axis=1)
    gw_a = _matmul(h, dproj, name="in_proj_dw_attn", out_dtype=bf16, ta=True, out_slots=4, b_cols=(4 * C, 4 * C))
    pa_send, pa_recv, gw_a, pair_a, pa_token = _pair_start("g_attn_pair_start", gw_a)
    gw_a, pair_a = _pair_wait("g_attn_pair_wait", gw_a, pair_a, pa_send, pa_recv, after=[pa_token])
    sum_a = _pair_sum("g_attn_pair_sum", gw_a, pair_a, core)
    part_a, part_b = (0, 3 * D // 4), (3 * D // 4, D // 4)
    ca_send, ca_recv, sum_a, land_gi, ca_token = _chip_start("g_attn_chip_start_a", sum_a, land_gi, 4, part_a)
    dh = _matmul_slabs_t(dproj, win_g, name="in_proj_dx", dep=ca_token)
    grad_x, pre_sums = _prenorm_bwd(dh, x2, dout, scale, g_pre)

    small = jnp.concatenate([pre_sums[0:1], pre_sums[1:2], post_sums[0:1],
                             pre_sums[2:3], post_sums[1:2],
                             conv_sums[2:3], conv_sums[3:4], conv_sums[4:5],
                             conv_sums[1:2], conv_sums[0:1], attn_sums[0:1]], axis=1)
    (small_all,) = _all_gather([small.reshape(8 * D // 128, 128)], "gather_small")
    cb_send, cb_recv, sum_a, land_gi, cb_token = _chip_start("g_attn_chip_start_b", sum_a, land_gi, 4, part_b,
                                                             after=[small_all])
    small_all = small_all.reshape(NDEV, 8 * D)
    tot = _sum_rows(small_all, cb_token)[0:1]

    g_b_ada = tot[:, :3 * D]
    g_g_pre, g_g_post = tot[:, 3 * D:4 * D], tot[:, 4 * D:5 * D]
    g_conv_w_full = tot[:, 5 * D:5 * D + 3 * C].reshape(3, C)
    g_conv_w = lax.dynamic_slice_in_dim(g_conv_w_full, me * CW, CW, axis=1)[None]
    g_conv_b, g_g_conv, g_g_attn = (tot[:, 5 * D + (3 + t) * C:5 * D + (4 + t) * C] for t in range(3))

    dmod_cols = lax.dynamic_slice_in_dim(small_all[:, :3 * D], me * W, W, axis=1)
    g_w_ada, d_w_ada, nm_w_ada, nv_w_ada = _adamw_ada(c_all.T, dmod_cols, w_ada2, m_w_ada[0], v_w_ada[0], cb_token)

    sum_o, land_go = _chip_wait("g_out_chip_wait", sum_o, land_go, co_send, co_recv, 0, after=[g_w_ada])
    own_out = lax.dynamic_index_in_dim(sum_o, me // 2, 0, keepdims=False)
    g_w_out, d_w_out, nm_w_out, nv_w_out = _adamw_sharded(land_go, own_out, w_out2, m_w_out[0], v_w_out[0], "adamw_w_out")
    sum_c, land_gi = _chip_wait("g_conv_chip_wait", sum_c, land_gi, cc_send, cc_recv, 0, after=[g_w_out])
    sum_a, land_gi = _chip_wait("g_attn_chip_wait_a", sum_a, land_gi, ca_send, ca_recv, 4, [g_w_out], part_a)
    sum_a, land_gi = _chip_wait("g_attn_chip_wait_b", sum_a, land_gi, cb_send, cb_recv, 4, [g_w_out], part_b)
    own_in = jnp.where(me < 4, lax.dynamic_index_in_dim(sum_c, (me % 4) // 2, 0, keepdims=False),
                       lax.dynamic_index_in_dim(sum_a, (me % 4) // 2, 0, keepdims=False))
    g_w_in, d_w_in, nm_w_in, nv_w_in = _adamw_sharded(land_gi, own_in, w_in2, m_w_in[0], v_w_in[0], "adamw_w_in")

    pack = lambda *vs: jnp.concatenate([a.reshape(1, -1) for a in vs], axis=1)
    smalls = [(b_ada, g_b_ada, m_b_ada, v_b_ada), (g_pre, g_g_pre, m_g_pre, v_g_pre),
              (conv_w, g_conv_w, m_conv_w, v_conv_w), (conv_b, g_conv_b, m_conv_b, v_conv_b),
              (g_conv, g_g_conv, m_g_conv, v_g_conv), (g_attn, g_g_attn, m_g_attn, v_g_attn),
              (g_post, g_g_post, m_g_post, v_g_post)]
    packed = [pack(*[s[t] for s in smalls]) for t in range(4)]
    npad = -packed[0].shape[1] % 128
    packed = [jnp.pad(p, ((0, 0), (0, npad)), constant_values=1.0) for p in packed]
    d_s, nm_s, nv_s = _adamw_small(*packed)

    def unpack(vec):
        out, at = [], 0
        for s in smalls:
            n = s[0].size
            out.append(vec[:, at:at + n].reshape(s[0].shape))
            at += n
        return out

    d_b_ada, d_g_pre, d_conv_w, d_conv_b, d_g_conv, d_g_attn, d_g_post = unpack(d_s)
    nm_b_ada, nm_g_pre, nm_conv_w, nm_conv_b, nm_g_conv, nm_g_attn, nm_g_post = unpack(nm_s)
    nv_b_ada, nv_g_pre, nv_conv_w, nv_conv_b, nv_g_conv, nv_g_attn, nv_g_post = unpack(nv_s)

    return (loss, grad_x[None],
            g_w_ada[None], g_b_ada, g_g_pre, g_w_in[None], g_conv_w, g_conv_b, g_g_conv, g_g_attn, g_w_out[None], g_g_post,
            d_w_ada[None], d_b_ada, d_g_pre, d_w_in[None], d_conv_w, d_conv_b, d_g_conv, d_g_attn, d_w_out[None], d_g_post,
            nm_w_ada[None], nm_b_ada, nm_g_pre, nm_w_in[None], nm_conv_w, nm_conv_b, nm_g_conv, nm_g_attn, nm_w_out[None], nm_g_post,
            nv_w_ada[None], nv_b_ada, nv_g_pre, nv_w_in[None], nv_conv_w, nv_conv_b, nv_g_conv, nv_g_attn, nv_w_out[None], nv_g_post)
```

```python
import functools

import jax
import jax.numpy as jnp
from jax import lax
from jax.experimental import pallas as pl
from jax.experimental.pallas import tpu as pltpu

f32 = jnp.float32
bf16 = jnp.bfloat16

NDEV = 8
HEAD_DIM = 64
PAIR = 2 * HEAD_DIM
BRANCHES = ((128, 1), (512, 4), (2048, 16))
HALF_WIN = 64
EPS = 1e-6
NEG_INF = -1e30
ADAM_LR, ADAM_B1, ADAM_B2, ADAM_EPS, ADAM_WD, ADAM_STEP = 0.001, 0.9, 0.999, 1e-08, 0.01, 10
MESH = pl.DeviceIdType.MESH
VMEM_LIMIT = 56 * 1024 * 1024
HBM_SPEC = pl.BlockSpec(memory_space=pltpu.HBM)
ANY_SPEC = pl.BlockSpec(memory_space=pl.ANY)
SEM_SPEC = pl.BlockSpec(memory_space=pltpu.SEMAPHORE)


def _params(*sem):
    return pltpu.CompilerParams(dimension_semantics=sem or None, vmem_limit_bytes=VMEM_LIMIT)


def _silu(z):
    return z * jax.nn.sigmoid(z)


def _dsilu(z):
    s = jax.nn.sigmoid(z)
    return s * (1.0 + z * (1.0 - s))


def _my_place():
    x, y, c = lax.axis_index("x"), lax.axis_index("y"), lax.axis_index("c")
    return x, y, c, 4 * x + 2 * y + c


def _peer(x, y, c, k):
    px, py, pc = x ^ (k >> 2 & 1), y ^ (k >> 1 & 1), c ^ (k & 1)
    return (px, py, pc), 4 * px + 2 * py + pc


def _all_gather(arrays, name):
    n = len(arrays)

    def body(*refs):
        srcs, dsts = refs[:n], refs[n:2 * n]
        send_sems, recv_sems, local_sems = refs[2 * n:]
        x, y, c, me = _my_place()
        locals_, sends = [], []
        for t in range(n):
            own = pltpu.make_async_copy(srcs[t], dsts[t].at[me], local_sems.at[t])
            own.start()
            locals_.append(own)
            for k in range(1, NDEV):
                peer, pidx = _peer(x, y, c, k)
                cp = pltpu.make_async_remote_copy(
                    src_ref=srcs[t], dst_ref=dsts[t].at[me], send_sem=send_sems.at[t, k],
                    recv_sem=recv_sems.at[t, k], device_id=peer, device_id_type=MESH)
                cp.start()
                sends.append(cp)
        for t in range(n):
            for k in range(1, NDEV):
                peer, pidx = _peer(x, y, c, k)
                pltpu.make_async_remote_copy(
                    src_ref=srcs[t], dst_ref=dsts[t].at[pidx], send_sem=send_sems.at[t, k],
                    recv_sem=recv_sems.at[t, k], device_id=peer, device_id_type=MESH).wait_recv()
        for cp in sends:
            cp.wait_send()
        for cp in locals_:
            cp.wait()

    return pl.pallas_call(
        body, name=name,
        out_shape=tuple(jax.ShapeDtypeStruct((NDEV,) + a.shape, a.dtype) for a in arrays),
        in_specs=[HBM_SPEC] * n, out_specs=tuple([HBM_SPEC] * n),
        scratch_shapes=[pltpu.SemaphoreType.DMA((n, NDEV)), pltpu.SemaphoreType.DMA((n, NDEV)),
                        pltpu.SemaphoreType.DMA((n,))],
    )(*arrays)


def _all_to_all(arrays, name):
    n = len(arrays)

    def body(*refs):
        srcs, dsts = refs[:n], refs[n:2 * n]
        send_sems, recv_sems, local_sems = refs[2 * n:]
        x, y, c, me = _my_place()
        locals_, sends = [], []
        for t in range(n):
            own = pltpu.make_async_copy(srcs[t].at[me], dsts[t].at[me], local_sems.at[t])
            own.start()
            locals_.append(own)
            for k in range(1, NDEV):
                peer, pidx = _peer(x, y, c, k)
                cp = pltpu.make_async_remote_copy(
                    src_ref=srcs[t].at[pidx], dst_ref=dsts[t].at[me], send_sem=send_sems.at[t, k],
                    recv_sem=recv_sems.at[t, k], device_id=peer, device_id_type=MESH)
                cp.start()
                sends.append(cp)
        for t in range(n):
            for k in range(1, NDEV):
                peer, pidx = _peer(x, y, c, k)
                pltpu.make_async_remote_copy(
                    src_ref=srcs[t].at[pidx], dst_ref=dsts[t].at[pidx], send_sem=send_sems.at[t, k],
                    recv_sem=recv_sems.at[t, k], device_id=peer, device_id_type=MESH).wait_recv()
        for cp in sends:
            cp.wait_send()
        for cp in locals_:
            cp.wait()

    return pl.pallas_call(
        body, name=name,
        out_shape=tuple(jax.ShapeDtypeStruct(a.shape, a.dtype) for a in arrays),
        in_specs=[HBM_SPEC] * n, out_specs=tuple([HBM_SPEC] * n),
        scratch_shapes=[pltpu.SemaphoreType.DMA((n, NDEV)), pltpu.SemaphoreType.DMA((n, NDEV)),
                        pltpu.SemaphoreType.DMA((n,))],
    )(*arrays)


def _comm_call(name, arrays, sems, new_sems, body, after=(), token=False):
    na, ns, nn, nf = len(arrays), len(sems), len(new_sems), len(after)

    def kern(*refs):
        ins, outs = refs[:na + ns + nf], refs[na + ns + nf:]
        body(ins[:na], ins[na:na + ns], outs[:nn])
        if token:
            outs[nn + na][...] = jnp.zeros((8, 128), f32)

    out_shape = ([pltpu.SemaphoreType.DMA(s) for s in new_sems] + [pltpu.HBM(a.shape, a.dtype) for a in arrays]
                 + ([jax.ShapeDtypeStruct((8, 128), f32)] if token else []))
    out_specs = [SEM_SPEC] * nn + [HBM_SPEC] * na + ([pl.BlockSpec(memory_space=pltpu.VMEM)] if token else [])
    res = pl.pallas_call(
        kern, name=name, out_shape=tuple(out_shape),
        in_specs=[HBM_SPEC] * na + [SEM_SPEC] * ns + [ANY_SPEC] * nf, out_specs=tuple(out_specs),
        input_output_aliases={t: nn + t for t in range(na)},
        compiler_params=pltpu.CompilerParams(has_side_effects=pltpu.SideEffectType.DATAFLOW_SIDE_EFFECTING),
    )(*[pltpu.with_memory_space_constraint(a, pltpu.HBM) for a in arrays], *sems, *after)
    return list(res[:nn]), list(res[nn:nn + na]), (res[nn + na] if token else None)


def _remote(src, dst, send_sem, recv_sem, device):
    return pltpu.make_async_remote_copy(src_ref=src, dst_ref=dst, send_sem=send_sem, recv_sem=recv_sem,
                                        device_id=device, device_id_type=MESH)


SAME_CORE = (2, 4, 6)
VIA_SIBLING = (3, 5, 7)


def _weights_start(lands, after):
    n = len(lands)

    def body(a, s, new):
        x, y, c, me = _my_place()
        for t, land in enumerate(a):
            send, recv = new[t], new[n + t]
            for k in (1,) + SAME_CORE:
                peer, _ = _peer(x, y, c, k)
                _remote(land.at[me], land.at[me], send.at[k], recv.at[k], peer).start()

    sems, lands, token = _comm_call("weights_start", lands, [], [(NDEV,)] * (2 * n), body, after=after, token=True)
    return sems[:n], sems[n:], lands, token


def _weights_forward(name, land, recv, after):
    def body(a, s, new):
        (land,), (recv,), (fsend, frecv) = a, s, new
        x, y, c, me = _my_place()
        sibling, _ = _peer(x, y, c, 1)
        for k in SAME_CORE:
            peer, slot = _peer(x, y, c, k)
            _remote(land.at[slot], land.at[slot], fsend.at[k], recv.at[k], peer).wait_recv()
            _remote(land.at[slot], land.at[slot], fsend.at[k], frecv.at[k ^ 1], sibling).start()

    return _comm_call(name, [land], [recv], [(NDEV,), (NDEV,)], body, after=after)


def _weights_wait(name, land, send, recv, fsend, frecv, after):
    def body(a, s, new):
        (land,), (send, recv, fsend, frecv) = a, s
        x, y, c, me = _my_place()
        sibling, sib_slot = _peer(x, y, c, 1)
        _remote(land.at[sib_slot], land.at[sib_slot], send.at[1], recv.at[1], sibling).wait_recv()
        for k in VIA_SIBLING:
            _, slot = _peer(x, y, c, k)
            _remote(land.at[slot], land.at[slot], fsend.at[k ^ 1], frecv.at[k], sibling).wait_recv()
        for k in (1,) + SAME_CORE:
            peer, _ = _peer(x, y, c, k)
            _remote(land.at[me], land.at[me], send.at[k], recv.at[k], peer).wait_send()
        for k in SAME_CORE:
            _, slot = _peer(x, y, c, k)
            _remote(land.at[slot], land.at[slot], fsend.at[k], frecv.at[k ^ 1], sibling).wait_send()

    return _comm_call(name, [land], [send, recv, fsend, frecv], [], body, after=after)[1][0]


def _diag_relay(x, y, c):
    slot = 4 * (x ^ (1 - c)) + 2 * (y ^ c) + c
    return slot, (x ^ c, y ^ (1 - c), c)


def _w_in_start(land, after):
    def body(a, s, new):
        (land,), (send, recv) = a, new
        x, y, c, me = _my_place()
        for k in (1, 2, 4):
            peer, _ = _peer(x, y, c, k)
            _remote(land.at[me], land.at[me], send.at[k], recv.at[k], peer).start()

    (send, recv), (land,), token = _comm_call("w_in_start", [land], [], [(NDEV,), (NDEV,)], body, after=after, token=True)
    return send, recv, land, token


def _w_in_sibling(land, recv, after):
    def body(a, s, new):
        (land,), (recv,) = a, s
        x, y, c, me = _my_place()
        sibling, slot = _peer(x, y, c, 1)
        _remote(land.at[slot], land.at[slot], recv.at[1], recv.at[1], sibling).wait_recv()

    return _comm_call("w_in_sibling", [land], [recv], [], body, after=after)[1][0]


def _w_in_relay(land, land_o, recv, after):
    def body(a, s, new):
        (land, land_o), (recv,), (fsend, frecv, osend, orecv) = a, s, new
        x, y, c, me = _my_place()
        sibling, _ = _peer(x, y, c, 1)
        for k in (2, 4):
            peer, slot = _peer(x, y, c, k)
            _remote(land.at[slot], land.at[slot], fsend.at[k], recv.at[k], peer).wait_recv()
        slot, target = _diag_relay(x, y, c)
        _remote(land.at[slot], land.at[slot], fsend.at[6], frecv.at[6], target).start()
        for k in (2, 4):
            _, slot = _peer(x, y, c, k)
            _remote(land.at[slot], land.at[slot], fsend.at[k], frecv.at[k ^ 1], sibling).start()
        for k in (1,) + SAME_CORE:
            peer, _ = _peer(x, y, c, k)
            _remote(land_o.at[me], land_o.at[me], osend.at[k], orecv.at[k], peer).start()

    sems, (land, land_o), _ = _comm_call("w_in_relay", [land, land_o], [recv], [(NDEV,)] * 4, body, after=after)
    return sems, land, land_o


def _w_in_forwarded(land, frecv, after):
    def body(a, s, new):
        (land,), (frecv,) = a, s
        x, y, c, me = _my_place()
        sibling, _ = _peer(x, y, c, 1)
        for k in (3, 5):
            _, slot = _peer(x, y, c, k)
            _remote(land.at[slot], land.at[slot], frecv.at[k], frecv.at[k], sibling).wait_recv()

    return _comm_call("w_in_forwarded", [land], [frecv], [], body, after=after)[1][0]


def _w_in_diag(land, frecv, after):
    def body(a, s, new):
        (land,), (frecv,), (dsend, drecv) = a, s, new
        x, y, c, me = _my_place()
        sibling, _ = _peer(x, y, c, 1)
        peer, slot = _peer(x, y, c, 6)
        _remote(land.at[slot], land.at[slot], dsend.at[6], frecv.at[6], peer).wait_recv()
        _remote(land.at[slot], land.at[slot], dsend.at[6], drecv.at[7], sibling).start()

    (dsend, drecv), (land,), _ = _comm_call("w_in_diag", [land], [frecv], [(NDEV,), (NDEV,)], body, after=after)
    return dsend, drecv, land


def _w_in_finish(land, send, fsend, dsend, drecv, after):
    def body(a, s, new):
        (land,), (send, fsend, dsend, drecv) = a, s
        x, y, c, me = _my_place()
        sibling, _ = _peer(x, y, c, 1)
        _, slot = _peer(x, y, c, 7)
        _remote(land.at[slot], land.at[slot], dsend.at[6], drecv.at[7], sibling).wait_recv()
        for k in (1, 2, 4):
            peer, _ = _peer(x, y, c, k)
            _remote(land.at[me], land.at[me], send.at[k], send.at[k], peer).wait_send()
        for k in (2, 4, 6):
            _, slot = _peer(x, y, c, k)
            _remote(land.at[slot], land.at[slot], fsend.at[k], fsend.at[k], sibling).wait_send()
        _, slot = _peer(x, y, c, 6)
        _remote(land.at[slot], land.at[slot], dsend.at[6], dsend.at[6], sibling).wait_send()

    return _comm_call("w_in_finish", [land], [send, fsend, dsend, drecv], [], body, after=after)[1][0]


def _in_proj_part(name, h, land, proj, me_arr, k0, kstep, nk, tm=1024):
    S, D = h.shape
    C = land.shape[2]
    tm = min(tm, S)

    def body(me_ref, a_ref, b_ref, *rest):
        rest[-1][...] = jnp.dot(a_ref[...], b_ref[...], preferred_element_type=f32)

    slot = lambda j, me: me[0] ^ (k0 + kstep * j)
    args = [h, land] + ([] if proj is None else [proj])
    grid_spec = pltpu.PrefetchScalarGridSpec(
        num_scalar_prefetch=1, grid=(nk, S // tm),
        in_specs=[pl.BlockSpec((tm, D), lambda j, i, me: (i, 0)),
                  pl.BlockSpec((None, D, C), lambda j, i, me: (slot(j, me), 0, 0))] + [ANY_SPEC] * (len(args) - 2),
        out_specs=pl.BlockSpec((tm, C), lambda j, i, me: (i, slot(j, me))))
    return pl.pallas_call(
        body, name=name, out_shape=jax.ShapeDtypeStruct((S, NDEV * C), f32), grid_spec=grid_spec,
        input_output_aliases={} if proj is None else {3: 0}, compiler_params=_params("arbitrary", "arbitrary"),
    )(me_arr, *args)


NCHIP = NDEV // 2


def _pair_start(name, src):
    npair = src.shape[0] // 2

    def body(a, s, new):
        (src, pair), (send, recv) = a, new
        x, y, c, me = _my_place()
        sibling, _ = _peer(x, y, c, 1)
        for i in range(npair):
            _remote(src.at[2 * i + 1 - c], pair.at[i], send.at[i], recv.at[i], sibling).start()

    pair = lax.empty((npair,) + src.shape[1:], src.dtype)
    (send, recv), (src, pair), token = _comm_call(name, [src, pair], [], [(npair,), (npair,)], body, token=True)
    return send, recv, src, pair, token


def _pair_wait(name, src, pair, send, recv, after):
    npair = pair.shape[0]

    def body(a, s, new):
        (src, pair), (send, recv) = a, s
        x, y, c, me = _my_place()
        sibling, _ = _peer(x, y, c, 1)
        for i in range(npair):
            cp = _remote(src.at[2 * i + 1 - c], pair.at[i], send.at[i], recv.at[i], sibling)
            cp.wait_recv()
            cp.wait_send()

    return _comm_call(name, [src, pair], [send, recv], [], body, after=after)[1]


def _pair_sum(name, src, pair, core, tr=256):
    npair, R, Cc = pair.shape
    tr = min(tr, R)

    def body(core_ref, a_ref, b_ref, o_ref):
        o_ref[...] = (a_ref[...].astype(f32) + b_ref[...].astype(f32)).astype(o_ref.dtype)

    grid_spec = pltpu.PrefetchScalarGridSpec(
        num_scalar_prefetch=1, grid=(npair, R // tr),
        in_specs=[pl.BlockSpec((None, tr, Cc), lambda i, r, core: (2 * i + core[0], r, 0)),
                  pl.BlockSpec((None, tr, Cc), lambda i, r, core: (i, r, 0))],
        out_specs=pl.BlockSpec((None, tr, Cc), lambda i, r, core: (i, r, 0)))
    return pl.pallas_call(body, name=name, out_shape=jax.ShapeDtypeStruct(pair.shape, pair.dtype),
                          grid_spec=grid_spec, compiler_params=_params("parallel", "parallel"))(core, src, pair)


def _owner_chip(first, i):
    q = first // 2 + i
    return q >> 1 & 1, q & 1


def _chip_start(name, sums, land, first, rows=None, after=()):
    npair = sums.shape[0]
    rows = pl.ds(*(rows or (0, sums.shape[1])))

    def body(a, s, new):
        (sums, land), (send, recv) = a, new
        x, y, c, me = _my_place()
        for i in range(npair):
            ox, oy = _owner_chip(first, i)

            @pl.when((x != ox) | (y != oy))
            def _():
                _remote(sums.at[i, rows], land.at[2 * x + y, rows], send.at[i], recv.at[2 * x + y], (ox, oy, c)).start()

    (send, recv), (sums, land), token = _comm_call(name, [sums, land], [], [(npair,), (NCHIP,)], body, after=after,
                                                   token=True)
    return send, recv, sums, land, token


def _chip_wait(name, sums, land, send, recv, first, after, rows=None):
    npair = sums.shape[0]
    rows = pl.ds(*(rows or (0, sums.shape[1])))

    def body(a, s, new):
        (sums, land), (send, recv) = a, s
        x, y, c, me = _my_place()
        mine = (me >= first) & (me < first + 2 * npair)
        for i in range(npair):
            ox, oy = _owner_chip(first, i)

            @pl.when((x != ox) | (y != oy))
            def _():
                _remote(sums.at[i, rows], land.at[2 * x + y, rows], send.at[i], recv.at[2 * x + y], (ox, oy, c)).wait_send()
        for q in range(NCHIP):
            @pl.when(mine & (2 * x + y != q))
            def _():
                _remote(sums.at[0, rows], land.at[q, rows], send.at[0], recv.at[q], (q >> 1, q & 1, c)).wait_recv()

    return _comm_call(name, [sums, land], [send, recv], [], body, after=after)[1]


def _matmul(a, b, *, name, out_dtype, ta=False, tb=False, b_slots=False, out_slots=0, b_cols=None,
            tm=1024, tn=1024, tk=2048, dep=None):
    M, K = (a.shape[1], a.shape[0]) if ta else a.shape
    col0 = 0
    if b_slots:
        slab = b.shape[2]
        N = b.shape[1] if tb else NDEV * slab
        assert (K if tb else N) == NDEV * slab
    elif b_cols is not None:
        assert not tb
        col0, N = b_cols
    else:
        N = b.shape[0] if tb else b.shape[1]
    tm, tn, tk = min(tm, M), min(tn, N), min(tk, K)
    if b_slots:
        if tb:
            tk = min(tk, slab)
        else:
            tn = min(tn, slab)
    if out_slots:
        tn = min(tn, N // out_slots)
    nm, nn, nk = M // tm, N // tn, K // tk
    assert (nm * tm, nn * tn, nk * tk) == (M, N, K) and col0 % tn == 0, (name, M, N, K, tm, tn, tk)
    j0 = col0 // tn

    a_spec = pl.BlockSpec((tk, tm), lambda i, j, k: (k, i)) if ta else pl.BlockSpec((tm, tk), lambda i, j, k: (i, k))
    if b_slots and tb:
        per = slab // tk
        b_spec = pl.BlockSpec((None, tn, tk), lambda i, j, k: (k // per, j, k % per))
    elif b_slots:
        per = slab // tn
        b_spec = pl.BlockSpec((None, tk, tn), lambda i, j, k: (j // per, k, j % per))
    elif tb:
        b_spec = pl.BlockSpec((tn, tk), lambda i, j, k: (j, k))
    else:
        b_spec = pl.BlockSpec((tk, tn), lambda i, j, k: (k, j + j0))
    if out_slots:
        per_o = (N // out_slots) // tn
        o_spec = pl.BlockSpec((None, tm, tn), lambda i, j, k: (j // per_o, i, j % per_o))
        out_shape = jax.ShapeDtypeStruct((out_slots, M, N // out_slots), out_dtype)
    else:
        o_spec = pl.BlockSpec((tm, tn), lambda i, j, k: (i, j))
        out_shape = jax.ShapeDtypeStruct((M, N), out_dtype)
    dims = (((0 if ta else 1,), (1 if tb else 0,)), ((), ()))
    deps = [] if dep is None else [dep]

    def body(a_ref, b_ref, *rest):
        o_ref = rest[len(deps)]
        prod = lax.dot_general(a_ref[...], b_ref[...], dims, preferred_element_type=f32)
        if nk == 1:
            o_ref[...] = prod.astype(out_dtype)
            return
        acc_ref = rest[len(deps) + 1]
        k = pl.program_id(2)

        @pl.when(k == 0)
        def _():
            acc_ref[...] = prod

        @pl.when((k > 0) & (k < nk - 1))
        def _():
            acc_ref[...] += prod

        @pl.when(k == nk - 1)
        def _():
            o_ref[...] = (acc_ref[...] + prod).astype(out_dtype)

    return pl.pallas_call(
        body, name=name, out_shape=out_shape, grid=(nm, nn, nk),
        in_specs=[a_spec, b_spec] + [ANY_SPEC] * len(deps), out_specs=o_spec,
        scratch_shapes=[pltpu.VMEM((tm, tn), f32)] if nk > 1 else [],
        compiler_params=_params("parallel", "parallel", "arbitrary"),
    )(a, b, *deps)


def _matmul_slabs_t(a, b, *, name, tm=512, tn=512, dep=None):
    M, K = a.shape
    n_slab, N, slab = b.shape
    assert K == n_slab * slab
    tm, tn = min(tm, M), min(tn, N)
    deps = [] if dep is None else [dep]

    def body(a_ref, b_ref, *rest):
        o_ref = rest[len(deps)]
        acc = None
        for s in range(n_slab):
            prod = lax.dot_general(a_ref[:, s * slab:(s + 1) * slab], b_ref[s], (((1,), (1,)), ((), ())),
                                   preferred_element_type=f32)
            acc = prod if acc is None else acc + prod
        o_ref[...] = acc

    return pl.pallas_call(
        body, name=name, out_shape=jax.ShapeDtypeStruct((M, N), f32), grid=(M // tm, N // tn),
        in_specs=[pl.BlockSpec((tm, K), lambda i, j: (i, 0)), pl.BlockSpec((n_slab, tn, slab), lambda i, j: (0, j, 0))]
        + [ANY_SPEC] * len(deps),
        out_specs=pl.BlockSpec((tm, tn), lambda i, j: (i, j)), compiler_params=_params("parallel", "parallel"),
    )(a, b, *deps)


def _silu_block(c):
    def body(c_ref, o_ref):
        o_ref[...] = _silu(c_ref[...])

    return pl.pallas_call(body, name="silu_c", out_shape=jax.ShapeDtypeStruct(c.shape, f32))(c)


def _ada_cols(c_all, w_ada, b_cols):
    D, W = w_ada.shape

    def body(c_ref, w_ref, b_ref, o_ref):
        mod = lax.dot_general(c_ref[...], w_ref[...], (((1,), (0,)), ((), ())), preferred_element_type=f32,
                              precision=lax.Precision.HIGHEST) + b_ref[...]
        for j in range(NDEV):
            o_ref[j] = jnp.broadcast_to(mod[j:j + 1, :], (8, W))

    return pl.pallas_call(body, name="ada_cols", out_shape=jax.ShapeDtypeStruct((NDEV, 8, W), f32),
                          compiler_params=_params())(c_all, w_ada, b_cols)


def _prenorm(x, scale, shift, g_pre, dep, tr=256):
    S, D = x.shape
    tr = min(tr, S)

    def body(x_ref, sc_ref, sh_ref, g_ref, dep_ref, h_ref):
        xv = x_ref[...]
        r = lax.rsqrt(jnp.mean(xv * xv, axis=-1, keepdims=True) + EPS)
        h_ref[...] = ((xv * r) * g_ref[...] * (1.0 + sc_ref[...]) + sh_ref[...]).astype(bf16)

    row = pl.BlockSpec((tr, D), lambda i: (i, 0))
    vec = pl.BlockSpec((1, D), lambda i: (0, 0))
    return pl.pallas_call(body, name="prenorm", out_shape=jax.ShapeDtypeStruct((S, D), bf16), grid=(S // tr,),
                          in_specs=[row, vec, vec, vec, ANY_SPEC], out_specs=row, compiler_params=_params("parallel"))(
                              x, scale, shift, g_pre, dep)


def _ext_rows(i, tr, S):
    g = lax.broadcasted_iota(jnp.int32, (tr + 16, 1), 0) + (i * tr - 8)
    return (g >= 0) & (g < S)


def _halo_specs(tr, S, C, col):
    nb8 = S // 8
    main = pl.BlockSpec((tr, C), lambda i: (i, col))
    prev = pl.BlockSpec((8, C), lambda i: (jnp.maximum(i * (tr // 8) - 1, 0), col))
    nxt = pl.BlockSpec((8, C), lambda i: (jnp.minimum((i + 1) * (tr // 8), nb8 - 1), col))
    return prev, main, nxt


def _conv_fwd(proj, conv_w, conv_b, g_conv, tr=256):
    S, C = proj.shape[0], proj.shape[1] // 8
    tr = min(tr, S)

    def body(up, um, un, cp, cm, cn, bg_ref, zc_ref, w_ref, cb_ref, g_ref, o_ref):
        i = pl.program_id(0)
        exists = _ext_rows(i, tr, S)
        u = jnp.concatenate([up[...], um[...], un[...]], axis=0)
        cg = jnp.concatenate([cp[...], cm[...], cn[...]], axis=0)
        t = jnp.where(exists, cg * u, 0.0)
        t_before = pltpu.roll(t, 1, 0)[8:tr + 8]
        t_after = pltpu.roll(t, tr + 15, 0)[8:tr + 8]
        w = w_ref[...]
        cv = w[0:1] * t_before + w[1:2] * t[8:tr + 8] + w[2:3] * t_after + cb_ref[...]
        yc = bg_ref[...] * cv
        rc = lax.rsqrt(jnp.mean(yc * yc, axis=-1, keepdims=True) + EPS)
        o_ref[...] = ((yc * rc) * g_ref[...] * _silu(zc_ref[...])).astype(bf16)

    u_specs = _halo_specs(tr, S, C, 0)
    c_specs = _halo_specs(tr, S, C, 2)
    vec = pl.BlockSpec((1, C), lambda i: (0, 0))
    return pl.pallas_call(
        body, name="conv_fwd", out_shape=jax.ShapeDtypeStruct((S, 2 * C), bf16), grid=(S // tr,),
        in_specs=[*u_specs, *c_specs, pl.BlockSpec((tr, C), lambda i: (i, 1)), pl.BlockSpec((tr, C), lambda i: (i, 3)),
                  pl.BlockSpec((8, C), lambda i: (0, 0)), vec, vec],
        out_specs=pl.BlockSpec((tr, C), lambda i: (i, 0)), compiler_params=_params("parallel"),
    )(proj, proj, proj, proj, proj, proj, proj, proj, conv_w, conv_b, g_conv)


def _branch_geometry(S, r):
    L = S // r
    nq = min(128, L)
    nk = min(nq + 2 * HALF_WIN, L)
    return L, nq, nk, L // nq


def _block_rows(idx, r, L, nq, nk, nblk):
    if r == 1:
        rho, qb = 0, idx
    else:
        rho, qb = idx // nblk, idx % nblk
    i0 = qb * nq
    ws = jnp.clip(i0 - HALF_WIN, 0, L - nk)
    if r == 1:
        qrows = pl.ds(pl.multiple_of(i0, 8), nq)
        krows = pl.ds(pl.multiple_of(ws, 8), nk)
    else:
        qrows = pl.ds(rho + r * i0, nq, stride=r)
        krows = pl.ds(rho + r * ws, nk, stride=r)
    return qrows, krows, i0 - ws


N_CASES = 3
SCALE = HEAD_DIM ** -0.5
ATTN_UNROLL = 8


def _bias_shape(S):
    return (len(BRANCHES) * N_CASES * 2, min(128, S), min(128 + 2 * HALF_WIN, S))


def _bias_index(b, case, head):
    return (b * N_CASES + case) * 2 + head


def _fill_bias(bias_scr, sl_ref, S):
    sl = sl_ref[...]
    slope = (sl[0:1, 0:1], sl[0:1, HEAD_DIM:HEAD_DIM + 1])
    for b, (_, r) in enumerate(BRANCHES):
        L, nq, nk, nblk = _branch_geometry(S, r)
        assert nblk == 1 or L >= nq + 2 * HALF_WIN
        rel = lax.broadcasted_iota(jnp.int32, (nq, nk), 0) - lax.broadcasted_iota(jnp.int32, (nq, nk), 1)
        for case in range(N_CASES):
            d = jnp.abs(rel + case * HALF_WIN)
            dist = d.astype(f32) * float(r)
            for head in range(2):
                bias_scr[_bias_index(b, case, head), 0:nq, 0:nk] = jnp.where(d <= HALF_WIN, -slope[head] * dist, NEG_INF)


def _head_slopes(n_heads):
    slopes = 2.0 ** (-8.0 * jnp.arange(1, n_heads + 1, dtype=f32) / n_heads)
    return jnp.broadcast_to(jnp.repeat(slopes.reshape(n_heads // 2, 2), HEAD_DIM, axis=1)[:, None, :],
                            (n_heads // 2, 8, PAIR))


def _attn_fwd(proj, slopes):
    S, C = proj.shape[0], proj.shape[1] // 8
    npair = C // PAIR

    def body(q_ref, k_ref, v_ref, sl_ref, o_ref, lse_ref, m_scr, l_scr, a_scr, bias_scr):
        lane = lax.broadcasted_iota(jnp.int32, (1, PAIR), 1)
        first = lane < HEAD_DIM
        _fill_bias(bias_scr, sl_ref, S)

        for b, (_, r) in enumerate(BRANCHES):
            L, nq, nk, nblk = _branch_geometry(S, r)

            def step(idx, carry, b=b, r=r, L=L, nq=nq, nk=nk, nblk=nblk):
                qrows, krows, off = _block_rows(idx, r, L, nq, nk, nblk)
                case = off // HALF_WIN
                q2 = q_ref[qrows, :] * SCALE
                k2 = k_ref[krows, :].astype(bf16)
                v2 = v_ref[krows, :].astype(bf16)
                ms, accs = [], []
                for hh in range(2):
                    mine = first if hh == 0 else ~first
                    qh = jnp.where(mine, q2, 0.0).astype(bf16)
                    s = lax.dot_general(qh, k2, (((1,), (1,)), ((), ())), preferred_element_type=f32)
                    s = s + bias_scr[_bias_index(b, case, hh), 0:nq, 0:nk]
                    m = jnp.max(s, axis=-1, keepdims=True)
                    p = jnp.exp(s - m).astype(bf16)
                    vh = jnp.where(mine, v2, jnp.ones_like(v2))
                    ms.append(m)
                    accs.append(jnp.dot(p, vh, preferred_element_type=f32))
                m_scr[b, qrows, :] = jnp.where(first, ms[0], ms[1])
                a_scr[b, qrows, :] = jnp.where(first, accs[0], accs[1])
                l_scr[b, qrows, :] = jnp.where(first, accs[1], accs[0])
                return carry

            lax.fori_loop(0, S // nq, step, 0, unroll=min(ATTN_UNROLL, S // nq))

        ch = min(256, S)

        def merge(i, carry):
            rows = pl.ds(pl.multiple_of(i * ch, 8), ch)
            m = jnp.maximum(jnp.maximum(m_scr[0, rows, :], m_scr[1, rows, :]), m_scr[2, rows, :])
            l = jnp.zeros((ch, PAIR), f32)
            acc = jnp.zeros((ch, PAIR), f32)
            for b in range(3):
                w = jnp.exp(m_scr[b, rows, :] - m)
                l = l + w * pltpu.roll(l_scr[b, rows, :], HEAD_DIM, 1)
                acc = acc + w * a_scr[b, rows, :]
            o_ref[rows, :] = acc / l
            lse_ref[rows, :] = m + jnp.log(l)
            return carry

        lax.fori_loop(0, S // ch, merge, 0)

    blk = lambda part: pl.BlockSpec((S, PAIR), lambda p: (0, part * npair + p))
    out = pl.BlockSpec((S, PAIR), lambda p: (0, p))
    return pl.pallas_call(
        body, name="attn_fwd",
        out_shape=(jax.ShapeDtypeStruct((S, C), f32), jax.ShapeDtypeStruct((S, C), f32)), grid=(npair,),
        in_specs=[blk(4), blk(5), blk(6), pl.BlockSpec((None, 8, PAIR), lambda p: (p, 0, 0))],
        out_specs=(out, out),
        scratch_shapes=[pltpu.VMEM((3, S, PAIR), f32)] * 3 + [pltpu.VMEM(_bias_shape(S), f32)],
        compiler_params=_params("parallel"),
    )(proj, proj, proj, slopes)


def _attn_post(ycat, o, proj, g_attn, tr=256):
    S, C = o.shape
    tr = min(tr, S)

    def body(y_ref, o_ref, z_ref, g_ref, out_ref):
        del y_ref
        ov = o_ref[...]
        ra = lax.rsqrt(jnp.mean(ov * ov, axis=-1, keepdims=True) + EPS)
        out_ref[...] = ((ov * ra) * g_ref[...] * _silu(z_ref[...])).astype(bf16)

    return pl.pallas_call(
        body, name="attn_post", out_shape=jax.ShapeDtypeStruct(ycat.shape, ycat.dtype), grid=(S // tr,),
        in_specs=[HBM_SPEC, pl.BlockSpec((tr, C), lambda i: (i, 0)), pl.BlockSpec((tr, C), lambda i: (i, 7)),
                  pl.BlockSpec((1, C), lambda i: (0, 0))],
        out_specs=pl.BlockSpec((tr, C), lambda i: (i, 1)), input_output_aliases={0: 0},
        compiler_params=_params("arbitrary"),
    )(ycat, o, proj, g_attn)


def _sandwich(y, x, target, gate, g_post, tr=256):
    S, D = y.shape
    tr = min(tr, S)

    def body(y_ref, x_ref, t_ref, gate_ref, g_ref, dy_ref, dout_ref, sums_ref):
        i = pl.program_id(0)
        yv = y_ref[...]
        rp = lax.rsqrt(jnp.mean(yv * yv, axis=-1, keepdims=True) + EPS)
        yhat = yv * rp
        yn = yhat * g_ref[...]
        err = (x_ref[...] + gate_ref[...] * yn) - t_ref[...]
        dout = err * (1.0 / D)
        dout_ref[...] = dout
        dyn = dout * gate_ref[...]
        w = dyn * g_ref[...]
        dy_ref[...] = (rp * (w - yhat * jnp.mean(w * yhat, axis=-1, keepdims=True))).astype(bf16)
        loss = 0.5 * jnp.sum(jnp.mean(err * err, axis=-1, keepdims=True), axis=0, keepdims=True)
        row = lax.broadcasted_iota(jnp.int32, (8, D), 0)
        upd = jnp.where(row == 0, jnp.sum(dout * yn, axis=0, keepdims=True),
                        jnp.where(row == 1, jnp.sum(dyn * yhat, axis=0, keepdims=True),
                                  jnp.where(row == 2, loss, 0.0)))

        @pl.when(i == 0)
        def _():
            sums_ref[...] = upd

        @pl.when(i > 0)
        def _():
            sums_ref[...] += upd

    row = pl.BlockSpec((tr, D), lambda i: (i, 0))
    vec = pl.BlockSpec((1, D), lambda i: (0, 0))
    return pl.pallas_call(
        body, name="sandwich",
        out_shape=(jax.ShapeDtypeStruct((S, D), bf16), jax.ShapeDtypeStruct((S, D), f32), jax.ShapeDtypeStruct((8, D), f32)),
        grid=(S // tr,), in_specs=[row, row, row, vec, vec],
        out_specs=(row, row, pl.BlockSpec((8, D), lambda i: (0, 0))), compiler_params=_params("arbitrary"),
    )(y, x, target, gate, g_post)


def _conv_bwd(proj, dycat, conv_w, conv_b, g_conv, dep, tr=256):
    S, C = proj.shape[0], proj.shape[1] // 8
    tr = min(tr, S)
    n = tr + 16

    def body(*refs):
        ins, (w_ref, cb_ref, g_ref, _, dp_ref, sums_ref) = refs[:15], refs[15:]
        i = pl.program_id(0)
        exists = _ext_rows(i, tr, S)
        u, bg, cg, zc, dyn = (jnp.concatenate([ins[3 * t][...], ins[3 * t + 1][...], ins[3 * t + 2][...]], axis=0)
                              for t in range(5))
        w = w_ref[...]
        t = jnp.where(exists, cg * u, 0.0)
        t_before, t_after = pltpu.roll(t, 1, 0), pltpu.roll(t, n - 1, 0)
        cv = w[0:1] * t_before + w[1:2] * t + w[2:3] * t_after + cb_ref[...]
        yc = bg * cv
        rc = lax.rsqrt(jnp.mean(yc * yc, axis=-1, keepdims=True) + EPS)
        yhat = yc * rc
        sz = _silu(zc)
        wgt = dyn * g_ref[...] * sz
        dyc = rc * (wgt - yhat * jnp.mean(wgt * yhat, axis=-1, keepdims=True))
        dcv = jnp.where(exists, dyc * bg, 0.0)
        dt = w[0:1] * pltpu.roll(dcv, n - 1, 0) + w[1:2] * dcv + w[2:3] * pltpu.roll(dcv, 1, 0)
        mid = slice(8, tr + 8)
        dp_ref[:, 0:C] = (dt * cg)[mid].astype(bf16)
        dp_ref[:, C:2 * C] = (dyc * cv)[mid].astype(bf16)
        dp_ref[:, 2 * C:3 * C] = (dt * u)[mid].astype(bf16)
        dp_ref[:, 3 * C:4 * C] = (dyn * yhat * g_ref[...] * _dsilu(zc))[mid].astype(bf16)
        colsum = lambda v: jnp.sum(v[mid], axis=0, keepdims=True)
        parts = [colsum(dyn * yhat * sz), colsum(dcv), colsum(dcv * t_before), colsum(dcv * t), colsum(dcv * t_after)]
        row = lax.broadcasted_iota(jnp.int32, (8, C), 0)
        upd = jnp.zeros((8, C), f32)
        for j, pj in enumerate(parts):
            upd = jnp.where(row == j, pj, upd)

        @pl.when(i == 0)
        def _():
            sums_ref[...] = upd

        @pl.when(i > 0)
        def _():
            sums_ref[...] += upd

    specs = []
    for col in range(4):
        specs += _halo_specs(tr, S, C, col)
    specs += _halo_specs(tr, S, C, 0)
    vec = pl.BlockSpec((1, C), lambda i: (0, 0))
    return pl.pallas_call(
        body, name="conv_bwd",
        out_shape=(jax.ShapeDtypeStruct((S, 4 * C), bf16), jax.ShapeDtypeStruct((8, C), f32)), grid=(S // tr,),
        in_specs=[*specs, pl.BlockSpec((8, C), lambda i: (0, 0)), vec, vec, ANY_SPEC],
        out_specs=(pl.BlockSpec((tr, 4 * C), lambda i: (i, 0)), pl.BlockSpec((8, C), lambda i: (0, 0))),
        compiler_params=_params("arbitrary"),
    )(*([proj] * 12), dycat, dycat, dycat, conv_w, conv_b, g_conv, dep)


def _attn_post_bwd(o, proj, dycat, g_attn, dep, tr=256):
    S, C = o.shape
    tr = min(tr, S)

    def body(o_ref, z_ref, dy_ref, g_ref, dep_ref, do_ref, dz_ref, sums_ref):
        i = pl.program_id(0)
        ov, zv, dyn = o_ref[...], z_ref[...], dy_ref[...]
        ra = lax.rsqrt(jnp.mean(ov * ov, axis=-1, keepdims=True) + EPS)
        ohat = ov * ra
        sz = _silu(zv)
        wgt = dyn * g_ref[...] * sz
        do_ref[...] = ra * (wgt - ohat * jnp.mean(wgt * ohat, axis=-1, keepdims=True))
        dz_ref[...] = (dyn * ohat * g_ref[...] * _dsilu(zv)).astype(bf16)
        row = lax.broadcasted_iota(jnp.int32, (8, C), 0)
        upd = jnp.where(row == 0, jnp.sum(dyn * ohat * sz, axis=0, keepdims=True), 0.0)

        @pl.when(i == 0)
        def _():
            sums_ref[...] = upd

        @pl.when(i > 0)
        def _():
            sums_ref[...] += upd

    return pl.pallas_call(
        body, name="attn_post_bwd",
        out_shape=(jax.ShapeDtypeStruct((S, C), f32), jax.ShapeDtypeStruct((S, C), bf16), jax.ShapeDtypeStruct((8, C), f32)),
        grid=(S // tr,),
        in_specs=[pl.BlockSpec((tr, C), lambda i: (i, 0)), pl.BlockSpec((tr, C), lambda i: (i, 7)),
                  pl.BlockSpec((tr, C), lambda i: (i, 1)), pl.BlockSpec((1, C), lambda i: (0, 0)), ANY_SPEC],
        out_specs=(pl.BlockSpec((tr, C), lambda i: (i, 0)), pl.BlockSpec((tr, C), lambda i: (i, 0)),
                   pl.BlockSpec((8, C), lambda i: (0, 0))),
        compiler_params=_params("arbitrary"),
    )(o, proj, dycat, g_attn, dep)


def _attn_bwd(proj, o, do, lse, slopes, dep):
    S, C = o.shape
    npair = C // PAIR

    def body(q_ref, k_ref, v_ref, o_ref, do_ref, lse_ref, sl_ref, dep_ref, dq_ref, dk_ref, dv_ref,
             dq_scr, dk_scr, dv_scr, dl_scr, bias_scr):
        lane = lax.broadcasted_iota(jnp.int32, (1, PAIR), 1)
        first = lane < HEAD_DIM
        _fill_bias(bias_scr, sl_ref, S)
        ch = min(256, S)

        def prep(i, carry):
            rows = pl.ds(pl.multiple_of(i * ch, 8), ch)
            prod = do_ref[rows, :] * o_ref[rows, :]
            d0 = jnp.sum(jnp.where(first, prod, 0.0), axis=-1, keepdims=True)
            d1 = jnp.sum(jnp.where(first, 0.0, prod), axis=-1, keepdims=True)
            dl_scr[rows, :] = jnp.where(first, d0, d1)
            zero = jnp.zeros((ch, PAIR), f32)
            dq_scr[rows, :] = zero
            dk_scr[rows, :] = zero
            dv_scr[rows, :] = zero
            return carry

        lax.fori_loop(0, S // ch, prep, 0)

        for b, (_, r) in enumerate(BRANCHES):
            L, nq, nk, nblk = _branch_geometry(S, r)

            def step(idx, carry, b=b, r=r, L=L, nq=nq, nk=nk, nblk=nblk):
                qrows, krows, off = _block_rows(idx, r, L, nq, nk, nblk)
                case = off // HALF_WIN
                q2 = q_ref[qrows, :] * SCALE
                k2 = k_ref[krows, :].astype(bf16)
                v2 = v_ref[krows, :].astype(bf16)
                do2 = do_ref[qrows, :]
                lse2 = lse_ref[qrows, :]
                dl2 = dl_scr[qrows, :]
                dq2 = jnp.zeros((nq, PAIR), f32)
                dk2 = jnp.zeros((nk, PAIR), f32)
                dv2 = jnp.zeros((nk, PAIR), f32)
                for hh in range(2):
                    mine = first if hh == 0 else ~first
                    lo = hh * HEAD_DIM
                    qh = jnp.where(mine, q2, 0.0).astype(bf16)
                    doh = jnp.where(mine, do2, 0.0).astype(bf16)
                    s = lax.dot_general(qh, k2, (((1,), (1,)), ((), ())), preferred_element_type=f32)
                    s = s + bias_scr[_bias_index(b, case, hh), 0:nq, 0:nk]
                    p = jnp.exp(s - lse2[:, lo:lo + 1])
                    dv2 = dv2 + lax.dot_general(p.astype(bf16), doh, (((0,), (0,)), ((), ())), preferred_element_type=f32)
                    dp = lax.dot_general(doh, v2, (((1,), (1,)), ((), ())), preferred_element_type=f32)
                    ds = (p * (dp - dl2[:, lo:lo + 1])).astype(bf16)
                    dq2 = dq2 + jnp.where(mine, jnp.dot(ds, k2, preferred_element_type=f32), 0.0)
                    dk2 = dk2 + lax.dot_general(ds, qh, (((0,), (0,)), ((), ())), preferred_element_type=f32)
                dq_scr[qrows, :] = dq_scr[qrows, :] + dq2
                dk_scr[krows, :] = dk_scr[krows, :] + dk2
                dv_scr[krows, :] = dv_scr[krows, :] + dv2
                return carry

            lax.fori_loop(0, S // nq, step, 0, unroll=min(ATTN_UNROLL, S // nq))

        dq_ref[...] = (dq_scr[...] * SCALE).astype(bf16)
        dk_ref[...] = dk_scr[...].astype(bf16)
        dv_ref[...] = dv_scr[...].astype(bf16)

    blk = lambda part: pl.BlockSpec((S, PAIR), lambda p: (0, part * npair + p))
    own = pl.BlockSpec((S, PAIR), lambda p: (0, p))
    grad = jax.ShapeDtypeStruct((S, C), bf16)
    return pl.pallas_call(
        body, name="attn_bwd", out_shape=(grad, grad, grad), grid=(npair,),
        in_specs=[blk(4), blk(5), blk(6), own, own, own, pl.BlockSpec((None, 8, PAIR), lambda p: (p, 0, 0)), ANY_SPEC],
        out_specs=(own, own, own),
        scratch_shapes=[pltpu.VMEM((S, PAIR), f32)] * 4 + [pltpu.VMEM(_bias_shape(S), f32)],
        compiler_params=_params("parallel"),
    )(proj, proj, proj, o, do, lse, slopes, dep)


def _prenorm_bwd(dh, x, dout, scale, g_pre, tr=256):
    S, D = x.shape
    tr = min(tr, S)

    def body(dh_ref, x_ref, dout_ref, sc_ref, g_ref, gx_ref, sums_ref):
        i = pl.program_id(0)
        xv, dhv = x_ref[...], dh_ref[...]
        r = lax.rsqrt(jnp.mean(xv * xv, axis=-1, keepdims=True) + EPS)
        xn = xv * r
        dxn = dhv * (g_ref[...] * (1.0 + sc_ref[...]))
        gx_ref[...] = dout_ref[...] + r * (dxn - xn * jnp.mean(dxn * xn, axis=-1, keepdims=True))
        dhx = dhv * xn
        row = lax.broadcasted_iota(jnp.int32, (8, D), 0)
        upd = jnp.where(row == 0, jnp.sum(dhv, axis=0, keepdims=True),
                        jnp.where(row == 1, jnp.sum(dhx, axis=0, keepdims=True) * g_ref[...],
                                  jnp.where(row == 2, jnp.sum(dhx, axis=0, keepdims=True) * (1.0 + sc_ref[...]), 0.0)))

        @pl.when(i == 0)
        def _():
            sums_ref[...] = upd

        @pl.when(i > 0)
        def _():
            sums_ref[...] += upd

    row = pl.BlockSpec((tr, D), lambda i: (i, 0))
    vec = pl.BlockSpec((1, D), lambda i: (0, 0))
    return pl.pallas_call(
        body, name="prenorm_bwd",
        out_shape=(jax.ShapeDtypeStruct((S, D), f32), jax.ShapeDtypeStruct((8, D), f32)), grid=(S // tr,),
        in_specs=[row, row, row, vec, vec], out_specs=(row, pl.BlockSpec((8, D), lambda i: (0, 0))),
        compiler_params=_params("arbitrary"),
    )(dh, x, dout, scale, g_pre)


def _adamw(w, g, m, v):
    m = ADAM_B1 * m + (1.0 - ADAM_B1) * g
    v = ADAM_B2 * v + (1.0 - ADAM_B2) * (g * g)
    m_hat = m / (1.0 - ADAM_B1 ** ADAM_STEP)
    v_hat = v / (1.0 - ADAM_B2 ** ADAM_STEP)
    delta = -ADAM_LR * (m_hat / (jnp.sqrt(v_hat) + ADAM_EPS) + ADAM_WD * w)
    return delta, m, v


def _sum_rows(parts, dep):
    P = parts.shape[1]

    def body(p_ref, dep_ref, o_ref):
        acc = p_ref[0:1, :]
        for j in range(1, NDEV):
            acc = acc + p_ref[j:j + 1, :]
        o_ref[...] = jnp.broadcast_to(acc, (8, P))

    vmem = pl.BlockSpec(memory_space=pltpu.VMEM)
    return pl.pallas_call(body, name="sum_small", out_shape=jax.ShapeDtypeStruct((8, P), f32),
                          in_specs=[vmem, ANY_SPEC], out_specs=vmem, compiler_params=_params())(parts, dep)


def _adamw_small(w, g, m, v):
    def body(w_ref, g_ref, m_ref, v_ref, d_ref, nm_ref, nv_ref):
        d_ref[...], nm_ref[...], nv_ref[...] = _adamw(w_ref[...], g_ref[...], m_ref[...], v_ref[...])

    out = jax.ShapeDtypeStruct(w.shape, f32)
    return pl.pallas_call(body, name="adamw_small", out_shape=(out, out, out), compiler_params=_params())(w, g, m, v)


def _adamw_sharded(parts, own, w, m, v, name, tr=128):
    R, Cc = w.shape
    tr = min(tr, R)
    n = parts.shape[0]

    def body(p_ref, own_ref, w_ref, m_ref, v_ref, g_ref, d_ref, nm_ref, nv_ref):
        g = own_ref[...].astype(f32)
        for j in range(n):
            g = g + p_ref[j].astype(f32)
        g_ref[...] = g
        d_ref[...], nm_ref[...], nv_ref[...] = _adamw(w_ref[...], g, m_ref[...], v_ref[...])

    row = pl.BlockSpec((tr, Cc), lambda i: (i, 0))
    out = jax.ShapeDtypeStruct((R, Cc), f32)
    return pl.pallas_call(
        body, name=name, out_shape=(out, out, out, out), grid=(R // tr,),
        in_specs=[pl.BlockSpec((n, tr, Cc), lambda i: (0, i, 0)), row, row, row, row],
        out_specs=(row, row, row, row), compiler_params=_params("parallel"),
    )(parts, own, w, m, v)


def _adamw_ada(c_t, dmod_cols, w, m, v, dep, tr=256):
    D, W = w.shape
    tr = min(tr, D)

    def body(c_ref, dm_ref, w_ref, m_ref, v_ref, dep_ref, g_ref, d_ref, nm_ref, nv_ref):
        cv, dm = c_ref[...], dm_ref[...]
        g = cv[:, 0:1] * dm[0:1, :]
        for b in range(1, NDEV):
            g = g + cv[:, b:b + 1] * dm[b:b + 1, :]
        g_ref[...] = g
        d_ref[...], nm_ref[...], nv_ref[...] = _adamw(w_ref[...], g, m_ref[...], v_ref[...])

    row = pl.BlockSpec((tr, W), lambda i: (i, 0))
    out = jax.ShapeDtypeStruct((D, W), f32)
    return pl.pallas_call(
        body, name="adamw_ada", out_shape=(out, out, out, out), grid=(D // tr,),
        in_specs=[pl.BlockSpec((tr, NDEV), lambda i: (i, 0)), pl.BlockSpec((NDEV, W), lambda i: (0, 0)), row, row, row,
                  ANY_SPEC],
        out_specs=(row, row, row, row), compiler_params=_params("parallel"),
    )(c_t, dmod_cols, w, m, v, dep)


def kernel(x, c, w_ada, b_ada, g_pre, w_in, conv_w, conv_b, g_conv, g_attn, w_out, g_post, loss_target, m_w_ada, m_b_ada, m_g_pre, m_w_in, m_conv_w, m_conv_b, m_g_conv, m_g_attn, m_w_out, m_g_post, v_w_ada, v_b_ada, v_g_pre, v_w_in, v_conv_w, v_conv_b, v_g_conv, v_g_attn, v_w_out, v_g_post):
    S, D = x.shape[1], x.shape[2]
    C = D // 2
    W = w_ada.shape[2]
    CW = conv_w.shape[2]
    me = 4 * lax.axis_index("x") + 2 * lax.axis_index("y") + lax.axis_index("c")
    x2, tgt = x[0], loss_target[0]
    w_ada2, w_in2, w_out2 = w_ada[0], w_in[0], w_out[0]

    R = D // NDEV
    core = lax.axis_index("c").astype(jnp.int32).reshape(1)

    cw_slab = jnp.zeros((8, CW), f32).at[:3].set(conv_w[0])
    c_blocks, cw_g = _all_gather([_silu_block(c.reshape(D // 128, 128)), cw_slab], "gather_c")
    c_all = c_blocks.reshape(NDEV, D)
    conv_w_full = jnp.transpose(cw_g, (1, 0, 2)).reshape(8, C)
    b_cols = lax.dynamic_slice_in_dim(b_ada, me * W, W, axis=1)
    (mod_slabs,) = _all_to_all([_ada_cols(c_all, w_ada2, b_cols)], "scatter_mod")
    mod = mod_slabs[:, 0, :].reshape(1, 3 * D)
    shift, scale, gate = mod[:, :D], mod[:, D:2 * D], mod[:, 2 * D:]

    land_i = lax.dynamic_update_slice(lax.empty((NDEV, D, C), bf16), w_in2.astype(bf16)[None], (me, 0, 0))
    land_o = lax.dynamic_update_slice(lax.empty((NDEV, R, D), bf16), w_out2.astype(bf16)[None], (me, 0, 0))
    wi_send, wi_recv, land_i, w_token = _w_in_start(land_i, [mod_slabs])

    me_arr = me.astype(jnp.int32).reshape(1)
    h = _prenorm(x2, scale, shift, g_pre, w_token)
    land_i = _w_in_sibling(land_i, wi_recv, after=[h])
    proj = _in_proj_part("in_proj_a", h, land_i, None, me_arr, 0, 1, 2)
    (fi_send, fi_recv, wo_send, wo_recv), land_i, land_o = _w_in_relay(land_i, land_o, wi_recv, after=[proj])
    proj = _in_proj_part("in_proj_b", h, land_i, proj, me_arr, 2, 2, 2)
    land_i = _w_in_forwarded(land_i, fi_recv, after=[proj])
    proj = _in_proj_part("in_proj_c", h, land_i, proj, me_arr, 3, 2, 2)
    di_send, di_recv, land_i = _w_in_diag(land_i, fi_recv, after=[proj])
    proj = _in_proj_part("in_proj_d", h, land_i, proj, me_arr, 6, 1, 1)
    win_g = _w_in_finish(land_i, wi_send, fi_send, di_send, di_recv, after=[proj])
    proj = _in_proj_part("in_proj_e", h, win_g, proj, me_arr, 7, 1, 1)
    (fo_send, fo_recv), (land_o,), _ = _weights_forward("w_out_forward", land_o, wo_recv, after=[proj])
    slopes = _head_slopes(C // HEAD_DIM)
    ycat = _conv_fwd(proj, conv_w_full, conv_b, g_conv)
    o, lse = _attn_fwd(proj, slopes)
    ycat = _attn_post(ycat, o, proj, g_attn)
    wout_g = _weights_wait("w_out_wait", land_o, wo_send, wo_recv, fo_send, fo_recv, after=[ycat])
    wout_full = wout_g.reshape(D, D)
    y = _matmul(ycat, wout_full, name="out_proj", out_dtype=f32)
    dy, dout, post_sums = _sandwich(y, x2, tgt, gate, g_post)

    loss = lax.psum(post_sums[2, 0], ("x", "y", "c"))
    gw_out = _matmul(ycat, dy, name="out_proj_dw", out_dtype=bf16, ta=True, dep=loss.reshape(1, 1)).reshape(NDEV, R, D)
    po_send, po_recv, gw_out, pair_o, po_token = _pair_start("g_out_pair_start", gw_out)
    dycat = _matmul(dy, wout_full, name="out_proj_dx", out_dtype=f32, tb=True, dep=po_token)
    gw_out, pair_o = _pair_wait("g_out_pair_wait", gw_out, pair_o, po_send, po_recv, after=[dycat])
    sum_o = _pair_sum("g_out_pair_sum", gw_out, pair_o, core)
    co_send, co_recv, sum_o, land_go, co_token = _chip_start(
        "g_out_chip_start", sum_o, jnp.zeros((NCHIP, R, D), bf16), 0)
    dpc, conv_sums = _conv_bwd(proj, dycat, conv_w_full, conv_b, g_conv, co_token)
    gw_c = _matmul(h, dpc, name="in_proj_dw_conv", out_dtype=bf16, ta=True, out_slots=4)
    pc_send, pc_recv, gw_c, pair_c, pc_token = _pair_start("g_conv_pair_start", gw_c)
    do, dza, attn_sums = _attn_post_bwd(o, proj, dycat, g_attn, pc_token)
    gw_c, pair_c = _pair_wait("g_conv_pair_wait", gw_c, pair_c, pc_send, pc_recv, after=[do])
    sum_c = _pair_sum("g_conv_pair_sum", gw_c, pair_c, core)
    cc_send, cc_recv, sum_c, land_gi, cc_token = _chip_start(
        "g_conv_chip_start", sum_c, jnp.zeros((NCHIP, D, C), bf16), 0)
    dq, dk, dv = _attn_bwd(proj, o, do, lse, slopes, cc_token)
    dproj = jnp.concatenate([dpc, dq, dk, dv, dza], axis=1)
    gw_a = _matmul(h, dproj, name="in_proj_dw_attn", out_dtype=bf16, ta=True, out_slots=4, b_cols=(4 * C, 4 * C))
    pa_send, pa_recv, gw_a, pair_a, pa_token = _pair_start("g_attn_pair_start", gw_a)
    gw_a, pair_a = _pair_wait("g_attn_pair_wait", gw_a, pair_a, pa_send, pa_recv, after=[pa_token])
    sum_a = _pair_sum("g_attn_pair_sum", gw_a, pair_a, core)
    part_a, part_b = (0, 3 * D // 4), (3 * D // 4, D // 4)
    ca_send, ca_recv, sum_a, land_gi, ca_token = _chip_start("g_attn_chip_start_a", sum_a, land_gi, 4, part_a)
    dh = _matmul_slabs_t(dproj, win_g, name="in_proj_dx", dep=ca_token)
    grad_x, pre_sums = _prenorm_bwd(dh, x2, dout, scale, g_pre)

    small = jnp.concatenate([pre_sums[0:1], pre_sums[1:2], post_sums[0:1],
                             pre_sums[2:3], post_sums[1:2],
                             conv_sums[2:3], conv_sums[3:4], conv_sums[4:5],
                             conv_sums[1:2], conv_sums[0:1], attn_sums[0:1]], axis=1)
    (small_all,) = _all_gather([small.reshape(8 * D // 128, 128)], "gather_small")
    cb_send, cb_recv, sum_a, land_gi, cb_token = _chip_start("g_attn_chip_start_b", sum_a, land_gi, 4, part_b,
                                                             after=[small_all])
    small_all = small_all.reshape(NDEV, 8 * D)
    tot = _sum_rows(small_all, cb_token)[0:1]

    g_b_ada = tot[:, :3 * D]
    g_g_pre, g_g_post = tot[:, 3 * D:4 * D], tot[:, 4 * D:5 * D]
    g_conv_w_full = tot[:, 5 * D:5 * D + 3 * C].reshape(3, C)
    g_conv_w = lax.dynamic_slice_in_dim(g_conv_w_full, me * CW, CW, axis=1)[None]
    g_conv_b, g_g_conv, g_g_attn = (tot[:, 5 * D + (3 + t) * C:5 * D + (4 + t) * C] for t in range(3))

    dmod_cols = lax.dynamic_slice_in_dim(small_all[:, :3 * D], me * W, W, axis=1)
    g_w_ada, d_w_ada, nm_w_ada, nv_w_ada = _adamw_ada(c_all.T, dmod_cols, w_ada2, m_w_ada[0], v_w_ada[0], cb_token)

    sum_o, land_go = _chip_wait("g_out_chip_wait", sum_o, land_go, co_send, co_recv, 0, after=[g_w_ada])
    own_out = lax.dynamic_index_in_dim(sum_o, me // 2, 0, keepdims=False)
    g_w_out, d_w_out, nm_w_out, nv_w_out = _adamw_sharded(land_go, own_out, w_out2, m_w_out[0], v_w_out[0], "adamw_w_out")
    sum_c, land_gi = _chip_wait("g_conv_chip_wait", sum_c, land_gi, cc_send, cc_recv, 0, after=[g_w_out])
    sum_a, land_gi = _chip_wait("g_attn_chip_wait_a", sum_a, land_gi, ca_send, ca_recv, 4, [g_w_out], part_a)
    sum_a, land_gi = _chip_wait("g_attn_chip_wait_b", sum_a, land_gi, cb_send, cb_recv, 4, [g_w_out], part_b)
    own_in = jnp.where(me < 4, lax.dynamic_index_in_dim(sum_c, (me % 4) // 2, 0, keepdims=False),
                       lax.dynamic_index_in_dim(sum_a, (me % 4) // 2, 0, keepdims=False))
    g_w_in, d_w_in, nm_w_in, nv_w_in = _adamw_sharded(land_gi, own_in, w_in2, m_w_in[0], v_w_in[0], "adamw_w_in")

    pack = lambda *vs: jnp.concatenate([a.reshape(1, -1) for a in vs], axis=1)
    smalls = [(b_ada, g_b_ada, m_b_ada, v_b_ada), (g_pre, g_g_pre, m_g_pre, v_g_pre),
              (conv_w, g_conv_w, m_conv_w, v_conv_w), (conv_b, g_conv_b, m_conv_b, v_conv_b),
              (g_conv, g_g_conv, m_g_conv, v_g_conv), (g_attn, g_g_attn, m_g_attn, v_g_attn),
              (g_post, g_g_post, m_g_post, v_g_post)]
    packed = [pack(*[s[t] for s in smalls]) for t in range(4)]
    npad = -packed[0].shape[1] % 128
    packed = [jnp.pad(p, ((0, 0), (0, npad)), constant_values=1.0) for p in packed]
    d_s, nm_s, nv_s = _adamw_small(*packed)

    def unpack(vec):
        out, at = [], 0
        for s in smalls:
            n = s[0].size
            out.append(vec[:, at:at + n].reshape(s[0].shape))
            at += n
        return out

    d_b_ada, d_g_pre, d_conv_w, d_conv_b, d_g_conv, d_g_attn, d_g_post = unpack(d_s)
    nm_b_ada, nm_g_pre, nm_conv_w, nm_conv_b, nm_g_conv, nm_g_attn, nm_g_post = unpack(nm_s)
    nv_b_ada, nv_g_pre, nv_conv_w, nv_conv_b, nv_g_conv, nv_g_attn, nv_g_post = unpack(nv_s)

    return (loss, grad_x[None],
            g_w_ada[None], g_b_ada, g_g_pre, g_w_in[None], g_conv_w, g_conv_b, g_g_conv, g_g_attn, g_w_out[None], g_g_post,
            d_w_ada[None], d_b_ada, d_g_pre, d_w_in[None], d_conv_w, d_conv_b, d_g_conv, d_g_attn, d_w_out[None], d_g_post,
            nm_w_ada[None], nm_b_ada, nm_g_pre, nm_w_in[None], nm_conv_w, nm_conv_b, nm_g_conv, nm_g_attn, nm_w_out[None], nm_g_post,
            nv_w_ada[None], nv_b_ada, nv_g_pre, nv_w_in[None], nv_conv_w, nv_conv_b, nv_g_conv, nv_g_attn, nv_w_out[None], nv_g_post)
```

```python
import functools

import jax
import jax.numpy as jnp
from jax import lax
from jax.experimental import pallas as pl
from jax.experimental.pallas import tpu as pltpu

f32 = jnp.float32
bf16 = jnp.bfloat16

NDEV = 8
HEAD_DIM = 64
PAIR = 2 * HEAD_DIM
BRANCHES = ((128, 1), (512, 4), (2048, 16))
HALF_WIN = 64
EPS = 1e-6
NEG_INF = -1e30
ADAM_LR, ADAM_B1, ADAM_B2, ADAM_EPS, ADAM_WD, ADAM_STEP = 0.001, 0.9, 0.999, 1e-08, 0.01, 10
MESH = pl.DeviceIdType.MESH
VMEM_LIMIT = 56 * 1024 * 1024
HBM_SPEC = pl.BlockSpec(memory_space=pltpu.HBM)
ANY_SPEC = pl.BlockSpec(memory_space=pl.ANY)
SEM_SPEC = pl.BlockSpec(memory_space=pltpu.SEMAPHORE)


def _params(*sem):
    return pltpu.CompilerParams(dimension_semantics=sem or None, vmem_limit_bytes=VMEM_LIMIT)


def _silu(z):
    return z * jax.nn.sigmoid(z)


def _dsilu(z):
    s = jax.nn.sigmoid(z)
    return s * (1.0 + z * (1.0 - s))


def _my_place():
    x, y, c = lax.axis_index("x"), lax.axis_index("y"), lax.axis_index("c")
    return x, y, c, 4 * x + 2 * y + c


def _peer(x, y, c, k):
    px, py, pc = x ^ (k >> 2 & 1), y ^ (k >> 1 & 1), c ^ (k & 1)
    return (px, py, pc), 4 * px + 2 * py + pc


def _all_gather(arrays, name):
    n = len(arrays)

    def body(*refs):
        srcs, dsts = refs[:n], refs[n:2 * n]
        send_sems, recv_sems, local_sems = refs[2 * n:]
        x, y, c, me = _my_place()
        locals_, sends = [], []
        for t in range(n):
            own = pltpu.make_async_copy(srcs[t], dsts[t].at[me], local_sems.at[t])
            own.start()
            locals_.append(own)
            for k in range(1, NDEV):
                peer, pidx = _peer(x, y, c, k)
                cp = pltpu.make_async_remote_copy(
                    src_ref=srcs[t], dst_ref=dsts[t].at[me], send_sem=send_sems.at[t, k],
                    recv_sem=recv_sems.at[t, k], device_id=peer, device_id_type=MESH)
                cp.start()
                sends.append(cp)
        for t in range(n):
            for k in range(1, NDEV):
                peer, pidx = _peer(x, y, c, k)
                pltpu.make_async_remote_copy(
                    src_ref=srcs[t], dst_ref=dsts[t].at[pidx], send_sem=send_sems.at[t, k],
                    recv_sem=recv_sems.at[t, k], device_id=peer, device_id_type=MESH).wait_recv()
        for cp in sends:
            cp.wait_send()
        for cp in locals_:
            cp.wait()

    return pl.pallas_call(
        body, name=name,
        out_shape=tuple(jax.ShapeDtypeStruct((NDEV,) + a.shape, a.dtype) for a in arrays),
        in_specs=[HBM_SPEC] * n, out_specs=tuple([HBM_SPEC] * n),
        scratch_shapes=[pltpu.SemaphoreType.DMA((n, NDEV)), pltpu.SemaphoreType.DMA((n, NDEV)),
                        pltpu.SemaphoreType.DMA((n,))],
    )(*arrays)


def _all_to_all(arrays, name):
    n = len(arrays)

    def body(*refs):
        srcs, dsts = refs[:n], refs[n:2 * n]
        send_sems, recv_sems, local_sems = refs[2 * n:]
        x, y, c, me = _my_place()
        locals_, sends = [], []
        for t in range(n):
            own = pltpu.make_async_copy(srcs[t].at[me], dsts[t].at[me], local_sems.at[t])
            own.start()
            locals_.append(own)
            for k in range(1, NDEV):
                peer, pidx = _peer(x, y, c, k)
                cp = pltpu.make_async_remote_copy(
                    src_ref=srcs[t].at[pidx], dst_ref=dsts[t].at[me], send_sem=send_sems.at[t, k],
                    recv_sem=recv_sems.at[t, k], device_id=peer, device_id_type=MESH)
                cp.start()
                sends.append(cp)
        for t in range(n):
            for k in range(1, NDEV):
                peer, pidx = _peer(x, y, c, k)
                pltpu.make_async_remote_copy(
                    src_ref=srcs[t].at[pidx], dst_ref=dsts[t].at[pidx], send_sem=send_sems.at[t, k],
                    recv_sem=recv_sems.at[t, k], device_id=peer, device_id_type=MESH).wait_recv()
        for cp in sends:
            cp.wait_send()
        for cp in locals_:
            cp.wait()

    return pl.pallas_call(
        body, name=name,
        out_shape=tuple(jax.ShapeDtypeStruct(a.shape, a.dtype) for a in arrays),
        in_specs=[HBM_SPEC] * n, out_specs=tuple([HBM_SPEC] * n),
        scratch_shapes=[pltpu.SemaphoreType.DMA((n, NDEV)), pltpu.SemaphoreType.DMA((n, NDEV)),
                        pltpu.SemaphoreType.DMA((n,))],
    )(*arrays)


def _comm_call(name, arrays, sems, new_sems, body, after=(), token=False):
    na, ns, nn, nf = len(arrays), len(sems), len(new_sems), len(after)

    def kern(*refs):
        ins, outs = refs[:na + ns + nf], refs[na + ns + nf:]
        body(ins[:na], ins[na:na + ns], outs[:nn])
        if token:
            outs[nn + na][...] = jnp.zeros((8, 128), f32)

    out_shape = ([pltpu.SemaphoreType.DMA(s) for s in new_sems] + [pltpu.HBM(a.shape, a.dtype) for a in arrays]
                 + ([jax.ShapeDtypeStruct((8, 128), f32)] if token else []))
    out_specs = [SEM_SPEC] * nn + [HBM_SPEC] * na + ([pl.BlockSpec(memory_space=pltpu.VMEM)] if token else [])
    res = pl.pallas_call(
        kern, name=name, out_shape=tuple(out_shape),
        in_specs=[HBM_SPEC] * na + [SEM_SPEC] * ns + [ANY_SPEC] * nf, out_specs=tuple(out_specs),
        input_output_aliases={t: nn + t for t in range(na)},
        compiler_params=pltpu.CompilerParams(has_side_effects=pltpu.SideEffectType.DATAFLOW_SIDE_EFFECTING),
    )(*[pltpu.with_memory_space_constraint(a, pltpu.HBM) for a in arrays], *sems, *after)
    return list(res[:nn]), list(res[nn:nn + na]), (res[nn + na] if token else None)


def _remote(src, dst, send_sem, recv_sem, device):
    return pltpu.make_async_remote_copy(src_ref=src, dst_ref=dst, send_sem=send_sem, recv_sem=recv_sem,
                                        device_id=device, device_id_type=MESH)


SAME_CORE = (2, 4, 6)
VIA_SIBLING = (3, 5, 7)


def _weights_forward(name, land, recv, after):
    def body(a, s, new):
        (land,), (recv,), (fsend, frecv) = a, s, new
        x, y, c, me = _my_place()
        sibling, _ = _peer(x, y, c, 1)
        for k in SAME_CORE:
            peer, slot = _peer(x, y, c, k)
            _remote(land.at[slot], land.at[slot], fsend.at[k], recv.at[k], peer).wait_recv()
            _remote(land.at[slot], land.at[slot], fsend.at[k], frecv.at[k ^ 1], sibling).start()

    return _comm_call(name, [land], [recv], [(NDEV,), (NDEV,)], body, after=after)


def _weights_wait(name, land, send, recv, fsend, frecv, after):
    def body(a, s, new):
        (land,), (send, recv, fsend, frecv) = a, s
        x, y, c, me = _my_place()
        sibling, sib_slot = _peer(x, y, c, 1)
        _remote(land.at[sib_slot], land.at[sib_slot], send.at[1], recv.at[1], sibling).wait_recv()
        for k in VIA_SIBLING:
            _, slot = _peer(x, y, c, k)
            _remote(land.at[slot], land.at[slot], fsend.at[k ^ 1], frecv.at[k], sibling).wait_recv()
        for k in (1,) + SAME_CORE:
            peer, _ = _peer(x, y, c, k)
            _remote(land.at[me], land.at[me], send.at[k], recv.at[k], peer).wait_send()
        for k in SAME_CORE:
            _, slot = _peer(x, y, c, k)
            _remote(land.at[slot], land.at[slot], fsend.at[k], frecv.at[k ^ 1], sibling).wait_send()

    return _comm_call(name, [land], [send, recv, fsend, frecv], [], body, after=after)[1][0]


def _diag_relay(x, y, c):
    slot = 4 * (x ^ (1 - c)) + 2 * (y ^ c) + c
    return slot, (x ^ c, y ^ (1 - c), c)


def _w_in_start(land, after):
    def body(a, s, new):
        (land,), (send, recv) = a, new
        x, y, c, me = _my_place()
        for k in (1, 2, 4):
            peer, _ = _peer(x, y, c, k)
            _remote(land.at[me], land.at[me], send.at[k], recv.at[k], peer).start()

    (send, recv), (land,), token = _comm_call("w_in_start", [land], [], [(NDEV,), (NDEV,)], body, after=after, token=True)
    return send, recv, land, token


def _w_in_sibling(land, recv, after):
    def body(a, s, new):
        (land,), (recv,) = a, s
        x, y, c, me = _my_place()
        sibling, slot = _peer(x, y, c, 1)
        _remote(land.at[slot], land.at[slot], recv.at[1], recv.at[1], sibling).wait_recv()

    return _comm_call("w_in_sibling", [land], [recv], [], body, after=after)[1][0]


def _w_in_relay(land, recv, after):
    def body(a, s, new):
        (land,), (recv,), (fsend, frecv) = a, s, new
        x, y, c, me = _my_place()
        sibling, _ = _peer(x, y, c, 1)
        for k in (2, 4):
            peer, slot = _peer(x, y, c, k)
            _remote(land.at[slot], land.at[slot], fsend.at[k], recv.at[k], peer).wait_recv()
        slot, target = _diag_relay(x, y, c)
        _remote(land.at[slot], land.at[slot], fsend.at[6], frecv.at[6], target).start()
        for k in (2, 4):
            _, slot = _peer(x, y, c, k)
            _remote(land.at[slot], land.at[slot], fsend.at[k], frecv.at[k ^ 1], sibling).start()

    (fsend, frecv), (land,), _ = _comm_call("w_in_relay", [land], [recv], [(NDEV,), (NDEV,)], body, after=after)
    return fsend, frecv, land


def _w_in_forwarded(land, frecv, after):
    def body(a, s, new):
        (land,), (frecv,) = a, s
        x, y, c, me = _my_place()
        sibling, _ = _peer(x, y, c, 1)
        for k in (3, 5):
            _, slot = _peer(x, y, c, k)
            _remote(land.at[slot], land.at[slot], frecv.at[k], frecv.at[k], sibling).wait_recv()

    return _comm_call("w_in_forwarded", [land], [frecv], [], body, after=after)[1][0]


def _w_in_diag(land, land_o, frecv, after):
    def body(a, s, new):
        (land, land_o), (frecv,), (dsend, drecv, osend, orecv) = a, s, new
        x, y, c, me = _my_place()
        sibling, _ = _peer(x, y, c, 1)
        peer, slot = _peer(x, y, c, 6)
        _remote(land.at[slot], land.at[slot], dsend.at[6], frecv.at[6], peer).wait_recv()
        _remote(land.at[slot], land.at[slot], dsend.at[6], drecv.at[7], sibling).start()
        for k in (1,) + SAME_CORE:
            peer, _ = _peer(x, y, c, k)
            _remote(land_o.at[me], land_o.at[me], osend.at[k], orecv.at[k], peer).start()

    sems, (land, land_o), _ = _comm_call("w_in_diag", [land, land_o], [frecv], [(NDEV,)] * 4, body, after=after)
    return sems, land, land_o


def _w_in_finish(land, send, fsend, dsend, drecv, after):
    def body(a, s, new):
        (land,), (send, fsend, dsend, drecv) = a, s
        x, y, c, me = _my_place()
        sibling, _ = _peer(x, y, c, 1)
        _, slot = _peer(x, y, c, 7)
        _remote(land.at[slot], land.at[slot], dsend.at[6], drecv.at[7], sibling).wait_recv()
        for k in (1, 2, 4):
            peer, _ = _peer(x, y, c, k)
            _remote(land.at[me], land.at[me], send.at[k], send.at[k], peer).wait_send()
        for k in (2, 4, 6):
            _, slot = _peer(x, y, c, k)
            _remote(land.at[slot], land.at[slot], fsend.at[k], fsend.at[k], sibling).wait_send()
        _, slot = _peer(x, y, c, 6)
        _remote(land.at[slot], land.at[slot], dsend.at[6], dsend.at[6], sibling).wait_send()

    return _comm_call("w_in_finish", [land], [send, fsend, dsend, drecv], [], body, after=after)[1][0]


def _in_proj_part(name, h, land, proj, me_arr, k0, kstep, nk, tm=512):
    S, D = h.shape
    C = land.shape[2]
    tm = min(tm, S)

    def body(me_ref, a_ref, b_ref, *rest):
        rest[-1][...] = jnp.dot(a_ref[...], b_ref[...], preferred_element_type=f32)

    slot = lambda j, me: me[0] ^ (k0 + kstep * j)
    args = [h, land] + ([] if proj is None else [proj])
    grid_spec = pltpu.PrefetchScalarGridSpec(
        num_scalar_prefetch=1, grid=(nk, S // tm),
        in_specs=[pl.BlockSpec((tm, D), lambda j, i, me: (i, 0)),
                  pl.BlockSpec((None, D, C), lambda j, i, me: (slot(j, me), 0, 0))] + [ANY_SPEC] * (len(args) - 2),
        out_specs=pl.BlockSpec((tm, C), lambda j, i, me: (i, slot(j, me))))
    return pl.pallas_call(
        body, name=name, out_shape=jax.ShapeDtypeStruct((S, NDEV * C), f32), grid_spec=grid_spec,
        input_output_aliases={} if proj is None else {3: 0}, compiler_params=_params("arbitrary", "arbitrary"),
    )(me_arr, *args)


NCHIP = NDEV // 2


def _pair_start(name, src):
    npair = src.shape[0] // 2

    def body(a, s, new):
        (src, pair), (send, recv) = a, new
        x, y, c, me = _my_place()
        sibling, _ = _peer(x, y, c, 1)
        for i in range(npair):
            _remote(src.at[2 * i + 1 - c], pair.at[i], send.at[i], recv.at[i], sibling).start()

    pair = lax.empty((npair,) + src.shape[1:], src.dtype)
    (send, recv), (src, pair), token = _comm_call(name, [src, pair], [], [(npair,), (npair,)], body, token=True)
    return send, recv, src, pair, token


def _pair_wait(name, src, pair, send, recv, after):
    npair = pair.shape[0]

    def body(a, s, new):
        (src, pair), (send, recv) = a, s
        x, y, c, me = _my_place()
        sibling, _ = _peer(x, y, c, 1)
        for i in range(npair):
            cp = _remote(src.at[2 * i + 1 - c], pair.at[i], send.at[i], recv.at[i], sibling)
            cp.wait_recv()
            cp.wait_send()

    return _comm_call(name, [src, pair], [send, recv], [], body, after=after)[1]


def _pair_sum(name, src, pair, core, tr=256):
    npair, R, Cc = pair.shape
    tr = min(tr, R)

    def body(core_ref, a_ref, b_ref, o_ref):
        o_ref[...] = (a_ref[...].astype(f32) + b_ref[...].astype(f32)).astype(o_ref.dtype)

    grid_spec = pltpu.PrefetchScalarGridSpec(
        num_scalar_prefetch=1, grid=(npair, R // tr),
        in_specs=[pl.BlockSpec((None, tr, Cc), lambda i, r, core: (2 * i + core[0], r, 0)),
                  pl.BlockSpec((None, tr, Cc), lambda i, r, core: (i, r, 0))],
        out_specs=pl.BlockSpec((None, tr, Cc), lambda i, r, core: (i, r, 0)))
    return pl.pallas_call(body, name=name, out_shape=jax.ShapeDtypeStruct(pair.shape, pair.dtype),
                          grid_spec=grid_spec, compiler_params=_params("parallel", "parallel"))(core, src, pair)


def _owner_chip(first, i):
    q = first // 2 + i
    return q >> 1 & 1, q & 1


def _chip_start(name, sums, land, first, rows=None, after=()):
    npair = sums.shape[0]
    rows = pl.ds(*(rows or (0, sums.shape[1])))

    def body(a, s, new):
        (sums, land), (send, recv) = a, new
        x, y, c, me = _my_place()
        for i in range(npair):
            ox, oy = _owner_chip(first, i)

            @pl.when((x != ox) | (y != oy))
            def _():
                _remote(sums.at[i, rows], land.at[2 * x + y, rows], send.at[i], recv.at[2 * x + y], (ox, oy, c)).start()

    (send, recv), (sums, land), token = _comm_call(name, [sums, land], [], [(npair,), (NCHIP,)], body, after=after,
                                                   token=True)
    return send, recv, sums, land, token


def _chip_wait(name, sums, land, send, recv, first, after, rows=None):
    npair = sums.shape[0]
    rows = pl.ds(*(rows or (0, sums.shape[1])))

    def body(a, s, new):
        (sums, land), (send, recv) = a, s
        x, y, c, me = _my_place()
        mine = (me >= first) & (me < first + 2 * npair)
        for i in range(npair):
            ox, oy = _owner_chip(first, i)

            @pl.when((x != ox) | (y != oy))
            def _():
                _remote(sums.at[i, rows], land.at[2 * x + y, rows], send.at[i], recv.at[2 * x + y], (ox, oy, c)).wait_send()
        for q in range(NCHIP):
            @pl.when(mine & (2 * x + y != q))
            def _():
                _remote(sums.at[0, rows], land.at[q, rows], send.at[0], recv.at[q], (q >> 1, q & 1, c)).wait_recv()

    return _comm_call(name, [sums, land], [send, recv], [], body, after=after)[1]


def _matmul(a, b, *, name, out_dtype, ta=False, tb=False, b_slots=False, out_slots=0, b_cols=None,
            tm=1024, tn=1024, tk=2048, dep=None):
    M, K = (a.shape[1], a.shape[0]) if ta else a.shape
    col0 = 0
    if b_slots:
        slab = b.shape[2]
        N = b.shape[1] if tb else b.shape[0] * slab
        assert (K if tb else N) == b.shape[0] * slab
    elif b_cols is not None:
        assert not tb
        col0, N = b_cols
    else:
        N = b.shape[0] if tb else b.shape[1]
    tm, tn, tk = min(tm, M), min(tn, N), min(tk, K)
    if b_slots:
        if tb:
            tk = min(tk, slab)
        else:
            tn = min(tn, slab)
    if out_slots:
        tn = min(tn, N // out_slots)
    nm, nn, nk = M // tm, N // tn, K // tk
    assert (nm * tm, nn * tn, nk * tk) == (M, N, K) and col0 % tn == 0, (name, M, N, K, tm, tn, tk)
    j0 = col0 // tn

    a_spec = pl.BlockSpec((tk, tm), lambda i, j, k: (k, i)) if ta else pl.BlockSpec((tm, tk), lambda i, j, k: (i, k))
    if b_slots and tb:
        per = slab // tk
        b_spec = pl.BlockSpec((None, tn, tk), lambda i, j, k: (k // per, j, k % per))
    elif b_slots:
        per = slab // tn
        b_spec = pl.BlockSpec((None, tk, tn), lambda i, j, k: (j // per, k, j % per))
    elif tb:
        b_spec = pl.BlockSpec((tn, tk), lambda i, j, k: (j, k))
    else:
        b_spec = pl.BlockSpec((tk, tn), lambda i, j, k: (k, j + j0))
    if out_slots:
        per_o = (N // out_slots) // tn
        o_spec = pl.BlockSpec((None, tm, tn), lambda i, j, k: (j // per_o, i, j % per_o))
        out_shape = jax.ShapeDtypeStruct((out_slots, M, N // out_slots), out_dtype)
    else:
        o_spec = pl.BlockSpec((tm, tn), lambda i, j, k: (i, j))
        out_shape = jax.ShapeDtypeStruct((M, N), out_dtype)
    dims = (((0 if ta else 1,), (1 if tb else 0,)), ((), ()))
    deps = [] if dep is None else [dep]

    def body(a_ref, b_ref, *rest):
        o_ref = rest[len(deps)]
        prod = lax.dot_general(a_ref[...], b_ref[...], dims, preferred_element_type=f32)
        if nk == 1:
            o_ref[...] = prod.astype(out_dtype)
            return
        acc_ref = rest[len(deps) + 1]
        k = pl.program_id(2)

        @pl.when(k == 0)
        def _():
            acc_ref[...] = prod

        @pl.when((k > 0) & (k < nk - 1))
        def _():
            acc_ref[...] += prod

        @pl.when(k == nk - 1)
        def _():
            o_ref[...] = (acc_ref[...] + prod).astype(out_dtype)

    return pl.pallas_call(
        body, name=name, out_shape=out_shape, grid=(nm, nn, nk),
        in_specs=[a_spec, b_spec] + [ANY_SPEC] * len(deps), out_specs=o_spec,
        scratch_shapes=[pltpu.VMEM((tm, tn), f32)] if nk > 1 else [],
        compiler_params=_params("parallel", "parallel", "arbitrary"),
    )(a, b, *deps)


def _matmul_slabs_t(a_cols, a_slots, b, *, name, tm=512, tn=512, dep=None):
    M = a_cols.shape[0]
    n_slab, N, slab = b.shape
    n1, n2 = a_cols.shape[1] // slab, a_slots.shape[0]
    assert n1 + n2 == n_slab and a_slots.shape[1:] == (M, slab)
    tm, tn = min(tm, M), min(tn, N)
    deps = [] if dep is None else [dep]

    def body(a1_ref, a2_ref, b_ref, *rest):
        o_ref = rest[len(deps)]
        acc = None
        for s in range(n_slab):
            lhs = a1_ref[:, s * slab:(s + 1) * slab] if s < n1 else a2_ref[s - n1]
            prod = lax.dot_general(lhs, b_ref[s], (((1,), (1,)), ((), ())), preferred_element_type=f32)
            acc = prod if acc is None else acc + prod
        o_ref[...] = acc

    return pl.pallas_call(
        body, name=name, out_shape=jax.ShapeDtypeStruct((M, N), f32), grid=(M // tm, N // tn),
        in_specs=[pl.BlockSpec((tm, n1 * slab), lambda i, j: (i, 0)), pl.BlockSpec((n2, tm, slab), lambda i, j: (0, i, 0)),
                  pl.BlockSpec((n_slab, tn, slab), lambda i, j: (0, j, 0))] + [ANY_SPEC] * len(deps),
        out_specs=pl.BlockSpec((tm, tn), lambda i, j: (i, j)), compiler_params=_params("parallel", "parallel"),
    )(a_cols, a_slots, b, *deps)


def _silu_block(c):
    def body(c_ref, o_ref):
        o_ref[...] = _silu(c_ref[...])

    return pl.pallas_call(body, name="silu_c", out_shape=jax.ShapeDtypeStruct(c.shape, f32))(c)


def _ada_cols(c_all, w_ada, b_cols):
    D, W = w_ada.shape

    def body(c_ref, w_ref, b_ref, o_ref):
        mod = lax.dot_general(c_ref[...], w_ref[...], (((1,), (0,)), ((), ())), preferred_element_type=f32,
                              precision=lax.Precision.HIGHEST) + b_ref[...]
        for j in range(NDEV):
            o_ref[j] = jnp.broadcast_to(mod[j:j + 1, :], (8, W))

    return pl.pallas_call(body, name="ada_cols", out_shape=jax.ShapeDtypeStruct((NDEV, 8, W), f32),
                          compiler_params=_params())(c_all, w_ada, b_cols)


def _prenorm(x, scale, shift, g_pre, dep, tr=256):
    S, D = x.shape
    tr = min(tr, S)

    def body(x_ref, sc_ref, sh_ref, g_ref, dep_ref, h_ref):
        xv = x_ref[...]
        r = lax.rsqrt(jnp.mean(xv * xv, axis=-1, keepdims=True) + EPS)
        h_ref[...] = ((xv * r) * g_ref[...] * (1.0 + sc_ref[...]) + sh_ref[...]).astype(bf16)

    row = pl.BlockSpec((tr, D), lambda i: (i, 0))
    vec = pl.BlockSpec((1, D), lambda i: (0, 0))
    return pl.pallas_call(body, name="prenorm", out_shape=jax.ShapeDtypeStruct((S, D), bf16), grid=(S // tr,),
                          in_specs=[row, vec, vec, vec, ANY_SPEC], out_specs=row, compiler_params=_params("parallel"))(
                              x, scale, shift, g_pre, dep)


def _ext_rows(i, tr, S):
    g = lax.broadcasted_iota(jnp.int32, (tr + 16, 1), 0) + (i * tr - 8)
    return (g >= 0) & (g < S)


def _halo_specs(tr, S, C, col):
    nb8 = S // 8
    main = pl.BlockSpec((tr, C), lambda i: (i, col))
    prev = pl.BlockSpec((8, C), lambda i: (jnp.maximum(i * (tr // 8) - 1, 0), col))
    nxt = pl.BlockSpec((8, C), lambda i: (jnp.minimum((i + 1) * (tr // 8), nb8 - 1), col))
    return prev, main, nxt


def _conv_fwd(proj, conv_w, conv_b, g_conv, tr=256):
    S, C = proj.shape[0], proj.shape[1] // 8
    tr = min(tr, S)

    def body(up, um, un, cp, cm, cn, bg_ref, zc_ref, w_ref, cb_ref, g_ref, o_ref):
        i = pl.program_id(0)
        exists = _ext_rows(i, tr, S)
        u = jnp.concatenate([up[...], um[...], un[...]], axis=0)
        cg = jnp.concatenate([cp[...], cm[...], cn[...]], axis=0)
        t = jnp.where(exists, cg * u, 0.0)
        t_before = pltpu.roll(t, 1, 0)[8:tr + 8]
        t_after = pltpu.roll(t, tr + 15, 0)[8:tr + 8]
        w = w_ref[...]
        cv = w[0:1] * t_before + w[1:2] * t[8:tr + 8] + w[2:3] * t_after + cb_ref[...]
        yc = bg_ref[...] * cv
        rc = lax.rsqrt(jnp.mean(yc * yc, axis=-1, keepdims=True) + EPS)
        o_ref[...] = ((yc * rc) * g_ref[...] * _silu(zc_ref[...])).astype(bf16)

    u_specs = _halo_specs(tr, S, C, 0)
    c_specs = _halo_specs(tr, S, C, 2)
    vec = pl.BlockSpec((1, C), lambda i: (0, 0))
    return pl.pallas_call(
        body, name="conv_fwd", out_shape=jax.ShapeDtypeStruct((S, 2 * C), bf16), grid=(S // tr,),
        in_specs=[*u_specs, *c_specs, pl.BlockSpec((tr, C), lambda i: (i, 1)), pl.BlockSpec((tr, C), lambda i: (i, 3)),
                  pl.BlockSpec((8, C), lambda i: (0, 0)), vec, vec],
        out_specs=pl.BlockSpec((tr, C), lambda i: (i, 0)), compiler_params=_params("parallel"),
    )(proj, proj, proj, proj, proj, proj, proj, proj, conv_w, conv_b, g_conv)


def _branch_geometry(S, r):
    L = S // r
    nq = min(128, L)
    nk = min(nq + 2 * HALF_WIN, L)
    return L, nq, nk, L // nq


def _block_rows(idx, r, L, nq, nk, nblk):
    if r == 1:
        rho, qb = 0, idx
    else:
        rho, qb = idx // nblk, idx % nblk
    i0 = qb * nq
    ws = jnp.clip(i0 - HALF_WIN, 0, L - nk)
    if r == 1:
        qrows = pl.ds(pl.multiple_of(i0, 8), nq)
        krows = pl.ds(pl.multiple_of(ws, 8), nk)
    else:
        qrows = pl.ds(rho + r * i0, nq, stride=r)
        krows = pl.ds(rho + r * ws, nk, stride=r)
    return qrows, krows, i0 - ws


N_CASES = 3
SCALE = HEAD_DIM ** -0.5
ATTN_UNROLL = 8


def _bias_shape(S):
    return (len(BRANCHES) * N_CASES * 2, min(128, S), min(128 + 2 * HALF_WIN, S))


def _bias_index(b, case, head):
    return (b * N_CASES + case) * 2 + head


def _fill_bias(bias_scr, sl_ref, S):
    sl = sl_ref[...]
    slope = (sl[0:1, 0:1], sl[0:1, HEAD_DIM:HEAD_DIM + 1])
    for b, (_, r) in enumerate(BRANCHES):
        L, nq, nk, nblk = _branch_geometry(S, r)
        assert nblk == 1 or L >= nq + 2 * HALF_WIN
        rel = lax.broadcasted_iota(jnp.int32, (nq, nk), 0) - lax.broadcasted_iota(jnp.int32, (nq, nk), 1)
        for case in range(N_CASES):
            d = jnp.abs(rel + case * HALF_WIN)
            dist = d.astype(f32) * float(r)
            for head in range(2):
                bias_scr[_bias_index(b, case, head), 0:nq, 0:nk] = jnp.where(d <= HALF_WIN, -slope[head] * dist, NEG_INF)


def _head_slopes(n_heads):
    slopes = 2.0 ** (-8.0 * jnp.arange(1, n_heads + 1, dtype=f32) / n_heads)
    return jnp.broadcast_to(jnp.repeat(slopes.reshape(n_heads // 2, 2), HEAD_DIM, axis=1)[:, None, :],
                            (n_heads // 2, 8, PAIR))


def _attn_fwd(proj, slopes):
    S, C = proj.shape[0], proj.shape[1] // 8
    npair = C // PAIR

    def body(q_ref, k_ref, v_ref, sl_ref, o_ref, lse_ref, m_scr, l_scr, a_scr, bias_scr):
        lane = lax.broadcasted_iota(jnp.int32, (1, PAIR), 1)
        first = lane < HEAD_DIM
        _fill_bias(bias_scr, sl_ref, S)

        for b, (_, r) in enumerate(BRANCHES):
            L, nq, nk, nblk = _branch_geometry(S, r)

            def step(idx, carry, b=b, r=r, L=L, nq=nq, nk=nk, nblk=nblk):
                qrows, krows, off = _block_rows(idx, r, L, nq, nk, nblk)
                case = off // HALF_WIN
                q2 = q_ref[qrows, :] * SCALE
                k2 = k_ref[krows, :].astype(bf16)
                v2 = v_ref[krows, :].astype(bf16)
                ms, accs = [], []
                for hh in range(2):
                    mine = first if hh == 0 else ~first
                    qh = jnp.where(mine, q2, 0.0).astype(bf16)
                    s = lax.dot_general(qh, k2, (((1,), (1,)), ((), ())), preferred_element_type=f32)
                    s = s + bias_scr[_bias_index(b, case, hh), 0:nq, 0:nk]
                    m = jnp.max(s, axis=-1, keepdims=True)
                    p = jnp.exp(s - m).astype(bf16)
                    vh = jnp.where(mine, v2, jnp.ones_like(v2))
                    ms.append(m)
                    accs.append(jnp.dot(p, vh, preferred_element_type=f32))
                m_scr[b, qrows, :] = jnp.where(first, ms[0], ms[1])
                a_scr[b, qrows, :] = jnp.where(first, accs[0], accs[1])
                l_scr[b, qrows, :] = jnp.where(first, accs[1], accs[0])
                return carry

            lax.fori_loop(0, S // nq, step, 0, unroll=min(ATTN_UNROLL, S // nq))

        ch = min(256, S)

        def merge(i, carry):
            rows = pl.ds(pl.multiple_of(i * ch, 8), ch)
            m = jnp.maximum(jnp.maximum(m_scr[0, rows, :], m_scr[1, rows, :]), m_scr[2, rows, :])
            l = jnp.zeros((ch, PAIR), f32)
            acc = jnp.zeros((ch, PAIR), f32)
            for b in range(3):
                w = jnp.exp(m_scr[b, rows, :] - m)
                l = l + w * pltpu.roll(l_scr[b, rows, :], HEAD_DIM, 1)
                acc = acc + w * a_scr[b, rows, :]
            o_ref[rows, :] = acc / l
            lse_ref[rows, :] = m + jnp.log(l)
            return carry

        lax.fori_loop(0, S // ch, merge, 0)

    blk = lambda part: pl.BlockSpec((S, PAIR), lambda p: (0, part * npair + p))
    out = pl.BlockSpec((S, PAIR), lambda p: (0, p))
    return pl.pallas_call(
        body, name="attn_fwd",
        out_shape=(jax.ShapeDtypeStruct((S, C), f32), jax.ShapeDtypeStruct((S, C), f32)), grid=(npair,),
        in_specs=[blk(4), blk(5), blk(6), pl.BlockSpec((None, 8, PAIR), lambda p: (p, 0, 0))],
        out_specs=(out, out),
        scratch_shapes=[pltpu.VMEM((3, S, PAIR), f32)] * 3 + [pltpu.VMEM(_bias_shape(S), f32)],
        compiler_params=_params("parallel"),
    )(proj, proj, proj, slopes)


def _attn_post(ycat, o, proj, g_attn, tr=256):
    S, C = o.shape
    tr = min(tr, S)

    def body(y_ref, o_ref, z_ref, g_ref, out_ref):
        del y_ref
        ov = o_ref[...]
        ra = lax.rsqrt(jnp.mean(ov * ov, axis=-1, keepdims=True) + EPS)
        out_ref[...] = ((ov * ra) * g_ref[...] * _silu(z_ref[...])).astype(bf16)

    return pl.pallas_call(
        body, name="attn_post", out_shape=jax.ShapeDtypeStruct(ycat.shape, ycat.dtype), grid=(S // tr,),
        in_specs=[HBM_SPEC, pl.BlockSpec((tr, C), lambda i: (i, 0)), pl.BlockSpec((tr, C), lambda i: (i, 7)),
                  pl.BlockSpec((1, C), lambda i: (0, 0))],
        out_specs=pl.BlockSpec((tr, C), lambda i: (i, 1)), input_output_aliases={0: 0},
        compiler_params=_params("arbitrary"),
    )(ycat, o, proj, g_attn)


def _sandwich(y, x, target, gate, g_post, tr=256):
    S, D = y.shape
    tr = min(tr, S)

    def body(y_ref, x_ref, t_ref, gate_ref, g_ref, dy_ref, dout_ref, sums_ref):
        i = pl.program_id(0)
        yv = y_ref[...]
        rp = lax.rsqrt(jnp.mean(yv * yv, axis=-1, keepdims=True) + EPS)
        yhat = yv * rp
        yn = yhat * g_ref[...]
        err = (x_ref[...] + gate_ref[...] * yn) - t_ref[...]
        dout = err * (1.0 / D)
        dout_ref[...] = dout
        dyn = dout * gate_ref[...]
        w = dyn * g_ref[...]
        dy_ref[...] = (rp * (w - yhat * jnp.mean(w * yhat, axis=-1, keepdims=True))).astype(bf16)
        loss = 0.5 * jnp.sum(jnp.mean(err * err, axis=-1, keepdims=True), axis=0, keepdims=True)
        row = lax.broadcasted_iota(jnp.int32, (8, D), 0)
        upd = jnp.where(row == 0, jnp.sum(dout * yn, axis=0, keepdims=True),
                        jnp.where(row == 1, jnp.sum(dyn * yhat, axis=0, keepdims=True),
                                  jnp.where(row == 2, loss, 0.0)))

        @pl.when(i == 0)
        def _():
            sums_ref[...] = upd

        @pl.when(i > 0)
        def _():
            sums_ref[...] += upd

    row = pl.BlockSpec((tr, D), lambda i: (i, 0))
    vec = pl.BlockSpec((1, D), lambda i: (0, 0))
    return pl.pallas_call(
        body, name="sandwich",
        out_shape=(jax.ShapeDtypeStruct((S, D), bf16), jax.ShapeDtypeStruct((S, D), f32), jax.ShapeDtypeStruct((8, D), f32)),
        grid=(S // tr,), in_specs=[row, row, row, vec, vec],
        out_specs=(row, row, pl.BlockSpec((8, D), lambda i: (0, 0))), compiler_params=_params("arbitrary"),
    )(y, x, target, gate, g_post)


def _conv_bwd(proj, dycat, conv_w, conv_b, g_conv, dep, tr=256):
    S, C = proj.shape[0], proj.shape[1] // 8
    tr = min(tr, S)
    n = tr + 16

    def body(*refs):
        ins, (w_ref, cb_ref, g_ref, _, dp_ref, sums_ref) = refs[:15], refs[15:]
        i = pl.program_id(0)
        exists = _ext_rows(i, tr, S)
        u, bg, cg, zc, dyn = (jnp.concatenate([ins[3 * t][...], ins[3 * t + 1][...], ins[3 * t + 2][...]], axis=0)
                              for t in range(5))
        w = w_ref[...]
        t = jnp.where(exists, cg * u, 0.0)
        t_before, t_after = pltpu.roll(t, 1, 0), pltpu.roll(t, n - 1, 0)
        cv = w[0:1] * t_before + w[1:2] * t + w[2:3] * t_after + cb_ref[...]
        yc = bg * cv
        rc = lax.rsqrt(jnp.mean(yc * yc, axis=-1, keepdims=True) + EPS)
        yhat = yc * rc
        sz = _silu(zc)
        wgt = dyn * g_ref[...] * sz
        dyc = rc * (wgt - yhat * jnp.mean(wgt * yhat, axis=-1, keepdims=True))
        dcv = jnp.where(exists, dyc * bg, 0.0)
        dt = w[0:1] * pltpu.roll(dcv, n - 1, 0) + w[1:2] * dcv + w[2:3] * pltpu.roll(dcv, 1, 0)
        mid = slice(8, tr + 8)
        dp_ref[:, 0:C] = (dt * cg)[mid].astype(bf16)
        dp_ref[:, C:2 * C] = (dyc * cv)[mid].astype(bf16)
        dp_ref[:, 2 * C:3 * C] = (dt * u)[mid].astype(bf16)
        dp_ref[:, 3 * C:4 * C] = (dyn * yhat * g_ref[...] * _dsilu(zc))[mid].astype(bf16)
        colsum = lambda v: jnp.sum(v[mid], axis=0, keepdims=True)
        parts = [colsum(dyn * yhat * sz), colsum(dcv), colsum(dcv * t_before), colsum(dcv * t), colsum(dcv * t_after)]
        row = lax.broadcasted_iota(jnp.int32, (8, C), 0)
        upd = jnp.zeros((8, C), f32)
        for j, pj in enumerate(parts):
            upd = jnp.where(row == j, pj, upd)

        @pl.when(i == 0)
        def _():
            sums_ref[...] = upd

        @pl.when(i > 0)
        def _():
            sums_ref[...] += upd

    specs = []
    for col in range(4):
        specs += _halo_specs(tr, S, C, col)
    specs += _halo_specs(tr, S, C, 0)
    vec = pl.BlockSpec((1, C), lambda i: (0, 0))
    return pl.pallas_call(
        body, name="conv_bwd",
        out_shape=(jax.ShapeDtypeStruct((S, 4 * C), bf16), jax.ShapeDtypeStruct((8, C), f32)), grid=(S // tr,),
        in_specs=[*specs, pl.BlockSpec((8, C), lambda i: (0, 0)), vec, vec, ANY_SPEC],
        out_specs=(pl.BlockSpec((tr, 4 * C), lambda i: (i, 0)), pl.BlockSpec((8, C), lambda i: (0, 0))),
        compiler_params=_params("arbitrary"),
    )(*([proj] * 12), dycat, dycat, dycat, conv_w, conv_b, g_conv, dep)


def _attn_post_bwd(o, proj, dycat, g_attn, dep, tr=256):
    S, C = o.shape
    tr = min(tr, S)

    def body(o_ref, z_ref, dy_ref, g_ref, dep_ref, do_ref, dz_ref, sums_ref):
        i = pl.program_id(0)
        ov, zv, dyn = o_ref[...], z_ref[...], dy_ref[...]
        ra = lax.rsqrt(jnp.mean(ov * ov, axis=-1, keepdims=True) + EPS)
        ohat = ov * ra
        sz = _silu(zv)
        wgt = dyn * g_ref[...] * sz
        do_ref[...] = ra * (wgt - ohat * jnp.mean(wgt * ohat, axis=-1, keepdims=True))
        dz_ref[...] = (dyn * ohat * g_ref[...] * _dsilu(zv)).astype(bf16)
        row = lax.broadcasted_iota(jnp.int32, (8, C), 0)
        upd = jnp.where(row == 0, jnp.sum(dyn * ohat * sz, axis=0, keepdims=True), 0.0)

        @pl.when(i == 0)
        def _():
            sums_ref[...] = upd

        @pl.when(i > 0)
        def _():
            sums_ref[...] += upd

    return pl.pallas_call(
        body, name="attn_post_bwd",
        out_shape=(jax.ShapeDtypeStruct((S, C), f32), jax.ShapeDtypeStruct((4, S, C), bf16),
                   jax.ShapeDtypeStruct((8, C), f32)),
        grid=(S // tr,),
        in_specs=[pl.BlockSpec((tr, C), lambda i: (i, 0)), pl.BlockSpec((tr, C), lambda i: (i, 7)),
                  pl.BlockSpec((tr, C), lambda i: (i, 1)), pl.BlockSpec((1, C), lambda i: (0, 0)), ANY_SPEC],
        out_specs=(pl.BlockSpec((tr, C), lambda i: (i, 0)), pl.BlockSpec((None, tr, C), lambda i: (3, i, 0)),
                   pl.BlockSpec((8, C), lambda i: (0, 0))),
        compiler_params=_params("arbitrary"),
    )(o, proj, dycat, g_attn, dep)


def _attn_bwd(proj, o, do, lse, slopes, dqkvz, dep):
    S, C = o.shape
    npair = C // PAIR

    def body(q_ref, k_ref, v_ref, o_ref, do_ref, lse_ref, sl_ref, old_ref, dep_ref, dqkv_ref,
             dq_scr, dk_scr, dv_scr, dl_scr, bias_scr):
        lane = lax.broadcasted_iota(jnp.int32, (1, PAIR), 1)
        first = lane < HEAD_DIM
        _fill_bias(bias_scr, sl_ref, S)
        ch = min(256, S)

        def prep(i, carry):
            rows = pl.ds(pl.multiple_of(i * ch, 8), ch)
            prod = do_ref[rows, :] * o_ref[rows, :]
            d0 = jnp.sum(jnp.where(first, prod, 0.0), axis=-1, keepdims=True)
            d1 = jnp.sum(jnp.where(first, 0.0, prod), axis=-1, keepdims=True)
            dl_scr[rows, :] = jnp.where(first, d0, d1)
            zero = jnp.zeros((ch, PAIR), f32)
            dq_scr[rows, :] = zero
            dk_scr[rows, :] = zero
            dv_scr[rows, :] = zero
            return carry

        lax.fori_loop(0, S // ch, prep, 0)

        for b, (_, r) in enumerate(BRANCHES):
            L, nq, nk, nblk = _branch_geometry(S, r)

            def step(idx, carry, b=b, r=r, L=L, nq=nq, nk=nk, nblk=nblk):
                qrows, krows, off = _block_rows(idx, r, L, nq, nk, nblk)
                case = off // HALF_WIN
                q2 = q_ref[qrows, :] * SCALE
                k2 = k_ref[krows, :].astype(bf16)
                v2 = v_ref[krows, :].astype(bf16)
                do2 = do_ref[qrows, :]
                lse2 = lse_ref[qrows, :]
                dl2 = dl_scr[qrows, :]
                dq2 = jnp.zeros((nq, PAIR), f32)
                dk2 = jnp.zeros((nk, PAIR), f32)
                dv2 = jnp.zeros((nk, PAIR), f32)
                for hh in range(2):
                    mine = first if hh == 0 else ~first
                    lo = hh * HEAD_DIM
                    qh = jnp.where(mine, q2, 0.0).astype(bf16)
                    doh = jnp.where(mine, do2, 0.0).astype(bf16)
                    s = lax.dot_general(qh, k2, (((1,), (1,)), ((), ())), preferred_element_type=f32)
                    s = s + bias_scr[_bias_index(b, case, hh), 0:nq, 0:nk]
                    p = jnp.exp(s - lse2[:, lo:lo + 1])
                    dv2 = dv2 + lax.dot_general(p.astype(bf16), doh, (((0,), (0,)), ((), ())), preferred_element_type=f32)
                    dp = lax.dot_general(doh, v2, (((1,), (1,)), ((), ())), preferred_element_type=f32)
                    ds = (p * (dp - dl2[:, lo:lo + 1])).astype(bf16)
                    dq2 = dq2 + jnp.where(mine, jnp.dot(ds, k2, preferred_element_type=f32), 0.0)
                    dk2 = dk2 + lax.dot_general(ds, qh, (((0,), (0,)), ((), ())), preferred_element_type=f32)
                dq_scr[qrows, :] = dq_scr[qrows, :] + dq2
                dk_scr[krows, :] = dk_scr[krows, :] + dk2
                dv_scr[krows, :] = dv_scr[krows, :] + dv2
                return carry

            lax.fori_loop(0, S // nq, step, 0, unroll=min(ATTN_UNROLL, S // nq))

        dqkv_ref[0] = (dq_scr[...] * SCALE).astype(bf16)
        dqkv_ref[1] = dk_scr[...].astype(bf16)
        dqkv_ref[2] = dv_scr[...].astype(bf16)

    blk = lambda part: pl.BlockSpec((S, PAIR), lambda p: (0, part * npair + p))
    own = pl.BlockSpec((S, PAIR), lambda p: (0, p))
    return pl.pallas_call(
        body, name="attn_bwd", out_shape=jax.ShapeDtypeStruct(dqkvz.shape, dqkvz.dtype), grid=(npair,),
        in_specs=[blk(4), blk(5), blk(6), own, own, own, pl.BlockSpec((None, 8, PAIR), lambda p: (p, 0, 0)),
                  ANY_SPEC, ANY_SPEC],
        out_specs=pl.BlockSpec((3, S, PAIR), lambda p: (0, 0, p)), input_output_aliases={7: 0},
        scratch_shapes=[pltpu.VMEM((S, PAIR), f32)] * 4 + [pltpu.VMEM(_bias_shape(S), f32)],
        compiler_params=_params("arbitrary"),
    )(proj, proj, proj, o, do, lse, slopes, dqkvz, dep)


def _prenorm_bwd(dh, x, dout, scale, g_pre, tr=256):
    S, D = x.shape
    tr = min(tr, S)

    def body(dh_ref, x_ref, dout_ref, sc_ref, g_ref, gx_ref, sums_ref):
        i = pl.program_id(0)
        xv, dhv = x_ref[...], dh_ref[...]
        r = lax.rsqrt(jnp.mean(xv * xv, axis=-1, keepdims=True) + EPS)
        xn = xv * r
        dxn = dhv * (g_ref[...] * (1.0 + sc_ref[...]))
        gx_ref[...] = dout_ref[...] + r * (dxn - xn * jnp.mean(dxn * xn, axis=-1, keepdims=True))
        dhx = dhv * xn
        row = lax.broadcasted_iota(jnp.int32, (8, D), 0)
        upd = jnp.where(row == 0, jnp.sum(dhv, axis=0, keepdims=True),
                        jnp.where(row == 1, jnp.sum(dhx, axis=0, keepdims=True) * g_ref[...],
                                  jnp.where(row == 2, jnp.sum(dhx, axis=0, keepdims=True) * (1.0 + sc_ref[...]), 0.0)))

        @pl.when(i == 0)
        def _():
            sums_ref[...] = upd

        @pl.when(i > 0)
        def _():
            sums_ref[...] += upd

    row = pl.BlockSpec((tr, D), lambda i: (i, 0))
    vec = pl.BlockSpec((1, D), lambda i: (0, 0))
    return pl.pallas_call(
        body, name="prenorm_bwd",
        out_shape=(jax.ShapeDtypeStruct((S, D), f32), jax.ShapeDtypeStruct((8, D), f32)), grid=(S // tr,),
        in_specs=[row, row, row, vec, vec], out_specs=(row, pl.BlockSpec((8, D), lambda i: (0, 0))),
        compiler_params=_params("arbitrary"),
    )(dh, x, dout, scale, g_pre)


def _adamw(w, g, m, v):
    m = ADAM_B1 * m + (1.0 - ADAM_B1) * g
    v = ADAM_B2 * v + (1.0 - ADAM_B2) * (g * g)
    m_hat = m / (1.0 - ADAM_B1 ** ADAM_STEP)
    v_hat = v / (1.0 - ADAM_B2 ** ADAM_STEP)
    delta = -ADAM_LR * (m_hat / (jnp.sqrt(v_hat) + ADAM_EPS) + ADAM_WD * w)
    return delta, m, v


def _sum_rows(parts, dep):
    P = parts.shape[1]

    def body(p_ref, dep_ref, o_ref):
        acc = p_ref[0:1, :]
        for j in range(1, NDEV):
            acc = acc + p_ref[j:j + 1, :]
        o_ref[...] = jnp.broadcast_to(acc, (8, P))

    vmem = pl.BlockSpec(memory_space=pltpu.VMEM)
    return pl.pallas_call(body, name="sum_small", out_shape=jax.ShapeDtypeStruct((8, P), f32),
                          in_specs=[vmem, ANY_SPEC], out_specs=vmem, compiler_params=_params())(parts, dep)


def _adamw_small(w, g, m, v):
    def body(w_ref, g_ref, m_ref, v_ref, d_ref, nm_ref, nv_ref):
        d_ref[...], nm_ref[...], nv_ref[...] = _adamw(w_ref[...], g_ref[...], m_ref[...], v_ref[...])

    out = jax.ShapeDtypeStruct(w.shape, f32)
    return pl.pallas_call(body, name="adamw_small", out_shape=(out, out, out), compiler_params=_params())(w, g, m, v)


def _adamw_sharded(parts, own, w, m, v, name, tr=128):
    R, Cc = w.shape
    tr = min(tr, R)
    n = parts.shape[0]

    def body(p_ref, own_ref, w_ref, m_ref, v_ref, g_ref, d_ref, nm_ref, nv_ref):
        g = own_ref[...].astype(f32)
        for j in range(n):
            g = g + p_ref[j].astype(f32)
        g_ref[...] = g
        d_ref[...], nm_ref[...], nv_ref[...] = _adamw(w_ref[...], g, m_ref[...], v_ref[...])

    row = pl.BlockSpec((tr, Cc), lambda i: (i, 0))
    out = jax.ShapeDtypeStruct((R, Cc), f32)
    return pl.pallas_call(
        body, name=name, out_shape=(out, out, out, out), grid=(R // tr,),
        in_specs=[pl.BlockSpec((n, tr, Cc), lambda i: (0, i, 0)), row, row, row, row],
        out_specs=(row, row, row, row), compiler_params=_params("parallel"),
    )(parts, own, w, m, v)


def _adamw_ada(c_t, dmod_cols, w, m, v, dep, tr=256):
    D, W = w.shape
    tr = min(tr, D)

    def body(c_ref, dm_ref, w_ref, m_ref, v_ref, dep_ref, g_ref, d_ref, nm_ref, nv_ref):
        cv, dm = c_ref[...], dm_ref[...]
        g = cv[:, 0:1] * dm[0:1, :]
        for b in range(1, NDEV):
            g = g + cv[:, b:b + 1] * dm[b:b + 1, :]
        g_ref[...] = g
        d_ref[...], nm_ref[...], nv_ref[...] = _adamw(w_ref[...], g, m_ref[...], v_ref[...])

    row = pl.BlockSpec((tr, W), lambda i: (i, 0))
    out = jax.ShapeDtypeStruct((D, W), f32)
    return pl.pallas_call(
        body, name="adamw_ada", out_shape=(out, out, out, out), grid=(D // tr,),
        in_specs=[pl.BlockSpec((tr, NDEV), lambda i: (i, 0)), pl.BlockSpec((NDEV, W), lambda i: (0, 0)), row, row, row,
                  ANY_SPEC],
        out_specs=(row, row, row, row), compiler_params=_params("parallel"),
    )(c_t, dmod_cols, w, m, v, dep)


def kernel(x, c, w_ada, b_ada, g_pre, w_in, conv_w, conv_b, g_conv, g_attn, w_out, g_post, loss_target, m_w_ada, m_b_ada, m_g_pre, m_w_in, m_conv_w, m_conv_b, m_g_conv, m_g_attn, m_w_out, m_g_post, v_w_ada, v_b_ada, v_g_pre, v_w_in, v_conv_w, v_conv_b, v_g_conv, v_g_attn, v_w_out, v_g_post):
    S, D = x.shape[1], x.shape[2]
    C = D // 2
    W = w_ada.shape[2]
    CW = conv_w.shape[2]
    me = 4 * lax.axis_index("x") + 2 * lax.axis_index("y") + lax.axis_index("c")
    x2, tgt = x[0], loss_target[0]
    w_ada2, w_in2, w_out2 = w_ada[0], w_in[0], w_out[0]

    R = D // NDEV
    core = lax.axis_index("c").astype(jnp.int32).reshape(1)

    cw_slab = jnp.zeros((8, CW), f32).at[:3].set(conv_w[0])
    c_blocks, cw_g = _all_gather([_silu_block(c.reshape(D // 128, 128)), cw_slab], "gather_c")
    c_all = c_blocks.reshape(NDEV, D)
    conv_w_full = jnp.transpose(cw_g, (1, 0, 2)).reshape(8, C)
    b_cols = lax.dynamic_slice_in_dim(b_ada, me * W, W, axis=1)
    (mod_slabs,) = _all_to_all([_ada_cols(c_all, w_ada2, b_cols)], "scatter_mod")
    mod = mod_slabs[:, 0, :].reshape(1, 3 * D)
    shift, scale, gate = mod[:, :D], mod[:, D:2 * D], mod[:, 2 * D:]

    land_i = lax.dynamic_update_slice(lax.empty((NDEV, D, C), bf16), w_in2.astype(bf16)[None], (me, 0, 0))
    land_o = lax.dynamic_update_slice(lax.empty((NDEV, R, D), bf16), w_out2.astype(bf16)[None], (me, 0, 0))
    wi_send, wi_recv, land_i, w_token = _w_in_start(land_i, [mod_slabs])

    me_arr = me.astype(jnp.int32).reshape(1)
    h = _prenorm(x2, scale, shift, g_pre, w_token)
    land_i = _w_in_sibling(land_i, wi_recv, after=[h])
    proj = _in_proj_part("in_proj_a", h, land_i, None, me_arr, 0, 1, 2)
    fi_send, fi_recv, land_i = _w_in_relay(land_i, wi_recv, after=[proj])
    proj = _in_proj_part("in_proj_b", h, land_i, proj, me_arr, 2, 2, 2)
    land_i = _w_in_forwarded(land_i, fi_recv, after=[proj])
    proj = _in_proj_part("in_proj_c", h, land_i, proj, me_arr, 3, 2, 2)
    (di_send, di_recv, wo_send, wo_recv), land_i, land_o = _w_in_diag(land_i, land_o, fi_recv, after=[proj])
    proj = _in_proj_part("in_proj_d", h, land_i, proj, me_arr, 6, 1, 1)
    win_g = _w_in_finish(land_i, wi_send, fi_send, di_send, di_recv, after=[proj])
    proj = _in_proj_part("in_proj_e", h, win_g, proj, me_arr, 7, 1, 1)
    (fo_send, fo_recv), (land_o,), _ = _weights_forward("w_out_forward", land_o, wo_recv, after=[proj])
    slopes = _head_slopes(C // HEAD_DIM)
    ycat = _conv_fwd(proj, conv_w_full, conv_b, g_conv)
    o, lse = _attn_fwd(proj, slopes)
    ycat = _attn_post(ycat, o, proj, g_attn)
    wout_g = _weights_wait("w_out_wait", land_o, wo_send, wo_recv, fo_send, fo_recv, after=[ycat])
    wout_full = wout_g.reshape(D, D)
    y = _matmul(ycat, wout_full, name="out_proj", out_dtype=f32)
    dy, dout, post_sums = _sandwich(y, x2, tgt, gate, g_post)

    loss = lax.psum(post_sums[2, 0], ("x", "y", "c"))
    gw_out = _matmul(ycat, dy, name="out_proj_dw", out_dtype=bf16, ta=True, dep=loss.reshape(1, 1)).reshape(NDEV, R, D)
    po_send, po_recv, gw_out, pair_o, po_token = _pair_start("g_out_pair_start", gw_out)
    dycat = _matmul(dy, wout_full, name="out_proj_dx", out_dtype=f32, tb=True, dep=po_token)
    gw_out, pair_o = _pair_wait("g_out_pair_wait", gw_out, pair_o, po_send, po_recv, after=[dycat])
    sum_o = _pair_sum("g_out_pair_sum", gw_out, pair_o, core)
    co_send, co_recv, sum_o, land_go, co_token = _chip_start(
        "g_out_chip_start", sum_o, jnp.zeros((NCHIP, R, D), bf16), 0)
    dpc, conv_sums = _conv_bwd(proj, dycat, conv_w_full, conv_b, g_conv, co_token)
    gw_c = _matmul(h, dpc, name="in_proj_dw_conv", out_dtype=bf16, ta=True, out_slots=4)
    pc_send, pc_recv, gw_c, pair_c, pc_token = _pair_start("g_conv_pair_start", gw_c)
    do, dpa, attn_sums = _attn_post_bwd(o, proj, dycat, g_attn, pc_token)
    gw_c, pair_c = _pair_wait("g_conv_pair_wait", gw_c, pair_c, pc_send, pc_recv, after=[do])
    sum_c = _pair_sum("g_conv_pair_sum", gw_c, pair_c, core)
    cc_send, cc_recv, sum_c, land_gi, cc_token = _chip_start(
        "g_conv_chip_start", sum_c, jnp.zeros((NCHIP, D, C), bf16), 0)
    dpa = _attn_bwd(proj, o, do, lse, slopes, dpa, cc_token)
    gw_a = _matmul(h, dpa, name="in_proj_dw_attn", out_dtype=bf16, ta=True, b_slots=True, out_slots=4)
    pa_send, pa_recv, gw_a, pair_a, pa_token = _pair_start("g_attn_pair_start", gw_a)
    gw_a, pair_a = _pair_wait("g_attn_pair_wait", gw_a, pair_a, pa_send, pa_recv, after=[pa_token])
    sum_a = _pair_sum("g_attn_pair_sum", gw_a, pair_a, core)
    part_a, part_b = (0, 3 * D // 4), (3 * D // 4, D // 4)
    ca_send, ca_recv, sum_a, land_gi, ca_token = _chip_start("g_attn_chip_start_a", sum_a, land_gi, 4, part_a)
    dh = _matmul_slabs_t(dpc, dpa, win_g, name="in_proj_dx", dep=ca_token)
    grad_x, pre_sums = _prenorm_bwd(dh, x2, dout, scale, g_pre)

    small = jnp.concatenate([pre_sums[0:1], pre_sums[1:2], post_sums[0:1],
                             pre_sums[2:3], post_sums[1:2],
                             conv_sums[2:3], conv_sums[3:4], conv_sums[4:5],
                             conv_sums[1:2], conv_sums[0:1], attn_sums[0:1]], axis=1)
    (small_all,) = _all_gather([small.reshape(8 * D // 128, 128)], "gather_small")
    cb_send, cb_recv, sum_a, land_gi, cb_token = _chip_start("g_attn_chip_start_b", sum_a, land_gi, 4, part_b,
                                                             after=[small_all])
    small_all = small_all.reshape(NDEV, 8 * D)
    tot = _sum_rows(small_all, cb_token)[0:1]

    g_b_ada = tot[:, :3 * D]
    g_g_pre, g_g_post = tot[:, 3 * D:4 * D], tot[:, 4 * D:5 * D]
    g_conv_w_full = tot[:, 5 * D:5 * D + 3 * C].reshape(3, C)
    g_conv_w = lax.dynamic_slice_in_dim(g_conv_w_full, me * CW, CW, axis=1)[None]
    g_conv_b, g_g_conv, g_g_attn = (tot[:, 5 * D + (3 + t) * C:5 * D + (4 + t) * C] for t in range(3))

    dmod_cols = lax.dynamic_slice_in_dim(small_all[:, :3 * D], me * W, W, axis=1)
    g_w_ada, d_w_ada, nm_w_ada, nv_w_ada = _adamw_ada(c_all.T, dmod_cols, w_ada2, m_w_ada[0], v_w_ada[0], cb_token)

    sum_o, land_go = _chip_wait("g_out_chip_wait", sum_o, land_go, co_send, co_recv, 0, after=[g_w_ada])
    own_out = lax.dynamic_index_in_dim(sum_o, me // 2, 0, keepdims=False)
    g_w_out, d_w_out, nm_w_out, nv_w_out = _adamw_sharded(land_go, own_out, w_out2, m_w_out[0], v_w_out[0], "adamw_w_out")
    sum_c, land_gi = _chip_wait("g_conv_chip_wait", sum_c, land_gi, cc_send, cc_recv, 0, after=[g_w_out])
    sum_a, land_gi = _chip_wait("g_attn_chip_wait_a", sum_a, land_gi, ca_send, ca_recv, 4, [g_w_out], part_a)
    sum_a, land_gi = _chip_wait("g_attn_chip_wait_b", sum_a, land_gi, cb_send, cb_recv, 4, [g_w_out], part_b)
    own_in = jnp.where(me < 4, lax.dynamic_index_in_dim(sum_c, (me % 4) // 2, 0, keepdims=False),
                       lax.dynamic_index_in_dim(sum_a, (me % 4) // 2, 0, keepdims=False))
    g_w_in, d_w_in, nm_w_in, nv_w_in = _adamw_sharded(land_gi, own_in, w_in2, m_w_in[0], v_w_in[0], "adamw_w_in")

    pack = lambda *vs: jnp.concatenate([a.reshape(1, -1) for a in vs], axis=1)
    smalls = [(b_ada, g_b_ada, m_b_ada, v_b_ada), (g_pre, g_g_pre, m_g_pre, v_g_pre),
              (conv_w, g_conv_w, m_conv_w, v_conv_w), (conv_b, g_conv_b, m_conv_b, v_conv_b),
              (g_conv, g_g_conv, m_g_conv, v_g_conv), (g_attn, g_g_attn, m_g_attn, v_g_attn),
              (g_post, g_g_post, m_g_post, v_g_post)]
    packed = [pack(*[s[t] for s in smalls]) for t in range(4)]
    npad = -packed[0].shape[1] % 128
    packed = [jnp.pad(p, ((0, 0), (0, npad)), constant_values=1.0) for p in packed]
    d_s, nm_s, nv_s = _adamw_small(*packed)

    def unpack(vec):
        out, at = [], 0
        for s in smalls:
            n = s[0].size
            out.append(vec[:, at:at + n].reshape(s[0].shape))
            at += n
        return out

    d_b_ada, d_g_pre, d_conv_w, d_conv_b, d_g_conv, d_g_attn, d_g_post = unpack(d_s)
    nm_b_ada, nm_g_pre, nm_conv_w, nm_conv_b, nm_g_conv, nm_g_attn, nm_g_post = unpack(nm_s)
    nv_b_ada, nv_g_pre, nv_conv_w, nv_conv_b, nv_g_conv, nv_g_attn, nv_g_post = unpack(nv_s)

    return (loss, grad_x[None],
            g_w_ada[None], g_b_ada, g_g_pre, g_w_in[None], g_conv_w, g_conv_b, g_g_conv, g_g_attn, g_w_out[None], g_g_post,
            d_w_ada[None], d_b_ada, d_g_pre, d_w_in[None], d_conv_w, d_conv_b, d_g_conv, d_g_attn, d_w_out[None], d_g_post,
            nm_w_ada[None], nm_b_ada, nm_g_pre, nm_w_in[None], nm_conv_w, nm_conv_b, nm_g_conv, nm_g_attn, nm_w_out[None], nm_g_post,
            nv_w_ada[None], nv_b_ada, nv_g_pre, nv_w_in[None], nv_conv_w, nv_conv_b, nv_g_conv, nv_g_attn, nv_w_out[None], nv_g_post)
```

```python
import functools

import jax
import jax.numpy as jnp
from jax import lax
from jax.experimental import pallas as pl
from jax.experimental.pallas import tpu as pltpu

f32 = jnp.float32
bf16 = jnp.bfloat16

NDEV = 8
HEAD_DIM = 64
PAIR = 2 * HEAD_DIM
BRANCHES = ((128, 1, 1), (512, 4, 1), (2048, 16, 2))
HALF_WIN = 64
EPS = 1e-6
NEG_INF = -1e30
ADAM_LR, ADAM_B1, ADAM_B2, ADAM_EPS, ADAM_WD, ADAM_STEP = 0.001, 0.9, 0.999, 1e-08, 0.01, 10
MESH = pl.DeviceIdType.MESH
VMEM_LIMIT = 56 * 1024 * 1024
HBM_SPEC = pl.BlockSpec(memory_space=pltpu.HBM)
ANY_SPEC = pl.BlockSpec(memory_space=pl.ANY)
SEM_SPEC = pl.BlockSpec(memory_space=pltpu.SEMAPHORE)


def _params(*sem):
    return pltpu.CompilerParams(dimension_semantics=sem or None, vmem_limit_bytes=VMEM_LIMIT)


def _silu(z):
    return z * jax.nn.sigmoid(z)


def _dsilu(z):
    s = jax.nn.sigmoid(z)
    return s * (1.0 + z * (1.0 - s))


def _my_place():
    x, y, c = lax.axis_index("x"), lax.axis_index("y"), lax.axis_index("c")
    return x, y, c, 4 * x + 2 * y + c


def _peer(x, y, c, k):
    px, py, pc = x ^ (k >> 2 & 1), y ^ (k >> 1 & 1), c ^ (k & 1)
    return (px, py, pc), 4 * px + 2 * py + pc


def _all_gather(arrays, name):
    n = len(arrays)

    def body(*refs):
        srcs, dsts = refs[:n], refs[n:2 * n]
        send_sems, recv_sems, local_sems = refs[2 * n:]
        x, y, c, me = _my_place()
        locals_, sends = [], []
        for t in range(n):
            own = pltpu.make_async_copy(srcs[t], dsts[t].at[me], local_sems.at[t])
            own.start()
            locals_.append(own)
            for k in range(1, NDEV):
                peer, pidx = _peer(x, y, c, k)
                cp = pltpu.make_async_remote_copy(
                    src_ref=srcs[t], dst_ref=dsts[t].at[me], send_sem=send_sems.at[t, k],
                    recv_sem=recv_sems.at[t, k], device_id=peer, device_id_type=MESH)
                cp.start()
                sends.append(cp)
        for t in range(n):
            for k in range(1, NDEV):
                peer, pidx = _peer(x, y, c, k)
                pltpu.make_async_remote_copy(
                    src_ref=srcs[t], dst_ref=dsts[t].at[pidx], send_sem=send_sems.at[t, k],
                    recv_sem=recv_sems.at[t, k], device_id=peer, device_id_type=MESH).wait_recv()
        for cp in sends:
            cp.wait_send()
        for cp in locals_:
            cp.wait()

    return pl.pallas_call(
        body, name=name,
        out_shape=tuple(jax.ShapeDtypeStruct((NDEV,) + a.shape, a.dtype) for a in arrays),
        in_specs=[HBM_SPEC] * n, out_specs=tuple([HBM_SPEC] * n),
        scratch_shapes=[pltpu.SemaphoreType.DMA((n, NDEV)), pltpu.SemaphoreType.DMA((n, NDEV)),
                        pltpu.SemaphoreType.DMA((n,))],
    )(*arrays)


def _all_to_all(arrays, name):
    n = len(arrays)

    def body(*refs):
        srcs, dsts = refs[:n], refs[n:2 * n]
        send_sems, recv_sems, local_sems = refs[2 * n:]
        x, y, c, me = _my_place()
        locals_, sends = [], []
        for t in range(n):
            own = pltpu.make_async_copy(srcs[t].at[me], dsts[t].at[me], local_sems.at[t])
            own.start()
            locals_.append(own)
            for k in range(1, NDEV):
                peer, pidx = _peer(x, y, c, k)
                cp = pltpu.make_async_remote_copy(
                    src_ref=srcs[t].at[pidx], dst_ref=dsts[t].at[me], send_sem=send_sems.at[t, k],
                    recv_sem=recv_sems.at[t, k], device_id=peer, device_id_type=MESH)
                cp.start()
                sends.append(cp)
        for t in range(n):
            for k in range(1, NDEV):
                peer, pidx = _peer(x, y, c, k)
                pltpu.make_async_remote_copy(
                    src_ref=srcs[t].at[pidx], dst_ref=dsts[t].at[pidx], send_sem=send_sems.at[t, k],
                    recv_sem=recv_sems.at[t, k], device_id=peer, device_id_type=MESH).wait_recv()
        for cp in sends:
            cp.wait_send()
        for cp in locals_:
            cp.wait()

    return pl.pallas_call(
        body, name=name,
        out_shape=tuple(jax.ShapeDtypeStruct(a.shape, a.dtype) for a in arrays),
        in_specs=[HBM_SPEC] * n, out_specs=tuple([HBM_SPEC] * n),
        scratch_shapes=[pltpu.SemaphoreType.DMA((n, NDEV)), pltpu.SemaphoreType.DMA((n, NDEV)),
                        pltpu.SemaphoreType.DMA((n,))],
    )(*arrays)


def _comm_call(name, arrays, sems, new_sems, body, after=(), token=False):
    na, ns, nn, nf = len(arrays), len(sems), len(new_sems), len(after)

    def kern(*refs):
        ins, outs = refs[:na + ns + nf], refs[na + ns + nf:]
        body(ins[:na], ins[na:na + ns], outs[:nn])
        if token:
            outs[nn + na][...] = jnp.zeros((8, 128), f32)

    out_shape = ([pltpu.SemaphoreType.DMA(s) for s in new_sems] + [pltpu.HBM(a.shape, a.dtype) for a in arrays]
                 + ([jax.ShapeDtypeStruct((8, 128), f32)] if token else []))
    out_specs = [SEM_SPEC] * nn + [HBM_SPEC] * na + ([pl.BlockSpec(memory_space=pltpu.VMEM)] if token else [])
    res = pl.pallas_call(
        kern, name=name, out_shape=tuple(out_shape),
        in_specs=[HBM_SPEC] * na + [SEM_SPEC] * ns + [ANY_SPEC] * nf, out_specs=tuple(out_specs),
        input_output_aliases={t: nn + t for t in range(na)},
        compiler_params=pltpu.CompilerParams(has_side_effects=pltpu.SideEffectType.DATAFLOW_SIDE_EFFECTING),
    )(*[pltpu.with_memory_space_constraint(a, pltpu.HBM) for a in arrays], *sems, *after)
    return list(res[:nn]), list(res[nn:nn + na]), (res[nn + na] if token else None)


def _remote(src, dst, send_sem, recv_sem, device):
    return pltpu.make_async_remote_copy(src_ref=src, dst_ref=dst, send_sem=send_sem, recv_sem=recv_sem,
                                        device_id=device, device_id_type=MESH)


SAME_CORE = (2, 4, 6)
VIA_SIBLING = (3, 5, 7)


def _weights_forward(name, land, recv, after):
    def body(a, s, new):
        (land,), (recv,), (fsend, frecv) = a, s, new
        x, y, c, me = _my_place()
        sibling, _ = _peer(x, y, c, 1)
        for k in SAME_CORE:
            peer, slot = _peer(x, y, c, k)
            _remote(land.at[slot], land.at[slot], fsend.at[k], recv.at[k], peer).wait_recv()
            _remote(land.at[slot], land.at[slot], fsend.at[k], frecv.at[k ^ 1], sibling).start()

    return _comm_call(name, [land], [recv], [(NDEV,), (NDEV,)], body, after=after)


def _weights_wait(name, land, send, recv, fsend, frecv, after):
    def body(a, s, new):
        (land,), (send, recv, fsend, frecv) = a, s
        x, y, c, me = _my_place()
        sibling, sib_slot = _peer(x, y, c, 1)
        _remote(land.at[sib_slot], land.at[sib_slot], send.at[1], recv.at[1], sibling).wait_recv()
        for k in VIA_SIBLING:
            _, slot = _peer(x, y, c, k)
            _remote(land.at[slot], land.at[slot], fsend.at[k ^ 1], frecv.at[k], sibling).wait_recv()
        for k in (1,) + SAME_CORE:
            peer, _ = _peer(x, y, c, k)
            _remote(land.at[me], land.at[me], send.at[k], recv.at[k], peer).wait_send()
        for k in SAME_CORE:
            _, slot = _peer(x, y, c, k)
            _remote(land.at[slot], land.at[slot], fsend.at[k], frecv.at[k ^ 1], sibling).wait_send()

    return _comm_call(name, [land], [send, recv, fsend, frecv], [], body, after=after)[1][0]


def _diag_relay(x, y, c):
    slot = 4 * (x ^ (1 - c)) + 2 * (y ^ c) + c
    return slot, (x ^ c, y ^ (1 - c), c)


def _w_in_start(land, after):
    def body(a, s, new):
        (land,), (send, recv) = a, new
        x, y, c, me = _my_place()
        for k in (1, 2, 4):
            peer, _ = _peer(x, y, c, k)
            _remote(land.at[me], land.at[me], send.at[k], recv.at[k], peer).start()

    (send, recv), (land,), token = _comm_call("w_in_start", [land], [], [(NDEV,), (NDEV,)], body, after=after, token=True)
    return send, recv, land, token


def _w_in_sibling(land, recv, after):
    def body(a, s, new):
        (land,), (recv,) = a, s
        x, y, c, me = _my_place()
        sibling, slot = _peer(x, y, c, 1)
        _remote(land.at[slot], land.at[slot], recv.at[1], recv.at[1], sibling).wait_recv()

    return _comm_call("w_in_sibling", [land], [recv], [], body, after=after)[1][0]


def _w_in_relay(land, recv, after):
    def body(a, s, new):
        (land,), (recv,), (fsend, frecv) = a, s, new
        x, y, c, me = _my_place()
        sibling, _ = _peer(x, y, c, 1)
        for k in (2, 4):
            peer, slot = _peer(x, y, c, k)
            _remote(land.at[slot], land.at[slot], fsend.at[k], recv.at[k], peer).wait_recv()
        slot, target = _diag_relay(x, y, c)
        _remote(land.at[slot], land.at[slot], fsend.at[6], frecv.at[6], target).start()
        for k in (2, 4):
            _, slot = _peer(x, y, c, k)
            _remote(land.at[slot], land.at[slot], fsend.at[k], frecv.at[k ^ 1], sibling).start()

    (fsend, frecv), (land,), _ = _comm_call("w_in_relay", [land], [recv], [(NDEV,), (NDEV,)], body, after=after)
    return fsend, frecv, land


def _w_in_forwarded(land, frecv, after):
    def body(a, s, new):
        (land,), (frecv,) = a, s
        x, y, c, me = _my_place()
        sibling, _ = _peer(x, y, c, 1)
        for k in (3, 5):
            _, slot = _peer(x, y, c, k)
            _remote(land.at[slot], land.at[slot], frecv.at[k], frecv.at[k], sibling).wait_recv()

    return _comm_call("w_in_forwarded", [land], [frecv], [], body, after=after)[1][0]


def _w_in_diag(land, land_o, frecv, after):
    def body(a, s, new):
        (land, land_o), (frecv,), (dsend, drecv, osend, orecv) = a, s, new
        x, y, c, me = _my_place()
        sibling, _ = _peer(x, y, c, 1)
        peer, slot = _peer(x, y, c, 6)
        _remote(land.at[slot], land.at[slot], dsend.at[6], frecv.at[6], peer).wait_recv()
        _remote(land.at[slot], land.at[slot], dsend.at[6], drecv.at[7], sibling).start()
        for k in (1,) + SAME_CORE:
            peer, _ = _peer(x, y, c, k)
            _remote(land_o.at[me], land_o.at[me], osend.at[k], orecv.at[k], peer).start()

    sems, (land, land_o), _ = _comm_call("w_in_diag", [land, land_o], [frecv], [(NDEV,)] * 4, body, after=after)
    return sems, land, land_o


def _w_in_finish(land, send, fsend, dsend, drecv, after):
    def body(a, s, new):
        (land,), (send, fsend, dsend, drecv) = a, s
        x, y, c, me = _my_place()
        sibling, _ = _peer(x, y, c, 1)
        _, slot = _peer(x, y, c, 7)
        _remote(land.at[slot], land.at[slot], dsend.at[6], drecv.at[7], sibling).wait_recv()
        for k in (1, 2, 4):
            peer, _ = _peer(x, y, c, k)
            _remote(land.at[me], land.at[me], send.at[k], send.at[k], peer).wait_send()
        for k in (2, 4, 6):
            _, slot = _peer(x, y, c, k)
            _remote(land.at[slot], land.at[slot], fsend.at[k], fsend.at[k], sibling).wait_send()
        _, slot = _peer(x, y, c, 6)
        _remote(land.at[slot], land.at[slot], dsend.at[6], dsend.at[6], sibling).wait_send()

    return _comm_call("w_in_finish", [land], [send, fsend, dsend, drecv], [], body, after=after)[1][0]


def _in_proj_part(name, h, land, proj, me_arr, k0, kstep, nk, tm=512):
    S, D = h.shape
    C = land.shape[2]
    tm = min(tm, S)

    def body(me_ref, a_ref, b_ref, *rest):
        rest[-1][...] = jnp.dot(a_ref[...], b_ref[...], preferred_element_type=f32)

    slot = lambda j, me: me[0] ^ (k0 + kstep * j)
    args = [h, land] + ([] if proj is None else [proj])
    grid_spec = pltpu.PrefetchScalarGridSpec(
        num_scalar_prefetch=1, grid=(nk, S // tm),
        in_specs=[pl.BlockSpec((tm, D), lambda j, i, me: (i, 0)),
                  pl.BlockSpec((None, D, C), lambda j, i, me: (slot(j, me), 0, 0))] + [ANY_SPEC] * (len(args) - 2),
        out_specs=pl.BlockSpec((tm, C), lambda j, i, me: (i, slot(j, me))))
    return pl.pallas_call(
        body, name=name, out_shape=jax.ShapeDtypeStruct((S, NDEV * C), f32), grid_spec=grid_spec,
        input_output_aliases={} if proj is None else {3: 0}, compiler_params=_params("arbitrary", "arbitrary"),
    )(me_arr, *args)


NCHIP = NDEV // 2


def _pair_start(name, src):
    npair = src.shape[0] // 2

    def body(a, s, new):
        (src, pair), (send, recv) = a, new
        x, y, c, me = _my_place()
        sibling, _ = _peer(x, y, c, 1)
        for i in range(npair):
            _remote(src.at[2 * i + 1 - c], pair.at[i], send.at[i], recv.at[i], sibling).start()

    pair = lax.empty((npair,) + src.shape[1:], src.dtype)
    (send, recv), (src, pair), token = _comm_call(name, [src, pair], [], [(npair,), (npair,)], body, token=True)
    return send, recv, src, pair, token


def _pair_wait(name, src, pair, send, recv, after):
    npair = pair.shape[0]

    def body(a, s, new):
        (src, pair), (send, recv) = a, s
        x, y, c, me = _my_place()
        sibling, _ = _peer(x, y, c, 1)
        for i in range(npair):
            cp = _remote(src.at[2 * i + 1 - c], pair.at[i], send.at[i], recv.at[i], sibling)
            cp.wait_recv()
            cp.wait_send()

    return _comm_call(name, [src, pair], [send, recv], [], body, after=after)[1]


def _pair_sum(name, src, pair, core, tr=256):
    npair, R, Cc = pair.shape
    tr = min(tr, R)

    def body(core_ref, a_ref, b_ref, o_ref):
        o_ref[...] = (a_ref[...].astype(f32) + b_ref[...].astype(f32)).astype(o_ref.dtype)

    grid_spec = pltpu.PrefetchScalarGridSpec(
        num_scalar_prefetch=1, grid=(npair, R // tr),
        in_specs=[pl.BlockSpec((None, tr, Cc), lambda i, r, core: (2 * i + core[0], r, 0)),
                  pl.BlockSpec((None, tr, Cc), lambda i, r, core: (i, r, 0))],
        out_specs=pl.BlockSpec((None, tr, Cc), lambda i, r, core: (i, r, 0)))
    return pl.pallas_call(body, name=name, out_shape=jax.ShapeDtypeStruct(pair.shape, pair.dtype),
                          grid_spec=grid_spec, compiler_params=_params("parallel", "parallel"))(core, src, pair)


def _owner_chip(first, i):
    q = first // 2 + i
    return q >> 1 & 1, q & 1


def _chip_start(name, sums, land, first, rows=None, after=()):
    npair = sums.shape[0]
    rows = pl.ds(*(rows or (0, sums.shape[1])))

    def body(a, s, new):
        (sums, land), (send, recv) = a, new
        x, y, c, me = _my_place()
        for i in range(npair):
            ox, oy = _owner_chip(first, i)

            @pl.when((x != ox) | (y != oy))
            def _():
                _remote(sums.at[i, rows], land.at[2 * x + y, rows], send.at[i], recv.at[2 * x + y], (ox, oy, c)).start()

    (send, recv), (sums, land), token = _comm_call(name, [sums, land], [], [(npair,), (NCHIP,)], body, after=after,
                                                   token=True)
    return send, recv, sums, land, token


def _chip_wait(name, sums, land, send, recv, first, after, rows=None):
    npair = sums.shape[0]
    rows = pl.ds(*(rows or (0, sums.shape[1])))

    def body(a, s, new):
        (sums, land), (send, recv) = a, s
        x, y, c, me = _my_place()
        mine = (me >= first) & (me < first + 2 * npair)
        for i in range(npair):
            ox, oy = _owner_chip(first, i)

            @pl.when((x != ox) | (y != oy))
            def _():
                _remote(sums.at[i, rows], land.at[2 * x + y, rows], send.at[i], recv.at[2 * x + y], (ox, oy, c)).wait_send()
        for q in range(NCHIP):
            @pl.when(mine & (2 * x + y != q))
            def _():
                _remote(sums.at[0, rows], land.at[q, rows], send.at[0], recv.at[q], (q >> 1, q & 1, c)).wait_recv()

    return _comm_call(name, [sums, land], [send, recv], [], body, after=after)[1]


def _matmul(a, b, *, name, out_dtype, ta=False, tb=False, b_slots=False, out_slots=0, b_cols=None,
            tm=1024, tn=1024, tk=2048, dep=None):
    M, K = (a.shape[1], a.shape[0]) if ta else a.shape
    col0 = 0
    if b_slots:
        slab = b.shape[2]
        N = b.shape[1] if tb else b.shape[0] * slab
        assert (K if tb else N) == b.shape[0] * slab
    elif b_cols is not None:
        assert not tb
        col0, N = b_cols
    else:
        N = b.shape[0] if tb else b.shape[1]
    tm, tn, tk = min(tm, M), min(tn, N), min(tk, K)
    if b_slots:
        if tb:
            tk = min(tk, slab)
        else:
            tn = min(tn, slab)
    if out_slots:
        tn = min(tn, N // out_slots)
    nm, nn, nk = M // tm, N // tn, K // tk
    assert (nm * tm, nn * tn, nk * tk) == (M, N, K) and col0 % tn == 0, (name, M, N, K, tm, tn, tk)
    j0 = col0 // tn

    a_spec = pl.BlockSpec((tk, tm), lambda i, j, k: (k, i)) if ta else pl.BlockSpec((tm, tk), lambda i, j, k: (i, k))
    if b_slots and tb:
        per = slab // tk
        b_spec = pl.BlockSpec((None, tn, tk), lambda i, j, k: (k // per, j, k % per))
    elif b_slots:
        per = slab // tn
        b_spec = pl.BlockSpec((None, tk, tn), lambda i, j, k: (j // per, k, j % per))
    elif tb:
        b_spec = pl.BlockSpec((tn, tk), lambda i, j, k: (j, k))
    else:
        b_spec = pl.BlockSpec((tk, tn), lambda i, j, k: (k, j + j0))
    if out_slots:
        per_o = (N // out_slots) // tn
        o_spec = pl.BlockSpec((None, tm, tn), lambda i, j, k: (j // per_o, i, j % per_o))
        out_shape = jax.ShapeDtypeStruct((out_slots, M, N // out_slots), out_dtype)
    else:
        o_spec = pl.BlockSpec((tm, tn), lambda i, j, k: (i, j))
        out_shape = jax.ShapeDtypeStruct((M, N), out_dtype)
    dims = (((0 if ta else 1,), (1 if tb else 0,)), ((), ()))
    deps = [] if dep is None else [dep]

    def body(a_ref, b_ref, *rest):
        o_ref = rest[len(deps)]
        prod = lax.dot_general(a_ref[...], b_ref[...], dims, preferred_element_type=f32)
        if nk == 1:
            o_ref[...] = prod.astype(out_dtype)
            return
        acc_ref = rest[len(deps) + 1]
        k = pl.program_id(2)

        @pl.when(k == 0)
        def _():
            acc_ref[...] = prod

        @pl.when((k > 0) & (k < nk - 1))
        def _():
            acc_ref[...] += prod

        @pl.when(k == nk - 1)
        def _():
            o_ref[...] = (acc_ref[...] + prod).astype(out_dtype)

    return pl.pallas_call(
        body, name=name, out_shape=out_shape, grid=(nm, nn, nk),
        in_specs=[a_spec, b_spec] + [ANY_SPEC] * len(deps), out_specs=o_spec,
        scratch_shapes=[pltpu.VMEM((tm, tn), f32)] if nk > 1 else [],
        compiler_params=_params("parallel", "parallel", "arbitrary"),
    )(a, b, *deps)


def _matmul_slabs_t(a_cols, a_slots, b, *, name, tm=512, tn=512, dep=None):
    M = a_cols.shape[0]
    n_slab, N, slab = b.shape
    n1, n2 = a_cols.shape[1] // slab, a_slots.shape[0]
    assert n1 + n2 == n_slab and a_slots.shape[1:] == (M, slab)
    tm, tn = min(tm, M), min(tn, N)
    deps = [] if dep is None else [dep]

    def body(a1_ref, a2_ref, b_ref, *rest):
        o_ref = rest[len(deps)]
        acc = None
        for s in range(n_slab):
            lhs = a1_ref[:, s * slab:(s + 1) * slab] if s < n1 else a2_ref[s - n1]
            prod = lax.dot_general(lhs, b_ref[s], (((1,), (1,)), ((), ())), preferred_element_type=f32)
            acc = prod if acc is None else acc + prod
        o_ref[...] = acc

    return pl.pallas_call(
        body, name=name, out_shape=jax.ShapeDtypeStruct((M, N), f32), grid=(M // tm, N // tn),
        in_specs=[pl.BlockSpec((tm, n1 * slab), lambda i, j: (i, 0)), pl.BlockSpec((n2, tm, slab), lambda i, j: (0, i, 0)),
                  pl.BlockSpec((n_slab, tn, slab), lambda i, j: (0, j, 0))] + [ANY_SPEC] * len(deps),
        out_specs=pl.BlockSpec((tm, tn), lambda i, j: (i, j)), compiler_params=_params("parallel", "parallel"),
    )(a_cols, a_slots, b, *deps)


def _silu_block(c):
    def body(c_ref, o_ref):
        o_ref[...] = _silu(c_ref[...])

    return pl.pallas_call(body, name="silu_c", out_shape=jax.ShapeDtypeStruct(c.shape, f32))(c)


def _ada_cols(c_all, w_ada, b_cols):
    D, W = w_ada.shape

    def body(c_ref, w_ref, b_ref, o_ref):
        mod = lax.dot_general(c_ref[...], w_ref[...], (((1,), (0,)), ((), ())), preferred_element_type=f32,
                              precision=lax.Precision.HIGHEST) + b_ref[...]
        for j in range(NDEV):
            o_ref[j] = jnp.broadcast_to(mod[j:j + 1, :], (8, W))

    return pl.pallas_call(body, name="ada_cols", out_shape=jax.ShapeDtypeStruct((NDEV, 8, W), f32),
                          compiler_params=_params())(c_all, w_ada, b_cols)


def _prenorm(x, scale, shift, g_pre, dep, tr=256):
    S, D = x.shape
    tr = min(tr, S)

    def body(x_ref, sc_ref, sh_ref, g_ref, dep_ref, h_ref):
        xv = x_ref[...]
        r = lax.rsqrt(jnp.mean(xv * xv, axis=-1, keepdims=True) + EPS)
        h_ref[...] = ((xv * r) * g_ref[...] * (1.0 + sc_ref[...]) + sh_ref[...]).astype(bf16)

    row = pl.BlockSpec((tr, D), lambda i: (i, 0))
    vec = pl.BlockSpec((1, D), lambda i: (0, 0))
    return pl.pallas_call(body, name="prenorm", out_shape=jax.ShapeDtypeStruct((S, D), bf16), grid=(S // tr,),
                          in_specs=[row, vec, vec, vec, ANY_SPEC], out_specs=row, compiler_params=_params("parallel"))(
                              x, scale, shift, g_pre, dep)


def _ext_rows(i, tr, S):
    g = lax.broadcasted_iota(jnp.int32, (tr + 16, 1), 0) + (i * tr - 8)
    return (g >= 0) & (g < S)


def _halo_specs(tr, S, C, col):
    nb8 = S // 8
    main = pl.BlockSpec((tr, C), lambda i: (i, col))
    prev = pl.BlockSpec((8, C), lambda i: (jnp.maximum(i * (tr // 8) - 1, 0), col))
    nxt = pl.BlockSpec((8, C), lambda i: (jnp.minimum((i + 1) * (tr // 8), nb8 - 1), col))
    return prev, main, nxt


def _conv_fwd(proj, conv_w, conv_b, g_conv, tr=256):
    S, C = proj.shape[0], proj.shape[1] // 8
    tr = min(tr, S)

    def body(up, um, un, cp, cm, cn, bg_ref, zc_ref, w_ref, cb_ref, g_ref, o_ref):
        i = pl.program_id(0)
        exists = _ext_rows(i, tr, S)
        u = jnp.concatenate([up[...], um[...], un[...]], axis=0)
        cg = jnp.concatenate([cp[...], cm[...], cn[...]], axis=0)
        t = jnp.where(exists, cg * u, 0.0)
        t_before = pltpu.roll(t, 1, 0)[8:tr + 8]
        t_after = pltpu.roll(t, tr + 15, 0)[8:tr + 8]
        w = w_ref[...]
        cv = w[0:1] * t_before + w[1:2] * t[8:tr + 8] + w[2:3] * t_after + cb_ref[...]
        yc = bg_ref[...] * cv
        rc = lax.rsqrt(jnp.mean(yc * yc, axis=-1, keepdims=True) + EPS)
        o_ref[...] = ((yc * rc) * g_ref[...] * _silu(zc_ref[...])).astype(bf16)

    u_specs = _halo_specs(tr, S, C, 0)
    c_specs = _halo_specs(tr, S, C, 2)
    vec = pl.BlockSpec((1, C), lambda i: (0, 0))
    return pl.pallas_call(
        body, name="conv_fwd", out_shape=jax.ShapeDtypeStruct((S, 2 * C), bf16), grid=(S // tr,),
        in_specs=[*u_specs, *c_specs, pl.BlockSpec((tr, C), lambda i: (i, 1)), pl.BlockSpec((tr, C), lambda i: (i, 3)),
                  pl.BlockSpec((8, C), lambda i: (0, 0)), vec, vec],
        out_specs=pl.BlockSpec((tr, C), lambda i: (i, 0)), compiler_params=_params("parallel"),
    )(proj, proj, proj, proj, proj, proj, proj, proj, conv_w, conv_b, g_conv)


def _branch_geometry(S, r, inter):
    L = S // r * inter
    nq = min(128, L)
    nk = min(nq + 2 * HALF_WIN * inter, L)
    assert L % nq == 0 and (L == nk or L >= nq + 2 * HALF_WIN * inter)
    return L, nq, nk, L // nq


QUAD = 4


def _to_quad(dst, src, S):
    n = S // QUAD
    for rho in range(QUAD):
        dst[pl.ds(rho * n, n), :] = src[pl.ds(rho, n, stride=QUAD), :]


def _block_rows(idx, r, inter, S, L, nq, nk, nblk):
    rho, qb = (0, idx) if r == 1 else (idx // nblk, idx % nblk)
    i0 = qb * nq
    ws = jnp.clip(i0 - HALF_WIN * inter, 0, L - nk)
    if r == 1:
        return pl.ds(pl.multiple_of(i0, 8), nq), pl.ds(pl.multiple_of(ws, 8), nk), i0 - ws
    assert r % (QUAD * inter) == 0
    step = r // QUAD // inter
    base = (rho % QUAD) * (S // QUAD) + rho // QUAD
    if step == 1:
        return pl.ds(pl.multiple_of(base + i0, 8), nq), pl.ds(pl.multiple_of(base + ws, 8), nk), i0 - ws
    return pl.ds(base + step * i0, nq, stride=step), pl.ds(base + step * ws, nk, stride=step), i0 - ws


N_CASES = 3
SCALE = HEAD_DIM ** -0.5
ATTN_UNROLL = 8


def _bias_shape(S):
    shapes = [_branch_geometry(S, r, inter)[1:3] for _, r, inter in BRANCHES]
    return (len(BRANCHES) * N_CASES * 2, max(nq for nq, _ in shapes), max(nk for _, nk in shapes))


def _bias_index(b, case, head):
    return (b * N_CASES + case) * 2 + head


def _fill_bias(bias_scr, sl_ref, S):
    sl = sl_ref[...]
    slope = (sl[0:1, 0:1], sl[0:1, HEAD_DIM:HEAD_DIM + 1])
    for b, (_, r, inter) in enumerate(BRANCHES):
        L, nq, nk, nblk = _branch_geometry(S, r, inter)
        rel = lax.broadcasted_iota(jnp.int32, (nq, nk), 0) - lax.broadcasted_iota(jnp.int32, (nq, nk), 1)
        for case in range(N_CASES):
            d = jnp.abs(rel + case * HALF_WIN)
            valid = d <= HALF_WIN * inter
            if inter > 1:
                valid = valid & (jnp.bitwise_and(d, inter - 1) == 0)
            dist = d.astype(f32) * float(r // inter)
            for head in range(2):
                bias_scr[_bias_index(b, case, head), 0:nq, 0:nk] = jnp.where(valid, -slope[head] * dist, NEG_INF)


def _head_slopes(n_heads):
    slopes = 2.0 ** (-8.0 * jnp.arange(1, n_heads + 1, dtype=f32) / n_heads)
    return jnp.broadcast_to(jnp.repeat(slopes.reshape(n_heads // 2, 2), HEAD_DIM, axis=1)[:, None, :],
                            (n_heads // 2, 8, PAIR))


def _attn_fwd(proj, slopes):
    S, C = proj.shape[0], proj.shape[1] // 8
    npair = C // PAIR

    def body(q_ref, k_ref, v_ref, sl_ref, o_ref, lse_ref, m_scr, l_scr, a_scr, bias_scr, q4_scr, k4_scr, v4_scr):
        lane = lax.broadcasted_iota(jnp.int32, (1, PAIR), 1)
        first = lane < HEAD_DIM
        _fill_bias(bias_scr, sl_ref, S)
        for dst, src in ((q4_scr, q_ref), (k4_scr, k_ref), (v4_scr, v_ref)):
            _to_quad(dst, src, S)

        for b, (_, r, inter) in enumerate(BRANCHES):
            L, nq, nk, nblk = _branch_geometry(S, r, inter)
            qs, ks, vs = (q_ref, k_ref, v_ref) if r == 1 else (q4_scr, k4_scr, v4_scr)

            def step(idx, carry, b=b, r=r, L=L, nq=nq, nk=nk, nblk=nblk, qs=qs, ks=ks, vs=vs):
                qrows, krows, off = _block_rows(idx, r, inter, S, L, nq, nk, nblk)
                case = off // HALF_WIN
                q2 = qs[qrows, :] * SCALE
                k2 = ks[krows, :].astype(bf16)
                v2 = vs[krows, :].astype(bf16)
                ms, accs = [], []
                for hh in range(2):
                    mine = first if hh == 0 else ~first
                    qh = jnp.where(mine, q2, 0.0).astype(bf16)
                    s = lax.dot_general(qh, k2, (((1,), (1,)), ((), ())), preferred_element_type=f32)
                    s = s + bias_scr[_bias_index(b, case, hh), 0:nq, 0:nk]
                    m = jnp.max(s, axis=-1, keepdims=True)
                    p = jnp.exp(s - m).astype(bf16)
                    vh = jnp.where(mine, v2, jnp.ones_like(v2))
                    ms.append(m)
                    accs.append(jnp.dot(p, vh, preferred_element_type=f32))
                m_scr[b, qrows, :] = jnp.where(first, ms[0], ms[1])
                a_scr[b, qrows, :] = jnp.where(first, accs[0], accs[1])
                l_scr[b, qrows, :] = jnp.where(first, accs[1], accs[0])
                return carry

            lax.fori_loop(0, S // nq, step, 0, unroll=min(ATTN_UNROLL, S // nq))

        n4 = S // QUAD
        ch = min(256, n4)
        nch = n4 // ch

        def merge(i, carry):
            rho, part = i // nch, i % nch
            sorted_rows = pl.ds(pl.multiple_of(rho * n4 + part * ch, 8), ch)
            token_rows = pl.ds(rho + QUAD * part * ch, ch, stride=QUAD)
            rows = (token_rows,) + (sorted_rows,) * (len(BRANCHES) - 1)
            ms = [m_scr[b, rows[b], :] for b in range(len(BRANCHES))]
            m = functools.reduce(jnp.maximum, ms)
            l = jnp.zeros((ch, PAIR), f32)
            acc = jnp.zeros((ch, PAIR), f32)
            for b in range(len(BRANCHES)):
                w = jnp.exp(ms[b] - m)
                l = l + w * pltpu.roll(l_scr[b, rows[b], :], HEAD_DIM, 1)
                acc = acc + w * a_scr[b, rows[b], :]
            o_ref[token_rows, :] = acc / l
            lse_ref[token_rows, :] = m + jnp.log(l)
            return carry

        lax.fori_loop(0, QUAD * nch, merge, 0)

    blk = lambda part: pl.BlockSpec((S, PAIR), lambda p: (0, part * npair + p))
    out = pl.BlockSpec((S, PAIR), lambda p: (0, p))
    return pl.pallas_call(
        body, name="attn_fwd",
        out_shape=(jax.ShapeDtypeStruct((S, C), f32), jax.ShapeDtypeStruct((S, C), f32)), grid=(npair,),
        in_specs=[blk(4), blk(5), blk(6), pl.BlockSpec((None, 8, PAIR), lambda p: (p, 0, 0))],
        out_specs=(out, out),
        scratch_shapes=[pltpu.VMEM((3, S, PAIR), f32)] * 3 + [pltpu.VMEM(_bias_shape(S), f32)]
        + [pltpu.VMEM((S, PAIR), f32)] * 3,
        compiler_params=_params("parallel"),
    )(proj, proj, proj, slopes)


def _attn_post(ycat, o, proj, g_attn, tr=256):
    S, C = o.shape
    tr = min(tr, S)

    def body(y_ref, o_ref, z_ref, g_ref, out_ref):
        del y_ref
        ov = o_ref[...]
        ra = lax.rsqrt(jnp.mean(ov * ov, axis=-1, keepdims=True) + EPS)
        out_ref[...] = ((ov * ra) * g_ref[...] * _silu(z_ref[...])).astype(bf16)

    return pl.pallas_call(
        body, name="attn_post", out_shape=jax.ShapeDtypeStruct(ycat.shape, ycat.dtype), grid=(S // tr,),
        in_specs=[HBM_SPEC, pl.BlockSpec((tr, C), lambda i: (i, 0)), pl.BlockSpec((tr, C), lambda i: (i, 7)),
                  pl.BlockSpec((1, C), lambda i: (0, 0))],
        out_specs=pl.BlockSpec((tr, C), lambda i: (i, 1)), input_output_aliases={0: 0},
        compiler_params=_params("arbitrary"),
    )(ycat, o, proj, g_attn)


def _sandwich(y, x, target, gate, g_post, tr=256):
    S, D = y.shape
    tr = min(tr, S)

    def body(y_ref, x_ref, t_ref, gate_ref, g_ref, dy_ref, dout_ref, sums_ref):
        i = pl.program_id(0)
        yv = y_ref[...]
        rp = lax.rsqrt(jnp.mean(yv * yv, axis=-1, keepdims=True) + EPS)
        yhat = yv * rp
        yn = yhat * g_ref[...]
        err = (x_ref[...] + gate_ref[...] * yn) - t_ref[...]
        dout = err * (1.0 / D)
        dout_ref[...] = dout
        dyn = dout * gate_ref[...]
        w = dyn * g_ref[...]
        dy_ref[...] = (rp * (w - yhat * jnp.mean(w * yhat, axis=-1, keepdims=True))).astype(bf16)
        loss = 0.5 * jnp.sum(jnp.mean(err * err, axis=-1, keepdims=True), axis=0, keepdims=True)
        row = lax.broadcasted_iota(jnp.int32, (8, D), 0)
        upd = jnp.where(row == 0, jnp.sum(dout * yn, axis=0, keepdims=True),
                        jnp.where(row == 1, jnp.sum(dyn * yhat, axis=0, keepdims=True),
                                  jnp.where(row == 2, loss, 0.0)))

        @pl.when(i == 0)
        def _():
            sums_ref[...] = upd

        @pl.when(i > 0)
        def _():
            sums_ref[...] += upd

    row = pl.BlockSpec((tr, D), lambda i: (i, 0))
    vec = pl.BlockSpec((1, D), lambda i: (0, 0))
    return pl.pallas_call(
        body, name="sandwich",
        out_shape=(jax.ShapeDtypeStruct((S, D), bf16), jax.ShapeDtypeStruct((S, D), f32), jax.ShapeDtypeStruct((8, D), f32)),
        grid=(S // tr,), in_specs=[row, row, row, vec, vec],
        out_specs=(row, row, pl.BlockSpec((8, D), lambda i: (0, 0))), compiler_params=_params("arbitrary"),
    )(y, x, target, gate, g_post)


def _conv_bwd(proj, dycat, conv_w, conv_b, g_conv, dep, tr=256):
    S, C = proj.shape[0], proj.shape[1] // 8
    tr = min(tr, S)
    n = tr + 16

    def body(*refs):
        ins, (w_ref, cb_ref, g_ref, _, dp_ref, sums_ref) = refs[:15], refs[15:]
        i = pl.program_id(0)
        exists = _ext_rows(i, tr, S)
        u, bg, cg, zc, dyn = (jnp.concatenate([ins[3 * t][...], ins[3 * t + 1][...], ins[3 * t + 2][...]], axis=0)
                              for t in range(5))
        w = w_ref[...]
        t = jnp.where(exists, cg * u, 0.0)
        t_before, t_after = pltpu.roll(t, 1, 0), pltpu.roll(t, n - 1, 0)
        cv = w[0:1] * t_before + w[1:2] * t + w[2:3] * t_after + cb_ref[...]
        yc = bg * cv
        rc = lax.rsqrt(jnp.mean(yc * yc, axis=-1, keepdims=True) + EPS)
        yhat = yc * rc
        sz = _silu(zc)
        wgt = dyn * g_ref[...] * sz
        dyc = rc * (wgt - yhat * jnp.mean(wgt * yhat, axis=-1, keepdims=True))
        dcv = jnp.where(exists, dyc * bg, 0.0)
        dt = w[0:1] * pltpu.roll(dcv, n - 1, 0) + w[1:2] * dcv + w[2:3] * pltpu.roll(dcv, 1, 0)
        mid = slice(8, tr + 8)
        dp_ref[:, 0:C] = (dt * cg)[mid].astype(bf16)
        dp_ref[:, C:2 * C] = (dyc * cv)[mid].astype(bf16)
        dp_ref[:, 2 * C:3 * C] = (dt * u)[mid].astype(bf16)
        dp_ref[:, 3 * C:4 * C] = (dyn * yhat * g_ref[...] * _dsilu(zc))[mid].astype(bf16)
        colsum = lambda v: jnp.sum(v[mid], axis=0, keepdims=True)
        parts = [colsum(dyn * yhat * sz), colsum(dcv), colsum(dcv * t_before), colsum(dcv * t), colsum(dcv * t_after)]
        row = lax.broadcasted_iota(jnp.int32, (8, C), 0)
        upd = jnp.zeros((8, C), f32)
        for j, pj in enumerate(parts):
            upd = jnp.where(row == j, pj, upd)

        @pl.when(i == 0)
        def _():
            sums_ref[...] = upd

        @pl.when(i > 0)
        def _():
            sums_ref[...] += upd

    specs = []
    for col in range(4):
        specs += _halo_specs(tr, S, C, col)
    specs += _halo_specs(tr, S, C, 0)
    vec = pl.BlockSpec((1, C), lambda i: (0, 0))
    return pl.pallas_call(
        body, name="conv_bwd",
        out_shape=(jax.ShapeDtypeStruct((S, 4 * C), bf16), jax.ShapeDtypeStruct((8, C), f32)), grid=(S // tr,),
        in_specs=[*specs, pl.BlockSpec((8, C), lambda i: (0, 0)), vec, vec, ANY_SPEC],
        out_specs=(pl.BlockSpec((tr, 4 * C), lambda i: (i, 0)), pl.BlockSpec((8, C), lambda i: (0, 0))),
        compiler_params=_params("arbitrary"),
    )(*([proj] * 12), dycat, dycat, dycat, conv_w, conv_b, g_conv, dep)


def _attn_post_bwd(o, proj, dycat, g_attn, dep, tr=256):
    S, C = o.shape
    tr = min(tr, S)

    def body(o_ref, z_ref, dy_ref, g_ref, dep_ref, do_ref, dz_ref, sums_ref):
        i = pl.program_id(0)
        ov, zv, dyn = o_ref[...], z_ref[...], dy_ref[...]
        ra = lax.rsqrt(jnp.mean(ov * ov, axis=-1, keepdims=True) + EPS)
        ohat = ov * ra
        sz = _silu(zv)
        wgt = dyn * g_ref[...] * sz
        do_ref[...] = ra * (wgt - ohat * jnp.mean(wgt * ohat, axis=-1, keepdims=True))
        dz_ref[...] = (dyn * ohat * g_ref[...] * _dsilu(zv)).astype(bf16)
        row = lax.broadcasted_iota(jnp.int32, (8, C), 0)
        upd = jnp.where(row == 0, jnp.sum(dyn * ohat * sz, axis=0, keepdims=True), 0.0)

        @pl.when(i == 0)
        def _():
            sums_ref[...] = upd

        @pl.when(i > 0)
        def _():
            sums_ref[...] += upd

    return pl.pallas_call(
        body, name="attn_post_bwd",
        out_shape=(jax.ShapeDtypeStruct((S, C), f32), jax.ShapeDtypeStruct((4, S, C), bf16),
                   jax.ShapeDtypeStruct((8, C), f32)),
        grid=(S // tr,),
        in_specs=[pl.BlockSpec((tr, C), lambda i: (i, 0)), pl.BlockSpec((tr, C), lambda i: (i, 7)),
                  pl.BlockSpec((tr, C), lambda i: (i, 1)), pl.BlockSpec((1, C), lambda i: (0, 0)), ANY_SPEC],
        out_specs=(pl.BlockSpec((tr, C), lambda i: (i, 0)), pl.BlockSpec((None, tr, C), lambda i: (3, i, 0)),
                   pl.BlockSpec((8, C), lambda i: (0, 0))),
        compiler_params=_params("arbitrary"),
    )(o, proj, dycat, g_attn, dep)


def _attn_bwd(proj, o, do, lse, slopes, dqkvz, dep):
    S, C = o.shape
    npair = C // PAIR

    def body(q_ref, k_ref, v_ref, o_ref, do_ref, lse_ref, sl_ref, old_ref, dep_ref, dqkv_ref,
             acc_scr, dl_scr, quad_scr, bias_scr):
        lane = lax.broadcasted_iota(jnp.int32, (1, PAIR), 1)
        first = lane < HEAD_DIM
        _fill_bias(bias_scr, sl_ref, S)
        ch = min(256, S)

        def prep(i, carry):
            rows = pl.ds(pl.multiple_of(i * ch, 8), ch)
            prod = do_ref[rows, :] * o_ref[rows, :]
            d0 = jnp.sum(jnp.where(first, prod, 0.0), axis=-1, keepdims=True)
            d1 = jnp.sum(jnp.where(first, 0.0, prod), axis=-1, keepdims=True)
            dl_scr[rows, :] = jnp.where(first, d0, d1)
            zero = jnp.zeros((ch, PAIR), f32)
            for order in range(2):
                for t in range(3):
                    acc_scr[order, t, rows, :] = zero
            return carry

        lax.fori_loop(0, S // ch, prep, 0)
        token_srcs = (q_ref, k_ref, v_ref, do_ref, lse_ref, dl_scr)
        for j, src in enumerate(token_srcs):
            _to_quad(quad_scr.at[j], src, S)

        for b, (_, r, inter) in enumerate(BRANCHES):
            L, nq, nk, nblk = _branch_geometry(S, r, inter)
            order = 0 if r == 1 else 1
            srcs = token_srcs if r == 1 else tuple(quad_scr.at[j] for j in range(6))

            def step(idx, carry, b=b, r=r, L=L, nq=nq, nk=nk, nblk=nblk, order=order, srcs=srcs):
                qs, ks, vs, dos, lses, dls = srcs
                dq_scr, dk_scr, dv_scr = (acc_scr.at[order, t] for t in range(3))
                qrows, krows, off = _block_rows(idx, r, inter, S, L, nq, nk, nblk)
                case = off // HALF_WIN
                q2 = qs[qrows, :] * SCALE
                k2 = ks[krows, :].astype(bf16)
                v2 = vs[krows, :].astype(bf16)
                do2 = dos[qrows, :]
                lse2 = lses[qrows, :]
                dl2 = dls[qrows, :]
                dq2 = jnp.zeros((nq, PAIR), f32)
                dk2 = jnp.zeros((nk, PAIR), f32)
                dv2 = jnp.zeros((nk, PAIR), f32)
                for hh in range(2):
                    mine = first if hh == 0 else ~first
                    lo = hh * HEAD_DIM
                    qh = jnp.where(mine, q2, 0.0).astype(bf16)
                    doh = jnp.where(mine, do2, 0.0).astype(bf16)
                    s = lax.dot_general(qh, k2, (((1,), (1,)), ((), ())), preferred_element_type=f32)
                    s = s + bias_scr[_bias_index(b, case, hh), 0:nq, 0:nk]
                    p = jnp.exp(s - lse2[:, lo:lo + 1])
                    dv2 = dv2 + lax.dot_general(p.astype(bf16), doh, (((0,), (0,)), ((), ())), preferred_element_type=f32)
                    dp = lax.dot_general(doh, v2, (((1,), (1,)), ((), ())), preferred_element_type=f32)
                    ds = (p * (dp - dl2[:, lo:lo + 1])).astype(bf16)
                    dq2 = dq2 + jnp.where(mine, jnp.dot(ds, k2, preferred_element_type=f32), 0.0)
                    dk2 = dk2 + lax.dot_general(ds, qh, (((0,), (0,)), ((), ())), preferred_element_type=f32)
                dq_scr[qrows, :] = dq_scr[qrows, :] + dq2
                dk_scr[krows, :] = dk_scr[krows, :] + dk2
                dv_scr[krows, :] = dv_scr[krows, :] + dv2
                return carry

            lax.fori_loop(0, S // nq, step, 0, unroll=min(ATTN_UNROLL, S // nq))

        n4 = S // QUAD
        for t in range(3):
            for rho in range(QUAD):
                token_rows = pl.ds(rho, n4, stride=QUAD)
                acc_scr[0, t, token_rows, :] = acc_scr[0, t, token_rows, :] + acc_scr[1, t, pl.ds(rho * n4, n4), :]
        dqkv_ref[0] = (acc_scr[0, 0] * SCALE).astype(bf16)
        dqkv_ref[1] = acc_scr[0, 1].astype(bf16)
        dqkv_ref[2] = acc_scr[0, 2].astype(bf16)

    blk = lambda part: pl.BlockSpec((S, PAIR), lambda p: (0, part * npair + p))
    own = pl.BlockSpec((S, PAIR), lambda p: (0, p))
    return pl.pallas_call(
        body, name="attn_bwd", out_shape=jax.ShapeDtypeStruct(dqkvz.shape, dqkvz.dtype), grid=(npair,),
        in_specs=[blk(4), blk(5), blk(6), own, own, own, pl.BlockSpec((None, 8, PAIR), lambda p: (p, 0, 0)),
                  ANY_SPEC, ANY_SPEC],
        out_specs=pl.BlockSpec((3, S, PAIR), lambda p: (0, 0, p)), input_output_aliases={7: 0},
        scratch_shapes=[pltpu.VMEM((2, 3, S, PAIR), f32), pltpu.VMEM((S, PAIR), f32), pltpu.VMEM((6, S, PAIR), f32),
                        pltpu.VMEM(_bias_shape(S), f32)],
        compiler_params=_params("arbitrary"),
    )(proj, proj, proj, o, do, lse, slopes, dqkvz, dep)


def _prenorm_bwd(dh, x, dout, scale, g_pre, tr=256):
    S, D = x.shape
    tr = min(tr, S)

    def body(dh_ref, x_ref, dout_ref, sc_ref, g_ref, gx_ref, sums_ref):
        i = pl.program_id(0)
        xv, dhv = x_ref[...], dh_ref[...]
        r = lax.rsqrt(jnp.mean(xv * xv, axis=-1, keepdims=True) + EPS)
        xn = xv * r
        dxn = dhv * (g_ref[...] * (1.0 + sc_ref[...]))
        gx_ref[...] = dout_ref[...] + r * (dxn - xn * jnp.mean(dxn * xn, axis=-1, keepdims=True))
        dhx = dhv * xn
        row = lax.broadcasted_iota(jnp.int32, (8, D), 0)
        upd = jnp.where(row == 0, jnp.sum(dhv, axis=0, keepdims=True),
                        jnp.where(row == 1, jnp.sum(dhx, axis=0, keepdims=True) * g_ref[...],
                                  jnp.where(row == 2, jnp.sum(dhx, axis=0, keepdims=True) * (1.0 + sc_ref[...]), 0.0)))

        @pl.when(i == 0)
        def _():
            sums_ref[...] = upd

        @pl.when(i > 0)
        def _():
            sums_ref[...] += upd

    row = pl.BlockSpec((tr, D), lambda i: (i, 0))
    vec = pl.BlockSpec((1, D), lambda i: (0, 0))
    return pl.pallas_call(
        body, name="prenorm_bwd",
        out_shape=(jax.ShapeDtypeStruct((S, D), f32), jax.ShapeDtypeStruct((8, D), f32)), grid=(S // tr,),
        in_specs=[row, row, row, vec, vec], out_specs=(row, pl.BlockSpec((8, D), lambda i: (0, 0))),
        compiler_params=_params("arbitrary"),
    )(dh, x, dout, scale, g_pre)


def _adamw(w, g, m, v):
    m = ADAM_B1 * m + (1.0 - ADAM_B1) * g
    v = ADAM_B2 * v + (1.0 - ADAM_B2) * (g * g)
    m_hat = m / (1.0 - ADAM_B1 ** ADAM_STEP)
    v_hat = v / (1.0 - ADAM_B2 ** ADAM_STEP)
    delta = -ADAM_LR * (m_hat / (jnp.sqrt(v_hat) + ADAM_EPS) + ADAM_WD * w)
    return delta, m, v


def _sum_rows(parts, dep):
    P = parts.shape[1]

    def body(p_ref, dep_ref, o_ref):
        acc = p_ref[0:1, :]
        for j in range(1, NDEV):
            acc = acc + p_ref[j:j + 1, :]
        o_ref[...] = jnp.broadcast_to(acc, (8, P))

    vmem = pl.BlockSpec(memory_space=pltpu.VMEM)
    return pl.pallas_call(body, name="sum_small", out_shape=jax.ShapeDtypeStruct((8, P), f32),
                          in_specs=[vmem, ANY_SPEC], out_specs=vmem, compiler_params=_params())(parts, dep)


def _adamw_small(w, g, m, v):
    def body(w_ref, g_ref, m_ref, v_ref, d_ref, nm_ref, nv_ref):
        d_ref[...], nm_ref[...], nv_ref[...] = _adamw(w_ref[...], g_ref[...], m_ref[...], v_ref[...])

    out = jax.ShapeDtypeStruct(w.shape, f32)
    return pl.pallas_call(body, name="adamw_small", out_shape=(out, out, out), compiler_params=_params())(w, g, m, v)


def _adamw_sharded(parts, own, w, m, v, name, tr=128):
    R, Cc = w.shape
    tr = min(tr, R)
    n = parts.shape[0]

    def body(p_ref, own_ref, w_ref, m_ref, v_ref, g_ref, d_ref, nm_ref, nv_ref):
        g = own_ref[...].astype(f32)
        for j in range(n):
            g = g + p_ref[j].astype(f32)
        g_ref[...] = g
        d_ref[...], nm_ref[...], nv_ref[...] = _adamw(w_ref[...], g, m_ref[...], v_ref[...])

    row = pl.BlockSpec((tr, Cc), lambda i: (i, 0))
    out = jax.ShapeDtypeStruct((R, Cc), f32)
    return pl.pallas_call(
        body, name=name, out_shape=(out, out, out, out), grid=(R // tr,),
        in_specs=[pl.BlockSpec((n, tr, Cc), lambda i: (0, i, 0)), row, row, row, row],
        out_specs=(row, row, row, row), compiler_params=_params("parallel"),
    )(parts, own, w, m, v)


def _adamw_ada(c_t, dmod_cols, w, m, v, dep, tr=256):
    D, W = w.shape
    tr = min(tr, D)

    def body(c_ref, dm_ref, w_ref, m_ref, v_ref, dep_ref, g_ref, d_ref, nm_ref, nv_ref):
        cv, dm = c_ref[...], dm_ref[...]
        g = cv[:, 0:1] * dm[0:1, :]
        for b in range(1, NDEV):
            g = g + cv[:, b:b + 1] * dm[b:b + 1, :]
        g_ref[...] = g
        d_ref[...], nm_ref[...], nv_ref[...] = _adamw(w_ref[...], g, m_ref[...], v_ref[...])

    row = pl.BlockSpec((tr, W), lambda i: (i, 0))
    out = jax.ShapeDtypeStruct((D, W), f32)
    return pl.pallas_call(
        body, name="adamw_ada", out_shape=(out, out, out, out), grid=(D // tr,),
        in_specs=[pl.BlockSpec((tr, NDEV), lambda i: (i, 0)), pl.BlockSpec((NDEV, W), lambda i: (0, 0)), row, row, row,
                  ANY_SPEC],
        out_specs=(row, row, row, row), compiler_params=_params("parallel"),
    )(c_t, dmod_cols, w, m, v, dep)


def kernel(x, c, w_ada, b_ada, g_pre, w_in, conv_w, conv_b, g_conv, g_attn, w_out, g_post, loss_target, m_w_ada, m_b_ada, m_g_pre, m_w_in, m_conv_w, m_conv_b, m_g_conv, m_g_attn, m_w_out, m_g_post, v_w_ada, v_b_ada, v_g_pre, v_w_in, v_conv_w, v_conv_b, v_g_conv, v_g_attn, v_w_out, v_g_post):
    S, D = x.shape[1], x.shape[2]
    C = D // 2
    W = w_ada.shape[2]
    CW = conv_w.shape[2]
    me = 4 * lax.axis_index("x") + 2 * lax.axis_index("y") + lax.axis_index("c")
    x2, tgt = x[0], loss_target[0]
    w_ada2, w_in2, w_out2 = w_ada[0], w_in[0], w_out[0]

    R = D // NDEV
    core = lax.axis_index("c").astype(jnp.int32).reshape(1)

    cw_slab = jnp.zeros((8, CW), f32).at[:3].set(conv_w[0])
    c_blocks, cw_g = _all_gather([_silu_block(c.reshape(D // 128, 128)), cw_slab], "gather_c")
    c_all = c_blocks.reshape(NDEV, D)
    conv_w_full = jnp.transpose(cw_g, (1, 0, 2)).reshape(8, C)
    b_cols = lax.dynamic_slice_in_dim(b_ada, me * W, W, axis=1)
    (mod_slabs,) = _all_to_all([_ada_cols(c_all, w_ada2, b_cols)], "scatter_mod")
    mod = mod_slabs[:, 0, :].reshape(1, 3 * D)
    shift, scale, gate = mod[:, :D], mod[:, D:2 * D], mod[:, 2 * D:]

    land_i = lax.dynamic_update_slice(lax.empty((NDEV, D, C), bf16), w_in2.astype(bf16)[None], (me, 0, 0))
    land_o = lax.dynamic_update_slice(lax.empty((NDEV, R, D), bf16), w_out2.astype(bf16)[None], (me, 0, 0))
    wi_send, wi_recv, land_i, w_token = _w_in_start(land_i, [mod_slabs])

    me_arr = me.astype(jnp.int32).reshape(1)
    h = _prenorm(x2, scale, shift, g_pre, w_token)
    land_i = _w_in_sibling(land_i, wi_recv, after=[h])
    proj = _in_proj_part("in_proj_a", h, land_i, None, me_arr, 0, 1, 2)
    fi_send, fi_recv, land_i = _w_in_relay(land_i, wi_recv, after=[proj])
    proj = _in_proj_part("in_proj_b", h, land_i, proj, me_arr, 2, 2, 2)
    land_i = _w_in_forwarded(land_i, fi_recv, after=[proj])
    proj = _in_proj_part("in_proj_c", h, land_i, proj, me_arr, 3, 2, 2)
    (di_send, di_recv, wo_send, wo_recv), land_i, land_o = _w_in_diag(land_i, land_o, fi_recv, after=[proj])
    proj = _in_proj_part("in_proj_d", h, land_i, proj, me_arr, 6, 1, 1)
    win_g = _w_in_finish(land_i, wi_send, fi_send, di_send, di_recv, after=[proj])
    proj = _in_proj_part("in_proj_e", h, win_g, proj, me_arr, 7, 1, 1)
    (fo_send, fo_recv), (land_o,), _ = _weights_forward("w_out_forward", land_o, wo_recv, after=[proj])
    slopes = _head_slopes(C // HEAD_DIM)
    ycat = _conv_fwd(proj, conv_w_full, conv_b, g_conv)
    o, lse = _attn_fwd(proj, slopes)
    ycat = _attn_post(ycat, o, proj, g_attn)
    wout_g = _weights_wait("w_out_wait", land_o, wo_send, wo_recv, fo_send, fo_recv, after=[ycat])
    wout_full = wout_g.reshape(D, D)
    y = _matmul(ycat, wout_full, name="out_proj", out_dtype=f32)
    dy, dout, post_sums = _sandwich(y, x2, tgt, gate, g_post)

    loss = lax.psum(post_sums[2, 0], ("x", "y", "c"))
    gw_out = _matmul(ycat, dy, name="out_proj_dw", out_dtype=bf16, ta=True, dep=loss.reshape(1, 1)).reshape(NDEV, R, D)
    po_send, po_recv, gw_out, pair_o, po_token = _pair_start("g_out_pair_start", gw_out)
    dycat = _matmul(dy, wout_full, name="out_proj_dx", out_dtype=f32, tb=True, dep=po_token)
    gw_out, pair_o = _pair_wait("g_out_pair_wait", gw_out, pair_o, po_send, po_recv, after=[dycat])
    sum_o = _pair_sum("g_out_pair_sum", gw_out, pair_o, core)
    co_send, co_recv, sum_o, land_go, co_token = _chip_start(
        "g_out_chip_start", sum_o, jnp.zeros((NCHIP, R, D), bf16), 0)
    dpc, conv_sums = _conv_bwd(proj, dycat, conv_w_full, conv_b, g_conv, co_token)
    gw_c = _matmul(h, dpc, name="in_proj_dw_conv", out_dtype=bf16, ta=True, out_slots=4)
    pc_send, pc_recv, gw_c, pair_c, pc_token = _pair_start("g_conv_pair_start", gw_c)
    do, dpa, attn_sums = _attn_post_bwd(o, proj, dycat, g_attn, pc_token)
    gw_c, pair_c = _pair_wait("g_conv_pair_wait", gw_c, pair_c, pc_send, pc_recv, after=[do])
    sum_c = _pair_sum("g_conv_pair_sum", gw_c, pair_c, core)
    cc_send, cc_recv, sum_c, land_gi, cc_token = _chip_start(
        "g_conv_chip_start", sum_c, jnp.zeros((NCHIP, D, C), bf16), 0)
    dpa = _attn_bwd(proj, o, do, lse, slopes, dpa, cc_token)
    gw_a = _matmul(h, dpa, name="in_proj_dw_attn", out_dtype=bf16, ta=True, b_slots=True, out_slots=4)
    pa_send, pa_recv, gw_a, pair_a, pa_token = _pair_start("g_attn_pair_start", gw_a)
    gw_a, pair_a = _pair_wait("g_attn_pair_wait", gw_a, pair_a, pa_send, pa_recv, after=[pa_token])
    sum_a = _pair_sum("g_attn_pair_sum", gw_a, pair_a, core)
    part_a, part_b = (0, 3 * D // 4), (3 * D // 4, D // 4)
    ca_send, ca_recv, sum_a, land_gi, ca_token = _chip_start("g_attn_chip_start_a", sum_a, land_gi, 4, part_a)
    dh = _matmul_slabs_t(dpc, dpa, win_g, name="in_proj_dx", dep=ca_token)
    grad_x, pre_sums = _prenorm_bwd(dh, x2, dout, scale, g_pre)

    small = jnp.concatenate([pre_sums[0:1], pre_sums[1:2], post_sums[0:1],
                             pre_sums[2:3], post_sums[1:2],
                             conv_sums[2:3], conv_sums[3:4], conv_sums[4:5],
                             conv_sums[1:2], conv_sums[0:1], attn_sums[0:1]], axis=1)
    (small_all,) = _all_gather([small.reshape(8 * D // 128, 128)], "gather_small")
    cb_send, cb_recv, sum_a, land_gi, cb_token = _chip_start("g_attn_chip_start_b", sum_a, land_gi, 4, part_b,
                                                             after=[small_all])
    small_all = small_all.reshape(NDEV, 8 * D)
    tot = _sum_rows(small_all, cb_token)[0:1]

    g_b_ada = tot[:, :3 * D]
    g_g_pre, g_g_post = tot[:, 3 * D:4 * D], tot[:, 4 * D:5 * D]
    g_conv_w_full = tot[:, 5 * D:5 * D + 3 * C].reshape(3, C)
    g_conv_w = lax.dynamic_slice_in_dim(g_conv_w_full, me * CW, CW, axis=1)[None]
    g_conv_b, g_g_conv, g_g_attn = (tot[:, 5 * D + (3 + t) * C:5 * D + (4 + t) * C] for t in range(3))

    dmod_cols = lax.dynamic_slice_in_dim(small_all[:, :3 * D], me * W, W, axis=1)
    g_w_ada, d_w_ada, nm_w_ada, nv_w_ada = _adamw_ada(c_all.T, dmod_cols, w_ada2, m_w_ada[0], v_w_ada[0], cb_token)

    sum_o, land_go = _chip_wait("g_out_chip_wait", sum_o, land_go, co_send, co_recv, 0, after=[g_w_ada])
    own_out = lax.dynamic_index_in_dim(sum_o, me // 2, 0, keepdims=False)
    g_w_out, d_w_out, nm_w_out, nv_w_out = _adamw_sharded(land_go, own_out, w_out2, m_w_out[0], v_w_out[0], "adamw_w_out")
    sum_c, land_gi = _chip_wait("g_conv_chip_wait", sum_c, land_gi, cc_send, cc_recv, 0, after=[g_w_out])
    sum_a, land_gi = _chip_wait("g_attn_chip_wait_a", sum_a, land_gi, ca_send, ca_recv, 4, [g_w_out], part_a)
    sum_a, land_gi = _chip_wait("g_attn_chip_wait_b", sum_a, land_gi, cb_send, cb_recv, 4, [g_w_out], part_b)
    own_in = jnp.where(me < 4, lax.dynamic_index_in_dim(sum_c, (me % 4) // 2, 0, keepdims=False),
                       lax.dynamic_index_in_dim(sum_a, (me % 4) // 2, 0, keepdims=False))
    g_w_in, d_w_in, nm_w_in, nv_w_in = _adamw_sharded(land_gi, own_in, w_in2, m_w_in[0], v_w_in[0], "adamw_w_in")

    pack = lambda *vs: jnp.concatenate([a.reshape(1, -1) for a in vs], axis=1)
    smalls = [(b_ada, g_b_ada, m_b_ada, v_b_ada), (g_pre, g_g_pre, m_g_pre, v_g_pre),
              (conv_w, g_conv_w, m_conv_w, v_conv_w), (conv_b, g_conv_b, m_conv_b, v_conv_b),
              (g_conv, g_g_conv, m_g_conv, v_g_conv), (g_attn, g_g_attn, m_g_attn, v_g_attn),
              (g_post, g_g_post, m_g_post, v_g_post)]
    packed = [pack(*[s[t] for s in smalls]) for t in range(4)]
    npad = -packed[0].shape[1] % 128
    packed = [jnp.pad(p, ((0, 0), (0, npad)), constant_values=1.0) for p in packed]
    d_s, nm_s, nv_s = _adamw_small(*packed)

    def unpack(vec):
        out, at = [], 0
        for s in smalls:
            n = s[0].size
            out.append(vec[:, at:at + n].reshape(s[0].shape))
            at += n
        return out

    d_b_ada, d_g_pre, d_conv_w, d_conv_b, d_g_conv, d_g_attn, d_g_post = unpack(d_s)
    nm_b_ada, nm_g_pre, nm_conv_w, nm_conv_b, nm_g_conv, nm_g_attn, nm_g_post = unpack(nm_s)
    nv_b_ada, nv_g_pre, nv_conv_w, nv_conv_b, nv_g_conv, nv_g_attn, nv_g_post = unpack(nv_s)

    return (loss, grad_x[None],
            g_w_ada[None], g_b_ada, g_g_pre, g_w_in[None], g_conv_w, g_conv_b, g_g_conv, g_g_attn, g_w_out[None], g_g_post,
            d_w_ada[None], d_b_ada, d_g_pre, d_w_in[None], d_conv_w, d_conv_b, d_g_conv, d_g_attn, d_w_out[None], d_g_post,
            nm_w_ada[None], nm_b_ada, nm_g_pre, nm_w_in[None], nm_conv_w, nm_conv_b, nm_g_conv, nm_g_attn, nm_w_out[None], nm_g_post,
            nv_w_ada[None], nv_b_ada, nv_g_pre, nv_w_in[None], nv_conv_w, nv_conv_b, nv_g_conv, nv_g_attn, nv_w_out[None], nv_g_post)
```

```python
import functools
import math

import jax
import jax.numpy as jnp
from jax import lax
from jax.experimental import pallas as pl
from jax.experimental.pallas import tpu as pltpu

f32 = jnp.float32
bf16 = jnp.bfloat16

NDEV = 8
HEAD_DIM = 64
PAIR = 2 * HEAD_DIM
BRANCHES = ((128, 1, 1), (512, 4, 1), (2048, 16, 2))
HALF_WIN = 64
EPS = 1e-6
NEG_INF = -1e30
ADAM_LR, ADAM_B1, ADAM_B2, ADAM_EPS, ADAM_WD, ADAM_STEP = 0.001, 0.9, 0.999, 1e-08, 0.01, 10
MESH = pl.DeviceIdType.MESH
VMEM_LIMIT = 56 * 1024 * 1024
HBM_SPEC = pl.BlockSpec(memory_space=pltpu.HBM)
ANY_SPEC = pl.BlockSpec(memory_space=pl.ANY)
SEM_SPEC = pl.BlockSpec(memory_space=pltpu.SEMAPHORE)


def _params(*sem):
    return pltpu.CompilerParams(dimension_semantics=sem or None, vmem_limit_bytes=VMEM_LIMIT)


def _silu(z):
    return z * jax.nn.sigmoid(z)


def _dsilu(z):
    s = jax.nn.sigmoid(z)
    return s * (1.0 + z * (1.0 - s))


def _my_place():
    x, y, c = lax.axis_index("x"), lax.axis_index("y"), lax.axis_index("c")
    return x, y, c, 4 * x + 2 * y + c


def _peer(x, y, c, k):
    px, py, pc = x ^ (k >> 2 & 1), y ^ (k >> 1 & 1), c ^ (k & 1)
    return (px, py, pc), 4 * px + 2 * py + pc


def _all_gather(arrays, name):
    n = len(arrays)

    def body(*refs):
        srcs, dsts = refs[:n], refs[n:2 * n]
        send_sems, recv_sems, local_sems = refs[2 * n:]
        x, y, c, me = _my_place()
        locals_, sends = [], []
        for t in range(n):
            own = pltpu.make_async_copy(srcs[t], dsts[t].at[me], local_sems.at[t])
            own.start()
            locals_.append(own)
            for k in range(1, NDEV):
                peer, pidx = _peer(x, y, c, k)
                cp = pltpu.make_async_remote_copy(
                    src_ref=srcs[t], dst_ref=dsts[t].at[me], send_sem=send_sems.at[t, k],
                    recv_sem=recv_sems.at[t, k], device_id=peer, device_id_type=MESH)
                cp.start()
                sends.append(cp)
        for t in range(n):
            for k in range(1, NDEV):
                peer, pidx = _peer(x, y, c, k)
                pltpu.make_async_remote_copy(
                    src_ref=srcs[t], dst_ref=dsts[t].at[pidx], send_sem=send_sems.at[t, k],
                    recv_sem=recv_sems.at[t, k], device_id=peer, device_id_type=MESH).wait_recv()
        for cp in sends:
            cp.wait_send()
        for cp in locals_:
            cp.wait()

    return pl.pallas_call(
        body, name=name,
        out_shape=tuple(jax.ShapeDtypeStruct((NDEV,) + a.shape, a.dtype) for a in arrays),
        in_specs=[HBM_SPEC] * n, out_specs=tuple([HBM_SPEC] * n),
        scratch_shapes=[pltpu.SemaphoreType.DMA((n, NDEV)), pltpu.SemaphoreType.DMA((n, NDEV)),
                        pltpu.SemaphoreType.DMA((n,))],
    )(*arrays)


def _all_to_all(arrays, name):
    n = len(arrays)

    def body(*refs):
        srcs, dsts = refs[:n], refs[n:2 * n]
        send_sems, recv_sems, local_sems = refs[2 * n:]
        x, y, c, me = _my_place()
        locals_, sends = [], []
        for t in range(n):
            own = pltpu.make_async_copy(srcs[t].at[me], dsts[t].at[me], local_sems.at[t])
            own.start()
            locals_.append(own)
            for k in range(1, NDEV):
                peer, pidx = _peer(x, y, c, k)
                cp = pltpu.make_async_remote_copy(
                    src_ref=srcs[t].at[pidx], dst_ref=dsts[t].at[me], send_sem=send_sems.at[t, k],
                    recv_sem=recv_sems.at[t, k], device_id=peer, device_id_type=MESH)
                cp.start()
                sends.append(cp)
        for t in range(n):
            for k in range(1, NDEV):
                peer, pidx = _peer(x, y, c, k)
                pltpu.make_async_remote_copy(
                    src_ref=srcs[t].at[pidx], dst_ref=dsts[t].at[pidx], send_sem=send_sems.at[t, k],
                    recv_sem=recv_sems.at[t, k], device_id=peer, device_id_type=MESH).wait_recv()
        for cp in sends:
            cp.wait_send()
        for cp in locals_:
            cp.wait()

    return pl.pallas_call(
        body, name=name,
        out_shape=tuple(jax.ShapeDtypeStruct(a.shape, a.dtype) for a in arrays),
        in_specs=[HBM_SPEC] * n, out_specs=tuple([HBM_SPEC] * n),
        scratch_shapes=[pltpu.SemaphoreType.DMA((n, NDEV)), pltpu.SemaphoreType.DMA((n, NDEV)),
                        pltpu.SemaphoreType.DMA((n,))],
    )(*arrays)


def _comm_call(name, arrays, sems, new_sems, body, after=(), token=False):
    na, ns, nn, nf = len(arrays), len(sems), len(new_sems), len(after)

    def kern(*refs):
        ins, outs = refs[:na + ns + nf], refs[na + ns + nf:]
        body(ins[:na], ins[na:na + ns], outs[:nn])
        if token:
            outs[nn + na][...] = jnp.zeros((8, 128), f32)

    out_shape = ([pltpu.SemaphoreType.DMA(s) for s in new_sems] + [pltpu.HBM(a.shape, a.dtype) for a in arrays]
                 + ([jax.ShapeDtypeStruct((8, 128), f32)] if token else []))
    out_specs = [SEM_SPEC] * nn + [HBM_SPEC] * na + ([pl.BlockSpec(memory_space=pltpu.VMEM)] if token else [])
    res = pl.pallas_call(
        kern, name=name, out_shape=tuple(out_shape),
        in_specs=[HBM_SPEC] * na + [SEM_SPEC] * ns + [ANY_SPEC] * nf, out_specs=tuple(out_specs),
        input_output_aliases={t: nn + t for t in range(na)},
        compiler_params=pltpu.CompilerParams(has_side_effects=pltpu.SideEffectType.DATAFLOW_SIDE_EFFECTING),
    )(*[pltpu.with_memory_space_constraint(a, pltpu.HBM) for a in arrays], *sems, *after)
    return list(res[:nn]), list(res[nn:nn + na]), (res[nn + na] if token else None)


def _remote(src, dst, send_sem, recv_sem, device):
    return pltpu.make_async_remote_copy(src_ref=src, dst_ref=dst, send_sem=send_sem, recv_sem=recv_sem,
                                        device_id=device, device_id_type=MESH)


SAME_CORE = (2, 4, 6)
VIA_SIBLING = (3, 5, 7)


def _weights_forward(name, land, recv, after):
    def body(a, s, new):
        (land,), (recv,), (fsend, frecv) = a, s, new
        x, y, c, me = _my_place()
        sibling, _ = _peer(x, y, c, 1)
        for k in SAME_CORE:
            peer, slot = _peer(x, y, c, k)
            _remote(land.at[slot], land.at[slot], fsend.at[k], recv.at[k], peer).wait_recv()
            _remote(land.at[slot], land.at[slot], fsend.at[k], frecv.at[k ^ 1], sibling).start()

    return _comm_call(name, [land], [recv], [(NDEV,), (NDEV,)], body, after=after)


def _weights_wait(name, land, send, recv, fsend, frecv, after):
    def body(a, s, new):
        (land,), (send, recv, fsend, frecv) = a, s
        x, y, c, me = _my_place()
        sibling, sib_slot = _peer(x, y, c, 1)
        _remote(land.at[sib_slot], land.at[sib_slot], send.at[1], recv.at[1], sibling).wait_recv()
        for k in VIA_SIBLING:
            _, slot = _peer(x, y, c, k)
            _remote(land.at[slot], land.at[slot], fsend.at[k ^ 1], frecv.at[k], sibling).wait_recv()
        for k in (1,) + SAME_CORE:
            peer, _ = _peer(x, y, c, k)
            _remote(land.at[me], land.at[me], send.at[k], recv.at[k], peer).wait_send()
        for k in SAME_CORE:
            _, slot = _peer(x, y, c, k)
            _remote(land.at[slot], land.at[slot], fsend.at[k], frecv.at[k ^ 1], sibling).wait_send()

    return _comm_call(name, [land], [send, recv, fsend, frecv], [], body, after=after)[1][0]


def _diag_relay(x, y, c):
    slot = 4 * (x ^ (1 - c)) + 2 * (y ^ c) + c
    return slot, (x ^ c, y ^ (1 - c), c)


def _w_in_start(land, after):
    def body(a, s, new):
        (land,), (send, recv) = a, new
        x, y, c, me = _my_place()
        for k in (1, 2, 4):
            peer, _ = _peer(x, y, c, k)
            _remote(land.at[me], land.at[me], send.at[k], recv.at[k], peer).start()

    (send, recv), (land,), token = _comm_call("w_in_start", [land], [], [(NDEV,), (NDEV,)], body, after=after, token=True)
    return send, recv, land, token


def _w_in_sibling(land, recv, after):
    def body(a, s, new):
        (land,), (recv,) = a, s
        x, y, c, me = _my_place()
        sibling, slot = _peer(x, y, c, 1)
        _remote(land.at[slot], land.at[slot], recv.at[1], recv.at[1], sibling).wait_recv()

    return _comm_call("w_in_sibling", [land], [recv], [], body, after=after)[1][0]


def _w_in_relay(land, recv, after):
    def body(a, s, new):
        (land,), (recv,), (fsend, frecv) = a, s, new
        x, y, c, me = _my_place()
        sibling, _ = _peer(x, y, c, 1)
        for k in (2, 4):
            peer, slot = _peer(x, y, c, k)
            _remote(land.at[slot], land.at[slot], fsend.at[k], recv.at[k], peer).wait_recv()
        slot, target = _diag_relay(x, y, c)
        _remote(land.at[slot], land.at[slot], fsend.at[6], frecv.at[6], target).start()
        for k in (2, 4):
            _, slot = _peer(x, y, c, k)
            _remote(land.at[slot], land.at[slot], fsend.at[k], frecv.at[k ^ 1], sibling).start()

    (fsend, frecv), (land,), _ = _comm_call("w_in_relay", [land], [recv], [(NDEV,), (NDEV,)], body, after=after)
    return fsend, frecv, land


def _w_in_forwarded(land, frecv, after):
    def body(a, s, new):
        (land,), (frecv,) = a, s
        x, y, c, me = _my_place()
        sibling, _ = _peer(x, y, c, 1)
        for k in (3, 5):
            _, slot = _peer(x, y, c, k)
            _remote(land.at[slot], land.at[slot], frecv.at[k], frecv.at[k], sibling).wait_recv()

    return _comm_call("w_in_forwarded", [land], [frecv], [], body, after=after)[1][0]


def _w_in_diag(land, land_o, frecv, after):
    def body(a, s, new):
        (land, land_o), (frecv,), (dsend, drecv, osend, orecv) = a, s, new
        x, y, c, me = _my_place()
        sibling, _ = _peer(x, y, c, 1)
        peer, slot = _peer(x, y, c, 6)
        _remote(land.at[slot], land.at[slot], dsend.at[6], frecv.at[6], peer).wait_recv()
        _remote(land.at[slot], land.at[slot], dsend.at[6], drecv.at[7], sibling).start()
        for k in (1,) + SAME_CORE:
            peer, _ = _peer(x, y, c, k)
            _remote(land_o.at[me], land_o.at[me], osend.at[k], orecv.at[k], peer).start()

    sems, (land, land_o), _ = _comm_call("w_in_diag", [land, land_o], [frecv], [(NDEV,)] * 4, body, after=after)
    return sems, land, land_o


def _w_in_finish(land, send, fsend, dsend, drecv, after):
    def body(a, s, new):
        (land,), (send, fsend, dsend, drecv) = a, s
        x, y, c, me = _my_place()
        sibling, _ = _peer(x, y, c, 1)
        _, slot = _peer(x, y, c, 7)
        _remote(land.at[slot], land.at[slot], dsend.at[6], drecv.at[7], sibling).wait_recv()
        for k in (1, 2, 4):
            peer, _ = _peer(x, y, c, k)
            _remote(land.at[me], land.at[me], send.at[k], send.at[k], peer).wait_send()
        for k in (2, 4, 6):
            _, slot = _peer(x, y, c, k)
            _remote(land.at[slot], land.at[slot], fsend.at[k], fsend.at[k], sibling).wait_send()
        _, slot = _peer(x, y, c, 6)
        _remote(land.at[slot], land.at[slot], dsend.at[6], dsend.at[6], sibling).wait_send()

    return _comm_call("w_in_finish", [land], [send, fsend, dsend, drecv], [], body, after=after)[1][0]


def _in_proj_part(name, h, land, proj, me_arr, k0, kstep, nk, tm=512):
    S, D = h.shape
    C = land.shape[2]
    tm = min(tm, S)

    def body(me_ref, a_ref, b_ref, *rest):
        rest[-1][...] = jnp.dot(a_ref[...], b_ref[...], preferred_element_type=f32)

    slot = lambda j, me: me[0] ^ (k0 + kstep * j)
    args = [h, land] + ([] if proj is None else [proj])
    grid_spec = pltpu.PrefetchScalarGridSpec(
        num_scalar_prefetch=1, grid=(nk, S // tm),
        in_specs=[pl.BlockSpec((tm, D), lambda j, i, me: (i, 0)),
                  pl.BlockSpec((None, D, C), lambda j, i, me: (slot(j, me), 0, 0))] + [ANY_SPEC] * (len(args) - 2),
        out_specs=pl.BlockSpec((tm, C), lambda j, i, me: (i, slot(j, me))))
    return pl.pallas_call(
        body, name=name, out_shape=jax.ShapeDtypeStruct((S, NDEV * C), f32), grid_spec=grid_spec,
        input_output_aliases={} if proj is None else {3: 0}, compiler_params=_params("arbitrary", "arbitrary"),
    )(me_arr, *args)


NCHIP = NDEV // 2


def _pair_start(name, src):
    npair = src.shape[0] // 2

    def body(a, s, new):
        (src, pair), (send, recv) = a, new
        x, y, c, me = _my_place()
        sibling, _ = _peer(x, y, c, 1)
        for i in range(npair):
            _remote(src.at[2 * i + 1 - c], pair.at[i], send.at[i], recv.at[i], sibling).start()

    pair = lax.empty((npair,) + src.shape[1:], src.dtype)
    (send, recv), (src, pair), token = _comm_call(name, [src, pair], [], [(npair,), (npair,)], body, token=True)
    return send, recv, src, pair, token


def _pair_wait(name, src, pair, send, recv, after):
    npair = pair.shape[0]

    def body(a, s, new):
        (src, pair), (send, recv) = a, s
        x, y, c, me = _my_place()
        sibling, _ = _peer(x, y, c, 1)
        for i in range(npair):
            cp = _remote(src.at[2 * i + 1 - c], pair.at[i], send.at[i], recv.at[i], sibling)
            cp.wait_recv()
            cp.wait_send()

    return _comm_call(name, [src, pair], [send, recv], [], body, after=after)[1]


def _pair_sum(name, src, pair, core, tr=256):
    npair, R, Cc = pair.shape
    tr = min(tr, R)

    def body(core_ref, a_ref, b_ref, o_ref):
        o_ref[...] = (a_ref[...].astype(f32) + b_ref[...].astype(f32)).astype(o_ref.dtype)

    grid_spec = pltpu.PrefetchScalarGridSpec(
        num_scalar_prefetch=1, grid=(npair, R // tr),
        in_specs=[pl.BlockSpec((None, tr, Cc), lambda i, r, core: (2 * i + core[0], r, 0)),
                  pl.BlockSpec((None, tr, Cc), lambda i, r, core: (i, r, 0))],
        out_specs=pl.BlockSpec((None, tr, Cc), lambda i, r, core: (i, r, 0)))
    return pl.pallas_call(body, name=name, out_shape=jax.ShapeDtypeStruct(pair.shape, pair.dtype),
                          grid_spec=grid_spec, compiler_params=_params("parallel", "parallel"))(core, src, pair)


def _owner_chip(first, i):
    q = first // 2 + i
    return q >> 1 & 1, q & 1


def _chip_start(name, sums, land, first, rows=None, after=()):
    npair = sums.shape[0]
    rows = pl.ds(*(rows or (0, sums.shape[1])))

    def body(a, s, new):
        (sums, land), (send, recv) = a, new
        x, y, c, me = _my_place()
        for i in range(npair):
            ox, oy = _owner_chip(first, i)

            @pl.when((x != ox) | (y != oy))
            def _():
                _remote(sums.at[i, rows], land.at[2 * x + y, rows], send.at[i], recv.at[2 * x + y], (ox, oy, c)).start()

    (send, recv), (sums, land), token = _comm_call(name, [sums, land], [], [(npair,), (NCHIP,)], body, after=after,
                                                   token=True)
    return send, recv, sums, land, token


def _chip_wait(name, sums, land, send, recv, first, after, rows=None):
    npair = sums.shape[0]
    rows = pl.ds(*(rows or (0, sums.shape[1])))

    def body(a, s, new):
        (sums, land), (send, recv) = a, s
        x, y, c, me = _my_place()
        mine = (me >= first) & (me < first + 2 * npair)
        for i in range(npair):
            ox, oy = _owner_chip(first, i)

            @pl.when((x != ox) | (y != oy))
            def _():
                _remote(sums.at[i, rows], land.at[2 * x + y, rows], send.at[i], recv.at[2 * x + y], (ox, oy, c)).wait_send()
        for q in range(NCHIP):
            @pl.when(mine & (2 * x + y != q))
            def _():
                _remote(sums.at[0, rows], land.at[q, rows], send.at[0], recv.at[q], (q >> 1, q & 1, c)).wait_recv()

    return _comm_call(name, [sums, land], [send, recv], [], body, after=after)[1]


def _matmul(a, b, *, name, out_dtype, ta=False, tb=False, b_slots=False, out_slots=0, b_cols=None,
            tm=1024, tn=1024, tk=2048, dep=None):
    M, K = (a.shape[1], a.shape[0]) if ta else a.shape
    col0 = 0
    if b_slots:
        slab = b.shape[2]
        N = b.shape[1] if tb else b.shape[0] * slab
        assert (K if tb else N) == b.shape[0] * slab
    elif b_cols is not None:
        assert not tb
        col0, N = b_cols
    else:
        N = b.shape[0] if tb else b.shape[1]
    tm, tn, tk = min(tm, M), min(tn, N), min(tk, K)
    if b_slots:
        if tb:
            tk = min(tk, slab)
        else:
            tn = min(tn, slab)
    if out_slots:
        tn = min(tn, N // out_slots)
    nm, nn, nk = M // tm, N // tn, K // tk
    assert (nm * tm, nn * tn, nk * tk) == (M, N, K) and col0 % tn == 0, (name, M, N, K, tm, tn, tk)
    j0 = col0 // tn

    a_spec = pl.BlockSpec((tk, tm), lambda i, j, k: (k, i)) if ta else pl.BlockSpec((tm, tk), lambda i, j, k: (i, k))
    if b_slots and tb:
        per = slab // tk
        b_spec = pl.BlockSpec((None, tn, tk), lambda i, j, k: (k // per, j, k % per))
    elif b_slots:
        per = slab // tn
        b_spec = pl.BlockSpec((None, tk, tn), lambda i, j, k: (j // per, k, j % per))
    elif tb:
        b_spec = pl.BlockSpec((tn, tk), lambda i, j, k: (j, k))
    else:
        b_spec = pl.BlockSpec((tk, tn), lambda i, j, k: (k, j + j0))
    if out_slots:
        per_o = (N // out_slots) // tn
        o_spec = pl.BlockSpec((None, tm, tn), lambda i, j, k: (j // per_o, i, j % per_o))
        out_shape = jax.ShapeDtypeStruct((out_slots, M, N // out_slots), out_dtype)
    else:
        o_spec = pl.BlockSpec((tm, tn), lambda i, j, k: (i, j))
        out_shape = jax.ShapeDtypeStruct((M, N), out_dtype)
    dims = (((0 if ta else 1,), (1 if tb else 0,)), ((), ()))
    deps = [] if dep is None else [dep]

    def body(a_ref, b_ref, *rest):
        o_ref = rest[len(deps)]
        prod = lax.dot_general(a_ref[...], b_ref[...], dims, preferred_element_type=f32)
        if nk == 1:
            o_ref[...] = prod.astype(out_dtype)
            return
        acc_ref = rest[len(deps) + 1]
        k = pl.program_id(2)

        @pl.when(k == 0)
        def _():
            acc_ref[...] = prod

        @pl.when((k > 0) & (k < nk - 1))
        def _():
            acc_ref[...] += prod

        @pl.when(k == nk - 1)
        def _():
            o_ref[...] = (acc_ref[...] + prod).astype(out_dtype)

    return pl.pallas_call(
        body, name=name, out_shape=out_shape, grid=(nm, nn, nk),
        in_specs=[a_spec, b_spec] + [ANY_SPEC] * len(deps), out_specs=o_spec,
        scratch_shapes=[pltpu.VMEM((tm, tn), f32)] if nk > 1 else [],
        compiler_params=_params("parallel", "parallel", "arbitrary"),
    )(a, b, *deps)


def _matmul_slabs_t(a_cols, a_slots, b, *, name, tm=512, tn=512, dep=None):
    M = a_cols.shape[0]
    n_slab, N, slab = b.shape
    n1, n2 = a_cols.shape[1] // slab, a_slots.shape[0]
    assert n1 + n2 == n_slab and a_slots.shape[1:] == (M, slab)
    tm, tn = min(tm, M), min(tn, N)
    deps = [] if dep is None else [dep]

    def body(a1_ref, a2_ref, b_ref, *rest):
        o_ref = rest[len(deps)]
        acc = None
        for s in range(n_slab):
            lhs = a1_ref[:, s * slab:(s + 1) * slab] if s < n1 else a2_ref[s - n1]
            prod = lax.dot_general(lhs, b_ref[s], (((1,), (1,)), ((), ())), preferred_element_type=f32)
            acc = prod if acc is None else acc + prod
        o_ref[...] = acc

    return pl.pallas_call(
        body, name=name, out_shape=jax.ShapeDtypeStruct((M, N), f32), grid=(M // tm, N // tn),
        in_specs=[pl.BlockSpec((tm, n1 * slab), lambda i, j: (i, 0)), pl.BlockSpec((n2, tm, slab), lambda i, j: (0, i, 0)),
                  pl.BlockSpec((n_slab, tn, slab), lambda i, j: (0, j, 0))] + [ANY_SPEC] * len(deps),
        out_specs=pl.BlockSpec((tm, tn), lambda i, j: (i, j)), compiler_params=_params("parallel", "parallel"),
    )(a_cols, a_slots, b, *deps)


def _silu_block(c):
    def body(c_ref, o_ref):
        o_ref[...] = _silu(c_ref[...])

    return pl.pallas_call(body, name="silu_c", out_shape=jax.ShapeDtypeStruct(c.shape, f32))(c)


def _ada_cols(c_all, w_ada, b_cols):
    D, W = w_ada.shape

    def body(c_ref, w_ref, b_ref, o_ref):
        mod = lax.dot_general(c_ref[...], w_ref[...], (((1,), (0,)), ((), ())), preferred_element_type=f32,
                              precision=lax.Precision.HIGHEST) + b_ref[...]
        for j in range(NDEV):
            o_ref[j] = jnp.broadcast_to(mod[j:j + 1, :], (8, W))

    return pl.pallas_call(body, name="ada_cols", out_shape=jax.ShapeDtypeStruct((NDEV, 8, W), f32),
                          compiler_params=_params())(c_all, w_ada, b_cols)


def _prenorm(x, scale, shift, g_pre, dep, tr=256):
    S, D = x.shape
    tr = min(tr, S)

    def body(x_ref, sc_ref, sh_ref, g_ref, dep_ref, h_ref):
        xv = x_ref[...]
        r = lax.rsqrt(jnp.mean(xv * xv, axis=-1, keepdims=True) + EPS)
        h_ref[...] = ((xv * r) * g_ref[...] * (1.0 + sc_ref[...]) + sh_ref[...]).astype(bf16)

    row = pl.BlockSpec((tr, D), lambda i: (i, 0))
    vec = pl.BlockSpec((1, D), lambda i: (0, 0))
    return pl.pallas_call(body, name="prenorm", out_shape=jax.ShapeDtypeStruct((S, D), bf16), grid=(S // tr,),
                          in_specs=[row, vec, vec, vec, ANY_SPEC], out_specs=row, compiler_params=_params("parallel"))(
                              x, scale, shift, g_pre, dep)


def _ext_rows(i, tr, S):
    g = lax.broadcasted_iota(jnp.int32, (tr + 16, 1), 0) + (i * tr - 8)
    return (g >= 0) & (g < S)


def _halo_specs(tr, S, C, col):
    nb8 = S // 8
    main = pl.BlockSpec((tr, C), lambda i: (i, col))
    prev = pl.BlockSpec((8, C), lambda i: (jnp.maximum(i * (tr // 8) - 1, 0), col))
    nxt = pl.BlockSpec((8, C), lambda i: (jnp.minimum((i + 1) * (tr // 8), nb8 - 1), col))
    return prev, main, nxt


def _conv_fwd(proj, conv_w, conv_b, g_conv, tr=256):
    S, C = proj.shape[0], proj.shape[1] // 8
    tr = min(tr, S)

    def body(up, um, un, cp, cm, cn, bg_ref, zc_ref, w_ref, cb_ref, g_ref, o_ref):
        i = pl.program_id(0)
        exists = _ext_rows(i, tr, S)
        u = jnp.concatenate([up[...], um[...], un[...]], axis=0)
        cg = jnp.concatenate([cp[...], cm[...], cn[...]], axis=0)
        t = jnp.where(exists, cg * u, 0.0)
        t_before = pltpu.roll(t, 1, 0)[8:tr + 8]
        t_after = pltpu.roll(t, tr + 15, 0)[8:tr + 8]
        w = w_ref[...]
        cv = w[0:1] * t_before + w[1:2] * t[8:tr + 8] + w[2:3] * t_after + cb_ref[...]
        yc = bg_ref[...] * cv
        rc = lax.rsqrt(jnp.mean(yc * yc, axis=-1, keepdims=True) + EPS)
        o_ref[...] = ((yc * rc) * g_ref[...] * _silu(zc_ref[...])).astype(bf16)

    u_specs = _halo_specs(tr, S, C, 0)
    c_specs = _halo_specs(tr, S, C, 2)
    vec = pl.BlockSpec((1, C), lambda i: (0, 0))
    return pl.pallas_call(
        body, name="conv_fwd", out_shape=jax.ShapeDtypeStruct((S, 2 * C), bf16), grid=(S // tr,),
        in_specs=[*u_specs, *c_specs, pl.BlockSpec((tr, C), lambda i: (i, 1)), pl.BlockSpec((tr, C), lambda i: (i, 3)),
                  pl.BlockSpec((8, C), lambda i: (0, 0)), vec, vec],
        out_specs=pl.BlockSpec((tr, C), lambda i: (i, 0)), compiler_params=_params("parallel"),
    )(proj, proj, proj, proj, proj, proj, proj, proj, conv_w, conv_b, g_conv)


def _branch_geometry(S, r, inter):
    L = S // r * inter
    nq = min(128, L)
    nk = min(nq + 2 * HALF_WIN * inter, L)
    assert L % nq == 0 and (L == nk or L >= nq + 2 * HALF_WIN * inter)
    return L, nq, nk, L // nq


QUAD = 4


def _to_quad(dst, src, S):
    n = S // QUAD
    for rho in range(QUAD):
        dst[pl.ds(rho * n, n), :] = src[pl.ds(rho, n, stride=QUAD), :]


def _block_rows(idx, r, inter, S, L, nq, nk, nblk):
    rho, qb = (0, idx) if r == 1 else (idx // nblk, idx % nblk)
    i0 = qb * nq
    ws = jnp.clip(i0 - HALF_WIN * inter, 0, L - nk)
    if r == 1:
        return pl.ds(pl.multiple_of(i0, 8), nq), pl.ds(pl.multiple_of(ws, 8), nk), i0 - ws
    assert r % (QUAD * inter) == 0
    step = r // QUAD // inter
    base = (rho % QUAD) * (S // QUAD) + rho // QUAD
    if step == 1:
        return pl.ds(pl.multiple_of(base + i0, 8), nq), pl.ds(pl.multiple_of(base + ws, 8), nk), i0 - ws
    return pl.ds(base + step * i0, nq, stride=step), pl.ds(base + step * ws, nk, stride=step), i0 - ws


N_CASES = 3
SCALE = HEAD_DIM ** -0.5
ATTN_UNROLL = 8


def _bias_shape(S):
    shapes = [_branch_geometry(S, r, inter)[1:3] for _, r, inter in BRANCHES]
    return (len(BRANCHES) * N_CASES * 2, max(nq for nq, _ in shapes), max(nk for _, nk in shapes))


def _bias_index(b, case, head):
    return (b * N_CASES + case) * 2 + head


def _fill_bias(bias_scr, sl_ref, S):
    sl = sl_ref[...]
    slope = (sl[0:1, 0:1], sl[0:1, HEAD_DIM:HEAD_DIM + 1])
    for b, (_, r, inter) in enumerate(BRANCHES):
        L, nq, nk, nblk = _branch_geometry(S, r, inter)
        rel = lax.broadcasted_iota(jnp.int32, (nq, nk), 0) - lax.broadcasted_iota(jnp.int32, (nq, nk), 1)
        for case in range(N_CASES):
            d = jnp.abs(rel + case * HALF_WIN)
            valid = d <= HALF_WIN * inter
            if inter > 1:
                valid = valid & (jnp.bitwise_and(d, inter - 1) == 0)
            dist = d.astype(f32) * float(r // inter)
            for head in range(2):
                bias_scr[_bias_index(b, case, head), 0:nq, 0:nk] = jnp.where(valid, -slope[head] * dist, NEG_INF)


def _head_slopes(n_heads):
    slopes = 2.0 ** (-8.0 * jnp.arange(1, n_heads + 1, dtype=f32) / n_heads)
    return jnp.broadcast_to(jnp.repeat(slopes.reshape(n_heads // 2, 2), HEAD_DIM, axis=1)[:, None, :],
                            (n_heads // 2, 8, PAIR))


def _attn_fwd(proj, slopes):
    S, C = proj.shape[0], proj.shape[1] // 8
    npair = C // PAIR

    def body(q_ref, k_ref, v_ref, sl_ref, o_ref, lse_ref, m_scr, l_scr, a_scr, bias_scr, q4_scr, k4_scr, v4_scr):
        lane = lax.broadcasted_iota(jnp.int32, (1, PAIR), 1)
        first = lane < HEAD_DIM
        _fill_bias(bias_scr, sl_ref, S)
        for dst, src in ((q4_scr, q_ref), (k4_scr, k_ref), (v4_scr, v_ref)):
            _to_quad(dst, src, S)

        for b, (_, r, inter) in enumerate(BRANCHES):
            L, nq, nk, nblk = _branch_geometry(S, r, inter)
            qs, ks, vs = (q_ref, k_ref, v_ref) if r == 1 else (q4_scr, k4_scr, v4_scr)

            def step(idx, carry, b=b, r=r, L=L, nq=nq, nk=nk, nblk=nblk, qs=qs, ks=ks, vs=vs):
                qrows, krows, off = _block_rows(idx, r, inter, S, L, nq, nk, nblk)
                case = off // HALF_WIN
                q2 = qs[qrows, :] * SCALE
                k2 = ks[krows, :].astype(bf16)
                v2 = vs[krows, :].astype(bf16)
                ms, accs = [], []
                for hh in range(2):
                    mine = first if hh == 0 else ~first
                    qh = jnp.where(mine, q2, 0.0).astype(bf16)
                    s = lax.dot_general(qh, k2, (((1,), (1,)), ((), ())), preferred_element_type=f32)
                    s = s + bias_scr[_bias_index(b, case, hh), 0:nq, 0:nk]
                    m = jnp.max(s, axis=-1, keepdims=True)
                    p = jnp.exp(s - m).astype(bf16)
                    vh = jnp.where(mine, v2, jnp.ones_like(v2))
                    ms.append(m)
                    accs.append(jnp.dot(p, vh, preferred_element_type=f32))
                m_scr[b, qrows, :] = jnp.where(first, ms[0], ms[1])
                a_scr[b, qrows, :] = jnp.where(first, accs[0], accs[1])
                l_scr[b, qrows, :] = jnp.where(first, accs[1], accs[0])
                return carry

            lax.fori_loop(0, S // nq, step, 0, unroll=min(ATTN_UNROLL, S // nq))

        n4 = S // QUAD
        ch = min(256, n4)
        nch = n4 // ch

        def merge(i, carry):
            rho, part = i // nch, i % nch
            sorted_rows = pl.ds(pl.multiple_of(rho * n4 + part * ch, 8), ch)
            token_rows = pl.ds(rho + QUAD * part * ch, ch, stride=QUAD)
            rows = (token_rows,) + (sorted_rows,) * (len(BRANCHES) - 1)
            ms = [m_scr[b, rows[b], :] for b in range(len(BRANCHES))]
            m = functools.reduce(jnp.maximum, ms)
            l = jnp.zeros((ch, PAIR), f32)
            acc = jnp.zeros((ch, PAIR), f32)
            for b in range(len(BRANCHES)):
                w = jnp.exp(ms[b] - m)
                l = l + w * pltpu.roll(l_scr[b, rows[b], :], HEAD_DIM, 1)
                acc = acc + w * a_scr[b, rows[b], :]
            o_ref[token_rows, :] = acc / l
            lse_ref[token_rows, :] = m + jnp.log(l)
            return carry

        lax.fori_loop(0, QUAD * nch, merge, 0)

    blk = lambda part: pl.BlockSpec((S, PAIR), lambda p: (0, part * npair + p))
    out = pl.BlockSpec((S, PAIR), lambda p: (0, p))
    return pl.pallas_call(
        body, name="attn_fwd",
        out_shape=(jax.ShapeDtypeStruct((S, C), f32), jax.ShapeDtypeStruct((S, C), f32)), grid=(npair,),
        in_specs=[blk(4), blk(5), blk(6), pl.BlockSpec((None, 8, PAIR), lambda p: (p, 0, 0))],
        out_specs=(out, out),
        scratch_shapes=[pltpu.VMEM((3, S, PAIR), f32)] * 3 + [pltpu.VMEM(_bias_shape(S), f32)]
        + [pltpu.VMEM((S, PAIR), f32)] * 3,
        compiler_params=_params("parallel"),
    )(proj, proj, proj, slopes)


def _attn_post(ycat, o, proj, g_attn, tr=256):
    S, C = o.shape
    tr = min(tr, S)

    def body(y_ref, o_ref, z_ref, g_ref, out_ref):
        del y_ref
        ov = o_ref[...]
        ra = lax.rsqrt(jnp.mean(ov * ov, axis=-1, keepdims=True) + EPS)
        out_ref[...] = ((ov * ra) * g_ref[...] * _silu(z_ref[...])).astype(bf16)

    return pl.pallas_call(
        body, name="attn_post", out_shape=jax.ShapeDtypeStruct(ycat.shape, ycat.dtype), grid=(S // tr,),
        in_specs=[HBM_SPEC, pl.BlockSpec((tr, C), lambda i: (i, 0)), pl.BlockSpec((tr, C), lambda i: (i, 7)),
                  pl.BlockSpec((1, C), lambda i: (0, 0))],
        out_specs=pl.BlockSpec((tr, C), lambda i: (i, 1)), input_output_aliases={0: 0},
        compiler_params=_params("arbitrary"),
    )(ycat, o, proj, g_attn)


def _sandwich(y, x, target, gate, g_post, tr=256):
    S, D = y.shape
    tr = min(tr, S)

    def body(y_ref, x_ref, t_ref, gate_ref, g_ref, dy_ref, dout_ref, sums_ref):
        i = pl.program_id(0)
        yv = y_ref[...]
        rp = lax.rsqrt(jnp.mean(yv * yv, axis=-1, keepdims=True) + EPS)
        yhat = yv * rp
        yn = yhat * g_ref[...]
        err = (x_ref[...] + gate_ref[...] * yn) - t_ref[...]
        dout = err * (1.0 / D)
        dout_ref[...] = dout
        dyn = dout * gate_ref[...]
        w = dyn * g_ref[...]
        dy_ref[...] = (rp * (w - yhat * jnp.mean(w * yhat, axis=-1, keepdims=True))).astype(bf16)
        loss = 0.5 * jnp.sum(jnp.mean(err * err, axis=-1, keepdims=True), axis=0, keepdims=True)
        row = lax.broadcasted_iota(jnp.int32, (8, D), 0)
        upd = jnp.where(row == 0, jnp.sum(dout * yn, axis=0, keepdims=True),
                        jnp.where(row == 1, jnp.sum(dyn * yhat, axis=0, keepdims=True),
                                  jnp.where(row == 2, loss, 0.0)))

        @pl.when(i == 0)
        def _():
            sums_ref[...] = upd

        @pl.when(i > 0)
        def _():
            sums_ref[...] += upd

    row = pl.BlockSpec((tr, D), lambda i: (i, 0))
    vec = pl.BlockSpec((1, D), lambda i: (0, 0))
    return pl.pallas_call(
        body, name="sandwich",
        out_shape=(jax.ShapeDtypeStruct((S, D), bf16), jax.ShapeDtypeStruct((S, D), f32), jax.ShapeDtypeStruct((8, D), f32)),
        grid=(S // tr,), in_specs=[row, row, row, vec, vec],
        out_specs=(row, row, pl.BlockSpec((8, D), lambda i: (0, 0))), compiler_params=_params("arbitrary"),
    )(y, x, target, gate, g_post)


def _conv_bwd(proj, dycat, conv_w, conv_b, g_conv, dep, tr=256):
    S, C = proj.shape[0], proj.shape[1] // 8
    tr = min(tr, S)
    n = tr + 16

    def body(*refs):
        ins, (w_ref, cb_ref, g_ref, _, dp_ref, sums_ref) = refs[:15], refs[15:]
        i = pl.program_id(0)
        exists = _ext_rows(i, tr, S)
        u, bg, cg, zc, dyn = (jnp.concatenate([ins[3 * t][...], ins[3 * t + 1][...], ins[3 * t + 2][...]], axis=0)
                              for t in range(5))
        w = w_ref[...]
        t = jnp.where(exists, cg * u, 0.0)
        t_before, t_after = pltpu.roll(t, 1, 0), pltpu.roll(t, n - 1, 0)
        cv = w[0:1] * t_before + w[1:2] * t + w[2:3] * t_after + cb_ref[...]
        yc = bg * cv
        rc = lax.rsqrt(jnp.mean(yc * yc, axis=-1, keepdims=True) + EPS)
        yhat = yc * rc
        sz = _silu(zc)
        wgt = dyn * g_ref[...] * sz
        dyc = rc * (wgt - yhat * jnp.mean(wgt * yhat, axis=-1, keepdims=True))
        dcv = jnp.where(exists, dyc * bg, 0.0)
        dt = w[0:1] * pltpu.roll(dcv, n - 1, 0) + w[1:2] * dcv + w[2:3] * pltpu.roll(dcv, 1, 0)
        mid = slice(8, tr + 8)
        dp_ref[:, 0:C] = (dt * cg)[mid].astype(bf16)
        dp_ref[:, C:2 * C] = (dyc * cv)[mid].astype(bf16)
        dp_ref[:, 2 * C:3 * C] = (dt * u)[mid].astype(bf16)
        dp_ref[:, 3 * C:4 * C] = (dyn * yhat * g_ref[...] * _dsilu(zc))[mid].astype(bf16)
        colsum = lambda v: jnp.sum(v[mid], axis=0, keepdims=True)
        parts = [colsum(dyn * yhat * sz), colsum(dcv), colsum(dcv * t_before), colsum(dcv * t), colsum(dcv * t_after)]
        row = lax.broadcasted_iota(jnp.int32, (8, C), 0)
        upd = jnp.zeros((8, C), f32)
        for j, pj in enumerate(parts):
            upd = jnp.where(row == j, pj, upd)

        @pl.when(i == 0)
        def _():
            sums_ref[...] = upd

        @pl.when(i > 0)
        def _():
            sums_ref[...] += upd

    specs = []
    for col in range(4):
        specs += _halo_specs(tr, S, C, col)
    specs += _halo_specs(tr, S, C, 0)
    vec = pl.BlockSpec((1, C), lambda i: (0, 0))
    return pl.pallas_call(
        body, name="conv_bwd",
        out_shape=(jax.ShapeDtypeStruct((S, 4 * C), bf16), jax.ShapeDtypeStruct((8, C), f32)), grid=(S // tr,),
        in_specs=[*specs, pl.BlockSpec((8, C), lambda i: (0, 0)), vec, vec, ANY_SPEC],
        out_specs=(pl.BlockSpec((tr, 4 * C), lambda i: (i, 0)), pl.BlockSpec((8, C), lambda i: (0, 0))),
        compiler_params=_params("arbitrary"),
    )(*([proj] * 12), dycat, dycat, dycat, conv_w, conv_b, g_conv, dep)


def _attn_post_bwd(o, proj, dycat, g_attn, dep, tr=256):
    S, C = o.shape
    tr = min(tr, S)

    def body(o_ref, z_ref, dy_ref, g_ref, dep_ref, do_ref, dz_ref, sums_ref):
        i = pl.program_id(0)
        ov, zv, dyn = o_ref[...], z_ref[...], dy_ref[...]
        ra = lax.rsqrt(jnp.mean(ov * ov, axis=-1, keepdims=True) + EPS)
        ohat = ov * ra
        sz = _silu(zv)
        wgt = dyn * g_ref[...] * sz
        do_ref[...] = ra * (wgt - ohat * jnp.mean(wgt * ohat, axis=-1, keepdims=True))
        dz_ref[...] = (dyn * ohat * g_ref[...] * _dsilu(zv)).astype(bf16)
        row = lax.broadcasted_iota(jnp.int32, (8, C), 0)
        upd = jnp.where(row == 0, jnp.sum(dyn * ohat * sz, axis=0, keepdims=True), 0.0)

        @pl.when(i == 0)
        def _():
            sums_ref[...] = upd

        @pl.when(i > 0)
        def _():
            sums_ref[...] += upd

    return pl.pallas_call(
        body, name="attn_post_bwd",
        out_shape=(jax.ShapeDtypeStruct((S, C), f32), jax.ShapeDtypeStruct((4, S, C), bf16),
                   jax.ShapeDtypeStruct((8, C), f32)),
        grid=(S // tr,),
        in_specs=[pl.BlockSpec((tr, C), lambda i: (i, 0)), pl.BlockSpec((tr, C), lambda i: (i, 7)),
                  pl.BlockSpec((tr, C), lambda i: (i, 1)), pl.BlockSpec((1, C), lambda i: (0, 0)), ANY_SPEC],
        out_specs=(pl.BlockSpec((tr, C), lambda i: (i, 0)), pl.BlockSpec((None, tr, C), lambda i: (3, i, 0)),
                   pl.BlockSpec((8, C), lambda i: (0, 0))),
        compiler_params=_params("arbitrary"),
    )(o, proj, dycat, g_attn, dep)


def _attn_bwd(proj, o, do, lse, slopes, dqkvz, dep):
    S, C = o.shape
    npair = C // PAIR

    def body(q_ref, k_ref, v_ref, o_ref, do_ref, lse_ref, sl_ref, old_ref, dep_ref, dqkv_ref,
             acc_scr, dl_scr, quad_scr, bias_scr):
        lane = lax.broadcasted_iota(jnp.int32, (1, PAIR), 1)
        first = lane < HEAD_DIM
        _fill_bias(bias_scr, sl_ref, S)
        ch = min(256, S)

        def prep(i, carry):
            rows = pl.ds(pl.multiple_of(i * ch, 8), ch)
            prod = do_ref[rows, :] * o_ref[rows, :]
            d0 = jnp.sum(jnp.where(first, prod, 0.0), axis=-1, keepdims=True)
            d1 = jnp.sum(jnp.where(first, 0.0, prod), axis=-1, keepdims=True)
            dl_scr[rows, :] = jnp.where(first, d0, d1)
            zero = jnp.zeros((ch, PAIR), f32)
            for order in range(2):
                for t in range(3):
                    acc_scr[order, t, rows, :] = zero
            return carry

        lax.fori_loop(0, S // ch, prep, 0)
        token_srcs = (q_ref, k_ref, v_ref, do_ref, lse_ref, dl_scr)
        for j, src in enumerate(token_srcs):
            _to_quad(quad_scr.at[j], src, S)

        for b, (_, r, inter) in enumerate(BRANCHES):
            L, nq, nk, nblk = _branch_geometry(S, r, inter)
            order = 0 if r == 1 else 1
            srcs = token_srcs if r == 1 else tuple(quad_scr.at[j] for j in range(6))

            def step(idx, carry, b=b, r=r, L=L, nq=nq, nk=nk, nblk=nblk, order=order, srcs=srcs):
                qs, ks, vs, dos, lses, dls = srcs
                dq_scr, dk_scr, dv_scr = (acc_scr.at[order, t] for t in range(3))
                qrows, krows, off = _block_rows(idx, r, inter, S, L, nq, nk, nblk)
                case = off // HALF_WIN
                q2 = qs[qrows, :] * SCALE
                k2 = ks[krows, :].astype(bf16)
                v2 = vs[krows, :].astype(bf16)
                do2 = dos[qrows, :]
                lse2 = lses[qrows, :]
                dl2 = dls[qrows, :]
                dq2 = jnp.zeros((nq, PAIR), f32)
                dk2 = jnp.zeros((nk, PAIR), f32)
                dv2 = jnp.zeros((nk, PAIR), f32)
                for hh in range(2):
                    mine = first if hh == 0 else ~first
                    lo = hh * HEAD_DIM
                    qh = jnp.where(mine, q2, 0.0).astype(bf16)
                    doh = jnp.where(mine, do2, 0.0).astype(bf16)
                    s = lax.dot_general(qh, k2, (((1,), (1,)), ((), ())), preferred_element_type=f32)
                    s = s + bias_scr[_bias_index(b, case, hh), 0:nq, 0:nk]
                    p = jnp.exp(s - lse2[:, lo:lo + 1])
                    dv2 = dv2 + lax.dot_general(p.astype(bf16), doh, (((0,), (0,)), ((), ())), preferred_element_type=f32)
                    dp = lax.dot_general(doh, v2, (((1,), (1,)), ((), ())), preferred_element_type=f32)
                    ds = (p * (dp - dl2[:, lo:lo + 1])).astype(bf16)
                    dq2 = dq2 + jnp.where(mine, jnp.dot(ds, k2, preferred_element_type=f32), 0.0)
                    dk2 = dk2 + lax.dot_general(ds, qh, (((0,), (0,)), ((), ())), preferred_element_type=f32)
                dq_scr[qrows, :] = dq_scr[qrows, :] + dq2
                dk_scr[krows, :] = dk_scr[krows, :] + dk2
                dv_scr[krows, :] = dv_scr[krows, :] + dv2
                return carry

            lax.fori_loop(0, S // nq, step, 0, unroll=min(ATTN_UNROLL, S // nq))

        n4 = S // QUAD
        for t in range(3):
            for rho in range(QUAD):
                token_rows = pl.ds(rho, n4, stride=QUAD)
                acc_scr[0, t, token_rows, :] = acc_scr[0, t, token_rows, :] + acc_scr[1, t, pl.ds(rho * n4, n4), :]
        dqkv_ref[0] = (acc_scr[0, 0] * SCALE).astype(bf16)
        dqkv_ref[1] = acc_scr[0, 1].astype(bf16)
        dqkv_ref[2] = acc_scr[0, 2].astype(bf16)

    blk = lambda part: pl.BlockSpec((S, PAIR), lambda p: (0, part * npair + p))
    own = pl.BlockSpec((S, PAIR), lambda p: (0, p))
    return pl.pallas_call(
        body, name="attn_bwd", out_shape=jax.ShapeDtypeStruct(dqkvz.shape, dqkvz.dtype), grid=(npair,),
        in_specs=[blk(4), blk(5), blk(6), own, own, own, pl.BlockSpec((None, 8, PAIR), lambda p: (p, 0, 0)),
                  ANY_SPEC, ANY_SPEC],
        out_specs=pl.BlockSpec((3, S, PAIR), lambda p: (0, 0, p)), input_output_aliases={7: 0},
        scratch_shapes=[pltpu.VMEM((2, 3, S, PAIR), f32), pltpu.VMEM((S, PAIR), f32), pltpu.VMEM((6, S, PAIR), f32),
                        pltpu.VMEM(_bias_shape(S), f32)],
        compiler_params=_params("arbitrary"),
    )(proj, proj, proj, o, do, lse, slopes, dqkvz, dep)


def _prenorm_bwd(dh, x, dout, scale, g_pre, tr=256):
    S, D = x.shape
    tr = min(tr, S)

    def body(dh_ref, x_ref, dout_ref, sc_ref, g_ref, gx_ref, sums_ref):
        i = pl.program_id(0)
        xv, dhv = x_ref[...], dh_ref[...]
        r = lax.rsqrt(jnp.mean(xv * xv, axis=-1, keepdims=True) + EPS)
        xn = xv * r
        dxn = dhv * (g_ref[...] * (1.0 + sc_ref[...]))
        gx_ref[...] = dout_ref[...] + r * (dxn - xn * jnp.mean(dxn * xn, axis=-1, keepdims=True))
        dhx = dhv * xn
        row = lax.broadcasted_iota(jnp.int32, (8, D), 0)
        upd = jnp.where(row == 0, jnp.sum(dhv, axis=0, keepdims=True),
                        jnp.where(row == 1, jnp.sum(dhx, axis=0, keepdims=True) * g_ref[...],
                                  jnp.where(row == 2, jnp.sum(dhx, axis=0, keepdims=True) * (1.0 + sc_ref[...]), 0.0)))

        @pl.when(i == 0)
        def _():
            sums_ref[...] = upd

        @pl.when(i > 0)
        def _():
            sums_ref[...] += upd

    row = pl.BlockSpec((tr, D), lambda i: (i, 0))
    vec = pl.BlockSpec((1, D), lambda i: (0, 0))
    return pl.pallas_call(
        body, name="prenorm_bwd",
        out_shape=(jax.ShapeDtypeStruct((S, D), f32), jax.ShapeDtypeStruct((8, D), f32)), grid=(S // tr,),
        in_specs=[row, row, row, vec, vec], out_specs=(row, pl.BlockSpec((8, D), lambda i: (0, 0))),
        compiler_params=_params("arbitrary"),
    )(dh, x, dout, scale, g_pre)


def _adamw(w, g, m, v):
    m = ADAM_B1 * m + (1.0 - ADAM_B1) * g
    v = ADAM_B2 * v + (1.0 - ADAM_B2) * (g * g)
    m_hat = m / (1.0 - ADAM_B1 ** ADAM_STEP)
    v_hat = v / (1.0 - ADAM_B2 ** ADAM_STEP)
    delta = -ADAM_LR * (m_hat / (jnp.sqrt(v_hat) + ADAM_EPS) + ADAM_WD * w)
    return delta, m, v


def _sum_rows(parts, dep):
    P = parts.shape[1]

    def body(p_ref, dep_ref, o_ref):
        acc = p_ref[0:1, :]
        for j in range(1, NDEV):
            acc = acc + p_ref[j:j + 1, :]
        o_ref[...] = jnp.broadcast_to(acc, (8, P))

    vmem = pl.BlockSpec(memory_space=pltpu.VMEM)
    return pl.pallas_call(body, name="sum_small", out_shape=jax.ShapeDtypeStruct((8, P), f32),
                          in_specs=[vmem, ANY_SPEC], out_specs=vmem, compiler_params=_params())(parts, dep)


def _adamw_small(w, g, m, v):
    def body(w_ref, g_ref, m_ref, v_ref, d_ref, nm_ref, nv_ref):
        d_ref[...], nm_ref[...], nv_ref[...] = _adamw(w_ref[...], g_ref[...], m_ref[...], v_ref[...])

    out = jax.ShapeDtypeStruct(w.shape, f32)
    return pl.pallas_call(body, name="adamw_small", out_shape=(out, out, out), compiler_params=_params())(w, g, m, v)


def _adamw_sharded(name, parts, sums_a, sums_b, pick, w, m, v, rows=None, prev=None, tr=128):
    R, Cc = w.shape
    r0, nr = rows or (0, R)
    tr = math.gcd(tr, r0, nr)
    n, b0 = parts.shape[0], r0 // tr

    def body(pick_ref, p_ref, a_ref, b_ref, w_ref, m_ref, v_ref, *rest):
        g_ref, d_ref, nm_ref, nv_ref = rest[-4:]
        g = jnp.where(pick_ref[0] == 1, b_ref[...], a_ref[...]).astype(f32)
        for j in range(n):
            g = g + p_ref[j].astype(f32)
        g_ref[...] = g
        d_ref[...], nm_ref[...], nv_ref[...] = _adamw(w_ref[...], g, m_ref[...], v_ref[...])

    row = pl.BlockSpec((tr, Cc), lambda i, pick: (i + b0, 0))
    mine = pl.BlockSpec((None, tr, Cc), lambda i, pick: (pick[1], i + b0, 0))
    out = jax.ShapeDtypeStruct((R, Cc), f32)
    prev = list(prev or [])
    grid_spec = pltpu.PrefetchScalarGridSpec(
        num_scalar_prefetch=1, grid=(nr // tr,),
        in_specs=[pl.BlockSpec((n, tr, Cc), lambda i, pick: (0, i + b0, 0)), mine, mine, row, row, row]
        + [ANY_SPEC] * len(prev),
        out_specs=(row, row, row, row))
    return pl.pallas_call(
        body, name=name, out_shape=(out, out, out, out), grid_spec=grid_spec,
        input_output_aliases={7 + t: t for t in range(len(prev))}, compiler_params=_params("arbitrary"),
    )(pick, parts, sums_a, sums_b, w, m, v, *prev)


def _adamw_ada(c_t, dmod_cols, w, m, v, dep, tr=256):
    D, W = w.shape
    tr = min(tr, D)

    def body(c_ref, dm_ref, w_ref, m_ref, v_ref, dep_ref, g_ref, d_ref, nm_ref, nv_ref):
        cv, dm = c_ref[...], dm_ref[...]
        g = cv[:, 0:1] * dm[0:1, :]
        for b in range(1, NDEV):
            g = g + cv[:, b:b + 1] * dm[b:b + 1, :]
        g_ref[...] = g
        d_ref[...], nm_ref[...], nv_ref[...] = _adamw(w_ref[...], g, m_ref[...], v_ref[...])

    row = pl.BlockSpec((tr, W), lambda i: (i, 0))
    out = jax.ShapeDtypeStruct((D, W), f32)
    return pl.pallas_call(
        body, name="adamw_ada", out_shape=(out, out, out, out), grid=(D // tr,),
        in_specs=[pl.BlockSpec((tr, NDEV), lambda i: (i, 0)), pl.BlockSpec((NDEV, W), lambda i: (0, 0)), row, row, row,
                  ANY_SPEC],
        out_specs=(row, row, row, row), compiler_params=_params("parallel"),
    )(c_t, dmod_cols, w, m, v, dep)


def kernel(x, c, w_ada, b_ada, g_pre, w_in, conv_w, conv_b, g_conv, g_attn, w_out, g_post, loss_target, m_w_ada, m_b_ada, m_g_pre, m_w_in, m_conv_w, m_conv_b, m_g_conv, m_g_attn, m_w_out, m_g_post, v_w_ada, v_b_ada, v_g_pre, v_w_in, v_conv_w, v_conv_b, v_g_conv, v_g_attn, v_w_out, v_g_post):
    S, D = x.shape[1], x.shape[2]
    C = D // 2
    W = w_ada.shape[2]
    CW = conv_w.shape[2]
    me = 4 * lax.axis_index("x") + 2 * lax.axis_index("y") + lax.axis_index("c")
    x2, tgt = x[0], loss_target[0]
    w_ada2, w_in2, w_out2 = w_ada[0], w_in[0], w_out[0]

    R = D // NDEV
    core = lax.axis_index("c").astype(jnp.int32).reshape(1)

    cw_slab = jnp.zeros((8, CW), f32).at[:3].set(conv_w[0])
    c_blocks, cw_g = _all_gather([_silu_block(c.reshape(D // 128, 128)), cw_slab], "gather_c")
    c_all = c_blocks.reshape(NDEV, D)
    conv_w_full = jnp.transpose(cw_g, (1, 0, 2)).reshape(8, C)
    b_cols = lax.dynamic_slice_in_dim(b_ada, me * W, W, axis=1)
    (mod_slabs,) = _all_to_all([_ada_cols(c_all, w_ada2, b_cols)], "scatter_mod")
    mod = mod_slabs[:, 0, :].reshape(1, 3 * D)
    shift, scale, gate = mod[:, :D], mod[:, D:2 * D], mod[:, 2 * D:]

    land_i = lax.dynamic_update_slice(lax.empty((NDEV, D, C), bf16), w_in2.astype(bf16)[None], (me, 0, 0))
    land_o = lax.dynamic_update_slice(lax.empty((NDEV, R, D), bf16), w_out2.astype(bf16)[None], (me, 0, 0))
    wi_send, wi_recv, land_i, w_token = _w_in_start(land_i, [mod_slabs])

    me_arr = me.astype(jnp.int32).reshape(1)
    h = _prenorm(x2, scale, shift, g_pre, w_token)
    land_i = _w_in_sibling(land_i, wi_recv, after=[h])
    proj = _in_proj_part("in_proj_a", h, land_i, None, me_arr, 0, 1, 2)
    fi_send, fi_recv, land_i = _w_in_relay(land_i, wi_recv, after=[proj])
    proj = _in_proj_part("in_proj_b", h, land_i, proj, me_arr, 2, 2, 2)
    land_i = _w_in_forwarded(land_i, fi_recv, after=[proj])
    proj = _in_proj_part("in_proj_c", h, land_i, proj, me_arr, 3, 2, 2)
    (di_send, di_recv, wo_send, wo_recv), land_i, land_o = _w_in_diag(land_i, land_o, fi_recv, after=[proj])
    proj = _in_proj_part("in_proj_d", h, land_i, proj, me_arr, 6, 1, 1)
    win_g = _w_in_finish(land_i, wi_send, fi_send, di_send, di_recv, after=[proj])
    proj = _in_proj_part("in_proj_e", h, win_g, proj, me_arr, 7, 1, 1)
    (fo_send, fo_recv), (land_o,), _ = _weights_forward("w_out_forward", land_o, wo_recv, after=[proj])
    slopes = _head_slopes(C // HEAD_DIM)
    ycat = _conv_fwd(proj, conv_w_full, conv_b, g_conv)
    o, lse = _attn_fwd(proj, slopes)
    ycat = _attn_post(ycat, o, proj, g_attn)
    wout_g = _weights_wait("w_out_wait", land_o, wo_send, wo_recv, fo_send, fo_recv, after=[ycat])
    wout_full = wout_g.reshape(D, D)
    y = _matmul(ycat, wout_full, name="out_proj", out_dtype=f32)
    dy, dout, post_sums = _sandwich(y, x2, tgt, gate, g_post)

    chip = me // 2

    def landing(rows, cols):
        return lax.dynamic_update_slice(lax.empty((NCHIP, rows, cols), bf16), jnp.zeros((1, rows, cols), bf16),
                                        (chip, 0, 0))

    gw_out = _matmul(ycat, dy, name="out_proj_dw", out_dtype=bf16, ta=True).reshape(NDEV, R, D)
    po_send, po_recv, gw_out, pair_o, po_token = _pair_start("g_out_pair_start", gw_out)
    dycat = _matmul(dy, wout_full, name="out_proj_dx", out_dtype=f32, tb=True, dep=po_token)
    gw_out, pair_o = _pair_wait("g_out_pair_wait", gw_out, pair_o, po_send, po_recv, after=[dycat])
    sum_o = _pair_sum("g_out_pair_sum", gw_out, pair_o, core)
    co_send, co_recv, sum_o, land_go, co_token = _chip_start(
        "g_out_chip_start", sum_o, landing(R, D), 0)
    dpc, conv_sums = _conv_bwd(proj, dycat, conv_w_full, conv_b, g_conv, co_token)
    gw_c = _matmul(h, dpc, name="in_proj_dw_conv", out_dtype=bf16, ta=True, out_slots=4)
    pc_send, pc_recv, gw_c, pair_c, pc_token = _pair_start("g_conv_pair_start", gw_c)
    do, dpa, attn_sums = _attn_post_bwd(o, proj, dycat, g_attn, pc_token)
    gw_c, pair_c = _pair_wait("g_conv_pair_wait", gw_c, pair_c, pc_send, pc_recv, after=[do])
    sum_c = _pair_sum("g_conv_pair_sum", gw_c, pair_c, core)
    cc_send, cc_recv, sum_c, land_gi, cc_token = _chip_start(
        "g_conv_chip_start", sum_c, landing(D, C), 0)
    dpa = _attn_bwd(proj, o, do, lse, slopes, dpa, cc_token)
    gw_a = _matmul(h, dpa, name="in_proj_dw_attn", out_dtype=bf16, ta=True, b_slots=True, out_slots=4)
    pa_send, pa_recv, gw_a, pair_a, pa_token = _pair_start("g_attn_pair_start", gw_a)
    gw_a, pair_a = _pair_wait("g_attn_pair_wait", gw_a, pair_a, pa_send, pa_recv, after=[pa_token])
    sum_a = _pair_sum("g_attn_pair_sum", gw_a, pair_a, core)
    part_a, part_b = (0, 3 * D // 4), (3 * D // 4, D // 4)
    ca_send, ca_recv, sum_a, land_gi, ca_token = _chip_start("g_attn_chip_start_a", sum_a, land_gi, 4, part_a)
    dh = _matmul_slabs_t(dpc, dpa, win_g, name="in_proj_dx", dep=ca_token)
    grad_x, pre_sums = _prenorm_bwd(dh, x2, dout, scale, g_pre)

    small = jnp.concatenate([pre_sums[0:1], pre_sums[1:2], post_sums[0:1],
                             pre_sums[2:3], post_sums[1:2],
                             conv_sums[2:3], conv_sums[3:4], conv_sums[4:5],
                             conv_sums[1:2], conv_sums[0:1], attn_sums[0:1]], axis=1)
    small = jnp.concatenate([small.reshape(8 * D // 128, 128), jnp.broadcast_to(post_sums[2:3, :128], (8, 128))])
    (small_all,) = _all_gather([small], "gather_small")
    cb_send, cb_recv, sum_a, land_gi, cb_token = _chip_start("g_attn_chip_start_b", sum_a, land_gi, 4, part_b,
                                                             after=[small_all])
    small_all = small_all.reshape(NDEV, small.size)
    tot = _sum_rows(small_all, cb_token)[0:1]
    loss = tot[0, 8 * D]

    g_b_ada = tot[:, :3 * D]
    g_g_pre, g_g_post = tot[:, 3 * D:4 * D], tot[:, 4 * D:5 * D]
    g_conv_w_full = tot[:, 5 * D:5 * D + 3 * C].reshape(3, C)
    g_conv_w = lax.dynamic_slice_in_dim(g_conv_w_full, me * CW, CW, axis=1)[None]
    g_conv_b, g_g_conv, g_g_attn = (tot[:, 5 * D + (3 + t) * C:5 * D + (4 + t) * C] for t in range(3))

    dmod_cols = lax.dynamic_slice_in_dim(small_all[:, :3 * D], me * W, W, axis=1)
    g_w_ada, d_w_ada, nm_w_ada, nv_w_ada = _adamw_ada(c_all.T, dmod_cols, w_ada2, m_w_ada[0], v_w_ada[0], cb_token)

    sum_o, land_go = _chip_wait("g_out_chip_wait", sum_o, land_go, co_send, co_recv, 0, after=[g_w_ada])
    pick_out = jnp.stack([jnp.int32(0), me // 2]).astype(jnp.int32)
    g_w_out, d_w_out, nm_w_out, nv_w_out = _adamw_sharded(
        "adamw_w_out", land_go, sum_o, sum_o, pick_out, w_out2, m_w_out[0], v_w_out[0])
    pick_in = jnp.stack([me // 4, (me % 4) // 2]).astype(jnp.int32)
    sum_c, land_gi = _chip_wait("g_conv_chip_wait", sum_c, land_gi, cc_send, cc_recv, 0, after=[g_w_out])
    sum_a, land_gi = _chip_wait("g_attn_chip_wait_a", sum_a, land_gi, ca_send, ca_recv, 4, [g_w_out], part_a)
    first = _adamw_sharded("adamw_w_in_a", land_gi, sum_c, sum_a, pick_in, w_in2, m_w_in[0], v_w_in[0], rows=part_a)
    sum_a, land_gi = _chip_wait("g_attn_chip_wait_b", sum_a, land_gi, cb_send, cb_recv, 4, [first[0]], part_b)
    g_w_in, d_w_in, nm_w_in, nv_w_in = _adamw_sharded(
        "adamw_w_in_b", land_gi, sum_c, sum_a, pick_in, w_in2, m_w_in[0], v_w_in[0], rows=part_b, prev=first)

    pack = lambda *vs: jnp.concatenate([a.reshape(1, -1) for a in vs], axis=1)
    smalls = [(b_ada, g_b_ada, m_b_ada, v_b_ada), (g_pre, g_g_pre, m_g_pre, v_g_pre),
              (conv_w, g_conv_w, m_conv_w, v_conv_w), (conv_b, g_conv_b, m_conv_b, v_conv_b),
              (g_conv, g_g_conv, m_g_conv, v_g_conv), (g_attn, g_g_attn, m_g_attn, v_g_attn),
              (g_post, g_g_post, m_g_post, v_g_post)]
    packed = [pack(*[s[t] for s in smalls]) for t in range(4)]
    npad = -packed[0].shape[1] % 128
    packed = [jnp.pad(p, ((0, 0), (0, npad)), constant_values=1.0) for p in packed]
    d_s, nm_s, nv_s = _adamw_small(*packed)

    def unpack(vec):
        out, at = [], 0
        for s in smalls:
            n = s[0].size
            out.append(vec[:, at:at + n].reshape(s[0].shape))
            at += n
        return out

    d_b_ada, d_g_pre, d_conv_w, d_conv_b, d_g_conv, d_g_attn, d_g_post = unpack(d_s)
    nm_b_ada, nm_g_pre, nm_conv_w, nm_conv_b, nm_g_conv, nm_g_attn, nm_g_post = unpack(nm_s)
    nv_b_ada, nv_g_pre, nv_conv_w, nv_conv_b, nv_g_conv, nv_g_attn, nv_g_post = unpack(nv_s)

    return (loss, grad_x[None],
            g_w_ada[None], g_b_ada, g_g_pre, g_w_in[None], g_conv_w, g_conv_b, g_g_conv, g_g_attn, g_w_out[None], g_g_post,
            d_w_ada[None], d_b_ada, d_g_pre, d_w_in[None], d_conv_w, d_conv_b, d_g_conv, d_g_attn, d_w_out[None], d_g_post,
            nm_w_ada[None], nm_b_ada, nm_g_pre, nm_w_in[None], nm_conv_w, nm_conv_b, nm_g_conv, nm_g_attn, nm_w_out[None], nm_g_post,
            nv_w_ada[None], nv_b_ada, nv_g_pre, nv_w_in[None], nv_conv_w, nv_conv_b, nv_g_conv, nv_g_attn, nv_w_out[None], nv_g_post)
```

```python
import functools
import math

import jax
import jax.numpy as jnp
from jax import lax
from jax.experimental import pallas as pl
from jax.experimental.pallas import tpu as pltpu

f32 = jnp.float32
bf16 = jnp.bfloat16

NDEV = 8
HEAD_DIM = 64
PAIR = 2 * HEAD_DIM
BRANCHES = ((128, 1, 1), (512, 4, 1), (2048, 16, 2))
HALF_WIN = 64
EPS = 1e-6
NEG_INF = -1e30
ADAM_LR, ADAM_B1, ADAM_B2, ADAM_EPS, ADAM_WD, ADAM_STEP = 0.001, 0.9, 0.999, 1e-08, 0.01, 10
MESH = pl.DeviceIdType.MESH
VMEM_LIMIT = 56 * 1024 * 1024
HBM_SPEC = pl.BlockSpec(memory_space=pltpu.HBM)
ANY_SPEC = pl.BlockSpec(memory_space=pl.ANY)
SEM_SPEC = pl.BlockSpec(memory_space=pltpu.SEMAPHORE)


def _params(*sem):
    return pltpu.CompilerParams(dimension_semantics=sem or None, vmem_limit_bytes=VMEM_LIMIT)


def _silu(z):
    return z * jax.nn.sigmoid(z)


def _dsilu(z):
    s = jax.nn.sigmoid(z)
    return s * (1.0 + z * (1.0 - s))


def _my_place():
    x, y, c = lax.axis_index("x"), lax.axis_index("y"), lax.axis_index("c")
    return x, y, c, 4 * x + 2 * y + c


def _peer(x, y, c, k):
    px, py, pc = x ^ (k >> 2 & 1), y ^ (k >> 1 & 1), c ^ (k & 1)
    return (px, py, pc), 4 * px + 2 * py + pc


def _all_gather(arrays, name):
    n = len(arrays)

    def body(*refs):
        srcs, dsts = refs[:n], refs[n:2 * n]
        send_sems, recv_sems, local_sems = refs[2 * n:]
        x, y, c, me = _my_place()
        locals_, sends = [], []
        for t in range(n):
            own = pltpu.make_async_copy(srcs[t], dsts[t].at[me], local_sems.at[t])
            own.start()
            locals_.append(own)
            for k in range(1, NDEV):
                peer, pidx = _peer(x, y, c, k)
                cp = pltpu.make_async_remote_copy(
                    src_ref=srcs[t], dst_ref=dsts[t].at[me], send_sem=send_sems.at[t, k],
                    recv_sem=recv_sems.at[t, k], device_id=peer, device_id_type=MESH)
                cp.start()
                sends.append(cp)
        for t in range(n):
            for k in range(1, NDEV):
                peer, pidx = _peer(x, y, c, k)
                pltpu.make_async_remote_copy(
                    src_ref=srcs[t], dst_ref=dsts[t].at[pidx], send_sem=send_sems.at[t, k],
                    recv_sem=recv_sems.at[t, k], device_id=peer, device_id_type=MESH).wait_recv()
        for cp in sends:
            cp.wait_send()
        for cp in locals_:
            cp.wait()

    return pl.pallas_call(
        body, name=name,
        out_shape=tuple(jax.ShapeDtypeStruct((NDEV,) + a.shape, a.dtype) for a in arrays),
        in_specs=[HBM_SPEC] * n, out_specs=tuple([HBM_SPEC] * n),
        scratch_shapes=[pltpu.SemaphoreType.DMA((n, NDEV)), pltpu.SemaphoreType.DMA((n, NDEV)),
                        pltpu.SemaphoreType.DMA((n,))],
    )(*arrays)


def _comm_call(name, arrays, sems, new_sems, body, after=(), token=False):
    na, ns, nn, nf = len(arrays), len(sems), len(new_sems), len(after)

    def kern(*refs):
        ins, outs = refs[:na + ns + nf], refs[na + ns + nf:]
        body(ins[:na], ins[na:na + ns], outs[:nn])
        if token:
            outs[nn + na][...] = jnp.zeros((8, 128), f32)

    out_shape = ([pltpu.SemaphoreType.DMA(s) for s in new_sems] + [pltpu.HBM(a.shape, a.dtype) for a in arrays]
                 + ([jax.ShapeDtypeStruct((8, 128), f32)] if token else []))
    out_specs = [SEM_SPEC] * nn + [HBM_SPEC] * na + ([pl.BlockSpec(memory_space=pltpu.VMEM)] if token else [])
    res = pl.pallas_call(
        kern, name=name, out_shape=tuple(out_shape),
        in_specs=[HBM_SPEC] * na + [SEM_SPEC] * ns + [ANY_SPEC] * nf, out_specs=tuple(out_specs),
        input_output_aliases={t: nn + t for t in range(na)},
        compiler_params=pltpu.CompilerParams(has_side_effects=pltpu.SideEffectType.DATAFLOW_SIDE_EFFECTING),
    )(*[pltpu.with_memory_space_constraint(a, pltpu.HBM) for a in arrays], *sems, *after)
    return list(res[:nn]), list(res[nn:nn + na]), (res[nn + na] if token else None)


def _remote(src, dst, send_sem, recv_sem, device):
    return pltpu.make_async_remote_copy(src_ref=src, dst_ref=dst, send_sem=send_sem, recv_sem=recv_sem,
                                        device_id=device, device_id_type=MESH)


SAME_CORE = (2, 4, 6)
VIA_SIBLING = (3, 5, 7)


def _weights_forward(name, land, recv, after):
    def body(a, s, new):
        (land,), (recv,), (fsend, frecv) = a, s, new
        x, y, c, me = _my_place()
        sibling, _ = _peer(x, y, c, 1)
        for k in SAME_CORE:
            peer, slot = _peer(x, y, c, k)
            _remote(land.at[slot], land.at[slot], fsend.at[k], recv.at[k], peer).wait_recv()
            _remote(land.at[slot], land.at[slot], fsend.at[k], frecv.at[k ^ 1], sibling).start()

    return _comm_call(name, [land], [recv], [(NDEV,), (NDEV,)], body, after=after)


def _weights_wait(name, land, send, recv, fsend, frecv, after):
    def body(a, s, new):
        (land,), (send, recv, fsend, frecv) = a, s
        x, y, c, me = _my_place()
        sibling, sib_slot = _peer(x, y, c, 1)
        _remote(land.at[sib_slot], land.at[sib_slot], send.at[1], recv.at[1], sibling).wait_recv()
        for k in VIA_SIBLING:
            _, slot = _peer(x, y, c, k)
            _remote(land.at[slot], land.at[slot], fsend.at[k ^ 1], frecv.at[k], sibling).wait_recv()
        for k in (1,) + SAME_CORE:
            peer, _ = _peer(x, y, c, k)
            _remote(land.at[me], land.at[me], send.at[k], recv.at[k], peer).wait_send()
        for k in SAME_CORE:
            _, slot = _peer(x, y, c, k)
            _remote(land.at[slot], land.at[slot], fsend.at[k], frecv.at[k ^ 1], sibling).wait_send()

    return _comm_call(name, [land], [send, recv, fsend, frecv], [], body, after=after)[1][0]


def _diag_relay(x, y, c):
    slot = 4 * (x ^ (1 - c)) + 2 * (y ^ c) + c
    return slot, (x ^ c, y ^ (1 - c), c)


def _w_in_start(land, after):
    def body(a, s, new):
        (land,), (send, recv) = a, new
        x, y, c, me = _my_place()
        for k in (1, 2, 4):
            peer, _ = _peer(x, y, c, k)
            _remote(land.at[me], land.at[me], send.at[k], recv.at[k], peer).start()

    (send, recv), (land,), token = _comm_call("w_in_start", [land], [], [(NDEV,), (NDEV,)], body, after=after, token=True)
    return send, recv, land, token


def _w_in_sibling(land, recv, after):
    def body(a, s, new):
        (land,), (recv,) = a, s
        x, y, c, me = _my_place()
        sibling, slot = _peer(x, y, c, 1)
        _remote(land.at[slot], land.at[slot], recv.at[1], recv.at[1], sibling).wait_recv()

    return _comm_call("w_in_sibling", [land], [recv], [], body, after=after)[1][0]


def _w_in_relay(land, recv, after):
    def body(a, s, new):
        (land,), (recv,), (fsend, frecv) = a, s, new
        x, y, c, me = _my_place()
        sibling, _ = _peer(x, y, c, 1)
        for k in (2, 4):
            peer, slot = _peer(x, y, c, k)
            _remote(land.at[slot], land.at[slot], fsend.at[k], recv.at[k], peer).wait_recv()
        slot, target = _diag_relay(x, y, c)
        _remote(land.at[slot], land.at[slot], fsend.at[6], frecv.at[6], target).start()
        for k in (2, 4):
            _, slot = _peer(x, y, c, k)
            _remote(land.at[slot], land.at[slot], fsend.at[k], frecv.at[k ^ 1], sibling).start()

    (fsend, frecv), (land,), _ = _comm_call("w_in_relay", [land], [recv], [(NDEV,), (NDEV,)], body, after=after)
    return fsend, frecv, land


def _w_in_forwarded(land, frecv, after):
    def body(a, s, new):
        (land,), (frecv,) = a, s
        x, y, c, me = _my_place()
        sibling, _ = _peer(x, y, c, 1)
        for k in (3, 5):
            _, slot = _peer(x, y, c, k)
            _remote(land.at[slot], land.at[slot], frecv.at[k], frecv.at[k], sibling).wait_recv()

    return _comm_call("w_in_forwarded", [land], [frecv], [], body, after=after)[1][0]


def _w_in_diag(land, land_o, frecv, after):
    def body(a, s, new):
        (land, land_o), (frecv,), (dsend, drecv, osend, orecv) = a, s, new
        x, y, c, me = _my_place()
        sibling, _ = _peer(x, y, c, 1)
        peer, slot = _peer(x, y, c, 6)
        _remote(land.at[slot], land.at[slot], dsend.at[6], frecv.at[6], peer).wait_recv()
        _remote(land.at[slot], land.at[slot], dsend.at[6], drecv.at[7], sibling).start()
        for k in (1,) + SAME_CORE:
            peer, _ = _peer(x, y, c, k)
            _remote(land_o.at[me], land_o.at[me], osend.at[k], orecv.at[k], peer).start()

    sems, (land, land_o), _ = _comm_call("w_in_diag", [land, land_o], [frecv], [(NDEV,)] * 4, body, after=after)
    return sems, land, land_o


def _w_in_finish(land, send, fsend, dsend, drecv, after):
    def body(a, s, new):
        (land,), (send, fsend, dsend, drecv) = a, s
        x, y, c, me = _my_place()
        sibling, _ = _peer(x, y, c, 1)
        _, slot = _peer(x, y, c, 7)
        _remote(land.at[slot], land.at[slot], dsend.at[6], drecv.at[7], sibling).wait_recv()
        for k in (1, 2, 4):
            peer, _ = _peer(x, y, c, k)
            _remote(land.at[me], land.at[me], send.at[k], send.at[k], peer).wait_send()
        for k in (2, 4, 6):
            _, slot = _peer(x, y, c, k)
            _remote(land.at[slot], land.at[slot], fsend.at[k], fsend.at[k], sibling).wait_send()
        _, slot = _peer(x, y, c, 6)
        _remote(land.at[slot], land.at[slot], dsend.at[6], dsend.at[6], sibling).wait_send()

    return _comm_call("w_in_finish", [land], [send, fsend, dsend, drecv], [], body, after=after)[1][0]


def _in_proj_part(name, h, land, proj, me_arr, k0, kstep, nk, tm=512):
    S, D = h.shape
    C = land.shape[2]
    tm = min(tm, S)

    def body(me_ref, a_ref, b_ref, *rest):
        rest[-1][...] = jnp.dot(a_ref[...], b_ref[...], preferred_element_type=f32)

    slot = lambda j, me: me[0] ^ (k0 + kstep * j)
    args = [h, land] + ([] if proj is None else [proj])
    grid_spec = pltpu.PrefetchScalarGridSpec(
        num_scalar_prefetch=1, grid=(nk, S // tm),
        in_specs=[pl.BlockSpec((tm, D), lambda j, i, me: (i, 0)),
                  pl.BlockSpec((None, D, C), lambda j, i, me: (slot(j, me), 0, 0))] + [ANY_SPEC] * (len(args) - 2),
        out_specs=pl.BlockSpec((tm, C), lambda j, i, me: (i, slot(j, me))))
    return pl.pallas_call(
        body, name=name, out_shape=jax.ShapeDtypeStruct((S, NDEV * C), f32), grid_spec=grid_spec,
        input_output_aliases={} if proj is None else {3: 0}, compiler_params=_params("arbitrary", "arbitrary"),
    )(me_arr, *args)


NCHIP = NDEV // 2


def _pair_start(name, src):
    npair = src.shape[0] // 2

    def body(a, s, new):
        (src, pair), (send, recv) = a, new
        x, y, c, me = _my_place()
        sibling, _ = _peer(x, y, c, 1)
        for i in range(npair):
            _remote(src.at[2 * i + 1 - c], pair.at[i], send.at[i], recv.at[i], sibling).start()

    pair = lax.empty((npair,) + src.shape[1:], src.dtype)
    (send, recv), (src, pair), token = _comm_call(name, [src, pair], [], [(npair,), (npair,)], body, token=True)
    return send, recv, src, pair, token


def _pair_wait(name, src, pair, send, recv, after):
    npair = pair.shape[0]

    def body(a, s, new):
        (src, pair), (send, recv) = a, s
        x, y, c, me = _my_place()
        sibling, _ = _peer(x, y, c, 1)
        for i in range(npair):
            cp = _remote(src.at[2 * i + 1 - c], pair.at[i], send.at[i], recv.at[i], sibling)
            cp.wait_recv()
            cp.wait_send()

    return _comm_call(name, [src, pair], [send, recv], [], body, after=after)[1]


def _pair_sum(name, src, pair, core, tr=256):
    npair, R, Cc = pair.shape
    tr = min(tr, R)

    def body(core_ref, a_ref, b_ref, o_ref):
        o_ref[...] = (a_ref[...].astype(f32) + b_ref[...].astype(f32)).astype(o_ref.dtype)

    grid_spec = pltpu.PrefetchScalarGridSpec(
        num_scalar_prefetch=1, grid=(npair, R // tr),
        in_specs=[pl.BlockSpec((None, tr, Cc), lambda i, r, core: (2 * i + core[0], r, 0)),
                  pl.BlockSpec((None, tr, Cc), lambda i, r, core: (i, r, 0))],
        out_specs=pl.BlockSpec((None, tr, Cc), lambda i, r, core: (i, r, 0)))
    return pl.pallas_call(body, name=name, out_shape=jax.ShapeDtypeStruct(pair.shape, pair.dtype),
                          grid_spec=grid_spec, compiler_params=_params("parallel", "parallel"))(core, src, pair)


def _owner_chip(first, i):
    q = first // 2 + i
    return q >> 1 & 1, q & 1


def _chip_start(name, sums, land, first, rows=None, after=()):
    npair = sums.shape[0]
    rows = pl.ds(*(rows or (0, sums.shape[1])))

    def body(a, s, new):
        (sums, land), (send, recv) = a, new
        x, y, c, me = _my_place()
        for i in range(npair):
            ox, oy = _owner_chip(first, i)

            @pl.when((x != ox) | (y != oy))
            def _():
                _remote(sums.at[i, rows], land.at[2 * x + y, rows], send.at[i], recv.at[2 * x + y], (ox, oy, c)).start()

    (send, recv), (sums, land), token = _comm_call(name, [sums, land], [], [(npair,), (NCHIP,)], body, after=after,
                                                   token=True)
    return send, recv, sums, land, token


def _chip_wait(name, sums, land, send, recv, first, after, rows=None):
    npair = sums.shape[0]
    rows = pl.ds(*(rows or (0, sums.shape[1])))

    def body(a, s, new):
        (sums, land), (send, recv) = a, s
        x, y, c, me = _my_place()
        mine = (me >= first) & (me < first + 2 * npair)
        for i in range(npair):
            ox, oy = _owner_chip(first, i)

            @pl.when((x != ox) | (y != oy))
            def _():
                _remote(sums.at[i, rows], land.at[2 * x + y, rows], send.at[i], recv.at[2 * x + y], (ox, oy, c)).wait_send()
        for q in range(NCHIP):
            @pl.when(mine & (2 * x + y != q))
            def _():
                _remote(sums.at[0, rows], land.at[q, rows], send.at[0], recv.at[q], (q >> 1, q & 1, c)).wait_recv()

    return _comm_call(name, [sums, land], [send, recv], [], body, after=after)[1]


def _matmul(a, b, *, name, out_dtype, ta=False, tb=False, b_slots=False, out_slots=0, b_cols=None,
            tm=1024, tn=1024, tk=2048, dep=None):
    M, K = (a.shape[1], a.shape[0]) if ta else a.shape
    col0 = 0
    if b_slots:
        slab = b.shape[2]
        N = b.shape[1] if tb else b.shape[0] * slab
        assert (K if tb else N) == b.shape[0] * slab
    elif b_cols is not None:
        assert not tb
        col0, N = b_cols
    else:
        N = b.shape[0] if tb else b.shape[1]
    tm, tn, tk = min(tm, M), min(tn, N), min(tk, K)
    if b_slots:
        if tb:
            tk = min(tk, slab)
        else:
            tn = min(tn, slab)
    if out_slots:
        tn = min(tn, N // out_slots)
    nm, nn, nk = M // tm, N // tn, K // tk
    assert (nm * tm, nn * tn, nk * tk) == (M, N, K) and col0 % tn == 0, (name, M, N, K, tm, tn, tk)
    j0 = col0 // tn

    a_spec = pl.BlockSpec((tk, tm), lambda i, j, k: (k, i)) if ta else pl.BlockSpec((tm, tk), lambda i, j, k: (i, k))
    if b_slots and tb:
        per = slab // tk
        b_spec = pl.BlockSpec((None, tn, tk), lambda i, j, k: (k // per, j, k % per))
    elif b_slots:
        per = slab // tn
        b_spec = pl.BlockSpec((None, tk, tn), lambda i, j, k: (j // per, k, j % per))
    elif tb:
        b_spec = pl.BlockSpec((tn, tk), lambda i, j, k: (j, k))
    else:
        b_spec = pl.BlockSpec((tk, tn), lambda i, j, k: (k, j + j0))
    if out_slots:
        per_o = (N // out_slots) // tn
        o_spec = pl.BlockSpec((None, tm, tn), lambda i, j, k: (j // per_o, i, j % per_o))
        out_shape = jax.ShapeDtypeStruct((out_slots, M, N // out_slots), out_dtype)
    else:
        o_spec = pl.BlockSpec((tm, tn), lambda i, j, k: (i, j))
        out_shape = jax.ShapeDtypeStruct((M, N), out_dtype)
    dims = (((0 if ta else 1,), (1 if tb else 0,)), ((), ()))
    deps = [] if dep is None else [dep]

    def body(a_ref, b_ref, *rest):
        o_ref = rest[len(deps)]
        prod = lax.dot_general(a_ref[...], b_ref[...], dims, preferred_element_type=f32)
        if nk == 1:
            o_ref[...] = prod.astype(out_dtype)
            return
        acc_ref = rest[len(deps) + 1]
        k = pl.program_id(2)

        @pl.when(k == 0)
        def _():
            acc_ref[...] = prod

        @pl.when((k > 0) & (k < nk - 1))
        def _():
            acc_ref[...] += prod

        @pl.when(k == nk - 1)
        def _():
            o_ref[...] = (acc_ref[...] + prod).astype(out_dtype)

    return pl.pallas_call(
        body, name=name, out_shape=out_shape, grid=(nm, nn, nk),
        in_specs=[a_spec, b_spec] + [ANY_SPEC] * len(deps), out_specs=o_spec,
        scratch_shapes=[pltpu.VMEM((tm, tn), f32)] if nk > 1 else [],
        compiler_params=_params("parallel", "parallel", "arbitrary"),
    )(a, b, *deps)


def _matmul_slabs_t(a_cols, a_slots, b, *, name, tm=512, tn=512, dep=None):
    M = a_cols.shape[0]
    n_slab, N, slab = b.shape
    n1, n2 = a_cols.shape[1] // slab, a_slots.shape[0]
    assert n1 + n2 == n_slab and a_slots.shape[1:] == (M, slab)
    tm, tn = min(tm, M), min(tn, N)
    deps = [] if dep is None else [dep]

    def body(a1_ref, a2_ref, b_ref, *rest):
        o_ref = rest[len(deps)]
        acc = None
        for s in range(n_slab):
            lhs = a1_ref[:, s * slab:(s + 1) * slab] if s < n1 else a2_ref[s - n1]
            prod = lax.dot_general(lhs, b_ref[s], (((1,), (1,)), ((), ())), preferred_element_type=f32)
            acc = prod if acc is None else acc + prod
        o_ref[...] = acc

    return pl.pallas_call(
        body, name=name, out_shape=jax.ShapeDtypeStruct((M, N), f32), grid=(M // tm, N // tn),
        in_specs=[pl.BlockSpec((tm, n1 * slab), lambda i, j: (i, 0)), pl.BlockSpec((n2, tm, slab), lambda i, j: (0, i, 0)),
                  pl.BlockSpec((n_slab, tn, slab), lambda i, j: (0, j, 0))] + [ANY_SPEC] * len(deps),
        out_specs=pl.BlockSpec((tm, tn), lambda i, j: (i, j)), compiler_params=_params("parallel", "parallel"),
    )(a_cols, a_slots, b, *deps)


def _ada_exchange(c_blk, cw_slab, w_ada, b_cols):
    nblk = c_blk.shape[0]
    D, W = w_ada.shape
    CW = cw_slab.shape[1]

    def body(c_ref, cw_ref, w_ref, b_ref, mod_ref, call_ref, cwg_ref, msend, send_sems, recv_sems):
        x, y, c, me = _my_place()
        call_ref[me] = _silu(c_ref[...])
        cwg_ref[me] = cw_ref[...]
        first = []
        for k in range(1, NDEV):
            peer, _ = _peer(x, y, c, k)
            first.append(_remote(call_ref.at[me], call_ref.at[me], send_sems.at[0, k], recv_sems.at[0, k], peer))
            first.append(_remote(cwg_ref.at[me], cwg_ref.at[me], send_sems.at[1, k], recv_sems.at[1, k], peer))
        for cp in first:
            cp.start()
        for k in range(1, NDEV):
            peer, slot = _peer(x, y, c, k)
            _remote(call_ref.at[slot], call_ref.at[slot], send_sems.at[0, k], recv_sems.at[0, k], peer).wait_recv()
            _remote(cwg_ref.at[slot], cwg_ref.at[slot], send_sems.at[1, k], recv_sems.at[1, k], peer).wait_recv()
        mod = jnp.broadcast_to(b_ref[...], (NDEV, W))
        for r in range(nblk):
            mod = mod + lax.dot_general(call_ref[:, r, :], w_ref[r * 128:(r + 1) * 128, :], (((1,), (0,)), ((), ())),
                                        preferred_element_type=f32, precision=lax.Precision.HIGHEST)
        row = lax.broadcasted_iota(jnp.int32, (NDEV, 1), 0)
        pick = lambda j: jnp.broadcast_to(jnp.sum(jnp.where(row == j, mod, 0.0), axis=0, keepdims=True), (8, W))
        mod_ref[me] = pick(me)
        second = []
        for k in range(1, NDEV):
            peer, slot = _peer(x, y, c, k)
            msend[k] = pick(slot)
            second.append(_remote(msend.at[k], mod_ref.at[me], send_sems.at[2, k], recv_sems.at[2, k], peer))
        for cp in second:
            cp.start()
        for k in range(1, NDEV):
            peer, slot = _peer(x, y, c, k)
            _remote(msend.at[k], mod_ref.at[slot], send_sems.at[2, k], recv_sems.at[2, k], peer).wait_recv()
        for cp in first + second:
            cp.wait_send()

    vmem = pl.BlockSpec(memory_space=pltpu.VMEM)
    return pl.pallas_call(
        body, name="ada_exchange",
        out_shape=(jax.ShapeDtypeStruct((NDEV, 8, W), f32), jax.ShapeDtypeStruct((NDEV, nblk, 128), f32),
                   jax.ShapeDtypeStruct((NDEV, 8, CW), f32)),
        in_specs=[vmem] * 4, out_specs=(vmem, vmem, vmem),
        scratch_shapes=[pltpu.VMEM((NDEV, 8, W), f32), pltpu.SemaphoreType.DMA((3, NDEV)),
                        pltpu.SemaphoreType.DMA((3, NDEV))],
        compiler_params=_params(),
    )(c_blk, cw_slab, w_ada, b_cols)


def _prenorm(x, scale, shift, g_pre, dep, tr=64):
    S, D = x.shape
    tr = min(tr, S)

    def body(x_ref, sc_ref, sh_ref, g_ref, dep_ref, h_ref):
        xv = x_ref[...]
        r = lax.rsqrt(jnp.mean(xv * xv, axis=-1, keepdims=True) + EPS)
        h_ref[...] = ((xv * r) * g_ref[...] * (1.0 + sc_ref[...]) + sh_ref[...]).astype(bf16)

    row = pl.BlockSpec((tr, D), lambda i: (i, 0))
    vec = pl.BlockSpec((1, D), lambda i: (0, 0))
    return pl.pallas_call(body, name="prenorm", out_shape=jax.ShapeDtypeStruct((S, D), bf16), grid=(S // tr,),
                          in_specs=[row, vec, vec, vec, ANY_SPEC], out_specs=row, compiler_params=_params("parallel"))(
                              x, scale, shift, g_pre, dep)


def _ext_rows(i, tr, S):
    g = lax.broadcasted_iota(jnp.int32, (tr + 16, 1), 0) + (i * tr - 8)
    return (g >= 0) & (g < S)


def _halo_specs(tr, S, C, col):
    nb8 = S // 8
    main = pl.BlockSpec((tr, C), lambda i: (i, col))
    prev = pl.BlockSpec((8, C), lambda i: (jnp.maximum(i * (tr // 8) - 1, 0), col))
    nxt = pl.BlockSpec((8, C), lambda i: (jnp.minimum((i + 1) * (tr // 8), nb8 - 1), col))
    return prev, main, nxt


def _conv_fwd(proj, conv_w, conv_b, g_conv, tr=128):
    S, C = proj.shape[0], proj.shape[1] // 8
    tr = min(tr, S)

    def body(up, um, un, cp, cm, cn, bg_ref, zc_ref, w_ref, cb_ref, g_ref, o_ref):
        i = pl.program_id(0)
        exists = _ext_rows(i, tr, S)
        u = jnp.concatenate([up[...], um[...], un[...]], axis=0)
        cg = jnp.concatenate([cp[...], cm[...], cn[...]], axis=0)
        t = jnp.where(exists, cg * u, 0.0)
        t_before = pltpu.roll(t, 1, 0)[8:tr + 8]
        t_after = pltpu.roll(t, tr + 15, 0)[8:tr + 8]
        w = w_ref[...]
        cv = w[0:1] * t_before + w[1:2] * t[8:tr + 8] + w[2:3] * t_after + cb_ref[...]
        yc = bg_ref[...] * cv
        rc = lax.rsqrt(jnp.mean(yc * yc, axis=-1, keepdims=True) + EPS)
        o_ref[...] = ((yc * rc) * g_ref[...] * _silu(zc_ref[...])).astype(bf16)

    u_specs = _halo_specs(tr, S, C, 0)
    c_specs = _halo_specs(tr, S, C, 2)
    vec = pl.BlockSpec((1, C), lambda i: (0, 0))
    return pl.pallas_call(
        body, name="conv_fwd", out_shape=jax.ShapeDtypeStruct((S, 2 * C), bf16), grid=(S // tr,),
        in_specs=[*u_specs, *c_specs, pl.BlockSpec((tr, C), lambda i: (i, 1)), pl.BlockSpec((tr, C), lambda i: (i, 3)),
                  pl.BlockSpec((8, C), lambda i: (0, 0)), vec, vec],
        out_specs=pl.BlockSpec((tr, C), lambda i: (i, 0)), compiler_params=_params("parallel"),
    )(proj, proj, proj, proj, proj, proj, proj, proj, conv_w, conv_b, g_conv)


def _branch_geometry(S, r, inter):
    L = S // r * inter
    nq = min(128, L)
    nk = min(nq + 2 * HALF_WIN * inter, L)
    assert L % nq == 0 and (L == nk or L >= nq + 2 * HALF_WIN * inter)
    return L, nq, nk, L // nq


QUAD = 4


def _to_quad(dst, src, S):
    n = S // QUAD
    for rho in range(QUAD):
        dst[pl.ds(rho * n, n), :] = src[pl.ds(rho, n, stride=QUAD), :]


def _block_rows(idx, r, inter, S, L, nq, nk, nblk):
    rho, qb = (0, idx) if r == 1 else (idx // nblk, idx % nblk)
    i0 = qb * nq
    ws = jnp.clip(i0 - HALF_WIN * inter, 0, L - nk)
    if r == 1:
        return pl.ds(pl.multiple_of(i0, 8), nq), pl.ds(pl.multiple_of(ws, 8), nk), i0 - ws
    assert r % (QUAD * inter) == 0
    step = r // QUAD // inter
    base = (rho % QUAD) * (S // QUAD) + rho // QUAD
    if step == 1:
        return pl.ds(pl.multiple_of(base + i0, 8), nq), pl.ds(pl.multiple_of(base + ws, 8), nk), i0 - ws
    return pl.ds(base + step * i0, nq, stride=step), pl.ds(base + step * ws, nk, stride=step), i0 - ws


N_CASES = 3
SCALE = HEAD_DIM ** -0.5
ATTN_UNROLL = 8


def _bias_shape(S):
    shapes = [_branch_geometry(S, r, inter)[1:3] for _, r, inter in BRANCHES]
    return (len(BRANCHES) * N_CASES * 2, max(nq for nq, _ in shapes), max(nk for _, nk in shapes))


def _bias_index(b, case, head):
    return (b * N_CASES + case) * 2 + head


def _fill_bias(bias_scr, sl_ref, S):
    sl = sl_ref[...]
    slope = (sl[0:1, 0:1], sl[0:1, HEAD_DIM:HEAD_DIM + 1])
    for b, (_, r, inter) in enumerate(BRANCHES):
        L, nq, nk, nblk = _branch_geometry(S, r, inter)
        rel = lax.broadcasted_iota(jnp.int32, (nq, nk), 0) - lax.broadcasted_iota(jnp.int32, (nq, nk), 1)
        for case in range(N_CASES):
            d = jnp.abs(rel + case * HALF_WIN)
            valid = d <= HALF_WIN * inter
            if inter > 1:
                valid = valid & (jnp.bitwise_and(d, inter - 1) == 0)
            dist = d.astype(f32) * float(r // inter)
            for head in range(2):
                bias_scr[_bias_index(b, case, head), 0:nq, 0:nk] = jnp.where(valid, -slope[head] * dist, NEG_INF)


def _head_slopes(n_heads):
    slopes = 2.0 ** (-8.0 * jnp.arange(1, n_heads + 1, dtype=f32) / n_heads)
    return jnp.broadcast_to(jnp.repeat(slopes.reshape(n_heads // 2, 2), HEAD_DIM, axis=1)[:, None, :],
                            (n_heads // 2, 8, PAIR))


def _attn_fwd(proj, slopes):
    S, C = proj.shape[0], proj.shape[1] // 8
    npair = C // PAIR

    def body(q_ref, k_ref, v_ref, sl_ref, o_ref, lse_ref, m_scr, l_scr, a_scr, bias_scr, q4_scr, k4_scr, v4_scr):
        lane = lax.broadcasted_iota(jnp.int32, (1, PAIR), 1)
        first = lane < HEAD_DIM
        _fill_bias(bias_scr, sl_ref, S)
        for dst, src in ((q4_scr, q_ref), (k4_scr, k_ref), (v4_scr, v_ref)):
            _to_quad(dst, src, S)

        for b, (_, r, inter) in enumerate(BRANCHES):
            L, nq, nk, nblk = _branch_geometry(S, r, inter)
            qs, ks, vs = (q_ref, k_ref, v_ref) if r == 1 else (q4_scr, k4_scr, v4_scr)

            def step(idx, carry, b=b, r=r, L=L, nq=nq, nk=nk, nblk=nblk, qs=qs, ks=ks, vs=vs):
                qrows, krows, off = _block_rows(idx, r, inter, S, L, nq, nk, nblk)
                case = off // HALF_WIN
                q2 = qs[qrows, :] * SCALE
                k2 = ks[krows, :].astype(bf16)
                v2 = vs[krows, :].astype(bf16)
                ms, accs = [], []
                for hh in range(2):
                    mine = first if hh == 0 else ~first
                    qh = jnp.where(mine, q2, 0.0).astype(bf16)
                    s = lax.dot_general(qh, k2, (((1,), (1,)), ((), ())), preferred_element_type=f32)
                    s = s + bias_scr[_bias_index(b, case, hh), 0:nq, 0:nk]
                    m = jnp.max(s, axis=-1, keepdims=True)
                    p = jnp.exp(s - m).astype(bf16)
                    vh = jnp.where(mine, v2, jnp.ones_like(v2))
                    ms.append(m)
                    accs.append(jnp.dot(p, vh, preferred_element_type=f32))
                m_scr[b, qrows, :] = jnp.where(first, ms[0], ms[1])
                a_scr[b, qrows, :] = jnp.where(first, accs[0], accs[1])
                l_scr[b, qrows, :] = jnp.where(first, accs[1], accs[0])
                return carry

            lax.fori_loop(0, S // nq, step, 0, unroll=min(ATTN_UNROLL, S // nq))

        n4 = S // QUAD
        ch = min(256, n4)
        nch = n4 // ch

        def merge(i, carry):
            rho, part = i // nch, i % nch
            sorted_rows = pl.ds(pl.multiple_of(rho * n4 + part * ch, 8), ch)
            token_rows = pl.ds(rho + QUAD * part * ch, ch, stride=QUAD)
            rows = (token_rows,) + (sorted_rows,) * (len(BRANCHES) - 1)
            ms = [m_scr[b, rows[b], :] for b in range(len(BRANCHES))]
            m = functools.reduce(jnp.maximum, ms)
            l = jnp.zeros((ch, PAIR), f32)
            acc = jnp.zeros((ch, PAIR), f32)
            for b in range(len(BRANCHES)):
                w = jnp.exp(ms[b] - m)
                l = l + w * pltpu.roll(l_scr[b, rows[b], :], HEAD_DIM, 1)
                acc = acc + w * a_scr[b, rows[b], :]
            o_ref[token_rows, :] = acc / l
            lse_ref[token_rows, :] = m + jnp.log(l)
            return carry

        lax.fori_loop(0, QUAD * nch, merge, 0)

    blk = lambda part: pl.BlockSpec((S, PAIR), lambda p: (0, part * npair + p))
    out = pl.BlockSpec((S, PAIR), lambda p: (0, p))
    return pl.pallas_call(
        body, name="attn_fwd",
        out_shape=(jax.ShapeDtypeStruct((S, C), f32), jax.ShapeDtypeStruct((S, C), f32)), grid=(npair,),
        in_specs=[blk(4), blk(5), blk(6), pl.BlockSpec((None, 8, PAIR), lambda p: (p, 0, 0))],
        out_specs=(out, out),
        scratch_shapes=[pltpu.VMEM((3, S, PAIR), f32)] * 3 + [pltpu.VMEM(_bias_shape(S), f32)]
        + [pltpu.VMEM((S, PAIR), f32)] * 3,
        compiler_params=_params("parallel"),
    )(proj, proj, proj, slopes)


def _attn_post(ycat, o, proj, g_attn, tr=128):
    S, C = o.shape
    tr = min(tr, S)

    def body(y_ref, o_ref, z_ref, g_ref, out_ref):
        del y_ref
        ov = o_ref[...]
        ra = lax.rsqrt(jnp.mean(ov * ov, axis=-1, keepdims=True) + EPS)
        out_ref[...] = ((ov * ra) * g_ref[...] * _silu(z_ref[...])).astype(bf16)

    return pl.pallas_call(
        body, name="attn_post", out_shape=jax.ShapeDtypeStruct(ycat.shape, ycat.dtype), grid=(S // tr,),
        in_specs=[HBM_SPEC, pl.BlockSpec((tr, C), lambda i: (i, 0)), pl.BlockSpec((tr, C), lambda i: (i, 7)),
                  pl.BlockSpec((1, C), lambda i: (0, 0))],
        out_specs=pl.BlockSpec((tr, C), lambda i: (i, 1)), input_output_aliases={0: 0},
        compiler_params=_params("arbitrary"),
    )(ycat, o, proj, g_attn)


def _sandwich(y, x, target, gate, g_post, tr=64):
    S, D = y.shape
    tr = min(tr, S)

    def body(y_ref, x_ref, t_ref, gate_ref, g_ref, dy_ref, dout_ref, sums_ref):
        i = pl.program_id(0)
        yv = y_ref[...]
        rp = lax.rsqrt(jnp.mean(yv * yv, axis=-1, keepdims=True) + EPS)
        yhat = yv * rp
        yn = yhat * g_ref[...]
        err = (x_ref[...] + gate_ref[...] * yn) - t_ref[...]
        dout = err * (1.0 / D)
        dout_ref[...] = dout
        dyn = dout * gate_ref[...]
        w = dyn * g_ref[...]
        dy_ref[...] = (rp * (w - yhat * jnp.mean(w * yhat, axis=-1, keepdims=True))).astype(bf16)
        loss = 0.5 * jnp.sum(jnp.mean(err * err, axis=-1, keepdims=True), axis=0, keepdims=True)
        row = lax.broadcasted_iota(jnp.int32, (8, D), 0)
        upd = jnp.where(row == 0, jnp.sum(dout * yn, axis=0, keepdims=True),
                        jnp.where(row == 1, jnp.sum(dyn * yhat, axis=0, keepdims=True),
                                  jnp.where(row == 2, loss, 0.0)))

        @pl.when(i == 0)
        def _():
            sums_ref[...] = upd

        @pl.when(i > 0)
        def _():
            sums_ref[...] += upd

    row = pl.BlockSpec((tr, D), lambda i: (i, 0))
    vec = pl.BlockSpec((1, D), lambda i: (0, 0))
    return pl.pallas_call(
        body, name="sandwich",
        out_shape=(jax.ShapeDtypeStruct((S, D), bf16), jax.ShapeDtypeStruct((S, D), f32), jax.ShapeDtypeStruct((8, D), f32)),
        grid=(S // tr,), in_specs=[row, row, row, vec, vec],
        out_specs=(row, row, pl.BlockSpec((8, D), lambda i: (0, 0))), compiler_params=_params("arbitrary"),
    )(y, x, target, gate, g_post)


def _conv_bwd(proj, dycat, conv_w, conv_b, g_conv, dep, tr=64):
    S, C = proj.shape[0], proj.shape[1] // 8
    tr = min(tr, S)
    n = tr + 16

    def body(*refs):
        ins, (w_ref, cb_ref, g_ref, _, dp_ref, sums_ref) = refs[:15], refs[15:]
        i = pl.program_id(0)
        exists = _ext_rows(i, tr, S)
        u, bg, cg, zc, dyn = (jnp.concatenate([ins[3 * t][...], ins[3 * t + 1][...], ins[3 * t + 2][...]], axis=0)
                              for t in range(5))
        w = w_ref[...]
        t = jnp.where(exists, cg * u, 0.0)
        t_before, t_after = pltpu.roll(t, 1, 0), pltpu.roll(t, n - 1, 0)
        cv = w[0:1] * t_before + w[1:2] * t + w[2:3] * t_after + cb_ref[...]
        yc = bg * cv
        rc = lax.rsqrt(jnp.mean(yc * yc, axis=-1, keepdims=True) + EPS)
        yhat = yc * rc
        sz = _silu(zc)
        wgt = dyn * g_ref[...] * sz
        dyc = rc * (wgt - yhat * jnp.mean(wgt * yhat, axis=-1, keepdims=True))
        dcv = jnp.where(exists, dyc * bg, 0.0)
        dt = w[0:1] * pltpu.roll(dcv, n - 1, 0) + w[1:2] * dcv + w[2:3] * pltpu.roll(dcv, 1, 0)
        mid = slice(8, tr + 8)
        dp_ref[:, 0:C] = (dt * cg)[mid].astype(bf16)
        dp_ref[:, C:2 * C] = (dyc * cv)[mid].astype(bf16)
        dp_ref[:, 2 * C:3 * C] = (dt * u)[mid].astype(bf16)
        dp_ref[:, 3 * C:4 * C] = (dyn * yhat * g_ref[...] * _dsilu(zc))[mid].astype(bf16)
        colsum = lambda v: jnp.sum(v[mid], axis=0, keepdims=True)
        parts = [colsum(dyn * yhat * sz), colsum(dcv), colsum(dcv * t_before), colsum(dcv * t), colsum(dcv * t_after)]
        row = lax.broadcasted_iota(jnp.int32, (8, C), 0)
        upd = jnp.zeros((8, C), f32)
        for j, pj in enumerate(parts):
            upd = jnp.where(row == j, pj, upd)

        @pl.when(i == 0)
        def _():
            sums_ref[...] = upd

        @pl.when(i > 0)
        def _():
            sums_ref[...] += upd

    specs = []
    for col in range(4):
        specs += _halo_specs(tr, S, C, col)
    specs += _halo_specs(tr, S, C, 0)
    vec = pl.BlockSpec((1, C), lambda i: (0, 0))
    return pl.pallas_call(
        body, name="conv_bwd",
        out_shape=(jax.ShapeDtypeStruct((S, 4 * C), bf16), jax.ShapeDtypeStruct((8, C), f32)), grid=(S // tr,),
        in_specs=[*specs, pl.BlockSpec((8, C), lambda i: (0, 0)), vec, vec, ANY_SPEC],
        out_specs=(pl.BlockSpec((tr, 4 * C), lambda i: (i, 0)), pl.BlockSpec((8, C), lambda i: (0, 0))),
        compiler_params=_params("arbitrary"),
    )(*([proj] * 12), dycat, dycat, dycat, conv_w, conv_b, g_conv, dep)


def _attn_post_bwd(o, proj, dycat, g_attn, dep, tr=128):
    S, C = o.shape
    tr = min(tr, S)

    def body(o_ref, z_ref, dy_ref, g_ref, dep_ref, do_ref, dz_ref, sums_ref):
        i = pl.program_id(0)
        ov, zv, dyn = o_ref[...], z_ref[...], dy_ref[...]
        ra = lax.rsqrt(jnp.mean(ov * ov, axis=-1, keepdims=True) + EPS)
        ohat = ov * ra
        sz = _silu(zv)
        wgt = dyn * g_ref[...] * sz
        do_ref[...] = ra * (wgt - ohat * jnp.mean(wgt * ohat, axis=-1, keepdims=True))
        dz_ref[...] = (dyn * ohat * g_ref[...] * _dsilu(zv)).astype(bf16)
        row = lax.broadcasted_iota(jnp.int32, (8, C), 0)
        upd = jnp.where(row == 0, jnp.sum(dyn * ohat * sz, axis=0, keepdims=True), 0.0)

        @pl.when(i == 0)
        def _():
            sums_ref[...] = upd

        @pl.when(i > 0)
        def _():
            sums_ref[...] += upd

    return pl.pallas_call(
        body, name="attn_post_bwd",
        out_shape=(jax.ShapeDtypeStruct((S, C), f32), jax.ShapeDtypeStruct((4, S, C), bf16),
                   jax.ShapeDtypeStruct((8, C), f32)),
        grid=(S // tr,),
        in_specs=[pl.BlockSpec((tr, C), lambda i: (i, 0)), pl.BlockSpec((tr, C), lambda i: (i, 7)),
                  pl.BlockSpec((tr, C), lambda i: (i, 1)), pl.BlockSpec((1, C), lambda i: (0, 0)), ANY_SPEC],
        out_specs=(pl.BlockSpec((tr, C), lambda i: (i, 0)), pl.BlockSpec((None, tr, C), lambda i: (3, i, 0)),
                   pl.BlockSpec((8, C), lambda i: (0, 0))),
        compiler_params=_params("arbitrary"),
    )(o, proj, dycat, g_attn, dep)


def _attn_bwd(proj, o, do, lse, slopes, dqkvz, dep):
    S, C = o.shape
    npair = C // PAIR

    def body(q_ref, k_ref, v_ref, o_ref, do_ref, lse_ref, sl_ref, old_ref, dep_ref, dqkv_ref,
             acc_scr, dl_scr, quad_scr, bias_scr):
        lane = lax.broadcasted_iota(jnp.int32, (1, PAIR), 1)
        first = lane < HEAD_DIM
        _fill_bias(bias_scr, sl_ref, S)
        ch = min(256, S)

        def prep(i, carry):
            rows = pl.ds(pl.multiple_of(i * ch, 8), ch)
            prod = do_ref[rows, :] * o_ref[rows, :]
            d0 = jnp.sum(jnp.where(first, prod, 0.0), axis=-1, keepdims=True)
            d1 = jnp.sum(jnp.where(first, 0.0, prod), axis=-1, keepdims=True)
            dl_scr[rows, :] = jnp.where(first, d0, d1)
            zero = jnp.zeros((ch, PAIR), f32)
            for order in range(2):
                for t in range(3):
                    acc_scr[order, t, rows, :] = zero
            return carry

        lax.fori_loop(0, S // ch, prep, 0)
        token_srcs = (q_ref, k_ref, v_ref, do_ref, lse_ref, dl_scr)
        for j, src in enumerate(token_srcs):
            _to_quad(quad_scr.at[j], src, S)

        for b, (_, r, inter) in enumerate(BRANCHES):
            L, nq, nk, nblk = _branch_geometry(S, r, inter)
            order = 0 if r == 1 else 1
            srcs = token_srcs if r == 1 else tuple(quad_scr.at[j] for j in range(6))

            def step(idx, carry, b=b, r=r, L=L, nq=nq, nk=nk, nblk=nblk, order=order, srcs=srcs):
                qs, ks, vs, dos, lses, dls = srcs
                dq_scr, dk_scr, dv_scr = (acc_scr.at[order, t] for t in range(3))
                qrows, krows, off = _block_rows(idx, r, inter, S, L, nq, nk, nblk)
                case = off // HALF_WIN
                q2 = qs[qrows, :] * SCALE
                k2 = ks[krows, :].astype(bf16)
                v2 = vs[krows, :].astype(bf16)
                do2 = dos[qrows, :]
                lse2 = lses[qrows, :]
                dl2 = dls[qrows, :]
                dq2 = jnp.zeros((nq, PAIR), f32)
                dk2 = jnp.zeros((nk, PAIR), f32)
                dv2 = jnp.zeros((nk, PAIR), f32)
                for hh in range(2):
                    mine = first if hh == 0 else ~first
                    lo = hh * HEAD_DIM
                    qh = jnp.where(mine, q2, 0.0).astype(bf16)
                    doh = jnp.where(mine, do2, 0.0).astype(bf16)
                    s = lax.dot_general(qh, k2, (((1,), (1,)), ((), ())), preferred_element_type=f32)
                    s = s + bias_scr[_bias_index(b, case, hh), 0:nq, 0:nk]
                    p = jnp.exp(s - lse2[:, lo:lo + 1])
                    dv2 = dv2 + lax.dot_general(p.astype(bf16), doh, (((0,), (0,)), ((), ())), preferred_element_type=f32)
                    dp = lax.dot_general(doh, v2, (((1,), (1,)), ((), ())), preferred_element_type=f32)
                    ds = (p * (dp - dl2[:, lo:lo + 1])).astype(bf16)
                    dq2 = dq2 + jnp.where(mine, jnp.dot(ds, k2, preferred_element_type=f32), 0.0)
                    dk2 = dk2 + lax.dot_general(ds, qh, (((0,), (0,)), ((), ())), preferred_element_type=f32)
                dq_scr[qrows, :] = dq_scr[qrows, :] + dq2
                dk_scr[krows, :] = dk_scr[krows, :] + dk2
                dv_scr[krows, :] = dv_scr[krows, :] + dv2
                return carry

            lax.fori_loop(0, S // nq, step, 0, unroll=min(ATTN_UNROLL, S // nq))

        n4 = S // QUAD
        for t in range(3):
            for rho in range(QUAD):
                token_rows = pl.ds(rho, n4, stride=QUAD)
                acc_scr[0, t, token_rows, :] = acc_scr[0, t, token_rows, :] + acc_scr[1, t, pl.ds(rho * n4, n4), :]
        dqkv_ref[0] = (acc_scr[0, 0] * SCALE).astype(bf16)
        dqkv_ref[1] = acc_scr[0, 1].astype(bf16)
        dqkv_ref[2] = acc_scr[0, 2].astype(bf16)

    blk = lambda part: pl.BlockSpec((S, PAIR), lambda p: (0, part * npair + p))
    own = pl.BlockSpec((S, PAIR), lambda p: (0, p))
    return pl.pallas_call(
        body, name="attn_bwd", out_shape=jax.ShapeDtypeStruct(dqkvz.shape, dqkvz.dtype), grid=(npair,),
        in_specs=[blk(4), blk(5), blk(6), own, own, own, pl.BlockSpec((None, 8, PAIR), lambda p: (p, 0, 0)),
                  ANY_SPEC, ANY_SPEC],
        out_specs=pl.BlockSpec((3, S, PAIR), lambda p: (0, 0, p)), input_output_aliases={7: 0},
        scratch_shapes=[pltpu.VMEM((2, 3, S, PAIR), f32), pltpu.VMEM((S, PAIR), f32), pltpu.VMEM((6, S, PAIR), f32),
                        pltpu.VMEM(_bias_shape(S), f32)],
        compiler_params=_params("arbitrary"),
    )(proj, proj, proj, o, do, lse, slopes, dqkvz, dep)


def _prenorm_bwd(dh, x, dout, scale, g_pre, tr=64):
    S, D = x.shape
    tr = min(tr, S)

    def body(dh_ref, x_ref, dout_ref, sc_ref, g_ref, gx_ref, sums_ref):
        i = pl.program_id(0)
        xv, dhv = x_ref[...], dh_ref[...]
        r = lax.rsqrt(jnp.mean(xv * xv, axis=-1, keepdims=True) + EPS)
        xn = xv * r
        dxn = dhv * (g_ref[...] * (1.0 + sc_ref[...]))
        gx_ref[...] = dout_ref[...] + r * (dxn - xn * jnp.mean(dxn * xn, axis=-1, keepdims=True))
        dhx = dhv * xn
        row = lax.broadcasted_iota(jnp.int32, (8, D), 0)
        upd = jnp.where(row == 0, jnp.sum(dhv, axis=0, keepdims=True),
                        jnp.where(row == 1, jnp.sum(dhx, axis=0, keepdims=True) * g_ref[...],
                                  jnp.where(row == 2, jnp.sum(dhx, axis=0, keepdims=True) * (1.0 + sc_ref[...]), 0.0)))

        @pl.when(i == 0)
        def _():
            sums_ref[...] = upd

        @pl.when(i > 0)
        def _():
            sums_ref[...] += upd

    row = pl.BlockSpec((tr, D), lambda i: (i, 0))
    vec = pl.BlockSpec((1, D), lambda i: (0, 0))
    return pl.pallas_call(
        body, name="prenorm_bwd",
        out_shape=(jax.ShapeDtypeStruct((S, D), f32), jax.ShapeDtypeStruct((8, D), f32)), grid=(S // tr,),
        in_specs=[row, row, row, vec, vec], out_specs=(row, pl.BlockSpec((8, D), lambda i: (0, 0))),
        compiler_params=_params("arbitrary"),
    )(dh, x, dout, scale, g_pre)


def _adamw(w, g, m, v):
    m = ADAM_B1 * m + (1.0 - ADAM_B1) * g
    v = ADAM_B2 * v + (1.0 - ADAM_B2) * (g * g)
    m_hat = m / (1.0 - ADAM_B1 ** ADAM_STEP)
    v_hat = v / (1.0 - ADAM_B2 ** ADAM_STEP)
    delta = -ADAM_LR * (m_hat / (jnp.sqrt(v_hat) + ADAM_EPS) + ADAM_WD * w)
    return delta, m, v


def _sum_rows(parts, dep):
    P = parts.shape[1]

    def body(p_ref, dep_ref, o_ref):
        acc = p_ref[0:1, :]
        for j in range(1, NDEV):
            acc = acc + p_ref[j:j + 1, :]
        o_ref[...] = jnp.broadcast_to(acc, (8, P))

    vmem = pl.BlockSpec(memory_space=pltpu.VMEM)
    return pl.pallas_call(body, name="sum_small", out_shape=jax.ShapeDtypeStruct((8, P), f32),
                          in_specs=[vmem, ANY_SPEC], out_specs=vmem, compiler_params=_params())(parts, dep)


def _adamw_small(tot, params):
    given = [p[3] for p in params if not isinstance(p[3], int)]

    def body(tot_ref, *refs):
        given_refs = list(refs[:len(given)])
        ins = refs[len(given):len(given) + 3 * len(params)]
        outs = refs[len(given) + 3 * len(params):]
        for t, (w, _, _, where) in enumerate(params):
            w_ref, m_ref, v_ref = ins[3 * t:3 * t + 3]
            g = tot_ref[0:1, where:where + w.size] if isinstance(where, int) else given_refs.pop(0)[...]
            outs[4 * t][...] = g
            outs[4 * t + 1][...], outs[4 * t + 2][...], outs[4 * t + 3][...] = _adamw(w_ref[...], g, m_ref[...], v_ref[...])

    out_shape = tuple(jax.ShapeDtypeStruct(p[0].shape, f32) for p in params for _ in range(4))
    res = pl.pallas_call(body, name="adamw_small", out_shape=out_shape, compiler_params=_params())(
        tot, *given, *[a for p in params for a in p[:3]])
    return [res[4 * t:4 * t + 4] for t in range(len(params))]


def _adamw_sharded(name, parts, sums_a, sums_b, pick, w, m, v, rows=None, prev=None, tr=128):
    R, Cc = w.shape
    r0, nr = rows or (0, R)
    tr = math.gcd(tr, r0, nr)
    n, b0 = parts.shape[0], r0 // tr

    def body(pick_ref, p_ref, a_ref, b_ref, w_ref, m_ref, v_ref, *rest):
        g_ref, d_ref, nm_ref, nv_ref = rest[-4:]
        g = jnp.where(pick_ref[0] == 1, b_ref[...], a_ref[...]).astype(f32)
        for j in range(n):
            g = g + p_ref[j].astype(f32)
        g_ref[...] = g
        d_ref[...], nm_ref[...], nv_ref[...] = _adamw(w_ref[...], g, m_ref[...], v_ref[...])

    row = pl.BlockSpec((tr, Cc), lambda i, pick: (i + b0, 0))
    mine = pl.BlockSpec((None, tr, Cc), lambda i, pick: (pick[1], i + b0, 0))
    out = jax.ShapeDtypeStruct((R, Cc), f32)
    prev = list(prev or [])
    grid_spec = pltpu.PrefetchScalarGridSpec(
        num_scalar_prefetch=1, grid=(nr // tr,),
        in_specs=[pl.BlockSpec((n, tr, Cc), lambda i, pick: (0, i + b0, 0)), mine, mine, row, row, row]
        + [ANY_SPEC] * len(prev),
        out_specs=(row, row, row, row))
    return pl.pallas_call(
        body, name=name, out_shape=(out, out, out, out), grid_spec=grid_spec,
        input_output_aliases={7 + t: t for t in range(len(prev))}, compiler_params=_params("arbitrary"),
    )(pick, parts, sums_a, sums_b, w, m, v, *prev)


def _adamw_ada(c_t, dmod_cols, w, m, v, dep, tr=256):
    D, W = w.shape
    tr = min(tr, D)

    def body(c_ref, dm_ref, w_ref, m_ref, v_ref, dep_ref, g_ref, d_ref, nm_ref, nv_ref):
        cv, dm = c_ref[...], dm_ref[...]
        g = cv[:, 0:1] * dm[0:1, :]
        for b in range(1, NDEV):
            g = g + cv[:, b:b + 1] * dm[b:b + 1, :]
        g_ref[...] = g
        d_ref[...], nm_ref[...], nv_ref[...] = _adamw(w_ref[...], g, m_ref[...], v_ref[...])

    row = pl.BlockSpec((tr, W), lambda i: (i, 0))
    out = jax.ShapeDtypeStruct((D, W), f32)
    return pl.pallas_call(
        body, name="adamw_ada", out_shape=(out, out, out, out), grid=(D // tr,),
        in_specs=[pl.BlockSpec((tr, NDEV), lambda i: (i, 0)), pl.BlockSpec((NDEV, W), lambda i: (0, 0)), row, row, row,
                  ANY_SPEC],
        out_specs=(row, row, row, row), compiler_params=_params("parallel"),
    )(c_t, dmod_cols, w, m, v, dep)


def kernel(x, c, w_ada, b_ada, g_pre, w_in, conv_w, conv_b, g_conv, g_attn, w_out, g_post, loss_target, m_w_ada, m_b_ada, m_g_pre, m_w_in, m_conv_w, m_conv_b, m_g_conv, m_g_attn, m_w_out, m_g_post, v_w_ada, v_b_ada, v_g_pre, v_w_in, v_conv_w, v_conv_b, v_g_conv, v_g_attn, v_w_out, v_g_post):
    S, D = x.shape[1], x.shape[2]
    C = D // 2
    W = w_ada.shape[2]
    CW = conv_w.shape[2]
    me = 4 * lax.axis_index("x") + 2 * lax.axis_index("y") + lax.axis_index("c")
    x2, tgt = x[0], loss_target[0]
    w_ada2, w_in2, w_out2 = w_ada[0], w_in[0], w_out[0]

    R = D // NDEV
    core = lax.axis_index("c").astype(jnp.int32).reshape(1)

    cw_slab = jnp.zeros((8, CW), f32).at[:3].set(conv_w[0])
    b_cols = lax.dynamic_slice_in_dim(b_ada, me * W, W, axis=1)
    mod_slabs, c_blocks, cw_g = _ada_exchange(c.reshape(D // 128, 128), cw_slab, w_ada2, b_cols)
    c_all = c_blocks.reshape(NDEV, D)
    conv_w_full = jnp.transpose(cw_g, (1, 0, 2)).reshape(8, C)
    mod = mod_slabs[:, 0, :].reshape(1, 3 * D)
    shift, scale, gate = mod[:, :D], mod[:, D:2 * D], mod[:, 2 * D:]

    land_i = lax.dynamic_update_slice(lax.empty((NDEV, D, C), bf16), w_in2.astype(bf16)[None], (me, 0, 0))
    land_o = lax.dynamic_update_slice(lax.empty((NDEV, R, D), bf16), w_out2.astype(bf16)[None], (me, 0, 0))
    wi_send, wi_recv, land_i, w_token = _w_in_start(land_i, [mod_slabs])

    me_arr = me.astype(jnp.int32).reshape(1)
    h = _prenorm(x2, scale, shift, g_pre, w_token)
    land_i = _w_in_sibling(land_i, wi_recv, after=[h])
    proj = _in_proj_part("in_proj_a", h, land_i, None, me_arr, 0, 1, 2)
    fi_send, fi_recv, land_i = _w_in_relay(land_i, wi_recv, after=[proj])
    proj = _in_proj_part("in_proj_b", h, land_i, proj, me_arr, 2, 2, 2)
    land_i = _w_in_forwarded(land_i, fi_recv, after=[proj])
    proj = _in_proj_part("in_proj_c", h, land_i, proj, me_arr, 3, 2, 2)
    (di_send, di_recv, wo_send, wo_recv), land_i, land_o = _w_in_diag(land_i, land_o, fi_recv, after=[proj])
    proj = _in_proj_part("in_proj_d", h, land_i, proj, me_arr, 6, 1, 1)
    win_g = _w_in_finish(land_i, wi_send, fi_send, di_send, di_recv, after=[proj])
    proj = _in_proj_part("in_proj_e", h, win_g, proj, me_arr, 7, 1, 1)
    (fo_send, fo_recv), (land_o,), _ = _weights_forward("w_out_forward", land_o, wo_recv, after=[proj])
    slopes = _head_slopes(C // HEAD_DIM)
    ycat = _conv_fwd(proj, conv_w_full, conv_b, g_conv)
    o, lse = _attn_fwd(proj, slopes)
    ycat = _attn_post(ycat, o, proj, g_attn)
    wout_g = _weights_wait("w_out_wait", land_o, wo_send, wo_recv, fo_send, fo_recv, after=[ycat])
    wout_full = wout_g.reshape(D, D)
    y = _matmul(ycat, wout_full, name="out_proj", out_dtype=f32)
    dy, dout, post_sums = _sandwich(y, x2, tgt, gate, g_post)

    chip = me // 2

    def landing(rows, cols):
        return lax.dynamic_update_slice(lax.empty((NCHIP, rows, cols), bf16), jnp.zeros((1, rows, cols), bf16),
                                        (chip, 0, 0))

    gw_out = _matmul(ycat, dy, name="out_proj_dw", out_dtype=bf16, ta=True).reshape(NDEV, R, D)
    po_send, po_recv, gw_out, pair_o, po_token = _pair_start("g_out_pair_start", gw_out)
    dycat = _matmul(dy, wout_full, name="out_proj_dx", out_dtype=f32, tb=True, dep=po_token)
    gw_out, pair_o = _pair_wait("g_out_pair_wait", gw_out, pair_o, po_send, po_recv, after=[dycat])
    sum_o = _pair_sum("g_out_pair_sum", gw_out, pair_o, core)
    co_send, co_recv, sum_o, land_go, co_token = _chip_start(
        "g_out_chip_start", sum_o, landing(R, D), 0)
    dpc, conv_sums = _conv_bwd(proj, dycat, conv_w_full, conv_b, g_conv, co_token)
    gw_c = _matmul(h, dpc, name="in_proj_dw_conv", out_dtype=bf16, ta=True, out_slots=4)
    pc_send, pc_recv, gw_c, pair_c, pc_token = _pair_start("g_conv_pair_start", gw_c)
    do, dpa, attn_sums = _attn_post_bwd(o, proj, dycat, g_attn, pc_token)
    gw_c, pair_c = _pair_wait("g_conv_pair_wait", gw_c, pair_c, pc_send, pc_recv, after=[do])
    sum_c = _pair_sum("g_conv_pair_sum", gw_c, pair_c, core)
    cc_send, cc_recv, sum_c, land_gi, cc_token = _chip_start(
        "g_conv_chip_start", sum_c, landing(D, C), 0)
    dpa = _attn_bwd(proj, o, do, lse, slopes, dpa, cc_token)
    gw_a = _matmul(h, dpa, name="in_proj_dw_attn", out_dtype=bf16, ta=True, b_slots=True, out_slots=4)
    pa_send, pa_recv, gw_a, pair_a, pa_token = _pair_start("g_attn_pair_start", gw_a)
    gw_a, pair_a = _pair_wait("g_attn_pair_wait", gw_a, pair_a, pa_send, pa_recv, after=[pa_token])
    sum_a = _pair_sum("g_attn_pair_sum", gw_a, pair_a, core)
    part_a, part_b = (0, 3 * D // 4), (3 * D // 4, D // 4)
    ca_send, ca_recv, sum_a, land_gi, ca_token = _chip_start("g_attn_chip_start_a", sum_a, land_gi, 4, part_a)
    dh = _matmul_slabs_t(dpc, dpa, win_g, name="in_proj_dx", dep=ca_token)
    grad_x, pre_sums = _prenorm_bwd(dh, x2, dout, scale, g_pre)

    small = jnp.concatenate([pre_sums[0:1], pre_sums[1:2], post_sums[0:1],
                             pre_sums[2:3], post_sums[1:2],
                             conv_sums[2:3], conv_sums[3:4], conv_sums[4:5],
                             conv_sums[1:2], conv_sums[0:1], attn_sums[0:1]], axis=1)
    small = jnp.concatenate([small.reshape(8 * D // 128, 128), jnp.broadcast_to(post_sums[2:3, :128], (8, 128))])
    (small_all,) = _all_gather([small], "gather_small")
    cb_send, cb_recv, sum_a, land_gi, cb_token = _chip_start("g_attn_chip_start_b", sum_a, land_gi, 4, part_b,
                                                             after=[small_all])
    small_all = small_all.reshape(NDEV, small.size)
    tot = _sum_rows(small_all, cb_token)
    loss = tot[0, 8 * D]
    g_conv_w = lax.dynamic_slice_in_dim(tot[0:1, 5 * D:5 * D + 3 * C].reshape(1, 3, C), me * CW, CW, axis=2)
    ((g_b_ada, d_b_ada, nm_b_ada, nv_b_ada), (g_g_pre, d_g_pre, nm_g_pre, nv_g_pre),
     (g_g_post, d_g_post, nm_g_post, nv_g_post), (g_conv_w, d_conv_w, nm_conv_w, nv_conv_w),
     (g_conv_b, d_conv_b, nm_conv_b, nv_conv_b), (g_g_conv, d_g_conv, nm_g_conv, nv_g_conv),
     (g_g_attn, d_g_attn, nm_g_attn, nv_g_attn)) = _adamw_small(tot, [
         (b_ada, m_b_ada, v_b_ada, 0), (g_pre, m_g_pre, v_g_pre, 3 * D), (g_post, m_g_post, v_g_post, 4 * D),
         (conv_w, m_conv_w, v_conv_w, g_conv_w), (conv_b, m_conv_b, v_conv_b, 5 * D + 3 * C),
         (g_conv, m_g_conv, v_g_conv, 5 * D + 4 * C), (g_attn, m_g_attn, v_g_attn, 5 * D + 5 * C)])

    dmod_cols = lax.dynamic_slice_in_dim(small_all[:, :3 * D], me * W, W, axis=1)
    g_w_ada, d_w_ada, nm_w_ada, nv_w_ada = _adamw_ada(c_all.T, dmod_cols, w_ada2, m_w_ada[0], v_w_ada[0], cb_token)

    sum_o, land_go = _chip_wait("g_out_chip_wait", sum_o, land_go, co_send, co_recv, 0, after=[g_w_ada])
    pick_out = jnp.stack([jnp.int32(0), me // 2]).astype(jnp.int32)
    g_w_out, d_w_out, nm_w_out, nv_w_out = _adamw_sharded(
        "adamw_w_out", land_go, sum_o, sum_o, pick_out, w_out2, m_w_out[0], v_w_out[0])
    pick_in = jnp.stack([me // 4, (me % 4) // 2]).astype(jnp.int32)
    sum_c, land_gi = _chip_wait("g_conv_chip_wait", sum_c, land_gi, cc_send, cc_recv, 0, after=[g_w_out])
    sum_a, land_gi = _chip_wait("g_attn_chip_wait_a", sum_a, land_gi, ca_send, ca_recv, 4, [g_w_out], part_a)
    first = _adamw_sharded("adamw_w_in_a", land_gi, sum_c, sum_a, pick_in, w_in2, m_w_in[0], v_w_in[0], rows=part_a)
    sum_a, land_gi = _chip_wait("g_attn_chip_wait_b", sum_a, land_gi, cb_send, cb_recv, 4, [first[0]], part_b)
    g_w_in, d_w_in, nm_w_in, nv_w_in = _adamw_sharded(
        "adamw_w_in_b", land_gi, sum_c, sum_a, pick_in, w_in2, m_w_in[0], v_w_in[0], rows=part_b, prev=first)

    return (loss, grad_x[None],
            g_w_ada[None], g_b_ada, g_g_pre, g_w_in[None], g_conv_w, g_conv_b, g_g_conv, g_g_attn, g_w_out[None], g_g_post,
            d_w_ada[None], d_b_ada, d_g_pre, d_w_in[None], d_conv_w, d_conv_b, d_g_conv, d_g_attn, d_w_out[None], d_g_post,
            nm_w_ada[None], nm_b_ada, nm_g_pre, nm_w_in[None], nm_conv_w, nm_conv_b, nm_g_conv, nm_g_attn, nm_w_out[None], nm_g_post,
            nv_w_ada[None], nv_b_ada, nv_g_pre, nv_w_in[None], nv_conv_w, nv_conv_b, nv_g_conv, nv_g_attn, nv_w_out[None], nv_g_post)
```

```python
import functools
import math

import jax
import jax.numpy as jnp
from jax import lax
from jax.experimental import pallas as pl
from jax.experimental.pallas import tpu as pltpu

f32 = jnp.float32
bf16 = jnp.bfloat16

NDEV = 8
HEAD_DIM = 64
PAIR = 2 * HEAD_DIM
BRANCHES = ((128, 1, 1), (512, 4, 1), (2048, 16, 2))
HALF_WIN = 64
EPS = 1e-6
NEG_INF = -1e30
ADAM_LR, ADAM_B1, ADAM_B2, ADAM_EPS, ADAM_WD, ADAM_STEP = 0.001, 0.9, 0.999, 1e-08, 0.01, 10
MESH = pl.DeviceIdType.MESH
VMEM_LIMIT = 56 * 1024 * 1024
HBM_SPEC = pl.BlockSpec(memory_space=pltpu.HBM)
ANY_SPEC = pl.BlockSpec(memory_space=pl.ANY)
SEM_SPEC = pl.BlockSpec(memory_space=pltpu.SEMAPHORE)


def _params(*sem):
    return pltpu.CompilerParams(dimension_semantics=sem or None, vmem_limit_bytes=VMEM_LIMIT)


def _silu(z):
    return z * jax.nn.sigmoid(z)


def _dsilu(z):
    s = jax.nn.sigmoid(z)
    return s * (1.0 + z * (1.0 - s))


def _my_place():
    x, y, c = lax.axis_index("x"), lax.axis_index("y"), lax.axis_index("c")
    return x, y, c, 4 * x + 2 * y + c


def _peer(x, y, c, k):
    px, py, pc = x ^ (k >> 2 & 1), y ^ (k >> 1 & 1), c ^ (k & 1)
    return (px, py, pc), 4 * px + 2 * py + pc


def _all_gather(arrays, name):
    n = len(arrays)

    def body(*refs):
        srcs, dsts = refs[:n], refs[n:2 * n]
        send_sems, recv_sems, local_sems = refs[2 * n:]
        x, y, c, me = _my_place()
        locals_, sends = [], []
        for t in range(n):
            own = pltpu.make_async_copy(srcs[t], dsts[t].at[me], local_sems.at[t])
            own.start()
            locals_.append(own)
            for k in range(1, NDEV):
                peer, pidx = _peer(x, y, c, k)
                cp = pltpu.make_async_remote_copy(
                    src_ref=srcs[t], dst_ref=dsts[t].at[me], send_sem=send_sems.at[t, k],
                    recv_sem=recv_sems.at[t, k], device_id=peer, device_id_type=MESH)
                cp.start()
                sends.append(cp)
        for t in range(n):
            for k in range(1, NDEV):
                peer, pidx = _peer(x, y, c, k)
                pltpu.make_async_remote_copy(
                    src_ref=srcs[t], dst_ref=dsts[t].at[pidx], send_sem=send_sems.at[t, k],
                    recv_sem=recv_sems.at[t, k], device_id=peer, device_id_type=MESH).wait_recv()
        for cp in sends:
            cp.wait_send()
        for cp in locals_:
            cp.wait()

    return pl.pallas_call(
        body, name=name,
        out_shape=tuple(jax.ShapeDtypeStruct((NDEV,) + a.shape, a.dtype) for a in arrays),
        in_specs=[HBM_SPEC] * n, out_specs=tuple([HBM_SPEC] * n),
        scratch_shapes=[pltpu.SemaphoreType.DMA((n, NDEV)), pltpu.SemaphoreType.DMA((n, NDEV)),
                        pltpu.SemaphoreType.DMA((n,))],
    )(*arrays)


def _comm_call(name, arrays, sems, new_sems, body, after=(), token=False):
    na, ns, nn, nf = len(arrays), len(sems), len(new_sems), len(after)

    def kern(*refs):
        ins, outs = refs[:na + ns + nf], refs[na + ns + nf:]
        body(ins[:na], ins[na:na + ns], outs[:nn])
        if token:
            outs[nn + na][...] = jnp.zeros((8, 128), f32)

    out_shape = ([pltpu.SemaphoreType.DMA(s) for s in new_sems] + [pltpu.HBM(a.shape, a.dtype) for a in arrays]
                 + ([jax.ShapeDtypeStruct((8, 128), f32)] if token else []))
    out_specs = [SEM_SPEC] * nn + [HBM_SPEC] * na + ([pl.BlockSpec(memory_space=pltpu.VMEM)] if token else [])
    res = pl.pallas_call(
        kern, name=name, out_shape=tuple(out_shape),
        in_specs=[HBM_SPEC] * na + [SEM_SPEC] * ns + [ANY_SPEC] * nf, out_specs=tuple(out_specs),
        input_output_aliases={t: nn + t for t in range(na)},
        compiler_params=pltpu.CompilerParams(has_side_effects=pltpu.SideEffectType.DATAFLOW_SIDE_EFFECTING),
    )(*[pltpu.with_memory_space_constraint(a, pltpu.HBM) for a in arrays], *sems, *after)
    return list(res[:nn]), list(res[nn:nn + na]), (res[nn + na] if token else None)


def _remote(src, dst, send_sem, recv_sem, device):
    return pltpu.make_async_remote_copy(src_ref=src, dst_ref=dst, send_sem=send_sem, recv_sem=recv_sem,
                                        device_id=device, device_id_type=MESH)


SAME_CORE = (2, 4, 6)
VIA_SIBLING = (3, 5, 7)


def _weights_forward(name, land, recv, after):
    def body(a, s, new):
        (land,), (recv,), (fsend, frecv) = a, s, new
        x, y, c, me = _my_place()
        sibling, _ = _peer(x, y, c, 1)
        for k in SAME_CORE:
            peer, slot = _peer(x, y, c, k)
            _remote(land.at[slot], land.at[slot], fsend.at[k], recv.at[k], peer).wait_recv()
            _remote(land.at[slot], land.at[slot], fsend.at[k], frecv.at[k ^ 1], sibling).start()

    return _comm_call(name, [land], [recv], [(NDEV,), (NDEV,)], body, after=after)


def _weights_wait(name, land, send, recv, fsend, frecv, after):
    def body(a, s, new):
        (land,), (send, recv, fsend, frecv) = a, s
        x, y, c, me = _my_place()
        sibling, sib_slot = _peer(x, y, c, 1)
        _remote(land.at[sib_slot], land.at[sib_slot], send.at[1], recv.at[1], sibling).wait_recv()
        for k in VIA_SIBLING:
            _, slot = _peer(x, y, c, k)
            _remote(land.at[slot], land.at[slot], fsend.at[k ^ 1], frecv.at[k], sibling).wait_recv()
        for k in (1,) + SAME_CORE:
            peer, _ = _peer(x, y, c, k)
            _remote(land.at[me], land.at[me], send.at[k], recv.at[k], peer).wait_send()
        for k in SAME_CORE:
            _, slot = _peer(x, y, c, k)
            _remote(land.at[slot], land.at[slot], fsend.at[k], frecv.at[k ^ 1], sibling).wait_send()

    return _comm_call(name, [land], [send, recv, fsend, frecv], [], body, after=after)[1][0]


def _diag_relay(x, y, c):
    slot = 4 * (x ^ (1 - c)) + 2 * (y ^ c) + c
    return slot, (x ^ c, y ^ (1 - c), c)


def _w_in_start(land, after):
    def body(a, s, new):
        (land,), (send, recv) = a, new
        x, y, c, me = _my_place()
        for k in (1, 2, 4):
            peer, _ = _peer(x, y, c, k)
            _remote(land.at[me], land.at[me], send.at[k], recv.at[k], peer).start()

    (send, recv), (land,), token = _comm_call("w_in_start", [land], [], [(NDEV,), (NDEV,)], body, after=after, token=True)
    return send, recv, land, token


def _w_in_sibling(land, recv, after):
    def body(a, s, new):
        (land,), (recv,) = a, s
        x, y, c, me = _my_place()
        sibling, slot = _peer(x, y, c, 1)
        _remote(land.at[slot], land.at[slot], recv.at[1], recv.at[1], sibling).wait_recv()

    return _comm_call("w_in_sibling", [land], [recv], [], body, after=after)[1][0]


def _w_in_relay(land, recv, after):
    def body(a, s, new):
        (land,), (recv,), (fsend, frecv) = a, s, new
        x, y, c, me = _my_place()
        sibling, _ = _peer(x, y, c, 1)
        for k in (2, 4):
            peer, slot = _peer(x, y, c, k)
            _remote(land.at[slot], land.at[slot], fsend.at[k], recv.at[k], peer).wait_recv()
        slot, target = _diag_relay(x, y, c)
        _remote(land.at[slot], land.at[slot], fsend.at[6], frecv.at[6], target).start()
        for k in (2, 4):
            _, slot = _peer(x, y, c, k)
            _remote(land.at[slot], land.at[slot], fsend.at[k], frecv.at[k ^ 1], sibling).start()

    (fsend, frecv), (land,), _ = _comm_call("w_in_relay", [land], [recv], [(NDEV,), (NDEV,)], body, after=after)
    return fsend, frecv, land


def _w_in_forwarded(land, frecv, after):
    def body(a, s, new):
        (land,), (frecv,) = a, s
        x, y, c, me = _my_place()
        sibling, _ = _peer(x, y, c, 1)
        for k in (3, 5):
            _, slot = _peer(x, y, c, k)
            _remote(land.at[slot], land.at[slot], frecv.at[k], frecv.at[k], sibling).wait_recv()

    return _comm_call("w_in_forwarded", [land], [frecv], [], body, after=after)[1][0]


def _w_in_diag(land, land_o, frecv, after):
    def body(a, s, new):
        (land, land_o), (frecv,), (dsend, drecv, osend, orecv) = a, s, new
        x, y, c, me = _my_place()
        sibling, _ = _peer(x, y, c, 1)
        peer, slot = _peer(x, y, c, 6)
        _remote(land.at[slot], land.at[slot], dsend.at[6], frecv.at[6], peer).wait_recv()
        _remote(land.at[slot], land.at[slot], dsend.at[6], drecv.at[7], sibling).start()
        for k in (1,) + SAME_CORE:
            peer, _ = _peer(x, y, c, k)
            _remote(land_o.at[me], land_o.at[me], osend.at[k], orecv.at[k], peer).start()

    sems, (land, land_o), _ = _comm_call("w_in_diag", [land, land_o], [frecv], [(NDEV,)] * 4, body, after=after)
    return sems, land, land_o


def _w_in_finish(land, send, fsend, dsend, drecv, after):
    def body(a, s, new):
        (land,), (send, fsend, dsend, drecv) = a, s
        x, y, c, me = _my_place()
        sibling, _ = _peer(x, y, c, 1)
        _, slot = _peer(x, y, c, 7)
        _remote(land.at[slot], land.at[slot], dsend.at[6], drecv.at[7], sibling).wait_recv()
        for k in (1, 2, 4):
            peer, _ = _peer(x, y, c, k)
            _remote(land.at[me], land.at[me], send.at[k], send.at[k], peer).wait_send()
        for k in (2, 4, 6):
            _, slot = _peer(x, y, c, k)
            _remote(land.at[slot], land.at[slot], fsend.at[k], fsend.at[k], sibling).wait_send()
        _, slot = _peer(x, y, c, 6)
        _remote(land.at[slot], land.at[slot], dsend.at[6], dsend.at[6], sibling).wait_send()

    return _comm_call("w_in_finish", [land], [send, fsend, dsend, drecv], [], body, after=after)[1][0]


def _in_proj_part(name, h, land, proj, me_arr, k0, kstep, nk, tm=512):
    S, D = h.shape
    C = land.shape[2]
    tm = min(tm, S)

    def body(me_ref, a_ref, b_ref, *rest):
        rest[-1][...] = jnp.dot(a_ref[...], b_ref[...], preferred_element_type=f32)

    slot = lambda j, me: me[0] ^ (k0 + kstep * j)
    args = [h, land] + ([] if proj is None else [proj])
    grid_spec = pltpu.PrefetchScalarGridSpec(
        num_scalar_prefetch=1, grid=(nk, S // tm),
        in_specs=[pl.BlockSpec((tm, D), lambda j, i, me: (i, 0)),
                  pl.BlockSpec((None, D, C), lambda j, i, me: (slot(j, me), 0, 0))] + [ANY_SPEC] * (len(args) - 2),
        out_specs=pl.BlockSpec((tm, C), lambda j, i, me: (i, slot(j, me))))
    return pl.pallas_call(
        body, name=name, out_shape=jax.ShapeDtypeStruct((S, NDEV * C), f32), grid_spec=grid_spec,
        input_output_aliases={} if proj is None else {3: 0}, compiler_params=_params("arbitrary", "arbitrary"),
    )(me_arr, *args)


NCHIP = NDEV // 2


def _pair_start(name, src):
    npair = src.shape[0] // 2

    def body(a, s, new):
        (src, pair), (send, recv) = a, new
        x, y, c, me = _my_place()
        sibling, _ = _peer(x, y, c, 1)
        for i in range(npair):
            _remote(src.at[2 * i + 1 - c], pair.at[i], send.at[i], recv.at[i], sibling).start()

    pair = lax.empty((npair,) + src.shape[1:], src.dtype)
    (send, recv), (src, pair), token = _comm_call(name, [src, pair], [], [(npair,), (npair,)], body, token=True)
    return send, recv, src, pair, token


def _pair_wait(name, src, pair, send, recv, after):
    npair = pair.shape[0]

    def body(a, s, new):
        (src, pair), (send, recv) = a, s
        x, y, c, me = _my_place()
        sibling, _ = _peer(x, y, c, 1)
        for i in range(npair):
            cp = _remote(src.at[2 * i + 1 - c], pair.at[i], send.at[i], recv.at[i], sibling)
            cp.wait_recv()
            cp.wait_send()

    return _comm_call(name, [src, pair], [send, recv], [], body, after=after)[1]


def _pair_sum(name, src, pair, core, tr=256):
    npair, R, Cc = pair.shape
    tr = min(tr, R)

    def body(core_ref, a_ref, b_ref, o_ref):
        o_ref[...] = (a_ref[...].astype(f32) + b_ref[...].astype(f32)).astype(o_ref.dtype)

    grid_spec = pltpu.PrefetchScalarGridSpec(
        num_scalar_prefetch=1, grid=(npair, R // tr),
        in_specs=[pl.BlockSpec((None, tr, Cc), lambda i, r, core: (2 * i + core[0], r, 0)),
                  pl.BlockSpec((None, tr, Cc), lambda i, r, core: (i, r, 0))],
        out_specs=pl.BlockSpec((None, tr, Cc), lambda i, r, core: (i, r, 0)))
    return pl.pallas_call(body, name=name, out_shape=jax.ShapeDtypeStruct(pair.shape, pair.dtype),
                          grid_spec=grid_spec, compiler_params=_params("parallel", "parallel"))(core, src, pair)


def _owner_chip(first, i):
    q = first // 2 + i
    return q >> 1 & 1, q & 1


def _chip_start(name, sums, land, first, rows=None, after=()):
    npair = sums.shape[0]
    rows = pl.ds(*(rows or (0, sums.shape[1])))

    def body(a, s, new):
        (sums, land), (send, recv) = a, new
        x, y, c, me = _my_place()
        for i in range(npair):
            ox, oy = _owner_chip(first, i)

            @pl.when((x != ox) | (y != oy))
            def _():
                _remote(sums.at[i, rows], land.at[2 * x + y, rows], send.at[i], recv.at[2 * x + y], (ox, oy, c)).start()

    (send, recv), (sums, land), token = _comm_call(name, [sums, land], [], [(npair,), (NCHIP,)], body, after=after,
                                                   token=True)
    return send, recv, sums, land, token


def _chip_wait(name, sums, land, send, recv, first, after, rows=None):
    npair = sums.shape[0]
    rows = pl.ds(*(rows or (0, sums.shape[1])))

    def body(a, s, new):
        (sums, land), (send, recv) = a, s
        x, y, c, me = _my_place()
        mine = (me >= first) & (me < first + 2 * npair)
        for i in range(npair):
            ox, oy = _owner_chip(first, i)

            @pl.when((x != ox) | (y != oy))
            def _():
                _remote(sums.at[i, rows], land.at[2 * x + y, rows], send.at[i], recv.at[2 * x + y], (ox, oy, c)).wait_send()
        for q in range(NCHIP):
            @pl.when(mine & (2 * x + y != q))
            def _():
                _remote(sums.at[0, rows], land.at[q, rows], send.at[0], recv.at[q], (q >> 1, q & 1, c)).wait_recv()

    return _comm_call(name, [sums, land], [send, recv], [], body, after=after)[1]


def _matmul(a, b, *, name, out_dtype, ta=False, tb=False, b_slots=False, out_slots=0, b_cols=None,
            tm=1024, tn=1024, tk=2048, dep=None):
    M, K = (a.shape[1], a.shape[0]) if ta else a.shape
    col0 = 0
    if b_slots:
        slab = b.shape[2]
        N = b.shape[1] if tb else b.shape[0] * slab
        assert (K if tb else N) == b.shape[0] * slab
    elif b_cols is not None:
        assert not tb
        col0, N = b_cols
    else:
        N = b.shape[0] if tb else b.shape[1]
    tm, tn, tk = min(tm, M), min(tn, N), min(tk, K)
    if b_slots:
        if tb:
            tk = min(tk, slab)
        else:
            tn = min(tn, slab)
    if out_slots:
        tn = min(tn, N // out_slots)
    nm, nn, nk = M // tm, N // tn, K // tk
    assert (nm * tm, nn * tn, nk * tk) == (M, N, K) and col0 % tn == 0, (name, M, N, K, tm, tn, tk)
    j0 = col0 // tn

    a_spec = pl.BlockSpec((tk, tm), lambda i, j, k: (k, i)) if ta else pl.BlockSpec((tm, tk), lambda i, j, k: (i, k))
    if b_slots and tb:
        per = slab // tk
        b_spec = pl.BlockSpec((None, tn, tk), lambda i, j, k: (k // per, j, k % per))
    elif b_slots:
        per = slab // tn
        b_spec = pl.BlockSpec((None, tk, tn), lambda i, j, k: (j // per, k, j % per))
    elif tb:
        b_spec = pl.BlockSpec((tn, tk), lambda i, j, k: (j, k))
    else:
        b_spec = pl.BlockSpec((tk, tn), lambda i, j, k: (k, j + j0))
    if out_slots:
        per_o = (N // out_slots) // tn
        o_spec = pl.BlockSpec((None, tm, tn), lambda i, j, k: (j // per_o, i, j % per_o))
        out_shape = jax.ShapeDtypeStruct((out_slots, M, N // out_slots), out_dtype)
    else:
        o_spec = pl.BlockSpec((tm, tn), lambda i, j, k: (i, j))
        out_shape = jax.ShapeDtypeStruct((M, N), out_dtype)
    dims = (((0 if ta else 1,), (1 if tb else 0,)), ((), ()))
    deps = [] if dep is None else [dep]

    def body(a_ref, b_ref, *rest):
        o_ref = rest[len(deps)]
        prod = lax.dot_general(a_ref[...], b_ref[...], dims, preferred_element_type=f32)
        if nk == 1:
            o_ref[...] = prod.astype(out_dtype)
            return
        acc_ref = rest[len(deps) + 1]
        k = pl.program_id(2)

        @pl.when(k == 0)
        def _():
            acc_ref[...] = prod

        @pl.when((k > 0) & (k < nk - 1))
        def _():
            acc_ref[...] += prod

        @pl.when(k == nk - 1)
        def _():
            o_ref[...] = (acc_ref[...] + prod).astype(out_dtype)

    return pl.pallas_call(
        body, name=name, out_shape=out_shape, grid=(nm, nn, nk),
        in_specs=[a_spec, b_spec] + [ANY_SPEC] * len(deps), out_specs=o_spec,
        scratch_shapes=[pltpu.VMEM((tm, tn), f32)] if nk > 1 else [],
        compiler_params=_params("parallel", "parallel", "arbitrary"),
    )(a, b, *deps)


def _matmul_slabs_t(a_cols, a_slots, b, *, name, tm=512, tn=512, dep=None):
    M = a_cols.shape[0]
    n_slab, N, slab = b.shape
    n1, n2 = a_cols.shape[1] // slab, a_slots.shape[0]
    assert n1 + n2 == n_slab and a_slots.shape[1:] == (M, slab)
    tm, tn = min(tm, M), min(tn, N)
    deps = [] if dep is None else [dep]

    def body(a1_ref, a2_ref, b_ref, *rest):
        o_ref = rest[len(deps)]
        acc = None
        for s in range(n_slab):
            lhs = a1_ref[:, s * slab:(s + 1) * slab] if s < n1 else a2_ref[s - n1]
            prod = lax.dot_general(lhs, b_ref[s], (((1,), (1,)), ((), ())), preferred_element_type=f32)
            acc = prod if acc is None else acc + prod
        o_ref[...] = acc

    return pl.pallas_call(
        body, name=name, out_shape=jax.ShapeDtypeStruct((M, N), f32), grid=(M // tm, N // tn),
        in_specs=[pl.BlockSpec((tm, n1 * slab), lambda i, j: (i, 0)), pl.BlockSpec((n2, tm, slab), lambda i, j: (0, i, 0)),
                  pl.BlockSpec((n_slab, tn, slab), lambda i, j: (0, j, 0))] + [ANY_SPEC] * len(deps),
        out_specs=pl.BlockSpec((tm, tn), lambda i, j: (i, j)), compiler_params=_params("parallel", "parallel"),
    )(a_cols, a_slots, b, *deps)


def _ada_exchange(c_blk, cw_slab, w_ada, b_cols):
    nblk = c_blk.shape[0]
    D, W = w_ada.shape
    CW = cw_slab.shape[1]

    def body(c_ref, cw_ref, w_ref, b_ref, mod_ref, call_ref, cwg_ref, msend, send_sems, recv_sems):
        x, y, c, me = _my_place()
        call_ref[me] = _silu(c_ref[...])
        cwg_ref[me] = cw_ref[...]
        first = []
        for k in range(1, NDEV):
            peer, _ = _peer(x, y, c, k)
            first.append(_remote(call_ref.at[me], call_ref.at[me], send_sems.at[0, k], recv_sems.at[0, k], peer))
            first.append(_remote(cwg_ref.at[me], cwg_ref.at[me], send_sems.at[1, k], recv_sems.at[1, k], peer))
        for cp in first:
            cp.start()
        for k in range(1, NDEV):
            peer, slot = _peer(x, y, c, k)
            _remote(call_ref.at[slot], call_ref.at[slot], send_sems.at[0, k], recv_sems.at[0, k], peer).wait_recv()
            _remote(cwg_ref.at[slot], cwg_ref.at[slot], send_sems.at[1, k], recv_sems.at[1, k], peer).wait_recv()
        mod = jnp.broadcast_to(b_ref[...], (NDEV, W))
        for r in range(nblk):
            mod = mod + lax.dot_general(call_ref[:, r, :], w_ref[r * 128:(r + 1) * 128, :], (((1,), (0,)), ((), ())),
                                        preferred_element_type=f32, precision=lax.Precision.HIGHEST)
        row = lax.broadcasted_iota(jnp.int32, (NDEV, 1), 0)
        pick = lambda j: jnp.broadcast_to(jnp.sum(jnp.where(row == j, mod, 0.0), axis=0, keepdims=True), (8, W))
        mod_ref[me] = pick(me)
        second = []
        for k in range(1, NDEV):
            peer, slot = _peer(x, y, c, k)
            msend[k] = pick(slot)
            second.append(_remote(msend.at[k], mod_ref.at[me], send_sems.at[2, k], recv_sems.at[2, k], peer))
        for cp in second:
            cp.start()
        for k in range(1, NDEV):
            peer, slot = _peer(x, y, c, k)
            _remote(msend.at[k], mod_ref.at[slot], send_sems.at[2, k], recv_sems.at[2, k], peer).wait_recv()
        for cp in first + second:
            cp.wait_send()

    vmem = pl.BlockSpec(memory_space=pltpu.VMEM)
    return pl.pallas_call(
        body, name="ada_exchange",
        out_shape=(jax.ShapeDtypeStruct((NDEV, 8, W), f32), jax.ShapeDtypeStruct((NDEV, nblk, 128), f32),
                   jax.ShapeDtypeStruct((NDEV, 8, CW), f32)),
        in_specs=[vmem] * 4, out_specs=(vmem, vmem, vmem),
        scratch_shapes=[pltpu.VMEM((NDEV, 8, W), f32), pltpu.SemaphoreType.DMA((3, NDEV)),
                        pltpu.SemaphoreType.DMA((3, NDEV))],
        compiler_params=_params(),
    )(c_blk, cw_slab, w_ada, b_cols)


def _prenorm(x, scale, shift, g_pre, dep, tr=256):
    S, D = x.shape
    tr = min(tr, S)

    def body(x_ref, sc_ref, sh_ref, g_ref, dep_ref, h_ref):
        xv = x_ref[...]
        r = lax.rsqrt(jnp.mean(xv * xv, axis=-1, keepdims=True) + EPS)
        h_ref[...] = ((xv * r) * g_ref[...] * (1.0 + sc_ref[...]) + sh_ref[...]).astype(bf16)

    row = pl.BlockSpec((tr, D), lambda i: (i, 0))
    vec = pl.BlockSpec((1, D), lambda i: (0, 0))
    return pl.pallas_call(body, name="prenorm", out_shape=jax.ShapeDtypeStruct((S, D), bf16), grid=(S // tr,),
                          in_specs=[row, vec, vec, vec, ANY_SPEC], out_specs=row, compiler_params=_params("parallel"))(
                              x, scale, shift, g_pre, dep)


def _ext_rows(i, tr, S):
    g = lax.broadcasted_iota(jnp.int32, (tr + 16, 1), 0) + (i * tr - 8)
    return (g >= 0) & (g < S)


def _halo_specs(tr, S, C, col):
    nb8 = S // 8
    main = pl.BlockSpec((tr, C), lambda i: (i, col))
    prev = pl.BlockSpec((8, C), lambda i: (jnp.maximum(i * (tr // 8) - 1, 0), col))
    nxt = pl.BlockSpec((8, C), lambda i: (jnp.minimum((i + 1) * (tr // 8), nb8 - 1), col))
    return prev, main, nxt


def _conv_fwd(proj, conv_w, conv_b, g_conv, tr=256):
    S, C = proj.shape[0], proj.shape[1] // 8
    tr = min(tr, S)

    def body(up, um, un, cp, cm, cn, bg_ref, zc_ref, w_ref, cb_ref, g_ref, o_ref):
        i = pl.program_id(0)
        exists = _ext_rows(i, tr, S)
        u = jnp.concatenate([up[...], um[...], un[...]], axis=0)
        cg = jnp.concatenate([cp[...], cm[...], cn[...]], axis=0)
        t = jnp.where(exists, cg * u, 0.0)
        t_before = pltpu.roll(t, 1, 0)[8:tr + 8]
        t_after = pltpu.roll(t, tr + 15, 0)[8:tr + 8]
        w = w_ref[...]
        cv = w[0:1] * t_before + w[1:2] * t[8:tr + 8] + w[2:3] * t_after + cb_ref[...]
        yc = bg_ref[...] * cv
        rc = lax.rsqrt(jnp.mean(yc * yc, axis=-1, keepdims=True) + EPS)
        o_ref[...] = ((yc * rc) * g_ref[...] * _silu(zc_ref[...])).astype(bf16)

    u_specs = _halo_specs(tr, S, C, 0)
    c_specs = _halo_specs(tr, S, C, 2)
    vec = pl.BlockSpec((1, C), lambda i: (0, 0))
    return pl.pallas_call(
        body, name="conv_fwd", out_shape=jax.ShapeDtypeStruct((S, 2 * C), bf16), grid=(S // tr,),
        in_specs=[*u_specs, *c_specs, pl.BlockSpec((tr, C), lambda i: (i, 1)), pl.BlockSpec((tr, C), lambda i: (i, 3)),
                  pl.BlockSpec((8, C), lambda i: (0, 0)), vec, vec],
        out_specs=pl.BlockSpec((tr, C), lambda i: (i, 0)), compiler_params=_params("parallel"),
    )(proj, proj, proj, proj, proj, proj, proj, proj, conv_w, conv_b, g_conv)


def _branch_geometry(S, r, inter):
    L = S // r * inter
    nq = min(128, L)
    nk = min(nq + 2 * HALF_WIN * inter, L)
    assert L % nq == 0 and (L == nk or L >= nq + 2 * HALF_WIN * inter)
    return L, nq, nk, L // nq


QUAD = 4


def _to_quad(dst, src, S):
    n = S // QUAD
    for rho in range(QUAD):
        dst[pl.ds(rho * n, n), :] = src[pl.ds(rho, n, stride=QUAD), :]


def _block_rows(idx, r, inter, S, L, nq, nk, nblk):
    rho, qb = (0, idx) if r == 1 else (idx // nblk, idx % nblk)
    i0 = qb * nq
    ws = jnp.clip(i0 - HALF_WIN * inter, 0, L - nk)
    if r == 1:
        return pl.ds(pl.multiple_of(i0, 8), nq), pl.ds(pl.multiple_of(ws, 8), nk), i0 - ws
    assert r % (QUAD * inter) == 0
    step = r // QUAD // inter
    base = (rho % QUAD) * (S // QUAD) + rho // QUAD
    if step == 1:
        return pl.ds(pl.multiple_of(base + i0, 8), nq), pl.ds(pl.multiple_of(base + ws, 8), nk), i0 - ws
    return pl.ds(base + step * i0, nq, stride=step), pl.ds(base + step * ws, nk, stride=step), i0 - ws


N_CASES = 3
SCALE = HEAD_DIM ** -0.5
ATTN_UNROLL = 16


def _bias_shape(S):
    shapes = [_branch_geometry(S, r, inter)[1:3] for _, r, inter in BRANCHES]
    return (len(BRANCHES) * N_CASES * 2, max(nq for nq, _ in shapes), max(nk for _, nk in shapes))


def _bias_index(b, case, head):
    return (b * N_CASES + case) * 2 + head


def _fill_bias(bias_scr, sl_ref, S):
    sl = sl_ref[...]
    slope = (sl[0:1, 0:1], sl[0:1, HEAD_DIM:HEAD_DIM + 1])
    for b, (_, r, inter) in enumerate(BRANCHES):
        L, nq, nk, nblk = _branch_geometry(S, r, inter)
        rel = lax.broadcasted_iota(jnp.int32, (nq, nk), 0) - lax.broadcasted_iota(jnp.int32, (nq, nk), 1)
        for case in range(N_CASES):
            d = jnp.abs(rel + case * HALF_WIN)
            valid = d <= HALF_WIN * inter
            if inter > 1:
                valid = valid & (jnp.bitwise_and(d, inter - 1) == 0)
            dist = d.astype(f32) * float(r // inter)
            for head in range(2):
                bias_scr[_bias_index(b, case, head), 0:nq, 0:nk] = jnp.where(valid, -slope[head] * dist, NEG_INF)


def _head_slopes(n_heads):
    slopes = 2.0 ** (-8.0 * jnp.arange(1, n_heads + 1, dtype=f32) / n_heads)
    return jnp.broadcast_to(jnp.repeat(slopes.reshape(n_heads // 2, 2), HEAD_DIM, axis=1)[:, None, :],
                            (n_heads // 2, 8, PAIR))


def _attn_fwd(proj, slopes):
    S, C = proj.shape[0], proj.shape[1] // 8
    npair = C // PAIR

    def body(q_ref, k_ref, v_ref, sl_ref, o_ref, lse_ref, m_scr, l_scr, a_scr, bias_scr, q4_scr, k4_scr, v4_scr):
        lane = lax.broadcasted_iota(jnp.int32, (1, PAIR), 1)
        first = lane < HEAD_DIM
        _fill_bias(bias_scr, sl_ref, S)
        for dst, src in ((q4_scr, q_ref), (k4_scr, k_ref), (v4_scr, v_ref)):
            _to_quad(dst, src, S)

        for b, (_, r, inter) in enumerate(BRANCHES):
            L, nq, nk, nblk = _branch_geometry(S, r, inter)
            qs, ks, vs = (q_ref, k_ref, v_ref) if r == 1 else (q4_scr, k4_scr, v4_scr)

            def step(idx, carry, b=b, r=r, L=L, nq=nq, nk=nk, nblk=nblk, qs=qs, ks=ks, vs=vs):
                qrows, krows, off = _block_rows(idx, r, inter, S, L, nq, nk, nblk)
                case = off // HALF_WIN
                q2 = qs[qrows, :] * SCALE
                k2 = ks[krows, :].astype(bf16)
                v2 = vs[krows, :].astype(bf16)
                ms, accs = [], []
                for hh in range(2):
                    mine = first if hh == 0 else ~first
                    qh = jnp.where(mine, q2, 0.0).astype(bf16)
                    s = lax.dot_general(qh, k2, (((1,), (1,)), ((), ())), preferred_element_type=f32)
                    s = s + bias_scr[_bias_index(b, case, hh), 0:nq, 0:nk]
                    m = jnp.max(s, axis=-1, keepdims=True)
                    p = jnp.exp(s - m).astype(bf16)
                    vh = jnp.where(mine, v2, jnp.ones_like(v2))
                    ms.append(m)
                    accs.append(jnp.dot(p, vh, preferred_element_type=f32))
                m_scr[b, qrows, :] = jnp.where(first, ms[0], ms[1])
                a_scr[b, qrows, :] = jnp.where(first, accs[0], accs[1])
                l_scr[b, qrows, :] = jnp.where(first, accs[1], accs[0])
                return carry

            lax.fori_loop(0, S // nq, step, 0, unroll=min(ATTN_UNROLL, S // nq))

        n4 = S // QUAD
        ch = min(256, n4)
        nch = n4 // ch

        def merge(i, carry):
            rho, part = i // nch, i % nch
            sorted_rows = pl.ds(pl.multiple_of(rho * n4 + part * ch, 8), ch)
            token_rows = pl.ds(rho + QUAD * part * ch, ch, stride=QUAD)
            rows = (token_rows,) + (sorted_rows,) * (len(BRANCHES) - 1)
            ms = [m_scr[b, rows[b], :] for b in range(len(BRANCHES))]
            m = functools.reduce(jnp.maximum, ms)
            l = jnp.zeros((ch, PAIR), f32)
            acc = jnp.zeros((ch, PAIR), f32)
            for b in range(len(BRANCHES)):
                w = jnp.exp(ms[b] - m)
                l = l + w * pltpu.roll(l_scr[b, rows[b], :], HEAD_DIM, 1)
                acc = acc + w * a_scr[b, rows[b], :]
            o_ref[token_rows, :] = acc / l
            lse_ref[token_rows, :] = m + jnp.log(l)
            return carry

        lax.fori_loop(0, QUAD * nch, merge, 0)

    blk = lambda part: pl.BlockSpec((S, PAIR), lambda p: (0, part * npair + p))
    out = pl.BlockSpec((S, PAIR), lambda p: (0, p))
    return pl.pallas_call(
        body, name="attn_fwd",
        out_shape=(jax.ShapeDtypeStruct((S, C), f32), jax.ShapeDtypeStruct((S, C), f32)), grid=(npair,),
        in_specs=[blk(4), blk(5), blk(6), pl.BlockSpec((None, 8, PAIR), lambda p: (p, 0, 0))],
        out_specs=(out, out),
        scratch_shapes=[pltpu.VMEM((3, S, PAIR), f32)] * 3 + [pltpu.VMEM(_bias_shape(S), f32)]
        + [pltpu.VMEM((S, PAIR), f32)] * 3,
        compiler_params=_params("parallel"),
    )(proj, proj, proj, slopes)


def _attn_post(ycat, o, proj, g_attn, tr=256):
    S, C = o.shape
    tr = min(tr, S)

    def body(y_ref, o_ref, z_ref, g_ref, out_ref):
        del y_ref
        ov = o_ref[...]
        ra = lax.rsqrt(jnp.mean(ov * ov, axis=-1, keepdims=True) + EPS)
        out_ref[...] = ((ov * ra) * g_ref[...] * _silu(z_ref[...])).astype(bf16)

    return pl.pallas_call(
        body, name="attn_post", out_shape=jax.ShapeDtypeStruct(ycat.shape, ycat.dtype), grid=(S // tr,),
        in_specs=[HBM_SPEC, pl.BlockSpec((tr, C), lambda i: (i, 0)), pl.BlockSpec((tr, C), lambda i: (i, 7)),
                  pl.BlockSpec((1, C), lambda i: (0, 0))],
        out_specs=pl.BlockSpec((tr, C), lambda i: (i, 1)), input_output_aliases={0: 0},
        compiler_params=_params("arbitrary"),
    )(ycat, o, proj, g_attn)


def _sandwich(y, x, target, gate, g_post, tr=256):
    S, D = y.shape
    tr = min(tr, S)

    def body(y_ref, x_ref, t_ref, gate_ref, g_ref, dy_ref, dout_ref, sums_ref):
        i = pl.program_id(0)
        yv = y_ref[...]
        rp = lax.rsqrt(jnp.mean(yv * yv, axis=-1, keepdims=True) + EPS)
        yhat = yv * rp
        yn = yhat * g_ref[...]
        err = (x_ref[...] + gate_ref[...] * yn) - t_ref[...]
        dout = err * (1.0 / D)
        dout_ref[...] = dout
        dyn = dout * gate_ref[...]
        w = dyn * g_ref[...]
        dy_ref[...] = (rp * (w - yhat * jnp.mean(w * yhat, axis=-1, keepdims=True))).astype(bf16)
        loss = 0.5 * jnp.sum(jnp.mean(err * err, axis=-1, keepdims=True), axis=0, keepdims=True)
        row = lax.broadcasted_iota(jnp.int32, (8, D), 0)
        upd = jnp.where(row == 0, jnp.sum(dout * yn, axis=0, keepdims=True),
                        jnp.where(row == 1, jnp.sum(dyn * yhat, axis=0, keepdims=True),
                                  jnp.where(row == 2, loss, 0.0)))

        @pl.when(i == 0)
        def _():
            sums_ref[...] = upd

        @pl.when(i > 0)
        def _():
            sums_ref[...] += upd

    row = pl.BlockSpec((tr, D), lambda i: (i, 0))
    vec = pl.BlockSpec((1, D), lambda i: (0, 0))
    return pl.pallas_call(
        body, name="sandwich",
        out_shape=(jax.ShapeDtypeStruct((S, D), bf16), jax.ShapeDtypeStruct((S, D), f32), jax.ShapeDtypeStruct((8, D), f32)),
        grid=(S // tr,), in_specs=[row, row, row, vec, vec],
        out_specs=(row, row, pl.BlockSpec((8, D), lambda i: (0, 0))), compiler_params=_params("arbitrary"),
    )(y, x, target, gate, g_post)


def _conv_bwd(proj, dycat, conv_w, conv_b, g_conv, dep, tr=256):
    S, C = proj.shape[0], proj.shape[1] // 8
    tr = min(tr, S)
    n = tr + 16

    def body(*refs):
        ins, (w_ref, cb_ref, g_ref, _, dp_ref, sums_ref) = refs[:15], refs[15:]
        i = pl.program_id(0)
        exists = _ext_rows(i, tr, S)
        u, bg, cg, zc, dyn = (jnp.concatenate([ins[3 * t][...], ins[3 * t + 1][...], ins[3 * t + 2][...]], axis=0)
                              for t in range(5))
        w = w_ref[...]
        t = jnp.where(exists, cg * u, 0.0)
        t_before, t_after = pltpu.roll(t, 1, 0), pltpu.roll(t, n - 1, 0)
        cv = w[0:1] * t_before + w[1:2] * t + w[2:3] * t_after + cb_ref[...]
        yc = bg * cv
        rc = lax.rsqrt(jnp.mean(yc * yc, axis=-1, keepdims=True) + EPS)
        yhat = yc * rc
        sz = _silu(zc)
        wgt = dyn * g_ref[...] * sz
        dyc = rc * (wgt - yhat * jnp.mean(wgt * yhat, axis=-1, keepdims=True))
        dcv = jnp.where(exists, dyc * bg, 0.0)
        dt = w[0:1] * pltpu.roll(dcv, n - 1, 0) + w[1:2] * dcv + w[2:3] * pltpu.roll(dcv, 1, 0)
        mid = slice(8, tr + 8)
        dp_ref[:, 0:C] = (dt * cg)[mid].astype(bf16)
        dp_ref[:, C:2 * C] = (dyc * cv)[mid].astype(bf16)
        dp_ref[:, 2 * C:3 * C] = (dt * u)[mid].astype(bf16)
        dp_ref[:, 3 * C:4 * C] = (dyn * yhat * g_ref[...] * _dsilu(zc))[mid].astype(bf16)
        colsum = lambda v: jnp.sum(v[mid], axis=0, keepdims=True)
        parts = [colsum(dyn * yhat * sz), colsum(dcv), colsum(dcv * t_before), colsum(dcv * t), colsum(dcv * t_after)]
        row = lax.broadcasted_iota(jnp.int32, (8, C), 0)
        upd = jnp.zeros((8, C), f32)
        for j, pj in enumerate(parts):
            upd = jnp.where(row == j, pj, upd)

        @pl.when(i == 0)
        def _():
            sums_ref[...] = upd

        @pl.when(i > 0)
        def _():
            sums_ref[...] += upd

    specs = []
    for col in range(4):
        specs += _halo_specs(tr, S, C, col)
    specs += _halo_specs(tr, S, C, 0)
    vec = pl.BlockSpec((1, C), lambda i: (0, 0))
    return pl.pallas_call(
        body, name="conv_bwd",
        out_shape=(jax.ShapeDtypeStruct((S, 4 * C), bf16), jax.ShapeDtypeStruct((8, C), f32)), grid=(S // tr,),
        in_specs=[*specs, pl.BlockSpec((8, C), lambda i: (0, 0)), vec, vec, ANY_SPEC],
        out_specs=(pl.BlockSpec((tr, 4 * C), lambda i: (i, 0)), pl.BlockSpec((8, C), lambda i: (0, 0))),
        compiler_params=_params("arbitrary"),
    )(*([proj] * 12), dycat, dycat, dycat, conv_w, conv_b, g_conv, dep)


def _attn_post_bwd(o, proj, dycat, g_attn, dep, tr=256):
    S, C = o.shape
    tr = min(tr, S)

    def body(o_ref, z_ref, dy_ref, g_ref, dep_ref, do_ref, dz_ref, sums_ref):
        i = pl.program_id(0)
        ov, zv, dyn = o_ref[...], z_ref[...], dy_ref[...]
        ra = lax.rsqrt(jnp.mean(ov * ov, axis=-1, keepdims=True) + EPS)
        ohat = ov * ra
        sz = _silu(zv)
        wgt = dyn * g_ref[...] * sz
        do_ref[...] = ra * (wgt - ohat * jnp.mean(wgt * ohat, axis=-1, keepdims=True))
        dz_ref[...] = (dyn * ohat * g_ref[...] * _dsilu(zv)).astype(bf16)
        row = lax.broadcasted_iota(jnp.int32, (8, C), 0)
        upd = jnp.where(row == 0, jnp.sum(dyn * ohat * sz, axis=0, keepdims=True), 0.0)

        @pl.when(i == 0)
        def _():
            sums_ref[...] = upd

        @pl.when(i > 0)
        def _():
            sums_ref[...] += upd

    return pl.pallas_call(
        body, name="attn_post_bwd",
        out_shape=(jax.ShapeDtypeStruct((S, C), f32), jax.ShapeDtypeStruct((4, S, C), bf16),
                   jax.ShapeDtypeStruct((8, C), f32)),
        grid=(S // tr,),
        in_specs=[pl.BlockSpec((tr, C), lambda i: (i, 0)), pl.BlockSpec((tr, C), lambda i: (i, 7)),
                  pl.BlockSpec((tr, C), lambda i: (i, 1)), pl.BlockSpec((1, C), lambda i: (0, 0)), ANY_SPEC],
        out_specs=(pl.BlockSpec((tr, C), lambda i: (i, 0)), pl.BlockSpec((None, tr, C), lambda i: (3, i, 0)),
                   pl.BlockSpec((8, C), lambda i: (0, 0))),
        compiler_params=_params("arbitrary"),
    )(o, proj, dycat, g_attn, dep)


def _attn_bwd(proj, o, do, lse, slopes, dqkvz, dep):
    S, C = o.shape
    npair = C // PAIR

    def body(q_ref, k_ref, v_ref, o_ref, do_ref, lse_ref, sl_ref, old_ref, dep_ref, dqkv_ref,
             acc_scr, dl_scr, quad_scr, bias_scr):
        lane = lax.broadcasted_iota(jnp.int32, (1, PAIR), 1)
        first = lane < HEAD_DIM
        _fill_bias(bias_scr, sl_ref, S)
        ch = min(256, S)

        def prep(i, carry):
            rows = pl.ds(pl.multiple_of(i * ch, 8), ch)
            prod = do_ref[rows, :] * o_ref[rows, :]
            d0 = jnp.sum(jnp.where(first, prod, 0.0), axis=-1, keepdims=True)
            d1 = jnp.sum(jnp.where(first, 0.0, prod), axis=-1, keepdims=True)
            dl_scr[rows, :] = jnp.where(first, d0, d1)
            zero = jnp.zeros((ch, PAIR), f32)
            for order in range(2):
                for t in range(3):
                    acc_scr[order, t, rows, :] = zero
            return carry

        lax.fori_loop(0, S // ch, prep, 0)
        token_srcs = (q_ref, k_ref, v_ref, do_ref, lse_ref, dl_scr)
        for j, src in enumerate(token_srcs):
            _to_quad(quad_scr.at[j], src, S)

        for b, (_, r, inter) in enumerate(BRANCHES):
            L, nq, nk, nblk = _branch_geometry(S, r, inter)
            order = 0 if r == 1 else 1
            srcs = token_srcs if r == 1 else tuple(quad_scr.at[j] for j in range(6))

            def step(idx, carry, b=b, r=r, L=L, nq=nq, nk=nk, nblk=nblk, order=order, srcs=srcs):
                qs, ks, vs, dos, lses, dls = srcs
                dq_scr, dk_scr, dv_scr = (acc_scr.at[order, t] for t in range(3))
                qrows, krows, off = _block_rows(idx, r, inter, S, L, nq, nk, nblk)
                case = off // HALF_WIN
                q2 = qs[qrows, :] * SCALE
                k2 = ks[krows, :].astype(bf16)
                v2 = vs[krows, :].astype(bf16)
                do2 = dos[qrows, :]
                lse2 = lses[qrows, :]
                dl2 = dls[qrows, :]
                dq2 = jnp.zeros((nq, PAIR), f32)
                dk2 = jnp.zeros((nk, PAIR), f32)
                dv2 = jnp.zeros((nk, PAIR), f32)
                for hh in range(2):
                    mine = first if hh == 0 else ~first
                    lo = hh * HEAD_DIM
                    qh = jnp.where(mine, q2, 0.0).astype(bf16)
                    doh = jnp.where(mine, do2, 0.0).astype(bf16)
                    s = lax.dot_general(qh, k2, (((1,), (1,)), ((), ())), preferred_element_type=f32)
                    s = s + bias_scr[_bias_index(b, case, hh), 0:nq, 0:nk]
                    p = jnp.exp(s - lse2[:, lo:lo + 1])
                    dv2 = dv2 + lax.dot_general(p.astype(bf16), doh, (((0,), (0,)), ((), ())), preferred_element_type=f32)
                    dp = lax.dot_general(doh, v2, (((1,), (1,)), ((), ())), preferred_element_type=f32)
                    ds = (p * (dp - dl2[:, lo:lo + 1])).astype(bf16)
                    dq2 = dq2 + jnp.where(mine, jnp.dot(ds, k2, preferred_element_type=f32), 0.0)
                    dk2 = dk2 + lax.dot_general(ds, qh, (((0,), (0,)), ((), ())), preferred_element_type=f32)
                dq_scr[qrows, :] = dq_scr[qrows, :] + dq2
                dk_scr[krows, :] = dk_scr[krows, :] + dk2
                dv_scr[krows, :] = dv_scr[krows, :] + dv2
                return carry

            lax.fori_loop(0, S // nq, step, 0, unroll=min(ATTN_UNROLL, S // nq))

        n4 = S // QUAD
        for t in range(3):
            for rho in range(QUAD):
                token_rows = pl.ds(rho, n4, stride=QUAD)
                acc_scr[0, t, token_rows, :] = acc_scr[0, t, token_rows, :] + acc_scr[1, t, pl.ds(rho * n4, n4), :]
        dqkv_ref[0] = (acc_scr[0, 0] * SCALE).astype(bf16)
        dqkv_ref[1] = acc_scr[0, 1].astype(bf16)
        dqkv_ref[2] = acc_scr[0, 2].astype(bf16)

    blk = lambda part: pl.BlockSpec((S, PAIR), lambda p: (0, part * npair + p))
    own = pl.BlockSpec((S, PAIR), lambda p: (0, p))
    return pl.pallas_call(
        body, name="attn_bwd", out_shape=jax.ShapeDtypeStruct(dqkvz.shape, dqkvz.dtype), grid=(npair,),
        in_specs=[blk(4), blk(5), blk(6), own, own, own, pl.BlockSpec((None, 8, PAIR), lambda p: (p, 0, 0)),
                  ANY_SPEC, ANY_SPEC],
        out_specs=pl.BlockSpec((3, S, PAIR), lambda p: (0, 0, p)), input_output_aliases={7: 0},
        scratch_shapes=[pltpu.VMEM((2, 3, S, PAIR), f32), pltpu.VMEM((S, PAIR), f32), pltpu.VMEM((6, S, PAIR), f32),
                        pltpu.VMEM(_bias_shape(S), f32)],
        compiler_params=_params("arbitrary"),
    )(proj, proj, proj, o, do, lse, slopes, dqkvz, dep)


def _prenorm_bwd(dh, x, dout, scale, g_pre, tr=256):
    S, D = x.shape
    tr = min(tr, S)

    def body(dh_ref, x_ref, dout_ref, sc_ref, g_ref, gx_ref, sums_ref):
        i = pl.program_id(0)
        xv, dhv = x_ref[...], dh_ref[...]
        r = lax.rsqrt(jnp.mean(xv * xv, axis=-1, keepdims=True) + EPS)
        xn = xv * r
        dxn = dhv * (g_ref[...] * (1.0 + sc_ref[...]))
        gx_ref[...] = dout_ref[...] + r * (dxn - xn * jnp.mean(dxn * xn, axis=-1, keepdims=True))
        dhx = dhv * xn
        row = lax.broadcasted_iota(jnp.int32, (8, D), 0)
        upd = jnp.where(row == 0, jnp.sum(dhv, axis=0, keepdims=True),
                        jnp.where(row == 1, jnp.sum(dhx, axis=0, keepdims=True) * g_ref[...],
                                  jnp.where(row == 2, jnp.sum(dhx, axis=0, keepdims=True) * (1.0 + sc_ref[...]), 0.0)))

        @pl.when(i == 0)
        def _():
            sums_ref[...] = upd

        @pl.when(i > 0)
        def _():
            sums_ref[...] += upd

    row = pl.BlockSpec((tr, D), lambda i: (i, 0))
    vec = pl.BlockSpec((1, D), lambda i: (0, 0))
    return pl.pallas_call(
        body, name="prenorm_bwd",
        out_shape=(jax.ShapeDtypeStruct((S, D), f32), jax.ShapeDtypeStruct((8, D), f32)), grid=(S // tr,),
        in_specs=[row, row, row, vec, vec], out_specs=(row, pl.BlockSpec((8, D), lambda i: (0, 0))),
        compiler_params=_params("arbitrary"),
    )(dh, x, dout, scale, g_pre)


def _adamw(w, g, m, v):
    m = ADAM_B1 * m + (1.0 - ADAM_B1) * g
    v = ADAM_B2 * v + (1.0 - ADAM_B2) * (g * g)
    m_hat = m / (1.0 - ADAM_B1 ** ADAM_STEP)
    v_hat = v / (1.0 - ADAM_B2 ** ADAM_STEP)
    delta = -ADAM_LR * (m_hat / (jnp.sqrt(v_hat) + ADAM_EPS) + ADAM_WD * w)
    return delta, m, v


def _sum_rows(parts, dep):
    P = parts.shape[1]

    def body(p_ref, dep_ref, o_ref):
        acc = p_ref[0:1, :]
        for j in range(1, NDEV):
            acc = acc + p_ref[j:j + 1, :]
        o_ref[...] = jnp.broadcast_to(acc, (8, P))

    vmem = pl.BlockSpec(memory_space=pltpu.VMEM)
    return pl.pallas_call(body, name="sum_small", out_shape=jax.ShapeDtypeStruct((8, P), f32),
                          in_specs=[vmem, ANY_SPEC], out_specs=vmem, compiler_params=_params())(parts, dep)


def _adamw_small(tot, params):
    given = [p[3] for p in params if not isinstance(p[3], int)]

    def body(tot_ref, *refs):
        given_refs = list(refs[:len(given)])
        ins = refs[len(given):len(given) + 3 * len(params)]
        outs = refs[len(given) + 3 * len(params):]
        for t, (w, _, _, where) in enumerate(params):
            w_ref, m_ref, v_ref = ins[3 * t:3 * t + 3]
            g = tot_ref[0:1, where:where + w.size] if isinstance(where, int) else given_refs.pop(0)[...]
            outs[4 * t][...] = g
            outs[4 * t + 1][...], outs[4 * t + 2][...], outs[4 * t + 3][...] = _adamw(w_ref[...], g, m_ref[...], v_ref[...])

    out_shape = tuple(jax.ShapeDtypeStruct(p[0].shape, f32) for p in params for _ in range(4))
    res = pl.pallas_call(body, name="adamw_small", out_shape=out_shape, compiler_params=_params())(
        tot, *given, *[a for p in params for a in p[:3]])
    return [res[4 * t:4 * t + 4] for t in range(len(params))]


def _adamw_sharded(name, parts, sums_a, sums_b, pick, w, m, v, rows=None, prev=None, tr=128):
    R, Cc = w.shape
    r0, nr = rows or (0, R)
    tr = math.gcd(tr, r0, nr)
    n, b0 = parts.shape[0], r0 // tr

    def body(pick_ref, p_ref, a_ref, b_ref, w_ref, m_ref, v_ref, *rest):
        g_ref, d_ref, nm_ref, nv_ref = rest[-4:]
        g = jnp.where(pick_ref[0] == 1, b_ref[...], a_ref[...]).astype(f32)
        for j in range(n):
            g = g + p_ref[j].astype(f32)
        g_ref[...] = g
        d_ref[...], nm_ref[...], nv_ref[...] = _adamw(w_ref[...], g, m_ref[...], v_ref[...])

    row = pl.BlockSpec((tr, Cc), lambda i, pick: (i + b0, 0))
    mine = pl.BlockSpec((None, tr, Cc), lambda i, pick: (pick[1], i + b0, 0))
    out = jax.ShapeDtypeStruct((R, Cc), f32)
    prev = list(prev or [])
    grid_spec = pltpu.PrefetchScalarGridSpec(
        num_scalar_prefetch=1, grid=(nr // tr,),
        in_specs=[pl.BlockSpec((n, tr, Cc), lambda i, pick: (0, i + b0, 0)), mine, mine, row, row, row]
        + [ANY_SPEC] * len(prev),
        out_specs=(row, row, row, row))
    return pl.pallas_call(
        body, name=name, out_shape=(out, out, out, out), grid_spec=grid_spec,
        input_output_aliases={7 + t: t for t in range(len(prev))}, compiler_params=_params("arbitrary"),
    )(pick, parts, sums_a, sums_b, w, m, v, *prev)


def _adamw_ada(c_t, dmod_cols, w, m, v, dep, tr=256):
    D, W = w.shape
    tr = min(tr, D)

    def body(c_ref, dm_ref, w_ref, m_ref, v_ref, dep_ref, g_ref, d_ref, nm_ref, nv_ref):
        cv, dm = c_ref[...], dm_ref[...]
        g = cv[:, 0:1] * dm[0:1, :]
        for b in range(1, NDEV):
            g = g + cv[:, b:b + 1] * dm[b:b + 1, :]
        g_ref[...] = g
        d_ref[...], nm_ref[...], nv_ref[...] = _adamw(w_ref[...], g, m_ref[...], v_ref[...])

    row = pl.BlockSpec((tr, W), lambda i: (i, 0))
    out = jax.ShapeDtypeStruct((D, W), f32)
    return pl.pallas_call(
        body, name="adamw_ada", out_shape=(out, out, out, out), grid=(D // tr,),
        in_specs=[pl.BlockSpec((tr, NDEV), lambda i: (i, 0)), pl.BlockSpec((NDEV, W), lambda i: (0, 0)), row, row, row,
                  ANY_SPEC],
        out_specs=(row, row, row, row), compiler_params=_params("parallel"),
    )(c_t, dmod_cols, w, m, v, dep)


def kernel(x, c, w_ada, b_ada, g_pre, w_in, conv_w, conv_b, g_conv, g_attn, w_out, g_post, loss_target, m_w_ada, m_b_ada, m_g_pre, m_w_in, m_conv_w, m_conv_b, m_g_conv, m_g_attn, m_w_out, m_g_post, v_w_ada, v_b_ada, v_g_pre, v_w_in, v_conv_w, v_conv_b, v_g_conv, v_g_attn, v_w_out, v_g_post):
    S, D = x.shape[1], x.shape[2]
    C = D // 2
    W = w_ada.shape[2]
    CW = conv_w.shape[2]
    me = 4 * lax.axis_index("x") + 2 * lax.axis_index("y") + lax.axis_index("c")
    x2, tgt = x[0], loss_target[0]
    w_ada2, w_in2, w_out2 = w_ada[0], w_in[0], w_out[0]

    R = D // NDEV
    core = lax.axis_index("c").astype(jnp.int32).reshape(1)

    cw_slab = jnp.zeros((8, CW), f32).at[:3].set(conv_w[0])
    b_cols = lax.dynamic_slice_in_dim(b_ada, me * W, W, axis=1)
    mod_slabs, c_blocks, cw_g = _ada_exchange(c.reshape(D // 128, 128), cw_slab, w_ada2, b_cols)
    c_all = c_blocks.reshape(NDEV, D)
    conv_w_full = jnp.transpose(cw_g, (1, 0, 2)).reshape(8, C)
    mod = mod_slabs[:, 0, :].reshape(1, 3 * D)
    shift, scale, gate = mod[:, :D], mod[:, D:2 * D], mod[:, 2 * D:]

    land_i = lax.dynamic_update_slice(lax.empty((NDEV, D, C), bf16), w_in2.astype(bf16)[None], (me, 0, 0))
    land_o = lax.dynamic_update_slice(lax.empty((NDEV, R, D), bf16), w_out2.astype(bf16)[None], (me, 0, 0))
    wi_send, wi_recv, land_i, w_token = _w_in_start(land_i, [mod_slabs])

    me_arr = me.astype(jnp.int32).reshape(1)
    h = _prenorm(x2, scale, shift, g_pre, w_token)
    land_i = _w_in_sibling(land_i, wi_recv, after=[h])
    proj = _in_proj_part("in_proj_a", h, land_i, None, me_arr, 0, 1, 2)
    fi_send, fi_recv, land_i = _w_in_relay(land_i, wi_recv, after=[proj])
    proj = _in_proj_part("in_proj_b", h, land_i, proj, me_arr, 2, 2, 2)
    land_i = _w_in_forwarded(land_i, fi_recv, after=[proj])
    proj = _in_proj_part("in_proj_c", h, land_i, proj, me_arr, 3, 2, 2)
    (di_send, di_recv, wo_send, wo_recv), land_i, land_o = _w_in_diag(land_i, land_o, fi_recv, after=[proj])
    proj = _in_proj_part("in_proj_d", h, land_i, proj, me_arr, 6, 1, 1)
    win_g = _w_in_finish(land_i, wi_send, fi_send, di_send, di_recv, after=[proj])
    proj = _in_proj_part("in_proj_e", h, win_g, proj, me_arr, 7, 1, 1)
    slopes = _head_slopes(C // HEAD_DIM)
    ycat = _conv_fwd(proj, conv_w_full, conv_b, g_conv)
    o, lse = _attn_fwd(proj, slopes)
    (fo_send, fo_recv), (land_o,), _ = _weights_forward("w_out_forward", land_o, wo_recv, after=[o])
    ycat = _attn_post(ycat, o, proj, g_attn)
    wout_g = _weights_wait("w_out_wait", land_o, wo_send, wo_recv, fo_send, fo_recv, after=[ycat])
    wout_full = wout_g.reshape(D, D)
    y = _matmul(ycat, wout_full, name="out_proj", out_dtype=f32)
    dy, dout, post_sums = _sandwich(y, x2, tgt, gate, g_post)

    chip = me // 2

    def landing(rows, cols):
        return lax.dynamic_update_slice(lax.empty((NCHIP, rows, cols), bf16), jnp.zeros((1, rows, cols), bf16),
                                        (chip, 0, 0))

    gw_out = _matmul(ycat, dy, name="out_proj_dw", out_dtype=bf16, ta=True).reshape(NDEV, R, D)
    po_send, po_recv, gw_out, pair_o, po_token = _pair_start("g_out_pair_start", gw_out)
    dycat = _matmul(dy, wout_full, name="out_proj_dx", out_dtype=f32, tb=True, dep=po_token)
    gw_out, pair_o = _pair_wait("g_out_pair_wait", gw_out, pair_o, po_send, po_recv, after=[dycat])
    sum_o = _pair_sum("g_out_pair_sum", gw_out, pair_o, core)
    co_send, co_recv, sum_o, land_go, co_token = _chip_start(
        "g_out_chip_start", sum_o, landing(R, D), 0)
    dpc, conv_sums = _conv_bwd(proj, dycat, conv_w_full, conv_b, g_conv, co_token)
    gw_c = _matmul(h, dpc, name="in_proj_dw_conv", out_dtype=bf16, ta=True, out_slots=4)
    pc_send, pc_recv, gw_c, pair_c, pc_token = _pair_start("g_conv_pair_start", gw_c)
    do, dpa, attn_sums = _attn_post_bwd(o, proj, dycat, g_attn, pc_token)
    gw_c, pair_c = _pair_wait("g_conv_pair_wait", gw_c, pair_c, pc_send, pc_recv, after=[do])
    sum_c = _pair_sum("g_conv_pair_sum", gw_c, pair_c, core)
    cc_send, cc_recv, sum_c, land_gi, cc_token = _chip_start(
        "g_conv_chip_start", sum_c, landing(D, C), 0)
    dpa = _attn_bwd(proj, o, do, lse, slopes, dpa, cc_token)
    gw_a = _matmul(h, dpa, name="in_proj_dw_attn", out_dtype=bf16, ta=True, b_slots=True, out_slots=4)
    pa_send, pa_recv, gw_a, pair_a, pa_token = _pair_start("g_attn_pair_start", gw_a)
    gw_a, pair_a = _pair_wait("g_attn_pair_wait", gw_a, pair_a, pa_send, pa_recv, after=[pa_token])
    sum_a = _pair_sum("g_attn_pair_sum", gw_a, pair_a, core)
    part_a, part_b = (0, 3 * D // 4), (3 * D // 4, D // 4)
    ca_send, ca_recv, sum_a, land_gi, ca_token = _chip_start("g_attn_chip_start_a", sum_a, land_gi, 4, part_a)
    dh = _matmul_slabs_t(dpc, dpa, win_g, name="in_proj_dx", dep=ca_token)
    grad_x, pre_sums = _prenorm_bwd(dh, x2, dout, scale, g_pre)

    small = jnp.concatenate([pre_sums[0:1], pre_sums[1:2], post_sums[0:1],
                             pre_sums[2:3], post_sums[1:2],
                             conv_sums[2:3], conv_sums[3:4], conv_sums[4:5],
                             conv_sums[1:2], conv_sums[0:1], attn_sums[0:1]], axis=1)
    small = jnp.concatenate([small.reshape(8 * D // 128, 128), jnp.broadcast_to(post_sums[2:3, :128], (8, 128))])
    (small_all,) = _all_gather([small], "gather_small")
    cb_send, cb_recv, sum_a, land_gi, cb_token = _chip_start("g_attn_chip_start_b", sum_a, land_gi, 4, part_b,
                                                             after=[small_all])
    small_all = small_all.reshape(NDEV, small.size)
    tot = _sum_rows(small_all, cb_token)
    loss = tot[0, 8 * D]
    g_conv_w = lax.dynamic_slice_in_dim(tot[0:1, 5 * D:5 * D + 3 * C].reshape(1, 3, C), me * CW, CW, axis=2)
    ((g_b_ada, d_b_ada, nm_b_ada, nv_b_ada), (g_g_pre, d_g_pre, nm_g_pre, nv_g_pre),
     (g_g_post, d_g_post, nm_g_post, nv_g_post), (g_conv_w, d_conv_w, nm_conv_w, nv_conv_w),
     (g_conv_b, d_conv_b, nm_conv_b, nv_conv_b), (g_g_conv, d_g_conv, nm_g_conv, nv_g_conv),
     (g_g_attn, d_g_attn, nm_g_attn, nv_g_attn)) = _adamw_small(tot, [
         (b_ada, m_b_ada, v_b_ada, 0), (g_pre, m_g_pre, v_g_pre, 3 * D), (g_post, m_g_post, v_g_post, 4 * D),
         (conv_w, m_conv_w, v_conv_w, g_conv_w), (conv_b, m_conv_b, v_conv_b, 5 * D + 3 * C),
         (g_conv, m_g_conv, v_g_conv, 5 * D + 4 * C), (g_attn, m_g_attn, v_g_attn, 5 * D + 5 * C)])

    dmod_cols = lax.dynamic_slice_in_dim(small_all[:, :3 * D], me * W, W, axis=1)
    g_w_ada, d_w_ada, nm_w_ada, nv_w_ada = _adamw_ada(c_all.T, dmod_cols, w_ada2, m_w_ada[0], v_w_ada[0], cb_token)

    sum_o, land_go = _chip_wait("g_out_chip_wait", sum_o, land_go, co_send, co_recv, 0, after=[g_w_ada])
    pick_out = jnp.stack([jnp.int32(0), me // 2]).astype(jnp.int32)
    g_w_out, d_w_out, nm_w_out, nv_w_out = _adamw_sharded(
        "adamw_w_out", land_go, sum_o, sum_o, pick_out, w_out2, m_w_out[0], v_w_out[0])
    pick_in = jnp.stack([me // 4, (me % 4) // 2]).astype(jnp.int32)
    sum_c, land_gi = _chip_wait("g_conv_chip_wait", sum_c, land_gi, cc_send, cc_recv, 0, after=[g_w_out])
    sum_a, land_gi = _chip_wait("g_attn_chip_wait_a", sum_a, land_gi, ca_send, ca_recv, 4, [g_w_out], part_a)
    first = _adamw_sharded("adamw_w_in_a", land_gi, sum_c, sum_a, pick_in, w_in2, m_w_in[0], v_w_in[0], rows=part_a)
    sum_a, land_gi = _chip_wait("g_attn_chip_wait_b", sum_a, land_gi, cb_send, cb_recv, 4, [first[0]], part_b)
    g_w_in, d_w_in, nm_w_in, nv_w_in = _adamw_sharded(
        "adamw_w_in_b", land_gi, sum_c, sum_a, pick_in, w_in2, m_w_in[0], v_w_in[0], rows=part_b, prev=first)

    return (loss, grad_x[None],
            g_w_ada[None], g_b_ada, g_g_pre, g_w_in[None], g_conv_w, g_conv_b, g_g_conv, g_g_attn, g_w_out[None], g_g_post,
            d_w_ada[None], d_b_ada, d_g_pre, d_w_in[None], d_conv_w, d_conv_b, d_g_conv, d_g_attn, d_w_out[None], d_g_post,
            nm_w_ada[None], nm_b_ada, nm_g_pre, nm_w_in[None], nm_conv_w, nm_conv_b, nm_g_conv, nm_g_attn, nm_w_out[None], nm_g_post,
            nv_w_ada[None], nv_b_ada, nv_g_pre, nv_w_in[None], nv_conv_w, nv_conv_b, nv_g_conv, nv_g_attn, nv_w_out[None], nv_g_post)
```

```python
import functools
import math

import jax
import jax.numpy as jnp
from jax import lax
from jax.experimental import pallas as pl
from jax.experimental.pallas import tpu as pltpu

f32 = jnp.float32
bf16 = jnp.bfloat16

NDEV = 8
HEAD_DIM = 64
PAIR = 2 * HEAD_DIM
BRANCHES = ((128, 1, 1), (512, 4, 1), (2048, 16, 2))
HALF_WIN = 64
EPS = 1e-6
NEG_INF = -1e30
ADAM_LR, ADAM_B1, ADAM_B2, ADAM_EPS, ADAM_WD, ADAM_STEP = 0.001, 0.9, 0.999, 1e-08, 0.01, 10
MESH = pl.DeviceIdType.MESH
VMEM_LIMIT = 56 * 1024 * 1024
HBM_SPEC = pl.BlockSpec(memory_space=pltpu.HBM)
ANY_SPEC = pl.BlockSpec(memory_space=pl.ANY)
SEM_SPEC = pl.BlockSpec(memory_space=pltpu.SEMAPHORE)


def _params(*sem):
    return pltpu.CompilerParams(dimension_semantics=sem or None, vmem_limit_bytes=VMEM_LIMIT)


def _silu(z):
    return z * jax.nn.sigmoid(z)


def _silu_and_slope(z):
    s = jax.nn.sigmoid(z)
    return z * s, s * (1.0 + z * (1.0 - s))


def _my_place():
    x, y, c = lax.axis_index("x"), lax.axis_index("y"), lax.axis_index("c")
    return x, y, c, 4 * x + 2 * y + c


def _peer(x, y, c, k):
    px, py, pc = x ^ (k >> 2 & 1), y ^ (k >> 1 & 1), c ^ (k & 1)
    return (px, py, pc), 4 * px + 2 * py + pc


def _all_gather(arrays, name):
    n = len(arrays)

    def body(*refs):
        srcs, dsts = refs[:n], refs[n:2 * n]
        send_sems, recv_sems, local_sems = refs[2 * n:]
        x, y, c, me = _my_place()
        locals_, sends = [], []
        for t in range(n):
            own = pltpu.make_async_copy(srcs[t], dsts[t].at[me], local_sems.at[t])
            own.start()
            locals_.append(own)
            for k in range(1, NDEV):
                peer, pidx = _peer(x, y, c, k)
                cp = pltpu.make_async_remote_copy(
                    src_ref=srcs[t], dst_ref=dsts[t].at[me], send_sem=send_sems.at[t, k],
                    recv_sem=recv_sems.at[t, k], device_id=peer, device_id_type=MESH)
                cp.start()
                sends.append(cp)
        for t in range(n):
            for k in range(1, NDEV):
                peer, pidx = _peer(x, y, c, k)
                pltpu.make_async_remote_copy(
                    src_ref=srcs[t], dst_ref=dsts[t].at[pidx], send_sem=send_sems.at[t, k],
                    recv_sem=recv_sems.at[t, k], device_id=peer, device_id_type=MESH).wait_recv()
        for cp in sends:
            cp.wait_send()
        for cp in locals_:
            cp.wait()

    return pl.pallas_call(
        body, name=name,
        out_shape=tuple(jax.ShapeDtypeStruct((NDEV,) + a.shape, a.dtype) for a in arrays),
        in_specs=[HBM_SPEC] * n, out_specs=tuple([HBM_SPEC] * n),
        scratch_shapes=[pltpu.SemaphoreType.DMA((n, NDEV)), pltpu.SemaphoreType.DMA((n, NDEV)),
                        pltpu.SemaphoreType.DMA((n,))],
    )(*arrays)


def _comm_call(name, arrays, sems, new_sems, body, after=(), token=False):
    na, ns, nn, nf = len(arrays), len(sems), len(new_sems), len(after)

    def kern(*refs):
        ins, outs = refs[:na + ns + nf], refs[na + ns + nf:]
        body(ins[:na], ins[na:na + ns], outs[:nn])
        if token:
            outs[nn + na][...] = jnp.zeros((8, 128), f32)

    out_shape = ([pltpu.SemaphoreType.DMA(s) for s in new_sems] + [pltpu.HBM(a.shape, a.dtype) for a in arrays]
                 + ([jax.ShapeDtypeStruct((8, 128), f32)] if token else []))
    out_specs = [SEM_SPEC] * nn + [HBM_SPEC] * na + ([pl.BlockSpec(memory_space=pltpu.VMEM)] if token else [])
    res = pl.pallas_call(
        kern, name=name, out_shape=tuple(out_shape),
        in_specs=[HBM_SPEC] * na + [SEM_SPEC] * ns + [ANY_SPEC] * nf, out_specs=tuple(out_specs),
        input_output_aliases={t: nn + t for t in range(na)},
        compiler_params=pltpu.CompilerParams(has_side_effects=pltpu.SideEffectType.DATAFLOW_SIDE_EFFECTING),
    )(*[pltpu.with_memory_space_constraint(a, pltpu.HBM) for a in arrays], *sems, *after)
    return list(res[:nn]), list(res[nn:nn + na]), (res[nn + na] if token else None)


def _remote(src, dst, send_sem, recv_sem, device):
    return pltpu.make_async_remote_copy(src_ref=src, dst_ref=dst, send_sem=send_sem, recv_sem=recv_sem,
                                        device_id=device, device_id_type=MESH)


SAME_CORE = (2, 4, 6)
VIA_SIBLING = (3, 5, 7)


def _weights_forward(name, land, recv, after):
    def body(a, s, new):
        (land,), (recv,), (fsend, frecv) = a, s, new
        x, y, c, me = _my_place()
        sibling, _ = _peer(x, y, c, 1)
        for k in SAME_CORE:
            peer, slot = _peer(x, y, c, k)
            _remote(land.at[slot], land.at[slot], fsend.at[k], recv.at[k], peer).wait_recv()
            _remote(land.at[slot], land.at[slot], fsend.at[k], frecv.at[k ^ 1], sibling).start()

    return _comm_call(name, [land], [recv], [(NDEV,), (NDEV,)], body, after=after)


def _weights_wait(name, land, send, recv, fsend, frecv, after):
    def body(a, s, new):
        (land,), (send, recv, fsend, frecv) = a, s
        x, y, c, me = _my_place()
        sibling, sib_slot = _peer(x, y, c, 1)
        _remote(land.at[sib_slot], land.at[sib_slot], send.at[1], recv.at[1], sibling).wait_recv()
        for k in VIA_SIBLING:
            _, slot = _peer(x, y, c, k)
            _remote(land.at[slot], land.at[slot], fsend.at[k ^ 1], frecv.at[k], sibling).wait_recv()
        for k in (1,) + SAME_CORE:
            peer, _ = _peer(x, y, c, k)
            _remote(land.at[me], land.at[me], send.at[k], recv.at[k], peer).wait_send()
        for k in SAME_CORE:
            _, slot = _peer(x, y, c, k)
            _remote(land.at[slot], land.at[slot], fsend.at[k], frecv.at[k ^ 1], sibling).wait_send()

    return _comm_call(name, [land], [send, recv, fsend, frecv], [], body, after=after)[1][0]


def _diag_relay(x, y, c):
    slot = 4 * (x ^ (1 - c)) + 2 * (y ^ c) + c
    return slot, (x ^ c, y ^ (1 - c), c)


def _w_in_start(land, after):
    def body(a, s, new):
        (land,), (send, recv) = a, new
        x, y, c, me = _my_place()
        for k in (1, 2, 4):
            peer, _ = _peer(x, y, c, k)
            _remote(land.at[me], land.at[me], send.at[k], recv.at[k], peer).start()

    (send, recv), (land,), token = _comm_call("w_in_start", [land], [], [(NDEV,), (NDEV,)], body, after=after, token=True)
    return send, recv, land, token


def _w_in_sibling(land, recv, after):
    def body(a, s, new):
        (land,), (recv,) = a, s
        x, y, c, me = _my_place()
        sibling, slot = _peer(x, y, c, 1)
        _remote(land.at[slot], land.at[slot], recv.at[1], recv.at[1], sibling).wait_recv()

    return _comm_call("w_in_sibling", [land], [recv], [], body, after=after)[1][0]


def _w_in_relay(land, recv, after):
    def body(a, s, new):
        (land,), (recv,), (fsend, frecv) = a, s, new
        x, y, c, me = _my_place()
        sibling, _ = _peer(x, y, c, 1)
        for k in (2, 4):
            peer, slot = _peer(x, y, c, k)
            _remote(land.at[slot], land.at[slot], fsend.at[k], recv.at[k], peer).wait_recv()
        slot, target = _diag_relay(x, y, c)
        _remote(land.at[slot], land.at[slot], fsend.at[6], frecv.at[6], target).start()
        for k in (2, 4):
            _, slot = _peer(x, y, c, k)
            _remote(land.at[slot], land.at[slot], fsend.at[k], frecv.at[k ^ 1], sibling).start()

    (fsend, frecv), (land,), _ = _comm_call("w_in_relay", [land], [recv], [(NDEV,), (NDEV,)], body, after=after)
    return fsend, frecv, land


def _w_in_forwarded(land, frecv, after):
    def body(a, s, new):
        (land,), (frecv,) = a, s
        x, y, c, me = _my_place()
        sibling, _ = _peer(x, y, c, 1)
        for k in (3, 5):
            _, slot = _peer(x, y, c, k)
            _remote(land.at[slot], land.at[slot], frecv.at[k], frecv.at[k], sibling).wait_recv()

    return _comm_call("w_in_forwarded", [land], [frecv], [], body, after=after)[1][0]


def _w_in_diag(land, land_o, frecv, after):
    def body(a, s, new):
        (land, land_o), (frecv,), (dsend, drecv, osend, orecv) = a, s, new
        x, y, c, me = _my_place()
        sibling, _ = _peer(x, y, c, 1)
        peer, slot = _peer(x, y, c, 6)
        _remote(land.at[slot], land.at[slot], dsend.at[6], frecv.at[6], peer).wait_recv()
        _remote(land.at[slot], land.at[slot], dsend.at[6], drecv.at[7], sibling).start()
        for k in (1,) + SAME_CORE:
            peer, _ = _peer(x, y, c, k)
            _remote(land_o.at[me], land_o.at[me], osend.at[k], orecv.at[k], peer).start()

    sems, (land, land_o), _ = _comm_call("w_in_diag", [land, land_o], [frecv], [(NDEV,)] * 4, body, after=after)
    return sems, land, land_o


def _w_in_finish(land, send, fsend, dsend, drecv, after):
    def body(a, s, new):
        (land,), (send, fsend, dsend, drecv) = a, s
        x, y, c, me = _my_place()
        sibling, _ = _peer(x, y, c, 1)
        _, slot = _peer(x, y, c, 7)
        _remote(land.at[slot], land.at[slot], dsend.at[6], drecv.at[7], sibling).wait_recv()
        for k in (1, 2, 4):
            peer, _ = _peer(x, y, c, k)
            _remote(land.at[me], land.at[me], send.at[k], send.at[k], peer).wait_send()
        for k in (2, 4, 6):
            _, slot = _peer(x, y, c, k)
            _remote(land.at[slot], land.at[slot], fsend.at[k], fsend.at[k], sibling).wait_send()
        _, slot = _peer(x, y, c, 6)
        _remote(land.at[slot], land.at[slot], dsend.at[6], dsend.at[6], sibling).wait_send()

    return _comm_call("w_in_finish", [land], [send, fsend, dsend, drecv], [], body, after=after)[1][0]


def _in_proj_part(name, h, land, proj, me_arr, k0, kstep, nk, tm=512):
    S, D = h.shape
    C = land.shape[2]
    tm = min(tm, S)

    def body(me_ref, a_ref, b_ref, *rest):
        rest[-1][...] = jnp.dot(a_ref[...], b_ref[...], preferred_element_type=f32)

    slot = lambda j, me: me[0] ^ (k0 + kstep * j)
    args = [h, land] + ([] if proj is None else [proj])
    grid_spec = pltpu.PrefetchScalarGridSpec(
        num_scalar_prefetch=1, grid=(nk, S // tm),
        in_specs=[pl.BlockSpec((tm, D), lambda j, i, me: (i, 0)),
                  pl.BlockSpec((None, D, C), lambda j, i, me: (slot(j, me), 0, 0))] + [ANY_SPEC] * (len(args) - 2),
        out_specs=pl.BlockSpec((tm, C), lambda j, i, me: (i, slot(j, me))))
    return pl.pallas_call(
        body, name=name, out_shape=jax.ShapeDtypeStruct((S, NDEV * C), f32), grid_spec=grid_spec,
        input_output_aliases={} if proj is None else {3: 0}, compiler_params=_params("arbitrary", "arbitrary"),
    )(me_arr, *args)


NCHIP = NDEV // 2


def _pair_start(name, src):
    npair = src.shape[0] // 2

    def body(a, s, new):
        (src, pair), (send, recv) = a, new
        x, y, c, me = _my_place()
        sibling, _ = _peer(x, y, c, 1)
        for i in range(npair):
            _remote(src.at[2 * i + 1 - c], pair.at[i], send.at[i], recv.at[i], sibling).start()

    pair = lax.empty((npair,) + src.shape[1:], src.dtype)
    (send, recv), (src, pair), token = _comm_call(name, [src, pair], [], [(npair,), (npair,)], body, token=True)
    return send, recv, src, pair, token


def _pair_wait(name, src, pair, send, recv, after):
    npair = pair.shape[0]

    def body(a, s, new):
        (src, pair), (send, recv) = a, s
        x, y, c, me = _my_place()
        sibling, _ = _peer(x, y, c, 1)
        for i in range(npair):
            cp = _remote(src.at[2 * i + 1 - c], pair.at[i], send.at[i], recv.at[i], sibling)
            cp.wait_recv()
            cp.wait_send()

    return _comm_call(name, [src, pair], [send, recv], [], body, after=after)[1]


def _pair_sum(name, src, pair, core, tr=1024):
    npair, R, Cc = pair.shape
    tr = min(tr, R)

    def body(core_ref, a_ref, b_ref, o_ref):
        o_ref[...] = (a_ref[...].astype(f32) + b_ref[...].astype(f32)).astype(o_ref.dtype)

    grid_spec = pltpu.PrefetchScalarGridSpec(
        num_scalar_prefetch=1, grid=(npair, R // tr),
        in_specs=[pl.BlockSpec((None, tr, Cc), lambda i, r, core: (2 * i + core[0], r, 0)),
                  pl.BlockSpec((None, tr, Cc), lambda i, r, core: (i, r, 0))],
        out_specs=pl.BlockSpec((None, tr, Cc), lambda i, r, core: (i, r, 0)))
    return pl.pallas_call(body, name=name, out_shape=jax.ShapeDtypeStruct(pair.shape, pair.dtype),
                          grid_spec=grid_spec, compiler_params=_params("parallel", "parallel"))(core, src, pair)


def _owner_chip(first, i):
    q = first // 2 + i
    return q >> 1 & 1, q & 1


def _chip_start(name, sums, land, first, rows=None, after=()):
    npair = sums.shape[0]
    rows = pl.ds(*(rows or (0, sums.shape[1])))

    def body(a, s, new):
        (sums, land), (send, recv) = a, new
        x, y, c, me = _my_place()
        for i in range(npair):
            ox, oy = _owner_chip(first, i)

            @pl.when((x != ox) | (y != oy))
            def _():
                _remote(sums.at[i, rows], land.at[2 * x + y, rows], send.at[i], recv.at[2 * x + y], (ox, oy, c)).start()

    (send, recv), (sums, land), token = _comm_call(name, [sums, land], [], [(npair,), (NCHIP,)], body, after=after,
                                                   token=True)
    return send, recv, sums, land, token


def _chip_wait(name, sums, land, send, recv, first, after, rows=None):
    npair = sums.shape[0]
    rows = pl.ds(*(rows or (0, sums.shape[1])))

    def body(a, s, new):
        (sums, land), (send, recv) = a, s
        x, y, c, me = _my_place()
        mine = (me >= first) & (me < first + 2 * npair)
        for i in range(npair):
            ox, oy = _owner_chip(first, i)

            @pl.when((x != ox) | (y != oy))
            def _():
                _remote(sums.at[i, rows], land.at[2 * x + y, rows], send.at[i], recv.at[2 * x + y], (ox, oy, c)).wait_send()
        for q in range(NCHIP):
            @pl.when(mine & (2 * x + y != q))
            def _():
                _remote(sums.at[0, rows], land.at[q, rows], send.at[0], recv.at[q], (q >> 1, q & 1, c)).wait_recv()

    return _comm_call(name, [sums, land], [send, recv], [], body, after=after)[1]


def _matmul(a, b, *, name, out_dtype, ta=False, tb=False, b_slots=False, out_slots=0, b_cols=None,
            tm=1024, tn=1024, tk=2048, dep=None):
    M, K = (a.shape[1], a.shape[0]) if ta else a.shape
    col0 = 0
    if b_slots:
        slab = b.shape[2]
        N = b.shape[1] if tb else b.shape[0] * slab
        assert (K if tb else N) == b.shape[0] * slab
    elif b_cols is not None:
        assert not tb
        col0, N = b_cols
    else:
        N = b.shape[0] if tb else b.shape[1]
    tm, tn, tk = min(tm, M), min(tn, N), min(tk, K)
    if b_slots:
        if tb:
            tk = min(tk, slab)
        else:
            tn = min(tn, slab)
    if out_slots:
        tn = min(tn, N // out_slots)
    nm, nn, nk = M // tm, N // tn, K // tk
    assert (nm * tm, nn * tn, nk * tk) == (M, N, K) and col0 % tn == 0, (name, M, N, K, tm, tn, tk)
    j0 = col0 // tn

    a_spec = pl.BlockSpec((tk, tm), lambda i, j, k: (k, i)) if ta else pl.BlockSpec((tm, tk), lambda i, j, k: (i, k))
    if b_slots and tb:
        per = slab // tk
        b_spec = pl.BlockSpec((None, tn, tk), lambda i, j, k: (k // per, j, k % per))
    elif b_slots:
        per = slab // tn
        b_spec = pl.BlockSpec((None, tk, tn), lambda i, j, k: (j // per, k, j % per))
    elif tb:
        b_spec = pl.BlockSpec((tn, tk), lambda i, j, k: (j, k))
    else:
        b_spec = pl.BlockSpec((tk, tn), lambda i, j, k: (k, j + j0))
    if out_slots:
        per_o = (N // out_slots) // tn
        o_spec = pl.BlockSpec((None, tm, tn), lambda i, j, k: (j // per_o, i, j % per_o))
        out_shape = jax.ShapeDtypeStruct((out_slots, M, N // out_slots), out_dtype)
    else:
        o_spec = pl.BlockSpec((tm, tn), lambda i, j, k: (i, j))
        out_shape = jax.ShapeDtypeStruct((M, N), out_dtype)
    dims = (((0 if ta else 1,), (1 if tb else 0,)), ((), ()))
    deps = [] if dep is None else [dep]

    def body(a_ref, b_ref, *rest):
        o_ref = rest[len(deps)]
        prod = lax.dot_general(a_ref[...], b_ref[...], dims, preferred_element_type=f32)
        if nk == 1:
            o_ref[...] = prod.astype(out_dtype)
            return
        acc_ref = rest[len(deps) + 1]
        k = pl.program_id(2)

        @pl.when(k == 0)
        def _():
            acc_ref[...] = prod

        @pl.when((k > 0) & (k < nk - 1))
        def _():
            acc_ref[...] += prod

        @pl.when(k == nk - 1)
        def _():
            o_ref[...] = (acc_ref[...] + prod).astype(out_dtype)

    return pl.pallas_call(
        body, name=name, out_shape=out_shape, grid=(nm, nn, nk),
        in_specs=[a_spec, b_spec] + [ANY_SPEC] * len(deps), out_specs=o_spec,
        scratch_shapes=[pltpu.VMEM((tm, tn), f32)] if nk > 1 else [],
        compiler_params=_params("parallel", "parallel", "arbitrary"),
    )(a, b, *deps)


def _matmul_slabs_t(a_cols, a_slots, b, *, name, tm=512, tn=512, dep=None):
    M = a_cols.shape[0]
    n_slab, N, slab = b.shape
    n1, n2 = a_cols.shape[1] // slab, a_slots.shape[0]
    assert n1 + n2 == n_slab and a_slots.shape[1:] == (M, slab)
    tm, tn = min(tm, M), min(tn, N)
    deps = [] if dep is None else [dep]

    def body(a1_ref, a2_ref, b_ref, *rest):
        o_ref = rest[len(deps)]
        acc = None
        for s in range(n_slab):
            lhs = a1_ref[:, s * slab:(s + 1) * slab] if s < n1 else a2_ref[s - n1]
            prod = lax.dot_general(lhs, b_ref[s], (((1,), (1,)), ((), ())), preferred_element_type=f32)
            acc = prod if acc is None else acc + prod
        o_ref[...] = acc

    return pl.pallas_call(
        body, name=name, out_shape=jax.ShapeDtypeStruct((M, N), f32), grid=(M // tm, N // tn),
        in_specs=[pl.BlockSpec((tm, n1 * slab), lambda i, j: (i, 0)), pl.BlockSpec((n2, tm, slab), lambda i, j: (0, i, 0)),
                  pl.BlockSpec((n_slab, tn, slab), lambda i, j: (0, j, 0))] + [ANY_SPEC] * len(deps),
        out_specs=pl.BlockSpec((tm, tn), lambda i, j: (i, j)), compiler_params=_params("parallel", "parallel"),
    )(a_cols, a_slots, b, *deps)


def _ada_exchange(c_blk, cw_slab, w_ada, b_cols):
    nblk = c_blk.shape[0]
    D, W = w_ada.shape
    CW = cw_slab.shape[1]

    def body(c_ref, cw_ref, w_ref, b_ref, mod_ref, call_ref, cwg_ref, msend, send_sems, recv_sems):
        x, y, c, me = _my_place()
        call_ref[me] = _silu(c_ref[...])
        cwg_ref[me] = cw_ref[...]
        first = []
        for k in range(1, NDEV):
            peer, _ = _peer(x, y, c, k)
            first.append(_remote(call_ref.at[me], call_ref.at[me], send_sems.at[0, k], recv_sems.at[0, k], peer))
            first.append(_remote(cwg_ref.at[me], cwg_ref.at[me], send_sems.at[1, k], recv_sems.at[1, k], peer))
        for cp in first:
            cp.start()
        for k in range(1, NDEV):
            peer, slot = _peer(x, y, c, k)
            _remote(call_ref.at[slot], call_ref.at[slot], send_sems.at[0, k], recv_sems.at[0, k], peer).wait_recv()
            _remote(cwg_ref.at[slot], cwg_ref.at[slot], send_sems.at[1, k], recv_sems.at[1, k], peer).wait_recv()
        mod = jnp.broadcast_to(b_ref[...], (NDEV, W))
        for r in range(nblk):
            mod = mod + lax.dot_general(call_ref[:, r, :], w_ref[r * 128:(r + 1) * 128, :], (((1,), (0,)), ((), ())),
                                        preferred_element_type=f32, precision=lax.Precision.HIGHEST)
        row = lax.broadcasted_iota(jnp.int32, (NDEV, 1), 0)
        pick = lambda j: jnp.broadcast_to(jnp.sum(jnp.where(row == j, mod, 0.0), axis=0, keepdims=True), (8, W))
        mod_ref[me] = pick(me)
        second = []
        for k in range(1, NDEV):
            peer, slot = _peer(x, y, c, k)
            msend[k] = pick(slot)
            second.append(_remote(msend.at[k], mod_ref.at[me], send_sems.at[2, k], recv_sems.at[2, k], peer))
        for cp in second:
            cp.start()
        for k in range(1, NDEV):
            peer, slot = _peer(x, y, c, k)
            _remote(msend.at[k], mod_ref.at[slot], send_sems.at[2, k], recv_sems.at[2, k], peer).wait_recv()
        for cp in first + second:
            cp.wait_send()

    vmem = pl.BlockSpec(memory_space=pltpu.VMEM)
    return pl.pallas_call(
        body, name="ada_exchange",
        out_shape=(jax.ShapeDtypeStruct((NDEV, 8, W), f32), jax.ShapeDtypeStruct((NDEV, nblk, 128), f32),
                   jax.ShapeDtypeStruct((NDEV, 8, CW), f32)),
        in_specs=[vmem] * 4, out_specs=(vmem, vmem, vmem),
        scratch_shapes=[pltpu.VMEM((NDEV, 8, W), f32), pltpu.SemaphoreType.DMA((3, NDEV)),
                        pltpu.SemaphoreType.DMA((3, NDEV))],
        compiler_params=_params(),
    )(c_blk, cw_slab, w_ada, b_cols)


def _prenorm(x, scale, shift, g_pre, dep, tr=256):
    S, D = x.shape
    tr = min(tr, S)

    def body(x_ref, sc_ref, sh_ref, g_ref, dep_ref, h_ref):
        xv = x_ref[...]
        r = lax.rsqrt(jnp.mean(xv * xv, axis=-1, keepdims=True) + EPS)
        h_ref[...] = ((xv * r) * g_ref[...] * (1.0 + sc_ref[...]) + sh_ref[...]).astype(bf16)

    row = pl.BlockSpec((tr, D), lambda i: (i, 0))
    vec = pl.BlockSpec((1, D), lambda i: (0, 0))
    return pl.pallas_call(body, name="prenorm", out_shape=jax.ShapeDtypeStruct((S, D), bf16), grid=(S // tr,),
                          in_specs=[row, vec, vec, vec, ANY_SPEC], out_specs=row, compiler_params=_params("parallel"))(
                              x, scale, shift, g_pre, dep)


def _ext_rows(i, tr, S):
    g = lax.broadcasted_iota(jnp.int32, (tr + 16, 1), 0) + (i * tr - 8)
    return (g >= 0) & (g < S)


def _halo_specs(tr, S, C, col):
    nb8 = S // 8
    main = pl.BlockSpec((tr, C), lambda i: (i, col))
    prev = pl.BlockSpec((8, C), lambda i: (jnp.maximum(i * (tr // 8) - 1, 0), col))
    nxt = pl.BlockSpec((8, C), lambda i: (jnp.minimum((i + 1) * (tr // 8), nb8 - 1), col))
    return prev, main, nxt


def _conv_fwd(proj, conv_w, conv_b, g_conv, tr=256):
    S, C = proj.shape[0], proj.shape[1] // 8
    tr = min(tr, S)

    def body(up, um, un, cp, cm, cn, bg_ref, zc_ref, w_ref, cb_ref, g_ref, o_ref):
        i = pl.program_id(0)
        exists = _ext_rows(i, tr, S)
        u = jnp.concatenate([up[...], um[...], un[...]], axis=0)
        cg = jnp.concatenate([cp[...], cm[...], cn[...]], axis=0)
        t = jnp.where(exists, cg * u, 0.0)
        t_before = pltpu.roll(t, 1, 0)[8:tr + 8]
        t_after = pltpu.roll(t, tr + 15, 0)[8:tr + 8]
        w = w_ref[...]
        cv = w[0:1] * t_before + w[1:2] * t[8:tr + 8] + w[2:3] * t_after + cb_ref[...]
        yc = bg_ref[...] * cv
        rc = lax.rsqrt(jnp.mean(yc * yc, axis=-1, keepdims=True) + EPS)
        o_ref[...] = ((yc * rc) * g_ref[...] * _silu(zc_ref[...])).astype(bf16)

    u_specs = _halo_specs(tr, S, C, 0)
    c_specs = _halo_specs(tr, S, C, 2)
    vec = pl.BlockSpec((1, C), lambda i: (0, 0))
    return pl.pallas_call(
        body, name="conv_fwd", out_shape=jax.ShapeDtypeStruct((S, 2 * C), bf16), grid=(S // tr,),
        in_specs=[*u_specs, *c_specs, pl.BlockSpec((tr, C), lambda i: (i, 1)), pl.BlockSpec((tr, C), lambda i: (i, 3)),
                  pl.BlockSpec((8, C), lambda i: (0, 0)), vec, vec],
        out_specs=pl.BlockSpec((tr, C), lambda i: (i, 0)), compiler_params=_params("parallel"),
    )(proj, proj, proj, proj, proj, proj, proj, proj, conv_w, conv_b, g_conv)


def _branch_geometry(S, r, inter):
    L = S // r * inter
    nq = min(128, L)
    nk = min(nq + 2 * HALF_WIN * inter, L)
    assert L % nq == 0 and (L == nk or L >= nq + 2 * HALF_WIN * inter)
    return L, nq, nk, L // nq


QUAD = 4


def _to_quad(dst, src, S):
    n = S // QUAD
    for rho in range(QUAD):
        dst[pl.ds(rho * n, n), :] = src[pl.ds(rho, n, stride=QUAD), :]


def _block_rows(idx, r, inter, S, L, nq, nk, nblk):
    rho, qb = (0, idx) if r == 1 else (idx // nblk, idx % nblk)
    i0 = qb * nq
    ws = jnp.clip(i0 - HALF_WIN * inter, 0, L - nk)
    if r == 1:
        return pl.ds(pl.multiple_of(i0, 8), nq), pl.ds(pl.multiple_of(ws, 8), nk), i0 - ws
    assert r % (QUAD * inter) == 0
    step = r // QUAD // inter
    base = (rho % QUAD) * (S // QUAD) + rho // QUAD
    if step == 1:
        return pl.ds(pl.multiple_of(base + i0, 8), nq), pl.ds(pl.multiple_of(base + ws, 8), nk), i0 - ws
    return pl.ds(base + step * i0, nq, stride=step), pl.ds(base + step * ws, nk, stride=step), i0 - ws


N_CASES = 3
SCALE = HEAD_DIM ** -0.5
ATTN_UNROLL = 16


def _bias_shape(S):
    shapes = [_branch_geometry(S, r, inter)[1:3] for _, r, inter in BRANCHES]
    return (len(BRANCHES) * N_CASES * 2, max(nq for nq, _ in shapes), max(nk for _, nk in shapes))


def _bias_index(b, case, head):
    return (b * N_CASES + case) * 2 + head


def _fill_bias(bias_scr, sl_ref, S):
    sl = sl_ref[...]
    slope = (sl[0:1, 0:1], sl[0:1, HEAD_DIM:HEAD_DIM + 1])
    for b, (_, r, inter) in enumerate(BRANCHES):
        L, nq, nk, nblk = _branch_geometry(S, r, inter)
        rel = lax.broadcasted_iota(jnp.int32, (nq, nk), 0) - lax.broadcasted_iota(jnp.int32, (nq, nk), 1)
        for case in range(N_CASES):
            d = jnp.abs(rel + case * HALF_WIN)
            valid = d <= HALF_WIN * inter
            if inter > 1:
                valid = valid & (jnp.bitwise_and(d, inter - 1) == 0)
            dist = d.astype(f32) * float(r // inter)
            for head in range(2):
                bias_scr[_bias_index(b, case, head), 0:nq, 0:nk] = jnp.where(valid, -slope[head] * dist, NEG_INF)


def _head_slopes(n_heads):
    slopes = 2.0 ** (-8.0 * jnp.arange(1, n_heads + 1, dtype=f32) / n_heads)
    return jnp.broadcast_to(jnp.repeat(slopes.reshape(n_heads // 2, 2), HEAD_DIM, axis=1)[:, None, :],
                            (n_heads // 2, 8, PAIR))


def _attn_fwd(proj, slopes):
    S, C = proj.shape[0], proj.shape[1] // 8
    npair = C // PAIR

    def body(q_ref, k_ref, v_ref, sl_ref, o_ref, lse_ref, m_scr, l_scr, a_scr, bias_scr, q4_scr, k4_scr, v4_scr):
        lane = lax.broadcasted_iota(jnp.int32, (1, PAIR), 1)
        first = lane < HEAD_DIM
        _fill_bias(bias_scr, sl_ref, S)
        for dst, src in ((q4_scr, q_ref), (k4_scr, k_ref), (v4_scr, v_ref)):
            _to_quad(dst, src, S)

        for b, (_, r, inter) in enumerate(BRANCHES):
            L, nq, nk, nblk = _branch_geometry(S, r, inter)
            qs, ks, vs = (q_ref, k_ref, v_ref) if r == 1 else (q4_scr, k4_scr, v4_scr)

            def step(idx, carry, b=b, r=r, L=L, nq=nq, nk=nk, nblk=nblk, qs=qs, ks=ks, vs=vs):
                qrows, krows, off = _block_rows(idx, r, inter, S, L, nq, nk, nblk)
                case = off // HALF_WIN
                q2 = qs[qrows, :] * SCALE
                k2 = ks[krows, :].astype(bf16)
                v2 = vs[krows, :].astype(bf16)
                ms, accs = [], []
                for hh in range(2):
                    mine = first if hh == 0 else ~first
                    qh = jnp.where(mine, q2, 0.0).astype(bf16)
                    s = lax.dot_general(qh, k2, (((1,), (1,)), ((), ())), preferred_element_type=f32)
                    s = s + bias_scr[_bias_index(b, case, hh), 0:nq, 0:nk]
                    m = jnp.max(s, axis=-1, keepdims=True)
                    p = jnp.exp(s - m).astype(bf16)
                    vh = jnp.where(mine, v2, jnp.ones_like(v2))
                    ms.append(m)
                    accs.append(jnp.dot(p, vh, preferred_element_type=f32))
                m_scr[b, qrows, :] = jnp.where(first, ms[0], ms[1])
                a_scr[b, qrows, :] = jnp.where(first, accs[0], accs[1])
                l_scr[b, qrows, :] = jnp.where(first, accs[1], accs[0])
                return carry

            lax.fori_loop(0, S // nq, step, 0, unroll=min(ATTN_UNROLL, S // nq))

        n4 = S // QUAD
        ch = min(256, n4)
        nch = n4 // ch

        def merge(i, carry):
            rho, part = i // nch, i % nch
            sorted_rows = pl.ds(pl.multiple_of(rho * n4 + part * ch, 8), ch)
            token_rows = pl.ds(rho + QUAD * part * ch, ch, stride=QUAD)
            rows = (token_rows,) + (sorted_rows,) * (len(BRANCHES) - 1)
            ms = [m_scr[b, rows[b], :] for b in range(len(BRANCHES))]
            m = functools.reduce(jnp.maximum, ms)
            l = jnp.zeros((ch, PAIR), f32)
            acc = jnp.zeros((ch, PAIR), f32)
            for b in range(len(BRANCHES)):
                w = jnp.exp(ms[b] - m)
                l = l + w * pltpu.roll(l_scr[b, rows[b], :], HEAD_DIM, 1)
                acc = acc + w * a_scr[b, rows[b], :]
            o_ref[token_rows, :] = acc / l
            lse_ref[token_rows, :] = m + jnp.log(l)
            return carry

        lax.fori_loop(0, QUAD * nch, merge, 0)

    blk = lambda part: pl.BlockSpec((S, PAIR), lambda p: (0, part * npair + p))
    out = pl.BlockSpec((S, PAIR), lambda p: (0, p))
    return pl.pallas_call(
        body, name="attn_fwd",
        out_shape=(jax.ShapeDtypeStruct((S, C), f32), jax.ShapeDtypeStruct((S, C), f32)), grid=(npair,),
        in_specs=[blk(4), blk(5), blk(6), pl.BlockSpec((None, 8, PAIR), lambda p: (p, 0, 0))],
        out_specs=(out, out),
        scratch_shapes=[pltpu.VMEM((3, S, PAIR), f32)] * 3 + [pltpu.VMEM(_bias_shape(S), f32)]
        + [pltpu.VMEM((S, PAIR), f32)] * 3,
        compiler_params=_params("parallel"),
    )(proj, proj, proj, slopes)


def _attn_post(ycat, o, proj, g_attn, tr=256):
    S, C = o.shape
    tr = min(tr, S)

    def body(y_ref, o_ref, z_ref, g_ref, out_ref):
        del y_ref
        ov = o_ref[...]
        ra = lax.rsqrt(jnp.mean(ov * ov, axis=-1, keepdims=True) + EPS)
        out_ref[...] = ((ov * ra) * g_ref[...] * _silu(z_ref[...])).astype(bf16)

    return pl.pallas_call(
        body, name="attn_post", out_shape=jax.ShapeDtypeStruct(ycat.shape, ycat.dtype), grid=(S // tr,),
        in_specs=[HBM_SPEC, pl.BlockSpec((tr, C), lambda i: (i, 0)), pl.BlockSpec((tr, C), lambda i: (i, 7)),
                  pl.BlockSpec((1, C), lambda i: (0, 0))],
        out_specs=pl.BlockSpec((tr, C), lambda i: (i, 1)), input_output_aliases={0: 0},
        compiler_params=_params("arbitrary"),
    )(ycat, o, proj, g_attn)


def _sandwich(y, x, target, gate, g_post, tr=256):
    S, D = y.shape
    tr = min(tr, S)

    def body(y_ref, x_ref, t_ref, gate_ref, g_ref, dy_ref, dout_ref, sums_ref):
        i = pl.program_id(0)
        yv = y_ref[...]
        rp = lax.rsqrt(jnp.mean(yv * yv, axis=-1, keepdims=True) + EPS)
        yhat = yv * rp
        yn = yhat * g_ref[...]
        err = (x_ref[...] + gate_ref[...] * yn) - t_ref[...]
        dout = err * (1.0 / D)
        dout_ref[...] = dout
        dyn = dout * gate_ref[...]
        w = dyn * g_ref[...]
        dy_ref[...] = (rp * (w - yhat * jnp.mean(w * yhat, axis=-1, keepdims=True))).astype(bf16)
        loss = 0.5 * jnp.sum(jnp.mean(err * err, axis=-1, keepdims=True), axis=0, keepdims=True)
        row = lax.broadcasted_iota(jnp.int32, (8, D), 0)
        upd = jnp.where(row == 0, jnp.sum(dout * yn, axis=0, keepdims=True),
                        jnp.where(row == 1, jnp.sum(dyn * yhat, axis=0, keepdims=True),
                                  jnp.where(row == 2, loss, 0.0)))

        @pl.when(i == 0)
        def _():
            sums_ref[...] = upd

        @pl.when(i > 0)
        def _():
            sums_ref[...] += upd

    row = pl.BlockSpec((tr, D), lambda i: (i, 0))
    vec = pl.BlockSpec((1, D), lambda i: (0, 0))
    return pl.pallas_call(
        body, name="sandwich",
        out_shape=(jax.ShapeDtypeStruct((S, D), bf16), jax.ShapeDtypeStruct((S, D), f32), jax.ShapeDtypeStruct((8, D), f32)),
        grid=(S // tr,), in_specs=[row, row, row, vec, vec],
        out_specs=(row, row, pl.BlockSpec((8, D), lambda i: (0, 0))), compiler_params=_params("arbitrary"),
    )(y, x, target, gate, g_post)


def _conv_bwd(proj, dycat, conv_w, conv_b, g_conv, dep, tr=256):
    S, C = proj.shape[0], proj.shape[1] // 8
    tr = min(tr, S)
    n = tr + 16

    def body(*refs):
        ins, (w_ref, cb_ref, g_ref, _, dp_ref, sums_ref) = refs[:15], refs[15:]
        i = pl.program_id(0)
        exists = _ext_rows(i, tr, S)
        u, bg, cg, zc, dyn = (jnp.concatenate([ins[3 * t][...], ins[3 * t + 1][...], ins[3 * t + 2][...]], axis=0)
                              for t in range(5))
        w = w_ref[...]
        t = jnp.where(exists, cg * u, 0.0)
        t_before, t_after = pltpu.roll(t, 1, 0), pltpu.roll(t, n - 1, 0)
        cv = w[0:1] * t_before + w[1:2] * t + w[2:3] * t_after + cb_ref[...]
        yc = bg * cv
        rc = lax.rsqrt(jnp.mean(yc * yc, axis=-1, keepdims=True) + EPS)
        yhat = yc * rc
        sz, dsz = _silu_and_slope(zc)
        wgt = dyn * g_ref[...] * sz
        dyc = rc * (wgt - yhat * jnp.mean(wgt * yhat, axis=-1, keepdims=True))
        dcv = jnp.where(exists, dyc * bg, 0.0)
        dt = w[0:1] * pltpu.roll(dcv, n - 1, 0) + w[1:2] * dcv + w[2:3] * pltpu.roll(dcv, 1, 0)
        mid = slice(8, tr + 8)
        dp_ref[:, 0:C] = (dt * cg)[mid].astype(bf16)
        dp_ref[:, C:2 * C] = (dyc * cv)[mid].astype(bf16)
        dp_ref[:, 2 * C:3 * C] = (dt * u)[mid].astype(bf16)
        dp_ref[:, 3 * C:4 * C] = (dyn * yhat * g_ref[...] * dsz)[mid].astype(bf16)
        colsum = lambda v: jnp.sum(v[mid], axis=0, keepdims=True)
        parts = [colsum(dyn * yhat * sz), colsum(dcv), colsum(dcv * t_before), colsum(dcv * t), colsum(dcv * t_after)]
        row = lax.broadcasted_iota(jnp.int32, (8, C), 0)
        upd = jnp.zeros((8, C), f32)
        for j, pj in enumerate(parts):
            upd = jnp.where(row == j, pj, upd)

        @pl.when(i == 0)
        def _():
            sums_ref[...] = upd

        @pl.when(i > 0)
        def _():
            sums_ref[...] += upd

    specs = []
    for col in range(4):
        specs += _halo_specs(tr, S, C, col)
    specs += _halo_specs(tr, S, C, 0)
    vec = pl.BlockSpec((1, C), lambda i: (0, 0))
    return pl.pallas_call(
        body, name="conv_bwd",
        out_shape=(jax.ShapeDtypeStruct((S, 4 * C), bf16), jax.ShapeDtypeStruct((8, C), f32)), grid=(S // tr,),
        in_specs=[*specs, pl.BlockSpec((8, C), lambda i: (0, 0)), vec, vec, ANY_SPEC],
        out_specs=(pl.BlockSpec((tr, 4 * C), lambda i: (i, 0)), pl.BlockSpec((8, C), lambda i: (0, 0))),
        compiler_params=_params("arbitrary"),
    )(*([proj] * 12), dycat, dycat, dycat, conv_w, conv_b, g_conv, dep)


def _attn_post_bwd(o, proj, dycat, g_attn, dep, tr=256):
    S, C = o.shape
    tr = min(tr, S)

    def body(o_ref, z_ref, dy_ref, g_ref, dep_ref, do_ref, dz_ref, sums_ref):
        i = pl.program_id(0)
        ov, zv, dyn = o_ref[...], z_ref[...], dy_ref[...]
        ra = lax.rsqrt(jnp.mean(ov * ov, axis=-1, keepdims=True) + EPS)
        ohat = ov * ra
        sz, dsz = _silu_and_slope(zv)
        wgt = dyn * g_ref[...] * sz
        do_ref[...] = ra * (wgt - ohat * jnp.mean(wgt * ohat, axis=-1, keepdims=True))
        dz_ref[...] = (dyn * ohat * g_ref[...] * dsz).astype(bf16)
        row = lax.broadcasted_iota(jnp.int32, (8, C), 0)
        upd = jnp.where(row == 0, jnp.sum(dyn * ohat * sz, axis=0, keepdims=True), 0.0)

        @pl.when(i == 0)
        def _():
            sums_ref[...] = upd

        @pl.when(i > 0)
        def _():
            sums_ref[...] += upd

    return pl.pallas_call(
        body, name="attn_post_bwd",
        out_shape=(jax.ShapeDtypeStruct((S, C), f32), jax.ShapeDtypeStruct((4, S, C), bf16),
                   jax.ShapeDtypeStruct((8, C), f32)),
        grid=(S // tr,),
        in_specs=[pl.BlockSpec((tr, C), lambda i: (i, 0)), pl.BlockSpec((tr, C), lambda i: (i, 7)),
                  pl.BlockSpec((tr, C), lambda i: (i, 1)), pl.BlockSpec((1, C), lambda i: (0, 0)), ANY_SPEC],
        out_specs=(pl.BlockSpec((tr, C), lambda i: (i, 0)), pl.BlockSpec((None, tr, C), lambda i: (3, i, 0)),
                   pl.BlockSpec((8, C), lambda i: (0, 0))),
        compiler_params=_params("arbitrary"),
    )(o, proj, dycat, g_attn, dep)


def _attn_bwd(proj, o, do, lse, slopes, dqkvz, dep):
    S, C = o.shape
    npair = C // PAIR

    def body(q_ref, k_ref, v_ref, o_ref, do_ref, lse_ref, sl_ref, old_ref, dep_ref, dqkv_ref,
             acc_scr, dl_scr, quad_scr, bias_scr):
        lane = lax.broadcasted_iota(jnp.int32, (1, PAIR), 1)
        first = lane < HEAD_DIM
        _fill_bias(bias_scr, sl_ref, S)
        ch = min(256, S)

        def prep(i, carry):
            rows = pl.ds(pl.multiple_of(i * ch, 8), ch)
            prod = do_ref[rows, :] * o_ref[rows, :]
            d0 = jnp.sum(jnp.where(first, prod, 0.0), axis=-1, keepdims=True)
            d1 = jnp.sum(jnp.where(first, 0.0, prod), axis=-1, keepdims=True)
            dl_scr[rows, :] = jnp.where(first, d0, d1)
            zero = jnp.zeros((ch, PAIR), f32)
            for order in range(2):
                for t in range(3):
                    acc_scr[order, t, rows, :] = zero
            return carry

        lax.fori_loop(0, S // ch, prep, 0)
        token_srcs = (q_ref, k_ref, v_ref, do_ref, lse_ref, dl_scr)
        for j, src in enumerate(token_srcs):
            _to_quad(quad_scr.at[j], src, S)

        for b, (_, r, inter) in enumerate(BRANCHES):
            L, nq, nk, nblk = _branch_geometry(S, r, inter)
            order = 0 if r == 1 else 1
            srcs = token_srcs if r == 1 else tuple(quad_scr.at[j] for j in range(6))

            def step(idx, carry, b=b, r=r, L=L, nq=nq, nk=nk, nblk=nblk, order=order, srcs=srcs):
                qs, ks, vs, dos, lses, dls = srcs
                dq_scr, dk_scr, dv_scr = (acc_scr.at[order, t] for t in range(3))
                qrows, krows, off = _block_rows(idx, r, inter, S, L, nq, nk, nblk)
                case = off // HALF_WIN
                q2 = qs[qrows, :] * SCALE
                k2 = ks[krows, :].astype(bf16)
                v2 = vs[krows, :].astype(bf16)
                do2 = dos[qrows, :]
                lse2 = lses[qrows, :]
                dl2 = dls[qrows, :]
                dq2 = jnp.zeros((nq, PAIR), f32)
                dk2 = jnp.zeros((nk, PAIR), f32)
                dv2 = jnp.zeros((nk, PAIR), f32)
                for hh in range(2):
                    mine = first if hh == 0 else ~first
                    lo = hh * HEAD_DIM
                    qh = jnp.where(mine, q2, 0.0).astype(bf16)
                    doh = jnp.where(mine, do2, 0.0).astype(bf16)
                    s = lax.dot_general(qh, k2, (((1,), (1,)), ((), ())), preferred_element_type=f32)
                    s = s + bias_scr[_bias_index(b, case, hh), 0:nq, 0:nk]
                    p = jnp.exp(s - lse2[:, lo:lo + 1])
                    dv2 = dv2 + lax.dot_general(p.astype(bf16), doh, (((0,), (0,)), ((), ())), preferred_element_type=f32)
                    dp = lax.dot_general(doh, v2, (((1,), (1,)), ((), ())), preferred_element_type=f32)
                    ds = (p * (dp - dl2[:, lo:lo + 1])).astype(bf16)
                    dq2 = dq2 + jnp.where(mine, jnp.dot(ds, k2, preferred_element_type=f32), 0.0)
                    dk2 = dk2 + lax.dot_general(ds, qh, (((0,), (0,)), ((), ())), preferred_element_type=f32)
                dq_scr[qrows, :] = dq_scr[qrows, :] + dq2
                dk_scr[krows, :] = dk_scr[krows, :] + dk2
                dv_scr[krows, :] = dv_scr[krows, :] + dv2
                return carry

            lax.fori_loop(0, S // nq, step, 0, unroll=min(ATTN_UNROLL, S // nq))

        n4 = S // QUAD
        for t in range(3):
            for rho in range(QUAD):
                token_rows = pl.ds(rho, n4, stride=QUAD)
                acc_scr[0, t, token_rows, :] = acc_scr[0, t, token_rows, :] + acc_scr[1, t, pl.ds(rho * n4, n4), :]
        dqkv_ref[0] = (acc_scr[0, 0] * SCALE).astype(bf16)
        dqkv_ref[1] = acc_scr[0, 1].astype(bf16)
        dqkv_ref[2] = acc_scr[0, 2].astype(bf16)

    blk = lambda part: pl.BlockSpec((S, PAIR), lambda p: (0, part * npair + p))
    own = pl.BlockSpec((S, PAIR), lambda p: (0, p))
    return pl.pallas_call(
        body, name="attn_bwd", out_shape=jax.ShapeDtypeStruct(dqkvz.shape, dqkvz.dtype), grid=(npair,),
        in_specs=[blk(4), blk(5), blk(6), own, own, own, pl.BlockSpec((None, 8, PAIR), lambda p: (p, 0, 0)),
                  ANY_SPEC, ANY_SPEC],
        out_specs=pl.BlockSpec((3, S, PAIR), lambda p: (0, 0, p)), input_output_aliases={7: 0},
        scratch_shapes=[pltpu.VMEM((2, 3, S, PAIR), f32), pltpu.VMEM((S, PAIR), f32), pltpu.VMEM((6, S, PAIR), f32),
                        pltpu.VMEM(_bias_shape(S), f32)],
        compiler_params=_params("arbitrary"),
    )(proj, proj, proj, o, do, lse, slopes, dqkvz, dep)


def _prenorm_bwd(dh, x, dout, scale, g_pre, tr=256):
    S, D = x.shape
    tr = min(tr, S)

    def body(dh_ref, x_ref, dout_ref, sc_ref, g_ref, gx_ref, sums_ref):
        i = pl.program_id(0)
        xv, dhv = x_ref[...], dh_ref[...]
        r = lax.rsqrt(jnp.mean(xv * xv, axis=-1, keepdims=True) + EPS)
        xn = xv * r
        dxn = dhv * (g_ref[...] * (1.0 + sc_ref[...]))
        gx_ref[...] = dout_ref[...] + r * (dxn - xn * jnp.mean(dxn * xn, axis=-1, keepdims=True))
        dhx = dhv * xn
        row = lax.broadcasted_iota(jnp.int32, (8, D), 0)
        upd = jnp.where(row == 0, jnp.sum(dhv, axis=0, keepdims=True),
                        jnp.where(row == 1, jnp.sum(dhx, axis=0, keepdims=True) * g_ref[...],
                                  jnp.where(row == 2, jnp.sum(dhx, axis=0, keepdims=True) * (1.0 + sc_ref[...]), 0.0)))

        @pl.when(i == 0)
        def _():
            sums_ref[...] = upd

        @pl.when(i > 0)
        def _():
            sums_ref[...] += upd

    row = pl.BlockSpec((tr, D), lambda i: (i, 0))
    vec = pl.BlockSpec((1, D), lambda i: (0, 0))
    return pl.pallas_call(
        body, name="prenorm_bwd",
        out_shape=(jax.ShapeDtypeStruct((S, D), f32), jax.ShapeDtypeStruct((8, D), f32)), grid=(S // tr,),
        in_specs=[row, row, row, vec, vec], out_specs=(row, pl.BlockSpec((8, D), lambda i: (0, 0))),
        compiler_params=_params("arbitrary"),
    )(dh, x, dout, scale, g_pre)


def _adamw(w, g, m, v):
    m = ADAM_B1 * m + (1.0 - ADAM_B1) * g
    v = ADAM_B2 * v + (1.0 - ADAM_B2) * (g * g)
    m_hat = m / (1.0 - ADAM_B1 ** ADAM_STEP)
    v_hat = v / (1.0 - ADAM_B2 ** ADAM_STEP)
    delta = -ADAM_LR * (m_hat / (jnp.sqrt(v_hat) + ADAM_EPS) + ADAM_WD * w)
    return delta, m, v


def _sum_rows(parts, dep):
    P = parts.shape[1]

    def body(p_ref, dep_ref, o_ref):
        acc = p_ref[0:1, :]
        for j in range(1, NDEV):
            acc = acc + p_ref[j:j + 1, :]
        o_ref[...] = jnp.broadcast_to(acc, (8, P))

    vmem = pl.BlockSpec(memory_space=pltpu.VMEM)
    return pl.pallas_call(body, name="sum_small", out_shape=jax.ShapeDtypeStruct((8, P), f32),
                          in_specs=[vmem, ANY_SPEC], out_specs=vmem, compiler_params=_params())(parts, dep)


def _adamw_small(tot, params):
    given = [p[3] for p in params if not isinstance(p[3], int)]

    def body(tot_ref, *refs):
        given_refs = list(refs[:len(given)])
        ins = refs[len(given):len(given) + 3 * len(params)]
        outs = refs[len(given) + 3 * len(params):]
        for t, (w, _, _, where) in enumerate(params):
            w_ref, m_ref, v_ref = ins[3 * t:3 * t + 3]
            g = tot_ref[0:1, where:where + w.size] if isinstance(where, int) else given_refs.pop(0)[...]
            outs[4 * t][...] = g
            outs[4 * t + 1][...], outs[4 * t + 2][...], outs[4 * t + 3][...] = _adamw(w_ref[...], g, m_ref[...], v_ref[...])

    out_shape = tuple(jax.ShapeDtypeStruct(p[0].shape, f32) for p in params for _ in range(4))
    res = pl.pallas_call(body, name="adamw_small", out_shape=out_shape, compiler_params=_params())(
        tot, *given, *[a for p in params for a in p[:3]])
    return [res[4 * t:4 * t + 4] for t in range(len(params))]


def _adamw_sharded(name, parts, sums_a, sums_b, pick, w, m, v, rows=None, prev=None, tr=128):
    R, Cc = w.shape
    r0, nr = rows or (0, R)
    tr = math.gcd(tr, r0, nr)
    n, b0 = parts.shape[0], r0 // tr

    def body(pick_ref, p_ref, a_ref, b_ref, w_ref, m_ref, v_ref, *rest):
        g_ref, d_ref, nm_ref, nv_ref = rest[-4:]
        g = jnp.where(pick_ref[0] == 1, b_ref[...], a_ref[...]).astype(f32)
        for j in range(n):
            g = g + p_ref[j].astype(f32)
        g_ref[...] = g
        d_ref[...], nm_ref[...], nv_ref[...] = _adamw(w_ref[...], g, m_ref[...], v_ref[...])

    row = pl.BlockSpec((tr, Cc), lambda i, pick: (i + b0, 0))
    mine = pl.BlockSpec((None, tr, Cc), lambda i, pick: (pick[1], i + b0, 0))
    out = jax.ShapeDtypeStruct((R, Cc), f32)
    prev = list(prev or [])
    grid_spec = pltpu.PrefetchScalarGridSpec(
        num_scalar_prefetch=1, grid=(nr // tr,),
        in_specs=[pl.BlockSpec((n, tr, Cc), lambda i, pick: (0, i + b0, 0)), mine, mine, row, row, row]
        + [ANY_SPEC] * len(prev),
        out_specs=(row, row, row, row))
    return pl.pallas_call(
        body, name=name, out_shape=(out, out, out, out), grid_spec=grid_spec,
        input_output_aliases={7 + t: t for t in range(len(prev))}, compiler_params=_params("arbitrary"),
    )(pick, parts, sums_a, sums_b, w, m, v, *prev)


def _adamw_ada(c_t, dmod_cols, w, m, v, dep, tr=512):
    D, W = w.shape
    tr = min(tr, D)

    def body(c_ref, dm_ref, w_ref, m_ref, v_ref, dep_ref, g_ref, d_ref, nm_ref, nv_ref):
        cv, dm = c_ref[...], dm_ref[...]
        g = cv[:, 0:1] * dm[0:1, :]
        for b in range(1, NDEV):
            g = g + cv[:, b:b + 1] * dm[b:b + 1, :]
        g_ref[...] = g
        d_ref[...], nm_ref[...], nv_ref[...] = _adamw(w_ref[...], g, m_ref[...], v_ref[...])

    row = pl.BlockSpec((tr, W), lambda i: (i, 0))
    out = jax.ShapeDtypeStruct((D, W), f32)
    return pl.pallas_call(
        body, name="adamw_ada", out_shape=(out, out, out, out), grid=(D // tr,),
        in_specs=[pl.BlockSpec((tr, NDEV), lambda i: (i, 0)), pl.BlockSpec((NDEV, W), lambda i: (0, 0)), row, row, row,
                  ANY_SPEC],
        out_specs=(row, row, row, row), compiler_params=_params("parallel"),
    )(c_t, dmod_cols, w, m, v, dep)


def kernel(x, c, w_ada, b_ada, g_pre, w_in, conv_w, conv_b, g_conv, g_attn, w_out, g_post, loss_target, m_w_ada, m_b_ada, m_g_pre, m_w_in, m_conv_w, m_conv_b, m_g_conv, m_g_attn, m_w_out, m_g_post, v_w_ada, v_b_ada, v_g_pre, v_w_in, v_conv_w, v_conv_b, v_g_conv, v_g_attn, v_w_out, v_g_post):
    S, D = x.shape[1], x.shape[2]
    C = D // 2
    W = w_ada.shape[2]
    CW = conv_w.shape[2]
    me = 4 * lax.axis_index("x") + 2 * lax.axis_index("y") + lax.axis_index("c")
    x2, tgt = x[0], loss_target[0]
    w_ada2, w_in2, w_out2 = w_ada[0], w_in[0], w_out[0]

    R = D // NDEV
    core = lax.axis_index("c").astype(jnp.int32).reshape(1)

    cw_slab = jnp.zeros((8, CW), f32).at[:3].set(conv_w[0])
    b_cols = lax.dynamic_slice_in_dim(b_ada, me * W, W, axis=1)
    mod_slabs, c_blocks, cw_g = _ada_exchange(c.reshape(D // 128, 128), cw_slab, w_ada2, b_cols)
    c_all = c_blocks.reshape(NDEV, D)
    conv_w_full = jnp.transpose(cw_g, (1, 0, 2)).reshape(8, C)
    mod = mod_slabs[:, 0, :].reshape(1, 3 * D)
    shift, scale, gate = mod[:, :D], mod[:, D:2 * D], mod[:, 2 * D:]

    land_i = lax.dynamic_update_slice(lax.empty((NDEV, D, C), bf16), w_in2.astype(bf16)[None], (me, 0, 0))
    land_o = lax.dynamic_update_slice(lax.empty((NDEV, R, D), bf16), w_out2.astype(bf16)[None], (me, 0, 0))
    wi_send, wi_recv, land_i, w_token = _w_in_start(land_i, [mod_slabs])

    me_arr = me.astype(jnp.int32).reshape(1)
    h = _prenorm(x2, scale, shift, g_pre, w_token)
    land_i = _w_in_sibling(land_i, wi_recv, after=[h])
    proj = _in_proj_part("in_proj_a", h, land_i, None, me_arr, 0, 1, 2)
    fi_send, fi_recv, land_i = _w_in_relay(land_i, wi_recv, after=[proj])
    proj = _in_proj_part("in_proj_b", h, land_i, proj, me_arr, 2, 2, 2)
    land_i = _w_in_forwarded(land_i, fi_recv, after=[proj])
    proj = _in_proj_part("in_proj_c", h, land_i, proj, me_arr, 3, 2, 2)
    (di_send, di_recv, wo_send, wo_recv), land_i, land_o = _w_in_diag(land_i, land_o, fi_recv, after=[proj])
    proj = _in_proj_part("in_proj_d", h, land_i, proj, me_arr, 6, 1, 1)
    win_g = _w_in_finish(land_i, wi_send, fi_send, di_send, di_recv, after=[proj])
    proj = _in_proj_part("in_proj_e", h, win_g, proj, me_arr, 7, 1, 1)
    slopes = _head_slopes(C // HEAD_DIM)
    ycat = _conv_fwd(proj, conv_w_full, conv_b, g_conv)
    o, lse = _attn_fwd(proj, slopes)
    (fo_send, fo_recv), (land_o,), _ = _weights_forward("w_out_forward", land_o, wo_recv, after=[o])
    ycat = _attn_post(ycat, o, proj, g_attn)
    wout_g = _weights_wait("w_out_wait", land_o, wo_send, wo_recv, fo_send, fo_recv, after=[ycat])
    wout_full = wout_g.reshape(D, D)
    y = _matmul(ycat, wout_full, name="out_proj", out_dtype=f32)
    dy, dout, post_sums = _sandwich(y, x2, tgt, gate, g_post)

    chip = me // 2

    def landing(rows, cols):
        return lax.dynamic_update_slice(lax.empty((NCHIP, rows, cols), bf16), jnp.zeros((1, rows, cols), bf16),
                                        (chip, 0, 0))

    gw_out = _matmul(ycat, dy, name="out_proj_dw", out_dtype=bf16, ta=True).reshape(NDEV, R, D)
    po_send, po_recv, gw_out, pair_o, po_token = _pair_start("g_out_pair_start", gw_out)
    dycat = _matmul(dy, wout_full, name="out_proj_dx", out_dtype=f32, tb=True, dep=po_token)
    gw_out, pair_o = _pair_wait("g_out_pair_wait", gw_out, pair_o, po_send, po_recv, after=[dycat])
    sum_o = _pair_sum("g_out_pair_sum", gw_out, pair_o, core)
    co_send, co_recv, sum_o, land_go, co_token = _chip_start(
        "g_out_chip_start", sum_o, landing(R, D), 0)
    dpc, conv_sums = _conv_bwd(proj, dycat, conv_w_full, conv_b, g_conv, co_token)
    gw_c = _matmul(h, dpc, name="in_proj_dw_conv", out_dtype=bf16, ta=True, out_slots=4)
    pc_send, pc_recv, gw_c, pair_c, pc_token = _pair_start("g_conv_pair_start", gw_c)
    do, dpa, attn_sums = _attn_post_bwd(o, proj, dycat, g_attn, pc_token)
    gw_c, pair_c = _pair_wait("g_conv_pair_wait", gw_c, pair_c, pc_send, pc_recv, after=[do])
    sum_c = _pair_sum("g_conv_pair_sum", gw_c, pair_c, core)
    cc_send, cc_recv, sum_c, land_gi, cc_token = _chip_start(
        "g_conv_chip_start", sum_c, landing(D, C), 0)
    dpa = _attn_bwd(proj, o, do, lse, slopes, dpa, cc_token)
    gw_a = _matmul(h, dpa, name="in_proj_dw_attn", out_dtype=bf16, ta=True, b_slots=True, out_slots=4)
    pa_send, pa_recv, gw_a, pair_a, pa_token = _pair_start("g_attn_pair_start", gw_a)
    sum_o, land_go = _chip_wait("g_out_chip_wait", sum_o, land_go, co_send, co_recv, 0, after=[pa_token])
    pick_out = jnp.stack([jnp.int32(0), me // 2]).astype(jnp.int32)
    g_w_out, d_w_out, nm_w_out, nv_w_out = _adamw_sharded(
        "adamw_w_out", land_go, sum_o, sum_o, pick_out, w_out2, m_w_out[0], v_w_out[0])
    gw_a, pair_a = _pair_wait("g_attn_pair_wait", gw_a, pair_a, pa_send, pa_recv, after=[g_w_out])
    sum_a = _pair_sum("g_attn_pair_sum", gw_a, pair_a, core)
    part_a, part_b = (0, 3 * D // 4), (3 * D // 4, D // 4)
    ca_send, ca_recv, sum_a, land_gi, ca_token = _chip_start("g_attn_chip_start_a", sum_a, land_gi, 4, part_a)
    dh = _matmul_slabs_t(dpc, dpa, win_g, name="in_proj_dx", dep=ca_token)
    grad_x, pre_sums = _prenorm_bwd(dh, x2, dout, scale, g_pre)

    small = jnp.concatenate([pre_sums[0:1], pre_sums[1:2], post_sums[0:1],
                             pre_sums[2:3], post_sums[1:2],
                             conv_sums[2:3], conv_sums[3:4], conv_sums[4:5],
                             conv_sums[1:2], conv_sums[0:1], attn_sums[0:1]], axis=1)
    small = jnp.concatenate([small.reshape(8 * D // 128, 128), jnp.broadcast_to(post_sums[2:3, :128], (8, 128))])
    (small_all,) = _all_gather([small], "gather_small")
    cb_send, cb_recv, sum_a, land_gi, cb_token = _chip_start("g_attn_chip_start_b", sum_a, land_gi, 4, part_b,
                                                             after=[small_all])
    small_all = small_all.reshape(NDEV, small.size)
    tot = _sum_rows(small_all, cb_token)
    loss = tot[0, 8 * D]
    g_conv_w = lax.dynamic_slice_in_dim(tot[0:1, 5 * D:5 * D + 3 * C].reshape(1, 3, C), me * CW, CW, axis=2)
    ((g_b_ada, d_b_ada, nm_b_ada, nv_b_ada), (g_g_pre, d_g_pre, nm_g_pre, nv_g_pre),
     (g_g_post, d_g_post, nm_g_post, nv_g_post), (g_conv_w, d_conv_w, nm_conv_w, nv_conv_w),
     (g_conv_b, d_conv_b, nm_conv_b, nv_conv_b), (g_g_conv, d_g_conv, nm_g_conv, nv_g_conv),
     (g_g_attn, d_g_attn, nm_g_attn, nv_g_attn)) = _adamw_small(tot, [
         (b_ada, m_b_ada, v_b_ada, 0), (g_pre, m_g_pre, v_g_pre, 3 * D), (g_post, m_g_post, v_g_post, 4 * D),
         (conv_w, m_conv_w, v_conv_w, g_conv_w), (conv_b, m_conv_b, v_conv_b, 5 * D + 3 * C),
         (g_conv, m_g_conv, v_g_conv, 5 * D + 4 * C), (g_attn, m_g_attn, v_g_attn, 5 * D + 5 * C)])

    dmod_cols = lax.dynamic_slice_in_dim(small_all[:, :3 * D], me * W, W, axis=1)
    g_w_ada, d_w_ada, nm_w_ada, nv_w_ada = _adamw_ada(c_all.T, dmod_cols, w_ada2, m_w_ada[0], v_w_ada[0], cb_token)

    pick_in = jnp.stack([me // 4, (me % 4) // 2]).astype(jnp.int32)
    sum_c, land_gi = _chip_wait("g_conv_chip_wait", sum_c, land_gi, cc_send, cc_recv, 0, after=[g_w_ada])
    sum_a, land_gi = _chip_wait("g_attn_chip_wait_a", sum_a, land_gi, ca_send, ca_recv, 4, [g_w_ada], part_a)
    first = _adamw_sharded("adamw_w_in_a", land_gi, sum_c, sum_a, pick_in, w_in2, m_w_in[0], v_w_in[0], rows=part_a,
                           tr=256)
    sum_a, land_gi = _chip_wait("g_attn_chip_wait_b", sum_a, land_gi, cb_send, cb_recv, 4, [first[0]], part_b)
    g_w_in, d_w_in, nm_w_in, nv_w_in = _adamw_sharded(
        "adamw_w_in_b", land_gi, sum_c, sum_a, pick_in, w_in2, m_w_in[0], v_w_in[0], rows=part_b, prev=first, tr=256)

    return (loss, grad_x[None],
            g_w_ada[None], g_b_ada, g_g_pre, g_w_in[None], g_conv_w, g_conv_b, g_g_conv, g_g_attn, g_w_out[None], g_g_post,
            d_w_ada[None], d_b_ada, d_g_pre, d_w_in[None], d_conv_w, d_conv_b, d_g_conv, d_g_attn, d_w_out[None], d_g_post,
            nm_w_ada[None], nm_b_ada, nm_g_pre, nm_w_in[None], nm_conv_w, nm_conv_b, nm_g_conv, nm_g_attn, nm_w_out[None], nm_g_post,
            nv_w_ada[None], nv_b_ada, nv_g_pre, nv_w_in[None], nv_conv_w, nv_conv_b, nv_g_conv, nv_g_attn, nv_w_out[None], nv_g_post)
```

```python
import functools
import math

import jax
import jax.numpy as jnp
from jax import lax
from jax.experimental import pallas as pl
from jax.experimental.pallas import tpu as pltpu

f32 = jnp.float32
bf16 = jnp.bfloat16

NDEV = 8
HEAD_DIM = 64
PAIR = 2 * HEAD_DIM
BRANCHES = ((128, 1, 1), (512, 4, 1), (2048, 16, 2))
HALF_WIN = 64
EPS = 1e-6
NEG_INF = -1e30
ADAM_LR, ADAM_B1, ADAM_B2, ADAM_EPS, ADAM_WD, ADAM_STEP = 0.001, 0.9, 0.999, 1e-08, 0.01, 10
MESH = pl.DeviceIdType.MESH
VMEM_LIMIT = 56 * 1024 * 1024
HBM_SPEC = pl.BlockSpec(memory_space=pltpu.HBM)
ANY_SPEC = pl.BlockSpec(memory_space=pl.ANY)
SEM_SPEC = pl.BlockSpec(memory_space=pltpu.SEMAPHORE)


def _params(*sem):
    return pltpu.CompilerParams(dimension_semantics=sem or None, vmem_limit_bytes=VMEM_LIMIT)


def _silu(z):
    return z * jax.nn.sigmoid(z)


def _silu_and_slope(z):
    s = jax.nn.sigmoid(z)
    return z * s, s * (1.0 + z * (1.0 - s))


def _my_place():
    x, y, c = lax.axis_index("x"), lax.axis_index("y"), lax.axis_index("c")
    return x, y, c, 4 * x + 2 * y + c


def _peer(x, y, c, k):
    px, py, pc = x ^ (k >> 2 & 1), y ^ (k >> 1 & 1), c ^ (k & 1)
    return (px, py, pc), 4 * px + 2 * py + pc


def _all_gather(arrays, name):
    n = len(arrays)

    def body(*refs):
        srcs, dsts = refs[:n], refs[n:2 * n]
        send_sems, recv_sems, local_sems = refs[2 * n:]
        x, y, c, me = _my_place()
        locals_, sends = [], []
        for t in range(n):
            own = pltpu.make_async_copy(srcs[t], dsts[t].at[me], local_sems.at[t])
            own.start()
            locals_.append(own)
            for k in range(1, NDEV):
                peer, pidx = _peer(x, y, c, k)
                cp = pltpu.make_async_remote_copy(
                    src_ref=srcs[t], dst_ref=dsts[t].at[me], send_sem=send_sems.at[t, k],
                    recv_sem=recv_sems.at[t, k], device_id=peer, device_id_type=MESH)
                cp.start()
                sends.append(cp)
        for t in range(n):
            for k in range(1, NDEV):
                peer, pidx = _peer(x, y, c, k)
                pltpu.make_async_remote_copy(
                    src_ref=srcs[t], dst_ref=dsts[t].at[pidx], send_sem=send_sems.at[t, k],
                    recv_sem=recv_sems.at[t, k], device_id=peer, device_id_type=MESH).wait_recv()
        for cp in sends:
            cp.wait_send()
        for cp in locals_:
            cp.wait()

    return pl.pallas_call(
        body, name=name,
        out_shape=tuple(jax.ShapeDtypeStruct((NDEV,) + a.shape, a.dtype) for a in arrays),
        in_specs=[HBM_SPEC] * n, out_specs=tuple([HBM_SPEC] * n),
        scratch_shapes=[pltpu.SemaphoreType.DMA((n, NDEV)), pltpu.SemaphoreType.DMA((n, NDEV)),
                        pltpu.SemaphoreType.DMA((n,))],
    )(*arrays)


def _comm_call(name, arrays, sems, new_sems, body, after=(), token=False):
    na, ns, nn, nf = len(arrays), len(sems), len(new_sems), len(after)

    def kern(*refs):
        ins, outs = refs[:na + ns + nf], refs[na + ns + nf:]
        body(ins[:na], ins[na:na + ns], outs[:nn])
        if token:
            outs[nn + na][...] = jnp.zeros((8, 128), f32)

    out_shape = ([pltpu.SemaphoreType.DMA(s) for s in new_sems] + [pltpu.HBM(a.shape, a.dtype) for a in arrays]
                 + ([jax.ShapeDtypeStruct((8, 128), f32)] if token else []))
    out_specs = [SEM_SPEC] * nn + [HBM_SPEC] * na + ([pl.BlockSpec(memory_space=pltpu.VMEM)] if token else [])
    res = pl.pallas_call(
        kern, name=name, out_shape=tuple(out_shape),
        in_specs=[HBM_SPEC] * na + [SEM_SPEC] * ns + [ANY_SPEC] * nf, out_specs=tuple(out_specs),
        input_output_aliases={t: nn + t for t in range(na)},
        compiler_params=pltpu.CompilerParams(has_side_effects=pltpu.SideEffectType.DATAFLOW_SIDE_EFFECTING),
    )(*[pltpu.with_memory_space_constraint(a, pltpu.HBM) for a in arrays], *sems, *after)
    return list(res[:nn]), list(res[nn:nn + na]), (res[nn + na] if token else None)


def _remote(src, dst, send_sem, recv_sem, device):
    return pltpu.make_async_remote_copy(src_ref=src, dst_ref=dst, send_sem=send_sem, recv_sem=recv_sem,
                                        device_id=device, device_id_type=MESH)


SAME_CORE = (2, 4, 6)
VIA_SIBLING = (3, 5, 7)


def _weights_forward(name, land, recv, after):
    def body(a, s, new):
        (land,), (recv,), (fsend, frecv) = a, s, new
        x, y, c, me = _my_place()
        sibling, _ = _peer(x, y, c, 1)
        for k in SAME_CORE:
            peer, slot = _peer(x, y, c, k)
            _remote(land.at[slot], land.at[slot], fsend.at[k], recv.at[k], peer).wait_recv()
            _remote(land.at[slot], land.at[slot], fsend.at[k], frecv.at[k ^ 1], sibling).start()

    return _comm_call(name, [land], [recv], [(NDEV,), (NDEV,)], body, after=after)


def _weights_wait(name, land, send, recv, fsend, frecv, after):
    def body(a, s, new):
        (land,), (send, recv, fsend, frecv) = a, s
        x, y, c, me = _my_place()
        sibling, sib_slot = _peer(x, y, c, 1)
        _remote(land.at[sib_slot], land.at[sib_slot], send.at[1], recv.at[1], sibling).wait_recv()
        for k in VIA_SIBLING:
            _, slot = _peer(x, y, c, k)
            _remote(land.at[slot], land.at[slot], fsend.at[k ^ 1], frecv.at[k], sibling).wait_recv()
        for k in (1,) + SAME_CORE:
            peer, _ = _peer(x, y, c, k)
            _remote(land.at[me], land.at[me], send.at[k], recv.at[k], peer).wait_send()
        for k in SAME_CORE:
            _, slot = _peer(x, y, c, k)
            _remote(land.at[slot], land.at[slot], fsend.at[k], frecv.at[k ^ 1], sibling).wait_send()

    return _comm_call(name, [land], [send, recv, fsend, frecv], [], body, after=after)[1][0]


def _diag_relay(x, y, c):
    slot = 4 * (x ^ (1 - c)) + 2 * (y ^ c) + c
    return slot, (x ^ c, y ^ (1 - c), c)


def _w_in_start(land, after):
    def body(a, s, new):
        (land,), (send, recv) = a, new
        x, y, c, me = _my_place()
        for k in (1, 2, 4):
            peer, _ = _peer(x, y, c, k)
            _remote(land.at[me], land.at[me], send.at[k], recv.at[k], peer).start()

    (send, recv), (land,), token = _comm_call("w_in_start", [land], [], [(NDEV,), (NDEV,)], body, after=after, token=True)
    return send, recv, land, token


def _w_in_sibling(land, recv, after):
    def body(a, s, new):
        (land,), (recv,) = a, s
        x, y, c, me = _my_place()
        sibling, slot = _peer(x, y, c, 1)
        _remote(land.at[slot], land.at[slot], recv.at[1], recv.at[1], sibling).wait_recv()

    return _comm_call("w_in_sibling", [land], [recv], [], body, after=after)[1][0]


def _w_in_relay(land, recv, after):
    def body(a, s, new):
        (land,), (recv,), (fsend, frecv) = a, s, new
        x, y, c, me = _my_place()
        sibling, _ = _peer(x, y, c, 1)
        for k in (2, 4):
            peer, slot = _peer(x, y, c, k)
            _remote(land.at[slot], land.at[slot], fsend.at[k], recv.at[k], peer).wait_recv()
        slot, target = _diag_relay(x, y, c)
        _remote(land.at[slot], land.at[slot], fsend.at[6], frecv.at[6], target).start()
        for k in (2, 4):
            _, slot = _peer(x, y, c, k)
            _remote(land.at[slot], land.at[slot], fsend.at[k], frecv.at[k ^ 1], sibling).start()

    (fsend, frecv), (land,), _ = _comm_call("w_in_relay", [land], [recv], [(NDEV,), (NDEV,)], body, after=after)
    return fsend, frecv, land


def _w_in_forwarded(land, frecv, after):
    def body(a, s, new):
        (land,), (frecv,) = a, s
        x, y, c, me = _my_place()
        sibling, _ = _peer(x, y, c, 1)
        for k in (3, 5):
            _, slot = _peer(x, y, c, k)
            _remote(land.at[slot], land.at[slot], frecv.at[k], frecv.at[k], sibling).wait_recv()

    return _comm_call("w_in_forwarded", [land], [frecv], [], body, after=after)[1][0]


def _w_in_diag(land, land_o, frecv, after):
    def body(a, s, new):
        (land, land_o), (frecv,), (dsend, drecv, osend, orecv) = a, s, new
        x, y, c, me = _my_place()
        sibling, _ = _peer(x, y, c, 1)
        peer, slot = _peer(x, y, c, 6)
        _remote(land.at[slot], land.at[slot], dsend.at[6], frecv.at[6], peer).wait_recv()
        _remote(land.at[slot], land.at[slot], dsend.at[6], drecv.at[7], sibling).start()
        for k in (1,) + SAME_CORE:
            peer, _ = _peer(x, y, c, k)
            _remote(land_o.at[me], land_o.at[me], osend.at[k], orecv.at[k], peer).start()

    sems, (land, land_o), _ = _comm_call("w_in_diag", [land, land_o], [frecv], [(NDEV,)] * 4, body, after=after)
    return sems, land, land_o


def _w_in_finish(land, send, fsend, dsend, drecv, after):
    def body(a, s, new):
        (land,), (send, fsend, dsend, drecv) = a, s
        x, y, c, me = _my_place()
        sibling, _ = _peer(x, y, c, 1)
        _, slot = _peer(x, y, c, 7)
        _remote(land.at[slot], land.at[slot], dsend.at[6], drecv.at[7], sibling).wait_recv()
        for k in (1, 2, 4):
            peer, _ = _peer(x, y, c, k)
            _remote(land.at[me], land.at[me], send.at[k], send.at[k], peer).wait_send()
        for k in (2, 4, 6):
            _, slot = _peer(x, y, c, k)
            _remote(land.at[slot], land.at[slot], fsend.at[k], fsend.at[k], sibling).wait_send()
        _, slot = _peer(x, y, c, 6)
        _remote(land.at[slot], land.at[slot], dsend.at[6], dsend.at[6], sibling).wait_send()

    return _comm_call("w_in_finish", [land], [send, fsend, dsend, drecv], [], body, after=after)[1][0]


def _in_proj_part(name, h, land, proj, me_arr, k0, kstep, nk, tm=512):
    S, D = h.shape
    C = land.shape[2]
    tm = min(tm, S)

    def body(me_ref, a_ref, b_ref, *rest):
        rest[-1][...] = jnp.dot(a_ref[...], b_ref[...], preferred_element_type=f32)

    slot = lambda j, me: me[0] ^ (k0 + kstep * j)
    args = [h, land] + ([] if proj is None else [proj])
    grid_spec = pltpu.PrefetchScalarGridSpec(
        num_scalar_prefetch=1, grid=(nk, S // tm),
        in_specs=[pl.BlockSpec((tm, D), lambda j, i, me: (i, 0)),
                  pl.BlockSpec((None, D, C), lambda j, i, me: (slot(j, me), 0, 0))] + [ANY_SPEC] * (len(args) - 2),
        out_specs=pl.BlockSpec((tm, C), lambda j, i, me: (i, slot(j, me))))
    return pl.pallas_call(
        body, name=name, out_shape=jax.ShapeDtypeStruct((S, NDEV * C), f32), grid_spec=grid_spec,
        input_output_aliases={} if proj is None else {3: 0}, compiler_params=_params("arbitrary", "arbitrary"),
    )(me_arr, *args)


NCHIP = NDEV // 2


def _pair_start(name, src):
    npair = src.shape[0] // 2

    def body(a, s, new):
        (src, pair), (send, recv) = a, new
        x, y, c, me = _my_place()
        sibling, _ = _peer(x, y, c, 1)
        for i in range(npair):
            _remote(src.at[2 * i + 1 - c], pair.at[i], send.at[i], recv.at[i], sibling).start()

    pair = lax.empty((npair,) + src.shape[1:], src.dtype)
    (send, recv), (src, pair), token = _comm_call(name, [src, pair], [], [(npair,), (npair,)], body, token=True)
    return send, recv, src, pair, token


def _pair_wait(name, src, pair, send, recv, after):
    npair = pair.shape[0]

    def body(a, s, new):
        (src, pair), (send, recv) = a, s
        x, y, c, me = _my_place()
        sibling, _ = _peer(x, y, c, 1)
        for i in range(npair):
            cp = _remote(src.at[2 * i + 1 - c], pair.at[i], send.at[i], recv.at[i], sibling)
            cp.wait_recv()
            cp.wait_send()

    return _comm_call(name, [src, pair], [send, recv], [], body, after=after)[1]


def _pair_sum(name, src, pair, core, tr=1024):
    npair, R, Cc = pair.shape
    tr = min(tr, R)

    def body(core_ref, a_ref, b_ref, o_ref):
        o_ref[...] = (a_ref[...].astype(f32) + b_ref[...].astype(f32)).astype(o_ref.dtype)

    grid_spec = pltpu.PrefetchScalarGridSpec(
        num_scalar_prefetch=1, grid=(npair, R // tr),
        in_specs=[pl.BlockSpec((None, tr, Cc), lambda i, r, core: (2 * i + core[0], r, 0)),
                  pl.BlockSpec((None, tr, Cc), lambda i, r, core: (i, r, 0))],
        out_specs=pl.BlockSpec((None, tr, Cc), lambda i, r, core: (i, r, 0)))
    return pl.pallas_call(body, name=name, out_shape=jax.ShapeDtypeStruct(pair.shape, pair.dtype),
                          grid_spec=grid_spec, compiler_params=_params("parallel", "parallel"))(core, src, pair)


def _owner_chip(first, i):
    q = first // 2 + i
    return q >> 1 & 1, q & 1


def _chip_start(name, sums, land, first, rows=None, after=()):
    npair = sums.shape[0]
    rows = pl.ds(*(rows or (0, sums.shape[1])))

    def body(a, s, new):
        (sums, land), (send, recv) = a, new
        x, y, c, me = _my_place()
        for i in range(npair):
            ox, oy = _owner_chip(first, i)

            @pl.when((x != ox) | (y != oy))
            def _():
                _remote(sums.at[i, rows], land.at[2 * x + y, rows], send.at[i], recv.at[2 * x + y], (ox, oy, c)).start()

    (send, recv), (sums, land), token = _comm_call(name, [sums, land], [], [(npair,), (NCHIP,)], body, after=after,
                                                   token=True)
    return send, recv, sums, land, token


def _chip_wait(name, sums, land, send, recv, first, after, rows=None):
    npair = sums.shape[0]
    rows = pl.ds(*(rows or (0, sums.shape[1])))

    def body(a, s, new):
        (sums, land), (send, recv) = a, s
        x, y, c, me = _my_place()
        mine = (me >= first) & (me < first + 2 * npair)
        for i in range(npair):
            ox, oy = _owner_chip(first, i)

            @pl.when((x != ox) | (y != oy))
            def _():
                _remote(sums.at[i, rows], land.at[2 * x + y, rows], send.at[i], recv.at[2 * x + y], (ox, oy, c)).wait_send()
        for q in range(NCHIP):
            @pl.when(mine & (2 * x + y != q))
            def _():
                _remote(sums.at[0, rows], land.at[q, rows], send.at[0], recv.at[q], (q >> 1, q & 1, c)).wait_recv()

    return _comm_call(name, [sums, land], [send, recv], [], body, after=after)[1]


def _matmul(a, b, *, name, out_dtype, ta=False, tb=False, b_slots=False, out_slots=0, b_cols=None,
            tm=512, tn=1024, tk=2048, dep=None):
    M, K = (a.shape[1], a.shape[0]) if ta else a.shape
    col0 = 0
    if b_slots:
        slab = b.shape[2]
        N = b.shape[1] if tb else b.shape[0] * slab
        assert (K if tb else N) == b.shape[0] * slab
    elif b_cols is not None:
        assert not tb
        col0, N = b_cols
    else:
        N = b.shape[0] if tb else b.shape[1]
    tm, tn, tk = min(tm, M), min(tn, N), min(tk, K)
    if b_slots:
        if tb:
            tk = min(tk, slab)
        else:
            tn = min(tn, slab)
    if out_slots:
        tn = min(tn, N // out_slots)
    nm, nn, nk = M // tm, N // tn, K // tk
    assert (nm * tm, nn * tn, nk * tk) == (M, N, K) and col0 % tn == 0, (name, M, N, K, tm, tn, tk)
    j0 = col0 // tn

    a_spec = pl.BlockSpec((tk, tm), lambda i, j, k: (k, i)) if ta else pl.BlockSpec((tm, tk), lambda i, j, k: (i, k))
    if b_slots and tb:
        per = slab // tk
        b_spec = pl.BlockSpec((None, tn, tk), lambda i, j, k: (k // per, j, k % per))
    elif b_slots:
        per = slab // tn
        b_spec = pl.BlockSpec((None, tk, tn), lambda i, j, k: (j // per, k, j % per))
    elif tb:
        b_spec = pl.BlockSpec((tn, tk), lambda i, j, k: (j, k))
    else:
        b_spec = pl.BlockSpec((tk, tn), lambda i, j, k: (k, j + j0))
    if out_slots:
        per_o = (N // out_slots) // tn
        o_spec = pl.BlockSpec((None, tm, tn), lambda i, j, k: (j // per_o, i, j % per_o))
        out_shape = jax.ShapeDtypeStruct((out_slots, M, N // out_slots), out_dtype)
    else:
        o_spec = pl.BlockSpec((tm, tn), lambda i, j, k: (i, j))
        out_shape = jax.ShapeDtypeStruct((M, N), out_dtype)
    dims = (((0 if ta else 1,), (1 if tb else 0,)), ((), ()))
    deps = [] if dep is None else [dep]

    def body(a_ref, b_ref, *rest):
        o_ref = rest[len(deps)]
        prod = lax.dot_general(a_ref[...], b_ref[...], dims, preferred_element_type=f32)
        if nk == 1:
            o_ref[...] = prod.astype(out_dtype)
            return
        acc_ref = rest[len(deps) + 1]
        k = pl.program_id(2)

        @pl.when(k == 0)
        def _():
            acc_ref[...] = prod

        @pl.when((k > 0) & (k < nk - 1))
        def _():
            acc_ref[...] += prod

        @pl.when(k == nk - 1)
        def _():
            o_ref[...] = (acc_ref[...] + prod).astype(out_dtype)

    return pl.pallas_call(
        body, name=name, out_shape=out_shape, grid=(nm, nn, nk),
        in_specs=[a_spec, b_spec] + [ANY_SPEC] * len(deps), out_specs=o_spec,
        scratch_shapes=[pltpu.VMEM((tm, tn), f32)] if nk > 1 else [],
        compiler_params=_params("parallel", "parallel", "arbitrary"),
    )(a, b, *deps)


def _matmul_slabs_t(a_cols, a_slots, b, *, name, tm=512, tn=512, dep=None):
    M = a_cols.shape[0]
    n_slab, N, slab = b.shape
    n1, n2 = a_cols.shape[1] // slab, a_slots.shape[0]
    assert n1 + n2 == n_slab and a_slots.shape[1:] == (M, slab)
    tm, tn = min(tm, M), min(tn, N)
    deps = [] if dep is None else [dep]

    def body(a1_ref, a2_ref, b_ref, *rest):
        o_ref = rest[len(deps)]
        acc = None
        for s in range(n_slab):
            lhs = a1_ref[:, s * slab:(s + 1) * slab] if s < n1 else a2_ref[s - n1]
            prod = lax.dot_general(lhs, b_ref[s], (((1,), (1,)), ((), ())), preferred_element_type=f32)
            acc = prod if acc is None else acc + prod
        o_ref[...] = acc

    return pl.pallas_call(
        body, name=name, out_shape=jax.ShapeDtypeStruct((M, N), f32), grid=(M // tm, N // tn),
        in_specs=[pl.BlockSpec((tm, n1 * slab), lambda i, j: (i, 0)), pl.BlockSpec((n2, tm, slab), lambda i, j: (0, i, 0)),
                  pl.BlockSpec((n_slab, tn, slab), lambda i, j: (0, j, 0))] + [ANY_SPEC] * len(deps),
        out_specs=pl.BlockSpec((tm, tn), lambda i, j: (i, j)), compiler_params=_params("parallel", "parallel"),
    )(a_cols, a_slots, b, *deps)


def _ada_exchange(c_blk, cw_slab, w_ada, b_cols):
    nblk = c_blk.shape[0]
    D, W = w_ada.shape
    CW = cw_slab.shape[1]

    def body(c_ref, cw_ref, w_ref, b_ref, mod_ref, call_ref, cwg_ref, msend, send_sems, recv_sems):
        x, y, c, me = _my_place()
        call_ref[me] = _silu(c_ref[...])
        cwg_ref[me] = cw_ref[...]
        first = []
        for k in range(1, NDEV):
            peer, _ = _peer(x, y, c, k)
            first.append(_remote(call_ref.at[me], call_ref.at[me], send_sems.at[0, k], recv_sems.at[0, k], peer))
            first.append(_remote(cwg_ref.at[me], cwg_ref.at[me], send_sems.at[1, k], recv_sems.at[1, k], peer))
        for cp in first:
            cp.start()
        for k in range(1, NDEV):
            peer, slot = _peer(x, y, c, k)
            _remote(call_ref.at[slot], call_ref.at[slot], send_sems.at[0, k], recv_sems.at[0, k], peer).wait_recv()
            _remote(cwg_ref.at[slot], cwg_ref.at[slot], send_sems.at[1, k], recv_sems.at[1, k], peer).wait_recv()
        mod = jnp.broadcast_to(b_ref[...], (NDEV, W))
        for r in range(nblk):
            mod = mod + lax.dot_general(call_ref[:, r, :], w_ref[r * 128:(r + 1) * 128, :], (((1,), (0,)), ((), ())),
                                        preferred_element_type=f32, precision=lax.Precision.HIGHEST)
        row = lax.broadcasted_iota(jnp.int32, (NDEV, 1), 0)
        pick = lambda j: jnp.broadcast_to(jnp.sum(jnp.where(row == j, mod, 0.0), axis=0, keepdims=True), (8, W))
        mod_ref[me] = pick(me)
        second = []
        for k in range(1, NDEV):
            peer, slot = _peer(x, y, c, k)
            msend[k] = pick(slot)
            second.append(_remote(msend.at[k], mod_ref.at[me], send_sems.at[2, k], recv_sems.at[2, k], peer))
        for cp in second:
            cp.start()
        for k in range(1, NDEV):
            peer, slot = _peer(x, y, c, k)
            _remote(msend.at[k], mod_ref.at[slot], send_sems.at[2, k], recv_sems.at[2, k], peer).wait_recv()
        for cp in first + second:
            cp.wait_send()

    vmem = pl.BlockSpec(memory_space=pltpu.VMEM)
    return pl.pallas_call(
        body, name="ada_exchange",
        out_shape=(jax.ShapeDtypeStruct((NDEV, 8, W), f32), jax.ShapeDtypeStruct((NDEV, nblk, 128), f32),
                   jax.ShapeDtypeStruct((NDEV, 8, CW), f32)),
        in_specs=[vmem] * 4, out_specs=(vmem, vmem, vmem),
        scratch_shapes=[pltpu.VMEM((NDEV, 8, W), f32), pltpu.SemaphoreType.DMA((3, NDEV)),
                        pltpu.SemaphoreType.DMA((3, NDEV))],
        compiler_params=_params(),
    )(c_blk, cw_slab, w_ada, b_cols)


def _prenorm(x, scale, shift, g_pre, dep, tr=256):
    S, D = x.shape
    tr = min(tr, S)

    def body(x_ref, sc_ref, sh_ref, g_ref, dep_ref, h_ref):
        xv = x_ref[...]
        r = lax.rsqrt(jnp.mean(xv * xv, axis=-1, keepdims=True) + EPS)
        h_ref[...] = ((xv * r) * g_ref[...] * (1.0 + sc_ref[...]) + sh_ref[...]).astype(bf16)

    row = pl.BlockSpec((tr, D), lambda i: (i, 0))
    vec = pl.BlockSpec((1, D), lambda i: (0, 0))
    return pl.pallas_call(body, name="prenorm", out_shape=jax.ShapeDtypeStruct((S, D), bf16), grid=(S // tr,),
                          in_specs=[row, vec, vec, vec, ANY_SPEC], out_specs=row, compiler_params=_params("parallel"))(
                              x, scale, shift, g_pre, dep)


def _ext_rows(i, tr, S):
    g = lax.broadcasted_iota(jnp.int32, (tr + 16, 1), 0) + (i * tr - 8)
    return (g >= 0) & (g < S)


def _halo_specs(tr, S, C, col):
    nb8 = S // 8
    main = pl.BlockSpec((tr, C), lambda i: (i, col))
    prev = pl.BlockSpec((8, C), lambda i: (jnp.maximum(i * (tr // 8) - 1, 0), col))
    nxt = pl.BlockSpec((8, C), lambda i: (jnp.minimum((i + 1) * (tr // 8), nb8 - 1), col))
    return prev, main, nxt


def _conv_fwd(proj, conv_w, conv_b, g_conv, tr=256):
    S, C = proj.shape[0], proj.shape[1] // 8
    tr = min(tr, S)

    def body(up, um, un, cp, cm, cn, bg_ref, zc_ref, w_ref, cb_ref, g_ref, o_ref):
        i = pl.program_id(0)
        exists = _ext_rows(i, tr, S)
        u = jnp.concatenate([up[...], um[...], un[...]], axis=0)
        cg = jnp.concatenate([cp[...], cm[...], cn[...]], axis=0)
        t = jnp.where(exists, cg * u, 0.0)
        t_before = pltpu.roll(t, 1, 0)[8:tr + 8]
        t_after = pltpu.roll(t, tr + 15, 0)[8:tr + 8]
        w = w_ref[...]
        cv = w[0:1] * t_before + w[1:2] * t[8:tr + 8] + w[2:3] * t_after + cb_ref[...]
        yc = bg_ref[...] * cv
        rc = lax.rsqrt(jnp.mean(yc * yc, axis=-1, keepdims=True) + EPS)
        o_ref[...] = ((yc * rc) * g_ref[...] * _silu(zc_ref[...])).astype(bf16)

    u_specs = _halo_specs(tr, S, C, 0)
    c_specs = _halo_specs(tr, S, C, 2)
    vec = pl.BlockSpec((1, C), lambda i: (0, 0))
    return pl.pallas_call(
        body, name="conv_fwd", out_shape=jax.ShapeDtypeStruct((S, 2 * C), bf16), grid=(S // tr,),
        in_specs=[*u_specs, *c_specs, pl.BlockSpec((tr, C), lambda i: (i, 1)), pl.BlockSpec((tr, C), lambda i: (i, 3)),
                  pl.BlockSpec((8, C), lambda i: (0, 0)), vec, vec],
        out_specs=pl.BlockSpec((tr, C), lambda i: (i, 0)), compiler_params=_params("parallel"),
    )(proj, proj, proj, proj, proj, proj, proj, proj, conv_w, conv_b, g_conv)


def _branch_geometry(S, r, inter):
    L = S // r * inter
    nq = min(128, L)
    nk = min(nq + 2 * HALF_WIN * inter, L)
    assert L % nq == 0 and (L == nk or L >= nq + 2 * HALF_WIN * inter)
    return L, nq, nk, L // nq


QUAD = 4


def _to_quad(dst, src, S):
    n = S // QUAD
    for rho in range(QUAD):
        dst[pl.ds(rho * n, n), :] = src[pl.ds(rho, n, stride=QUAD), :]


def _block_rows(idx, r, inter, S, L, nq, nk, nblk):
    rho, qb = (0, idx) if r == 1 else (idx // nblk, idx % nblk)
    i0 = qb * nq
    ws = jnp.clip(i0 - HALF_WIN * inter, 0, L - nk)
    if r == 1:
        return pl.ds(pl.multiple_of(i0, 8), nq), pl.ds(pl.multiple_of(ws, 8), nk), i0 - ws
    assert r % (QUAD * inter) == 0
    step = r // QUAD // inter
    base = (rho % QUAD) * (S // QUAD) + rho // QUAD
    if step == 1:
        return pl.ds(pl.multiple_of(base + i0, 8), nq), pl.ds(pl.multiple_of(base + ws, 8), nk), i0 - ws
    return pl.ds(base + step * i0, nq, stride=step), pl.ds(base + step * ws, nk, stride=step), i0 - ws


N_CASES = 3
SCALE = HEAD_DIM ** -0.5
ATTN_UNROLL = 16


def _bias_shape(S):
    shapes = [_branch_geometry(S, r, inter)[1:3] for _, r, inter in BRANCHES]
    return (len(BRANCHES) * N_CASES * 2, max(nq for nq, _ in shapes), max(nk for _, nk in shapes))


def _bias_index(b, case, head):
    return (b * N_CASES + case) * 2 + head


def _fill_bias(bias_scr, sl_ref, S):
    sl = sl_ref[...]
    slope = (sl[0:1, 0:1], sl[0:1, HEAD_DIM:HEAD_DIM + 1])
    for b, (_, r, inter) in enumerate(BRANCHES):
        L, nq, nk, nblk = _branch_geometry(S, r, inter)
        rel = lax.broadcasted_iota(jnp.int32, (nq, nk), 0) - lax.broadcasted_iota(jnp.int32, (nq, nk), 1)
        for case in range(N_CASES):
            d = jnp.abs(rel + case * HALF_WIN)
            valid = d <= HALF_WIN * inter
            if inter > 1:
                valid = valid & (jnp.bitwise_and(d, inter - 1) == 0)
            dist = d.astype(f32) * float(r // inter)
            for head in range(2):
                bias_scr[_bias_index(b, case, head), 0:nq, 0:nk] = jnp.where(valid, -slope[head] * dist, NEG_INF)


def _head_slopes(n_heads):
    slopes = 2.0 ** (-8.0 * jnp.arange(1, n_heads + 1, dtype=f32) / n_heads)
    return jnp.broadcast_to(jnp.repeat(slopes.reshape(n_heads // 2, 2), HEAD_DIM, axis=1)[:, None, :],
                            (n_heads // 2, 8, PAIR))


def _attn_fwd(proj, slopes):
    S, C = proj.shape[0], proj.shape[1] // 8
    npair = C // PAIR

    def body(q_ref, k_ref, v_ref, sl_ref, o_ref, lse_ref, m_scr, l_scr, a_scr, bias_scr, q4_scr, k4_scr, v4_scr):
        lane = lax.broadcasted_iota(jnp.int32, (1, PAIR), 1)
        first = lane < HEAD_DIM
        _fill_bias(bias_scr, sl_ref, S)
        for dst, src in ((q4_scr, q_ref), (k4_scr, k_ref), (v4_scr, v_ref)):
            _to_quad(dst, src, S)

        for b, (_, r, inter) in enumerate(BRANCHES):
            L, nq, nk, nblk = _branch_geometry(S, r, inter)
            qs, ks, vs = (q_ref, k_ref, v_ref) if r == 1 else (q4_scr, k4_scr, v4_scr)

            def step(idx, carry, b=b, r=r, L=L, nq=nq, nk=nk, nblk=nblk, qs=qs, ks=ks, vs=vs):
                qrows, krows, off = _block_rows(idx, r, inter, S, L, nq, nk, nblk)
                case = off // HALF_WIN
                q2 = qs[qrows, :] * SCALE
                k2 = ks[krows, :].astype(bf16)
                v2 = vs[krows, :].astype(bf16)
                ms, accs = [], []
                for hh in range(2):
                    mine = first if hh == 0 else ~first
                    qh = jnp.where(mine, q2, 0.0).astype(bf16)
                    s = lax.dot_general(qh, k2, (((1,), (1,)), ((), ())), preferred_element_type=f32)
                    s = s + bias_scr[_bias_index(b, case, hh), 0:nq, 0:nk]
                    m = jnp.max(s, axis=-1, keepdims=True)
                    p = jnp.exp(s - m).astype(bf16)
                    vh = jnp.where(mine, v2, jnp.ones_like(v2))
                    ms.append(m)
                    accs.append(jnp.dot(p, vh, preferred_element_type=f32))
                m_scr[b, qrows, :] = jnp.where(first, ms[0], ms[1])
                a_scr[b, qrows, :] = jnp.where(first, accs[0], accs[1])
                l_scr[b, qrows, :] = jnp.where(first, accs[1], accs[0])
                return carry

            lax.fori_loop(0, S // nq, step, 0, unroll=min(ATTN_UNROLL, S // nq))

        n4 = S // QUAD
        ch = min(256, n4)
        nch = n4 // ch

        def merge(i, carry):
            rho, part = i // nch, i % nch
            sorted_rows = pl.ds(pl.multiple_of(rho * n4 + part * ch, 8), ch)
            token_rows = pl.ds(rho + QUAD * part * ch, ch, stride=QUAD)
            rows = (token_rows,) + (sorted_rows,) * (len(BRANCHES) - 1)
            ms = [m_scr[b, rows[b], :] for b in range(len(BRANCHES))]
            m = functools.reduce(jnp.maximum, ms)
            l = jnp.zeros((ch, PAIR), f32)
            acc = jnp.zeros((ch, PAIR), f32)
            for b in range(len(BRANCHES)):
                w = jnp.exp(ms[b] - m)
                l = l + w * pltpu.roll(l_scr[b, rows[b], :], HEAD_DIM, 1)
                acc = acc + w * a_scr[b, rows[b], :]
            o_ref[token_rows, :] = acc / l
            lse_ref[token_rows, :] = m + jnp.log(l)
            return carry

        lax.fori_loop(0, QUAD * nch, merge, 0)

    blk = lambda part: pl.BlockSpec((S, PAIR), lambda p: (0, part * npair + p))
    out = pl.BlockSpec((S, PAIR), lambda p: (0, p))
    return pl.pallas_call(
        body, name="attn_fwd",
        out_shape=(jax.ShapeDtypeStruct((S, C), f32), jax.ShapeDtypeStruct((S, C), f32)), grid=(npair,),
        in_specs=[blk(4), blk(5), blk(6), pl.BlockSpec((None, 8, PAIR), lambda p: (p, 0, 0))],
        out_specs=(out, out),
        scratch_shapes=[pltpu.VMEM((3, S, PAIR), f32)] * 3 + [pltpu.VMEM(_bias_shape(S), f32)]
        + [pltpu.VMEM((S, PAIR), f32)] * 3,
        compiler_params=_params("parallel"),
    )(proj, proj, proj, slopes)


def _attn_post(ycat, o, proj, g_attn, tr=256):
    S, C = o.shape
    tr = min(tr, S)

    def body(y_ref, o_ref, z_ref, g_ref, out_ref):
        del y_ref
        ov = o_ref[...]
        ra = lax.rsqrt(jnp.mean(ov * ov, axis=-1, keepdims=True) + EPS)
        out_ref[...] = ((ov * ra) * g_ref[...] * _silu(z_ref[...])).astype(bf16)

    return pl.pallas_call(
        body, name="attn_post", out_shape=jax.ShapeDtypeStruct(ycat.shape, ycat.dtype), grid=(S // tr,),
        in_specs=[HBM_SPEC, pl.BlockSpec((tr, C), lambda i: (i, 0)), pl.BlockSpec((tr, C), lambda i: (i, 7)),
                  pl.BlockSpec((1, C), lambda i: (0, 0))],
        out_specs=pl.BlockSpec((tr, C), lambda i: (i, 1)), input_output_aliases={0: 0},
        compiler_params=_params("arbitrary"),
    )(ycat, o, proj, g_attn)


def _sandwich(y, x, target, gate, g_post, tr=256):
    S, D = y.shape
    tr = min(tr, S)

    def body(y_ref, x_ref, t_ref, gate_ref, g_ref, dy_ref, dout_ref, sums_ref):
        i = pl.program_id(0)
        yv = y_ref[...]
        rp = lax.rsqrt(jnp.mean(yv * yv, axis=-1, keepdims=True) + EPS)
        yhat = yv * rp
        yn = yhat * g_ref[...]
        err = (x_ref[...] + gate_ref[...] * yn) - t_ref[...]
        dout = err * (1.0 / D)
        dout_ref[...] = dout
        dyn = dout * gate_ref[...]
        w = dyn * g_ref[...]
        dy_ref[...] = (rp * (w - yhat * jnp.mean(w * yhat, axis=-1, keepdims=True))).astype(bf16)
        loss = 0.5 * jnp.sum(jnp.mean(err * err, axis=-1, keepdims=True), axis=0, keepdims=True)
        row = lax.broadcasted_iota(jnp.int32, (8, D), 0)
        upd = jnp.where(row == 0, jnp.sum(dout * yn, axis=0, keepdims=True),
                        jnp.where(row == 1, jnp.sum(dyn * yhat, axis=0, keepdims=True),
                                  jnp.where(row == 2, loss, 0.0)))

        @pl.when(i == 0)
        def _():
            sums_ref[...] = upd

        @pl.when(i > 0)
        def _():
            sums_ref[...] += upd

    row = pl.BlockSpec((tr, D), lambda i: (i, 0))
    vec = pl.BlockSpec((1, D), lambda i: (0, 0))
    return pl.pallas_call(
        body, name="sandwich",
        out_shape=(jax.ShapeDtypeStruct((S, D), bf16), jax.ShapeDtypeStruct((S, D), f32), jax.ShapeDtypeStruct((8, D), f32)),
        grid=(S // tr,), in_specs=[row, row, row, vec, vec],
        out_specs=(row, row, pl.BlockSpec((8, D), lambda i: (0, 0))), compiler_params=_params("arbitrary"),
    )(y, x, target, gate, g_post)


def _conv_bwd(proj, dycat, conv_w, conv_b, g_conv, dep, tr=256):
    S, C = proj.shape[0], proj.shape[1] // 8
    tr = min(tr, S)
    n = tr + 16

    def body(*refs):
        ins, (w_ref, cb_ref, g_ref, _, dp_ref, sums_ref) = refs[:15], refs[15:]
        i = pl.program_id(0)
        exists = _ext_rows(i, tr, S)
        u, bg, cg, zc, dyn = (jnp.concatenate([ins[3 * t][...], ins[3 * t + 1][...], ins[3 * t + 2][...]], axis=0)
                              for t in range(5))
        w = w_ref[...]
        t = jnp.where(exists, cg * u, 0.0)
        t_before, t_after = pltpu.roll(t, 1, 0), pltpu.roll(t, n - 1, 0)
        cv = w[0:1] * t_before + w[1:2] * t + w[2:3] * t_after + cb_ref[...]
        yc = bg * cv
        rc = lax.rsqrt(jnp.mean(yc * yc, axis=-1, keepdims=True) + EPS)
        yhat = yc * rc
        sz, dsz = _silu_and_slope(zc)
        wgt = dyn * g_ref[...] * sz
        dyc = rc * (wgt - yhat * jnp.mean(wgt * yhat, axis=-1, keepdims=True))
        dcv = jnp.where(exists, dyc * bg, 0.0)
        dt = w[0:1] * pltpu.roll(dcv, n - 1, 0) + w[1:2] * dcv + w[2:3] * pltpu.roll(dcv, 1, 0)
        mid = slice(8, tr + 8)
        dp_ref[:, 0:C] = (dt * cg)[mid].astype(bf16)
        dp_ref[:, C:2 * C] = (dyc * cv)[mid].astype(bf16)
        dp_ref[:, 2 * C:3 * C] = (dt * u)[mid].astype(bf16)
        dp_ref[:, 3 * C:4 * C] = (dyn * yhat * g_ref[...] * dsz)[mid].astype(bf16)
        colsum = lambda v: jnp.sum(v[mid], axis=0, keepdims=True)
        parts = [colsum(dyn * yhat * sz), colsum(dcv), colsum(dcv * t_before), colsum(dcv * t), colsum(dcv * t_after)]
        row = lax.broadcasted_iota(jnp.int32, (8, C), 0)
        upd = jnp.zeros((8, C), f32)
        for j, pj in enumerate(parts):
            upd = jnp.where(row == j, pj, upd)

        @pl.when(i == 0)
        def _():
            sums_ref[...] = upd

        @pl.when(i > 0)
        def _():
            sums_ref[...] += upd

    specs = []
    for col in range(4):
        specs += _halo_specs(tr, S, C, col)
    specs += _halo_specs(tr, S, C, 0)
    vec = pl.BlockSpec((1, C), lambda i: (0, 0))
    return pl.pallas_call(
        body, name="conv_bwd",
        out_shape=(jax.ShapeDtypeStruct((S, 4 * C), bf16), jax.ShapeDtypeStruct((8, C), f32)), grid=(S // tr,),
        in_specs=[*specs, pl.BlockSpec((8, C), lambda i: (0, 0)), vec, vec, ANY_SPEC],
        out_specs=(pl.BlockSpec((tr, 4 * C), lambda i: (i, 0)), pl.BlockSpec((8, C), lambda i: (0, 0))),
        compiler_params=_params("arbitrary"),
    )(*([proj] * 12), dycat, dycat, dycat, conv_w, conv_b, g_conv, dep)


def _attn_post_bwd(o, proj, dycat, g_attn, dep, tr=256):
    S, C = o.shape
    tr = min(tr, S)

    def body(o_ref, z_ref, dy_ref, g_ref, dep_ref, do_ref, dz_ref, sums_ref):
        i = pl.program_id(0)
        ov, zv, dyn = o_ref[...], z_ref[...], dy_ref[...]
        ra = lax.rsqrt(jnp.mean(ov * ov, axis=-1, keepdims=True) + EPS)
        ohat = ov * ra
        sz, dsz = _silu_and_slope(zv)
        wgt = dyn * g_ref[...] * sz
        do_ref[...] = ra * (wgt - ohat * jnp.mean(wgt * ohat, axis=-1, keepdims=True))
        dz_ref[...] = (dyn * ohat * g_ref[...] * dsz).astype(bf16)
        row = lax.broadcasted_iota(jnp.int32, (8, C), 0)
        upd = jnp.where(row == 0, jnp.sum(dyn * ohat * sz, axis=0, keepdims=True), 0.0)

        @pl.when(i == 0)
        def _():
            sums_ref[...] = upd

        @pl.when(i > 0)
        def _():
            sums_ref[...] += upd

    return pl.pallas_call(
        body, name="attn_post_bwd",
        out_shape=(jax.ShapeDtypeStruct((S, C), f32), jax.ShapeDtypeStruct((4, S, C), bf16),
                   jax.ShapeDtypeStruct((8, C), f32)),
        grid=(S // tr,),
        in_specs=[pl.BlockSpec((tr, C), lambda i: (i, 0)), pl.BlockSpec((tr, C), lambda i: (i, 7)),
                  pl.BlockSpec((tr, C), lambda i: (i, 1)), pl.BlockSpec((1, C), lambda i: (0, 0)), ANY_SPEC],
        out_specs=(pl.BlockSpec((tr, C), lambda i: (i, 0)), pl.BlockSpec((None, tr, C), lambda i: (3, i, 0)),
                   pl.BlockSpec((8, C), lambda i: (0, 0))),
        compiler_params=_params("arbitrary"),
    )(o, proj, dycat, g_attn, dep)


def _attn_bwd(proj, o, do, lse, slopes, dqkvz, dep):
    S, C = o.shape
    npair = C // PAIR

    def body(q_ref, k_ref, v_ref, o_ref, do_ref, lse_ref, sl_ref, old_ref, dep_ref, dqkv_ref,
             acc_scr, dl_scr, quad_scr, bias_scr):
        lane = lax.broadcasted_iota(jnp.int32, (1, PAIR), 1)
        first = lane < HEAD_DIM
        _fill_bias(bias_scr, sl_ref, S)
        ch = min(256, S)

        def prep(i, carry):
            rows = pl.ds(pl.multiple_of(i * ch, 8), ch)
            prod = do_ref[rows, :] * o_ref[rows, :]
            d0 = jnp.sum(jnp.where(first, prod, 0.0), axis=-1, keepdims=True)
            d1 = jnp.sum(jnp.where(first, 0.0, prod), axis=-1, keepdims=True)
            dl_scr[rows, :] = jnp.where(first, d0, d1)
            zero = jnp.zeros((ch, PAIR), f32)
            for order in range(2):
                for t in range(3):
                    acc_scr[order, t, rows, :] = zero
            return carry

        lax.fori_loop(0, S // ch, prep, 0)
        token_srcs = (q_ref, k_ref, v_ref, do_ref, lse_ref, dl_scr)
        for j, src in enumerate(token_srcs):
            _to_quad(quad_scr.at[j], src, S)

        for b, (_, r, inter) in enumerate(BRANCHES):
            L, nq, nk, nblk = _branch_geometry(S, r, inter)
            order = 0 if r == 1 else 1
            srcs = token_srcs if r == 1 else tuple(quad_scr.at[j] for j in range(6))

            def step(idx, carry, b=b, r=r, L=L, nq=nq, nk=nk, nblk=nblk, order=order, srcs=srcs):
                qs, ks, vs, dos, lses, dls = srcs
                dq_scr, dk_scr, dv_scr = (acc_scr.at[order, t] for t in range(3))
                qrows, krows, off = _block_rows(idx, r, inter, S, L, nq, nk, nblk)
                case = off // HALF_WIN
                q2 = qs[qrows, :] * SCALE
                k2 = ks[krows, :].astype(bf16)
                v2 = vs[krows, :].astype(bf16)
                do2 = dos[qrows, :]
                lse2 = lses[qrows, :]
                dl2 = dls[qrows, :]
                dq2 = jnp.zeros((nq, PAIR), f32)
                dk2 = jnp.zeros((nk, PAIR), f32)
                dv2 = jnp.zeros((nk, PAIR), f32)
                for hh in range(2):
                    mine = first if hh == 0 else ~first
                    lo = hh * HEAD_DIM
                    qh = jnp.where(mine, q2, 0.0).astype(bf16)
                    doh = jnp.where(mine, do2, 0.0).astype(bf16)
                    s = lax.dot_general(qh, k2, (((1,), (1,)), ((), ())), preferred_element_type=f32)
                    s = s + bias_scr[_bias_index(b, case, hh), 0:nq, 0:nk]
                    p = jnp.exp(s - lse2[:, lo:lo + 1])
                    dv2 = dv2 + lax.dot_general(p.astype(bf16), doh, (((0,), (0,)), ((), ())), preferred_element_type=f32)
                    dp = lax.dot_general(doh, v2, (((1,), (1,)), ((), ())), preferred_element_type=f32)
                    ds = (p * (dp - dl2[:, lo:lo + 1])).astype(bf16)
                    dq2 = dq2 + jnp.where(mine, jnp.dot(ds, k2, preferred_element_type=f32), 0.0)
                    dk2 = dk2 + lax.dot_general(ds, qh, (((0,), (0,)), ((), ())), preferred_element_type=f32)
                dq_scr[qrows, :] = dq_scr[qrows, :] + dq2
                dk_scr[krows, :] = dk_scr[krows, :] + dk2
                dv_scr[krows, :] = dv_scr[krows, :] + dv2
                return carry

            lax.fori_loop(0, S // nq, step, 0, unroll=min(ATTN_UNROLL, S // nq))

        n4 = S // QUAD
        for t in range(3):
            for rho in range(QUAD):
                token_rows = pl.ds(rho, n4, stride=QUAD)
                acc_scr[0, t, token_rows, :] = acc_scr[0, t, token_rows, :] + acc_scr[1, t, pl.ds(rho * n4, n4), :]
        dqkv_ref[0] = (acc_scr[0, 0] * SCALE).astype(bf16)
        dqkv_ref[1] = acc_scr[0, 1].astype(bf16)
        dqkv_ref[2] = acc_scr[0, 2].astype(bf16)

    blk = lambda part: pl.BlockSpec((S, PAIR), lambda p: (0, part * npair + p))
    own = pl.BlockSpec((S, PAIR), lambda p: (0, p))
    return pl.pallas_call(
        body, name="attn_bwd", out_shape=jax.ShapeDtypeStruct(dqkvz.shape, dqkvz.dtype), grid=(npair,),
        in_specs=[blk(4), blk(5), blk(6), own, own, own, pl.BlockSpec((None, 8, PAIR), lambda p: (p, 0, 0)),
                  ANY_SPEC, ANY_SPEC],
        out_specs=pl.BlockSpec((3, S, PAIR), lambda p: (0, 0, p)), input_output_aliases={7: 0},
        scratch_shapes=[pltpu.VMEM((2, 3, S, PAIR), f32), pltpu.VMEM((S, PAIR), f32), pltpu.VMEM((6, S, PAIR), f32),
                        pltpu.VMEM(_bias_shape(S), f32)],
        compiler_params=_params("arbitrary"),
    )(proj, proj, proj, o, do, lse, slopes, dqkvz, dep)


def _prenorm_bwd(dh, x, dout, scale, g_pre, tr=256):
    S, D = x.shape
    tr = min(tr, S)

    def body(dh_ref, x_ref, dout_ref, sc_ref, g_ref, gx_ref, sums_ref):
        i = pl.program_id(0)
        xv, dhv = x_ref[...], dh_ref[...]
        r = lax.rsqrt(jnp.mean(xv * xv, axis=-1, keepdims=True) + EPS)
        xn = xv * r
        dxn = dhv * (g_ref[...] * (1.0 + sc_ref[...]))
        gx_ref[...] = dout_ref[...] + r * (dxn - xn * jnp.mean(dxn * xn, axis=-1, keepdims=True))
        dhx = dhv * xn
        row = lax.broadcasted_iota(jnp.int32, (8, D), 0)
        upd = jnp.where(row == 0, jnp.sum(dhv, axis=0, keepdims=True),
                        jnp.where(row == 1, jnp.sum(dhx, axis=0, keepdims=True) * g_ref[...],
                                  jnp.where(row == 2, jnp.sum(dhx, axis=0, keepdims=True) * (1.0 + sc_ref[...]), 0.0)))

        @pl.when(i == 0)
        def _():
            sums_ref[...] = upd

        @pl.when(i > 0)
        def _():
            sums_ref[...] += upd

    row = pl.BlockSpec((tr, D), lambda i: (i, 0))
    vec = pl.BlockSpec((1, D), lambda i: (0, 0))
    return pl.pallas_call(
        body, name="prenorm_bwd",
        out_shape=(jax.ShapeDtypeStruct((S, D), f32), jax.ShapeDtypeStruct((8, D), f32)), grid=(S // tr,),
        in_specs=[row, row, row, vec, vec], out_specs=(row, pl.BlockSpec((8, D), lambda i: (0, 0))),
        compiler_params=_params("arbitrary"),
    )(dh, x, dout, scale, g_pre)


def _adamw(w, g, m, v):
    m = ADAM_B1 * m + (1.0 - ADAM_B1) * g
    v = ADAM_B2 * v + (1.0 - ADAM_B2) * (g * g)
    m_hat = m / (1.0 - ADAM_B1 ** ADAM_STEP)
    v_hat = v / (1.0 - ADAM_B2 ** ADAM_STEP)
    delta = -ADAM_LR * (m_hat / (jnp.sqrt(v_hat) + ADAM_EPS) + ADAM_WD * w)
    return delta, m, v


def _sum_rows(parts, dep):
    P = parts.shape[1]

    def body(p_ref, dep_ref, o_ref):
        acc = p_ref[0:1, :]
        for j in range(1, NDEV):
            acc = acc + p_ref[j:j + 1, :]
        o_ref[...] = jnp.broadcast_to(acc, (8, P))

    vmem = pl.BlockSpec(memory_space=pltpu.VMEM)
    return pl.pallas_call(body, name="sum_small", out_shape=jax.ShapeDtypeStruct((8, P), f32),
                          in_specs=[vmem, ANY_SPEC], out_specs=vmem, compiler_params=_params())(parts, dep)


def _adamw_small(tot, params):
    given = [p[3] for p in params if not isinstance(p[3], int)]

    def body(tot_ref, *refs):
        given_refs = list(refs[:len(given)])
        ins = refs[len(given):len(given) + 3 * len(params)]
        outs = refs[len(given) + 3 * len(params):]
        for t, (w, _, _, where) in enumerate(params):
            w_ref, m_ref, v_ref = ins[3 * t:3 * t + 3]
            g = tot_ref[0:1, where:where + w.size] if isinstance(where, int) else given_refs.pop(0)[...]
            outs[4 * t][...] = g
            outs[4 * t + 1][...], outs[4 * t + 2][...], outs[4 * t + 3][...] = _adamw(w_ref[...], g, m_ref[...], v_ref[...])

    out_shape = tuple(jax.ShapeDtypeStruct(p[0].shape, f32) for p in params for _ in range(4))
    res = pl.pallas_call(body, name="adamw_small", out_shape=out_shape, compiler_params=_params())(
        tot, *given, *[a for p in params for a in p[:3]])
    return [res[4 * t:4 * t + 4] for t in range(len(params))]


def _adamw_sharded(name, parts, sums_a, sums_b, pick, w, m, v, rows=None, prev=None, tr=128):
    R, Cc = w.shape
    r0, nr = rows or (0, R)
    tr = math.gcd(tr, r0, nr)
    n, b0 = parts.shape[0], r0 // tr

    def body(pick_ref, p_ref, a_ref, b_ref, w_ref, m_ref, v_ref, *rest):
        g_ref, d_ref, nm_ref, nv_ref = rest[-4:]
        g = jnp.where(pick_ref[0] == 1, b_ref[...], a_ref[...]).astype(f32)
        for j in range(n):
            g = g + p_ref[j].astype(f32)
        g_ref[...] = g
        d_ref[...], nm_ref[...], nv_ref[...] = _adamw(w_ref[...], g, m_ref[...], v_ref[...])

    row = pl.BlockSpec((tr, Cc), lambda i, pick: (i + b0, 0))
    mine = pl.BlockSpec((None, tr, Cc), lambda i, pick: (pick[1], i + b0, 0))
    out = jax.ShapeDtypeStruct((R, Cc), f32)
    prev = list(prev or [])
    grid_spec = pltpu.PrefetchScalarGridSpec(
        num_scalar_prefetch=1, grid=(nr // tr,),
        in_specs=[pl.BlockSpec((n, tr, Cc), lambda i, pick: (0, i + b0, 0)), mine, mine, row, row, row]
        + [ANY_SPEC] * len(prev),
        out_specs=(row, row, row, row))
    return pl.pallas_call(
        body, name=name, out_shape=(out, out, out, out), grid_spec=grid_spec,
        input_output_aliases={7 + t: t for t in range(len(prev))}, compiler_params=_params("arbitrary"),
    )(pick, parts, sums_a, sums_b, w, m, v, *prev)


def _adamw_ada(c_t, dmod_cols, w, m, v, dep, tr=512):
    D, W = w.shape
    tr = min(tr, D)

    def body(c_ref, dm_ref, w_ref, m_ref, v_ref, dep_ref, g_ref, d_ref, nm_ref, nv_ref):
        cv, dm = c_ref[...], dm_ref[...]
        g = cv[:, 0:1] * dm[0:1, :]
        for b in range(1, NDEV):
            g = g + cv[:, b:b + 1] * dm[b:b + 1, :]
        g_ref[...] = g
        d_ref[...], nm_ref[...], nv_ref[...] = _adamw(w_ref[...], g, m_ref[...], v_ref[...])

    row = pl.BlockSpec((tr, W), lambda i: (i, 0))
    out = jax.ShapeDtypeStruct((D, W), f32)
    return pl.pallas_call(
        body, name="adamw_ada", out_shape=(out, out, out, out), grid=(D // tr,),
        in_specs=[pl.BlockSpec((tr, NDEV), lambda i: (i, 0)), pl.BlockSpec((NDEV, W), lambda i: (0, 0)), row, row, row,
                  ANY_SPEC],
        out_specs=(row, row, row, row), compiler_params=_params("parallel"),
    )(c_t, dmod_cols, w, m, v, dep)


def kernel(x, c, w_ada, b_ada, g_pre, w_in, conv_w, conv_b, g_conv, g_attn, w_out, g_post, loss_target, m_w_ada, m_b_ada, m_g_pre, m_w_in, m_conv_w, m_conv_b, m_g_conv, m_g_attn, m_w_out, m_g_post, v_w_ada, v_b_ada, v_g_pre, v_w_in, v_conv_w, v_conv_b, v_g_conv, v_g_attn, v_w_out, v_g_post):
    S, D = x.shape[1], x.shape[2]
    C = D // 2
    W = w_ada.shape[2]
    CW = conv_w.shape[2]
    me = 4 * lax.axis_index("x") + 2 * lax.axis_index("y") + lax.axis_index("c")
    x2, tgt = x[0], loss_target[0]
    w_ada2, w_in2, w_out2 = w_ada[0], w_in[0], w_out[0]

    R = D // NDEV
    core = lax.axis_index("c").astype(jnp.int32).reshape(1)

    cw_slab = jnp.zeros((8, CW), f32).at[:3].set(conv_w[0])
    b_cols = lax.dynamic_slice_in_dim(b_ada, me * W, W, axis=1)
    mod_slabs, c_blocks, cw_g = _ada_exchange(c.reshape(D // 128, 128), cw_slab, w_ada2, b_cols)
    c_all = c_blocks.reshape(NDEV, D)
    conv_w_full = jnp.transpose(cw_g, (1, 0, 2)).reshape(8, C)
    mod = mod_slabs[:, 0, :].reshape(1, 3 * D)
    shift, scale, gate = mod[:, :D], mod[:, D:2 * D], mod[:, 2 * D:]

    land_i = lax.dynamic_update_slice(lax.empty((NDEV, D, C), bf16), w_in2.astype(bf16)[None], (me, 0, 0))
    land_o = lax.dynamic_update_slice(lax.empty((NDEV, R, D), bf16), w_out2.astype(bf16)[None], (me, 0, 0))
    wi_send, wi_recv, land_i, w_token = _w_in_start(land_i, [mod_slabs])

    me_arr = me.astype(jnp.int32).reshape(1)
    h = _prenorm(x2, scale, shift, g_pre, w_token)
    land_i = _w_in_sibling(land_i, wi_recv, after=[h])
    proj = _in_proj_part("in_proj_a", h, land_i, None, me_arr, 0, 1, 2)
    fi_send, fi_recv, land_i = _w_in_relay(land_i, wi_recv, after=[proj])
    proj = _in_proj_part("in_proj_b", h, land_i, proj, me_arr, 2, 2, 2)
    land_i = _w_in_forwarded(land_i, fi_recv, after=[proj])
    proj = _in_proj_part("in_proj_c", h, land_i, proj, me_arr, 3, 2, 2)
    (di_send, di_recv, wo_send, wo_recv), land_i, land_o = _w_in_diag(land_i, land_o, fi_recv, after=[proj])
    proj = _in_proj_part("in_proj_d", h, land_i, proj, me_arr, 6, 1, 1)
    win_g = _w_in_finish(land_i, wi_send, fi_send, di_send, di_recv, after=[proj])
    proj = _in_proj_part("in_proj_e", h, win_g, proj, me_arr, 7, 1, 1)
    slopes = _head_slopes(C // HEAD_DIM)
    ycat = _conv_fwd(proj, conv_w_full, conv_b, g_conv)
    o, lse = _attn_fwd(proj, slopes)
    (fo_send, fo_recv), (land_o,), _ = _weights_forward("w_out_forward", land_o, wo_recv, after=[o])
    ycat = _attn_post(ycat, o, proj, g_attn)
    wout_g = _weights_wait("w_out_wait", land_o, wo_send, wo_recv, fo_send, fo_recv, after=[ycat])
    wout_full = wout_g.reshape(D, D)
    y = _matmul(ycat, wout_full, name="out_proj", out_dtype=f32)
    dy, dout, post_sums = _sandwich(y, x2, tgt, gate, g_post)

    chip = me // 2

    def landing(rows, cols):
        return lax.dynamic_update_slice(lax.empty((NCHIP, rows, cols), bf16), jnp.zeros((1, rows, cols), bf16),
                                        (chip, 0, 0))

    gw_out = _matmul(ycat, dy, name="out_proj_dw", out_dtype=bf16, ta=True).reshape(NDEV, R, D)
    po_send, po_recv, gw_out, pair_o, po_token = _pair_start("g_out_pair_start", gw_out)
    dycat = _matmul(dy, wout_full, name="out_proj_dx", out_dtype=f32, tb=True, dep=po_token)
    gw_out, pair_o = _pair_wait("g_out_pair_wait", gw_out, pair_o, po_send, po_recv, after=[dycat])
    sum_o = _pair_sum("g_out_pair_sum", gw_out, pair_o, core)
    co_send, co_recv, sum_o, land_go, co_token = _chip_start(
        "g_out_chip_start", sum_o, landing(R, D), 0)
    dpc, conv_sums = _conv_bwd(proj, dycat, conv_w_full, conv_b, g_conv, co_token)
    gw_c = _matmul(h, dpc, name="in_proj_dw_conv", out_dtype=bf16, ta=True, out_slots=4)
    pc_send, pc_recv, gw_c, pair_c, pc_token = _pair_start("g_conv_pair_start", gw_c)
    do, dpa, attn_sums = _attn_post_bwd(o, proj, dycat, g_attn, pc_token)
    gw_c, pair_c = _pair_wait("g_conv_pair_wait", gw_c, pair_c, pc_send, pc_recv, after=[do])
    sum_c = _pair_sum("g_conv_pair_sum", gw_c, pair_c, core)
    cc_send, cc_recv, sum_c, land_gi, cc_token = _chip_start(
        "g_conv_chip_start", sum_c, landing(D, C), 0)
    dpa = _attn_bwd(proj, o, do, lse, slopes, dpa, cc_token)
    gw_a = _matmul(h, dpa, name="in_proj_dw_attn", out_dtype=bf16, ta=True, b_slots=True, out_slots=4)
    pa_send, pa_recv, gw_a, pair_a, pa_token = _pair_start("g_attn_pair_start", gw_a)
    sum_o, land_go = _chip_wait("g_out_chip_wait", sum_o, land_go, co_send, co_recv, 0, after=[pa_token])
    pick_out = jnp.stack([jnp.int32(0), me // 2]).astype(jnp.int32)
    g_w_out, d_w_out, nm_w_out, nv_w_out = _adamw_sharded(
        "adamw_w_out", land_go, sum_o, sum_o, pick_out, w_out2, m_w_out[0], v_w_out[0])
    gw_a, pair_a = _pair_wait("g_attn_pair_wait", gw_a, pair_a, pa_send, pa_recv, after=[g_w_out])
    sum_a = _pair_sum("g_attn_pair_sum", gw_a, pair_a, core)
    part_a, part_b = (0, 3 * D // 4), (3 * D // 4, D // 4)
    ca_send, ca_recv, sum_a, land_gi, ca_token = _chip_start("g_attn_chip_start_a", sum_a, land_gi, 4, part_a)
    dh = _matmul_slabs_t(dpc, dpa, win_g, name="in_proj_dx", dep=ca_token)
    grad_x, pre_sums = _prenorm_bwd(dh, x2, dout, scale, g_pre)

    small = jnp.concatenate([pre_sums[0:1], pre_sums[1:2], post_sums[0:1],
                             pre_sums[2:3], post_sums[1:2],
                             conv_sums[2:3], conv_sums[3:4], conv_sums[4:5],
                             conv_sums[1:2], conv_sums[0:1], attn_sums[0:1]], axis=1)
    small = jnp.concatenate([small.reshape(8 * D // 128, 128), jnp.broadcast_to(post_sums[2:3, :128], (8, 128))])
    (small_all,) = _all_gather([small], "gather_small")
    cb_send, cb_recv, sum_a, land_gi, cb_token = _chip_start("g_attn_chip_start_b", sum_a, land_gi, 4, part_b,
                                                             after=[small_all])
    small_all = small_all.reshape(NDEV, small.size)
    tot = _sum_rows(small_all, cb_token)
    loss = tot[0, 8 * D]
    g_conv_w = lax.dynamic_slice_in_dim(tot[0:1, 5 * D:5 * D + 3 * C].reshape(1, 3, C), me * CW, CW, axis=2)
    ((g_b_ada, d_b_ada, nm_b_ada, nv_b_ada), (g_g_pre, d_g_pre, nm_g_pre, nv_g_pre),
     (g_g_post, d_g_post, nm_g_post, nv_g_post), (g_conv_w, d_conv_w, nm_conv_w, nv_conv_w),
     (g_conv_b, d_conv_b, nm_conv_b, nv_conv_b), (g_g_conv, d_g_conv, nm_g_conv, nv_g_conv),
     (g_g_attn, d_g_attn, nm_g_attn, nv_g_attn)) = _adamw_small(tot, [
         (b_ada, m_b_ada, v_b_ada, 0), (g_pre, m_g_pre, v_g_pre, 3 * D), (g_post, m_g_post, v_g_post, 4 * D),
         (conv_w, m_conv_w, v_conv_w, g_conv_w), (conv_b, m_conv_b, v_conv_b, 5 * D + 3 * C),
         (g_conv, m_g_conv, v_g_conv, 5 * D + 4 * C), (g_attn, m_g_attn, v_g_attn, 5 * D + 5 * C)])

    dmod_cols = lax.dynamic_slice_in_dim(small_all[:, :3 * D], me * W, W, axis=1)
    g_w_ada, d_w_ada, nm_w_ada, nv_w_ada = _adamw_ada(c_all.T, dmod_cols, w_ada2, m_w_ada[0], v_w_ada[0], cb_token)

    pick_in = jnp.stack([me // 4, (me % 4) // 2]).astype(jnp.int32)
    sum_c, land_gi = _chip_wait("g_conv_chip_wait", sum_c, land_gi, cc_send, cc_recv, 0, after=[g_w_ada])
    sum_a, land_gi = _chip_wait("g_attn_chip_wait_a", sum_a, land_gi, ca_send, ca_recv, 4, [g_w_ada], part_a)
    first = _adamw_sharded("adamw_w_in_a", land_gi, sum_c, sum_a, pick_in, w_in2, m_w_in[0], v_w_in[0], rows=part_a,
                           tr=256)
    sum_a, land_gi = _chip_wait("g_attn_chip_wait_b", sum_a, land_gi, cb_send, cb_recv, 4, [first[0]], part_b)
    g_w_in, d_w_in, nm_w_in, nv_w_in = _adamw_sharded(
        "adamw_w_in_b", land_gi, sum_c, sum_a, pick_in, w_in2, m_w_in[0], v_w_in[0], rows=part_b, prev=first, tr=256)

    return (loss, grad_x[None],
            g_w_ada[None], g_b_ada, g_g_pre, g_w_in[None], g_conv_w, g_conv_b, g_g_conv, g_g_attn, g_w_out[None], g_g_post,
            d_w_ada[None], d_b_ada, d_g_pre, d_w_in[None], d_conv_w, d_conv_b, d_g_conv, d_g_attn, d_w_out[None], d_g_post,
            nm_w_ada[None], nm_b_ada, nm_g_pre, nm_w_in[None], nm_conv_w, nm_conv_b, nm_g_conv, nm_g_attn, nm_w_out[None], nm_g_post,
            nv_w_ada[None], nv_b_ada, nv_g_pre, nv_w_in[None], nv_conv_w, nv_conv_b, nv_g_conv, nv_g_attn, nv_w_out[None], nv_g_post)
```

```python
import functools
import math

import jax
import jax.numpy as jnp
from jax import lax
from jax.experimental import pallas as pl
from jax.experimental.pallas import tpu as pltpu

f32 = jnp.float32
bf16 = jnp.bfloat16

NDEV = 8
HEAD_DIM = 64
PAIR = 2 * HEAD_DIM
BRANCHES = ((128, 1, 1), (512, 4, 1), (2048, 16, 2))
HALF_WIN = 64
EPS = 1e-6
NEG_INF = -1e30
ADAM_LR, ADAM_B1, ADAM_B2, ADAM_EPS, ADAM_WD, ADAM_STEP = 0.001, 0.9, 0.999, 1e-08, 0.01, 10
MESH = pl.DeviceIdType.MESH
VMEM_LIMIT = 56 * 1024 * 1024
HBM_SPEC = pl.BlockSpec(memory_space=pltpu.HBM)
ANY_SPEC = pl.BlockSpec(memory_space=pl.ANY)
SEM_SPEC = pl.BlockSpec(memory_space=pltpu.SEMAPHORE)


def _params(*sem):
    return pltpu.CompilerParams(dimension_semantics=sem or None, vmem_limit_bytes=VMEM_LIMIT)


def _silu(z):
    return z * jax.nn.sigmoid(z)


def _silu_and_slope(z):
    s = jax.nn.sigmoid(z)
    return z * s, s * (1.0 + z * (1.0 - s))


def _my_place():
    x, y, c = lax.axis_index("x"), lax.axis_index("y"), lax.axis_index("c")
    return x, y, c, 4 * x + 2 * y + c


def _peer(x, y, c, k):
    px, py, pc = x ^ (k >> 2 & 1), y ^ (k >> 1 & 1), c ^ (k & 1)
    return (px, py, pc), 4 * px + 2 * py + pc


def _all_gather(arrays, name):
    n = len(arrays)

    def body(*refs):
        srcs, dsts = refs[:n], refs[n:2 * n]
        send_sems, recv_sems, local_sems = refs[2 * n:]
        x, y, c, me = _my_place()
        locals_, sends = [], []
        for t in range(n):
            own = pltpu.make_async_copy(srcs[t], dsts[t].at[me], local_sems.at[t])
            own.start()
            locals_.append(own)
            for k in range(1, NDEV):
                peer, pidx = _peer(x, y, c, k)
                cp = pltpu.make_async_remote_copy(
                    src_ref=srcs[t], dst_ref=dsts[t].at[me], send_sem=send_sems.at[t, k],
                    recv_sem=recv_sems.at[t, k], device_id=peer, device_id_type=MESH)
                cp.start()
                sends.append(cp)
        for t in range(n):
            for k in range(1, NDEV):
                peer, pidx = _peer(x, y, c, k)
                pltpu.make_async_remote_copy(
                    src_ref=srcs[t], dst_ref=dsts[t].at[pidx], send_sem=send_sems.at[t, k],
                    recv_sem=recv_sems.at[t, k], device_id=peer, device_id_type=MESH).wait_recv()
        for cp in sends:
            cp.wait_send()
        for cp in locals_:
            cp.wait()

    return pl.pallas_call(
        body, name=name,
        out_shape=tuple(jax.ShapeDtypeStruct((NDEV,) + a.shape, a.dtype) for a in arrays),
        in_specs=[HBM_SPEC] * n, out_specs=tuple([HBM_SPEC] * n),
        scratch_shapes=[pltpu.SemaphoreType.DMA((n, NDEV)), pltpu.SemaphoreType.DMA((n, NDEV)),
                        pltpu.SemaphoreType.DMA((n,))],
    )(*arrays)


def _comm_call(name, arrays, sems, new_sems, body, after=(), token=False):
    na, ns, nn, nf = len(arrays), len(sems), len(new_sems), len(after)

    def kern(*refs):
        ins, outs = refs[:na + ns + nf], refs[na + ns + nf:]
        body(ins[:na], ins[na:na + ns], outs[:nn])
        if token:
            outs[nn + na][...] = jnp.zeros((8, 128), f32)

    out_shape = ([pltpu.SemaphoreType.DMA(s) for s in new_sems] + [pltpu.HBM(a.shape, a.dtype) for a in arrays]
                 + ([jax.ShapeDtypeStruct((8, 128), f32)] if token else []))
    out_specs = [SEM_SPEC] * nn + [HBM_SPEC] * na + ([pl.BlockSpec(memory_space=pltpu.VMEM)] if token else [])
    res = pl.pallas_call(
        kern, name=name, out_shape=tuple(out_shape),
        in_specs=[HBM_SPEC] * na + [SEM_SPEC] * ns + [ANY_SPEC] * nf, out_specs=tuple(out_specs),
        input_output_aliases={t: nn + t for t in range(na)},
        compiler_params=pltpu.CompilerParams(has_side_effects=pltpu.SideEffectType.DATAFLOW_SIDE_EFFECTING),
    )(*[pltpu.with_memory_space_constraint(a, pltpu.HBM) for a in arrays], *sems, *after)
    return list(res[:nn]), list(res[nn:nn + na]), (res[nn + na] if token else None)


def _remote(src, dst, send_sem, recv_sem, device):
    return pltpu.make_async_remote_copy(src_ref=src, dst_ref=dst, send_sem=send_sem, recv_sem=recv_sem,
                                        device_id=device, device_id_type=MESH)


SAME_CORE = (2, 4, 6)
VIA_SIBLING = (3, 5, 7)


def _weights_forward(name, land, recv, after):
    def body(a, s, new):
        (land,), (recv,), (fsend, frecv) = a, s, new
        x, y, c, me = _my_place()
        sibling, _ = _peer(x, y, c, 1)
        for k in SAME_CORE:
            peer, slot = _peer(x, y, c, k)
            _remote(land.at[slot], land.at[slot], fsend.at[k], recv.at[k], peer).wait_recv()
            _remote(land.at[slot], land.at[slot], fsend.at[k], frecv.at[k ^ 1], sibling).start()

    return _comm_call(name, [land], [recv], [(NDEV,), (NDEV,)], body, after=after)


def _weights_wait(name, land, send, recv, fsend, frecv, after):
    def body(a, s, new):
        (land,), (send, recv, fsend, frecv) = a, s
        x, y, c, me = _my_place()
        sibling, sib_slot = _peer(x, y, c, 1)
        _remote(land.at[sib_slot], land.at[sib_slot], send.at[1], recv.at[1], sibling).wait_recv()
        for k in VIA_SIBLING:
            _, slot = _peer(x, y, c, k)
            _remote(land.at[slot], land.at[slot], fsend.at[k ^ 1], frecv.at[k], sibling).wait_recv()
        for k in (1,) + SAME_CORE:
            peer, _ = _peer(x, y, c, k)
            _remote(land.at[me], land.at[me], send.at[k], recv.at[k], peer).wait_send()
        for k in SAME_CORE:
            _, slot = _peer(x, y, c, k)
            _remote(land.at[slot], land.at[slot], fsend.at[k], frecv.at[k ^ 1], sibling).wait_send()

    return _comm_call(name, [land], [send, recv, fsend, frecv], [], body, after=after)[1][0]


def _diag_relay(x, y, c):
    slot = 4 * (x ^ (1 - c)) + 2 * (y ^ c) + c
    return slot, (x ^ c, y ^ (1 - c), c)


def _w_in_start(land, after):
    def body(a, s, new):
        (land,), (send, recv) = a, new
        x, y, c, me = _my_place()
        for k in (1, 2, 4):
            peer, _ = _peer(x, y, c, k)
            _remote(land.at[me], land.at[me], send.at[k], recv.at[k], peer).start()

    (send, recv), (land,), token = _comm_call("w_in_start", [land], [], [(NDEV,), (NDEV,)], body, after=after, token=True)
    return send, recv, land, token


def _w_in_sibling(land, recv, after):
    def body(a, s, new):
        (land,), (recv,) = a, s
        x, y, c, me = _my_place()
        sibling, slot = _peer(x, y, c, 1)
        _remote(land.at[slot], land.at[slot], recv.at[1], recv.at[1], sibling).wait_recv()

    return _comm_call("w_in_sibling", [land], [recv], [], body, after=after)[1][0]


def _w_in_relay(land, recv, after):
    def body(a, s, new):
        (land,), (recv,), (fsend, frecv) = a, s, new
        x, y, c, me = _my_place()
        sibling, _ = _peer(x, y, c, 1)
        for k in (2, 4):
            peer, slot = _peer(x, y, c, k)
            _remote(land.at[slot], land.at[slot], fsend.at[k], recv.at[k], peer).wait_recv()
        slot, target = _diag_relay(x, y, c)
        _remote(land.at[slot], land.at[slot], fsend.at[6], frecv.at[6], target).start()
        for k in (2, 4):
            _, slot = _peer(x, y, c, k)
            _remote(land.at[slot], land.at[slot], fsend.at[k], frecv.at[k ^ 1], sibling).start()

    (fsend, frecv), (land,), _ = _comm_call("w_in_relay", [land], [recv], [(NDEV,), (NDEV,)], body, after=after)
    return fsend, frecv, land


def _w_in_forwarded(land, frecv, after):
    def body(a, s, new):
        (land,), (frecv,) = a, s
        x, y, c, me = _my_place()
        sibling, _ = _peer(x, y, c, 1)
        for k in (3, 5):
            _, slot = _peer(x, y, c, k)
            _remote(land.at[slot], land.at[slot], frecv.at[k], frecv.at[k], sibling).wait_recv()

    return _comm_call("w_in_forwarded", [land], [frecv], [], body, after=after)[1][0]


def _w_in_diag(land, land_o, frecv, after):
    def body(a, s, new):
        (land, land_o), (frecv,), (dsend, drecv, osend, orecv) = a, s, new
        x, y, c, me = _my_place()
        sibling, _ = _peer(x, y, c, 1)
        peer, slot = _peer(x, y, c, 6)
        _remote(land.at[slot], land.at[slot], dsend.at[6], frecv.at[6], peer).wait_recv()
        _remote(land.at[slot], land.at[slot], dsend.at[6], drecv.at[7], sibling).start()
        for k in (1,) + SAME_CORE:
            peer, _ = _peer(x, y, c, k)
            _remote(land_o.at[me], land_o.at[me], osend.at[k], orecv.at[k], peer).start()

    sems, (land, land_o), _ = _comm_call("w_in_diag", [land, land_o], [frecv], [(NDEV,)] * 4, body, after=after)
    return sems, land, land_o


def _w_in_finish(land, send, fsend, dsend, drecv, after):
    def body(a, s, new):
        (land,), (send, fsend, dsend, drecv) = a, s
        x, y, c, me = _my_place()
        sibling, _ = _peer(x, y, c, 1)
        _, slot = _peer(x, y, c, 7)
        _remote(land.at[slot], land.at[slot], dsend.at[6], drecv.at[7], sibling).wait_recv()
        for k in (1, 2, 4):
            peer, _ = _peer(x, y, c, k)
            _remote(land.at[me], land.at[me], send.at[k], send.at[k], peer).wait_send()
        for k in (2, 4, 6):
            _, slot = _peer(x, y, c, k)
            _remote(land.at[slot], land.at[slot], fsend.at[k], fsend.at[k], sibling).wait_send()
        _, slot = _peer(x, y, c, 6)
        _remote(land.at[slot], land.at[slot], dsend.at[6], dsend.at[6], sibling).wait_send()

    return _comm_call("w_in_finish", [land], [send, fsend, dsend, drecv], [], body, after=after)[1][0]


def _in_proj_part(name, h, land, proj, me_arr, k0, kstep, nk, tm=512):
    S, D = h.shape
    C = land.shape[2]
    tm = min(tm, S)

    def body(me_ref, a_ref, b_ref, *rest):
        rest[-1][...] = jnp.dot(a_ref[...], b_ref[...], preferred_element_type=f32)

    slot = lambda j, me: me[0] ^ (k0 + kstep * j)
    args = [h, land] + ([] if proj is None else [proj])
    grid_spec = pltpu.PrefetchScalarGridSpec(
        num_scalar_prefetch=1, grid=(nk, S // tm),
        in_specs=[pl.BlockSpec((tm, D), lambda j, i, me: (i, 0)),
                  pl.BlockSpec((None, D, C), lambda j, i, me: (slot(j, me), 0, 0))] + [ANY_SPEC] * (len(args) - 2),
        out_specs=pl.BlockSpec((tm, C), lambda j, i, me: (i, slot(j, me))))
    return pl.pallas_call(
        body, name=name, out_shape=jax.ShapeDtypeStruct((S, NDEV * C), f32), grid_spec=grid_spec,
        input_output_aliases={} if proj is None else {3: 0}, compiler_params=_params("arbitrary", "arbitrary"),
    )(me_arr, *args)


NCHIP = NDEV // 2


def _pair_start(name, src):
    npair = src.shape[0] // 2

    def body(a, s, new):
        (src, pair), (send, recv) = a, new
        x, y, c, me = _my_place()
        sibling, _ = _peer(x, y, c, 1)
        for i in range(npair):
            _remote(src.at[2 * i + 1 - c], pair.at[i], send.at[i], recv.at[i], sibling).start()

    pair = lax.empty((npair,) + src.shape[1:], src.dtype)
    (send, recv), (src, pair), token = _comm_call(name, [src, pair], [], [(npair,), (npair,)], body, token=True)
    return send, recv, src, pair, token


def _pair_wait(name, src, pair, send, recv, after):
    npair = pair.shape[0]

    def body(a, s, new):
        (src, pair), (send, recv) = a, s
        x, y, c, me = _my_place()
        sibling, _ = _peer(x, y, c, 1)
        for i in range(npair):
            cp = _remote(src.at[2 * i + 1 - c], pair.at[i], send.at[i], recv.at[i], sibling)
            cp.wait_recv()
            cp.wait_send()

    return _comm_call(name, [src, pair], [send, recv], [], body, after=after)[1]


def _pair_sum(name, src, pair, core, tr=1024):
    npair, R, Cc = pair.shape
    tr = min(tr, R)

    def body(core_ref, a_ref, b_ref, o_ref):
        o_ref[...] = (a_ref[...].astype(f32) + b_ref[...].astype(f32)).astype(o_ref.dtype)

    grid_spec = pltpu.PrefetchScalarGridSpec(
        num_scalar_prefetch=1, grid=(npair, R // tr),
        in_specs=[pl.BlockSpec((None, tr, Cc), lambda i, r, core: (2 * i + core[0], r, 0)),
                  pl.BlockSpec((None, tr, Cc), lambda i, r, core: (i, r, 0))],
        out_specs=pl.BlockSpec((None, tr, Cc), lambda i, r, core: (i, r, 0)))
    return pl.pallas_call(body, name=name, out_shape=jax.ShapeDtypeStruct(pair.shape, pair.dtype),
                          grid_spec=grid_spec, compiler_params=_params("parallel", "parallel"))(core, src, pair)


def _owner_chip(first, i):
    q = first // 2 + i
    return q >> 1 & 1, q & 1


def _chip_start(name, sums, land, first, rows=None, after=()):
    npair = sums.shape[0]
    rows = pl.ds(*(rows or (0, sums.shape[1])))

    def body(a, s, new):
        (sums, land), (send, recv) = a, new
        x, y, c, me = _my_place()
        for i in range(npair):
            ox, oy = _owner_chip(first, i)

            @pl.when((x != ox) | (y != oy))
            def _():
                _remote(sums.at[i, rows], land.at[2 * x + y, rows], send.at[i], recv.at[2 * x + y], (ox, oy, c)).start()

    (send, recv), (sums, land), token = _comm_call(name, [sums, land], [], [(npair,), (NCHIP,)], body, after=after,
                                                   token=True)
    return send, recv, sums, land, token


def _chip_wait(name, sums, land, send, recv, first, after, rows=None):
    npair = sums.shape[0]
    rows = pl.ds(*(rows or (0, sums.shape[1])))

    def body(a, s, new):
        (sums, land), (send, recv) = a, s
        x, y, c, me = _my_place()
        mine = (me >= first) & (me < first + 2 * npair)
        for i in range(npair):
            ox, oy = _owner_chip(first, i)

            @pl.when((x != ox) | (y != oy))
            def _():
                _remote(sums.at[i, rows], land.at[2 * x + y, rows], send.at[i], recv.at[2 * x + y], (ox, oy, c)).wait_send()
        for q in range(NCHIP):
            @pl.when(mine & (2 * x + y != q))
            def _():
                _remote(sums.at[0, rows], land.at[q, rows], send.at[0], recv.at[q], (q >> 1, q & 1, c)).wait_recv()

    return _comm_call(name, [sums, land], [send, recv], [], body, after=after)[1]


def _matmul(a, b, *, name, out_dtype, ta=False, tb=False, b_slots=False, out_slots=0, b_cols=None,
            tm=1024, tn=1024, tk=2048, dep=None):
    M, K = (a.shape[1], a.shape[0]) if ta else a.shape
    col0 = 0
    if b_slots:
        slab = b.shape[2]
        N = b.shape[1] if tb else b.shape[0] * slab
        assert (K if tb else N) == b.shape[0] * slab
    elif b_cols is not None:
        assert not tb
        col0, N = b_cols
    else:
        N = b.shape[0] if tb else b.shape[1]
    tm, tn, tk = min(tm, M), min(tn, N), min(tk, K)
    if b_slots:
        if tb:
            tk = min(tk, slab)
        else:
            tn = min(tn, slab)
    if out_slots:
        tn = min(tn, N // out_slots)
    nm, nn, nk = M // tm, N // tn, K // tk
    assert (nm * tm, nn * tn, nk * tk) == (M, N, K) and col0 % tn == 0, (name, M, N, K, tm, tn, tk)
    j0 = col0 // tn

    a_spec = pl.BlockSpec((tk, tm), lambda i, j, k: (k, i)) if ta else pl.BlockSpec((tm, tk), lambda i, j, k: (i, k))
    if b_slots and tb:
        per = slab // tk
        b_spec = pl.BlockSpec((None, tn, tk), lambda i, j, k: (k // per, j, k % per))
    elif b_slots:
        per = slab // tn
        b_spec = pl.BlockSpec((None, tk, tn), lambda i, j, k: (j // per, k, j % per))
    elif tb:
        b_spec = pl.BlockSpec((tn, tk), lambda i, j, k: (j, k))
    else:
        b_spec = pl.BlockSpec((tk, tn), lambda i, j, k: (k, j + j0))
    if out_slots:
        per_o = (N // out_slots) // tn
        o_spec = pl.BlockSpec((None, tm, tn), lambda i, j, k: (j // per_o, i, j % per_o))
        out_shape = jax.ShapeDtypeStruct((out_slots, M, N // out_slots), out_dtype)
    else:
        o_spec = pl.BlockSpec((tm, tn), lambda i, j, k: (i, j))
        out_shape = jax.ShapeDtypeStruct((M, N), out_dtype)
    dims = (((0 if ta else 1,), (1 if tb else 0,)), ((), ()))
    deps = [] if dep is None else [dep]

    def body(a_ref, b_ref, *rest):
        o_ref = rest[len(deps)]
        prod = lax.dot_general(a_ref[...], b_ref[...], dims, preferred_element_type=f32)
        if nk == 1:
            o_ref[...] = prod.astype(out_dtype)
            return
        acc_ref = rest[len(deps) + 1]
        k = pl.program_id(2)

        @pl.when(k == 0)
        def _():
            acc_ref[...] = prod

        @pl.when((k > 0) & (k < nk - 1))
        def _():
            acc_ref[...] += prod

        @pl.when(k == nk - 1)
        def _():
            o_ref[...] = (acc_ref[...] + prod).astype(out_dtype)

    return pl.pallas_call(
        body, name=name, out_shape=out_shape, grid=(nm, nn, nk),
        in_specs=[a_spec, b_spec] + [ANY_SPEC] * len(deps), out_specs=o_spec,
        scratch_shapes=[pltpu.VMEM((tm, tn), f32)] if nk > 1 else [],
        compiler_params=_params("parallel", "parallel", "arbitrary"),
    )(a, b, *deps)


def _matmul_slabs_t(a_cols, a_slots, b, *, name, tm=512, tn=512, dep=None):
    M = a_cols.shape[0]
    n_slab, N, slab = b.shape
    n1, n2 = a_cols.shape[1] // slab, a_slots.shape[0]
    assert n1 + n2 == n_slab and a_slots.shape[1:] == (M, slab)
    tm, tn = min(tm, M), min(tn, N)
    deps = [] if dep is None else [dep]

    def body(a1_ref, a2_ref, b_ref, *rest):
        o_ref = rest[len(deps)]
        acc = None
        for s in range(n_slab):
            lhs = a1_ref[:, s * slab:(s + 1) * slab] if s < n1 else a2_ref[s - n1]
            prod = lax.dot_general(lhs, b_ref[s], (((1,), (1,)), ((), ())), preferred_element_type=f32)
            acc = prod if acc is None else acc + prod
        o_ref[...] = acc

    return pl.pallas_call(
        body, name=name, out_shape=jax.ShapeDtypeStruct((M, N), f32), grid=(M // tm, N // tn),
        in_specs=[pl.BlockSpec((tm, n1 * slab), lambda i, j: (i, 0)), pl.BlockSpec((n2, tm, slab), lambda i, j: (0, i, 0)),
                  pl.BlockSpec((n_slab, tn, slab), lambda i, j: (0, j, 0))] + [ANY_SPEC] * len(deps),
        out_specs=pl.BlockSpec((tm, tn), lambda i, j: (i, j)), compiler_params=_params("parallel", "parallel"),
    )(a_cols, a_slots, b, *deps)


def _ada_exchange(c_blk, cw_slab, w_ada, b_cols):
    nblk = c_blk.shape[0]
    D, W = w_ada.shape
    CW = cw_slab.shape[1]

    def body(c_ref, cw_ref, w_ref, b_ref, mod_ref, call_ref, cwg_ref, msend, send_sems, recv_sems):
        x, y, c, me = _my_place()
        call_ref[me] = _silu(c_ref[...])
        cwg_ref[me] = cw_ref[...]
        first = []
        for k in range(1, NDEV):
            peer, _ = _peer(x, y, c, k)
            first.append(_remote(call_ref.at[me], call_ref.at[me], send_sems.at[0, k], recv_sems.at[0, k], peer))
            first.append(_remote(cwg_ref.at[me], cwg_ref.at[me], send_sems.at[1, k], recv_sems.at[1, k], peer))
        for cp in first:
            cp.start()
        for k in range(1, NDEV):
            peer, slot = _peer(x, y, c, k)
            _remote(call_ref.at[slot], call_ref.at[slot], send_sems.at[0, k], recv_sems.at[0, k], peer).wait_recv()
            _remote(cwg_ref.at[slot], cwg_ref.at[slot], send_sems.at[1, k], recv_sems.at[1, k], peer).wait_recv()
        mod = jnp.broadcast_to(b_ref[...], (NDEV, W))
        for r in range(nblk):
            mod = mod + lax.dot_general(call_ref[:, r, :], w_ref[r * 128:(r + 1) * 128, :], (((1,), (0,)), ((), ())),
                                        preferred_element_type=f32, precision=lax.Precision.HIGHEST)
        row = lax.broadcasted_iota(jnp.int32, (NDEV, 1), 0)
        pick = lambda j: jnp.broadcast_to(jnp.sum(jnp.where(row == j, mod, 0.0), axis=0, keepdims=True), (8, W))
        mod_ref[me] = pick(me)
        second = []
        for k in range(1, NDEV):
            peer, slot = _peer(x, y, c, k)
            msend[k] = pick(slot)
            second.append(_remote(msend.at[k], mod_ref.at[me], send_sems.at[2, k], recv_sems.at[2, k], peer))
        for cp in second:
            cp.start()
        for k in range(1, NDEV):
            peer, slot = _peer(x, y, c, k)
            _remote(msend.at[k], mod_ref.at[slot], send_sems.at[2, k], recv_sems.at[2, k], peer).wait_recv()
        for cp in first + second:
            cp.wait_send()

    vmem = pl.BlockSpec(memory_space=pltpu.VMEM)
    return pl.pallas_call(
        body, name="ada_exchange",
        out_shape=(jax.ShapeDtypeStruct((NDEV, 8, W), f32), jax.ShapeDtypeStruct((NDEV, nblk, 128), f32),
                   jax.ShapeDtypeStruct((NDEV, 8, CW), f32)),
        in_specs=[vmem] * 4, out_specs=(vmem, vmem, vmem),
        scratch_shapes=[pltpu.VMEM((NDEV, 8, W), f32), pltpu.SemaphoreType.DMA((3, NDEV)),
                        pltpu.SemaphoreType.DMA((3, NDEV))],
        compiler_params=_params(),
    )(c_blk, cw_slab, w_ada, b_cols)


def _prenorm(x, scale, shift, g_pre, dep, tr=512):
    S, D = x.shape
    tr = min(tr, S)

    def body(x_ref, sc_ref, sh_ref, g_ref, dep_ref, h_ref):
        xv = x_ref[...]
        r = lax.rsqrt(jnp.mean(xv * xv, axis=-1, keepdims=True) + EPS)
        h_ref[...] = ((xv * r) * g_ref[...] * (1.0 + sc_ref[...]) + sh_ref[...]).astype(bf16)

    row = pl.BlockSpec((tr, D), lambda i: (i, 0))
    vec = pl.BlockSpec((1, D), lambda i: (0, 0))
    return pl.pallas_call(body, name="prenorm", out_shape=jax.ShapeDtypeStruct((S, D), bf16), grid=(S // tr,),
                          in_specs=[row, vec, vec, vec, ANY_SPEC], out_specs=row, compiler_params=_params("parallel"))(
                              x, scale, shift, g_pre, dep)


def _ext_rows(i, tr, S):
    g = lax.broadcasted_iota(jnp.int32, (tr + 16, 1), 0) + (i * tr - 8)
    return (g >= 0) & (g < S)


def _halo_specs(tr, S, C, col):
    nb8 = S // 8
    main = pl.BlockSpec((tr, C), lambda i: (i, col))
    prev = pl.BlockSpec((8, C), lambda i: (jnp.maximum(i * (tr // 8) - 1, 0), col))
    nxt = pl.BlockSpec((8, C), lambda i: (jnp.minimum((i + 1) * (tr // 8), nb8 - 1), col))
    return prev, main, nxt


def _conv_fwd(proj, conv_w, conv_b, g_conv, tr=512):
    S, C = proj.shape[0], proj.shape[1] // 8
    tr = min(tr, S)

    def body(up, um, un, cp, cm, cn, bg_ref, zc_ref, w_ref, cb_ref, g_ref, o_ref):
        i = pl.program_id(0)
        exists = _ext_rows(i, tr, S)
        u = jnp.concatenate([up[...], um[...], un[...]], axis=0)
        cg = jnp.concatenate([cp[...], cm[...], cn[...]], axis=0)
        t = jnp.where(exists, cg * u, 0.0)
        t_before = pltpu.roll(t, 1, 0)[8:tr + 8]
        t_after = pltpu.roll(t, tr + 15, 0)[8:tr + 8]
        w = w_ref[...]
        cv = w[0:1] * t_before + w[1:2] * t[8:tr + 8] + w[2:3] * t_after + cb_ref[...]
        yc = bg_ref[...] * cv
        rc = lax.rsqrt(jnp.mean(yc * yc, axis=-1, keepdims=True) + EPS)
        o_ref[...] = ((yc * rc) * g_ref[...] * _silu(zc_ref[...])).astype(bf16)

    u_specs = _halo_specs(tr, S, C, 0)
    c_specs = _halo_specs(tr, S, C, 2)
    vec = pl.BlockSpec((1, C), lambda i: (0, 0))
    return pl.pallas_call(
        body, name="conv_fwd", out_shape=jax.ShapeDtypeStruct((S, 2 * C), bf16), grid=(S // tr,),
        in_specs=[*u_specs, *c_specs, pl.BlockSpec((tr, C), lambda i: (i, 1)), pl.BlockSpec((tr, C), lambda i: (i, 3)),
                  pl.BlockSpec((8, C), lambda i: (0, 0)), vec, vec],
        out_specs=pl.BlockSpec((tr, C), lambda i: (i, 0)), compiler_params=_params("parallel"),
    )(proj, proj, proj, proj, proj, proj, proj, proj, conv_w, conv_b, g_conv)


def _branch_geometry(S, r, inter):
    L = S // r * inter
    nq = min(128, L)
    nk = min(nq + 2 * HALF_WIN * inter, L)
    assert L % nq == 0 and (L == nk or L >= nq + 2 * HALF_WIN * inter)
    return L, nq, nk, L // nq


QUAD = 4


def _to_quad(dst, src, S):
    n = S // QUAD
    for rho in range(QUAD):
        dst[pl.ds(rho * n, n), :] = src[pl.ds(rho, n, stride=QUAD), :]


def _block_rows(idx, r, inter, S, L, nq, nk, nblk):
    rho, qb = (0, idx) if r == 1 else (idx // nblk, idx % nblk)
    i0 = qb * nq
    ws = jnp.clip(i0 - HALF_WIN * inter, 0, L - nk)
    if r == 1:
        return pl.ds(pl.multiple_of(i0, 8), nq), pl.ds(pl.multiple_of(ws, 8), nk), i0 - ws
    assert r % (QUAD * inter) == 0
    step = r // QUAD // inter
    base = (rho % QUAD) * (S // QUAD) + rho // QUAD
    if step == 1:
        return pl.ds(pl.multiple_of(base + i0, 8), nq), pl.ds(pl.multiple_of(base + ws, 8), nk), i0 - ws
    return pl.ds(base + step * i0, nq, stride=step), pl.ds(base + step * ws, nk, stride=step), i0 - ws


N_CASES = 3
SCALE = HEAD_DIM ** -0.5
ATTN_UNROLL = 16


def _bias_shape(S):
    shapes = [_branch_geometry(S, r, inter)[1:3] for _, r, inter in BRANCHES]
    return (len(BRANCHES) * N_CASES * 2, max(nq for nq, _ in shapes), max(nk for _, nk in shapes))


def _bias_index(b, case, head):
    return (b * N_CASES + case) * 2 + head


def _fill_bias(bias_scr, sl_ref, S):
    sl = sl_ref[...]
    slope = (sl[0:1, 0:1], sl[0:1, HEAD_DIM:HEAD_DIM + 1])
    for b, (_, r, inter) in enumerate(BRANCHES):
        L, nq, nk, nblk = _branch_geometry(S, r, inter)
        rel = lax.broadcasted_iota(jnp.int32, (nq, nk), 0) - lax.broadcasted_iota(jnp.int32, (nq, nk), 1)
        for case in range(N_CASES):
            d = jnp.abs(rel + case * HALF_WIN)
            valid = d <= HALF_WIN * inter
            if inter > 1:
                valid = valid & (jnp.bitwise_and(d, inter - 1) == 0)
            dist = d.astype(f32) * float(r // inter)
            for head in range(2):
                bias_scr[_bias_index(b, case, head), 0:nq, 0:nk] = jnp.where(valid, -slope[head] * dist, NEG_INF)


def _head_slopes(n_heads):
    slopes = 2.0 ** (-8.0 * jnp.arange(1, n_heads + 1, dtype=f32) / n_heads)
    return jnp.broadcast_to(jnp.repeat(slopes.reshape(n_heads // 2, 2), HEAD_DIM, axis=1)[:, None, :],
                            (n_heads // 2, 8, PAIR))


def _attn_fwd(proj, slopes):
    S, C = proj.shape[0], proj.shape[1] // 8
    npair = C // PAIR

    def body(q_ref, k_ref, v_ref, sl_ref, o_ref, lse_ref, m_scr, l_scr, a_scr, bias_scr, q4_scr, k4_scr, v4_scr):
        lane = lax.broadcasted_iota(jnp.int32, (1, PAIR), 1)
        first = lane < HEAD_DIM
        _fill_bias(bias_scr, sl_ref, S)
        for dst, src in ((q4_scr, q_ref), (k4_scr, k_ref), (v4_scr, v_ref)):
            _to_quad(dst, src, S)

        for b, (_, r, inter) in enumerate(BRANCHES):
            L, nq, nk, nblk = _branch_geometry(S, r, inter)
            qs, ks, vs = (q_ref, k_ref, v_ref) if r == 1 else (q4_scr, k4_scr, v4_scr)

            def step(idx, carry, b=b, r=r, L=L, nq=nq, nk=nk, nblk=nblk, qs=qs, ks=ks, vs=vs):
                qrows, krows, off = _block_rows(idx, r, inter, S, L, nq, nk, nblk)
                case = off // HALF_WIN
                q2 = qs[qrows, :] * SCALE
                k2 = ks[krows, :].astype(bf16)
                v2 = vs[krows, :].astype(bf16)
                ms, accs = [], []
                for hh in range(2):
                    mine = first if hh == 0 else ~first
                    qh = jnp.where(mine, q2, 0.0).astype(bf16)
                    s = lax.dot_general(qh, k2, (((1,), (1,)), ((), ())), preferred_element_type=f32)
                    s = s + bias_scr[_bias_index(b, case, hh), 0:nq, 0:nk]
                    m = jnp.max(s, axis=-1, keepdims=True)
                    p = jnp.exp(s - m).astype(bf16)
                    vh = jnp.where(mine, v2, jnp.ones_like(v2))
                    ms.append(m)
                    accs.append(jnp.dot(p, vh, preferred_element_type=f32))
                m_scr[b, qrows, :] = jnp.where(first, ms[0], ms[1])
                a_scr[b, qrows, :] = jnp.where(first, accs[0], accs[1])
                l_scr[b, qrows, :] = jnp.where(first, accs[1], accs[0])
                return carry

            lax.fori_loop(0, S // nq, step, 0, unroll=min(ATTN_UNROLL, S // nq))

        n4 = S // QUAD
        ch = min(256, n4)
        nch = n4 // ch

        def merge(i, carry):
            rho, part = i // nch, i % nch
            sorted_rows = pl.ds(pl.multiple_of(rho * n4 + part * ch, 8), ch)
            token_rows = pl.ds(rho + QUAD * part * ch, ch, stride=QUAD)
            rows = (token_rows,) + (sorted_rows,) * (len(BRANCHES) - 1)
            ms = [m_scr[b, rows[b], :] for b in range(len(BRANCHES))]
            m = functools.reduce(jnp.maximum, ms)
            l = jnp.zeros((ch, PAIR), f32)
            acc = jnp.zeros((ch, PAIR), f32)
            for b in range(len(BRANCHES)):
                w = jnp.exp(ms[b] - m)
                l = l + w * pltpu.roll(l_scr[b, rows[b], :], HEAD_DIM, 1)
                acc = acc + w * a_scr[b, rows[b], :]
            o_ref[token_rows, :] = acc / l
            lse_ref[token_rows, :] = m + jnp.log(l)
            return carry

        lax.fori_loop(0, QUAD * nch, merge, 0)

    blk = lambda part: pl.BlockSpec((S, PAIR), lambda p: (0, part * npair + p))
    out = pl.BlockSpec((S, PAIR), lambda p: (0, p))
    return pl.pallas_call(
        body, name="attn_fwd",
        out_shape=(jax.ShapeDtypeStruct((S, C), f32), jax.ShapeDtypeStruct((S, C), f32)), grid=(npair,),
        in_specs=[blk(4), blk(5), blk(6), pl.BlockSpec((None, 8, PAIR), lambda p: (p, 0, 0))],
        out_specs=(out, out),
        scratch_shapes=[pltpu.VMEM((3, S, PAIR), f32)] * 3 + [pltpu.VMEM(_bias_shape(S), f32)]
        + [pltpu.VMEM((S, PAIR), f32)] * 3,
        compiler_params=_params("parallel"),
    )(proj, proj, proj, slopes)


def _attn_post(ycat, o, proj, g_attn, tr=512):
    S, C = o.shape
    tr = min(tr, S)

    def body(y_ref, o_ref, z_ref, g_ref, out_ref):
        del y_ref
        ov = o_ref[...]
        ra = lax.rsqrt(jnp.mean(ov * ov, axis=-1, keepdims=True) + EPS)
        out_ref[...] = ((ov * ra) * g_ref[...] * _silu(z_ref[...])).astype(bf16)

    return pl.pallas_call(
        body, name="attn_post", out_shape=jax.ShapeDtypeStruct(ycat.shape, ycat.dtype), grid=(S // tr,),
        in_specs=[HBM_SPEC, pl.BlockSpec((tr, C), lambda i: (i, 0)), pl.BlockSpec((tr, C), lambda i: (i, 7)),
                  pl.BlockSpec((1, C), lambda i: (0, 0))],
        out_specs=pl.BlockSpec((tr, C), lambda i: (i, 1)), input_output_aliases={0: 0},
        compiler_params=_params("arbitrary"),
    )(ycat, o, proj, g_attn)


def _sandwich(y, x, target, gate, g_post, tr=256):
    S, D = y.shape
    tr = min(tr, S)

    def body(y_ref, x_ref, t_ref, gate_ref, g_ref, dy_ref, dout_ref, sums_ref):
        i = pl.program_id(0)
        yv = y_ref[...]
        rp = lax.rsqrt(jnp.mean(yv * yv, axis=-1, keepdims=True) + EPS)
        yhat = yv * rp
        yn = yhat * g_ref[...]
        err = (x_ref[...] + gate_ref[...] * yn) - t_ref[...]
        dout = err * (1.0 / D)
        dout_ref[...] = dout
        dyn = dout * gate_ref[...]
        w = dyn * g_ref[...]
        dy_ref[...] = (rp * (w - yhat * jnp.mean(w * yhat, axis=-1, keepdims=True))).astype(bf16)
        loss = 0.5 * jnp.sum(jnp.mean(err * err, axis=-1, keepdims=True), axis=0, keepdims=True)
        row = lax.broadcasted_iota(jnp.int32, (8, D), 0)
        upd = jnp.where(row == 0, jnp.sum(dout * yn, axis=0, keepdims=True),
                        jnp.where(row == 1, jnp.sum(dyn * yhat, axis=0, keepdims=True),
                                  jnp.where(row == 2, loss, 0.0)))

        @pl.when(i == 0)
        def _():
            sums_ref[...] = upd

        @pl.when(i > 0)
        def _():
            sums_ref[...] += upd

    row = pl.BlockSpec((tr, D), lambda i: (i, 0))
    vec = pl.BlockSpec((1, D), lambda i: (0, 0))
    return pl.pallas_call(
        body, name="sandwich",
        out_shape=(jax.ShapeDtypeStruct((S, D), bf16), jax.ShapeDtypeStruct((S, D), f32), jax.ShapeDtypeStruct((8, D), f32)),
        grid=(S // tr,), in_specs=[row, row, row, vec, vec],
        out_specs=(row, row, pl.BlockSpec((8, D), lambda i: (0, 0))), compiler_params=_params("arbitrary"),
    )(y, x, target, gate, g_post)


def _conv_bwd(proj, dycat, conv_w, conv_b, g_conv, dep, tr=256):
    S, C = proj.shape[0], proj.shape[1] // 8
    tr = min(tr, S)
    n = tr + 16

    def body(*refs):
        ins, (w_ref, cb_ref, g_ref, _, dp_ref, sums_ref) = refs[:15], refs[15:]
        i = pl.program_id(0)
        exists = _ext_rows(i, tr, S)
        u, bg, cg, zc, dyn = (jnp.concatenate([ins[3 * t][...], ins[3 * t + 1][...], ins[3 * t + 2][...]], axis=0)
                              for t in range(5))
        w = w_ref[...]
        t = jnp.where(exists, cg * u, 0.0)
        t_before, t_after = pltpu.roll(t, 1, 0), pltpu.roll(t, n - 1, 0)
        cv = w[0:1] * t_before + w[1:2] * t + w[2:3] * t_after + cb_ref[...]
        yc = bg * cv
        rc = lax.rsqrt(jnp.mean(yc * yc, axis=-1, keepdims=True) + EPS)
        yhat = yc * rc
        sz, dsz = _silu_and_slope(zc)
        wgt = dyn * g_ref[...] * sz
        dyc = rc * (wgt - yhat * jnp.mean(wgt * yhat, axis=-1, keepdims=True))
        dcv = jnp.where(exists, dyc * bg, 0.0)
        dt = w[0:1] * pltpu.roll(dcv, n - 1, 0) + w[1:2] * dcv + w[2:3] * pltpu.roll(dcv, 1, 0)
        mid = slice(8, tr + 8)
        dp_ref[:, 0:C] = (dt * cg)[mid].astype(bf16)
        dp_ref[:, C:2 * C] = (dyc * cv)[mid].astype(bf16)
        dp_ref[:, 2 * C:3 * C] = (dt * u)[mid].astype(bf16)
        dp_ref[:, 3 * C:4 * C] = (dyn * yhat * g_ref[...] * dsz)[mid].astype(bf16)
        colsum = lambda v: jnp.sum(v[mid], axis=0, keepdims=True)
        parts = [colsum(dyn * yhat * sz), colsum(dcv), colsum(dcv * t_before), colsum(dcv * t), colsum(dcv * t_after)]
        row = lax.broadcasted_iota(jnp.int32, (8, C), 0)
        upd = jnp.zeros((8, C), f32)
        for j, pj in enumerate(parts):
            upd = jnp.where(row == j, pj, upd)

        @pl.when(i == 0)
        def _():
            sums_ref[...] = upd

        @pl.when(i > 0)
        def _():
            sums_ref[...] += upd

    specs = []
    for col in range(4):
        specs += _halo_specs(tr, S, C, col)
    specs += _halo_specs(tr, S, C, 0)
    vec = pl.BlockSpec((1, C), lambda i: (0, 0))
    return pl.pallas_call(
        body, name="conv_bwd",
        out_shape=(jax.ShapeDtypeStruct((S, 4 * C), bf16), jax.ShapeDtypeStruct((8, C), f32)), grid=(S // tr,),
        in_specs=[*specs, pl.BlockSpec((8, C), lambda i: (0, 0)), vec, vec, ANY_SPEC],
        out_specs=(pl.BlockSpec((tr, 4 * C), lambda i: (i, 0)), pl.BlockSpec((8, C), lambda i: (0, 0))),
        compiler_params=_params("arbitrary"),
    )(*([proj] * 12), dycat, dycat, dycat, conv_w, conv_b, g_conv, dep)


def _attn_post_bwd(o, proj, dycat, g_attn, dep, tr=512):
    S, C = o.shape
    tr = min(tr, S)

    def body(o_ref, z_ref, dy_ref, g_ref, dep_ref, do_ref, dz_ref, sums_ref):
        i = pl.program_id(0)
        ov, zv, dyn = o_ref[...], z_ref[...], dy_ref[...]
        ra = lax.rsqrt(jnp.mean(ov * ov, axis=-1, keepdims=True) + EPS)
        ohat = ov * ra
        sz, dsz = _silu_and_slope(zv)
        wgt = dyn * g_ref[...] * sz
        do_ref[...] = ra * (wgt - ohat * jnp.mean(wgt * ohat, axis=-1, keepdims=True))
        dz_ref[...] = (dyn * ohat * g_ref[...] * dsz).astype(bf16)
        row = lax.broadcasted_iota(jnp.int32, (8, C), 0)
        upd = jnp.where(row == 0, jnp.sum(dyn * ohat * sz, axis=0, keepdims=True), 0.0)

        @pl.when(i == 0)
        def _():
            sums_ref[...] = upd

        @pl.when(i > 0)
        def _():
            sums_ref[...] += upd

    return pl.pallas_call(
        body, name="attn_post_bwd",
        out_shape=(jax.ShapeDtypeStruct((S, C), f32), jax.ShapeDtypeStruct((4, S, C), bf16),
                   jax.ShapeDtypeStruct((8, C), f32)),
        grid=(S // tr,),
        in_specs=[pl.BlockSpec((tr, C), lambda i: (i, 0)), pl.BlockSpec((tr, C), lambda i: (i, 7)),
                  pl.BlockSpec((tr, C), lambda i: (i, 1)), pl.BlockSpec((1, C), lambda i: (0, 0)), ANY_SPEC],
        out_specs=(pl.BlockSpec((tr, C), lambda i: (i, 0)), pl.BlockSpec((None, tr, C), lambda i: (3, i, 0)),
                   pl.BlockSpec((8, C), lambda i: (0, 0))),
        compiler_params=_params("arbitrary"),
    )(o, proj, dycat, g_attn, dep)


def _attn_bwd(proj, o, do, lse, slopes, dqkvz, dep):
    S, C = o.shape
    npair = C // PAIR

    def body(q_ref, k_ref, v_ref, o_ref, do_ref, lse_ref, sl_ref, old_ref, dep_ref, dqkv_ref,
             acc_scr, dl_scr, quad_scr, bias_scr):
        lane = lax.broadcasted_iota(jnp.int32, (1, PAIR), 1)
        first = lane < HEAD_DIM
        _fill_bias(bias_scr, sl_ref, S)
        ch = min(256, S)

        def prep(i, carry):
            rows = pl.ds(pl.multiple_of(i * ch, 8), ch)
            prod = do_ref[rows, :] * o_ref[rows, :]
            d0 = jnp.sum(jnp.where(first, prod, 0.0), axis=-1, keepdims=True)
            d1 = jnp.sum(jnp.where(first, 0.0, prod), axis=-1, keepdims=True)
            dl_scr[rows, :] = jnp.where(first, d0, d1)
            zero = jnp.zeros((ch, PAIR), f32)
            for order in range(2):
                for t in range(3):
                    acc_scr[order, t, rows, :] = zero
            return carry

        lax.fori_loop(0, S // ch, prep, 0)
        token_srcs = (q_ref, k_ref, v_ref, do_ref, lse_ref, dl_scr)
        for j, src in enumerate(token_srcs):
            _to_quad(quad_scr.at[j], src, S)

        for b, (_, r, inter) in enumerate(BRANCHES):
            L, nq, nk, nblk = _branch_geometry(S, r, inter)
            order = 0 if r == 1 else 1
            srcs = token_srcs if r == 1 else tuple(quad_scr.at[j] for j in range(6))

            def step(idx, carry, b=b, r=r, L=L, nq=nq, nk=nk, nblk=nblk, order=order, srcs=srcs):
                qs, ks, vs, dos, lses, dls = srcs
                dq_scr, dk_scr, dv_scr = (acc_scr.at[order, t] for t in range(3))
                qrows, krows, off = _block_rows(idx, r, inter, S, L, nq, nk, nblk)
                case = off // HALF_WIN
                q2 = qs[qrows, :] * SCALE
                k2 = ks[krows, :].astype(bf16)
                v2 = vs[krows, :].astype(bf16)
                do2 = dos[qrows, :]
                lse2 = lses[qrows, :]
                dl2 = dls[qrows, :]
                dq2 = jnp.zeros((nq, PAIR), f32)
                dk2 = jnp.zeros((nk, PAIR), f32)
                dv2 = jnp.zeros((nk, PAIR), f32)
                for hh in range(2):
                    mine = first if hh == 0 else ~first
                    lo = hh * HEAD_DIM
                    qh = jnp.where(mine, q2, 0.0).astype(bf16)
                    doh = jnp.where(mine, do2, 0.0).astype(bf16)
                    s = lax.dot_general(qh, k2, (((1,), (1,)), ((), ())), preferred_element_type=f32)
                    s = s + bias_scr[_bias_index(b, case, hh), 0:nq, 0:nk]
                    p = jnp.exp(s - lse2[:, lo:lo + 1])
                    dv2 = dv2 + lax.dot_general(p.astype(bf16), doh, (((0,), (0,)), ((), ())), preferred_element_type=f32)
                    dp = lax.dot_general(doh, v2, (((1,), (1,)), ((), ())), preferred_element_type=f32)
                    ds = (p * (dp - dl2[:, lo:lo + 1])).astype(bf16)
                    dq2 = dq2 + jnp.where(mine, jnp.dot(ds, k2, preferred_element_type=f32), 0.0)
                    dk2 = dk2 + lax.dot_general(ds, qh, (((0,), (0,)), ((), ())), preferred_element_type=f32)
                dq_scr[qrows, :] = dq_scr[qrows, :] + dq2
                dk_scr[krows, :] = dk_scr[krows, :] + dk2
                dv_scr[krows, :] = dv_scr[krows, :] + dv2
                return carry

            lax.fori_loop(0, S // nq, step, 0, unroll=min(ATTN_UNROLL, S // nq))

        n4 = S // QUAD
        for t in range(3):
            for rho in range(QUAD):
                token_rows = pl.ds(rho, n4, stride=QUAD)
                acc_scr[0, t, token_rows, :] = acc_scr[0, t, token_rows, :] + acc_scr[1, t, pl.ds(rho * n4, n4), :]
        dqkv_ref[0] = (acc_scr[0, 0] * SCALE).astype(bf16)
        dqkv_ref[1] = acc_scr[0, 1].astype(bf16)
        dqkv_ref[2] = acc_scr[0, 2].astype(bf16)

    blk = lambda part: pl.BlockSpec((S, PAIR), lambda p: (0, part * npair + p))
    own = pl.BlockSpec((S, PAIR), lambda p: (0, p))
    return pl.pallas_call(
        body, name="attn_bwd", out_shape=jax.ShapeDtypeStruct(dqkvz.shape, dqkvz.dtype), grid=(npair,),
        in_specs=[blk(4), blk(5), blk(6), own, own, own, pl.BlockSpec((None, 8, PAIR), lambda p: (p, 0, 0)),
                  ANY_SPEC, ANY_SPEC],
        out_specs=pl.BlockSpec((3, S, PAIR), lambda p: (0, 0, p)), input_output_aliases={7: 0},
        scratch_shapes=[pltpu.VMEM((2, 3, S, PAIR), f32), pltpu.VMEM((S, PAIR), f32), pltpu.VMEM((6, S, PAIR), f32),
                        pltpu.VMEM(_bias_shape(S), f32)],
        compiler_params=_params("arbitrary"),
    )(proj, proj, proj, o, do, lse, slopes, dqkvz, dep)


def _prenorm_bwd(dh, x, dout, scale, g_pre, tr=256):
    S, D = x.shape
    tr = min(tr, S)

    def body(dh_ref, x_ref, dout_ref, sc_ref, g_ref, gx_ref, sums_ref):
        i = pl.program_id(0)
        xv, dhv = x_ref[...], dh_ref[...]
        r = lax.rsqrt(jnp.mean(xv * xv, axis=-1, keepdims=True) + EPS)
        xn = xv * r
        dxn = dhv * (g_ref[...] * (1.0 + sc_ref[...]))
        gx_ref[...] = dout_ref[...] + r * (dxn - xn * jnp.mean(dxn * xn, axis=-1, keepdims=True))
        dhx = dhv * xn
        row = lax.broadcasted_iota(jnp.int32, (8, D), 0)
        upd = jnp.where(row == 0, jnp.sum(dhv, axis=0, keepdims=True),
                        jnp.where(row == 1, jnp.sum(dhx, axis=0, keepdims=True) * g_ref[...],
                                  jnp.where(row == 2, jnp.sum(dhx, axis=0, keepdims=True) * (1.0 + sc_ref[...]), 0.0)))

        @pl.when(i == 0)
        def _():
            sums_ref[...] = upd

        @pl.when(i > 0)
        def _():
            sums_ref[...] += upd

    row = pl.BlockSpec((tr, D), lambda i: (i, 0))
    vec = pl.BlockSpec((1, D), lambda i: (0, 0))
    return pl.pallas_call(
        body, name="prenorm_bwd",
        out_shape=(jax.ShapeDtypeStruct((S, D), f32), jax.ShapeDtypeStruct((8, D), f32)), grid=(S // tr,),
        in_specs=[row, row, row, vec, vec], out_specs=(row, pl.BlockSpec((8, D), lambda i: (0, 0))),
        compiler_params=_params("arbitrary"),
    )(dh, x, dout, scale, g_pre)


def _adamw(w, g, m, v):
    m = ADAM_B1 * m + (1.0 - ADAM_B1) * g
    v = ADAM_B2 * v + (1.0 - ADAM_B2) * (g * g)
    m_hat = m / (1.0 - ADAM_B1 ** ADAM_STEP)
    v_hat = v / (1.0 - ADAM_B2 ** ADAM_STEP)
    delta = -ADAM_LR * (m_hat / (jnp.sqrt(v_hat) + ADAM_EPS) + ADAM_WD * w)
    return delta, m, v


def _sum_rows(parts, dep):
    P = parts.shape[1]

    def body(p_ref, dep_ref, o_ref):
        acc = p_ref[0:1, :]
        for j in range(1, NDEV):
            acc = acc + p_ref[j:j + 1, :]
        o_ref[...] = jnp.broadcast_to(acc, (8, P))

    vmem = pl.BlockSpec(memory_space=pltpu.VMEM)
    return pl.pallas_call(body, name="sum_small", out_shape=jax.ShapeDtypeStruct((8, P), f32),
                          in_specs=[vmem, ANY_SPEC], out_specs=vmem, compiler_params=_params())(parts, dep)


def _adamw_small(tot, params):
    given = [p[3] for p in params if not isinstance(p[3], int)]

    def body(tot_ref, *refs):
        given_refs = list(refs[:len(given)])
        ins = refs[len(given):len(given) + 3 * len(params)]
        outs = refs[len(given) + 3 * len(params):]
        for t, (w, _, _, where) in enumerate(params):
            w_ref, m_ref, v_ref = ins[3 * t:3 * t + 3]
            g = tot_ref[0:1, where:where + w.size] if isinstance(where, int) else given_refs.pop(0)[...]
            outs[4 * t][...] = g
            outs[4 * t + 1][...], outs[4 * t + 2][...], outs[4 * t + 3][...] = _adamw(w_ref[...], g, m_ref[...], v_ref[...])

    out_shape = tuple(jax.ShapeDtypeStruct(p[0].shape, f32) for p in params for _ in range(4))
    res = pl.pallas_call(body, name="adamw_small", out_shape=out_shape, compiler_params=_params())(
        tot, *given, *[a for p in params for a in p[:3]])
    return [res[4 * t:4 * t + 4] for t in range(len(params))]


def _adamw_sharded(name, parts, sums_a, sums_b, pick, w, m, v, rows=None, prev=None, tr=128):
    R, Cc = w.shape
    r0, nr = rows or (0, R)
    tr = math.gcd(tr, r0, nr)
    n, b0 = parts.shape[0], r0 // tr

    def body(pick_ref, p_ref, a_ref, b_ref, w_ref, m_ref, v_ref, *rest):
        g_ref, d_ref, nm_ref, nv_ref = rest[-4:]
        g = jnp.where(pick_ref[0] == 1, b_ref[...], a_ref[...]).astype(f32)
        for j in range(n):
            g = g + p_ref[j].astype(f32)
        g_ref[...] = g
        d_ref[...], nm_ref[...], nv_ref[...] = _adamw(w_ref[...], g, m_ref[...], v_ref[...])

    row = pl.BlockSpec((tr, Cc), lambda i, pick: (i + b0, 0))
    mine = pl.BlockSpec((None, tr, Cc), lambda i, pick: (pick[1], i + b0, 0))
    out = jax.ShapeDtypeStruct((R, Cc), f32)
    prev = list(prev or [])
    grid_spec = pltpu.PrefetchScalarGridSpec(
        num_scalar_prefetch=1, grid=(nr // tr,),
        in_specs=[pl.BlockSpec((n, tr, Cc), lambda i, pick: (0, i + b0, 0)), mine, mine, row, row, row]
        + [ANY_SPEC] * len(prev),
        out_specs=(row, row, row, row))
    return pl.pallas_call(
        body, name=name, out_shape=(out, out, out, out), grid_spec=grid_spec,
        input_output_aliases={7 + t: t for t in range(len(prev))}, compiler_params=_params("arbitrary"),
    )(pick, parts, sums_a, sums_b, w, m, v, *prev)


def _adamw_ada(c_t, dmod_cols, w, m, v, dep, tr=512):
    D, W = w.shape
    tr = min(tr, D)

    def body(c_ref, dm_ref, w_ref, m_ref, v_ref, dep_ref, g_ref, d_ref, nm_ref, nv_ref):
        cv, dm = c_ref[...], dm_ref[...]
        g = cv[:, 0:1] * dm[0:1, :]
        for b in range(1, NDEV):
            g = g + cv[:, b:b + 1] * dm[b:b + 1, :]
        g_ref[...] = g
        d_ref[...], nm_ref[...], nv_ref[...] = _adamw(w_ref[...], g, m_ref[...], v_ref[...])

    row = pl.BlockSpec((tr, W), lambda i: (i, 0))
    out = jax.ShapeDtypeStruct((D, W), f32)
    return pl.pallas_call(
        body, name="adamw_ada", out_shape=(out, out, out, out), grid=(D // tr,),
        in_specs=[pl.BlockSpec((tr, NDEV), lambda i: (i, 0)), pl.BlockSpec((NDEV, W), lambda i: (0, 0)), row, row, row,
                  ANY_SPEC],
        out_specs=(row, row, row, row), compiler_params=_params("parallel"),
    )(c_t, dmod_cols, w, m, v, dep)


def kernel(x, c, w_ada, b_ada, g_pre, w_in, conv_w, conv_b, g_conv, g_attn, w_out, g_post, loss_target, m_w_ada, m_b_ada, m_g_pre, m_w_in, m_conv_w, m_conv_b, m_g_conv, m_g_attn, m_w_out, m_g_post, v_w_ada, v_b_ada, v_g_pre, v_w_in, v_conv_w, v_conv_b, v_g_conv, v_g_attn, v_w_out, v_g_post):
    S, D = x.shape[1], x.shape[2]
    C = D // 2
    W = w_ada.shape[2]
    CW = conv_w.shape[2]
    me = 4 * lax.axis_index("x") + 2 * lax.axis_index("y") + lax.axis_index("c")
    x2, tgt = x[0], loss_target[0]
    w_ada2, w_in2, w_out2 = w_ada[0], w_in[0], w_out[0]

    R = D // NDEV
    core = lax.axis_index("c").astype(jnp.int32).reshape(1)

    cw_slab = jnp.zeros((8, CW), f32).at[:3].set(conv_w[0])
    b_cols = lax.dynamic_slice_in_dim(b_ada, me * W, W, axis=1)
    mod_slabs, c_blocks, cw_g = _ada_exchange(c.reshape(D // 128, 128), cw_slab, w_ada2, b_cols)
    c_all = c_blocks.reshape(NDEV, D)
    conv_w_full = jnp.transpose(cw_g, (1, 0, 2)).reshape(8, C)
    mod = mod_slabs[:, 0, :].reshape(1, 3 * D)
    shift, scale, gate = mod[:, :D], mod[:, D:2 * D], mod[:, 2 * D:]

    land_i = lax.dynamic_update_slice(lax.empty((NDEV, D, C), bf16), w_in2.astype(bf16)[None], (me, 0, 0))
    land_o = lax.dynamic_update_slice(lax.empty((NDEV, R, D), bf16), w_out2.astype(bf16)[None], (me, 0, 0))
    wi_send, wi_recv, land_i, w_token = _w_in_start(land_i, [mod_slabs])

    me_arr = me.astype(jnp.int32).reshape(1)
    h = _prenorm(x2, scale, shift, g_pre, w_token)
    land_i = _w_in_sibling(land_i, wi_recv, after=[h])
    proj = _in_proj_part("in_proj_a", h, land_i, None, me_arr, 0, 1, 2)
    fi_send, fi_recv, land_i = _w_in_relay(land_i, wi_recv, after=[proj])
    proj = _in_proj_part("in_proj_b", h, land_i, proj, me_arr, 2, 2, 2)
    land_i = _w_in_forwarded(land_i, fi_recv, after=[proj])
    proj = _in_proj_part("in_proj_c", h, land_i, proj, me_arr, 3, 2, 2)
    (di_send, di_recv, wo_send, wo_recv), land_i, land_o = _w_in_diag(land_i, land_o, fi_recv, after=[proj])
    proj = _in_proj_part("in_proj_d", h, land_i, proj, me_arr, 6, 1, 1)
    win_g = _w_in_finish(land_i, wi_send, fi_send, di_send, di_recv, after=[proj])
    proj = _in_proj_part("in_proj_e", h, win_g, proj, me_arr, 7, 1, 1)
    slopes = _head_slopes(C // HEAD_DIM)
    ycat = _conv_fwd(proj, conv_w_full, conv_b, g_conv)
    o, lse = _attn_fwd(proj, slopes)
    (fo_send, fo_recv), (land_o,), _ = _weights_forward("w_out_forward", land_o, wo_recv, after=[o])
    ycat = _attn_post(ycat, o, proj, g_attn)
    wout_g = _weights_wait("w_out_wait", land_o, wo_send, wo_recv, fo_send, fo_recv, after=[ycat])
    wout_full = wout_g.reshape(D, D)
    y = _matmul(ycat, wout_full, name="out_proj", out_dtype=f32)
    dy, dout, post_sums = _sandwich(y, x2, tgt, gate, g_post)

    chip = me // 2

    def landing(rows, cols):
        return lax.dynamic_update_slice(lax.empty((NCHIP, rows, cols), bf16), jnp.zeros((1, rows, cols), bf16),
                                        (chip, 0, 0))

    gw_out = _matmul(ycat, dy, name="out_proj_dw", out_dtype=bf16, ta=True).reshape(NDEV, R, D)
    po_send, po_recv, gw_out, pair_o, po_token = _pair_start("g_out_pair_start", gw_out)
    dycat = _matmul(dy, wout_full, name="out_proj_dx", out_dtype=f32, tb=True, dep=po_token)
    gw_out, pair_o = _pair_wait("g_out_pair_wait", gw_out, pair_o, po_send, po_recv, after=[dycat])
    sum_o = _pair_sum("g_out_pair_sum", gw_out, pair_o, core)
    co_send, co_recv, sum_o, land_go, co_token = _chip_start(
        "g_out_chip_start", sum_o, landing(R, D), 0)
    dpc, conv_sums = _conv_bwd(proj, dycat, conv_w_full, conv_b, g_conv, co_token)
    gw_c = _matmul(h, dpc, name="in_proj_dw_conv", out_dtype=bf16, ta=True, out_slots=4)
    pc_send, pc_recv, gw_c, pair_c, pc_token = _pair_start("g_conv_pair_start", gw_c)
    do, dpa, attn_sums = _attn_post_bwd(o, proj, dycat, g_attn, pc_token)
    gw_c, pair_c = _pair_wait("g_conv_pair_wait", gw_c, pair_c, pc_send, pc_recv, after=[do])
    sum_c = _pair_sum("g_conv_pair_sum", gw_c, pair_c, core)
    cc_send, cc_recv, sum_c, land_gi, cc_token = _chip_start(
        "g_conv_chip_start", sum_c, landing(D, C), 0)
    dpa = _attn_bwd(proj, o, do, lse, slopes, dpa, cc_token)
    gw_a = _matmul(h, dpa, name="in_proj_dw_attn", out_dtype=bf16, ta=True, b_slots=True, out_slots=4)
    pa_send, pa_recv, gw_a, pair_a, pa_token = _pair_start("g_attn_pair_start", gw_a)
    sum_o, land_go = _chip_wait("g_out_chip_wait", sum_o, land_go, co_send, co_recv, 0, after=[pa_token])
    pick_out = jnp.stack([jnp.int32(0), me // 2]).astype(jnp.int32)
    g_w_out, d_w_out, nm_w_out, nv_w_out = _adamw_sharded(
        "adamw_w_out", land_go, sum_o, sum_o, pick_out, w_out2, m_w_out[0], v_w_out[0])
    gw_a, pair_a = _pair_wait("g_attn_pair_wait", gw_a, pair_a, pa_send, pa_recv, after=[g_w_out])
    sum_a = _pair_sum("g_attn_pair_sum", gw_a, pair_a, core)
    part_a, part_b = (0, 3 * D // 4), (3 * D // 4, D // 4)
    ca_send, ca_recv, sum_a, land_gi, ca_token = _chip_start("g_attn_chip_start_a", sum_a, land_gi, 4, part_a)
    dh = _matmul_slabs_t(dpc, dpa, win_g, name="in_proj_dx", dep=ca_token)
    grad_x, pre_sums = _prenorm_bwd(dh, x2, dout, scale, g_pre)

    small = jnp.concatenate([pre_sums[0:1], pre_sums[1:2], post_sums[0:1],
                             pre_sums[2:3], post_sums[1:2],
                             conv_sums[2:3], conv_sums[3:4], conv_sums[4:5],
                             conv_sums[1:2], conv_sums[0:1], attn_sums[0:1]], axis=1)
    small = jnp.concatenate([small.reshape(8 * D // 128, 128), jnp.broadcast_to(post_sums[2:3, :128], (8, 128))])
    (small_all,) = _all_gather([small], "gather_small")
    cb_send, cb_recv, sum_a, land_gi, cb_token = _chip_start("g_attn_chip_start_b", sum_a, land_gi, 4, part_b,
                                                             after=[small_all])
    small_all = small_all.reshape(NDEV, small.size)
    tot = _sum_rows(small_all, cb_token)
    loss = tot[0, 8 * D]
    g_conv_w = lax.dynamic_slice_in_dim(tot[0:1, 5 * D:5 * D + 3 * C].reshape(1, 3, C), me * CW, CW, axis=2)
    ((g_b_ada, d_b_ada, nm_b_ada, nv_b_ada), (g_g_pre, d_g_pre, nm_g_pre, nv_g_pre),
     (g_g_post, d_g_post, nm_g_post, nv_g_post), (g_conv_w, d_conv_w, nm_conv_w, nv_conv_w),
     (g_conv_b, d_conv_b, nm_conv_b, nv_conv_b), (g_g_conv, d_g_conv, nm_g_conv, nv_g_conv),
     (g_g_attn, d_g_attn, nm_g_attn, nv_g_attn)) = _adamw_small(tot, [
         (b_ada, m_b_ada, v_b_ada, 0), (g_pre, m_g_pre, v_g_pre, 3 * D), (g_post, m_g_post, v_g_post, 4 * D),
         (conv_w, m_conv_w, v_conv_w, g_conv_w), (conv_b, m_conv_b, v_conv_b, 5 * D + 3 * C),
         (g_conv, m_g_conv, v_g_conv, 5 * D + 4 * C), (g_attn, m_g_attn, v_g_attn, 5 * D + 5 * C)])

    dmod_cols = lax.dynamic_slice_in_dim(small_all[:, :3 * D], me * W, W, axis=1)
    g_w_ada, d_w_ada, nm_w_ada, nv_w_ada = _adamw_ada(c_all.T, dmod_cols, w_ada2, m_w_ada[0], v_w_ada[0], cb_token)

    pick_in = jnp.stack([me // 4, (me % 4) // 2]).astype(jnp.int32)
    sum_c, land_gi = _chip_wait("g_conv_chip_wait", sum_c, land_gi, cc_send, cc_recv, 0, after=[g_w_ada])
    sum_a, land_gi = _chip_wait("g_attn_chip_wait_a", sum_a, land_gi, ca_send, ca_recv, 4, [g_w_ada], part_a)
    first = _adamw_sharded("adamw_w_in_a", land_gi, sum_c, sum_a, pick_in, w_in2, m_w_in[0], v_w_in[0], rows=part_a,
                           tr=256)
    sum_a, land_gi = _chip_wait("g_attn_chip_wait_b", sum_a, land_gi, cb_send, cb_recv, 4, [first[0]], part_b)
    g_w_in, d_w_in, nm_w_in, nv_w_in = _adamw_sharded(
        "adamw_w_in_b", land_gi, sum_c, sum_a, pick_in, w_in2, m_w_in[0], v_w_in[0], rows=part_b, prev=first, tr=256)

    return (loss, grad_x[None],
            g_w_ada[None], g_b_ada, g_g_pre, g_w_in[None], g_conv_w, g_conv_b, g_g_conv, g_g_attn, g_w_out[None], g_g_post,
            d_w_ada[None], d_b_ada, d_g_pre, d_w_in[None], d_conv_w, d_conv_b, d_g_conv, d_g_attn, d_w_out[None], d_g_post,
            nm_w_ada[None], nm_b_ada, nm_g_pre, nm_w_in[None], nm_conv_w, nm_conv_b, nm_g_conv, nm_g_attn, nm_w_out[None], nm_g_post,
            nv_w_ada[None], nv_b_ada, nv_g_pre, nv_w_in[None], nv_conv_w, nv_conv_b, nv_g_conv, nv_g_attn, nv_w_out[None], nv_g_post)
```

```python
import functools
import math

import jax
import jax.numpy as jnp
from jax import lax
from jax.experimental import pallas as pl
from jax.experimental.pallas import tpu as pltpu

f32 = jnp.float32
bf16 = jnp.bfloat16

NDEV = 8
HEAD_DIM = 64
PAIR = 2 * HEAD_DIM
BRANCHES = ((128, 1, 1), (512, 4, 1), (2048, 16, 2))
HALF_WIN = 64
EPS = 1e-6
NEG_INF = -1e30
ADAM_LR, ADAM_B1, ADAM_B2, ADAM_EPS, ADAM_WD, ADAM_STEP = 0.001, 0.9, 0.999, 1e-08, 0.01, 10
MESH = pl.DeviceIdType.MESH
VMEM_LIMIT = 56 * 1024 * 1024
HBM_SPEC = pl.BlockSpec(memory_space=pltpu.HBM)
ANY_SPEC = pl.BlockSpec(memory_space=pl.ANY)
SEM_SPEC = pl.BlockSpec(memory_space=pltpu.SEMAPHORE)


def _params(*sem):
    return pltpu.CompilerParams(dimension_semantics=sem or None, vmem_limit_bytes=VMEM_LIMIT)


def _silu(z):
    return z * jax.nn.sigmoid(z)


def _silu_and_slope(z):
    s = jax.nn.sigmoid(z)
    return z * s, s * (1.0 + z * (1.0 - s))


def _my_place():
    x, y, c = lax.axis_index("x"), lax.axis_index("y"), lax.axis_index("c")
    return x, y, c, 4 * x + 2 * y + c


def _peer(x, y, c, k):
    px, py, pc = x ^ (k >> 2 & 1), y ^ (k >> 1 & 1), c ^ (k & 1)
    return (px, py, pc), 4 * px + 2 * py + pc


def _all_gather(arrays, name):
    n = len(arrays)

    def body(*refs):
        srcs, dsts = refs[:n], refs[n:2 * n]
        send_sems, recv_sems, local_sems = refs[2 * n:]
        x, y, c, me = _my_place()
        locals_, sends = [], []
        for t in range(n):
            own = pltpu.make_async_copy(srcs[t], dsts[t].at[me], local_sems.at[t])
            own.start()
            locals_.append(own)
            for k in range(1, NDEV):
                peer, pidx = _peer(x, y, c, k)
                cp = pltpu.make_async_remote_copy(
                    src_ref=srcs[t], dst_ref=dsts[t].at[me], send_sem=send_sems.at[t, k],
                    recv_sem=recv_sems.at[t, k], device_id=peer, device_id_type=MESH)
                cp.start()
                sends.append(cp)
        for t in range(n):
            for k in range(1, NDEV):
                peer, pidx = _peer(x, y, c, k)
                pltpu.make_async_remote_copy(
                    src_ref=srcs[t], dst_ref=dsts[t].at[pidx], send_sem=send_sems.at[t, k],
                    recv_sem=recv_sems.at[t, k], device_id=peer, device_id_type=MESH).wait_recv()
        for cp in sends:
            cp.wait_send()
        for cp in locals_:
            cp.wait()

    return pl.pallas_call(
        body, name=name,
        out_shape=tuple(jax.ShapeDtypeStruct((NDEV,) + a.shape, a.dtype) for a in arrays),
        in_specs=[HBM_SPEC] * n, out_specs=tuple([HBM_SPEC] * n),
        scratch_shapes=[pltpu.SemaphoreType.DMA((n, NDEV)), pltpu.SemaphoreType.DMA((n, NDEV)),
                        pltpu.SemaphoreType.DMA((n,))],
    )(*arrays)


def _comm_call(name, arrays, sems, new_sems, body, after=(), token=False):
    na, ns, nn, nf = len(arrays), len(sems), len(new_sems), len(after)

    def kern(*refs):
        ins, outs = refs[:na + ns + nf], refs[na + ns + nf:]
        body(ins[:na], ins[na:na + ns], outs[:nn])
        if token:
            outs[nn + na][...] = jnp.zeros((8, 128), f32)

    out_shape = ([pltpu.SemaphoreType.DMA(s) for s in new_sems] + [pltpu.HBM(a.shape, a.dtype) for a in arrays]
                 + ([jax.ShapeDtypeStruct((8, 128), f32)] if token else []))
    out_specs = [SEM_SPEC] * nn + [HBM_SPEC] * na + ([pl.BlockSpec(memory_space=pltpu.VMEM)] if token else [])
    res = pl.pallas_call(
        kern, name=name, out_shape=tuple(out_shape),
        in_specs=[HBM_SPEC] * na + [SEM_SPEC] * ns + [ANY_SPEC] * nf, out_specs=tuple(out_specs),
        input_output_aliases={t: nn + t for t in range(na)},
        compiler_params=pltpu.CompilerParams(has_side_effects=pltpu.SideEffectType.DATAFLOW_SIDE_EFFECTING),
    )(*[pltpu.with_memory_space_constraint(a, pltpu.HBM) for a in arrays], *sems, *after)
    return list(res[:nn]), list(res[nn:nn + na]), (res[nn + na] if token else None)


def _remote(src, dst, send_sem, recv_sem, device):
    return pltpu.make_async_remote_copy(src_ref=src, dst_ref=dst, send_sem=send_sem, recv_sem=recv_sem,
                                        device_id=device, device_id_type=MESH)


SAME_CORE = (2, 4, 6)
VIA_SIBLING = (3, 5, 7)


def _weights_forward(name, land, recv, after):
    def body(a, s, new):
        (land,), (recv,), (fsend, frecv) = a, s, new
        x, y, c, me = _my_place()
        sibling, _ = _peer(x, y, c, 1)
        for k in SAME_CORE:
            peer, slot = _peer(x, y, c, k)
            _remote(land.at[slot], land.at[slot], fsend.at[k], recv.at[k], peer).wait_recv()
            _remote(land.at[slot], land.at[slot], fsend.at[k], frecv.at[k ^ 1], sibling).start()

    return _comm_call(name, [land], [recv], [(NDEV,), (NDEV,)], body, after=after)


def _weights_wait(name, land, send, recv, fsend, frecv, after):
    def body(a, s, new):
        (land,), (send, recv, fsend, frecv) = a, s
        x, y, c, me = _my_place()
        sibling, sib_slot = _peer(x, y, c, 1)
        _remote(land.at[sib_slot], land.at[sib_slot], send.at[1], recv.at[1], sibling).wait_recv()
        for k in VIA_SIBLING:
            _, slot = _peer(x, y, c, k)
            _remote(land.at[slot], land.at[slot], fsend.at[k ^ 1], frecv.at[k], sibling).wait_recv()
        for k in (1,) + SAME_CORE:
            peer, _ = _peer(x, y, c, k)
            _remote(land.at[me], land.at[me], send.at[k], recv.at[k], peer).wait_send()
        for k in SAME_CORE:
            _, slot = _peer(x, y, c, k)
            _remote(land.at[slot], land.at[slot], fsend.at[k], frecv.at[k ^ 1], sibling).wait_send()

    return _comm_call(name, [land], [send, recv, fsend, frecv], [], body, after=after)[1][0]


def _diag_relay(x, y, c):
    slot = 4 * (x ^ (1 - c)) + 2 * (y ^ c) + c
    return slot, (x ^ c, y ^ (1 - c), c)


def _w_in_start(land, after):
    def body(a, s, new):
        (land,), (send, recv) = a, new
        x, y, c, me = _my_place()
        for k in (1, 2, 4):
            peer, _ = _peer(x, y, c, k)
            _remote(land.at[me], land.at[me], send.at[k], recv.at[k], peer).start()

    (send, recv), (land,), token = _comm_call("w_in_start", [land], [], [(NDEV,), (NDEV,)], body, after=after, token=True)
    return send, recv, land, token


def _w_in_sibling(land, recv, after):
    def body(a, s, new):
        (land,), (recv,) = a, s
        x, y, c, me = _my_place()
        sibling, slot = _peer(x, y, c, 1)
        _remote(land.at[slot], land.at[slot], recv.at[1], recv.at[1], sibling).wait_recv()

    return _comm_call("w_in_sibling", [land], [recv], [], body, after=after)[1][0]


def _w_in_relay(land, recv, after):
    def body(a, s, new):
        (land,), (recv,), (fsend, frecv) = a, s, new
        x, y, c, me = _my_place()
        sibling, _ = _peer(x, y, c, 1)
        for k in (2, 4):
            peer, slot = _peer(x, y, c, k)
            _remote(land.at[slot], land.at[slot], fsend.at[k], recv.at[k], peer).wait_recv()
        slot, target = _diag_relay(x, y, c)
        _remote(land.at[slot], land.at[slot], fsend.at[6], frecv.at[6], target).start()
        for k in (2, 4):
            _, slot = _peer(x, y, c, k)
            _remote(land.at[slot], land.at[slot], fsend.at[k], frecv.at[k ^ 1], sibling).start()

    (fsend, frecv), (land,), _ = _comm_call("w_in_relay", [land], [recv], [(NDEV,), (NDEV,)], body, after=after)
    return fsend, frecv, land


def _w_in_forwarded(land, frecv, after):
    def body(a, s, new):
        (land,), (frecv,) = a, s
        x, y, c, me = _my_place()
        sibling, _ = _peer(x, y, c, 1)
        for k in (3, 5):
            _, slot = _peer(x, y, c, k)
            _remote(land.at[slot], land.at[slot], frecv.at[k], frecv.at[k], sibling).wait_recv()

    return _comm_call("w_in_forwarded", [land], [frecv], [], body, after=after)[1][0]


def _w_in_diag(land, land_o, frecv, after):
    def body(a, s, new):
        (land, land_o), (frecv,), (dsend, drecv, osend, orecv) = a, s, new
        x, y, c, me = _my_place()
        sibling, _ = _peer(x, y, c, 1)
        peer, slot = _peer(x, y, c, 6)
        _remote(land.at[slot], land.at[slot], dsend.at[6], frecv.at[6], peer).wait_recv()
        _remote(land.at[slot], land.at[slot], dsend.at[6], drecv.at[7], sibling).start()
        for k in (1,) + SAME_CORE:
            peer, _ = _peer(x, y, c, k)
            _remote(land_o.at[me], land_o.at[me], osend.at[k], orecv.at[k], peer).start()

    sems, (land, land_o), _ = _comm_call("w_in_diag", [land, land_o], [frecv], [(NDEV,)] * 4, body, after=after)
    return sems, land, land_o


def _w_in_finish(land, send, fsend, dsend, drecv, after):
    def body(a, s, new):
        (land,), (send, fsend, dsend, drecv) = a, s
        x, y, c, me = _my_place()
        sibling, _ = _peer(x, y, c, 1)
        _, slot = _peer(x, y, c, 7)
        _remote(land.at[slot], land.at[slot], dsend.at[6], drecv.at[7], sibling).wait_recv()
        for k in (1, 2, 4):
            peer, _ = _peer(x, y, c, k)
            _remote(land.at[me], land.at[me], send.at[k], send.at[k], peer).wait_send()
        for k in (2, 4, 6):
            _, slot = _peer(x, y, c, k)
            _remote(land.at[slot], land.at[slot], fsend.at[k], fsend.at[k], sibling).wait_send()
        _, slot = _peer(x, y, c, 6)
        _remote(land.at[slot], land.at[slot], dsend.at[6], dsend.at[6], sibling).wait_send()

    return _comm_call("w_in_finish", [land], [send, fsend, dsend, drecv], [], body, after=after)[1][0]


def _in_proj_part(name, h, land, proj, me_arr, k0, kstep, nk, tm=512):
    S, D = h.shape
    C = land.shape[2]
    tm = min(tm, S)

    def body(me_ref, a_ref, b_ref, *rest):
        rest[-1][...] = jnp.dot(a_ref[...], b_ref[...], preferred_element_type=f32)

    slot = lambda j, me: me[0] ^ (k0 + kstep * j)
    args = [h, land] + ([] if proj is None else [proj])
    grid_spec = pltpu.PrefetchScalarGridSpec(
        num_scalar_prefetch=1, grid=(nk, S // tm),
        in_specs=[pl.BlockSpec((tm, D), lambda j, i, me: (i, 0)),
                  pl.BlockSpec((None, D, C), lambda j, i, me: (slot(j, me), 0, 0))] + [ANY_SPEC] * (len(args) - 2),
        out_specs=pl.BlockSpec((tm, C), lambda j, i, me: (i, slot(j, me))))
    return pl.pallas_call(
        body, name=name, out_shape=jax.ShapeDtypeStruct((S, NDEV * C), f32), grid_spec=grid_spec,
        input_output_aliases={} if proj is None else {3: 0}, compiler_params=_params("arbitrary", "arbitrary"),
    )(me_arr, *args)


NCHIP = NDEV // 2


def _pair_start(name, src):
    npair = src.shape[0] // 2

    def body(a, s, new):
        (src, pair), (send, recv) = a, new
        x, y, c, me = _my_place()
        sibling, _ = _peer(x, y, c, 1)
        for i in range(npair):
            _remote(src.at[2 * i + 1 - c], pair.at[i], send.at[i], recv.at[i], sibling).start()

    pair = lax.empty((npair,) + src.shape[1:], src.dtype)
    (send, recv), (src, pair), token = _comm_call(name, [src, pair], [], [(npair,), (npair,)], body, token=True)
    return send, recv, src, pair, token


def _pair_wait(name, src, pair, send, recv, after):
    npair = pair.shape[0]

    def body(a, s, new):
        (src, pair), (send, recv) = a, s
        x, y, c, me = _my_place()
        sibling, _ = _peer(x, y, c, 1)
        for i in range(npair):
            cp = _remote(src.at[2 * i + 1 - c], pair.at[i], send.at[i], recv.at[i], sibling)
            cp.wait_recv()
            cp.wait_send()

    return _comm_call(name, [src, pair], [send, recv], [], body, after=after)[1]


def _pair_sum(name, src, pair, core, tr=1024):
    npair, R, Cc = pair.shape
    tr = min(tr, R)

    def body(core_ref, a_ref, b_ref, o_ref):
        o_ref[...] = (a_ref[...].astype(f32) + b_ref[...].astype(f32)).astype(o_ref.dtype)

    grid_spec = pltpu.PrefetchScalarGridSpec(
        num_scalar_prefetch=1, grid=(npair, R // tr),
        in_specs=[pl.BlockSpec((None, tr, Cc), lambda i, r, core: (2 * i + core[0], r, 0)),
                  pl.BlockSpec((None, tr, Cc), lambda i, r, core: (i, r, 0))],
        out_specs=pl.BlockSpec((None, tr, Cc), lambda i, r, core: (i, r, 0)))
    return pl.pallas_call(body, name=name, out_shape=jax.ShapeDtypeStruct(pair.shape, pair.dtype),
                          grid_spec=grid_spec, compiler_params=_params("parallel", "parallel"))(core, src, pair)


def _owner_chip(first, i):
    q = first // 2 + i
    return q >> 1 & 1, q & 1


def _chip_start(name, sums, land, first, rows=None, after=()):
    npair = sums.shape[0]
    rows = pl.ds(*(rows or (0, sums.shape[1])))

    def body(a, s, new):
        (sums, land), (send, recv) = a, new
        x, y, c, me = _my_place()
        for i in range(npair):
            ox, oy = _owner_chip(first, i)

            @pl.when((x != ox) | (y != oy))
            def _():
                _remote(sums.at[i, rows], land.at[2 * x + y, rows], send.at[i], recv.at[2 * x + y], (ox, oy, c)).start()

    (send, recv), (sums, land), token = _comm_call(name, [sums, land], [], [(npair,), (NCHIP,)], body, after=after,
                                                   token=True)
    return send, recv, sums, land, token


def _chip_wait(name, sums, land, send, recv, first, after, rows=None):
    npair = sums.shape[0]
    rows = pl.ds(*(rows or (0, sums.shape[1])))

    def body(a, s, new):
        (sums, land), (send, recv) = a, s
        x, y, c, me = _my_place()
        mine = (me >= first) & (me < first + 2 * npair)
        for i in range(npair):
            ox, oy = _owner_chip(first, i)

            @pl.when((x != ox) | (y != oy))
            def _():
                _remote(sums.at[i, rows], land.at[2 * x + y, rows], send.at[i], recv.at[2 * x + y], (ox, oy, c)).wait_send()
        for q in range(NCHIP):
            @pl.when(mine & (2 * x + y != q))
            def _():
                _remote(sums.at[0, rows], land.at[q, rows], send.at[0], recv.at[q], (q >> 1, q & 1, c)).wait_recv()

    return _comm_call(name, [sums, land], [send, recv], [], body, after=after)[1]


def _matmul(a, b, *, name, out_dtype, ta=False, tb=False, b_slots=False, out_slots=0, b_cols=None,
            tm=1024, tn=1024, tk=2048, dep=None):
    M, K = (a.shape[1], a.shape[0]) if ta else a.shape
    col0 = 0
    if b_slots:
        slab = b.shape[2]
        N = b.shape[1] if tb else b.shape[0] * slab
        assert (K if tb else N) == b.shape[0] * slab
    elif b_cols is not None:
        assert not tb
        col0, N = b_cols
    else:
        N = b.shape[0] if tb else b.shape[1]
    tm, tn, tk = min(tm, M), min(tn, N), min(tk, K)
    if b_slots:
        if tb:
            tk = min(tk, slab)
        else:
            tn = min(tn, slab)
    if out_slots:
        tn = min(tn, N // out_slots)
    nm, nn, nk = M // tm, N // tn, K // tk
    assert (nm * tm, nn * tn, nk * tk) == (M, N, K) and col0 % tn == 0, (name, M, N, K, tm, tn, tk)
    j0 = col0 // tn

    a_spec = pl.BlockSpec((tk, tm), lambda i, j, k: (k, i)) if ta else pl.BlockSpec((tm, tk), lambda i, j, k: (i, k))
    if b_slots and tb:
        per = slab // tk
        b_spec = pl.BlockSpec((None, tn, tk), lambda i, j, k: (k // per, j, k % per))
    elif b_slots:
        per = slab // tn
        b_spec = pl.BlockSpec((None, tk, tn), lambda i, j, k: (j // per, k, j % per))
    elif tb:
        b_spec = pl.BlockSpec((tn, tk), lambda i, j, k: (j, k))
    else:
        b_spec = pl.BlockSpec((tk, tn), lambda i, j, k: (k, j + j0))
    if out_slots:
        per_o = (N // out_slots) // tn
        o_spec = pl.BlockSpec((None, tm, tn), lambda i, j, k: (j // per_o, i, j % per_o))
        out_shape = jax.ShapeDtypeStruct((out_slots, M, N // out_slots), out_dtype)
    else:
        o_spec = pl.BlockSpec((tm, tn), lambda i, j, k: (i, j))
        out_shape = jax.ShapeDtypeStruct((M, N), out_dtype)
    dims = (((0 if ta else 1,), (1 if tb else 0,)), ((), ()))
    deps = [] if dep is None else [dep]

    def body(a_ref, b_ref, *rest):
        o_ref = rest[len(deps)]
        prod = lax.dot_general(a_ref[...], b_ref[...], dims, preferred_element_type=f32)
        if nk == 1:
            o_ref[...] = prod.astype(out_dtype)
            return
        acc_ref = rest[len(deps) + 1]
        k = pl.program_id(2)

        @pl.when(k == 0)
        def _():
            acc_ref[...] = prod

        @pl.when((k > 0) & (k < nk - 1))
        def _():
            acc_ref[...] += prod

        @pl.when(k == nk - 1)
        def _():
            o_ref[...] = (acc_ref[...] + prod).astype(out_dtype)

    return pl.pallas_call(
        body, name=name, out_shape=out_shape, grid=(nm, nn, nk),
        in_specs=[a_spec, b_spec] + [ANY_SPEC] * len(deps), out_specs=o_spec,
        scratch_shapes=[pltpu.VMEM((tm, tn), f32)] if nk > 1 else [],
        compiler_params=_params("parallel", "parallel", "arbitrary"),
    )(a, b, *deps)


def _matmul_slabs_t(a_cols, a_slots, b, *, name, tm=1024, tn=512, dep=None):
    M = a_cols.shape[0]
    n_slab, N, slab = b.shape
    n1, n2 = a_cols.shape[1] // slab, a_slots.shape[0]
    assert n1 + n2 == n_slab and a_slots.shape[1:] == (M, slab)
    tm, tn = min(tm, M), min(tn, N)
    deps = [] if dep is None else [dep]

    def body(a1_ref, a2_ref, b_ref, *rest):
        o_ref = rest[len(deps)]
        acc = None
        for s in range(n_slab):
            lhs = a1_ref[:, s * slab:(s + 1) * slab] if s < n1 else a2_ref[s - n1]
            prod = lax.dot_general(lhs, b_ref[s], (((1,), (1,)), ((), ())), preferred_element_type=f32)
            acc = prod if acc is None else acc + prod
        o_ref[...] = acc

    return pl.pallas_call(
        body, name=name, out_shape=jax.ShapeDtypeStruct((M, N), f32), grid=(M // tm, N // tn),
        in_specs=[pl.BlockSpec((tm, n1 * slab), lambda i, j: (i, 0), pipeline_mode=pl.Buffered(1)),
                  pl.BlockSpec((n2, tm, slab), lambda i, j: (0, i, 0), pipeline_mode=pl.Buffered(1)),
                  pl.BlockSpec((n_slab, tn, slab), lambda i, j: (0, j, 0))] + [ANY_SPEC] * len(deps),
        out_specs=pl.BlockSpec((tm, tn), lambda i, j: (i, j)), compiler_params=_params("parallel", "parallel"),
    )(a_cols, a_slots, b, *deps)


def _ada_exchange(c_blk, cw_slab, w_ada, b_cols):
    nblk = c_blk.shape[0]
    D, W = w_ada.shape
    CW = cw_slab.shape[1]

    def body(c_ref, cw_ref, w_ref, b_ref, mod_ref, call_ref, cwg_ref, msend, send_sems, recv_sems):
        x, y, c, me = _my_place()
        call_ref[me] = _silu(c_ref[...])
        cwg_ref[me] = cw_ref[...]
        first = []
        for k in range(1, NDEV):
            peer, _ = _peer(x, y, c, k)
            first.append(_remote(call_ref.at[me], call_ref.at[me], send_sems.at[0, k], recv_sems.at[0, k], peer))
            first.append(_remote(cwg_ref.at[me], cwg_ref.at[me], send_sems.at[1, k], recv_sems.at[1, k], peer))
        for cp in first:
            cp.start()
        for k in range(1, NDEV):
            peer, slot = _peer(x, y, c, k)
            _remote(call_ref.at[slot], call_ref.at[slot], send_sems.at[0, k], recv_sems.at[0, k], peer).wait_recv()
            _remote(cwg_ref.at[slot], cwg_ref.at[slot], send_sems.at[1, k], recv_sems.at[1, k], peer).wait_recv()
        mod = jnp.broadcast_to(b_ref[...], (NDEV, W))
        for r in range(nblk):
            mod = mod + lax.dot_general(call_ref[:, r, :], w_ref[r * 128:(r + 1) * 128, :], (((1,), (0,)), ((), ())),
                                        preferred_element_type=f32, precision=lax.Precision.HIGHEST)
        row = lax.broadcasted_iota(jnp.int32, (NDEV, 1), 0)
        pick = lambda j: jnp.broadcast_to(jnp.sum(jnp.where(row == j, mod, 0.0), axis=0, keepdims=True), (8, W))
        mod_ref[me] = pick(me)
        second = []
        for k in range(1, NDEV):
            peer, slot = _peer(x, y, c, k)
            msend[k] = pick(slot)
            second.append(_remote(msend.at[k], mod_ref.at[me], send_sems.at[2, k], recv_sems.at[2, k], peer))
        for cp in second:
            cp.start()
        for k in range(1, NDEV):
            peer, slot = _peer(x, y, c, k)
            _remote(msend.at[k], mod_ref.at[slot], send_sems.at[2, k], recv_sems.at[2, k], peer).wait_recv()
        for cp in first + second:
            cp.wait_send()

    vmem = pl.BlockSpec(memory_space=pltpu.VMEM)
    return pl.pallas_call(
        body, name="ada_exchange",
        out_shape=(jax.ShapeDtypeStruct((NDEV, 8, W), f32), jax.ShapeDtypeStruct((NDEV, nblk, 128), f32),
                   jax.ShapeDtypeStruct((NDEV, 8, CW), f32)),
        in_specs=[vmem] * 4, out_specs=(vmem, vmem, vmem),
        scratch_shapes=[pltpu.VMEM((NDEV, 8, W), f32), pltpu.SemaphoreType.DMA((3, NDEV)),
                        pltpu.SemaphoreType.DMA((3, NDEV))],
        compiler_params=_params(),
    )(c_blk, cw_slab, w_ada, b_cols)


def _prenorm(x, scale, shift, g_pre, dep, tr=512):
    S, D = x.shape
    tr = min(tr, S)

    def body(x_ref, sc_ref, sh_ref, g_ref, dep_ref, h_ref):
        xv = x_ref[...]
        r = lax.rsqrt(jnp.mean(xv * xv, axis=-1, keepdims=True) + EPS)
        h_ref[...] = ((xv * r) * g_ref[...] * (1.0 + sc_ref[...]) + sh_ref[...]).astype(bf16)

    row = pl.BlockSpec((tr, D), lambda i: (i, 0))
    vec = pl.BlockSpec((1, D), lambda i: (0, 0))
    return pl.pallas_call(body, name="prenorm", out_shape=jax.ShapeDtypeStruct((S, D), bf16), grid=(S // tr,),
                          in_specs=[row, vec, vec, vec, ANY_SPEC], out_specs=row, compiler_params=_params("parallel"))(
                              x, scale, shift, g_pre, dep)


def _ext_rows(i, tr, S):
    g = lax.broadcasted_iota(jnp.int32, (tr + 16, 1), 0) + (i * tr - 8)
    return (g >= 0) & (g < S)


def _halo_specs(tr, S, C, col):
    nb8 = S // 8
    main = pl.BlockSpec((tr, C), lambda i: (i, col))
    prev = pl.BlockSpec((8, C), lambda i: (jnp.maximum(i * (tr // 8) - 1, 0), col))
    nxt = pl.BlockSpec((8, C), lambda i: (jnp.minimum((i + 1) * (tr // 8), nb8 - 1), col))
    return prev, main, nxt


def _conv_fwd(proj, conv_w, conv_b, g_conv, tr=512):
    S, C = proj.shape[0], proj.shape[1] // 8
    tr = min(tr, S)

    def body(up, um, un, cp, cm, cn, bg_ref, zc_ref, w_ref, cb_ref, g_ref, o_ref):
        i = pl.program_id(0)
        exists = _ext_rows(i, tr, S)
        u = jnp.concatenate([up[...], um[...], un[...]], axis=0)
        cg = jnp.concatenate([cp[...], cm[...], cn[...]], axis=0)
        t = jnp.where(exists, cg * u, 0.0)
        t_before = pltpu.roll(t, 1, 0)[8:tr + 8]
        t_after = pltpu.roll(t, tr + 15, 0)[8:tr + 8]
        w = w_ref[...]
        cv = w[0:1] * t_before + w[1:2] * t[8:tr + 8] + w[2:3] * t_after + cb_ref[...]
        yc = bg_ref[...] * cv
        rc = lax.rsqrt(jnp.mean(yc * yc, axis=-1, keepdims=True) + EPS)
        o_ref[...] = ((yc * rc) * g_ref[...] * _silu(zc_ref[...])).astype(bf16)

    u_specs = _halo_specs(tr, S, C, 0)
    c_specs = _halo_specs(tr, S, C, 2)
    vec = pl.BlockSpec((1, C), lambda i: (0, 0))
    return pl.pallas_call(
        body, name="conv_fwd", out_shape=jax.ShapeDtypeStruct((S, 2 * C), bf16), grid=(S // tr,),
        in_specs=[*u_specs, *c_specs, pl.BlockSpec((tr, C), lambda i: (i, 1)), pl.BlockSpec((tr, C), lambda i: (i, 3)),
                  pl.BlockSpec((8, C), lambda i: (0, 0)), vec, vec],
        out_specs=pl.BlockSpec((tr, C), lambda i: (i, 0)), compiler_params=_params("parallel"),
    )(proj, proj, proj, proj, proj, proj, proj, proj, conv_w, conv_b, g_conv)


def _branch_geometry(S, r, inter):
    L = S // r * inter
    nq = min(128, L)
    nk = min(nq + 2 * HALF_WIN * inter, L)
    assert L % nq == 0 and (L == nk or L >= nq + 2 * HALF_WIN * inter)
    return L, nq, nk, L // nq


QUAD = 4


def _to_quad(dst, src, S):
    n = S // QUAD
    for rho in range(QUAD):
        dst[pl.ds(rho * n, n), :] = src[pl.ds(rho, n, stride=QUAD), :]


def _block_rows(idx, r, inter, S, L, nq, nk, nblk):
    rho, qb = (0, idx) if r == 1 else (idx // nblk, idx % nblk)
    i0 = qb * nq
    ws = jnp.clip(i0 - HALF_WIN * inter, 0, L - nk)
    if r == 1:
        return pl.ds(pl.multiple_of(i0, 8), nq), pl.ds(pl.multiple_of(ws, 8), nk), i0 - ws
    assert r % (QUAD * inter) == 0
    step = r // QUAD // inter
    base = (rho % QUAD) * (S // QUAD) + rho // QUAD
    if step == 1:
        return pl.ds(pl.multiple_of(base + i0, 8), nq), pl.ds(pl.multiple_of(base + ws, 8), nk), i0 - ws
    return pl.ds(base + step * i0, nq, stride=step), pl.ds(base + step * ws, nk, stride=step), i0 - ws


N_CASES = 3
SCALE = HEAD_DIM ** -0.5
ATTN_UNROLL = 16


def _bias_shape(S):
    shapes = [_branch_geometry(S, r, inter)[1:3] for _, r, inter in BRANCHES]
    return (len(BRANCHES) * N_CASES * 2, max(nq for nq, _ in shapes), max(nk for _, nk in shapes))


def _bias_index(b, case, head):
    return (b * N_CASES + case) * 2 + head


def _fill_bias(bias_scr, sl_ref, S):
    sl = sl_ref[...]
    slope = (sl[0:1, 0:1], sl[0:1, HEAD_DIM:HEAD_DIM + 1])
    for b, (_, r, inter) in enumerate(BRANCHES):
        L, nq, nk, nblk = _branch_geometry(S, r, inter)
        rel = lax.broadcasted_iota(jnp.int32, (nq, nk), 0) - lax.broadcasted_iota(jnp.int32, (nq, nk), 1)
        for case in range(N_CASES):
            d = jnp.abs(rel + case * HALF_WIN)
            valid = d <= HALF_WIN * inter
            if inter > 1:
                valid = valid & (jnp.bitwise_and(d, inter - 1) == 0)
            dist = d.astype(f32) * float(r // inter)
            for head in range(2):
                bias_scr[_bias_index(b, case, head), 0:nq, 0:nk] = jnp.where(valid, -slope[head] * dist, NEG_INF)


def _head_slopes(n_heads):
    slopes = 2.0 ** (-8.0 * jnp.arange(1, n_heads + 1, dtype=f32) / n_heads)
    return jnp.broadcast_to(jnp.repeat(slopes.reshape(n_heads // 2, 2), HEAD_DIM, axis=1)[:, None, :],
                            (n_heads // 2, 8, PAIR))


def _attn_fwd(proj, slopes):
    S, C = proj.shape[0], proj.shape[1] // 8
    npair = C // PAIR

    def body(q_ref, k_ref, v_ref, sl_ref, o_ref, lse_ref, m_scr, l_scr, a_scr, bias_scr, q4_scr, k4_scr, v4_scr):
        lane = lax.broadcasted_iota(jnp.int32, (1, PAIR), 1)
        first = lane < HEAD_DIM
        _fill_bias(bias_scr, sl_ref, S)
        for dst, src in ((q4_scr, q_ref), (k4_scr, k_ref), (v4_scr, v_ref)):
            _to_quad(dst, src, S)

        for b, (_, r, inter) in enumerate(BRANCHES):
            L, nq, nk, nblk = _branch_geometry(S, r, inter)
            qs, ks, vs = (q_ref, k_ref, v_ref) if r == 1 else (q4_scr, k4_scr, v4_scr)

            def step(idx, carry, b=b, r=r, L=L, nq=nq, nk=nk, nblk=nblk, qs=qs, ks=ks, vs=vs):
                qrows, krows, off = _block_rows(idx, r, inter, S, L, nq, nk, nblk)
                case = off // HALF_WIN
                q2 = qs[qrows, :] * SCALE
                k2 = ks[krows, :].astype(bf16)
                v2 = vs[krows, :].astype(bf16)
                ms, accs = [], []
                for hh in range(2):
                    mine = first if hh == 0 else ~first
                    qh = jnp.where(mine, q2, 0.0).astype(bf16)
                    s = lax.dot_general(qh, k2, (((1,), (1,)), ((), ())), preferred_element_type=f32)
                    s = s + bias_scr[_bias_index(b, case, hh), 0:nq, 0:nk]
                    m = jnp.max(s, axis=-1, keepdims=True)
                    p = jnp.exp(s - m).astype(bf16)
                    vh = jnp.where(mine, v2, jnp.ones_like(v2))
                    ms.append(m)
                    accs.append(jnp.dot(p, vh, preferred_element_type=f32))
                m_scr[b, qrows, :] = jnp.where(first, ms[0], ms[1])
                a_scr[b, qrows, :] = jnp.where(first, accs[0], accs[1])
                l_scr[b, qrows, :] = jnp.where(first, accs[1], accs[0])
                return carry

            lax.fori_loop(0, S // nq, step, 0, unroll=min(ATTN_UNROLL, S // nq))

        n4 = S // QUAD
        ch = min(256, n4)
        nch = n4 // ch

        def merge(i, carry):
            rho, part = i // nch, i % nch
            sorted_rows = pl.ds(pl.multiple_of(rho * n4 + part * ch, 8), ch)
            token_rows = pl.ds(rho + QUAD * part * ch, ch, stride=QUAD)
            rows = (token_rows,) + (sorted_rows,) * (len(BRANCHES) - 1)
            ms = [m_scr[b, rows[b], :] for b in range(len(BRANCHES))]
            m = functools.reduce(jnp.maximum, ms)
            l = jnp.zeros((ch, PAIR), f32)
            acc = jnp.zeros((ch, PAIR), f32)
            for b in range(len(BRANCHES)):
                w = jnp.exp(ms[b] - m)
                l = l + w * pltpu.roll(l_scr[b, rows[b], :], HEAD_DIM, 1)
                acc = acc + w * a_scr[b, rows[b], :]
            o_ref[token_rows, :] = acc / l
            lse_ref[token_rows, :] = m + jnp.log(l)
            return carry

        lax.fori_loop(0, QUAD * nch, merge, 0)

    blk = lambda part: pl.BlockSpec((S, PAIR), lambda p: (0, part * npair + p))
    out = pl.BlockSpec((S, PAIR), lambda p: (0, p))
    return pl.pallas_call(
        body, name="attn_fwd",
        out_shape=(jax.ShapeDtypeStruct((S, C), f32), jax.ShapeDtypeStruct((S, C), f32)), grid=(npair,),
        in_specs=[blk(4), blk(5), blk(6), pl.BlockSpec((None, 8, PAIR), lambda p: (p, 0, 0))],
        out_specs=(out, out),
        scratch_shapes=[pltpu.VMEM((3, S, PAIR), f32)] * 3 + [pltpu.VMEM(_bias_shape(S), f32)]
        + [pltpu.VMEM((S, PAIR), f32)] * 3,
        compiler_params=_params("parallel"),
    )(proj, proj, proj, slopes)


def _attn_post(ycat, o, proj, g_attn, tr=512):
    S, C = o.shape
    tr = min(tr, S)

    def body(y_ref, o_ref, z_ref, g_ref, out_ref):
        del y_ref
        ov = o_ref[...]
        ra = lax.rsqrt(jnp.mean(ov * ov, axis=-1, keepdims=True) + EPS)
        out_ref[...] = ((ov * ra) * g_ref[...] * _silu(z_ref[...])).astype(bf16)

    return pl.pallas_call(
        body, name="attn_post", out_shape=jax.ShapeDtypeStruct(ycat.shape, ycat.dtype), grid=(S // tr,),
        in_specs=[HBM_SPEC, pl.BlockSpec((tr, C), lambda i: (i, 0)), pl.BlockSpec((tr, C), lambda i: (i, 7)),
                  pl.BlockSpec((1, C), lambda i: (0, 0))],
        out_specs=pl.BlockSpec((tr, C), lambda i: (i, 1)), input_output_aliases={0: 0},
        compiler_params=_params("arbitrary"),
    )(ycat, o, proj, g_attn)


def _sandwich(y, x, target, gate, g_post, tr=256):
    S, D = y.shape
    tr = min(tr, S)

    def body(y_ref, x_ref, t_ref, gate_ref, g_ref, dy_ref, dout_ref, sums_ref):
        i = pl.program_id(0)
        yv = y_ref[...]
        rp = lax.rsqrt(jnp.mean(yv * yv, axis=-1, keepdims=True) + EPS)
        yhat = yv * rp
        yn = yhat * g_ref[...]
        err = (x_ref[...] + gate_ref[...] * yn) - t_ref[...]
        dout = err * (1.0 / D)
        dout_ref[...] = dout
        dyn = dout * gate_ref[...]
        w = dyn * g_ref[...]
        dy_ref[...] = (rp * (w - yhat * jnp.mean(w * yhat, axis=-1, keepdims=True))).astype(bf16)
        loss = 0.5 * jnp.sum(jnp.mean(err * err, axis=-1, keepdims=True), axis=0, keepdims=True)
        row = lax.broadcasted_iota(jnp.int32, (8, D), 0)
        upd = jnp.where(row == 0, jnp.sum(dout * yn, axis=0, keepdims=True),
                        jnp.where(row == 1, jnp.sum(dyn * yhat, axis=0, keepdims=True),
                                  jnp.where(row == 2, loss, 0.0)))

        @pl.when(i == 0)
        def _():
            sums_ref[...] = upd

        @pl.when(i > 0)
        def _():
            sums_ref[...] += upd

    row = pl.BlockSpec((tr, D), lambda i: (i, 0))
    vec = pl.BlockSpec((1, D), lambda i: (0, 0))
    return pl.pallas_call(
        body, name="sandwich",
        out_shape=(jax.ShapeDtypeStruct((S, D), bf16), jax.ShapeDtypeStruct((S, D), f32), jax.ShapeDtypeStruct((8, D), f32)),
        grid=(S // tr,), in_specs=[row, row, row, vec, vec],
        out_specs=(row, row, pl.BlockSpec((8, D), lambda i: (0, 0))), compiler_params=_params("arbitrary"),
    )(y, x, target, gate, g_post)


def _conv_bwd(proj, dycat, conv_w, conv_b, g_conv, dep, tr=256):
    S, C = proj.shape[0], proj.shape[1] // 8
    tr = min(tr, S)
    n = tr + 16

    def body(*refs):
        ins, (w_ref, cb_ref, g_ref, _, dp_ref, sums_ref) = refs[:15], refs[15:]
        i = pl.program_id(0)
        exists = _ext_rows(i, tr, S)
        u, bg, cg, zc, dyn = (jnp.concatenate([ins[3 * t][...], ins[3 * t + 1][...], ins[3 * t + 2][...]], axis=0)
                              for t in range(5))
        w = w_ref[...]
        t = jnp.where(exists, cg * u, 0.0)
        t_before, t_after = pltpu.roll(t, 1, 0), pltpu.roll(t, n - 1, 0)
        cv = w[0:1] * t_before + w[1:2] * t + w[2:3] * t_after + cb_ref[...]
        yc = bg * cv
        rc = lax.rsqrt(jnp.mean(yc * yc, axis=-1, keepdims=True) + EPS)
        yhat = yc * rc
        sz, dsz = _silu_and_slope(zc)
        wgt = dyn * g_ref[...] * sz
        dyc = rc * (wgt - yhat * jnp.mean(wgt * yhat, axis=-1, keepdims=True))
        dcv = jnp.where(exists, dyc * bg, 0.0)
        dt = w[0:1] * pltpu.roll(dcv, n - 1, 0) + w[1:2] * dcv + w[2:3] * pltpu.roll(dcv, 1, 0)
        mid = slice(8, tr + 8)
        dp_ref[:, 0:C] = (dt * cg)[mid].astype(bf16)
        dp_ref[:, C:2 * C] = (dyc * cv)[mid].astype(bf16)
        dp_ref[:, 2 * C:3 * C] = (dt * u)[mid].astype(bf16)
        dp_ref[:, 3 * C:4 * C] = (dyn * yhat * g_ref[...] * dsz)[mid].astype(bf16)
        colsum = lambda v: jnp.sum(v[mid], axis=0, keepdims=True)
        parts = [colsum(dyn * yhat * sz), colsum(dcv), colsum(dcv * t_before), colsum(dcv * t), colsum(dcv * t_after)]
        row = lax.broadcasted_iota(jnp.int32, (8, C), 0)
        upd = jnp.zeros((8, C), f32)
        for j, pj in enumerate(parts):
            upd = jnp.where(row == j, pj, upd)

        @pl.when(i == 0)
        def _():
            sums_ref[...] = upd

        @pl.when(i > 0)
        def _():
            sums_ref[...] += upd

    specs = []
    for col in range(4):
        specs += _halo_specs(tr, S, C, col)
    specs += _halo_specs(tr, S, C, 0)
    vec = pl.BlockSpec((1, C), lambda i: (0, 0))
    return pl.pallas_call(
        body, name="conv_bwd",
        out_shape=(jax.ShapeDtypeStruct((S, 4 * C), bf16), jax.ShapeDtypeStruct((8, C), f32)), grid=(S // tr,),
        in_specs=[*specs, pl.BlockSpec((8, C), lambda i: (0, 0)), vec, vec, ANY_SPEC],
        out_specs=(pl.BlockSpec((tr, 4 * C), lambda i: (i, 0)), pl.BlockSpec((8, C), lambda i: (0, 0))),
        compiler_params=_params("arbitrary"),
    )(*([proj] * 12), dycat, dycat, dycat, conv_w, conv_b, g_conv, dep)


def _attn_post_bwd(o, proj, dycat, g_attn, dep, tr=512):
    S, C = o.shape
    tr = min(tr, S)

    def body(o_ref, z_ref, dy_ref, g_ref, dep_ref, do_ref, dz_ref, sums_ref):
        i = pl.program_id(0)
        ov, zv, dyn = o_ref[...], z_ref[...], dy_ref[...]
        ra = lax.rsqrt(jnp.mean(ov * ov, axis=-1, keepdims=True) + EPS)
        ohat = ov * ra
        sz, dsz = _silu_and_slope(zv)
        wgt = dyn * g_ref[...] * sz
        do_ref[...] = ra * (wgt - ohat * jnp.mean(wgt * ohat, axis=-1, keepdims=True))
        dz_ref[...] = (dyn * ohat * g_ref[...] * dsz).astype(bf16)
        row = lax.broadcasted_iota(jnp.int32, (8, C), 0)
        upd = jnp.where(row == 0, jnp.sum(dyn * ohat * sz, axis=0, keepdims=True), 0.0)

        @pl.when(i == 0)
        def _():
            sums_ref[...] = upd

        @pl.when(i > 0)
        def _():
            sums_ref[...] += upd

    return pl.pallas_call(
        body, name="attn_post_bwd",
        out_shape=(jax.ShapeDtypeStruct((S, C), f32), jax.ShapeDtypeStruct((4, S, C), bf16),
                   jax.ShapeDtypeStruct((8, C), f32)),
        grid=(S // tr,),
        in_specs=[pl.BlockSpec((tr, C), lambda i: (i, 0)), pl.BlockSpec((tr, C), lambda i: (i, 7)),
                  pl.BlockSpec((tr, C), lambda i: (i, 1)), pl.BlockSpec((1, C), lambda i: (0, 0)), ANY_SPEC],
        out_specs=(pl.BlockSpec((tr, C), lambda i: (i, 0)), pl.BlockSpec((None, tr, C), lambda i: (3, i, 0)),
                   pl.BlockSpec((8, C), lambda i: (0, 0))),
        compiler_params=_params("arbitrary"),
    )(o, proj, dycat, g_attn, dep)


def _attn_bwd(proj, o, do, lse, slopes, dqkvz, dep):
    S, C = o.shape
    npair = C // PAIR

    def body(q_ref, k_ref, v_ref, o_ref, do_ref, lse_ref, sl_ref, old_ref, dep_ref, dqkv_ref,
             acc_scr, dl_scr, quad_scr, bias_scr):
        lane = lax.broadcasted_iota(jnp.int32, (1, PAIR), 1)
        first = lane < HEAD_DIM
        _fill_bias(bias_scr, sl_ref, S)
        ch = min(256, S)

        def prep(i, carry):
            rows = pl.ds(pl.multiple_of(i * ch, 8), ch)
            prod = do_ref[rows, :] * o_ref[rows, :]
            d0 = jnp.sum(jnp.where(first, prod, 0.0), axis=-1, keepdims=True)
            d1 = jnp.sum(jnp.where(first, 0.0, prod), axis=-1, keepdims=True)
            dl_scr[rows, :] = jnp.where(first, d0, d1)
            zero = jnp.zeros((ch, PAIR), f32)
            for order in range(2):
                for t in range(3):
                    acc_scr[order, t, rows, :] = zero
            return carry

        lax.fori_loop(0, S // ch, prep, 0)
        token_srcs = (q_ref, k_ref, v_ref, do_ref, lse_ref, dl_scr)
        for j, src in enumerate(token_srcs):
            _to_quad(quad_scr.at[j], src, S)

        for b, (_, r, inter) in enumerate(BRANCHES):
            L, nq, nk, nblk = _branch_geometry(S, r, inter)
            order = 0 if r == 1 else 1
            srcs = token_srcs if r == 1 else tuple(quad_scr.at[j] for j in range(6))

            def step(idx, carry, b=b, r=r, L=L, nq=nq, nk=nk, nblk=nblk, order=order, srcs=srcs):
                qs, ks, vs, dos, lses, dls = srcs
                dq_scr, dk_scr, dv_scr = (acc_scr.at[order, t] for t in range(3))
                qrows, krows, off = _block_rows(idx, r, inter, S, L, nq, nk, nblk)
                case = off // HALF_WIN
                q2 = qs[qrows, :] * SCALE
                k2 = ks[krows, :].astype(bf16)
                v2 = vs[krows, :].astype(bf16)
                do2 = dos[qrows, :]
                lse2 = lses[qrows, :]
                dl2 = dls[qrows, :]
                dq2 = jnp.zeros((nq, PAIR), f32)
                dk2 = jnp.zeros((nk, PAIR), f32)
                dv2 = jnp.zeros((nk, PAIR), f32)
                for hh in range(2):
                    mine = first if hh == 0 else ~first
                    lo = hh * HEAD_DIM
                    qh = jnp.where(mine, q2, 0.0).astype(bf16)
                    doh = jnp.where(mine, do2, 0.0).astype(bf16)
                    s = lax.dot_general(qh, k2, (((1,), (1,)), ((), ())), preferred_element_type=f32)
                    s = s + bias_scr[_bias_index(b, case, hh), 0:nq, 0:nk]
                    p = jnp.exp(s - lse2[:, lo:lo + 1])
                    dv2 = dv2 + lax.dot_general(p.astype(bf16), doh, (((0,), (0,)), ((), ())), preferred_element_type=f32)
                    dp = lax.dot_general(doh, v2, (((1,), (1,)), ((), ())), preferred_element_type=f32)
                    ds = (p * (dp - dl2[:, lo:lo + 1])).astype(bf16)
                    dq2 = dq2 + jnp.where(mine, jnp.dot(ds, k2, preferred_element_type=f32), 0.0)
                    dk2 = dk2 + lax.dot_general(ds, qh, (((0,), (0,)), ((), ())), preferred_element_type=f32)
                dq_scr[qrows, :] = dq_scr[qrows, :] + dq2
                dk_scr[krows, :] = dk_scr[krows, :] + dk2
                dv_scr[krows, :] = dv_scr[krows, :] + dv2
                return carry

            lax.fori_loop(0, S // nq, step, 0, unroll=min(ATTN_UNROLL, S // nq))

        n4 = S // QUAD
        for t in range(3):
            for rho in range(QUAD):
                token_rows = pl.ds(rho, n4, stride=QUAD)
                acc_scr[0, t, token_rows, :] = acc_scr[0, t, token_rows, :] + acc_scr[1, t, pl.ds(rho * n4, n4), :]
        dqkv_ref[0] = (acc_scr[0, 0] * SCALE).astype(bf16)
        dqkv_ref[1] = acc_scr[0, 1].astype(bf16)
        dqkv_ref[2] = acc_scr[0, 2].astype(bf16)

    blk = lambda part: pl.BlockSpec((S, PAIR), lambda p: (0, part * npair + p))
    own = pl.BlockSpec((S, PAIR), lambda p: (0, p))
    return pl.pallas_call(
        body, name="attn_bwd", out_shape=jax.ShapeDtypeStruct(dqkvz.shape, dqkvz.dtype), grid=(npair,),
        in_specs=[blk(4), blk(5), blk(6), own, own, own, pl.BlockSpec((None, 8, PAIR), lambda p: (p, 0, 0)),
                  ANY_SPEC, ANY_SPEC],
        out_specs=pl.BlockSpec((3, S, PAIR), lambda p: (0, 0, p)), input_output_aliases={7: 0},
        scratch_shapes=[pltpu.VMEM((2, 3, S, PAIR), f32), pltpu.VMEM((S, PAIR), f32), pltpu.VMEM((6, S, PAIR), f32),
                        pltpu.VMEM(_bias_shape(S), f32)],
        compiler_params=_params("arbitrary"),
    )(proj, proj, proj, o, do, lse, slopes, dqkvz, dep)


def _prenorm_bwd(dh, x, dout, scale, g_pre, tr=256):
    S, D = x.shape
    tr = min(tr, S)

    def body(dh_ref, x_ref, dout_ref, sc_ref, g_ref, gx_ref, sums_ref):
        i = pl.program_id(0)
        xv, dhv = x_ref[...], dh_ref[...]
        r = lax.rsqrt(jnp.mean(xv * xv, axis=-1, keepdims=True) + EPS)
        xn = xv * r
        dxn = dhv * (g_ref[...] * (1.0 + sc_ref[...]))
        gx_ref[...] = dout_ref[...] + r * (dxn - xn * jnp.mean(dxn * xn, axis=-1, keepdims=True))
        dhx = dhv * xn
        row = lax.broadcasted_iota(jnp.int32, (8, D), 0)
        upd = jnp.where(row == 0, jnp.sum(dhv, axis=0, keepdims=True),
                        jnp.where(row == 1, jnp.sum(dhx, axis=0, keepdims=True) * g_ref[...],
                                  jnp.where(row == 2, jnp.sum(dhx, axis=0, keepdims=True) * (1.0 + sc_ref[...]), 0.0)))

        @pl.when(i == 0)
        def _():
            sums_ref[...] = upd

        @pl.when(i > 0)
        def _():
            sums_ref[...] += upd

    row = pl.BlockSpec((tr, D), lambda i: (i, 0))
    vec = pl.BlockSpec((1, D), lambda i: (0, 0))
    return pl.pallas_call(
        body, name="prenorm_bwd",
        out_shape=(jax.ShapeDtypeStruct((S, D), f32), jax.ShapeDtypeStruct((8, D), f32)), grid=(S // tr,),
        in_specs=[row, row, row, vec, vec], out_specs=(row, pl.BlockSpec((8, D), lambda i: (0, 0))),
        compiler_params=_params("arbitrary"),
    )(dh, x, dout, scale, g_pre)


def _adamw(w, g, m, v):
    m = ADAM_B1 * m + (1.0 - ADAM_B1) * g
    v = ADAM_B2 * v + (1.0 - ADAM_B2) * (g * g)
    m_hat = m / (1.0 - ADAM_B1 ** ADAM_STEP)
    v_hat = v / (1.0 - ADAM_B2 ** ADAM_STEP)
    delta = -ADAM_LR * (m_hat / (jnp.sqrt(v_hat) + ADAM_EPS) + ADAM_WD * w)
    return delta, m, v


def _sum_rows(parts, dep):
    P = parts.shape[1]

    def body(p_ref, dep_ref, o_ref):
        acc = p_ref[0:1, :]
        for j in range(1, NDEV):
            acc = acc + p_ref[j:j + 1, :]
        o_ref[...] = jnp.broadcast_to(acc, (8, P))

    vmem = pl.BlockSpec(memory_space=pltpu.VMEM)
    return pl.pallas_call(body, name="sum_small", out_shape=jax.ShapeDtypeStruct((8, P), f32),
                          in_specs=[vmem, ANY_SPEC], out_specs=vmem, compiler_params=_params())(parts, dep)


def _adamw_small(tot, params):
    given = [p[3] for p in params if not isinstance(p[3], int)]

    def body(tot_ref, *refs):
        given_refs = list(refs[:len(given)])
        ins = refs[len(given):len(given) + 3 * len(params)]
        outs = refs[len(given) + 3 * len(params):]
        for t, (w, _, _, where) in enumerate(params):
            w_ref, m_ref, v_ref = ins[3 * t:3 * t + 3]
            g = tot_ref[0:1, where:where + w.size] if isinstance(where, int) else given_refs.pop(0)[...]
            outs[4 * t][...] = g
            outs[4 * t + 1][...], outs[4 * t + 2][...], outs[4 * t + 3][...] = _adamw(w_ref[...], g, m_ref[...], v_ref[...])

    out_shape = tuple(jax.ShapeDtypeStruct(p[0].shape, f32) for p in params for _ in range(4))
    res = pl.pallas_call(body, name="adamw_small", out_shape=out_shape, compiler_params=_params())(
        tot, *given, *[a for p in params for a in p[:3]])
    return [res[4 * t:4 * t + 4] for t in range(len(params))]


def _adamw_sharded(name, parts, sums_a, sums_b, pick, w, m, v, rows=None, prev=None, tr=128):
    R, Cc = w.shape
    r0, nr = rows or (0, R)
    tr = math.gcd(tr, r0, nr)
    n, b0 = parts.shape[0], r0 // tr

    def body(pick_ref, p_ref, a_ref, b_ref, w_ref, m_ref, v_ref, *rest):
        g_ref, d_ref, nm_ref, nv_ref = rest[-4:]
        g = jnp.where(pick_ref[0] == 1, b_ref[...], a_ref[...]).astype(f32)
        for j in range(n):
            g = g + p_ref[j].astype(f32)
        g_ref[...] = g
        d_ref[...], nm_ref[...], nv_ref[...] = _adamw(w_ref[...], g, m_ref[...], v_ref[...])

    row = pl.BlockSpec((tr, Cc), lambda i, pick: (i + b0, 0))
    mine = pl.BlockSpec((None, tr, Cc), lambda i, pick: (pick[1], i + b0, 0))
    out = jax.ShapeDtypeStruct((R, Cc), f32)
    prev = list(prev or [])
    grid_spec = pltpu.PrefetchScalarGridSpec(
        num_scalar_prefetch=1, grid=(nr // tr,),
        in_specs=[pl.BlockSpec((n, tr, Cc), lambda i, pick: (0, i + b0, 0)), mine, mine, row, row, row]
        + [ANY_SPEC] * len(prev),
        out_specs=(row, row, row, row))
    return pl.pallas_call(
        body, name=name, out_shape=(out, out, out, out), grid_spec=grid_spec,
        input_output_aliases={7 + t: t for t in range(len(prev))}, compiler_params=_params("arbitrary"),
    )(pick, parts, sums_a, sums_b, w, m, v, *prev)


def _adamw_ada(c_t, dmod_cols, w, m, v, dep, tr=512):
    D, W = w.shape
    tr = min(tr, D)

    def body(c_ref, dm_ref, w_ref, m_ref, v_ref, dep_ref, g_ref, d_ref, nm_ref, nv_ref):
        cv, dm = c_ref[...], dm_ref[...]
        g = cv[:, 0:1] * dm[0:1, :]
        for b in range(1, NDEV):
            g = g + cv[:, b:b + 1] * dm[b:b + 1, :]
        g_ref[...] = g
        d_ref[...], nm_ref[...], nv_ref[...] = _adamw(w_ref[...], g, m_ref[...], v_ref[...])

    row = pl.BlockSpec((tr, W), lambda i: (i, 0))
    out = jax.ShapeDtypeStruct((D, W), f32)
    return pl.pallas_call(
        body, name="adamw_ada", out_shape=(out, out, out, out), grid=(D // tr,),
        in_specs=[pl.BlockSpec((tr, NDEV), lambda i: (i, 0)), pl.BlockSpec((NDEV, W), lambda i: (0, 0)), row, row, row,
                  ANY_SPEC],
        out_specs=(row, row, row, row), compiler_params=_params("parallel"),
    )(c_t, dmod_cols, w, m, v, dep)


def kernel(x, c, w_ada, b_ada, g_pre, w_in, conv_w, conv_b, g_conv, g_attn, w_out, g_post, loss_target, m_w_ada, m_b_ada, m_g_pre, m_w_in, m_conv_w, m_conv_b, m_g_conv, m_g_attn, m_w_out, m_g_post, v_w_ada, v_b_ada, v_g_pre, v_w_in, v_conv_w, v_conv_b, v_g_conv, v_g_attn, v_w_out, v_g_post):
    S, D = x.shape[1], x.shape[2]
    C = D // 2
    W = w_ada.shape[2]
    CW = conv_w.shape[2]
    me = 4 * lax.axis_index("x") + 2 * lax.axis_index("y") + lax.axis_index("c")
    x2, tgt = x[0], loss_target[0]
    w_ada2, w_in2, w_out2 = w_ada[0], w_in[0], w_out[0]

    R = D // NDEV
    core = lax.axis_index("c").astype(jnp.int32).reshape(1)

    cw_slab = jnp.zeros((8, CW), f32).at[:3].set(conv_w[0])
    b_cols = lax.dynamic_slice_in_dim(b_ada, me * W, W, axis=1)
    mod_slabs, c_blocks, cw_g = _ada_exchange(c.reshape(D // 128, 128), cw_slab, w_ada2, b_cols)
    c_all = c_blocks.reshape(NDEV, D)
    conv_w_full = jnp.transpose(cw_g, (1, 0, 2)).reshape(8, C)
    mod = mod_slabs[:, 0, :].reshape(1, 3 * D)
    shift, scale, gate = mod[:, :D], mod[:, D:2 * D], mod[:, 2 * D:]

    land_i = lax.dynamic_update_slice(lax.empty((NDEV, D, C), bf16), w_in2.astype(bf16)[None], (me, 0, 0))
    land_o = lax.dynamic_update_slice(lax.empty((NDEV, R, D), bf16), w_out2.astype(bf16)[None], (me, 0, 0))
    wi_send, wi_recv, land_i, w_token = _w_in_start(land_i, [mod_slabs])

    me_arr = me.astype(jnp.int32).reshape(1)
    h = _prenorm(x2, scale, shift, g_pre, w_token)
    land_i = _w_in_sibling(land_i, wi_recv, after=[h])
    proj = _in_proj_part("in_proj_a", h, land_i, None, me_arr, 0, 1, 2)
    fi_send, fi_recv, land_i = _w_in_relay(land_i, wi_recv, after=[proj])
    proj = _in_proj_part("in_proj_b", h, land_i, proj, me_arr, 2, 2, 2)
    land_i = _w_in_forwarded(land_i, fi_recv, after=[proj])
    proj = _in_proj_part("in_proj_c", h, land_i, proj, me_arr, 3, 2, 2)
    (di_send, di_recv, wo_send, wo_recv), land_i, land_o = _w_in_diag(land_i, land_o, fi_recv, after=[proj])
    proj = _in_proj_part("in_proj_d", h, land_i, proj, me_arr, 6, 1, 1)
    win_g = _w_in_finish(land_i, wi_send, fi_send, di_send, di_recv, after=[proj])
    proj = _in_proj_part("in_proj_e", h, win_g, proj, me_arr, 7, 1, 1)
    slopes = _head_slopes(C // HEAD_DIM)
    ycat = _conv_fwd(proj, conv_w_full, conv_b, g_conv)
    o, lse = _attn_fwd(proj, slopes)
    (fo_send, fo_recv), (land_o,), _ = _weights_forward("w_out_forward", land_o, wo_recv, after=[o])
    ycat = _attn_post(ycat, o, proj, g_attn)
    wout_g = _weights_wait("w_out_wait", land_o, wo_send, wo_recv, fo_send, fo_recv, after=[ycat])
    wout_full = wout_g.reshape(D, D)
    y = _matmul(ycat, wout_full, name="out_proj", out_dtype=f32)
    dy, dout, post_sums = _sandwich(y, x2, tgt, gate, g_post)

    chip = me // 2

    def landing(rows, cols):
        return lax.dynamic_update_slice(lax.empty((NCHIP, rows, cols), bf16), jnp.zeros((1, rows, cols), bf16),
                                        (chip, 0, 0))

    gw_out = _matmul(ycat, dy, name="out_proj_dw", out_dtype=bf16, ta=True).reshape(NDEV, R, D)
    po_send, po_recv, gw_out, pair_o, po_token = _pair_start("g_out_pair_start", gw_out)
    dycat = _matmul(dy, wout_full, name="out_proj_dx", out_dtype=f32, tb=True, dep=po_token)
    gw_out, pair_o = _pair_wait("g_out_pair_wait", gw_out, pair_o, po_send, po_recv, after=[dycat])
    sum_o = _pair_sum("g_out_pair_sum", gw_out, pair_o, core)
    co_send, co_recv, sum_o, land_go, co_token = _chip_start(
        "g_out_chip_start", sum_o, landing(R, D), 0)
    dpc, conv_sums = _conv_bwd(proj, dycat, conv_w_full, conv_b, g_conv, co_token)
    gw_c = _matmul(h, dpc, name="in_proj_dw_conv", out_dtype=bf16, ta=True, out_slots=4)
    pc_send, pc_recv, gw_c, pair_c, pc_token = _pair_start("g_conv_pair_start", gw_c)
    do, dpa, attn_sums = _attn_post_bwd(o, proj, dycat, g_attn, pc_token)
    gw_c, pair_c = _pair_wait("g_conv_pair_wait", gw_c, pair_c, pc_send, pc_recv, after=[do])
    sum_c = _pair_sum("g_conv_pair_sum", gw_c, pair_c, core)
    cc_send, cc_recv, sum_c, land_gi, cc_token = _chip_start(
        "g_conv_chip_start", sum_c, landing(D, C), 0)
    dpa = _attn_bwd(proj, o, do, lse, slopes, dpa, cc_token)
    gw_a = _matmul(h, dpa, name="in_proj_dw_attn", out_dtype=bf16, ta=True, b_slots=True, out_slots=4)
    pa_send, pa_recv, gw_a, pair_a, pa_token = _pair_start("g_attn_pair_start", gw_a)
    sum_o, land_go = _chip_wait("g_out_chip_wait", sum_o, land_go, co_send, co_recv, 0, after=[pa_token])
    pick_out = jnp.stack([jnp.int32(0), me // 2]).astype(jnp.int32)
    g_w_out, d_w_out, nm_w_out, nv_w_out = _adamw_sharded(
        "adamw_w_out", land_go, sum_o, sum_o, pick_out, w_out2, m_w_out[0], v_w_out[0])
    gw_a, pair_a = _pair_wait("g_attn_pair_wait", gw_a, pair_a, pa_send, pa_recv, after=[g_w_out])
    sum_a = _pair_sum("g_attn_pair_sum", gw_a, pair_a, core)
    part_a, part_b = (0, 3 * D // 4), (3 * D // 4, D // 4)
    ca_send, ca_recv, sum_a, land_gi, ca_token = _chip_start("g_attn_chip_start_a", sum_a, land_gi, 4, part_a)
    dh = _matmul_slabs_t(dpc, dpa, win_g, name="in_proj_dx", dep=ca_token)
    grad_x, pre_sums = _prenorm_bwd(dh, x2, dout, scale, g_pre)

    small = jnp.concatenate([pre_sums[0:1], pre_sums[1:2], post_sums[0:1],
                             pre_sums[2:3], post_sums[1:2],
                             conv_sums[2:3], conv_sums[3:4], conv_sums[4:5],
                             conv_sums[1:2], conv_sums[0:1], attn_sums[0:1]], axis=1)
    small = jnp.concatenate([small.reshape(8 * D // 128, 128), jnp.broadcast_to(post_sums[2:3, :128], (8, 128))])
    (small_all,) = _all_gather([small], "gather_small")
    cb_send, cb_recv, sum_a, land_gi, cb_token = _chip_start("g_attn_chip_start_b", sum_a, land_gi, 4, part_b,
                                                             after=[small_all])
    small_all = small_all.reshape(NDEV, small.size)
    tot = _sum_rows(small_all, cb_token)
    loss = tot[0, 8 * D]
    g_conv_w = lax.dynamic_slice_in_dim(tot[0:1, 5 * D:5 * D + 3 * C].reshape(1, 3, C), me * CW, CW, axis=2)
    ((g_b_ada, d_b_ada, nm_b_ada, nv_b_ada), (g_g_pre, d_g_pre, nm_g_pre, nv_g_pre),
     (g_g_post, d_g_post, nm_g_post, nv_g_post), (g_conv_w, d_conv_w, nm_conv_w, nv_conv_w),
     (g_conv_b, d_conv_b, nm_conv_b, nv_conv_b), (g_g_conv, d_g_conv, nm_g_conv, nv_g_conv),
     (g_g_attn, d_g_attn, nm_g_attn, nv_g_attn)) = _adamw_small(tot, [
         (b_ada, m_b_ada, v_b_ada, 0), (g_pre, m_g_pre, v_g_pre, 3 * D), (g_post, m_g_post, v_g_post, 4 * D),
         (conv_w, m_conv_w, v_conv_w, g_conv_w), (conv_b, m_conv_b, v_conv_b, 5 * D + 3 * C),
         (g_conv, m_g_conv, v_g_conv, 5 * D + 4 * C), (g_attn, m_g_attn, v_g_attn, 5 * D + 5 * C)])

    dmod_cols = lax.dynamic_slice_in_dim(small_all[:, :3 * D], me * W, W, axis=1)
    g_w_ada, d_w_ada, nm_w_ada, nv_w_ada = _adamw_ada(c_all.T, dmod_cols, w_ada2, m_w_ada[0], v_w_ada[0], cb_token)

    pick_in = jnp.stack([me // 4, (me % 4) // 2]).astype(jnp.int32)
    sum_c, land_gi = _chip_wait("g_conv_chip_wait", sum_c, land_gi, cc_send, cc_recv, 0, after=[g_w_ada])
    sum_a, land_gi = _chip_wait("g_attn_chip_wait_a", sum_a, land_gi, ca_send, ca_recv, 4, [g_w_ada], part_a)
    first = _adamw_sharded("adamw_w_in_a", land_gi, sum_c, sum_a, pick_in, w_in2, m_w_in[0], v_w_in[0], rows=part_a,
                           tr=256)
    sum_a, land_gi = _chip_wait("g_attn_chip_wait_b", sum_a, land_gi, cb_send, cb_recv, 4, [first[0]], part_b)
    g_w_in, d_w_in, nm_w_in, nv_w_in = _adamw_sharded(
        "adamw_w_in_b", land_gi, sum_c, sum_a, pick_in, w_in2, m_w_in[0], v_w_in[0], rows=part_b, prev=first, tr=256)

    return (loss, grad_x[None],
            g_w_ada[None], g_b_ada, g_g_pre, g_w_in[None], g_conv_w, g_conv_b, g_g_conv, g_g_attn, g_w_out[None], g_g_post,
            d_w_ada[None], d_b_ada, d_g_pre, d_w_in[None], d_conv_w, d_conv_b, d_g_conv, d_g_attn, d_w_out[None], d_g_post,
            nm_w_ada[None], nm_b_ada, nm_g_pre, nm_w_in[None], nm_conv_w, nm_conv_b, nm_g_conv, nm_g_attn, nm_w_out[None], nm_g_post,
            nv_w_ada[None], nv_b_ada, nv_g_pre, nv_w_in[None], nv_conv_w, nv_conv_b, nv_g_conv, nv_g_attn, nv_w_out[None], nv_g_post)
```

```python
import functools
import math

import jax
import jax.numpy as jnp
from jax import lax
from jax.experimental import pallas as pl
from jax.experimental.pallas import tpu as pltpu

f32 = jnp.float32
bf16 = jnp.bfloat16

NDEV = 8
HEAD_DIM = 64
PAIR = 2 * HEAD_DIM
BRANCHES = ((128, 1, 1), (512, 4, 1), (2048, 16, 2))
HALF_WIN = 64
EPS = 1e-6
NEG_INF = -1e30
ADAM_LR, ADAM_B1, ADAM_B2, ADAM_EPS, ADAM_WD, ADAM_STEP = 0.001, 0.9, 0.999, 1e-08, 0.01, 10
MESH = pl.DeviceIdType.MESH
VMEM_LIMIT = 56 * 1024 * 1024
HBM_SPEC = pl.BlockSpec(memory_space=pltpu.HBM)
ANY_SPEC = pl.BlockSpec(memory_space=pl.ANY)
SEM_SPEC = pl.BlockSpec(memory_space=pltpu.SEMAPHORE)


def _params(*sem):
    return pltpu.CompilerParams(dimension_semantics=sem or None, vmem_limit_bytes=VMEM_LIMIT)


def _silu(z):
    return z * jax.nn.sigmoid(z)


def _silu_and_slope(z):
    s = jax.nn.sigmoid(z)
    return z * s, s * (1.0 + z * (1.0 - s))


def _my_place():
    x, y, c = lax.axis_index("x"), lax.axis_index("y"), lax.axis_index("c")
    return x, y, c, 4 * x + 2 * y + c


def _peer(x, y, c, k):
    px, py, pc = x ^ (k >> 2 & 1), y ^ (k >> 1 & 1), c ^ (k & 1)
    return (px, py, pc), 4 * px + 2 * py + pc


def _all_gather(arrays, name):
    n = len(arrays)

    def body(*refs):
        srcs, dsts = refs[:n], refs[n:2 * n]
        send_sems, recv_sems, local_sems = refs[2 * n:]
        x, y, c, me = _my_place()
        locals_, sends = [], []
        for t in range(n):
            own = pltpu.make_async_copy(srcs[t], dsts[t].at[me], local_sems.at[t])
            own.start()
            locals_.append(own)
            for k in range(1, NDEV):
                peer, pidx = _peer(x, y, c, k)
                cp = pltpu.make_async_remote_copy(
                    src_ref=srcs[t], dst_ref=dsts[t].at[me], send_sem=send_sems.at[t, k],
                    recv_sem=recv_sems.at[t, k], device_id=peer, device_id_type=MESH)
                cp.start()
                sends.append(cp)
        for t in range(n):
            for k in range(1, NDEV):
                peer, pidx = _peer(x, y, c, k)
                pltpu.make_async_remote_copy(
                    src_ref=srcs[t], dst_ref=dsts[t].at[pidx], send_sem=send_sems.at[t, k],
                    recv_sem=recv_sems.at[t, k], device_id=peer, device_id_type=MESH).wait_recv()
        for cp in sends:
            cp.wait_send()
        for cp in locals_:
            cp.wait()

    return pl.pallas_call(
        body, name=name,
        out_shape=tuple(jax.ShapeDtypeStruct((NDEV,) + a.shape, a.dtype) for a in arrays),
        in_specs=[HBM_SPEC] * n, out_specs=tuple([HBM_SPEC] * n),
        scratch_shapes=[pltpu.SemaphoreType.DMA((n, NDEV)), pltpu.SemaphoreType.DMA((n, NDEV)),
                        pltpu.SemaphoreType.DMA((n,))],
    )(*arrays)


def _comm_call(name, arrays, sems, new_sems, body, after=(), token=False):
    na, ns, nn, nf = len(arrays), len(sems), len(new_sems), len(after)

    def kern(*refs):
        ins, outs = refs[:na + ns + nf], refs[na + ns + nf:]
        body(ins[:na], ins[na:na + ns], outs[:nn])
        if token:
            outs[nn + na][...] = jnp.zeros((8, 128), f32)

    out_shape = ([pltpu.SemaphoreType.DMA(s) for s in new_sems] + [pltpu.HBM(a.shape, a.dtype) for a in arrays]
                 + ([jax.ShapeDtypeStruct((8, 128), f32)] if token else []))
    out_specs = [SEM_SPEC] * nn + [HBM_SPEC] * na + ([pl.BlockSpec(memory_space=pltpu.VMEM)] if token else [])
    res = pl.pallas_call(
        kern, name=name, out_shape=tuple(out_shape),
        in_specs=[HBM_SPEC] * na + [SEM_SPEC] * ns + [ANY_SPEC] * nf, out_specs=tuple(out_specs),
        input_output_aliases={t: nn + t for t in range(na)},
        compiler_params=pltpu.CompilerParams(has_side_effects=pltpu.SideEffectType.DATAFLOW_SIDE_EFFECTING),
    )(*[pltpu.with_memory_space_constraint(a, pltpu.HBM) for a in arrays], *sems, *after)
    return list(res[:nn]), list(res[nn:nn + na]), (res[nn + na] if token else None)


def _remote(src, dst, send_sem, recv_sem, device):
    return pltpu.make_async_remote_copy(src_ref=src, dst_ref=dst, send_sem=send_sem, recv_sem=recv_sem,
                                        device_id=device, device_id_type=MESH)


SAME_CORE = (2, 4, 6)
VIA_SIBLING = (3, 5, 7)


def _weights_forward(name, land, recv, after):
    def body(a, s, new):
        (land,), (recv,), (fsend, frecv) = a, s, new
        x, y, c, me = _my_place()
        sibling, _ = _peer(x, y, c, 1)
        for k in SAME_CORE:
            peer, slot = _peer(x, y, c, k)
            _remote(land.at[slot], land.at[slot], fsend.at[k], recv.at[k], peer).wait_recv()
            _remote(land.at[slot], land.at[slot], fsend.at[k], frecv.at[k ^ 1], sibling).start()

    return _comm_call(name, [land], [recv], [(NDEV,), (NDEV,)], body, after=after)


def _weights_wait(name, land, send, recv, fsend, frecv, after):
    def body(a, s, new):
        (land,), (send, recv, fsend, frecv) = a, s
        x, y, c, me = _my_place()
        sibling, sib_slot = _peer(x, y, c, 1)
        _remote(land.at[sib_slot], land.at[sib_slot], send.at[1], recv.at[1], sibling).wait_recv()
        for k in VIA_SIBLING:
            _, slot = _peer(x, y, c, k)
            _remote(land.at[slot], land.at[slot], fsend.at[k ^ 1], frecv.at[k], sibling).wait_recv()
        for k in (1,) + SAME_CORE:
            peer, _ = _peer(x, y, c, k)
            _remote(land.at[me], land.at[me], send.at[k], recv.at[k], peer).wait_send()
        for k in SAME_CORE:
            _, slot = _peer(x, y, c, k)
            _remote(land.at[slot], land.at[slot], fsend.at[k], frecv.at[k ^ 1], sibling).wait_send()

    return _comm_call(name, [land], [send, recv, fsend, frecv], [], body, after=after)[1][0]


def _diag_relay(x, y, c):
    slot = 4 * (x ^ (1 - c)) + 2 * (y ^ c) + c
    return slot, (x ^ c, y ^ (1 - c), c)


def _w_in_start(land, after):
    def body(a, s, new):
        (land,), (send, recv) = a, new
        x, y, c, me = _my_place()
        for k in (1, 2, 4):
            peer, _ = _peer(x, y, c, k)
            _remote(land.at[me], land.at[me], send.at[k], recv.at[k], peer).start()

    (send, recv), (land,), token = _comm_call("w_in_start", [land], [], [(NDEV,), (NDEV,)], body, after=after, token=True)
    return send, recv, land, token


def _w_in_sibling(land, recv, after):
    def body(a, s, new):
        (land,), (recv,) = a, s
        x, y, c, me = _my_place()
        sibling, slot = _peer(x, y, c, 1)
        _remote(land.at[slot], land.at[slot], recv.at[1], recv.at[1], sibling).wait_recv()

    return _comm_call("w_in_sibling", [land], [recv], [], body, after=after)[1][0]


def _w_in_relay(land, recv, after):
    def body(a, s, new):
        (land,), (recv,), (fsend, frecv) = a, s, new
        x, y, c, me = _my_place()
        sibling, _ = _peer(x, y, c, 1)
        for k in (2, 4):
            peer, slot = _peer(x, y, c, k)
            _remote(land.at[slot], land.at[slot], fsend.at[k], recv.at[k], peer).wait_recv()
        slot, target = _diag_relay(x, y, c)
        _remote(land.at[slot], land.at[slot], fsend.at[6], frecv.at[6], target).start()
        for k in (2, 4):
            _, slot = _peer(x, y, c, k)
            _remote(land.at[slot], land.at[slot], fsend.at[k], frecv.at[k ^ 1], sibling).start()

    (fsend, frecv), (land,), _ = _comm_call("w_in_relay", [land], [recv], [(NDEV,), (NDEV,)], body, after=after)
    return fsend, frecv, land


def _w_in_forwarded(land, frecv, after):
    def body(a, s, new):
        (land,), (frecv,) = a, s
        x, y, c, me = _my_place()
        sibling, _ = _peer(x, y, c, 1)
        for k in (3, 5):
            _, slot = _peer(x, y, c, k)
            _remote(land.at[slot], land.at[slot], frecv.at[k], frecv.at[k], sibling).wait_recv()

    return _comm_call("w_in_forwarded", [land], [frecv], [], body, after=after)[1][0]


def _w_in_diag(land, land_o, frecv, after):
    def body(a, s, new):
        (land, land_o), (frecv,), (dsend, drecv, osend, orecv) = a, s, new
        x, y, c, me = _my_place()
        sibling, _ = _peer(x, y, c, 1)
        peer, slot = _peer(x, y, c, 6)
        _remote(land.at[slot], land.at[slot], dsend.at[6], frecv.at[6], peer).wait_recv()
        _remote(land.at[slot], land.at[slot], dsend.at[6], drecv.at[7], sibling).start()
        for k in (1,) + SAME_CORE:
            peer, _ = _peer(x, y, c, k)
            _remote(land_o.at[me], land_o.at[me], osend.at[k], orecv.at[k], peer).start()

    sems, (land, land_o), _ = _comm_call("w_in_diag", [land, land_o], [frecv], [(NDEV,)] * 4, body, after=after)
    return sems, land, land_o


def _w_in_finish(land, send, fsend, dsend, drecv, after):
    def body(a, s, new):
        (land,), (send, fsend, dsend, drecv) = a, s
        x, y, c, me = _my_place()
        sibling, _ = _peer(x, y, c, 1)
        _, slot = _peer(x, y, c, 7)
        _remote(land.at[slot], land.at[slot], dsend.at[6], drecv.at[7], sibling).wait_recv()
        for k in (1, 2, 4):
            peer, _ = _peer(x, y, c, k)
            _remote(land.at[me], land.at[me], send.at[k], send.at[k], peer).wait_send()
        for k in (2, 4, 6):
            _, slot = _peer(x, y, c, k)
            _remote(land.at[slot], land.at[slot], fsend.at[k], fsend.at[k], sibling).wait_send()
        _, slot = _peer(x, y, c, 6)
        _remote(land.at[slot], land.at[slot], dsend.at[6], dsend.at[6], sibling).wait_send()

    return _comm_call("w_in_finish", [land], [send, fsend, dsend, drecv], [], body, after=after)[1][0]


def _in_proj_part(name, h, land, proj, me_arr, k0, kstep, nk, tm=512):
    S, D = h.shape
    C = land.shape[2]
    tm = min(tm, S)

    def body(me_ref, a_ref, b_ref, *rest):
        rest[-1][...] = jnp.dot(a_ref[...], b_ref[...], preferred_element_type=f32)

    slot = lambda j, me: me[0] ^ (k0 + kstep * j)
    args = [h, land] + ([] if proj is None else [proj])
    grid_spec = pltpu.PrefetchScalarGridSpec(
        num_scalar_prefetch=1, grid=(nk, S // tm),
        in_specs=[pl.BlockSpec((tm, D), lambda j, i, me: (i, 0)),
                  pl.BlockSpec((None, D, C), lambda j, i, me: (slot(j, me), 0, 0))] + [ANY_SPEC] * (len(args) - 2),
        out_specs=pl.BlockSpec((tm, C), lambda j, i, me: (i, slot(j, me))))
    return pl.pallas_call(
        body, name=name, out_shape=jax.ShapeDtypeStruct((S, NDEV * C), f32), grid_spec=grid_spec,
        input_output_aliases={} if proj is None else {3: 0}, compiler_params=_params("arbitrary", "arbitrary"),
    )(me_arr, *args)


NCHIP = NDEV // 2


def _pair_start(name, src):
    npair = src.shape[0] // 2

    def body(a, s, new):
        (src, pair), (send, recv) = a, new
        x, y, c, me = _my_place()
        sibling, _ = _peer(x, y, c, 1)
        for i in range(npair):
            _remote(src.at[2 * i + 1 - c], pair.at[i], send.at[i], recv.at[i], sibling).start()

    pair = lax.empty((npair,) + src.shape[1:], src.dtype)
    (send, recv), (src, pair), token = _comm_call(name, [src, pair], [], [(npair,), (npair,)], body, token=True)
    return send, recv, src, pair, token


def _pair_wait(name, src, pair, send, recv, after):
    npair = pair.shape[0]

    def body(a, s, new):
        (src, pair), (send, recv) = a, s
        x, y, c, me = _my_place()
        sibling, _ = _peer(x, y, c, 1)
        for i in range(npair):
            cp = _remote(src.at[2 * i + 1 - c], pair.at[i], send.at[i], recv.at[i], sibling)
            cp.wait_recv()
            cp.wait_send()

    return _comm_call(name, [src, pair], [send, recv], [], body, after=after)[1]


def _pair_sum(name, src, pair, core, tr=1024):
    npair, R, Cc = pair.shape
    tr = min(tr, R)

    def body(core_ref, a_ref, b_ref, o_ref):
        o_ref[...] = (a_ref[...].astype(f32) + b_ref[...].astype(f32)).astype(o_ref.dtype)

    grid_spec = pltpu.PrefetchScalarGridSpec(
        num_scalar_prefetch=1, grid=(npair, R // tr),
        in_specs=[pl.BlockSpec((None, tr, Cc), lambda i, r, core: (2 * i + core[0], r, 0)),
                  pl.BlockSpec((None, tr, Cc), lambda i, r, core: (i, r, 0))],
        out_specs=pl.BlockSpec((None, tr, Cc), lambda i, r, core: (i, r, 0)))
    return pl.pallas_call(body, name=name, out_shape=jax.ShapeDtypeStruct(pair.shape, pair.dtype),
                          grid_spec=grid_spec, compiler_params=_params("parallel", "parallel"))(core, src, pair)


def _owner_chip(first, i):
    q = first // 2 + i
    return q >> 1 & 1, q & 1


def _chip_start(name, sums, land, first, rows=None, after=()):
    npair = sums.shape[0]
    rows = pl.ds(*(rows or (0, sums.shape[1])))

    def body(a, s, new):
        (sums, land), (send, recv) = a, new
        x, y, c, me = _my_place()
        for i in range(npair):
            ox, oy = _owner_chip(first, i)

            @pl.when((x != ox) | (y != oy))
            def _():
                _remote(sums.at[i, rows], land.at[2 * x + y, rows], send.at[i], recv.at[2 * x + y], (ox, oy, c)).start()

    (send, recv), (sums, land), token = _comm_call(name, [sums, land], [], [(npair,), (NCHIP,)], body, after=after,
                                                   token=True)
    return send, recv, sums, land, token


def _chip_wait(name, sums, land, send, recv, first, after, rows=None):
    npair = sums.shape[0]
    rows = pl.ds(*(rows or (0, sums.shape[1])))

    def body(a, s, new):
        (sums, land), (send, recv) = a, s
        x, y, c, me = _my_place()
        mine = (me >= first) & (me < first + 2 * npair)
        for i in range(npair):
            ox, oy = _owner_chip(first, i)

            @pl.when((x != ox) | (y != oy))
            def _():
                _remote(sums.at[i, rows], land.at[2 * x + y, rows], send.at[i], recv.at[2 * x + y], (ox, oy, c)).wait_send()
        for q in range(NCHIP):
            @pl.when(mine & (2 * x + y != q))
            def _():
                _remote(sums.at[0, rows], land.at[q, rows], send.at[0], recv.at[q], (q >> 1, q & 1, c)).wait_recv()

    return _comm_call(name, [sums, land], [send, recv], [], body, after=after)[1]


def _matmul(a, b, *, name, out_dtype, ta=False, tb=False, b_slots=False, out_slots=0, b_cols=None,
            tm=1024, tn=1024, tk=2048, dep=None):
    M, K = (a.shape[1], a.shape[0]) if ta else a.shape
    col0 = 0
    if b_slots:
        slab = b.shape[2]
        N = b.shape[1] if tb else b.shape[0] * slab
        assert (K if tb else N) == b.shape[0] * slab
    elif b_cols is not None:
        assert not tb
        col0, N = b_cols
    else:
        N = b.shape[0] if tb else b.shape[1]
    tm, tn, tk = min(tm, M), min(tn, N), min(tk, K)
    if b_slots:
        if tb:
            tk = min(tk, slab)
        else:
            tn = min(tn, slab)
    if out_slots:
        tn = min(tn, N // out_slots)
    nm, nn, nk = M // tm, N // tn, K // tk
    assert (nm * tm, nn * tn, nk * tk) == (M, N, K) and col0 % tn == 0, (name, M, N, K, tm, tn, tk)
    j0 = col0 // tn

    a_spec = pl.BlockSpec((tk, tm), lambda i, j, k: (k, i)) if ta else pl.BlockSpec((tm, tk), lambda i, j, k: (i, k))
    if b_slots and tb:
        per = slab // tk
        b_spec = pl.BlockSpec((None, tn, tk), lambda i, j, k: (k // per, j, k % per))
    elif b_slots:
        per = slab // tn
        b_spec = pl.BlockSpec((None, tk, tn), lambda i, j, k: (j // per, k, j % per))
    elif tb:
        b_spec = pl.BlockSpec((tn, tk), lambda i, j, k: (j, k))
    else:
        b_spec = pl.BlockSpec((tk, tn), lambda i, j, k: (k, j + j0))
    if out_slots:
        per_o = (N // out_slots) // tn
        o_spec = pl.BlockSpec((None, tm, tn), lambda i, j, k: (j // per_o, i, j % per_o))
        out_shape = jax.ShapeDtypeStruct((out_slots, M, N // out_slots), out_dtype)
    else:
        o_spec = pl.BlockSpec((tm, tn), lambda i, j, k: (i, j))
        out_shape = jax.ShapeDtypeStruct((M, N), out_dtype)
    dims = (((0 if ta else 1,), (1 if tb else 0,)), ((), ()))
    deps = [] if dep is None else [dep]

    def body(a_ref, b_ref, *rest):
        o_ref = rest[len(deps)]
        prod = lax.dot_general(a_ref[...], b_ref[...], dims, preferred_element_type=f32)
        if nk == 1:
            o_ref[...] = prod.astype(out_dtype)
            return
        acc_ref = rest[len(deps) + 1]
        k = pl.program_id(2)

        @pl.when(k == 0)
        def _():
            acc_ref[...] = prod

        @pl.when((k > 0) & (k < nk - 1))
        def _():
            acc_ref[...] += prod

        @pl.when(k == nk - 1)
        def _():
            o_ref[...] = (acc_ref[...] + prod).astype(out_dtype)

    return pl.pallas_call(
        body, name=name, out_shape=out_shape, grid=(nm, nn, nk),
        in_specs=[a_spec, b_spec] + [ANY_SPEC] * len(deps), out_specs=o_spec,
        scratch_shapes=[pltpu.VMEM((tm, tn), f32)] if nk > 1 else [],
        compiler_params=_params("parallel", "parallel", "arbitrary"),
    )(a, b, *deps)


def _matmul_slabs_t(a_cols, a_slots, b, *, name, tm=512, tn=512, dep=None):
    M = a_cols.shape[0]
    n_slab, N, slab = b.shape
    n1, n2 = a_cols.shape[1] // slab, a_slots.shape[0]
    assert n1 + n2 == n_slab and a_slots.shape[1:] == (M, slab)
    tm, tn = min(tm, M), min(tn, N)
    deps = [] if dep is None else [dep]

    def body(a1_ref, a2_ref, b_ref, *rest):
        o_ref = rest[len(deps)]
        acc = None
        for s in range(n_slab):
            lhs = a1_ref[:, s * slab:(s + 1) * slab] if s < n1 else a2_ref[s - n1]
            prod = lax.dot_general(lhs, b_ref[s], (((1,), (1,)), ((), ())), preferred_element_type=f32)
            acc = prod if acc is None else acc + prod
        o_ref[...] = acc

    return pl.pallas_call(
        body, name=name, out_shape=jax.ShapeDtypeStruct((M, N), f32), grid=(M // tm, N // tn),
        in_specs=[pl.BlockSpec((tm, n1 * slab), lambda i, j: (i, 0)), pl.BlockSpec((n2, tm, slab), lambda i, j: (0, i, 0)),
                  pl.BlockSpec((n_slab, tn, slab), lambda i, j: (0, j, 0))] + [ANY_SPEC] * len(deps),
        out_specs=pl.BlockSpec((tm, tn), lambda i, j: (i, j)), compiler_params=_params("parallel", "parallel"),
    )(a_cols, a_slots, b, *deps)


def _ada_exchange(c_blk, cw_slab, w_ada, b_cols):
    nblk = c_blk.shape[0]
    D, W = w_ada.shape
    CW = cw_slab.shape[1]

    def body(c_ref, cw_ref, w_ref, b_ref, mod_ref, call_ref, cwg_ref, msend, send_sems, recv_sems):
        x, y, c, me = _my_place()
        call_ref[me] = _silu(c_ref[...])
        cwg_ref[me] = cw_ref[...]
        first = []
        for k in range(1, NDEV):
            peer, _ = _peer(x, y, c, k)
            first.append(_remote(call_ref.at[me], call_ref.at[me], send_sems.at[0, k], recv_sems.at[0, k], peer))
            first.append(_remote(cwg_ref.at[me], cwg_ref.at[me], send_sems.at[1, k], recv_sems.at[1, k], peer))
        for cp in first:
            cp.start()
        for k in range(1, NDEV):
            peer, slot = _peer(x, y, c, k)
            _remote(call_ref.at[slot], call_ref.at[slot], send_sems.at[0, k], recv_sems.at[0, k], peer).wait_recv()
            _remote(cwg_ref.at[slot], cwg_ref.at[slot], send_sems.at[1, k], recv_sems.at[1, k], peer).wait_recv()
        mod = jnp.broadcast_to(b_ref[...], (NDEV, W))
        for r in range(nblk):
            mod = mod + lax.dot_general(call_ref[:, r, :], w_ref[r * 128:(r + 1) * 128, :], (((1,), (0,)), ((), ())),
                                        preferred_element_type=f32, precision=lax.Precision.HIGHEST)
        row = lax.broadcasted_iota(jnp.int32, (NDEV, 1), 0)
        pick = lambda j: jnp.broadcast_to(jnp.sum(jnp.where(row == j, mod, 0.0), axis=0, keepdims=True), (8, W))
        mod_ref[me] = pick(me)
        second = []
        for k in range(1, NDEV):
            peer, slot = _peer(x, y, c, k)
            msend[k] = pick(slot)
            second.append(_remote(msend.at[k], mod_ref.at[me], send_sems.at[2, k], recv_sems.at[2, k], peer))
        for cp in second:
            cp.start()
        for k in range(1, NDEV):
            peer, slot = _peer(x, y, c, k)
            _remote(msend.at[k], mod_ref.at[slot], send_sems.at[2, k], recv_sems.at[2, k], peer).wait_recv()
        for cp in first + second:
            cp.wait_send()

    vmem = pl.BlockSpec(memory_space=pltpu.VMEM)
    return pl.pallas_call(
        body, name="ada_exchange",
        out_shape=(jax.ShapeDtypeStruct((NDEV, 8, W), f32), jax.ShapeDtypeStruct((NDEV, nblk, 128), f32),
                   jax.ShapeDtypeStruct((NDEV, 8, CW), f32)),
        in_specs=[vmem] * 4, out_specs=(vmem, vmem, vmem),
        scratch_shapes=[pltpu.VMEM((NDEV, 8, W), f32), pltpu.SemaphoreType.DMA((3, NDEV)),
                        pltpu.SemaphoreType.DMA((3, NDEV))],
        compiler_params=_params(),
    )(c_blk, cw_slab, w_ada, b_cols)


def _prenorm(x, scale, shift, g_pre, dep, tr=512):
    S, D = x.shape
    tr = min(tr, S)

    def body(x_ref, sc_ref, sh_ref, g_ref, dep_ref, h_ref):
        xv = x_ref[...]
        r = lax.rsqrt(jnp.mean(xv * xv, axis=-1, keepdims=True) + EPS)
        h_ref[...] = ((xv * r) * g_ref[...] * (1.0 + sc_ref[...]) + sh_ref[...]).astype(bf16)

    row = pl.BlockSpec((tr, D), lambda i: (i, 0))
    vec = pl.BlockSpec((1, D), lambda i: (0, 0))
    return pl.pallas_call(body, name="prenorm", out_shape=jax.ShapeDtypeStruct((S, D), bf16), grid=(S // tr,),
                          in_specs=[row, vec, vec, vec, ANY_SPEC], out_specs=row, compiler_params=_params("parallel"))(
                              x, scale, shift, g_pre, dep)


def _ext_rows(i, tr, S):
    g = lax.broadcasted_iota(jnp.int32, (tr + 16, 1), 0) + (i * tr - 8)
    return (g >= 0) & (g < S)


def _halo_specs(tr, S, C, col):
    nb8 = S // 8
    main = pl.BlockSpec((tr, C), lambda i: (i, col))
    prev = pl.BlockSpec((8, C), lambda i: (jnp.maximum(i * (tr // 8) - 1, 0), col))
    nxt = pl.BlockSpec((8, C), lambda i: (jnp.minimum((i + 1) * (tr // 8), nb8 - 1), col))
    return prev, main, nxt


def _conv_fwd(proj, conv_w, conv_b, g_conv, tr=512):
    S, C = proj.shape[0], proj.shape[1] // 8
    tr = min(tr, S)

    def body(up, um, un, cp, cm, cn, bg_ref, zc_ref, w_ref, cb_ref, g_ref, o_ref):
        i = pl.program_id(0)
        exists = _ext_rows(i, tr, S)
        u = jnp.concatenate([up[...], um[...], un[...]], axis=0)
        cg = jnp.concatenate([cp[...], cm[...], cn[...]], axis=0)
        t = jnp.where(exists, cg * u, 0.0)
        t_before = pltpu.roll(t, 1, 0)[8:tr + 8]
        t_after = pltpu.roll(t, tr + 15, 0)[8:tr + 8]
        w = w_ref[...]
        cv = w[0:1] * t_before + w[1:2] * t[8:tr + 8] + w[2:3] * t_after + cb_ref[...]
        yc = bg_ref[...] * cv
        rc = lax.rsqrt(jnp.mean(yc * yc, axis=-1, keepdims=True) + EPS)
        o_ref[...] = ((yc * rc) * g_ref[...] * _silu(zc_ref[...])).astype(bf16)

    u_specs = _halo_specs(tr, S, C, 0)
    c_specs = _halo_specs(tr, S, C, 2)
    vec = pl.BlockSpec((1, C), lambda i: (0, 0))
    return pl.pallas_call(
        body, name="conv_fwd", out_shape=jax.ShapeDtypeStruct((S, 2 * C), bf16), grid=(S // tr,),
        in_specs=[*u_specs, *c_specs, pl.BlockSpec((tr, C), lambda i: (i, 1)), pl.BlockSpec((tr, C), lambda i: (i, 3)),
                  pl.BlockSpec((8, C), lambda i: (0, 0)), vec, vec],
        out_specs=pl.BlockSpec((tr, C), lambda i: (i, 0)), compiler_params=_params("parallel"),
    )(proj, proj, proj, proj, proj, proj, proj, proj, conv_w, conv_b, g_conv)


def _branch_geometry(S, r, inter):
    L = S // r * inter
    nq = min(128, L)
    nk = min(nq + 2 * HALF_WIN * inter, L)
    assert L % nq == 0 and (L == nk or L >= nq + 2 * HALF_WIN * inter)
    return L, nq, nk, L // nq


QUAD = 4


def _to_quad(dst, src, S):
    n = S // QUAD
    for rho in range(QUAD):
        dst[pl.ds(rho * n, n), :] = src[pl.ds(rho, n, stride=QUAD), :]


def _block_rows(idx, r, inter, S, L, nq, nk, nblk):
    rho, qb = (0, idx) if r == 1 else (idx // nblk, idx % nblk)
    i0 = qb * nq
    ws = jnp.clip(i0 - HALF_WIN * inter, 0, L - nk)
    if r == 1:
        return pl.ds(pl.multiple_of(i0, 8), nq), pl.ds(pl.multiple_of(ws, 8), nk), i0 - ws
    assert r % (QUAD * inter) == 0
    step = r // QUAD // inter
    base = (rho % QUAD) * (S // QUAD) + rho // QUAD
    if step == 1:
        return pl.ds(pl.multiple_of(base + i0, 8), nq), pl.ds(pl.multiple_of(base + ws, 8), nk), i0 - ws
    return pl.ds(base + step * i0, nq, stride=step), pl.ds(base + step * ws, nk, stride=step), i0 - ws


N_CASES = 3
SCALE = HEAD_DIM ** -0.5
ATTN_UNROLL = 16


def _bias_shape(S):
    shapes = [_branch_geometry(S, r, inter)[1:3] for _, r, inter in BRANCHES]
    return (len(BRANCHES) * N_CASES * 2, max(nq for nq, _ in shapes), max(nk for _, nk in shapes))


def _bias_index(b, case, head):
    return (b * N_CASES + case) * 2 + head


def _fill_bias(bias_scr, sl_ref, S):
    sl = sl_ref[...]
    slope = (sl[0:1, 0:1], sl[0:1, HEAD_DIM:HEAD_DIM + 1])
    for b, (_, r, inter) in enumerate(BRANCHES):
        L, nq, nk, nblk = _branch_geometry(S, r, inter)
        rel = lax.broadcasted_iota(jnp.int32, (nq, nk), 0) - lax.broadcasted_iota(jnp.int32, (nq, nk), 1)
        for case in range(N_CASES):
            d = jnp.abs(rel + case * HALF_WIN)
            valid = d <= HALF_WIN * inter
            if inter > 1:
                valid = valid & (jnp.bitwise_and(d, inter - 1) == 0)
            dist = d.astype(f32) * float(r // inter)
            for head in range(2):
                bias_scr[_bias_index(b, case, head), 0:nq, 0:nk] = jnp.where(valid, -slope[head] * dist, NEG_INF)


def _head_slopes(n_heads):
    slopes = 2.0 ** (-8.0 * jnp.arange(1, n_heads + 1, dtype=f32) / n_heads)
    return jnp.broadcast_to(jnp.repeat(slopes.reshape(n_heads // 2, 2), HEAD_DIM, axis=1)[:, None, :],
                            (n_heads // 2, 8, PAIR))


def _attn_fwd(proj, slopes):
    S, C = proj.shape[0], proj.shape[1] // 8
    npair = C // PAIR

    def body(q_ref, k_ref, v_ref, sl_ref, o_ref, lse_ref, m_scr, l_scr, a_scr, bias_scr, q4_scr, k4_scr, v4_scr):
        lane = lax.broadcasted_iota(jnp.int32, (1, PAIR), 1)
        first = lane < HEAD_DIM
        _fill_bias(bias_scr, sl_ref, S)
        for dst, src in ((q4_scr, q_ref), (k4_scr, k_ref), (v4_scr, v_ref)):
            _to_quad(dst, src, S)

        for b, (_, r, inter) in enumerate(BRANCHES):
            L, nq, nk, nblk = _branch_geometry(S, r, inter)
            qs, ks, vs = (q_ref, k_ref, v_ref) if r == 1 else (q4_scr, k4_scr, v4_scr)

            def step(idx, carry, b=b, r=r, L=L, nq=nq, nk=nk, nblk=nblk, qs=qs, ks=ks, vs=vs):
                qrows, krows, off = _block_rows(idx, r, inter, S, L, nq, nk, nblk)
                case = off // HALF_WIN
                q2 = qs[qrows, :] * SCALE
                k2 = ks[krows, :].astype(bf16)
                v2 = vs[krows, :].astype(bf16)
                ms, accs = [], []
                for hh in range(2):
                    mine = first if hh == 0 else ~first
                    qh = jnp.where(mine, q2, 0.0).astype(bf16)
                    s = lax.dot_general(qh, k2, (((1,), (1,)), ((), ())), preferred_element_type=f32)
                    s = s + bias_scr[_bias_index(b, case, hh), 0:nq, 0:nk]
                    m = jnp.max(s, axis=-1, keepdims=True)
                    p = jnp.exp(s - m).astype(bf16)
                    vh = jnp.where(mine, v2, jnp.ones_like(v2))
                    ms.append(m)
                    accs.append(jnp.dot(p, vh, preferred_element_type=f32))
                m_scr[b, qrows, :] = jnp.where(first, ms[0], ms[1])
                a_scr[b, qrows, :] = jnp.where(first, accs[0], accs[1])
                l_scr[b, qrows, :] = jnp.where(first, accs[1], accs[0])
                return carry

            lax.fori_loop(0, S // nq, step, 0, unroll=min(ATTN_UNROLL, S // nq))

        n4 = S // QUAD
        ch = min(256, n4)
        nch = n4 // ch

        def merge(i, carry):
            rho, part = i // nch, i % nch
            sorted_rows = pl.ds(pl.multiple_of(rho * n4 + part * ch, 8), ch)
            token_rows = pl.ds(rho + QUAD * part * ch, ch, stride=QUAD)
            rows = (token_rows,) + (sorted_rows,) * (len(BRANCHES) - 1)
            ms = [m_scr[b, rows[b], :] for b in range(len(BRANCHES))]
            m = functools.reduce(jnp.maximum, ms)
            l = jnp.zeros((ch, PAIR), f32)
            acc = jnp.zeros((ch, PAIR), f32)
            for b in range(len(BRANCHES)):
                w = jnp.exp(ms[b] - m)
                l = l + w * pltpu.roll(l_scr[b, rows[b], :], HEAD_DIM, 1)
                acc = acc + w * a_scr[b, rows[b], :]
            o_ref[token_rows, :] = acc / l
            lse_ref[token_rows, :] = m + jnp.log(l)
            return carry

        lax.fori_loop(0, QUAD * nch, merge, 0)

    blk = lambda part: pl.BlockSpec((S, PAIR), lambda p: (0, part * npair + p))
    out = pl.BlockSpec((S, PAIR), lambda p: (0, p))
    return pl.pallas_call(
        body, name="attn_fwd",
        out_shape=(jax.ShapeDtypeStruct((S, C), f32), jax.ShapeDtypeStruct((S, C), f32)), grid=(npair,),
        in_specs=[blk(4), blk(5), blk(6), pl.BlockSpec((None, 8, PAIR), lambda p: (p, 0, 0))],
        out_specs=(out, out),
        scratch_shapes=[pltpu.VMEM((3, S, PAIR), f32)] * 3 + [pltpu.VMEM(_bias_shape(S), f32)]
        + [pltpu.VMEM((S, PAIR), f32)] * 3,
        compiler_params=_params("parallel"),
    )(proj, proj, proj, slopes)


def _attn_post(ycat, o, proj, g_attn, tr=512):
    S, C = o.shape
    tr = min(tr, S)

    def body(y_ref, o_ref, z_ref, g_ref, out_ref):
        del y_ref
        ov = o_ref[...]
        ra = lax.rsqrt(jnp.mean(ov * ov, axis=-1, keepdims=True) + EPS)
        out_ref[...] = ((ov * ra) * g_ref[...] * _silu(z_ref[...])).astype(bf16)

    return pl.pallas_call(
        body, name="attn_post", out_shape=jax.ShapeDtypeStruct(ycat.shape, ycat.dtype), grid=(S // tr,),
        in_specs=[HBM_SPEC, pl.BlockSpec((tr, C), lambda i: (i, 0)), pl.BlockSpec((tr, C), lambda i: (i, 7)),
                  pl.BlockSpec((1, C), lambda i: (0, 0))],
        out_specs=pl.BlockSpec((tr, C), lambda i: (i, 1)), input_output_aliases={0: 0},
        compiler_params=_params("arbitrary"),
    )(ycat, o, proj, g_attn)


def _sandwich(y, x, target, gate, g_post, tr=256):
    S, D = y.shape
    tr = min(tr, S)

    def body(y_ref, x_ref, t_ref, gate_ref, g_ref, dy_ref, dout_ref, sums_ref):
        i = pl.program_id(0)
        gate, g = gate_ref[...], g_ref[...]
        gg = gate * g
        yv = y_ref[...]
        rp = lax.rsqrt(jnp.mean(yv * yv, axis=-1, keepdims=True) + EPS)
        yhat = yv * rp
        err = (x_ref[...] + gg * yhat) - t_ref[...]
        dout = err * (1.0 / D)
        dout_ref[...] = dout
        q = dout * yhat
        w = dout * gg
        dy_ref[...] = (rp * (w - yhat * jnp.sum(q * gg, axis=-1, keepdims=True) * (1.0 / D))).astype(bf16)
        loss = 0.5 * jnp.sum(jnp.mean(err * err, axis=-1, keepdims=True), axis=0, keepdims=True)
        q_sum = jnp.sum(q, axis=0, keepdims=True)
        row = lax.broadcasted_iota(jnp.int32, (8, D), 0)
        upd = jnp.where(row == 0, q_sum * g, jnp.where(row == 1, q_sum * gate, jnp.where(row == 2, loss, 0.0)))

        @pl.when(i == 0)
        def _():
            sums_ref[...] = upd

        @pl.when(i > 0)
        def _():
            sums_ref[...] += upd

    row = pl.BlockSpec((tr, D), lambda i: (i, 0))
    vec = pl.BlockSpec((1, D), lambda i: (0, 0))
    return pl.pallas_call(
        body, name="sandwich",
        out_shape=(jax.ShapeDtypeStruct((S, D), bf16), jax.ShapeDtypeStruct((S, D), f32), jax.ShapeDtypeStruct((8, D), f32)),
        grid=(S // tr,), in_specs=[row, row, row, vec, vec],
        out_specs=(row, row, pl.BlockSpec((8, D), lambda i: (0, 0))), compiler_params=_params("arbitrary"),
    )(y, x, target, gate, g_post)


def _conv_bwd(proj, dycat, conv_w, conv_b, g_conv, dep, tr=256):
    S, C = proj.shape[0], proj.shape[1] // 8
    tr = min(tr, S)
    n = tr + 16

    def body(*refs):
        ins, (w_ref, cb_ref, g_ref, _, dp_ref, sums_ref) = refs[:15], refs[15:]
        i = pl.program_id(0)
        exists = _ext_rows(i, tr, S)
        u, bg, cg, zc, dyn = (jnp.concatenate([ins[3 * t][...], ins[3 * t + 1][...], ins[3 * t + 2][...]], axis=0)
                              for t in range(5))
        w = w_ref[...]
        t = jnp.where(exists, cg * u, 0.0)
        t_before, t_after = pltpu.roll(t, 1, 0), pltpu.roll(t, n - 1, 0)
        cv = w[0:1] * t_before + w[1:2] * t + w[2:3] * t_after + cb_ref[...]
        yc = bg * cv
        rc = lax.rsqrt(jnp.mean(yc * yc, axis=-1, keepdims=True) + EPS)
        yhat = yc * rc
        sz, dsz = _silu_and_slope(zc)
        wgt = dyn * g_ref[...] * sz
        dyc = rc * (wgt - yhat * jnp.mean(wgt * yhat, axis=-1, keepdims=True))
        dcv = jnp.where(exists, dyc * bg, 0.0)
        dt = w[0:1] * pltpu.roll(dcv, n - 1, 0) + w[1:2] * dcv + w[2:3] * pltpu.roll(dcv, 1, 0)
        mid = slice(8, tr + 8)
        dp_ref[:, 0:C] = (dt * cg)[mid].astype(bf16)
        dp_ref[:, C:2 * C] = (dyc * cv)[mid].astype(bf16)
        dp_ref[:, 2 * C:3 * C] = (dt * u)[mid].astype(bf16)
        dp_ref[:, 3 * C:4 * C] = (dyn * yhat * g_ref[...] * dsz)[mid].astype(bf16)
        colsum = lambda v: jnp.sum(v[mid], axis=0, keepdims=True)
        parts = [colsum(dyn * yhat * sz), colsum(dcv), colsum(dcv * t_before), colsum(dcv * t), colsum(dcv * t_after)]
        row = lax.broadcasted_iota(jnp.int32, (8, C), 0)
        upd = jnp.zeros((8, C), f32)
        for j, pj in enumerate(parts):
            upd = jnp.where(row == j, pj, upd)

        @pl.when(i == 0)
        def _():
            sums_ref[...] = upd

        @pl.when(i > 0)
        def _():
            sums_ref[...] += upd

    specs = []
    for col in range(4):
        specs += _halo_specs(tr, S, C, col)
    specs += _halo_specs(tr, S, C, 0)
    vec = pl.BlockSpec((1, C), lambda i: (0, 0))
    return pl.pallas_call(
        body, name="conv_bwd",
        out_shape=(jax.ShapeDtypeStruct((S, 4 * C), bf16), jax.ShapeDtypeStruct((8, C), f32)), grid=(S // tr,),
        in_specs=[*specs, pl.BlockSpec((8, C), lambda i: (0, 0)), vec, vec, ANY_SPEC],
        out_specs=(pl.BlockSpec((tr, 4 * C), lambda i: (i, 0)), pl.BlockSpec((8, C), lambda i: (0, 0))),
        compiler_params=_params("arbitrary"),
    )(*([proj] * 12), dycat, dycat, dycat, conv_w, conv_b, g_conv, dep)


def _attn_post_bwd(o, proj, dycat, g_attn, dep, tr=512):
    S, C = o.shape
    tr = min(tr, S)

    def body(o_ref, z_ref, dy_ref, g_ref, dep_ref, do_ref, dz_ref, sums_ref):
        i = pl.program_id(0)
        ov, zv, dyn = o_ref[...], z_ref[...], dy_ref[...]
        ra = lax.rsqrt(jnp.mean(ov * ov, axis=-1, keepdims=True) + EPS)
        ohat = ov * ra
        sz, dsz = _silu_and_slope(zv)
        wgt = dyn * g_ref[...] * sz
        do_ref[...] = ra * (wgt - ohat * jnp.mean(wgt * ohat, axis=-1, keepdims=True))
        dz_ref[...] = (dyn * ohat * g_ref[...] * dsz).astype(bf16)
        row = lax.broadcasted_iota(jnp.int32, (8, C), 0)
        upd = jnp.where(row == 0, jnp.sum(dyn * ohat * sz, axis=0, keepdims=True), 0.0)

        @pl.when(i == 0)
        def _():
            sums_ref[...] = upd

        @pl.when(i > 0)
        def _():
            sums_ref[...] += upd

    return pl.pallas_call(
        body, name="attn_post_bwd",
        out_shape=(jax.ShapeDtypeStruct((S, C), f32), jax.ShapeDtypeStruct((4, S, C), bf16),
                   jax.ShapeDtypeStruct((8, C), f32)),
        grid=(S // tr,),
        in_specs=[pl.BlockSpec((tr, C), lambda i: (i, 0)), pl.BlockSpec((tr, C), lambda i: (i, 7)),
                  pl.BlockSpec((tr, C), lambda i: (i, 1)), pl.BlockSpec((1, C), lambda i: (0, 0)), ANY_SPEC],
        out_specs=(pl.BlockSpec((tr, C), lambda i: (i, 0)), pl.BlockSpec((None, tr, C), lambda i: (3, i, 0)),
                   pl.BlockSpec((8, C), lambda i: (0, 0))),
        compiler_params=_params("arbitrary"),
    )(o, proj, dycat, g_attn, dep)


def _attn_bwd(proj, o, do, lse, slopes, dqkvz, dep):
    S, C = o.shape
    npair = C // PAIR

    def body(q_ref, k_ref, v_ref, o_ref, do_ref, lse_ref, sl_ref, old_ref, dep_ref, dqkv_ref,
             acc_scr, dl_scr, quad_scr, bias_scr):
        lane = lax.broadcasted_iota(jnp.int32, (1, PAIR), 1)
        first = lane < HEAD_DIM
        _fill_bias(bias_scr, sl_ref, S)
        ch = min(256, S)

        def prep(i, carry):
            rows = pl.ds(pl.multiple_of(i * ch, 8), ch)
            prod = do_ref[rows, :] * o_ref[rows, :]
            d0 = jnp.sum(jnp.where(first, prod, 0.0), axis=-1, keepdims=True)
            d1 = jnp.sum(jnp.where(first, 0.0, prod), axis=-1, keepdims=True)
            dl_scr[rows, :] = jnp.where(first, d0, d1)
            zero = jnp.zeros((ch, PAIR), f32)
            for order in range(2):
                for t in range(3):
                    acc_scr[order, t, rows, :] = zero
            return carry

        lax.fori_loop(0, S // ch, prep, 0)
        token_srcs = (q_ref, k_ref, v_ref, do_ref, lse_ref, dl_scr)
        for j, src in enumerate(token_srcs):
            _to_quad(quad_scr.at[j], src, S)

        for b, (_, r, inter) in enumerate(BRANCHES):
            L, nq, nk, nblk = _branch_geometry(S, r, inter)
            order = 0 if r == 1 else 1
            srcs = token_srcs if r == 1 else tuple(quad_scr.at[j] for j in range(6))

            def step(idx, carry, b=b, r=r, L=L, nq=nq, nk=nk, nblk=nblk, order=order, srcs=srcs):
                qs, ks, vs, dos, lses, dls = srcs
                dq_scr, dk_scr, dv_scr = (acc_scr.at[order, t] for t in range(3))
                qrows, krows, off = _block_rows(idx, r, inter, S, L, nq, nk, nblk)
                case = off // HALF_WIN
                q2 = qs[qrows, :] * SCALE
                k2 = ks[krows, :].astype(bf16)
                v2 = vs[krows, :].astype(bf16)
                do2 = dos[qrows, :]
                lse2 = lses[qrows, :]
                dl2 = dls[qrows, :]
                dq2 = jnp.zeros((nq, PAIR), f32)
                dk2 = jnp.zeros((nk, PAIR), f32)
                dv2 = jnp.zeros((nk, PAIR), f32)
                for hh in range(2):
                    mine = first if hh == 0 else ~first
                    lo = hh * HEAD_DIM
                    qh = jnp.where(mine, q2, 0.0).astype(bf16)
                    doh = jnp.where(mine, do2, 0.0).astype(bf16)
                    s = lax.dot_general(qh, k2, (((1,), (1,)), ((), ())), preferred_element_type=f32)
                    s = s + bias_scr[_bias_index(b, case, hh), 0:nq, 0:nk]
                    p = jnp.exp(s - lse2[:, lo:lo + 1])
                    dv2 = dv2 + lax.dot_general(p.astype(bf16), doh, (((0,), (0,)), ((), ())), preferred_element_type=f32)
                    dp = lax.dot_general(doh, v2, (((1,), (1,)), ((), ())), preferred_element_type=f32)
                    ds = (p * (dp - dl2[:, lo:lo + 1])).astype(bf16)
                    dq2 = dq2 + jnp.where(mine, jnp.dot(ds, k2, preferred_element_type=f32), 0.0)
                    dk2 = dk2 + lax.dot_general(ds, qh, (((0,), (0,)), ((), ())), preferred_element_type=f32)
                dq_scr[qrows, :] = dq_scr[qrows, :] + dq2
                dk_scr[krows, :] = dk_scr[krows, :] + dk2
                dv_scr[krows, :] = dv_scr[krows, :] + dv2
                return carry

            lax.fori_loop(0, S // nq, step, 0, unroll=min(ATTN_UNROLL, S // nq))

        n4 = S // QUAD
        for t in range(3):
            for rho in range(QUAD):
                token_rows = pl.ds(rho, n4, stride=QUAD)
                acc_scr[0, t, token_rows, :] = acc_scr[0, t, token_rows, :] + acc_scr[1, t, pl.ds(rho * n4, n4), :]
        dqkv_ref[0] = (acc_scr[0, 0] * SCALE).astype(bf16)
        dqkv_ref[1] = acc_scr[0, 1].astype(bf16)
        dqkv_ref[2] = acc_scr[0, 2].astype(bf16)

    blk = lambda part: pl.BlockSpec((S, PAIR), lambda p: (0, part * npair + p))
    own = pl.BlockSpec((S, PAIR), lambda p: (0, p))
    return pl.pallas_call(
        body, name="attn_bwd", out_shape=jax.ShapeDtypeStruct(dqkvz.shape, dqkvz.dtype), grid=(npair,),
        in_specs=[blk(4), blk(5), blk(6), own, own, own, pl.BlockSpec((None, 8, PAIR), lambda p: (p, 0, 0)),
                  ANY_SPEC, ANY_SPEC],
        out_specs=pl.BlockSpec((3, S, PAIR), lambda p: (0, 0, p)), input_output_aliases={7: 0},
        scratch_shapes=[pltpu.VMEM((2, 3, S, PAIR), f32), pltpu.VMEM((S, PAIR), f32), pltpu.VMEM((6, S, PAIR), f32),
                        pltpu.VMEM(_bias_shape(S), f32)],
        compiler_params=_params("arbitrary"),
    )(proj, proj, proj, o, do, lse, slopes, dqkvz, dep)


def _prenorm_bwd(dh, x, dout, scale, g_pre, tr=256):
    S, D = x.shape
    tr = min(tr, S)

    def body(dh_ref, x_ref, dout_ref, sc_ref, g_ref, gx_ref, sums_ref):
        i = pl.program_id(0)
        xv, dhv = x_ref[...], dh_ref[...]
        r = lax.rsqrt(jnp.mean(xv * xv, axis=-1, keepdims=True) + EPS)
        xn = xv * r
        dxn = dhv * (g_ref[...] * (1.0 + sc_ref[...]))
        gx_ref[...] = dout_ref[...] + r * (dxn - xn * jnp.mean(dxn * xn, axis=-1, keepdims=True))
        dhx = dhv * xn
        row = lax.broadcasted_iota(jnp.int32, (8, D), 0)
        upd = jnp.where(row == 0, jnp.sum(dhv, axis=0, keepdims=True),
                        jnp.where(row == 1, jnp.sum(dhx, axis=0, keepdims=True) * g_ref[...],
                                  jnp.where(row == 2, jnp.sum(dhx, axis=0, keepdims=True) * (1.0 + sc_ref[...]), 0.0)))

        @pl.when(i == 0)
        def _():
            sums_ref[...] = upd

        @pl.when(i > 0)
        def _():
            sums_ref[...] += upd

    row = pl.BlockSpec((tr, D), lambda i: (i, 0))
    vec = pl.BlockSpec((1, D), lambda i: (0, 0))
    return pl.pallas_call(
        body, name="prenorm_bwd",
        out_shape=(jax.ShapeDtypeStruct((S, D), f32), jax.ShapeDtypeStruct((8, D), f32)), grid=(S // tr,),
        in_specs=[row, row, row, vec, vec], out_specs=(row, pl.BlockSpec((8, D), lambda i: (0, 0))),
        compiler_params=_params("arbitrary"),
    )(dh, x, dout, scale, g_pre)


def _adamw(w, g, m, v):
    m = ADAM_B1 * m + (1.0 - ADAM_B1) * g
    v = ADAM_B2 * v + (1.0 - ADAM_B2) * (g * g)
    m_hat = m / (1.0 - ADAM_B1 ** ADAM_STEP)
    v_hat = v / (1.0 - ADAM_B2 ** ADAM_STEP)
    delta = -ADAM_LR * (m_hat / (jnp.sqrt(v_hat) + ADAM_EPS) + ADAM_WD * w)
    return delta, m, v


def _sum_rows(parts, dep):
    P = parts.shape[1]

    def body(p_ref, dep_ref, o_ref):
        acc = p_ref[0:1, :]
        for j in range(1, NDEV):
            acc = acc + p_ref[j:j + 1, :]
        o_ref[...] = jnp.broadcast_to(acc, (8, P))

    vmem = pl.BlockSpec(memory_space=pltpu.VMEM)
    return pl.pallas_call(body, name="sum_small", out_shape=jax.ShapeDtypeStruct((8, P), f32),
                          in_specs=[vmem, ANY_SPEC], out_specs=vmem, compiler_params=_params())(parts, dep)


def _adamw_small(tot, params):
    given = [p[3] for p in params if not isinstance(p[3], int)]

    def body(tot_ref, *refs):
        given_refs = list(refs[:len(given)])
        ins = refs[len(given):len(given) + 3 * len(params)]
        outs = refs[len(given) + 3 * len(params):]
        for t, (w, _, _, where) in enumerate(params):
            w_ref, m_ref, v_ref = ins[3 * t:3 * t + 3]
            g = tot_ref[0:1, where:where + w.size] if isinstance(where, int) else given_refs.pop(0)[...]
            outs[4 * t][...] = g
            outs[4 * t + 1][...], outs[4 * t + 2][...], outs[4 * t + 3][...] = _adamw(w_ref[...], g, m_ref[...], v_ref[...])

    out_shape = tuple(jax.ShapeDtypeStruct(p[0].shape, f32) for p in params for _ in range(4))
    res = pl.pallas_call(body, name="adamw_small", out_shape=out_shape, compiler_params=_params())(
        tot, *given, *[a for p in params for a in p[:3]])
    return [res[4 * t:4 * t + 4] for t in range(len(params))]


def _adamw_sharded(name, parts, sums_a, sums_b, pick, w, m, v, rows=None, prev=None, tr=128):
    R, Cc = w.shape
    r0, nr = rows or (0, R)
    tr = math.gcd(tr, r0, nr)
    n, b0 = parts.shape[0], r0 // tr

    def body(pick_ref, p_ref, a_ref, b_ref, w_ref, m_ref, v_ref, *rest):
        g_ref, d_ref, nm_ref, nv_ref = rest[-4:]
        g = jnp.where(pick_ref[0] == 1, b_ref[...], a_ref[...]).astype(f32)
        for j in range(n):
            g = g + p_ref[j].astype(f32)
        g_ref[...] = g
        d_ref[...], nm_ref[...], nv_ref[...] = _adamw(w_ref[...], g, m_ref[...], v_ref[...])

    row = pl.BlockSpec((tr, Cc), lambda i, pick: (i + b0, 0))
    mine = pl.BlockSpec((None, tr, Cc), lambda i, pick: (pick[1], i + b0, 0))
    out = jax.ShapeDtypeStruct((R, Cc), f32)
    prev = list(prev or [])
    grid_spec = pltpu.PrefetchScalarGridSpec(
        num_scalar_prefetch=1, grid=(nr // tr,),
        in_specs=[pl.BlockSpec((n, tr, Cc), lambda i, pick: (0, i + b0, 0)), mine, mine, row, row, row]
        + [ANY_SPEC] * len(prev),
        out_specs=(row, row, row, row))
    return pl.pallas_call(
        body, name=name, out_shape=(out, out, out, out), grid_spec=grid_spec,
        input_output_aliases={7 + t: t for t in range(len(prev))}, compiler_params=_params("arbitrary"),
    )(pick, parts, sums_a, sums_b, w, m, v, *prev)


def _adamw_ada(c_t, dmod_cols, w, m, v, dep, tr=512):
    D, W = w.shape
    tr = min(tr, D)

    def body(c_ref, dm_ref, w_ref, m_ref, v_ref, dep_ref, g_ref, d_ref, nm_ref, nv_ref):
        cv, dm = c_ref[...], dm_ref[...]
        g = cv[:, 0:1] * dm[0:1, :]
        for b in range(1, NDEV):
            g = g + cv[:, b:b + 1] * dm[b:b + 1, :]
        g_ref[...] = g
        d_ref[...], nm_ref[...], nv_ref[...] = _adamw(w_ref[...], g, m_ref[...], v_ref[...])

    row = pl.BlockSpec((tr, W), lambda i: (i, 0))
    out = jax.ShapeDtypeStruct((D, W), f32)
    return pl.pallas_call(
        body, name="adamw_ada", out_shape=(out, out, out, out), grid=(D // tr,),
        in_specs=[pl.BlockSpec((tr, NDEV), lambda i: (i, 0)), pl.BlockSpec((NDEV, W), lambda i: (0, 0)), row, row, row,
                  ANY_SPEC],
        out_specs=(row, row, row, row), compiler_params=_params("parallel"),
    )(c_t, dmod_cols, w, m, v, dep)


def kernel(x, c, w_ada, b_ada, g_pre, w_in, conv_w, conv_b, g_conv, g_attn, w_out, g_post, loss_target, m_w_ada, m_b_ada, m_g_pre, m_w_in, m_conv_w, m_conv_b, m_g_conv, m_g_attn, m_w_out, m_g_post, v_w_ada, v_b_ada, v_g_pre, v_w_in, v_conv_w, v_conv_b, v_g_conv, v_g_attn, v_w_out, v_g_post):
    S, D = x.shape[1], x.shape[2]
    C = D // 2
    W = w_ada.shape[2]
    CW = conv_w.shape[2]
    me = 4 * lax.axis_index("x") + 2 * lax.axis_index("y") + lax.axis_index("c")
    x2, tgt = x[0], loss_target[0]
    w_ada2, w_in2, w_out2 = w_ada[0], w_in[0], w_out[0]

    R = D // NDEV
    core = lax.axis_index("c").astype(jnp.int32).reshape(1)

    cw_slab = jnp.zeros((8, CW), f32).at[:3].set(conv_w[0])
    b_cols = lax.dynamic_slice_in_dim(b_ada, me * W, W, axis=1)
    mod_slabs, c_blocks, cw_g = _ada_exchange(c.reshape(D // 128, 128), cw_slab, w_ada2, b_cols)
    c_all = c_blocks.reshape(NDEV, D)
    conv_w_full = jnp.transpose(cw_g, (1, 0, 2)).reshape(8, C)
    mod = mod_slabs[:, 0, :].reshape(1, 3 * D)
    shift, scale, gate = mod[:, :D], mod[:, D:2 * D], mod[:, 2 * D:]

    land_i = lax.dynamic_update_slice(lax.empty((NDEV, D, C), bf16), w_in2.astype(bf16)[None], (me, 0, 0))
    land_o = lax.dynamic_update_slice(lax.empty((NDEV, R, D), bf16), w_out2.astype(bf16)[None], (me, 0, 0))
    wi_send, wi_recv, land_i, w_token = _w_in_start(land_i, [mod_slabs])

    me_arr = me.astype(jnp.int32).reshape(1)
    h = _prenorm(x2, scale, shift, g_pre, w_token)
    land_i = _w_in_sibling(land_i, wi_recv, after=[h])
    proj = _in_proj_part("in_proj_a", h, land_i, None, me_arr, 0, 1, 2)
    fi_send, fi_recv, land_i = _w_in_relay(land_i, wi_recv, after=[proj])
    proj = _in_proj_part("in_proj_b", h, land_i, proj, me_arr, 2, 2, 2)
    land_i = _w_in_forwarded(land_i, fi_recv, after=[proj])
    proj = _in_proj_part("in_proj_c", h, land_i, proj, me_arr, 3, 2, 2)
    (di_send, di_recv, wo_send, wo_recv), land_i, land_o = _w_in_diag(land_i, land_o, fi_recv, after=[proj])
    proj = _in_proj_part("in_proj_d", h, land_i, proj, me_arr, 6, 1, 1)
    win_g = _w_in_finish(land_i, wi_send, fi_send, di_send, di_recv, after=[proj])
    proj = _in_proj_part("in_proj_e", h, win_g, proj, me_arr, 7, 1, 1)
    slopes = _head_slopes(C // HEAD_DIM)
    ycat = _conv_fwd(proj, conv_w_full, conv_b, g_conv)
    o, lse = _attn_fwd(proj, slopes)
    (fo_send, fo_recv), (land_o,), _ = _weights_forward("w_out_forward", land_o, wo_recv, after=[o])
    ycat = _attn_post(ycat, o, proj, g_attn)
    wout_g = _weights_wait("w_out_wait", land_o, wo_send, wo_recv, fo_send, fo_recv, after=[ycat])
    wout_full = wout_g.reshape(D, D)
    y = _matmul(ycat, wout_full, name="out_proj", out_dtype=f32)
    dy, dout, post_sums = _sandwich(y, x2, tgt, gate, g_post)

    chip = me // 2

    def landing(rows, cols):
        return lax.dynamic_update_slice(lax.empty((NCHIP, rows, cols), bf16), jnp.zeros((1, rows, cols), bf16),
                                        (chip, 0, 0))

    gw_out = _matmul(ycat, dy, name="out_proj_dw", out_dtype=bf16, ta=True).reshape(NDEV, R, D)
    po_send, po_recv, gw_out, pair_o, po_token = _pair_start("g_out_pair_start", gw_out)
    dycat = _matmul(dy, wout_full, name="out_proj_dx", out_dtype=f32, tb=True, dep=po_token)
    gw_out, pair_o = _pair_wait("g_out_pair_wait", gw_out, pair_o, po_send, po_recv, after=[dycat])
    sum_o = _pair_sum("g_out_pair_sum", gw_out, pair_o, core)
    co_send, co_recv, sum_o, land_go, co_token = _chip_start(
        "g_out_chip_start", sum_o, landing(R, D), 0)
    dpc, conv_sums = _conv_bwd(proj, dycat, conv_w_full, conv_b, g_conv, co_token)
    gw_c = _matmul(h, dpc, name="in_proj_dw_conv", out_dtype=bf16, ta=True, out_slots=4)
    pc_send, pc_recv, gw_c, pair_c, pc_token = _pair_start("g_conv_pair_start", gw_c)
    do, dpa, attn_sums = _attn_post_bwd(o, proj, dycat, g_attn, pc_token)
    gw_c, pair_c = _pair_wait("g_conv_pair_wait", gw_c, pair_c, pc_send, pc_recv, after=[do])
    sum_c = _pair_sum("g_conv_pair_sum", gw_c, pair_c, core)
    cc_send, cc_recv, sum_c, land_gi, cc_token = _chip_start(
        "g_conv_chip_start", sum_c, landing(D, C), 0)
    dpa = _attn_bwd(proj, o, do, lse, slopes, dpa, cc_token)
    gw_a = _matmul(h, dpa, name="in_proj_dw_attn", out_dtype=bf16, ta=True, b_slots=True, out_slots=4)
    pa_send, pa_recv, gw_a, pair_a, pa_token = _pair_start("g_attn_pair_start", gw_a)
    sum_o, land_go = _chip_wait("g_out_chip_wait", sum_o, land_go, co_send, co_recv, 0, after=[pa_token])
    pick_out = jnp.stack([jnp.int32(0), me // 2]).astype(jnp.int32)
    g_w_out, d_w_out, nm_w_out, nv_w_out = _adamw_sharded(
        "adamw_w_out", land_go, sum_o, sum_o, pick_out, w_out2, m_w_out[0], v_w_out[0])
    gw_a, pair_a = _pair_wait("g_attn_pair_wait", gw_a, pair_a, pa_send, pa_recv, after=[g_w_out])
    sum_a = _pair_sum("g_attn_pair_sum", gw_a, pair_a, core)
    part_a, part_b = (0, 3 * D // 4), (3 * D // 4, D // 4)
    ca_send, ca_recv, sum_a, land_gi, ca_token = _chip_start("g_attn_chip_start_a", sum_a, land_gi, 4, part_a)
    dh = _matmul_slabs_t(dpc, dpa, win_g, name="in_proj_dx", dep=ca_token)
    grad_x, pre_sums = _prenorm_bwd(dh, x2, dout, scale, g_pre)

    small = jnp.concatenate([pre_sums[0:1], pre_sums[1:2], post_sums[0:1],
                             pre_sums[2:3], post_sums[1:2],
                             conv_sums[2:3], conv_sums[3:4], conv_sums[4:5],
                             conv_sums[1:2], conv_sums[0:1], attn_sums[0:1]], axis=1)
    small = jnp.concatenate([small.reshape(8 * D // 128, 128), jnp.broadcast_to(post_sums[2:3, :128], (8, 128))])
    (small_all,) = _all_gather([small], "gather_small")
    cb_send, cb_recv, sum_a, land_gi, cb_token = _chip_start("g_attn_chip_start_b", sum_a, land_gi, 4, part_b,
                                                             after=[small_all])
    small_all = small_all.reshape(NDEV, small.size)
    tot = _sum_rows(small_all, cb_token)
    loss = tot[0, 8 * D]
    g_conv_w = lax.dynamic_slice_in_dim(tot[0:1, 5 * D:5 * D + 3 * C].reshape(1, 3, C), me * CW, CW, axis=2)
    ((g_b_ada, d_b_ada, nm_b_ada, nv_b_ada), (g_g_pre, d_g_pre, nm_g_pre, nv_g_pre),
     (g_g_post, d_g_post, nm_g_post, nv_g_post), (g_conv_w, d_conv_w, nm_conv_w, nv_conv_w),
     (g_conv_b, d_conv_b, nm_conv_b, nv_conv_b), (g_g_conv, d_g_conv, nm_g_conv, nv_g_conv),
     (g_g_attn, d_g_attn, nm_g_attn, nv_g_attn)) = _adamw_small(tot, [
         (b_ada, m_b_ada, v_b_ada, 0), (g_pre, m_g_pre, v_g_pre, 3 * D), (g_post, m_g_post, v_g_post, 4 * D),
         (conv_w, m_conv_w, v_conv_w, g_conv_w), (conv_b, m_conv_b, v_conv_b, 5 * D + 3 * C),
         (g_conv, m_g_conv, v_g_conv, 5 * D + 4 * C), (g_attn, m_g_attn, v_g_attn, 5 * D + 5 * C)])

    dmod_cols = lax.dynamic_slice_in_dim(small_all[:, :3 * D], me * W, W, axis=1)
    g_w_ada, d_w_ada, nm_w_ada, nv_w_ada = _adamw_ada(c_all.T, dmod_cols, w_ada2, m_w_ada[0], v_w_ada[0], cb_token)

    pick_in = jnp.stack([me // 4, (me % 4) // 2]).astype(jnp.int32)
    sum_c, land_gi = _chip_wait("g_conv_chip_wait", sum_c, land_gi, cc_send, cc_recv, 0, after=[g_w_ada])
    sum_a, land_gi = _chip_wait("g_attn_chip_wait_a", sum_a, land_gi, ca_send, ca_recv, 4, [g_w_ada], part_a)
    first = _adamw_sharded("adamw_w_in_a", land_gi, sum_c, sum_a, pick_in, w_in2, m_w_in[0], v_w_in[0], rows=part_a,
                           tr=256)
    sum_a, land_gi = _chip_wait("g_attn_chip_wait_b", sum_a, land_gi, cb_send, cb_recv, 4, [first[0]], part_b)
    g_w_in, d_w_in, nm_w_in, nv_w_in = _adamw_sharded(
        "adamw_w_in_b", land_gi, sum_c, sum_a, pick_in, w_in2, m_w_in[0], v_w_in[0], rows=part_b, prev=first, tr=256)

    return (loss, grad_x[None],
            g_w_ada[None], g_b_ada, g_g_pre, g_w_in[None], g_conv_w, g_conv_b, g_g_conv, g_g_attn, g_w_out[None], g_g_post,
            d_w_ada[None], d_b_ada, d_g_pre, d_w_in[None], d_conv_w, d_conv_b, d_g_conv, d_g_attn, d_w_out[None], d_g_post,
            nm_w_ada[None], nm_b_ada, nm_g_pre, nm_w_in[None], nm_conv_w, nm_conv_b, nm_g_conv, nm_g_attn, nm_w_out[None], nm_g_post,
            nv_w_ada[None], nv_b_ada, nv_g_pre, nv_w_in[None], nv_conv_w, nv_conv_b, nv_g_conv, nv_g_attn, nv_w_out[None], nv_g_post)
```

```python
import functools
import math

import jax
import jax.numpy as jnp
from jax import lax
from jax.experimental import pallas as pl
from jax.experimental.pallas import tpu as pltpu

f32 = jnp.float32
bf16 = jnp.bfloat16

NDEV = 8
HEAD_DIM = 64
PAIR = 2 * HEAD_DIM
BRANCHES = ((128, 1, 1), (512, 4, 1), (2048, 16, 2))
HALF_WIN = 64
EPS = 1e-6
NEG_INF = -1e30
ADAM_LR, ADAM_B1, ADAM_B2, ADAM_EPS, ADAM_WD, ADAM_STEP = 0.001, 0.9, 0.999, 1e-08, 0.01, 10
MESH = pl.DeviceIdType.MESH
VMEM_LIMIT = 56 * 1024 * 1024
HBM_SPEC = pl.BlockSpec(memory_space=pltpu.HBM)
ANY_SPEC = pl.BlockSpec(memory_space=pl.ANY)
SEM_SPEC = pl.BlockSpec(memory_space=pltpu.SEMAPHORE)


def _params(*sem):
    return pltpu.CompilerParams(dimension_semantics=sem or None, vmem_limit_bytes=VMEM_LIMIT)


def _silu(z):
    return z * jax.nn.sigmoid(z)


def _silu_and_slope(z):
    s = jax.nn.sigmoid(z)
    return z * s, s * (1.0 + z * (1.0 - s))


def _my_place():
    x, y, c = lax.axis_index("x"), lax.axis_index("y"), lax.axis_index("c")
    return x, y, c, 4 * x + 2 * y + c


def _peer(x, y, c, k):
    px, py, pc = x ^ (k >> 2 & 1), y ^ (k >> 1 & 1), c ^ (k & 1)
    return (px, py, pc), 4 * px + 2 * py + pc


def _all_gather(arrays, name):
    n = len(arrays)

    def body(*refs):
        srcs, dsts = refs[:n], refs[n:2 * n]
        send_sems, recv_sems, local_sems = refs[2 * n:]
        x, y, c, me = _my_place()
        locals_, sends = [], []
        for t in range(n):
            own = pltpu.make_async_copy(srcs[t], dsts[t].at[me], local_sems.at[t])
            own.start()
            locals_.append(own)
            for k in range(1, NDEV):
                peer, pidx = _peer(x, y, c, k)
                cp = pltpu.make_async_remote_copy(
                    src_ref=srcs[t], dst_ref=dsts[t].at[me], send_sem=send_sems.at[t, k],
                    recv_sem=recv_sems.at[t, k], device_id=peer, device_id_type=MESH)
                cp.start()
                sends.append(cp)
        for t in range(n):
            for k in range(1, NDEV):
                peer, pidx = _peer(x, y, c, k)
                pltpu.make_async_remote_copy(
                    src_ref=srcs[t], dst_ref=dsts[t].at[pidx], send_sem=send_sems.at[t, k],
                    recv_sem=recv_sems.at[t, k], device_id=peer, device_id_type=MESH).wait_recv()
        for cp in sends:
            cp.wait_send()
        for cp in locals_:
            cp.wait()

    return pl.pallas_call(
        body, name=name,
        out_shape=tuple(jax.ShapeDtypeStruct((NDEV,) + a.shape, a.dtype) for a in arrays),
        in_specs=[HBM_SPEC] * n, out_specs=tuple([HBM_SPEC] * n),
        scratch_shapes=[pltpu.SemaphoreType.DMA((n, NDEV)), pltpu.SemaphoreType.DMA((n, NDEV)),
                        pltpu.SemaphoreType.DMA((n,))],
    )(*arrays)


def _comm_call(name, arrays, sems, new_sems, body, after=(), token=False):
    na, ns, nn, nf = len(arrays), len(sems), len(new_sems), len(after)

    def kern(*refs):
        ins, outs = refs[:na + ns + nf], refs[na + ns + nf:]
        body(ins[:na], ins[na:na + ns], outs[:nn])
        if token:
            outs[nn + na][...] = jnp.zeros((8, 128), f32)

    out_shape = ([pltpu.SemaphoreType.DMA(s) for s in new_sems] + [pltpu.HBM(a.shape, a.dtype) for a in arrays]
                 + ([jax.ShapeDtypeStruct((8, 128), f32)] if token else []))
    out_specs = [SEM_SPEC] * nn + [HBM_SPEC] * na + ([pl.BlockSpec(memory_space=pltpu.VMEM)] if token else [])
    res = pl.pallas_call(
        kern, name=name, out_shape=tuple(out_shape),
        in_specs=[HBM_SPEC] * na + [SEM_SPEC] * ns + [ANY_SPEC] * nf, out_specs=tuple(out_specs),
        input_output_aliases={t: nn + t for t in range(na)},
        compiler_params=pltpu.CompilerParams(has_side_effects=pltpu.SideEffectType.DATAFLOW_SIDE_EFFECTING),
    )(*[pltpu.with_memory_space_constraint(a, pltpu.HBM) for a in arrays], *sems, *after)
    return list(res[:nn]), list(res[nn:nn + na]), (res[nn + na] if token else None)


def _remote(src, dst, send_sem, recv_sem, device):
    return pltpu.make_async_remote_copy(src_ref=src, dst_ref=dst, send_sem=send_sem, recv_sem=recv_sem,
                                        device_id=device, device_id_type=MESH)


SAME_CORE = (2, 4, 6)
VIA_SIBLING = (3, 5, 7)


def _weights_forward(name, land, recv, after):
    def body(a, s, new):
        (land,), (recv,), (fsend, frecv) = a, s, new
        x, y, c, me = _my_place()
        sibling, _ = _peer(x, y, c, 1)
        for k in SAME_CORE:
            peer, slot = _peer(x, y, c, k)
            _remote(land.at[slot], land.at[slot], fsend.at[k], recv.at[k], peer).wait_recv()
            _remote(land.at[slot], land.at[slot], fsend.at[k], frecv.at[k ^ 1], sibling).start()

    return _comm_call(name, [land], [recv], [(NDEV,), (NDEV,)], body, after=after)


def _weights_wait(name, land, send, recv, fsend, frecv, after):
    def body(a, s, new):
        (land,), (send, recv, fsend, frecv) = a, s
        x, y, c, me = _my_place()
        sibling, sib_slot = _peer(x, y, c, 1)
        _remote(land.at[sib_slot], land.at[sib_slot], send.at[1], recv.at[1], sibling).wait_recv()
        for k in VIA_SIBLING:
            _, slot = _peer(x, y, c, k)
            _remote(land.at[slot], land.at[slot], fsend.at[k ^ 1], frecv.at[k], sibling).wait_recv()
        for k in (1,) + SAME_CORE:
            peer, _ = _peer(x, y, c, k)
            _remote(land.at[me], land.at[me], send.at[k], recv.at[k], peer).wait_send()
        for k in SAME_CORE:
            _, slot = _peer(x, y, c, k)
            _remote(land.at[slot], land.at[slot], fsend.at[k], frecv.at[k ^ 1], sibling).wait_send()

    return _comm_call(name, [land], [send, recv, fsend, frecv], [], body, after=after)[1][0]


def _diag_relay(x, y, c):
    slot = 4 * (x ^ (1 - c)) + 2 * (y ^ c) + c
    return slot, (x ^ c, y ^ (1 - c), c)


def _w_in_start(land, after):
    def body(a, s, new):
        (land,), (send, recv) = a, new
        x, y, c, me = _my_place()
        for k in (1, 2, 4):
            peer, _ = _peer(x, y, c, k)
            _remote(land.at[me], land.at[me], send.at[k], recv.at[k], peer).start()

    (send, recv), (land,), token = _comm_call("w_in_start", [land], [], [(NDEV,), (NDEV,)], body, after=after, token=True)
    return send, recv, land, token


def _w_in_sibling(land, recv, after):
    def body(a, s, new):
        (land,), (recv,) = a, s
        x, y, c, me = _my_place()
        sibling, slot = _peer(x, y, c, 1)
        _remote(land.at[slot], land.at[slot], recv.at[1], recv.at[1], sibling).wait_recv()

    return _comm_call("w_in_sibling", [land], [recv], [], body, after=after)[1][0]


def _w_in_relay(land, recv, after):
    def body(a, s, new):
        (land,), (recv,), (fsend, frecv) = a, s, new
        x, y, c, me = _my_place()
        sibling, _ = _peer(x, y, c, 1)
        for k in (2, 4):
            peer, slot = _peer(x, y, c, k)
            _remote(land.at[slot], land.at[slot], fsend.at[k], recv.at[k], peer).wait_recv()
        slot, target = _diag_relay(x, y, c)
        _remote(land.at[slot], land.at[slot], fsend.at[6], frecv.at[6], target).start()
        for k in (2, 4):
            _, slot = _peer(x, y, c, k)
            _remote(land.at[slot], land.at[slot], fsend.at[k], frecv.at[k ^ 1], sibling).start()

    (fsend, frecv), (land,), _ = _comm_call("w_in_relay", [land], [recv], [(NDEV,), (NDEV,)], body, after=after)
    return fsend, frecv, land


def _w_in_forwarded(land, frecv, after):
    def body(a, s, new):
        (land,), (frecv,) = a, s
        x, y, c, me = _my_place()
        sibling, _ = _peer(x, y, c, 1)
        for k in (3, 5):
            _, slot = _peer(x, y, c, k)
            _remote(land.at[slot], land.at[slot], frecv.at[k], frecv.at[k], sibling).wait_recv()

    return _comm_call("w_in_forwarded", [land], [frecv], [], body, after=after)[1][0]


def _w_in_diag(land, land_o, frecv, after):
    def body(a, s, new):
        (land, land_o), (frecv,), (dsend, drecv, osend, orecv) = a, s, new
        x, y, c, me = _my_place()
        sibling, _ = _peer(x, y, c, 1)
        peer, slot = _peer(x, y, c, 6)
        _remote(land.at[slot], land.at[slot], dsend.at[6], frecv.at[6], peer).wait_recv()
        _remote(land.at[slot], land.at[slot], dsend.at[6], drecv.at[7], sibling).start()
        for k in (1,) + SAME_CORE:
            peer, _ = _peer(x, y, c, k)
            _remote(land_o.at[me], land_o.at[me], osend.at[k], orecv.at[k], peer).start()

    sems, (land, land_o), _ = _comm_call("w_in_diag", [land, land_o], [frecv], [(NDEV,)] * 4, body, after=after)
    return sems, land, land_o


def _w_in_finish(land, send, fsend, dsend, drecv, after):
    def body(a, s, new):
        (land,), (send, fsend, dsend, drecv) = a, s
        x, y, c, me = _my_place()
        sibling, _ = _peer(x, y, c, 1)
        _, slot = _peer(x, y, c, 7)
        _remote(land.at[slot], land.at[slot], dsend.at[6], drecv.at[7], sibling).wait_recv()
        for k in (1, 2, 4):
            peer, _ = _peer(x, y, c, k)
            _remote(land.at[me], land.at[me], send.at[k], send.at[k], peer).wait_send()
        for k in (2, 4, 6):
            _, slot = _peer(x, y, c, k)
            _remote(land.at[slot], land.at[slot], fsend.at[k], fsend.at[k], sibling).wait_send()
        _, slot = _peer(x, y, c, 6)
        _remote(land.at[slot], land.at[slot], dsend.at[6], dsend.at[6], sibling).wait_send()

    return _comm_call("w_in_finish", [land], [send, fsend, dsend, drecv], [], body, after=after)[1][0]


def _in_proj_part(name, h, land, proj, me_arr, k0, kstep, nk, tm=512):
    S, D = h.shape
    C = land.shape[2]
    tm = min(tm, S)

    def body(me_ref, a_ref, b_ref, *rest):
        rest[-1][...] = jnp.dot(a_ref[...], b_ref[...], preferred_element_type=f32)

    slot = lambda j, me: me[0] ^ (k0 + kstep * j)
    args = [h, land] + ([] if proj is None else [proj])
    grid_spec = pltpu.PrefetchScalarGridSpec(
        num_scalar_prefetch=1, grid=(nk, S // tm),
        in_specs=[pl.BlockSpec((tm, D), lambda j, i, me: (i, 0)),
                  pl.BlockSpec((None, D, C), lambda j, i, me: (slot(j, me), 0, 0))] + [ANY_SPEC] * (len(args) - 2),
        out_specs=pl.BlockSpec((tm, C), lambda j, i, me: (i, slot(j, me))))
    return pl.pallas_call(
        body, name=name, out_shape=jax.ShapeDtypeStruct((S, NDEV * C), f32), grid_spec=grid_spec,
        input_output_aliases={} if proj is None else {3: 0}, compiler_params=_params("arbitrary", "arbitrary"),
    )(me_arr, *args)


NCHIP = NDEV // 2


def _pair_start(name, src):
    npair = src.shape[0] // 2

    def body(a, s, new):
        (src, pair), (send, recv) = a, new
        x, y, c, me = _my_place()
        sibling, _ = _peer(x, y, c, 1)
        for i in range(npair):
            _remote(src.at[2 * i + 1 - c], pair.at[i], send.at[i], recv.at[i], sibling).start()

    pair = lax.empty((npair,) + src.shape[1:], src.dtype)
    (send, recv), (src, pair), token = _comm_call(name, [src, pair], [], [(npair,), (npair,)], body, token=True)
    return send, recv, src, pair, token


def _pair_wait(name, src, pair, send, recv, after):
    npair = pair.shape[0]

    def body(a, s, new):
        (src, pair), (send, recv) = a, s
        x, y, c, me = _my_place()
        sibling, _ = _peer(x, y, c, 1)
        for i in range(npair):
            cp = _remote(src.at[2 * i + 1 - c], pair.at[i], send.at[i], recv.at[i], sibling)
            cp.wait_recv()
            cp.wait_send()

    return _comm_call(name, [src, pair], [send, recv], [], body, after=after)[1]


def _pair_sum(name, src, pair, core, tr=1024):
    npair, R, Cc = pair.shape
    tr = min(tr, R)

    def body(core_ref, a_ref, b_ref, o_ref):
        o_ref[...] = (a_ref[...].astype(f32) + b_ref[...].astype(f32)).astype(o_ref.dtype)

    grid_spec = pltpu.PrefetchScalarGridSpec(
        num_scalar_prefetch=1, grid=(npair, R // tr),
        in_specs=[pl.BlockSpec((None, tr, Cc), lambda i, r, core: (2 * i + core[0], r, 0)),
                  pl.BlockSpec((None, tr, Cc), lambda i, r, core: (i, r, 0))],
        out_specs=pl.BlockSpec((None, tr, Cc), lambda i, r, core: (i, r, 0)))
    return pl.pallas_call(body, name=name, out_shape=jax.ShapeDtypeStruct(pair.shape, pair.dtype),
                          grid_spec=grid_spec, compiler_params=_params("parallel", "parallel"))(core, src, pair)


def _owner_chip(first, i):
    q = first // 2 + i
    return q >> 1 & 1, q & 1


def _chip_start(name, sums, land, first, rows=None, after=()):
    npair = sums.shape[0]
    rows = pl.ds(*(rows or (0, sums.shape[1])))

    def body(a, s, new):
        (sums, land), (send, recv) = a, new
        x, y, c, me = _my_place()
        for i in range(npair):
            ox, oy = _owner_chip(first, i)

            @pl.when((x != ox) | (y != oy))
            def _():
                _remote(sums.at[i, rows], land.at[2 * x + y, rows], send.at[i], recv.at[2 * x + y], (ox, oy, c)).start()

    (send, recv), (sums, land), token = _comm_call(name, [sums, land], [], [(npair,), (NCHIP,)], body, after=after,
                                                   token=True)
    return send, recv, sums, land, token


def _chip_wait(name, sums, land, send, recv, first, after, rows=None):
    npair = sums.shape[0]
    rows = pl.ds(*(rows or (0, sums.shape[1])))

    def body(a, s, new):
        (sums, land), (send, recv) = a, s
        x, y, c, me = _my_place()
        mine = (me >= first) & (me < first + 2 * npair)
        for i in range(npair):
            ox, oy = _owner_chip(first, i)

            @pl.when((x != ox) | (y != oy))
            def _():
                _remote(sums.at[i, rows], land.at[2 * x + y, rows], send.at[i], recv.at[2 * x + y], (ox, oy, c)).wait_send()
        for q in range(NCHIP):
            @pl.when(mine & (2 * x + y != q))
            def _():
                _remote(sums.at[0, rows], land.at[q, rows], send.at[0], recv.at[q], (q >> 1, q & 1, c)).wait_recv()

    return _comm_call(name, [sums, land], [send, recv], [], body, after=after)[1]


def _matmul(a, b, *, name, out_dtype, ta=False, tb=False, b_slots=False, out_slots=0, b_cols=None,
            tm=1024, tn=1024, tk=2048, dep=None):
    M, K = (a.shape[1], a.shape[0]) if ta else a.shape
    col0 = 0
    if b_slots:
        slab = b.shape[2]
        N = b.shape[1] if tb else b.shape[0] * slab
        assert (K if tb else N) == b.shape[0] * slab
    elif b_cols is not None:
        assert not tb
        col0, N = b_cols
    else:
        N = b.shape[0] if tb else b.shape[1]
    tm, tn, tk = min(tm, M), min(tn, N), min(tk, K)
    if b_slots:
        if tb:
            tk = min(tk, slab)
        else:
            tn = min(tn, slab)
    if out_slots:
        tn = min(tn, N // out_slots)
    nm, nn, nk = M // tm, N // tn, K // tk
    assert (nm * tm, nn * tn, nk * tk) == (M, N, K) and col0 % tn == 0, (name, M, N, K, tm, tn, tk)
    j0 = col0 // tn

    a_spec = pl.BlockSpec((tk, tm), lambda i, j, k: (k, i)) if ta else pl.BlockSpec((tm, tk), lambda i, j, k: (i, k))
    if b_slots and tb:
        per = slab // tk
        b_spec = pl.BlockSpec((None, tn, tk), lambda i, j, k: (k // per, j, k % per))
    elif b_slots:
        per = slab // tn
        b_spec = pl.BlockSpec((None, tk, tn), lambda i, j, k: (j // per, k, j % per))
    elif tb:
        b_spec = pl.BlockSpec((tn, tk), lambda i, j, k: (j, k))
    else:
        b_spec = pl.BlockSpec((tk, tn), lambda i, j, k: (k, j + j0))
    if out_slots:
        per_o = (N // out_slots) // tn
        o_spec = pl.BlockSpec((None, tm, tn), lambda i, j, k: (j // per_o, i, j % per_o))
        out_shape = jax.ShapeDtypeStruct((out_slots, M, N // out_slots), out_dtype)
    else:
        o_spec = pl.BlockSpec((tm, tn), lambda i, j, k: (i, j))
        out_shape = jax.ShapeDtypeStruct((M, N), out_dtype)
    dims = (((0 if ta else 1,), (1 if tb else 0,)), ((), ()))
    deps = [] if dep is None else [dep]

    def body(a_ref, b_ref, *rest):
        o_ref = rest[len(deps)]
        prod = lax.dot_general(a_ref[...], b_ref[...], dims, preferred_element_type=f32)
        if nk == 1:
            o_ref[...] = prod.astype(out_dtype)
            return
        acc_ref = rest[len(deps) + 1]
        k = pl.program_id(2)

        @pl.when(k == 0)
        def _():
            acc_ref[...] = prod

        @pl.when((k > 0) & (k < nk - 1))
        def _():
            acc_ref[...] += prod

        @pl.when(k == nk - 1)
        def _():
            o_ref[...] = (acc_ref[...] + prod).astype(out_dtype)

    return pl.pallas_call(
        body, name=name, out_shape=out_shape, grid=(nm, nn, nk),
        in_specs=[a_spec, b_spec] + [ANY_SPEC] * len(deps), out_specs=o_spec,
        scratch_shapes=[pltpu.VMEM((tm, tn), f32)] if nk > 1 else [],
        compiler_params=_params("parallel", "parallel", "arbitrary"),
    )(a, b, *deps)


def _matmul_slabs_t(a_cols, a_slots, b, *, name, tm=512, tn=512, dep=None):
    M = a_cols.shape[0]
    n_slab, N, slab = b.shape
    n1, n2 = a_cols.shape[1] // slab, a_slots.shape[0]
    assert n1 + n2 == n_slab and a_slots.shape[1:] == (M, slab)
    tm, tn = min(tm, M), min(tn, N)
    deps = [] if dep is None else [dep]

    def body(a1_ref, a2_ref, b_ref, *rest):
        o_ref = rest[len(deps)]
        acc = None
        for s in range(n_slab):
            lhs = a1_ref[:, s * slab:(s + 1) * slab] if s < n1 else a2_ref[s - n1]
            prod = lax.dot_general(lhs, b_ref[s], (((1,), (1,)), ((), ())), preferred_element_type=f32)
            acc = prod if acc is None else acc + prod
        o_ref[...] = acc

    return pl.pallas_call(
        body, name=name, out_shape=jax.ShapeDtypeStruct((M, N), f32), grid=(M // tm, N // tn),
        in_specs=[pl.BlockSpec((tm, n1 * slab), lambda i, j: (i, 0)), pl.BlockSpec((n2, tm, slab), lambda i, j: (0, i, 0)),
                  pl.BlockSpec((n_slab, tn, slab), lambda i, j: (0, j, 0))] + [ANY_SPEC] * len(deps),
        out_specs=pl.BlockSpec((tm, tn), lambda i, j: (i, j)), compiler_params=_params("parallel", "parallel"),
    )(a_cols, a_slots, b, *deps)


def _ada_exchange(c_blk, cw_slab, w_ada, b_cols):
    nblk = c_blk.shape[0]
    D, W = w_ada.shape
    CW = cw_slab.shape[1]

    def body(c_ref, cw_ref, w_ref, b_ref, mod_ref, call_ref, cwg_ref, msend, send_sems, recv_sems):
        x, y, c, me = _my_place()
        call_ref[me] = _silu(c_ref[...])
        cwg_ref[me] = cw_ref[...]
        first = []
        for k in range(1, NDEV):
            peer, _ = _peer(x, y, c, k)
            first.append(_remote(call_ref.at[me], call_ref.at[me], send_sems.at[0, k], recv_sems.at[0, k], peer))
            first.append(_remote(cwg_ref.at[me], cwg_ref.at[me], send_sems.at[1, k], recv_sems.at[1, k], peer))
        for cp in first:
            cp.start()
        for k in range(1, NDEV):
            peer, slot = _peer(x, y, c, k)
            _remote(call_ref.at[slot], call_ref.at[slot], send_sems.at[0, k], recv_sems.at[0, k], peer).wait_recv()
            _remote(cwg_ref.at[slot], cwg_ref.at[slot], send_sems.at[1, k], recv_sems.at[1, k], peer).wait_recv()
        mod = jnp.broadcast_to(b_ref[...], (NDEV, W))
        for r in range(nblk):
            mod = mod + lax.dot_general(call_ref[:, r, :], w_ref[r * 128:(r + 1) * 128, :], (((1,), (0,)), ((), ())),
                                        preferred_element_type=f32, precision=lax.Precision.HIGHEST)
        row = lax.broadcasted_iota(jnp.int32, (NDEV, 1), 0)
        pick = lambda j: jnp.broadcast_to(jnp.sum(jnp.where(row == j, mod, 0.0), axis=0, keepdims=True), (8, W))
        mod_ref[me] = pick(me)
        second = []
        for k in range(1, NDEV):
            peer, slot = _peer(x, y, c, k)
            msend[k] = pick(slot)
            second.append(_remote(msend.at[k], mod_ref.at[me], send_sems.at[2, k], recv_sems.at[2, k], peer))
        for cp in second:
            cp.start()
        for k in range(1, NDEV):
            peer, slot = _peer(x, y, c, k)
            _remote(msend.at[k], mod_ref.at[slot], send_sems.at[2, k], recv_sems.at[2, k], peer).wait_recv()
        for cp in first + second:
            cp.wait_send()

    vmem = pl.BlockSpec(memory_space=pltpu.VMEM)
    return pl.pallas_call(
        body, name="ada_exchange",
        out_shape=(jax.ShapeDtypeStruct((NDEV, 8, W), f32), jax.ShapeDtypeStruct((NDEV, nblk, 128), f32),
                   jax.ShapeDtypeStruct((NDEV, 8, CW), f32)),
        in_specs=[vmem] * 4, out_specs=(vmem, vmem, vmem),
        scratch_shapes=[pltpu.VMEM((NDEV, 8, W), f32), pltpu.SemaphoreType.DMA((3, NDEV)),
                        pltpu.SemaphoreType.DMA((3, NDEV))],
        compiler_params=_params(),
    )(c_blk, cw_slab, w_ada, b_cols)


def _prenorm(x, scale, shift, g_pre, dep, tr=512):
    S, D = x.shape
    tr = min(tr, S)

    def body(x_ref, sc_ref, sh_ref, g_ref, dep_ref, h_ref):
        xv = x_ref[...]
        r = lax.rsqrt(jnp.mean(xv * xv, axis=-1, keepdims=True) + EPS)
        h_ref[...] = ((xv * r) * g_ref[...] * (1.0 + sc_ref[...]) + sh_ref[...]).astype(bf16)

    row = pl.BlockSpec((tr, D), lambda i: (i, 0))
    vec = pl.BlockSpec((1, D), lambda i: (0, 0))
    return pl.pallas_call(body, name="prenorm", out_shape=jax.ShapeDtypeStruct((S, D), bf16), grid=(S // tr,),
                          in_specs=[row, vec, vec, vec, ANY_SPEC], out_specs=row, compiler_params=_params("parallel"))(
                              x, scale, shift, g_pre, dep)


def _ext_rows(i, tr, S):
    g = lax.broadcasted_iota(jnp.int32, (tr + 16, 1), 0) + (i * tr - 8)
    return (g >= 0) & (g < S)


def _halo_specs(tr, S, C, col):
    nb8 = S // 8
    main = pl.BlockSpec((tr, C), lambda i: (i, col))
    prev = pl.BlockSpec((8, C), lambda i: (jnp.maximum(i * (tr // 8) - 1, 0), col))
    nxt = pl.BlockSpec((8, C), lambda i: (jnp.minimum((i + 1) * (tr // 8), nb8 - 1), col))
    return prev, main, nxt


def _conv_fwd(proj, conv_w, conv_b, g_conv, tr=512):
    S, C = proj.shape[0], proj.shape[1] // 8
    tr = min(tr, S)

    def body(up, um, un, cp, cm, cn, bg_ref, zc_ref, w_ref, cb_ref, g_ref, o_ref):
        i = pl.program_id(0)
        exists = _ext_rows(i, tr, S)
        u = jnp.concatenate([up[...], um[...], un[...]], axis=0)
        cg = jnp.concatenate([cp[...], cm[...], cn[...]], axis=0)
        t = jnp.where(exists, cg * u, 0.0)
        t_before = pltpu.roll(t, 1, 0)[8:tr + 8]
        t_after = pltpu.roll(t, tr + 15, 0)[8:tr + 8]
        w = w_ref[...]
        cv = w[0:1] * t_before + w[1:2] * t[8:tr + 8] + w[2:3] * t_after + cb_ref[...]
        yc = bg_ref[...] * cv
        rc = lax.rsqrt(jnp.mean(yc * yc, axis=-1, keepdims=True) + EPS)
        o_ref[...] = ((yc * rc) * g_ref[...] * _silu(zc_ref[...])).astype(bf16)

    u_specs = _halo_specs(tr, S, C, 0)
    c_specs = _halo_specs(tr, S, C, 2)
    vec = pl.BlockSpec((1, C), lambda i: (0, 0))
    return pl.pallas_call(
        body, name="conv_fwd", out_shape=jax.ShapeDtypeStruct((S, 2 * C), bf16), grid=(S // tr,),
        in_specs=[*u_specs, *c_specs, pl.BlockSpec((tr, C), lambda i: (i, 1)), pl.BlockSpec((tr, C), lambda i: (i, 3)),
                  pl.BlockSpec((8, C), lambda i: (0, 0)), vec, vec],
        out_specs=pl.BlockSpec((tr, C), lambda i: (i, 0)), compiler_params=_params("parallel"),
    )(proj, proj, proj, proj, proj, proj, proj, proj, conv_w, conv_b, g_conv)


def _branch_geometry(S, r, inter):
    L = S // r * inter
    nq = min(128, L)
    nk = min(nq + 2 * HALF_WIN * inter, L)
    assert L % nq == 0 and (L == nk or L >= nq + 2 * HALF_WIN * inter)
    return L, nq, nk, L // nq


QUAD = 4


def _to_quad(dst, src, S):
    n = S // QUAD
    for rho in range(QUAD):
        dst[pl.ds(rho * n, n), :] = src[pl.ds(rho, n, stride=QUAD), :]


def _block_rows(idx, r, inter, S, L, nq, nk, nblk):
    rho, qb = (0, idx) if r == 1 else (idx // nblk, idx % nblk)
    i0 = qb * nq
    ws = jnp.clip(i0 - HALF_WIN * inter, 0, L - nk)
    if r == 1:
        return pl.ds(pl.multiple_of(i0, 8), nq), pl.ds(pl.multiple_of(ws, 8), nk), i0 - ws
    assert r % (QUAD * inter) == 0
    step = r // QUAD // inter
    base = (rho % QUAD) * (S // QUAD) + rho // QUAD
    if step == 1:
        return pl.ds(pl.multiple_of(base + i0, 8), nq), pl.ds(pl.multiple_of(base + ws, 8), nk), i0 - ws
    return pl.ds(base + step * i0, nq, stride=step), pl.ds(base + step * ws, nk, stride=step), i0 - ws


N_CASES = 3
SCALE = HEAD_DIM ** -0.5
ATTN_UNROLL = 16


def _bias_shape(S):
    shapes = [_branch_geometry(S, r, inter)[1:3] for _, r, inter in BRANCHES]
    return (len(BRANCHES) * N_CASES * 2, max(nq for nq, _ in shapes), max(nk for _, nk in shapes))


def _bias_index(b, case, head):
    return (b * N_CASES + case) * 2 + head


def _fill_bias(bias_scr, sl_ref, S):
    sl = sl_ref[...]
    slope = (sl[0:1, 0:1], sl[0:1, HEAD_DIM:HEAD_DIM + 1])
    for b, (_, r, inter) in enumerate(BRANCHES):
        L, nq, nk, nblk = _branch_geometry(S, r, inter)
        rel = lax.broadcasted_iota(jnp.int32, (nq, nk), 0) - lax.broadcasted_iota(jnp.int32, (nq, nk), 1)
        for case in range(N_CASES):
            d = jnp.abs(rel + case * HALF_WIN)
            valid = d <= HALF_WIN * inter
            if inter > 1:
                valid = valid & (jnp.bitwise_and(d, inter - 1) == 0)
            dist = d.astype(f32) * float(r // inter)
            for head in range(2):
                bias_scr[_bias_index(b, case, head), 0:nq, 0:nk] = jnp.where(valid, -slope[head] * dist, NEG_INF)


def _head_slopes(n_heads):
    slopes = 2.0 ** (-8.0 * jnp.arange(1, n_heads + 1, dtype=f32) / n_heads)
    return jnp.broadcast_to(jnp.repeat(slopes.reshape(n_heads // 2, 2), HEAD_DIM, axis=1)[:, None, :],
                            (n_heads // 2, 8, PAIR))


def _attn_fwd(proj, slopes):
    S, C = proj.shape[0], proj.shape[1] // 8
    npair = C // PAIR

    def body(q_ref, k_ref, v_ref, sl_ref, o_ref, lse_ref, m_scr, l_scr, a_scr, bias_scr, q4_scr, k4_scr, v4_scr):
        lane = lax.broadcasted_iota(jnp.int32, (1, PAIR), 1)
        first = lane < HEAD_DIM
        _fill_bias(bias_scr, sl_ref, S)
        for dst, src in ((q4_scr, q_ref), (k4_scr, k_ref), (v4_scr, v_ref)):
            _to_quad(dst, src, S)

        for b, (_, r, inter) in enumerate(BRANCHES):
            L, nq, nk, nblk = _branch_geometry(S, r, inter)
            qs, ks, vs = (q_ref, k_ref, v_ref) if r == 1 else (q4_scr, k4_scr, v4_scr)

            def step(idx, carry, b=b, r=r, L=L, nq=nq, nk=nk, nblk=nblk, qs=qs, ks=ks, vs=vs):
                qrows, krows, off = _block_rows(idx, r, inter, S, L, nq, nk, nblk)
                case = off // HALF_WIN
                q2 = qs[qrows, :] * SCALE
                k2 = ks[krows, :].astype(bf16)
                v2 = vs[krows, :].astype(bf16)
                ms, accs = [], []
                for hh in range(2):
                    mine = first if hh == 0 else ~first
                    qh = jnp.where(mine, q2, 0.0).astype(bf16)
                    s = lax.dot_general(qh, k2, (((1,), (1,)), ((), ())), preferred_element_type=f32)
                    s = s + bias_scr[_bias_index(b, case, hh), 0:nq, 0:nk]
                    m = jnp.max(s, axis=-1, keepdims=True)
                    p = jnp.exp(s - m).astype(bf16)
                    vh = jnp.where(mine, v2, jnp.ones_like(v2))
                    ms.append(m)
                    accs.append(jnp.dot(p, vh, preferred_element_type=f32))
                m_scr[b, qrows, :] = jnp.where(first, ms[0], ms[1])
                a_scr[b, qrows, :] = jnp.where(first, accs[0], accs[1])
                l_scr[b, qrows, :] = jnp.where(first, accs[1], accs[0])
                return carry

            lax.fori_loop(0, S // nq, step, 0, unroll=min(ATTN_UNROLL, S // nq))

        n4 = S // QUAD
        ch = min(256, n4)
        nch = n4 // ch

        def merge(i, carry):
            rho, part = i // nch, i % nch
            sorted_rows = pl.ds(pl.multiple_of(rho * n4 + part * ch, 8), ch)
            token_rows = pl.ds(rho + QUAD * part * ch, ch, stride=QUAD)
            rows = (token_rows,) + (sorted_rows,) * (len(BRANCHES) - 1)
            ms = [m_scr[b, rows[b], :] for b in range(len(BRANCHES))]
            m = functools.reduce(jnp.maximum, ms)
            l = jnp.zeros((ch, PAIR), f32)
            acc = jnp.zeros((ch, PAIR), f32)
            for b in range(len(BRANCHES)):
                w = jnp.exp(ms[b] - m)
                l = l + w * pltpu.roll(l_scr[b, rows[b], :], HEAD_DIM, 1)
                acc = acc + w * a_scr[b, rows[b], :]
            o_ref[token_rows, :] = acc / l
            lse_ref[token_rows, :] = m + jnp.log(l)
            return carry

        lax.fori_loop(0, QUAD * nch, merge, 0, unroll=2)

    blk = lambda part: pl.BlockSpec((S, PAIR), lambda p: (0, part * npair + p))
    out = pl.BlockSpec((S, PAIR), lambda p: (0, p))
    return pl.pallas_call(
        body, name="attn_fwd",
        out_shape=(jax.ShapeDtypeStruct((S, C), f32), jax.ShapeDtypeStruct((S, C), f32)), grid=(npair,),
        in_specs=[blk(4), blk(5), blk(6), pl.BlockSpec((None, 8, PAIR), lambda p: (p, 0, 0))],
        out_specs=(out, out),
        scratch_shapes=[pltpu.VMEM((3, S, PAIR), f32)] * 3 + [pltpu.VMEM(_bias_shape(S), f32)]
        + [pltpu.VMEM((S, PAIR), f32)] * 3,
        compiler_params=_params("parallel"),
    )(proj, proj, proj, slopes)


def _attn_post(ycat, o, proj, g_attn, tr=512):
    S, C = o.shape
    tr = min(tr, S)

    def body(y_ref, o_ref, z_ref, g_ref, out_ref):
        del y_ref
        ov = o_ref[...]
        ra = lax.rsqrt(jnp.mean(ov * ov, axis=-1, keepdims=True) + EPS)
        out_ref[...] = ((ov * ra) * g_ref[...] * _silu(z_ref[...])).astype(bf16)

    return pl.pallas_call(
        body, name="attn_post", out_shape=jax.ShapeDtypeStruct(ycat.shape, ycat.dtype), grid=(S // tr,),
        in_specs=[HBM_SPEC, pl.BlockSpec((tr, C), lambda i: (i, 0)), pl.BlockSpec((tr, C), lambda i: (i, 7)),
                  pl.BlockSpec((1, C), lambda i: (0, 0))],
        out_specs=pl.BlockSpec((tr, C), lambda i: (i, 1)), input_output_aliases={0: 0},
        compiler_params=_params("arbitrary"),
    )(ycat, o, proj, g_attn)


def _sandwich(y, x, target, gate, g_post, tr=256):
    S, D = y.shape
    tr = min(tr, S)

    def body(y_ref, x_ref, t_ref, gate_ref, g_ref, dy_ref, dout_ref, sums_ref):
        i = pl.program_id(0)
        gate, g = gate_ref[...], g_ref[...]
        gg = gate * g
        yv = y_ref[...]
        rp = lax.rsqrt(jnp.mean(yv * yv, axis=-1, keepdims=True) + EPS)
        yhat = yv * rp
        err = (x_ref[...] + gg * yhat) - t_ref[...]
        dout = err * (1.0 / D)
        dout_ref[...] = dout
        q = dout * yhat
        w = dout * gg
        dy_ref[...] = (rp * (w - yhat * jnp.sum(q * gg, axis=-1, keepdims=True) * (1.0 / D))).astype(bf16)
        loss = 0.5 * jnp.sum(jnp.mean(err * err, axis=-1, keepdims=True), axis=0, keepdims=True)
        q_sum = jnp.sum(q, axis=0, keepdims=True)
        row = lax.broadcasted_iota(jnp.int32, (8, D), 0)
        upd = jnp.where(row == 0, q_sum * g, jnp.where(row == 1, q_sum * gate, jnp.where(row == 2, loss, 0.0)))

        @pl.when(i == 0)
        def _():
            sums_ref[...] = upd

        @pl.when(i > 0)
        def _():
            sums_ref[...] += upd

    row = pl.BlockSpec((tr, D), lambda i: (i, 0))
    vec = pl.BlockSpec((1, D), lambda i: (0, 0))
    return pl.pallas_call(
        body, name="sandwich",
        out_shape=(jax.ShapeDtypeStruct((S, D), bf16), jax.ShapeDtypeStruct((S, D), f32), jax.ShapeDtypeStruct((8, D), f32)),
        grid=(S // tr,), in_specs=[row, row, row, vec, vec],
        out_specs=(row, row, pl.BlockSpec((8, D), lambda i: (0, 0))), compiler_params=_params("arbitrary"),
    )(y, x, target, gate, g_post)


def _conv_bwd(proj, dycat, conv_w, conv_b, g_conv, dep, tr=256):
    S, C = proj.shape[0], proj.shape[1] // 8
    tr = min(tr, S)
    n = tr + 16

    def body(*refs):
        ins, (w_ref, cb_ref, g_ref, _, dp_ref, sums_ref) = refs[:15], refs[15:]
        i = pl.program_id(0)
        exists = _ext_rows(i, tr, S)
        u, bg, cg, zc, dyn = (jnp.concatenate([ins[3 * t][...], ins[3 * t + 1][...], ins[3 * t + 2][...]], axis=0)
                              for t in range(5))
        w = w_ref[...]
        t = jnp.where(exists, cg * u, 0.0)
        t_before, t_after = pltpu.roll(t, 1, 0), pltpu.roll(t, n - 1, 0)
        cv = w[0:1] * t_before + w[1:2] * t + w[2:3] * t_after + cb_ref[...]
        yc = bg * cv
        rc = lax.rsqrt(jnp.mean(yc * yc, axis=-1, keepdims=True) + EPS)
        yhat = yc * rc
        sz, dsz = _silu_and_slope(zc)
        wgt = dyn * g_ref[...] * sz
        dyc = rc * (wgt - yhat * jnp.mean(wgt * yhat, axis=-1, keepdims=True))
        dcv = jnp.where(exists, dyc * bg, 0.0)
        dt = w[0:1] * pltpu.roll(dcv, n - 1, 0) + w[1:2] * dcv + w[2:3] * pltpu.roll(dcv, 1, 0)
        mid = slice(8, tr + 8)
        dp_ref[:, 0:C] = (dt * cg)[mid].astype(bf16)
        dp_ref[:, C:2 * C] = (dyc * cv)[mid].astype(bf16)
        dp_ref[:, 2 * C:3 * C] = (dt * u)[mid].astype(bf16)
        dp_ref[:, 3 * C:4 * C] = (dyn * yhat * g_ref[...] * dsz)[mid].astype(bf16)
        colsum = lambda v: jnp.sum(v[mid], axis=0, keepdims=True)
        parts = [colsum(dyn * yhat * sz), colsum(dcv), colsum(dcv * t_before), colsum(dcv * t), colsum(dcv * t_after)]
        row = lax.broadcasted_iota(jnp.int32, (8, C), 0)
        upd = jnp.zeros((8, C), f32)
        for j, pj in enumerate(parts):
            upd = jnp.where(row == j, pj, upd)

        @pl.when(i == 0)
        def _():
            sums_ref[...] = upd

        @pl.when(i > 0)
        def _():
            sums_ref[...] += upd

    specs = []
    for col in range(4):
        specs += _halo_specs(tr, S, C, col)
    specs += _halo_specs(tr, S, C, 0)
    vec = pl.BlockSpec((1, C), lambda i: (0, 0))
    return pl.pallas_call(
        body, name="conv_bwd",
        out_shape=(jax.ShapeDtypeStruct((S, 4 * C), bf16), jax.ShapeDtypeStruct((8, C), f32)), grid=(S // tr,),
        in_specs=[*specs, pl.BlockSpec((8, C), lambda i: (0, 0)), vec, vec, ANY_SPEC],
        out_specs=(pl.BlockSpec((tr, 4 * C), lambda i: (i, 0)), pl.BlockSpec((8, C), lambda i: (0, 0))),
        compiler_params=_params("arbitrary"),
    )(*([proj] * 12), dycat, dycat, dycat, conv_w, conv_b, g_conv, dep)


def _attn_post_bwd(o, proj, dycat, g_attn, dep, tr=512):
    S, C = o.shape
    tr = min(tr, S)

    def body(o_ref, z_ref, dy_ref, g_ref, dep_ref, do_ref, dz_ref, sums_ref):
        i = pl.program_id(0)
        ov, zv, dyn = o_ref[...], z_ref[...], dy_ref[...]
        ra = lax.rsqrt(jnp.mean(ov * ov, axis=-1, keepdims=True) + EPS)
        ohat = ov * ra
        sz, dsz = _silu_and_slope(zv)
        wgt = dyn * g_ref[...] * sz
        do_ref[...] = ra * (wgt - ohat * jnp.mean(wgt * ohat, axis=-1, keepdims=True))
        dz_ref[...] = (dyn * ohat * g_ref[...] * dsz).astype(bf16)
        row = lax.broadcasted_iota(jnp.int32, (8, C), 0)
        upd = jnp.where(row == 0, jnp.sum(dyn * ohat * sz, axis=0, keepdims=True), 0.0)

        @pl.when(i == 0)
        def _():
            sums_ref[...] = upd

        @pl.when(i > 0)
        def _():
            sums_ref[...] += upd

    return pl.pallas_call(
        body, name="attn_post_bwd",
        out_shape=(jax.ShapeDtypeStruct((S, C), f32), jax.ShapeDtypeStruct((4, S, C), bf16),
                   jax.ShapeDtypeStruct((8, C), f32)),
        grid=(S // tr,),
        in_specs=[pl.BlockSpec((tr, C), lambda i: (i, 0)), pl.BlockSpec((tr, C), lambda i: (i, 7)),
                  pl.BlockSpec((tr, C), lambda i: (i, 1)), pl.BlockSpec((1, C), lambda i: (0, 0)), ANY_SPEC],
        out_specs=(pl.BlockSpec((tr, C), lambda i: (i, 0)), pl.BlockSpec((None, tr, C), lambda i: (3, i, 0)),
                   pl.BlockSpec((8, C), lambda i: (0, 0))),
        compiler_params=_params("arbitrary"),
    )(o, proj, dycat, g_attn, dep)


def _attn_bwd(proj, o, do, lse, slopes, dqkvz, dep):
    S, C = o.shape
    npair = C // PAIR

    def body(q_ref, k_ref, v_ref, o_ref, do_ref, lse_ref, sl_ref, old_ref, dep_ref, dqkv_ref,
             acc_scr, dl_scr, quad_scr, bias_scr):
        lane = lax.broadcasted_iota(jnp.int32, (1, PAIR), 1)
        first = lane < HEAD_DIM
        _fill_bias(bias_scr, sl_ref, S)
        ch = min(256, S)

        def prep(i, carry):
            rows = pl.ds(pl.multiple_of(i * ch, 8), ch)
            prod = do_ref[rows, :] * o_ref[rows, :]
            d0 = jnp.sum(jnp.where(first, prod, 0.0), axis=-1, keepdims=True)
            d1 = jnp.sum(jnp.where(first, 0.0, prod), axis=-1, keepdims=True)
            dl_scr[rows, :] = jnp.where(first, d0, d1)
            zero = jnp.zeros((ch, PAIR), f32)
            for order in range(2):
                for t in range(3):
                    acc_scr[order, t, rows, :] = zero
            return carry

        lax.fori_loop(0, S // ch, prep, 0, unroll=2)
        token_srcs = (q_ref, k_ref, v_ref, do_ref, lse_ref, dl_scr)
        for j, src in enumerate(token_srcs):
            _to_quad(quad_scr.at[j], src, S)

        for b, (_, r, inter) in enumerate(BRANCHES):
            L, nq, nk, nblk = _branch_geometry(S, r, inter)
            order = 0 if r == 1 else 1
            srcs = token_srcs if r == 1 else tuple(quad_scr.at[j] for j in range(6))

            def step(idx, carry, b=b, r=r, L=L, nq=nq, nk=nk, nblk=nblk, order=order, srcs=srcs):
                qs, ks, vs, dos, lses, dls = srcs
                dq_scr, dk_scr, dv_scr = (acc_scr.at[order, t] for t in range(3))
                qrows, krows, off = _block_rows(idx, r, inter, S, L, nq, nk, nblk)
                case = off // HALF_WIN
                q2 = qs[qrows, :] * SCALE
                k2 = ks[krows, :].astype(bf16)
                v2 = vs[krows, :].astype(bf16)
                do2 = dos[qrows, :]
                lse2 = lses[qrows, :]
                dl2 = dls[qrows, :]
                dq2 = jnp.zeros((nq, PAIR), f32)
                dk2 = jnp.zeros((nk, PAIR), f32)
                dv2 = jnp.zeros((nk, PAIR), f32)
                for hh in range(2):
                    mine = first if hh == 0 else ~first
                    lo = hh * HEAD_DIM
                    qh = jnp.where(mine, q2, 0.0).astype(bf16)
                    doh = jnp.where(mine, do2, 0.0).astype(bf16)
                    s = lax.dot_general(qh, k2, (((1,), (1,)), ((), ())), preferred_element_type=f32)
                    s = s + bias_scr[_bias_index(b, case, hh), 0:nq, 0:nk]
                    p = jnp.exp(s - lse2[:, lo:lo + 1])
                    dv2 = dv2 + lax.dot_general(p.astype(bf16), doh, (((0,), (0,)), ((), ())), preferred_element_type=f32)
                    dp = lax.dot_general(doh, v2, (((1,), (1,)), ((), ())), preferred_element_type=f32)
                    ds = (p * (dp - dl2[:, lo:lo + 1])).astype(bf16)
                    dq2 = dq2 + jnp.where(mine, jnp.dot(ds, k2, preferred_element_type=f32), 0.0)
                    dk2 = dk2 + lax.dot_general(ds, qh, (((0,), (0,)), ((), ())), preferred_element_type=f32)
                dq_scr[qrows, :] = dq_scr[qrows, :] + dq2
                dk_scr[krows, :] = dk_scr[krows, :] + dk2
                dv_scr[krows, :] = dv_scr[krows, :] + dv2
                return carry

            lax.fori_loop(0, S // nq, step, 0, unroll=min(ATTN_UNROLL, S // nq))

        n4 = S // QUAD
        for t in range(3):
            for rho in range(QUAD):
                token_rows = pl.ds(rho, n4, stride=QUAD)
                acc_scr[0, t, token_rows, :] = acc_scr[0, t, token_rows, :] + acc_scr[1, t, pl.ds(rho * n4, n4), :]
        dqkv_ref[0] = (acc_scr[0, 0] * SCALE).astype(bf16)
        dqkv_ref[1] = acc_scr[0, 1].astype(bf16)
        dqkv_ref[2] = acc_scr[0, 2].astype(bf16)

    blk = lambda part: pl.BlockSpec((S, PAIR), lambda p: (0, part * npair + p))
    own = pl.BlockSpec((S, PAIR), lambda p: (0, p))
    return pl.pallas_call(
        body, name="attn_bwd", out_shape=jax.ShapeDtypeStruct(dqkvz.shape, dqkvz.dtype), grid=(npair,),
        in_specs=[blk(4), blk(5), blk(6), own, own, own, pl.BlockSpec((None, 8, PAIR), lambda p: (p, 0, 0)),
                  ANY_SPEC, ANY_SPEC],
        out_specs=pl.BlockSpec((3, S, PAIR), lambda p: (0, 0, p)), input_output_aliases={7: 0},
        scratch_shapes=[pltpu.VMEM((2, 3, S, PAIR), f32), pltpu.VMEM((S, PAIR), f32), pltpu.VMEM((6, S, PAIR), f32),
                        pltpu.VMEM(_bias_shape(S), f32)],
        compiler_params=_params("arbitrary"),
    )(proj, proj, proj, o, do, lse, slopes, dqkvz, dep)


def _prenorm_bwd(dh, x, dout, scale, g_pre, tr=256):
    S, D = x.shape
    tr = min(tr, S)

    def body(dh_ref, x_ref, dout_ref, sc_ref, g_ref, gx_ref, sums_ref):
        i = pl.program_id(0)
        xv, dhv = x_ref[...], dh_ref[...]
        r = lax.rsqrt(jnp.mean(xv * xv, axis=-1, keepdims=True) + EPS)
        xn = xv * r
        dxn = dhv * (g_ref[...] * (1.0 + sc_ref[...]))
        gx_ref[...] = dout_ref[...] + r * (dxn - xn * jnp.mean(dxn * xn, axis=-1, keepdims=True))
        dhx = dhv * xn
        row = lax.broadcasted_iota(jnp.int32, (8, D), 0)
        upd = jnp.where(row == 0, jnp.sum(dhv, axis=0, keepdims=True),
                        jnp.where(row == 1, jnp.sum(dhx, axis=0, keepdims=True) * g_ref[...],
                                  jnp.where(row == 2, jnp.sum(dhx, axis=0, keepdims=True) * (1.0 + sc_ref[...]), 0.0)))

        @pl.when(i == 0)
        def _():
            sums_ref[...] = upd

        @pl.when(i > 0)
        def _():
            sums_ref[...] += upd

    row = pl.BlockSpec((tr, D), lambda i: (i, 0))
    vec = pl.BlockSpec((1, D), lambda i: (0, 0))
    return pl.pallas_call(
        body, name="prenorm_bwd",
        out_shape=(jax.ShapeDtypeStruct((S, D), f32), jax.ShapeDtypeStruct((8, D), f32)), grid=(S // tr,),
        in_specs=[row, row, row, vec, vec], out_specs=(row, pl.BlockSpec((8, D), lambda i: (0, 0))),
        compiler_params=_params("arbitrary"),
    )(dh, x, dout, scale, g_pre)


def _adamw(w, g, m, v):
    m = ADAM_B1 * m + (1.0 - ADAM_B1) * g
    v = ADAM_B2 * v + (1.0 - ADAM_B2) * (g * g)
    m_hat = m / (1.0 - ADAM_B1 ** ADAM_STEP)
    v_hat = v / (1.0 - ADAM_B2 ** ADAM_STEP)
    delta = -ADAM_LR * (m_hat / (jnp.sqrt(v_hat) + ADAM_EPS) + ADAM_WD * w)
    return delta, m, v


def _sum_rows(parts, dep):
    P = parts.shape[1]

    def body(p_ref, dep_ref, o_ref):
        acc = p_ref[0:1, :]
        for j in range(1, NDEV):
            acc = acc + p_ref[j:j + 1, :]
        o_ref[...] = jnp.broadcast_to(acc, (8, P))

    vmem = pl.BlockSpec(memory_space=pltpu.VMEM)
    return pl.pallas_call(body, name="sum_small", out_shape=jax.ShapeDtypeStruct((8, P), f32),
                          in_specs=[vmem, ANY_SPEC], out_specs=vmem, compiler_params=_params())(parts, dep)


def _adamw_small(tot, params):
    given = [p[3] for p in params if not isinstance(p[3], int)]

    def body(tot_ref, *refs):
        given_refs = list(refs[:len(given)])
        ins = refs[len(given):len(given) + 3 * len(params)]
        outs = refs[len(given) + 3 * len(params):]
        for t, (w, _, _, where) in enumerate(params):
            w_ref, m_ref, v_ref = ins[3 * t:3 * t + 3]
            g = tot_ref[0:1, where:where + w.size] if isinstance(where, int) else given_refs.pop(0)[...]
            outs[4 * t][...] = g
            outs[4 * t + 1][...], outs[4 * t + 2][...], outs[4 * t + 3][...] = _adamw(w_ref[...], g, m_ref[...], v_ref[...])

    out_shape = tuple(jax.ShapeDtypeStruct(p[0].shape, f32) for p in params for _ in range(4))
    res = pl.pallas_call(body, name="adamw_small", out_shape=out_shape, compiler_params=_params())(
        tot, *given, *[a for p in params for a in p[:3]])
    return [res[4 * t:4 * t + 4] for t in range(len(params))]


def _adamw_sharded(name, parts, sums_a, sums_b, pick, w, m, v, rows=None, prev=None, tr=128):
    R, Cc = w.shape
    r0, nr = rows or (0, R)
    tr = math.gcd(tr, r0, nr)
    n, b0 = parts.shape[0], r0 // tr

    def body(pick_ref, p_ref, a_ref, b_ref, w_ref, m_ref, v_ref, *rest):
        g_ref, d_ref, nm_ref, nv_ref = rest[-4:]
        g = jnp.where(pick_ref[0] == 1, b_ref[...], a_ref[...]).astype(f32)
        for j in range(n):
            g = g + p_ref[j].astype(f32)
        g_ref[...] = g
        d_ref[...], nm_ref[...], nv_ref[...] = _adamw(w_ref[...], g, m_ref[...], v_ref[...])

    row = pl.BlockSpec((tr, Cc), lambda i, pick: (i + b0, 0))
    mine = pl.BlockSpec((None, tr, Cc), lambda i, pick: (pick[1], i + b0, 0))
    out = jax.ShapeDtypeStruct((R, Cc), f32)
    prev = list(prev or [])
    grid_spec = pltpu.PrefetchScalarGridSpec(
        num_scalar_prefetch=1, grid=(nr // tr,),
        in_specs=[pl.BlockSpec((n, tr, Cc), lambda i, pick: (0, i + b0, 0)), mine, mine, row, row, row]
        + [ANY_SPEC] * len(prev),
        out_specs=(row, row, row, row))
    return pl.pallas_call(
        body, name=name, out_shape=(out, out, out, out), grid_spec=grid_spec,
        input_output_aliases={7 + t: t for t in range(len(prev))}, compiler_params=_params("arbitrary"),
    )(pick, parts, sums_a, sums_b, w, m, v, *prev)


def _adamw_ada(c_t, dmod_cols, w, m, v, dep, tr=512):
    D, W = w.shape
    tr = min(tr, D)

    def body(c_ref, dm_ref, w_ref, m_ref, v_ref, dep_ref, g_ref, d_ref, nm_ref, nv_ref):
        cv, dm = c_ref[...], dm_ref[...]
        g = cv[:, 0:1] * dm[0:1, :]
        for b in range(1, NDEV):
            g = g + cv[:, b:b + 1] * dm[b:b + 1, :]
        g_ref[...] = g
        d_ref[...], nm_ref[...], nv_ref[...] = _adamw(w_ref[...], g, m_ref[...], v_ref[...])

    row = pl.BlockSpec((tr, W), lambda i: (i, 0))
    out = jax.ShapeDtypeStruct((D, W), f32)
    return pl.pallas_call(
        body, name="adamw_ada", out_shape=(out, out, out, out), grid=(D // tr,),
        in_specs=[pl.BlockSpec((tr, NDEV), lambda i: (i, 0)), pl.BlockSpec((NDEV, W), lambda i: (0, 0)), row, row, row,
                  ANY_SPEC],
        out_specs=(row, row, row, row), compiler_params=_params("parallel"),
    )(c_t, dmod_cols, w, m, v, dep)


def kernel(x, c, w_ada, b_ada, g_pre, w_in, conv_w, conv_b, g_conv, g_attn, w_out, g_post, loss_target, m_w_ada, m_b_ada, m_g_pre, m_w_in, m_conv_w, m_conv_b, m_g_conv, m_g_attn, m_w_out, m_g_post, v_w_ada, v_b_ada, v_g_pre, v_w_in, v_conv_w, v_conv_b, v_g_conv, v_g_attn, v_w_out, v_g_post):
    S, D = x.shape[1], x.shape[2]
    C = D // 2
    W = w_ada.shape[2]
    CW = conv_w.shape[2]
    me = 4 * lax.axis_index("x") + 2 * lax.axis_index("y") + lax.axis_index("c")
    x2, tgt = x[0], loss_target[0]
    w_ada2, w_in2, w_out2 = w_ada[0], w_in[0], w_out[0]

    R = D // NDEV
    core = lax.axis_index("c").astype(jnp.int32).reshape(1)

    cw_slab = jnp.zeros((8, CW), f32).at[:3].set(conv_w[0])
    b_cols = lax.dynamic_slice_in_dim(b_ada, me * W, W, axis=1)
    mod_slabs, c_blocks, cw_g = _ada_exchange(c.reshape(D // 128, 128), cw_slab, w_ada2, b_cols)
    c_all = c_blocks.reshape(NDEV, D)
    conv_w_full = jnp.transpose(cw_g, (1, 0, 2)).reshape(8, C)
    mod = mod_slabs[:, 0, :].reshape(1, 3 * D)
    shift, scale, gate = mod[:, :D], mod[:, D:2 * D], mod[:, 2 * D:]

    land_i = lax.dynamic_update_slice(lax.empty((NDEV, D, C), bf16), w_in2.astype(bf16)[None], (me, 0, 0))
    land_o = lax.dynamic_update_slice(lax.empty((NDEV, R, D), bf16), w_out2.astype(bf16)[None], (me, 0, 0))
    wi_send, wi_recv, land_i, w_token = _w_in_start(land_i, [mod_slabs])

    me_arr = me.astype(jnp.int32).reshape(1)
    h = _prenorm(x2, scale, shift, g_pre, w_token)
    land_i = _w_in_sibling(land_i, wi_recv, after=[h])
    proj = _in_proj_part("in_proj_a", h, land_i, None, me_arr, 0, 1, 2)
    fi_send, fi_recv, land_i = _w_in_relay(land_i, wi_recv, after=[proj])
    proj = _in_proj_part("in_proj_b", h, land_i, proj, me_arr, 2, 2, 2)
    land_i = _w_in_forwarded(land_i, fi_recv, after=[proj])
    proj = _in_proj_part("in_proj_c", h, land_i, proj, me_arr, 3, 2, 2)
    (di_send, di_recv, wo_send, wo_recv), land_i, land_o = _w_in_diag(land_i, land_o, fi_recv, after=[proj])
    proj = _in_proj_part("in_proj_d", h, land_i, proj, me_arr, 6, 1, 1)
    win_g = _w_in_finish(land_i, wi_send, fi_send, di_send, di_recv, after=[proj])
    proj = _in_proj_part("in_proj_e", h, win_g, proj, me_arr, 7, 1, 1)
    slopes = _head_slopes(C // HEAD_DIM)
    ycat = _conv_fwd(proj, conv_w_full, conv_b, g_conv)
    o, lse = _attn_fwd(proj, slopes)
    (fo_send, fo_recv), (land_o,), _ = _weights_forward("w_out_forward", land_o, wo_recv, after=[o])
    ycat = _attn_post(ycat, o, proj, g_attn)
    wout_g = _weights_wait("w_out_wait", land_o, wo_send, wo_recv, fo_send, fo_recv, after=[ycat])
    wout_full = wout_g.reshape(D, D)
    y = _matmul(ycat, wout_full, name="out_proj", out_dtype=f32)
    dy, dout, post_sums = _sandwich(y, x2, tgt, gate, g_post)

    chip = me // 2

    def landing(rows, cols):
        return lax.dynamic_update_slice(lax.empty((NCHIP, rows, cols), bf16), jnp.zeros((1, rows, cols), bf16),
                                        (chip, 0, 0))

    gw_out = _matmul(ycat, dy, name="out_proj_dw", out_dtype=bf16, ta=True).reshape(NDEV, R, D)
    po_send, po_recv, gw_out, pair_o, po_token = _pair_start("g_out_pair_start", gw_out)
    dycat = _matmul(dy, wout_full, name="out_proj_dx", out_dtype=f32, tb=True, dep=po_token)
    gw_out, pair_o = _pair_wait("g_out_pair_wait", gw_out, pair_o, po_send, po_recv, after=[dycat])
    sum_o = _pair_sum("g_out_pair_sum", gw_out, pair_o, core)
    co_send, co_recv, sum_o, land_go, co_token = _chip_start(
        "g_out_chip_start", sum_o, landing(R, D), 0)
    dpc, conv_sums = _conv_bwd(proj, dycat, conv_w_full, conv_b, g_conv, co_token)
    gw_c = _matmul(h, dpc, name="in_proj_dw_conv", out_dtype=bf16, ta=True, out_slots=4)
    pc_send, pc_recv, gw_c, pair_c, pc_token = _pair_start("g_conv_pair_start", gw_c)
    do, dpa, attn_sums = _attn_post_bwd(o, proj, dycat, g_attn, pc_token)
    gw_c, pair_c = _pair_wait("g_conv_pair_wait", gw_c, pair_c, pc_send, pc_recv, after=[do])
    sum_c = _pair_sum("g_conv_pair_sum", gw_c, pair_c, core)
    cc_send, cc_recv, sum_c, land_gi, cc_token = _chip_start(
        "g_conv_chip_start", sum_c, landing(D, C), 0)
    dpa = _attn_bwd(proj, o, do, lse, slopes, dpa, cc_token)
    gw_a = _matmul(h, dpa, name="in_proj_dw_attn", out_dtype=bf16, ta=True, b_slots=True, out_slots=4)
    pa_send, pa_recv, gw_a, pair_a, pa_token = _pair_start("g_attn_pair_start", gw_a)
    sum_o, land_go = _chip_wait("g_out_chip_wait", sum_o, land_go, co_send, co_recv, 0, after=[pa_token])
    pick_out = jnp.stack([jnp.int32(0), me // 2]).astype(jnp.int32)
    g_w_out, d_w_out, nm_w_out, nv_w_out = _adamw_sharded(
        "adamw_w_out", land_go, sum_o, sum_o, pick_out, w_out2, m_w_out[0], v_w_out[0])
    gw_a, pair_a = _pair_wait("g_attn_pair_wait", gw_a, pair_a, pa_send, pa_recv, after=[g_w_out])
    sum_a = _pair_sum("g_attn_pair_sum", gw_a, pair_a, core)
    part_a, part_b = (0, 3 * D // 4), (3 * D // 4, D // 4)
    ca_send, ca_recv, sum_a, land_gi, ca_token = _chip_start("g_attn_chip_start_a", sum_a, land_gi, 4, part_a)
    dh = _matmul_slabs_t(dpc, dpa, win_g, name="in_proj_dx", dep=ca_token)
    grad_x, pre_sums = _prenorm_bwd(dh, x2, dout, scale, g_pre)

    small = jnp.concatenate([pre_sums[0:1], pre_sums[1:2], post_sums[0:1],
                             pre_sums[2:3], post_sums[1:2],
                             conv_sums[2:3], conv_sums[3:4], conv_sums[4:5],
                             conv_sums[1:2], conv_sums[0:1], attn_sums[0:1]], axis=1)
    small = jnp.concatenate([small.reshape(8 * D // 128, 128), jnp.broadcast_to(post_sums[2:3, :128], (8, 128))])
    (small_all,) = _all_gather([small], "gather_small")
    cb_send, cb_recv, sum_a, land_gi, cb_token = _chip_start("g_attn_chip_start_b", sum_a, land_gi, 4, part_b,
                                                             after=[small_all])
    small_all = small_all.reshape(NDEV, small.size)
    tot = _sum_rows(small_all, cb_token)
    loss = tot[0, 8 * D]
    g_conv_w = lax.dynamic_slice_in_dim(tot[0:1, 5 * D:5 * D + 3 * C].reshape(1, 3, C), me * CW, CW, axis=2)
    ((g_b_ada, d_b_ada, nm_b_ada, nv_b_ada), (g_g_pre, d_g_pre, nm_g_pre, nv_g_pre),
     (g_g_post, d_g_post, nm_g_post, nv_g_post), (g_conv_w, d_conv_w, nm_conv_w, nv_conv_w),
     (g_conv_b, d_conv_b, nm_conv_b, nv_conv_b), (g_g_conv, d_g_conv, nm_g_conv, nv_g_conv),
     (g_g_attn, d_g_attn, nm_g_attn, nv_g_attn)) = _adamw_small(tot, [
         (b_ada, m_b_ada, v_b_ada, 0), (g_pre, m_g_pre, v_g_pre, 3 * D), (g_post, m_g_post, v_g_post, 4 * D),
         (conv_w, m_conv_w, v_conv_w, g_conv_w), (conv_b, m_conv_b, v_conv_b, 5 * D + 3 * C),
         (g_conv, m_g_conv, v_g_conv, 5 * D + 4 * C), (g_attn, m_g_attn, v_g_attn, 5 * D + 5 * C)])

    dmod_cols = lax.dynamic_slice_in_dim(small_all[:, :3 * D], me * W, W, axis=1)
    g_w_ada, d_w_ada, nm_w_ada, nv_w_ada = _adamw_ada(c_all.T, dmod_cols, w_ada2, m_w_ada[0], v_w_ada[0], cb_token)

    pick_in = jnp.stack([me // 4, (me % 4) // 2]).astype(jnp.int32)
    sum_c, land_gi = _chip_wait("g_conv_chip_wait", sum_c, land_gi, cc_send, cc_recv, 0, after=[g_w_ada])
    sum_a, land_gi = _chip_wait("g_attn_chip_wait_a", sum_a, land_gi, ca_send, ca_recv, 4, [g_w_ada], part_a)
    first = _adamw_sharded("adamw_w_in_a", land_gi, sum_c, sum_a, pick_in, w_in2, m_w_in[0], v_w_in[0], rows=part_a,
                           tr=256)
    sum_a, land_gi = _chip_wait("g_attn_chip_wait_b", sum_a, land_gi, cb_send, cb_recv, 4, [first[0]], part_b)
    g_w_in, d_w_in, nm_w_in, nv_w_in = _adamw_sharded(
        "adamw_w_in_b", land_gi, sum_c, sum_a, pick_in, w_in2, m_w_in[0], v_w_in[0], rows=part_b, prev=first, tr=256)

    return (loss, grad_x[None],
            g_w_ada[None], g_b_ada, g_g_pre, g_w_in[None], g_conv_w, g_conv_b, g_g_conv, g_g_attn, g_w_out[None], g_g_post,
            d_w_ada[None], d_b_ada, d_g_pre, d_w_in[None], d_conv_w, d_conv_b, d_g_conv, d_g_attn, d_w_out[None], d_g_post,
            nm_w_ada[None], nm_b_ada, nm_g_pre, nm_w_in[None], nm_conv_w, nm_conv_b, nm_g_conv, nm_g_attn, nm_w_out[None], nm_g_post,
            nv_w_ada[None], nv_b_ada, nv_g_pre, nv_w_in[None], nv_conv_w, nv_conv_b, nv_g_conv, nv_g_attn, nv_w_out[None], nv_g_post)
```

```python
import functools
import math

import jax
import jax.numpy as jnp
from jax import lax
from jax.experimental import pallas as pl
from jax.experimental.pallas import tpu as pltpu

f32 = jnp.float32
bf16 = jnp.bfloat16

NDEV = 8
HEAD_DIM = 64
PAIR = 2 * HEAD_DIM
BRANCHES = ((128, 1, 1), (512, 4, 1), (2048, 16, 2))
HALF_WIN = 64
EPS = 1e-6
NEG_INF = -1e30
ADAM_LR, ADAM_B1, ADAM_B2, ADAM_EPS, ADAM_WD, ADAM_STEP = 0.001, 0.9, 0.999, 1e-08, 0.01, 10
MESH = pl.DeviceIdType.MESH
VMEM_LIMIT = 56 * 1024 * 1024
HBM_SPEC = pl.BlockSpec(memory_space=pltpu.HBM)
ANY_SPEC = pl.BlockSpec(memory_space=pl.ANY)
SEM_SPEC = pl.BlockSpec(memory_space=pltpu.SEMAPHORE)


def _params(*sem):
    return pltpu.CompilerParams(dimension_semantics=sem or None, vmem_limit_bytes=VMEM_LIMIT)


def _silu(z):
    return z * jax.nn.sigmoid(z)


def _silu_and_slope(z):
    s = jax.nn.sigmoid(z)
    return z * s, s * (1.0 + z * (1.0 - s))


def _my_place():
    x, y, c = lax.axis_index("x"), lax.axis_index("y"), lax.axis_index("c")
    return x, y, c, 4 * x + 2 * y + c


def _peer(x, y, c, k):
    px, py, pc = x ^ (k >> 2 & 1), y ^ (k >> 1 & 1), c ^ (k & 1)
    return (px, py, pc), 4 * px + 2 * py + pc


def _all_gather(arrays, name):
    n = len(arrays)

    def body(*refs):
        srcs, dsts = refs[:n], refs[n:2 * n]
        send_sems, recv_sems, local_sems = refs[2 * n:]
        x, y, c, me = _my_place()
        locals_, sends = [], []
        for t in range(n):
            own = pltpu.make_async_copy(srcs[t], dsts[t].at[me], local_sems.at[t])
            own.start()
            locals_.append(own)
            for k in range(1, NDEV):
                peer, pidx = _peer(x, y, c, k)
                cp = pltpu.make_async_remote_copy(
                    src_ref=srcs[t], dst_ref=dsts[t].at[me], send_sem=send_sems.at[t, k],
                    recv_sem=recv_sems.at[t, k], device_id=peer, device_id_type=MESH)
                cp.start()
                sends.append(cp)
        for t in range(n):
            for k in range(1, NDEV):
                peer, pidx = _peer(x, y, c, k)
                pltpu.make_async_remote_copy(
                    src_ref=srcs[t], dst_ref=dsts[t].at[pidx], send_sem=send_sems.at[t, k],
                    recv_sem=recv_sems.at[t, k], device_id=peer, device_id_type=MESH).wait_recv()
        for cp in sends:
            cp.wait_send()
        for cp in locals_:
            cp.wait()

    return pl.pallas_call(
        body, name=name,
        out_shape=tuple(jax.ShapeDtypeStruct((NDEV,) + a.shape, a.dtype) for a in arrays),
        in_specs=[HBM_SPEC] * n, out_specs=tuple([HBM_SPEC] * n),
        scratch_shapes=[pltpu.SemaphoreType.DMA((n, NDEV)), pltpu.SemaphoreType.DMA((n, NDEV)),
                        pltpu.SemaphoreType.DMA((n,))],
    )(*arrays)


def _comm_call(name, arrays, sems, new_sems, body, after=(), token=False):
    na, ns, nn, nf = len(arrays), len(sems), len(new_sems), len(after)

    def kern(*refs):
        ins, outs = refs[:na + ns + nf], refs[na + ns + nf:]
        body(ins[:na], ins[na:na + ns], outs[:nn])
        if token:
            outs[nn + na][...] = jnp.zeros((8, 128), f32)

    out_shape = ([pltpu.SemaphoreType.DMA(s) for s in new_sems] + [pltpu.HBM(a.shape, a.dtype) for a in arrays]
                 + ([jax.ShapeDtypeStruct((8, 128), f32)] if token else []))
    out_specs = [SEM_SPEC] * nn + [HBM_SPEC] * na + ([pl.BlockSpec(memory_space=pltpu.VMEM)] if token else [])
    res = pl.pallas_call(
        kern, name=name, out_shape=tuple(out_shape),
        in_specs=[HBM_SPEC] * na + [SEM_SPEC] * ns + [ANY_SPEC] * nf, out_specs=tuple(out_specs),
        input_output_aliases={t: nn + t for t in range(na)},
        compiler_params=pltpu.CompilerParams(has_side_effects=pltpu.SideEffectType.DATAFLOW_SIDE_EFFECTING),
    )(*[pltpu.with_memory_space_constraint(a, pltpu.HBM) for a in arrays], *sems, *after)
    return list(res[:nn]), list(res[nn:nn + na]), (res[nn + na] if token else None)


def _remote(src, dst, send_sem, recv_sem, device):
    return pltpu.make_async_remote_copy(src_ref=src, dst_ref=dst, send_sem=send_sem, recv_sem=recv_sem,
                                        device_id=device, device_id_type=MESH)


SAME_CORE = (2, 4, 6)
VIA_SIBLING = (3, 5, 7)


def _weights_forward(name, land, recv, after):
    def body(a, s, new):
        (land,), (recv,), (fsend, frecv) = a, s, new
        x, y, c, me = _my_place()
        sibling, _ = _peer(x, y, c, 1)
        for k in SAME_CORE:
            peer, slot = _peer(x, y, c, k)
            _remote(land.at[slot], land.at[slot], fsend.at[k], recv.at[k], peer).wait_recv()
            _remote(land.at[slot], land.at[slot], fsend.at[k], frecv.at[k ^ 1], sibling).start()

    return _comm_call(name, [land], [recv], [(NDEV,), (NDEV,)], body, after=after)


def _weights_wait(name, land, send, recv, fsend, frecv, after):
    def body(a, s, new):
        (land,), (send, recv, fsend, frecv) = a, s
        x, y, c, me = _my_place()
        sibling, sib_slot = _peer(x, y, c, 1)
        _remote(land.at[sib_slot], land.at[sib_slot], send.at[1], recv.at[1], sibling).wait_recv()
        for k in VIA_SIBLING:
            _, slot = _peer(x, y, c, k)
            _remote(land.at[slot], land.at[slot], fsend.at[k ^ 1], frecv.at[k], sibling).wait_recv()
        for k in (1,) + SAME_CORE:
            peer, _ = _peer(x, y, c, k)
            _remote(land.at[me], land.at[me], send.at[k], recv.at[k], peer).wait_send()
        for k in SAME_CORE:
            _, slot = _peer(x, y, c, k)
            _remote(land.at[slot], land.at[slot], fsend.at[k], frecv.at[k ^ 1], sibling).wait_send()

    return _comm_call(name, [land], [send, recv, fsend, frecv], [], body, after=after)[1][0]


def _diag_relay(x, y, c):
    slot = 4 * (x ^ (1 - c)) + 2 * (y ^ c) + c
    return slot, (x ^ c, y ^ (1 - c), c)


def _w_in_start(land, after):
    def body(a, s, new):
        (land,), (send, recv) = a, new
        x, y, c, me = _my_place()
        for k in (1, 2, 4):
            peer, _ = _peer(x, y, c, k)
            _remote(land.at[me], land.at[me], send.at[k], recv.at[k], peer).start()

    (send, recv), (land,), token = _comm_call("w_in_start", [land], [], [(NDEV,), (NDEV,)], body, after=after, token=True)
    return send, recv, land, token


def _w_in_sibling(land, recv, after):
    def body(a, s, new):
        (land,), (recv,) = a, s
        x, y, c, me = _my_place()
        sibling, slot = _peer(x, y, c, 1)
        _remote(land.at[slot], land.at[slot], recv.at[1], recv.at[1], sibling).wait_recv()

    return _comm_call("w_in_sibling", [land], [recv], [], body, after=after)[1][0]


def _w_in_relay(land, recv, after):
    def body(a, s, new):
        (land,), (recv,), (fsend, frecv) = a, s, new
        x, y, c, me = _my_place()
        sibling, _ = _peer(x, y, c, 1)
        for k in (2, 4):
            peer, slot = _peer(x, y, c, k)
            _remote(land.at[slot], land.at[slot], fsend.at[k], recv.at[k], peer).wait_recv()
        slot, target = _diag_relay(x, y, c)
        _remote(land.at[slot], land.at[slot], fsend.at[6], frecv.at[6], target).start()
        for k in (2, 4):
            _, slot = _peer(x, y, c, k)
            _remote(land.at[slot], land.at[slot], fsend.at[k], frecv.at[k ^ 1], sibling).start()

    (fsend, frecv), (land,), _ = _comm_call("w_in_relay", [land], [recv], [(NDEV,), (NDEV,)], body, after=after)
    return fsend, frecv, land


def _w_in_forwarded(land, frecv, after):
    def body(a, s, new):
        (land,), (frecv,) = a, s
        x, y, c, me = _my_place()
        sibling, _ = _peer(x, y, c, 1)
        for k in (3, 5):
            _, slot = _peer(x, y, c, k)
            _remote(land.at[slot], land.at[slot], frecv.at[k], frecv.at[k], sibling).wait_recv()

    return _comm_call("w_in_forwarded", [land], [frecv], [], body, after=after)[1][0]


def _w_in_diag(land, land_o, frecv, after):
    def body(a, s, new):
        (land, land_o), (frecv,), (dsend, drecv, osend, orecv) = a, s, new
        x, y, c, me = _my_place()
        sibling, _ = _peer(x, y, c, 1)
        peer, slot = _peer(x, y, c, 6)
        _remote(land.at[slot], land.at[slot], dsend.at[6], frecv.at[6], peer).wait_recv()
        _remote(land.at[slot], land.at[slot], dsend.at[6], drecv.at[7], sibling).start()
        for k in (1,) + SAME_CORE:
            peer, _ = _peer(x, y, c, k)
            _remote(land_o.at[me], land_o.at[me], osend.at[k], orecv.at[k], peer).start()

    sems, (land, land_o), _ = _comm_call("w_in_diag", [land, land_o], [frecv], [(NDEV,)] * 4, body, after=after)
    return sems, land, land_o


def _w_in_finish(land, send, fsend, dsend, drecv, after):
    def body(a, s, new):
        (land,), (send, fsend, dsend, drecv) = a, s
        x, y, c, me = _my_place()
        sibling, _ = _peer(x, y, c, 1)
        _, slot = _peer(x, y, c, 7)
        _remote(land.at[slot], land.at[slot], dsend.at[6], drecv.at[7], sibling).wait_recv()
        for k in (1, 2, 4):
            peer, _ = _peer(x, y, c, k)
            _remote(land.at[me], land.at[me], send.at[k], send.at[k], peer).wait_send()
        for k in (2, 4, 6):
            _, slot = _peer(x, y, c, k)
            _remote(land.at[slot], land.at[slot], fsend.at[k], fsend.at[k], sibling).wait_send()
        _, slot = _peer(x, y, c, 6)
        _remote(land.at[slot], land.at[slot], dsend.at[6], dsend.at[6], sibling).wait_send()

    return _comm_call("w_in_finish", [land], [send, fsend, dsend, drecv], [], body, after=after)[1][0]


def _in_proj_part(name, h, land, proj, me_arr, k0, kstep, nk, tm=512):
    S, D = h.shape
    C = land.shape[2]
    tm = min(tm, S)

    def body(me_ref, a_ref, b_ref, *rest):
        rest[-1][...] = jnp.dot(a_ref[...], b_ref[...], preferred_element_type=f32)

    slot = lambda j, me: me[0] ^ (k0 + kstep * j)
    args = [h, land] + ([] if proj is None else [proj])
    grid_spec = pltpu.PrefetchScalarGridSpec(
        num_scalar_prefetch=1, grid=(nk, S // tm),
        in_specs=[pl.BlockSpec((tm, D), lambda j, i, me: (i, 0)),
                  pl.BlockSpec((None, D, C), lambda j, i, me: (slot(j, me), 0, 0))] + [ANY_SPEC] * (len(args) - 2),
        out_specs=pl.BlockSpec((tm, C), lambda j, i, me: (i, slot(j, me))))
    return pl.pallas_call(
        body, name=name, out_shape=jax.ShapeDtypeStruct((S, NDEV * C), f32), grid_spec=grid_spec,
        input_output_aliases={} if proj is None else {3: 0}, compiler_params=_params("arbitrary", "arbitrary"),
    )(me_arr, *args)


NCHIP = NDEV // 2


def _pair_start(name, src):
    npair = src.shape[0] // 2

    def body(a, s, new):
        (src, pair), (send, recv) = a, new
        x, y, c, me = _my_place()
        sibling, _ = _peer(x, y, c, 1)
        for i in range(npair):
            _remote(src.at[2 * i + 1 - c], pair.at[i], send.at[i], recv.at[i], sibling).start()

    pair = lax.empty((npair,) + src.shape[1:], src.dtype)
    (send, recv), (src, pair), token = _comm_call(name, [src, pair], [], [(npair,), (npair,)], body, token=True)
    return send, recv, src, pair, token


def _pair_wait(name, src, pair, send, recv, after):
    npair = pair.shape[0]

    def body(a, s, new):
        (src, pair), (send, recv) = a, s
        x, y, c, me = _my_place()
        sibling, _ = _peer(x, y, c, 1)
        for i in range(npair):
            cp = _remote(src.at[2 * i + 1 - c], pair.at[i], send.at[i], recv.at[i], sibling)
            cp.wait_recv()
            cp.wait_send()

    return _comm_call(name, [src, pair], [send, recv], [], body, after=after)[1]


def _pair_sum(name, src, pair, core, tr=1024):
    npair, R, Cc = pair.shape
    tr = min(tr, R)

    def body(core_ref, a_ref, b_ref, o_ref):
        o_ref[...] = (a_ref[...].astype(f32) + b_ref[...].astype(f32)).astype(o_ref.dtype)

    grid_spec = pltpu.PrefetchScalarGridSpec(
        num_scalar_prefetch=1, grid=(npair, R // tr),
        in_specs=[pl.BlockSpec((None, tr, Cc), lambda i, r, core: (2 * i + core[0], r, 0)),
                  pl.BlockSpec((None, tr, Cc), lambda i, r, core: (i, r, 0))],
        out_specs=pl.BlockSpec((None, tr, Cc), lambda i, r, core: (i, r, 0)))
    return pl.pallas_call(body, name=name, out_shape=jax.ShapeDtypeStruct(pair.shape, pair.dtype),
                          grid_spec=grid_spec, compiler_params=_params("parallel", "parallel"))(core, src, pair)


def _owner_chip(first, i):
    q = first // 2 + i
    return q >> 1 & 1, q & 1


def _chip_start(name, sums, land, first, rows=None, after=()):
    npair = sums.shape[0]
    rows = pl.ds(*(rows or (0, sums.shape[1])))

    def body(a, s, new):
        (sums, land), (send, recv) = a, new
        x, y, c, me = _my_place()
        for i in range(npair):
            ox, oy = _owner_chip(first, i)

            @pl.when((x != ox) | (y != oy))
            def _():
                _remote(sums.at[i, rows], land.at[2 * x + y, rows], send.at[i], recv.at[2 * x + y], (ox, oy, c)).start()

    (send, recv), (sums, land), token = _comm_call(name, [sums, land], [], [(npair,), (NCHIP,)], body, after=after,
                                                   token=True)
    return send, recv, sums, land, token


def _chip_wait(name, sums, land, send, recv, first, after, rows=None):
    npair = sums.shape[0]
    rows = pl.ds(*(rows or (0, sums.shape[1])))

    def body(a, s, new):
        (sums, land), (send, recv) = a, s
        x, y, c, me = _my_place()
        mine = (me >= first) & (me < first + 2 * npair)
        for i in range(npair):
            ox, oy = _owner_chip(first, i)

            @pl.when((x != ox) | (y != oy))
            def _():
                _remote(sums.at[i, rows], land.at[2 * x + y, rows], send.at[i], recv.at[2 * x + y], (ox, oy, c)).wait_send()
        for q in range(NCHIP):
            @pl.when(mine & (2 * x + y != q))
            def _():
                _remote(sums.at[0, rows], land.at[q, rows], send.at[0], recv.at[q], (q >> 1, q & 1, c)).wait_recv()

    return _comm_call(name, [sums, land], [send, recv], [], body, after=after)[1]


def _matmul(a, b, *, name, out_dtype, ta=False, tb=False, b_slots=False, out_slots=0, b_cols=None,
            tm=1024, tn=1024, tk=2048, dep=None):
    M, K = (a.shape[1], a.shape[0]) if ta else a.shape
    col0 = 0
    if b_slots:
        slab = b.shape[2]
        N = b.shape[1] if tb else b.shape[0] * slab
        assert (K if tb else N) == b.shape[0] * slab
    elif b_cols is not None:
        assert not tb
        col0, N = b_cols
    else:
        N = b.shape[0] if tb else b.shape[1]
    tm, tn, tk = min(tm, M), min(tn, N), min(tk, K)
    if b_slots:
        if tb:
            tk = min(tk, slab)
        else:
            tn = min(tn, slab)
    if out_slots:
        tn = min(tn, N // out_slots)
    nm, nn, nk = M // tm, N // tn, K // tk
    assert (nm * tm, nn * tn, nk * tk) == (M, N, K) and col0 % tn == 0, (name, M, N, K, tm, tn, tk)
    j0 = col0 // tn

    a_spec = pl.BlockSpec((tk, tm), lambda i, j, k: (k, i)) if ta else pl.BlockSpec((tm, tk), lambda i, j, k: (i, k))
    if b_slots and tb:
        per = slab // tk
        b_spec = pl.BlockSpec((None, tn, tk), lambda i, j, k: (k // per, j, k % per))
    elif b_slots:
        per = slab // tn
        b_spec = pl.BlockSpec((None, tk, tn), lambda i, j, k: (j // per, k, j % per))
    elif tb:
        b_spec = pl.BlockSpec((tn, tk), lambda i, j, k: (j, k))
    else:
        b_spec = pl.BlockSpec((tk, tn), lambda i, j, k: (k, j + j0))
    if out_slots:
        per_o = (N // out_slots) // tn
        o_spec = pl.BlockSpec((None, tm, tn), lambda i, j, k: (j // per_o, i, j % per_o))
        out_shape = jax.ShapeDtypeStruct((out_slots, M, N // out_slots), out_dtype)
    else:
        o_spec = pl.BlockSpec((tm, tn), lambda i, j, k: (i, j))
        out_shape = jax.ShapeDtypeStruct((M, N), out_dtype)
    dims = (((0 if ta else 1,), (1 if tb else 0,)), ((), ()))
    deps = [] if dep is None else [dep]

    def body(a_ref, b_ref, *rest):
        o_ref = rest[len(deps)]
        prod = lax.dot_general(a_ref[...], b_ref[...], dims, preferred_element_type=f32)
        if nk == 1:
            o_ref[...] = prod.astype(out_dtype)
            return
        acc_ref = rest[len(deps) + 1]
        k = pl.program_id(2)

        @pl.when(k == 0)
        def _():
            acc_ref[...] = prod

        @pl.when((k > 0) & (k < nk - 1))
        def _():
            acc_ref[...] += prod

        @pl.when(k == nk - 1)
        def _():
            o_ref[...] = (acc_ref[...] + prod).astype(out_dtype)

    return pl.pallas_call(
        body, name=name, out_shape=out_shape, grid=(nm, nn, nk),
        in_specs=[a_spec, b_spec] + [ANY_SPEC] * len(deps), out_specs=o_spec,
        scratch_shapes=[pltpu.VMEM((tm, tn), f32)] if nk > 1 else [],
        compiler_params=_params("parallel", "parallel", "arbitrary"),
    )(a, b, *deps)


def _matmul_slabs_t(a_cols, a_slots, b, *, name, tm=512, tn=512, dep=None):
    M = a_cols.shape[0]
    n_slab, N, slab = b.shape
    n1, n2 = a_cols.shape[1] // slab, a_slots.shape[0]
    assert n1 + n2 == n_slab and a_slots.shape[1:] == (M, slab)
    tm, tn = min(tm, M), min(tn, N)
    deps = [] if dep is None else [dep]

    def body(a1_ref, a2_ref, b_ref, *rest):
        o_ref = rest[len(deps)]
        acc = None
        for s in range(n_slab):
            lhs = a1_ref[:, s * slab:(s + 1) * slab] if s < n1 else a2_ref[s - n1]
            prod = lax.dot_general(lhs, b_ref[s], (((1,), (1,)), ((), ())), preferred_element_type=f32)
            acc = prod if acc is None else acc + prod
        o_ref[...] = acc

    return pl.pallas_call(
        body, name=name, out_shape=jax.ShapeDtypeStruct((M, N), f32), grid=(M // tm, N // tn),
        in_specs=[pl.BlockSpec((tm, n1 * slab), lambda i, j: (i, 0)), pl.BlockSpec((n2, tm, slab), lambda i, j: (0, i, 0)),
                  pl.BlockSpec((n_slab, tn, slab), lambda i, j: (0, j, 0))] + [ANY_SPEC] * len(deps),
        out_specs=pl.BlockSpec((tm, tn), lambda i, j: (i, j)), compiler_params=_params("parallel", "parallel"),
    )(a_cols, a_slots, b, *deps)


def _ada_exchange(c_blk, cw_slab, w_ada, b_cols):
    nblk = c_blk.shape[0]
    D, W = w_ada.shape
    CW = cw_slab.shape[1]

    def body(c_ref, cw_ref, w_ref, b_ref, mod_ref, call_ref, cwg_ref, msend, send_sems, recv_sems):
        x, y, c, me = _my_place()
        call_ref[me] = _silu(c_ref[...])
        cwg_ref[me] = cw_ref[...]
        first = []
        for k in range(1, NDEV):
            peer, _ = _peer(x, y, c, k)
            first.append(_remote(call_ref.at[me], call_ref.at[me], send_sems.at[0, k], recv_sems.at[0, k], peer))
            first.append(_remote(cwg_ref.at[me], cwg_ref.at[me], send_sems.at[1, k], recv_sems.at[1, k], peer))
        for cp in first:
            cp.start()
        for k in range(1, NDEV):
            peer, slot = _peer(x, y, c, k)
            _remote(call_ref.at[slot], call_ref.at[slot], send_sems.at[0, k], recv_sems.at[0, k], peer).wait_recv()
            _remote(cwg_ref.at[slot], cwg_ref.at[slot], send_sems.at[1, k], recv_sems.at[1, k], peer).wait_recv()
        mod = jnp.broadcast_to(b_ref[...], (NDEV, W))
        for r in range(nblk):
            mod = mod + lax.dot_general(call_ref[:, r, :], w_ref[r * 128:(r + 1) * 128, :], (((1,), (0,)), ((), ())),
                                        preferred_element_type=f32, precision=lax.Precision.HIGHEST)
        row = lax.broadcasted_iota(jnp.int32, (NDEV, 1), 0)
        pick = lambda j: jnp.broadcast_to(jnp.sum(jnp.where(row == j, mod, 0.0), axis=0, keepdims=True), (8, W))
        mod_ref[me] = pick(me)
        second = []
        for k in range(1, NDEV):
            peer, slot = _peer(x, y, c, k)
            msend[k] = pick(slot)
            second.append(_remote(msend.at[k], mod_ref.at[me], send_sems.at[2, k], recv_sems.at[2, k], peer))
        for cp in second:
            cp.start()
        for k in range(1, NDEV):
            peer, slot = _peer(x, y, c, k)
            _remote(msend.at[k], mod_ref.at[slot], send_sems.at[2, k], recv_sems.at[2, k], peer).wait_recv()
        for cp in first + second:
            cp.wait_send()

    vmem = pl.BlockSpec(memory_space=pltpu.VMEM)
    return pl.pallas_call(
        body, name="ada_exchange",
        out_shape=(jax.ShapeDtypeStruct((NDEV, 8, W), f32), jax.ShapeDtypeStruct((NDEV, nblk, 128), f32),
                   jax.ShapeDtypeStruct((NDEV, 8, CW), f32)),
        in_specs=[vmem] * 4, out_specs=(vmem, vmem, vmem),
        scratch_shapes=[pltpu.VMEM((NDEV, 8, W), f32), pltpu.SemaphoreType.DMA((3, NDEV)),
                        pltpu.SemaphoreType.DMA((3, NDEV))],
        compiler_params=_params(),
    )(c_blk, cw_slab, w_ada, b_cols)


def _prenorm(x, scale, shift, g_pre, dep, tr=512):
    S, D = x.shape
    tr = min(tr, S)

    def body(x_ref, sc_ref, sh_ref, g_ref, dep_ref, h_ref):
        xv = x_ref[...]
        r = lax.rsqrt(jnp.mean(xv * xv, axis=-1, keepdims=True) + EPS)
        h_ref[...] = ((xv * r) * g_ref[...] * (1.0 + sc_ref[...]) + sh_ref[...]).astype(bf16)

    row = pl.BlockSpec((tr, D), lambda i: (i, 0))
    vec = pl.BlockSpec((1, D), lambda i: (0, 0))
    return pl.pallas_call(body, name="prenorm", out_shape=jax.ShapeDtypeStruct((S, D), bf16), grid=(S // tr,),
                          in_specs=[row, vec, vec, vec, ANY_SPEC], out_specs=row, compiler_params=_params("parallel"))(
                              x, scale, shift, g_pre, dep)


def _ext_rows(i, tr, S):
    g = lax.broadcasted_iota(jnp.int32, (tr + 16, 1), 0) + (i * tr - 8)
    return (g >= 0) & (g < S)


def _halo_specs(tr, S, C, col):
    nb8 = S // 8
    main = pl.BlockSpec((tr, C), lambda i: (i, col))
    prev = pl.BlockSpec((8, C), lambda i: (jnp.maximum(i * (tr // 8) - 1, 0), col))
    nxt = pl.BlockSpec((8, C), lambda i: (jnp.minimum((i + 1) * (tr // 8), nb8 - 1), col))
    return prev, main, nxt


def _conv_fwd(proj, conv_w, conv_b, g_conv, tr=512):
    S, C = proj.shape[0], proj.shape[1] // 8
    tr = min(tr, S)

    def body(up, um, un, cp, cm, cn, bg_ref, zc_ref, w_ref, cb_ref, g_ref, o_ref):
        i = pl.program_id(0)
        exists = _ext_rows(i, tr, S)
        u = jnp.concatenate([up[...], um[...], un[...]], axis=0)
        cg = jnp.concatenate([cp[...], cm[...], cn[...]], axis=0)
        t = jnp.where(exists, cg * u, 0.0)
        t_before = pltpu.roll(t, 1, 0)[8:tr + 8]
        t_after = pltpu.roll(t, tr + 15, 0)[8:tr + 8]
        w = w_ref[...]
        cv = w[0:1] * t_before + w[1:2] * t[8:tr + 8] + w[2:3] * t_after + cb_ref[...]
        yc = bg_ref[...] * cv
        rc = lax.rsqrt(jnp.mean(yc * yc, axis=-1, keepdims=True) + EPS)
        o_ref[...] = ((yc * rc) * g_ref[...] * _silu(zc_ref[...])).astype(bf16)

    u_specs = _halo_specs(tr, S, C, 0)
    c_specs = _halo_specs(tr, S, C, 2)
    vec = pl.BlockSpec((1, C), lambda i: (0, 0))
    return pl.pallas_call(
        body, name="conv_fwd", out_shape=jax.ShapeDtypeStruct((S, 2 * C), bf16), grid=(S // tr,),
        in_specs=[*u_specs, *c_specs, pl.BlockSpec((tr, C), lambda i: (i, 1)), pl.BlockSpec((tr, C), lambda i: (i, 3)),
                  pl.BlockSpec((8, C), lambda i: (0, 0)), vec, vec],
        out_specs=pl.BlockSpec((tr, C), lambda i: (i, 0)), compiler_params=_params("parallel"),
    )(proj, proj, proj, proj, proj, proj, proj, proj, conv_w, conv_b, g_conv)


def _branch_geometry(S, r, inter):
    L = S // r * inter
    nq = min(128, L)
    nk = min(nq + 2 * HALF_WIN * inter, L)
    assert L % nq == 0 and (L == nk or L >= nq + 2 * HALF_WIN * inter)
    return L, nq, nk, L // nq


QUAD = 4


def _to_quad(dst, src, S):
    n = S // QUAD
    for rho in range(QUAD):
        dst[pl.ds(rho * n, n), :] = src[pl.ds(rho, n, stride=QUAD), :]


def _block_rows(idx, r, inter, S, L, nq, nk, nblk):
    rho, qb = (0, idx) if r == 1 else (idx // nblk, idx % nblk)
    i0 = qb * nq
    ws = jnp.clip(i0 - HALF_WIN * inter, 0, L - nk)
    if r == 1:
        return pl.ds(pl.multiple_of(i0, 8), nq), pl.ds(pl.multiple_of(ws, 8), nk), i0 - ws
    assert r % (QUAD * inter) == 0
    step = r // QUAD // inter
    base = (rho % QUAD) * (S // QUAD) + rho // QUAD
    if step == 1:
        return pl.ds(pl.multiple_of(base + i0, 8), nq), pl.ds(pl.multiple_of(base + ws, 8), nk), i0 - ws
    return pl.ds(base + step * i0, nq, stride=step), pl.ds(base + step * ws, nk, stride=step), i0 - ws


N_CASES = 3
SCALE = HEAD_DIM ** -0.5
ATTN_UNROLL = 16


def _bias_shape(S):
    shapes = [_branch_geometry(S, r, inter)[1:3] for _, r, inter in BRANCHES]
    return (len(BRANCHES) * N_CASES * 2, max(nq for nq, _ in shapes), max(nk for _, nk in shapes))


def _bias_index(b, case, head):
    return (b * N_CASES + case) * 2 + head


def _fill_bias(bias_scr, sl_ref, S):
    sl = sl_ref[...]
    slope = (sl[0:1, 0:1], sl[0:1, HEAD_DIM:HEAD_DIM + 1])
    for b, (_, r, inter) in enumerate(BRANCHES):
        L, nq, nk, nblk = _branch_geometry(S, r, inter)
        rel = lax.broadcasted_iota(jnp.int32, (nq, nk), 0) - lax.broadcasted_iota(jnp.int32, (nq, nk), 1)
        for case in range(N_CASES):
            d = jnp.abs(rel + case * HALF_WIN)
            valid = d <= HALF_WIN * inter
            if inter > 1:
                valid = valid & (jnp.bitwise_and(d, inter - 1) == 0)
            dist = d.astype(f32) * float(r // inter)
            for head in range(2):
                bias_scr[_bias_index(b, case, head), 0:nq, 0:nk] = jnp.where(valid, -slope[head] * dist, NEG_INF)


def _head_slopes(n_heads):
    slopes = 2.0 ** (-8.0 * jnp.arange(1, n_heads + 1, dtype=f32) / n_heads)
    return jnp.broadcast_to(jnp.repeat(slopes.reshape(n_heads // 2, 2), HEAD_DIM, axis=1)[:, None, :],
                            (n_heads // 2, 8, PAIR))


def _attn_fwd(proj, slopes):
    S, C = proj.shape[0], proj.shape[1] // 8
    npair = C // PAIR

    def body(q_ref, k_ref, v_ref, sl_ref, o_ref, lse_ref, m_scr, l_scr, a_scr, bias_scr, q4_scr, k4_scr, v4_scr):
        lane = lax.broadcasted_iota(jnp.int32, (1, PAIR), 1)
        first = lane < HEAD_DIM
        _fill_bias(bias_scr, sl_ref, S)
        for dst, src in ((q4_scr, q_ref), (k4_scr, k_ref), (v4_scr, v_ref)):
            _to_quad(dst, src, S)

        for b, (_, r, inter) in enumerate(BRANCHES):
            L, nq, nk, nblk = _branch_geometry(S, r, inter)
            qs, ks, vs = (q_ref, k_ref, v_ref) if r == 1 else (q4_scr, k4_scr, v4_scr)

            def step(idx, carry, b=b, r=r, L=L, nq=nq, nk=nk, nblk=nblk, qs=qs, ks=ks, vs=vs):
                qrows, krows, off = _block_rows(idx, r, inter, S, L, nq, nk, nblk)
                case = off // HALF_WIN
                q2 = qs[qrows, :] * SCALE
                k2 = ks[krows, :].astype(bf16)
                v2 = vs[krows, :].astype(bf16)
                ms, accs = [], []
                for hh in range(2):
                    mine = first if hh == 0 else ~first
                    qh = jnp.where(mine, q2, 0.0).astype(bf16)
                    s = lax.dot_general(qh, k2, (((1,), (1,)), ((), ())), preferred_element_type=f32)
                    s = s + bias_scr[_bias_index(b, case, hh), 0:nq, 0:nk]
                    m = jnp.max(s, axis=-1, keepdims=True)
                    p = jnp.exp(s - m).astype(bf16)
                    vh = jnp.where(mine, v2, jnp.ones_like(v2))
                    ms.append(m)
                    accs.append(jnp.dot(p, vh, preferred_element_type=f32))
                m_scr[b, qrows, :] = jnp.where(first, ms[0], ms[1])
                a_scr[b, qrows, :] = jnp.where(first, accs[0], accs[1])
                l_scr[b, qrows, :] = jnp.where(first, accs[1], accs[0])
                return carry

            lax.fori_loop(0, S // nq, step, 0, unroll=min(ATTN_UNROLL, S // nq))

        n4 = S // QUAD
        ch = min(256, n4)
        nch = n4 // ch

        def merge(i, carry):
            rho, part = i // nch, i % nch
            sorted_rows = pl.ds(pl.multiple_of(rho * n4 + part * ch, 8), ch)
            token_rows = pl.ds(rho + QUAD * part * ch, ch, stride=QUAD)
            rows = (token_rows,) + (sorted_rows,) * (len(BRANCHES) - 1)
            ms = [m_scr[b, rows[b], :] for b in range(len(BRANCHES))]
            m = functools.reduce(jnp.maximum, ms)
            l = jnp.zeros((ch, PAIR), f32)
            acc = jnp.zeros((ch, PAIR), f32)
            for b in range(len(BRANCHES)):
                w = jnp.exp(ms[b] - m)
                l = l + w * pltpu.roll(l_scr[b, rows[b], :], HEAD_DIM, 1)
                acc = acc + w * a_scr[b, rows[b], :]
            o_ref[token_rows, :] = acc / l
            lse_ref[token_rows, :] = m + jnp.log(l)
            return carry

        lax.fori_loop(0, QUAD * nch, merge, 0, unroll=2)

    blk = lambda part: pl.BlockSpec((S, PAIR), lambda p: (0, part * npair + p))
    out = pl.BlockSpec((S, PAIR), lambda p: (0, p))
    return pl.pallas_call(
        body, name="attn_fwd",
        out_shape=(jax.ShapeDtypeStruct((S, C), f32), jax.ShapeDtypeStruct((S, C), f32)), grid=(npair,),
        in_specs=[blk(4), blk(5), blk(6), pl.BlockSpec((None, 8, PAIR), lambda p: (p, 0, 0))],
        out_specs=(out, out),
        scratch_shapes=[pltpu.VMEM((3, S, PAIR), f32)] * 3 + [pltpu.VMEM(_bias_shape(S), f32)]
        + [pltpu.VMEM((S, PAIR), f32)] * 3,
        compiler_params=_params("parallel"),
    )(proj, proj, proj, slopes)


def _attn_post(ycat, o, proj, g_attn, tr=512):
    S, C = o.shape
    tr = min(tr, S)

    def body(y_ref, o_ref, z_ref, g_ref, out_ref):
        del y_ref
        ov = o_ref[...]
        ra = lax.rsqrt(jnp.mean(ov * ov, axis=-1, keepdims=True) + EPS)
        out_ref[...] = ((ov * ra) * g_ref[...] * _silu(z_ref[...])).astype(bf16)

    return pl.pallas_call(
        body, name="attn_post", out_shape=jax.ShapeDtypeStruct(ycat.shape, ycat.dtype), grid=(S // tr,),
        in_specs=[HBM_SPEC, pl.BlockSpec((tr, C), lambda i: (i, 0)), pl.BlockSpec((tr, C), lambda i: (i, 7)),
                  pl.BlockSpec((1, C), lambda i: (0, 0))],
        out_specs=pl.BlockSpec((tr, C), lambda i: (i, 1)), input_output_aliases={0: 0},
        compiler_params=_params("arbitrary"),
    )(ycat, o, proj, g_attn)


def _residual_minus_target(x, target, dep, tr=512):
    S, D = x.shape
    tr = min(tr, S)

    def body(x_ref, t_ref, dep_ref, o_ref):
        o_ref[...] = x_ref[...] - t_ref[...]

    row = pl.BlockSpec((tr, D), lambda i: (i, 0))
    return pl.pallas_call(body, name="residual_minus_target", out_shape=jax.ShapeDtypeStruct((S, D), f32),
                          grid=(S // tr,), in_specs=[row, row, ANY_SPEC], out_specs=row,
                          compiler_params=_params("parallel"))(x, target, dep)


def _sandwich(y, x_minus_t, gate, g_post, tr=256):
    S, D = y.shape
    tr = min(tr, S)

    def body(y_ref, xt_ref, gate_ref, g_ref, dy_ref, dout_ref, sums_ref):
        i = pl.program_id(0)
        gate, g = gate_ref[...], g_ref[...]
        gg = gate * g
        yv = y_ref[...]
        rp = lax.rsqrt(jnp.mean(yv * yv, axis=-1, keepdims=True) + EPS)
        yhat = yv * rp
        err = xt_ref[...] + gg * yhat
        dout = err * (1.0 / D)
        dout_ref[...] = dout
        q = dout * yhat
        w = dout * gg
        dy_ref[...] = (rp * (w - yhat * jnp.sum(q * gg, axis=-1, keepdims=True) * (1.0 / D))).astype(bf16)
        loss = 0.5 * jnp.sum(jnp.mean(err * err, axis=-1, keepdims=True), axis=0, keepdims=True)
        q_sum = jnp.sum(q, axis=0, keepdims=True)
        row = lax.broadcasted_iota(jnp.int32, (8, D), 0)
        upd = jnp.where(row == 0, q_sum * g, jnp.where(row == 1, q_sum * gate, jnp.where(row == 2, loss, 0.0)))

        @pl.when(i == 0)
        def _():
            sums_ref[...] = upd

        @pl.when(i > 0)
        def _():
            sums_ref[...] += upd

    row = pl.BlockSpec((tr, D), lambda i: (i, 0))
    vec = pl.BlockSpec((1, D), lambda i: (0, 0))
    return pl.pallas_call(
        body, name="sandwich",
        out_shape=(jax.ShapeDtypeStruct((S, D), bf16), jax.ShapeDtypeStruct((S, D), f32), jax.ShapeDtypeStruct((8, D), f32)),
        grid=(S // tr,), in_specs=[row, row, vec, vec],
        out_specs=(row, row, pl.BlockSpec((8, D), lambda i: (0, 0))), compiler_params=_params("arbitrary"),
    )(y, x_minus_t, gate, g_post)


def _conv_bwd(proj, dycat, conv_w, conv_b, g_conv, dep, tr=256):
    S, C = proj.shape[0], proj.shape[1] // 8
    tr = min(tr, S)
    n = tr + 16

    def body(*refs):
        ins, (w_ref, cb_ref, g_ref, _, dp_ref, sums_ref) = refs[:15], refs[15:]
        i = pl.program_id(0)
        exists = _ext_rows(i, tr, S)
        u, bg, cg, zc, dyn = (jnp.concatenate([ins[3 * t][...], ins[3 * t + 1][...], ins[3 * t + 2][...]], axis=0)
                              for t in range(5))
        w = w_ref[...]
        t = jnp.where(exists, cg * u, 0.0)
        t_before, t_after = pltpu.roll(t, 1, 0), pltpu.roll(t, n - 1, 0)
        cv = w[0:1] * t_before + w[1:2] * t + w[2:3] * t_after + cb_ref[...]
        yc = bg * cv
        rc = lax.rsqrt(jnp.mean(yc * yc, axis=-1, keepdims=True) + EPS)
        yhat = yc * rc
        sz, dsz = _silu_and_slope(zc)
        wgt = dyn * g_ref[...] * sz
        dyc = rc * (wgt - yhat * jnp.mean(wgt * yhat, axis=-1, keepdims=True))
        dcv = jnp.where(exists, dyc * bg, 0.0)
        dt = w[0:1] * pltpu.roll(dcv, n - 1, 0) + w[1:2] * dcv + w[2:3] * pltpu.roll(dcv, 1, 0)
        mid = slice(8, tr + 8)
        dp_ref[:, 0:C] = (dt * cg)[mid].astype(bf16)
        dp_ref[:, C:2 * C] = (dyc * cv)[mid].astype(bf16)
        dp_ref[:, 2 * C:3 * C] = (dt * u)[mid].astype(bf16)
        dp_ref[:, 3 * C:4 * C] = (dyn * yhat * g_ref[...] * dsz)[mid].astype(bf16)
        colsum = lambda v: jnp.sum(v[mid], axis=0, keepdims=True)
        parts = [colsum(dyn * yhat * sz), colsum(dcv), colsum(dcv * t_before), colsum(dcv * t), colsum(dcv * t_after)]
        row = lax.broadcasted_iota(jnp.int32, (8, C), 0)
        upd = jnp.zeros((8, C), f32)
        for j, pj in enumerate(parts):
            upd = jnp.where(row == j, pj, upd)

        @pl.when(i == 0)
        def _():
            sums_ref[...] = upd

        @pl.when(i > 0)
        def _():
            sums_ref[...] += upd

    specs = []
    for col in range(4):
        specs += _halo_specs(tr, S, C, col)
    specs += _halo_specs(tr, S, C, 0)
    vec = pl.BlockSpec((1, C), lambda i: (0, 0))
    return pl.pallas_call(
        body, name="conv_bwd",
        out_shape=(jax.ShapeDtypeStruct((S, 4 * C), bf16), jax.ShapeDtypeStruct((8, C), f32)), grid=(S // tr,),
        in_specs=[*specs, pl.BlockSpec((8, C), lambda i: (0, 0)), vec, vec, ANY_SPEC],
        out_specs=(pl.BlockSpec((tr, 4 * C), lambda i: (i, 0)), pl.BlockSpec((8, C), lambda i: (0, 0))),
        compiler_params=_params("arbitrary"),
    )(*([proj] * 12), dycat, dycat, dycat, conv_w, conv_b, g_conv, dep)


def _attn_post_bwd(o, proj, dycat, g_attn, dep, tr=512):
    S, C = o.shape
    tr = min(tr, S)

    def body(o_ref, z_ref, dy_ref, g_ref, dep_ref, do_ref, dz_ref, sums_ref):
        i = pl.program_id(0)
        ov, zv, dyn = o_ref[...], z_ref[...], dy_ref[...]
        ra = lax.rsqrt(jnp.mean(ov * ov, axis=-1, keepdims=True) + EPS)
        ohat = ov * ra
        sz, dsz = _silu_and_slope(zv)
        wgt = dyn * g_ref[...] * sz
        do_ref[...] = ra * (wgt - ohat * jnp.mean(wgt * ohat, axis=-1, keepdims=True))
        dz_ref[...] = (dyn * ohat * g_ref[...] * dsz).astype(bf16)
        row = lax.broadcasted_iota(jnp.int32, (8, C), 0)
        upd = jnp.where(row == 0, jnp.sum(dyn * ohat * sz, axis=0, keepdims=True), 0.0)

        @pl.when(i == 0)
        def _():
            sums_ref[...] = upd

        @pl.when(i > 0)
        def _():
            sums_ref[...] += upd

    return pl.pallas_call(
        body, name="attn_post_bwd",
        out_shape=(jax.ShapeDtypeStruct((S, C), f32), jax.ShapeDtypeStruct((4, S, C), bf16),
                   jax.ShapeDtypeStruct((8, C), f32)),
        grid=(S // tr,),
        in_specs=[pl.BlockSpec((tr, C), lambda i: (i, 0)), pl.BlockSpec((tr, C), lambda i: (i, 7)),
                  pl.BlockSpec((tr, C), lambda i: (i, 1)), pl.BlockSpec((1, C), lambda i: (0, 0)), ANY_SPEC],
        out_specs=(pl.BlockSpec((tr, C), lambda i: (i, 0)), pl.BlockSpec((None, tr, C), lambda i: (3, i, 0)),
                   pl.BlockSpec((8, C), lambda i: (0, 0))),
        compiler_params=_params("arbitrary"),
    )(o, proj, dycat, g_attn, dep)


def _attn_bwd(proj, o, do, lse, slopes, dqkvz, dep):
    S, C = o.shape
    npair = C // PAIR

    def body(q_ref, k_ref, v_ref, o_ref, do_ref, lse_ref, sl_ref, old_ref, dep_ref, dqkv_ref,
             acc_scr, dl_scr, quad_scr, bias_scr):
        lane = lax.broadcasted_iota(jnp.int32, (1, PAIR), 1)
        first = lane < HEAD_DIM
        _fill_bias(bias_scr, sl_ref, S)
        ch = min(256, S)

        def prep(i, carry):
            rows = pl.ds(pl.multiple_of(i * ch, 8), ch)
            prod = do_ref[rows, :] * o_ref[rows, :]
            d0 = jnp.sum(jnp.where(first, prod, 0.0), axis=-1, keepdims=True)
            d1 = jnp.sum(jnp.where(first, 0.0, prod), axis=-1, keepdims=True)
            dl_scr[rows, :] = jnp.where(first, d0, d1)
            zero = jnp.zeros((ch, PAIR), f32)
            for order in range(2):
                for t in range(3):
                    acc_scr[order, t, rows, :] = zero
            return carry

        lax.fori_loop(0, S // ch, prep, 0, unroll=2)
        token_srcs = (q_ref, k_ref, v_ref, do_ref, lse_ref, dl_scr)
        for j, src in enumerate(token_srcs):
            _to_quad(quad_scr.at[j], src, S)

        for b, (_, r, inter) in enumerate(BRANCHES):
            L, nq, nk, nblk = _branch_geometry(S, r, inter)
            order = 0 if r == 1 else 1
            srcs = token_srcs if r == 1 else tuple(quad_scr.at[j] for j in range(6))

            def step(idx, carry, b=b, r=r, L=L, nq=nq, nk=nk, nblk=nblk, order=order, srcs=srcs):
                qs, ks, vs, dos, lses, dls = srcs
                dq_scr, dk_scr, dv_scr = (acc_scr.at[order, t] for t in range(3))
                qrows, krows, off = _block_rows(idx, r, inter, S, L, nq, nk, nblk)
                case = off // HALF_WIN
                q2 = qs[qrows, :] * SCALE
                k2 = ks[krows, :].astype(bf16)
                v2 = vs[krows, :].astype(bf16)
                do2 = dos[qrows, :]
                lse2 = lses[qrows, :]
                dl2 = dls[qrows, :]
                dq2 = jnp.zeros((nq, PAIR), f32)
                dk2 = jnp.zeros((nk, PAIR), f32)
                dv2 = jnp.zeros((nk, PAIR), f32)
                for hh in range(2):
                    mine = first if hh == 0 else ~first
                    lo = hh * HEAD_DIM
                    qh = jnp.where(mine, q2, 0.0).astype(bf16)
                    doh = jnp.where(mine, do2, 0.0).astype(bf16)
                    s = lax.dot_general(qh, k2, (((1,), (1,)), ((), ())), preferred_element_type=f32)
                    s = s + bias_scr[_bias_index(b, case, hh), 0:nq, 0:nk]
                    p = jnp.exp(s - lse2[:, lo:lo + 1])
                    dv2 = dv2 + lax.dot_general(p.astype(bf16), doh, (((0,), (0,)), ((), ())), preferred_element_type=f32)
                    dp = lax.dot_general(doh, v2, (((1,), (1,)), ((), ())), preferred_element_type=f32)
                    ds = (p * (dp - dl2[:, lo:lo + 1])).astype(bf16)
                    dq2 = dq2 + jnp.where(mine, jnp.dot(ds, k2, preferred_element_type=f32), 0.0)
                    dk2 = dk2 + lax.dot_general(ds, qh, (((0,), (0,)), ((), ())), preferred_element_type=f32)
                dq_scr[qrows, :] = dq_scr[qrows, :] + dq2
                dk_scr[krows, :] = dk_scr[krows, :] + dk2
                dv_scr[krows, :] = dv_scr[krows, :] + dv2
                return carry

            lax.fori_loop(0, S // nq, step, 0, unroll=min(ATTN_UNROLL, S // nq))

        n4 = S // QUAD
        for t in range(3):
            for rho in range(QUAD):
                token_rows = pl.ds(rho, n4, stride=QUAD)
                acc_scr[0, t, token_rows, :] = acc_scr[0, t, token_rows, :] + acc_scr[1, t, pl.ds(rho * n4, n4), :]
        dqkv_ref[0] = (acc_scr[0, 0] * SCALE).astype(bf16)
        dqkv_ref[1] = acc_scr[0, 1].astype(bf16)
        dqkv_ref[2] = acc_scr[0, 2].astype(bf16)

    blk = lambda part: pl.BlockSpec((S, PAIR), lambda p: (0, part * npair + p))
    own = pl.BlockSpec((S, PAIR), lambda p: (0, p))
    return pl.pallas_call(
        body, name="attn_bwd", out_shape=jax.ShapeDtypeStruct(dqkvz.shape, dqkvz.dtype), grid=(npair,),
        in_specs=[blk(4), blk(5), blk(6), own, own, own, pl.BlockSpec((None, 8, PAIR), lambda p: (p, 0, 0)),
                  ANY_SPEC, ANY_SPEC],
        out_specs=pl.BlockSpec((3, S, PAIR), lambda p: (0, 0, p)), input_output_aliases={7: 0},
        scratch_shapes=[pltpu.VMEM((2, 3, S, PAIR), f32), pltpu.VMEM((S, PAIR), f32), pltpu.VMEM((6, S, PAIR), f32),
                        pltpu.VMEM(_bias_shape(S), f32)],
        compiler_params=_params("arbitrary"),
    )(proj, proj, proj, o, do, lse, slopes, dqkvz, dep)


def _prenorm_bwd(dh, x, dout, scale, g_pre, tr=256):
    S, D = x.shape
    tr = min(tr, S)

    def body(dh_ref, x_ref, dout_ref, sc_ref, g_ref, gx_ref, sums_ref):
        i = pl.program_id(0)
        xv, dhv = x_ref[...], dh_ref[...]
        r = lax.rsqrt(jnp.mean(xv * xv, axis=-1, keepdims=True) + EPS)
        xn = xv * r
        dxn = dhv * (g_ref[...] * (1.0 + sc_ref[...]))
        gx_ref[...] = dout_ref[...] + r * (dxn - xn * jnp.mean(dxn * xn, axis=-1, keepdims=True))
        dhx = dhv * xn
        row = lax.broadcasted_iota(jnp.int32, (8, D), 0)
        upd = jnp.where(row == 0, jnp.sum(dhv, axis=0, keepdims=True),
                        jnp.where(row == 1, jnp.sum(dhx, axis=0, keepdims=True) * g_ref[...],
                                  jnp.where(row == 2, jnp.sum(dhx, axis=0, keepdims=True) * (1.0 + sc_ref[...]), 0.0)))

        @pl.when(i == 0)
        def _():
            sums_ref[...] = upd

        @pl.when(i > 0)
        def _():
            sums_ref[...] += upd

    row = pl.BlockSpec((tr, D), lambda i: (i, 0))
    vec = pl.BlockSpec((1, D), lambda i: (0, 0))
    return pl.pallas_call(
        body, name="prenorm_bwd",
        out_shape=(jax.ShapeDtypeStruct((S, D), f32), jax.ShapeDtypeStruct((8, D), f32)), grid=(S // tr,),
        in_specs=[row, row, row, vec, vec], out_specs=(row, pl.BlockSpec((8, D), lambda i: (0, 0))),
        compiler_params=_params("arbitrary"),
    )(dh, x, dout, scale, g_pre)


def _adamw(w, g, m, v):
    m = ADAM_B1 * m + (1.0 - ADAM_B1) * g
    v = ADAM_B2 * v + (1.0 - ADAM_B2) * (g * g)
    m_hat = m / (1.0 - ADAM_B1 ** ADAM_STEP)
    v_hat = v / (1.0 - ADAM_B2 ** ADAM_STEP)
    delta = -ADAM_LR * (m_hat / (jnp.sqrt(v_hat) + ADAM_EPS) + ADAM_WD * w)
    return delta, m, v


def _sum_rows(parts, dep):
    P = parts.shape[1]

    def body(p_ref, dep_ref, o_ref):
        acc = p_ref[0:1, :]
        for j in range(1, NDEV):
            acc = acc + p_ref[j:j + 1, :]
        o_ref[...] = jnp.broadcast_to(acc, (8, P))

    vmem = pl.BlockSpec(memory_space=pltpu.VMEM)
    return pl.pallas_call(body, name="sum_small", out_shape=jax.ShapeDtypeStruct((8, P), f32),
                          in_specs=[vmem, ANY_SPEC], out_specs=vmem, compiler_params=_params())(parts, dep)


def _adamw_small(tot, params):
    given = [p[3] for p in params if not isinstance(p[3], int)]

    def body(tot_ref, *refs):
        given_refs = list(refs[:len(given)])
        ins = refs[len(given):len(given) + 3 * len(params)]
        outs = refs[len(given) + 3 * len(params):]
        for t, (w, _, _, where) in enumerate(params):
            w_ref, m_ref, v_ref = ins[3 * t:3 * t + 3]
            g = tot_ref[0:1, where:where + w.size] if isinstance(where, int) else given_refs.pop(0)[...]
            outs[4 * t][...] = g
            outs[4 * t + 1][...], outs[4 * t + 2][...], outs[4 * t + 3][...] = _adamw(w_ref[...], g, m_ref[...], v_ref[...])

    out_shape = tuple(jax.ShapeDtypeStruct(p[0].shape, f32) for p in params for _ in range(4))
    res = pl.pallas_call(body, name="adamw_small", out_shape=out_shape, compiler_params=_params())(
        tot, *given, *[a for p in params for a in p[:3]])
    return [res[4 * t:4 * t + 4] for t in range(len(params))]


def _adamw_sharded(name, parts, sums_a, sums_b, pick, w, m, v, rows=None, prev=None, tr=128):
    R, Cc = w.shape
    r0, nr = rows or (0, R)
    tr = math.gcd(tr, r0, nr)
    n, b0 = parts.shape[0], r0 // tr

    def body(pick_ref, p_ref, a_ref, b_ref, w_ref, m_ref, v_ref, *rest):
        g_ref, d_ref, nm_ref, nv_ref = rest[-4:]
        g = jnp.where(pick_ref[0] == 1, b_ref[...], a_ref[...]).astype(f32)
        for j in range(n):
            g = g + p_ref[j].astype(f32)
        g_ref[...] = g
        d_ref[...], nm_ref[...], nv_ref[...] = _adamw(w_ref[...], g, m_ref[...], v_ref[...])

    row = pl.BlockSpec((tr, Cc), lambda i, pick: (i + b0, 0))
    mine = pl.BlockSpec((None, tr, Cc), lambda i, pick: (pick[1], i + b0, 0))
    out = jax.ShapeDtypeStruct((R, Cc), f32)
    prev = list(prev or [])
    grid_spec = pltpu.PrefetchScalarGridSpec(
        num_scalar_prefetch=1, grid=(nr // tr,),
        in_specs=[pl.BlockSpec((n, tr, Cc), lambda i, pick: (0, i + b0, 0)), mine, mine, row, row, row]
        + [ANY_SPEC] * len(prev),
        out_specs=(row, row, row, row))
    return pl.pallas_call(
        body, name=name, out_shape=(out, out, out, out), grid_spec=grid_spec,
        input_output_aliases={7 + t: t for t in range(len(prev))}, compiler_params=_params("arbitrary"),
    )(pick, parts, sums_a, sums_b, w, m, v, *prev)


def _adamw_ada(c_t, dmod_cols, w, m, v, dep, tr=512):
    D, W = w.shape
    tr = min(tr, D)

    def body(c_ref, dm_ref, w_ref, m_ref, v_ref, dep_ref, g_ref, d_ref, nm_ref, nv_ref):
        cv, dm = c_ref[...], dm_ref[...]
        g = cv[:, 0:1] * dm[0:1, :]
        for b in range(1, NDEV):
            g = g + cv[:, b:b + 1] * dm[b:b + 1, :]
        g_ref[...] = g
        d_ref[...], nm_ref[...], nv_ref[...] = _adamw(w_ref[...], g, m_ref[...], v_ref[...])

    row = pl.BlockSpec((tr, W), lambda i: (i, 0))
    out = jax.ShapeDtypeStruct((D, W), f32)
    return pl.pallas_call(
        body, name="adamw_ada", out_shape=(out, out, out, out), grid=(D // tr,),
        in_specs=[pl.BlockSpec((tr, NDEV), lambda i: (i, 0)), pl.BlockSpec((NDEV, W), lambda i: (0, 0)), row, row, row,
                  ANY_SPEC],
        out_specs=(row, row, row, row), compiler_params=_params("parallel"),
    )(c_t, dmod_cols, w, m, v, dep)


def kernel(x, c, w_ada, b_ada, g_pre, w_in, conv_w, conv_b, g_conv, g_attn, w_out, g_post, loss_target, m_w_ada, m_b_ada, m_g_pre, m_w_in, m_conv_w, m_conv_b, m_g_conv, m_g_attn, m_w_out, m_g_post, v_w_ada, v_b_ada, v_g_pre, v_w_in, v_conv_w, v_conv_b, v_g_conv, v_g_attn, v_w_out, v_g_post):
    S, D = x.shape[1], x.shape[2]
    C = D // 2
    W = w_ada.shape[2]
    CW = conv_w.shape[2]
    me = 4 * lax.axis_index("x") + 2 * lax.axis_index("y") + lax.axis_index("c")
    x2, tgt = x[0], loss_target[0]
    w_ada2, w_in2, w_out2 = w_ada[0], w_in[0], w_out[0]

    R = D // NDEV
    core = lax.axis_index("c").astype(jnp.int32).reshape(1)

    cw_slab = jnp.zeros((8, CW), f32).at[:3].set(conv_w[0])
    b_cols = lax.dynamic_slice_in_dim(b_ada, me * W, W, axis=1)
    mod_slabs, c_blocks, cw_g = _ada_exchange(c.reshape(D // 128, 128), cw_slab, w_ada2, b_cols)
    c_all = c_blocks.reshape(NDEV, D)
    conv_w_full = jnp.transpose(cw_g, (1, 0, 2)).reshape(8, C)
    mod = mod_slabs[:, 0, :].reshape(1, 3 * D)
    shift, scale, gate = mod[:, :D], mod[:, D:2 * D], mod[:, 2 * D:]

    land_i = lax.dynamic_update_slice(lax.empty((NDEV, D, C), bf16), w_in2.astype(bf16)[None], (me, 0, 0))
    land_o = lax.dynamic_update_slice(lax.empty((NDEV, R, D), bf16), w_out2.astype(bf16)[None], (me, 0, 0))
    wi_send, wi_recv, land_i, w_token = _w_in_start(land_i, [mod_slabs])

    me_arr = me.astype(jnp.int32).reshape(1)
    h = _prenorm(x2, scale, shift, g_pre, w_token)
    land_i = _w_in_sibling(land_i, wi_recv, after=[h])
    proj = _in_proj_part("in_proj_a", h, land_i, None, me_arr, 0, 1, 2)
    x_minus_t = _residual_minus_target(x2, tgt, proj)
    chip = me // 2
    landing = lambda rows, cols: lax.dynamic_update_slice(
        lax.empty((NCHIP, rows, cols), bf16), jnp.zeros((1, rows, cols), bf16), (chip, 0, 0))
    land_go, land_gi = landing(R, D), landing(D, C)
    fi_send, fi_recv, land_i = _w_in_relay(land_i, wi_recv, after=[proj, x_minus_t, land_go, land_gi])
    proj = _in_proj_part("in_proj_b", h, land_i, proj, me_arr, 2, 2, 2)
    land_i = _w_in_forwarded(land_i, fi_recv, after=[proj])
    proj = _in_proj_part("in_proj_c", h, land_i, proj, me_arr, 3, 2, 2)
    (di_send, di_recv, wo_send, wo_recv), land_i, land_o = _w_in_diag(land_i, land_o, fi_recv, after=[proj])
    proj = _in_proj_part("in_proj_d", h, land_i, proj, me_arr, 6, 1, 1)
    win_g = _w_in_finish(land_i, wi_send, fi_send, di_send, di_recv, after=[proj])
    proj = _in_proj_part("in_proj_e", h, win_g, proj, me_arr, 7, 1, 1)
    slopes = _head_slopes(C // HEAD_DIM)
    ycat = _conv_fwd(proj, conv_w_full, conv_b, g_conv)
    o, lse = _attn_fwd(proj, slopes)
    (fo_send, fo_recv), (land_o,), _ = _weights_forward("w_out_forward", land_o, wo_recv, after=[o])
    ycat = _attn_post(ycat, o, proj, g_attn)
    wout_g = _weights_wait("w_out_wait", land_o, wo_send, wo_recv, fo_send, fo_recv, after=[ycat])
    wout_full = wout_g.reshape(D, D)
    y = _matmul(ycat, wout_full, name="out_proj", out_dtype=f32)
    dy, dout, post_sums = _sandwich(y, x_minus_t, gate, g_post)

    gw_out = _matmul(ycat, dy, name="out_proj_dw", out_dtype=bf16, ta=True).reshape(NDEV, R, D)
    po_send, po_recv, gw_out, pair_o, po_token = _pair_start("g_out_pair_start", gw_out)
    dycat = _matmul(dy, wout_full, name="out_proj_dx", out_dtype=f32, tb=True, dep=po_token)
    gw_out, pair_o = _pair_wait("g_out_pair_wait", gw_out, pair_o, po_send, po_recv, after=[dycat])
    sum_o = _pair_sum("g_out_pair_sum", gw_out, pair_o, core)
    co_send, co_recv, sum_o, land_go, co_token = _chip_start(
        "g_out_chip_start", sum_o, land_go, 0)
    dpc, conv_sums = _conv_bwd(proj, dycat, conv_w_full, conv_b, g_conv, co_token)
    gw_c = _matmul(h, dpc, name="in_proj_dw_conv", out_dtype=bf16, ta=True, out_slots=4)
    pc_send, pc_recv, gw_c, pair_c, pc_token = _pair_start("g_conv_pair_start", gw_c)
    do, dpa, attn_sums = _attn_post_bwd(o, proj, dycat, g_attn, pc_token)
    gw_c, pair_c = _pair_wait("g_conv_pair_wait", gw_c, pair_c, pc_send, pc_recv, after=[do])
    sum_c = _pair_sum("g_conv_pair_sum", gw_c, pair_c, core)
    cc_send, cc_recv, sum_c, land_gi, cc_token = _chip_start(
        "g_conv_chip_start", sum_c, land_gi, 0)
    dpa = _attn_bwd(proj, o, do, lse, slopes, dpa, cc_token)
    gw_a = _matmul(h, dpa, name="in_proj_dw_attn", out_dtype=bf16, ta=True, b_slots=True, out_slots=4)
    pa_send, pa_recv, gw_a, pair_a, pa_token = _pair_start("g_attn_pair_start", gw_a)
    sum_o, land_go = _chip_wait("g_out_chip_wait", sum_o, land_go, co_send, co_recv, 0, after=[pa_token])
    pick_out = jnp.stack([jnp.int32(0), me // 2]).astype(jnp.int32)
    g_w_out, d_w_out, nm_w_out, nv_w_out = _adamw_sharded(
        "adamw_w_out", land_go, sum_o, sum_o, pick_out, w_out2, m_w_out[0], v_w_out[0])
    gw_a, pair_a = _pair_wait("g_attn_pair_wait", gw_a, pair_a, pa_send, pa_recv, after=[g_w_out])
    sum_a = _pair_sum("g_attn_pair_sum", gw_a, pair_a, core)
    part_a, part_b = (0, 3 * D // 4), (3 * D // 4, D // 4)
    ca_send, ca_recv, sum_a, land_gi, ca_token = _chip_start("g_attn_chip_start_a", sum_a, land_gi, 4, part_a)
    dh = _matmul_slabs_t(dpc, dpa, win_g, name="in_proj_dx", dep=ca_token)
    grad_x, pre_sums = _prenorm_bwd(dh, x2, dout, scale, g_pre)

    small = jnp.concatenate([pre_sums[0:1], pre_sums[1:2], post_sums[0:1],
                             pre_sums[2:3], post_sums[1:2],
                             conv_sums[2:3], conv_sums[3:4], conv_sums[4:5],
                             conv_sums[1:2], conv_sums[0:1], attn_sums[0:1]], axis=1)
    small = jnp.concatenate([small.reshape(8 * D // 128, 128), jnp.broadcast_to(post_sums[2:3, :128], (8, 128))])
    (small_all,) = _all_gather([small], "gather_small")
    cb_send, cb_recv, sum_a, land_gi, cb_token = _chip_start("g_attn_chip_start_b", sum_a, land_gi, 4, part_b,
                                                             after=[small_all])
    small_all = small_all.reshape(NDEV, small.size)
    tot = _sum_rows(small_all, cb_token)
    loss = tot[0, 8 * D]
    g_conv_w = lax.dynamic_slice_in_dim(tot[0:1, 5 * D:5 * D + 3 * C].reshape(1, 3, C), me * CW, CW, axis=2)
    ((g_b_ada, d_b_ada, nm_b_ada, nv_b_ada), (g_g_pre, d_g_pre, nm_g_pre, nv_g_pre),
     (g_g_post, d_g_post, nm_g_post, nv_g_post), (g_conv_w, d_conv_w, nm_conv_w, nv_conv_w),
     (g_conv_b, d_conv_b, nm_conv_b, nv_conv_b), (g_g_conv, d_g_conv, nm_g_conv, nv_g_conv),
     (g_g_attn, d_g_attn, nm_g_attn, nv_g_attn)) = _adamw_small(tot, [
         (b_ada, m_b_ada, v_b_ada, 0), (g_pre, m_g_pre, v_g_pre, 3 * D), (g_post, m_g_post, v_g_post, 4 * D),
         (conv_w, m_conv_w, v_conv_w, g_conv_w), (conv_b, m_conv_b, v_conv_b, 5 * D + 3 * C),
         (g_conv, m_g_conv, v_g_conv, 5 * D + 4 * C), (g_attn, m_g_attn, v_g_attn, 5 * D + 5 * C)])

    dmod_cols = lax.dynamic_slice_in_dim(small_all[:, :3 * D], me * W, W, axis=1)
    g_w_ada, d_w_ada, nm_w_ada, nv_w_ada = _adamw_ada(c_all.T, dmod_cols, w_ada2, m_w_ada[0], v_w_ada[0], cb_token)

    pick_in = jnp.stack([me // 4, (me % 4) // 2]).astype(jnp.int32)
    sum_c, land_gi = _chip_wait("g_conv_chip_wait", sum_c, land_gi, cc_send, cc_recv, 0, after=[g_w_ada])
    sum_a, land_gi = _chip_wait("g_attn_chip_wait_a", sum_a, land_gi, ca_send, ca_recv, 4, [g_w_ada], part_a)
    first = _adamw_sharded("adamw_w_in_a", land_gi, sum_c, sum_a, pick_in, w_in2, m_w_in[0], v_w_in[0], rows=part_a,
                           tr=256)
    sum_a, land_gi = _chip_wait("g_attn_chip_wait_b", sum_a, land_gi, cb_send, cb_recv, 4, [first[0]], part_b)
    g_w_in, d_w_in, nm_w_in, nv_w_in = _adamw_sharded(
        "adamw_w_in_b", land_gi, sum_c, sum_a, pick_in, w_in2, m_w_in[0], v_w_in[0], rows=part_b, prev=first, tr=256)

    return (loss, grad_x[None],
            g_w_ada[None], g_b_ada, g_g_pre, g_w_in[None], g_conv_w, g_conv_b, g_g_conv, g_g_attn, g_w_out[None], g_g_post,
            d_w_ada[None], d_b_ada, d_g_pre, d_w_in[None], d_conv_w, d_conv_b, d_g_conv, d_g_attn, d_w_out[None], d_g_post,
            nm_w_ada[None], nm_b_ada, nm_g_pre, nm_w_in[None], nm_conv_w, nm_conv_b, nm_g_conv, nm_g_attn, nm_w_out[None], nm_g_post,
            nv_w_ada[None], nv_b_ada, nv_g_pre, nv_w_in[None], nv_conv_w, nv_conv_b, nv_g_conv, nv_g_attn, nv_w_out[None], nv_g_post)
```

```python
import functools
import math

import jax
import jax.numpy as jnp
from jax import lax
from jax.experimental import pallas as pl
from jax.experimental.pallas import tpu as pltpu

f32 = jnp.float32
bf16 = jnp.bfloat16

NDEV = 8
HEAD_DIM = 64
PAIR = 2 * HEAD_DIM
BRANCHES = ((128, 1, 1), (512, 4, 1), (2048, 16, 2))
HALF_WIN = 64
EPS = 1e-6
NEG_INF = -1e30
ADAM_LR, ADAM_B1, ADAM_B2, ADAM_EPS, ADAM_WD, ADAM_STEP = 0.001, 0.9, 0.999, 1e-08, 0.01, 10
MESH = pl.DeviceIdType.MESH
VMEM_LIMIT = 56 * 1024 * 1024
HBM_SPEC = pl.BlockSpec(memory_space=pltpu.HBM)
ANY_SPEC = pl.BlockSpec(memory_space=pl.ANY)
SEM_SPEC = pl.BlockSpec(memory_space=pltpu.SEMAPHORE)


def _params(*sem):
    return pltpu.CompilerParams(dimension_semantics=sem or None, vmem_limit_bytes=VMEM_LIMIT)


def _silu(z):
    return z * jax.nn.sigmoid(z)


def _silu_and_slope(z):
    s = jax.nn.sigmoid(z)
    return z * s, s * (1.0 + z * (1.0 - s))


def _my_place():
    x, y, c = lax.axis_index("x"), lax.axis_index("y"), lax.axis_index("c")
    return x, y, c, 4 * x + 2 * y + c


def _peer(x, y, c, k):
    px, py, pc = x ^ (k >> 2 & 1), y ^ (k >> 1 & 1), c ^ (k & 1)
    return (px, py, pc), 4 * px + 2 * py + pc


def _all_gather(arrays, name):
    n = len(arrays)

    def body(*refs):
        srcs, dsts = refs[:n], refs[n:2 * n]
        send_sems, recv_sems, local_sems = refs[2 * n:]
        x, y, c, me = _my_place()
        locals_, sends = [], []
        for t in range(n):
            own = pltpu.make_async_copy(srcs[t], dsts[t].at[me], local_sems.at[t])
            own.start()
            locals_.append(own)
            for k in range(1, NDEV):
                peer, pidx = _peer(x, y, c, k)
                cp = pltpu.make_async_remote_copy(
                    src_ref=srcs[t], dst_ref=dsts[t].at[me], send_sem=send_sems.at[t, k],
                    recv_sem=recv_sems.at[t, k], device_id=peer, device_id_type=MESH)
                cp.start()
                sends.append(cp)
        for t in range(n):
            for k in range(1, NDEV):
                peer, pidx = _peer(x, y, c, k)
                pltpu.make_async_remote_copy(
                    src_ref=srcs[t], dst_ref=dsts[t].at[pidx], send_sem=send_sems.at[t, k],
                    recv_sem=recv_sems.at[t, k], device_id=peer, device_id_type=MESH).wait_recv()
        for cp in sends:
            cp.wait_send()
        for cp in locals_:
            cp.wait()

    return pl.pallas_call(
        body, name=name,
        out_shape=tuple(jax.ShapeDtypeStruct((NDEV,) + a.shape, a.dtype) for a in arrays),
        in_specs=[HBM_SPEC] * n, out_specs=tuple([HBM_SPEC] * n),
        scratch_shapes=[pltpu.SemaphoreType.DMA((n, NDEV)), pltpu.SemaphoreType.DMA((n, NDEV)),
                        pltpu.SemaphoreType.DMA((n,))],
    )(*arrays)


def _comm_call(name, arrays, sems, new_sems, body, after=(), token=False):
    na, ns, nn, nf = len(arrays), len(sems), len(new_sems), len(after)

    def kern(*refs):
        ins, outs = refs[:na + ns + nf], refs[na + ns + nf:]
        body(ins[:na], ins[na:na + ns], outs[:nn])
        if token:
            outs[nn + na][...] = jnp.zeros((8, 128), f32)

    out_shape = ([pltpu.SemaphoreType.DMA(s) for s in new_sems] + [pltpu.HBM(a.shape, a.dtype) for a in arrays]
                 + ([jax.ShapeDtypeStruct((8, 128), f32)] if token else []))
    out_specs = [SEM_SPEC] * nn + [HBM_SPEC] * na + ([pl.BlockSpec(memory_space=pltpu.VMEM)] if token else [])
    res = pl.pallas_call(
        kern, name=name, out_shape=tuple(out_shape),
        in_specs=[HBM_SPEC] * na + [SEM_SPEC] * ns + [ANY_SPEC] * nf, out_specs=tuple(out_specs),
        input_output_aliases={t: nn + t for t in range(na)},
        compiler_params=pltpu.CompilerParams(has_side_effects=pltpu.SideEffectType.DATAFLOW_SIDE_EFFECTING),
    )(*[pltpu.with_memory_space_constraint(a, pltpu.HBM) for a in arrays], *sems, *after)
    return list(res[:nn]), list(res[nn:nn + na]), (res[nn + na] if token else None)


def _remote(src, dst, send_sem, recv_sem, device):
    return pltpu.make_async_remote_copy(src_ref=src, dst_ref=dst, send_sem=send_sem, recv_sem=recv_sem,
                                        device_id=device, device_id_type=MESH)


SAME_CORE = (2, 4, 6)
VIA_SIBLING = (3, 5, 7)


def _weights_forward(name, land, recv, after):
    def body(a, s, new):
        (land,), (recv,), (fsend, frecv) = a, s, new
        x, y, c, me = _my_place()
        sibling, _ = _peer(x, y, c, 1)
        for k in SAME_CORE:
            peer, slot = _peer(x, y, c, k)
            _remote(land.at[slot], land.at[slot], fsend.at[k], recv.at[k], peer).wait_recv()
            _remote(land.at[slot], land.at[slot], fsend.at[k], frecv.at[k ^ 1], sibling).start()

    return _comm_call(name, [land], [recv], [(NDEV,), (NDEV,)], body, after=after)


def _weights_wait(name, land, send, recv, fsend, frecv, after):
    def body(a, s, new):
        (land,), (send, recv, fsend, frecv) = a, s
        x, y, c, me = _my_place()
        sibling, sib_slot = _peer(x, y, c, 1)
        _remote(land.at[sib_slot], land.at[sib_slot], send.at[1], recv.at[1], sibling).wait_recv()
        for k in VIA_SIBLING:
            _, slot = _peer(x, y, c, k)
            _remote(land.at[slot], land.at[slot], fsend.at[k ^ 1], frecv.at[k], sibling).wait_recv()
        for k in (1,) + SAME_CORE:
            peer, _ = _peer(x, y, c, k)
            _remote(land.at[me], land.at[me], send.at[k], recv.at[k], peer).wait_send()
        for k in SAME_CORE:
            _, slot = _peer(x, y, c, k)
            _remote(land.at[slot], land.at[slot], fsend.at[k], frecv.at[k ^ 1], sibling).wait_send()

    return _comm_call(name, [land], [send, recv, fsend, frecv], [], body, after=after)[1][0]


def _diag_relay(x, y, c):
    slot = 4 * (x ^ (1 - c)) + 2 * (y ^ c) + c
    return slot, (x ^ c, y ^ (1 - c), c)


def _w_in_start(land, after):
    def body(a, s, new):
        (land,), (send, recv) = a, new
        x, y, c, me = _my_place()
        for k in (1, 2, 4):
            peer, _ = _peer(x, y, c, k)
            _remote(land.at[me], land.at[me], send.at[k], recv.at[k], peer).start()

    (send, recv), (land,), token = _comm_call("w_in_start", [land], [], [(NDEV,), (NDEV,)], body, after=after, token=True)
    return send, recv, land, token


def _w_in_sibling(land, recv, after):
    def body(a, s, new):
        (land,), (recv,) = a, s
        x, y, c, me = _my_place()
        sibling, slot = _peer(x, y, c, 1)
        _remote(land.at[slot], land.at[slot], recv.at[1], recv.at[1], sibling).wait_recv()

    return _comm_call("w_in_sibling", [land], [recv], [], body, after=after)[1][0]


def _w_in_relay(land, recv, after):
    def body(a, s, new):
        (land,), (recv,), (fsend, frecv) = a, s, new
        x, y, c, me = _my_place()
        sibling, _ = _peer(x, y, c, 1)
        for k in (2, 4):
            peer, slot = _peer(x, y, c, k)
            _remote(land.at[slot], land.at[slot], fsend.at[k], recv.at[k], peer).wait_recv()
        slot, target = _diag_relay(x, y, c)
        _remote(land.at[slot], land.at[slot], fsend.at[6], frecv.at[6], target).start()
        for k in (2, 4):
            _, slot = _peer(x, y, c, k)
            _remote(land.at[slot], land.at[slot], fsend.at[k], frecv.at[k ^ 1], sibling).start()

    (fsend, frecv), (land,), _ = _comm_call("w_in_relay", [land], [recv], [(NDEV,), (NDEV,)], body, after=after)
    return fsend, frecv, land


def _w_in_forwarded(land, frecv, after):
    def body(a, s, new):
        (land,), (frecv,) = a, s
        x, y, c, me = _my_place()
        sibling, _ = _peer(x, y, c, 1)
        for k in (3, 5):
            _, slot = _peer(x, y, c, k)
            _remote(land.at[slot], land.at[slot], frecv.at[k], frecv.at[k], sibling).wait_recv()

    return _comm_call("w_in_forwarded", [land], [frecv], [], body, after=after)[1][0]


def _w_in_diag(land, land_o, frecv, after):
    def body(a, s, new):
        (land, land_o), (frecv,), (dsend, drecv, osend, orecv) = a, s, new
        x, y, c, me = _my_place()
        sibling, _ = _peer(x, y, c, 1)
        peer, slot = _peer(x, y, c, 6)
        _remote(land.at[slot], land.at[slot], dsend.at[6], frecv.at[6], peer).wait_recv()
        _remote(land.at[slot], land.at[slot], dsend.at[6], drecv.at[7], sibling).start()
        for k in (1,) + SAME_CORE:
            peer, _ = _peer(x, y, c, k)
            _remote(land_o.at[me], land_o.at[me], osend.at[k], orecv.at[k], peer).start()

    sems, (land, land_o), _ = _comm_call("w_in_diag", [land, land_o], [frecv], [(NDEV,)] * 4, body, after=after)
    return sems, land, land_o


def _w_in_finish(land, send, fsend, dsend, drecv, after):
    def body(a, s, new):
        (land,), (send, fsend, dsend, drecv) = a, s
        x, y, c, me = _my_place()
        sibling, _ = _peer(x, y, c, 1)
        _, slot = _peer(x, y, c, 7)
        _remote(land.at[slot], land.at[slot], dsend.at[6], drecv.at[7], sibling).wait_recv()
        for k in (1, 2, 4):
            peer, _ = _peer(x, y, c, k)
            _remote(land.at[me], land.at[me], send.at[k], send.at[k], peer).wait_send()
        for k in (2, 4, 6):
            _, slot = _peer(x, y, c, k)
            _remote(land.at[slot], land.at[slot], fsend.at[k], fsend.at[k], sibling).wait_send()
        _, slot = _peer(x, y, c, 6)
        _remote(land.at[slot], land.at[slot], dsend.at[6], dsend.at[6], sibling).wait_send()

    return _comm_call("w_in_finish", [land], [send, fsend, dsend, drecv], [], body, after=after)[1][0]


def _in_proj_part(name, h, land, proj, me_arr, k0, kstep, nk, tm=512):
    S, D = h.shape
    C = land.shape[2]
    tm = min(tm, S)

    def body(me_ref, a_ref, b_ref, *rest):
        rest[-1][...] = jnp.dot(a_ref[...], b_ref[...], preferred_element_type=f32)

    slot = lambda j, me: me[0] ^ (k0 + kstep * j)
    args = [h, land] + ([] if proj is None else [proj])
    grid_spec = pltpu.PrefetchScalarGridSpec(
        num_scalar_prefetch=1, grid=(nk, S // tm),
        in_specs=[pl.BlockSpec((tm, D), lambda j, i, me: (i, 0)),
                  pl.BlockSpec((None, D, C), lambda j, i, me: (slot(j, me), 0, 0))] + [ANY_SPEC] * (len(args) - 2),
        out_specs=pl.BlockSpec((tm, C), lambda j, i, me: (i, slot(j, me))))
    return pl.pallas_call(
        body, name=name, out_shape=jax.ShapeDtypeStruct((S, NDEV * C), f32), grid_spec=grid_spec,
        input_output_aliases={} if proj is None else {3: 0}, compiler_params=_params("arbitrary", "arbitrary"),
    )(me_arr, *args)


NCHIP = NDEV // 2


def _pair_start(name, src):
    npair = src.shape[0] // 2

    def body(a, s, new):
        (src, pair), (send, recv) = a, new
        x, y, c, me = _my_place()
        sibling, _ = _peer(x, y, c, 1)
        for i in range(npair):
            _remote(src.at[2 * i + 1 - c], pair.at[i], send.at[i], recv.at[i], sibling).start()

    pair = lax.empty((npair,) + src.shape[1:], src.dtype)
    (send, recv), (src, pair), token = _comm_call(name, [src, pair], [], [(npair,), (npair,)], body, token=True)
    return send, recv, src, pair, token


def _pair_wait(name, src, pair, send, recv, after):
    npair = pair.shape[0]

    def body(a, s, new):
        (src, pair), (send, recv) = a, s
        x, y, c, me = _my_place()
        sibling, _ = _peer(x, y, c, 1)
        for i in range(npair):
            cp = _remote(src.at[2 * i + 1 - c], pair.at[i], send.at[i], recv.at[i], sibling)
            cp.wait_recv()
            cp.wait_send()

    return _comm_call(name, [src, pair], [send, recv], [], body, after=after)[1]


def _pair_sum(name, src, pair, core, tr=1024):
    npair, R, Cc = pair.shape
    tr = min(tr, R)

    def body(core_ref, a_ref, b_ref, o_ref):
        o_ref[...] = (a_ref[...].astype(f32) + b_ref[...].astype(f32)).astype(o_ref.dtype)

    grid_spec = pltpu.PrefetchScalarGridSpec(
        num_scalar_prefetch=1, grid=(npair, R // tr),
        in_specs=[pl.BlockSpec((None, tr, Cc), lambda i, r, core: (2 * i + core[0], r, 0)),
                  pl.BlockSpec((None, tr, Cc), lambda i, r, core: (i, r, 0))],
        out_specs=pl.BlockSpec((None, tr, Cc), lambda i, r, core: (i, r, 0)))
    return pl.pallas_call(body, name=name, out_shape=jax.ShapeDtypeStruct(pair.shape, pair.dtype),
                          grid_spec=grid_spec, compiler_params=_params("parallel", "parallel"))(core, src, pair)


def _owner_chip(first, i):
    q = first // 2 + i
    return q >> 1 & 1, q & 1


def _chip_start(name, sums, land, first, rows=None, after=()):
    npair = sums.shape[0]
    rows = pl.ds(*(rows or (0, sums.shape[1])))

    def body(a, s, new):
        (sums, land), (send, recv) = a, new
        x, y, c, me = _my_place()
        for i in range(npair):
            ox, oy = _owner_chip(first, i)

            @pl.when((x != ox) | (y != oy))
            def _():
                _remote(sums.at[i, rows], land.at[2 * x + y, rows], send.at[i], recv.at[2 * x + y], (ox, oy, c)).start()

    (send, recv), (sums, land), token = _comm_call(name, [sums, land], [], [(npair,), (NCHIP,)], body, after=after,
                                                   token=True)
    return send, recv, sums, land, token


def _chip_wait(name, sums, land, send, recv, first, after, rows=None):
    npair = sums.shape[0]
    rows = pl.ds(*(rows or (0, sums.shape[1])))

    def body(a, s, new):
        (sums, land), (send, recv) = a, s
        x, y, c, me = _my_place()
        mine = (me >= first) & (me < first + 2 * npair)
        for i in range(npair):
            ox, oy = _owner_chip(first, i)

            @pl.when((x != ox) | (y != oy))
            def _():
                _remote(sums.at[i, rows], land.at[2 * x + y, rows], send.at[i], recv.at[2 * x + y], (ox, oy, c)).wait_send()
        for q in range(NCHIP):
            @pl.when(mine & (2 * x + y != q))
            def _():
                _remote(sums.at[0, rows], land.at[q, rows], send.at[0], recv.at[q], (q >> 1, q & 1, c)).wait_recv()

    return _comm_call(name, [sums, land], [send, recv], [], body, after=after)[1]


def _matmul(a, b, *, name, out_dtype, ta=False, tb=False, b_slots=False, out_slots=0, b_cols=None,
            tm=1024, tn=1024, tk=2048, dep=None):
    M, K = (a.shape[1], a.shape[0]) if ta else a.shape
    col0 = 0
    if b_slots:
        slab = b.shape[2]
        N = b.shape[1] if tb else b.shape[0] * slab
        assert (K if tb else N) == b.shape[0] * slab
    elif b_cols is not None:
        assert not tb
        col0, N = b_cols
    else:
        N = b.shape[0] if tb else b.shape[1]
    tm, tn, tk = min(tm, M), min(tn, N), min(tk, K)
    if b_slots:
        if tb:
            tk = min(tk, slab)
        else:
            tn = min(tn, slab)
    if out_slots:
        tn = min(tn, N // out_slots)
    nm, nn, nk = M // tm, N // tn, K // tk
    assert (nm * tm, nn * tn, nk * tk) == (M, N, K) and col0 % tn == 0, (name, M, N, K, tm, tn, tk)
    j0 = col0 // tn

    a_spec = pl.BlockSpec((tk, tm), lambda i, j, k: (k, i)) if ta else pl.BlockSpec((tm, tk), lambda i, j, k: (i, k))
    if b_slots and tb:
        per = slab // tk
        b_spec = pl.BlockSpec((None, tn, tk), lambda i, j, k: (k // per, j, k % per))
    elif b_slots:
        per = slab // tn
        b_spec = pl.BlockSpec((None, tk, tn), lambda i, j, k: (j // per, k, j % per))
    elif tb:
        b_spec = pl.BlockSpec((tn, tk), lambda i, j, k: (j, k))
    else:
        b_spec = pl.BlockSpec((tk, tn), lambda i, j, k: (k, j + j0))
    if out_slots:
        per_o = (N // out_slots) // tn
        o_spec = pl.BlockSpec((None, tm, tn), lambda i, j, k: (j // per_o, i, j % per_o))
        out_shape = jax.ShapeDtypeStruct((out_slots, M, N // out_slots), out_dtype)
    else:
        o_spec = pl.BlockSpec((tm, tn), lambda i, j, k: (i, j))
        out_shape = jax.ShapeDtypeStruct((M, N), out_dtype)
    dims = (((0 if ta else 1,), (1 if tb else 0,)), ((), ()))
    deps = [] if dep is None else [dep]

    def body(a_ref, b_ref, *rest):
        o_ref = rest[len(deps)]
        prod = lax.dot_general(a_ref[...], b_ref[...], dims, preferred_element_type=f32)
        if nk == 1:
            o_ref[...] = prod.astype(out_dtype)
            return
        acc_ref = rest[len(deps) + 1]
        k = pl.program_id(2)

        @pl.when(k == 0)
        def _():
            acc_ref[...] = prod

        @pl.when((k > 0) & (k < nk - 1))
        def _():
            acc_ref[...] += prod

        @pl.when(k == nk - 1)
        def _():
            o_ref[...] = (acc_ref[...] + prod).astype(out_dtype)

    return pl.pallas_call(
        body, name=name, out_shape=out_shape, grid=(nm, nn, nk),
        in_specs=[a_spec, b_spec] + [ANY_SPEC] * len(deps), out_specs=o_spec,
        scratch_shapes=[pltpu.VMEM((tm, tn), f32)] if nk > 1 else [],
        compiler_params=_params("parallel", "parallel", "arbitrary"),
    )(a, b, *deps)


def _matmul_slabs_t(a_cols, a_slots, b, *, name, tm=512, tn=512, dep=None):
    M = a_cols.shape[0]
    n_slab, N, slab = b.shape
    n1, n2 = a_cols.shape[1] // slab, a_slots.shape[0]
    assert n1 + n2 == n_slab and a_slots.shape[1:] == (M, slab)
    tm, tn = min(tm, M), min(tn, N)
    deps = [] if dep is None else [dep]

    def body(a1_ref, a2_ref, b_ref, *rest):
        o_ref = rest[len(deps)]
        acc = None
        for s in range(n_slab):
            lhs = a1_ref[:, s * slab:(s + 1) * slab] if s < n1 else a2_ref[s - n1]
            prod = lax.dot_general(lhs, b_ref[s], (((1,), (1,)), ((), ())), preferred_element_type=f32)
            acc = prod if acc is None else acc + prod
        o_ref[...] = acc

    return pl.pallas_call(
        body, name=name, out_shape=jax.ShapeDtypeStruct((M, N), f32), grid=(M // tm, N // tn),
        in_specs=[pl.BlockSpec((tm, n1 * slab), lambda i, j: (i, 0)), pl.BlockSpec((n2, tm, slab), lambda i, j: (0, i, 0)),
                  pl.BlockSpec((n_slab, tn, slab), lambda i, j: (0, j, 0))] + [ANY_SPEC] * len(deps),
        out_specs=pl.BlockSpec((tm, tn), lambda i, j: (i, j)), compiler_params=_params("parallel", "parallel"),
    )(a_cols, a_slots, b, *deps)


def _ada_exchange(c_blk, cw_slab, w_ada, b_cols):
    nblk = c_blk.shape[0]
    D, W = w_ada.shape
    CW = cw_slab.shape[1]

    def body(c_ref, cw_ref, w_ref, b_ref, mod_ref, call_ref, cwg_ref, msend, send_sems, recv_sems):
        x, y, c, me = _my_place()
        call_ref[me] = _silu(c_ref[...])
        cwg_ref[me] = cw_ref[...]
        first = []
        for k in range(1, NDEV):
            peer, _ = _peer(x, y, c, k)
            first.append(_remote(call_ref.at[me], call_ref.at[me], send_sems.at[0, k], recv_sems.at[0, k], peer))
            first.append(_remote(cwg_ref.at[me], cwg_ref.at[me], send_sems.at[1, k], recv_sems.at[1, k], peer))
        for cp in first:
            cp.start()
        for k in range(1, NDEV):
            peer, slot = _peer(x, y, c, k)
            _remote(call_ref.at[slot], call_ref.at[slot], send_sems.at[0, k], recv_sems.at[0, k], peer).wait_recv()
            _remote(cwg_ref.at[slot], cwg_ref.at[slot], send_sems.at[1, k], recv_sems.at[1, k], peer).wait_recv()
        mod = jnp.broadcast_to(b_ref[...], (NDEV, W))
        for r in range(nblk):
            mod = mod + lax.dot_general(call_ref[:, r, :], w_ref[r * 128:(r + 1) * 128, :], (((1,), (0,)), ((), ())),
                                        preferred_element_type=f32, precision=lax.Precision.HIGHEST)
        row = lax.broadcasted_iota(jnp.int32, (NDEV, 1), 0)
        pick = lambda j: jnp.broadcast_to(jnp.sum(jnp.where(row == j, mod, 0.0), axis=0, keepdims=True), (8, W))
        mod_ref[me] = pick(me)
        second = []
        for k in range(1, NDEV):
            peer, slot = _peer(x, y, c, k)
            msend[k] = pick(slot)
            second.append(_remote(msend.at[k], mod_ref.at[me], send_sems.at[2, k], recv_sems.at[2, k], peer))
        for cp in second:
            cp.start()
        for k in range(1, NDEV):
            peer, slot = _peer(x, y, c, k)
            _remote(msend.at[k], mod_ref.at[slot], send_sems.at[2, k], recv_sems.at[2, k], peer).wait_recv()
        for cp in first + second:
            cp.wait_send()

    vmem = pl.BlockSpec(memory_space=pltpu.VMEM)
    return pl.pallas_call(
        body, name="ada_exchange",
        out_shape=(jax.ShapeDtypeStruct((NDEV, 8, W), f32), jax.ShapeDtypeStruct((NDEV, nblk, 128), f32),
                   jax.ShapeDtypeStruct((NDEV, 8, CW), f32)),
        in_specs=[vmem] * 4, out_specs=(vmem, vmem, vmem),
        scratch_shapes=[pltpu.VMEM((NDEV, 8, W), f32), pltpu.SemaphoreType.DMA((3, NDEV)),
                        pltpu.SemaphoreType.DMA((3, NDEV))],
        compiler_params=_params(),
    )(c_blk, cw_slab, w_ada, b_cols)


def _prenorm(x, scale, shift, g_pre, dep, tr=512):
    S, D = x.shape
    tr = min(tr, S)

    def body(x_ref, sc_ref, sh_ref, g_ref, dep_ref, h_ref):
        xv = x_ref[...]
        r = lax.rsqrt(jnp.mean(xv * xv, axis=-1, keepdims=True) + EPS)
        h_ref[...] = ((xv * r) * g_ref[...] * (1.0 + sc_ref[...]) + sh_ref[...]).astype(bf16)

    row = pl.BlockSpec((tr, D), lambda i: (i, 0))
    vec = pl.BlockSpec((1, D), lambda i: (0, 0))
    return pl.pallas_call(body, name="prenorm", out_shape=jax.ShapeDtypeStruct((S, D), bf16), grid=(S // tr,),
                          in_specs=[row, vec, vec, vec, ANY_SPEC], out_specs=row, compiler_params=_params("parallel"))(
                              x, scale, shift, g_pre, dep)


def _ext_rows(i, tr, S):
    g = lax.broadcasted_iota(jnp.int32, (tr + 16, 1), 0) + (i * tr - 8)
    return (g >= 0) & (g < S)


def _halo_specs(tr, S, C, col):
    nb8 = S // 8
    main = pl.BlockSpec((tr, C), lambda i: (i, col))
    prev = pl.BlockSpec((8, C), lambda i: (jnp.maximum(i * (tr // 8) - 1, 0), col))
    nxt = pl.BlockSpec((8, C), lambda i: (jnp.minimum((i + 1) * (tr // 8), nb8 - 1), col))
    return prev, main, nxt


def _conv_fwd(proj, conv_w, conv_b, g_conv, tr=512):
    S, C = proj.shape[0], proj.shape[1] // 8
    tr = min(tr, S)

    def body(up, um, un, cp, cm, cn, bg_ref, zc_ref, w_ref, cb_ref, g_ref, o_ref):
        i = pl.program_id(0)
        exists = _ext_rows(i, tr, S)
        u = jnp.concatenate([up[...], um[...], un[...]], axis=0)
        cg = jnp.concatenate([cp[...], cm[...], cn[...]], axis=0)
        t = jnp.where(exists, cg * u, 0.0)
        t_before = pltpu.roll(t, 1, 0)[8:tr + 8]
        t_after = pltpu.roll(t, tr + 15, 0)[8:tr + 8]
        w = w_ref[...]
        cv = w[0:1] * t_before + w[1:2] * t[8:tr + 8] + w[2:3] * t_after + cb_ref[...]
        yc = bg_ref[...] * cv
        rc = lax.rsqrt(jnp.mean(yc * yc, axis=-1, keepdims=True) + EPS)
        o_ref[...] = ((yc * rc) * g_ref[...] * _silu(zc_ref[...])).astype(bf16)

    u_specs = _halo_specs(tr, S, C, 0)
    c_specs = _halo_specs(tr, S, C, 2)
    vec = pl.BlockSpec((1, C), lambda i: (0, 0))
    return pl.pallas_call(
        body, name="conv_fwd", out_shape=jax.ShapeDtypeStruct((S, 2 * C), bf16), grid=(S // tr,),
        in_specs=[*u_specs, *c_specs, pl.BlockSpec((tr, C), lambda i: (i, 1)), pl.BlockSpec((tr, C), lambda i: (i, 3)),
                  pl.BlockSpec((8, C), lambda i: (0, 0)), vec, vec],
        out_specs=pl.BlockSpec((tr, C), lambda i: (i, 0)), compiler_params=_params("parallel"),
    )(proj, proj, proj, proj, proj, proj, proj, proj, conv_w, conv_b, g_conv)


def _branch_geometry(S, r, inter):
    L = S // r * inter
    nq = min(128, L)
    nk = min(nq + 2 * HALF_WIN * inter, L)
    assert L % nq == 0 and (L == nk or L >= nq + 2 * HALF_WIN * inter)
    return L, nq, nk, L // nq


QUAD = 4


def _to_quad(dst, src, S):
    n = S // QUAD
    for rho in range(QUAD):
        dst[pl.ds(rho * n, n), :] = src[pl.ds(rho, n, stride=QUAD), :]


def _block_rows(idx, r, inter, S, L, nq, nk, nblk):
    rho, qb = (0, idx) if r == 1 else (idx // nblk, idx % nblk)
    i0 = qb * nq
    ws = jnp.clip(i0 - HALF_WIN * inter, 0, L - nk)
    if r == 1:
        return pl.ds(pl.multiple_of(i0, 8), nq), pl.ds(pl.multiple_of(ws, 8), nk), i0 - ws
    assert r % (QUAD * inter) == 0
    step = r // QUAD // inter
    base = (rho % QUAD) * (S // QUAD) + rho // QUAD
    if step == 1:
        return pl.ds(pl.multiple_of(base + i0, 8), nq), pl.ds(pl.multiple_of(base + ws, 8), nk), i0 - ws
    return pl.ds(base + step * i0, nq, stride=step), pl.ds(base + step * ws, nk, stride=step), i0 - ws


N_CASES = 3
SCALE = HEAD_DIM ** -0.5
ATTN_UNROLL = 16


def _bias_shape(S):
    shapes = [_branch_geometry(S, r, inter)[1:3] for _, r, inter in BRANCHES]
    return (len(BRANCHES) * N_CASES * 2, max(nq for nq, _ in shapes), max(nk for _, nk in shapes))


def _bias_index(b, case, head):
    return (b * N_CASES + case) * 2 + head


def _fill_bias(bias_scr, sl_ref, S):
    sl = sl_ref[...]
    slope = (sl[0:1, 0:1], sl[0:1, HEAD_DIM:HEAD_DIM + 1])
    for b, (_, r, inter) in enumerate(BRANCHES):
        L, nq, nk, nblk = _branch_geometry(S, r, inter)
        rel = lax.broadcasted_iota(jnp.int32, (nq, nk), 0) - lax.broadcasted_iota(jnp.int32, (nq, nk), 1)
        for case in range(N_CASES):
            d = jnp.abs(rel + case * HALF_WIN)
            valid = d <= HALF_WIN * inter
            if inter > 1:
                valid = valid & (jnp.bitwise_and(d, inter - 1) == 0)
            dist = d.astype(f32) * float(r // inter)
            for head in range(2):
                bias_scr[_bias_index(b, case, head), 0:nq, 0:nk] = jnp.where(valid, -slope[head] * dist, NEG_INF)


def _bias_tiles(slopes, S, dep):
    npair = slopes.shape[0]
    shape = _bias_shape(S)

    def body(sl_ref, dep_ref, o_ref):
        _fill_bias(o_ref, sl_ref, S)

    return pl.pallas_call(
        body, name="bias_tiles", out_shape=jax.ShapeDtypeStruct((npair,) + shape, f32), grid=(npair,),
        in_specs=[pl.BlockSpec((None, 8, PAIR), lambda p: (p, 0, 0)), ANY_SPEC],
        out_specs=pl.BlockSpec((None,) + shape, lambda p: (p, 0, 0, 0)), compiler_params=_params("parallel"),
    )(slopes, dep)


def _head_slopes(n_heads):
    slopes = 2.0 ** (-8.0 * jnp.arange(1, n_heads + 1, dtype=f32) / n_heads)
    return jnp.broadcast_to(jnp.repeat(slopes.reshape(n_heads // 2, 2), HEAD_DIM, axis=1)[:, None, :],
                            (n_heads // 2, 8, PAIR))


def _attn_fwd(proj, bias):
    S, C = proj.shape[0], proj.shape[1] // 8
    npair = C // PAIR

    def body(q_ref, k_ref, v_ref, bias_scr, o_ref, lse_ref, m_scr, l_scr, a_scr, q4_scr, k4_scr, v4_scr):
        lane = lax.broadcasted_iota(jnp.int32, (1, PAIR), 1)
        first = lane < HEAD_DIM
        for dst, src in ((q4_scr, q_ref), (k4_scr, k_ref), (v4_scr, v_ref)):
            _to_quad(dst, src, S)

        for b, (_, r, inter) in enumerate(BRANCHES):
            L, nq, nk, nblk = _branch_geometry(S, r, inter)
            qs, ks, vs = (q_ref, k_ref, v_ref) if r == 1 else (q4_scr, k4_scr, v4_scr)

            def step(idx, carry, b=b, r=r, L=L, nq=nq, nk=nk, nblk=nblk, qs=qs, ks=ks, vs=vs):
                qrows, krows, off = _block_rows(idx, r, inter, S, L, nq, nk, nblk)
                case = off // HALF_WIN
                q2 = qs[qrows, :] * SCALE
                k2 = ks[krows, :].astype(bf16)
                v2 = vs[krows, :].astype(bf16)
                ms, accs = [], []
                for hh in range(2):
                    mine = first if hh == 0 else ~first
                    qh = jnp.where(mine, q2, 0.0).astype(bf16)
                    s = lax.dot_general(qh, k2, (((1,), (1,)), ((), ())), preferred_element_type=f32)
                    s = s + bias_scr[_bias_index(b, case, hh), 0:nq, 0:nk]
                    m = jnp.max(s, axis=-1, keepdims=True)
                    p = jnp.exp(s - m).astype(bf16)
                    vh = jnp.where(mine, v2, jnp.ones_like(v2))
                    ms.append(m)
                    accs.append(jnp.dot(p, vh, preferred_element_type=f32))
                m_scr[b, qrows, :] = jnp.where(first, ms[0], ms[1])
                a_scr[b, qrows, :] = jnp.where(first, accs[0], accs[1])
                l_scr[b, qrows, :] = jnp.where(first, accs[1], accs[0])
                return carry

            lax.fori_loop(0, S // nq, step, 0, unroll=min(ATTN_UNROLL, S // nq))

        n4 = S // QUAD
        ch = min(256, n4)
        nch = n4 // ch

        def merge(i, carry):
            rho, part = i // nch, i % nch
            sorted_rows = pl.ds(pl.multiple_of(rho * n4 + part * ch, 8), ch)
            token_rows = pl.ds(rho + QUAD * part * ch, ch, stride=QUAD)
            rows = (token_rows,) + (sorted_rows,) * (len(BRANCHES) - 1)
            ms = [m_scr[b, rows[b], :] for b in range(len(BRANCHES))]
            m = functools.reduce(jnp.maximum, ms)
            l = jnp.zeros((ch, PAIR), f32)
            acc = jnp.zeros((ch, PAIR), f32)
            for b in range(len(BRANCHES)):
                w = jnp.exp(ms[b] - m)
                l = l + w * pltpu.roll(l_scr[b, rows[b], :], HEAD_DIM, 1)
                acc = acc + w * a_scr[b, rows[b], :]
            o_ref[token_rows, :] = acc / l
            lse_ref[token_rows, :] = m + jnp.log(l)
            return carry

        lax.fori_loop(0, QUAD * nch, merge, 0, unroll=2)

    blk = lambda part: pl.BlockSpec((S, PAIR), lambda p: (0, part * npair + p))
    out = pl.BlockSpec((S, PAIR), lambda p: (0, p))
    return pl.pallas_call(
        body, name="attn_fwd",
        out_shape=(jax.ShapeDtypeStruct((S, C), f32), jax.ShapeDtypeStruct((S, C), f32)), grid=(npair,),
        in_specs=[blk(4), blk(5), blk(6), pl.BlockSpec((None,) + _bias_shape(S), lambda p: (p, 0, 0, 0))],
        out_specs=(out, out),
        scratch_shapes=[pltpu.VMEM((3, S, PAIR), f32)] * 3 + [pltpu.VMEM((S, PAIR), f32)] * 3,
        compiler_params=_params("parallel"),
    )(proj, proj, proj, bias)


def _attn_post(ycat, o, proj, g_attn, tr=512):
    S, C = o.shape
    tr = min(tr, S)

    def body(y_ref, o_ref, z_ref, g_ref, out_ref):
        del y_ref
        ov = o_ref[...]
        ra = lax.rsqrt(jnp.mean(ov * ov, axis=-1, keepdims=True) + EPS)
        out_ref[...] = ((ov * ra) * g_ref[...] * _silu(z_ref[...])).astype(bf16)

    return pl.pallas_call(
        body, name="attn_post", out_shape=jax.ShapeDtypeStruct(ycat.shape, ycat.dtype), grid=(S // tr,),
        in_specs=[HBM_SPEC, pl.BlockSpec((tr, C), lambda i: (i, 0)), pl.BlockSpec((tr, C), lambda i: (i, 7)),
                  pl.BlockSpec((1, C), lambda i: (0, 0))],
        out_specs=pl.BlockSpec((tr, C), lambda i: (i, 1)), input_output_aliases={0: 0},
        compiler_params=_params("arbitrary"),
    )(ycat, o, proj, g_attn)


def _residual_minus_target(x, target, dep, tr=512):
    S, D = x.shape
    tr = min(tr, S)

    def body(x_ref, t_ref, dep_ref, o_ref):
        o_ref[...] = x_ref[...] - t_ref[...]

    row = pl.BlockSpec((tr, D), lambda i: (i, 0))
    return pl.pallas_call(body, name="residual_minus_target", out_shape=jax.ShapeDtypeStruct((S, D), f32),
                          grid=(S // tr,), in_specs=[row, row, ANY_SPEC], out_specs=row,
                          compiler_params=_params("parallel"))(x, target, dep)


def _sandwich(y, x_minus_t, gate, g_post, tr=256):
    S, D = y.shape
    tr = min(tr, S)

    def body(y_ref, xt_ref, gate_ref, g_ref, dy_ref, dout_ref, sums_ref):
        i = pl.program_id(0)
        gate, g = gate_ref[...], g_ref[...]
        gg = gate * g
        yv = y_ref[...]
        rp = lax.rsqrt(jnp.mean(yv * yv, axis=-1, keepdims=True) + EPS)
        yhat = yv * rp
        err = xt_ref[...] + gg * yhat
        dout = err * (1.0 / D)
        dout_ref[...] = dout
        q = dout * yhat
        w = dout * gg
        dy_ref[...] = (rp * (w - yhat * jnp.sum(q * gg, axis=-1, keepdims=True) * (1.0 / D))).astype(bf16)
        loss = 0.5 * jnp.sum(jnp.mean(err * err, axis=-1, keepdims=True), axis=0, keepdims=True)
        q_sum = jnp.sum(q, axis=0, keepdims=True)
        row = lax.broadcasted_iota(jnp.int32, (8, D), 0)
        upd = jnp.where(row == 0, q_sum * g, jnp.where(row == 1, q_sum * gate, jnp.where(row == 2, loss, 0.0)))

        @pl.when(i == 0)
        def _():
            sums_ref[...] = upd

        @pl.when(i > 0)
        def _():
            sums_ref[...] += upd

    row = pl.BlockSpec((tr, D), lambda i: (i, 0))
    vec = pl.BlockSpec((1, D), lambda i: (0, 0))
    return pl.pallas_call(
        body, name="sandwich",
        out_shape=(jax.ShapeDtypeStruct((S, D), bf16), jax.ShapeDtypeStruct((S, D), f32), jax.ShapeDtypeStruct((8, D), f32)),
        grid=(S // tr,), in_specs=[row, row, vec, vec],
        out_specs=(row, row, pl.BlockSpec((8, D), lambda i: (0, 0))), compiler_params=_params("arbitrary"),
    )(y, x_minus_t, gate, g_post)


def _conv_bwd(proj, dycat, conv_w, conv_b, g_conv, dep, tr=256):
    S, C = proj.shape[0], proj.shape[1] // 8
    tr = min(tr, S)
    n = tr + 16

    def body(*refs):
        ins, (w_ref, cb_ref, g_ref, _, dp_ref, sums_ref) = refs[:15], refs[15:]
        i = pl.program_id(0)
        exists = _ext_rows(i, tr, S)
        u, bg, cg, zc, dyn = (jnp.concatenate([ins[3 * t][...], ins[3 * t + 1][...], ins[3 * t + 2][...]], axis=0)
                              for t in range(5))
        w = w_ref[...]
        t = jnp.where(exists, cg * u, 0.0)
        t_before, t_after = pltpu.roll(t, 1, 0), pltpu.roll(t, n - 1, 0)
        cv = w[0:1] * t_before + w[1:2] * t + w[2:3] * t_after + cb_ref[...]
        yc = bg * cv
        rc = lax.rsqrt(jnp.mean(yc * yc, axis=-1, keepdims=True) + EPS)
        yhat = yc * rc
        sz, dsz = _silu_and_slope(zc)
        wgt = dyn * g_ref[...] * sz
        dyc = rc * (wgt - yhat * jnp.mean(wgt * yhat, axis=-1, keepdims=True))
        dcv = jnp.where(exists, dyc * bg, 0.0)
        dt = w[0:1] * pltpu.roll(dcv, n - 1, 0) + w[1:2] * dcv + w[2:3] * pltpu.roll(dcv, 1, 0)
        mid = slice(8, tr + 8)
        dp_ref[:, 0:C] = (dt * cg)[mid].astype(bf16)
        dp_ref[:, C:2 * C] = (dyc * cv)[mid].astype(bf16)
        dp_ref[:, 2 * C:3 * C] = (dt * u)[mid].astype(bf16)
        dp_ref[:, 3 * C:4 * C] = (dyn * yhat * g_ref[...] * dsz)[mid].astype(bf16)
        colsum = lambda v: jnp.sum(v[mid], axis=0, keepdims=True)
        parts = [colsum(dyn * yhat * sz), colsum(dcv), colsum(dcv * t_before), colsum(dcv * t), colsum(dcv * t_after)]
        row = lax.broadcasted_iota(jnp.int32, (8, C), 0)
        upd = jnp.zeros((8, C), f32)
        for j, pj in enumerate(parts):
            upd = jnp.where(row == j, pj, upd)

        @pl.when(i == 0)
        def _():
            sums_ref[...] = upd

        @pl.when(i > 0)
        def _():
            sums_ref[...] += upd

    specs = []
    for col in range(4):
        specs += _halo_specs(tr, S, C, col)
    specs += _halo_specs(tr, S, C, 0)
    vec = pl.BlockSpec((1, C), lambda i: (0, 0))
    return pl.pallas_call(
        body, name="conv_bwd",
        out_shape=(jax.ShapeDtypeStruct((S, 4 * C), bf16), jax.ShapeDtypeStruct((8, C), f32)), grid=(S // tr,),
        in_specs=[*specs, pl.BlockSpec((8, C), lambda i: (0, 0)), vec, vec, ANY_SPEC],
        out_specs=(pl.BlockSpec((tr, 4 * C), lambda i: (i, 0)), pl.BlockSpec((8, C), lambda i: (0, 0))),
        compiler_params=_params("arbitrary"),
    )(*([proj] * 12), dycat, dycat, dycat, conv_w, conv_b, g_conv, dep)


def _attn_post_bwd(o, proj, dycat, g_attn, dep, tr=512):
    S, C = o.shape
    tr = min(tr, S)

    def body(o_ref, z_ref, dy_ref, g_ref, dep_ref, do_ref, dz_ref, sums_ref):
        i = pl.program_id(0)
        ov, zv, dyn = o_ref[...], z_ref[...], dy_ref[...]
        ra = lax.rsqrt(jnp.mean(ov * ov, axis=-1, keepdims=True) + EPS)
        ohat = ov * ra
        sz, dsz = _silu_and_slope(zv)
        wgt = dyn * g_ref[...] * sz
        do_ref[...] = ra * (wgt - ohat * jnp.mean(wgt * ohat, axis=-1, keepdims=True))
        dz_ref[...] = (dyn * ohat * g_ref[...] * dsz).astype(bf16)
        row = lax.broadcasted_iota(jnp.int32, (8, C), 0)
        upd = jnp.where(row == 0, jnp.sum(dyn * ohat * sz, axis=0, keepdims=True), 0.0)

        @pl.when(i == 0)
        def _():
            sums_ref[...] = upd

        @pl.when(i > 0)
        def _():
            sums_ref[...] += upd

    return pl.pallas_call(
        body, name="attn_post_bwd",
        out_shape=(jax.ShapeDtypeStruct((S, C), f32), jax.ShapeDtypeStruct((4, S, C), bf16),
                   jax.ShapeDtypeStruct((8, C), f32)),
        grid=(S // tr,),
        in_specs=[pl.BlockSpec((tr, C), lambda i: (i, 0)), pl.BlockSpec((tr, C), lambda i: (i, 7)),
                  pl.BlockSpec((tr, C), lambda i: (i, 1)), pl.BlockSpec((1, C), lambda i: (0, 0)), ANY_SPEC],
        out_specs=(pl.BlockSpec((tr, C), lambda i: (i, 0)), pl.BlockSpec((None, tr, C), lambda i: (3, i, 0)),
                   pl.BlockSpec((8, C), lambda i: (0, 0))),
        compiler_params=_params("arbitrary"),
    )(o, proj, dycat, g_attn, dep)


def _attn_bwd(proj, o, do, lse, bias, dqkvz, dep):
    S, C = o.shape
    npair = C // PAIR

    def body(q_ref, k_ref, v_ref, o_ref, do_ref, lse_ref, bias_scr, old_ref, dep_ref, dqkv_ref,
             acc_scr, dl_scr, quad_scr):
        lane = lax.broadcasted_iota(jnp.int32, (1, PAIR), 1)
        first = lane < HEAD_DIM
        ch = min(256, S)

        def prep(i, carry):
            rows = pl.ds(pl.multiple_of(i * ch, 8), ch)
            prod = do_ref[rows, :] * o_ref[rows, :]
            d0 = jnp.sum(jnp.where(first, prod, 0.0), axis=-1, keepdims=True)
            d1 = jnp.sum(jnp.where(first, 0.0, prod), axis=-1, keepdims=True)
            dl_scr[rows, :] = jnp.where(first, d0, d1)
            zero = jnp.zeros((ch, PAIR), f32)
            for order in range(2):
                for t in range(3):
                    acc_scr[order, t, rows, :] = zero
            return carry

        lax.fori_loop(0, S // ch, prep, 0, unroll=2)
        token_srcs = (q_ref, k_ref, v_ref, do_ref, lse_ref, dl_scr)
        for j, src in enumerate(token_srcs):
            _to_quad(quad_scr.at[j], src, S)

        for b, (_, r, inter) in enumerate(BRANCHES):
            L, nq, nk, nblk = _branch_geometry(S, r, inter)
            order = 0 if r == 1 else 1
            srcs = token_srcs if r == 1 else tuple(quad_scr.at[j] for j in range(6))

            def step(idx, carry, b=b, r=r, L=L, nq=nq, nk=nk, nblk=nblk, order=order, srcs=srcs):
                qs, ks, vs, dos, lses, dls = srcs
                dq_scr, dk_scr, dv_scr = (acc_scr.at[order, t] for t in range(3))
                qrows, krows, off = _block_rows(idx, r, inter, S, L, nq, nk, nblk)
                case = off // HALF_WIN
                q2 = qs[qrows, :] * SCALE
                k2 = ks[krows, :].astype(bf16)
                v2 = vs[krows, :].astype(bf16)
                do2 = dos[qrows, :]
                lse2 = lses[qrows, :]
                dl2 = dls[qrows, :]
                dq2 = jnp.zeros((nq, PAIR), f32)
                dk2 = jnp.zeros((nk, PAIR), f32)
                dv2 = jnp.zeros((nk, PAIR), f32)
                for hh in range(2):
                    mine = first if hh == 0 else ~first
                    lo = hh * HEAD_DIM
                    qh = jnp.where(mine, q2, 0.0).astype(bf16)
                    doh = jnp.where(mine, do2, 0.0).astype(bf16)
                    s = lax.dot_general(qh, k2, (((1,), (1,)), ((), ())), preferred_element_type=f32)
                    s = s + bias_scr[_bias_index(b, case, hh), 0:nq, 0:nk]
                    p = jnp.exp(s - lse2[:, lo:lo + 1])
                    dv2 = dv2 + lax.dot_general(p.astype(bf16), doh, (((0,), (0,)), ((), ())), preferred_element_type=f32)
                    dp = lax.dot_general(doh, v2, (((1,), (1,)), ((), ())), preferred_element_type=f32)
                    ds = (p * (dp - dl2[:, lo:lo + 1])).astype(bf16)
                    dq2 = dq2 + jnp.where(mine, jnp.dot(ds, k2, preferred_element_type=f32), 0.0)
                    dk2 = dk2 + lax.dot_general(ds, qh, (((0,), (0,)), ((), ())), preferred_element_type=f32)
                dq_scr[qrows, :] = dq_scr[qrows, :] + dq2
                dk_scr[krows, :] = dk_scr[krows, :] + dk2
                dv_scr[krows, :] = dv_scr[krows, :] + dv2
                return carry

            lax.fori_loop(0, S // nq, step, 0, unroll=min(ATTN_UNROLL, S // nq))

        n4 = S // QUAD
        for t in range(3):
            for rho in range(QUAD):
                token_rows = pl.ds(rho, n4, stride=QUAD)
                acc_scr[0, t, token_rows, :] = acc_scr[0, t, token_rows, :] + acc_scr[1, t, pl.ds(rho * n4, n4), :]
        dqkv_ref[0] = (acc_scr[0, 0] * SCALE).astype(bf16)
        dqkv_ref[1] = acc_scr[0, 1].astype(bf16)
        dqkv_ref[2] = acc_scr[0, 2].astype(bf16)

    blk = lambda part: pl.BlockSpec((S, PAIR), lambda p: (0, part * npair + p))
    own = pl.BlockSpec((S, PAIR), lambda p: (0, p))
    return pl.pallas_call(
        body, name="attn_bwd", out_shape=jax.ShapeDtypeStruct(dqkvz.shape, dqkvz.dtype), grid=(npair,),
        in_specs=[blk(4), blk(5), blk(6), own, own, own,
                  pl.BlockSpec((None,) + _bias_shape(S), lambda p: (p, 0, 0, 0)), ANY_SPEC, ANY_SPEC],
        out_specs=pl.BlockSpec((3, S, PAIR), lambda p: (0, 0, p)), input_output_aliases={7: 0},
        scratch_shapes=[pltpu.VMEM((2, 3, S, PAIR), f32), pltpu.VMEM((S, PAIR), f32), pltpu.VMEM((6, S, PAIR), f32)],
        compiler_params=_params("arbitrary"),
    )(proj, proj, proj, o, do, lse, bias, dqkvz, dep)


def _prenorm_bwd(dh, x, dout, scale, g_pre, tr=256):
    S, D = x.shape
    tr = min(tr, S)

    def body(dh_ref, x_ref, dout_ref, sc_ref, g_ref, gx_ref, sums_ref):
        i = pl.program_id(0)
        xv, dhv = x_ref[...], dh_ref[...]
        r = lax.rsqrt(jnp.mean(xv * xv, axis=-1, keepdims=True) + EPS)
        xn = xv * r
        dxn = dhv * (g_ref[...] * (1.0 + sc_ref[...]))
        gx_ref[...] = dout_ref[...] + r * (dxn - xn * jnp.mean(dxn * xn, axis=-1, keepdims=True))
        dhx = dhv * xn
        row = lax.broadcasted_iota(jnp.int32, (8, D), 0)
        upd = jnp.where(row == 0, jnp.sum(dhv, axis=0, keepdims=True),
                        jnp.where(row == 1, jnp.sum(dhx, axis=0, keepdims=True) * g_ref[...],
                                  jnp.where(row == 2, jnp.sum(dhx, axis=0, keepdims=True) * (1.0 + sc_ref[...]), 0.0)))

        @pl.when(i == 0)
        def _():
            sums_ref[...] = upd

        @pl.when(i > 0)
        def _():
            sums_ref[...] += upd

    row = pl.BlockSpec((tr, D), lambda i: (i, 0))
    vec = pl.BlockSpec((1, D), lambda i: (0, 0))
    return pl.pallas_call(
        body, name="prenorm_bwd",
        out_shape=(jax.ShapeDtypeStruct((S, D), f32), jax.ShapeDtypeStruct((8, D), f32)), grid=(S // tr,),
        in_specs=[row, row, row, vec, vec], out_specs=(row, pl.BlockSpec((8, D), lambda i: (0, 0))),
        compiler_params=_params("arbitrary"),
    )(dh, x, dout, scale, g_pre)


def _adamw(w, g, m, v):
    m = ADAM_B1 * m + (1.0 - ADAM_B1) * g
    v = ADAM_B2 * v + (1.0 - ADAM_B2) * (g * g)
    m_hat = m / (1.0 - ADAM_B1 ** ADAM_STEP)
    v_hat = v / (1.0 - ADAM_B2 ** ADAM_STEP)
    delta = -ADAM_LR * (m_hat / (jnp.sqrt(v_hat) + ADAM_EPS) + ADAM_WD * w)
    return delta, m, v


def _sum_rows(parts, dep):
    P = parts.shape[1]

    def body(p_ref, dep_ref, o_ref):
        acc = p_ref[0:1, :]
        for j in range(1, NDEV):
            acc = acc + p_ref[j:j + 1, :]
        o_ref[...] = jnp.broadcast_to(acc, (8, P))

    vmem = pl.BlockSpec(memory_space=pltpu.VMEM)
    return pl.pallas_call(body, name="sum_small", out_shape=jax.ShapeDtypeStruct((8, P), f32),
                          in_specs=[vmem, ANY_SPEC], out_specs=vmem, compiler_params=_params())(parts, dep)


def _adamw_small(tot, params):
    given = [p[3] for p in params if not isinstance(p[3], int)]

    def body(tot_ref, *refs):
        given_refs = list(refs[:len(given)])
        ins = refs[len(given):len(given) + 3 * len(params)]
        outs = refs[len(given) + 3 * len(params):]
        for t, (w, _, _, where) in enumerate(params):
            w_ref, m_ref, v_ref = ins[3 * t:3 * t + 3]
            g = tot_ref[0:1, where:where + w.size] if isinstance(where, int) else given_refs.pop(0)[...]
            outs[4 * t][...] = g
            outs[4 * t + 1][...], outs[4 * t + 2][...], outs[4 * t + 3][...] = _adamw(w_ref[...], g, m_ref[...], v_ref[...])

    out_shape = tuple(jax.ShapeDtypeStruct(p[0].shape, f32) for p in params for _ in range(4))
    res = pl.pallas_call(body, name="adamw_small", out_shape=out_shape, compiler_params=_params())(
        tot, *given, *[a for p in params for a in p[:3]])
    return [res[4 * t:4 * t + 4] for t in range(len(params))]


def _adamw_sharded(name, parts, sums_a, sums_b, pick, w, m, v, rows=None, prev=None, tr=128):
    R, Cc = w.shape
    r0, nr = rows or (0, R)
    tr = math.gcd(tr, r0, nr)
    n, b0 = parts.shape[0], r0 // tr

    def body(pick_ref, p_ref, a_ref, b_ref, w_ref, m_ref, v_ref, *rest):
        g_ref, d_ref, nm_ref, nv_ref = rest[-4:]
        g = jnp.where(pick_ref[0] == 1, b_ref[...], a_ref[...]).astype(f32)
        for j in range(n):
            g = g + p_ref[j].astype(f32)
        g_ref[...] = g
        d_ref[...], nm_ref[...], nv_ref[...] = _adamw(w_ref[...], g, m_ref[...], v_ref[...])

    row = pl.BlockSpec((tr, Cc), lambda i, pick: (i + b0, 0))
    mine = pl.BlockSpec((None, tr, Cc), lambda i, pick: (pick[1], i + b0, 0))
    out = jax.ShapeDtypeStruct((R, Cc), f32)
    prev = list(prev or [])
    grid_spec = pltpu.PrefetchScalarGridSpec(
        num_scalar_prefetch=1, grid=(nr // tr,),
        in_specs=[pl.BlockSpec((n, tr, Cc), lambda i, pick: (0, i + b0, 0)), mine, mine, row, row, row]
        + [ANY_SPEC] * len(prev),
        out_specs=(row, row, row, row))
    return pl.pallas_call(
        body, name=name, out_shape=(out, out, out, out), grid_spec=grid_spec,
        input_output_aliases={7 + t: t for t in range(len(prev))}, compiler_params=_params("arbitrary"),
    )(pick, parts, sums_a, sums_b, w, m, v, *prev)


def _adamw_ada(c_t, dmod_cols, w, m, v, dep, tr=512):
    D, W = w.shape
    tr = min(tr, D)

    def body(c_ref, dm_ref, w_ref, m_ref, v_ref, dep_ref, g_ref, d_ref, nm_ref, nv_ref):
        cv, dm = c_ref[...], dm_ref[...]
        g = cv[:, 0:1] * dm[0:1, :]
        for b in range(1, NDEV):
            g = g + cv[:, b:b + 1] * dm[b:b + 1, :]
        g_ref[...] = g
        d_ref[...], nm_ref[...], nv_ref[...] = _adamw(w_ref[...], g, m_ref[...], v_ref[...])

    row = pl.BlockSpec((tr, W), lambda i: (i, 0))
    out = jax.ShapeDtypeStruct((D, W), f32)
    return pl.pallas_call(
        body, name="adamw_ada", out_shape=(out, out, out, out), grid=(D // tr,),
        in_specs=[pl.BlockSpec((tr, NDEV), lambda i: (i, 0)), pl.BlockSpec((NDEV, W), lambda i: (0, 0)), row, row, row,
                  ANY_SPEC],
        out_specs=(row, row, row, row), compiler_params=_params("parallel"),
    )(c_t, dmod_cols, w, m, v, dep)


def kernel(x, c, w_ada, b_ada, g_pre, w_in, conv_w, conv_b, g_conv, g_attn, w_out, g_post, loss_target, m_w_ada, m_b_ada, m_g_pre, m_w_in, m_conv_w, m_conv_b, m_g_conv, m_g_attn, m_w_out, m_g_post, v_w_ada, v_b_ada, v_g_pre, v_w_in, v_conv_w, v_conv_b, v_g_conv, v_g_attn, v_w_out, v_g_post):
    S, D = x.shape[1], x.shape[2]
    C = D // 2
    W = w_ada.shape[2]
    CW = conv_w.shape[2]
    me = 4 * lax.axis_index("x") + 2 * lax.axis_index("y") + lax.axis_index("c")
    x2, tgt = x[0], loss_target[0]
    w_ada2, w_in2, w_out2 = w_ada[0], w_in[0], w_out[0]

    R = D // NDEV
    core = lax.axis_index("c").astype(jnp.int32).reshape(1)

    cw_slab = jnp.zeros((8, CW), f32).at[:3].set(conv_w[0])
    b_cols = lax.dynamic_slice_in_dim(b_ada, me * W, W, axis=1)
    mod_slabs, c_blocks, cw_g = _ada_exchange(c.reshape(D // 128, 128), cw_slab, w_ada2, b_cols)
    c_all = c_blocks.reshape(NDEV, D)
    conv_w_full = jnp.transpose(cw_g, (1, 0, 2)).reshape(8, C)
    mod = mod_slabs[:, 0, :].reshape(1, 3 * D)
    shift, scale, gate = mod[:, :D], mod[:, D:2 * D], mod[:, 2 * D:]

    land_i = lax.dynamic_update_slice(lax.empty((NDEV, D, C), bf16), w_in2.astype(bf16)[None], (me, 0, 0))
    land_o = lax.dynamic_update_slice(lax.empty((NDEV, R, D), bf16), w_out2.astype(bf16)[None], (me, 0, 0))
    wi_send, wi_recv, land_i, w_token = _w_in_start(land_i, [mod_slabs])

    me_arr = me.astype(jnp.int32).reshape(1)
    h = _prenorm(x2, scale, shift, g_pre, w_token)
    land_i = _w_in_sibling(land_i, wi_recv, after=[h])
    proj = _in_proj_part("in_proj_a", h, land_i, None, me_arr, 0, 1, 2)
    x_minus_t = _residual_minus_target(x2, tgt, proj)
    bias = _bias_tiles(_head_slopes(C // HEAD_DIM), S, x_minus_t)

    def landing(rows, cols):
        return lax.dynamic_update_slice(lax.empty((NCHIP, rows, cols), bf16), jnp.zeros((1, rows, cols), bf16),
                                        (me // 2, 0, 0))

    land_go, land_gi = landing(R, D), landing(D, C)
    fi_send, fi_recv, land_i = _w_in_relay(land_i, wi_recv, after=[proj, bias, land_go, land_gi])
    proj = _in_proj_part("in_proj_b", h, land_i, proj, me_arr, 2, 2, 2)
    land_i = _w_in_forwarded(land_i, fi_recv, after=[proj])
    proj = _in_proj_part("in_proj_c", h, land_i, proj, me_arr, 3, 2, 2)
    (di_send, di_recv, wo_send, wo_recv), land_i, land_o = _w_in_diag(land_i, land_o, fi_recv, after=[proj])
    proj = _in_proj_part("in_proj_d", h, land_i, proj, me_arr, 6, 1, 1)
    win_g = _w_in_finish(land_i, wi_send, fi_send, di_send, di_recv, after=[proj])
    proj = _in_proj_part("in_proj_e", h, win_g, proj, me_arr, 7, 1, 1)
    ycat = _conv_fwd(proj, conv_w_full, conv_b, g_conv)
    o, lse = _attn_fwd(proj, bias)
    (fo_send, fo_recv), (land_o,), _ = _weights_forward("w_out_forward", land_o, wo_recv, after=[o])
    ycat = _attn_post(ycat, o, proj, g_attn)
    wout_g = _weights_wait("w_out_wait", land_o, wo_send, wo_recv, fo_send, fo_recv, after=[ycat])
    wout_full = wout_g.reshape(D, D)
    y = _matmul(ycat, wout_full, name="out_proj", out_dtype=f32)
    dy, dout, post_sums = _sandwich(y, x_minus_t, gate, g_post)

    gw_out = _matmul(ycat, dy, name="out_proj_dw", out_dtype=bf16, ta=True).reshape(NDEV, R, D)
    po_send, po_recv, gw_out, pair_o, po_token = _pair_start("g_out_pair_start", gw_out)
    dycat = _matmul(dy, wout_full, name="out_proj_dx", out_dtype=f32, tb=True, dep=po_token)
    gw_out, pair_o = _pair_wait("g_out_pair_wait", gw_out, pair_o, po_send, po_recv, after=[dycat])
    sum_o = _pair_sum("g_out_pair_sum", gw_out, pair_o, core)
    co_send, co_recv, sum_o, land_go, co_token = _chip_start(
        "g_out_chip_start", sum_o, land_go, 0)
    dpc, conv_sums = _conv_bwd(proj, dycat, conv_w_full, conv_b, g_conv, co_token)
    gw_c = _matmul(h, dpc, name="in_proj_dw_conv", out_dtype=bf16, ta=True, out_slots=4)
    pc_send, pc_recv, gw_c, pair_c, pc_token = _pair_start("g_conv_pair_start", gw_c)
    do, dpa, attn_sums = _attn_post_bwd(o, proj, dycat, g_attn, pc_token)
    gw_c, pair_c = _pair_wait("g_conv_pair_wait", gw_c, pair_c, pc_send, pc_recv, after=[do])
    sum_c = _pair_sum("g_conv_pair_sum", gw_c, pair_c, core)
    cc_send, cc_recv, sum_c, land_gi, cc_token = _chip_start(
        "g_conv_chip_start", sum_c, land_gi, 0)
    dpa = _attn_bwd(proj, o, do, lse, bias, dpa, cc_token)
    gw_a = _matmul(h, dpa, name="in_proj_dw_attn", out_dtype=bf16, ta=True, b_slots=True, out_slots=4)
    pa_send, pa_recv, gw_a, pair_a, pa_token = _pair_start("g_attn_pair_start", gw_a)
    sum_o, land_go = _chip_wait("g_out_chip_wait", sum_o, land_go, co_send, co_recv, 0, after=[pa_token])
    pick_out = jnp.stack([jnp.int32(0), me // 2]).astype(jnp.int32)
    g_w_out, d_w_out, nm_w_out, nv_w_out = _adamw_sharded(
        "adamw_w_out", land_go, sum_o, sum_o, pick_out, w_out2, m_w_out[0], v_w_out[0])
    gw_a, pair_a = _pair_wait("g_attn_pair_wait", gw_a, pair_a, pa_send, pa_recv, after=[g_w_out])
    sum_a = _pair_sum("g_attn_pair_sum", gw_a, pair_a, core)
    part_a, part_b = (0, 3 * D // 4), (3 * D // 4, D // 4)
    ca_send, ca_recv, sum_a, land_gi, ca_token = _chip_start("g_attn_chip_start_a", sum_a, land_gi, 4, part_a)
    dh = _matmul_slabs_t(dpc, dpa, win_g, name="in_proj_dx", dep=ca_token)
    grad_x, pre_sums = _prenorm_bwd(dh, x2, dout, scale, g_pre)

    small = jnp.concatenate([pre_sums[0:1], pre_sums[1:2], post_sums[0:1],
                             pre_sums[2:3], post_sums[1:2],
                             conv_sums[2:3], conv_sums[3:4], conv_sums[4:5],
                             conv_sums[1:2], conv_sums[0:1], attn_sums[0:1]], axis=1)
    small = jnp.concatenate([small.reshape(8 * D // 128, 128), jnp.broadcast_to(post_sums[2:3, :128], (8, 128))])
    (small_all,) = _all_gather([small], "gather_small")
    cb_send, cb_recv, sum_a, land_gi, cb_token = _chip_start("g_attn_chip_start_b", sum_a, land_gi, 4, part_b,
                                                             after=[small_all])
    small_all = small_all.reshape(NDEV, small.size)
    tot = _sum_rows(small_all, cb_token)
    loss = tot[0, 8 * D]
    g_conv_w = lax.dynamic_slice_in_dim(tot[0:1, 5 * D:5 * D + 3 * C].reshape(1, 3, C), me * CW, CW, axis=2)
    ((g_b_ada, d_b_ada, nm_b_ada, nv_b_ada), (g_g_pre, d_g_pre, nm_g_pre, nv_g_pre),
     (g_g_post, d_g_post, nm_g_post, nv_g_post), (g_conv_w, d_conv_w, nm_conv_w, nv_conv_w),
     (g_conv_b, d_conv_b, nm_conv_b, nv_conv_b), (g_g_conv, d_g_conv, nm_g_conv, nv_g_conv),
     (g_g_attn, d_g_attn, nm_g_attn, nv_g_attn)) = _adamw_small(tot, [
         (b_ada, m_b_ada, v_b_ada, 0), (g_pre, m_g_pre, v_g_pre, 3 * D), (g_post, m_g_post, v_g_post, 4 * D),
         (conv_w, m_conv_w, v_conv_w, g_conv_w), (conv_b, m_conv_b, v_conv_b, 5 * D + 3 * C),
         (g_conv, m_g_conv, v_g_conv, 5 * D + 4 * C), (g_attn, m_g_attn, v_g_attn, 5 * D + 5 * C)])

    dmod_cols = lax.dynamic_slice_in_dim(small_all[:, :3 * D], me * W, W, axis=1)
    g_w_ada, d_w_ada, nm_w_ada, nv_w_ada = _adamw_ada(c_all.T, dmod_cols, w_ada2, m_w_ada[0], v_w_ada[0], cb_token)

    pick_in = jnp.stack([me // 4, (me % 4) // 2]).astype(jnp.int32)
    sum_c, land_gi = _chip_wait("g_conv_chip_wait", sum_c, land_gi, cc_send, cc_recv, 0, after=[g_w_ada])
    sum_a, land_gi = _chip_wait("g_attn_chip_wait_a", sum_a, land_gi, ca_send, ca_recv, 4, [g_w_ada], part_a)
    first = _adamw_sharded("adamw_w_in_a", land_gi, sum_c, sum_a, pick_in, w_in2, m_w_in[0], v_w_in[0], rows=part_a,
                           tr=256)
    sum_a, land_gi = _chip_wait("g_attn_chip_wait_b", sum_a, land_gi, cb_send, cb_recv, 4, [first[0]], part_b)
    g_w_in, d_w_in, nm_w_in, nv_w_in = _adamw_sharded(
        "adamw_w_in_b", land_gi, sum_c, sum_a, pick_in, w_in2, m_w_in[0], v_w_in[0], rows=part_b, prev=first, tr=256)

    return (loss, grad_x[None],
            g_w_ada[None], g_b_ada, g_g_pre, g_w_in[None], g_conv_w, g_conv_b, g_g_conv, g_g_attn, g_w_out[None], g_g_post,
            d_w_ada[None], d_b_ada, d_g_pre, d_w_in[None], d_conv_w, d_conv_b, d_g_conv, d_g_attn, d_w_out[None], d_g_post,
            nm_w_ada[None], nm_b_ada, nm_g_pre, nm_w_in[None], nm_conv_w, nm_conv_b, nm_g_conv, nm_g_attn, nm_w_out[None], nm_g_post,
            nv_w_ada[None], nv_b_ada, nv_g_pre, nv_w_in[None], nv_conv_w, nv_conv_b, nv_g_conv, nv_g_attn, nv_w_out[None], nv_g_post)
```

```python
import functools
import math

import jax
import jax.numpy as jnp
from jax import lax
from jax.experimental import pallas as pl
from jax.experimental.pallas import tpu as pltpu

f32 = jnp.float32
bf16 = jnp.bfloat16

NDEV = 8
HEAD_DIM = 64
PAIR = 2 * HEAD_DIM
BRANCHES = ((128, 1, 1), (512, 4, 1), (2048, 16, 2))
HALF_WIN = 64
EPS = 1e-6
NEG_INF = -1e30
ADAM_LR, ADAM_B1, ADAM_B2, ADAM_EPS, ADAM_WD, ADAM_STEP = 0.001, 0.9, 0.999, 1e-08, 0.01, 10
MESH = pl.DeviceIdType.MESH
VMEM_LIMIT = 56 * 1024 * 1024
HBM_SPEC = pl.BlockSpec(memory_space=pltpu.HBM)
ANY_SPEC = pl.BlockSpec(memory_space=pl.ANY)
SEM_SPEC = pl.BlockSpec(memory_space=pltpu.SEMAPHORE)


def _params(*sem):
    return pltpu.CompilerParams(dimension_semantics=sem or None, vmem_limit_bytes=VMEM_LIMIT)


def _silu(z):
    return z * jax.nn.sigmoid(z)


def _silu_and_slope(z):
    s = jax.nn.sigmoid(z)
    return z * s, s * (1.0 + z * (1.0 - s))


def _my_place():
    x, y, c = lax.axis_index("x"), lax.axis_index("y"), lax.axis_index("c")
    return x, y, c, 4 * x + 2 * y + c


def _peer(x, y, c, k):
    px, py, pc = x ^ (k >> 2 & 1), y ^ (k >> 1 & 1), c ^ (k & 1)
    return (px, py, pc), 4 * px + 2 * py + pc


def _all_gather(arrays, name):
    n = len(arrays)

    def body(*refs):
        srcs, dsts = refs[:n], refs[n:2 * n]
        send_sems, recv_sems, local_sems = refs[2 * n:]
        x, y, c, me = _my_place()
        locals_, sends = [], []
        for t in range(n):
            own = pltpu.make_async_copy(srcs[t], dsts[t].at[me], local_sems.at[t])
            own.start()
            locals_.append(own)
            for k in range(1, NDEV):
                peer, pidx = _peer(x, y, c, k)
                cp = pltpu.make_async_remote_copy(
                    src_ref=srcs[t], dst_ref=dsts[t].at[me], send_sem=send_sems.at[t, k],
                    recv_sem=recv_sems.at[t, k], device_id=peer, device_id_type=MESH)
                cp.start()
                sends.append(cp)
        for t in range(n):
            for k in range(1, NDEV):
                peer, pidx = _peer(x, y, c, k)
                pltpu.make_async_remote_copy(
                    src_ref=srcs[t], dst_ref=dsts[t].at[pidx], send_sem=send_sems.at[t, k],
                    recv_sem=recv_sems.at[t, k], device_id=peer, device_id_type=MESH).wait_recv()
        for cp in sends:
            cp.wait_send()
        for cp in locals_:
            cp.wait()

    return pl.pallas_call(
        body, name=name,
        out_shape=tuple(jax.ShapeDtypeStruct((NDEV,) + a.shape, a.dtype) for a in arrays),
        in_specs=[HBM_SPEC] * n, out_specs=tuple([HBM_SPEC] * n),
        scratch_shapes=[pltpu.SemaphoreType.DMA((n, NDEV)), pltpu.SemaphoreType.DMA((n, NDEV)),
                        pltpu.SemaphoreType.DMA((n,))],
    )(*arrays)


def _comm_call(name, arrays, sems, new_sems, body, after=(), token=False):
    na, ns, nn, nf = len(arrays), len(sems), len(new_sems), len(after)

    def kern(*refs):
        ins, outs = refs[:na + ns + nf], refs[na + ns + nf:]
        body(ins[:na], ins[na:na + ns], outs[:nn])
        if token:
            outs[nn + na][...] = jnp.zeros((8, 128), f32)

    out_shape = ([pltpu.SemaphoreType.DMA(s) for s in new_sems] + [pltpu.HBM(a.shape, a.dtype) for a in arrays]
                 + ([jax.ShapeDtypeStruct((8, 128), f32)] if token else []))
    out_specs = [SEM_SPEC] * nn + [HBM_SPEC] * na + ([pl.BlockSpec(memory_space=pltpu.VMEM)] if token else [])
    res = pl.pallas_call(
        kern, name=name, out_shape=tuple(out_shape),
        in_specs=[HBM_SPEC] * na + [SEM_SPEC] * ns + [ANY_SPEC] * nf, out_specs=tuple(out_specs),
        input_output_aliases={t: nn + t for t in range(na)},
        compiler_params=pltpu.CompilerParams(has_side_effects=pltpu.SideEffectType.DATAFLOW_SIDE_EFFECTING),
    )(*[pltpu.with_memory_space_constraint(a, pltpu.HBM) for a in arrays], *sems, *after)
    return list(res[:nn]), list(res[nn:nn + na]), (res[nn + na] if token else None)


def _remote(src, dst, send_sem, recv_sem, device):
    return pltpu.make_async_remote_copy(src_ref=src, dst_ref=dst, send_sem=send_sem, recv_sem=recv_sem,
                                        device_id=device, device_id_type=MESH)


SAME_CORE = (2, 4, 6)
VIA_SIBLING = (3, 5, 7)


def _weights_forward(name, land, recv, after):
    def body(a, s, new):
        (land,), (recv,), (fsend, frecv) = a, s, new
        x, y, c, me = _my_place()
        sibling, _ = _peer(x, y, c, 1)
        for k in SAME_CORE:
            peer, slot = _peer(x, y, c, k)
            _remote(land.at[slot], land.at[slot], fsend.at[k], recv.at[k], peer).wait_recv()
            _remote(land.at[slot], land.at[slot], fsend.at[k], frecv.at[k ^ 1], sibling).start()

    return _comm_call(name, [land], [recv], [(NDEV,), (NDEV,)], body, after=after)


def _weights_wait(name, land, send, recv, fsend, frecv, after):
    def body(a, s, new):
        (land,), (send, recv, fsend, frecv) = a, s
        x, y, c, me = _my_place()
        sibling, sib_slot = _peer(x, y, c, 1)
        _remote(land.at[sib_slot], land.at[sib_slot], send.at[1], recv.at[1], sibling).wait_recv()
        for k in VIA_SIBLING:
            _, slot = _peer(x, y, c, k)
            _remote(land.at[slot], land.at[slot], fsend.at[k ^ 1], frecv.at[k], sibling).wait_recv()
        for k in (1,) + SAME_CORE:
            peer, _ = _peer(x, y, c, k)
            _remote(land.at[me], land.at[me], send.at[k], recv.at[k], peer).wait_send()
        for k in SAME_CORE:
            _, slot = _peer(x, y, c, k)
            _remote(land.at[slot], land.at[slot], fsend.at[k], frecv.at[k ^ 1], sibling).wait_send()

    return _comm_call(name, [land], [send, recv, fsend, frecv], [], body, after=after)[1][0]


def _diag_relay(x, y, c):
    slot = 4 * (x ^ (1 - c)) + 2 * (y ^ c) + c
    return slot, (x ^ c, y ^ (1 - c), c)


def _w_in_start(land, after):
    def body(a, s, new):
        (land,), (send, recv) = a, new
        x, y, c, me = _my_place()
        for k in (1, 2, 4):
            peer, _ = _peer(x, y, c, k)
            _remote(land.at[me], land.at[me], send.at[k], recv.at[k], peer).start()

    (send, recv), (land,), token = _comm_call("w_in_start", [land], [], [(NDEV,), (NDEV,)], body, after=after, token=True)
    return send, recv, land, token


def _w_in_sibling(land, recv, after):
    def body(a, s, new):
        (land,), (recv,) = a, s
        x, y, c, me = _my_place()
        sibling, slot = _peer(x, y, c, 1)
        _remote(land.at[slot], land.at[slot], recv.at[1], recv.at[1], sibling).wait_recv()

    return _comm_call("w_in_sibling", [land], [recv], [], body, after=after)[1][0]


def _w_in_relay(land, recv, after):
    def body(a, s, new):
        (land,), (recv,), (fsend, frecv) = a, s, new
        x, y, c, me = _my_place()
        sibling, _ = _peer(x, y, c, 1)
        for k in (2, 4):
            peer, slot = _peer(x, y, c, k)
            _remote(land.at[slot], land.at[slot], fsend.at[k], recv.at[k], peer).wait_recv()
        slot, target = _diag_relay(x, y, c)
        _remote(land.at[slot], land.at[slot], fsend.at[6], frecv.at[6], target).start()
        for k in (2, 4):
            _, slot = _peer(x, y, c, k)
            _remote(land.at[slot], land.at[slot], fsend.at[k], frecv.at[k ^ 1], sibling).start()

    (fsend, frecv), (land,), _ = _comm_call("w_in_relay", [land], [recv], [(NDEV,), (NDEV,)], body, after=after)
    return fsend, frecv, land


def _w_in_forwarded(land, frecv, after):
    def body(a, s, new):
        (land,), (frecv,) = a, s
        x, y, c, me = _my_place()
        sibling, _ = _peer(x, y, c, 1)
        for k in (3, 5):
            _, slot = _peer(x, y, c, k)
            _remote(land.at[slot], land.at[slot], frecv.at[k], frecv.at[k], sibling).wait_recv()

    return _comm_call("w_in_forwarded", [land], [frecv], [], body, after=after)[1][0]


def _w_in_diag(land, land_o, frecv, after):
    def body(a, s, new):
        (land, land_o), (frecv,), (dsend, drecv, osend, orecv) = a, s, new
        x, y, c, me = _my_place()
        sibling, _ = _peer(x, y, c, 1)
        peer, slot = _peer(x, y, c, 6)
        _remote(land.at[slot], land.at[slot], dsend.at[6], frecv.at[6], peer).wait_recv()
        _remote(land.at[slot], land.at[slot], dsend.at[6], drecv.at[7], sibling).start()
        for k in (1,) + SAME_CORE:
            peer, _ = _peer(x, y, c, k)
            _remote(land_o.at[me], land_o.at[me], osend.at[k], orecv.at[k], peer).start()

    sems, (land, land_o), _ = _comm_call("w_in_diag", [land, land_o], [frecv], [(NDEV,)] * 4, body, after=after)
    return sems, land, land_o


def _w_in_finish(land, send, fsend, dsend, drecv, after):
    def body(a, s, new):
        (land,), (send, fsend, dsend, drecv) = a, s
        x, y, c, me = _my_place()
        sibling, _ = _peer(x, y, c, 1)
        _, slot = _peer(x, y, c, 7)
        _remote(land.at[slot], land.at[slot], dsend.at[6], drecv.at[7], sibling).wait_recv()
        for k in (1, 2, 4):
            peer, _ = _peer(x, y, c, k)
            _remote(land.at[me], land.at[me], send.at[k], send.at[k], peer).wait_send()
        for k in (2, 4, 6):
            _, slot = _peer(x, y, c, k)
            _remote(land.at[slot], land.at[slot], fsend.at[k], fsend.at[k], sibling).wait_send()
        _, slot = _peer(x, y, c, 6)
        _remote(land.at[slot], land.at[slot], dsend.at[6], dsend.at[6], sibling).wait_send()

    return _comm_call("w_in_finish", [land], [send, fsend, dsend, drecv], [], body, after=after)[1][0]


def _in_proj_part(name, h, land, proj, me_arr, k0, kstep, nk, tm=512):
    S, D = h.shape
    C = land.shape[2]
    tm = min(tm, S)

    def body(me_ref, a_ref, b_ref, *rest):
        rest[-1][...] = jnp.dot(a_ref[...], b_ref[...], preferred_element_type=f32)

    slot = lambda j, me: me[0] ^ (k0 + kstep * j)
    args = [h, land] + ([] if proj is None else [proj])
    grid_spec = pltpu.PrefetchScalarGridSpec(
        num_scalar_prefetch=1, grid=(nk, S // tm),
        in_specs=[pl.BlockSpec((tm, D), lambda j, i, me: (i, 0)),
                  pl.BlockSpec((None, D, C), lambda j, i, me: (slot(j, me), 0, 0))] + [ANY_SPEC] * (len(args) - 2),
        out_specs=pl.BlockSpec((tm, C), lambda j, i, me: (i, slot(j, me))))
    return pl.pallas_call(
        body, name=name, out_shape=jax.ShapeDtypeStruct((S, NDEV * C), f32), grid_spec=grid_spec,
        input_output_aliases={} if proj is None else {3: 0}, compiler_params=_params("arbitrary", "arbitrary"),
    )(me_arr, *args)


NCHIP = NDEV // 2


def _pair_start(name, src):
    npair = src.shape[0] // 2

    def body(a, s, new):
        (src, pair), (send, recv) = a, new
        x, y, c, me = _my_place()
        sibling, _ = _peer(x, y, c, 1)
        for i in range(npair):
            _remote(src.at[2 * i + 1 - c], pair.at[i], send.at[i], recv.at[i], sibling).start()

    pair = lax.empty((npair,) + src.shape[1:], src.dtype)
    (send, recv), (src, pair), token = _comm_call(name, [src, pair], [], [(npair,), (npair,)], body, token=True)
    return send, recv, src, pair, token


def _pair_wait(name, src, pair, send, recv, after):
    npair = pair.shape[0]

    def body(a, s, new):
        (src, pair), (send, recv) = a, s
        x, y, c, me = _my_place()
        sibling, _ = _peer(x, y, c, 1)
        for i in range(npair):
            cp = _remote(src.at[2 * i + 1 - c], pair.at[i], send.at[i], recv.at[i], sibling)
            cp.wait_recv()
            cp.wait_send()

    return _comm_call(name, [src, pair], [send, recv], [], body, after=after)[1]


def _pair_sum(name, src, pair, core, tr=1024):
    npair, R, Cc = pair.shape
    tr = min(tr, R)

    def body(core_ref, a_ref, b_ref, o_ref):
        o_ref[...] = (a_ref[...].astype(f32) + b_ref[...].astype(f32)).astype(o_ref.dtype)

    grid_spec = pltpu.PrefetchScalarGridSpec(
        num_scalar_prefetch=1, grid=(npair, R // tr),
        in_specs=[pl.BlockSpec((None, tr, Cc), lambda i, r, core: (2 * i + core[0], r, 0)),
                  pl.BlockSpec((None, tr, Cc), lambda i, r, core: (i, r, 0))],
        out_specs=pl.BlockSpec((None, tr, Cc), lambda i, r, core: (i, r, 0)))
    return pl.pallas_call(body, name=name, out_shape=jax.ShapeDtypeStruct(pair.shape, pair.dtype),
                          grid_spec=grid_spec, compiler_params=_params("parallel", "parallel"))(core, src, pair)


def _owner_chip(first, i):
    q = first // 2 + i
    return q >> 1 & 1, q & 1


def _chip_start(name, sums, land, first, rows=None, after=()):
    npair = sums.shape[0]
    rows = pl.ds(*(rows or (0, sums.shape[1])))

    def body(a, s, new):
        (sums, land), (send, recv) = a, new
        x, y, c, me = _my_place()
        for i in range(npair):
            ox, oy = _owner_chip(first, i)

            @pl.when((x != ox) | (y != oy))
            def _():
                _remote(sums.at[i, rows], land.at[2 * x + y, rows], send.at[i], recv.at[2 * x + y], (ox, oy, c)).start()

    (send, recv), (sums, land), token = _comm_call(name, [sums, land], [], [(npair,), (NCHIP,)], body, after=after,
                                                   token=True)
    return send, recv, sums, land, token


def _chip_wait(name, sums, land, send, recv, first, after, rows=None):
    npair = sums.shape[0]
    rows = pl.ds(*(rows or (0, sums.shape[1])))

    def body(a, s, new):
        (sums, land), (send, recv) = a, s
        x, y, c, me = _my_place()
        mine = (me >= first) & (me < first + 2 * npair)
        for i in range(npair):
            ox, oy = _owner_chip(first, i)

            @pl.when((x != ox) | (y != oy))
            def _():
                _remote(sums.at[i, rows], land.at[2 * x + y, rows], send.at[i], recv.at[2 * x + y], (ox, oy, c)).wait_send()
        for q in range(NCHIP):
            @pl.when(mine & (2 * x + y != q))
            def _():
                _remote(sums.at[0, rows], land.at[q, rows], send.at[0], recv.at[q], (q >> 1, q & 1, c)).wait_recv()

    return _comm_call(name, [sums, land], [send, recv], [], body, after=after)[1]


def _matmul(a, b, *, name, out_dtype, ta=False, tb=False, b_slots=False, out_slots=0, b_cols=None,
            tm=1024, tn=1024, tk=2048, dep=None):
    M, K = (a.shape[1], a.shape[0]) if ta else a.shape
    col0 = 0
    if b_slots:
        slab = b.shape[2]
        N = b.shape[1] if tb else b.shape[0] * slab
        assert (K if tb else N) == b.shape[0] * slab
    elif b_cols is not None:
        assert not tb
        col0, N = b_cols
    else:
        N = b.shape[0] if tb else b.shape[1]
    tm, tn, tk = min(tm, M), min(tn, N), min(tk, K)
    if b_slots:
        if tb:
            tk = min(tk, slab)
        else:
            tn = min(tn, slab)
    if out_slots:
        tn = min(tn, N // out_slots)
    nm, nn, nk = M // tm, N // tn, K // tk
    assert (nm * tm, nn * tn, nk * tk) == (M, N, K) and col0 % tn == 0, (name, M, N, K, tm, tn, tk)
    j0 = col0 // tn

    a_spec = pl.BlockSpec((tk, tm), lambda i, j, k: (k, i)) if ta else pl.BlockSpec((tm, tk), lambda i, j, k: (i, k))
    if b_slots and tb:
        per = slab // tk
        b_spec = pl.BlockSpec((None, tn, tk), lambda i, j, k: (k // per, j, k % per))
    elif b_slots:
        per = slab // tn
        b_spec = pl.BlockSpec((None, tk, tn), lambda i, j, k: (j // per, k, j % per))
    elif tb:
        b_spec = pl.BlockSpec((tn, tk), lambda i, j, k: (j, k))
    else:
        b_spec = pl.BlockSpec((tk, tn), lambda i, j, k: (k, j + j0))
    if out_slots:
        per_o = (N // out_slots) // tn
        o_spec = pl.BlockSpec((None, tm, tn), lambda i, j, k: (j // per_o, i, j % per_o))
        out_shape = jax.ShapeDtypeStruct((out_slots, M, N // out_slots), out_dtype)
    else:
        o_spec = pl.BlockSpec((tm, tn), lambda i, j, k: (i, j))
        out_shape = jax.ShapeDtypeStruct((M, N), out_dtype)
    dims = (((0 if ta else 1,), (1 if tb else 0,)), ((), ()))
    deps = [] if dep is None else [dep]

    def body(a_ref, b_ref, *rest):
        o_ref = rest[len(deps)]
        prod = lax.dot_general(a_ref[...], b_ref[...], dims, preferred_element_type=f32)
        if nk == 1:
            o_ref[...] = prod.astype(out_dtype)
            return
        acc_ref = rest[len(deps) + 1]
        k = pl.program_id(2)

        @pl.when(k == 0)
        def _():
            acc_ref[...] = prod

        @pl.when((k > 0) & (k < nk - 1))
        def _():
            acc_ref[...] += prod

        @pl.when(k == nk - 1)
        def _():
            o_ref[...] = (acc_ref[...] + prod).astype(out_dtype)

    return pl.pallas_call(
        body, name=name, out_shape=out_shape, grid=(nm, nn, nk),
        in_specs=[a_spec, b_spec] + [ANY_SPEC] * len(deps), out_specs=o_spec,
        scratch_shapes=[pltpu.VMEM((tm, tn), f32)] if nk > 1 else [],
        compiler_params=_params("parallel", "parallel", "arbitrary"),
    )(a, b, *deps)


def _matmul_slabs_t(a_cols, a_slots, b, *, name, tm=512, tn=512, dep=None):
    M = a_cols.shape[0]
    n_slab, N, slab = b.shape
    n1, n2 = a_cols.shape[1] // slab, a_slots.shape[0]
    assert n1 + n2 == n_slab and a_slots.shape[1:] == (M, slab)
    tm, tn = min(tm, M), min(tn, N)
    deps = [] if dep is None else [dep]

    def body(a1_ref, a2_ref, b_ref, *rest):
        o_ref = rest[len(deps)]
        acc = None
        for s in range(n_slab):
            lhs = a1_ref[:, s * slab:(s + 1) * slab] if s < n1 else a2_ref[s - n1]
            prod = lax.dot_general(lhs, b_ref[s], (((1,), (1,)), ((), ())), preferred_element_type=f32)
            acc = prod if acc is None else acc + prod
        o_ref[...] = acc

    return pl.pallas_call(
        body, name=name, out_shape=jax.ShapeDtypeStruct((M, N), f32), grid=(M // tm, N // tn),
        in_specs=[pl.BlockSpec((tm, n1 * slab), lambda i, j: (i, 0)), pl.BlockSpec((n2, tm, slab), lambda i, j: (0, i, 0)),
                  pl.BlockSpec((n_slab, tn, slab), lambda i, j: (0, j, 0))] + [ANY_SPEC] * len(deps),
        out_specs=pl.BlockSpec((tm, tn), lambda i, j: (i, j)), compiler_params=_params("parallel", "parallel"),
    )(a_cols, a_slots, b, *deps)


def _ada_exchange(c_blk, cw_slab, w_ada, b_cols):
    nblk = c_blk.shape[0]
    D, W = w_ada.shape
    CW = cw_slab.shape[1]

    def body(c_ref, cw_ref, w_ref, b_ref, mod_ref, call_ref, cwg_ref, msend, send_sems, recv_sems):
        x, y, c, me = _my_place()
        call_ref[me] = _silu(c_ref[...])
        cwg_ref[me] = cw_ref[...]
        first = []
        for k in range(1, NDEV):
            peer, _ = _peer(x, y, c, k)
            first.append(_remote(call_ref.at[me], call_ref.at[me], send_sems.at[0, k], recv_sems.at[0, k], peer))
            first.append(_remote(cwg_ref.at[me], cwg_ref.at[me], send_sems.at[1, k], recv_sems.at[1, k], peer))
        for cp in first:
            cp.start()
        for k in range(1, NDEV):
            peer, slot = _peer(x, y, c, k)
            _remote(call_ref.at[slot], call_ref.at[slot], send_sems.at[0, k], recv_sems.at[0, k], peer).wait_recv()
            _remote(cwg_ref.at[slot], cwg_ref.at[slot], send_sems.at[1, k], recv_sems.at[1, k], peer).wait_recv()
        mod = jnp.broadcast_to(b_ref[...], (NDEV, W))
        for r in range(nblk):
            mod = mod + lax.dot_general(call_ref[:, r, :], w_ref[r * 128:(r + 1) * 128, :], (((1,), (0,)), ((), ())),
                                        preferred_element_type=f32, precision=lax.Precision.HIGHEST)
        row = lax.broadcasted_iota(jnp.int32, (NDEV, 1), 0)
        pick = lambda j: jnp.broadcast_to(jnp.sum(jnp.where(row == j, mod, 0.0), axis=0, keepdims=True), (8, W))
        mod_ref[me] = pick(me)
        second = []
        for k in range(1, NDEV):
            peer, slot = _peer(x, y, c, k)
            msend[k] = pick(slot)
            second.append(_remote(msend.at[k], mod_ref.at[me], send_sems.at[2, k], recv_sems.at[2, k], peer))
        for cp in second:
            cp.start()
        for k in range(1, NDEV):
            peer, slot = _peer(x, y, c, k)
            _remote(msend.at[k], mod_ref.at[slot], send_sems.at[2, k], recv_sems.at[2, k], peer).wait_recv()
        for cp in first + second:
            cp.wait_send()

    vmem = pl.BlockSpec(memory_space=pltpu.VMEM)
    return pl.pallas_call(
        body, name="ada_exchange",
        out_shape=(jax.ShapeDtypeStruct((NDEV, 8, W), f32), jax.ShapeDtypeStruct((NDEV, nblk, 128), f32),
                   jax.ShapeDtypeStruct((NDEV, 8, CW), f32)),
        in_specs=[vmem] * 4, out_specs=(vmem, vmem, vmem),
        scratch_shapes=[pltpu.VMEM((NDEV, 8, W), f32), pltpu.SemaphoreType.DMA((3, NDEV)),
                        pltpu.SemaphoreType.DMA((3, NDEV))],
        compiler_params=_params(),
    )(c_blk, cw_slab, w_ada, b_cols)


def _prenorm(x, scale, shift, g_pre, dep, tr=512):
    S, D = x.shape
    tr = min(tr, S)

    def body(x_ref, sc_ref, sh_ref, g_ref, dep_ref, h_ref):
        xv = x_ref[...]
        r = lax.rsqrt(jnp.mean(xv * xv, axis=-1, keepdims=True) + EPS)
        h_ref[...] = ((xv * r) * g_ref[...] * (1.0 + sc_ref[...]) + sh_ref[...]).astype(bf16)

    row = pl.BlockSpec((tr, D), lambda i: (i, 0))
    vec = pl.BlockSpec((1, D), lambda i: (0, 0))
    return pl.pallas_call(body, name="prenorm", out_shape=jax.ShapeDtypeStruct((S, D), bf16), grid=(S // tr,),
                          in_specs=[row, vec, vec, vec, ANY_SPEC], out_specs=row, compiler_params=_params("parallel"))(
                              x, scale, shift, g_pre, dep)


def _ext_rows(i, tr, S):
    g = lax.broadcasted_iota(jnp.int32, (tr + 16, 1), 0) + (i * tr - 8)
    return (g >= 0) & (g < S)


def _halo_specs(tr, S, C, col):
    nb8 = S // 8
    main = pl.BlockSpec((tr, C), lambda i: (i, col))
    prev = pl.BlockSpec((8, C), lambda i: (jnp.maximum(i * (tr // 8) - 1, 0), col))
    nxt = pl.BlockSpec((8, C), lambda i: (jnp.minimum((i + 1) * (tr // 8), nb8 - 1), col))
    return prev, main, nxt


def _conv_fwd(proj, conv_w, conv_b, g_conv, tr=512):
    S, C = proj.shape[0], proj.shape[1] // 8
    tr = min(tr, S)

    def body(up, um, un, cp, cm, cn, bg_ref, zc_ref, w_ref, cb_ref, g_ref, o_ref):
        i = pl.program_id(0)
        exists = _ext_rows(i, tr, S)
        u = jnp.concatenate([up[...], um[...], un[...]], axis=0)
        cg = jnp.concatenate([cp[...], cm[...], cn[...]], axis=0)
        t = jnp.where(exists, cg * u, 0.0)
        t_before = pltpu.roll(t, 1, 0)[8:tr + 8]
        t_after = pltpu.roll(t, tr + 15, 0)[8:tr + 8]
        w = w_ref[...]
        cv = w[0:1] * t_before + w[1:2] * t[8:tr + 8] + w[2:3] * t_after + cb_ref[...]
        yc = bg_ref[...] * cv
        rc = lax.rsqrt(jnp.mean(yc * yc, axis=-1, keepdims=True) + EPS)
        o_ref[...] = ((yc * rc) * g_ref[...] * _silu(zc_ref[...])).astype(bf16)

    u_specs = _halo_specs(tr, S, C, 0)
    c_specs = _halo_specs(tr, S, C, 2)
    vec = pl.BlockSpec((1, C), lambda i: (0, 0))
    return pl.pallas_call(
        body, name="conv_fwd", out_shape=jax.ShapeDtypeStruct((S, 2 * C), bf16), grid=(S // tr,),
        in_specs=[*u_specs, *c_specs, pl.BlockSpec((tr, C), lambda i: (i, 1)), pl.BlockSpec((tr, C), lambda i: (i, 3)),
                  pl.BlockSpec((8, C), lambda i: (0, 0)), vec, vec],
        out_specs=pl.BlockSpec((tr, C), lambda i: (i, 0)), compiler_params=_params("parallel"),
    )(proj, proj, proj, proj, proj, proj, proj, proj, conv_w, conv_b, g_conv)


def _branch_geometry(S, r, inter):
    L = S // r * inter
    nq = min(128, L)
    nk = min(nq + 2 * HALF_WIN * inter, L)
    assert L % nq == 0 and (L == nk or L >= nq + 2 * HALF_WIN * inter)
    return L, nq, nk, L // nq


QUAD = 4


def _to_quad(dst, src, S):
    n = S // QUAD
    for rho in range(QUAD):
        dst[pl.ds(rho * n, n), :] = src[pl.ds(rho, n, stride=QUAD), :]


def _block_rows(idx, r, inter, S, L, nq, nk, nblk):
    rho, qb = (0, idx) if r == 1 else (idx // nblk, idx % nblk)
    i0 = qb * nq
    ws = jnp.clip(i0 - HALF_WIN * inter, 0, L - nk)
    if r == 1:
        return pl.ds(pl.multiple_of(i0, 8), nq), pl.ds(pl.multiple_of(ws, 8), nk), i0 - ws
    assert r % (QUAD * inter) == 0
    step = r // QUAD // inter
    base = (rho % QUAD) * (S // QUAD) + rho // QUAD
    if step == 1:
        return pl.ds(pl.multiple_of(base + i0, 8), nq), pl.ds(pl.multiple_of(base + ws, 8), nk), i0 - ws
    return pl.ds(base + step * i0, nq, stride=step), pl.ds(base + step * ws, nk, stride=step), i0 - ws


N_CASES = 3
SCALE = HEAD_DIM ** -0.5
ATTN_UNROLL = 16


def _bias_shape(S):
    shapes = [_branch_geometry(S, r, inter)[1:3] for _, r, inter in BRANCHES]
    return (len(BRANCHES) * N_CASES * 2, max(nq for nq, _ in shapes), max(nk for _, nk in shapes))


def _bias_index(b, case, head):
    return (b * N_CASES + case) * 2 + head


def _fill_bias(bias_scr, sl_ref, S):
    sl = sl_ref[...]
    slope = (sl[0:1, 0:1], sl[0:1, HEAD_DIM:HEAD_DIM + 1])
    for b, (_, r, inter) in enumerate(BRANCHES):
        L, nq, nk, nblk = _branch_geometry(S, r, inter)
        rel = lax.broadcasted_iota(jnp.int32, (nq, nk), 0) - lax.broadcasted_iota(jnp.int32, (nq, nk), 1)
        for case in range(N_CASES):
            d = jnp.abs(rel + case * HALF_WIN)
            valid = d <= HALF_WIN * inter
            if inter > 1:
                valid = valid & (jnp.bitwise_and(d, inter - 1) == 0)
            dist = d.astype(f32) * float(r // inter)
            for head in range(2):
                bias_scr[_bias_index(b, case, head), 0:nq, 0:nk] = jnp.where(valid, -slope[head] * dist, NEG_INF)


def _bias_tiles(slopes, S, dep):
    npair = slopes.shape[0]
    shape = _bias_shape(S)

    def body(sl_ref, dep_ref, o_ref):
        _fill_bias(o_ref, sl_ref, S)

    return pl.pallas_call(
        body, name="bias_tiles", out_shape=jax.ShapeDtypeStruct((npair,) + shape, f32), grid=(npair,),
        in_specs=[pl.BlockSpec((None, 8, PAIR), lambda p: (p, 0, 0)), ANY_SPEC],
        out_specs=pl.BlockSpec((None,) + shape, lambda p: (p, 0, 0, 0)), compiler_params=_params("parallel"),
    )(slopes, dep)


def _head_slopes(n_heads):
    slopes = 2.0 ** (-8.0 * jnp.arange(1, n_heads + 1, dtype=f32) / n_heads)
    return jnp.broadcast_to(jnp.repeat(slopes.reshape(n_heads // 2, 2), HEAD_DIM, axis=1)[:, None, :],
                            (n_heads // 2, 8, PAIR))


def _attn_fwd(proj, bias):
    S, C = proj.shape[0], proj.shape[1] // 8
    npair = C // PAIR

    def body(q_ref, k_ref, v_ref, bias_scr, o_ref, lse_ref, m_scr, l_scr, a_scr, q4_scr, k4_scr, v4_scr):
        lane = lax.broadcasted_iota(jnp.int32, (1, PAIR), 1)
        first = lane < HEAD_DIM
        for dst, src in ((q4_scr, q_ref), (k4_scr, k_ref), (v4_scr, v_ref)):
            _to_quad(dst, src, S)

        for b, (_, r, inter) in enumerate(BRANCHES):
            L, nq, nk, nblk = _branch_geometry(S, r, inter)
            qs, ks, vs = (q_ref, k_ref, v_ref) if r == 1 else (q4_scr, k4_scr, v4_scr)

            def step(idx, carry, b=b, r=r, L=L, nq=nq, nk=nk, nblk=nblk, qs=qs, ks=ks, vs=vs):
                qrows, krows, off = _block_rows(idx, r, inter, S, L, nq, nk, nblk)
                case = off // HALF_WIN
                q2 = qs[qrows, :] * SCALE
                k2 = ks[krows, :].astype(bf16)
                v2 = vs[krows, :].astype(bf16)
                ms, accs = [], []
                for hh in range(2):
                    mine = first if hh == 0 else ~first
                    qh = jnp.where(mine, q2, 0.0).astype(bf16)
                    s = lax.dot_general(qh, k2, (((1,), (1,)), ((), ())), preferred_element_type=f32)
                    s = s + bias_scr[_bias_index(b, case, hh), 0:nq, 0:nk]
                    m = jnp.max(s, axis=-1, keepdims=True)
                    p = jnp.exp(s - m).astype(bf16)
                    vh = jnp.where(mine, v2, jnp.ones_like(v2))
                    ms.append(m)
                    accs.append(jnp.dot(p, vh, preferred_element_type=f32))
                m_scr[b, qrows, :] = jnp.where(first, ms[0], ms[1])
                a_scr[b, qrows, :] = jnp.where(first, accs[0], accs[1])
                l_scr[b, qrows, :] = jnp.where(first, accs[1], accs[0])
                return carry

            lax.fori_loop(0, S // nq, step, 0, unroll=min(ATTN_UNROLL, S // nq))

        n4 = S // QUAD
        ch = min(256, n4)
        nch = n4 // ch

        def merge(i, carry):
            rho, part = i // nch, i % nch
            sorted_rows = pl.ds(pl.multiple_of(rho * n4 + part * ch, 8), ch)
            token_rows = pl.ds(rho + QUAD * part * ch, ch, stride=QUAD)
            rows = (token_rows,) + (sorted_rows,) * (len(BRANCHES) - 1)
            ms = [m_scr[b, rows[b], :] for b in range(len(BRANCHES))]
            m = functools.reduce(jnp.maximum, ms)
            l = jnp.zeros((ch, PAIR), f32)
            acc = jnp.zeros((ch, PAIR), f32)
            for b in range(len(BRANCHES)):
                w = jnp.exp(ms[b] - m)
                l = l + w * pltpu.roll(l_scr[b, rows[b], :], HEAD_DIM, 1)
                acc = acc + w * a_scr[b, rows[b], :]
            o_ref[token_rows, :] = acc / l
            lse_ref[token_rows, :] = m + jnp.log(l)
            return carry

        lax.fori_loop(0, QUAD * nch, merge, 0, unroll=2)

    blk = lambda part: pl.BlockSpec((S, PAIR), lambda p: (0, part * npair + p))
    out = pl.BlockSpec((S, PAIR), lambda p: (0, p))
    return pl.pallas_call(
        body, name="attn_fwd",
        out_shape=(jax.ShapeDtypeStruct((S, C), f32), jax.ShapeDtypeStruct((S, C), f32)), grid=(npair,),
        in_specs=[blk(4), blk(5), blk(6), pl.BlockSpec((None,) + _bias_shape(S), lambda p: (p, 0, 0, 0))],
        out_specs=(out, out),
        scratch_shapes=[pltpu.VMEM((3, S, PAIR), f32)] * 3 + [pltpu.VMEM((S, PAIR), f32)] * 3,
        compiler_params=_params("parallel"),
    )(proj, proj, proj, bias)


def _attn_post(ycat, o, proj, g_attn, tr=512):
    S, C = o.shape
    tr = min(tr, S)

    def body(y_ref, o_ref, z_ref, g_ref, out_ref):
        del y_ref
        ov = o_ref[...]
        ra = lax.rsqrt(jnp.mean(ov * ov, axis=-1, keepdims=True) + EPS)
        out_ref[...] = ((ov * ra) * g_ref[...] * _silu(z_ref[...])).astype(bf16)

    return pl.pallas_call(
        body, name="attn_post", out_shape=jax.ShapeDtypeStruct(ycat.shape, ycat.dtype), grid=(S // tr,),
        in_specs=[HBM_SPEC, pl.BlockSpec((tr, C), lambda i: (i, 0)), pl.BlockSpec((tr, C), lambda i: (i, 7)),
                  pl.BlockSpec((1, C), lambda i: (0, 0))],
        out_specs=pl.BlockSpec((tr, C), lambda i: (i, 1)), input_output_aliases={0: 0},
        compiler_params=_params("arbitrary"),
    )(ycat, o, proj, g_attn)


def _residual_minus_target(x, target, dep, tr=512):
    S, D = x.shape
    tr = min(tr, S)

    def body(x_ref, t_ref, dep_ref, o_ref):
        o_ref[...] = x_ref[...] - t_ref[...]

    row = pl.BlockSpec((tr, D), lambda i: (i, 0))
    return pl.pallas_call(body, name="residual_minus_target", out_shape=jax.ShapeDtypeStruct((S, D), f32),
                          grid=(S // tr,), in_specs=[row, row, ANY_SPEC], out_specs=row,
                          compiler_params=_params("parallel"))(x, target, dep)


def _sandwich(y, x_minus_t, gate, g_post, tr=256):
    S, D = y.shape
    tr = min(tr, S)

    def body(y_ref, xt_ref, gate_ref, g_ref, dy_ref, dout_ref, sums_ref):
        i = pl.program_id(0)
        gate, g = gate_ref[...], g_ref[...]
        gg = gate * g
        yv = y_ref[...]
        rp = lax.rsqrt(jnp.mean(yv * yv, axis=-1, keepdims=True) + EPS)
        yhat = yv * rp
        err = xt_ref[...] + gg * yhat
        dout = err * (1.0 / D)
        dout_ref[...] = dout
        q = dout * yhat
        w = dout * gg
        dy_ref[...] = (rp * (w - yhat * jnp.sum(q * gg, axis=-1, keepdims=True) * (1.0 / D))).astype(bf16)
        loss = 0.5 * jnp.sum(jnp.mean(err * err, axis=-1, keepdims=True), axis=0, keepdims=True)
        q_sum = jnp.sum(q, axis=0, keepdims=True)
        row = lax.broadcasted_iota(jnp.int32, (8, D), 0)
        upd = jnp.where(row == 0, q_sum * g, jnp.where(row == 1, q_sum * gate, jnp.where(row == 2, loss, 0.0)))

        @pl.when(i == 0)
        def _():
            sums_ref[...] = upd

        @pl.when(i > 0)
        def _():
            sums_ref[...] += upd

    row = pl.BlockSpec((tr, D), lambda i: (i, 0))
    vec = pl.BlockSpec((1, D), lambda i: (0, 0))
    return pl.pallas_call(
        body, name="sandwich",
        out_shape=(jax.ShapeDtypeStruct((S, D), bf16), jax.ShapeDtypeStruct((S, D), f32), jax.ShapeDtypeStruct((8, D), f32)),
        grid=(S // tr,), in_specs=[row, row, vec, vec],
        out_specs=(row, row, pl.BlockSpec((8, D), lambda i: (0, 0))), compiler_params=_params("arbitrary"),
    )(y, x_minus_t, gate, g_post)


def _conv_bwd(proj, dycat, conv_w, conv_b, g_conv, dep, tr=256):
    S, C = proj.shape[0], proj.shape[1] // 8
    tr = min(tr, S)
    n = tr + 16

    def body(*refs):
        ins, (w_ref, cb_ref, g_ref, _, dp_ref, sums_ref) = refs[:15], refs[15:]
        i = pl.program_id(0)
        exists = _ext_rows(i, tr, S)
        u, bg, cg, zc, dyn = (jnp.concatenate([ins[3 * t][...], ins[3 * t + 1][...], ins[3 * t + 2][...]], axis=0)
                              for t in range(5))
        w = w_ref[...]
        t = jnp.where(exists, cg * u, 0.0)
        t_before, t_after = pltpu.roll(t, 1, 0), pltpu.roll(t, n - 1, 0)
        cv = w[0:1] * t_before + w[1:2] * t + w[2:3] * t_after + cb_ref[...]
        yc = bg * cv
        rc = lax.rsqrt(jnp.mean(yc * yc, axis=-1, keepdims=True) + EPS)
        yhat = yc * rc
        sz, dsz = _silu_and_slope(zc)
        wgt = dyn * g_ref[...] * sz
        dyc = rc * (wgt - yhat * jnp.mean(wgt * yhat, axis=-1, keepdims=True))
        dcv = jnp.where(exists, dyc * bg, 0.0)
        dt = w[0:1] * pltpu.roll(dcv, n - 1, 0) + w[1:2] * dcv + w[2:3] * pltpu.roll(dcv, 1, 0)
        mid = slice(8, tr + 8)
        dp_ref[:, 0:C] = (dt * cg)[mid].astype(bf16)
        dp_ref[:, C:2 * C] = (dyc * cv)[mid].astype(bf16)
        dp_ref[:, 2 * C:3 * C] = (dt * u)[mid].astype(bf16)
        dp_ref[:, 3 * C:4 * C] = (dyn * yhat * g_ref[...] * dsz)[mid].astype(bf16)
        colsum = lambda v: jnp.sum(v[mid], axis=0, keepdims=True)
        parts = [colsum(dyn * yhat * sz), colsum(dcv), colsum(dcv * t_before), colsum(dcv * t), colsum(dcv * t_after)]
        row = lax.broadcasted_iota(jnp.int32, (8, C), 0)
        upd = jnp.zeros((8, C), f32)
        for j, pj in enumerate(parts):
            upd = jnp.where(row == j, pj, upd)

        @pl.when(i == 0)
        def _():
            sums_ref[...] = upd

        @pl.when(i > 0)
        def _():
            sums_ref[...] += upd

    specs = []
    for col in range(4):
        specs += _halo_specs(tr, S, C, col)
    specs += _halo_specs(tr, S, C, 0)
    vec = pl.BlockSpec((1, C), lambda i: (0, 0))
    return pl.pallas_call(
        body, name="conv_bwd",
        out_shape=(jax.ShapeDtypeStruct((S, 4 * C), bf16), jax.ShapeDtypeStruct((8, C), f32)), grid=(S // tr,),
        in_specs=[*specs, pl.BlockSpec((8, C), lambda i: (0, 0)), vec, vec, ANY_SPEC],
        out_specs=(pl.BlockSpec((tr, 4 * C), lambda i: (i, 0)), pl.BlockSpec((8, C), lambda i: (0, 0))),
        compiler_params=_params("arbitrary"),
    )(*([proj] * 12), dycat, dycat, dycat, conv_w, conv_b, g_conv, dep)


def _attn_post_bwd(o, proj, dycat, g_attn, dep, tr=512):
    S, C = o.shape
    tr = min(tr, S)

    def body(o_ref, z_ref, dy_ref, g_ref, dep_ref, do_ref, dz_ref, sums_ref):
        i = pl.program_id(0)
        ov, zv, dyn = o_ref[...], z_ref[...], dy_ref[...]
        ra = lax.rsqrt(jnp.mean(ov * ov, axis=-1, keepdims=True) + EPS)
        ohat = ov * ra
        sz, dsz = _silu_and_slope(zv)
        wgt = dyn * g_ref[...] * sz
        do_ref[...] = ra * (wgt - ohat * jnp.mean(wgt * ohat, axis=-1, keepdims=True))
        dz_ref[...] = (dyn * ohat * g_ref[...] * dsz).astype(bf16)
        row = lax.broadcasted_iota(jnp.int32, (8, C), 0)
        upd = jnp.where(row == 0, jnp.sum(dyn * ohat * sz, axis=0, keepdims=True), 0.0)

        @pl.when(i == 0)
        def _():
            sums_ref[...] = upd

        @pl.when(i > 0)
        def _():
            sums_ref[...] += upd

    return pl.pallas_call(
        body, name="attn_post_bwd",
        out_shape=(jax.ShapeDtypeStruct((S, C), f32), jax.ShapeDtypeStruct((4, S, C), bf16),
                   jax.ShapeDtypeStruct((8, C), f32)),
        grid=(S // tr,),
        in_specs=[pl.BlockSpec((tr, C), lambda i: (i, 0)), pl.BlockSpec((tr, C), lambda i: (i, 7)),
                  pl.BlockSpec((tr, C), lambda i: (i, 1)), pl.BlockSpec((1, C), lambda i: (0, 0)), ANY_SPEC],
        out_specs=(pl.BlockSpec((tr, C), lambda i: (i, 0)), pl.BlockSpec((None, tr, C), lambda i: (3, i, 0)),
                   pl.BlockSpec((8, C), lambda i: (0, 0))),
        compiler_params=_params("arbitrary"),
    )(o, proj, dycat, g_attn, dep)


def _attn_bwd(proj, o, do, lse, bias, dqkvz, dep):
    S, C = o.shape
    npair = C // PAIR

    def body(q_ref, k_ref, v_ref, o_ref, do_ref, lse_ref, bias_scr, old_ref, dep_ref, dqkv_ref,
             acc_scr, dl_scr, quad_scr):
        lane = lax.broadcasted_iota(jnp.int32, (1, PAIR), 1)
        first = lane < HEAD_DIM
        ch = min(256, S)

        def prep(i, carry):
            rows = pl.ds(pl.multiple_of(i * ch, 8), ch)
            prod = do_ref[rows, :] * o_ref[rows, :]
            d0 = jnp.sum(jnp.where(first, prod, 0.0), axis=-1, keepdims=True)
            d1 = jnp.sum(jnp.where(first, 0.0, prod), axis=-1, keepdims=True)
            dl_scr[rows, :] = jnp.where(first, d0, d1)
            zero = jnp.zeros((ch, PAIR), f32)
            for order in range(2):
                for t in range(3):
                    acc_scr[order, t, rows, :] = zero
            return carry

        lax.fori_loop(0, S // ch, prep, 0, unroll=2)
        token_srcs = (q_ref, k_ref, v_ref, do_ref, lse_ref, dl_scr)
        for j, src in enumerate(token_srcs):
            _to_quad(quad_scr.at[j], src, S)

        for b, (_, r, inter) in enumerate(BRANCHES):
            L, nq, nk, nblk = _branch_geometry(S, r, inter)
            order = 0 if r == 1 else 1
            srcs = token_srcs if r == 1 else tuple(quad_scr.at[j] for j in range(6))

            def step(idx, carry, b=b, r=r, L=L, nq=nq, nk=nk, nblk=nblk, order=order, srcs=srcs):
                qs, ks, vs, dos, lses, dls = srcs
                dq_scr, dk_scr, dv_scr = (acc_scr.at[order, t] for t in range(3))
                qrows, krows, off = _block_rows(idx, r, inter, S, L, nq, nk, nblk)
                case = off // HALF_WIN
                q2 = qs[qrows, :] * SCALE
                k2 = ks[krows, :].astype(bf16)
                v2 = vs[krows, :].astype(bf16)
                do2 = dos[qrows, :]
                lse2 = lses[qrows, :]
                dl2 = dls[qrows, :]
                dq2 = jnp.zeros((nq, PAIR), f32)
                dk2 = jnp.zeros((nk, PAIR), f32)
                dv2 = jnp.zeros((nk, PAIR), f32)
                for hh in range(2):
                    mine = first if hh == 0 else ~first
                    lo = hh * HEAD_DIM
                    qh = jnp.where(mine, q2, 0.0).astype(bf16)
                    doh = jnp.where(mine, do2, 0.0).astype(bf16)
                    s = lax.dot_general(qh, k2, (((1,), (1,)), ((), ())), preferred_element_type=f32)
                    s = s + bias_scr[_bias_index(b, case, hh), 0:nq, 0:nk]
                    p = jnp.exp(s - lse2[:, lo:lo + 1])
                    dv2 = dv2 + lax.dot_general(p.astype(bf16), doh, (((0,), (0,)), ((), ())), preferred_element_type=f32)
                    dp = lax.dot_general(doh, v2, (((1,), (1,)), ((), ())), preferred_element_type=f32)
                    ds = (p * (dp - dl2[:, lo:lo + 1])).astype(bf16)
                    dq2 = dq2 + jnp.where(mine, jnp.dot(ds, k2, preferred_element_type=f32), 0.0)
                    dk2 = dk2 + lax.dot_general(ds, qh, (((0,), (0,)), ((), ())), preferred_element_type=f32)
                dq_scr[qrows, :] = dq_scr[qrows, :] + dq2
                dk_scr[krows, :] = dk_scr[krows, :] + dk2
                dv_scr[krows, :] = dv_scr[krows, :] + dv2
                return carry

            lax.fori_loop(0, S // nq, step, 0, unroll=min(ATTN_UNROLL, S // nq))

        n4 = S // QUAD
        for t in range(3):
            for rho in range(QUAD):
                token_rows = pl.ds(rho, n4, stride=QUAD)
                acc_scr[0, t, token_rows, :] = acc_scr[0, t, token_rows, :] + acc_scr[1, t, pl.ds(rho * n4, n4), :]
        dqkv_ref[0] = (acc_scr[0, 0] * SCALE).astype(bf16)
        dqkv_ref[1] = acc_scr[0, 1].astype(bf16)
        dqkv_ref[2] = acc_scr[0, 2].astype(bf16)

    blk = lambda part: pl.BlockSpec((S, PAIR), lambda p: (0, part * npair + p))
    own = pl.BlockSpec((S, PAIR), lambda p: (0, p))
    return pl.pallas_call(
        body, name="attn_bwd", out_shape=jax.ShapeDtypeStruct(dqkvz.shape, dqkvz.dtype), grid=(npair,),
        in_specs=[blk(4), blk(5), blk(6), own, own, own,
                  pl.BlockSpec((None,) + _bias_shape(S), lambda p: (p, 0, 0, 0)), ANY_SPEC, ANY_SPEC],
        out_specs=pl.BlockSpec((3, S, PAIR), lambda p: (0, 0, p)), input_output_aliases={7: 0},
        scratch_shapes=[pltpu.VMEM((2, 3, S, PAIR), f32), pltpu.VMEM((S, PAIR), f32), pltpu.VMEM((6, S, PAIR), f32)],
        compiler_params=_params("arbitrary"),
    )(proj, proj, proj, o, do, lse, bias, dqkvz, dep)


def _normalized(x, dep, tr=512):
    S, D = x.shape
    tr = min(tr, S)

    def body(x_ref, dep_ref, xn_ref, r_ref):
        xv = x_ref[...]
        r = lax.rsqrt(jnp.mean(xv * xv, axis=-1, keepdims=True) + EPS)
        xn_ref[...] = xv * r
        r_ref[...] = jnp.broadcast_to(r, (tr, 128))

    row = pl.BlockSpec((tr, D), lambda i: (i, 0))
    return pl.pallas_call(
        body, name="normalized", out_shape=(jax.ShapeDtypeStruct((S, D), f32), jax.ShapeDtypeStruct((S, 128), f32)),
        grid=(S // tr,), in_specs=[row, ANY_SPEC], out_specs=(row, pl.BlockSpec((tr, 128), lambda i: (i, 0))),
        compiler_params=_params("parallel"))(x, dep)


def _prenorm_bwd(dh, xn, r, dout, scale, g_pre, tr=256):
    S, D = xn.shape
    tr = min(tr, S)

    def body(dh_ref, x_ref, r_ref, dout_ref, sc_ref, g_ref, gx_ref, sums_ref):
        i = pl.program_id(0)
        xn, dhv = x_ref[...], dh_ref[...]
        r = r_ref[:, 0:1]
        dxn = dhv * (g_ref[...] * (1.0 + sc_ref[...]))
        gx_ref[...] = dout_ref[...] + r * (dxn - xn * jnp.mean(dxn * xn, axis=-1, keepdims=True))
        dhx = dhv * xn
        row = lax.broadcasted_iota(jnp.int32, (8, D), 0)
        upd = jnp.where(row == 0, jnp.sum(dhv, axis=0, keepdims=True),
                        jnp.where(row == 1, jnp.sum(dhx, axis=0, keepdims=True) * g_ref[...],
                                  jnp.where(row == 2, jnp.sum(dhx, axis=0, keepdims=True) * (1.0 + sc_ref[...]), 0.0)))

        @pl.when(i == 0)
        def _():
            sums_ref[...] = upd

        @pl.when(i > 0)
        def _():
            sums_ref[...] += upd

    row = pl.BlockSpec((tr, D), lambda i: (i, 0))
    vec = pl.BlockSpec((1, D), lambda i: (0, 0))
    return pl.pallas_call(
        body, name="prenorm_bwd",
        out_shape=(jax.ShapeDtypeStruct((S, D), f32), jax.ShapeDtypeStruct((8, D), f32)), grid=(S // tr,),
        in_specs=[row, row, pl.BlockSpec((tr, 128), lambda i: (i, 0)), row, vec, vec],
        out_specs=(row, pl.BlockSpec((8, D), lambda i: (0, 0))), compiler_params=_params("arbitrary"),
    )(dh, xn, r, dout, scale, g_pre)


def _adamw(w, g, m, v):
    m = ADAM_B1 * m + (1.0 - ADAM_B1) * g
    v = ADAM_B2 * v + (1.0 - ADAM_B2) * (g * g)
    m_hat = m / (1.0 - ADAM_B1 ** ADAM_STEP)
    v_hat = v / (1.0 - ADAM_B2 ** ADAM_STEP)
    delta = -ADAM_LR * (m_hat / (jnp.sqrt(v_hat) + ADAM_EPS) + ADAM_WD * w)
    return delta, m, v


def _sum_rows(parts, dep):
    P = parts.shape[1]

    def body(p_ref, dep_ref, o_ref):
        acc = p_ref[0:1, :]
        for j in range(1, NDEV):
            acc = acc + p_ref[j:j + 1, :]
        o_ref[...] = jnp.broadcast_to(acc, (8, P))

    vmem = pl.BlockSpec(memory_space=pltpu.VMEM)
    return pl.pallas_call(body, name="sum_small", out_shape=jax.ShapeDtypeStruct((8, P), f32),
                          in_specs=[vmem, ANY_SPEC], out_specs=vmem, compiler_params=_params())(parts, dep)


def _adamw_small(tot, params):
    given = [p[3] for p in params if not isinstance(p[3], int)]

    def body(tot_ref, *refs):
        given_refs = list(refs[:len(given)])
        ins = refs[len(given):len(given) + 3 * len(params)]
        outs = refs[len(given) + 3 * len(params):]
        for t, (w, _, _, where) in enumerate(params):
            w_ref, m_ref, v_ref = ins[3 * t:3 * t + 3]
            g = tot_ref[0:1, where:where + w.size] if isinstance(where, int) else given_refs.pop(0)[...]
            outs[4 * t][...] = g
            outs[4 * t + 1][...], outs[4 * t + 2][...], outs[4 * t + 3][...] = _adamw(w_ref[...], g, m_ref[...], v_ref[...])

    out_shape = tuple(jax.ShapeDtypeStruct(p[0].shape, f32) for p in params for _ in range(4))
    res = pl.pallas_call(body, name="adamw_small", out_shape=out_shape, compiler_params=_params())(
        tot, *given, *[a for p in params for a in p[:3]])
    return [res[4 * t:4 * t + 4] for t in range(len(params))]


def _adamw_sharded(name, parts, sums_a, sums_b, pick, w, m, v, rows=None, prev=None, tr=128):
    R, Cc = w.shape
    r0, nr = rows or (0, R)
    tr = math.gcd(tr, r0, nr)
    n, b0 = parts.shape[0], r0 // tr

    def body(pick_ref, p_ref, a_ref, b_ref, w_ref, m_ref, v_ref, *rest):
        g_ref, d_ref, nm_ref, nv_ref = rest[-4:]
        g = jnp.where(pick_ref[0] == 1, b_ref[...], a_ref[...]).astype(f32)
        for j in range(n):
            g = g + p_ref[j].astype(f32)
        g_ref[...] = g
        d_ref[...], nm_ref[...], nv_ref[...] = _adamw(w_ref[...], g, m_ref[...], v_ref[...])

    row = pl.BlockSpec((tr, Cc), lambda i, pick: (i + b0, 0))
    mine = pl.BlockSpec((None, tr, Cc), lambda i, pick: (pick[1], i + b0, 0))
    out = jax.ShapeDtypeStruct((R, Cc), f32)
    prev = list(prev or [])
    grid_spec = pltpu.PrefetchScalarGridSpec(
        num_scalar_prefetch=1, grid=(nr // tr,),
        in_specs=[pl.BlockSpec((n, tr, Cc), lambda i, pick: (0, i + b0, 0)), mine, mine, row, row, row]
        + [ANY_SPEC] * len(prev),
        out_specs=(row, row, row, row))
    return pl.pallas_call(
        body, name=name, out_shape=(out, out, out, out), grid_spec=grid_spec,
        input_output_aliases={7 + t: t for t in range(len(prev))}, compiler_params=_params("arbitrary"),
    )(pick, parts, sums_a, sums_b, w, m, v, *prev)


def _adamw_ada(c_t, dmod_cols, w, m, v, dep, tr=512):
    D, W = w.shape
    tr = min(tr, D)

    def body(c_ref, dm_ref, w_ref, m_ref, v_ref, dep_ref, g_ref, d_ref, nm_ref, nv_ref):
        cv, dm = c_ref[...], dm_ref[...]
        g = cv[:, 0:1] * dm[0:1, :]
        for b in range(1, NDEV):
            g = g + cv[:, b:b + 1] * dm[b:b + 1, :]
        g_ref[...] = g
        d_ref[...], nm_ref[...], nv_ref[...] = _adamw(w_ref[...], g, m_ref[...], v_ref[...])

    row = pl.BlockSpec((tr, W), lambda i: (i, 0))
    out = jax.ShapeDtypeStruct((D, W), f32)
    return pl.pallas_call(
        body, name="adamw_ada", out_shape=(out, out, out, out), grid=(D // tr,),
        in_specs=[pl.BlockSpec((tr, NDEV), lambda i: (i, 0)), pl.BlockSpec((NDEV, W), lambda i: (0, 0)), row, row, row,
                  ANY_SPEC],
        out_specs=(row, row, row, row), compiler_params=_params("parallel"),
    )(c_t, dmod_cols, w, m, v, dep)


def kernel(x, c, w_ada, b_ada, g_pre, w_in, conv_w, conv_b, g_conv, g_attn, w_out, g_post, loss_target, m_w_ada, m_b_ada, m_g_pre, m_w_in, m_conv_w, m_conv_b, m_g_conv, m_g_attn, m_w_out, m_g_post, v_w_ada, v_b_ada, v_g_pre, v_w_in, v_conv_w, v_conv_b, v_g_conv, v_g_attn, v_w_out, v_g_post):
    S, D = x.shape[1], x.shape[2]
    C = D // 2
    W = w_ada.shape[2]
    CW = conv_w.shape[2]
    me = 4 * lax.axis_index("x") + 2 * lax.axis_index("y") + lax.axis_index("c")
    x2, tgt = x[0], loss_target[0]
    w_ada2, w_in2, w_out2 = w_ada[0], w_in[0], w_out[0]

    R = D // NDEV
    core = lax.axis_index("c").astype(jnp.int32).reshape(1)

    cw_slab = jnp.zeros((8, CW), f32).at[:3].set(conv_w[0])
    b_cols = lax.dynamic_slice_in_dim(b_ada, me * W, W, axis=1)
    mod_slabs, c_blocks, cw_g = _ada_exchange(c.reshape(D // 128, 128), cw_slab, w_ada2, b_cols)
    c_all = c_blocks.reshape(NDEV, D)
    conv_w_full = jnp.transpose(cw_g, (1, 0, 2)).reshape(8, C)
    mod = mod_slabs[:, 0, :].reshape(1, 3 * D)
    shift, scale, gate = mod[:, :D], mod[:, D:2 * D], mod[:, 2 * D:]

    land_i = lax.dynamic_update_slice(lax.empty((NDEV, D, C), bf16), w_in2.astype(bf16)[None], (me, 0, 0))
    land_o = lax.dynamic_update_slice(lax.empty((NDEV, R, D), bf16), w_out2.astype(bf16)[None], (me, 0, 0))
    wi_send, wi_recv, land_i, w_token = _w_in_start(land_i, [mod_slabs])

    me_arr = me.astype(jnp.int32).reshape(1)
    h = _prenorm(x2, scale, shift, g_pre, w_token)
    land_i = _w_in_sibling(land_i, wi_recv, after=[h])
    proj = _in_proj_part("in_proj_a", h, land_i, None, me_arr, 0, 1, 2)
    x_minus_t = _residual_minus_target(x2, tgt, proj)
    bias = _bias_tiles(_head_slopes(C // HEAD_DIM), S, x_minus_t)
    xn, r_x = _normalized(x2, bias)

    def landing(rows, cols):
        return lax.dynamic_update_slice(lax.empty((NCHIP, rows, cols), bf16), jnp.zeros((1, rows, cols), bf16),
                                        (me // 2, 0, 0))

    land_go, land_gi = landing(R, D), landing(D, C)
    fi_send, fi_recv, land_i = _w_in_relay(land_i, wi_recv, after=[proj, xn, land_go, land_gi])
    proj = _in_proj_part("in_proj_b", h, land_i, proj, me_arr, 2, 2, 2)
    land_i = _w_in_forwarded(land_i, fi_recv, after=[proj])
    proj = _in_proj_part("in_proj_c", h, land_i, proj, me_arr, 3, 2, 2)
    (di_send, di_recv, wo_send, wo_recv), land_i, land_o = _w_in_diag(land_i, land_o, fi_recv, after=[proj])
    proj = _in_proj_part("in_proj_d", h, land_i, proj, me_arr, 6, 1, 1)
    win_g = _w_in_finish(land_i, wi_send, fi_send, di_send, di_recv, after=[proj])
    proj = _in_proj_part("in_proj_e", h, win_g, proj, me_arr, 7, 1, 1)
    ycat = _conv_fwd(proj, conv_w_full, conv_b, g_conv)
    o, lse = _attn_fwd(proj, bias)
    (fo_send, fo_recv), (land_o,), _ = _weights_forward("w_out_forward", land_o, wo_recv, after=[o])
    ycat = _attn_post(ycat, o, proj, g_attn)
    wout_g = _weights_wait("w_out_wait", land_o, wo_send, wo_recv, fo_send, fo_recv, after=[ycat])
    wout_full = wout_g.reshape(D, D)
    y = _matmul(ycat, wout_full, name="out_proj", out_dtype=f32)
    dy, dout, post_sums = _sandwich(y, x_minus_t, gate, g_post)

    gw_out = _matmul(ycat, dy, name="out_proj_dw", out_dtype=bf16, ta=True).reshape(NDEV, R, D)
    po_send, po_recv, gw_out, pair_o, po_token = _pair_start("g_out_pair_start", gw_out)
    dycat = _matmul(dy, wout_full, name="out_proj_dx", out_dtype=f32, tb=True, dep=po_token)
    gw_out, pair_o = _pair_wait("g_out_pair_wait", gw_out, pair_o, po_send, po_recv, after=[dycat])
    sum_o = _pair_sum("g_out_pair_sum", gw_out, pair_o, core)
    co_send, co_recv, sum_o, land_go, co_token = _chip_start(
        "g_out_chip_start", sum_o, land_go, 0)
    dpc, conv_sums = _conv_bwd(proj, dycat, conv_w_full, conv_b, g_conv, co_token)
    gw_c = _matmul(h, dpc, name="in_proj_dw_conv", out_dtype=bf16, ta=True, out_slots=4)
    pc_send, pc_recv, gw_c, pair_c, pc_token = _pair_start("g_conv_pair_start", gw_c)
    do, dpa, attn_sums = _attn_post_bwd(o, proj, dycat, g_attn, pc_token)
    gw_c, pair_c = _pair_wait("g_conv_pair_wait", gw_c, pair_c, pc_send, pc_recv, after=[do])
    sum_c = _pair_sum("g_conv_pair_sum", gw_c, pair_c, core)
    cc_send, cc_recv, sum_c, land_gi, cc_token = _chip_start(
        "g_conv_chip_start", sum_c, land_gi, 0)
    dpa = _attn_bwd(proj, o, do, lse, bias, dpa, cc_token)
    gw_a = _matmul(h, dpa, name="in_proj_dw_attn", out_dtype=bf16, ta=True, b_slots=True, out_slots=4)
    pa_send, pa_recv, gw_a, pair_a, pa_token = _pair_start("g_attn_pair_start", gw_a)
    sum_o, land_go = _chip_wait("g_out_chip_wait", sum_o, land_go, co_send, co_recv, 0, after=[pa_token])
    pick_out = jnp.stack([jnp.int32(0), me // 2]).astype(jnp.int32)
    g_w_out, d_w_out, nm_w_out, nv_w_out = _adamw_sharded(
        "adamw_w_out", land_go, sum_o, sum_o, pick_out, w_out2, m_w_out[0], v_w_out[0])
    gw_a, pair_a = _pair_wait("g_attn_pair_wait", gw_a, pair_a, pa_send, pa_recv, after=[g_w_out])
    sum_a = _pair_sum("g_attn_pair_sum", gw_a, pair_a, core)
    part_a, part_b = (0, 3 * D // 4), (3 * D // 4, D // 4)
    ca_send, ca_recv, sum_a, land_gi, ca_token = _chip_start("g_attn_chip_start_a", sum_a, land_gi, 4, part_a)
    dh = _matmul_slabs_t(dpc, dpa, win_g, name="in_proj_dx", dep=ca_token)
    grad_x, pre_sums = _prenorm_bwd(dh, xn, r_x, dout, scale, g_pre)

    small = jnp.concatenate([pre_sums[0:1], pre_sums[1:2], post_sums[0:1],
                             pre_sums[2:3], post_sums[1:2],
                             conv_sums[2:3], conv_sums[3:4], conv_sums[4:5],
                             conv_sums[1:2], conv_sums[0:1], attn_sums[0:1]], axis=1)
    small = jnp.concatenate([small.reshape(8 * D // 128, 128), jnp.broadcast_to(post_sums[2:3, :128], (8, 128))])
    (small_all,) = _all_gather([small], "gather_small")
    cb_send, cb_recv, sum_a, land_gi, cb_token = _chip_start("g_attn_chip_start_b", sum_a, land_gi, 4, part_b,
                                                             after=[small_all])
    small_all = small_all.reshape(NDEV, small.size)
    tot = _sum_rows(small_all, cb_token)
    loss = tot[0, 8 * D]
    g_conv_w = lax.dynamic_slice_in_dim(tot[0:1, 5 * D:5 * D + 3 * C].reshape(1, 3, C), me * CW, CW, axis=2)
    ((g_b_ada, d_b_ada, nm_b_ada, nv_b_ada), (g_g_pre, d_g_pre, nm_g_pre, nv_g_pre),
     (g_g_post, d_g_post, nm_g_post, nv_g_post), (g_conv_w, d_conv_w, nm_conv_w, nv_conv_w),
     (g_conv_b, d_conv_b, nm_conv_b, nv_conv_b), (g_g_conv, d_g_conv, nm_g_conv, nv_g_conv),
     (g_g_attn, d_g_attn, nm_g_attn, nv_g_attn)) = _adamw_small(tot, [
         (b_ada, m_b_ada, v_b_ada, 0), (g_pre, m_g_pre, v_g_pre, 3 * D), (g_post, m_g_post, v_g_post, 4 * D),
         (conv_w, m_conv_w, v_conv_w, g_conv_w), (conv_b, m_conv_b, v_conv_b, 5 * D + 3 * C),
         (g_conv, m_g_conv, v_g_conv, 5 * D + 4 * C), (g_attn, m_g_attn, v_g_attn, 5 * D + 5 * C)])

    dmod_cols = lax.dynamic_slice_in_dim(small_all[:, :3 * D], me * W, W, axis=1)
    g_w_ada, d_w_ada, nm_w_ada, nv_w_ada = _adamw_ada(c_all.T, dmod_cols, w_ada2, m_w_ada[0], v_w_ada[0], cb_token)

    pick_in = jnp.stack([me // 4, (me % 4) // 2]).astype(jnp.int32)
    sum_c, land_gi = _chip_wait("g_conv_chip_wait", sum_c, land_gi, cc_send, cc_recv, 0, after=[g_w_ada])
    sum_a, land_gi = _chip_wait("g_attn_chip_wait_a", sum_a, land_gi, ca_send, ca_recv, 4, [g_w_ada], part_a)
    first = _adamw_sharded("adamw_w_in_a", land_gi, sum_c, sum_a, pick_in, w_in2, m_w_in[0], v_w_in[0], rows=part_a,
                           tr=256)
    sum_a, land_gi = _chip_wait("g_attn_chip_wait_b", sum_a, land_gi, cb_send, cb_recv, 4, [first[0]], part_b)
    g_w_in, d_w_in, nm_w_in, nv_w_in = _adamw_sharded(
        "adamw_w_in_b", land_gi, sum_c, sum_a, pick_in, w_in2, m_w_in[0], v_w_in[0], rows=part_b, prev=first, tr=256)

    return (loss, grad_x[None],
            g_w_ada[None], g_b_ada, g_g_pre, g_w_in[None], g_conv_w, g_conv_b, g_g_conv, g_g_attn, g_w_out[None], g_g_post,
            d_w_ada[None], d_b_ada, d_g_pre, d_w_in[None], d_conv_w, d_conv_b, d_g_conv, d_g_attn, d_w_out[None], d_g_post,
            nm_w_ada[None], nm_b_ada, nm_g_pre, nm_w_in[None], nm_conv_w, nm_conv_b, nm_g_conv, nm_g_attn, nm_w_out[None], nm_g_post,
            nv_w_ada[None], nv_b_ada, nv_g_pre, nv_w_in[None], nv_conv_w, nv_conv_b, nv_g_conv, nv_g_attn, nv_w_out[None], nv_g_post)
```

```python
import functools
import math

import jax
import jax.numpy as jnp
from jax import lax
from jax.experimental import pallas as pl
from jax.experimental.pallas import tpu as pltpu

f32 = jnp.float32
bf16 = jnp.bfloat16

NDEV = 8
HEAD_DIM = 64
PAIR = 2 * HEAD_DIM
BRANCHES = ((128, 1, 1), (512, 4, 1), (2048, 16, 2))
HALF_WIN = 64
EPS = 1e-6
NEG_INF = -1e30
ADAM_LR, ADAM_B1, ADAM_B2, ADAM_EPS, ADAM_WD, ADAM_STEP = 0.001, 0.9, 0.999, 1e-08, 0.01, 10
MESH = pl.DeviceIdType.MESH
VMEM_LIMIT = 56 * 1024 * 1024
HBM_SPEC = pl.BlockSpec(memory_space=pltpu.HBM)
ANY_SPEC = pl.BlockSpec(memory_space=pl.ANY)
SEM_SPEC = pl.BlockSpec(memory_space=pltpu.SEMAPHORE)


def _params(*sem):
    return pltpu.CompilerParams(dimension_semantics=sem or None, vmem_limit_bytes=VMEM_LIMIT)


def _silu(z):
    return z * jax.nn.sigmoid(z)


def _silu_and_slope(z):
    s = jax.nn.sigmoid(z)
    return z * s, s * (1.0 + z * (1.0 - s))


def _my_place():
    x, y, c = lax.axis_index("x"), lax.axis_index("y"), lax.axis_index("c")
    return x, y, c, 4 * x + 2 * y + c


def _peer(x, y, c, k):
    px, py, pc = x ^ (k >> 2 & 1), y ^ (k >> 1 & 1), c ^ (k & 1)
    return (px, py, pc), 4 * px + 2 * py + pc


def _all_gather(arrays, name):
    n = len(arrays)

    def body(*refs):
        srcs, dsts = refs[:n], refs[n:2 * n]
        send_sems, recv_sems, local_sems = refs[2 * n:]
        x, y, c, me = _my_place()
        locals_, sends = [], []
        for t in range(n):
            own = pltpu.make_async_copy(srcs[t], dsts[t].at[me], local_sems.at[t])
            own.start()
            locals_.append(own)
            for k in range(1, NDEV):
                peer, pidx = _peer(x, y, c, k)
                cp = pltpu.make_async_remote_copy(
                    src_ref=srcs[t], dst_ref=dsts[t].at[me], send_sem=send_sems.at[t, k],
                    recv_sem=recv_sems.at[t, k], device_id=peer, device_id_type=MESH)
                cp.start()
                sends.append(cp)
        for t in range(n):
            for k in range(1, NDEV):
                peer, pidx = _peer(x, y, c, k)
                pltpu.make_async_remote_copy(
                    src_ref=srcs[t], dst_ref=dsts[t].at[pidx], send_sem=send_sems.at[t, k],
                    recv_sem=recv_sems.at[t, k], device_id=peer, device_id_type=MESH).wait_recv()
        for cp in sends:
            cp.wait_send()
        for cp in locals_:
            cp.wait()

    return pl.pallas_call(
        body, name=name,
        out_shape=tuple(jax.ShapeDtypeStruct((NDEV,) + a.shape, a.dtype) for a in arrays),
        in_specs=[HBM_SPEC] * n, out_specs=tuple([HBM_SPEC] * n),
        scratch_shapes=[pltpu.SemaphoreType.DMA((n, NDEV)), pltpu.SemaphoreType.DMA((n, NDEV)),
                        pltpu.SemaphoreType.DMA((n,))],
    )(*arrays)


def _comm_call(name, arrays, sems, new_sems, body, after=(), token=False):
    na, ns, nn, nf = len(arrays), len(sems), len(new_sems), len(after)

    def kern(*refs):
        ins, outs = refs[:na + ns + nf], refs[na + ns + nf:]
        body(ins[:na], ins[na:na + ns], outs[:nn])
        if token:
            outs[nn + na][...] = jnp.zeros((8, 128), f32)

    out_shape = ([pltpu.SemaphoreType.DMA(s) for s in new_sems] + [pltpu.HBM(a.shape, a.dtype) for a in arrays]
                 + ([jax.ShapeDtypeStruct((8, 128), f32)] if token else []))
    out_specs = [SEM_SPEC] * nn + [HBM_SPEC] * na + ([pl.BlockSpec(memory_space=pltpu.VMEM)] if token else [])
    res = pl.pallas_call(
        kern, name=name, out_shape=tuple(out_shape),
        in_specs=[HBM_SPEC] * na + [SEM_SPEC] * ns + [ANY_SPEC] * nf, out_specs=tuple(out_specs),
        input_output_aliases={t: nn + t for t in range(na)},
        compiler_params=pltpu.CompilerParams(has_side_effects=pltpu.SideEffectType.DATAFLOW_SIDE_EFFECTING),
    )(*[pltpu.with_memory_space_constraint(a, pltpu.HBM) for a in arrays], *sems, *after)
    return list(res[:nn]), list(res[nn:nn + na]), (res[nn + na] if token else None)


def _remote(src, dst, send_sem, recv_sem, device):
    return pltpu.make_async_remote_copy(src_ref=src, dst_ref=dst, send_sem=send_sem, recv_sem=recv_sem,
                                        device_id=device, device_id_type=MESH)


SAME_CORE = (2, 4, 6)
VIA_SIBLING = (3, 5, 7)


def _weights_forward(name, land, recv, after):
    def body(a, s, new):
        (land,), (recv,), (fsend, frecv) = a, s, new
        x, y, c, me = _my_place()
        sibling, _ = _peer(x, y, c, 1)
        for k in SAME_CORE:
            peer, slot = _peer(x, y, c, k)
            _remote(land.at[slot], land.at[slot], fsend.at[k], recv.at[k], peer).wait_recv()
            _remote(land.at[slot], land.at[slot], fsend.at[k], frecv.at[k ^ 1], sibling).start()

    return _comm_call(name, [land], [recv], [(NDEV,), (NDEV,)], body, after=after)


def _weights_wait(name, land, send, recv, fsend, frecv, after):
    def body(a, s, new):
        (land,), (send, recv, fsend, frecv) = a, s
        x, y, c, me = _my_place()
        sibling, sib_slot = _peer(x, y, c, 1)
        _remote(land.at[sib_slot], land.at[sib_slot], send.at[1], recv.at[1], sibling).wait_recv()
        for k in VIA_SIBLING:
            _, slot = _peer(x, y, c, k)
            _remote(land.at[slot], land.at[slot], fsend.at[k ^ 1], frecv.at[k], sibling).wait_recv()
        for k in (1,) + SAME_CORE:
            peer, _ = _peer(x, y, c, k)
            _remote(land.at[me], land.at[me], send.at[k], recv.at[k], peer).wait_send()
        for k in SAME_CORE:
            _, slot = _peer(x, y, c, k)
            _remote(land.at[slot], land.at[slot], fsend.at[k], frecv.at[k ^ 1], sibling).wait_send()

    return _comm_call(name, [land], [send, recv, fsend, frecv], [], body, after=after)[1][0]


def _diag_relay(x, y, c):
    slot = 4 * (x ^ (1 - c)) + 2 * (y ^ c) + c
    return slot, (x ^ c, y ^ (1 - c), c)


def _w_in_start(land, after):
    def body(a, s, new):
        (land,), (send, recv) = a, new
        x, y, c, me = _my_place()
        for k in (1, 2, 4):
            peer, _ = _peer(x, y, c, k)
            _remote(land.at[me], land.at[me], send.at[k], recv.at[k], peer).start()

    (send, recv), (land,), token = _comm_call("w_in_start", [land], [], [(NDEV,), (NDEV,)], body, after=after, token=True)
    return send, recv, land, token


def _w_in_sibling(land, recv, after):
    def body(a, s, new):
        (land,), (recv,) = a, s
        x, y, c, me = _my_place()
        sibling, slot = _peer(x, y, c, 1)
        _remote(land.at[slot], land.at[slot], recv.at[1], recv.at[1], sibling).wait_recv()

    return _comm_call("w_in_sibling", [land], [recv], [], body, after=after)[1][0]


def _w_in_relay(land, recv, after):
    def body(a, s, new):
        (land,), (recv,), (fsend, frecv) = a, s, new
        x, y, c, me = _my_place()
        sibling, _ = _peer(x, y, c, 1)
        for k in (2, 4):
            peer, slot = _peer(x, y, c, k)
            _remote(land.at[slot], land.at[slot], fsend.at[k], recv.at[k], peer).wait_recv()
        slot, target = _diag_relay(x, y, c)
        _remote(land.at[slot], land.at[slot], fsend.at[6], frecv.at[6], target).start()
        for k in (2, 4):
            _, slot = _peer(x, y, c, k)
            _remote(land.at[slot], land.at[slot], fsend.at[k], frecv.at[k ^ 1], sibling).start()

    (fsend, frecv), (land,), _ = _comm_call("w_in_relay", [land], [recv], [(NDEV,), (NDEV,)], body, after=after)
    return fsend, frecv, land


def _w_in_forwarded(land, frecv, after):
    def body(a, s, new):
        (land,), (frecv,) = a, s
        x, y, c, me = _my_place()
        sibling, _ = _peer(x, y, c, 1)
        for k in (3, 5):
            _, slot = _peer(x, y, c, k)
            _remote(land.at[slot], land.at[slot], frecv.at[k], frecv.at[k], sibling).wait_recv()

    return _comm_call("w_in_forwarded", [land], [frecv], [], body, after=after)[1][0]


def _w_in_diag(land, land_o, frecv, after):
    def body(a, s, new):
        (land, land_o), (frecv,), (dsend, drecv, osend, orecv) = a, s, new
        x, y, c, me = _my_place()
        sibling, _ = _peer(x, y, c, 1)
        peer, slot = _peer(x, y, c, 6)
        _remote(land.at[slot], land.at[slot], dsend.at[6], frecv.at[6], peer).wait_recv()
        _remote(land.at[slot], land.at[slot], dsend.at[6], drecv.at[7], sibling).start()
        for k in (1,) + SAME_CORE:
            peer, _ = _peer(x, y, c, k)
            _remote(land_o.at[me], land_o.at[me], osend.at[k], orecv.at[k], peer).start()

    sems, (land, land_o), _ = _comm_call("w_in_diag", [land, land_o], [frecv], [(NDEV,)] * 4, body, after=after)
    return sems, land, land_o


def _w_in_finish(land, send, fsend, dsend, drecv, after):
    def body(a, s, new):
        (land,), (send, fsend, dsend, drecv) = a, s
        x, y, c, me = _my_place()
        sibling, _ = _peer(x, y, c, 1)
        _, slot = _peer(x, y, c, 7)
        _remote(land.at[slot], land.at[slot], dsend.at[6], drecv.at[7], sibling).wait_recv()
        for k in (1, 2, 4):
            peer, _ = _peer(x, y, c, k)
            _remote(land.at[me], land.at[me], send.at[k], send.at[k], peer).wait_send()
        for k in (2, 4, 6):
            _, slot = _peer(x, y, c, k)
            _remote(land.at[slot], land.at[slot], fsend.at[k], fsend.at[k], sibling).wait_send()
        _, slot = _peer(x, y, c, 6)
        _remote(land.at[slot], land.at[slot], dsend.at[6], dsend.at[6], sibling).wait_send()

    return _comm_call("w_in_finish", [land], [send, fsend, dsend, drecv], [], body, after=after)[1][0]


def _in_proj_part(name, h, land, proj, me_arr, k0, kstep, nk, tm=512):
    S, D = h.shape
    C = land.shape[2]
    tm = min(tm, S)

    def body(me_ref, a_ref, b_ref, *rest):
        rest[-1][...] = jnp.dot(a_ref[...], b_ref[...], preferred_element_type=f32)

    slot = lambda j, me: me[0] ^ (k0 + kstep * j)
    args = [h, land] + ([] if proj is None else [proj])
    grid_spec = pltpu.PrefetchScalarGridSpec(
        num_scalar_prefetch=1, grid=(nk, S // tm),
        in_specs=[pl.BlockSpec((tm, D), lambda j, i, me: (i, 0)),
                  pl.BlockSpec((None, D, C), lambda j, i, me: (slot(j, me), 0, 0))] + [ANY_SPEC] * (len(args) - 2),
        out_specs=pl.BlockSpec((tm, C), lambda j, i, me: (i, slot(j, me))))
    return pl.pallas_call(
        body, name=name, out_shape=jax.ShapeDtypeStruct((S, NDEV * C), f32), grid_spec=grid_spec,
        input_output_aliases={} if proj is None else {3: 0}, compiler_params=_params("arbitrary", "arbitrary"),
    )(me_arr, *args)


NCHIP = NDEV // 2


def _pairs_start(name, srcs):
    n = len(srcs)
    npairs = [src.shape[0] // 2 for src in srcs]

    def body(a, s, new):
        x, y, c, me = _my_place()
        sibling, _ = _peer(x, y, c, 1)
        for t in range(n):
            src, pair, send, recv = a[t], a[n + t], new[t], new[n + t]
            for i in range(npairs[t]):
                _remote(src.at[2 * i + 1 - c], pair.at[i], send.at[i], recv.at[i], sibling).start()

    pairs = [lax.empty((npairs[t],) + srcs[t].shape[1:], srcs[t].dtype) for t in range(n)]
    sems, arrays, token = _comm_call(name, list(srcs) + pairs, [], [(m,) for m in npairs] * 2, body, token=True)
    return [(sems[t], sems[n + t], arrays[t], arrays[n + t]) for t in range(n)], token


def _pairs_wait(name, groups, after):
    n = len(groups)

    def body(a, s, new):
        x, y, c, me = _my_place()
        sibling, _ = _peer(x, y, c, 1)
        for t in range(n):
            src, pair, send, recv = a[t], a[n + t], s[t], s[n + t]
            for i in range(pair.shape[0]):
                cp = _remote(src.at[2 * i + 1 - c], pair.at[i], send.at[i], recv.at[i], sibling)
                cp.wait_recv()
                cp.wait_send()

    arrays = _comm_call(name, [g[2] for g in groups] + [g[3] for g in groups],
                        [g[0] for g in groups] + [g[1] for g in groups], [], body, after=after)[1]
    return [(arrays[t], arrays[n + t]) for t in range(n)]


def _pair_start(name, src):
    ((send, recv, src, pair),), token = _pairs_start(name, [src])
    return send, recv, src, pair, token


def _pair_wait(name, src, pair, send, recv, after):
    return _pairs_wait(name, [(send, recv, src, pair)], after)[0]


def _pair_sum(name, src, pair, core, tr=1024):
    npair, R, Cc = pair.shape
    tr = min(tr, R)

    def body(core_ref, a_ref, b_ref, o_ref):
        o_ref[...] = (a_ref[...].astype(f32) + b_ref[...].astype(f32)).astype(o_ref.dtype)

    grid_spec = pltpu.PrefetchScalarGridSpec(
        num_scalar_prefetch=1, grid=(npair, R // tr),
        in_specs=[pl.BlockSpec((None, tr, Cc), lambda i, r, core: (2 * i + core[0], r, 0)),
                  pl.BlockSpec((None, tr, Cc), lambda i, r, core: (i, r, 0))],
        out_specs=pl.BlockSpec((None, tr, Cc), lambda i, r, core: (i, r, 0)))
    return pl.pallas_call(body, name=name, out_shape=jax.ShapeDtypeStruct(pair.shape, pair.dtype),
                          grid_spec=grid_spec, compiler_params=_params("parallel", "parallel"))(core, src, pair)


def _owner_chip(first, i):
    q = first // 2 + i
    return q >> 1 & 1, q & 1


def _chips_start(name, groups, rows=None, after=()):
    n = len(groups)
    row_of = [pl.ds(*(rows or (0, g[0].shape[1]))) for g in groups]

    def body(a, s, new):
        x, y, c, me = _my_place()
        for t, (_, _, first) in enumerate(groups):
            sums, land, send, recv = a[t], a[n + t], new[t], new[n + t]
            for i in range(sums.shape[0]):
                ox, oy = _owner_chip(first, i)

                @pl.when((x != ox) | (y != oy))
                def _():
                    _remote(sums.at[i, row_of[t]], land.at[2 * x + y, row_of[t]], send.at[i], recv.at[2 * x + y],
                            (ox, oy, c)).start()

    sems, arrays, token = _comm_call(name, [g[0] for g in groups] + [g[1] for g in groups], [],
                                     [(g[0].shape[0],) for g in groups] + [(NCHIP,)] * n, body, after=after, token=True)
    return [(sems[t], sems[n + t], arrays[t], arrays[n + t]) for t in range(n)], token


def _chips_wait(name, groups, firsts, after, rows=None):
    n = len(groups)
    row_of = [pl.ds(*(rows or (0, g[2].shape[1]))) for g in groups]

    def body(a, s, new):
        x, y, c, me = _my_place()
        for t in range(n):
            sums, land, send, recv, first = a[t], a[n + t], s[t], s[n + t], firsts[t]
            npair = sums.shape[0]
            mine = (me >= first) & (me < first + 2 * npair)
            for i in range(npair):
                ox, oy = _owner_chip(first, i)

                @pl.when((x != ox) | (y != oy))
                def _():
                    _remote(sums.at[i, row_of[t]], land.at[2 * x + y, row_of[t]], send.at[i], recv.at[2 * x + y],
                            (ox, oy, c)).wait_send()
            for q in range(NCHIP):
                @pl.when(mine & (2 * x + y != q))
                def _():
                    _remote(sums.at[0, row_of[t]], land.at[q, row_of[t]], send.at[0], recv.at[q],
                            (q >> 1, q & 1, c)).wait_recv()

    arrays = _comm_call(name, [g[2] for g in groups] + [g[3] for g in groups],
                        [g[0] for g in groups] + [g[1] for g in groups], [], body, after=after)[1]
    return [(arrays[t], arrays[n + t]) for t in range(n)]


def _chip_start(name, sums, land, first, rows=None, after=()):
    ((send, recv, sums, land),), token = _chips_start(name, [(sums, land, first)], rows, after)
    return send, recv, sums, land, token


def _chip_wait(name, sums, land, send, recv, first, after, rows=None):
    return _chips_wait(name, [(send, recv, sums, land)], [first], after, rows)[0]


def _matmul(a, b, *, name, out_dtype, ta=False, tb=False, b_slots=False, out_slots=0, b_cols=None,
            tm=1024, tn=1024, tk=2048, dep=None):
    M, K = (a.shape[1], a.shape[0]) if ta else a.shape
    col0 = 0
    if b_slots:
        slab = b.shape[2]
        N = b.shape[1] if tb else b.shape[0] * slab
        assert (K if tb else N) == b.shape[0] * slab
    elif b_cols is not None:
        assert not tb
        col0, N = b_cols
    else:
        N = b.shape[0] if tb else b.shape[1]
    tm, tn, tk = min(tm, M), min(tn, N), min(tk, K)
    if b_slots:
        if tb:
            tk = min(tk, slab)
        else:
            tn = min(tn, slab)
    if out_slots:
        tn = min(tn, N // out_slots)
    nm, nn, nk = M // tm, N // tn, K // tk
    assert (nm * tm, nn * tn, nk * tk) == (M, N, K) and col0 % tn == 0, (name, M, N, K, tm, tn, tk)
    j0 = col0 // tn

    a_spec = pl.BlockSpec((tk, tm), lambda i, j, k: (k, i)) if ta else pl.BlockSpec((tm, tk), lambda i, j, k: (i, k))
    if b_slots and tb:
        per = slab // tk
        b_spec = pl.BlockSpec((None, tn, tk), lambda i, j, k: (k // per, j, k % per))
    elif b_slots:
        per = slab // tn
        b_spec = pl.BlockSpec((None, tk, tn), lambda i, j, k: (j // per, k, j % per))
    elif tb:
        b_spec = pl.BlockSpec((tn, tk), lambda i, j, k: (j, k))
    else:
        b_spec = pl.BlockSpec((tk, tn), lambda i, j, k: (k, j + j0))
    if out_slots:
        per_o = (N // out_slots) // tn
        o_spec = pl.BlockSpec((None, tm, tn), lambda i, j, k: (j // per_o, i, j % per_o))
        out_shape = jax.ShapeDtypeStruct((out_slots, M, N // out_slots), out_dtype)
    else:
        o_spec = pl.BlockSpec((tm, tn), lambda i, j, k: (i, j))
        out_shape = jax.ShapeDtypeStruct((M, N), out_dtype)
    dims = (((0 if ta else 1,), (1 if tb else 0,)), ((), ()))
    deps = [] if dep is None else [dep]

    def body(a_ref, b_ref, *rest):
        o_ref = rest[len(deps)]
        prod = lax.dot_general(a_ref[...], b_ref[...], dims, preferred_element_type=f32)
        if nk == 1:
            o_ref[...] = prod.astype(out_dtype)
            return
        acc_ref = rest[len(deps) + 1]
        k = pl.program_id(2)

        @pl.when(k == 0)
        def _():
            acc_ref[...] = prod

        @pl.when((k > 0) & (k < nk - 1))
        def _():
            acc_ref[...] += prod

        @pl.when(k == nk - 1)
        def _():
            o_ref[...] = (acc_ref[...] + prod).astype(out_dtype)

    return pl.pallas_call(
        body, name=name, out_shape=out_shape, grid=(nm, nn, nk),
        in_specs=[a_spec, b_spec] + [ANY_SPEC] * len(deps), out_specs=o_spec,
        scratch_shapes=[pltpu.VMEM((tm, tn), f32)] if nk > 1 else [],
        compiler_params=_params("parallel", "parallel", "arbitrary"),
    )(a, b, *deps)


def _matmul_slabs_t(a_cols, a_slots, b, *, name, tm=512, tn=512, dep=None):
    M = a_cols.shape[0]
    n_slab, N, slab = b.shape
    n1, n2 = a_cols.shape[1] // slab, a_slots.shape[0]
    assert n1 + n2 == n_slab and a_slots.shape[1:] == (M, slab)
    tm, tn = min(tm, M), min(tn, N)
    deps = [] if dep is None else [dep]

    def body(a1_ref, a2_ref, b_ref, *rest):
        o_ref = rest[len(deps)]
        acc = None
        for s in range(n_slab):
            lhs = a1_ref[:, s * slab:(s + 1) * slab] if s < n1 else a2_ref[s - n1]
            prod = lax.dot_general(lhs, b_ref[s], (((1,), (1,)), ((), ())), preferred_element_type=f32)
            acc = prod if acc is None else acc + prod
        o_ref[...] = acc

    return pl.pallas_call(
        body, name=name, out_shape=jax.ShapeDtypeStruct((M, N), f32), grid=(M // tm, N // tn),
        in_specs=[pl.BlockSpec((tm, n1 * slab), lambda i, j: (i, 0)), pl.BlockSpec((n2, tm, slab), lambda i, j: (0, i, 0)),
                  pl.BlockSpec((n_slab, tn, slab), lambda i, j: (0, j, 0))] + [ANY_SPEC] * len(deps),
        out_specs=pl.BlockSpec((tm, tn), lambda i, j: (i, j)), compiler_params=_params("parallel", "parallel"),
    )(a_cols, a_slots, b, *deps)


def _ada_exchange(c_blk, cw_slab, w_ada, b_cols):
    nblk = c_blk.shape[0]
    D, W = w_ada.shape
    CW = cw_slab.shape[1]

    def body(c_ref, cw_ref, w_ref, b_ref, mod_ref, call_ref, cwg_ref, msend, send_sems, recv_sems):
        x, y, c, me = _my_place()
        call_ref[me] = _silu(c_ref[...])
        cwg_ref[me] = cw_ref[...]
        first = []
        for k in range(1, NDEV):
            peer, _ = _peer(x, y, c, k)
            first.append(_remote(call_ref.at[me], call_ref.at[me], send_sems.at[0, k], recv_sems.at[0, k], peer))
            first.append(_remote(cwg_ref.at[me], cwg_ref.at[me], send_sems.at[1, k], recv_sems.at[1, k], peer))
        for cp in first:
            cp.start()
        for k in range(1, NDEV):
            peer, slot = _peer(x, y, c, k)
            _remote(call_ref.at[slot], call_ref.at[slot], send_sems.at[0, k], recv_sems.at[0, k], peer).wait_recv()
            _remote(cwg_ref.at[slot], cwg_ref.at[slot], send_sems.at[1, k], recv_sems.at[1, k], peer).wait_recv()
        mod = jnp.broadcast_to(b_ref[...], (NDEV, W))
        for r in range(nblk):
            mod = mod + lax.dot_general(call_ref[:, r, :], w_ref[r * 128:(r + 1) * 128, :], (((1,), (0,)), ((), ())),
                                        preferred_element_type=f32, precision=lax.Precision.HIGHEST)
        row = lax.broadcasted_iota(jnp.int32, (NDEV, 1), 0)
        pick = lambda j: jnp.broadcast_to(jnp.sum(jnp.where(row == j, mod, 0.0), axis=0, keepdims=True), (8, W))
        mod_ref[me] = pick(me)
        second = []
        for k in range(1, NDEV):
            peer, slot = _peer(x, y, c, k)
            msend[k] = pick(slot)
            second.append(_remote(msend.at[k], mod_ref.at[me], send_sems.at[2, k], recv_sems.at[2, k], peer))
        for cp in second:
            cp.start()
        for k in range(1, NDEV):
            peer, slot = _peer(x, y, c, k)
            _remote(msend.at[k], mod_ref.at[slot], send_sems.at[2, k], recv_sems.at[2, k], peer).wait_recv()
        for cp in first + second:
            cp.wait_send()

    vmem = pl.BlockSpec(memory_space=pltpu.VMEM)
    return pl.pallas_call(
        body, name="ada_exchange",
        out_shape=(jax.ShapeDtypeStruct((NDEV, 8, W), f32), jax.ShapeDtypeStruct((NDEV, nblk, 128), f32),
                   jax.ShapeDtypeStruct((NDEV, 8, CW), f32)),
        in_specs=[vmem] * 4, out_specs=(vmem, vmem, vmem),
        scratch_shapes=[pltpu.VMEM((NDEV, 8, W), f32), pltpu.SemaphoreType.DMA((3, NDEV)),
                        pltpu.SemaphoreType.DMA((3, NDEV))],
        compiler_params=_params(),
    )(c_blk, cw_slab, w_ada, b_cols)


def _prenorm(x, scale, shift, g_pre, dep, tr=512):
    S, D = x.shape
    tr = min(tr, S)

    def body(x_ref, sc_ref, sh_ref, g_ref, dep_ref, h_ref):
        xv = x_ref[...]
        r = lax.rsqrt(jnp.mean(xv * xv, axis=-1, keepdims=True) + EPS)
        h_ref[...] = ((xv * r) * g_ref[...] * (1.0 + sc_ref[...]) + sh_ref[...]).astype(bf16)

    row = pl.BlockSpec((tr, D), lambda i: (i, 0))
    vec = pl.BlockSpec((1, D), lambda i: (0, 0))
    return pl.pallas_call(body, name="prenorm", out_shape=jax.ShapeDtypeStruct((S, D), bf16), grid=(S // tr,),
                          in_specs=[row, vec, vec, vec, ANY_SPEC], out_specs=row, compiler_params=_params("parallel"))(
                              x, scale, shift, g_pre, dep)


def _ext_rows(i, tr, S):
    g = lax.broadcasted_iota(jnp.int32, (tr + 16, 1), 0) + (i * tr - 8)
    return (g >= 0) & (g < S)


def _halo_specs(tr, S, C, col):
    nb8 = S // 8
    main = pl.BlockSpec((tr, C), lambda i: (i, col))
    prev = pl.BlockSpec((8, C), lambda i: (jnp.maximum(i * (tr // 8) - 1, 0), col))
    nxt = pl.BlockSpec((8, C), lambda i: (jnp.minimum((i + 1) * (tr // 8), nb8 - 1), col))
    return prev, main, nxt


def _conv_fwd(proj, conv_w, conv_b, g_conv, tr=512):
    S, C = proj.shape[0], proj.shape[1] // 8
    tr = min(tr, S)

    def body(up, um, un, cp, cm, cn, bg_ref, zc_ref, w_ref, cb_ref, g_ref, o_ref):
        i = pl.program_id(0)
        exists = _ext_rows(i, tr, S)
        u = jnp.concatenate([up[...], um[...], un[...]], axis=0)
        cg = jnp.concatenate([cp[...], cm[...], cn[...]], axis=0)
        t = jnp.where(exists, cg * u, 0.0)
        t_before = pltpu.roll(t, 1, 0)[8:tr + 8]
        t_after = pltpu.roll(t, tr + 15, 0)[8:tr + 8]
        w = w_ref[...]
        cv = w[0:1] * t_before + w[1:2] * t[8:tr + 8] + w[2:3] * t_after + cb_ref[...]
        yc = bg_ref[...] * cv
        rc = lax.rsqrt(jnp.mean(yc * yc, axis=-1, keepdims=True) + EPS)
        o_ref[...] = ((yc * rc) * g_ref[...] * _silu(zc_ref[...])).astype(bf16)

    u_specs = _halo_specs(tr, S, C, 0)
    c_specs = _halo_specs(tr, S, C, 2)
    vec = pl.BlockSpec((1, C), lambda i: (0, 0))
    return pl.pallas_call(
        body, name="conv_fwd", out_shape=jax.ShapeDtypeStruct((S, 2 * C), bf16), grid=(S // tr,),
        in_specs=[*u_specs, *c_specs, pl.BlockSpec((tr, C), lambda i: (i, 1)), pl.BlockSpec((tr, C), lambda i: (i, 3)),
                  pl.BlockSpec((8, C), lambda i: (0, 0)), vec, vec],
        out_specs=pl.BlockSpec((tr, C), lambda i: (i, 0)), compiler_params=_params("parallel"),
    )(proj, proj, proj, proj, proj, proj, proj, proj, conv_w, conv_b, g_conv)


def _branch_geometry(S, r, inter):
    L = S // r * inter
    nq = min(128, L)
    nk = min(nq + 2 * HALF_WIN * inter, L)
    assert L % nq == 0 and (L == nk or L >= nq + 2 * HALF_WIN * inter)
    return L, nq, nk, L // nq


QUAD = 4


def _to_quad(dst, src, S):
    n = S // QUAD
    for rho in range(QUAD):
        dst[pl.ds(rho * n, n), :] = src[pl.ds(rho, n, stride=QUAD), :]


def _block_rows(idx, r, inter, S, L, nq, nk, nblk):
    rho, qb = (0, idx) if r == 1 else (idx // nblk, idx % nblk)
    i0 = qb * nq
    ws = jnp.clip(i0 - HALF_WIN * inter, 0, L - nk)
    if r == 1:
        return pl.ds(pl.multiple_of(i0, 8), nq), pl.ds(pl.multiple_of(ws, 8), nk), i0 - ws
    assert r % (QUAD * inter) == 0
    step = r // QUAD // inter
    base = (rho % QUAD) * (S // QUAD) + rho // QUAD
    if step == 1:
        return pl.ds(pl.multiple_of(base + i0, 8), nq), pl.ds(pl.multiple_of(base + ws, 8), nk), i0 - ws
    return pl.ds(base + step * i0, nq, stride=step), pl.ds(base + step * ws, nk, stride=step), i0 - ws


N_CASES = 3
SCALE = HEAD_DIM ** -0.5
ATTN_UNROLL = 16


def _bias_shape(S):
    shapes = [_branch_geometry(S, r, inter)[1:3] for _, r, inter in BRANCHES]
    return (len(BRANCHES) * N_CASES * 2, max(nq for nq, _ in shapes), max(nk for _, nk in shapes))


def _bias_index(b, case, head):
    return (b * N_CASES + case) * 2 + head


def _fill_bias(bias_scr, sl_ref, S):
    sl = sl_ref[...]
    slope = (sl[0:1, 0:1], sl[0:1, HEAD_DIM:HEAD_DIM + 1])
    for b, (_, r, inter) in enumerate(BRANCHES):
        L, nq, nk, nblk = _branch_geometry(S, r, inter)
        rel = lax.broadcasted_iota(jnp.int32, (nq, nk), 0) - lax.broadcasted_iota(jnp.int32, (nq, nk), 1)
        for case in range(N_CASES):
            d = jnp.abs(rel + case * HALF_WIN)
            valid = d <= HALF_WIN * inter
            if inter > 1:
                valid = valid & (jnp.bitwise_and(d, inter - 1) == 0)
            dist = d.astype(f32) * float(r // inter)
            for head in range(2):
                bias_scr[_bias_index(b, case, head), 0:nq, 0:nk] = jnp.where(valid, -slope[head] * dist, NEG_INF)


def _bias_tiles(slopes, S, dep):
    npair = slopes.shape[0]
    shape = _bias_shape(S)

    def body(sl_ref, dep_ref, o_ref):
        _fill_bias(o_ref, sl_ref, S)

    return pl.pallas_call(
        body, name="bias_tiles", out_shape=jax.ShapeDtypeStruct((npair,) + shape, f32), grid=(npair,),
        in_specs=[pl.BlockSpec((None, 8, PAIR), lambda p: (p, 0, 0)), ANY_SPEC],
        out_specs=pl.BlockSpec((None,) + shape, lambda p: (p, 0, 0, 0)), compiler_params=_params("parallel"),
    )(slopes, dep)


def _head_slopes(n_heads):
    slopes = 2.0 ** (-8.0 * jnp.arange(1, n_heads + 1, dtype=f32) / n_heads)
    return jnp.broadcast_to(jnp.repeat(slopes.reshape(n_heads // 2, 2), HEAD_DIM, axis=1)[:, None, :],
                            (n_heads // 2, 8, PAIR))


def _attn_fwd(proj, bias):
    S, C = proj.shape[0], proj.shape[1] // 8
    npair = C // PAIR

    def body(q_ref, k_ref, v_ref, bias_scr, o_ref, lse_ref, m_scr, l_scr, a_scr, q4_scr, k4_scr, v4_scr):
        lane = lax.broadcasted_iota(jnp.int32, (1, PAIR), 1)
        first = lane < HEAD_DIM
        for dst, src in ((q4_scr, q_ref), (k4_scr, k_ref), (v4_scr, v_ref)):
            _to_quad(dst, src, S)

        for b, (_, r, inter) in enumerate(BRANCHES):
            L, nq, nk, nblk = _branch_geometry(S, r, inter)
            qs, ks, vs = (q_ref, k_ref, v_ref) if r == 1 else (q4_scr, k4_scr, v4_scr)

            def step(idx, carry, b=b, r=r, L=L, nq=nq, nk=nk, nblk=nblk, qs=qs, ks=ks, vs=vs):
                qrows, krows, off = _block_rows(idx, r, inter, S, L, nq, nk, nblk)
                case = off // HALF_WIN
                q2 = qs[qrows, :] * SCALE
                k2 = ks[krows, :].astype(bf16)
                v2 = vs[krows, :].astype(bf16)
                ms, accs = [], []
                for hh in range(2):
                    mine = first if hh == 0 else ~first
                    qh = jnp.where(mine, q2, 0.0).astype(bf16)
                    s = lax.dot_general(qh, k2, (((1,), (1,)), ((), ())), preferred_element_type=f32)
                    s = s + bias_scr[_bias_index(b, case, hh), 0:nq, 0:nk]
                    m = jnp.max(s, axis=-1, keepdims=True)
                    p = jnp.exp(s - m).astype(bf16)
                    vh = jnp.where(mine, v2, jnp.ones_like(v2))
                    ms.append(m)
                    accs.append(jnp.dot(p, vh, preferred_element_type=f32))
                m_scr[b, qrows, :] = jnp.where(first, ms[0], ms[1])
                a_scr[b, qrows, :] = jnp.where(first, accs[0], accs[1])
                l_scr[b, qrows, :] = jnp.where(first, accs[1], accs[0])
                return carry

            lax.fori_loop(0, S // nq, step, 0, unroll=min(ATTN_UNROLL, S // nq))

        n4 = S // QUAD
        ch = min(256, n4)
        nch = n4 // ch

        def merge(i, carry):
            rho, part = i // nch, i % nch
            sorted_rows = pl.ds(pl.multiple_of(rho * n4 + part * ch, 8), ch)
            token_rows = pl.ds(rho + QUAD * part * ch, ch, stride=QUAD)
            rows = (token_rows,) + (sorted_rows,) * (len(BRANCHES) - 1)
            ms = [m_scr[b, rows[b], :] for b in range(len(BRANCHES))]
            m = functools.reduce(jnp.maximum, ms)
            l = jnp.zeros((ch, PAIR), f32)
            acc = jnp.zeros((ch, PAIR), f32)
            for b in range(len(BRANCHES)):
                w = jnp.exp(ms[b] - m)
                l = l + w * pltpu.roll(l_scr[b, rows[b], :], HEAD_DIM, 1)
                acc = acc + w * a_scr[b, rows[b], :]
            o_ref[token_rows, :] = acc / l
            lse_ref[token_rows, :] = m + jnp.log(l)
            return carry

        lax.fori_loop(0, QUAD * nch, merge, 0, unroll=2)

    blk = lambda part: pl.BlockSpec((S, PAIR), lambda p: (0, part * npair + p))
    out = pl.BlockSpec((S, PAIR), lambda p: (0, p))
    return pl.pallas_call(
        body, name="attn_fwd",
        out_shape=(jax.ShapeDtypeStruct((S, C), f32), jax.ShapeDtypeStruct((S, C), f32)), grid=(npair,),
        in_specs=[blk(4), blk(5), blk(6), pl.BlockSpec((None,) + _bias_shape(S), lambda p: (p, 0, 0, 0))],
        out_specs=(out, out),
        scratch_shapes=[pltpu.VMEM((3, S, PAIR), f32)] * 3 + [pltpu.VMEM((S, PAIR), f32)] * 3,
        compiler_params=_params("parallel"),
    )(proj, proj, proj, bias)


def _attn_post(ycat, o, proj, g_attn, tr=512):
    S, C = o.shape
    tr = min(tr, S)

    def body(y_ref, o_ref, z_ref, g_ref, out_ref):
        del y_ref
        ov = o_ref[...]
        ra = lax.rsqrt(jnp.mean(ov * ov, axis=-1, keepdims=True) + EPS)
        out_ref[...] = ((ov * ra) * g_ref[...] * _silu(z_ref[...])).astype(bf16)

    return pl.pallas_call(
        body, name="attn_post", out_shape=jax.ShapeDtypeStruct(ycat.shape, ycat.dtype), grid=(S // tr,),
        in_specs=[HBM_SPEC, pl.BlockSpec((tr, C), lambda i: (i, 0)), pl.BlockSpec((tr, C), lambda i: (i, 7)),
                  pl.BlockSpec((1, C), lambda i: (0, 0))],
        out_specs=pl.BlockSpec((tr, C), lambda i: (i, 1)), input_output_aliases={0: 0},
        compiler_params=_params("arbitrary"),
    )(ycat, o, proj, g_attn)


def _residual_minus_target(x, target, dep, tr=512):
    S, D = x.shape
    tr = min(tr, S)

    def body(x_ref, t_ref, dep_ref, o_ref):
        o_ref[...] = x_ref[...] - t_ref[...]

    row = pl.BlockSpec((tr, D), lambda i: (i, 0))
    return pl.pallas_call(body, name="residual_minus_target", out_shape=jax.ShapeDtypeStruct((S, D), f32),
                          grid=(S // tr,), in_specs=[row, row, ANY_SPEC], out_specs=row,
                          compiler_params=_params("parallel"))(x, target, dep)


def _sandwich(y, x_minus_t, gate, g_post, tr=256):
    S, D = y.shape
    tr = min(tr, S)

    def body(y_ref, xt_ref, gate_ref, g_ref, dy_ref, dout_ref, sums_ref):
        i = pl.program_id(0)
        gate, g = gate_ref[...], g_ref[...]
        gg = gate * g
        yv = y_ref[...]
        rp = lax.rsqrt(jnp.mean(yv * yv, axis=-1, keepdims=True) + EPS)
        yhat = yv * rp
        err = xt_ref[...] + gg * yhat
        dout = err * (1.0 / D)
        dout_ref[...] = dout
        q = dout * yhat
        w = dout * gg
        dy_ref[...] = (rp * (w - yhat * jnp.sum(q * gg, axis=-1, keepdims=True) * (1.0 / D))).astype(bf16)
        loss = 0.5 * jnp.sum(jnp.mean(err * err, axis=-1, keepdims=True), axis=0, keepdims=True)
        q_sum = jnp.sum(q, axis=0, keepdims=True)
        row = lax.broadcasted_iota(jnp.int32, (8, D), 0)
        upd = jnp.where(row == 0, q_sum * g, jnp.where(row == 1, q_sum * gate, jnp.where(row == 2, loss, 0.0)))

        @pl.when(i == 0)
        def _():
            sums_ref[...] = upd

        @pl.when(i > 0)
        def _():
            sums_ref[...] += upd

    row = pl.BlockSpec((tr, D), lambda i: (i, 0))
    vec = pl.BlockSpec((1, D), lambda i: (0, 0))
    return pl.pallas_call(
        body, name="sandwich",
        out_shape=(jax.ShapeDtypeStruct((S, D), bf16), jax.ShapeDtypeStruct((S, D), f32), jax.ShapeDtypeStruct((8, D), f32)),
        grid=(S // tr,), in_specs=[row, row, vec, vec],
        out_specs=(row, row, pl.BlockSpec((8, D), lambda i: (0, 0))), compiler_params=_params("arbitrary"),
    )(y, x_minus_t, gate, g_post)


def _conv_bwd(proj, dycat, conv_w, conv_b, g_conv, dep, tr=256):
    S, C = proj.shape[0], proj.shape[1] // 8
    tr = min(tr, S)
    n = tr + 16

    def body(*refs):
        ins, (w_ref, cb_ref, g_ref, _, dp_ref, sums_ref) = refs[:15], refs[15:]
        i = pl.program_id(0)
        exists = _ext_rows(i, tr, S)
        u, bg, cg, zc, dyn = (jnp.concatenate([ins[3 * t][...], ins[3 * t + 1][...], ins[3 * t + 2][...]], axis=0)
                              for t in range(5))
        w = w_ref[...]
        t = jnp.where(exists, cg * u, 0.0)
        t_before, t_after = pltpu.roll(t, 1, 0), pltpu.roll(t, n - 1, 0)
        cv = w[0:1] * t_before + w[1:2] * t + w[2:3] * t_after + cb_ref[...]
        yc = bg * cv
        rc = lax.rsqrt(jnp.mean(yc * yc, axis=-1, keepdims=True) + EPS)
        yhat = yc * rc
        sz, dsz = _silu_and_slope(zc)
        wgt = dyn * g_ref[...] * sz
        dyc = rc * (wgt - yhat * jnp.mean(wgt * yhat, axis=-1, keepdims=True))
        dcv = jnp.where(exists, dyc * bg, 0.0)
        dt = w[0:1] * pltpu.roll(dcv, n - 1, 0) + w[1:2] * dcv + w[2:3] * pltpu.roll(dcv, 1, 0)
        mid = slice(8, tr + 8)
        dp_ref[:, 0:C] = (dt * cg)[mid].astype(bf16)
        dp_ref[:, C:2 * C] = (dyc * cv)[mid].astype(bf16)
        dp_ref[:, 2 * C:3 * C] = (dt * u)[mid].astype(bf16)
        dp_ref[:, 3 * C:4 * C] = (dyn * yhat * g_ref[...] * dsz)[mid].astype(bf16)
        colsum = lambda v: jnp.sum(v[mid], axis=0, keepdims=True)
        parts = [colsum(dyn * yhat * sz), colsum(dcv), colsum(dcv * t_before), colsum(dcv * t), colsum(dcv * t_after)]
        row = lax.broadcasted_iota(jnp.int32, (8, C), 0)
        upd = jnp.zeros((8, C), f32)
        for j, pj in enumerate(parts):
            upd = jnp.where(row == j, pj, upd)

        @pl.when(i == 0)
        def _():
            sums_ref[...] = upd

        @pl.when(i > 0)
        def _():
            sums_ref[...] += upd

    specs = []
    for col in range(4):
        specs += _halo_specs(tr, S, C, col)
    specs += _halo_specs(tr, S, C, 0)
    vec = pl.BlockSpec((1, C), lambda i: (0, 0))
    return pl.pallas_call(
        body, name="conv_bwd",
        out_shape=(jax.ShapeDtypeStruct((S, 4 * C), bf16), jax.ShapeDtypeStruct((8, C), f32)), grid=(S // tr,),
        in_specs=[*specs, pl.BlockSpec((8, C), lambda i: (0, 0)), vec, vec, ANY_SPEC],
        out_specs=(pl.BlockSpec((tr, 4 * C), lambda i: (i, 0)), pl.BlockSpec((8, C), lambda i: (0, 0))),
        compiler_params=_params("arbitrary"),
    )(*([proj] * 12), dycat, dycat, dycat, conv_w, conv_b, g_conv, dep)


def _attn_post_bwd(o, proj, dycat, g_attn, dep, tr=512):
    S, C = o.shape
    tr = min(tr, S)

    def body(o_ref, z_ref, dy_ref, g_ref, dep_ref, do_ref, dz_ref, sums_ref):
        i = pl.program_id(0)
        ov, zv, dyn = o_ref[...], z_ref[...], dy_ref[...]
        ra = lax.rsqrt(jnp.mean(ov * ov, axis=-1, keepdims=True) + EPS)
        ohat = ov * ra
        sz, dsz = _silu_and_slope(zv)
        wgt = dyn * g_ref[...] * sz
        do_ref[...] = ra * (wgt - ohat * jnp.mean(wgt * ohat, axis=-1, keepdims=True))
        dz_ref[...] = (dyn * ohat * g_ref[...] * dsz).astype(bf16)
        row = lax.broadcasted_iota(jnp.int32, (8, C), 0)
        upd = jnp.where(row == 0, jnp.sum(dyn * ohat * sz, axis=0, keepdims=True), 0.0)

        @pl.when(i == 0)
        def _():
            sums_ref[...] = upd

        @pl.when(i > 0)
        def _():
            sums_ref[...] += upd

    return pl.pallas_call(
        body, name="attn_post_bwd",
        out_shape=(jax.ShapeDtypeStruct((S, C), f32), jax.ShapeDtypeStruct((4, S, C), bf16),
                   jax.ShapeDtypeStruct((8, C), f32)),
        grid=(S // tr,),
        in_specs=[pl.BlockSpec((tr, C), lambda i: (i, 0)), pl.BlockSpec((tr, C), lambda i: (i, 7)),
                  pl.BlockSpec((tr, C), lambda i: (i, 1)), pl.BlockSpec((1, C), lambda i: (0, 0)), ANY_SPEC],
        out_specs=(pl.BlockSpec((tr, C), lambda i: (i, 0)), pl.BlockSpec((None, tr, C), lambda i: (3, i, 0)),
                   pl.BlockSpec((8, C), lambda i: (0, 0))),
        compiler_params=_params("arbitrary"),
    )(o, proj, dycat, g_attn, dep)


def _attn_bwd(proj, o, do, lse, bias, dqkvz, dep):
    S, C = o.shape
    npair = C // PAIR

    def body(q_ref, k_ref, v_ref, o_ref, do_ref, lse_ref, bias_scr, old_ref, dep_ref, dqkv_ref,
             acc_scr, dl_scr, quad_scr):
        lane = lax.broadcasted_iota(jnp.int32, (1, PAIR), 1)
        first = lane < HEAD_DIM
        ch = min(256, S)

        def prep(i, carry):
            rows = pl.ds(pl.multiple_of(i * ch, 8), ch)
            prod = do_ref[rows, :] * o_ref[rows, :]
            d0 = jnp.sum(jnp.where(first, prod, 0.0), axis=-1, keepdims=True)
            d1 = jnp.sum(jnp.where(first, 0.0, prod), axis=-1, keepdims=True)
            dl_scr[rows, :] = jnp.where(first, d0, d1)
            zero = jnp.zeros((ch, PAIR), f32)
            for order in range(2):
                for t in range(3):
                    acc_scr[order, t, rows, :] = zero
            return carry

        lax.fori_loop(0, S // ch, prep, 0, unroll=2)
        token_srcs = (q_ref, k_ref, v_ref, do_ref, lse_ref, dl_scr)
        for j, src in enumerate(token_srcs):
            _to_quad(quad_scr.at[j], src, S)

        for b, (_, r, inter) in enumerate(BRANCHES):
            L, nq, nk, nblk = _branch_geometry(S, r, inter)
            order = 0 if r == 1 else 1
            srcs = token_srcs if r == 1 else tuple(quad_scr.at[j] for j in range(6))

            def step(idx, carry, b=b, r=r, L=L, nq=nq, nk=nk, nblk=nblk, order=order, srcs=srcs):
                qs, ks, vs, dos, lses, dls = srcs
                dq_scr, dk_scr, dv_scr = (acc_scr.at[order, t] for t in range(3))
                qrows, krows, off = _block_rows(idx, r, inter, S, L, nq, nk, nblk)
                case = off // HALF_WIN
                q2 = qs[qrows, :] * SCALE
                k2 = ks[krows, :].astype(bf16)
                v2 = vs[krows, :].astype(bf16)
                do2 = dos[qrows, :]
                lse2 = lses[qrows, :]
                dl2 = dls[qrows, :]
                dq2 = jnp.zeros((nq, PAIR), f32)
                dk2 = jnp.zeros((nk, PAIR), f32)
                dv2 = jnp.zeros((nk, PAIR), f32)
                for hh in range(2):
                    mine = first if hh == 0 else ~first
                    lo = hh * HEAD_DIM
                    qh = jnp.where(mine, q2, 0.0).astype(bf16)
                    doh = jnp.where(mine, do2, 0.0).astype(bf16)
                    s = lax.dot_general(qh, k2, (((1,), (1,)), ((), ())), preferred_element_type=f32)
                    s = s + bias_scr[_bias_index(b, case, hh), 0:nq, 0:nk]
                    p = jnp.exp(s - lse2[:, lo:lo + 1])
                    dv2 = dv2 + lax.dot_general(p.astype(bf16), doh, (((0,), (0,)), ((), ())), preferred_element_type=f32)
                    dp = lax.dot_general(doh, v2, (((1,), (1,)), ((), ())), preferred_element_type=f32)
                    ds = (p * (dp - dl2[:, lo:lo + 1])).astype(bf16)
                    dq2 = dq2 + jnp.where(mine, jnp.dot(ds, k2, preferred_element_type=f32), 0.0)
                    dk2 = dk2 + lax.dot_general(ds, qh, (((0,), (0,)), ((), ())), preferred_element_type=f32)
                dq_scr[qrows, :] = dq_scr[qrows, :] + dq2
                dk_scr[krows, :] = dk_scr[krows, :] + dk2
                dv_scr[krows, :] = dv_scr[krows, :] + dv2
                return carry

            lax.fori_loop(0, S // nq, step, 0, unroll=min(ATTN_UNROLL, S // nq))

        n4 = S // QUAD
        for t in range(3):
            for rho in range(QUAD):
                token_rows = pl.ds(rho, n4, stride=QUAD)
                acc_scr[0, t, token_rows, :] = acc_scr[0, t, token_rows, :] + acc_scr[1, t, pl.ds(rho * n4, n4), :]
        dqkv_ref[0] = (acc_scr[0, 0] * SCALE).astype(bf16)
        dqkv_ref[1] = acc_scr[0, 1].astype(bf16)
        dqkv_ref[2] = acc_scr[0, 2].astype(bf16)

    blk = lambda part: pl.BlockSpec((S, PAIR), lambda p: (0, part * npair + p))
    own = pl.BlockSpec((S, PAIR), lambda p: (0, p))
    return pl.pallas_call(
        body, name="attn_bwd", out_shape=jax.ShapeDtypeStruct(dqkvz.shape, dqkvz.dtype), grid=(npair,),
        in_specs=[blk(4), blk(5), blk(6), own, own, own,
                  pl.BlockSpec((None,) + _bias_shape(S), lambda p: (p, 0, 0, 0)), ANY_SPEC, ANY_SPEC],
        out_specs=pl.BlockSpec((3, S, PAIR), lambda p: (0, 0, p)), input_output_aliases={7: 0},
        scratch_shapes=[pltpu.VMEM((2, 3, S, PAIR), f32), pltpu.VMEM((S, PAIR), f32), pltpu.VMEM((6, S, PAIR), f32)],
        compiler_params=_params("arbitrary"),
    )(proj, proj, proj, o, do, lse, bias, dqkvz, dep)


def _prenorm_bwd(dh, x, dout, scale, g_pre, tr=256):
    S, D = x.shape
    tr = min(tr, S)

    def body(dh_ref, x_ref, dout_ref, sc_ref, g_ref, gx_ref, sums_ref):
        i = pl.program_id(0)
        xv, dhv = x_ref[...], dh_ref[...]
        r = lax.rsqrt(jnp.mean(xv * xv, axis=-1, keepdims=True) + EPS)
        xn = xv * r
        dxn = dhv * (g_ref[...] * (1.0 + sc_ref[...]))
        gx_ref[...] = dout_ref[...] + r * (dxn - xn * jnp.mean(dxn * xn, axis=-1, keepdims=True))
        dhx = dhv * xn
        row = lax.broadcasted_iota(jnp.int32, (8, D), 0)
        upd = jnp.where(row == 0, jnp.sum(dhv, axis=0, keepdims=True),
                        jnp.where(row == 1, jnp.sum(dhx, axis=0, keepdims=True) * g_ref[...],
                                  jnp.where(row == 2, jnp.sum(dhx, axis=0, keepdims=True) * (1.0 + sc_ref[...]), 0.0)))

        @pl.when(i == 0)
        def _():
            sums_ref[...] = upd

        @pl.when(i > 0)
        def _():
            sums_ref[...] += upd

    row = pl.BlockSpec((tr, D), lambda i: (i, 0))
    vec = pl.BlockSpec((1, D), lambda i: (0, 0))
    return pl.pallas_call(
        body, name="prenorm_bwd",
        out_shape=(jax.ShapeDtypeStruct((S, D), f32), jax.ShapeDtypeStruct((8, D), f32)), grid=(S // tr,),
        in_specs=[row, row, row, vec, vec], out_specs=(row, pl.BlockSpec((8, D), lambda i: (0, 0))),
        compiler_params=_params("arbitrary"),
    )(dh, x, dout, scale, g_pre)


def _adamw(w, g, m, v):
    m = ADAM_B1 * m + (1.0 - ADAM_B1) * g
    v = ADAM_B2 * v + (1.0 - ADAM_B2) * (g * g)
    m_hat = m / (1.0 - ADAM_B1 ** ADAM_STEP)
    v_hat = v / (1.0 - ADAM_B2 ** ADAM_STEP)
    delta = -ADAM_LR * (m_hat / (jnp.sqrt(v_hat) + ADAM_EPS) + ADAM_WD * w)
    return delta, m, v


def _sum_rows(parts, dep):
    P = parts.shape[1]

    def body(p_ref, dep_ref, o_ref):
        acc = p_ref[0:1, :]
        for j in range(1, NDEV):
            acc = acc + p_ref[j:j + 1, :]
        o_ref[...] = jnp.broadcast_to(acc, (8, P))

    vmem = pl.BlockSpec(memory_space=pltpu.VMEM)
    return pl.pallas_call(body, name="sum_small", out_shape=jax.ShapeDtypeStruct((8, P), f32),
                          in_specs=[vmem, ANY_SPEC], out_specs=vmem, compiler_params=_params())(parts, dep)


def _adamw_small(tot, params):
    given = [p[3] for p in params if not isinstance(p[3], int)]

    def body(tot_ref, *refs):
        given_refs = list(refs[:len(given)])
        ins = refs[len(given):len(given) + 3 * len(params)]
        outs = refs[len(given) + 3 * len(params):]
        for t, (w, _, _, where) in enumerate(params):
            w_ref, m_ref, v_ref = ins[3 * t:3 * t + 3]
            g = tot_ref[0:1, where:where + w.size] if isinstance(where, int) else given_refs.pop(0)[...]
            outs[4 * t][...] = g
            outs[4 * t + 1][...], outs[4 * t + 2][...], outs[4 * t + 3][...] = _adamw(w_ref[...], g, m_ref[...], v_ref[...])

    out_shape = tuple(jax.ShapeDtypeStruct(p[0].shape, f32) for p in params for _ in range(4))
    res = pl.pallas_call(body, name="adamw_small", out_shape=out_shape, compiler_params=_params())(
        tot, *given, *[a for p in params for a in p[:3]])
    return [res[4 * t:4 * t + 4] for t in range(len(params))]


def _adamw_sharded(name, parts, sums_a, sums_b, pick, w, m, v, rows=None, prev=None, tr=128):
    R, Cc = w.shape
    r0, nr = rows or (0, R)
    tr = math.gcd(tr, r0, nr)
    n, b0 = parts.shape[0], r0 // tr

    def body(pick_ref, p_ref, a_ref, b_ref, w_ref, m_ref, v_ref, *rest):
        g_ref, d_ref, nm_ref, nv_ref = rest[-4:]
        g = jnp.where(pick_ref[0] == 1, b_ref[...], a_ref[...]).astype(f32)
        for j in range(n):
            g = g + p_ref[j].astype(f32)
        g_ref[...] = g
        d_ref[...], nm_ref[...], nv_ref[...] = _adamw(w_ref[...], g, m_ref[...], v_ref[...])

    row = pl.BlockSpec((tr, Cc), lambda i, pick: (i + b0, 0))
    mine = pl.BlockSpec((None, tr, Cc), lambda i, pick: (pick[1], i + b0, 0))
    out = jax.ShapeDtypeStruct((R, Cc), f32)
    prev = list(prev or [])
    grid_spec = pltpu.PrefetchScalarGridSpec(
        num_scalar_prefetch=1, grid=(nr // tr,),
        in_specs=[pl.BlockSpec((n, tr, Cc), lambda i, pick: (0, i + b0, 0)), mine, mine, row, row, row]
        + [ANY_SPEC] * len(prev),
        out_specs=(row, row, row, row))
    return pl.pallas_call(
        body, name=name, out_shape=(out, out, out, out), grid_spec=grid_spec,
        input_output_aliases={7 + t: t for t in range(len(prev))}, compiler_params=_params("arbitrary"),
    )(pick, parts, sums_a, sums_b, w, m, v, *prev)


def _adamw_ada(c_t, dmod_cols, w, m, v, dep, tr=512):
    D, W = w.shape
    tr = min(tr, D)

    def body(c_ref, dm_ref, w_ref, m_ref, v_ref, dep_ref, g_ref, d_ref, nm_ref, nv_ref):
        cv, dm = c_ref[...], dm_ref[...]
        g = cv[:, 0:1] * dm[0:1, :]
        for b in range(1, NDEV):
            g = g + cv[:, b:b + 1] * dm[b:b + 1, :]
        g_ref[...] = g
        d_ref[...], nm_ref[...], nv_ref[...] = _adamw(w_ref[...], g, m_ref[...], v_ref[...])

    row = pl.BlockSpec((tr, W), lambda i: (i, 0))
    out = jax.ShapeDtypeStruct((D, W), f32)
    return pl.pallas_call(
        body, name="adamw_ada", out_shape=(out, out, out, out), grid=(D // tr,),
        in_specs=[pl.BlockSpec((tr, NDEV), lambda i: (i, 0)), pl.BlockSpec((NDEV, W), lambda i: (0, 0)), row, row, row,
                  ANY_SPEC],
        out_specs=(row, row, row, row), compiler_params=_params("parallel"),
    )(c_t, dmod_cols, w, m, v, dep)


def kernel(x, c, w_ada, b_ada, g_pre, w_in, conv_w, conv_b, g_conv, g_attn, w_out, g_post, loss_target, m_w_ada, m_b_ada, m_g_pre, m_w_in, m_conv_w, m_conv_b, m_g_conv, m_g_attn, m_w_out, m_g_post, v_w_ada, v_b_ada, v_g_pre, v_w_in, v_conv_w, v_conv_b, v_g_conv, v_g_attn, v_w_out, v_g_post):
    S, D = x.shape[1], x.shape[2]
    C = D // 2
    W = w_ada.shape[2]
    CW = conv_w.shape[2]
    me = 4 * lax.axis_index("x") + 2 * lax.axis_index("y") + lax.axis_index("c")
    x2, tgt = x[0], loss_target[0]
    w_ada2, w_in2, w_out2 = w_ada[0], w_in[0], w_out[0]

    R = D // NDEV
    core = lax.axis_index("c").astype(jnp.int32).reshape(1)

    cw_slab = jnp.zeros((8, CW), f32).at[:3].set(conv_w[0])
    b_cols = lax.dynamic_slice_in_dim(b_ada, me * W, W, axis=1)
    mod_slabs, c_blocks, cw_g = _ada_exchange(c.reshape(D // 128, 128), cw_slab, w_ada2, b_cols)
    c_all = c_blocks.reshape(NDEV, D)
    conv_w_full = jnp.transpose(cw_g, (1, 0, 2)).reshape(8, C)
    mod = mod_slabs[:, 0, :].reshape(1, 3 * D)
    shift, scale, gate = mod[:, :D], mod[:, D:2 * D], mod[:, 2 * D:]

    land_i = lax.dynamic_update_slice(lax.empty((NDEV, D, C), bf16), w_in2.astype(bf16)[None], (me, 0, 0))
    land_o = lax.dynamic_update_slice(lax.empty((NDEV, R, D), bf16), w_out2.astype(bf16)[None], (me, 0, 0))
    wi_send, wi_recv, land_i, w_token = _w_in_start(land_i, [mod_slabs])

    me_arr = me.astype(jnp.int32).reshape(1)
    h = _prenorm(x2, scale, shift, g_pre, w_token)
    land_i = _w_in_sibling(land_i, wi_recv, after=[h])
    proj = _in_proj_part("in_proj_a", h, land_i, None, me_arr, 0, 1, 2)
    x_minus_t = _residual_minus_target(x2, tgt, proj)
    bias = _bias_tiles(_head_slopes(C // HEAD_DIM), S, x_minus_t)

    def landing(rows, cols):
        return lax.dynamic_update_slice(lax.empty((NCHIP, rows, cols), bf16), jnp.zeros((1, rows, cols), bf16),
                                        (me // 2, 0, 0))

    land_go, land_gi = landing(R, D), landing(D, C)
    fi_send, fi_recv, land_i = _w_in_relay(land_i, wi_recv, after=[proj, bias, land_go, land_gi])
    proj = _in_proj_part("in_proj_b", h, land_i, proj, me_arr, 2, 2, 2)
    land_i = _w_in_forwarded(land_i, fi_recv, after=[proj])
    proj = _in_proj_part("in_proj_c", h, land_i, proj, me_arr, 3, 2, 2)
    (di_send, di_recv, wo_send, wo_recv), land_i, land_o = _w_in_diag(land_i, land_o, fi_recv, after=[proj])
    proj = _in_proj_part("in_proj_d", h, land_i, proj, me_arr, 6, 1, 1)
    win_g = _w_in_finish(land_i, wi_send, fi_send, di_send, di_recv, after=[proj])
    proj = _in_proj_part("in_proj_e", h, win_g, proj, me_arr, 7, 1, 1)
    ycat = _conv_fwd(proj, conv_w_full, conv_b, g_conv)
    o, lse = _attn_fwd(proj, bias)
    (fo_send, fo_recv), (land_o,), _ = _weights_forward("w_out_forward", land_o, wo_recv, after=[o])
    ycat = _attn_post(ycat, o, proj, g_attn)
    wout_g = _weights_wait("w_out_wait", land_o, wo_send, wo_recv, fo_send, fo_recv, after=[ycat])
    wout_full = wout_g.reshape(D, D)
    y = _matmul(ycat, wout_full, name="out_proj", out_dtype=f32)
    dy, dout, post_sums = _sandwich(y, x_minus_t, gate, g_post)

    gw_out = _matmul(ycat, dy, name="out_proj_dw", out_dtype=bf16, ta=True).reshape(NDEV, R, D)
    dycat = _matmul(dy, wout_full, name="out_proj_dx", out_dtype=f32, tb=True)
    dpc, conv_sums = _conv_bwd(proj, dycat, conv_w_full, conv_b, g_conv, gw_out)
    gw_c = _matmul(h, dpc, name="in_proj_dw_conv", out_dtype=bf16, ta=True, out_slots=4)
    first_pairs, p1_token = _pairs_start("g_first_pair_start", [gw_out, gw_c])
    do, dpa, attn_sums = _attn_post_bwd(o, proj, dycat, g_attn, p1_token)
    (gw_out, pair_o), (gw_c, pair_c) = _pairs_wait("g_first_pair_wait", first_pairs, after=[do])
    sum_o = _pair_sum("g_out_pair_sum", gw_out, pair_o, core)
    sum_c = _pair_sum("g_conv_pair_sum", gw_c, pair_c, core)
    ((co_send, co_recv, sum_o, land_go), (cc_send, cc_recv, sum_c, land_gi)), cc_token = _chips_start(
        "g_first_chip_start", [(sum_o, land_go, 0), (sum_c, land_gi, 0)])
    dpa = _attn_bwd(proj, o, do, lse, bias, dpa, cc_token)
    gw_a = _matmul(h, dpa, name="in_proj_dw_attn", out_dtype=bf16, ta=True, b_slots=True, out_slots=4)
    pa_send, pa_recv, gw_a, pair_a, pa_token = _pair_start("g_attn_pair_start", gw_a)
    sum_o, land_go = _chip_wait("g_out_chip_wait", sum_o, land_go, co_send, co_recv, 0, after=[pa_token])
    pick_out = jnp.stack([jnp.int32(0), me // 2]).astype(jnp.int32)
    g_w_out, d_w_out, nm_w_out, nv_w_out = _adamw_sharded(
        "adamw_w_out", land_go, sum_o, sum_o, pick_out, w_out2, m_w_out[0], v_w_out[0])
    gw_a, pair_a = _pair_wait("g_attn_pair_wait", gw_a, pair_a, pa_send, pa_recv, after=[g_w_out])
    sum_a = _pair_sum("g_attn_pair_sum", gw_a, pair_a, core)
    part_a, part_b = (0, 3 * D // 4), (3 * D // 4, D // 4)
    ca_send, ca_recv, sum_a, land_gi, ca_token = _chip_start("g_attn_chip_start_a", sum_a, land_gi, 4, part_a)
    dh = _matmul_slabs_t(dpc, dpa, win_g, name="in_proj_dx", dep=ca_token)
    grad_x, pre_sums = _prenorm_bwd(dh, x2, dout, scale, g_pre)

    small = jnp.concatenate([pre_sums[0:1], pre_sums[1:2], post_sums[0:1],
                             pre_sums[2:3], post_sums[1:2],
                             conv_sums[2:3], conv_sums[3:4], conv_sums[4:5],
                             conv_sums[1:2], conv_sums[0:1], attn_sums[0:1]], axis=1)
    small = jnp.concatenate([small.reshape(8 * D // 128, 128), jnp.broadcast_to(post_sums[2:3, :128], (8, 128))])
    (small_all,) = _all_gather([small], "gather_small")
    cb_send, cb_recv, sum_a, land_gi, cb_token = _chip_start("g_attn_chip_start_b", sum_a, land_gi, 4, part_b,
                                                             after=[small_all])
    small_all = small_all.reshape(NDEV, small.size)
    tot = _sum_rows(small_all, cb_token)
    loss = tot[0, 8 * D]
    g_conv_w = lax.dynamic_slice_in_dim(tot[0:1, 5 * D:5 * D + 3 * C].reshape(1, 3, C), me * CW, CW, axis=2)
    ((g_b_ada, d_b_ada, nm_b_ada, nv_b_ada), (g_g_pre, d_g_pre, nm_g_pre, nv_g_pre),
     (g_g_post, d_g_post, nm_g_post, nv_g_post), (g_conv_w, d_conv_w, nm_conv_w, nv_conv_w),
     (g_conv_b, d_conv_b, nm_conv_b, nv_conv_b), (g_g_conv, d_g_conv, nm_g_conv, nv_g_conv),
     (g_g_attn, d_g_attn, nm_g_attn, nv_g_attn)) = _adamw_small(tot, [
         (b_ada, m_b_ada, v_b_ada, 0), (g_pre, m_g_pre, v_g_pre, 3 * D), (g_post, m_g_post, v_g_post, 4 * D),
         (conv_w, m_conv_w, v_conv_w, g_conv_w), (conv_b, m_conv_b, v_conv_b, 5 * D + 3 * C),
         (g_conv, m_g_conv, v_g_conv, 5 * D + 4 * C), (g_attn, m_g_attn, v_g_attn, 5 * D + 5 * C)])

    dmod_cols = lax.dynamic_slice_in_dim(small_all[:, :3 * D], me * W, W, axis=1)
    g_w_ada, d_w_ada, nm_w_ada, nv_w_ada = _adamw_ada(c_all.T, dmod_cols, w_ada2, m_w_ada[0], v_w_ada[0], cb_token)

    pick_in = jnp.stack([me // 4, (me % 4) // 2]).astype(jnp.int32)
    sum_c, land_gi = _chip_wait("g_conv_chip_wait", sum_c, land_gi, cc_send, cc_recv, 0, after=[g_w_ada])
    sum_a, land_gi = _chip_wait("g_attn_chip_wait_a", sum_a, land_gi, ca_send, ca_recv, 4, [g_w_ada], part_a)
    first = _adamw_sharded("adamw_w_in_a", land_gi, sum_c, sum_a, pick_in, w_in2, m_w_in[0], v_w_in[0], rows=part_a,
                           tr=256)
    sum_a, land_gi = _chip_wait("g_attn_chip_wait_b", sum_a, land_gi, cb_send, cb_recv, 4, [first[0]], part_b)
    g_w_in, d_w_in, nm_w_in, nv_w_in = _adamw_sharded(
        "adamw_w_in_b", land_gi, sum_c, sum_a, pick_in, w_in2, m_w_in[0], v_w_in[0], rows=part_b, prev=first, tr=256)

    return (loss, grad_x[None],
            g_w_ada[None], g_b_ada, g_g_pre, g_w_in[None], g_conv_w, g_conv_b, g_g_conv, g_g_attn, g_w_out[None], g_g_post,
            d_w_ada[None], d_b_ada, d_g_pre, d_w_in[None], d_conv_w, d_conv_b, d_g_conv, d_g_attn, d_w_out[None], d_g_post,
            nm_w_ada[None], nm_b_ada, nm_g_pre, nm_w_in[None], nm_conv_w, nm_conv_b, nm_g_conv, nm_g_attn, nm_w_out[None], nm_g_post,
            nv_w_ada[None], nv_b_ada, nv_g_pre, nv_w_in[None], nv_conv_w, nv_conv_b, nv_g_conv, nv_g_attn, nv_w_out[None], nv_g_post)
```

```python
import functools
import math

import jax
import jax.numpy as jnp
from jax import lax
from jax.experimental import pallas as pl
from jax.experimental.pallas import tpu as pltpu

f32 = jnp.float32
bf16 = jnp.bfloat16

NDEV = 8
HEAD_DIM = 64
PAIR = 2 * HEAD_DIM
BRANCHES = ((128, 1, 1), (512, 4, 1), (2048, 16, 2))
HALF_WIN = 64
EPS = 1e-6
NEG_INF = -1e30
ADAM_LR, ADAM_B1, ADAM_B2, ADAM_EPS, ADAM_WD, ADAM_STEP = 0.001, 0.9, 0.999, 1e-08, 0.01, 10
MESH = pl.DeviceIdType.MESH
VMEM_LIMIT = 56 * 1024 * 1024
HBM_SPEC = pl.BlockSpec(memory_space=pltpu.HBM)
ANY_SPEC = pl.BlockSpec(memory_space=pl.ANY)
SEM_SPEC = pl.BlockSpec(memory_space=pltpu.SEMAPHORE)


def _params(*sem):
    return pltpu.CompilerParams(dimension_semantics=sem or None, vmem_limit_bytes=VMEM_LIMIT)


def _silu(z):
    return z * jax.nn.sigmoid(z)


def _silu_and_slope(z):
    s = jax.nn.sigmoid(z)
    return z * s, s * (1.0 + z * (1.0 - s))


def _my_place():
    x, y, c = lax.axis_index("x"), lax.axis_index("y"), lax.axis_index("c")
    return x, y, c, 4 * x + 2 * y + c


def _peer(x, y, c, k):
    px, py, pc = x ^ (k >> 2 & 1), y ^ (k >> 1 & 1), c ^ (k & 1)
    return (px, py, pc), 4 * px + 2 * py + pc


def _all_gather(arrays, name):
    n = len(arrays)

    def body(*refs):
        srcs, dsts = refs[:n], refs[n:2 * n]
        send_sems, recv_sems, local_sems = refs[2 * n:]
        x, y, c, me = _my_place()
        locals_, sends = [], []
        for t in range(n):
            own = pltpu.make_async_copy(srcs[t], dsts[t].at[me], local_sems.at[t])
            own.start()
            locals_.append(own)
            for k in range(1, NDEV):
                peer, pidx = _peer(x, y, c, k)
                cp = pltpu.make_async_remote_copy(
                    src_ref=srcs[t], dst_ref=dsts[t].at[me], send_sem=send_sems.at[t, k],
                    recv_sem=recv_sems.at[t, k], device_id=peer, device_id_type=MESH)
                cp.start()
                sends.append(cp)
        for t in range(n):
            for k in range(1, NDEV):
                peer, pidx = _peer(x, y, c, k)
                pltpu.make_async_remote_copy(
                    src_ref=srcs[t], dst_ref=dsts[t].at[pidx], send_sem=send_sems.at[t, k],
                    recv_sem=recv_sems.at[t, k], device_id=peer, device_id_type=MESH).wait_recv()
        for cp in sends:
            cp.wait_send()
        for cp in locals_:
            cp.wait()

    return pl.pallas_call(
        body, name=name,
        out_shape=tuple(jax.ShapeDtypeStruct((NDEV,) + a.shape, a.dtype) for a in arrays),
        in_specs=[HBM_SPEC] * n, out_specs=tuple([HBM_SPEC] * n),
        scratch_shapes=[pltpu.SemaphoreType.DMA((n, NDEV)), pltpu.SemaphoreType.DMA((n, NDEV)),
                        pltpu.SemaphoreType.DMA((n,))],
    )(*arrays)


def _comm_call(name, arrays, sems, new_sems, body, after=(), token=False):
    na, ns, nn, nf = len(arrays), len(sems), len(new_sems), len(after)

    def kern(*refs):
        ins, outs = refs[:na + ns + nf], refs[na + ns + nf:]
        body(ins[:na], ins[na:na + ns], outs[:nn])
        if token:
            outs[nn + na][...] = jnp.zeros((8, 128), f32)

    out_shape = ([pltpu.SemaphoreType.DMA(s) for s in new_sems] + [pltpu.HBM(a.shape, a.dtype) for a in arrays]
                 + ([jax.ShapeDtypeStruct((8, 128), f32)] if token else []))
    out_specs = [SEM_SPEC] * nn + [HBM_SPEC] * na + ([pl.BlockSpec(memory_space=pltpu.VMEM)] if token else [])
    res = pl.pallas_call(
        kern, name=name, out_shape=tuple(out_shape),
        in_specs=[HBM_SPEC] * na + [SEM_SPEC] * ns + [ANY_SPEC] * nf, out_specs=tuple(out_specs),
        input_output_aliases={t: nn + t for t in range(na)},
        compiler_params=pltpu.CompilerParams(has_side_effects=pltpu.SideEffectType.DATAFLOW_SIDE_EFFECTING),
    )(*[pltpu.with_memory_space_constraint(a, pltpu.HBM) for a in arrays], *sems, *after)
    return list(res[:nn]), list(res[nn:nn + na]), (res[nn + na] if token else None)


def _remote(src, dst, send_sem, recv_sem, device):
    return pltpu.make_async_remote_copy(src_ref=src, dst_ref=dst, send_sem=send_sem, recv_sem=recv_sem,
                                        device_id=device, device_id_type=MESH)


SAME_CORE = (2, 4, 6)
VIA_SIBLING = (3, 5, 7)


def _weights_forward(name, land, recv, after):
    def body(a, s, new):
        (land,), (recv,), (fsend, frecv) = a, s, new
        x, y, c, me = _my_place()
        sibling, _ = _peer(x, y, c, 1)
        for k in SAME_CORE:
            peer, slot = _peer(x, y, c, k)
            _remote(land.at[slot], land.at[slot], fsend.at[k], recv.at[k], peer).wait_recv()
            _remote(land.at[slot], land.at[slot], fsend.at[k], frecv.at[k ^ 1], sibling).start()

    return _comm_call(name, [land], [recv], [(NDEV,), (NDEV,)], body, after=after)


def _weights_wait(name, land, send, recv, fsend, frecv, after):
    def body(a, s, new):
        (land,), (send, recv, fsend, frecv) = a, s
        x, y, c, me = _my_place()
        sibling, sib_slot = _peer(x, y, c, 1)
        _remote(land.at[sib_slot], land.at[sib_slot], send.at[1], recv.at[1], sibling).wait_recv()
        for k in VIA_SIBLING:
            _, slot = _peer(x, y, c, k)
            _remote(land.at[slot], land.at[slot], fsend.at[k ^ 1], frecv.at[k], sibling).wait_recv()
        for k in (1,) + SAME_CORE:
            peer, _ = _peer(x, y, c, k)
            _remote(land.at[me], land.at[me], send.at[k], recv.at[k], peer).wait_send()
        for k in SAME_CORE:
            _, slot = _peer(x, y, c, k)
            _remote(land.at[slot], land.at[slot], fsend.at[k], frecv.at[k ^ 1], sibling).wait_send()

    return _comm_call(name, [land], [send, recv, fsend, frecv], [], body, after=after)[1][0]


def _diag_relay(x, y, c):
    slot = 4 * (x ^ (1 - c)) + 2 * (y ^ c) + c
    return slot, (x ^ c, y ^ (1 - c), c)


def _w_in_start(land, after):
    def body(a, s, new):
        (land,), (send, recv) = a, new
        x, y, c, me = _my_place()
        for k in (1, 2, 4):
            peer, _ = _peer(x, y, c, k)
            _remote(land.at[me], land.at[me], send.at[k], recv.at[k], peer).start()

    (send, recv), (land,), token = _comm_call("w_in_start", [land], [], [(NDEV,), (NDEV,)], body, after=after, token=True)
    return send, recv, land, token


def _w_in_sibling(land, recv, after):
    def body(a, s, new):
        (land,), (recv,) = a, s
        x, y, c, me = _my_place()
        sibling, slot = _peer(x, y, c, 1)
        _remote(land.at[slot], land.at[slot], recv.at[1], recv.at[1], sibling).wait_recv()

    return _comm_call("w_in_sibling", [land], [recv], [], body, after=after)[1][0]


def _w_in_relay(land, recv, after):
    def body(a, s, new):
        (land,), (recv,), (fsend, frecv) = a, s, new
        x, y, c, me = _my_place()
        sibling, _ = _peer(x, y, c, 1)
        for k in (2, 4):
            peer, slot = _peer(x, y, c, k)
            _remote(land.at[slot], land.at[slot], fsend.at[k], recv.at[k], peer).wait_recv()
        slot, target = _diag_relay(x, y, c)
        _remote(land.at[slot], land.at[slot], fsend.at[6], frecv.at[6], target).start()
        for k in (2, 4):
            _, slot = _peer(x, y, c, k)
            _remote(land.at[slot], land.at[slot], fsend.at[k], frecv.at[k ^ 1], sibling).start()

    (fsend, frecv), (land,), _ = _comm_call("w_in_relay", [land], [recv], [(NDEV,), (NDEV,)], body, after=after)
    return fsend, frecv, land


def _w_in_forwarded(land, frecv, after):
    def body(a, s, new):
        (land,), (frecv,) = a, s
        x, y, c, me = _my_place()
        sibling, _ = _peer(x, y, c, 1)
        for k in (3, 5):
            _, slot = _peer(x, y, c, k)
            _remote(land.at[slot], land.at[slot], frecv.at[k], frecv.at[k], sibling).wait_recv()

    return _comm_call("w_in_forwarded", [land], [frecv], [], body, after=after)[1][0]


def _w_in_diag(land, land_o, frecv, after):
    def body(a, s, new):
        (land, land_o), (frecv,), (dsend, drecv, osend, orecv) = a, s, new
        x, y, c, me = _my_place()
        sibling, _ = _peer(x, y, c, 1)
        peer, slot = _peer(x, y, c, 6)
        _remote(land.at[slot], land.at[slot], dsend.at[6], frecv.at[6], peer).wait_recv()
        _remote(land.at[slot], land.at[slot], dsend.at[6], drecv.at[7], sibling).start()
        for k in (1,) + SAME_CORE:
            peer, _ = _peer(x, y, c, k)
            _remote(land_o.at[me], land_o.at[me], osend.at[k], orecv.at[k], peer).start()

    sems, (land, land_o), _ = _comm_call("w_in_diag", [land, land_o], [frecv], [(NDEV,)] * 4, body, after=after)
    return sems, land, land_o


def _w_in_finish(land, send, fsend, dsend, drecv, after):
    def body(a, s, new):
        (land,), (send, fsend, dsend, drecv) = a, s
        x, y, c, me = _my_place()
        sibling, _ = _peer(x, y, c, 1)
        _, slot = _peer(x, y, c, 7)
        _remote(land.at[slot], land.at[slot], dsend.at[6], drecv.at[7], sibling).wait_recv()
        for k in (1, 2, 4):
            peer, _ = _peer(x, y, c, k)
            _remote(land.at[me], land.at[me], send.at[k], send.at[k], peer).wait_send()
        for k in (2, 4, 6):
            _, slot = _peer(x, y, c, k)
            _remote(land.at[slot], land.at[slot], fsend.at[k], fsend.at[k], sibling).wait_send()
        _, slot = _peer(x, y, c, 6)
        _remote(land.at[slot], land.at[slot], dsend.at[6], dsend.at[6], sibling).wait_send()

    return _comm_call("w_in_finish", [land], [send, fsend, dsend, drecv], [], body, after=after)[1][0]


def _in_proj_part(name, h, land, proj, me_arr, k0, kstep, nk, tm=512):
    S, D = h.shape
    C = land.shape[2]
    tm = min(tm, S)

    def body(me_ref, a_ref, b_ref, *rest):
        rest[-1][...] = jnp.dot(a_ref[...], b_ref[...], preferred_element_type=f32)

    slot = lambda j, me: me[0] ^ (k0 + kstep * j)
    args = [h, land] + ([] if proj is None else [proj])
    grid_spec = pltpu.PrefetchScalarGridSpec(
        num_scalar_prefetch=1, grid=(nk, S // tm),
        in_specs=[pl.BlockSpec((tm, D), lambda j, i, me: (i, 0)),
                  pl.BlockSpec((None, D, C), lambda j, i, me: (slot(j, me), 0, 0))] + [ANY_SPEC] * (len(args) - 2),
        out_specs=pl.BlockSpec((tm, C), lambda j, i, me: (i, slot(j, me))))
    return pl.pallas_call(
        body, name=name, out_shape=jax.ShapeDtypeStruct((S, NDEV * C), f32), grid_spec=grid_spec,
        input_output_aliases={} if proj is None else {3: 0}, compiler_params=_params("arbitrary", "arbitrary"),
    )(me_arr, *args)


NCHIP = NDEV // 2


def _pairs_start(name, srcs):
    n = len(srcs)
    npairs = [src.shape[0] // 2 for src in srcs]

    def body(a, s, new):
        x, y, c, me = _my_place()
        sibling, _ = _peer(x, y, c, 1)
        for t in range(n):
            src, pair, send, recv = a[t], a[n + t], new[t], new[n + t]
            for i in range(npairs[t]):
                _remote(src.at[2 * i + 1 - c], pair.at[i], send.at[i], recv.at[i], sibling).start()

    pairs = [lax.empty((npairs[t],) + srcs[t].shape[1:], srcs[t].dtype) for t in range(n)]
    sems, arrays, token = _comm_call(name, list(srcs) + pairs, [], [(m,) for m in npairs] * 2, body, token=True)
    return [(sems[t], sems[n + t], arrays[t], arrays[n + t]) for t in range(n)], token


def _pairs_wait(name, groups, after):
    n = len(groups)

    def body(a, s, new):
        x, y, c, me = _my_place()
        sibling, _ = _peer(x, y, c, 1)
        for t in range(n):
            src, pair, send, recv = a[t], a[n + t], s[t], s[n + t]
            for i in range(pair.shape[0]):
                cp = _remote(src.at[2 * i + 1 - c], pair.at[i], send.at[i], recv.at[i], sibling)
                cp.wait_recv()
                cp.wait_send()

    arrays = _comm_call(name, [g[2] for g in groups] + [g[3] for g in groups],
                        [g[0] for g in groups] + [g[1] for g in groups], [], body, after=after)[1]
    return [(arrays[t], arrays[n + t]) for t in range(n)]


def _pair_start(name, src):
    ((send, recv, src, pair),), token = _pairs_start(name, [src])
    return send, recv, src, pair, token


def _pair_wait(name, src, pair, send, recv, after):
    return _pairs_wait(name, [(send, recv, src, pair)], after)[0]


def _pair_sum(name, src, pair, core, tr=1024):
    npair, R, Cc = pair.shape
    tr = min(tr, R)

    def body(core_ref, a_ref, b_ref, o_ref):
        o_ref[...] = (a_ref[...].astype(f32) + b_ref[...].astype(f32)).astype(o_ref.dtype)

    grid_spec = pltpu.PrefetchScalarGridSpec(
        num_scalar_prefetch=1, grid=(npair, R // tr),
        in_specs=[pl.BlockSpec((None, tr, Cc), lambda i, r, core: (2 * i + core[0], r, 0)),
                  pl.BlockSpec((None, tr, Cc), lambda i, r, core: (i, r, 0))],
        out_specs=pl.BlockSpec((None, tr, Cc), lambda i, r, core: (i, r, 0)))
    return pl.pallas_call(body, name=name, out_shape=jax.ShapeDtypeStruct(pair.shape, pair.dtype),
                          grid_spec=grid_spec, compiler_params=_params("parallel", "parallel"))(core, src, pair)


def _owner_chip(first, i):
    q = first // 2 + i
    return q >> 1 & 1, q & 1


def _chips_start(name, groups, rows=None, after=()):
    n = len(groups)
    row_of = [pl.ds(*(rows or (0, g[0].shape[1]))) for g in groups]

    def body(a, s, new):
        x, y, c, me = _my_place()
        for t, (_, _, first) in enumerate(groups):
            sums, land, send, recv = a[t], a[n + t], new[t], new[n + t]
            for i in range(sums.shape[0]):
                ox, oy = _owner_chip(first, i)

                @pl.when((x != ox) | (y != oy))
                def _():
                    _remote(sums.at[i, row_of[t]], land.at[2 * x + y, row_of[t]], send.at[i], recv.at[2 * x + y],
                            (ox, oy, c)).start()

    sems, arrays, token = _comm_call(name, [g[0] for g in groups] + [g[1] for g in groups], [],
                                     [(g[0].shape[0],) for g in groups] + [(NCHIP,)] * n, body, after=after, token=True)
    return [(sems[t], sems[n + t], arrays[t], arrays[n + t]) for t in range(n)], token


def _chips_wait(name, groups, firsts, after, rows=None):
    n = len(groups)
    row_of = [pl.ds(*(rows or (0, g[2].shape[1]))) for g in groups]

    def body(a, s, new):
        x, y, c, me = _my_place()
        for t in range(n):
            sums, land, send, recv, first = a[t], a[n + t], s[t], s[n + t], firsts[t]
            npair = sums.shape[0]
            mine = (me >= first) & (me < first + 2 * npair)
            for i in range(npair):
                ox, oy = _owner_chip(first, i)

                @pl.when((x != ox) | (y != oy))
                def _():
                    _remote(sums.at[i, row_of[t]], land.at[2 * x + y, row_of[t]], send.at[i], recv.at[2 * x + y],
                            (ox, oy, c)).wait_send()
            for q in range(NCHIP):
                @pl.when(mine & (2 * x + y != q))
                def _():
                    _remote(sums.at[0, row_of[t]], land.at[q, row_of[t]], send.at[0], recv.at[q],
                            (q >> 1, q & 1, c)).wait_recv()

    arrays = _comm_call(name, [g[2] for g in groups] + [g[3] for g in groups],
                        [g[0] for g in groups] + [g[1] for g in groups], [], body, after=after)[1]
    return [(arrays[t], arrays[n + t]) for t in range(n)]


def _chip_start(name, sums, land, first, rows=None, after=()):
    ((send, recv, sums, land),), token = _chips_start(name, [(sums, land, first)], rows, after)
    return send, recv, sums, land, token


def _chip_wait(name, sums, land, send, recv, first, after, rows=None):
    return _chips_wait(name, [(send, recv, sums, land)], [first], after, rows)[0]


def _matmul(a, b, *, name, out_dtype, ta=False, tb=False, b_slots=False, out_slots=0, b_cols=None,
            tm=1024, tn=1024, tk=2048, dep=None):
    M, K = (a.shape[1], a.shape[0]) if ta else a.shape
    col0 = 0
    if b_slots:
        slab = b.shape[2]
        N = b.shape[1] if tb else b.shape[0] * slab
        assert (K if tb else N) == b.shape[0] * slab
    elif b_cols is not None:
        assert not tb
        col0, N = b_cols
    else:
        N = b.shape[0] if tb else b.shape[1]
    tm, tn, tk = min(tm, M), min(tn, N), min(tk, K)
    if b_slots:
        if tb:
            tk = min(tk, slab)
        else:
            tn = min(tn, slab)
    if out_slots:
        tn = min(tn, N // out_slots)
    nm, nn, nk = M // tm, N // tn, K // tk
    assert (nm * tm, nn * tn, nk * tk) == (M, N, K) and col0 % tn == 0, (name, M, N, K, tm, tn, tk)
    j0 = col0 // tn

    a_spec = pl.BlockSpec((tk, tm), lambda i, j, k: (k, i)) if ta else pl.BlockSpec((tm, tk), lambda i, j, k: (i, k))
    if b_slots and tb:
        per = slab // tk
        b_spec = pl.BlockSpec((None, tn, tk), lambda i, j, k: (k // per, j, k % per))
    elif b_slots:
        per = slab // tn
        b_spec = pl.BlockSpec((None, tk, tn), lambda i, j, k: (j // per, k, j % per))
    elif tb:
        b_spec = pl.BlockSpec((tn, tk), lambda i, j, k: (j, k))
    else:
        b_spec = pl.BlockSpec((tk, tn), lambda i, j, k: (k, j + j0))
    if out_slots:
        per_o = (N // out_slots) // tn
        o_spec = pl.BlockSpec((None, tm, tn), lambda i, j, k: (j // per_o, i, j % per_o))
        out_shape = jax.ShapeDtypeStruct((out_slots, M, N // out_slots), out_dtype)
    else:
        o_spec = pl.BlockSpec((tm, tn), lambda i, j, k: (i, j))
        out_shape = jax.ShapeDtypeStruct((M, N), out_dtype)
    dims = (((0 if ta else 1,), (1 if tb else 0,)), ((), ()))
    deps = [] if dep is None else [dep]

    def body(a_ref, b_ref, *rest):
        o_ref = rest[len(deps)]
        prod = lax.dot_general(a_ref[...], b_ref[...], dims, preferred_element_type=f32)
        if nk == 1:
            o_ref[...] = prod.astype(out_dtype)
            return
        acc_ref = rest[len(deps) + 1]
        k = pl.program_id(2)

        @pl.when(k == 0)
        def _():
            acc_ref[...] = prod

        @pl.when((k > 0) & (k < nk - 1))
        def _():
            acc_ref[...] += prod

        @pl.when(k == nk - 1)
        def _():
            o_ref[...] = (acc_ref[...] + prod).astype(out_dtype)

    return pl.pallas_call(
        body, name=name, out_shape=out_shape, grid=(nm, nn, nk),
        in_specs=[a_spec, b_spec] + [ANY_SPEC] * len(deps), out_specs=o_spec,
        scratch_shapes=[pltpu.VMEM((tm, tn), f32)] if nk > 1 else [],
        compiler_params=_params("parallel", "parallel", "arbitrary"),
    )(a, b, *deps)


def _matmul_slabs_t(a_cols, a_slots, b, *, name, tm=512, tn=512, dep=None):
    M = a_cols.shape[0]
    n_slab, N, slab = b.shape
    n1, n2 = a_cols.shape[1] // slab, a_slots.shape[0]
    assert n1 + n2 == n_slab and a_slots.shape[1:] == (M, slab)
    tm, tn = min(tm, M), min(tn, N)
    deps = [] if dep is None else [dep]

    def body(a1_ref, a2_ref, b_ref, *rest):
        o_ref = rest[len(deps)]
        acc = None
        for s in range(n_slab):
            lhs = a1_ref[:, s * slab:(s + 1) * slab] if s < n1 else a2_ref[s - n1]
            prod = lax.dot_general(lhs, b_ref[s], (((1,), (1,)), ((), ())), preferred_element_type=f32)
            acc = prod if acc is None else acc + prod
        o_ref[...] = acc

    return pl.pallas_call(
        body, name=name, out_shape=jax.ShapeDtypeStruct((M, N), f32), grid=(M // tm, N // tn),
        in_specs=[pl.BlockSpec((tm, n1 * slab), lambda i, j: (i, 0)), pl.BlockSpec((n2, tm, slab), lambda i, j: (0, i, 0)),
                  pl.BlockSpec((n_slab, tn, slab), lambda i, j: (0, j, 0))] + [ANY_SPEC] * len(deps),
        out_specs=pl.BlockSpec((tm, tn), lambda i, j: (i, j)), compiler_params=_params("parallel", "parallel"),
    )(a_cols, a_slots, b, *deps)


def _ada_exchange(c_blk, cw_slab, w_ada, b_cols):
    nblk = c_blk.shape[0]
    D, W = w_ada.shape
    CW = cw_slab.shape[1]

    def body(c_ref, cw_ref, w_ref, b_ref, mod_ref, call_ref, cwg_ref, msend, send_sems, recv_sems):
        x, y, c, me = _my_place()
        call_ref[me] = _silu(c_ref[...])
        cwg_ref[me] = cw_ref[...]
        first = []
        for k in range(1, NDEV):
            peer, _ = _peer(x, y, c, k)
            first.append(_remote(call_ref.at[me], call_ref.at[me], send_sems.at[0, k], recv_sems.at[0, k], peer))
            first.append(_remote(cwg_ref.at[me], cwg_ref.at[me], send_sems.at[1, k], recv_sems.at[1, k], peer))
        for cp in first:
            cp.start()
        for k in range(1, NDEV):
            peer, slot = _peer(x, y, c, k)
            _remote(call_ref.at[slot], call_ref.at[slot], send_sems.at[0, k], recv_sems.at[0, k], peer).wait_recv()
            _remote(cwg_ref.at[slot], cwg_ref.at[slot], send_sems.at[1, k], recv_sems.at[1, k], peer).wait_recv()
        mod = jnp.broadcast_to(b_ref[...], (NDEV, W))
        for r in range(nblk):
            mod = mod + lax.dot_general(call_ref[:, r, :], w_ref[r * 128:(r + 1) * 128, :], (((1,), (0,)), ((), ())),
                                        preferred_element_type=f32, precision=lax.Precision.HIGHEST)
        row = lax.broadcasted_iota(jnp.int32, (NDEV, 1), 0)
        pick = lambda j: jnp.broadcast_to(jnp.sum(jnp.where(row == j, mod, 0.0), axis=0, keepdims=True), (8, W))
        mod_ref[me] = pick(me)
        second = []
        for k in range(1, NDEV):
            peer, slot = _peer(x, y, c, k)
            msend[k] = pick(slot)
            second.append(_remote(msend.at[k], mod_ref.at[me], send_sems.at[2, k], recv_sems.at[2, k], peer))
        for cp in second:
            cp.start()
        for k in range(1, NDEV):
            peer, slot = _peer(x, y, c, k)
            _remote(msend.at[k], mod_ref.at[slot], send_sems.at[2, k], recv_sems.at[2, k], peer).wait_recv()
        for cp in first + second:
            cp.wait_send()

    vmem = pl.BlockSpec(memory_space=pltpu.VMEM)
    return pl.pallas_call(
        body, name="ada_exchange",
        out_shape=(jax.ShapeDtypeStruct((NDEV, 8, W), f32), jax.ShapeDtypeStruct((NDEV, nblk, 128), f32),
                   jax.ShapeDtypeStruct((NDEV, 8, CW), f32)),
        in_specs=[vmem] * 4, out_specs=(vmem, vmem, vmem),
        scratch_shapes=[pltpu.VMEM((NDEV, 8, W), f32), pltpu.SemaphoreType.DMA((3, NDEV)),
                        pltpu.SemaphoreType.DMA((3, NDEV))],
        compiler_params=_params(),
    )(c_blk, cw_slab, w_ada, b_cols)


def _prenorm(x, scale, shift, g_pre, dep, tr=512):
    S, D = x.shape
    tr = min(tr, S)

    def body(x_ref, sc_ref, sh_ref, g_ref, dep_ref, h_ref):
        xv = x_ref[...]
        r = lax.rsqrt(jnp.mean(xv * xv, axis=-1, keepdims=True) + EPS)
        h_ref[...] = ((xv * r) * g_ref[...] * (1.0 + sc_ref[...]) + sh_ref[...]).astype(bf16)

    row = pl.BlockSpec((tr, D), lambda i: (i, 0))
    vec = pl.BlockSpec((1, D), lambda i: (0, 0))
    return pl.pallas_call(body, name="prenorm", out_shape=jax.ShapeDtypeStruct((S, D), bf16), grid=(S // tr,),
                          in_specs=[row, vec, vec, vec, ANY_SPEC], out_specs=row, compiler_params=_params("parallel"))(
                              x, scale, shift, g_pre, dep)


def _ext_rows(i, tr, S):
    g = lax.broadcasted_iota(jnp.int32, (tr + 16, 1), 0) + (i * tr - 8)
    return (g >= 0) & (g < S)


def _halo_specs(tr, S, C, col):
    nb8 = S // 8
    main = pl.BlockSpec((tr, C), lambda i: (i, col))
    prev = pl.BlockSpec((8, C), lambda i: (jnp.maximum(i * (tr // 8) - 1, 0), col))
    nxt = pl.BlockSpec((8, C), lambda i: (jnp.minimum((i + 1) * (tr // 8), nb8 - 1), col))
    return prev, main, nxt


def _conv_fwd(proj, conv_w, conv_b, g_conv, tr=512):
    S, C = proj.shape[0], proj.shape[1] // 8
    tr = min(tr, S)

    def body(up, um, un, cp, cm, cn, bg_ref, zc_ref, w_ref, cb_ref, g_ref, o_ref):
        i = pl.program_id(0)
        exists = _ext_rows(i, tr, S)
        u = jnp.concatenate([up[...], um[...], un[...]], axis=0)
        cg = jnp.concatenate([cp[...], cm[...], cn[...]], axis=0)
        t = jnp.where(exists, cg * u, 0.0)
        t_before = pltpu.roll(t, 1, 0)[8:tr + 8]
        t_after = pltpu.roll(t, tr + 15, 0)[8:tr + 8]
        w = w_ref[...]
        cv = w[0:1] * t_before + w[1:2] * t[8:tr + 8] + w[2:3] * t_after + cb_ref[...]
        yc = bg_ref[...] * cv
        rc = lax.rsqrt(jnp.mean(yc * yc, axis=-1, keepdims=True) + EPS)
        o_ref[...] = ((yc * rc) * g_ref[...] * _silu(zc_ref[...])).astype(bf16)

    u_specs = _halo_specs(tr, S, C, 0)
    c_specs = _halo_specs(tr, S, C, 2)
    vec = pl.BlockSpec((1, C), lambda i: (0, 0))
    return pl.pallas_call(
        body, name="conv_fwd", out_shape=jax.ShapeDtypeStruct((S, 2 * C), bf16), grid=(S // tr,),
        in_specs=[*u_specs, *c_specs, pl.BlockSpec((tr, C), lambda i: (i, 1)), pl.BlockSpec((tr, C), lambda i: (i, 3)),
                  pl.BlockSpec((8, C), lambda i: (0, 0)), vec, vec],
        out_specs=pl.BlockSpec((tr, C), lambda i: (i, 0)), compiler_params=_params("parallel"),
    )(proj, proj, proj, proj, proj, proj, proj, proj, conv_w, conv_b, g_conv)


def _branch_geometry(S, r, inter):
    L = S // r * inter
    nq = min(128, L)
    nk = min(nq + 2 * HALF_WIN * inter, L)
    assert L % nq == 0 and (L == nk or L >= nq + 2 * HALF_WIN * inter)
    return L, nq, nk, L // nq


QUAD = 4


def _to_quad(dst, src, S):
    n = S // QUAD
    for rho in range(QUAD):
        dst[pl.ds(rho * n, n), :] = src[pl.ds(rho, n, stride=QUAD), :]


def _block_rows(idx, r, inter, S, L, nq, nk, nblk):
    rho, qb = (0, idx) if r == 1 else (idx // nblk, idx % nblk)
    i0 = qb * nq
    ws = jnp.clip(i0 - HALF_WIN * inter, 0, L - nk)
    if r == 1:
        return pl.ds(pl.multiple_of(i0, 8), nq), pl.ds(pl.multiple_of(ws, 8), nk), i0 - ws
    assert r % (QUAD * inter) == 0
    step = r // QUAD // inter
    base = (rho % QUAD) * (S // QUAD) + rho // QUAD
    if step == 1:
        return pl.ds(pl.multiple_of(base + i0, 8), nq), pl.ds(pl.multiple_of(base + ws, 8), nk), i0 - ws
    return pl.ds(base + step * i0, nq, stride=step), pl.ds(base + step * ws, nk, stride=step), i0 - ws


N_CASES = 3
SCALE = HEAD_DIM ** -0.5
ATTN_UNROLL = 16


def _bias_shape(S):
    shapes = [_branch_geometry(S, r, inter)[1:3] for _, r, inter in BRANCHES]
    return (len(BRANCHES) * N_CASES * 2, max(nq for nq, _ in shapes), max(nk for _, nk in shapes))


def _bias_index(b, case, head):
    return (b * N_CASES + case) * 2 + head


def _fill_bias(bias_scr, sl_ref, S):
    sl = sl_ref[...]
    slope = (sl[0:1, 0:1], sl[0:1, HEAD_DIM:HEAD_DIM + 1])
    for b, (_, r, inter) in enumerate(BRANCHES):
        L, nq, nk, nblk = _branch_geometry(S, r, inter)
        rel = lax.broadcasted_iota(jnp.int32, (nq, nk), 0) - lax.broadcasted_iota(jnp.int32, (nq, nk), 1)
        for case in range(N_CASES):
            d = jnp.abs(rel + case * HALF_WIN)
            valid = d <= HALF_WIN * inter
            if inter > 1:
                valid = valid & (jnp.bitwise_and(d, inter - 1) == 0)
            dist = d.astype(f32) * float(r // inter)
            for head in range(2):
                bias_scr[_bias_index(b, case, head), 0:nq, 0:nk] = jnp.where(valid, -slope[head] * dist, NEG_INF)


def _bias_tiles(slopes, S, dep):
    npair = slopes.shape[0]
    shape = _bias_shape(S)

    def body(sl_ref, dep_ref, o_ref):
        _fill_bias(o_ref, sl_ref, S)

    return pl.pallas_call(
        body, name="bias_tiles", out_shape=jax.ShapeDtypeStruct((npair,) + shape, f32), grid=(npair,),
        in_specs=[pl.BlockSpec((None, 8, PAIR), lambda p: (p, 0, 0)), ANY_SPEC],
        out_specs=pl.BlockSpec((None,) + shape, lambda p: (p, 0, 0, 0)), compiler_params=_params("parallel"),
    )(slopes, dep)


def _head_slopes(n_heads):
    slopes = 2.0 ** (-8.0 * jnp.arange(1, n_heads + 1, dtype=f32) / n_heads)
    return jnp.broadcast_to(jnp.repeat(slopes.reshape(n_heads // 2, 2), HEAD_DIM, axis=1)[:, None, :],
                            (n_heads // 2, 8, PAIR))


def _attn_fwd(proj, bias):
    S, C = proj.shape[0], proj.shape[1] // 8
    npair = C // PAIR

    def body(q_ref, k_ref, v_ref, bias_scr, o_ref, lse_ref, m_scr, l_scr, a_scr, q4_scr, k4_scr, v4_scr):
        lane = lax.broadcasted_iota(jnp.int32, (1, PAIR), 1)
        first = lane < HEAD_DIM
        for dst, src in ((q4_scr, q_ref), (k4_scr, k_ref), (v4_scr, v_ref)):
            _to_quad(dst, src, S)

        for b, (_, r, inter) in enumerate(BRANCHES):
            L, nq, nk, nblk = _branch_geometry(S, r, inter)
            qs, ks, vs = (q_ref, k_ref, v_ref) if r == 1 else (q4_scr, k4_scr, v4_scr)

            def step(idx, carry, b=b, r=r, L=L, nq=nq, nk=nk, nblk=nblk, qs=qs, ks=ks, vs=vs):
                qrows, krows, off = _block_rows(idx, r, inter, S, L, nq, nk, nblk)
                case = off // HALF_WIN
                q2 = qs[qrows, :] * SCALE
                k2 = ks[krows, :].astype(bf16)
                v2 = vs[krows, :].astype(bf16)
                ms, accs = [], []
                for hh in range(2):
                    mine = first if hh == 0 else ~first
                    qh = jnp.where(mine, q2, 0.0).astype(bf16)
                    s = lax.dot_general(qh, k2, (((1,), (1,)), ((), ())), preferred_element_type=f32)
                    s = s + bias_scr[_bias_index(b, case, hh), 0:nq, 0:nk]
                    m = jnp.max(s, axis=-1, keepdims=True)
                    p = jnp.exp(s - m).astype(bf16)
                    vh = jnp.where(mine, v2, jnp.ones_like(v2))
                    ms.append(m)
                    accs.append(jnp.dot(p, vh, preferred_element_type=f32))
                m_scr[b, qrows, :] = jnp.where(first, ms[0], ms[1])
                a_scr[b, qrows, :] = jnp.where(first, accs[0], accs[1])
                l_scr[b, qrows, :] = jnp.where(first, accs[1], accs[0])
                return carry

            lax.fori_loop(0, S // nq, step, 0, unroll=min(ATTN_UNROLL, S // nq))

        n4 = S // QUAD
        ch = min(256, n4)
        nch = n4 // ch

        def merge(i, carry):
            rho, part = i // nch, i % nch
            sorted_rows = pl.ds(pl.multiple_of(rho * n4 + part * ch, 8), ch)
            token_rows = pl.ds(rho + QUAD * part * ch, ch, stride=QUAD)
            rows = (token_rows,) + (sorted_rows,) * (len(BRANCHES) - 1)
            ms = [m_scr[b, rows[b], :] for b in range(len(BRANCHES))]
            m = functools.reduce(jnp.maximum, ms)
            l = jnp.zeros((ch, PAIR), f32)
            acc = jnp.zeros((ch, PAIR), f32)
            for b in range(len(BRANCHES)):
                w = jnp.exp(ms[b] - m)
                l = l + w * pltpu.roll(l_scr[b, rows[b], :], HEAD_DIM, 1)
                acc = acc + w * a_scr[b, rows[b], :]
            o_ref[token_rows, :] = acc / l
            lse_ref[token_rows, :] = m + jnp.log(l)
            return carry

        lax.fori_loop(0, QUAD * nch, merge, 0, unroll=2)

    blk = lambda part: pl.BlockSpec((S, PAIR), lambda p: (0, part * npair + p))
    out = pl.BlockSpec((S, PAIR), lambda p: (0, p))
    return pl.pallas_call(
        body, name="attn_fwd",
        out_shape=(jax.ShapeDtypeStruct((S, C), f32), jax.ShapeDtypeStruct((S, C), f32)), grid=(npair,),
        in_specs=[blk(4), blk(5), blk(6), pl.BlockSpec((None,) + _bias_shape(S), lambda p: (p, 0, 0, 0))],
        out_specs=(out, out),
        scratch_shapes=[pltpu.VMEM((3, S, PAIR), f32)] * 3 + [pltpu.VMEM((S, PAIR), f32)] * 3,
        compiler_params=_params("parallel"),
    )(proj, proj, proj, bias)


def _attn_post(ycat, o, proj, g_attn, tr=512):
    S, C = o.shape
    tr = min(tr, S)

    def body(y_ref, o_ref, z_ref, g_ref, out_ref):
        del y_ref
        ov = o_ref[...]
        ra = lax.rsqrt(jnp.mean(ov * ov, axis=-1, keepdims=True) + EPS)
        out_ref[...] = ((ov * ra) * g_ref[...] * _silu(z_ref[...])).astype(bf16)

    return pl.pallas_call(
        body, name="attn_post", out_shape=jax.ShapeDtypeStruct(ycat.shape, ycat.dtype), grid=(S // tr,),
        in_specs=[HBM_SPEC, pl.BlockSpec((tr, C), lambda i: (i, 0)), pl.BlockSpec((tr, C), lambda i: (i, 7)),
                  pl.BlockSpec((1, C), lambda i: (0, 0))],
        out_specs=pl.BlockSpec((tr, C), lambda i: (i, 1)), input_output_aliases={0: 0},
        compiler_params=_params("arbitrary"),
    )(ycat, o, proj, g_attn)


def _residual_minus_target(x, target, dep, tr=512):
    S, D = x.shape
    tr = min(tr, S)

    def body(x_ref, t_ref, dep_ref, o_ref):
        o_ref[...] = x_ref[...] - t_ref[...]

    row = pl.BlockSpec((tr, D), lambda i: (i, 0))
    return pl.pallas_call(body, name="residual_minus_target", out_shape=jax.ShapeDtypeStruct((S, D), f32),
                          grid=(S // tr,), in_specs=[row, row, ANY_SPEC], out_specs=row,
                          compiler_params=_params("parallel"))(x, target, dep)


def _sandwich(y, x_minus_t, gate, g_post, tr=256):
    S, D = y.shape
    tr = min(tr, S)

    def body(y_ref, xt_ref, gate_ref, g_ref, dy_ref, dout_ref, sums_ref):
        i = pl.program_id(0)
        gate, g = gate_ref[...], g_ref[...]
        gg = gate * g
        yv = y_ref[...]
        rp = lax.rsqrt(jnp.mean(yv * yv, axis=-1, keepdims=True) + EPS)
        yhat = yv * rp
        err = xt_ref[...] + gg * yhat
        dout = err * (1.0 / D)
        dout_ref[...] = dout
        q = dout * yhat
        w = dout * gg
        dy_ref[...] = (rp * (w - yhat * jnp.sum(q * gg, axis=-1, keepdims=True) * (1.0 / D))).astype(bf16)
        loss = 0.5 * jnp.sum(jnp.mean(err * err, axis=-1, keepdims=True), axis=0, keepdims=True)
        q_sum = jnp.sum(q, axis=0, keepdims=True)
        row = lax.broadcasted_iota(jnp.int32, (8, D), 0)
        upd = jnp.where(row == 0, q_sum * g, jnp.where(row == 1, q_sum * gate, jnp.where(row == 2, loss, 0.0)))

        @pl.when(i == 0)
        def _():
            sums_ref[...] = upd

        @pl.when(i > 0)
        def _():
            sums_ref[...] += upd

    row = pl.BlockSpec((tr, D), lambda i: (i, 0))
    vec = pl.BlockSpec((1, D), lambda i: (0, 0))
    return pl.pallas_call(
        body, name="sandwich",
        out_shape=(jax.ShapeDtypeStruct((S, D), bf16), jax.ShapeDtypeStruct((S, D), f32), jax.ShapeDtypeStruct((8, D), f32)),
        grid=(S // tr,), in_specs=[row, row, vec, vec],
        out_specs=(row, row, pl.BlockSpec((8, D), lambda i: (0, 0))), compiler_params=_params("arbitrary"),
    )(y, x_minus_t, gate, g_post)


def _conv_bwd(proj, dycat, conv_w, conv_b, g_conv, dep, tr=256):
    S, C = proj.shape[0], proj.shape[1] // 8
    tr = min(tr, S)
    n = tr + 16

    def body(*refs):
        ins, (w_ref, cb_ref, g_ref, _, dp_ref, sums_ref) = refs[:15], refs[15:]
        i = pl.program_id(0)
        exists = _ext_rows(i, tr, S)
        u, bg, cg, zc, dyn = (jnp.concatenate([ins[3 * t][...], ins[3 * t + 1][...], ins[3 * t + 2][...]], axis=0)
                              for t in range(5))
        w = w_ref[...]
        t = jnp.where(exists, cg * u, 0.0)
        t_before, t_after = pltpu.roll(t, 1, 0), pltpu.roll(t, n - 1, 0)
        cv = w[0:1] * t_before + w[1:2] * t + w[2:3] * t_after + cb_ref[...]
        yc = bg * cv
        rc = lax.rsqrt(jnp.mean(yc * yc, axis=-1, keepdims=True) + EPS)
        yhat = yc * rc
        sz, dsz = _silu_and_slope(zc)
        wgt = dyn * g_ref[...] * sz
        dyc = rc * (wgt - yhat * jnp.mean(wgt * yhat, axis=-1, keepdims=True))
        dcv = jnp.where(exists, dyc * bg, 0.0)
        dt = w[0:1] * pltpu.roll(dcv, n - 1, 0) + w[1:2] * dcv + w[2:3] * pltpu.roll(dcv, 1, 0)
        mid = slice(8, tr + 8)
        dp_ref[:, 0:C] = (dt * cg)[mid].astype(bf16)
        dp_ref[:, C:2 * C] = (dyc * cv)[mid].astype(bf16)
        dp_ref[:, 2 * C:3 * C] = (dt * u)[mid].astype(bf16)
        dp_ref[:, 3 * C:4 * C] = (dyn * yhat * g_ref[...] * dsz)[mid].astype(bf16)
        colsum = lambda v: jnp.sum(v[mid], axis=0, keepdims=True)
        parts = [colsum(dyn * yhat * sz), colsum(dcv), colsum(dcv * t_before), colsum(dcv * t), colsum(dcv * t_after)]
        row = lax.broadcasted_iota(jnp.int32, (8, C), 0)
        upd = jnp.zeros((8, C), f32)
        for j, pj in enumerate(parts):
            upd = jnp.where(row == j, pj, upd)

        @pl.when(i == 0)
        def _():
            sums_ref[...] = upd

        @pl.when(i > 0)
        def _():
            sums_ref[...] += upd

    specs = []
    for col in range(4):
        specs += _halo_specs(tr, S, C, col)
    specs += _halo_specs(tr, S, C, 0)
    vec = pl.BlockSpec((1, C), lambda i: (0, 0))
    return pl.pallas_call(
        body, name="conv_bwd",
        out_shape=(jax.ShapeDtypeStruct((S, 4 * C), bf16), jax.ShapeDtypeStruct((8, C), f32)), grid=(S // tr,),
        in_specs=[*specs, pl.BlockSpec((8, C), lambda i: (0, 0)), vec, vec, ANY_SPEC],
        out_specs=(pl.BlockSpec((tr, 4 * C), lambda i: (i, 0)), pl.BlockSpec((8, C), lambda i: (0, 0))),
        compiler_params=_params("arbitrary"),
    )(*([proj] * 12), dycat, dycat, dycat, conv_w, conv_b, g_conv, dep)


def _attn_post_bwd(o, proj, dycat, g_attn, dep, tr=512):
    S, C = o.shape
    tr = min(tr, S)

    def body(o_ref, z_ref, dy_ref, g_ref, dep_ref, do_ref, dz_ref, sums_ref):
        i = pl.program_id(0)
        ov, zv, dyn = o_ref[...], z_ref[...], dy_ref[...]
        ra = lax.rsqrt(jnp.mean(ov * ov, axis=-1, keepdims=True) + EPS)
        ohat = ov * ra
        sz, dsz = _silu_and_slope(zv)
        wgt = dyn * g_ref[...] * sz
        do_ref[...] = ra * (wgt - ohat * jnp.mean(wgt * ohat, axis=-1, keepdims=True))
        dz_ref[...] = (dyn * ohat * g_ref[...] * dsz).astype(bf16)
        row = lax.broadcasted_iota(jnp.int32, (8, C), 0)
        upd = jnp.where(row == 0, jnp.sum(dyn * ohat * sz, axis=0, keepdims=True), 0.0)

        @pl.when(i == 0)
        def _():
            sums_ref[...] = upd

        @pl.when(i > 0)
        def _():
            sums_ref[...] += upd

    return pl.pallas_call(
        body, name="attn_post_bwd",
        out_shape=(jax.ShapeDtypeStruct((S, C), f32), jax.ShapeDtypeStruct((4, S, C), bf16),
                   jax.ShapeDtypeStruct((8, C), f32)),
        grid=(S // tr,),
        in_specs=[pl.BlockSpec((tr, C), lambda i: (i, 0)), pl.BlockSpec((tr, C), lambda i: (i, 7)),
                  pl.BlockSpec((tr, C), lambda i: (i, 1)), pl.BlockSpec((1, C), lambda i: (0, 0)), ANY_SPEC],
        out_specs=(pl.BlockSpec((tr, C), lambda i: (i, 0)), pl.BlockSpec((None, tr, C), lambda i: (3, i, 0)),
                   pl.BlockSpec((8, C), lambda i: (0, 0))),
        compiler_params=_params("arbitrary"),
    )(o, proj, dycat, g_attn, dep)


def _attn_bwd(proj, o, do, lse, bias, dqkvz, dep):
    S, C = o.shape
    npair = C // PAIR

    def body(q_ref, k_ref, v_ref, o_ref, do_ref, lse_ref, bias_scr, old_ref, dep_ref, dqkv_ref,
             acc_scr, dl_scr, quad_scr):
        lane = lax.broadcasted_iota(jnp.int32, (1, PAIR), 1)
        first = lane < HEAD_DIM
        ch = min(256, S)

        def prep(i, carry):
            rows = pl.ds(pl.multiple_of(i * ch, 8), ch)
            prod = do_ref[rows, :] * o_ref[rows, :]
            d0 = jnp.sum(jnp.where(first, prod, 0.0), axis=-1, keepdims=True)
            d1 = jnp.sum(jnp.where(first, 0.0, prod), axis=-1, keepdims=True)
            dl_scr[rows, :] = jnp.where(first, d0, d1)
            zero = jnp.zeros((ch, PAIR), f32)
            for order in range(2):
                for t in range(3):
                    acc_scr[order, t, rows, :] = zero
            return carry

        lax.fori_loop(0, S // ch, prep, 0, unroll=2)
        token_srcs = (q_ref, k_ref, v_ref, do_ref, lse_ref, dl_scr)
        for j, src in enumerate(token_srcs):
            _to_quad(quad_scr.at[j], src, S)

        for b, (_, r, inter) in enumerate(BRANCHES):
            L, nq, nk, nblk = _branch_geometry(S, r, inter)
            order = 0 if r == 1 else 1
            srcs = token_srcs if r == 1 else tuple(quad_scr.at[j] for j in range(6))

            def step(idx, carry, b=b, r=r, L=L, nq=nq, nk=nk, nblk=nblk, order=order, srcs=srcs):
                qs, ks, vs, dos, lses, dls = srcs
                dq_scr, dk_scr, dv_scr = (acc_scr.at[order, t] for t in range(3))
                qrows, krows, off = _block_rows(idx, r, inter, S, L, nq, nk, nblk)
                case = off // HALF_WIN
                q2 = qs[qrows, :] * SCALE
                k2 = ks[krows, :].astype(bf16)
                v2 = vs[krows, :].astype(bf16)
                do2 = dos[qrows, :]
                lse2 = lses[qrows, :]
                dl2 = dls[qrows, :]
                dq2 = jnp.zeros((nq, PAIR), f32)
                dk2 = jnp.zeros((nk, PAIR), f32)
                dv2 = jnp.zeros((nk, PAIR), f32)
                for hh in range(2):
                    mine = first if hh == 0 else ~first
                    lo = hh * HEAD_DIM
                    qh = jnp.where(mine, q2, 0.0).astype(bf16)
                    doh = jnp.where(mine, do2, 0.0).astype(bf16)
                    s = lax.dot_general(qh, k2, (((1,), (1,)), ((), ())), preferred_element_type=f32)
                    s = s + bias_scr[_bias_index(b, case, hh), 0:nq, 0:nk]
                    p = jnp.exp(s - lse2[:, lo:lo + 1])
                    dv2 = dv2 + lax.dot_general(p.astype(bf16), doh, (((0,), (0,)), ((), ())), preferred_element_type=f32)
                    dp = lax.dot_general(doh, v2, (((1,), (1,)), ((), ())), preferred_element_type=f32)
                    ds = (p * (dp - dl2[:, lo:lo + 1])).astype(bf16)
                    dq2 = dq2 + jnp.where(mine, jnp.dot(ds, k2, preferred_element_type=f32), 0.0)
                    dk2 = dk2 + lax.dot_general(ds, qh, (((0,), (0,)), ((), ())), preferred_element_type=f32)
                dq_scr[qrows, :] = dq_scr[qrows, :] + dq2
                dk_scr[krows, :] = dk_scr[krows, :] + dk2
                dv_scr[krows, :] = dv_scr[krows, :] + dv2
                return carry

            lax.fori_loop(0, S // nq, step, 0, unroll=min(ATTN_UNROLL, S // nq))

        n4 = S // QUAD
        for t in range(3):
            for rho in range(QUAD):
                token_rows = pl.ds(rho, n4, stride=QUAD)
                acc_scr[0, t, token_rows, :] = acc_scr[0, t, token_rows, :] + acc_scr[1, t, pl.ds(rho * n4, n4), :]
        dqkv_ref[0] = (acc_scr[0, 0] * SCALE).astype(bf16)
        dqkv_ref[1] = acc_scr[0, 1].astype(bf16)
        dqkv_ref[2] = acc_scr[0, 2].astype(bf16)

    blk = lambda part: pl.BlockSpec((S, PAIR), lambda p: (0, part * npair + p))
    own = pl.BlockSpec((S, PAIR), lambda p: (0, p))
    return pl.pallas_call(
        body, name="attn_bwd", out_shape=jax.ShapeDtypeStruct(dqkvz.shape, dqkvz.dtype), grid=(npair,),
        in_specs=[blk(4), blk(5), blk(6), own, own, own,
                  pl.BlockSpec((None,) + _bias_shape(S), lambda p: (p, 0, 0, 0)), ANY_SPEC, ANY_SPEC],
        out_specs=pl.BlockSpec((3, S, PAIR), lambda p: (0, 0, p)), input_output_aliases={7: 0},
        scratch_shapes=[pltpu.VMEM((2, 3, S, PAIR), f32), pltpu.VMEM((S, PAIR), f32), pltpu.VMEM((6, S, PAIR), f32)],
        compiler_params=_params("arbitrary"),
    )(proj, proj, proj, o, do, lse, bias, dqkvz, dep)


def _prenorm_bwd(dh, x, dout, scale, g_pre, tr=256):
    S, D = x.shape
    tr = min(tr, S)

    def body(dh_ref, x_ref, dout_ref, sc_ref, g_ref, gx_ref, sums_ref):
        i = pl.program_id(0)
        xv, dhv = x_ref[...], dh_ref[...]
        r = lax.rsqrt(jnp.mean(xv * xv, axis=-1, keepdims=True) + EPS)
        xn = xv * r
        dxn = dhv * (g_ref[...] * (1.0 + sc_ref[...]))
        gx_ref[...] = dout_ref[...] + r * (dxn - xn * jnp.mean(dxn * xn, axis=-1, keepdims=True))
        dhx = dhv * xn
        row = lax.broadcasted_iota(jnp.int32, (8, D), 0)
        upd = jnp.where(row == 0, jnp.sum(dhv, axis=0, keepdims=True),
                        jnp.where(row == 1, jnp.sum(dhx, axis=0, keepdims=True) * g_ref[...],
                                  jnp.where(row == 2, jnp.sum(dhx, axis=0, keepdims=True) * (1.0 + sc_ref[...]), 0.0)))

        @pl.when(i == 0)
        def _():
            sums_ref[...] = upd

        @pl.when(i > 0)
        def _():
            sums_ref[...] += upd

    row = pl.BlockSpec((tr, D), lambda i: (i, 0))
    vec = pl.BlockSpec((1, D), lambda i: (0, 0))
    return pl.pallas_call(
        body, name="prenorm_bwd",
        out_shape=(jax.ShapeDtypeStruct((S, D), f32), jax.ShapeDtypeStruct((8, D), f32)), grid=(S // tr,),
        in_specs=[row, row, row, vec, vec], out_specs=(row, pl.BlockSpec((8, D), lambda i: (0, 0))),
        compiler_params=_params("arbitrary"),
    )(dh, x, dout, scale, g_pre)


def _adamw(w, g, m, v):
    m = ADAM_B1 * m + (1.0 - ADAM_B1) * g
    v = ADAM_B2 * v + (1.0 - ADAM_B2) * (g * g)
    m_hat = m / (1.0 - ADAM_B1 ** ADAM_STEP)
    v_hat = v / (1.0 - ADAM_B2 ** ADAM_STEP)
    delta = -ADAM_LR * (m_hat / (jnp.sqrt(v_hat) + ADAM_EPS) + ADAM_WD * w)
    return delta, m, v


def _sum_rows(parts, dep):
    P = parts.shape[1]

    def body(p_ref, dep_ref, o_ref):
        acc = p_ref[0:1, :]
        for j in range(1, NDEV):
            acc = acc + p_ref[j:j + 1, :]
        o_ref[...] = jnp.broadcast_to(acc, (8, P))

    vmem = pl.BlockSpec(memory_space=pltpu.VMEM)
    return pl.pallas_call(body, name="sum_small", out_shape=jax.ShapeDtypeStruct((8, P), f32),
                          in_specs=[vmem, ANY_SPEC], out_specs=vmem, compiler_params=_params())(parts, dep)


def _adamw_small(tot, params):
    given = [p[3] for p in params if not isinstance(p[3], int)]

    def body(tot_ref, *refs):
        given_refs = list(refs[:len(given)])
        ins = refs[len(given):len(given) + 3 * len(params)]
        outs = refs[len(given) + 3 * len(params):]
        for t, (w, _, _, where) in enumerate(params):
            w_ref, m_ref, v_ref = ins[3 * t:3 * t + 3]
            g = tot_ref[0:1, where:where + w.size] if isinstance(where, int) else given_refs.pop(0)[...]
            outs[4 * t][...] = g
            outs[4 * t + 1][...], outs[4 * t + 2][...], outs[4 * t + 3][...] = _adamw(w_ref[...], g, m_ref[...], v_ref[...])

    out_shape = tuple(jax.ShapeDtypeStruct(p[0].shape, f32) for p in params for _ in range(4))
    res = pl.pallas_call(body, name="adamw_small", out_shape=out_shape, compiler_params=_params())(
        tot, *given, *[a for p in params for a in p[:3]])
    return [res[4 * t:4 * t + 4] for t in range(len(params))]


def _adamw_sharded(name, parts, sums_a, sums_b, pick, w, m, v, rows=None, prev=None, tr=128):
    R, Cc = w.shape
    r0, nr = rows or (0, R)
    tr = math.gcd(tr, r0, nr)
    n, b0 = parts.shape[0], r0 // tr

    def body(pick_ref, p_ref, a_ref, b_ref, w_ref, m_ref, v_ref, *rest):
        g_ref, d_ref, nm_ref, nv_ref = rest[-4:]
        g = jnp.where(pick_ref[0] == 1, b_ref[...], a_ref[...]).astype(f32)
        for j in range(n):
            g = g + p_ref[j].astype(f32)
        g_ref[...] = g
        d_ref[...], nm_ref[...], nv_ref[...] = _adamw(w_ref[...], g, m_ref[...], v_ref[...])

    row = pl.BlockSpec((tr, Cc), lambda i, pick: (i + b0, 0))
    mine = pl.BlockSpec((None, tr, Cc), lambda i, pick: (pick[1], i + b0, 0))
    out = jax.ShapeDtypeStruct((R, Cc), f32)
    prev = list(prev or [])
    grid_spec = pltpu.PrefetchScalarGridSpec(
        num_scalar_prefetch=1, grid=(nr // tr,),
        in_specs=[pl.BlockSpec((n, tr, Cc), lambda i, pick: (0, i + b0, 0)), mine, mine, row, row, row]
        + [ANY_SPEC] * len(prev),
        out_specs=(row, row, row, row))
    return pl.pallas_call(
        body, name=name, out_shape=(out, out, out, out), grid_spec=grid_spec,
        input_output_aliases={7 + t: t for t in range(len(prev))}, compiler_params=_params("arbitrary"),
    )(pick, parts, sums_a, sums_b, w, m, v, *prev)


def _adamw_ada(c_t, dmod_cols, w, m, v, dep, tr=512):
    D, W = w.shape
    tr = min(tr, D)

    def body(c_ref, dm_ref, w_ref, m_ref, v_ref, dep_ref, g_ref, d_ref, nm_ref, nv_ref):
        g = lax.dot_general(c_ref[...], dm_ref[...], (((1,), (0,)), ((), ())), preferred_element_type=f32,
                            precision=lax.Precision.HIGHEST)
        g_ref[...] = g
        d_ref[...], nm_ref[...], nv_ref[...] = _adamw(w_ref[...], g, m_ref[...], v_ref[...])

    row = pl.BlockSpec((tr, W), lambda i: (i, 0))
    out = jax.ShapeDtypeStruct((D, W), f32)
    return pl.pallas_call(
        body, name="adamw_ada", out_shape=(out, out, out, out), grid=(D // tr,),
        in_specs=[pl.BlockSpec((tr, NDEV), lambda i: (i, 0)), pl.BlockSpec((NDEV, W), lambda i: (0, 0)), row, row, row,
                  ANY_SPEC],
        out_specs=(row, row, row, row), compiler_params=_params("parallel"),
    )(c_t, dmod_cols, w, m, v, dep)


def kernel(x, c, w_ada, b_ada, g_pre, w_in, conv_w, conv_b, g_conv, g_attn, w_out, g_post, loss_target, m_w_ada, m_b_ada, m_g_pre, m_w_in, m_conv_w, m_conv_b, m_g_conv, m_g_attn, m_w_out, m_g_post, v_w_ada, v_b_ada, v_g_pre, v_w_in, v_conv_w, v_conv_b, v_g_conv, v_g_attn, v_w_out, v_g_post):
    S, D = x.shape[1], x.shape[2]
    C = D // 2
    W = w_ada.shape[2]
    CW = conv_w.shape[2]
    me = 4 * lax.axis_index("x") + 2 * lax.axis_index("y") + lax.axis_index("c")
    x2, tgt = x[0], loss_target[0]
    w_ada2, w_in2, w_out2 = w_ada[0], w_in[0], w_out[0]

    R = D // NDEV
    core = lax.axis_index("c").astype(jnp.int32).reshape(1)

    cw_slab = jnp.zeros((8, CW), f32).at[:3].set(conv_w[0])
    b_cols = lax.dynamic_slice_in_dim(b_ada, me * W, W, axis=1)
    mod_slabs, c_blocks, cw_g = _ada_exchange(c.reshape(D // 128, 128), cw_slab, w_ada2, b_cols)
    c_all = c_blocks.reshape(NDEV, D)
    conv_w_full = jnp.transpose(cw_g, (1, 0, 2)).reshape(8, C)
    mod = mod_slabs[:, 0, :].reshape(1, 3 * D)
    shift, scale, gate = mod[:, :D], mod[:, D:2 * D], mod[:, 2 * D:]

    land_i = lax.dynamic_update_slice(lax.empty((NDEV, D, C), bf16), w_in2.astype(bf16)[None], (me, 0, 0))
    land_o = lax.dynamic_update_slice(lax.empty((NDEV, R, D), bf16), w_out2.astype(bf16)[None], (me, 0, 0))
    wi_send, wi_recv, land_i, w_token = _w_in_start(land_i, [mod_slabs])

    me_arr = me.astype(jnp.int32).reshape(1)
    h = _prenorm(x2, scale, shift, g_pre, w_token)
    land_i = _w_in_sibling(land_i, wi_recv, after=[h])
    proj = _in_proj_part("in_proj_a", h, land_i, None, me_arr, 0, 1, 2)
    x_minus_t = _residual_minus_target(x2, tgt, proj)
    bias = _bias_tiles(_head_slopes(C // HEAD_DIM), S, x_minus_t)

    def landing(rows, cols):
        return lax.dynamic_update_slice(lax.empty((NCHIP, rows, cols), bf16), jnp.zeros((1, rows, cols), bf16),
                                        (me // 2, 0, 0))

    land_go, land_gi = landing(R, D), landing(D, C)
    fi_send, fi_recv, land_i = _w_in_relay(land_i, wi_recv, after=[proj, bias, land_go, land_gi])
    proj = _in_proj_part("in_proj_b", h, land_i, proj, me_arr, 2, 2, 2)
    land_i = _w_in_forwarded(land_i, fi_recv, after=[proj])
    proj = _in_proj_part("in_proj_c", h, land_i, proj, me_arr, 3, 2, 2)
    (di_send, di_recv, wo_send, wo_recv), land_i, land_o = _w_in_diag(land_i, land_o, fi_recv, after=[proj])
    proj = _in_proj_part("in_proj_d", h, land_i, proj, me_arr, 6, 1, 1)
    win_g = _w_in_finish(land_i, wi_send, fi_send, di_send, di_recv, after=[proj])
    proj = _in_proj_part("in_proj_e", h, win_g, proj, me_arr, 7, 1, 1)
    ycat = _conv_fwd(proj, conv_w_full, conv_b, g_conv)
    o, lse = _attn_fwd(proj, bias)
    (fo_send, fo_recv), (land_o,), _ = _weights_forward("w_out_forward", land_o, wo_recv, after=[o])
    ycat = _attn_post(ycat, o, proj, g_attn)
    wout_g = _weights_wait("w_out_wait", land_o, wo_send, wo_recv, fo_send, fo_recv, after=[ycat])
    wout_full = wout_g.reshape(D, D)
    y = _matmul(ycat, wout_full, name="out_proj", out_dtype=f32)
    dy, dout, post_sums = _sandwich(y, x_minus_t, gate, g_post)

    gw_out = _matmul(ycat, dy, name="out_proj_dw", out_dtype=bf16, ta=True).reshape(NDEV, R, D)
    dycat = _matmul(dy, wout_full, name="out_proj_dx", out_dtype=f32, tb=True)
    dpc, conv_sums = _conv_bwd(proj, dycat, conv_w_full, conv_b, g_conv, gw_out)
    gw_c = _matmul(h, dpc, name="in_proj_dw_conv", out_dtype=bf16, ta=True, out_slots=4)
    first_pairs, p1_token = _pairs_start("g_first_pair_start", [gw_out, gw_c])
    do, dpa, attn_sums = _attn_post_bwd(o, proj, dycat, g_attn, p1_token)
    (gw_out, pair_o), (gw_c, pair_c) = _pairs_wait("g_first_pair_wait", first_pairs, after=[do])
    sum_o = _pair_sum("g_out_pair_sum", gw_out, pair_o, core)
    sum_c = _pair_sum("g_conv_pair_sum", gw_c, pair_c, core)
    ((co_send, co_recv, sum_o, land_go), (cc_send, cc_recv, sum_c, land_gi)), cc_token = _chips_start(
        "g_first_chip_start", [(sum_o, land_go, 0), (sum_c, land_gi, 0)])
    dpa = _attn_bwd(proj, o, do, lse, bias, dpa, cc_token)
    gw_a = _matmul(h, dpa, name="in_proj_dw_attn", out_dtype=bf16, ta=True, b_slots=True, out_slots=4)
    pa_send, pa_recv, gw_a, pair_a, pa_token = _pair_start("g_attn_pair_start", gw_a)
    sum_o, land_go = _chip_wait("g_out_chip_wait", sum_o, land_go, co_send, co_recv, 0, after=[pa_token])
    pick_out = jnp.stack([jnp.int32(0), me // 2]).astype(jnp.int32)
    g_w_out, d_w_out, nm_w_out, nv_w_out = _adamw_sharded(
        "adamw_w_out", land_go, sum_o, sum_o, pick_out, w_out2, m_w_out[0], v_w_out[0])
    gw_a, pair_a = _pair_wait("g_attn_pair_wait", gw_a, pair_a, pa_send, pa_recv, after=[g_w_out])
    sum_a = _pair_sum("g_attn_pair_sum", gw_a, pair_a, core)
    part_a, part_b = (0, 3 * D // 4), (3 * D // 4, D // 4)
    ca_send, ca_recv, sum_a, land_gi, ca_token = _chip_start("g_attn_chip_start_a", sum_a, land_gi, 4, part_a)
    dh = _matmul_slabs_t(dpc, dpa, win_g, name="in_proj_dx", dep=ca_token)
    grad_x, pre_sums = _prenorm_bwd(dh, x2, dout, scale, g_pre)

    small = jnp.concatenate([pre_sums[0:1], pre_sums[1:2], post_sums[0:1],
                             pre_sums[2:3], post_sums[1:2],
                             conv_sums[2:3], conv_sums[3:4], conv_sums[4:5],
                             conv_sums[1:2], conv_sums[0:1], attn_sums[0:1]], axis=1)
    small = jnp.concatenate([small.reshape(8 * D // 128, 128), jnp.broadcast_to(post_sums[2:3, :128], (8, 128))])
    (small_all,) = _all_gather([small], "gather_small")
    cb_send, cb_recv, sum_a, land_gi, cb_token = _chip_start("g_attn_chip_start_b", sum_a, land_gi, 4, part_b,
                                                             after=[small_all])
    small_all = small_all.reshape(NDEV, small.size)
    tot = _sum_rows(small_all, cb_token)
    loss = tot[0, 8 * D]
    g_conv_w = lax.dynamic_slice_in_dim(tot[0:1, 5 * D:5 * D + 3 * C].reshape(1, 3, C), me * CW, CW, axis=2)
    ((g_b_ada, d_b_ada, nm_b_ada, nv_b_ada), (g_g_pre, d_g_pre, nm_g_pre, nv_g_pre),
     (g_g_post, d_g_post, nm_g_post, nv_g_post), (g_conv_w, d_conv_w, nm_conv_w, nv_conv_w),
     (g_conv_b, d_conv_b, nm_conv_b, nv_conv_b), (g_g_conv, d_g_conv, nm_g_conv, nv_g_conv),
     (g_g_attn, d_g_attn, nm_g_attn, nv_g_attn)) = _adamw_small(tot, [
         (b_ada, m_b_ada, v_b_ada, 0), (g_pre, m_g_pre, v_g_pre, 3 * D), (g_post, m_g_post, v_g_post, 4 * D),
         (conv_w, m_conv_w, v_conv_w, g_conv_w), (conv_b, m_conv_b, v_conv_b, 5 * D + 3 * C),
         (g_conv, m_g_conv, v_g_conv, 5 * D + 4 * C), (g_attn, m_g_attn, v_g_attn, 5 * D + 5 * C)])

    dmod_cols = lax.dynamic_slice_in_dim(small_all[:, :3 * D], me * W, W, axis=1)
    g_w_ada, d_w_ada, nm_w_ada, nv_w_ada = _adamw_ada(c_all.T, dmod_cols, w_ada2, m_w_ada[0], v_w_ada[0], cb_token)

    pick_in = jnp.stack([me // 4, (me % 4) // 2]).astype(jnp.int32)
    sum_c, land_gi = _chip_wait("g_conv_chip_wait", sum_c, land_gi, cc_send, cc_recv, 0, after=[g_w_ada])
    sum_a, land_gi = _chip_wait("g_attn_chip_wait_a", sum_a, land_gi, ca_send, ca_recv, 4, [g_w_ada], part_a)
    first = _adamw_sharded("adamw_w_in_a", land_gi, sum_c, sum_a, pick_in, w_in2, m_w_in[0], v_w_in[0], rows=part_a,
                           tr=256)
    sum_a, land_gi = _chip_wait("g_attn_chip_wait_b", sum_a, land_gi, cb_send, cb_recv, 4, [first[0]], part_b)
    g_w_in, d_w_in, nm_w_in, nv_w_in = _adamw_sharded(
        "adamw_w_in_b", land_gi, sum_c, sum_a, pick_in, w_in2, m_w_in[0], v_w_in[0], rows=part_b, prev=first, tr=256)

    return (loss, grad_x[None],
            g_w_ada[None], g_b_ada, g_g_pre, g_w_in[None], g_conv_w, g_conv_b, g_g_conv, g_g_attn, g_w_out[None], g_g_post,
            d_w_ada[None], d_b_ada, d_g_pre, d_w_in[None], d_conv_w, d_conv_b, d_g_conv, d_g_attn, d_w_out[None], d_g_post,
            nm_w_ada[None], nm_b_ada, nm_g_pre, nm_w_in[None], nm_conv_w, nm_conv_b, nm_g_conv, nm_g_attn, nm_w_out[None], nm_g_post,
            nv_w_ada[None], nv_b_ada, nv_g_pre, nv_w_in[None], nv_conv_w, nv_conv_b, nv_g_conv, nv_g_attn, nv_w_out[None], nv_g_post)
```

```python
import functools
import math

import jax
import jax.numpy as jnp
from jax import lax
from jax.experimental import pallas as pl
from jax.experimental.pallas import tpu as pltpu

f32 = jnp.float32
bf16 = jnp.bfloat16

NDEV = 8
HEAD_DIM = 64
PAIR = 2 * HEAD_DIM
BRANCHES = ((128, 1, 1), (512, 4, 1), (2048, 16, 2))
HALF_WIN = 64
EPS = 1e-6
NEG_INF = -1e30
ADAM_LR, ADAM_B1, ADAM_B2, ADAM_EPS, ADAM_WD, ADAM_STEP = 0.001, 0.9, 0.999, 1e-08, 0.01, 10
MESH = pl.DeviceIdType.MESH
VMEM_LIMIT = 56 * 1024 * 1024
HBM_SPEC = pl.BlockSpec(memory_space=pltpu.HBM)
ANY_SPEC = pl.BlockSpec(memory_space=pl.ANY)
SEM_SPEC = pl.BlockSpec(memory_space=pltpu.SEMAPHORE)


def _params(*sem):
    return pltpu.CompilerParams(dimension_semantics=sem or None, vmem_limit_bytes=VMEM_LIMIT)


def _silu(z):
    return z * jax.nn.sigmoid(z)


def _silu_and_slope(z):
    s = jax.nn.sigmoid(z)
    return z * s, s * (1.0 + z * (1.0 - s))


def _my_place():
    x, y, c = lax.axis_index("x"), lax.axis_index("y"), lax.axis_index("c")
    return x, y, c, 4 * x + 2 * y + c


def _peer(x, y, c, k):
    px, py, pc = x ^ (k >> 2 & 1), y ^ (k >> 1 & 1), c ^ (k & 1)
    return (px, py, pc), 4 * px + 2 * py + pc


def _all_gather(arrays, name):
    n = len(arrays)

    def body(*refs):
        srcs, dsts = refs[:n], refs[n:2 * n]
        send_sems, recv_sems, local_sems = refs[2 * n:]
        x, y, c, me = _my_place()
        locals_, sends = [], []
        for t in range(n):
            own = pltpu.make_async_copy(srcs[t], dsts[t].at[me], local_sems.at[t])
            own.start()
            locals_.append(own)
            for k in range(1, NDEV):
                peer, pidx = _peer(x, y, c, k)
                cp = pltpu.make_async_remote_copy(
                    src_ref=srcs[t], dst_ref=dsts[t].at[me], send_sem=send_sems.at[t, k],
                    recv_sem=recv_sems.at[t, k], device_id=peer, device_id_type=MESH)
                cp.start()
                sends.append(cp)
        for t in range(n):
            for k in range(1, NDEV):
                peer, pidx = _peer(x, y, c, k)
                pltpu.make_async_remote_copy(
                    src_ref=srcs[t], dst_ref=dsts[t].at[pidx], send_sem=send_sems.at[t, k],
                    recv_sem=recv_sems.at[t, k], device_id=peer, device_id_type=MESH).wait_recv()
        for cp in sends:
            cp.wait_send()
        for cp in locals_:
            cp.wait()

    return pl.pallas_call(
        body, name=name,
        out_shape=tuple(jax.ShapeDtypeStruct((NDEV,) + a.shape, a.dtype) for a in arrays),
        in_specs=[HBM_SPEC] * n, out_specs=tuple([HBM_SPEC] * n),
        scratch_shapes=[pltpu.SemaphoreType.DMA((n, NDEV)), pltpu.SemaphoreType.DMA((n, NDEV)),
                        pltpu.SemaphoreType.DMA((n,))],
    )(*arrays)


def _comm_call(name, arrays, sems, new_sems, body, after=(), token=False):
    na, ns, nn, nf = len(arrays), len(sems), len(new_sems), len(after)

    def kern(*refs):
        ins, outs = refs[:na + ns + nf], refs[na + ns + nf:]
        body(ins[:na], ins[na:na + ns], outs[:nn])
        if token:
            outs[nn + na][...] = jnp.zeros((8, 128), f32)

    out_shape = ([pltpu.SemaphoreType.DMA(s) for s in new_sems] + [pltpu.HBM(a.shape, a.dtype) for a in arrays]
                 + ([jax.ShapeDtypeStruct((8, 128), f32)] if token else []))
    out_specs = [SEM_SPEC] * nn + [HBM_SPEC] * na + ([pl.BlockSpec(memory_space=pltpu.VMEM)] if token else [])
    res = pl.pallas_call(
        kern, name=name, out_shape=tuple(out_shape),
        in_specs=[HBM_SPEC] * na + [SEM_SPEC] * ns + [ANY_SPEC] * nf, out_specs=tuple(out_specs),
        input_output_aliases={t: nn + t for t in range(na)},
        compiler_params=pltpu.CompilerParams(has_side_effects=pltpu.SideEffectType.DATAFLOW_SIDE_EFFECTING),
    )(*[pltpu.with_memory_space_constraint(a, pltpu.HBM) for a in arrays], *sems, *after)
    return list(res[:nn]), list(res[nn:nn + na]), (res[nn + na] if token else None)


def _remote(src, dst, send_sem, recv_sem, device):
    return pltpu.make_async_remote_copy(src_ref=src, dst_ref=dst, send_sem=send_sem, recv_sem=recv_sem,
                                        device_id=device, device_id_type=MESH)


def _gather_start(name, src):
    def body(a, s, new):
        (src, land), (send, recv) = a, new
        x, y, c, me = _my_place()
        pltpu.make_async_copy(src, land.at[me], recv.at[0]).start()
        for k in range(1, NDEV):
            peer, _ = _peer(x, y, c, k)
            _remote(src, land.at[me], send.at[k], recv.at[k], peer).start()

    land = lax.empty((NDEV,) + src.shape, src.dtype)
    (send, recv), (src, land), token = _comm_call(name, [src, land], [], [(NDEV,), (NDEV,)], body, token=True)
    return send, recv, src, land, token


def _gather_wait(name, src, land, send, recv, after):
    def body(a, s, new):
        (src, land), (send, recv) = a, s
        x, y, c, me = _my_place()
        pltpu.make_async_copy(src, land.at[me], recv.at[0]).wait()
        for k in range(1, NDEV):
            peer, slot = _peer(x, y, c, k)
            _remote(src, land.at[slot], send.at[k], recv.at[k], peer).wait_recv()
        for k in range(1, NDEV):
            peer, _ = _peer(x, y, c, k)
            _remote(src, land.at[me], send.at[k], recv.at[k], peer).wait_send()

    return _comm_call(name, [src, land], [send, recv], [], body, after=after)[1][1]


SAME_CORE = (2, 4, 6)
VIA_SIBLING = (3, 5, 7)


def _weights_forward(name, land, recv, after):
    def body(a, s, new):
        (land,), (recv,), (fsend, frecv) = a, s, new
        x, y, c, me = _my_place()
        sibling, _ = _peer(x, y, c, 1)
        for k in SAME_CORE:
            peer, slot = _peer(x, y, c, k)
            _remote(land.at[slot], land.at[slot], fsend.at[k], recv.at[k], peer).wait_recv()
            _remote(land.at[slot], land.at[slot], fsend.at[k], frecv.at[k ^ 1], sibling).start()

    return _comm_call(name, [land], [recv], [(NDEV,), (NDEV,)], body, after=after)


def _weights_wait(name, land, send, recv, fsend, frecv, after):
    def body(a, s, new):
        (land,), (send, recv, fsend, frecv) = a, s
        x, y, c, me = _my_place()
        sibling, sib_slot = _peer(x, y, c, 1)
        _remote(land.at[sib_slot], land.at[sib_slot], send.at[1], recv.at[1], sibling).wait_recv()
        for k in VIA_SIBLING:
            _, slot = _peer(x, y, c, k)
            _remote(land.at[slot], land.at[slot], fsend.at[k ^ 1], frecv.at[k], sibling).wait_recv()
        for k in (1,) + SAME_CORE:
            peer, _ = _peer(x, y, c, k)
            _remote(land.at[me], land.at[me], send.at[k], recv.at[k], peer).wait_send()
        for k in SAME_CORE:
            _, slot = _peer(x, y, c, k)
            _remote(land.at[slot], land.at[slot], fsend.at[k], frecv.at[k ^ 1], sibling).wait_send()

    return _comm_call(name, [land], [send, recv, fsend, frecv], [], body, after=after)[1][0]


def _diag_relay(x, y, c):
    slot = 4 * (x ^ (1 - c)) + 2 * (y ^ c) + c
    return slot, (x ^ c, y ^ (1 - c), c)


def _w_in_start(land, after):
    def body(a, s, new):
        (land,), (send, recv) = a, new
        x, y, c, me = _my_place()
        for k in (1, 2, 4):
            peer, _ = _peer(x, y, c, k)
            _remote(land.at[me], land.at[me], send.at[k], recv.at[k], peer).start()

    (send, recv), (land,), token = _comm_call("w_in_start", [land], [], [(NDEV,), (NDEV,)], body, after=after, token=True)
    return send, recv, land, token


def _w_in_sibling(land, recv, after):
    def body(a, s, new):
        (land,), (recv,) = a, s
        x, y, c, me = _my_place()
        sibling, slot = _peer(x, y, c, 1)
        _remote(land.at[slot], land.at[slot], recv.at[1], recv.at[1], sibling).wait_recv()

    return _comm_call("w_in_sibling", [land], [recv], [], body, after=after)[1][0]


def _w_in_relay(land, recv, after):
    def body(a, s, new):
        (land,), (recv,), (fsend, frecv) = a, s, new
        x, y, c, me = _my_place()
        sibling, _ = _peer(x, y, c, 1)
        for k in (2, 4):
            peer, slot = _peer(x, y, c, k)
            _remote(land.at[slot], land.at[slot], fsend.at[k], recv.at[k], peer).wait_recv()
        slot, target = _diag_relay(x, y, c)
        _remote(land.at[slot], land.at[slot], fsend.at[6], frecv.at[6], target).start()
        for k in (2, 4):
            _, slot = _peer(x, y, c, k)
            _remote(land.at[slot], land.at[slot], fsend.at[k], frecv.at[k ^ 1], sibling).start()

    (fsend, frecv), (land,), _ = _comm_call("w_in_relay", [land], [recv], [(NDEV,), (NDEV,)], body, after=after)
    return fsend, frecv, land


def _w_in_forwarded(land, frecv, after):
    def body(a, s, new):
        (land,), (frecv,) = a, s
        x, y, c, me = _my_place()
        sibling, _ = _peer(x, y, c, 1)
        for k in (3, 5):
            _, slot = _peer(x, y, c, k)
            _remote(land.at[slot], land.at[slot], frecv.at[k], frecv.at[k], sibling).wait_recv()

    return _comm_call("w_in_forwarded", [land], [frecv], [], body, after=after)[1][0]


def _w_in_diag(land, land_o, frecv, after):
    def body(a, s, new):
        (land, land_o), (frecv,), (dsend, drecv, osend, orecv) = a, s, new
        x, y, c, me = _my_place()
        sibling, _ = _peer(x, y, c, 1)
        peer, slot = _peer(x, y, c, 6)
        _remote(land.at[slot], land.at[slot], dsend.at[6], frecv.at[6], peer).wait_recv()
        _remote(land.at[slot], land.at[slot], dsend.at[6], drecv.at[7], sibling).start()
        for k in (1,) + SAME_CORE:
            peer, _ = _peer(x, y, c, k)
            _remote(land_o.at[me], land_o.at[me], osend.at[k], orecv.at[k], peer).start()

    sems, (land, land_o), _ = _comm_call("w_in_diag", [land, land_o], [frecv], [(NDEV,)] * 4, body, after=after)
    return sems, land, land_o


def _w_in_finish(land, send, fsend, dsend, drecv, after):
    def body(a, s, new):
        (land,), (send, fsend, dsend, drecv) = a, s
        x, y, c, me = _my_place()
        sibling, _ = _peer(x, y, c, 1)
        _, slot = _peer(x, y, c, 7)
        _remote(land.at[slot], land.at[slot], dsend.at[6], drecv.at[7], sibling).wait_recv()
        for k in (1, 2, 4):
            peer, _ = _peer(x, y, c, k)
            _remote(land.at[me], land.at[me], send.at[k], send.at[k], peer).wait_send()
        for k in (2, 4, 6):
            _, slot = _peer(x, y, c, k)
            _remote(land.at[slot], land.at[slot], fsend.at[k], fsend.at[k], sibling).wait_send()
        _, slot = _peer(x, y, c, 6)
        _remote(land.at[slot], land.at[slot], dsend.at[6], dsend.at[6], sibling).wait_send()

    return _comm_call("w_in_finish", [land], [send, fsend, dsend, drecv], [], body, after=after)[1][0]


def _in_proj_part(name, h, land, proj, me_arr, k0, kstep, nk, tm=512):
    S, D = h.shape
    C = land.shape[2]
    tm = min(tm, S)

    def body(me_ref, a_ref, b_ref, *rest):
        rest[-1][...] = jnp.dot(a_ref[...], b_ref[...], preferred_element_type=f32)

    slot = lambda j, me: me[0] ^ (k0 + kstep * j)
    args = [h, land] + ([] if proj is None else [proj])
    grid_spec = pltpu.PrefetchScalarGridSpec(
        num_scalar_prefetch=1, grid=(nk, S // tm),
        in_specs=[pl.BlockSpec((tm, D), lambda j, i, me: (i, 0)),
                  pl.BlockSpec((None, D, C), lambda j, i, me: (slot(j, me), 0, 0))] + [ANY_SPEC] * (len(args) - 2),
        out_specs=pl.BlockSpec((tm, C), lambda j, i, me: (i, slot(j, me))))
    return pl.pallas_call(
        body, name=name, out_shape=jax.ShapeDtypeStruct((S, NDEV * C), f32), grid_spec=grid_spec,
        input_output_aliases={} if proj is None else {3: 0}, compiler_params=_params("arbitrary", "arbitrary"),
    )(me_arr, *args)


NCHIP = NDEV // 2


def _pairs_start(name, srcs):
    n = len(srcs)
    npairs = [src.shape[0] // 2 for src in srcs]

    def body(a, s, new):
        x, y, c, me = _my_place()
        sibling, _ = _peer(x, y, c, 1)
        for t in range(n):
            src, pair, send, recv = a[t], a[n + t], new[t], new[n + t]
            for i in range(npairs[t]):
                _remote(src.at[2 * i + 1 - c], pair.at[i], send.at[i], recv.at[i], sibling).start()

    pairs = [lax.empty((npairs[t],) + srcs[t].shape[1:], srcs[t].dtype) for t in range(n)]
    sems, arrays, token = _comm_call(name, list(srcs) + pairs, [], [(m,) for m in npairs] * 2, body, token=True)
    return [(sems[t], sems[n + t], arrays[t], arrays[n + t]) for t in range(n)], token


def _pairs_wait(name, groups, after):
    n = len(groups)

    def body(a, s, new):
        x, y, c, me = _my_place()
        sibling, _ = _peer(x, y, c, 1)
        for t in range(n):
            src, pair, send, recv = a[t], a[n + t], s[t], s[n + t]
            for i in range(pair.shape[0]):
                cp = _remote(src.at[2 * i + 1 - c], pair.at[i], send.at[i], recv.at[i], sibling)
                cp.wait_recv()
                cp.wait_send()

    arrays = _comm_call(name, [g[2] for g in groups] + [g[3] for g in groups],
                        [g[0] for g in groups] + [g[1] for g in groups], [], body, after=after)[1]
    return [(arrays[t], arrays[n + t]) for t in range(n)]


def _pair_start(name, src):
    ((send, recv, src, pair),), token = _pairs_start(name, [src])
    return send, recv, src, pair, token


def _pair_wait(name, src, pair, send, recv, after):
    return _pairs_wait(name, [(send, recv, src, pair)], after)[0]


def _pair_sum(name, src, pair, core, tr=1024):
    npair, R, Cc = pair.shape
    tr = min(tr, R)

    def body(core_ref, a_ref, b_ref, o_ref):
        o_ref[...] = (a_ref[...].astype(f32) + b_ref[...].astype(f32)).astype(o_ref.dtype)

    grid_spec = pltpu.PrefetchScalarGridSpec(
        num_scalar_prefetch=1, grid=(npair, R // tr),
        in_specs=[pl.BlockSpec((None, tr, Cc), lambda i, r, core: (2 * i + core[0], r, 0)),
                  pl.BlockSpec((None, tr, Cc), lambda i, r, core: (i, r, 0))],
        out_specs=pl.BlockSpec((None, tr, Cc), lambda i, r, core: (i, r, 0)))
    return pl.pallas_call(body, name=name, out_shape=jax.ShapeDtypeStruct(pair.shape, pair.dtype),
                          grid_spec=grid_spec, compiler_params=_params("parallel", "parallel"))(core, src, pair)


def _owner_chip(first, i):
    q = first // 2 + i
    return q >> 1 & 1, q & 1


def _chips_start(name, groups, rows=None, after=()):
    n = len(groups)
    row_of = [pl.ds(*(rows or (0, g[0].shape[1]))) for g in groups]

    def body(a, s, new):
        x, y, c, me = _my_place()
        for t, (_, _, first) in enumerate(groups):
            sums, land, send, recv = a[t], a[n + t], new[t], new[n + t]
            for i in range(sums.shape[0]):
                ox, oy = _owner_chip(first, i)

                @pl.when((x != ox) | (y != oy))
                def _():
                    _remote(sums.at[i, row_of[t]], land.at[2 * x + y, row_of[t]], send.at[i], recv.at[2 * x + y],
                            (ox, oy, c)).start()

    sems, arrays, token = _comm_call(name, [g[0] for g in groups] + [g[1] for g in groups], [],
                                     [(g[0].shape[0],) for g in groups] + [(NCHIP,)] * n, body, after=after, token=True)
    return [(sems[t], sems[n + t], arrays[t], arrays[n + t]) for t in range(n)], token


def _chips_wait(name, groups, firsts, after, rows=None):
    n = len(groups)
    row_of = [pl.ds(*(rows or (0, g[2].shape[1]))) for g in groups]

    def body(a, s, new):
        x, y, c, me = _my_place()
        for t in range(n):
            sums, land, send, recv, first = a[t], a[n + t], s[t], s[n + t], firsts[t]
            npair = sums.shape[0]
            mine = (me >= first) & (me < first + 2 * npair)
            for i in range(npair):
                ox, oy = _owner_chip(first, i)

                @pl.when((x != ox) | (y != oy))
                def _():
                    _remote(sums.at[i, row_of[t]], land.at[2 * x + y, row_of[t]], send.at[i], recv.at[2 * x + y],
                            (ox, oy, c)).wait_send()
            for q in range(NCHIP):
                @pl.when(mine & (2 * x + y != q))
                def _():
                    _remote(sums.at[0, row_of[t]], land.at[q, row_of[t]], send.at[0], recv.at[q],
                            (q >> 1, q & 1, c)).wait_recv()

    arrays = _comm_call(name, [g[2] for g in groups] + [g[3] for g in groups],
                        [g[0] for g in groups] + [g[1] for g in groups], [], body, after=after)[1]
    return [(arrays[t], arrays[n + t]) for t in range(n)]


def _chip_start(name, sums, land, first, rows=None, after=()):
    ((send, recv, sums, land),), token = _chips_start(name, [(sums, land, first)], rows, after)
    return send, recv, sums, land, token


def _chip_wait(name, sums, land, send, recv, first, after, rows=None):
    return _chips_wait(name, [(send, recv, sums, land)], [first], after, rows)[0]


def _matmul(a, b, *, name, out_dtype, ta=False, tb=False, b_slots=False, out_slots=0, b_cols=None,
            tm=1024, tn=1024, tk=2048, dep=None):
    M, K = (a.shape[1], a.shape[0]) if ta else a.shape
    col0 = 0
    if b_slots:
        slab = b.shape[2]
        N = b.shape[1] if tb else b.shape[0] * slab
        assert (K if tb else N) == b.shape[0] * slab
    elif b_cols is not None:
        assert not tb
        col0, N = b_cols
    else:
        N = b.shape[0] if tb else b.shape[1]
    tm, tn, tk = min(tm, M), min(tn, N), min(tk, K)
    if b_slots:
        if tb:
            tk = min(tk, slab)
        else:
            tn = min(tn, slab)
    if out_slots:
        tn = min(tn, N // out_slots)
    nm, nn, nk = M // tm, N // tn, K // tk
    assert (nm * tm, nn * tn, nk * tk) == (M, N, K) and col0 % tn == 0, (name, M, N, K, tm, tn, tk)
    j0 = col0 // tn

    a_spec = pl.BlockSpec((tk, tm), lambda i, j, k: (k, i)) if ta else pl.BlockSpec((tm, tk), lambda i, j, k: (i, k))
    if b_slots and tb:
        per = slab // tk
        b_spec = pl.BlockSpec((None, tn, tk), lambda i, j, k: (k // per, j, k % per))
    elif b_slots:
        per = slab // tn
        b_spec = pl.BlockSpec((None, tk, tn), lambda i, j, k: (j // per, k, j % per))
    elif tb:
        b_spec = pl.BlockSpec((tn, tk), lambda i, j, k: (j, k))
    else:
        b_spec = pl.BlockSpec((tk, tn), lambda i, j, k: (k, j + j0))
    if out_slots:
        per_o = (N // out_slots) // tn
        o_spec = pl.BlockSpec((None, tm, tn), lambda i, j, k: (j // per_o, i, j % per_o))
        out_shape = jax.ShapeDtypeStruct((out_slots, M, N // out_slots), out_dtype)
    else:
        o_spec = pl.BlockSpec((tm, tn), lambda i, j, k: (i, j))
        out_shape = jax.ShapeDtypeStruct((M, N), out_dtype)
    dims = (((0 if ta else 1,), (1 if tb else 0,)), ((), ()))
    deps = [] if dep is None else [dep]

    def body(a_ref, b_ref, *rest):
        o_ref = rest[len(deps)]
        prod = lax.dot_general(a_ref[...], b_ref[...], dims, preferred_element_type=f32)
        if nk == 1:
            o_ref[...] = prod.astype(out_dtype)
            return
        acc_ref = rest[len(deps) + 1]
        k = pl.program_id(2)

        @pl.when(k == 0)
        def _():
            acc_ref[...] = prod

        @pl.when((k > 0) & (k < nk - 1))
        def _():
            acc_ref[...] += prod

        @pl.when(k == nk - 1)
        def _():
            o_ref[...] = (acc_ref[...] + prod).astype(out_dtype)

    return pl.pallas_call(
        body, name=name, out_shape=out_shape, grid=(nm, nn, nk),
        in_specs=[a_spec, b_spec] + [ANY_SPEC] * len(deps), out_specs=o_spec,
        scratch_shapes=[pltpu.VMEM((tm, tn), f32)] if nk > 1 else [],
        compiler_params=_params("parallel", "parallel", "arbitrary"),
    )(a, b, *deps)


def _matmul_slabs_t(a_cols, a_slots, b, *, name, tm=512, tn=512, dep=None):
    M = a_cols.shape[0]
    n_slab, N, slab = b.shape
    n1, n2 = a_cols.shape[1] // slab, a_slots.shape[0]
    assert n1 + n2 == n_slab and a_slots.shape[1:] == (M, slab)
    tm, tn = min(tm, M), min(tn, N)
    deps = [] if dep is None else [dep]

    def body(a1_ref, a2_ref, b_ref, *rest):
        o_ref = rest[len(deps)]
        acc = None
        for s in range(n_slab):
            lhs = a1_ref[:, s * slab:(s + 1) * slab] if s < n1 else a2_ref[s - n1]
            prod = lax.dot_general(lhs, b_ref[s], (((1,), (1,)), ((), ())), preferred_element_type=f32)
            acc = prod if acc is None else acc + prod
        o_ref[...] = acc

    return pl.pallas_call(
        body, name=name, out_shape=jax.ShapeDtypeStruct((M, N), f32), grid=(M // tm, N // tn),
        in_specs=[pl.BlockSpec((tm, n1 * slab), lambda i, j: (i, 0)), pl.BlockSpec((n2, tm, slab), lambda i, j: (0, i, 0)),
                  pl.BlockSpec((n_slab, tn, slab), lambda i, j: (0, j, 0))] + [ANY_SPEC] * len(deps),
        out_specs=pl.BlockSpec((tm, tn), lambda i, j: (i, j)), compiler_params=_params("parallel", "parallel"),
    )(a_cols, a_slots, b, *deps)


def _ada_exchange(c_blk, cw_slab, w_ada, b_cols):
    nblk = c_blk.shape[0]
    D, W = w_ada.shape
    CW = cw_slab.shape[1]

    def body(c_ref, cw_ref, w_ref, b_ref, mod_ref, call_ref, cwg_ref, msend, send_sems, recv_sems):
        x, y, c, me = _my_place()
        call_ref[me] = _silu(c_ref[...])
        cwg_ref[me] = cw_ref[...]
        first = []
        for k in range(1, NDEV):
            peer, _ = _peer(x, y, c, k)
            first.append(_remote(call_ref.at[me], call_ref.at[me], send_sems.at[0, k], recv_sems.at[0, k], peer))
            first.append(_remote(cwg_ref.at[me], cwg_ref.at[me], send_sems.at[1, k], recv_sems.at[1, k], peer))
        for cp in first:
            cp.start()
        for k in range(1, NDEV):
            peer, slot = _peer(x, y, c, k)
            _remote(call_ref.at[slot], call_ref.at[slot], send_sems.at[0, k], recv_sems.at[0, k], peer).wait_recv()
            _remote(cwg_ref.at[slot], cwg_ref.at[slot], send_sems.at[1, k], recv_sems.at[1, k], peer).wait_recv()
        mod = jnp.broadcast_to(b_ref[...], (NDEV, W))
        for r in range(nblk):
            mod = mod + lax.dot_general(call_ref[:, r, :], w_ref[r * 128:(r + 1) * 128, :], (((1,), (0,)), ((), ())),
                                        preferred_element_type=f32, precision=lax.Precision.HIGHEST)
        row = lax.broadcasted_iota(jnp.int32, (NDEV, 1), 0)
        pick = lambda j: jnp.broadcast_to(jnp.sum(jnp.where(row == j, mod, 0.0), axis=0, keepdims=True), (8, W))
        mod_ref[me] = pick(me)
        second = []
        for k in range(1, NDEV):
            peer, slot = _peer(x, y, c, k)
            msend[k] = pick(slot)
            second.append(_remote(msend.at[k], mod_ref.at[me], send_sems.at[2, k], recv_sems.at[2, k], peer))
        for cp in second:
            cp.start()
        for k in range(1, NDEV):
            peer, slot = _peer(x, y, c, k)
            _remote(msend.at[k], mod_ref.at[slot], send_sems.at[2, k], recv_sems.at[2, k], peer).wait_recv()
        for cp in first + second:
            cp.wait_send()

    vmem = pl.BlockSpec(memory_space=pltpu.VMEM)
    return pl.pallas_call(
        body, name="ada_exchange",
        out_shape=(jax.ShapeDtypeStruct((NDEV, 8, W), f32), jax.ShapeDtypeStruct((NDEV, nblk, 128), f32),
                   jax.ShapeDtypeStruct((NDEV, 8, CW), f32)),
        in_specs=[vmem] * 4, out_specs=(vmem, vmem, vmem),
        scratch_shapes=[pltpu.VMEM((NDEV, 8, W), f32), pltpu.SemaphoreType.DMA((3, NDEV)),
                        pltpu.SemaphoreType.DMA((3, NDEV))],
        compiler_params=_params(),
    )(c_blk, cw_slab, w_ada, b_cols)


def _prenorm(x, scale, shift, g_pre, dep, tr=512):
    S, D = x.shape
    tr = min(tr, S)

    def body(x_ref, sc_ref, sh_ref, g_ref, dep_ref, h_ref):
        xv = x_ref[...]
        r = lax.rsqrt(jnp.mean(xv * xv, axis=-1, keepdims=True) + EPS)
        h_ref[...] = ((xv * r) * g_ref[...] * (1.0 + sc_ref[...]) + sh_ref[...]).astype(bf16)

    row = pl.BlockSpec((tr, D), lambda i: (i, 0))
    vec = pl.BlockSpec((1, D), lambda i: (0, 0))
    return pl.pallas_call(body, name="prenorm", out_shape=jax.ShapeDtypeStruct((S, D), bf16), grid=(S // tr,),
                          in_specs=[row, vec, vec, vec, ANY_SPEC], out_specs=row, compiler_params=_params("parallel"))(
                              x, scale, shift, g_pre, dep)


def _ext_rows(i, tr, S):
    g = lax.broadcasted_iota(jnp.int32, (tr + 16, 1), 0) + (i * tr - 8)
    return (g >= 0) & (g < S)


def _halo_specs(tr, S, C, col):
    nb8 = S // 8
    main = pl.BlockSpec((tr, C), lambda i: (i, col))
    prev = pl.BlockSpec((8, C), lambda i: (jnp.maximum(i * (tr // 8) - 1, 0), col))
    nxt = pl.BlockSpec((8, C), lambda i: (jnp.minimum((i + 1) * (tr // 8), nb8 - 1), col))
    return prev, main, nxt


def _conv_fwd(proj, conv_w, conv_b, g_conv, tr=512):
    S, C = proj.shape[0], proj.shape[1] // 8
    tr = min(tr, S)

    def body(up, um, un, cp, cm, cn, bg_ref, zc_ref, w_ref, cb_ref, g_ref, o_ref):
        i = pl.program_id(0)
        exists = _ext_rows(i, tr, S)
        u = jnp.concatenate([up[...], um[...], un[...]], axis=0)
        cg = jnp.concatenate([cp[...], cm[...], cn[...]], axis=0)
        t = jnp.where(exists, cg * u, 0.0)
        t_before = pltpu.roll(t, 1, 0)[8:tr + 8]
        t_after = pltpu.roll(t, tr + 15, 0)[8:tr + 8]
        w = w_ref[...]
        cv = w[0:1] * t_before + w[1:2] * t[8:tr + 8] + w[2:3] * t_after + cb_ref[...]
        yc = bg_ref[...] * cv
        rc = lax.rsqrt(jnp.mean(yc * yc, axis=-1, keepdims=True) + EPS)
        o_ref[...] = ((yc * rc) * g_ref[...] * _silu(zc_ref[...])).astype(bf16)

    u_specs = _halo_specs(tr, S, C, 0)
    c_specs = _halo_specs(tr, S, C, 2)
    vec = pl.BlockSpec((1, C), lambda i: (0, 0))
    return pl.pallas_call(
        body, name="conv_fwd", out_shape=jax.ShapeDtypeStruct((S, 2 * C), bf16), grid=(S // tr,),
        in_specs=[*u_specs, *c_specs, pl.BlockSpec((tr, C), lambda i: (i, 1)), pl.BlockSpec((tr, C), lambda i: (i, 3)),
                  pl.BlockSpec((8, C), lambda i: (0, 0)), vec, vec],
        out_specs=pl.BlockSpec((tr, C), lambda i: (i, 0)), compiler_params=_params("parallel"),
    )(proj, proj, proj, proj, proj, proj, proj, proj, conv_w, conv_b, g_conv)


def _branch_geometry(S, r, inter):
    L = S // r * inter
    nq = min(128, L)
    nk = min(nq + 2 * HALF_WIN * inter, L)
    assert L % nq == 0 and (L == nk or L >= nq + 2 * HALF_WIN * inter)
    return L, nq, nk, L // nq


QUAD = 4


def _to_quad(dst, src, S):
    n = S // QUAD
    for rho in range(QUAD):
        dst[pl.ds(rho * n, n), :] = src[pl.ds(rho, n, stride=QUAD), :]


def _block_rows(idx, r, inter, S, L, nq, nk, nblk):
    rho, qb = (0, idx) if r == 1 else (idx // nblk, idx % nblk)
    i0 = qb * nq
    ws = jnp.clip(i0 - HALF_WIN * inter, 0, L - nk)
    if r == 1:
        return pl.ds(pl.multiple_of(i0, 8), nq), pl.ds(pl.multiple_of(ws, 8), nk), i0 - ws
    assert r % (QUAD * inter) == 0
    step = r // QUAD // inter
    base = (rho % QUAD) * (S // QUAD) + rho // QUAD
    if step == 1:
        return pl.ds(pl.multiple_of(base + i0, 8), nq), pl.ds(pl.multiple_of(base + ws, 8), nk), i0 - ws
    return pl.ds(base + step * i0, nq, stride=step), pl.ds(base + step * ws, nk, stride=step), i0 - ws


N_CASES = 3
SCALE = HEAD_DIM ** -0.5
ATTN_UNROLL = 16


def _bias_shape(S):
    shapes = [_branch_geometry(S, r, inter)[1:3] for _, r, inter in BRANCHES]
    return (len(BRANCHES) * N_CASES * 2, max(nq for nq, _ in shapes), max(nk for _, nk in shapes))


def _bias_index(b, case, head):
    return (b * N_CASES + case) * 2 + head


def _fill_bias(bias_scr, sl_ref, S):
    sl = sl_ref[...]
    slope = (sl[0:1, 0:1], sl[0:1, HEAD_DIM:HEAD_DIM + 1])
    for b, (_, r, inter) in enumerate(BRANCHES):
        L, nq, nk, nblk = _branch_geometry(S, r, inter)
        rel = lax.broadcasted_iota(jnp.int32, (nq, nk), 0) - lax.broadcasted_iota(jnp.int32, (nq, nk), 1)
        for case in range(N_CASES):
            d = jnp.abs(rel + case * HALF_WIN)
            valid = d <= HALF_WIN * inter
            if inter > 1:
                valid = valid & (jnp.bitwise_and(d, inter - 1) == 0)
            dist = d.astype(f32) * float(r // inter)
            for head in range(2):
                bias_scr[_bias_index(b, case, head), 0:nq, 0:nk] = jnp.where(valid, -slope[head] * dist, NEG_INF)


def _bias_tiles(slopes, S, dep):
    npair = slopes.shape[0]
    shape = _bias_shape(S)

    def body(sl_ref, dep_ref, o_ref):
        _fill_bias(o_ref, sl_ref, S)

    return pl.pallas_call(
        body, name="bias_tiles", out_shape=jax.ShapeDtypeStruct((npair,) + shape, f32), grid=(npair,),
        in_specs=[pl.BlockSpec((None, 8, PAIR), lambda p: (p, 0, 0)), ANY_SPEC],
        out_specs=pl.BlockSpec((None,) + shape, lambda p: (p, 0, 0, 0)), compiler_params=_params("parallel"),
    )(slopes, dep)


def _head_slopes(n_heads):
    slopes = 2.0 ** (-8.0 * jnp.arange(1, n_heads + 1, dtype=f32) / n_heads)
    return jnp.broadcast_to(jnp.repeat(slopes.reshape(n_heads // 2, 2), HEAD_DIM, axis=1)[:, None, :],
                            (n_heads // 2, 8, PAIR))


def _attn_fwd(proj, bias):
    S, C = proj.shape[0], proj.shape[1] // 8
    npair = C // PAIR

    def body(q_ref, k_ref, v_ref, bias_scr, o_ref, lse_ref, m_scr, l_scr, a_scr, q4_scr, k4_scr, v4_scr):
        lane = lax.broadcasted_iota(jnp.int32, (1, PAIR), 1)
        first = lane < HEAD_DIM
        for dst, src in ((q4_scr, q_ref), (k4_scr, k_ref), (v4_scr, v_ref)):
            _to_quad(dst, src, S)

        for b, (_, r, inter) in enumerate(BRANCHES):
            L, nq, nk, nblk = _branch_geometry(S, r, inter)
            qs, ks, vs = (q_ref, k_ref, v_ref) if r == 1 else (q4_scr, k4_scr, v4_scr)

            def step(idx, carry, b=b, r=r, L=L, nq=nq, nk=nk, nblk=nblk, qs=qs, ks=ks, vs=vs):
                qrows, krows, off = _block_rows(idx, r, inter, S, L, nq, nk, nblk)
                case = off // HALF_WIN
                q2 = qs[qrows, :] * SCALE
                k2 = ks[krows, :].astype(bf16)
                v2 = vs[krows, :].astype(bf16)
                ms, accs = [], []
                for hh in range(2):
                    mine = first if hh == 0 else ~first
                    qh = jnp.where(mine, q2, 0.0).astype(bf16)
                    s = lax.dot_general(qh, k2, (((1,), (1,)), ((), ())), preferred_element_type=f32)
                    s = s + bias_scr[_bias_index(b, case, hh), 0:nq, 0:nk]
                    m = jnp.max(s, axis=-1, keepdims=True)
                    p = jnp.exp(s - m).astype(bf16)
                    vh = jnp.where(mine, v2, jnp.ones_like(v2))
                    ms.append(m)
                    accs.append(jnp.dot(p, vh, preferred_element_type=f32))
                m_scr[b, qrows, :] = jnp.where(first, ms[0], ms[1])
                a_scr[b, qrows, :] = jnp.where(first, accs[0], accs[1])
                l_scr[b, qrows, :] = jnp.where(first, accs[1], accs[0])
                return carry

            lax.fori_loop(0, S // nq, step, 0, unroll=min(ATTN_UNROLL, S // nq))

        n4 = S // QUAD
        ch = min(256, n4)
        nch = n4 // ch

        def merge(i, carry):
            rho, part = i // nch, i % nch
            sorted_rows = pl.ds(pl.multiple_of(rho * n4 + part * ch, 8), ch)
            token_rows = pl.ds(rho + QUAD * part * ch, ch, stride=QUAD)
            rows = (token_rows,) + (sorted_rows,) * (len(BRANCHES) - 1)
            ms = [m_scr[b, rows[b], :] for b in range(len(BRANCHES))]
            m = functools.reduce(jnp.maximum, ms)
            l = jnp.zeros((ch, PAIR), f32)
            acc = jnp.zeros((ch, PAIR), f32)
            for b in range(len(BRANCHES)):
                w = jnp.exp(ms[b] - m)
                l = l + w * pltpu.roll(l_scr[b, rows[b], :], HEAD_DIM, 1)
                acc = acc + w * a_scr[b, rows[b], :]
            o_ref[token_rows, :] = acc / l
            lse_ref[token_rows, :] = m + jnp.log(l)
            return carry

        lax.fori_loop(0, QUAD * nch, merge, 0, unroll=2)

    blk = lambda part: pl.BlockSpec((S, PAIR), lambda p: (0, part * npair + p))
    out = pl.BlockSpec((S, PAIR), lambda p: (0, p))
    return pl.pallas_call(
        body, name="attn_fwd",
        out_shape=(jax.ShapeDtypeStruct((S, C), f32), jax.ShapeDtypeStruct((S, C), f32)), grid=(npair,),
        in_specs=[blk(4), blk(5), blk(6), pl.BlockSpec((None,) + _bias_shape(S), lambda p: (p, 0, 0, 0))],
        out_specs=(out, out),
        scratch_shapes=[pltpu.VMEM((3, S, PAIR), f32)] * 3 + [pltpu.VMEM((S, PAIR), f32)] * 3,
        compiler_params=_params("parallel"),
    )(proj, proj, proj, bias)


def _attn_post(ycat, o, proj, g_attn, tr=512):
    S, C = o.shape
    tr = min(tr, S)

    def body(y_ref, o_ref, z_ref, g_ref, out_ref):
        del y_ref
        ov = o_ref[...]
        ra = lax.rsqrt(jnp.mean(ov * ov, axis=-1, keepdims=True) + EPS)
        out_ref[...] = ((ov * ra) * g_ref[...] * _silu(z_ref[...])).astype(bf16)

    return pl.pallas_call(
        body, name="attn_post", out_shape=jax.ShapeDtypeStruct(ycat.shape, ycat.dtype), grid=(S // tr,),
        in_specs=[HBM_SPEC, pl.BlockSpec((tr, C), lambda i: (i, 0)), pl.BlockSpec((tr, C), lambda i: (i, 7)),
                  pl.BlockSpec((1, C), lambda i: (0, 0))],
        out_specs=pl.BlockSpec((tr, C), lambda i: (i, 1)), input_output_aliases={0: 0},
        compiler_params=_params("arbitrary"),
    )(ycat, o, proj, g_attn)


def _residual_minus_target(x, target, dep, tr=512):
    S, D = x.shape
    tr = min(tr, S)

    def body(x_ref, t_ref, dep_ref, o_ref):
        o_ref[...] = x_ref[...] - t_ref[...]

    row = pl.BlockSpec((tr, D), lambda i: (i, 0))
    return pl.pallas_call(body, name="residual_minus_target", out_shape=jax.ShapeDtypeStruct((S, D), f32),
                          grid=(S // tr,), in_specs=[row, row, ANY_SPEC], out_specs=row,
                          compiler_params=_params("parallel"))(x, target, dep)


def _sandwich(y, x_minus_t, gate, g_post, tr=256):
    S, D = y.shape
    tr = min(tr, S)

    def body(y_ref, xt_ref, gate_ref, g_ref, dy_ref, dout_ref, sums_ref):
        i = pl.program_id(0)
        gate, g = gate_ref[...], g_ref[...]
        gg = gate * g
        yv = y_ref[...]
        rp = lax.rsqrt(jnp.mean(yv * yv, axis=-1, keepdims=True) + EPS)
        yhat = yv * rp
        err = xt_ref[...] + gg * yhat
        dout = err * (1.0 / D)
        dout_ref[...] = dout
        q = dout * yhat
        w = dout * gg
        dy_ref[...] = (rp * (w - yhat * jnp.sum(q * gg, axis=-1, keepdims=True) * (1.0 / D))).astype(bf16)
        loss = 0.5 * jnp.sum(jnp.mean(err * err, axis=-1, keepdims=True), axis=0, keepdims=True)
        q_sum = jnp.sum(q, axis=0, keepdims=True)
        row = lax.broadcasted_iota(jnp.int32, (8, D), 0)
        upd = jnp.where(row == 0, q_sum * g, jnp.where(row == 1, q_sum * gate, jnp.where(row == 2, loss, 0.0)))

        @pl.when(i == 0)
        def _():
            sums_ref[...] = upd

        @pl.when(i > 0)
        def _():
            sums_ref[...] += upd

    row = pl.BlockSpec((tr, D), lambda i: (i, 0))
    vec = pl.BlockSpec((1, D), lambda i: (0, 0))
    return pl.pallas_call(
        body, name="sandwich",
        out_shape=(jax.ShapeDtypeStruct((S, D), bf16), jax.ShapeDtypeStruct((S, D), f32), jax.ShapeDtypeStruct((8, D), f32)),
        grid=(S // tr,), in_specs=[row, row, vec, vec],
        out_specs=(row, row, pl.BlockSpec((8, D), lambda i: (0, 0))), compiler_params=_params("arbitrary"),
    )(y, x_minus_t, gate, g_post)


def _conv_bwd(proj, dycat, conv_w, conv_b, g_conv, dep, tr=256):
    S, C = proj.shape[0], proj.shape[1] // 8
    tr = min(tr, S)
    n = tr + 16

    def body(*refs):
        ins, (w_ref, cb_ref, g_ref, _, dp_ref, sums_ref) = refs[:15], refs[15:]
        i = pl.program_id(0)
        exists = _ext_rows(i, tr, S)
        u, bg, cg, zc, dyn = (jnp.concatenate([ins[3 * t][...], ins[3 * t + 1][...], ins[3 * t + 2][...]], axis=0)
                              for t in range(5))
        w = w_ref[...]
        t = jnp.where(exists, cg * u, 0.0)
        t_before, t_after = pltpu.roll(t, 1, 0), pltpu.roll(t, n - 1, 0)
        cv = w[0:1] * t_before + w[1:2] * t + w[2:3] * t_after + cb_ref[...]
        yc = bg * cv
        rc = lax.rsqrt(jnp.mean(yc * yc, axis=-1, keepdims=True) + EPS)
        yhat = yc * rc
        sz, dsz = _silu_and_slope(zc)
        wgt = dyn * g_ref[...] * sz
        dyc = rc * (wgt - yhat * jnp.mean(wgt * yhat, axis=-1, keepdims=True))
        dcv = jnp.where(exists, dyc * bg, 0.0)
        dt = w[0:1] * pltpu.roll(dcv, n - 1, 0) + w[1:2] * dcv + w[2:3] * pltpu.roll(dcv, 1, 0)
        mid = slice(8, tr + 8)
        dp_ref[:, 0:C] = (dt * cg)[mid].astype(bf16)
        dp_ref[:, C:2 * C] = (dyc * cv)[mid].astype(bf16)
        dp_ref[:, 2 * C:3 * C] = (dt * u)[mid].astype(bf16)
        dp_ref[:, 3 * C:4 * C] = (dyn * yhat * g_ref[...] * dsz)[mid].astype(bf16)
        colsum = lambda v: jnp.sum(v[mid], axis=0, keepdims=True)
        parts = [colsum(dyn * yhat * sz), colsum(dcv), colsum(dcv * t_before), colsum(dcv * t), colsum(dcv * t_after)]
        row = lax.broadcasted_iota(jnp.int32, (8, C), 0)
        upd = jnp.zeros((8, C), f32)
        for j, pj in enumerate(parts):
            upd = jnp.where(row == j, pj, upd)

        @pl.when(i == 0)
        def _():
            sums_ref[...] = upd

        @pl.when(i > 0)
        def _():
            sums_ref[...] += upd

    specs = []
    for col in range(4):
        specs += _halo_specs(tr, S, C, col)
    specs += _halo_specs(tr, S, C, 0)
    vec = pl.BlockSpec((1, C), lambda i: (0, 0))
    return pl.pallas_call(
        body, name="conv_bwd",
        out_shape=(jax.ShapeDtypeStruct((S, 4 * C), bf16), jax.ShapeDtypeStruct((8, C), f32)), grid=(S // tr,),
        in_specs=[*specs, pl.BlockSpec((8, C), lambda i: (0, 0)), vec, vec, ANY_SPEC],
        out_specs=(pl.BlockSpec((tr, 4 * C), lambda i: (i, 0)), pl.BlockSpec((8, C), lambda i: (0, 0))),
        compiler_params=_params("arbitrary"),
    )(*([proj] * 12), dycat, dycat, dycat, conv_w, conv_b, g_conv, dep)


def _attn_post_bwd(o, proj, dycat, g_attn, dep, tr=512):
    S, C = o.shape
    tr = min(tr, S)

    def body(o_ref, z_ref, dy_ref, g_ref, dep_ref, do_ref, dz_ref, sums_ref):
        i = pl.program_id(0)
        ov, zv, dyn = o_ref[...], z_ref[...], dy_ref[...]
        ra = lax.rsqrt(jnp.mean(ov * ov, axis=-1, keepdims=True) + EPS)
        ohat = ov * ra
        sz, dsz = _silu_and_slope(zv)
        wgt = dyn * g_ref[...] * sz
        do_ref[...] = ra * (wgt - ohat * jnp.mean(wgt * ohat, axis=-1, keepdims=True))
        dz_ref[...] = (dyn * ohat * g_ref[...] * dsz).astype(bf16)
        row = lax.broadcasted_iota(jnp.int32, (8, C), 0)
        upd = jnp.where(row == 0, jnp.sum(dyn * ohat * sz, axis=0, keepdims=True), 0.0)

        @pl.when(i == 0)
        def _():
            sums_ref[...] = upd

        @pl.when(i > 0)
        def _():
            sums_ref[...] += upd

    return pl.pallas_call(
        body, name="attn_post_bwd",
        out_shape=(jax.ShapeDtypeStruct((S, C), f32), jax.ShapeDtypeStruct((4, S, C), bf16),
                   jax.ShapeDtypeStruct((8, C), f32)),
        grid=(S // tr,),
        in_specs=[pl.BlockSpec((tr, C), lambda i: (i, 0)), pl.BlockSpec((tr, C), lambda i: (i, 7)),
                  pl.BlockSpec((tr, C), lambda i: (i, 1)), pl.BlockSpec((1, C), lambda i: (0, 0)), ANY_SPEC],
        out_specs=(pl.BlockSpec((tr, C), lambda i: (i, 0)), pl.BlockSpec((None, tr, C), lambda i: (3, i, 0)),
                   pl.BlockSpec((8, C), lambda i: (0, 0))),
        compiler_params=_params("arbitrary"),
    )(o, proj, dycat, g_attn, dep)


def _attn_bwd(proj, o, do, lse, bias, dqkvz, dep):
    S, C = o.shape
    npair = C // PAIR

    def body(q_ref, k_ref, v_ref, o_ref, do_ref, lse_ref, bias_scr, old_ref, dep_ref, dqkv_ref,
             acc_scr, dl_scr, quad_scr):
        lane = lax.broadcasted_iota(jnp.int32, (1, PAIR), 1)
        first = lane < HEAD_DIM
        ch = min(256, S)

        def prep(i, carry):
            rows = pl.ds(pl.multiple_of(i * ch, 8), ch)
            prod = do_ref[rows, :] * o_ref[rows, :]
            d0 = jnp.sum(jnp.where(first, prod, 0.0), axis=-1, keepdims=True)
            d1 = jnp.sum(jnp.where(first, 0.0, prod), axis=-1, keepdims=True)
            dl_scr[rows, :] = jnp.where(first, d0, d1)
            zero = jnp.zeros((ch, PAIR), f32)
            for order in range(2):
                for t in range(3):
                    acc_scr[order, t, rows, :] = zero
            return carry

        lax.fori_loop(0, S // ch, prep, 0, unroll=2)
        token_srcs = (q_ref, k_ref, v_ref, do_ref, lse_ref, dl_scr)
        for j, src in enumerate(token_srcs):
            _to_quad(quad_scr.at[j], src, S)

        for b, (_, r, inter) in enumerate(BRANCHES):
            L, nq, nk, nblk = _branch_geometry(S, r, inter)
            order = 0 if r == 1 else 1
            srcs = token_srcs if r == 1 else tuple(quad_scr.at[j] for j in range(6))

            def step(idx, carry, b=b, r=r, L=L, nq=nq, nk=nk, nblk=nblk, order=order, srcs=srcs):
                qs, ks, vs, dos, lses, dls = srcs
                dq_scr, dk_scr, dv_scr = (acc_scr.at[order, t] for t in range(3))
                qrows, krows, off = _block_rows(idx, r, inter, S, L, nq, nk, nblk)
                case = off // HALF_WIN
                q2 = qs[qrows, :] * SCALE
                k2 = ks[krows, :].astype(bf16)
                v2 = vs[krows, :].astype(bf16)
                do2 = dos[qrows, :]
                lse2 = lses[qrows, :]
                dl2 = dls[qrows, :]
                dq2 = jnp.zeros((nq, PAIR), f32)
                dk2 = jnp.zeros((nk, PAIR), f32)
                dv2 = jnp.zeros((nk, PAIR), f32)
                for hh in range(2):
                    mine = first if hh == 0 else ~first
                    lo = hh * HEAD_DIM
                    qh = jnp.where(mine, q2, 0.0).astype(bf16)
                    doh = jnp.where(mine, do2, 0.0).astype(bf16)
                    s = lax.dot_general(qh, k2, (((1,), (1,)), ((), ())), preferred_element_type=f32)
                    s = s + bias_scr[_bias_index(b, case, hh), 0:nq, 0:nk]
                    p = jnp.exp(s - lse2[:, lo:lo + 1])
                    dv2 = dv2 + lax.dot_general(p.astype(bf16), doh, (((0,), (0,)), ((), ())), preferred_element_type=f32)
                    dp = lax.dot_general(doh, v2, (((1,), (1,)), ((), ())), preferred_element_type=f32)
                    ds = (p * (dp - dl2[:, lo:lo + 1])).astype(bf16)
                    dq2 = dq2 + jnp.where(mine, jnp.dot(ds, k2, preferred_element_type=f32), 0.0)
                    dk2 = dk2 + lax.dot_general(ds, qh, (((0,), (0,)), ((), ())), preferred_element_type=f32)
                dq_scr[qrows, :] = dq_scr[qrows, :] + dq2
                dk_scr[krows, :] = dk_scr[krows, :] + dk2
                dv_scr[krows, :] = dv_scr[krows, :] + dv2
                return carry

            lax.fori_loop(0, S // nq, step, 0, unroll=min(ATTN_UNROLL, S // nq))

        n4 = S // QUAD
        for t in range(3):
            for rho in range(QUAD):
                token_rows = pl.ds(rho, n4, stride=QUAD)
                acc_scr[0, t, token_rows, :] = acc_scr[0, t, token_rows, :] + acc_scr[1, t, pl.ds(rho * n4, n4), :]
        dqkv_ref[0] = (acc_scr[0, 0] * SCALE).astype(bf16)
        dqkv_ref[1] = acc_scr[0, 1].astype(bf16)
        dqkv_ref[2] = acc_scr[0, 2].astype(bf16)

    blk = lambda part: pl.BlockSpec((S, PAIR), lambda p: (0, part * npair + p))
    own = pl.BlockSpec((S, PAIR), lambda p: (0, p))
    return pl.pallas_call(
        body, name="attn_bwd", out_shape=jax.ShapeDtypeStruct(dqkvz.shape, dqkvz.dtype), grid=(npair,),
        in_specs=[blk(4), blk(5), blk(6), own, own, own,
                  pl.BlockSpec((None,) + _bias_shape(S), lambda p: (p, 0, 0, 0)), ANY_SPEC, ANY_SPEC],
        out_specs=pl.BlockSpec((3, S, PAIR), lambda p: (0, 0, p)), input_output_aliases={7: 0},
        scratch_shapes=[pltpu.VMEM((2, 3, S, PAIR), f32), pltpu.VMEM((S, PAIR), f32), pltpu.VMEM((6, S, PAIR), f32)],
        compiler_params=_params("arbitrary"),
    )(proj, proj, proj, o, do, lse, bias, dqkvz, dep)


def _prenorm_bwd(dh, x, dout, scale, g_pre, tr=256):
    S, D = x.shape
    tr = min(tr, S)

    def body(dh_ref, x_ref, dout_ref, sc_ref, g_ref, gx_ref, sums_ref):
        i = pl.program_id(0)
        xv, dhv = x_ref[...], dh_ref[...]
        r = lax.rsqrt(jnp.mean(xv * xv, axis=-1, keepdims=True) + EPS)
        xn = xv * r
        dxn = dhv * (g_ref[...] * (1.0 + sc_ref[...]))
        gx_ref[...] = dout_ref[...] + r * (dxn - xn * jnp.mean(dxn * xn, axis=-1, keepdims=True))
        dhx = dhv * xn
        row = lax.broadcasted_iota(jnp.int32, (8, D), 0)
        upd = jnp.where(row == 0, jnp.sum(dhv, axis=0, keepdims=True),
                        jnp.where(row == 1, jnp.sum(dhx, axis=0, keepdims=True) * g_ref[...],
                                  jnp.where(row == 2, jnp.sum(dhx, axis=0, keepdims=True) * (1.0 + sc_ref[...]), 0.0)))

        @pl.when(i == 0)
        def _():
            sums_ref[...] = upd

        @pl.when(i > 0)
        def _():
            sums_ref[...] += upd

    row = pl.BlockSpec((tr, D), lambda i: (i, 0))
    vec = pl.BlockSpec((1, D), lambda i: (0, 0))
    return pl.pallas_call(
        body, name="prenorm_bwd",
        out_shape=(jax.ShapeDtypeStruct((S, D), f32), jax.ShapeDtypeStruct((8, D), f32)), grid=(S // tr,),
        in_specs=[row, row, row, vec, vec], out_specs=(row, pl.BlockSpec((8, D), lambda i: (0, 0))),
        compiler_params=_params("arbitrary"),
    )(dh, x, dout, scale, g_pre)


def _adamw(w, g, m, v):
    m = ADAM_B1 * m + (1.0 - ADAM_B1) * g
    v = ADAM_B2 * v + (1.0 - ADAM_B2) * (g * g)
    m_hat = m / (1.0 - ADAM_B1 ** ADAM_STEP)
    v_hat = v / (1.0 - ADAM_B2 ** ADAM_STEP)
    delta = -ADAM_LR * (m_hat / (jnp.sqrt(v_hat) + ADAM_EPS) + ADAM_WD * w)
    return delta, m, v


def _sum_rows(parts, dep):
    P = parts.shape[1]

    def body(p_ref, dep_ref, o_ref):
        acc = p_ref[0:1, :]
        for j in range(1, NDEV):
            acc = acc + p_ref[j:j + 1, :]
        o_ref[...] = jnp.broadcast_to(acc, (8, P))

    vmem = pl.BlockSpec(memory_space=pltpu.VMEM)
    return pl.pallas_call(body, name="sum_small", out_shape=jax.ShapeDtypeStruct((8, P), f32),
                          in_specs=[vmem, ANY_SPEC], out_specs=vmem, compiler_params=_params())(parts, dep)


def _adamw_small(tot, params):
    given = [p[3] for p in params if not isinstance(p[3], int)]

    def body(tot_ref, *refs):
        given_refs = list(refs[:len(given)])
        ins = refs[len(given):len(given) + 3 * len(params)]
        outs = refs[len(given) + 3 * len(params):]
        for t, (w, _, _, where) in enumerate(params):
            w_ref, m_ref, v_ref = ins[3 * t:3 * t + 3]
            g = tot_ref[0:1, where:where + w.size] if isinstance(where, int) else given_refs.pop(0)[...]
            outs[4 * t][...] = g
            outs[4 * t + 1][...], outs[4 * t + 2][...], outs[4 * t + 3][...] = _adamw(w_ref[...], g, m_ref[...], v_ref[...])

    out_shape = tuple(jax.ShapeDtypeStruct(p[0].shape, f32) for p in params for _ in range(4))
    res = pl.pallas_call(body, name="adamw_small", out_shape=out_shape, compiler_params=_params())(
        tot, *given, *[a for p in params for a in p[:3]])
    return [res[4 * t:4 * t + 4] for t in range(len(params))]


def _adamw_sharded(name, parts, sums_a, sums_b, pick, w, m, v, rows=None, prev=None, tr=128):
    R, Cc = w.shape
    r0, nr = rows or (0, R)
    tr = math.gcd(tr, r0, nr)
    n, b0 = parts.shape[0], r0 // tr

    def body(pick_ref, p_ref, a_ref, b_ref, w_ref, m_ref, v_ref, *rest):
        g_ref, d_ref, nm_ref, nv_ref = rest[-4:]
        g = jnp.where(pick_ref[0] == 1, b_ref[...], a_ref[...]).astype(f32)
        for j in range(n):
            g = g + p_ref[j].astype(f32)
        g_ref[...] = g
        d_ref[...], nm_ref[...], nv_ref[...] = _adamw(w_ref[...], g, m_ref[...], v_ref[...])

    row = pl.BlockSpec((tr, Cc), lambda i, pick: (i + b0, 0))
    mine = pl.BlockSpec((None, tr, Cc), lambda i, pick: (pick[1], i + b0, 0))
    out = jax.ShapeDtypeStruct((R, Cc), f32)
    prev = list(prev or [])
    grid_spec = pltpu.PrefetchScalarGridSpec(
        num_scalar_prefetch=1, grid=(nr // tr,),
        in_specs=[pl.BlockSpec((n, tr, Cc), lambda i, pick: (0, i + b0, 0)), mine, mine, row, row, row]
        + [ANY_SPEC] * len(prev),
        out_specs=(row, row, row, row))
    return pl.pallas_call(
        body, name=name, out_shape=(out, out, out, out), grid_spec=grid_spec,
        input_output_aliases={7 + t: t for t in range(len(prev))}, compiler_params=_params("arbitrary"),
    )(pick, parts, sums_a, sums_b, w, m, v, *prev)


def _adamw_ada(c_t, dmod_cols, w, m, v, dep, tr=512):
    D, W = w.shape
    tr = min(tr, D)

    def body(c_ref, dm_ref, w_ref, m_ref, v_ref, dep_ref, g_ref, d_ref, nm_ref, nv_ref):
        g = lax.dot_general(c_ref[...], dm_ref[...], (((1,), (0,)), ((), ())), preferred_element_type=f32,
                            precision=lax.Precision.HIGHEST)
        g_ref[...] = g
        d_ref[...], nm_ref[...], nv_ref[...] = _adamw(w_ref[...], g, m_ref[...], v_ref[...])

    row = pl.BlockSpec((tr, W), lambda i: (i, 0))
    out = jax.ShapeDtypeStruct((D, W), f32)
    return pl.pallas_call(
        body, name="adamw_ada", out_shape=(out, out, out, out), grid=(D // tr,),
        in_specs=[pl.BlockSpec((tr, NDEV), lambda i: (i, 0)), pl.BlockSpec((NDEV, W), lambda i: (0, 0)), row, row, row,
                  ANY_SPEC],
        out_specs=(row, row, row, row), compiler_params=_params("parallel"),
    )(c_t, dmod_cols, w, m, v, dep)


def kernel(x, c, w_ada, b_ada, g_pre, w_in, conv_w, conv_b, g_conv, g_attn, w_out, g_post, loss_target, m_w_ada, m_b_ada, m_g_pre, m_w_in, m_conv_w, m_conv_b, m_g_conv, m_g_attn, m_w_out, m_g_post, v_w_ada, v_b_ada, v_g_pre, v_w_in, v_conv_w, v_conv_b, v_g_conv, v_g_attn, v_w_out, v_g_post):
    S, D = x.shape[1], x.shape[2]
    C = D // 2
    W = w_ada.shape[2]
    CW = conv_w.shape[2]
    me = 4 * lax.axis_index("x") + 2 * lax.axis_index("y") + lax.axis_index("c")
    x2, tgt = x[0], loss_target[0]
    w_ada2, w_in2, w_out2 = w_ada[0], w_in[0], w_out[0]

    R = D // NDEV
    core = lax.axis_index("c").astype(jnp.int32).reshape(1)

    cw_slab = jnp.zeros((8, CW), f32).at[:3].set(conv_w[0])
    b_cols = lax.dynamic_slice_in_dim(b_ada, me * W, W, axis=1)
    mod_slabs, c_blocks, cw_g = _ada_exchange(c.reshape(D // 128, 128), cw_slab, w_ada2, b_cols)
    c_all = c_blocks.reshape(NDEV, D)
    conv_w_full = jnp.transpose(cw_g, (1, 0, 2)).reshape(8, C)
    mod = mod_slabs[:, 0, :].reshape(1, 3 * D)
    shift, scale, gate = mod[:, :D], mod[:, D:2 * D], mod[:, 2 * D:]

    land_i = lax.dynamic_update_slice(lax.empty((NDEV, D, C), bf16), w_in2.astype(bf16)[None], (me, 0, 0))
    land_o = lax.dynamic_update_slice(lax.empty((NDEV, R, D), bf16), w_out2.astype(bf16)[None], (me, 0, 0))
    wi_send, wi_recv, land_i, w_token = _w_in_start(land_i, [mod_slabs])

    me_arr = me.astype(jnp.int32).reshape(1)
    h = _prenorm(x2, scale, shift, g_pre, w_token)
    land_i = _w_in_sibling(land_i, wi_recv, after=[h])
    proj = _in_proj_part("in_proj_a", h, land_i, None, me_arr, 0, 1, 2)
    x_minus_t = _residual_minus_target(x2, tgt, proj)
    bias = _bias_tiles(_head_slopes(C // HEAD_DIM), S, x_minus_t)

    def landing(rows, cols):
        return lax.dynamic_update_slice(lax.empty((NCHIP, rows, cols), bf16), jnp.zeros((1, rows, cols), bf16),
                                        (me // 2, 0, 0))

    land_go, land_gi = landing(R, D), landing(D, C)
    fi_send, fi_recv, land_i = _w_in_relay(land_i, wi_recv, after=[proj, bias, land_go, land_gi])
    proj = _in_proj_part("in_proj_b", h, land_i, proj, me_arr, 2, 2, 2)
    land_i = _w_in_forwarded(land_i, fi_recv, after=[proj])
    proj = _in_proj_part("in_proj_c", h, land_i, proj, me_arr, 3, 2, 2)
    (di_send, di_recv, wo_send, wo_recv), land_i, land_o = _w_in_diag(land_i, land_o, fi_recv, after=[proj])
    proj = _in_proj_part("in_proj_d", h, land_i, proj, me_arr, 6, 1, 1)
    win_g = _w_in_finish(land_i, wi_send, fi_send, di_send, di_recv, after=[proj])
    proj = _in_proj_part("in_proj_e", h, win_g, proj, me_arr, 7, 1, 1)
    ycat = _conv_fwd(proj, conv_w_full, conv_b, g_conv)
    o, lse = _attn_fwd(proj, bias)
    (fo_send, fo_recv), (land_o,), _ = _weights_forward("w_out_forward", land_o, wo_recv, after=[o])
    ycat = _attn_post(ycat, o, proj, g_attn)
    wout_g = _weights_wait("w_out_wait", land_o, wo_send, wo_recv, fo_send, fo_recv, after=[ycat])
    wout_full = wout_g.reshape(D, D)
    y = _matmul(ycat, wout_full, name="out_proj", out_dtype=f32)
    dy, dout, post_sums = _sandwich(y, x_minus_t, gate, g_post)

    gw_out = _matmul(ycat, dy, name="out_proj_dw", out_dtype=bf16, ta=True).reshape(NDEV, R, D)
    dycat = _matmul(dy, wout_full, name="out_proj_dx", out_dtype=f32, tb=True)
    dpc, conv_sums = _conv_bwd(proj, dycat, conv_w_full, conv_b, g_conv, gw_out)
    gw_c = _matmul(h, dpc, name="in_proj_dw_conv", out_dtype=bf16, ta=True, out_slots=4)
    first_pairs, p1_token = _pairs_start("g_first_pair_start", [gw_out, gw_c])
    do, dpa, attn_sums = _attn_post_bwd(o, proj, dycat, g_attn, p1_token)
    (gw_out, pair_o), (gw_c, pair_c) = _pairs_wait("g_first_pair_wait", first_pairs, after=[do])
    sum_o = _pair_sum("g_out_pair_sum", gw_out, pair_o, core)
    sum_c = _pair_sum("g_conv_pair_sum", gw_c, pair_c, core)
    ((co_send, co_recv, sum_o, land_go), (cc_send, cc_recv, sum_c, land_gi)), cc_token = _chips_start(
        "g_first_chip_start", [(sum_o, land_go, 0), (sum_c, land_gi, 0)])
    dpa = _attn_bwd(proj, o, do, lse, bias, dpa, cc_token)
    gw_a = _matmul(h, dpa, name="in_proj_dw_attn", out_dtype=bf16, ta=True, b_slots=True, out_slots=4)
    pa_send, pa_recv, gw_a, pair_a, pa_token = _pair_start("g_attn_pair_start", gw_a)
    sum_o, land_go = _chip_wait("g_out_chip_wait", sum_o, land_go, co_send, co_recv, 0, after=[pa_token])
    pick_out = jnp.stack([jnp.int32(0), me // 2]).astype(jnp.int32)
    g_w_out, d_w_out, nm_w_out, nv_w_out = _adamw_sharded(
        "adamw_w_out", land_go, sum_o, sum_o, pick_out, w_out2, m_w_out[0], v_w_out[0])
    gw_a, pair_a = _pair_wait("g_attn_pair_wait", gw_a, pair_a, pa_send, pa_recv, after=[g_w_out])
    sum_a = _pair_sum("g_attn_pair_sum", gw_a, pair_a, core)
    ca_send, ca_recv, sum_a, land_gi, ca_token = _chip_start("g_attn_chip_start", sum_a, land_gi, 4)
    dh = _matmul_slabs_t(dpc, dpa, win_g, name="in_proj_dx", dep=ca_token)
    grad_x, pre_sums = _prenorm_bwd(dh, x2, dout, scale, g_pre)

    small = jnp.concatenate([pre_sums[0:1], pre_sums[1:2], post_sums[0:1],
                             pre_sums[2:3], post_sums[1:2],
                             conv_sums[2:3], conv_sums[3:4], conv_sums[4:5],
                             conv_sums[1:2], conv_sums[0:1], attn_sums[0:1]], axis=1)
    small = jnp.concatenate([small.reshape(8 * D // 128, 128), jnp.broadcast_to(post_sums[2:3, :128], (8, 128))])
    gs_send, gs_recv, small, small_all, gs_token = _gather_start("gather_small_start", small)

    pick_in = jnp.stack([me // 4, (me % 4) // 2]).astype(jnp.int32)
    sum_c, land_gi = _chip_wait("g_conv_chip_wait", sum_c, land_gi, cc_send, cc_recv, 0, after=[gs_token])
    sum_a, land_gi = _chip_wait("g_attn_chip_wait", sum_a, land_gi, ca_send, ca_recv, 4, after=[gs_token])
    g_w_in, d_w_in, nm_w_in, nv_w_in = _adamw_sharded(
        "adamw_w_in", land_gi, sum_c, sum_a, pick_in, w_in2, m_w_in[0], v_w_in[0], tr=256)

    small_all = _gather_wait("gather_small_wait", small, small_all, gs_send, gs_recv, after=[g_w_in])
    small_all = small_all.reshape(NDEV, small.size)
    tot = _sum_rows(small_all, gs_token)
    loss = tot[0, 8 * D]
    g_conv_w = lax.dynamic_slice_in_dim(tot[0:1, 5 * D:5 * D + 3 * C].reshape(1, 3, C), me * CW, CW, axis=2)
    ((g_b_ada, d_b_ada, nm_b_ada, nv_b_ada), (g_g_pre, d_g_pre, nm_g_pre, nv_g_pre),
     (g_g_post, d_g_post, nm_g_post, nv_g_post), (g_conv_w, d_conv_w, nm_conv_w, nv_conv_w),
     (g_conv_b, d_conv_b, nm_conv_b, nv_conv_b), (g_g_conv, d_g_conv, nm_g_conv, nv_g_conv),
     (g_g_attn, d_g_attn, nm_g_attn, nv_g_attn)) = _adamw_small(tot, [
         (b_ada, m_b_ada, v_b_ada, 0), (g_pre, m_g_pre, v_g_pre, 3 * D), (g_post, m_g_post, v_g_post, 4 * D),
         (conv_w, m_conv_w, v_conv_w, g_conv_w), (conv_b, m_conv_b, v_conv_b, 5 * D + 3 * C),
         (g_conv, m_g_conv, v_g_conv, 5 * D + 4 * C), (g_attn, m_g_attn, v_g_attn, 5 * D + 5 * C)])

    dmod_cols = lax.dynamic_slice_in_dim(small_all[:, :3 * D], me * W, W, axis=1)
    g_w_ada, d_w_ada, nm_w_ada, nv_w_ada = _adamw_ada(c_all.T, dmod_cols, w_ada2, m_w_ada[0], v_w_ada[0], gs_token)

    return (loss, grad_x[None],
            g_w_ada[None], g_b_ada, g_g_pre, g_w_in[None], g_conv_w, g_conv_b, g_g_conv, g_g_attn, g_w_out[None], g_g_post,
            d_w_ada[None], d_b_ada, d_g_pre, d_w_in[None], d_conv_w, d_conv_b, d_g_conv, d_g_attn, d_w_out[None], d_g_post,
            nm_w_ada[None], nm_b_ada, nm_g_pre, nm_w_in[None], nm_conv_w, nm_conv_b, nm_g_conv, nm_g_attn, nm_w_out[None], nm_g_post,
            nv_w_ada[None], nv_b_ada, nv_g_pre, nv_w_in[None], nv_conv_w, nv_conv_b, nv_g_conv, nv_g_attn, nv_w_out[None], nv_g_post)
```

```python
import functools
import math

import jax
import jax.numpy as jnp
from jax import lax
from jax.experimental import pallas as pl
from jax.experimental.pallas import tpu as pltpu

f32 = jnp.float32
bf16 = jnp.bfloat16

NDEV = 8
HEAD_DIM = 64
PAIR = 2 * HEAD_DIM
BRANCHES = ((128, 1, 1), (512, 4, 1), (2048, 16, 2))
HALF_WIN = 64
EPS = 1e-6
NEG_INF = -1e30
ADAM_LR, ADAM_B1, ADAM_B2, ADAM_EPS, ADAM_WD, ADAM_STEP = 0.001, 0.9, 0.999, 1e-08, 0.01, 10
MESH = pl.DeviceIdType.MESH
VMEM_LIMIT = 56 * 1024 * 1024
HBM_SPEC = pl.BlockSpec(memory_space=pltpu.HBM)
ANY_SPEC = pl.BlockSpec(memory_space=pl.ANY)
SEM_SPEC = pl.BlockSpec(memory_space=pltpu.SEMAPHORE)


def _params(*sem):
    return pltpu.CompilerParams(dimension_semantics=sem or None, vmem_limit_bytes=VMEM_LIMIT)


def _silu(z):
    return z * jax.nn.sigmoid(z)


def _silu_and_slope(z):
    s = jax.nn.sigmoid(z)
    return z * s, s * (1.0 + z * (1.0 - s))


def _my_place():
    x, y, c = lax.axis_index("x"), lax.axis_index("y"), lax.axis_index("c")
    return x, y, c, 4 * x + 2 * y + c


def _peer(x, y, c, k):
    px, py, pc = x ^ (k >> 2 & 1), y ^ (k >> 1 & 1), c ^ (k & 1)
    return (px, py, pc), 4 * px + 2 * py + pc


def _comm_call(name, arrays, sems, new_sems, body, after=(), token=False):
    na, ns, nn, nf = len(arrays), len(sems), len(new_sems), len(after)

    def kern(*refs):
        ins, outs = refs[:na + ns + nf], refs[na + ns + nf:]
        body(ins[:na], ins[na:na + ns], outs[:nn])
        if token:
            outs[nn + na][...] = jnp.zeros((8, 128), f32)

    out_shape = ([pltpu.SemaphoreType.DMA(s) for s in new_sems] + [pltpu.HBM(a.shape, a.dtype) for a in arrays]
                 + ([jax.ShapeDtypeStruct((8, 128), f32)] if token else []))
    out_specs = [SEM_SPEC] * nn + [HBM_SPEC] * na + ([pl.BlockSpec(memory_space=pltpu.VMEM)] if token else [])
    res = pl.pallas_call(
        kern, name=name, out_shape=tuple(out_shape),
        in_specs=[HBM_SPEC] * na + [SEM_SPEC] * ns + [ANY_SPEC] * nf, out_specs=tuple(out_specs),
        input_output_aliases={t: nn + t for t in range(na)},
        compiler_params=pltpu.CompilerParams(has_side_effects=pltpu.SideEffectType.DATAFLOW_SIDE_EFFECTING),
    )(*[pltpu.with_memory_space_constraint(a, pltpu.HBM) for a in arrays], *sems, *after)
    return list(res[:nn]), list(res[nn:nn + na]), (res[nn + na] if token else None)


def _remote(src, dst, send_sem, recv_sem, device):
    return pltpu.make_async_remote_copy(src_ref=src, dst_ref=dst, send_sem=send_sem, recv_sem=recv_sem,
                                        device_id=device, device_id_type=MESH)


def _gather_start(name, src):
    def body(a, s, new):
        (src, land), (send, recv) = a, new
        x, y, c, me = _my_place()
        pltpu.make_async_copy(src, land.at[me], recv.at[0]).start()
        for k in range(1, NDEV):
            peer, _ = _peer(x, y, c, k)
            _remote(src, land.at[me], send.at[k], recv.at[k], peer).start()

    land = lax.empty((NDEV,) + src.shape, src.dtype)
    (send, recv), (src, land), token = _comm_call(name, [src, land], [], [(NDEV,), (NDEV,)], body, token=True)
    return send, recv, src, land, token


def _gather_wait(name, src, land, send, recv, after):
    def body(a, s, new):
        (src, land), (send, recv) = a, s
        x, y, c, me = _my_place()
        pltpu.make_async_copy(src, land.at[me], recv.at[0]).wait()
        for k in range(1, NDEV):
            peer, slot = _peer(x, y, c, k)
            _remote(src, land.at[slot], send.at[k], recv.at[k], peer).wait_recv()
        for k in range(1, NDEV):
            peer, _ = _peer(x, y, c, k)
            _remote(src, land.at[me], send.at[k], recv.at[k], peer).wait_send()

    return _comm_call(name, [src, land], [send, recv], [], body, after=after)[1][1]


SAME_CORE = (2, 4, 6)
VIA_SIBLING = (3, 5, 7)


def _weights_forward(name, land, recv, after):
    def body(a, s, new):
        (land,), (recv,), (fsend, frecv) = a, s, new
        x, y, c, me = _my_place()
        sibling, _ = _peer(x, y, c, 1)
        for k in SAME_CORE:
            peer, slot = _peer(x, y, c, k)
            _remote(land.at[slot], land.at[slot], fsend.at[k], recv.at[k], peer).wait_recv()
            _remote(land.at[slot], land.at[slot], fsend.at[k], frecv.at[k ^ 1], sibling).start()

    return _comm_call(name, [land], [recv], [(NDEV,), (NDEV,)], body, after=after)


def _weights_wait(name, land, send, recv, fsend, frecv, after):
    def body(a, s, new):
        (land,), (send, recv, fsend, frecv) = a, s
        x, y, c, me = _my_place()
        sibling, sib_slot = _peer(x, y, c, 1)
        _remote(land.at[sib_slot], land.at[sib_slot], send.at[1], recv.at[1], sibling).wait_recv()
        for k in VIA_SIBLING:
            _, slot = _peer(x, y, c, k)
            _remote(land.at[slot], land.at[slot], fsend.at[k ^ 1], frecv.at[k], sibling).wait_recv()
        for k in (1,) + SAME_CORE:
            peer, _ = _peer(x, y, c, k)
            _remote(land.at[me], land.at[me], send.at[k], recv.at[k], peer).wait_send()
        for k in SAME_CORE:
            _, slot = _peer(x, y, c, k)
            _remote(land.at[slot], land.at[slot], fsend.at[k], frecv.at[k ^ 1], sibling).wait_send()

    return _comm_call(name, [land], [send, recv, fsend, frecv], [], body, after=after)[1][0]


def _diag_relay(x, y, c):
    slot = 4 * (x ^ (1 - c)) + 2 * (y ^ c) + c
    return slot, (x ^ c, y ^ (1 - c), c)


def _w_in_start(land, after):
    def body(a, s, new):
        (land,), (send, recv) = a, new
        x, y, c, me = _my_place()
        for k in (1, 2, 4):
            peer, _ = _peer(x, y, c, k)
            _remote(land.at[me], land.at[me], send.at[k], recv.at[k], peer).start()

    (send, recv), (land,), token = _comm_call("w_in_start", [land], [], [(NDEV,), (NDEV,)], body, after=after, token=True)
    return send, recv, land, token


def _w_in_sibling(land, recv, after):
    def body(a, s, new):
        (land,), (recv,) = a, s
        x, y, c, me = _my_place()
        sibling, slot = _peer(x, y, c, 1)
        _remote(land.at[slot], land.at[slot], recv.at[1], recv.at[1], sibling).wait_recv()

    return _comm_call("w_in_sibling", [land], [recv], [], body, after=after)[1][0]


def _w_in_relay(land, recv, after):
    def body(a, s, new):
        (land,), (recv,), (fsend, frecv) = a, s, new
        x, y, c, me = _my_place()
        sibling, _ = _peer(x, y, c, 1)
        for k in (2, 4):
            peer, slot = _peer(x, y, c, k)
            _remote(land.at[slot], land.at[slot], fsend.at[k], recv.at[k], peer).wait_recv()
        slot, target = _diag_relay(x, y, c)
        _remote(land.at[slot], land.at[slot], fsend.at[6], frecv.at[6], target).start()
        for k in (2, 4):
            _, slot = _peer(x, y, c, k)
            _remote(land.at[slot], land.at[slot], fsend.at[k], frecv.at[k ^ 1], sibling).start()

    (fsend, frecv), (land,), _ = _comm_call("w_in_relay", [land], [recv], [(NDEV,), (NDEV,)], body, after=after)
    return fsend, frecv, land


def _w_in_forwarded(land, frecv, after):
    def body(a, s, new):
        (land,), (frecv,) = a, s
        x, y, c, me = _my_place()
        sibling, _ = _peer(x, y, c, 1)
        for k in (3, 5):
            _, slot = _peer(x, y, c, k)
            _remote(land.at[slot], land.at[slot], frecv.at[k], frecv.at[k], sibling).wait_recv()

    return _comm_call("w_in_forwarded", [land], [frecv], [], body, after=after)[1][0]


def _w_in_diag(land, land_o, frecv, after):
    def body(a, s, new):
        (land, land_o), (frecv,), (dsend, drecv, osend, orecv) = a, s, new
        x, y, c, me = _my_place()
        sibling, _ = _peer(x, y, c, 1)
        peer, slot = _peer(x, y, c, 6)
        _remote(land.at[slot], land.at[slot], dsend.at[6], frecv.at[6], peer).wait_recv()
        _remote(land.at[slot], land.at[slot], dsend.at[6], drecv.at[7], sibling).start()
        for k in (1,) + SAME_CORE:
            peer, _ = _peer(x, y, c, k)
            _remote(land_o.at[me], land_o.at[me], osend.at[k], orecv.at[k], peer).start()

    sems, (land, land_o), _ = _comm_call("w_in_diag", [land, land_o], [frecv], [(NDEV,)] * 4, body, after=after)
    return sems, land, land_o


def _w_in_finish(land, send, fsend, dsend, drecv, after):
    def body(a, s, new):
        (land,), (send, fsend, dsend, drecv) = a, s
        x, y, c, me = _my_place()
        sibling, _ = _peer(x, y, c, 1)
        _, slot = _peer(x, y, c, 7)
        _remote(land.at[slot], land.at[slot], dsend.at[6], drecv.at[7], sibling).wait_recv()
        for k in (1, 2, 4):
            peer, _ = _peer(x, y, c, k)
            _remote(land.at[me], land.at[me], send.at[k], send.at[k], peer).wait_send()
        for k in (2, 4, 6):
            _, slot = _peer(x, y, c, k)
            _remote(land.at[slot], land.at[slot], fsend.at[k], fsend.at[k], sibling).wait_send()
        _, slot = _peer(x, y, c, 6)
        _remote(land.at[slot], land.at[slot], dsend.at[6], dsend.at[6], sibling).wait_send()

    return _comm_call("w_in_finish", [land], [send, fsend, dsend, drecv], [], body, after=after)[1][0]


def _in_proj_part(name, h, land, proj, me_arr, k0, kstep, nk, tm=512):
    S, D = h.shape
    C = land.shape[2]
    tm = min(tm, S)

    def body(me_ref, a_ref, b_ref, *rest):
        rest[-1][...] = jnp.dot(a_ref[...], b_ref[...], preferred_element_type=f32)

    slot = lambda j, me: me[0] ^ (k0 + kstep * j)
    args = [h, land] + ([] if proj is None else [proj])
    grid_spec = pltpu.PrefetchScalarGridSpec(
        num_scalar_prefetch=1, grid=(nk, S // tm),
        in_specs=[pl.BlockSpec((tm, D), lambda j, i, me: (i, 0)),
                  pl.BlockSpec((None, D, C), lambda j, i, me: (slot(j, me), 0, 0))] + [ANY_SPEC] * (len(args) - 2),
        out_specs=pl.BlockSpec((tm, C), lambda j, i, me: (i, slot(j, me))))
    return pl.pallas_call(
        body, name=name, out_shape=jax.ShapeDtypeStruct((S, NDEV * C), f32), grid_spec=grid_spec,
        input_output_aliases={} if proj is None else {3: 0}, compiler_params=_params("arbitrary", "arbitrary"),
    )(me_arr, *args)


NCHIP = NDEV // 2


def _pairs_start(name, srcs):
    n = len(srcs)
    npairs = [src.shape[0] // 2 for src in srcs]

    def body(a, s, new):
        x, y, c, me = _my_place()
        sibling, _ = _peer(x, y, c, 1)
        for t in range(n):
            src, pair, send, recv = a[t], a[n + t], new[t], new[n + t]
            for i in range(npairs[t]):
                _remote(src.at[2 * i + 1 - c], pair.at[i], send.at[i], recv.at[i], sibling).start()

    pairs = [lax.empty((npairs[t],) + srcs[t].shape[1:], srcs[t].dtype) for t in range(n)]
    sems, arrays, token = _comm_call(name, list(srcs) + pairs, [], [(m,) for m in npairs] * 2, body, token=True)
    return [(sems[t], sems[n + t], arrays[t], arrays[n + t]) for t in range(n)], token


def _pairs_wait(name, groups, after):
    n = len(groups)

    def body(a, s, new):
        x, y, c, me = _my_place()
        sibling, _ = _peer(x, y, c, 1)
        for t in range(n):
            src, pair, send, recv = a[t], a[n + t], s[t], s[n + t]
            for i in range(pair.shape[0]):
                cp = _remote(src.at[2 * i + 1 - c], pair.at[i], send.at[i], recv.at[i], sibling)
                cp.wait_recv()
                cp.wait_send()

    arrays = _comm_call(name, [g[2] for g in groups] + [g[3] for g in groups],
                        [g[0] for g in groups] + [g[1] for g in groups], [], body, after=after)[1]
    return [(arrays[t], arrays[n + t]) for t in range(n)]


def _pair_start(name, src):
    ((send, recv, src, pair),), token = _pairs_start(name, [src])
    return send, recv, src, pair, token


def _pair_wait(name, src, pair, send, recv, after):
    return _pairs_wait(name, [(send, recv, src, pair)], after)[0]


def _pair_sum(name, src, pair, core, tr=1024):
    npair, R, Cc = pair.shape
    tr = min(tr, R)

    def body(core_ref, a_ref, b_ref, o_ref):
        o_ref[...] = (a_ref[...].astype(f32) + b_ref[...].astype(f32)).astype(o_ref.dtype)

    grid_spec = pltpu.PrefetchScalarGridSpec(
        num_scalar_prefetch=1, grid=(npair, R // tr),
        in_specs=[pl.BlockSpec((None, tr, Cc), lambda i, r, core: (2 * i + core[0], r, 0)),
                  pl.BlockSpec((None, tr, Cc), lambda i, r, core: (i, r, 0))],
        out_specs=pl.BlockSpec((None, tr, Cc), lambda i, r, core: (i, r, 0)))
    return pl.pallas_call(body, name=name, out_shape=jax.ShapeDtypeStruct(pair.shape, pair.dtype),
                          grid_spec=grid_spec, compiler_params=_params("parallel", "parallel"))(core, src, pair)


def _owner_chip(first, i):
    q = first // 2 + i
    return q >> 1 & 1, q & 1


def _chips_start(name, groups, rows=None, after=()):
    n = len(groups)
    row_of = [pl.ds(*(rows or (0, g[0].shape[1]))) for g in groups]

    def body(a, s, new):
        x, y, c, me = _my_place()
        for t, (_, _, first) in enumerate(groups):
            sums, land, send, recv = a[t], a[n + t], new[t], new[n + t]
            for i in range(sums.shape[0]):
                ox, oy = _owner_chip(first, i)

                @pl.when((x != ox) | (y != oy))
                def _():
                    _remote(sums.at[i, row_of[t]], land.at[2 * x + y, row_of[t]], send.at[i], recv.at[2 * x + y],
                            (ox, oy, c)).start()

    sems, arrays, token = _comm_call(name, [g[0] for g in groups] + [g[1] for g in groups], [],
                                     [(g[0].shape[0],) for g in groups] + [(NCHIP,)] * n, body, after=after, token=True)
    return [(sems[t], sems[n + t], arrays[t], arrays[n + t]) for t in range(n)], token


def _chips_wait(name, groups, firsts, after, rows=None):
    n = len(groups)
    row_of = [pl.ds(*(rows or (0, g[2].shape[1]))) for g in groups]

    def body(a, s, new):
        x, y, c, me = _my_place()
        for t in range(n):
            sums, land, send, recv, first = a[t], a[n + t], s[t], s[n + t], firsts[t]
            npair = sums.shape[0]
            mine = (me >= first) & (me < first + 2 * npair)
            for i in range(npair):
                ox, oy = _owner_chip(first, i)

                @pl.when((x != ox) | (y != oy))
                def _():
                    _remote(sums.at[i, row_of[t]], land.at[2 * x + y, row_of[t]], send.at[i], recv.at[2 * x + y],
                            (ox, oy, c)).wait_send()
            for q in range(NCHIP):
                @pl.when(mine & (2 * x + y != q))
                def _():
                    _remote(sums.at[0, row_of[t]], land.at[q, row_of[t]], send.at[0], recv.at[q],
                            (q >> 1, q & 1, c)).wait_recv()

    arrays = _comm_call(name, [g[2] for g in groups] + [g[3] for g in groups],
                        [g[0] for g in groups] + [g[1] for g in groups], [], body, after=after)[1]
    return [(arrays[t], arrays[n + t]) for t in range(n)]


def _chip_start(name, sums, land, first, rows=None, after=()):
    ((send, recv, sums, land),), token = _chips_start(name, [(sums, land, first)], rows, after)
    return send, recv, sums, land, token


def _chip_wait(name, sums, land, send, recv, first, after, rows=None):
    return _chips_wait(name, [(send, recv, sums, land)], [first], after, rows)[0]


def _matmul(a, b, *, name, out_dtype, ta=False, tb=False, b_slots=False, out_slots=0, b_cols=None,
            tm=1024, tn=1024, tk=2048, dep=None):
    M, K = (a.shape[1], a.shape[0]) if ta else a.shape
    col0 = 0
    if b_slots:
        slab = b.shape[2]
        N = b.shape[1] if tb else b.shape[0] * slab
        assert (K if tb else N) == b.shape[0] * slab
    elif b_cols is not None:
        assert not tb
        col0, N = b_cols
    else:
        N = b.shape[0] if tb else b.shape[1]
    tm, tn, tk = min(tm, M), min(tn, N), min(tk, K)
    if b_slots:
        if tb:
            tk = min(tk, slab)
        else:
            tn = min(tn, slab)
    if out_slots:
        tn = min(tn, N // out_slots)
    nm, nn, nk = M // tm, N // tn, K // tk
    assert (nm * tm, nn * tn, nk * tk) == (M, N, K) and col0 % tn == 0, (name, M, N, K, tm, tn, tk)
    j0 = col0 // tn

    a_spec = pl.BlockSpec((tk, tm), lambda i, j, k: (k, i)) if ta else pl.BlockSpec((tm, tk), lambda i, j, k: (i, k))
    if b_slots and tb:
        per = slab // tk
        b_spec = pl.BlockSpec((None, tn, tk), lambda i, j, k: (k // per, j, k % per))
    elif b_slots:
        per = slab // tn
        b_spec = pl.BlockSpec((None, tk, tn), lambda i, j, k: (j // per, k, j % per))
    elif tb:
        b_spec = pl.BlockSpec((tn, tk), lambda i, j, k: (j, k))
    else:
        b_spec = pl.BlockSpec((tk, tn), lambda i, j, k: (k, j + j0))
    if out_slots:
        per_o = (N // out_slots) // tn
        o_spec = pl.BlockSpec((None, tm, tn), lambda i, j, k: (j // per_o, i, j % per_o))
        out_shape = jax.ShapeDtypeStruct((out_slots, M, N // out_slots), out_dtype)
    else:
        o_spec = pl.BlockSpec((tm, tn), lambda i, j, k: (i, j))
        out_shape = jax.ShapeDtypeStruct((M, N), out_dtype)
    dims = (((0 if ta else 1,), (1 if tb else 0,)), ((), ()))
    deps = [] if dep is None else [dep]

    def body(a_ref, b_ref, *rest):
        o_ref = rest[len(deps)]
        prod = lax.dot_general(a_ref[...], b_ref[...], dims, preferred_element_type=f32)
        if nk == 1:
            o_ref[...] = prod.astype(out_dtype)
            return
        acc_ref = rest[len(deps) + 1]
        k = pl.program_id(2)

        @pl.when(k == 0)
        def _():
            acc_ref[...] = prod

        @pl.when((k > 0) & (k < nk - 1))
        def _():
            acc_ref[...] += prod

        @pl.when(k == nk - 1)
        def _():
            o_ref[...] = (acc_ref[...] + prod).astype(out_dtype)

    return pl.pallas_call(
        body, name=name, out_shape=out_shape, grid=(nm, nn, nk),
        in_specs=[a_spec, b_spec] + [ANY_SPEC] * len(deps), out_specs=o_spec,
        scratch_shapes=[pltpu.VMEM((tm, tn), f32)] if nk > 1 else [],
        compiler_params=_params("parallel", "parallel", "arbitrary"),
    )(a, b, *deps)


def _matmul_slabs_t(a_cols, a_slots, b, *, name, tm=512, tn=512, dep=None):
    M = a_cols.shape[0]
    n_slab, N, slab = b.shape
    n1, n2 = a_cols.shape[1] // slab, a_slots.shape[0]
    assert n1 + n2 == n_slab and a_slots.shape[1:] == (M, slab)
    tm, tn = min(tm, M), min(tn, N)
    deps = [] if dep is None else [dep]

    def body(a1_ref, a2_ref, b_ref, *rest):
        o_ref = rest[len(deps)]
        acc = None
        for s in range(n_slab):
            lhs = a1_ref[:, s * slab:(s + 1) * slab] if s < n1 else a2_ref[s - n1]
            prod = lax.dot_general(lhs, b_ref[s], (((1,), (1,)), ((), ())), preferred_element_type=f32)
            acc = prod if acc is None else acc + prod
        o_ref[...] = acc

    return pl.pallas_call(
        body, name=name, out_shape=jax.ShapeDtypeStruct((M, N), f32), grid=(M // tm, N // tn),
        in_specs=[pl.BlockSpec((tm, n1 * slab), lambda i, j: (i, 0)), pl.BlockSpec((n2, tm, slab), lambda i, j: (0, i, 0)),
                  pl.BlockSpec((n_slab, tn, slab), lambda i, j: (0, j, 0))] + [ANY_SPEC] * len(deps),
        out_specs=pl.BlockSpec((tm, tn), lambda i, j: (i, j)), compiler_params=_params("parallel", "parallel"),
    )(a_cols, a_slots, b, *deps)


def _ada_exchange(c_blk, cw_slab, w_ada, b_cols):
    nblk = c_blk.shape[0]
    D, W = w_ada.shape
    CW = cw_slab.shape[1]

    def body(c_ref, cw_ref, w_ref, b_ref, mod_ref, call_ref, cwg_ref, msend, send_sems, recv_sems):
        x, y, c, me = _my_place()
        call_ref[me] = _silu(c_ref[...])
        cwg_ref[me] = cw_ref[...]
        first = []
        for k in range(1, NDEV):
            peer, _ = _peer(x, y, c, k)
            first.append(_remote(call_ref.at[me], call_ref.at[me], send_sems.at[0, k], recv_sems.at[0, k], peer))
            first.append(_remote(cwg_ref.at[me], cwg_ref.at[me], send_sems.at[1, k], recv_sems.at[1, k], peer))
        for cp in first:
            cp.start()
        for k in range(1, NDEV):
            peer, slot = _peer(x, y, c, k)
            _remote(call_ref.at[slot], call_ref.at[slot], send_sems.at[0, k], recv_sems.at[0, k], peer).wait_recv()
            _remote(cwg_ref.at[slot], cwg_ref.at[slot], send_sems.at[1, k], recv_sems.at[1, k], peer).wait_recv()
        mod = jnp.broadcast_to(b_ref[...], (NDEV, W))
        for r in range(nblk):
            mod = mod + lax.dot_general(call_ref[:, r, :], w_ref[r * 128:(r + 1) * 128, :], (((1,), (0,)), ((), ())),
                                        preferred_element_type=f32, precision=lax.Precision.HIGHEST)
        row = lax.broadcasted_iota(jnp.int32, (NDEV, 1), 0)
        pick = lambda j: jnp.broadcast_to(jnp.sum(jnp.where(row == j, mod, 0.0), axis=0, keepdims=True), (8, W))
        mod_ref[me] = pick(me)
        second = []
        for k in range(1, NDEV):
            peer, slot = _peer(x, y, c, k)
            msend[k] = pick(slot)
            second.append(_remote(msend.at[k], mod_ref.at[me], send_sems.at[2, k], recv_sems.at[2, k], peer))
        for cp in second:
            cp.start()
        for k in range(1, NDEV):
            peer, slot = _peer(x, y, c, k)
            _remote(msend.at[k], mod_ref.at[slot], send_sems.at[2, k], recv_sems.at[2, k], peer).wait_recv()
        for cp in first + second:
            cp.wait_send()

    vmem = pl.BlockSpec(memory_space=pltpu.VMEM)
    return pl.pallas_call(
        body, name="ada_exchange",
        out_shape=(jax.ShapeDtypeStruct((NDEV, 8, W), f32), jax.ShapeDtypeStruct((NDEV, nblk, 128), f32),
                   jax.ShapeDtypeStruct((NDEV, 8, CW), f32)),
        in_specs=[vmem] * 4, out_specs=(vmem, vmem, vmem),
        scratch_shapes=[pltpu.VMEM((NDEV, 8, W), f32), pltpu.SemaphoreType.DMA((3, NDEV)),
                        pltpu.SemaphoreType.DMA((3, NDEV))],
        compiler_params=_params(),
    )(c_blk, cw_slab, w_ada, b_cols)


def _prenorm(x, scale, shift, g_pre, dep, tr=512):
    S, D = x.shape
    tr = min(tr, S)

    def body(x_ref, sc_ref, sh_ref, g_ref, dep_ref, h_ref):
        xv = x_ref[...]
        r = lax.rsqrt(jnp.mean(xv * xv, axis=-1, keepdims=True) + EPS)
        h_ref[...] = ((xv * r) * g_ref[...] * (1.0 + sc_ref[...]) + sh_ref[...]).astype(bf16)

    row = pl.BlockSpec((tr, D), lambda i: (i, 0))
    vec = pl.BlockSpec((1, D), lambda i: (0, 0))
    return pl.pallas_call(body, name="prenorm", out_shape=jax.ShapeDtypeStruct((S, D), bf16), grid=(S // tr,),
                          in_specs=[row, vec, vec, vec, ANY_SPEC], out_specs=row, compiler_params=_params("parallel"))(
                              x, scale, shift, g_pre, dep)


def _ext_rows(i, tr, S):
    g = lax.broadcasted_iota(jnp.int32, (tr + 16, 1), 0) + (i * tr - 8)
    return (g >= 0) & (g < S)


def _halo_specs(tr, S, C, col):
    nb8 = S // 8
    main = pl.BlockSpec((tr, C), lambda i: (i, col))
    prev = pl.BlockSpec((8, C), lambda i: (jnp.maximum(i * (tr // 8) - 1, 0), col))
    nxt = pl.BlockSpec((8, C), lambda i: (jnp.minimum((i + 1) * (tr // 8), nb8 - 1), col))
    return prev, main, nxt


def _conv_fwd(proj, conv_w, conv_b, g_conv, tr=512):
    S, C = proj.shape[0], proj.shape[1] // 8
    tr = min(tr, S)

    def body(up, um, un, cp, cm, cn, bg_ref, zc_ref, w_ref, cb_ref, g_ref, o_ref):
        i = pl.program_id(0)
        exists = _ext_rows(i, tr, S)
        u = jnp.concatenate([up[...], um[...], un[...]], axis=0)
        cg = jnp.concatenate([cp[...], cm[...], cn[...]], axis=0)
        t = jnp.where(exists, cg * u, 0.0)
        t_before = pltpu.roll(t, 1, 0)[8:tr + 8]
        t_after = pltpu.roll(t, tr + 15, 0)[8:tr + 8]
        w = w_ref[...]
        cv = w[0:1] * t_before + w[1:2] * t[8:tr + 8] + w[2:3] * t_after + cb_ref[...]
        yc = bg_ref[...] * cv
        rc = lax.rsqrt(jnp.mean(yc * yc, axis=-1, keepdims=True) + EPS)
        o_ref[...] = ((yc * rc) * g_ref[...] * _silu(zc_ref[...])).astype(bf16)

    u_specs = _halo_specs(tr, S, C, 0)
    c_specs = _halo_specs(tr, S, C, 2)
    vec = pl.BlockSpec((1, C), lambda i: (0, 0))
    return pl.pallas_call(
        body, name="conv_fwd", out_shape=jax.ShapeDtypeStruct((S, 2 * C), bf16), grid=(S // tr,),
        in_specs=[*u_specs, *c_specs, pl.BlockSpec((tr, C), lambda i: (i, 1)), pl.BlockSpec((tr, C), lambda i: (i, 3)),
                  pl.BlockSpec((8, C), lambda i: (0, 0)), vec, vec],
        out_specs=pl.BlockSpec((tr, C), lambda i: (i, 0)), compiler_params=_params("parallel"),
    )(proj, proj, proj, proj, proj, proj, proj, proj, conv_w, conv_b, g_conv)


def _branch_geometry(S, r, inter):
    L = S // r * inter
    nq = min(128, L)
    nk = min(nq + 2 * HALF_WIN * inter, L)
    assert L % nq == 0 and (L == nk or L >= nq + 2 * HALF_WIN * inter)
    return L, nq, nk, L // nq


QUAD = 4


def _to_quad(dst, src, S):
    n = S // QUAD
    for rho in range(QUAD):
        dst[pl.ds(rho * n, n), :] = src[pl.ds(rho, n, stride=QUAD), :]


def _block_rows(idx, r, inter, S, L, nq, nk, nblk):
    rho, qb = (0, idx) if r == 1 else (idx // nblk, idx % nblk)
    i0 = qb * nq
    ws = jnp.clip(i0 - HALF_WIN * inter, 0, L - nk)
    if r == 1:
        return pl.ds(pl.multiple_of(i0, 8), nq), pl.ds(pl.multiple_of(ws, 8), nk), i0 - ws
    assert r % (QUAD * inter) == 0
    step = r // QUAD // inter
    base = (rho % QUAD) * (S // QUAD) + rho // QUAD
    if step == 1:
        return pl.ds(pl.multiple_of(base + i0, 8), nq), pl.ds(pl.multiple_of(base + ws, 8), nk), i0 - ws
    return pl.ds(base + step * i0, nq, stride=step), pl.ds(base + step * ws, nk, stride=step), i0 - ws


N_CASES = 3
SCALE = HEAD_DIM ** -0.5
ATTN_UNROLL = 16


def _bias_shape(S):
    shapes = [_branch_geometry(S, r, inter)[1:3] for _, r, inter in BRANCHES]
    return (len(BRANCHES) * N_CASES * 2, max(nq for nq, _ in shapes), max(nk for _, nk in shapes))


def _bias_index(b, case, head):
    return (b * N_CASES + case) * 2 + head


def _fill_bias(bias_scr, sl_ref, S):
    sl = sl_ref[...]
    slope = (sl[0:1, 0:1], sl[0:1, HEAD_DIM:HEAD_DIM + 1])
    for b, (_, r, inter) in enumerate(BRANCHES):
        L, nq, nk, nblk = _branch_geometry(S, r, inter)
        rel = lax.broadcasted_iota(jnp.int32, (nq, nk), 0) - lax.broadcasted_iota(jnp.int32, (nq, nk), 1)
        for case in range(N_CASES):
            d = jnp.abs(rel + case * HALF_WIN)
            valid = d <= HALF_WIN * inter
            if inter > 1:
                valid = valid & (jnp.bitwise_and(d, inter - 1) == 0)
            dist = d.astype(f32) * float(r // inter)
            for head in range(2):
                bias_scr[_bias_index(b, case, head), 0:nq, 0:nk] = jnp.where(valid, -slope[head] * dist, NEG_INF)


def _bias_tiles(slopes, S, dep):
    npair = slopes.shape[0]
    shape = _bias_shape(S)

    def body(sl_ref, dep_ref, o_ref):
        _fill_bias(o_ref, sl_ref, S)

    return pl.pallas_call(
        body, name="bias_tiles", out_shape=jax.ShapeDtypeStruct((npair,) + shape, f32), grid=(npair,),
        in_specs=[pl.BlockSpec((None, 8, PAIR), lambda p: (p, 0, 0)), ANY_SPEC],
        out_specs=pl.BlockSpec((None,) + shape, lambda p: (p, 0, 0, 0)), compiler_params=_params("parallel"),
    )(slopes, dep)


def _head_slopes(n_heads):
    slopes = 2.0 ** (-8.0 * jnp.arange(1, n_heads + 1, dtype=f32) / n_heads)
    return jnp.broadcast_to(jnp.repeat(slopes.reshape(n_heads // 2, 2), HEAD_DIM, axis=1)[:, None, :],
                            (n_heads // 2, 8, PAIR))


def _attn_fwd(proj, bias):
    S, C = proj.shape[0], proj.shape[1] // 8
    npair = C // PAIR

    def body(q_ref, k_ref, v_ref, bias_scr, o_ref, lse_ref, m_scr, l_scr, a_scr, q4_scr, k4_scr, v4_scr):
        lane = lax.broadcasted_iota(jnp.int32, (1, PAIR), 1)
        first = lane < HEAD_DIM
        for dst, src in ((q4_scr, q_ref), (k4_scr, k_ref), (v4_scr, v_ref)):
            _to_quad(dst, src, S)

        for b, (_, r, inter) in enumerate(BRANCHES):
            L, nq, nk, nblk = _branch_geometry(S, r, inter)
            qs, ks, vs = (q_ref, k_ref, v_ref) if r == 1 else (q4_scr, k4_scr, v4_scr)

            def step(idx, carry, b=b, r=r, L=L, nq=nq, nk=nk, nblk=nblk, qs=qs, ks=ks, vs=vs):
                qrows, krows, off = _block_rows(idx, r, inter, S, L, nq, nk, nblk)
                case = off // HALF_WIN
                q2 = qs[qrows, :] * SCALE
                k2 = ks[krows, :].astype(bf16)
                v2 = vs[krows, :].astype(bf16)
                ms, accs = [], []
                for hh in range(2):
                    mine = first if hh == 0 else ~first
                    qh = jnp.where(mine, q2, 0.0).astype(bf16)
                    s = lax.dot_general(qh, k2, (((1,), (1,)), ((), ())), preferred_element_type=f32)
                    s = s + bias_scr[_bias_index(b, case, hh), 0:nq, 0:nk]
                    m = jnp.max(s, axis=-1, keepdims=True)
                    p = jnp.exp(s - m).astype(bf16)
                    vh = jnp.where(mine, v2, jnp.ones_like(v2))
                    ms.append(m)
                    accs.append(jnp.dot(p, vh, preferred_element_type=f32))
                m_scr[b, qrows, :] = jnp.where(first, ms[0], ms[1])
                a_scr[b, qrows, :] = jnp.where(first, accs[0], accs[1])
                l_scr[b, qrows, :] = jnp.where(first, accs[1], accs[0])
                return carry

            lax.fori_loop(0, S // nq, step, 0, unroll=min(ATTN_UNROLL, S // nq))

        n4 = S // QUAD
        ch = min(256, n4)
        nch = n4 // ch

        def merge(i, carry):
            rho, part = i // nch, i % nch
            sorted_rows = pl.ds(pl.multiple_of(rho * n4 + part * ch, 8), ch)
            token_rows = pl.ds(rho + QUAD * part * ch, ch, stride=QUAD)
            rows = (token_rows,) + (sorted_rows,) * (len(BRANCHES) - 1)
            ms = [m_scr[b, rows[b], :] for b in range(len(BRANCHES))]
            m = functools.reduce(jnp.maximum, ms)
            l = jnp.zeros((ch, PAIR), f32)
            acc = jnp.zeros((ch, PAIR), f32)
            for b in range(len(BRANCHES)):
                w = jnp.exp(ms[b] - m)
                l = l + w * pltpu.roll(l_scr[b, rows[b], :], HEAD_DIM, 1)
                acc = acc + w * a_scr[b, rows[b], :]
            o_ref[token_rows, :] = acc / l
            lse_ref[token_rows, :] = m + jnp.log(l)
            return carry

        lax.fori_loop(0, QUAD * nch, merge, 0, unroll=2)

    blk = lambda part: pl.BlockSpec((S, PAIR), lambda p: (0, part * npair + p))
    out = pl.BlockSpec((S, PAIR), lambda p: (0, p))
    return pl.pallas_call(
        body, name="attn_fwd",
        out_shape=(jax.ShapeDtypeStruct((S, C), f32), jax.ShapeDtypeStruct((S, C), f32)), grid=(npair,),
        in_specs=[blk(4), blk(5), blk(6), pl.BlockSpec((None,) + _bias_shape(S), lambda p: (p, 0, 0, 0))],
        out_specs=(out, out),
        scratch_shapes=[pltpu.VMEM((3, S, PAIR), f32)] * 3 + [pltpu.VMEM((S, PAIR), f32)] * 3,
        compiler_params=_params("parallel"),
    )(proj, proj, proj, bias)


def _attn_post(ycat, o, proj, g_attn, tr=512):
    S, C = o.shape
    tr = min(tr, S)

    def body(y_ref, o_ref, z_ref, g_ref, out_ref):
        del y_ref
        ov = o_ref[...]
        ra = lax.rsqrt(jnp.mean(ov * ov, axis=-1, keepdims=True) + EPS)
        out_ref[...] = ((ov * ra) * g_ref[...] * _silu(z_ref[...])).astype(bf16)

    return pl.pallas_call(
        body, name="attn_post", out_shape=jax.ShapeDtypeStruct(ycat.shape, ycat.dtype), grid=(S // tr,),
        in_specs=[HBM_SPEC, pl.BlockSpec((tr, C), lambda i: (i, 0)), pl.BlockSpec((tr, C), lambda i: (i, 7)),
                  pl.BlockSpec((1, C), lambda i: (0, 0))],
        out_specs=pl.BlockSpec((tr, C), lambda i: (i, 1)), input_output_aliases={0: 0},
        compiler_params=_params("arbitrary"),
    )(ycat, o, proj, g_attn)


def _residual_minus_target(x, target, dep, tr=512):
    S, D = x.shape
    tr = min(tr, S)

    def body(x_ref, t_ref, dep_ref, o_ref):
        o_ref[...] = x_ref[...] - t_ref[...]

    row = pl.BlockSpec((tr, D), lambda i: (i, 0))
    return pl.pallas_call(body, name="residual_minus_target", out_shape=jax.ShapeDtypeStruct((S, D), f32),
                          grid=(S // tr,), in_specs=[row, row, ANY_SPEC], out_specs=row,
                          compiler_params=_params("parallel"))(x, target, dep)


def _sandwich(y, x_minus_t, gate, g_post, tr=256):
    S, D = y.shape
    tr = min(tr, S)

    def body(y_ref, xt_ref, gate_ref, g_ref, dy_ref, dout_ref, sums_ref):
        i = pl.program_id(0)
        gate, g = gate_ref[...], g_ref[...]
        gg = gate * g
        yv = y_ref[...]
        rp = lax.rsqrt(jnp.mean(yv * yv, axis=-1, keepdims=True) + EPS)
        yhat = yv * rp
        err = xt_ref[...] + gg * yhat
        dout = err * (1.0 / D)
        dout_ref[...] = dout
        q = dout * yhat
        w = dout * gg
        dy_ref[...] = (rp * (w - yhat * jnp.sum(q * gg, axis=-1, keepdims=True) * (1.0 / D))).astype(bf16)
        loss = 0.5 * jnp.sum(jnp.mean(err * err, axis=-1, keepdims=True), axis=0, keepdims=True)
        q_sum = jnp.sum(q, axis=0, keepdims=True)
        row = lax.broadcasted_iota(jnp.int32, (8, D), 0)
        upd = jnp.where(row == 0, q_sum * g, jnp.where(row == 1, q_sum * gate, jnp.where(row == 2, loss, 0.0)))

        @pl.when(i == 0)
        def _():
            sums_ref[...] = upd

        @pl.when(i > 0)
        def _():
            sums_ref[...] += upd

    row = pl.BlockSpec((tr, D), lambda i: (i, 0))
    vec = pl.BlockSpec((1, D), lambda i: (0, 0))
    return pl.pallas_call(
        body, name="sandwich",
        out_shape=(jax.ShapeDtypeStruct((S, D), bf16), jax.ShapeDtypeStruct((S, D), f32), jax.ShapeDtypeStruct((8, D), f32)),
        grid=(S // tr,), in_specs=[row, row, vec, vec],
        out_specs=(row, row, pl.BlockSpec((8, D), lambda i: (0, 0))), compiler_params=_params("arbitrary"),
    )(y, x_minus_t, gate, g_post)


def _conv_bwd(proj, dycat, conv_w, conv_b, g_conv, dep, tr=256):
    S, C = proj.shape[0], proj.shape[1] // 8
    tr = min(tr, S)
    n = tr + 16

    def body(*refs):
        ins, (w_ref, cb_ref, g_ref, _, dp_ref, sums_ref) = refs[:15], refs[15:]
        i = pl.program_id(0)
        exists = _ext_rows(i, tr, S)
        u, bg, cg, zc, dyn = (jnp.concatenate([ins[3 * t][...], ins[3 * t + 1][...], ins[3 * t + 2][...]], axis=0)
                              for t in range(5))
        w = w_ref[...]
        t = jnp.where(exists, cg * u, 0.0)
        t_before, t_after = pltpu.roll(t, 1, 0), pltpu.roll(t, n - 1, 0)
        cv = w[0:1] * t_before + w[1:2] * t + w[2:3] * t_after + cb_ref[...]
        yc = bg * cv
        rc = lax.rsqrt(jnp.mean(yc * yc, axis=-1, keepdims=True) + EPS)
        yhat = yc * rc
        sz, dsz = _silu_and_slope(zc)
        wgt = dyn * g_ref[...] * sz
        dyc = rc * (wgt - yhat * jnp.mean(wgt * yhat, axis=-1, keepdims=True))
        dcv = jnp.where(exists, dyc * bg, 0.0)
        dt = w[0:1] * pltpu.roll(dcv, n - 1, 0) + w[1:2] * dcv + w[2:3] * pltpu.roll(dcv, 1, 0)
        mid = slice(8, tr + 8)
        dp_ref[:, 0:C] = (dt * cg)[mid].astype(bf16)
        dp_ref[:, C:2 * C] = (dyc * cv)[mid].astype(bf16)
        dp_ref[:, 2 * C:3 * C] = (dt * u)[mid].astype(bf16)
        dp_ref[:, 3 * C:4 * C] = (dyn * yhat * g_ref[...] * dsz)[mid].astype(bf16)
        colsum = lambda v: jnp.sum(v[mid], axis=0, keepdims=True)
        parts = [colsum(dyn * yhat * sz), colsum(dcv), colsum(dcv * t_before), colsum(dcv * t), colsum(dcv * t_after)]
        row = lax.broadcasted_iota(jnp.int32, (8, C), 0)
        upd = jnp.zeros((8, C), f32)
        for j, pj in enumerate(parts):
            upd = jnp.where(row == j, pj, upd)

        @pl.when(i == 0)
        def _():
            sums_ref[...] = upd

        @pl.when(i > 0)
        def _():
            sums_ref[...] += upd

    specs = []
    for col in range(4):
        specs += _halo_specs(tr, S, C, col)
    specs += _halo_specs(tr, S, C, 0)
    vec = pl.BlockSpec((1, C), lambda i: (0, 0))
    return pl.pallas_call(
        body, name="conv_bwd",
        out_shape=(jax.ShapeDtypeStruct((S, 4 * C), bf16), jax.ShapeDtypeStruct((8, C), f32)), grid=(S // tr,),
        in_specs=[*specs, pl.BlockSpec((8, C), lambda i: (0, 0)), vec, vec, ANY_SPEC],
        out_specs=(pl.BlockSpec((tr, 4 * C), lambda i: (i, 0)), pl.BlockSpec((8, C), lambda i: (0, 0))),
        compiler_params=_params("arbitrary"),
    )(*([proj] * 12), dycat, dycat, dycat, conv_w, conv_b, g_conv, dep)


def _attn_post_bwd(o, proj, dycat, g_attn, dep, tr=512):
    S, C = o.shape
    tr = min(tr, S)

    def body(o_ref, z_ref, dy_ref, g_ref, dep_ref, do_ref, dz_ref, sums_ref):
        i = pl.program_id(0)
        ov, zv, dyn = o_ref[...], z_ref[...], dy_ref[...]
        ra = lax.rsqrt(jnp.mean(ov * ov, axis=-1, keepdims=True) + EPS)
        ohat = ov * ra
        sz, dsz = _silu_and_slope(zv)
        wgt = dyn * g_ref[...] * sz
        do_ref[...] = ra * (wgt - ohat * jnp.mean(wgt * ohat, axis=-1, keepdims=True))
        dz_ref[...] = (dyn * ohat * g_ref[...] * dsz).astype(bf16)
        row = lax.broadcasted_iota(jnp.int32, (8, C), 0)
        upd = jnp.where(row == 0, jnp.sum(dyn * ohat * sz, axis=0, keepdims=True), 0.0)

        @pl.when(i == 0)
        def _():
            sums_ref[...] = upd

        @pl.when(i > 0)
        def _():
            sums_ref[...] += upd

    return pl.pallas_call(
        body, name="attn_post_bwd",
        out_shape=(jax.ShapeDtypeStruct((S, C), f32), jax.ShapeDtypeStruct((4, S, C), bf16),
                   jax.ShapeDtypeStruct((8, C), f32)),
        grid=(S // tr,),
        in_specs=[pl.BlockSpec((tr, C), lambda i: (i, 0)), pl.BlockSpec((tr, C), lambda i: (i, 7)),
                  pl.BlockSpec((tr, C), lambda i: (i, 1)), pl.BlockSpec((1, C), lambda i: (0, 0)), ANY_SPEC],
        out_specs=(pl.BlockSpec((tr, C), lambda i: (i, 0)), pl.BlockSpec((None, tr, C), lambda i: (3, i, 0)),
                   pl.BlockSpec((8, C), lambda i: (0, 0))),
        compiler_params=_params("arbitrary"),
    )(o, proj, dycat, g_attn, dep)


def _attn_bwd(proj, o, do, lse, bias, dqkvz, dep):
    S, C = o.shape
    npair = C // PAIR

    def body(q_ref, k_ref, v_ref, o_ref, do_ref, lse_ref, bias_scr, old_ref, dep_ref, dqkv_ref,
             acc_scr, dl_scr, quad_scr):
        lane = lax.broadcasted_iota(jnp.int32, (1, PAIR), 1)
        first = lane < HEAD_DIM
        ch = min(256, S)

        def prep(i, carry):
            rows = pl.ds(pl.multiple_of(i * ch, 8), ch)
            prod = do_ref[rows, :] * o_ref[rows, :]
            d0 = jnp.sum(jnp.where(first, prod, 0.0), axis=-1, keepdims=True)
            d1 = jnp.sum(jnp.where(first, 0.0, prod), axis=-1, keepdims=True)
            dl_scr[rows, :] = jnp.where(first, d0, d1)
            zero = jnp.zeros((ch, PAIR), f32)
            for order in range(2):
                for t in range(3):
                    acc_scr[order, t, rows, :] = zero
            return carry

        lax.fori_loop(0, S // ch, prep, 0, unroll=2)
        token_srcs = (q_ref, k_ref, v_ref, do_ref, lse_ref, dl_scr)
        for j, src in enumerate(token_srcs):
            _to_quad(quad_scr.at[j], src, S)

        for b, (_, r, inter) in enumerate(BRANCHES):
            L, nq, nk, nblk = _branch_geometry(S, r, inter)
            order = 0 if r == 1 else 1
            srcs = token_srcs if r == 1 else tuple(quad_scr.at[j] for j in range(6))

            def step(idx, carry, b=b, r=r, L=L, nq=nq, nk=nk, nblk=nblk, order=order, srcs=srcs):
                qs, ks, vs, dos, lses, dls = srcs
                dq_scr, dk_scr, dv_scr = (acc_scr.at[order, t] for t in range(3))
                qrows, krows, off = _block_rows(idx, r, inter, S, L, nq, nk, nblk)
                case = off // HALF_WIN
                q2 = qs[qrows, :] * SCALE
                k2 = ks[krows, :].astype(bf16)
                v2 = vs[krows, :].astype(bf16)
                do2 = dos[qrows, :]
                lse2 = lses[qrows, :]
                dl2 = dls[qrows, :]
                dq2 = jnp.zeros((nq, PAIR), f32)
                dk2 = jnp.zeros((nk, PAIR), f32)
                dv2 = jnp.zeros((nk, PAIR), f32)
                for hh in range(2):
                    mine = first if hh == 0 else ~first
                    lo = hh * HEAD_DIM
                    qh = jnp.where(mine, q2, 0.0).astype(bf16)
                    doh = jnp.where(mine, do2, 0.0).astype(bf16)
                    s = lax.dot_general(qh, k2, (((1,), (1,)), ((), ())), preferred_element_type=f32)
                    s = s + bias_scr[_bias_index(b, case, hh), 0:nq, 0:nk]
                    p = jnp.exp(s - lse2[:, lo:lo + 1])
                    dv2 = dv2 + lax.dot_general(p.astype(bf16), doh, (((0,), (0,)), ((), ())), preferred_element_type=f32)
                    dp = lax.dot_general(doh, v2, (((1,), (1,)), ((), ())), preferred_element_type=f32)
                    ds = (p * (dp - dl2[:, lo:lo + 1])).astype(bf16)
                    dq2 = dq2 + jnp.where(mine, jnp.dot(ds, k2, preferred_element_type=f32), 0.0)
                    dk2 = dk2 + lax.dot_general(ds, qh, (((0,), (0,)), ((), ())), preferred_element_type=f32)
                dq_scr[qrows, :] = dq_scr[qrows, :] + dq2
                dk_scr[krows, :] = dk_scr[krows, :] + dk2
                dv_scr[krows, :] = dv_scr[krows, :] + dv2
                return carry

            lax.fori_loop(0, S // nq, step, 0, unroll=min(ATTN_UNROLL, S // nq))

        n4 = S // QUAD
        for t in range(3):
            for rho in range(QUAD):
                token_rows = pl.ds(rho, n4, stride=QUAD)
                acc_scr[0, t, token_rows, :] = acc_scr[0, t, token_rows, :] + acc_scr[1, t, pl.ds(rho * n4, n4), :]
        dqkv_ref[0] = (acc_scr[0, 0] * SCALE).astype(bf16)
        dqkv_ref[1] = acc_scr[0, 1].astype(bf16)
        dqkv_ref[2] = acc_scr[0, 2].astype(bf16)

    blk = lambda part: pl.BlockSpec((S, PAIR), lambda p: (0, part * npair + p))
    own = pl.BlockSpec((S, PAIR), lambda p: (0, p))
    return pl.pallas_call(
        body, name="attn_bwd", out_shape=jax.ShapeDtypeStruct(dqkvz.shape, dqkvz.dtype), grid=(npair,),
        in_specs=[blk(4), blk(5), blk(6), own, own, own,
                  pl.BlockSpec((None,) + _bias_shape(S), lambda p: (p, 0, 0, 0)), ANY_SPEC, ANY_SPEC],
        out_specs=pl.BlockSpec((3, S, PAIR), lambda p: (0, 0, p)), input_output_aliases={7: 0},
        scratch_shapes=[pltpu.VMEM((2, 3, S, PAIR), f32), pltpu.VMEM((S, PAIR), f32), pltpu.VMEM((6, S, PAIR), f32)],
        compiler_params=_params("arbitrary"),
    )(proj, proj, proj, o, do, lse, bias, dqkvz, dep)


def _prenorm_bwd(dh, x, dout, scale, g_pre, tr=256):
    S, D = x.shape
    tr = min(tr, S)

    def body(dh_ref, x_ref, dout_ref, sc_ref, g_ref, gx_ref, sums_ref):
        i = pl.program_id(0)
        xv, dhv = x_ref[...], dh_ref[...]
        r = lax.rsqrt(jnp.mean(xv * xv, axis=-1, keepdims=True) + EPS)
        xn = xv * r
        dxn = dhv * (g_ref[...] * (1.0 + sc_ref[...]))
        gx_ref[...] = dout_ref[...] + r * (dxn - xn * jnp.mean(dxn * xn, axis=-1, keepdims=True))
        dhx = dhv * xn
        row = lax.broadcasted_iota(jnp.int32, (8, D), 0)
        upd = jnp.where(row == 0, jnp.sum(dhv, axis=0, keepdims=True),
                        jnp.where(row == 1, jnp.sum(dhx, axis=0, keepdims=True) * g_ref[...],
                                  jnp.where(row == 2, jnp.sum(dhx, axis=0, keepdims=True) * (1.0 + sc_ref[...]), 0.0)))

        @pl.when(i == 0)
        def _():
            sums_ref[...] = upd

        @pl.when(i > 0)
        def _():
            sums_ref[...] += upd

    row = pl.BlockSpec((tr, D), lambda i: (i, 0))
    vec = pl.BlockSpec((1, D), lambda i: (0, 0))
    return pl.pallas_call(
        body, name="prenorm_bwd",
        out_shape=(jax.ShapeDtypeStruct((S, D), f32), jax.ShapeDtypeStruct((8, D), f32)), grid=(S // tr,),
        in_specs=[row, row, row, vec, vec], out_specs=(row, pl.BlockSpec((8, D), lambda i: (0, 0))),
        compiler_params=_params("arbitrary"),
    )(dh, x, dout, scale, g_pre)


def _adamw(w, g, m, v):
    m = ADAM_B1 * m + (1.0 - ADAM_B1) * g
    v = ADAM_B2 * v + (1.0 - ADAM_B2) * (g * g)
    m_hat = m / (1.0 - ADAM_B1 ** ADAM_STEP)
    v_hat = v / (1.0 - ADAM_B2 ** ADAM_STEP)
    delta = -ADAM_LR * (m_hat / (jnp.sqrt(v_hat) + ADAM_EPS) + ADAM_WD * w)
    return delta, m, v


def _sum_rows(parts, dep):
    P = parts.shape[1]

    def body(p_ref, dep_ref, o_ref):
        acc = p_ref[0:1, :]
        for j in range(1, NDEV):
            acc = acc + p_ref[j:j + 1, :]
        o_ref[...] = jnp.broadcast_to(acc, (8, P))

    vmem = pl.BlockSpec(memory_space=pltpu.VMEM)
    return pl.pallas_call(body, name="sum_small", out_shape=jax.ShapeDtypeStruct((8, P), f32),
                          in_specs=[vmem, ANY_SPEC], out_specs=vmem, compiler_params=_params())(parts, dep)


def _adamw_small(tot, params):
    given = [p[3] for p in params if not isinstance(p[3], int)]

    def body(tot_ref, *refs):
        given_refs = list(refs[:len(given)])
        ins = refs[len(given):len(given) + 3 * len(params)]
        outs = refs[len(given) + 3 * len(params):]
        for t, (w, _, _, where) in enumerate(params):
            w_ref, m_ref, v_ref = ins[3 * t:3 * t + 3]
            g = tot_ref[0:1, where:where + w.size] if isinstance(where, int) else given_refs.pop(0)[...]
            outs[4 * t][...] = g
            outs[4 * t + 1][...], outs[4 * t + 2][...], outs[4 * t + 3][...] = _adamw(w_ref[...], g, m_ref[...], v_ref[...])

    out_shape = tuple(jax.ShapeDtypeStruct(p[0].shape, f32) for p in params for _ in range(4))
    res = pl.pallas_call(body, name="adamw_small", out_shape=out_shape, compiler_params=_params())(
        tot, *given, *[a for p in params for a in p[:3]])
    return [res[4 * t:4 * t + 4] for t in range(len(params))]


def _adamw_sharded(name, parts, sums_a, sums_b, pick, w, m, v, rows=None, prev=None, tr=128):
    R, Cc = w.shape
    r0, nr = rows or (0, R)
    tr = math.gcd(tr, r0, nr)
    n, b0 = parts.shape[0], r0 // tr

    def body(pick_ref, p_ref, a_ref, b_ref, w_ref, m_ref, v_ref, *rest):
        g_ref, d_ref, nm_ref, nv_ref = rest[-4:]
        g = jnp.where(pick_ref[0] == 1, b_ref[...], a_ref[...]).astype(f32)
        for j in range(n):
            g = g + p_ref[j].astype(f32)
        g_ref[...] = g
        d_ref[...], nm_ref[...], nv_ref[...] = _adamw(w_ref[...], g, m_ref[...], v_ref[...])

    row = pl.BlockSpec((tr, Cc), lambda i, pick: (i + b0, 0))
    mine = pl.BlockSpec((None, tr, Cc), lambda i, pick: (pick[1], i + b0, 0))
    out = jax.ShapeDtypeStruct((R, Cc), f32)
    prev = list(prev or [])
    grid_spec = pltpu.PrefetchScalarGridSpec(
        num_scalar_prefetch=1, grid=(nr // tr,),
        in_specs=[pl.BlockSpec((n, tr, Cc), lambda i, pick: (0, i + b0, 0)), mine, mine, row, row, row]
        + [ANY_SPEC] * len(prev),
        out_specs=(row, row, row, row))
    return pl.pallas_call(
        body, name=name, out_shape=(out, out, out, out), grid_spec=grid_spec,
        input_output_aliases={7 + t: t for t in range(len(prev))}, compiler_params=_params("arbitrary"),
    )(pick, parts, sums_a, sums_b, w, m, v, *prev)


def _adamw_ada(c_t, dmod_cols, w, m, v, dep, tr=512):
    D, W = w.shape
    tr = min(tr, D)

    def body(c_ref, dm_ref, w_ref, m_ref, v_ref, dep_ref, g_ref, d_ref, nm_ref, nv_ref):
        g = lax.dot_general(c_ref[...], dm_ref[...], (((1,), (0,)), ((), ())), preferred_element_type=f32,
                            precision=lax.Precision.HIGHEST)
        g_ref[...] = g
        d_ref[...], nm_ref[...], nv_ref[...] = _adamw(w_ref[...], g, m_ref[...], v_ref[...])

    row = pl.BlockSpec((tr, W), lambda i: (i, 0))
    out = jax.ShapeDtypeStruct((D, W), f32)
    return pl.pallas_call(
        body, name="adamw_ada", out_shape=(out, out, out, out), grid=(D // tr,),
        in_specs=[pl.BlockSpec((tr, NDEV), lambda i: (i, 0)), pl.BlockSpec((NDEV, W), lambda i: (0, 0)), row, row, row,
                  ANY_SPEC],
        out_specs=(row, row, row, row), compiler_params=_params("parallel"),
    )(c_t, dmod_cols, w, m, v, dep)


def kernel(x, c, w_ada, b_ada, g_pre, w_in, conv_w, conv_b, g_conv, g_attn, w_out, g_post, loss_target, m_w_ada, m_b_ada, m_g_pre, m_w_in, m_conv_w, m_conv_b, m_g_conv, m_g_attn, m_w_out, m_g_post, v_w_ada, v_b_ada, v_g_pre, v_w_in, v_conv_w, v_conv_b, v_g_conv, v_g_attn, v_w_out, v_g_post):
    S, D = x.shape[1], x.shape[2]
    C = D // 2
    W = w_ada.shape[2]
    CW = conv_w.shape[2]
    me = 4 * lax.axis_index("x") + 2 * lax.axis_index("y") + lax.axis_index("c")
    x2, tgt = x[0], loss_target[0]
    w_ada2, w_in2, w_out2 = w_ada[0], w_in[0], w_out[0]

    R = D // NDEV
    core = lax.axis_index("c").astype(jnp.int32).reshape(1)

    cw_slab = jnp.zeros((8, CW), f32).at[:3].set(conv_w[0])
    b_cols = lax.dynamic_slice_in_dim(b_ada, me * W, W, axis=1)
    mod_slabs, c_blocks, cw_g = _ada_exchange(c.reshape(D // 128, 128), cw_slab, w_ada2, b_cols)
    c_all = c_blocks.reshape(NDEV, D)
    conv_w_full = jnp.transpose(cw_g, (1, 0, 2)).reshape(8, C)
    mod = mod_slabs[:, 0, :].reshape(1, 3 * D)
    shift, scale, gate = mod[:, :D], mod[:, D:2 * D], mod[:, 2 * D:]

    land_i = lax.dynamic_update_slice(lax.empty((NDEV, D, C), bf16), w_in2.astype(bf16)[None], (me, 0, 0))
    land_o = lax.dynamic_update_slice(lax.empty((NDEV, R, D), bf16), w_out2.astype(bf16)[None], (me, 0, 0))
    wi_send, wi_recv, land_i, w_token = _w_in_start(land_i, [mod_slabs])

    me_arr = me.astype(jnp.int32).reshape(1)
    h = _prenorm(x2, scale, shift, g_pre, w_token)
    land_i = _w_in_sibling(land_i, wi_recv, after=[h])
    proj = _in_proj_part("in_proj_a", h, land_i, None, me_arr, 0, 1, 2)
    x_minus_t = _residual_minus_target(x2, tgt, proj)
    bias = _bias_tiles(_head_slopes(C // HEAD_DIM), S, x_minus_t)

    def landing(rows, cols):
        return lax.dynamic_update_slice(lax.empty((NCHIP, rows, cols), bf16), jnp.zeros((1, rows, cols), bf16),
                                        (me // 2, 0, 0))

    land_go, land_gi = landing(R, D), landing(D, C)
    fi_send, fi_recv, land_i = _w_in_relay(land_i, wi_recv, after=[proj, bias, land_go, land_gi])
    proj = _in_proj_part("in_proj_b", h, land_i, proj, me_arr, 2, 2, 2)
    land_i = _w_in_forwarded(land_i, fi_recv, after=[proj])
    proj = _in_proj_part("in_proj_c", h, land_i, proj, me_arr, 3, 2, 2)
    (di_send, di_recv, wo_send, wo_recv), land_i, land_o = _w_in_diag(land_i, land_o, fi_recv, after=[proj])
    proj = _in_proj_part("in_proj_d", h, land_i, proj, me_arr, 6, 1, 1)
    win_g = _w_in_finish(land_i, wi_send, fi_send, di_send, di_recv, after=[proj])
    proj = _in_proj_part("in_proj_e", h, win_g, proj, me_arr, 7, 1, 1)
    ycat = _conv_fwd(proj, conv_w_full, conv_b, g_conv)
    o, lse = _attn_fwd(proj, bias)
    (fo_send, fo_recv), (land_o,), _ = _weights_forward("w_out_forward", land_o, wo_recv, after=[o])
    ycat = _attn_post(ycat, o, proj, g_attn)
    wout_g = _weights_wait("w_out_wait", land_o, wo_send, wo_recv, fo_send, fo_recv, after=[ycat])
    wout_full = wout_g.reshape(D, D)
    y = _matmul(ycat, wout_full, name="out_proj", out_dtype=f32)
    dy, dout, post_sums = _sandwich(y, x_minus_t, gate, g_post)

    gw_out = _matmul(ycat, dy, name="out_proj_dw", out_dtype=bf16, ta=True).reshape(NDEV, R, D)
    dycat = _matmul(dy, wout_full, name="out_proj_dx", out_dtype=f32, tb=True)
    dpc, conv_sums = _conv_bwd(proj, dycat, conv_w_full, conv_b, g_conv, gw_out)
    gw_c = _matmul(h, dpc, name="in_proj_dw_conv", out_dtype=bf16, ta=True, out_slots=4)
    first_pairs, p1_token = _pairs_start("g_first_pair_start", [gw_out, gw_c])
    do, dpa, attn_sums = _attn_post_bwd(o, proj, dycat, g_attn, p1_token)
    (gw_out, pair_o), (gw_c, pair_c) = _pairs_wait("g_first_pair_wait", first_pairs, after=[do])
    sum_o = _pair_sum("g_out_pair_sum", gw_out, pair_o, core)
    sum_c = _pair_sum("g_conv_pair_sum", gw_c, pair_c, core)
    ((co_send, co_recv, sum_o, land_go), (cc_send, cc_recv, sum_c, land_gi)), cc_token = _chips_start(
        "g_first_chip_start", [(sum_o, land_go, 0), (sum_c, land_gi, 0)])
    dpa = _attn_bwd(proj, o, do, lse, bias, dpa, cc_token)
    gw_a = _matmul(h, dpa, name="in_proj_dw_attn", out_dtype=bf16, ta=True, b_slots=True, out_slots=4)
    pa_send, pa_recv, gw_a, pair_a, pa_token = _pair_start("g_attn_pair_start", gw_a)
    sum_o, land_go = _chip_wait("g_out_chip_wait", sum_o, land_go, co_send, co_recv, 0, after=[pa_token])
    pick_out = jnp.stack([jnp.int32(0), me // 2]).astype(jnp.int32)
    g_w_out, d_w_out, nm_w_out, nv_w_out = _adamw_sharded(
        "adamw_w_out", land_go, sum_o, sum_o, pick_out, w_out2, m_w_out[0], v_w_out[0])
    gw_a, pair_a = _pair_wait("g_attn_pair_wait", gw_a, pair_a, pa_send, pa_recv, after=[g_w_out])
    sum_a = _pair_sum("g_attn_pair_sum", gw_a, pair_a, core)
    part_a, part_b = (0, 3 * D // 4), (3 * D // 4, D // 4)
    ca_send, ca_recv, sum_a, land_gi, ca_token = _chip_start("g_attn_chip_start_a", sum_a, land_gi, 4, part_a)
    dh = _matmul_slabs_t(dpc, dpa, win_g, name="in_proj_dx", dep=ca_token)
    grad_x, pre_sums = _prenorm_bwd(dh, x2, dout, scale, g_pre)

    small = jnp.concatenate([pre_sums[0:1], pre_sums[1:2], post_sums[0:1],
                             pre_sums[2:3], post_sums[1:2],
                             conv_sums[2:3], conv_sums[3:4], conv_sums[4:5],
                             conv_sums[1:2], conv_sums[0:1], attn_sums[0:1]], axis=1)
    small = jnp.concatenate([small.reshape(8 * D // 128, 128), jnp.broadcast_to(post_sums[2:3, :128], (8, 128))])
    gs_send, gs_recv, small, small_all, gs_token = _gather_start("gather_small_start", small)

    cb_send, cb_recv, sum_a, land_gi, cb_token = _chip_start("g_attn_chip_start_b", sum_a, land_gi, 4, part_b,
                                                             after=[gs_token])

    pick_in = jnp.stack([me // 4, (me % 4) // 2]).astype(jnp.int32)
    sum_c, land_gi = _chip_wait("g_conv_chip_wait", sum_c, land_gi, cc_send, cc_recv, 0, after=[cb_token])
    sum_a, land_gi = _chip_wait("g_attn_chip_wait_a", sum_a, land_gi, ca_send, ca_recv, 4, [cb_token], part_a)
    first = _adamw_sharded("adamw_w_in_a", land_gi, sum_c, sum_a, pick_in, w_in2, m_w_in[0], v_w_in[0], rows=part_a,
                           tr=256)

    small_all = _gather_wait("gather_small_wait", small, small_all, gs_send, gs_recv, after=[first[0]])
    small_all = small_all.reshape(NDEV, small.size)
    tot = _sum_rows(small_all, gs_token)
    loss = tot[0, 8 * D]
    g_conv_w = lax.dynamic_slice_in_dim(tot[0:1, 5 * D:5 * D + 3 * C].reshape(1, 3, C), me * CW, CW, axis=2)
    ((g_b_ada, d_b_ada, nm_b_ada, nv_b_ada), (g_g_pre, d_g_pre, nm_g_pre, nv_g_pre),
     (g_g_post, d_g_post, nm_g_post, nv_g_post), (g_conv_w, d_conv_w, nm_conv_w, nv_conv_w),
     (g_conv_b, d_conv_b, nm_conv_b, nv_conv_b), (g_g_conv, d_g_conv, nm_g_conv, nv_g_conv),
     (g_g_attn, d_g_attn, nm_g_attn, nv_g_attn)) = _adamw_small(tot, [
         (b_ada, m_b_ada, v_b_ada, 0), (g_pre, m_g_pre, v_g_pre, 3 * D), (g_post, m_g_post, v_g_post, 4 * D),
         (conv_w, m_conv_w, v_conv_w, g_conv_w), (conv_b, m_conv_b, v_conv_b, 5 * D + 3 * C),
         (g_conv, m_g_conv, v_g_conv, 5 * D + 4 * C), (g_attn, m_g_attn, v_g_attn, 5 * D + 5 * C)])

    dmod_cols = lax.dynamic_slice_in_dim(small_all[:, :3 * D], me * W, W, axis=1)
    g_w_ada, d_w_ada, nm_w_ada, nv_w_ada = _adamw_ada(c_all.T, dmod_cols, w_ada2, m_w_ada[0], v_w_ada[0], gs_token)

    sum_a, land_gi = _chip_wait("g_attn_chip_wait_b", sum_a, land_gi, cb_send, cb_recv, 4, [g_w_ada], part_b)
    g_w_in, d_w_in, nm_w_in, nv_w_in = _adamw_sharded(
        "adamw_w_in_b", land_gi, sum_c, sum_a, pick_in, w_in2, m_w_in[0], v_w_in[0], rows=part_b, prev=first, tr=256)

    return (loss, grad_x[None],
            g_w_ada[None], g_b_ada, g_g_pre, g_w_in[None], g_conv_w, g_conv_b, g_g_conv, g_g_attn, g_w_out[None], g_g_post,
            d_w_ada[None], d_b_ada, d_g_pre, d_w_in[None], d_conv_w, d_conv_b, d_g_conv, d_g_attn, d_w_out[None], d_g_post,
            nm_w_ada[None], nm_b_ada, nm_g_pre, nm_w_in[None], nm_conv_w, nm_conv_b, nm_g_conv, nm_g_attn, nm_w_out[None], nm_g_post,
            nv_w_ada[None], nv_b_ada, nv_g_pre, nv_w_in[None], nv_conv_w, nv_conv_b, nv_g_conv, nv_g_attn, nv_w_out[None], nv_g_post)
```

```python
import functools
import math

import jax
import jax.numpy as jnp
from jax import lax
from jax.experimental import pallas as pl
from jax.experimental.pallas import tpu as pltpu

f32 = jnp.float32
bf16 = jnp.bfloat16

NDEV = 8
HEAD_DIM = 64
PAIR = 2 * HEAD_DIM
BRANCHES = ((128, 1, 1), (512, 4, 1), (2048, 16, 2))
HALF_WIN = 64
EPS = 1e-6
NEG_INF = -1e30
ADAM_LR, ADAM_B1, ADAM_B2, ADAM_EPS, ADAM_WD, ADAM_STEP = 0.001, 0.9, 0.999, 1e-08, 0.01, 10
MESH = pl.DeviceIdType.MESH
VMEM_LIMIT = 56 * 1024 * 1024
HBM_SPEC = pl.BlockSpec(memory_space=pltpu.HBM)
ANY_SPEC = pl.BlockSpec(memory_space=pl.ANY)
SEM_SPEC = pl.BlockSpec(memory_space=pltpu.SEMAPHORE)


def _params(*sem):
    return pltpu.CompilerParams(dimension_semantics=sem or None, vmem_limit_bytes=VMEM_LIMIT)


def _silu(z):
    return z * jax.nn.sigmoid(z)


def _silu_and_slope(z):
    s = jax.nn.sigmoid(z)
    return z * s, s * (1.0 + z * (1.0 - s))


def _my_place():
    x, y, c = lax.axis_index("x"), lax.axis_index("y"), lax.axis_index("c")
    return x, y, c, 4 * x + 2 * y + c


def _peer(x, y, c, k):
    px, py, pc = x ^ (k >> 2 & 1), y ^ (k >> 1 & 1), c ^ (k & 1)
    return (px, py, pc), 4 * px + 2 * py + pc


def _comm_call(name, arrays, sems, new_sems, body, after=(), token=False):
    na, ns, nn, nf = len(arrays), len(sems), len(new_sems), len(after)

    def kern(*refs):
        ins, outs = refs[:na + ns + nf], refs[na + ns + nf:]
        body(ins[:na], ins[na:na + ns], outs[:nn])
        if token:
            outs[nn + na][...] = jnp.zeros((8, 128), f32)

    out_shape = ([pltpu.SemaphoreType.DMA(s) for s in new_sems] + [pltpu.HBM(a.shape, a.dtype) for a in arrays]
                 + ([jax.ShapeDtypeStruct((8, 128), f32)] if token else []))
    out_specs = [SEM_SPEC] * nn + [HBM_SPEC] * na + ([pl.BlockSpec(memory_space=pltpu.VMEM)] if token else [])
    res = pl.pallas_call(
        kern, name=name, out_shape=tuple(out_shape),
        in_specs=[HBM_SPEC] * na + [SEM_SPEC] * ns + [ANY_SPEC] * nf, out_specs=tuple(out_specs),
        input_output_aliases={t: nn + t for t in range(na)},
        compiler_params=pltpu.CompilerParams(has_side_effects=pltpu.SideEffectType.DATAFLOW_SIDE_EFFECTING),
    )(*[pltpu.with_memory_space_constraint(a, pltpu.HBM) for a in arrays], *sems, *after)
    return list(res[:nn]), list(res[nn:nn + na]), (res[nn + na] if token else None)


def _remote(src, dst, send_sem, recv_sem, device):
    return pltpu.make_async_remote_copy(src_ref=src, dst_ref=dst, send_sem=send_sem, recv_sem=recv_sem,
                                        device_id=device, device_id_type=MESH)


def _gather_start(name, src):
    def body(a, s, new):
        (src, land), (send, recv) = a, new
        x, y, c, me = _my_place()
        pltpu.make_async_copy(src, land.at[me], recv.at[0]).start()
        for k in range(1, NDEV):
            peer, _ = _peer(x, y, c, k)
            _remote(src, land.at[me], send.at[k], recv.at[k], peer).start()

    land = lax.empty((NDEV,) + src.shape, src.dtype)
    (send, recv), (src, land), token = _comm_call(name, [src, land], [], [(NDEV,), (NDEV,)], body, token=True)
    return send, recv, src, land, token


def _gather_wait(name, src, land, send, recv, after):
    def body(a, s, new):
        (src, land), (send, recv) = a, s
        x, y, c, me = _my_place()
        pltpu.make_async_copy(src, land.at[me], recv.at[0]).wait()
        for k in range(1, NDEV):
            peer, slot = _peer(x, y, c, k)
            _remote(src, land.at[slot], send.at[k], recv.at[k], peer).wait_recv()
        for k in range(1, NDEV):
            peer, _ = _peer(x, y, c, k)
            _remote(src, land.at[me], send.at[k], recv.at[k], peer).wait_send()

    return _comm_call(name, [src, land], [send, recv], [], body, after=after)[1][1]


SAME_CORE = (2, 4, 6)
VIA_SIBLING = (3, 5, 7)


def _weights_forward(name, land, recv, after):
    def body(a, s, new):
        (land,), (recv,), (fsend, frecv) = a, s, new
        x, y, c, me = _my_place()
        sibling, _ = _peer(x, y, c, 1)
        for k in SAME_CORE:
            peer, slot = _peer(x, y, c, k)
            _remote(land.at[slot], land.at[slot], fsend.at[k], recv.at[k], peer).wait_recv()
            _remote(land.at[slot], land.at[slot], fsend.at[k], frecv.at[k ^ 1], sibling).start()

    return _comm_call(name, [land], [recv], [(NDEV,), (NDEV,)], body, after=after)


def _weights_wait(name, land, send, recv, fsend, frecv, after):
    def body(a, s, new):
        (land,), (send, recv, fsend, frecv) = a, s
        x, y, c, me = _my_place()
        sibling, sib_slot = _peer(x, y, c, 1)
        _remote(land.at[sib_slot], land.at[sib_slot], send.at[1], recv.at[1], sibling).wait_recv()
        for k in VIA_SIBLING:
            _, slot = _peer(x, y, c, k)
            _remote(land.at[slot], land.at[slot], fsend.at[k ^ 1], frecv.at[k], sibling).wait_recv()
        for k in (1,) + SAME_CORE:
            peer, _ = _peer(x, y, c, k)
            _remote(land.at[me], land.at[me], send.at[k], recv.at[k], peer).wait_send()
        for k in SAME_CORE:
            _, slot = _peer(x, y, c, k)
            _remote(land.at[slot], land.at[slot], fsend.at[k], frecv.at[k ^ 1], sibling).wait_send()

    return _comm_call(name, [land], [send, recv, fsend, frecv], [], body, after=after)[1][0]


def _diag_relay(x, y, c):
    slot = 4 * (x ^ (1 - c)) + 2 * (y ^ c) + c
    return slot, (x ^ c, y ^ (1 - c), c)


def _w_in_start(land, after):
    def body(a, s, new):
        (land,), (send, recv) = a, new
        x, y, c, me = _my_place()
        for k in (1, 2, 4):
            peer, _ = _peer(x, y, c, k)
            _remote(land.at[me], land.at[me], send.at[k], recv.at[k], peer).start()

    (send, recv), (land,), token = _comm_call("w_in_start", [land], [], [(NDEV,), (NDEV,)], body, after=after, token=True)
    return send, recv, land, token


def _w_in_sibling(land, recv, after):
    def body(a, s, new):
        (land,), (recv,) = a, s
        x, y, c, me = _my_place()
        sibling, slot = _peer(x, y, c, 1)
        _remote(land.at[slot], land.at[slot], recv.at[1], recv.at[1], sibling).wait_recv()

    return _comm_call("w_in_sibling", [land], [recv], [], body, after=after)[1][0]


def _w_in_relay(land, recv, after):
    def body(a, s, new):
        (land,), (recv,), (fsend, frecv) = a, s, new
        x, y, c, me = _my_place()
        sibling, _ = _peer(x, y, c, 1)
        for k in (2, 4):
            peer, slot = _peer(x, y, c, k)
            _remote(land.at[slot], land.at[slot], fsend.at[k], recv.at[k], peer).wait_recv()
        slot, target = _diag_relay(x, y, c)
        _remote(land.at[slot], land.at[slot], fsend.at[6], frecv.at[6], target).start()
        for k in (2, 4):
            _, slot = _peer(x, y, c, k)
            _remote(land.at[slot], land.at[slot], fsend.at[k], frecv.at[k ^ 1], sibling).start()

    (fsend, frecv), (land,), _ = _comm_call("w_in_relay", [land], [recv], [(NDEV,), (NDEV,)], body, after=after)
    return fsend, frecv, land


def _w_in_forwarded(land, frecv, after):
    def body(a, s, new):
        (land,), (frecv,) = a, s
        x, y, c, me = _my_place()
        sibling, _ = _peer(x, y, c, 1)
        for k in (3, 5):
            _, slot = _peer(x, y, c, k)
            _remote(land.at[slot], land.at[slot], frecv.at[k], frecv.at[k], sibling).wait_recv()

    return _comm_call("w_in_forwarded", [land], [frecv], [], body, after=after)[1][0]


def _w_in_diag(land, land_o, frecv, after):
    def body(a, s, new):
        (land, land_o), (frecv,), (dsend, drecv, osend, orecv) = a, s, new
        x, y, c, me = _my_place()
        sibling, _ = _peer(x, y, c, 1)
        peer, slot = _peer(x, y, c, 6)
        _remote(land.at[slot], land.at[slot], dsend.at[6], frecv.at[6], peer).wait_recv()
        _remote(land.at[slot], land.at[slot], dsend.at[6], drecv.at[7], sibling).start()
        for k in (1,) + SAME_CORE:
            peer, _ = _peer(x, y, c, k)
            _remote(land_o.at[me], land_o.at[me], osend.at[k], orecv.at[k], peer).start()

    sems, (land, land_o), _ = _comm_call("w_in_diag", [land, land_o], [frecv], [(NDEV,)] * 4, body, after=after)
    return sems, land, land_o


def _w_in_finish(land, send, fsend, dsend, drecv, after):
    def body(a, s, new):
        (land,), (send, fsend, dsend, drecv) = a, s
        x, y, c, me = _my_place()
        sibling, _ = _peer(x, y, c, 1)
        _, slot = _peer(x, y, c, 7)
        _remote(land.at[slot], land.at[slot], dsend.at[6], drecv.at[7], sibling).wait_recv()
        for k in (1, 2, 4):
            peer, _ = _peer(x, y, c, k)
            _remote(land.at[me], land.at[me], send.at[k], send.at[k], peer).wait_send()
        for k in (2, 4, 6):
            _, slot = _peer(x, y, c, k)
            _remote(land.at[slot], land.at[slot], fsend.at[k], fsend.at[k], sibling).wait_send()
        _, slot = _peer(x, y, c, 6)
        _remote(land.at[slot], land.at[slot], dsend.at[6], dsend.at[6], sibling).wait_send()

    return _comm_call("w_in_finish", [land], [send, fsend, dsend, drecv], [], body, after=after)[1][0]


def _in_proj_part(name, h, land, proj, me_arr, k0, kstep, nk, tm=512):
    S, D = h.shape
    C = land.shape[2]
    tm = min(tm, S)

    def body(me_ref, a_ref, b_ref, *rest):
        rest[-1][...] = jnp.dot(a_ref[...], b_ref[...], preferred_element_type=f32)

    slot = lambda j, me: me[0] ^ (k0 + kstep * j)
    args = [h, land] + ([] if proj is None else [proj])
    grid_spec = pltpu.PrefetchScalarGridSpec(
        num_scalar_prefetch=1, grid=(nk, S // tm),
        in_specs=[pl.BlockSpec((tm, D), lambda j, i, me: (i, 0)),
                  pl.BlockSpec((None, D, C), lambda j, i, me: (slot(j, me), 0, 0))] + [ANY_SPEC] * (len(args) - 2),
        out_specs=pl.BlockSpec((tm, C), lambda j, i, me: (i, slot(j, me))))
    return pl.pallas_call(
        body, name=name, out_shape=jax.ShapeDtypeStruct((S, NDEV * C), f32), grid_spec=grid_spec,
        input_output_aliases={} if proj is None else {3: 0}, compiler_params=_params("arbitrary", "arbitrary"),
    )(me_arr, *args)


NCHIP = NDEV // 2


def _pairs_start(name, srcs):
    n = len(srcs)
    npairs = [src.shape[0] // 2 for src in srcs]

    def body(a, s, new):
        x, y, c, me = _my_place()
        sibling, _ = _peer(x, y, c, 1)
        for t in range(n):
            src, pair, send, recv = a[t], a[n + t], new[t], new[n + t]
            for i in range(npairs[t]):
                _remote(src.at[2 * i + 1 - c], pair.at[i], send.at[i], recv.at[i], sibling).start()

    pairs = [lax.empty((npairs[t],) + srcs[t].shape[1:], srcs[t].dtype) for t in range(n)]
    sems, arrays, token = _comm_call(name, list(srcs) + pairs, [], [(m,) for m in npairs] * 2, body, token=True)
    return [(sems[t], sems[n + t], arrays[t], arrays[n + t]) for t in range(n)], token


def _pairs_wait(name, groups, after):
    n = len(groups)

    def body(a, s, new):
        x, y, c, me = _my_place()
        sibling, _ = _peer(x, y, c, 1)
        for t in range(n):
            src, pair, send, recv = a[t], a[n + t], s[t], s[n + t]
            for i in range(pair.shape[0]):
                cp = _remote(src.at[2 * i + 1 - c], pair.at[i], send.at[i], recv.at[i], sibling)
                cp.wait_recv()
                cp.wait_send()

    arrays = _comm_call(name, [g[2] for g in groups] + [g[3] for g in groups],
                        [g[0] for g in groups] + [g[1] for g in groups], [], body, after=after)[1]
    return [(arrays[t], arrays[n + t]) for t in range(n)]


def _pair_start(name, src):
    ((send, recv, src, pair),), token = _pairs_start(name, [src])
    return send, recv, src, pair, token


def _pair_wait(name, src, pair, send, recv, after):
    return _pairs_wait(name, [(send, recv, src, pair)], after)[0]


def _pair_sum(name, src, pair, core, tr=1024):
    npair, R, Cc = pair.shape
    tr = min(tr, R)

    def body(core_ref, a_ref, b_ref, o_ref):
        o_ref[...] = (a_ref[...].astype(f32) + b_ref[...].astype(f32)).astype(o_ref.dtype)

    grid_spec = pltpu.PrefetchScalarGridSpec(
        num_scalar_prefetch=1, grid=(npair, R // tr),
        in_specs=[pl.BlockSpec((None, tr, Cc), lambda i, r, core: (2 * i + core[0], r, 0)),
                  pl.BlockSpec((None, tr, Cc), lambda i, r, core: (i, r, 0))],
        out_specs=pl.BlockSpec((None, tr, Cc), lambda i, r, core: (i, r, 0)))
    return pl.pallas_call(body, name=name, out_shape=jax.ShapeDtypeStruct(pair.shape, pair.dtype),
                          grid_spec=grid_spec, compiler_params=_params("parallel", "parallel"))(core, src, pair)


def _owner_chip(first, i):
    q = first // 2 + i
    return q >> 1 & 1, q & 1


def _chips_start(name, groups, rows=None, after=()):
    n = len(groups)
    row_of = [pl.ds(*(rows or (0, g[0].shape[1]))) for g in groups]

    def body(a, s, new):
        x, y, c, me = _my_place()
        for t, (_, _, first) in enumerate(groups):
            sums, land, send, recv = a[t], a[n + t], new[t], new[n + t]
            for i in range(sums.shape[0]):
                ox, oy = _owner_chip(first, i)

                @pl.when((x != ox) | (y != oy))
                def _():
                    _remote(sums.at[i, row_of[t]], land.at[2 * x + y, row_of[t]], send.at[i], recv.at[2 * x + y],
                            (ox, oy, c)).start()

    sems, arrays, token = _comm_call(name, [g[0] for g in groups] + [g[1] for g in groups], [],
                                     [(g[0].shape[0],) for g in groups] + [(NCHIP,)] * n, body, after=after, token=True)
    return [(sems[t], sems[n + t], arrays[t], arrays[n + t]) for t in range(n)], token


def _chips_wait(name, groups, firsts, after, rows=None):
    n = len(groups)
    row_of = [pl.ds(*(rows or (0, g[2].shape[1]))) for g in groups]

    def body(a, s, new):
        x, y, c, me = _my_place()
        for t in range(n):
            sums, land, send, recv, first = a[t], a[n + t], s[t], s[n + t], firsts[t]
            npair = sums.shape[0]
            mine = (me >= first) & (me < first + 2 * npair)
            for i in range(npair):
                ox, oy = _owner_chip(first, i)

                @pl.when((x != ox) | (y != oy))
                def _():
                    _remote(sums.at[i, row_of[t]], land.at[2 * x + y, row_of[t]], send.at[i], recv.at[2 * x + y],
                            (ox, oy, c)).wait_send()
            for q in range(NCHIP):
                @pl.when(mine & (2 * x + y != q))
                def _():
                    _remote(sums.at[0, row_of[t]], land.at[q, row_of[t]], send.at[0], recv.at[q],
                            (q >> 1, q & 1, c)).wait_recv()

    arrays = _comm_call(name, [g[2] for g in groups] + [g[3] for g in groups],
                        [g[0] for g in groups] + [g[1] for g in groups], [], body, after=after)[1]
    return [(arrays[t], arrays[n + t]) for t in range(n)]


def _chip_start(name, sums, land, first, rows=None, after=()):
    ((send, recv, sums, land),), token = _chips_start(name, [(sums, land, first)], rows, after)
    return send, recv, sums, land, token


def _chip_wait(name, sums, land, send, recv, first, after, rows=None):
    return _chips_wait(name, [(send, recv, sums, land)], [first], after, rows)[0]


def _matmul(a, b, *, name, out_dtype, ta=False, tb=False, b_slots=False, out_slots=0, b_cols=None,
            tm=1024, tn=1024, tk=2048, dep=None):
    M, K = (a.shape[1], a.shape[0]) if ta else a.shape
    col0 = 0
    if b_slots:
        slab = b.shape[2]
        N = b.shape[1] if tb else b.shape[0] * slab
        assert (K if tb else N) == b.shape[0] * slab
    elif b_cols is not None:
        assert not tb
        col0, N = b_cols
    else:
        N = b.shape[0] if tb else b.shape[1]
    tm, tn, tk = min(tm, M), min(tn, N), min(tk, K)
    if b_slots:
        if tb:
            tk = min(tk, slab)
        else:
            tn = min(tn, slab)
    if out_slots:
        tn = min(tn, N // out_slots)
    nm, nn, nk = M // tm, N // tn, K // tk
    assert (nm * tm, nn * tn, nk * tk) == (M, N, K) and col0 % tn == 0, (name, M, N, K, tm, tn, tk)
    j0 = col0 // tn

    a_spec = pl.BlockSpec((tk, tm), lambda i, j, k: (k, i)) if ta else pl.BlockSpec((tm, tk), lambda i, j, k: (i, k))
    if b_slots and tb:
        per = slab // tk
        b_spec = pl.BlockSpec((None, tn, tk), lambda i, j, k: (k // per, j, k % per))
    elif b_slots:
        per = slab // tn
        b_spec = pl.BlockSpec((None, tk, tn), lambda i, j, k: (j // per, k, j % per))
    elif tb:
        b_spec = pl.BlockSpec((tn, tk), lambda i, j, k: (j, k))
    else:
        b_spec = pl.BlockSpec((tk, tn), lambda i, j, k: (k, j + j0))
    if out_slots:
        per_o = (N // out_slots) // tn
        o_spec = pl.BlockSpec((None, tm, tn), lambda i, j, k: (j // per_o, i, j % per_o))
        out_shape = jax.ShapeDtypeStruct((out_slots, M, N // out_slots), out_dtype)
    else:
        o_spec = pl.BlockSpec((tm, tn), lambda i, j, k: (i, j))
        out_shape = jax.ShapeDtypeStruct((M, N), out_dtype)
    dims = (((0 if ta else 1,), (1 if tb else 0,)), ((), ()))
    deps = [] if dep is None else [dep]

    def body(a_ref, b_ref, *rest):
        o_ref = rest[len(deps)]
        prod = lax.dot_general(a_ref[...], b_ref[...], dims, preferred_element_type=f32)
        if nk == 1:
            o_ref[...] = prod.astype(out_dtype)
            return
        acc_ref = rest[len(deps) + 1]
        k = pl.program_id(2)

        @pl.when(k == 0)
        def _():
            acc_ref[...] = prod

        @pl.when((k > 0) & (k < nk - 1))
        def _():
            acc_ref[...] += prod

        @pl.when(k == nk - 1)
        def _():
            o_ref[...] = (acc_ref[...] + prod).astype(out_dtype)

    return pl.pallas_call(
        body, name=name, out_shape=out_shape, grid=(nm, nn, nk),
        in_specs=[a_spec, b_spec] + [ANY_SPEC] * len(deps), out_specs=o_spec,
        scratch_shapes=[pltpu.VMEM((tm, tn), f32)] if nk > 1 else [],
        compiler_params=_params("parallel", "parallel", "arbitrary"),
    )(a, b, *deps)


def _matmul_slabs_t(a_cols, a_slots, b, *, name, tm=512, tn=512, dep=None):
    M = a_cols.shape[0]
    n_slab, N, slab = b.shape
    n1, n2 = a_cols.shape[1] // slab, a_slots.shape[0]
    assert n1 + n2 == n_slab and a_slots.shape[1:] == (M, slab)
    tm, tn = min(tm, M), min(tn, N)
    deps = [] if dep is None else [dep]

    def body(a1_ref, a2_ref, b_ref, *rest):
        o_ref = rest[len(deps)]
        acc = None
        for s in range(n_slab):
            lhs = a1_ref[:, s * slab:(s + 1) * slab] if s < n1 else a2_ref[s - n1]
            prod = lax.dot_general(lhs, b_ref[s], (((1,), (1,)), ((), ())), preferred_element_type=f32)
            acc = prod if acc is None else acc + prod
        o_ref[...] = acc

    return pl.pallas_call(
        body, name=name, out_shape=jax.ShapeDtypeStruct((M, N), f32), grid=(M // tm, N // tn),
        in_specs=[pl.BlockSpec((tm, n1 * slab), lambda i, j: (i, 0)), pl.BlockSpec((n2, tm, slab), lambda i, j: (0, i, 0)),
                  pl.BlockSpec((n_slab, tn, slab), lambda i, j: (0, j, 0))] + [ANY_SPEC] * len(deps),
        out_specs=pl.BlockSpec((tm, tn), lambda i, j: (i, j)), compiler_params=_params("parallel", "parallel"),
    )(a_cols, a_slots, b, *deps)


def _ada_exchange(c_blk, cw_slab, w_ada, b_cols):
    nblk = c_blk.shape[0]
    D, W = w_ada.shape
    CW = cw_slab.shape[1]

    def body(c_ref, cw_ref, w_ref, b_ref, mod_ref, call_ref, cwg_ref, msend, send_sems, recv_sems):
        x, y, c, me = _my_place()
        call_ref[me] = _silu(c_ref[...])
        cwg_ref[me] = cw_ref[...]
        first = []
        for k in range(1, NDEV):
            peer, _ = _peer(x, y, c, k)
            first.append(_remote(call_ref.at[me], call_ref.at[me], send_sems.at[0, k], recv_sems.at[0, k], peer))
            first.append(_remote(cwg_ref.at[me], cwg_ref.at[me], send_sems.at[1, k], recv_sems.at[1, k], peer))
        for cp in first:
            cp.start()
        for k in range(1, NDEV):
            peer, slot = _peer(x, y, c, k)
            _remote(call_ref.at[slot], call_ref.at[slot], send_sems.at[0, k], recv_sems.at[0, k], peer).wait_recv()
            _remote(cwg_ref.at[slot], cwg_ref.at[slot], send_sems.at[1, k], recv_sems.at[1, k], peer).wait_recv()
        mod = jnp.broadcast_to(b_ref[...], (NDEV, W))
        for r in range(nblk):
            mod = mod + lax.dot_general(call_ref[:, r, :], w_ref[r * 128:(r + 1) * 128, :], (((1,), (0,)), ((), ())),
                                        preferred_element_type=f32, precision=lax.Precision.HIGHEST)
        row = lax.broadcasted_iota(jnp.int32, (NDEV, 1), 0)
        pick = lambda j: jnp.broadcast_to(jnp.sum(jnp.where(row == j, mod, 0.0), axis=0, keepdims=True), (8, W))
        mod_ref[me] = pick(me)
        second = []
        for k in range(1, NDEV):
            peer, slot = _peer(x, y, c, k)
            msend[k] = pick(slot)
            second.append(_remote(msend.at[k], mod_ref.at[me], send_sems.at[2, k], recv_sems.at[2, k], peer))
        for cp in second:
            cp.start()
        for k in range(1, NDEV):
            peer, slot = _peer(x, y, c, k)
            _remote(msend.at[k], mod_ref.at[slot], send_sems.at[2, k], recv_sems.at[2, k], peer).wait_recv()
        for cp in first + second:
            cp.wait_send()

    vmem = pl.BlockSpec(memory_space=pltpu.VMEM)
    return pl.pallas_call(
        body, name="ada_exchange",
        out_shape=(jax.ShapeDtypeStruct((NDEV, 8, W), f32), jax.ShapeDtypeStruct((NDEV, nblk, 128), f32),
                   jax.ShapeDtypeStruct((NDEV, 8, CW), f32)),
        in_specs=[vmem] * 4, out_specs=(vmem, vmem, vmem),
        scratch_shapes=[pltpu.VMEM((NDEV, 8, W), f32), pltpu.SemaphoreType.DMA((3, NDEV)),
                        pltpu.SemaphoreType.DMA((3, NDEV))],
        compiler_params=_params(),
    )(c_blk, cw_slab, w_ada, b_cols)


def _prenorm(x, scale, shift, g_pre, dep, tr=512):
    S, D = x.shape
    tr = min(tr, S)

    def body(x_ref, sc_ref, sh_ref, g_ref, dep_ref, h_ref):
        xv = x_ref[...]
        r = lax.rsqrt(jnp.mean(xv * xv, axis=-1, keepdims=True) + EPS)
        h_ref[...] = ((xv * r) * g_ref[...] * (1.0 + sc_ref[...]) + sh_ref[...]).astype(bf16)

    row = pl.BlockSpec((tr, D), lambda i: (i, 0))
    vec = pl.BlockSpec((1, D), lambda i: (0, 0))
    return pl.pallas_call(body, name="prenorm", out_shape=jax.ShapeDtypeStruct((S, D), bf16), grid=(S // tr,),
                          in_specs=[row, vec, vec, vec, ANY_SPEC], out_specs=row, compiler_params=_params("parallel"))(
                              x, scale, shift, g_pre, dep)


def _ext_rows(i, tr, S):
    g = lax.broadcasted_iota(jnp.int32, (tr + 16, 1), 0) + (i * tr - 8)
    return (g >= 0) & (g < S)


def _halo_specs(tr, S, C, col):
    nb8 = S // 8
    main = pl.BlockSpec((tr, C), lambda i: (i, col))
    prev = pl.BlockSpec((8, C), lambda i: (jnp.maximum(i * (tr // 8) - 1, 0), col))
    nxt = pl.BlockSpec((8, C), lambda i: (jnp.minimum((i + 1) * (tr // 8), nb8 - 1), col))
    return prev, main, nxt


def _conv_fwd(proj, conv_w, conv_b, g_conv, tr=512):
    S, C = proj.shape[0], proj.shape[1] // 8
    tr = min(tr, S)

    def body(up, um, un, cp, cm, cn, bg_ref, zc_ref, w_ref, cb_ref, g_ref, o_ref):
        i = pl.program_id(0)
        exists = _ext_rows(i, tr, S)
        u = jnp.concatenate([up[...], um[...], un[...]], axis=0)
        cg = jnp.concatenate([cp[...], cm[...], cn[...]], axis=0)
        t = jnp.where(exists, cg * u, 0.0)
        t_before = pltpu.roll(t, 1, 0)[8:tr + 8]
        t_after = pltpu.roll(t, tr + 15, 0)[8:tr + 8]
        w = w_ref[...]
        cv = w[0:1] * t_before + w[1:2] * t[8:tr + 8] + w[2:3] * t_after + cb_ref[...]
        yc = bg_ref[...] * cv
        rc = lax.rsqrt(jnp.mean(yc * yc, axis=-1, keepdims=True) + EPS)
        o_ref[...] = ((yc * rc) * g_ref[...] * _silu(zc_ref[...])).astype(bf16)

    u_specs = _halo_specs(tr, S, C, 0)
    c_specs = _halo_specs(tr, S, C, 2)
    vec = pl.BlockSpec((1, C), lambda i: (0, 0))
    return pl.pallas_call(
        body, name="conv_fwd", out_shape=jax.ShapeDtypeStruct((S, 2 * C), bf16), grid=(S // tr,),
        in_specs=[*u_specs, *c_specs, pl.BlockSpec((tr, C), lambda i: (i, 1)), pl.BlockSpec((tr, C), lambda i: (i, 3)),
                  pl.BlockSpec((8, C), lambda i: (0, 0)), vec, vec],
        out_specs=pl.BlockSpec((tr, C), lambda i: (i, 0)), compiler_params=_params("parallel"),
    )(proj, proj, proj, proj, proj, proj, proj, proj, conv_w, conv_b, g_conv)


def _branch_geometry(S, r, inter):
    L = S // r * inter
    nq = min(128, L)
    nk = min(nq + 2 * HALF_WIN * inter, L)
    assert L % nq == 0 and (L == nk or L >= nq + 2 * HALF_WIN * inter)
    return L, nq, nk, L // nq


QUAD = 4


def _to_quad(dst, src, S):
    n = S // QUAD
    for rho in range(QUAD):
        dst[pl.ds(rho * n, n), :] = src[pl.ds(rho, n, stride=QUAD), :]


def _block_rows(idx, r, inter, S, L, nq, nk, nblk):
    rho, qb = (0, idx) if r == 1 else (idx // nblk, idx % nblk)
    i0 = qb * nq
    ws = jnp.clip(i0 - HALF_WIN * inter, 0, L - nk)
    if r == 1:
        return pl.ds(pl.multiple_of(i0, 8), nq), pl.ds(pl.multiple_of(ws, 8), nk), i0 - ws
    assert r % (QUAD * inter) == 0
    step = r // QUAD // inter
    base = (rho % QUAD) * (S // QUAD) + rho // QUAD
    if step == 1:
        return pl.ds(pl.multiple_of(base + i0, 8), nq), pl.ds(pl.multiple_of(base + ws, 8), nk), i0 - ws
    return pl.ds(base + step * i0, nq, stride=step), pl.ds(base + step * ws, nk, stride=step), i0 - ws


N_CASES = 3
SCALE = HEAD_DIM ** -0.5
ATTN_UNROLL = 16


def _bias_shape(S):
    shapes = [_branch_geometry(S, r, inter)[1:3] for _, r, inter in BRANCHES]
    return (len(BRANCHES) * N_CASES * 2, max(nq for nq, _ in shapes), max(nk for _, nk in shapes))


def _bias_index(b, case, head):
    return (b * N_CASES + case) * 2 + head


def _fill_bias(bias_scr, sl_ref, S):
    sl = sl_ref[...]
    slope = (sl[0:1, 0:1], sl[0:1, HEAD_DIM:HEAD_DIM + 1])
    for b, (_, r, inter) in enumerate(BRANCHES):
        L, nq, nk, nblk = _branch_geometry(S, r, inter)
        rel = lax.broadcasted_iota(jnp.int32, (nq, nk), 0) - lax.broadcasted_iota(jnp.int32, (nq, nk), 1)
        for case in range(N_CASES):
            d = jnp.abs(rel + case * HALF_WIN)
            valid = d <= HALF_WIN * inter
            if inter > 1:
                valid = valid & (jnp.bitwise_and(d, inter - 1) == 0)
            dist = d.astype(f32) * float(r // inter)
            for head in range(2):
                bias_scr[_bias_index(b, case, head), 0:nq, 0:nk] = jnp.where(valid, -slope[head] * dist, NEG_INF)


def _bias_tiles(slopes, S, dep):
    npair = slopes.shape[0]
    shape = _bias_shape(S)

    def body(sl_ref, dep_ref, o_ref):
        _fill_bias(o_ref, sl_ref, S)

    return pl.pallas_call(
        body, name="bias_tiles", out_shape=jax.ShapeDtypeStruct((npair,) + shape, f32), grid=(npair,),
        in_specs=[pl.BlockSpec((None, 8, PAIR), lambda p: (p, 0, 0)), ANY_SPEC],
        out_specs=pl.BlockSpec((None,) + shape, lambda p: (p, 0, 0, 0)), compiler_params=_params("parallel"),
    )(slopes, dep)


def _head_slopes(n_heads):
    slopes = 2.0 ** (-8.0 * jnp.arange(1, n_heads + 1, dtype=f32) / n_heads)
    return jnp.broadcast_to(jnp.repeat(slopes.reshape(n_heads // 2, 2), HEAD_DIM, axis=1)[:, None, :],
                            (n_heads // 2, 8, PAIR))


def _attn_fwd(proj, bias):
    S, C = proj.shape[0], proj.shape[1] // 8
    npair = C // PAIR

    def body(q_ref, k_ref, v_ref, bias_scr, o_ref, lse_ref, m_scr, l_scr, a_scr, q4_scr, k4_scr, v4_scr):
        lane = lax.broadcasted_iota(jnp.int32, (1, PAIR), 1)
        first = lane < HEAD_DIM
        for dst, src in ((q4_scr, q_ref), (k4_scr, k_ref), (v4_scr, v_ref)):
            _to_quad(dst, src, S)

        for b, (_, r, inter) in enumerate(BRANCHES):
            L, nq, nk, nblk = _branch_geometry(S, r, inter)
            qs, ks, vs = (q_ref, k_ref, v_ref) if r == 1 else (q4_scr, k4_scr, v4_scr)

            def step(idx, carry, b=b, r=r, L=L, nq=nq, nk=nk, nblk=nblk, qs=qs, ks=ks, vs=vs):
                qrows, krows, off = _block_rows(idx, r, inter, S, L, nq, nk, nblk)
                case = off // HALF_WIN
                q2 = qs[qrows, :] * SCALE
                k2 = ks[krows, :].astype(bf16)
                v2 = vs[krows, :].astype(bf16)
                ms, accs = [], []
                for hh in range(2):
                    mine = first if hh == 0 else ~first
                    qh = jnp.where(mine, q2, 0.0).astype(bf16)
                    s = lax.dot_general(qh, k2, (((1,), (1,)), ((), ())), preferred_element_type=f32)
                    s = s + bias_scr[_bias_index(b, case, hh), 0:nq, 0:nk]
                    m = jnp.max(s, axis=-1, keepdims=True)
                    p = jnp.exp(s - m).astype(bf16)
                    vh = jnp.where(mine, v2, jnp.ones_like(v2))
                    ms.append(m)
                    accs.append(jnp.dot(p, vh, preferred_element_type=f32))
                m_scr[b, qrows, :] = jnp.where(first, ms[0], ms[1])
                a_scr[b, qrows, :] = jnp.where(first, accs[0], accs[1])
                l_scr[b, qrows, :] = jnp.where(first, accs[1], accs[0])
                return carry

            lax.fori_loop(0, S // nq, step, 0, unroll=min(ATTN_UNROLL, S // nq))

        n4 = S // QUAD
        ch = min(256, n4)
        nch = n4 // ch

        def merge(i, carry):
            rho, part = i // nch, i % nch
            sorted_rows = pl.ds(pl.multiple_of(rho * n4 + part * ch, 8), ch)
            token_rows = pl.ds(rho + QUAD * part * ch, ch, stride=QUAD)
            rows = (token_rows,) + (sorted_rows,) * (len(BRANCHES) - 1)
            ms = [m_scr[b, rows[b], :] for b in range(len(BRANCHES))]
            m = functools.reduce(jnp.maximum, ms)
            l = jnp.zeros((ch, PAIR), f32)
            acc = jnp.zeros((ch, PAIR), f32)
            for b in range(len(BRANCHES)):
                w = jnp.exp(ms[b] - m)
                l = l + w * pltpu.roll(l_scr[b, rows[b], :], HEAD_DIM, 1)
                acc = acc + w * a_scr[b, rows[b], :]
            o_ref[token_rows, :] = acc / l
            lse_ref[token_rows, :] = m + jnp.log(l)
            return carry

        lax.fori_loop(0, QUAD * nch, merge, 0, unroll=2)

    blk = lambda part: pl.BlockSpec((S, PAIR), lambda p: (0, part * npair + p))
    out = pl.BlockSpec((S, PAIR), lambda p: (0, p))
    return pl.pallas_call(
        body, name="attn_fwd",
        out_shape=(jax.ShapeDtypeStruct((S, C), f32), jax.ShapeDtypeStruct((S, C), f32)), grid=(npair,),
        in_specs=[blk(4), blk(5), blk(6), pl.BlockSpec((None,) + _bias_shape(S), lambda p: (p, 0, 0, 0))],
        out_specs=(out, out),
        scratch_shapes=[pltpu.VMEM((3, S, PAIR), f32)] * 3 + [pltpu.VMEM((S, PAIR), f32)] * 3,
        compiler_params=_params("parallel"),
    )(proj, proj, proj, bias)


def _attn_post(ycat, o, proj, g_attn, tr=512):
    S, C = o.shape
    tr = min(tr, S)

    def body(y_ref, o_ref, z_ref, g_ref, out_ref):
        del y_ref
        ov = o_ref[...]
        ra = lax.rsqrt(jnp.mean(ov * ov, axis=-1, keepdims=True) + EPS)
        out_ref[...] = ((ov * ra) * g_ref[...] * _silu(z_ref[...])).astype(bf16)

    return pl.pallas_call(
        body, name="attn_post", out_shape=jax.ShapeDtypeStruct(ycat.shape, ycat.dtype), grid=(S // tr,),
        in_specs=[HBM_SPEC, pl.BlockSpec((tr, C), lambda i: (i, 0)), pl.BlockSpec((tr, C), lambda i: (i, 7)),
                  pl.BlockSpec((1, C), lambda i: (0, 0))],
        out_specs=pl.BlockSpec((tr, C), lambda i: (i, 1)), input_output_aliases={0: 0},
        compiler_params=_params("arbitrary"),
    )(ycat, o, proj, g_attn)


def _residual_minus_target(x, target, dep, tr=512):
    S, D = x.shape
    tr = min(tr, S)

    def body(x_ref, t_ref, dep_ref, o_ref):
        o_ref[...] = x_ref[...] - t_ref[...]

    row = pl.BlockSpec((tr, D), lambda i: (i, 0))
    return pl.pallas_call(body, name="residual_minus_target", out_shape=jax.ShapeDtypeStruct((S, D), f32),
                          grid=(S // tr,), in_specs=[row, row, ANY_SPEC], out_specs=row,
                          compiler_params=_params("parallel"))(x, target, dep)


def _sandwich(y, x_minus_t, gate, g_post, tr=256):
    S, D = y.shape
    tr = min(tr, S)

    def body(y_ref, xt_ref, gate_ref, g_ref, dy_ref, dout_ref, sums_ref):
        i = pl.program_id(0)
        gate, g = gate_ref[...], g_ref[...]
        gg = gate * g
        yv = y_ref[...]
        rp = lax.rsqrt(jnp.mean(yv * yv, axis=-1, keepdims=True) + EPS)
        yhat = yv * rp
        err = xt_ref[...] + gg * yhat
        dout = err * (1.0 / D)
        dout_ref[...] = dout
        q = dout * yhat
        w = dout * gg
        dy_ref[...] = (rp * (w - yhat * jnp.sum(q * gg, axis=-1, keepdims=True) * (1.0 / D))).astype(bf16)
        loss = 0.5 * jnp.sum(jnp.mean(err * err, axis=-1, keepdims=True), axis=0, keepdims=True)
        q_sum = jnp.sum(q, axis=0, keepdims=True)
        row = lax.broadcasted_iota(jnp.int32, (8, D), 0)
        upd = jnp.where(row == 0, q_sum * g, jnp.where(row == 1, q_sum * gate, jnp.where(row == 2, loss, 0.0)))

        @pl.when(i == 0)
        def _():
            sums_ref[...] = upd

        @pl.when(i > 0)
        def _():
            sums_ref[...] += upd

    row = pl.BlockSpec((tr, D), lambda i: (i, 0))
    vec = pl.BlockSpec((1, D), lambda i: (0, 0))
    return pl.pallas_call(
        body, name="sandwich",
        out_shape=(jax.ShapeDtypeStruct((S, D), bf16), jax.ShapeDtypeStruct((S, D), f32), jax.ShapeDtypeStruct((8, D), f32)),
        grid=(S // tr,), in_specs=[row, row, vec, vec],
        out_specs=(row, row, pl.BlockSpec((8, D), lambda i: (0, 0))), compiler_params=_params("arbitrary"),
    )(y, x_minus_t, gate, g_post)


def _conv_bwd(proj, dycat, conv_w, conv_b, g_conv, dep, tr=256):
    S, C = proj.shape[0], proj.shape[1] // 8
    tr = min(tr, S)
    n = tr + 16

    def body(*refs):
        ins, (w_ref, cb_ref, g_ref, _, dp_ref, sums_ref) = refs[:15], refs[15:]
        i = pl.program_id(0)
        exists = _ext_rows(i, tr, S)
        u, bg, cg, zc, dyn = (jnp.concatenate([ins[3 * t][...], ins[3 * t + 1][...], ins[3 * t + 2][...]], axis=0)
                              for t in range(5))
        w = w_ref[...]
        t = jnp.where(exists, cg * u, 0.0)
        t_before, t_after = pltpu.roll(t, 1, 0), pltpu.roll(t, n - 1, 0)
        cv = w[0:1] * t_before + w[1:2] * t + w[2:3] * t_after + cb_ref[...]
        yc = bg * cv
        rc = lax.rsqrt(jnp.mean(yc * yc, axis=-1, keepdims=True) + EPS)
        yhat = yc * rc
        sz, dsz = _silu_and_slope(zc)
        wgt = dyn * g_ref[...] * sz
        dyc = rc * (wgt - yhat * jnp.mean(wgt * yhat, axis=-1, keepdims=True))
        dcv = jnp.where(exists, dyc * bg, 0.0)
        dt = w[0:1] * pltpu.roll(dcv, n - 1, 0) + w[1:2] * dcv + w[2:3] * pltpu.roll(dcv, 1, 0)
        mid = slice(8, tr + 8)
        dp_ref[:, 0:C] = (dt * cg)[mid].astype(bf16)
        dp_ref[:, C:2 * C] = (dyc * cv)[mid].astype(bf16)
        dp_ref[:, 2 * C:3 * C] = (dt * u)[mid].astype(bf16)
        dp_ref[:, 3 * C:4 * C] = (dyn * yhat * g_ref[...] * dsz)[mid].astype(bf16)
        colsum = lambda v: jnp.sum(v[mid], axis=0, keepdims=True)
        parts = [colsum(dyn * yhat * sz), colsum(dcv), colsum(dcv * t_before), colsum(dcv * t), colsum(dcv * t_after)]
        row = lax.broadcasted_iota(jnp.int32, (8, C), 0)
        upd = jnp.zeros((8, C), f32)
        for j, pj in enumerate(parts):
            upd = jnp.where(row == j, pj, upd)

        @pl.when(i == 0)
        def _():
            sums_ref[...] = upd

        @pl.when(i > 0)
        def _():
            sums_ref[...] += upd

    specs = []
    for col in range(4):
        specs += _halo_specs(tr, S, C, col)
    specs += _halo_specs(tr, S, C, 0)
    vec = pl.BlockSpec((1, C), lambda i: (0, 0))
    return pl.pallas_call(
        body, name="conv_bwd",
        out_shape=(jax.ShapeDtypeStruct((S, 4 * C), bf16), jax.ShapeDtypeStruct((8, C), f32)), grid=(S // tr,),
        in_specs=[*specs, pl.BlockSpec((8, C), lambda i: (0, 0)), vec, vec, ANY_SPEC],
        out_specs=(pl.BlockSpec((tr, 4 * C), lambda i: (i, 0)), pl.BlockSpec((8, C), lambda i: (0, 0))),
        compiler_params=_params("arbitrary"),
    )(*([proj] * 12), dycat, dycat, dycat, conv_w, conv_b, g_conv, dep)


def _attn_post_bwd(o, proj, dycat, g_attn, dep, tr=512):
    S, C = o.shape
    tr = min(tr, S)

    def body(o_ref, z_ref, dy_ref, g_ref, dep_ref, do_ref, dz_ref, sums_ref):
        i = pl.program_id(0)
        ov, zv, dyn = o_ref[...], z_ref[...], dy_ref[...]
        ra = lax.rsqrt(jnp.mean(ov * ov, axis=-1, keepdims=True) + EPS)
        ohat = ov * ra
        sz, dsz = _silu_and_slope(zv)
        wgt = dyn * g_ref[...] * sz
        do_ref[...] = ra * (wgt - ohat * jnp.mean(wgt * ohat, axis=-1, keepdims=True))
        dz_ref[...] = (dyn * ohat * g_ref[...] * dsz).astype(bf16)
        row = lax.broadcasted_iota(jnp.int32, (8, C), 0)
        upd = jnp.where(row == 0, jnp.sum(dyn * ohat * sz, axis=0, keepdims=True), 0.0)

        @pl.when(i == 0)
        def _():
            sums_ref[...] = upd

        @pl.when(i > 0)
        def _():
            sums_ref[...] += upd

    return pl.pallas_call(
        body, name="attn_post_bwd",
        out_shape=(jax.ShapeDtypeStruct((S, C), f32), jax.ShapeDtypeStruct((4, S, C), bf16),
                   jax.ShapeDtypeStruct((8, C), f32)),
        grid=(S // tr,),
        in_specs=[pl.BlockSpec((tr, C), lambda i: (i, 0)), pl.BlockSpec((tr, C), lambda i: (i, 7)),
                  pl.BlockSpec((tr, C), lambda i: (i, 1)), pl.BlockSpec((1, C), lambda i: (0, 0)), ANY_SPEC],
        out_specs=(pl.BlockSpec((tr, C), lambda i: (i, 0)), pl.BlockSpec((None, tr, C), lambda i: (3, i, 0)),
                   pl.BlockSpec((8, C), lambda i: (0, 0))),
        compiler_params=_params("arbitrary"),
    )(o, proj, dycat, g_attn, dep)


def _attn_bwd(proj, o, do, lse, bias, dqkvz, dep):
    S, C = o.shape
    npair = C // PAIR

    def body(q_ref, k_ref, v_ref, o_ref, do_ref, lse_ref, bias_scr, old_ref, dep_ref, dqkv_ref,
             acc_scr, dl_scr, quad_scr):
        lane = lax.broadcasted_iota(jnp.int32, (1, PAIR), 1)
        first = lane < HEAD_DIM
        ch = min(256, S)

        def prep(i, carry):
            rows = pl.ds(pl.multiple_of(i * ch, 8), ch)
            prod = do_ref[rows, :] * o_ref[rows, :]
            d0 = jnp.sum(jnp.where(first, prod, 0.0), axis=-1, keepdims=True)
            d1 = jnp.sum(jnp.where(first, 0.0, prod), axis=-1, keepdims=True)
            dl_scr[rows, :] = jnp.where(first, d0, d1)
            zero = jnp.zeros((ch, PAIR), f32)
            for order in range(2):
                for t in range(3):
                    acc_scr[order, t, rows, :] = zero
            return carry

        lax.fori_loop(0, S // ch, prep, 0, unroll=2)
        token_srcs = (q_ref, k_ref, v_ref, do_ref, lse_ref, dl_scr)
        for j, src in enumerate(token_srcs):
            _to_quad(quad_scr.at[j], src, S)

        for b, (_, r, inter) in enumerate(BRANCHES):
            L, nq, nk, nblk = _branch_geometry(S, r, inter)
            order = 0 if r == 1 else 1
            srcs = token_srcs if r == 1 else tuple(quad_scr.at[j] for j in range(6))

            def step(idx, carry, b=b, r=r, L=L, nq=nq, nk=nk, nblk=nblk, order=order, srcs=srcs):
                qs, ks, vs, dos, lses, dls = srcs
                dq_scr, dk_scr, dv_scr = (acc_scr.at[order, t] for t in range(3))
                qrows, krows, off = _block_rows(idx, r, inter, S, L, nq, nk, nblk)
                case = off // HALF_WIN
                q2 = qs[qrows, :] * SCALE
                k2 = ks[krows, :].astype(bf16)
                v2 = vs[krows, :].astype(bf16)
                do2 = dos[qrows, :]
                lse2 = lses[qrows, :]
                dl2 = dls[qrows, :]
                dq2 = jnp.zeros((nq, PAIR), f32)
                dk2 = jnp.zeros((nk, PAIR), f32)
                dv2 = jnp.zeros((nk, PAIR), f32)
                for hh in range(2):
                    mine = first if hh == 0 else ~first
                    lo = hh * HEAD_DIM
                    qh = jnp.where(mine, q2, 0.0).astype(bf16)
                    doh = jnp.where(mine, do2, 0.0).astype(bf16)
                    s = lax.dot_general(qh, k2, (((1,), (1,)), ((), ())), preferred_element_type=f32)
                    s = s + bias_scr[_bias_index(b, case, hh), 0:nq, 0:nk]
                    p = jnp.exp(s - lse2[:, lo:lo + 1])
                    dv2 = dv2 + lax.dot_general(p.astype(bf16), doh, (((0,), (0,)), ((), ())), preferred_element_type=f32)
                    dp = lax.dot_general(doh, v2, (((1,), (1,)), ((), ())), preferred_element_type=f32)
                    ds = (p * (dp - dl2[:, lo:lo + 1])).astype(bf16)
                    dq2 = dq2 + jnp.where(mine, jnp.dot(ds, k2, preferred_element_type=f32), 0.0)
                    dk2 = dk2 + lax.dot_general(ds, qh, (((0,), (0,)), ((), ())), preferred_element_type=f32)
                dq_scr[qrows, :] = dq_scr[qrows, :] + dq2
                dk_scr[krows, :] = dk_scr[krows, :] + dk2
                dv_scr[krows, :] = dv_scr[krows, :] + dv2
                return carry

            lax.fori_loop(0, S // nq, step, 0, unroll=min(ATTN_UNROLL, S // nq))

        n4 = S // QUAD
        for t in range(3):
            for rho in range(QUAD):
                token_rows = pl.ds(rho, n4, stride=QUAD)
                acc_scr[0, t, token_rows, :] = acc_scr[0, t, token_rows, :] + acc_scr[1, t, pl.ds(rho * n4, n4), :]
        dqkv_ref[0] = (acc_scr[0, 0] * SCALE).astype(bf16)
        dqkv_ref[1] = acc_scr[0, 1].astype(bf16)
        dqkv_ref[2] = acc_scr[0, 2].astype(bf16)

    blk = lambda part: pl.BlockSpec((S, PAIR), lambda p: (0, part * npair + p))
    own = pl.BlockSpec((S, PAIR), lambda p: (0, p))
    return pl.pallas_call(
        body, name="attn_bwd", out_shape=jax.ShapeDtypeStruct(dqkvz.shape, dqkvz.dtype), grid=(npair,),
        in_specs=[blk(4), blk(5), blk(6), own, own, own,
                  pl.BlockSpec((None,) + _bias_shape(S), lambda p: (p, 0, 0, 0)), ANY_SPEC, ANY_SPEC],
        out_specs=pl.BlockSpec((3, S, PAIR), lambda p: (0, 0, p)), input_output_aliases={7: 0},
        scratch_shapes=[pltpu.VMEM((2, 3, S, PAIR), f32), pltpu.VMEM((S, PAIR), f32), pltpu.VMEM((6, S, PAIR), f32)],
        compiler_params=_params("arbitrary"),
    )(proj, proj, proj, o, do, lse, bias, dqkvz, dep)


def _prenorm_bwd(dh, x, dout, scale, g_pre, tr=256):
    S, D = x.shape
    tr = min(tr, S)

    def body(dh_ref, x_ref, dout_ref, sc_ref, g_ref, gx_ref, sums_ref):
        i = pl.program_id(0)
        xv, dhv = x_ref[...], dh_ref[...]
        r = lax.rsqrt(jnp.mean(xv * xv, axis=-1, keepdims=True) + EPS)
        xn = xv * r
        dxn = dhv * (g_ref[...] * (1.0 + sc_ref[...]))
        gx_ref[...] = dout_ref[...] + r * (dxn - xn * jnp.mean(dxn * xn, axis=-1, keepdims=True))
        dhx = dhv * xn
        row = lax.broadcasted_iota(jnp.int32, (8, D), 0)
        upd = jnp.where(row == 0, jnp.sum(dhv, axis=0, keepdims=True),
                        jnp.where(row == 1, jnp.sum(dhx, axis=0, keepdims=True) * g_ref[...],
                                  jnp.where(row == 2, jnp.sum(dhx, axis=0, keepdims=True) * (1.0 + sc_ref[...]), 0.0)))

        @pl.when(i == 0)
        def _():
            sums_ref[...] = upd

        @pl.when(i > 0)
        def _():
            sums_ref[...] += upd

    row = pl.BlockSpec((tr, D), lambda i: (i, 0))
    vec = pl.BlockSpec((1, D), lambda i: (0, 0))
    return pl.pallas_call(
        body, name="prenorm_bwd",
        out_shape=(jax.ShapeDtypeStruct((S, D), f32), jax.ShapeDtypeStruct((8, D), f32)), grid=(S // tr,),
        in_specs=[row, row, row, vec, vec], out_specs=(row, pl.BlockSpec((8, D), lambda i: (0, 0))),
        compiler_params=_params("arbitrary"),
    )(dh, x, dout, scale, g_pre)


def _adamw(w, g, m, v):
    m = ADAM_B1 * m + (1.0 - ADAM_B1) * g
    v = ADAM_B2 * v + (1.0 - ADAM_B2) * (g * g)
    m_hat = m / (1.0 - ADAM_B1 ** ADAM_STEP)
    v_hat = v / (1.0 - ADAM_B2 ** ADAM_STEP)
    delta = -ADAM_LR * (m_hat / (jnp.sqrt(v_hat) + ADAM_EPS) + ADAM_WD * w)
    return delta, m, v


def _sum_rows(parts, dep):
    P = parts.shape[1]

    def body(p_ref, dep_ref, o_ref):
        acc = p_ref[0:1, :]
        for j in range(1, NDEV):
            acc = acc + p_ref[j:j + 1, :]
        o_ref[...] = jnp.broadcast_to(acc, (8, P))

    vmem = pl.BlockSpec(memory_space=pltpu.VMEM)
    return pl.pallas_call(body, name="sum_small", out_shape=jax.ShapeDtypeStruct((8, P), f32),
                          in_specs=[vmem, ANY_SPEC], out_specs=vmem, compiler_params=_params())(parts, dep)


def _adamw_small(tot, params):
    given = [p[3] for p in params if not isinstance(p[3], int)]

    def body(tot_ref, *refs):
        given_refs = list(refs[:len(given)])
        ins = refs[len(given):len(given) + 3 * len(params)]
        outs = refs[len(given) + 3 * len(params):]
        for t, (w, _, _, where) in enumerate(params):
            w_ref, m_ref, v_ref = ins[3 * t:3 * t + 3]
            g = tot_ref[0:1, where:where + w.size] if isinstance(where, int) else given_refs.pop(0)[...]
            outs[4 * t][...] = g
            outs[4 * t + 1][...], outs[4 * t + 2][...], outs[4 * t + 3][...] = _adamw(w_ref[...], g, m_ref[...], v_ref[...])

    out_shape = tuple(jax.ShapeDtypeStruct(p[0].shape, f32) for p in params for _ in range(4))
    res = pl.pallas_call(body, name="adamw_small", out_shape=out_shape, compiler_params=_params())(
        tot, *given, *[a for p in params for a in p[:3]])
    return [res[4 * t:4 * t + 4] for t in range(len(params))]


def _adamw_sharded(name, parts, sums_a, sums_b, pick, w, m, v, rows=None, prev=None, tr=128):
    R, Cc = w.shape
    r0, nr = rows or (0, R)
    tr = math.gcd(tr, r0, nr)
    n, b0 = parts.shape[0], r0 // tr

    def body(pick_ref, p_ref, a_ref, b_ref, w_ref, m_ref, v_ref, *rest):
        g_ref, d_ref, nm_ref, nv_ref = rest[-4:]
        g = jnp.where(pick_ref[0] == 1, b_ref[...], a_ref[...]).astype(f32)
        for j in range(n):
            g = g + p_ref[j].astype(f32)
        g_ref[...] = g
        d_ref[...], nm_ref[...], nv_ref[...] = _adamw(w_ref[...], g, m_ref[...], v_ref[...])

    row = pl.BlockSpec((tr, Cc), lambda i, pick: (i + b0, 0))
    mine = pl.BlockSpec((None, tr, Cc), lambda i, pick: (pick[1], i + b0, 0))
    out = jax.ShapeDtypeStruct((R, Cc), f32)
    prev = list(prev or [])
    grid_spec = pltpu.PrefetchScalarGridSpec(
        num_scalar_prefetch=1, grid=(nr // tr,),
        in_specs=[pl.BlockSpec((n, tr, Cc), lambda i, pick: (0, i + b0, 0)), mine, mine, row, row, row]
        + [ANY_SPEC] * len(prev),
        out_specs=(row, row, row, row))
    return pl.pallas_call(
        body, name=name, out_shape=(out, out, out, out), grid_spec=grid_spec,
        input_output_aliases={7 + t: t for t in range(len(prev))}, compiler_params=_params("arbitrary"),
    )(pick, parts, sums_a, sums_b, w, m, v, *prev)


def _adamw_ada(c_t, dmod_cols, w, m, v, dep, tr=512):
    D, W = w.shape
    tr = min(tr, D)

    def body(c_ref, dm_ref, w_ref, m_ref, v_ref, dep_ref, g_ref, d_ref, nm_ref, nv_ref):
        g = lax.dot_general(c_ref[...], dm_ref[...], (((1,), (0,)), ((), ())), preferred_element_type=f32,
                            precision=lax.Precision.HIGHEST)
        g_ref[...] = g
        d_ref[...], nm_ref[...], nv_ref[...] = _adamw(w_ref[...], g, m_ref[...], v_ref[...])

    row = pl.BlockSpec((tr, W), lambda i: (i, 0))
    out = jax.ShapeDtypeStruct((D, W), f32)
    return pl.pallas_call(
        body, name="adamw_ada", out_shape=(out, out, out, out), grid=(D // tr,),
        in_specs=[pl.BlockSpec((tr, NDEV), lambda i: (i, 0)), pl.BlockSpec((NDEV, W), lambda i: (0, 0)), row, row, row,
                  ANY_SPEC],
        out_specs=(row, row, row, row), compiler_params=_params("parallel"),
    )(c_t, dmod_cols, w, m, v, dep)


def kernel(x, c, w_ada, b_ada, g_pre, w_in, conv_w, conv_b, g_conv, g_attn, w_out, g_post, loss_target, m_w_ada, m_b_ada, m_g_pre, m_w_in, m_conv_w, m_conv_b, m_g_conv, m_g_attn, m_w_out, m_g_post, v_w_ada, v_b_ada, v_g_pre, v_w_in, v_conv_w, v_conv_b, v_g_conv, v_g_attn, v_w_out, v_g_post):
    S, D = x.shape[1], x.shape[2]
    C = D // 2
    W = w_ada.shape[2]
    CW = conv_w.shape[2]
    me = 4 * lax.axis_index("x") + 2 * lax.axis_index("y") + lax.axis_index("c")
    x2, tgt = x[0], loss_target[0]
    w_ada2, w_in2, w_out2 = w_ada[0], w_in[0], w_out[0]

    R = D // NDEV
    core = lax.axis_index("c").astype(jnp.int32).reshape(1)

    cw_slab = jnp.zeros((8, CW), f32).at[:3].set(conv_w[0])
    b_cols = lax.dynamic_slice_in_dim(b_ada, me * W, W, axis=1)
    mod_slabs, c_blocks, cw_g = _ada_exchange(c.reshape(D // 128, 128), cw_slab, w_ada2, b_cols)
    c_all = c_blocks.reshape(NDEV, D)
    conv_w_full = jnp.transpose(cw_g, (1, 0, 2)).reshape(8, C)
    mod = mod_slabs[:, 0, :].reshape(1, 3 * D)
    shift, scale, gate = mod[:, :D], mod[:, D:2 * D], mod[:, 2 * D:]

    land_i = lax.dynamic_update_slice(lax.empty((NDEV, D, C), bf16), w_in2.astype(bf16)[None], (me, 0, 0))
    land_o = lax.dynamic_update_slice(lax.empty((NDEV, R, D), bf16), w_out2.astype(bf16)[None], (me, 0, 0))
    wi_send, wi_recv, land_i, w_token = _w_in_start(land_i, [mod_slabs])

    me_arr = me.astype(jnp.int32).reshape(1)
    h = _prenorm(x2, scale, shift, g_pre, w_token)
    land_i = _w_in_sibling(land_i, wi_recv, after=[h])
    proj = _in_proj_part("in_proj_a", h, land_i, None, me_arr, 0, 1, 2)
    x_minus_t = _residual_minus_target(x2, tgt, proj)
    bias = _bias_tiles(_head_slopes(C // HEAD_DIM), S, x_minus_t)

    def landing(rows, cols):
        return lax.dynamic_update_slice(lax.empty((NCHIP, rows, cols), bf16), jnp.zeros((1, rows, cols), bf16),
                                        (me // 2, 0, 0))

    land_go, land_gi = landing(R, D), landing(D, C)
    fi_send, fi_recv, land_i = _w_in_relay(land_i, wi_recv, after=[proj, bias, land_go, land_gi])
    proj = _in_proj_part("in_proj_b", h, land_i, proj, me_arr, 2, 2, 2)
    land_i = _w_in_forwarded(land_i, fi_recv, after=[proj])
    proj = _in_proj_part("in_proj_c", h, land_i, proj, me_arr, 3, 2, 2)
    (di_send, di_recv, wo_send, wo_recv), land_i, land_o = _w_in_diag(land_i, land_o, fi_recv, after=[proj])
    proj = _in_proj_part("in_proj_d", h, land_i, proj, me_arr, 6, 1, 1)
    win_g = _w_in_finish(land_i, wi_send, fi_send, di_send, di_recv, after=[proj])
    proj = _in_proj_part("in_proj_e", h, win_g, proj, me_arr, 7, 1, 1)
    ycat = _conv_fwd(proj, conv_w_full, conv_b, g_conv)
    o, lse = _attn_fwd(proj, bias)
    (fo_send, fo_recv), (land_o,), _ = _weights_forward("w_out_forward", land_o, wo_recv, after=[o])
    ycat = _attn_post(ycat, o, proj, g_attn)
    wout_g = _weights_wait("w_out_wait", land_o, wo_send, wo_recv, fo_send, fo_recv, after=[ycat])
    wout_full = wout_g.reshape(D, D)
    y = _matmul(ycat, wout_full, name="out_proj", out_dtype=f32)
    dy, dout, post_sums = _sandwich(y, x_minus_t, gate, g_post)

    gw_out = _matmul(ycat, dy, name="out_proj_dw", out_dtype=bf16, ta=True).reshape(NDEV, R, D)
    dycat = _matmul(dy, wout_full, name="out_proj_dx", out_dtype=f32, tb=True)
    dpc, conv_sums = _conv_bwd(proj, dycat, conv_w_full, conv_b, g_conv, gw_out)
    gw_c = _matmul(h, dpc, name="in_proj_dw_conv", out_dtype=bf16, ta=True, out_slots=4)
    first_pairs, p1_token = _pairs_start("g_first_pair_start", [gw_out, gw_c])
    do, dpa, attn_sums = _attn_post_bwd(o, proj, dycat, g_attn, p1_token)
    (gw_out, pair_o), (gw_c, pair_c) = _pairs_wait("g_first_pair_wait", first_pairs, after=[do])
    sum_o = _pair_sum("g_out_pair_sum", gw_out, pair_o, core)
    sum_c = _pair_sum("g_conv_pair_sum", gw_c, pair_c, core)
    ((co_send, co_recv, sum_o, land_go), (cc_send, cc_recv, sum_c, land_gi)), cc_token = _chips_start(
        "g_first_chip_start", [(sum_o, land_go, 0), (sum_c, land_gi, 0)])
    dpa = _attn_bwd(proj, o, do, lse, bias, dpa, cc_token)
    gw_a = _matmul(h, dpa, name="in_proj_dw_attn", out_dtype=bf16, ta=True, b_slots=True, out_slots=4)
    pa_send, pa_recv, gw_a, pair_a, pa_token = _pair_start("g_attn_pair_start", gw_a)
    sum_o, land_go = _chip_wait("g_out_chip_wait", sum_o, land_go, co_send, co_recv, 0, after=[pa_token])
    pick_out = jnp.stack([jnp.int32(0), me // 2]).astype(jnp.int32)
    g_w_out, d_w_out, nm_w_out, nv_w_out = _adamw_sharded(
        "adamw_w_out", land_go, sum_o, sum_o, pick_out, w_out2, m_w_out[0], v_w_out[0])
    gw_a, pair_a = _pair_wait("g_attn_pair_wait", gw_a, pair_a, pa_send, pa_recv, after=[g_w_out])
    sum_a = _pair_sum("g_attn_pair_sum", gw_a, pair_a, core)
    part_a, part_b = (0, 3 * D // 4), (3 * D // 4, D // 4)
    ca_send, ca_recv, sum_a, land_gi, ca_token = _chip_start("g_attn_chip_start_a", sum_a, land_gi, 4, part_a)
    dh = _matmul_slabs_t(dpc, dpa, win_g, name="in_proj_dx", dep=ca_token)
    grad_x, pre_sums = _prenorm_bwd(dh, x2, dout, scale, g_pre)

    small = jnp.concatenate([pre_sums[0:1], pre_sums[1:2], post_sums[0:1],
                             pre_sums[2:3], post_sums[1:2],
                             conv_sums[2:3], conv_sums[3:4], conv_sums[4:5],
                             conv_sums[1:2], conv_sums[0:1], attn_sums[0:1]], axis=1)
    small = jnp.concatenate([small.reshape(8 * D // 128, 128), jnp.broadcast_to(post_sums[2:3, :128], (8, 128))])
    gs_send, gs_recv, small, small_all, gs_token = _gather_start("gather_small_start", small)

    cb_send, cb_recv, sum_a, land_gi, cb_token = _chip_start("g_attn_chip_start_b", sum_a, land_gi, 4, part_b,
                                                             after=[gs_token])

    pick_in = jnp.stack([me // 4, (me % 4) // 2]).astype(jnp.int32)
    sum_c, land_gi = _chip_wait("g_conv_chip_wait", sum_c, land_gi, cc_send, cc_recv, 0, after=[cb_token])
    sum_a, land_gi = _chip_wait("g_attn_chip_wait_a", sum_a, land_gi, ca_send, ca_recv, 4, [cb_token], part_a)
    first = _adamw_sharded("adamw_w_in_a", land_gi, sum_c, sum_a, pick_in, w_in2, m_w_in[0], v_w_in[0], rows=part_a,
                           tr=256)

    small_all = _gather_wait("gather_small_wait", small, small_all, gs_send, gs_recv, after=[first[0]])
    small_all = small_all.reshape(NDEV, small.size)
    tot = _sum_rows(small_all, gs_token)
    loss = tot[0, 8 * D]
    taps_first = lambda a: jnp.transpose(a, (1, 0, 2))
    g_conv_w = lax.dynamic_slice_in_dim(tot[0:1, 5 * D:5 * D + 3 * C].reshape(3, 1, C), me * CW, CW, axis=2)
    ((g_b_ada, d_b_ada, nm_b_ada, nv_b_ada), (g_g_pre, d_g_pre, nm_g_pre, nv_g_pre),
     (g_g_post, d_g_post, nm_g_post, nv_g_post), conv_w_results,
     (g_conv_b, d_conv_b, nm_conv_b, nv_conv_b), (g_g_conv, d_g_conv, nm_g_conv, nv_g_conv),
     (g_g_attn, d_g_attn, nm_g_attn, nv_g_attn)) = _adamw_small(tot, [
         (b_ada, m_b_ada, v_b_ada, 0), (g_pre, m_g_pre, v_g_pre, 3 * D), (g_post, m_g_post, v_g_post, 4 * D),
         (taps_first(conv_w), taps_first(m_conv_w), taps_first(v_conv_w), g_conv_w),
         (conv_b, m_conv_b, v_conv_b, 5 * D + 3 * C),
         (g_conv, m_g_conv, v_g_conv, 5 * D + 4 * C), (g_attn, m_g_attn, v_g_attn, 5 * D + 5 * C)])
    g_conv_w, d_conv_w, nm_conv_w, nv_conv_w = (taps_first(a) for a in conv_w_results)

    dmod_cols = lax.dynamic_slice_in_dim(small_all[:, :3 * D], me * W, W, axis=1)
    g_w_ada, d_w_ada, nm_w_ada, nv_w_ada = _adamw_ada(c_all.T, dmod_cols, w_ada2, m_w_ada[0], v_w_ada[0], gs_token)

    sum_a, land_gi = _chip_wait("g_attn_chip_wait_b", sum_a, land_gi, cb_send, cb_recv, 4, [g_w_ada], part_b)
    g_w_in, d_w_in, nm_w_in, nv_w_in = _adamw_sharded(
        "adamw_w_in_b", land_gi, sum_c, sum_a, pick_in, w_in2, m_w_in[0], v_w_in[0], rows=part_b, prev=first, tr=256)

    return (loss, grad_x[None],
            g_w_ada[None], g_b_ada, g_g_pre, g_w_in[None], g_conv_w, g_conv_b, g_g_conv, g_g_attn, g_w_out[None], g_g_post,
            d_w_ada[None], d_b_ada, d_g_pre, d_w_in[None], d_conv_w, d_conv_b, d_g_conv, d_g_attn, d_w_out[None], d_g_post,
            nm_w_ada[None], nm_b_ada, nm_g_pre, nm_w_in[None], nm_conv_w, nm_conv_b, nm_g_conv, nm_g_attn, nm_w_out[None], nm_g_post,
            nv_w_ada[None], nv_b_ada, nv_g_pre, nv_w_in[None], nv_conv_w, nv_conv_b, nv_g_conv, nv_g_attn, nv_w_out[None], nv_g_post)
```

```python
import functools
import math

import jax
import jax.numpy as jnp
from jax import lax
from jax.experimental import pallas as pl
from jax.experimental.pallas import tpu as pltpu

f32 = jnp.float32
bf16 = jnp.bfloat16

NDEV = 8
HEAD_DIM = 64
LANES = 128
PAIR = 2 * HEAD_DIM
BRANCHES = ((128, 1, 1), (512, 4, 1), (2048, 16, 2))
HALF_WIN = 64
EPS = 1e-6
NEG_INF = -1e30
ADAM_LR, ADAM_B1, ADAM_B2, ADAM_EPS, ADAM_WD, ADAM_STEP = 0.001, 0.9, 0.999, 1e-08, 0.01, 10
MESH = pl.DeviceIdType.MESH
VMEM_LIMIT = 56 * 1024 * 1024
HBM_SPEC = pl.BlockSpec(memory_space=pltpu.HBM)
ANY_SPEC = pl.BlockSpec(memory_space=pl.ANY)
SEM_SPEC = pl.BlockSpec(memory_space=pltpu.SEMAPHORE)


def _params(*sem):
    return pltpu.CompilerParams(dimension_semantics=sem or None, vmem_limit_bytes=VMEM_LIMIT)


def _silu(z):
    return z * jax.nn.sigmoid(z)


def _silu_and_slope(z):
    s = jax.nn.sigmoid(z)
    return z * s, s * (1.0 + z * (1.0 - s))


def _my_place():
    x, y, c = lax.axis_index("x"), lax.axis_index("y"), lax.axis_index("c")
    return x, y, c, 4 * x + 2 * y + c


def _peer(x, y, c, k):
    px, py, pc = x ^ (k >> 2 & 1), y ^ (k >> 1 & 1), c ^ (k & 1)
    return (px, py, pc), 4 * px + 2 * py + pc


def _comm_call(name, arrays, sems, new_sems, body, after=(), token=False):
    na, ns, nn, nf = len(arrays), len(sems), len(new_sems), len(after)

    def kern(*refs):
        ins, outs = refs[:na + ns + nf], refs[na + ns + nf:]
        body(ins[:na], ins[na:na + ns], outs[:nn])
        if token:
            outs[nn + na][...] = jnp.zeros((8, 128), f32)

    out_shape = ([pltpu.SemaphoreType.DMA(s) for s in new_sems] + [pltpu.HBM(a.shape, a.dtype) for a in arrays]
                 + ([jax.ShapeDtypeStruct((8, 128), f32)] if token else []))
    out_specs = [SEM_SPEC] * nn + [HBM_SPEC] * na + ([pl.BlockSpec(memory_space=pltpu.VMEM)] if token else [])
    res = pl.pallas_call(
        kern, name=name, out_shape=tuple(out_shape),
        in_specs=[HBM_SPEC] * na + [SEM_SPEC] * ns + [ANY_SPEC] * nf, out_specs=tuple(out_specs),
        input_output_aliases={t: nn + t for t in range(na)},
        compiler_params=pltpu.CompilerParams(has_side_effects=pltpu.SideEffectType.DATAFLOW_SIDE_EFFECTING),
    )(*[pltpu.with_memory_space_constraint(a, pltpu.HBM) for a in arrays], *sems, *after)
    return list(res[:nn]), list(res[nn:nn + na]), (res[nn + na] if token else None)


def _remote(src, dst, send_sem, recv_sem, device):
    return pltpu.make_async_remote_copy(src_ref=src, dst_ref=dst, send_sem=send_sem, recv_sem=recv_sem,
                                        device_id=device, device_id_type=MESH)


def _gather_start(name, src):
    def body(a, s, new):
        (src, land), (send, recv) = a, new
        x, y, c, me = _my_place()
        pltpu.make_async_copy(src, land.at[me], recv.at[0]).start()
        for k in range(1, NDEV):
            peer, _ = _peer(x, y, c, k)
            _remote(src, land.at[me], send.at[k], recv.at[k], peer).start()

    land = lax.empty((NDEV,) + src.shape, src.dtype)
    (send, recv), (src, land), token = _comm_call(name, [src, land], [], [(NDEV,), (NDEV,)], body, token=True)
    return send, recv, src, land, token


def _gather_wait(name, src, land, send, recv, after):
    def body(a, s, new):
        (src, land), (send, recv) = a, s
        x, y, c, me = _my_place()
        pltpu.make_async_copy(src, land.at[me], recv.at[0]).wait()
        for k in range(1, NDEV):
            peer, slot = _peer(x, y, c, k)
            _remote(src, land.at[slot], send.at[k], recv.at[k], peer).wait_recv()
        for k in range(1, NDEV):
            peer, _ = _peer(x, y, c, k)
            _remote(src, land.at[me], send.at[k], recv.at[k], peer).wait_send()

    return _comm_call(name, [src, land], [send, recv], [], body, after=after)[1][1]


SAME_CORE = (2, 4, 6)
VIA_SIBLING = (3, 5, 7)


def _weights_forward(name, land, recv, after):
    def body(a, s, new):
        (land,), (recv,), (fsend, frecv) = a, s, new
        x, y, c, me = _my_place()
        sibling, _ = _peer(x, y, c, 1)
        for k in SAME_CORE:
            peer, slot = _peer(x, y, c, k)
            _remote(land.at[slot], land.at[slot], fsend.at[k], recv.at[k], peer).wait_recv()
            _remote(land.at[slot], land.at[slot], fsend.at[k], frecv.at[k ^ 1], sibling).start()

    return _comm_call(name, [land], [recv], [(NDEV,), (NDEV,)], body, after=after)


def _weights_wait(name, land, send, recv, fsend, frecv, after):
    def body(a, s, new):
        (land,), (send, recv, fsend, frecv) = a, s
        x, y, c, me = _my_place()
        sibling, sib_slot = _peer(x, y, c, 1)
        _remote(land.at[sib_slot], land.at[sib_slot], send.at[1], recv.at[1], sibling).wait_recv()
        for k in VIA_SIBLING:
            _, slot = _peer(x, y, c, k)
            _remote(land.at[slot], land.at[slot], fsend.at[k ^ 1], frecv.at[k], sibling).wait_recv()
        for k in (1,) + SAME_CORE:
            peer, _ = _peer(x, y, c, k)
            _remote(land.at[me], land.at[me], send.at[k], recv.at[k], peer).wait_send()
        for k in SAME_CORE:
            _, slot = _peer(x, y, c, k)
            _remote(land.at[slot], land.at[slot], fsend.at[k], frecv.at[k ^ 1], sibling).wait_send()

    return _comm_call(name, [land], [send, recv, fsend, frecv], [], body, after=after)[1][0]


def _diag_relay(x, y, c):
    slot = 4 * (x ^ (1 - c)) + 2 * (y ^ c) + c
    return slot, (x ^ c, y ^ (1 - c), c)


def _w_in_start(land, after):
    def body(a, s, new):
        (land,), (send, recv) = a, new
        x, y, c, me = _my_place()
        for k in (1, 2, 4):
            peer, _ = _peer(x, y, c, k)
            _remote(land.at[me], land.at[me], send.at[k], recv.at[k], peer).start()

    (send, recv), (land,), token = _comm_call("w_in_start", [land], [], [(NDEV,), (NDEV,)], body, after=after, token=True)
    return send, recv, land, token


def _w_in_sibling(land, recv, after):
    def body(a, s, new):
        (land,), (recv,) = a, s
        x, y, c, me = _my_place()
        sibling, slot = _peer(x, y, c, 1)
        _remote(land.at[slot], land.at[slot], recv.at[1], recv.at[1], sibling).wait_recv()

    return _comm_call("w_in_sibling", [land], [recv], [], body, after=after)[1][0]


def _w_in_relay(land, recv, after):
    def body(a, s, new):
        (land,), (recv,), (fsend, frecv) = a, s, new
        x, y, c, me = _my_place()
        sibling, _ = _peer(x, y, c, 1)
        for k in (2, 4):
            peer, slot = _peer(x, y, c, k)
            _remote(land.at[slot], land.at[slot], fsend.at[k], recv.at[k], peer).wait_recv()
        slot, target = _diag_relay(x, y, c)
        _remote(land.at[slot], land.at[slot], fsend.at[6], frecv.at[6], target).start()
        for k in (2, 4):
            _, slot = _peer(x, y, c, k)
            _remote(land.at[slot], land.at[slot], fsend.at[k], frecv.at[k ^ 1], sibling).start()

    (fsend, frecv), (land,), _ = _comm_call("w_in_relay", [land], [recv], [(NDEV,), (NDEV,)], body, after=after)
    return fsend, frecv, land


def _w_in_forwarded(land, frecv, after):
    def body(a, s, new):
        (land,), (frecv,) = a, s
        x, y, c, me = _my_place()
        sibling, _ = _peer(x, y, c, 1)
        for k in (3, 5):
            _, slot = _peer(x, y, c, k)
            _remote(land.at[slot], land.at[slot], frecv.at[k], frecv.at[k], sibling).wait_recv()

    return _comm_call("w_in_forwarded", [land], [frecv], [], body, after=after)[1][0]


def _w_in_diag(land, land_o, frecv, after):
    def body(a, s, new):
        (land, land_o), (frecv,), (dsend, drecv, osend, orecv) = a, s, new
        x, y, c, me = _my_place()
        sibling, _ = _peer(x, y, c, 1)
        peer, slot = _peer(x, y, c, 6)
        _remote(land.at[slot], land.at[slot], dsend.at[6], frecv.at[6], peer).wait_recv()
        _remote(land.at[slot], land.at[slot], dsend.at[6], drecv.at[7], sibling).start()
        for k in (1,) + SAME_CORE:
            peer, _ = _peer(x, y, c, k)
            _remote(land_o.at[me], land_o.at[me], osend.at[k], orecv.at[k], peer).start()

    sems, (land, land_o), _ = _comm_call("w_in_diag", [land, land_o], [frecv], [(NDEV,)] * 4, body, after=after)
    return sems, land, land_o


def _w_in_finish(land, send, fsend, dsend, drecv, after):
    def body(a, s, new):
        (land,), (send, fsend, dsend, drecv) = a, s
        x, y, c, me = _my_place()
        sibling, _ = _peer(x, y, c, 1)
        _, slot = _peer(x, y, c, 7)
        _remote(land.at[slot], land.at[slot], dsend.at[6], drecv.at[7], sibling).wait_recv()
        for k in (1, 2, 4):
            peer, _ = _peer(x, y, c, k)
            _remote(land.at[me], land.at[me], send.at[k], send.at[k], peer).wait_send()
        for k in (2, 4, 6):
            _, slot = _peer(x, y, c, k)
            _remote(land.at[slot], land.at[slot], fsend.at[k], fsend.at[k], sibling).wait_send()
        _, slot = _peer(x, y, c, 6)
        _remote(land.at[slot], land.at[slot], dsend.at[6], dsend.at[6], sibling).wait_send()

    return _comm_call("w_in_finish", [land], [send, fsend, dsend, drecv], [], body, after=after)[1][0]


def _in_proj_part(name, h, land, proj, me_arr, k0, kstep, nk, tm=512):
    S, D = h.shape
    C = land.shape[2]
    tm = min(tm, S)

    def body(me_ref, a_ref, b_ref, *rest):
        rest[-1][...] = jnp.dot(a_ref[...], b_ref[...], preferred_element_type=f32)

    slot = lambda j, me: me[0] ^ (k0 + kstep * j)
    args = [h, land] + ([] if proj is None else [proj])
    grid_spec = pltpu.PrefetchScalarGridSpec(
        num_scalar_prefetch=1, grid=(nk, S // tm),
        in_specs=[pl.BlockSpec((tm, D), lambda j, i, me: (i, 0)),
                  pl.BlockSpec((None, D, C), lambda j, i, me: (slot(j, me), 0, 0))] + [ANY_SPEC] * (len(args) - 2),
        out_specs=pl.BlockSpec((tm, C), lambda j, i, me: (i, slot(j, me))))
    return pl.pallas_call(
        body, name=name, out_shape=jax.ShapeDtypeStruct((S, NDEV * C), f32), grid_spec=grid_spec,
        input_output_aliases={} if proj is None else {3: 0}, compiler_params=_params("arbitrary", "arbitrary"),
    )(me_arr, *args)


NCHIP = NDEV // 2


def _pairs_start(name, srcs):
    n = len(srcs)
    npairs = [src.shape[0] // 2 for src in srcs]

    def body(a, s, new):
        x, y, c, me = _my_place()
        sibling, _ = _peer(x, y, c, 1)
        for t in range(n):
            src, pair, send, recv = a[t], a[n + t], new[t], new[n + t]
            for i in range(npairs[t]):
                _remote(src.at[2 * i + 1 - c], pair.at[i], send.at[i], recv.at[i], sibling).start()

    pairs = [lax.empty((npairs[t],) + srcs[t].shape[1:], srcs[t].dtype) for t in range(n)]
    sems, arrays, token = _comm_call(name, list(srcs) + pairs, [], [(m,) for m in npairs] * 2, body, token=True)
    return [(sems[t], sems[n + t], arrays[t], arrays[n + t]) for t in range(n)], token


def _pairs_wait(name, groups, after):
    n = len(groups)

    def body(a, s, new):
        x, y, c, me = _my_place()
        sibling, _ = _peer(x, y, c, 1)
        for t in range(n):
            src, pair, send, recv = a[t], a[n + t], s[t], s[n + t]
            for i in range(pair.shape[0]):
                cp = _remote(src.at[2 * i + 1 - c], pair.at[i], send.at[i], recv.at[i], sibling)
                cp.wait_recv()
                cp.wait_send()

    arrays = _comm_call(name, [g[2] for g in groups] + [g[3] for g in groups],
                        [g[0] for g in groups] + [g[1] for g in groups], [], body, after=after)[1]
    return [(arrays[t], arrays[n + t]) for t in range(n)]


def _pair_start(name, src):
    ((send, recv, src, pair),), token = _pairs_start(name, [src])
    return send, recv, src, pair, token


def _pair_wait(name, src, pair, send, recv, after):
    return _pairs_wait(name, [(send, recv, src, pair)], after)[0]


def _pair_sum(name, src, pair, core, tr=1024):
    npair, R, Cc = pair.shape
    tr = min(tr, R)

    def body(core_ref, a_ref, b_ref, o_ref):
        o_ref[...] = (a_ref[...].astype(f32) + b_ref[...].astype(f32)).astype(o_ref.dtype)

    grid_spec = pltpu.PrefetchScalarGridSpec(
        num_scalar_prefetch=1, grid=(npair, R // tr),
        in_specs=[pl.BlockSpec((None, tr, Cc), lambda i, r, core: (2 * i + core[0], r, 0)),
                  pl.BlockSpec((None, tr, Cc), lambda i, r, core: (i, r, 0))],
        out_specs=pl.BlockSpec((None, tr, Cc), lambda i, r, core: (i, r, 0)))
    return pl.pallas_call(body, name=name, out_shape=jax.ShapeDtypeStruct(pair.shape, pair.dtype),
                          grid_spec=grid_spec, compiler_params=_params("parallel", "parallel"))(core, src, pair)


def _owner_chip(first, i):
    q = first // 2 + i
    return q >> 1 & 1, q & 1


def _chips_start(name, groups, rows=None, after=()):
    n = len(groups)
    row_of = [pl.ds(*(rows or (0, g[0].shape[1]))) for g in groups]

    def body(a, s, new):
        x, y, c, me = _my_place()
        for t, (_, _, first) in enumerate(groups):
            sums, land, send, recv = a[t], a[n + t], new[t], new[n + t]
            for i in range(sums.shape[0]):
                ox, oy = _owner_chip(first, i)

                @pl.when((x != ox) | (y != oy))
                def _():
                    _remote(sums.at[i, row_of[t]], land.at[2 * x + y, row_of[t]], send.at[i], recv.at[2 * x + y],
                            (ox, oy, c)).start()

    sems, arrays, token = _comm_call(name, [g[0] for g in groups] + [g[1] for g in groups], [],
                                     [(g[0].shape[0],) for g in groups] + [(NCHIP,)] * n, body, after=after, token=True)
    return [(sems[t], sems[n + t], arrays[t], arrays[n + t]) for t in range(n)], token


def _chips_wait(name, groups, firsts, after, rows=None):
    n = len(groups)
    row_of = [pl.ds(*(rows or (0, g[2].shape[1]))) for g in groups]

    def body(a, s, new):
        x, y, c, me = _my_place()
        for t in range(n):
            sums, land, send, recv, first = a[t], a[n + t], s[t], s[n + t], firsts[t]
            npair = sums.shape[0]
            mine = (me >= first) & (me < first + 2 * npair)
            for i in range(npair):
                ox, oy = _owner_chip(first, i)

                @pl.when((x != ox) | (y != oy))
                def _():
                    _remote(sums.at[i, row_of[t]], land.at[2 * x + y, row_of[t]], send.at[i], recv.at[2 * x + y],
                            (ox, oy, c)).wait_send()
            for q in range(NCHIP):
                @pl.when(mine & (2 * x + y != q))
                def _():
                    _remote(sums.at[0, row_of[t]], land.at[q, row_of[t]], send.at[0], recv.at[q],
                            (q >> 1, q & 1, c)).wait_recv()

    arrays = _comm_call(name, [g[2] for g in groups] + [g[3] for g in groups],
                        [g[0] for g in groups] + [g[1] for g in groups], [], body, after=after)[1]
    return [(arrays[t], arrays[n + t]) for t in range(n)]


def _chip_start(name, sums, land, first, rows=None, after=()):
    ((send, recv, sums, land),), token = _chips_start(name, [(sums, land, first)], rows, after)
    return send, recv, sums, land, token


def _chip_wait(name, sums, land, send, recv, first, after, rows=None):
    return _chips_wait(name, [(send, recv, sums, land)], [first], after, rows)[0]


def _matmul(a, b, *, name, out_dtype, ta=False, tb=False, b_slots=False, out_slots=0, b_cols=None,
            tm=1024, tn=1024, tk=2048, dep=None):
    M, K = (a.shape[1], a.shape[0]) if ta else a.shape
    col0 = 0
    if b_slots:
        slab = b.shape[2]
        N = b.shape[1] if tb else b.shape[0] * slab
        assert (K if tb else N) == b.shape[0] * slab
    elif b_cols is not None:
        assert not tb
        col0, N = b_cols
    else:
        N = b.shape[0] if tb else b.shape[1]
    tm, tn, tk = min(tm, M), min(tn, N), min(tk, K)
    if b_slots:
        if tb:
            tk = min(tk, slab)
        else:
            tn = min(tn, slab)
    if out_slots:
        tn = min(tn, N // out_slots)
    nm, nn, nk = M // tm, N // tn, K // tk
    assert (nm * tm, nn * tn, nk * tk) == (M, N, K) and col0 % tn == 0, (name, M, N, K, tm, tn, tk)
    j0 = col0 // tn

    a_spec = pl.BlockSpec((tk, tm), lambda i, j, k: (k, i)) if ta else pl.BlockSpec((tm, tk), lambda i, j, k: (i, k))
    if b_slots and tb:
        per = slab // tk
        b_spec = pl.BlockSpec((None, tn, tk), lambda i, j, k: (k // per, j, k % per))
    elif b_slots:
        per = slab // tn
        b_spec = pl.BlockSpec((None, tk, tn), lambda i, j, k: (j // per, k, j % per))
    elif tb:
        b_spec = pl.BlockSpec((tn, tk), lambda i, j, k: (j, k))
    else:
        b_spec = pl.BlockSpec((tk, tn), lambda i, j, k: (k, j + j0))
    if out_slots:
        per_o = (N // out_slots) // tn
        o_spec = pl.BlockSpec((None, tm, tn), lambda i, j, k: (j // per_o, i, j % per_o))
        out_shape = jax.ShapeDtypeStruct((out_slots, M, N // out_slots), out_dtype)
    else:
        o_spec = pl.BlockSpec((tm, tn), lambda i, j, k: (i, j))
        out_shape = jax.ShapeDtypeStruct((M, N), out_dtype)
    dims = (((0 if ta else 1,), (1 if tb else 0,)), ((), ()))
    deps = [] if dep is None else [dep]

    def body(a_ref, b_ref, *rest):
        o_ref = rest[len(deps)]
        prod = lax.dot_general(a_ref[...], b_ref[...], dims, preferred_element_type=f32)
        if nk == 1:
            o_ref[...] = prod.astype(out_dtype)
            return
        acc_ref = rest[len(deps) + 1]
        k = pl.program_id(2)

        @pl.when(k == 0)
        def _():
            acc_ref[...] = prod

        @pl.when((k > 0) & (k < nk - 1))
        def _():
            acc_ref[...] += prod

        @pl.when(k == nk - 1)
        def _():
            o_ref[...] = (acc_ref[...] + prod).astype(out_dtype)

    return pl.pallas_call(
        body, name=name, out_shape=out_shape, grid=(nm, nn, nk),
        in_specs=[a_spec, b_spec] + [ANY_SPEC] * len(deps), out_specs=o_spec,
        scratch_shapes=[pltpu.VMEM((tm, tn), f32)] if nk > 1 else [],
        compiler_params=_params("parallel", "parallel", "arbitrary"),
    )(a, b, *deps)


def _matmul_slabs_t(a_cols, a_slots, b, *, name, tm=512, tn=512, dep=None):
    M = a_cols.shape[0]
    n_slab, N, slab = b.shape
    n1, n2 = a_cols.shape[1] // slab, a_slots.shape[0]
    assert n1 + n2 == n_slab and a_slots.shape[1:] == (M, slab)
    tm, tn = min(tm, M), min(tn, N)
    deps = [] if dep is None else [dep]

    def body(a1_ref, a2_ref, b_ref, *rest):
        o_ref = rest[len(deps)]
        acc = None
        for s in range(n_slab):
            lhs = a1_ref[:, s * slab:(s + 1) * slab] if s < n1 else a2_ref[s - n1]
            prod = lax.dot_general(lhs, b_ref[s], (((1,), (1,)), ((), ())), preferred_element_type=f32)
            acc = prod if acc is None else acc + prod
        o_ref[...] = acc

    return pl.pallas_call(
        body, name=name, out_shape=jax.ShapeDtypeStruct((M, N), f32), grid=(M // tm, N // tn),
        in_specs=[pl.BlockSpec((tm, n1 * slab), lambda i, j: (i, 0)), pl.BlockSpec((n2, tm, slab), lambda i, j: (0, i, 0)),
                  pl.BlockSpec((n_slab, tn, slab), lambda i, j: (0, j, 0))] + [ANY_SPEC] * len(deps),
        out_specs=pl.BlockSpec((tm, tn), lambda i, j: (i, j)), compiler_params=_params("parallel", "parallel"),
    )(a_cols, a_slots, b, *deps)


def _ada_exchange(c_blk, cw_slab, w_ada, b_cols):
    nblk = c_blk.shape[0]
    D, W = w_ada.shape
    CW = cw_slab.shape[1]

    def body(c_ref, cw_ref, w_ref, b_ref, mod_ref, call_ref, cwg_ref, msend, send_sems, recv_sems):
        x, y, c, me = _my_place()
        call_ref[me] = _silu(c_ref[...])
        cwg_ref[me] = cw_ref[...]
        first = []
        for k in range(1, NDEV):
            peer, _ = _peer(x, y, c, k)
            first.append(_remote(call_ref.at[me], call_ref.at[me], send_sems.at[0, k], recv_sems.at[0, k], peer))
            first.append(_remote(cwg_ref.at[me], cwg_ref.at[me], send_sems.at[1, k], recv_sems.at[1, k], peer))
        for cp in first:
            cp.start()
        for k in range(1, NDEV):
            peer, slot = _peer(x, y, c, k)
            _remote(call_ref.at[slot], call_ref.at[slot], send_sems.at[0, k], recv_sems.at[0, k], peer).wait_recv()
            _remote(cwg_ref.at[slot], cwg_ref.at[slot], send_sems.at[1, k], recv_sems.at[1, k], peer).wait_recv()
        mod = jnp.broadcast_to(b_ref[...], (NDEV, W))
        for r in range(nblk):
            mod = mod + lax.dot_general(call_ref[:, r, :], w_ref[r * 128:(r + 1) * 128, :], (((1,), (0,)), ((), ())),
                                        preferred_element_type=f32, precision=lax.Precision.HIGHEST)
        row = lax.broadcasted_iota(jnp.int32, (NDEV, 1), 0)
        pick = lambda j: jnp.broadcast_to(jnp.sum(jnp.where(row == j, mod, 0.0), axis=0, keepdims=True), (8, W))
        mod_ref[me] = pick(me)
        second = []
        for k in range(1, NDEV):
            peer, slot = _peer(x, y, c, k)
            msend[k] = pick(slot)
            second.append(_remote(msend.at[k], mod_ref.at[me], send_sems.at[2, k], recv_sems.at[2, k], peer))
        for cp in second:
            cp.start()
        for k in range(1, NDEV):
            peer, slot = _peer(x, y, c, k)
            _remote(msend.at[k], mod_ref.at[slot], send_sems.at[2, k], recv_sems.at[2, k], peer).wait_recv()
        for cp in first + second:
            cp.wait_send()

    vmem = pl.BlockSpec(memory_space=pltpu.VMEM)
    return pl.pallas_call(
        body, name="ada_exchange",
        out_shape=(jax.ShapeDtypeStruct((NDEV, 8, W), f32), jax.ShapeDtypeStruct((NDEV, nblk, 128), f32),
                   jax.ShapeDtypeStruct((NDEV, 8, CW), f32)),
        in_specs=[vmem] * 4, out_specs=(vmem, vmem, vmem),
        scratch_shapes=[pltpu.VMEM((NDEV, 8, W), f32), pltpu.SemaphoreType.DMA((3, NDEV)),
                        pltpu.SemaphoreType.DMA((3, NDEV))],
        compiler_params=_params(),
    )(c_blk, cw_slab, w_ada, b_cols)


def _prenorm(x, scale, shift, g_pre, dep, tr=512):
    S, D = x.shape
    tr = min(tr, S)

    def body(x_ref, sc_ref, sh_ref, g_ref, dep_ref, h_ref):
        xv = x_ref[...]
        r = lax.rsqrt(jnp.mean(xv * xv, axis=-1, keepdims=True) + EPS)
        h_ref[...] = ((xv * r) * g_ref[...] * (1.0 + sc_ref[...]) + sh_ref[...]).astype(bf16)

    row = pl.BlockSpec((tr, D), lambda i: (i, 0))
    vec = pl.BlockSpec((1, D), lambda i: (0, 0))
    return pl.pallas_call(body, name="prenorm", out_shape=jax.ShapeDtypeStruct((S, D), bf16), grid=(S // tr,),
                          in_specs=[row, vec, vec, vec, ANY_SPEC], out_specs=row, compiler_params=_params("parallel"))(
                              x, scale, shift, g_pre, dep)


def _ext_rows(i, tr, S):
    g = lax.broadcasted_iota(jnp.int32, (tr + 16, 1), 0) + (i * tr - 8)
    return (g >= 0) & (g < S)


def _halo_specs(tr, S, C, col):
    nb8 = S // 8
    main = pl.BlockSpec((tr, C), lambda i: (i, col))
    prev = pl.BlockSpec((8, C), lambda i: (jnp.maximum(i * (tr // 8) - 1, 0), col))
    nxt = pl.BlockSpec((8, C), lambda i: (jnp.minimum((i + 1) * (tr // 8), nb8 - 1), col))
    return prev, main, nxt


def _conv_fwd(proj, conv_w, conv_b, g_conv, tr=512):
    S, C = proj.shape[0], proj.shape[1] // 8
    tr = min(tr, S)

    def body(up, um, un, cp, cm, cn, bg_ref, zc_ref, w_ref, cb_ref, g_ref, o_ref):
        i = pl.program_id(0)
        exists = _ext_rows(i, tr, S)
        u = jnp.concatenate([up[...], um[...], un[...]], axis=0)
        cg = jnp.concatenate([cp[...], cm[...], cn[...]], axis=0)
        t = jnp.where(exists, cg * u, 0.0)
        t_before = pltpu.roll(t, 1, 0)[8:tr + 8]
        t_after = pltpu.roll(t, tr + 15, 0)[8:tr + 8]
        w = w_ref[...]
        cv = w[0:1] * t_before + w[1:2] * t[8:tr + 8] + w[2:3] * t_after + cb_ref[...]
        yc = bg_ref[...] * cv
        rc = lax.rsqrt(jnp.mean(yc * yc, axis=-1, keepdims=True) + EPS)
        o_ref[...] = ((yc * rc) * g_ref[...] * _silu(zc_ref[...])).astype(bf16)

    u_specs = _halo_specs(tr, S, C, 0)
    c_specs = _halo_specs(tr, S, C, 2)
    vec = pl.BlockSpec((1, C), lambda i: (0, 0))
    return pl.pallas_call(
        body, name="conv_fwd", out_shape=jax.ShapeDtypeStruct((S, 2 * C), bf16), grid=(S // tr,),
        in_specs=[*u_specs, *c_specs, pl.BlockSpec((tr, C), lambda i: (i, 1)), pl.BlockSpec((tr, C), lambda i: (i, 3)),
                  pl.BlockSpec((8, C), lambda i: (0, 0)), vec, vec],
        out_specs=pl.BlockSpec((tr, C), lambda i: (i, 0)), compiler_params=_params("parallel"),
    )(proj, proj, proj, proj, proj, proj, proj, proj, conv_w, conv_b, g_conv)


def _branch_geometry(S, r, inter):
    L = S // r * inter
    nq = min(128, L)
    nk = min(nq + 2 * HALF_WIN * inter, L)
    assert L % nq == 0 and (L == nk or L >= nq + 2 * HALF_WIN * inter)
    return L, nq, nk, L // nq


QUAD = 4


def _to_quad(dst, src, S):
    n = S // QUAD
    for rho in range(QUAD):
        dst[pl.ds(rho * n, n), :] = src[pl.ds(rho, n, stride=QUAD), :]


def _block_rows(idx, r, inter, S, L, nq, nk, nblk):
    rho, qb = (0, idx) if r == 1 else (idx // nblk, idx % nblk)
    i0 = qb * nq
    ws = jnp.clip(i0 - HALF_WIN * inter, 0, L - nk)
    if r == 1:
        return pl.ds(pl.multiple_of(i0, 8), nq), pl.ds(pl.multiple_of(ws, 8), nk), i0 - ws
    assert r % (QUAD * inter) == 0
    step = r // QUAD // inter
    base = (rho % QUAD) * (S // QUAD) + rho // QUAD
    if step == 1:
        return pl.ds(pl.multiple_of(base + i0, 8), nq), pl.ds(pl.multiple_of(base + ws, 8), nk), i0 - ws
    return pl.ds(base + step * i0, nq, stride=step), pl.ds(base + step * ws, nk, stride=step), i0 - ws


N_CASES = 3
SCALE = HEAD_DIM ** -0.5
ATTN_UNROLL = 16


def _bias_shape(S):
    shapes = [_branch_geometry(S, r, inter)[1:3] for _, r, inter in BRANCHES]
    return (len(BRANCHES) * N_CASES * 2, max(nq for nq, _ in shapes), max(nk for _, nk in shapes))


def _bias_index(b, case, head):
    return (b * N_CASES + case) * 2 + head


def _fill_bias(bias_scr, sl_ref, S):
    sl = sl_ref[...]
    slope = (sl[0:1, 0:1], sl[0:1, HEAD_DIM:HEAD_DIM + 1])
    for b, (_, r, inter) in enumerate(BRANCHES):
        L, nq, nk, nblk = _branch_geometry(S, r, inter)
        rel = lax.broadcasted_iota(jnp.int32, (nq, nk), 0) - lax.broadcasted_iota(jnp.int32, (nq, nk), 1)
        for case in range(N_CASES):
            d = jnp.abs(rel + case * HALF_WIN)
            valid = d <= HALF_WIN * inter
            if inter > 1:
                valid = valid & (jnp.bitwise_and(d, inter - 1) == 0)
            dist = d.astype(f32) * float(r // inter)
            for head in range(2):
                bias_scr[_bias_index(b, case, head), 0:nq, 0:nk] = jnp.where(valid, -slope[head] * dist, NEG_INF)


def _bias_tiles(slopes, S, dep):
    npair = slopes.shape[0]
    shape = _bias_shape(S)

    def body(sl_ref, dep_ref, o_ref):
        _fill_bias(o_ref, sl_ref, S)

    return pl.pallas_call(
        body, name="bias_tiles", out_shape=jax.ShapeDtypeStruct((npair,) + shape, f32), grid=(npair,),
        in_specs=[pl.BlockSpec((None, 8, PAIR), lambda p: (p, 0, 0)), ANY_SPEC],
        out_specs=pl.BlockSpec((None,) + shape, lambda p: (p, 0, 0, 0)), compiler_params=_params("parallel"),
    )(slopes, dep)


def _head_slopes(n_heads):
    slopes = 2.0 ** (-8.0 * jnp.arange(1, n_heads + 1, dtype=f32) / n_heads)
    return jnp.broadcast_to(jnp.repeat(slopes.reshape(n_heads // 2, 2), HEAD_DIM, axis=1)[:, None, :],
                            (n_heads // 2, 8, PAIR))


def _attn_fwd(proj, bias):
    S, C = proj.shape[0], proj.shape[1] // 8
    npair = C // PAIR

    def body(q_ref, k_ref, v_ref, bias_scr, o_ref, lse_ref, m_scr, l_scr, a_scr, q4_scr, k4_scr, v4_scr):
        lane = lax.broadcasted_iota(jnp.int32, (1, PAIR), 1)
        first = lane < HEAD_DIM
        for dst, src in ((q4_scr, q_ref), (k4_scr, k_ref), (v4_scr, v_ref)):
            _to_quad(dst, src, S)

        for b, (_, r, inter) in enumerate(BRANCHES):
            L, nq, nk, nblk = _branch_geometry(S, r, inter)
            qs, ks, vs = (q_ref, k_ref, v_ref) if r == 1 else (q4_scr, k4_scr, v4_scr)

            def step(idx, carry, b=b, r=r, L=L, nq=nq, nk=nk, nblk=nblk, qs=qs, ks=ks, vs=vs):
                qrows, krows, off = _block_rows(idx, r, inter, S, L, nq, nk, nblk)
                case = off // HALF_WIN
                q2 = qs[qrows, :] * SCALE
                k2 = ks[krows, :].astype(bf16)
                v2 = vs[krows, :].astype(bf16)
                ms, accs = [], []
                for hh in range(2):
                    mine = first if hh == 0 else ~first
                    qh = jnp.where(mine, q2, 0.0).astype(bf16)
                    s = lax.dot_general(qh, k2, (((1,), (1,)), ((), ())), preferred_element_type=f32)
                    s = s + bias_scr[_bias_index(b, case, hh), 0:nq, 0:nk]
                    m = jnp.max(s, axis=-1, keepdims=True)
                    p = jnp.exp(s - m).astype(bf16)
                    vh = jnp.where(mine, v2, jnp.ones_like(v2))
                    ms.append(m)
                    accs.append(jnp.dot(p, vh, preferred_element_type=f32))
                m_scr[b, qrows, :] = jnp.where(first, ms[0], ms[1])
                a_scr[b, qrows, :] = jnp.where(first, accs[0], accs[1])
                l_scr[b, qrows, :] = jnp.where(first, accs[1], accs[0])
                return carry

            lax.fori_loop(0, S // nq, step, 0, unroll=min(ATTN_UNROLL, S // nq))

        n4 = S // QUAD
        ch = min(256, n4)
        nch = n4 // ch

        def merge(i, carry):
            rho, part = i // nch, i % nch
            sorted_rows = pl.ds(pl.multiple_of(rho * n4 + part * ch, 8), ch)
            token_rows = pl.ds(rho + QUAD * part * ch, ch, stride=QUAD)
            rows = (token_rows,) + (sorted_rows,) * (len(BRANCHES) - 1)
            ms = [m_scr[b, rows[b], :] for b in range(len(BRANCHES))]
            m = functools.reduce(jnp.maximum, ms)
            l = jnp.zeros((ch, PAIR), f32)
            acc = jnp.zeros((ch, PAIR), f32)
            for b in range(len(BRANCHES)):
                w = jnp.exp(ms[b] - m)
                l = l + w * pltpu.roll(l_scr[b, rows[b], :], HEAD_DIM, 1)
                acc = acc + w * a_scr[b, rows[b], :]
            o_ref[token_rows, :] = acc / l
            lse_ref[token_rows, :] = m + jnp.log(l)
            return carry

        lax.fori_loop(0, QUAD * nch, merge, 0, unroll=2)

    blk = lambda part: pl.BlockSpec((S, PAIR), lambda p: (0, part * npair + p))
    out = pl.BlockSpec((S, PAIR), lambda p: (0, p))
    return pl.pallas_call(
        body, name="attn_fwd",
        out_shape=(jax.ShapeDtypeStruct((S, C), f32), jax.ShapeDtypeStruct((S, C), f32)), grid=(npair,),
        in_specs=[blk(4), blk(5), blk(6), pl.BlockSpec((None,) + _bias_shape(S), lambda p: (p, 0, 0, 0))],
        out_specs=(out, out),
        scratch_shapes=[pltpu.VMEM((3, S, PAIR), f32)] * 3 + [pltpu.VMEM((S, PAIR), f32)] * 3,
        compiler_params=_params("parallel"),
    )(proj, proj, proj, bias)


def _attn_post(ycat, o, proj, g_attn, tr=512):
    S, C = o.shape
    tr = min(tr, S)

    def body(y_ref, o_ref, z_ref, g_ref, out_ref):
        del y_ref
        ov = o_ref[...]
        ra = lax.rsqrt(jnp.mean(ov * ov, axis=-1, keepdims=True) + EPS)
        out_ref[...] = ((ov * ra) * g_ref[...] * _silu(z_ref[...])).astype(bf16)

    return pl.pallas_call(
        body, name="attn_post", out_shape=jax.ShapeDtypeStruct(ycat.shape, ycat.dtype), grid=(S // tr,),
        in_specs=[HBM_SPEC, pl.BlockSpec((tr, C), lambda i: (i, 0)), pl.BlockSpec((tr, C), lambda i: (i, 7)),
                  pl.BlockSpec((1, C), lambda i: (0, 0))],
        out_specs=pl.BlockSpec((tr, C), lambda i: (i, 1)), input_output_aliases={0: 0},
        compiler_params=_params("arbitrary"),
    )(ycat, o, proj, g_attn)


def _residual_minus_target(x, target, dep, tr=512):
    S, D = x.shape
    tr = min(tr, S)

    def body(x_ref, t_ref, dep_ref, o_ref):
        o_ref[...] = x_ref[...] - t_ref[...]

    row = pl.BlockSpec((tr, D), lambda i: (i, 0))
    return pl.pallas_call(body, name="residual_minus_target", out_shape=jax.ShapeDtypeStruct((S, D), f32),
                          grid=(S // tr,), in_specs=[row, row, ANY_SPEC], out_specs=row,
                          compiler_params=_params("parallel"))(x, target, dep)


def _sandwich(y, x_minus_t, gate, g_post, tr=256):
    S, D = y.shape
    tr = min(tr, S)

    def body(y_ref, xt_ref, gate_ref, g_ref, dy_ref, dout_ref, sums_ref):
        i = pl.program_id(0)
        gate, g = gate_ref[...], g_ref[...]
        gg = gate * g
        yv = y_ref[...]
        rp = lax.rsqrt(jnp.mean(yv * yv, axis=-1, keepdims=True) + EPS)
        yhat = yv * rp
        err = xt_ref[...] + gg * yhat
        dout = err * (1.0 / D)
        dout_ref[...] = dout
        q = dout * yhat
        w = dout * gg
        dy_ref[...] = (rp * (w - yhat * jnp.sum(q * gg, axis=-1, keepdims=True) * (1.0 / D))).astype(bf16)
        loss = 0.5 * jnp.sum(jnp.mean(err * err, axis=-1, keepdims=True), axis=0, keepdims=True)
        q_sum = jnp.sum(q, axis=0, keepdims=True)
        row = lax.broadcasted_iota(jnp.int32, (8, D), 0)
        upd = jnp.where(row == 0, q_sum * g, jnp.where(row == 1, q_sum * gate, jnp.where(row == 2, loss, 0.0)))

        @pl.when(i == 0)
        def _():
            sums_ref[...] = upd

        @pl.when(i > 0)
        def _():
            sums_ref[...] += upd

    row = pl.BlockSpec((tr, D), lambda i: (i, 0))
    vec = pl.BlockSpec((1, D), lambda i: (0, 0))
    return pl.pallas_call(
        body, name="sandwich",
        out_shape=(jax.ShapeDtypeStruct((S, D), bf16), jax.ShapeDtypeStruct((S, D), f32), jax.ShapeDtypeStruct((8, D), f32)),
        grid=(S // tr,), in_specs=[row, row, vec, vec],
        out_specs=(row, row, pl.BlockSpec((8, D), lambda i: (0, 0))), compiler_params=_params("arbitrary"),
    )(y, x_minus_t, gate, g_post)


def _conv_bwd(proj, dycat, conv_w, conv_b, g_conv, dep, tr=256):
    S, C = proj.shape[0], proj.shape[1] // 8
    tr = min(tr, S)
    n = tr + 16

    def body(*refs):
        ins, (w_ref, cb_ref, g_ref, _, dp_ref, sums_ref) = refs[:15], refs[15:]
        i = pl.program_id(0)
        exists = _ext_rows(i, tr, S)
        u, bg, cg, zc, dyn = (jnp.concatenate([ins[3 * t][...], ins[3 * t + 1][...], ins[3 * t + 2][...]], axis=0)
                              for t in range(5))
        w = w_ref[...]
        t = jnp.where(exists, cg * u, 0.0)
        t_before, t_after = pltpu.roll(t, 1, 0), pltpu.roll(t, n - 1, 0)
        cv = w[0:1] * t_before + w[1:2] * t + w[2:3] * t_after + cb_ref[...]
        yc = bg * cv
        rc = lax.rsqrt(jnp.mean(yc * yc, axis=-1, keepdims=True) + EPS)
        yhat = yc * rc
        sz, dsz = _silu_and_slope(zc)
        wgt = dyn * g_ref[...] * sz
        dyc = rc * (wgt - yhat * jnp.mean(wgt * yhat, axis=-1, keepdims=True))
        dcv = jnp.where(exists, dyc * bg, 0.0)
        dt = w[0:1] * pltpu.roll(dcv, n - 1, 0) + w[1:2] * dcv + w[2:3] * pltpu.roll(dcv, 1, 0)
        mid = slice(8, tr + 8)
        dp_ref[:, 0:C] = (dt * cg)[mid].astype(bf16)
        dp_ref[:, C:2 * C] = (dyc * cv)[mid].astype(bf16)
        dp_ref[:, 2 * C:3 * C] = (dt * u)[mid].astype(bf16)
        dp_ref[:, 3 * C:4 * C] = (dyn * yhat * g_ref[...] * dsz)[mid].astype(bf16)
        colsum = lambda v: jnp.sum(v[mid], axis=0, keepdims=True)
        parts = [colsum(dyn * yhat * sz), colsum(dcv), colsum(dcv * t_before), colsum(dcv * t), colsum(dcv * t_after)]
        row = lax.broadcasted_iota(jnp.int32, (8, C), 0)
        upd = jnp.zeros((8, C), f32)
        for j, pj in enumerate(parts):
            upd = jnp.where(row == j, pj, upd)

        @pl.when(i == 0)
        def _():
            sums_ref[...] = upd

        @pl.when(i > 0)
        def _():
            sums_ref[...] += upd

    specs = []
    for col in range(4):
        specs += _halo_specs(tr, S, C, col)
    specs += _halo_specs(tr, S, C, 0)
    vec = pl.BlockSpec((1, C), lambda i: (0, 0))
    return pl.pallas_call(
        body, name="conv_bwd",
        out_shape=(jax.ShapeDtypeStruct((S, 4 * C), bf16), jax.ShapeDtypeStruct((8, C), f32)), grid=(S // tr,),
        in_specs=[*specs, pl.BlockSpec((8, C), lambda i: (0, 0)), vec, vec, ANY_SPEC],
        out_specs=(pl.BlockSpec((tr, 4 * C), lambda i: (i, 0)), pl.BlockSpec((8, C), lambda i: (0, 0))),
        compiler_params=_params("arbitrary"),
    )(*([proj] * 12), dycat, dycat, dycat, conv_w, conv_b, g_conv, dep)


def _attn_post_bwd(o, proj, dycat, g_attn, dep, tr=512):
    S, C = o.shape
    tr = min(tr, S)

    def body(o_ref, z_ref, dy_ref, g_ref, dep_ref, do_ref, dz_ref, sums_ref):
        i = pl.program_id(0)
        ov, zv, dyn = o_ref[...], z_ref[...], dy_ref[...]
        ra = lax.rsqrt(jnp.mean(ov * ov, axis=-1, keepdims=True) + EPS)
        ohat = ov * ra
        sz, dsz = _silu_and_slope(zv)
        wgt = dyn * g_ref[...] * sz
        do_ref[...] = ra * (wgt - ohat * jnp.mean(wgt * ohat, axis=-1, keepdims=True))
        dz_ref[...] = (dyn * ohat * g_ref[...] * dsz).astype(bf16)
        row = lax.broadcasted_iota(jnp.int32, (8, C), 0)
        upd = jnp.where(row == 0, jnp.sum(dyn * ohat * sz, axis=0, keepdims=True), 0.0)

        @pl.when(i == 0)
        def _():
            sums_ref[...] = upd

        @pl.when(i > 0)
        def _():
            sums_ref[...] += upd

    return pl.pallas_call(
        body, name="attn_post_bwd",
        out_shape=(jax.ShapeDtypeStruct((S, C), f32), jax.ShapeDtypeStruct((4, S, C), bf16),
                   jax.ShapeDtypeStruct((8, C), f32)),
        grid=(S // tr,),
        in_specs=[pl.BlockSpec((tr, C), lambda i: (i, 0)), pl.BlockSpec((tr, C), lambda i: (i, 7)),
                  pl.BlockSpec((tr, C), lambda i: (i, 1)), pl.BlockSpec((1, C), lambda i: (0, 0)), ANY_SPEC],
        out_specs=(pl.BlockSpec((tr, C), lambda i: (i, 0)), pl.BlockSpec((None, tr, C), lambda i: (3, i, 0)),
                   pl.BlockSpec((8, C), lambda i: (0, 0))),
        compiler_params=_params("arbitrary"),
    )(o, proj, dycat, g_attn, dep)


def _attn_bwd(proj, o, do, lse, bias, dqkvz, dep):
    S, C = o.shape
    npair = C // PAIR

    def body(q_ref, k_ref, v_ref, o_ref, do_ref, lse_ref, bias_scr, old_ref, dep_ref, dqkv_ref,
             acc_scr, dl_scr, quad_scr):
        lane = lax.broadcasted_iota(jnp.int32, (1, PAIR), 1)
        first = lane < HEAD_DIM
        ch = min(256, S)

        def prep(i, carry):
            rows = pl.ds(pl.multiple_of(i * ch, 8), ch)
            prod = do_ref[rows, :] * o_ref[rows, :]
            d0 = jnp.sum(jnp.where(first, prod, 0.0), axis=-1, keepdims=True)
            d1 = jnp.sum(jnp.where(first, 0.0, prod), axis=-1, keepdims=True)
            dl_scr[rows, :] = jnp.where(first, d0, d1)
            zero = jnp.zeros((ch, PAIR), f32)
            for order in range(2):
                for t in range(3):
                    acc_scr[order, t, rows, :] = zero
            return carry

        lax.fori_loop(0, S // ch, prep, 0, unroll=2)
        token_srcs = (q_ref, k_ref, v_ref, do_ref, lse_ref, dl_scr)
        for j, src in enumerate(token_srcs):
            _to_quad(quad_scr.at[j], src, S)

        for b, (_, r, inter) in enumerate(BRANCHES):
            L, nq, nk, nblk = _branch_geometry(S, r, inter)
            order = 0 if r == 1 else 1
            srcs = token_srcs if r == 1 else tuple(quad_scr.at[j] for j in range(6))

            def step(idx, carry, b=b, r=r, L=L, nq=nq, nk=nk, nblk=nblk, order=order, srcs=srcs):
                qs, ks, vs, dos, lses, dls = srcs
                dq_scr, dk_scr, dv_scr = (acc_scr.at[order, t] for t in range(3))
                qrows, krows, off = _block_rows(idx, r, inter, S, L, nq, nk, nblk)
                case = off // HALF_WIN
                q2 = qs[qrows, :] * SCALE
                k2 = ks[krows, :].astype(bf16)
                v2 = vs[krows, :].astype(bf16)
                do2 = dos[qrows, :]
                lse2 = lses[qrows, :]
                dl2 = dls[qrows, :]
                dq2 = jnp.zeros((nq, PAIR), f32)
                dk2 = jnp.zeros((nk, PAIR), f32)
                dv2 = jnp.zeros((nk, PAIR), f32)
                for hh in range(2):
                    mine = first if hh == 0 else ~first
                    lo = hh * HEAD_DIM
                    qh = jnp.where(mine, q2, 0.0).astype(bf16)
                    doh = jnp.where(mine, do2, 0.0).astype(bf16)
                    s = lax.dot_general(qh, k2, (((1,), (1,)), ((), ())), preferred_element_type=f32)
                    s = s + bias_scr[_bias_index(b, case, hh), 0:nq, 0:nk]
                    p = jnp.exp(s - lse2[:, lo:lo + 1])
                    dv2 = dv2 + lax.dot_general(p.astype(bf16), doh, (((0,), (0,)), ((), ())), preferred_element_type=f32)
                    dp = lax.dot_general(doh, v2, (((1,), (1,)), ((), ())), preferred_element_type=f32)
                    ds = (p * (dp - dl2[:, lo:lo + 1])).astype(bf16)
                    dq2 = dq2 + jnp.where(mine, jnp.dot(ds, k2, preferred_element_type=f32), 0.0)
                    dk2 = dk2 + lax.dot_general(ds, qh, (((0,), (0,)), ((), ())), preferred_element_type=f32)
                dq_scr[qrows, :] = dq_scr[qrows, :] + dq2
                dk_scr[krows, :] = dk_scr[krows, :] + dk2
                dv_scr[krows, :] = dv_scr[krows, :] + dv2
                return carry

            lax.fori_loop(0, S // nq, step, 0, unroll=min(ATTN_UNROLL, S // nq))

        n4 = S // QUAD
        for t in range(3):
            for rho in range(QUAD):
                token_rows = pl.ds(rho, n4, stride=QUAD)
                acc_scr[0, t, token_rows, :] = acc_scr[0, t, token_rows, :] + acc_scr[1, t, pl.ds(rho * n4, n4), :]
        dqkv_ref[0] = (acc_scr[0, 0] * SCALE).astype(bf16)
        dqkv_ref[1] = acc_scr[0, 1].astype(bf16)
        dqkv_ref[2] = acc_scr[0, 2].astype(bf16)

    blk = lambda part: pl.BlockSpec((S, PAIR), lambda p: (0, part * npair + p))
    own = pl.BlockSpec((S, PAIR), lambda p: (0, p))
    return pl.pallas_call(
        body, name="attn_bwd", out_shape=jax.ShapeDtypeStruct(dqkvz.shape, dqkvz.dtype), grid=(npair,),
        in_specs=[blk(4), blk(5), blk(6), own, own, own,
                  pl.BlockSpec((None,) + _bias_shape(S), lambda p: (p, 0, 0, 0)), ANY_SPEC, ANY_SPEC],
        out_specs=pl.BlockSpec((3, S, PAIR), lambda p: (0, 0, p)), input_output_aliases={7: 0},
        scratch_shapes=[pltpu.VMEM((2, 3, S, PAIR), f32), pltpu.VMEM((S, PAIR), f32), pltpu.VMEM((6, S, PAIR), f32)],
        compiler_params=_params("arbitrary"),
    )(proj, proj, proj, o, do, lse, bias, dqkvz, dep)


def _prenorm_bwd(dh, x, dout, scale, g_pre, tr=256):
    S, D = x.shape
    tr = min(tr, S)

    def body(dh_ref, x_ref, dout_ref, sc_ref, g_ref, gx_ref, sums_ref):
        i = pl.program_id(0)
        xv, dhv = x_ref[...], dh_ref[...]
        r = lax.rsqrt(jnp.mean(xv * xv, axis=-1, keepdims=True) + EPS)
        xn = xv * r
        dxn = dhv * (g_ref[...] * (1.0 + sc_ref[...]))
        gx_ref[...] = dout_ref[...] + r * (dxn - xn * jnp.mean(dxn * xn, axis=-1, keepdims=True))
        dhx = dhv * xn
        row = lax.broadcasted_iota(jnp.int32, (8, D), 0)
        upd = jnp.where(row == 0, jnp.sum(dhv, axis=0, keepdims=True),
                        jnp.where(row == 1, jnp.sum(dhx, axis=0, keepdims=True) * g_ref[...],
                                  jnp.where(row == 2, jnp.sum(dhx, axis=0, keepdims=True) * (1.0 + sc_ref[...]), 0.0)))

        @pl.when(i == 0)
        def _():
            sums_ref[...] = upd

        @pl.when(i > 0)
        def _():
            sums_ref[...] += upd

    row = pl.BlockSpec((tr, D), lambda i: (i, 0))
    vec = pl.BlockSpec((1, D), lambda i: (0, 0))
    return pl.pallas_call(
        body, name="prenorm_bwd",
        out_shape=(jax.ShapeDtypeStruct((S, D), f32), jax.ShapeDtypeStruct((8, D), f32)), grid=(S // tr,),
        in_specs=[row, row, row, vec, vec], out_specs=(row, pl.BlockSpec((8, D), lambda i: (0, 0))),
        compiler_params=_params("arbitrary"),
    )(dh, x, dout, scale, g_pre)


def _adamw(w, g, m, v):
    m = ADAM_B1 * m + (1.0 - ADAM_B1) * g
    v = ADAM_B2 * v + (1.0 - ADAM_B2) * (g * g)
    m_hat = m / (1.0 - ADAM_B1 ** ADAM_STEP)
    v_hat = v / (1.0 - ADAM_B2 ** ADAM_STEP)
    delta = -ADAM_LR * (m_hat / (jnp.sqrt(v_hat) + ADAM_EPS) + ADAM_WD * w)
    return delta, m, v


def _pack_rows(arrays, plan, last):
    nrows = sum(arrays[t].shape[1] for t, _ in plan) // LANES

    def body(*refs):
        srcs, o_ref = refs[:-1], refs[-1]
        at = 0
        for t, r in plan:
            for j in range(arrays[t].shape[1] // LANES):
                o_ref[at:at + 1, :] = srcs[t][r:r + 1, j * LANES:(j + 1) * LANES]
                at += 1
        t, r = last
        o_ref[nrows:nrows + 8, :] = jnp.broadcast_to(srcs[t][r:r + 1, 0:LANES], (8, LANES))

    return pl.pallas_call(body, name="pack_small", out_shape=jax.ShapeDtypeStruct((nrows + 8, LANES), f32),
                          compiler_params=_params())(*arrays)


def _sum_rows(parts, dep):
    P = parts.shape[1]

    def body(p_ref, dep_ref, o_ref):
        acc = p_ref[0:1, :]
        for j in range(1, NDEV):
            acc = acc + p_ref[j:j + 1, :]
        o_ref[...] = jnp.broadcast_to(acc, (8, P))

    vmem = pl.BlockSpec(memory_space=pltpu.VMEM)
    return pl.pallas_call(body, name="sum_small", out_shape=jax.ShapeDtypeStruct((8, P), f32),
                          in_specs=[vmem, ANY_SPEC], out_specs=vmem, compiler_params=_params())(parts, dep)


def _adamw_small(tot, params):
    given = [p[3] for p in params if not isinstance(p[3], int)]

    def body(tot_ref, *refs):
        given_refs = list(refs[:len(given)])
        ins = refs[len(given):len(given) + 3 * len(params)]
        outs = refs[len(given) + 3 * len(params):]
        for t, (w, _, _, where) in enumerate(params):
            w_ref, m_ref, v_ref = ins[3 * t:3 * t + 3]
            g = tot_ref[0:1, where:where + w.size] if isinstance(where, int) else given_refs.pop(0)[...]
            outs[4 * t][...] = g
            outs[4 * t + 1][...], outs[4 * t + 2][...], outs[4 * t + 3][...] = _adamw(w_ref[...], g, m_ref[...], v_ref[...])

    out_shape = tuple(jax.ShapeDtypeStruct(p[0].shape, f32) for p in params for _ in range(4))
    res = pl.pallas_call(body, name="adamw_small", out_shape=out_shape, compiler_params=_params())(
        tot, *given, *[a for p in params for a in p[:3]])
    return [res[4 * t:4 * t + 4] for t in range(len(params))]


def _adamw_sharded(name, parts, sums_a, sums_b, pick, w, m, v, rows=None, prev=None, tr=128):
    R, Cc = w.shape
    r0, nr = rows or (0, R)
    tr = math.gcd(tr, r0, nr)
    n, b0 = parts.shape[0], r0 // tr

    def body(pick_ref, p_ref, a_ref, b_ref, w_ref, m_ref, v_ref, *rest):
        g_ref, d_ref, nm_ref, nv_ref = rest[-4:]
        g = jnp.where(pick_ref[0] == 1, b_ref[...], a_ref[...]).astype(f32)
        for j in range(n):
            g = g + p_ref[j].astype(f32)
        g_ref[...] = g
        d_ref[...], nm_ref[...], nv_ref[...] = _adamw(w_ref[...], g, m_ref[...], v_ref[...])

    row = pl.BlockSpec((tr, Cc), lambda i, pick: (i + b0, 0))
    mine = pl.BlockSpec((None, tr, Cc), lambda i, pick: (pick[1], i + b0, 0))
    out = jax.ShapeDtypeStruct((R, Cc), f32)
    prev = list(prev or [])
    grid_spec = pltpu.PrefetchScalarGridSpec(
        num_scalar_prefetch=1, grid=(nr // tr,),
        in_specs=[pl.BlockSpec((n, tr, Cc), lambda i, pick: (0, i + b0, 0)), mine, mine, row, row, row]
        + [ANY_SPEC] * len(prev),
        out_specs=(row, row, row, row))
    return pl.pallas_call(
        body, name=name, out_shape=(out, out, out, out), grid_spec=grid_spec,
        input_output_aliases={7 + t: t for t in range(len(prev))}, compiler_params=_params("arbitrary"),
    )(pick, parts, sums_a, sums_b, w, m, v, *prev)


def _adamw_ada(c_t, dmod_cols, w, m, v, dep, tr=512):
    D, W = w.shape
    tr = min(tr, D)

    def body(c_ref, dm_ref, w_ref, m_ref, v_ref, dep_ref, g_ref, d_ref, nm_ref, nv_ref):
        g = lax.dot_general(c_ref[...], dm_ref[...], (((1,), (0,)), ((), ())), preferred_element_type=f32,
                            precision=lax.Precision.HIGHEST)
        g_ref[...] = g
        d_ref[...], nm_ref[...], nv_ref[...] = _adamw(w_ref[...], g, m_ref[...], v_ref[...])

    row = pl.BlockSpec((tr, W), lambda i: (i, 0))
    out = jax.ShapeDtypeStruct((D, W), f32)
    return pl.pallas_call(
        body, name="adamw_ada", out_shape=(out, out, out, out), grid=(D // tr,),
        in_specs=[pl.BlockSpec((tr, NDEV), lambda i: (i, 0)), pl.BlockSpec((NDEV, W), lambda i: (0, 0)), row, row, row,
                  ANY_SPEC],
        out_specs=(row, row, row, row), compiler_params=_params("parallel"),
    )(c_t, dmod_cols, w, m, v, dep)


def kernel(x, c, w_ada, b_ada, g_pre, w_in, conv_w, conv_b, g_conv, g_attn, w_out, g_post, loss_target, m_w_ada, m_b_ada, m_g_pre, m_w_in, m_conv_w, m_conv_b, m_g_conv, m_g_attn, m_w_out, m_g_post, v_w_ada, v_b_ada, v_g_pre, v_w_in, v_conv_w, v_conv_b, v_g_conv, v_g_attn, v_w_out, v_g_post):
    S, D = x.shape[1], x.shape[2]
    C = D // 2
    W = w_ada.shape[2]
    CW = conv_w.shape[2]
    me = 4 * lax.axis_index("x") + 2 * lax.axis_index("y") + lax.axis_index("c")
    x2, tgt = x[0], loss_target[0]
    w_ada2, w_in2, w_out2 = w_ada[0], w_in[0], w_out[0]

    R = D // NDEV
    core = lax.axis_index("c").astype(jnp.int32).reshape(1)

    cw_slab = jnp.zeros((8, CW), f32).at[:3].set(conv_w[0])
    b_cols = lax.dynamic_slice_in_dim(b_ada, me * W, W, axis=1)
    mod_slabs, c_blocks, cw_g = _ada_exchange(c.reshape(D // 128, 128), cw_slab, w_ada2, b_cols)
    c_all = c_blocks.reshape(NDEV, D)
    conv_w_full = jnp.transpose(cw_g, (1, 0, 2)).reshape(8, C)
    mod = mod_slabs[:, 0, :].reshape(1, 3 * D)
    shift, scale, gate = mod[:, :D], mod[:, D:2 * D], mod[:, 2 * D:]

    land_i = lax.dynamic_update_slice(lax.empty((NDEV, D, C), bf16), w_in2.astype(bf16)[None], (me, 0, 0))
    land_o = lax.dynamic_update_slice(lax.empty((NDEV, R, D), bf16), w_out2.astype(bf16)[None], (me, 0, 0))
    wi_send, wi_recv, land_i, w_token = _w_in_start(land_i, [mod_slabs])

    me_arr = me.astype(jnp.int32).reshape(1)
    h = _prenorm(x2, scale, shift, g_pre, w_token)
    land_i = _w_in_sibling(land_i, wi_recv, after=[h])
    proj = _in_proj_part("in_proj_a", h, land_i, None, me_arr, 0, 1, 2)
    x_minus_t = _residual_minus_target(x2, tgt, proj)
    bias = _bias_tiles(_head_slopes(C // HEAD_DIM), S, x_minus_t)

    def landing(rows, cols):
        return lax.dynamic_update_slice(lax.empty((NCHIP, rows, cols), bf16), jnp.zeros((1, rows, cols), bf16),
                                        (me // 2, 0, 0))

    land_go, land_gi = landing(R, D), landing(D, C)
    fi_send, fi_recv, land_i = _w_in_relay(land_i, wi_recv, after=[proj, bias, land_go, land_gi])
    proj = _in_proj_part("in_proj_b", h, land_i, proj, me_arr, 2, 2, 2)
    land_i = _w_in_forwarded(land_i, fi_recv, after=[proj])
    proj = _in_proj_part("in_proj_c", h, land_i, proj, me_arr, 3, 2, 2)
    (di_send, di_recv, wo_send, wo_recv), land_i, land_o = _w_in_diag(land_i, land_o, fi_recv, after=[proj])
    proj = _in_proj_part("in_proj_d", h, land_i, proj, me_arr, 6, 1, 1)
    win_g = _w_in_finish(land_i, wi_send, fi_send, di_send, di_recv, after=[proj])
    proj = _in_proj_part("in_proj_e", h, win_g, proj, me_arr, 7, 1, 1)
    ycat = _conv_fwd(proj, conv_w_full, conv_b, g_conv)
    o, lse = _attn_fwd(proj, bias)
    (fo_send, fo_recv), (land_o,), _ = _weights_forward("w_out_forward", land_o, wo_recv, after=[o])
    ycat = _attn_post(ycat, o, proj, g_attn)
    wout_g = _weights_wait("w_out_wait", land_o, wo_send, wo_recv, fo_send, fo_recv, after=[ycat])
    wout_full = wout_g.reshape(D, D)
    y = _matmul(ycat, wout_full, name="out_proj", out_dtype=f32)
    dy, dout, post_sums = _sandwich(y, x_minus_t, gate, g_post)

    gw_out = _matmul(ycat, dy, name="out_proj_dw", out_dtype=bf16, ta=True).reshape(NDEV, R, D)
    dycat = _matmul(dy, wout_full, name="out_proj_dx", out_dtype=f32, tb=True)
    dpc, conv_sums = _conv_bwd(proj, dycat, conv_w_full, conv_b, g_conv, gw_out)
    gw_c = _matmul(h, dpc, name="in_proj_dw_conv", out_dtype=bf16, ta=True, out_slots=4)
    first_pairs, p1_token = _pairs_start("g_first_pair_start", [gw_out, gw_c])
    do, dpa, attn_sums = _attn_post_bwd(o, proj, dycat, g_attn, p1_token)
    (gw_out, pair_o), (gw_c, pair_c) = _pairs_wait("g_first_pair_wait", first_pairs, after=[do])
    sum_o = _pair_sum("g_out_pair_sum", gw_out, pair_o, core)
    sum_c = _pair_sum("g_conv_pair_sum", gw_c, pair_c, core)
    ((co_send, co_recv, sum_o, land_go), (cc_send, cc_recv, sum_c, land_gi)), cc_token = _chips_start(
        "g_first_chip_start", [(sum_o, land_go, 0), (sum_c, land_gi, 0)])
    dpa = _attn_bwd(proj, o, do, lse, bias, dpa, cc_token)
    gw_a = _matmul(h, dpa, name="in_proj_dw_attn", out_dtype=bf16, ta=True, b_slots=True, out_slots=4)
    pa_send, pa_recv, gw_a, pair_a, pa_token = _pair_start("g_attn_pair_start", gw_a)
    sum_o, land_go = _chip_wait("g_out_chip_wait", sum_o, land_go, co_send, co_recv, 0, after=[pa_token])
    pick_out = jnp.stack([jnp.int32(0), me // 2]).astype(jnp.int32)
    g_w_out, d_w_out, nm_w_out, nv_w_out = _adamw_sharded(
        "adamw_w_out", land_go, sum_o, sum_o, pick_out, w_out2, m_w_out[0], v_w_out[0])
    gw_a, pair_a = _pair_wait("g_attn_pair_wait", gw_a, pair_a, pa_send, pa_recv, after=[g_w_out])
    sum_a = _pair_sum("g_attn_pair_sum", gw_a, pair_a, core)
    part_a, part_b = (0, 3 * D // 4), (3 * D // 4, D // 4)
    ca_send, ca_recv, sum_a, land_gi, ca_token = _chip_start("g_attn_chip_start_a", sum_a, land_gi, 4, part_a)
    dh = _matmul_slabs_t(dpc, dpa, win_g, name="in_proj_dx", dep=ca_token)
    grad_x, pre_sums = _prenorm_bwd(dh, x2, dout, scale, g_pre)

    pre, post, conv, attn = range(4)
    small = _pack_rows([pre_sums, post_sums, conv_sums, attn_sums],
                       [(pre, 0), (pre, 1), (post, 0),
                        (pre, 2), (post, 1),
                        (conv, 2), (conv, 3), (conv, 4),
                        (conv, 1), (conv, 0), (attn, 0)],
                       last=(post, 2))
    gs_send, gs_recv, small, small_all, gs_token = _gather_start("gather_small_start", small)

    cb_send, cb_recv, sum_a, land_gi, cb_token = _chip_start("g_attn_chip_start_b", sum_a, land_gi, 4, part_b,
                                                             after=[gs_token])

    pick_in = jnp.stack([me // 4, (me % 4) // 2]).astype(jnp.int32)
    sum_c, land_gi = _chip_wait("g_conv_chip_wait", sum_c, land_gi, cc_send, cc_recv, 0, after=[cb_token])
    sum_a, land_gi = _chip_wait("g_attn_chip_wait_a", sum_a, land_gi, ca_send, ca_recv, 4, [cb_token], part_a)
    first = _adamw_sharded("adamw_w_in_a", land_gi, sum_c, sum_a, pick_in, w_in2, m_w_in[0], v_w_in[0], rows=part_a,
                           tr=256)

    small_all = _gather_wait("gather_small_wait", small, small_all, gs_send, gs_recv, after=[first[0]])
    small_all = small_all.reshape(NDEV, small.size)
    tot = _sum_rows(small_all, gs_token)
    loss = tot[0, 8 * D]
    taps_first = lambda a: jnp.transpose(a, (1, 0, 2))
    g_conv_w = lax.dynamic_slice_in_dim(tot[0:1, 5 * D:5 * D + 3 * C].reshape(3, 1, C), me * CW, CW, axis=2)
    ((g_b_ada, d_b_ada, nm_b_ada, nv_b_ada), (g_g_pre, d_g_pre, nm_g_pre, nv_g_pre),
     (g_g_post, d_g_post, nm_g_post, nv_g_post), conv_w_results,
     (g_conv_b, d_conv_b, nm_conv_b, nv_conv_b), (g_g_conv, d_g_conv, nm_g_conv, nv_g_conv),
     (g_g_attn, d_g_attn, nm_g_attn, nv_g_attn)) = _adamw_small(tot, [
         (b_ada, m_b_ada, v_b_ada, 0), (g_pre, m_g_pre, v_g_pre, 3 * D), (g_post, m_g_post, v_g_post, 4 * D),
         (taps_first(conv_w), taps_first(m_conv_w), taps_first(v_conv_w), g_conv_w),
         (conv_b, m_conv_b, v_conv_b, 5 * D + 3 * C),
         (g_conv, m_g_conv, v_g_conv, 5 * D + 4 * C), (g_attn, m_g_attn, v_g_attn, 5 * D + 5 * C)])
    g_conv_w, d_conv_w, nm_conv_w, nv_conv_w = (taps_first(a) for a in conv_w_results)

    dmod_cols = lax.dynamic_slice_in_dim(small_all[:, :3 * D], me * W, W, axis=1)
    g_w_ada, d_w_ada, nm_w_ada, nv_w_ada = _adamw_ada(c_all.T, dmod_cols, w_ada2, m_w_ada[0], v_w_ada[0], gs_token)

    sum_a, land_gi = _chip_wait("g_attn_chip_wait_b", sum_a, land_gi, cb_send, cb_recv, 4, [g_w_ada], part_b)
    g_w_in, d_w_in, nm_w_in, nv_w_in = _adamw_sharded(
        "adamw_w_in_b", land_gi, sum_c, sum_a, pick_in, w_in2, m_w_in[0], v_w_in[0], rows=part_b, prev=first, tr=256)

    return (loss, grad_x[None],
            g_w_ada[None], g_b_ada, g_g_pre, g_w_in[None], g_conv_w, g_conv_b, g_g_conv, g_g_attn, g_w_out[None], g_g_post,
            d_w_ada[None], d_b_ada, d_g_pre, d_w_in[None], d_conv_w, d_conv_b, d_g_conv, d_g_attn, d_w_out[None], d_g_post,
            nm_w_ada[None], nm_b_ada, nm_g_pre, nm_w_in[None], nm_conv_w, nm_conv_b, nm_g_conv, nm_g_attn, nm_w_out[None], nm_g_post,
            nv_w_ada[None], nv_b_ada, nv_g_pre, nv_w_in[None], nv_conv_w, nv_conv_b, nv_g_conv, nv_g_attn, nv_w_out[None], nv_g_post)
```

```python
import functools
import math

import jax
import jax.numpy as jnp
from jax import lax
from jax.experimental import pallas as pl
from jax.experimental.pallas import tpu as pltpu

f32 = jnp.float32
bf16 = jnp.bfloat16

NDEV = 8
HEAD_DIM = 64
LANES = 128
PAIR = 2 * HEAD_DIM
BRANCHES = ((128, 1, 1), (512, 4, 1), (2048, 16, 2))
HALF_WIN = 64
EPS = 1e-6
NEG_INF = -1e30
ADAM_LR, ADAM_B1, ADAM_B2, ADAM_EPS, ADAM_WD, ADAM_STEP = 0.001, 0.9, 0.999, 1e-08, 0.01, 10
MESH = pl.DeviceIdType.MESH
VMEM_LIMIT = 56 * 1024 * 1024
HBM_SPEC = pl.BlockSpec(memory_space=pltpu.HBM)
ANY_SPEC = pl.BlockSpec(memory_space=pl.ANY)
SEM_SPEC = pl.BlockSpec(memory_space=pltpu.SEMAPHORE)


def _params(*sem):
    return pltpu.CompilerParams(dimension_semantics=sem or None, vmem_limit_bytes=VMEM_LIMIT)


def _silu(z):
    return z * jax.nn.sigmoid(z)


def _silu_and_slope(z):
    s = jax.nn.sigmoid(z)
    return z * s, s * (1.0 + z * (1.0 - s))


def _my_place():
    x, y, c = lax.axis_index("x"), lax.axis_index("y"), lax.axis_index("c")
    return x, y, c, 4 * x + 2 * y + c


def _peer(x, y, c, k):
    px, py, pc = x ^ (k >> 2 & 1), y ^ (k >> 1 & 1), c ^ (k & 1)
    return (px, py, pc), 4 * px + 2 * py + pc


def _comm_call(name, arrays, sems, new_sems, body, after=(), token=False):
    na, ns, nn, nf = len(arrays), len(sems), len(new_sems), len(after)

    def kern(*refs):
        ins, outs = refs[:na + ns + nf], refs[na + ns + nf:]
        body(ins[:na], ins[na:na + ns], outs[:nn])
        if token:
            outs[nn + na][...] = jnp.zeros((8, 128), f32)

    out_shape = ([pltpu.SemaphoreType.DMA(s) for s in new_sems] + [pltpu.HBM(a.shape, a.dtype) for a in arrays]
                 + ([jax.ShapeDtypeStruct((8, 128), f32)] if token else []))
    out_specs = [SEM_SPEC] * nn + [HBM_SPEC] * na + ([pl.BlockSpec(memory_space=pltpu.VMEM)] if token else [])
    res = pl.pallas_call(
        kern, name=name, out_shape=tuple(out_shape),
        in_specs=[HBM_SPEC] * na + [SEM_SPEC] * ns + [ANY_SPEC] * nf, out_specs=tuple(out_specs),
        input_output_aliases={t: nn + t for t in range(na)},
        compiler_params=pltpu.CompilerParams(has_side_effects=pltpu.SideEffectType.DATAFLOW_SIDE_EFFECTING),
    )(*[pltpu.with_memory_space_constraint(a, pltpu.HBM) for a in arrays], *sems, *after)
    return list(res[:nn]), list(res[nn:nn + na]), (res[nn + na] if token else None)


def _remote(src, dst, send_sem, recv_sem, device):
    return pltpu.make_async_remote_copy(src_ref=src, dst_ref=dst, send_sem=send_sem, recv_sem=recv_sem,
                                        device_id=device, device_id_type=MESH)


def _gather_start(name, src):
    def body(a, s, new):
        (src, land), (send, recv) = a, new
        x, y, c, me = _my_place()
        pltpu.make_async_copy(src, land.at[me], recv.at[0]).start()
        for k in range(1, NDEV):
            peer, _ = _peer(x, y, c, k)
            _remote(src, land.at[me], send.at[k], recv.at[k], peer).start()

    land = lax.empty((NDEV,) + src.shape, src.dtype)
    (send, recv), (src, land), token = _comm_call(name, [src, land], [], [(NDEV,), (NDEV,)], body, token=True)
    return send, recv, src, land, token


def _gather_wait(name, src, land, send, recv, after):
    def body(a, s, new):
        (src, land), (send, recv) = a, s
        x, y, c, me = _my_place()
        pltpu.make_async_copy(src, land.at[me], recv.at[0]).wait()
        for k in range(1, NDEV):
            peer, slot = _peer(x, y, c, k)
            _remote(src, land.at[slot], send.at[k], recv.at[k], peer).wait_recv()
        for k in range(1, NDEV):
            peer, _ = _peer(x, y, c, k)
            _remote(src, land.at[me], send.at[k], recv.at[k], peer).wait_send()

    return _comm_call(name, [src, land], [send, recv], [], body, after=after)[1][1]


SAME_CORE = (2, 4, 6)
VIA_SIBLING = (3, 5, 7)


def _weights_forward(name, land, recv, after):
    def body(a, s, new):
        (land,), (recv,), (fsend, frecv) = a, s, new
        x, y, c, me = _my_place()
        sibling, _ = _peer(x, y, c, 1)
        for k in SAME_CORE:
            peer, slot = _peer(x, y, c, k)
            _remote(land.at[slot], land.at[slot], fsend.at[k], recv.at[k], peer).wait_recv()
            _remote(land.at[slot], land.at[slot], fsend.at[k], frecv.at[k ^ 1], sibling).start()

    return _comm_call(name, [land], [recv], [(NDEV,), (NDEV,)], body, after=after, token=True)


def _weights_wait(name, land, send, recv, fsend, frecv, after):
    def body(a, s, new):
        (land,), (send, recv, fsend, frecv) = a, s
        x, y, c, me = _my_place()
        sibling, sib_slot = _peer(x, y, c, 1)
        _remote(land.at[sib_slot], land.at[sib_slot], send.at[1], recv.at[1], sibling).wait_recv()
        for k in VIA_SIBLING:
            _, slot = _peer(x, y, c, k)
            _remote(land.at[slot], land.at[slot], fsend.at[k ^ 1], frecv.at[k], sibling).wait_recv()
        for k in (1,) + SAME_CORE:
            peer, _ = _peer(x, y, c, k)
            _remote(land.at[me], land.at[me], send.at[k], recv.at[k], peer).wait_send()
        for k in SAME_CORE:
            _, slot = _peer(x, y, c, k)
            _remote(land.at[slot], land.at[slot], fsend.at[k], frecv.at[k ^ 1], sibling).wait_send()

    return _comm_call(name, [land], [send, recv, fsend, frecv], [], body, after=after)[1][0]


def _diag_relay(x, y, c):
    slot = 4 * (x ^ (1 - c)) + 2 * (y ^ c) + c
    return slot, (x ^ c, y ^ (1 - c), c)


def _w_in_start(land, after):
    def body(a, s, new):
        (land,), (send, recv) = a, new
        x, y, c, me = _my_place()
        for k in (1, 2, 4):
            peer, _ = _peer(x, y, c, k)
            _remote(land.at[me], land.at[me], send.at[k], recv.at[k], peer).start()

    (send, recv), (land,), token = _comm_call("w_in_start", [land], [], [(NDEV,), (NDEV,)], body, after=after, token=True)
    return send, recv, land, token


def _w_in_sibling(land, recv, after):
    def body(a, s, new):
        (land,), (recv,) = a, s
        x, y, c, me = _my_place()
        sibling, slot = _peer(x, y, c, 1)
        _remote(land.at[slot], land.at[slot], recv.at[1], recv.at[1], sibling).wait_recv()

    return _comm_call("w_in_sibling", [land], [recv], [], body, after=after)[1][0]


def _w_in_relay(land, recv, after):
    def body(a, s, new):
        (land,), (recv,), (fsend, frecv) = a, s, new
        x, y, c, me = _my_place()
        sibling, _ = _peer(x, y, c, 1)
        for k in (2, 4):
            peer, slot = _peer(x, y, c, k)
            _remote(land.at[slot], land.at[slot], fsend.at[k], recv.at[k], peer).wait_recv()
        slot, target = _diag_relay(x, y, c)
        _remote(land.at[slot], land.at[slot], fsend.at[6], frecv.at[6], target).start()
        for k in (2, 4):
            _, slot = _peer(x, y, c, k)
            _remote(land.at[slot], land.at[slot], fsend.at[k], frecv.at[k ^ 1], sibling).start()

    (fsend, frecv), (land,), _ = _comm_call("w_in_relay", [land], [recv], [(NDEV,), (NDEV,)], body, after=after)
    return fsend, frecv, land


def _w_in_forwarded(land, frecv, after):
    def body(a, s, new):
        (land,), (frecv,) = a, s
        x, y, c, me = _my_place()
        sibling, _ = _peer(x, y, c, 1)
        for k in (3, 5):
            _, slot = _peer(x, y, c, k)
            _remote(land.at[slot], land.at[slot], frecv.at[k], frecv.at[k], sibling).wait_recv()

    return _comm_call("w_in_forwarded", [land], [frecv], [], body, after=after)[1][0]


def _w_in_diag(land, land_o, frecv, after):
    def body(a, s, new):
        (land, land_o), (frecv,), (dsend, drecv, osend, orecv) = a, s, new
        x, y, c, me = _my_place()
        sibling, _ = _peer(x, y, c, 1)
        peer, slot = _peer(x, y, c, 6)
        _remote(land.at[slot], land.at[slot], dsend.at[6], frecv.at[6], peer).wait_recv()
        _remote(land.at[slot], land.at[slot], dsend.at[6], drecv.at[7], sibling).start()
        for k in (1,) + SAME_CORE:
            peer, _ = _peer(x, y, c, k)
            _remote(land_o.at[me], land_o.at[me], osend.at[k], orecv.at[k], peer).start()

    sems, (land, land_o), _ = _comm_call("w_in_diag", [land, land_o], [frecv], [(NDEV,)] * 4, body, after=after)
    return sems, land, land_o


def _w_in_finish(land, send, fsend, dsend, drecv, after):
    def body(a, s, new):
        (land,), (send, fsend, dsend, drecv) = a, s
        x, y, c, me = _my_place()
        sibling, _ = _peer(x, y, c, 1)
        _, slot = _peer(x, y, c, 7)
        _remote(land.at[slot], land.at[slot], dsend.at[6], drecv.at[7], sibling).wait_recv()
        for k in (1, 2, 4):
            peer, _ = _peer(x, y, c, k)
            _remote(land.at[me], land.at[me], send.at[k], send.at[k], peer).wait_send()
        for k in (2, 4, 6):
            _, slot = _peer(x, y, c, k)
            _remote(land.at[slot], land.at[slot], fsend.at[k], fsend.at[k], sibling).wait_send()
        _, slot = _peer(x, y, c, 6)
        _remote(land.at[slot], land.at[slot], dsend.at[6], dsend.at[6], sibling).wait_send()

    return _comm_call("w_in_finish", [land], [send, fsend, dsend, drecv], [], body, after=after)[1][0]


def _in_proj_part(name, h, land, proj, me_arr, k0, kstep, nk, tm=512):
    S, D = h.shape
    C = land.shape[2]
    tm = min(tm, S)

    def body(me_ref, a_ref, b_ref, *rest):
        rest[-1][...] = jnp.dot(a_ref[...], b_ref[...], preferred_element_type=f32)

    slot = lambda j, me: me[0] ^ (k0 + kstep * j)
    args = [h, land] + ([] if proj is None else [proj])
    grid_spec = pltpu.PrefetchScalarGridSpec(
        num_scalar_prefetch=1, grid=(nk, S // tm),
        in_specs=[pl.BlockSpec((tm, D), lambda j, i, me: (i, 0)),
                  pl.BlockSpec((None, D, C), lambda j, i, me: (slot(j, me), 0, 0))] + [ANY_SPEC] * (len(args) - 2),
        out_specs=pl.BlockSpec((tm, C), lambda j, i, me: (i, slot(j, me))))
    return pl.pallas_call(
        body, name=name, out_shape=jax.ShapeDtypeStruct((S, NDEV * C), f32), grid_spec=grid_spec,
        input_output_aliases={} if proj is None else {3: 0}, compiler_params=_params("arbitrary", "arbitrary"),
    )(me_arr, *args)


NCHIP = NDEV // 2


def _pairs_start(name, srcs):
    n = len(srcs)
    npairs = [src.shape[0] // 2 for src in srcs]

    def body(a, s, new):
        x, y, c, me = _my_place()
        sibling, _ = _peer(x, y, c, 1)
        for t in range(n):
            src, pair, send, recv = a[t], a[n + t], new[t], new[n + t]
            for i in range(npairs[t]):
                _remote(src.at[2 * i + 1 - c], pair.at[i], send.at[i], recv.at[i], sibling).start()

    pairs = [lax.empty((npairs[t],) + srcs[t].shape[1:], srcs[t].dtype) for t in range(n)]
    sems, arrays, token = _comm_call(name, list(srcs) + pairs, [], [(m,) for m in npairs] * 2, body, token=True)
    return [(sems[t], sems[n + t], arrays[t], arrays[n + t]) for t in range(n)], token


def _pairs_wait(name, groups, after):
    n = len(groups)

    def body(a, s, new):
        x, y, c, me = _my_place()
        sibling, _ = _peer(x, y, c, 1)
        for t in range(n):
            src, pair, send, recv = a[t], a[n + t], s[t], s[n + t]
            for i in range(pair.shape[0]):
                cp = _remote(src.at[2 * i + 1 - c], pair.at[i], send.at[i], recv.at[i], sibling)
                cp.wait_recv()
                cp.wait_send()

    arrays = _comm_call(name, [g[2] for g in groups] + [g[3] for g in groups],
                        [g[0] for g in groups] + [g[1] for g in groups], [], body, after=after)[1]
    return [(arrays[t], arrays[n + t]) for t in range(n)]


def _pair_start(name, src):
    ((send, recv, src, pair),), token = _pairs_start(name, [src])
    return send, recv, src, pair, token


def _pair_wait(name, src, pair, send, recv, after):
    return _pairs_wait(name, [(send, recv, src, pair)], after)[0]


def _pair_sum(name, src, pair, core, tr=1024):
    npair, R, Cc = pair.shape
    tr = min(tr, R)

    def body(core_ref, a_ref, b_ref, o_ref):
        o_ref[...] = (a_ref[...].astype(f32) + b_ref[...].astype(f32)).astype(o_ref.dtype)

    grid_spec = pltpu.PrefetchScalarGridSpec(
        num_scalar_prefetch=1, grid=(npair, R // tr),
        in_specs=[pl.BlockSpec((None, tr, Cc), lambda i, r, core: (2 * i + core[0], r, 0)),
                  pl.BlockSpec((None, tr, Cc), lambda i, r, core: (i, r, 0))],
        out_specs=pl.BlockSpec((None, tr, Cc), lambda i, r, core: (i, r, 0)))
    return pl.pallas_call(body, name=name, out_shape=jax.ShapeDtypeStruct(pair.shape, pair.dtype),
                          grid_spec=grid_spec, compiler_params=_params("parallel", "parallel"))(core, src, pair)


def _owner_chip(first, i):
    q = first // 2 + i
    return q >> 1 & 1, q & 1


def _chips_start(name, groups, rows=None, after=()):
    n = len(groups)
    row_of = [pl.ds(*(rows or (0, g[0].shape[1]))) for g in groups]

    def body(a, s, new):
        x, y, c, me = _my_place()
        for t, (_, _, first) in enumerate(groups):
            sums, land, send, recv = a[t], a[n + t], new[t], new[n + t]
            for i in range(sums.shape[0]):
                ox, oy = _owner_chip(first, i)

                @pl.when((x != ox) | (y != oy))
                def _():
                    _remote(sums.at[i, row_of[t]], land.at[2 * x + y, row_of[t]], send.at[i], recv.at[2 * x + y],
                            (ox, oy, c)).start()

    sems, arrays, token = _comm_call(name, [g[0] for g in groups] + [g[1] for g in groups], [],
                                     [(g[0].shape[0],) for g in groups] + [(NCHIP,)] * n, body, after=after, token=True)
    return [(sems[t], sems[n + t], arrays[t], arrays[n + t]) for t in range(n)], token


def _chips_wait(name, groups, firsts, after, rows=None):
    n = len(groups)
    row_of = [pl.ds(*(rows or (0, g[2].shape[1]))) for g in groups]

    def body(a, s, new):
        x, y, c, me = _my_place()
        for t in range(n):
            sums, land, send, recv, first = a[t], a[n + t], s[t], s[n + t], firsts[t]
            npair = sums.shape[0]
            mine = (me >= first) & (me < first + 2 * npair)
            for i in range(npair):
                ox, oy = _owner_chip(first, i)

                @pl.when((x != ox) | (y != oy))
                def _():
                    _remote(sums.at[i, row_of[t]], land.at[2 * x + y, row_of[t]], send.at[i], recv.at[2 * x + y],
                            (ox, oy, c)).wait_send()
            for q in range(NCHIP):
                @pl.when(mine & (2 * x + y != q))
                def _():
                    _remote(sums.at[0, row_of[t]], land.at[q, row_of[t]], send.at[0], recv.at[q],
                            (q >> 1, q & 1, c)).wait_recv()

    arrays = _comm_call(name, [g[2] for g in groups] + [g[3] for g in groups],
                        [g[0] for g in groups] + [g[1] for g in groups], [], body, after=after)[1]
    return [(arrays[t], arrays[n + t]) for t in range(n)]


def _chip_start(name, sums, land, first, rows=None, after=()):
    ((send, recv, sums, land),), token = _chips_start(name, [(sums, land, first)], rows, after)
    return send, recv, sums, land, token


def _chip_wait(name, sums, land, send, recv, first, after, rows=None):
    return _chips_wait(name, [(send, recv, sums, land)], [first], after, rows)[0]


def _matmul(a, b, *, name, out_dtype, ta=False, tb=False, b_slots=False, out_slots=0, b_cols=None,
            tm=1024, tn=1024, tk=2048, dep=None):
    M, K = (a.shape[1], a.shape[0]) if ta else a.shape
    col0 = 0
    if b_slots:
        slab = b.shape[2]
        N = b.shape[1] if tb else b.shape[0] * slab
        assert (K if tb else N) == b.shape[0] * slab
    elif b_cols is not None:
        assert not tb
        col0, N = b_cols
    else:
        N = b.shape[0] if tb else b.shape[1]
    tm, tn, tk = min(tm, M), min(tn, N), min(tk, K)
    if b_slots:
        if tb:
            tk = min(tk, slab)
        else:
            tn = min(tn, slab)
    if out_slots:
        tn = min(tn, N // out_slots)
    nm, nn, nk = M // tm, N // tn, K // tk
    assert (nm * tm, nn * tn, nk * tk) == (M, N, K) and col0 % tn == 0, (name, M, N, K, tm, tn, tk)
    j0 = col0 // tn

    a_spec = pl.BlockSpec((tk, tm), lambda i, j, k: (k, i)) if ta else pl.BlockSpec((tm, tk), lambda i, j, k: (i, k))
    if b_slots and tb:
        per = slab // tk
        b_spec = pl.BlockSpec((None, tn, tk), lambda i, j, k: (k // per, j, k % per))
    elif b_slots:
        per = slab // tn
        b_spec = pl.BlockSpec((None, tk, tn), lambda i, j, k: (j // per, k, j % per))
    elif tb:
        b_spec = pl.BlockSpec((tn, tk), lambda i, j, k: (j, k))
    else:
        b_spec = pl.BlockSpec((tk, tn), lambda i, j, k: (k, j + j0))
    if out_slots:
        per_o = (N // out_slots) // tn
        o_spec = pl.BlockSpec((None, tm, tn), lambda i, j, k: (j // per_o, i, j % per_o))
        out_shape = jax.ShapeDtypeStruct((out_slots, M, N // out_slots), out_dtype)
    else:
        o_spec = pl.BlockSpec((tm, tn), lambda i, j, k: (i, j))
        out_shape = jax.ShapeDtypeStruct((M, N), out_dtype)
    dims = (((0 if ta else 1,), (1 if tb else 0,)), ((), ()))
    deps = [] if dep is None else [dep]

    def body(a_ref, b_ref, *rest):
        o_ref = rest[len(deps)]
        prod = lax.dot_general(a_ref[...], b_ref[...], dims, preferred_element_type=f32)
        if nk == 1:
            o_ref[...] = prod.astype(out_dtype)
            return
        acc_ref = rest[len(deps) + 1]
        k = pl.program_id(2)

        @pl.when(k == 0)
        def _():
            acc_ref[...] = prod

        @pl.when((k > 0) & (k < nk - 1))
        def _():
            acc_ref[...] += prod

        @pl.when(k == nk - 1)
        def _():
            o_ref[...] = (acc_ref[...] + prod).astype(out_dtype)

    return pl.pallas_call(
        body, name=name, out_shape=out_shape, grid=(nm, nn, nk),
        in_specs=[a_spec, b_spec] + [ANY_SPEC] * len(deps), out_specs=o_spec,
        scratch_shapes=[pltpu.VMEM((tm, tn), f32)] if nk > 1 else [],
        compiler_params=_params("parallel", "parallel", "arbitrary"),
    )(a, b, *deps)


def _matmul_slabs_t(a_cols, a_slots, b, *, name, tm=512, tn=512, dep=None):
    M = a_cols.shape[0]
    n_slab, N, slab = b.shape
    n1, n2 = a_cols.shape[1] // slab, a_slots.shape[0]
    assert n1 + n2 == n_slab and a_slots.shape[1:] == (M, slab)
    tm, tn = min(tm, M), min(tn, N)
    deps = [] if dep is None else [dep]

    def body(a1_ref, a2_ref, b_ref, *rest):
        o_ref = rest[len(deps)]
        acc = None
        for s in range(n_slab):
            lhs = a1_ref[:, s * slab:(s + 1) * slab] if s < n1 else a2_ref[s - n1]
            prod = lax.dot_general(lhs, b_ref[s], (((1,), (1,)), ((), ())), preferred_element_type=f32)
            acc = prod if acc is None else acc + prod
        o_ref[...] = acc

    return pl.pallas_call(
        body, name=name, out_shape=jax.ShapeDtypeStruct((M, N), f32), grid=(M // tm, N // tn),
        in_specs=[pl.BlockSpec((tm, n1 * slab), lambda i, j: (i, 0)), pl.BlockSpec((n2, tm, slab), lambda i, j: (0, i, 0)),
                  pl.BlockSpec((n_slab, tn, slab), lambda i, j: (0, j, 0))] + [ANY_SPEC] * len(deps),
        out_specs=pl.BlockSpec((tm, tn), lambda i, j: (i, j)), compiler_params=_params("parallel", "parallel"),
    )(a_cols, a_slots, b, *deps)


def _ada_exchange(c_blk, cw_slab, w_ada, b_cols):
    nblk = c_blk.shape[0]
    D, W = w_ada.shape
    CW = cw_slab.shape[1]

    def body(c_ref, cw_ref, w_ref, b_ref, mod_ref, call_ref, cwg_ref, msend, send_sems, recv_sems):
        x, y, c, me = _my_place()
        call_ref[me] = _silu(c_ref[...])
        cwg_ref[me] = cw_ref[...]
        first = []
        for k in range(1, NDEV):
            peer, _ = _peer(x, y, c, k)
            first.append(_remote(call_ref.at[me], call_ref.at[me], send_sems.at[0, k], recv_sems.at[0, k], peer))
            first.append(_remote(cwg_ref.at[me], cwg_ref.at[me], send_sems.at[1, k], recv_sems.at[1, k], peer))
        for cp in first:
            cp.start()
        for k in range(1, NDEV):
            peer, slot = _peer(x, y, c, k)
            _remote(call_ref.at[slot], call_ref.at[slot], send_sems.at[0, k], recv_sems.at[0, k], peer).wait_recv()
            _remote(cwg_ref.at[slot], cwg_ref.at[slot], send_sems.at[1, k], recv_sems.at[1, k], peer).wait_recv()
        mod = jnp.broadcast_to(b_ref[...], (NDEV, W))
        for r in range(nblk):
            mod = mod + lax.dot_general(call_ref[:, r, :], w_ref[r * 128:(r + 1) * 128, :], (((1,), (0,)), ((), ())),
                                        preferred_element_type=f32, precision=lax.Precision.HIGHEST)
        row = lax.broadcasted_iota(jnp.int32, (NDEV, 1), 0)
        pick = lambda j: jnp.broadcast_to(jnp.sum(jnp.where(row == j, mod, 0.0), axis=0, keepdims=True), (8, W))
        mod_ref[me] = pick(me)
        second = []
        for k in range(1, NDEV):
            peer, slot = _peer(x, y, c, k)
            msend[k] = pick(slot)
            second.append(_remote(msend.at[k], mod_ref.at[me], send_sems.at[2, k], recv_sems.at[2, k], peer))
        for cp in second:
            cp.start()
        for k in range(1, NDEV):
            peer, slot = _peer(x, y, c, k)
            _remote(msend.at[k], mod_ref.at[slot], send_sems.at[2, k], recv_sems.at[2, k], peer).wait_recv()
        for cp in first + second:
            cp.wait_send()

    vmem = pl.BlockSpec(memory_space=pltpu.VMEM)
    return pl.pallas_call(
        body, name="ada_exchange",
        out_shape=(jax.ShapeDtypeStruct((NDEV, 8, W), f32), jax.ShapeDtypeStruct((NDEV, nblk, 128), f32),
                   jax.ShapeDtypeStruct((NDEV, 8, CW), f32)),
        in_specs=[vmem] * 4, out_specs=(vmem, vmem, vmem),
        scratch_shapes=[pltpu.VMEM((NDEV, 8, W), f32), pltpu.SemaphoreType.DMA((3, NDEV)),
                        pltpu.SemaphoreType.DMA((3, NDEV))],
        compiler_params=_params(),
    )(c_blk, cw_slab, w_ada, b_cols)


def _prenorm(x, scale, shift, g_pre, dep, tr=512):
    S, D = x.shape
    tr = min(tr, S)

    def body(x_ref, sc_ref, sh_ref, g_ref, dep_ref, h_ref):
        xv = x_ref[...]
        r = lax.rsqrt(jnp.mean(xv * xv, axis=-1, keepdims=True) + EPS)
        h_ref[...] = ((xv * r) * g_ref[...] * (1.0 + sc_ref[...]) + sh_ref[...]).astype(bf16)

    row = pl.BlockSpec((tr, D), lambda i: (i, 0))
    vec = pl.BlockSpec((1, D), lambda i: (0, 0))
    return pl.pallas_call(body, name="prenorm", out_shape=jax.ShapeDtypeStruct((S, D), bf16), grid=(S // tr,),
                          in_specs=[row, vec, vec, vec, ANY_SPEC], out_specs=row, compiler_params=_params("parallel"))(
                              x, scale, shift, g_pre, dep)


def _ext_rows(i, tr, S):
    g = lax.broadcasted_iota(jnp.int32, (tr + 16, 1), 0) + (i * tr - 8)
    return (g >= 0) & (g < S)


def _halo_specs(tr, S, C, col):
    nb8 = S // 8
    main = pl.BlockSpec((tr, C), lambda i: (i, col))
    prev = pl.BlockSpec((8, C), lambda i: (jnp.maximum(i * (tr // 8) - 1, 0), col))
    nxt = pl.BlockSpec((8, C), lambda i: (jnp.minimum((i + 1) * (tr // 8), nb8 - 1), col))
    return prev, main, nxt


def _conv_fwd(proj, conv_w, conv_b, g_conv, dep, tr=512):
    S, C = proj.shape[0], proj.shape[1] // 8
    tr = min(tr, S)

    def body(up, um, un, cp, cm, cn, bg_ref, zc_ref, w_ref, cb_ref, g_ref, dep_ref, o_ref):
        i = pl.program_id(0)
        exists = _ext_rows(i, tr, S)
        u = jnp.concatenate([up[...], um[...], un[...]], axis=0)
        cg = jnp.concatenate([cp[...], cm[...], cn[...]], axis=0)
        t = jnp.where(exists, cg * u, 0.0)
        t_before = pltpu.roll(t, 1, 0)[8:tr + 8]
        t_after = pltpu.roll(t, tr + 15, 0)[8:tr + 8]
        w = w_ref[...]
        cv = w[0:1] * t_before + w[1:2] * t[8:tr + 8] + w[2:3] * t_after + cb_ref[...]
        yc = bg_ref[...] * cv
        rc = lax.rsqrt(jnp.mean(yc * yc, axis=-1, keepdims=True) + EPS)
        o_ref[...] = ((yc * rc) * g_ref[...] * _silu(zc_ref[...])).astype(bf16)

    u_specs = _halo_specs(tr, S, C, 0)
    c_specs = _halo_specs(tr, S, C, 2)
    vec = pl.BlockSpec((1, C), lambda i: (0, 0))
    return pl.pallas_call(
        body, name="conv_fwd", out_shape=jax.ShapeDtypeStruct((S, 2 * C), bf16), grid=(S // tr,),
        in_specs=[*u_specs, *c_specs, pl.BlockSpec((tr, C), lambda i: (i, 1)), pl.BlockSpec((tr, C), lambda i: (i, 3)),
                  pl.BlockSpec((8, C), lambda i: (0, 0)), vec, vec, ANY_SPEC],
        out_specs=pl.BlockSpec((tr, C), lambda i: (i, 0)), compiler_params=_params("parallel"),
    )(proj, proj, proj, proj, proj, proj, proj, proj, conv_w, conv_b, g_conv, dep)


def _branch_geometry(S, r, inter):
    L = S // r * inter
    nq = min(128, L)
    nk = min(nq + 2 * HALF_WIN * inter, L)
    assert L % nq == 0 and (L == nk or L >= nq + 2 * HALF_WIN * inter)
    return L, nq, nk, L // nq


QUAD = 4


def _to_quad(dst, src, S):
    n = S // QUAD
    for rho in range(QUAD):
        dst[pl.ds(rho * n, n), :] = src[pl.ds(rho, n, stride=QUAD), :]


def _block_rows(idx, r, inter, S, L, nq, nk, nblk):
    rho, qb = (0, idx) if r == 1 else (idx // nblk, idx % nblk)
    i0 = qb * nq
    ws = jnp.clip(i0 - HALF_WIN * inter, 0, L - nk)
    if r == 1:
        return pl.ds(pl.multiple_of(i0, 8), nq), pl.ds(pl.multiple_of(ws, 8), nk), i0 - ws
    assert r % (QUAD * inter) == 0
    step = r // QUAD // inter
    base = (rho % QUAD) * (S // QUAD) + rho // QUAD
    if step == 1:
        return pl.ds(pl.multiple_of(base + i0, 8), nq), pl.ds(pl.multiple_of(base + ws, 8), nk), i0 - ws
    return pl.ds(base + step * i0, nq, stride=step), pl.ds(base + step * ws, nk, stride=step), i0 - ws


N_CASES = 3
SCALE = HEAD_DIM ** -0.5
ATTN_UNROLL = 16


def _bias_shape(S):
    shapes = [_branch_geometry(S, r, inter)[1:3] for _, r, inter in BRANCHES]
    return (len(BRANCHES) * N_CASES * 2, max(nq for nq, _ in shapes), max(nk for _, nk in shapes))


def _bias_index(b, case, head):
    return (b * N_CASES + case) * 2 + head


def _fill_bias(bias_scr, sl_ref, S):
    sl = sl_ref[...]
    slope = (sl[0:1, 0:1], sl[0:1, HEAD_DIM:HEAD_DIM + 1])
    for b, (_, r, inter) in enumerate(BRANCHES):
        L, nq, nk, nblk = _branch_geometry(S, r, inter)
        rel = lax.broadcasted_iota(jnp.int32, (nq, nk), 0) - lax.broadcasted_iota(jnp.int32, (nq, nk), 1)
        for case in range(N_CASES):
            d = jnp.abs(rel + case * HALF_WIN)
            valid = d <= HALF_WIN * inter
            if inter > 1:
                valid = valid & (jnp.bitwise_and(d, inter - 1) == 0)
            dist = d.astype(f32) * float(r // inter)
            for head in range(2):
                bias_scr[_bias_index(b, case, head), 0:nq, 0:nk] = jnp.where(valid, -slope[head] * dist, NEG_INF)


def _bias_tiles(slopes, S, dep):
    npair = slopes.shape[0]
    shape = _bias_shape(S)

    def body(sl_ref, dep_ref, o_ref):
        _fill_bias(o_ref, sl_ref, S)

    return pl.pallas_call(
        body, name="bias_tiles", out_shape=jax.ShapeDtypeStruct((npair,) + shape, f32), grid=(npair,),
        in_specs=[pl.BlockSpec((None, 8, PAIR), lambda p: (p, 0, 0)), ANY_SPEC],
        out_specs=pl.BlockSpec((None,) + shape, lambda p: (p, 0, 0, 0)), compiler_params=_params("parallel"),
    )(slopes, dep)


def _head_slopes(n_heads):
    slopes = 2.0 ** (-8.0 * jnp.arange(1, n_heads + 1, dtype=f32) / n_heads)
    return jnp.broadcast_to(jnp.repeat(slopes.reshape(n_heads // 2, 2), HEAD_DIM, axis=1)[:, None, :],
                            (n_heads // 2, 8, PAIR))


def _attn_fwd(proj, bias):
    S, C = proj.shape[0], proj.shape[1] // 8
    npair = C // PAIR

    def body(q_ref, k_ref, v_ref, bias_scr, o_ref, lse_ref, m_scr, l_scr, a_scr, q4_scr, k4_scr, v4_scr):
        lane = lax.broadcasted_iota(jnp.int32, (1, PAIR), 1)
        first = lane < HEAD_DIM
        for dst, src in ((q4_scr, q_ref), (k4_scr, k_ref), (v4_scr, v_ref)):
            _to_quad(dst, src, S)

        for b, (_, r, inter) in enumerate(BRANCHES):
            L, nq, nk, nblk = _branch_geometry(S, r, inter)
            qs, ks, vs = (q_ref, k_ref, v_ref) if r == 1 else (q4_scr, k4_scr, v4_scr)

            def step(idx, carry, b=b, r=r, L=L, nq=nq, nk=nk, nblk=nblk, qs=qs, ks=ks, vs=vs):
                qrows, krows, off = _block_rows(idx, r, inter, S, L, nq, nk, nblk)
                case = off // HALF_WIN
                q2 = qs[qrows, :] * SCALE
                k2 = ks[krows, :].astype(bf16)
                v2 = vs[krows, :].astype(bf16)
                ms, accs = [], []
                for hh in range(2):
                    mine = first if hh == 0 else ~first
                    qh = jnp.where(mine, q2, 0.0).astype(bf16)
                    s = lax.dot_general(qh, k2, (((1,), (1,)), ((), ())), preferred_element_type=f32)
                    s = s + bias_scr[_bias_index(b, case, hh), 0:nq, 0:nk]
                    m = jnp.max(s, axis=-1, keepdims=True)
                    p = jnp.exp(s - m).astype(bf16)
                    vh = jnp.where(mine, v2, jnp.ones_like(v2))
                    ms.append(m)
                    accs.append(jnp.dot(p, vh, preferred_element_type=f32))
                m_scr[b, qrows, :] = jnp.where(first, ms[0], ms[1])
                a_scr[b, qrows, :] = jnp.where(first, accs[0], accs[1])
                l_scr[b, qrows, :] = jnp.where(first, accs[1], accs[0])
                return carry

            lax.fori_loop(0, S // nq, step, 0, unroll=min(ATTN_UNROLL, S // nq))

        n4 = S // QUAD
        ch = min(256, n4)
        nch = n4 // ch

        def merge(i, carry):
            rho, part = i // nch, i % nch
            sorted_rows = pl.ds(pl.multiple_of(rho * n4 + part * ch, 8), ch)
            token_rows = pl.ds(rho + QUAD * part * ch, ch, stride=QUAD)
            rows = (token_rows,) + (sorted_rows,) * (len(BRANCHES) - 1)
            ms = [m_scr[b, rows[b], :] for b in range(len(BRANCHES))]
            m = functools.reduce(jnp.maximum, ms)
            l = jnp.zeros((ch, PAIR), f32)
            acc = jnp.zeros((ch, PAIR), f32)
            for b in range(len(BRANCHES)):
                w = jnp.exp(ms[b] - m)
                l = l + w * pltpu.roll(l_scr[b, rows[b], :], HEAD_DIM, 1)
                acc = acc + w * a_scr[b, rows[b], :]
            o_ref[token_rows, :] = acc / l
            lse_ref[token_rows, :] = m + jnp.log(l)
            return carry

        lax.fori_loop(0, QUAD * nch, merge, 0, unroll=2)

    blk = lambda part: pl.BlockSpec((S, PAIR), lambda p: (0, part * npair + p))
    out = pl.BlockSpec((S, PAIR), lambda p: (0, p))
    return pl.pallas_call(
        body, name="attn_fwd",
        out_shape=(jax.ShapeDtypeStruct((S, C), f32), jax.ShapeDtypeStruct((S, C), f32)), grid=(npair,),
        in_specs=[blk(4), blk(5), blk(6), pl.BlockSpec((None,) + _bias_shape(S), lambda p: (p, 0, 0, 0))],
        out_specs=(out, out),
        scratch_shapes=[pltpu.VMEM((3, S, PAIR), f32)] * 3 + [pltpu.VMEM((S, PAIR), f32)] * 3,
        compiler_params=_params("parallel"),
    )(proj, proj, proj, bias)


def _attn_post(ycat, o, proj, g_attn, tr=512):
    S, C = o.shape
    tr = min(tr, S)

    def body(y_ref, o_ref, z_ref, g_ref, out_ref):
        del y_ref
        ov = o_ref[...]
        ra = lax.rsqrt(jnp.mean(ov * ov, axis=-1, keepdims=True) + EPS)
        out_ref[...] = ((ov * ra) * g_ref[...] * _silu(z_ref[...])).astype(bf16)

    return pl.pallas_call(
        body, name="attn_post", out_shape=jax.ShapeDtypeStruct(ycat.shape, ycat.dtype), grid=(S // tr,),
        in_specs=[HBM_SPEC, pl.BlockSpec((tr, C), lambda i: (i, 0)), pl.BlockSpec((tr, C), lambda i: (i, 7)),
                  pl.BlockSpec((1, C), lambda i: (0, 0))],
        out_specs=pl.BlockSpec((tr, C), lambda i: (i, 1)), input_output_aliases={0: 0},
        compiler_params=_params("arbitrary"),
    )(ycat, o, proj, g_attn)


def _residual_minus_target(x, target, dep, tr=512):
    S, D = x.shape
    tr = min(tr, S)

    def body(x_ref, t_ref, dep_ref, o_ref):
        o_ref[...] = x_ref[...] - t_ref[...]

    row = pl.BlockSpec((tr, D), lambda i: (i, 0))
    return pl.pallas_call(body, name="residual_minus_target", out_shape=jax.ShapeDtypeStruct((S, D), f32),
                          grid=(S // tr,), in_specs=[row, row, ANY_SPEC], out_specs=row,
                          compiler_params=_params("parallel"))(x, target, dep)


def _sandwich(y, x_minus_t, gate, g_post, tr=256):
    S, D = y.shape
    tr = min(tr, S)

    def body(y_ref, xt_ref, gate_ref, g_ref, dy_ref, dout_ref, sums_ref):
        i = pl.program_id(0)
        gate, g = gate_ref[...], g_ref[...]
        gg = gate * g
        yv = y_ref[...]
        rp = lax.rsqrt(jnp.mean(yv * yv, axis=-1, keepdims=True) + EPS)
        yhat = yv * rp
        err = xt_ref[...] + gg * yhat
        dout = err * (1.0 / D)
        dout_ref[...] = dout
        q = dout * yhat
        w = dout * gg
        dy_ref[...] = (rp * (w - yhat * jnp.sum(q * gg, axis=-1, keepdims=True) * (1.0 / D))).astype(bf16)
        loss = 0.5 * jnp.sum(jnp.mean(err * err, axis=-1, keepdims=True), axis=0, keepdims=True)
        q_sum = jnp.sum(q, axis=0, keepdims=True)
        row = lax.broadcasted_iota(jnp.int32, (8, D), 0)
        upd = jnp.where(row == 0, q_sum * g, jnp.where(row == 1, q_sum * gate, jnp.where(row == 2, loss, 0.0)))

        @pl.when(i == 0)
        def _():
            sums_ref[...] = upd

        @pl.when(i > 0)
        def _():
            sums_ref[...] += upd

    row = pl.BlockSpec((tr, D), lambda i: (i, 0))
    vec = pl.BlockSpec((1, D), lambda i: (0, 0))
    return pl.pallas_call(
        body, name="sandwich",
        out_shape=(jax.ShapeDtypeStruct((S, D), bf16), jax.ShapeDtypeStruct((S, D), f32), jax.ShapeDtypeStruct((8, D), f32)),
        grid=(S // tr,), in_specs=[row, row, vec, vec],
        out_specs=(row, row, pl.BlockSpec((8, D), lambda i: (0, 0))), compiler_params=_params("arbitrary"),
    )(y, x_minus_t, gate, g_post)


def _conv_bwd(proj, dycat, conv_w, conv_b, g_conv, dep, tr=256):
    S, C = proj.shape[0], proj.shape[1] // 8
    tr = min(tr, S)
    n = tr + 16

    def body(*refs):
        ins, (w_ref, cb_ref, g_ref, _, dp_ref, sums_ref) = refs[:15], refs[15:]
        i = pl.program_id(0)
        exists = _ext_rows(i, tr, S)
        u, bg, cg, zc, dyn = (jnp.concatenate([ins[3 * t][...], ins[3 * t + 1][...], ins[3 * t + 2][...]], axis=0)
                              for t in range(5))
        w = w_ref[...]
        t = jnp.where(exists, cg * u, 0.0)
        t_before, t_after = pltpu.roll(t, 1, 0), pltpu.roll(t, n - 1, 0)
        cv = w[0:1] * t_before + w[1:2] * t + w[2:3] * t_after + cb_ref[...]
        yc = bg * cv
        rc = lax.rsqrt(jnp.mean(yc * yc, axis=-1, keepdims=True) + EPS)
        yhat = yc * rc
        sz, dsz = _silu_and_slope(zc)
        wgt = dyn * g_ref[...] * sz
        dyc = rc * (wgt - yhat * jnp.mean(wgt * yhat, axis=-1, keepdims=True))
        dcv = jnp.where(exists, dyc * bg, 0.0)
        dt = w[0:1] * pltpu.roll(dcv, n - 1, 0) + w[1:2] * dcv + w[2:3] * pltpu.roll(dcv, 1, 0)
        mid = slice(8, tr + 8)
        dp_ref[:, 0:C] = (dt * cg)[mid].astype(bf16)
        dp_ref[:, C:2 * C] = (dyc * cv)[mid].astype(bf16)
        dp_ref[:, 2 * C:3 * C] = (dt * u)[mid].astype(bf16)
        dp_ref[:, 3 * C:4 * C] = (dyn * yhat * g_ref[...] * dsz)[mid].astype(bf16)
        colsum = lambda v: jnp.sum(v[mid], axis=0, keepdims=True)
        parts = [colsum(dyn * yhat * sz), colsum(dcv), colsum(dcv * t_before), colsum(dcv * t), colsum(dcv * t_after)]
        row = lax.broadcasted_iota(jnp.int32, (8, C), 0)
        upd = jnp.zeros((8, C), f32)
        for j, pj in enumerate(parts):
            upd = jnp.where(row == j, pj, upd)

        @pl.when(i == 0)
        def _():
            sums_ref[...] = upd

        @pl.when(i > 0)
        def _():
            sums_ref[...] += upd

    specs = []
    for col in range(4):
        specs += _halo_specs(tr, S, C, col)
    specs += _halo_specs(tr, S, C, 0)
    vec = pl.BlockSpec((1, C), lambda i: (0, 0))
    return pl.pallas_call(
        body, name="conv_bwd",
        out_shape=(jax.ShapeDtypeStruct((S, 4 * C), bf16), jax.ShapeDtypeStruct((8, C), f32)), grid=(S // tr,),
        in_specs=[*specs, pl.BlockSpec((8, C), lambda i: (0, 0)), vec, vec, ANY_SPEC],
        out_specs=(pl.BlockSpec((tr, 4 * C), lambda i: (i, 0)), pl.BlockSpec((8, C), lambda i: (0, 0))),
        compiler_params=_params("arbitrary"),
    )(*([proj] * 12), dycat, dycat, dycat, conv_w, conv_b, g_conv, dep)


def _attn_post_bwd(o, proj, dycat, g_attn, dep, tr=512):
    S, C = o.shape
    tr = min(tr, S)

    def body(o_ref, z_ref, dy_ref, g_ref, dep_ref, do_ref, dz_ref, sums_ref):
        i = pl.program_id(0)
        ov, zv, dyn = o_ref[...], z_ref[...], dy_ref[...]
        ra = lax.rsqrt(jnp.mean(ov * ov, axis=-1, keepdims=True) + EPS)
        ohat = ov * ra
        sz, dsz = _silu_and_slope(zv)
        wgt = dyn * g_ref[...] * sz
        do_ref[...] = ra * (wgt - ohat * jnp.mean(wgt * ohat, axis=-1, keepdims=True))
        dz_ref[...] = (dyn * ohat * g_ref[...] * dsz).astype(bf16)
        row = lax.broadcasted_iota(jnp.int32, (8, C), 0)
        upd = jnp.where(row == 0, jnp.sum(dyn * ohat * sz, axis=0, keepdims=True), 0.0)

        @pl.when(i == 0)
        def _():
            sums_ref[...] = upd

        @pl.when(i > 0)
        def _():
            sums_ref[...] += upd

    return pl.pallas_call(
        body, name="attn_post_bwd",
        out_shape=(jax.ShapeDtypeStruct((S, C), f32), jax.ShapeDtypeStruct((4, S, C), bf16),
                   jax.ShapeDtypeStruct((8, C), f32)),
        grid=(S // tr,),
        in_specs=[pl.BlockSpec((tr, C), lambda i: (i, 0)), pl.BlockSpec((tr, C), lambda i: (i, 7)),
                  pl.BlockSpec((tr, C), lambda i: (i, 1)), pl.BlockSpec((1, C), lambda i: (0, 0)), ANY_SPEC],
        out_specs=(pl.BlockSpec((tr, C), lambda i: (i, 0)), pl.BlockSpec((None, tr, C), lambda i: (3, i, 0)),
                   pl.BlockSpec((8, C), lambda i: (0, 0))),
        compiler_params=_params("arbitrary"),
    )(o, proj, dycat, g_attn, dep)


def _attn_bwd(proj, o, do, lse, bias, dqkvz, dep):
    S, C = o.shape
    npair = C // PAIR

    def body(q_ref, k_ref, v_ref, o_ref, do_ref, lse_ref, bias_scr, old_ref, dep_ref, dqkv_ref,
             acc_scr, dl_scr, quad_scr):
        lane = lax.broadcasted_iota(jnp.int32, (1, PAIR), 1)
        first = lane < HEAD_DIM
        ch = min(256, S)

        def prep(i, carry):
            rows = pl.ds(pl.multiple_of(i * ch, 8), ch)
            prod = do_ref[rows, :] * o_ref[rows, :]
            d0 = jnp.sum(jnp.where(first, prod, 0.0), axis=-1, keepdims=True)
            d1 = jnp.sum(jnp.where(first, 0.0, prod), axis=-1, keepdims=True)
            dl_scr[rows, :] = jnp.where(first, d0, d1)
            zero = jnp.zeros((ch, PAIR), f32)
            for order in range(2):
                for t in range(3):
                    acc_scr[order, t, rows, :] = zero
            return carry

        lax.fori_loop(0, S // ch, prep, 0, unroll=2)
        token_srcs = (q_ref, k_ref, v_ref, do_ref, lse_ref, dl_scr)
        for j, src in enumerate(token_srcs):
            _to_quad(quad_scr.at[j], src, S)

        for b, (_, r, inter) in enumerate(BRANCHES):
            L, nq, nk, nblk = _branch_geometry(S, r, inter)
            order = 0 if r == 1 else 1
            srcs = token_srcs if r == 1 else tuple(quad_scr.at[j] for j in range(6))

            def step(idx, carry, b=b, r=r, L=L, nq=nq, nk=nk, nblk=nblk, order=order, srcs=srcs):
                qs, ks, vs, dos, lses, dls = srcs
                dq_scr, dk_scr, dv_scr = (acc_scr.at[order, t] for t in range(3))
                qrows, krows, off = _block_rows(idx, r, inter, S, L, nq, nk, nblk)
                case = off // HALF_WIN
                q2 = qs[qrows, :] * SCALE
                k2 = ks[krows, :].astype(bf16)
                v2 = vs[krows, :].astype(bf16)
                do2 = dos[qrows, :]
                lse2 = lses[qrows, :]
                dl2 = dls[qrows, :]
                dq2 = jnp.zeros((nq, PAIR), f32)
                dk2 = jnp.zeros((nk, PAIR), f32)
                dv2 = jnp.zeros((nk, PAIR), f32)
                for hh in range(2):
                    mine = first if hh == 0 else ~first
                    lo = hh * HEAD_DIM
                    qh = jnp.where(mine, q2, 0.0).astype(bf16)
                    doh = jnp.where(mine, do2, 0.0).astype(bf16)
                    s = lax.dot_general(qh, k2, (((1,), (1,)), ((), ())), preferred_element_type=f32)
                    s = s + bias_scr[_bias_index(b, case, hh), 0:nq, 0:nk]
                    p = jnp.exp(s - lse2[:, lo:lo + 1])
                    dv2 = dv2 + lax.dot_general(p.astype(bf16), doh, (((0,), (0,)), ((), ())), preferred_element_type=f32)
                    dp = lax.dot_general(doh, v2, (((1,), (1,)), ((), ())), preferred_element_type=f32)
                    ds = (p * (dp - dl2[:, lo:lo + 1])).astype(bf16)
                    dq2 = dq2 + jnp.where(mine, jnp.dot(ds, k2, preferred_element_type=f32), 0.0)
                    dk2 = dk2 + lax.dot_general(ds, qh, (((0,), (0,)), ((), ())), preferred_element_type=f32)
                dq_scr[qrows, :] = dq_scr[qrows, :] + dq2
                dk_scr[krows, :] = dk_scr[krows, :] + dk2
                dv_scr[krows, :] = dv_scr[krows, :] + dv2
                return carry

            lax.fori_loop(0, S // nq, step, 0, unroll=min(ATTN_UNROLL, S // nq))

        n4 = S // QUAD
        for t in range(3):
            for rho in range(QUAD):
                token_rows = pl.ds(rho, n4, stride=QUAD)
                acc_scr[0, t, token_rows, :] = acc_scr[0, t, token_rows, :] + acc_scr[1, t, pl.ds(rho * n4, n4), :]
        dqkv_ref[0] = (acc_scr[0, 0] * SCALE).astype(bf16)
        dqkv_ref[1] = acc_scr[0, 1].astype(bf16)
        dqkv_ref[2] = acc_scr[0, 2].astype(bf16)

    blk = lambda part: pl.BlockSpec((S, PAIR), lambda p: (0, part * npair + p))
    own = pl.BlockSpec((S, PAIR), lambda p: (0, p))
    return pl.pallas_call(
        body, name="attn_bwd", out_shape=jax.ShapeDtypeStruct(dqkvz.shape, dqkvz.dtype), grid=(npair,),
        in_specs=[blk(4), blk(5), blk(6), own, own, own,
                  pl.BlockSpec((None,) + _bias_shape(S), lambda p: (p, 0, 0, 0)), ANY_SPEC, ANY_SPEC],
        out_specs=pl.BlockSpec((3, S, PAIR), lambda p: (0, 0, p)), input_output_aliases={7: 0},
        scratch_shapes=[pltpu.VMEM((2, 3, S, PAIR), f32), pltpu.VMEM((S, PAIR), f32), pltpu.VMEM((6, S, PAIR), f32)],
        compiler_params=_params("arbitrary"),
    )(proj, proj, proj, o, do, lse, bias, dqkvz, dep)


def _prenorm_bwd(dh, x, dout, scale, g_pre, tr=256):
    S, D = x.shape
    tr = min(tr, S)

    def body(dh_ref, x_ref, dout_ref, sc_ref, g_ref, gx_ref, sums_ref):
        i = pl.program_id(0)
        xv, dhv = x_ref[...], dh_ref[...]
        r = lax.rsqrt(jnp.mean(xv * xv, axis=-1, keepdims=True) + EPS)
        xn = xv * r
        dxn = dhv * (g_ref[...] * (1.0 + sc_ref[...]))
        gx_ref[...] = dout_ref[...] + r * (dxn - xn * jnp.mean(dxn * xn, axis=-1, keepdims=True))
        dhx = dhv * xn
        row = lax.broadcasted_iota(jnp.int32, (8, D), 0)
        upd = jnp.where(row == 0, jnp.sum(dhv, axis=0, keepdims=True),
                        jnp.where(row == 1, jnp.sum(dhx, axis=0, keepdims=True) * g_ref[...],
                                  jnp.where(row == 2, jnp.sum(dhx, axis=0, keepdims=True) * (1.0 + sc_ref[...]), 0.0)))

        @pl.when(i == 0)
        def _():
            sums_ref[...] = upd

        @pl.when(i > 0)
        def _():
            sums_ref[...] += upd

    row = pl.BlockSpec((tr, D), lambda i: (i, 0))
    vec = pl.BlockSpec((1, D), lambda i: (0, 0))
    return pl.pallas_call(
        body, name="prenorm_bwd",
        out_shape=(jax.ShapeDtypeStruct((S, D), f32), jax.ShapeDtypeStruct((8, D), f32)), grid=(S // tr,),
        in_specs=[row, row, row, vec, vec], out_specs=(row, pl.BlockSpec((8, D), lambda i: (0, 0))),
        compiler_params=_params("arbitrary"),
    )(dh, x, dout, scale, g_pre)


def _adamw(w, g, m, v):
    m = ADAM_B1 * m + (1.0 - ADAM_B1) * g
    v = ADAM_B2 * v + (1.0 - ADAM_B2) * (g * g)
    m_hat = m / (1.0 - ADAM_B1 ** ADAM_STEP)
    v_hat = v / (1.0 - ADAM_B2 ** ADAM_STEP)
    delta = -ADAM_LR * (m_hat / (jnp.sqrt(v_hat) + ADAM_EPS) + ADAM_WD * w)
    return delta, m, v


def _pack_rows(arrays, plan, last):
    nrows = sum(arrays[t].shape[1] for t, _ in plan) // LANES

    def body(*refs):
        srcs, o_ref = refs[:-1], refs[-1]
        at = 0
        for t, r in plan:
            for j in range(arrays[t].shape[1] // LANES):
                o_ref[at:at + 1, :] = srcs[t][r:r + 1, j * LANES:(j + 1) * LANES]
                at += 1
        t, r = last
        o_ref[nrows:nrows + 8, :] = jnp.broadcast_to(srcs[t][r:r + 1, 0:LANES], (8, LANES))

    return pl.pallas_call(body, name="pack_small", out_shape=jax.ShapeDtypeStruct((nrows + 8, LANES), f32),
                          compiler_params=_params())(*arrays)


def _sum_rows(parts, dep):
    P = parts.shape[1]

    def body(p_ref, dep_ref, o_ref):
        acc = p_ref[0:1, :]
        for j in range(1, NDEV):
            acc = acc + p_ref[j:j + 1, :]
        o_ref[...] = jnp.broadcast_to(acc, (8, P))

    vmem = pl.BlockSpec(memory_space=pltpu.VMEM)
    return pl.pallas_call(body, name="sum_small", out_shape=jax.ShapeDtypeStruct((8, P), f32),
                          in_specs=[vmem, ANY_SPEC], out_specs=vmem, compiler_params=_params())(parts, dep)


def _adamw_small(tot, params):
    given = [p[3] for p in params if not isinstance(p[3], int)]

    def body(tot_ref, *refs):
        given_refs = list(refs[:len(given)])
        ins = refs[len(given):len(given) + 3 * len(params)]
        outs = refs[len(given) + 3 * len(params):]
        for t, (w, _, _, where) in enumerate(params):
            w_ref, m_ref, v_ref = ins[3 * t:3 * t + 3]
            g = tot_ref[0:1, where:where + w.size] if isinstance(where, int) else given_refs.pop(0)[...]
            outs[4 * t][...] = g
            outs[4 * t + 1][...], outs[4 * t + 2][...], outs[4 * t + 3][...] = _adamw(w_ref[...], g, m_ref[...], v_ref[...])

    out_shape = tuple(jax.ShapeDtypeStruct(p[0].shape, f32) for p in params for _ in range(4))
    res = pl.pallas_call(body, name="adamw_small", out_shape=out_shape, compiler_params=_params())(
        tot, *given, *[a for p in params for a in p[:3]])
    return [res[4 * t:4 * t + 4] for t in range(len(params))]


def _adamw_sharded(name, parts, sums_a, sums_b, pick, w, m, v, rows=None, prev=None, tr=128):
    R, Cc = w.shape
    r0, nr = rows or (0, R)
    tr = math.gcd(tr, r0, nr)
    n, b0 = parts.shape[0], r0 // tr

    def body(pick_ref, p_ref, a_ref, b_ref, w_ref, m_ref, v_ref, *rest):
        g_ref, d_ref, nm_ref, nv_ref = rest[-4:]
        g = jnp.where(pick_ref[0] == 1, b_ref[...], a_ref[...]).astype(f32)
        for j in range(n):
            g = g + p_ref[j].astype(f32)
        g_ref[...] = g
        d_ref[...], nm_ref[...], nv_ref[...] = _adamw(w_ref[...], g, m_ref[...], v_ref[...])

    row = pl.BlockSpec((tr, Cc), lambda i, pick: (i + b0, 0))
    mine = pl.BlockSpec((None, tr, Cc), lambda i, pick: (pick[1], i + b0, 0))
    out = jax.ShapeDtypeStruct((R, Cc), f32)
    prev = list(prev or [])
    grid_spec = pltpu.PrefetchScalarGridSpec(
        num_scalar_prefetch=1, grid=(nr // tr,),
        in_specs=[pl.BlockSpec((n, tr, Cc), lambda i, pick: (0, i + b0, 0)), mine, mine, row, row, row]
        + [ANY_SPEC] * len(prev),
        out_specs=(row, row, row, row))
    return pl.pallas_call(
        body, name=name, out_shape=(out, out, out, out), grid_spec=grid_spec,
        input_output_aliases={7 + t: t for t in range(len(prev))}, compiler_params=_params("arbitrary"),
    )(pick, parts, sums_a, sums_b, w, m, v, *prev)


def _adamw_ada(c_t, dmod_cols, w, m, v, dep, tr=512):
    D, W = w.shape
    tr = min(tr, D)

    def body(c_ref, dm_ref, w_ref, m_ref, v_ref, dep_ref, g_ref, d_ref, nm_ref, nv_ref):
        g = lax.dot_general(c_ref[...], dm_ref[...], (((1,), (0,)), ((), ())), preferred_element_type=f32,
                            precision=lax.Precision.HIGHEST)
        g_ref[...] = g
        d_ref[...], nm_ref[...], nv_ref[...] = _adamw(w_ref[...], g, m_ref[...], v_ref[...])

    row = pl.BlockSpec((tr, W), lambda i: (i, 0))
    out = jax.ShapeDtypeStruct((D, W), f32)
    return pl.pallas_call(
        body, name="adamw_ada", out_shape=(out, out, out, out), grid=(D // tr,),
        in_specs=[pl.BlockSpec((tr, NDEV), lambda i: (i, 0)), pl.BlockSpec((NDEV, W), lambda i: (0, 0)), row, row, row,
                  ANY_SPEC],
        out_specs=(row, row, row, row), compiler_params=_params("parallel"),
    )(c_t, dmod_cols, w, m, v, dep)


def kernel(x, c, w_ada, b_ada, g_pre, w_in, conv_w, conv_b, g_conv, g_attn, w_out, g_post, loss_target, m_w_ada, m_b_ada, m_g_pre, m_w_in, m_conv_w, m_conv_b, m_g_conv, m_g_attn, m_w_out, m_g_post, v_w_ada, v_b_ada, v_g_pre, v_w_in, v_conv_w, v_conv_b, v_g_conv, v_g_attn, v_w_out, v_g_post):
    S, D = x.shape[1], x.shape[2]
    C = D // 2
    W = w_ada.shape[2]
    CW = conv_w.shape[2]
    me = 4 * lax.axis_index("x") + 2 * lax.axis_index("y") + lax.axis_index("c")
    x2, tgt = x[0], loss_target[0]
    w_ada2, w_in2, w_out2 = w_ada[0], w_in[0], w_out[0]

    R = D // NDEV
    core = lax.axis_index("c").astype(jnp.int32).reshape(1)

    cw_slab = jnp.zeros((8, CW), f32).at[:3].set(conv_w[0])
    b_cols = lax.dynamic_slice_in_dim(b_ada, me * W, W, axis=1)
    mod_slabs, c_blocks, cw_g = _ada_exchange(c.reshape(D // 128, 128), cw_slab, w_ada2, b_cols)
    c_all = c_blocks.reshape(NDEV, D)
    conv_w_full = jnp.transpose(cw_g, (1, 0, 2)).reshape(8, C)
    mod = mod_slabs[:, 0, :].reshape(1, 3 * D)
    shift, scale, gate = mod[:, :D], mod[:, D:2 * D], mod[:, 2 * D:]

    land_i = lax.dynamic_update_slice(lax.empty((NDEV, D, C), bf16), w_in2.astype(bf16)[None], (me, 0, 0))
    land_o = lax.dynamic_update_slice(lax.empty((NDEV, R, D), bf16), w_out2.astype(bf16)[None], (me, 0, 0))
    wi_send, wi_recv, land_i, w_token = _w_in_start(land_i, [mod_slabs])

    me_arr = me.astype(jnp.int32).reshape(1)
    h = _prenorm(x2, scale, shift, g_pre, w_token)
    land_i = _w_in_sibling(land_i, wi_recv, after=[h])
    proj = _in_proj_part("in_proj_a", h, land_i, None, me_arr, 0, 1, 2)
    x_minus_t = _residual_minus_target(x2, tgt, proj)
    bias = _bias_tiles(_head_slopes(C // HEAD_DIM), S, x_minus_t)

    def landing(rows, cols):
        return lax.dynamic_update_slice(lax.empty((NCHIP, rows, cols), bf16), jnp.zeros((1, rows, cols), bf16),
                                        (me // 2, 0, 0))

    land_go, land_gi = landing(R, D), landing(D, C)
    fi_send, fi_recv, land_i = _w_in_relay(land_i, wi_recv, after=[proj, bias, land_go, land_gi])
    proj = _in_proj_part("in_proj_b", h, land_i, proj, me_arr, 2, 2, 2)
    land_i = _w_in_forwarded(land_i, fi_recv, after=[proj])
    proj = _in_proj_part("in_proj_c", h, land_i, proj, me_arr, 3, 2, 2)
    (di_send, di_recv, wo_send, wo_recv), land_i, land_o = _w_in_diag(land_i, land_o, fi_recv, after=[proj])
    proj = _in_proj_part("in_proj_d", h, land_i, proj, me_arr, 6, 1, 1)
    win_g = _w_in_finish(land_i, wi_send, fi_send, di_send, di_recv, after=[proj])
    proj = _in_proj_part("in_proj_e", h, win_g, proj, me_arr, 7, 1, 1)
    o, lse = _attn_fwd(proj, bias)
    (fo_send, fo_recv), (land_o,), fo_token = _weights_forward("w_out_forward", land_o, wo_recv, after=[o])
    ycat = _conv_fwd(proj, conv_w_full, conv_b, g_conv, fo_token)
    ycat = _attn_post(ycat, o, proj, g_attn)
    wout_g = _weights_wait("w_out_wait", land_o, wo_send, wo_recv, fo_send, fo_recv, after=[ycat])
    wout_full = wout_g.reshape(D, D)
    y = _matmul(ycat, wout_full, name="out_proj", out_dtype=f32)
    dy, dout, post_sums = _sandwich(y, x_minus_t, gate, g_post)

    gw_out = _matmul(ycat, dy, name="out_proj_dw", out_dtype=bf16, ta=True).reshape(NDEV, R, D)
    dycat = _matmul(dy, wout_full, name="out_proj_dx", out_dtype=f32, tb=True)
    dpc, conv_sums = _conv_bwd(proj, dycat, conv_w_full, conv_b, g_conv, gw_out)
    gw_c = _matmul(h, dpc, name="in_proj_dw_conv", out_dtype=bf16, ta=True, out_slots=4)
    first_pairs, p1_token = _pairs_start("g_first_pair_start", [gw_out, gw_c])
    do, dpa, attn_sums = _attn_post_bwd(o, proj, dycat, g_attn, p1_token)
    (gw_out, pair_o), (gw_c, pair_c) = _pairs_wait("g_first_pair_wait", first_pairs, after=[do])
    sum_o = _pair_sum("g_out_pair_sum", gw_out, pair_o, core)
    sum_c = _pair_sum("g_conv_pair_sum", gw_c, pair_c, core)
    ((co_send, co_recv, sum_o, land_go), (cc_send, cc_recv, sum_c, land_gi)), cc_token = _chips_start(
        "g_first_chip_start", [(sum_o, land_go, 0), (sum_c, land_gi, 0)])
    dpa = _attn_bwd(proj, o, do, lse, bias, dpa, cc_token)
    gw_a = _matmul(h, dpa, name="in_proj_dw_attn", out_dtype=bf16, ta=True, b_slots=True, out_slots=4)
    pa_send, pa_recv, gw_a, pair_a, pa_token = _pair_start("g_attn_pair_start", gw_a)
    sum_o, land_go = _chip_wait("g_out_chip_wait", sum_o, land_go, co_send, co_recv, 0, after=[pa_token])
    pick_out = jnp.stack([jnp.int32(0), me // 2]).astype(jnp.int32)
    g_w_out, d_w_out, nm_w_out, nv_w_out = _adamw_sharded(
        "adamw_w_out", land_go, sum_o, sum_o, pick_out, w_out2, m_w_out[0], v_w_out[0])
    gw_a, pair_a = _pair_wait("g_attn_pair_wait", gw_a, pair_a, pa_send, pa_recv, after=[g_w_out])
    sum_a = _pair_sum("g_attn_pair_sum", gw_a, pair_a, core)
    part_a, part_b = (0, 3 * D // 4), (3 * D // 4, D // 4)
    ca_send, ca_recv, sum_a, land_gi, ca_token = _chip_start("g_attn_chip_start_a", sum_a, land_gi, 4, part_a)
    dh = _matmul_slabs_t(dpc, dpa, win_g, name="in_proj_dx", dep=ca_token)
    grad_x, pre_sums = _prenorm_bwd(dh, x2, dout, scale, g_pre)

    pre, post, conv, attn = range(4)
    small = _pack_rows([pre_sums, post_sums, conv_sums, attn_sums],
                       [(pre, 0), (pre, 1), (post, 0),
                        (pre, 2), (post, 1),
                        (conv, 2), (conv, 3), (conv, 4),
                        (conv, 1), (conv, 0), (attn, 0)],
                       last=(post, 2))
    gs_send, gs_recv, small, small_all, gs_token = _gather_start("gather_small_start", small)

    cb_send, cb_recv, sum_a, land_gi, cb_token = _chip_start("g_attn_chip_start_b", sum_a, land_gi, 4, part_b,
                                                             after=[gs_token])

    pick_in = jnp.stack([me // 4, (me % 4) // 2]).astype(jnp.int32)
    sum_c, land_gi = _chip_wait("g_conv_chip_wait", sum_c, land_gi, cc_send, cc_recv, 0, after=[cb_token])
    sum_a, land_gi = _chip_wait("g_attn_chip_wait_a", sum_a, land_gi, ca_send, ca_recv, 4, [cb_token], part_a)
    first = _adamw_sharded("adamw_w_in_a", land_gi, sum_c, sum_a, pick_in, w_in2, m_w_in[0], v_w_in[0], rows=part_a,
                           tr=256)

    small_all = _gather_wait("gather_small_wait", small, small_all, gs_send, gs_recv, after=[first[0]])
    small_all = small_all.reshape(NDEV, small.size)
    tot = _sum_rows(small_all, gs_token)
    loss = tot[0, 8 * D]
    taps_first = lambda a: jnp.transpose(a, (1, 0, 2))
    g_conv_w = lax.dynamic_slice_in_dim(tot[0:1, 5 * D:5 * D + 3 * C].reshape(3, 1, C), me * CW, CW, axis=2)
    ((g_b_ada, d_b_ada, nm_b_ada, nv_b_ada), (g_g_pre, d_g_pre, nm_g_pre, nv_g_pre),
     (g_g_post, d_g_post, nm_g_post, nv_g_post), conv_w_results,
     (g_conv_b, d_conv_b, nm_conv_b, nv_conv_b), (g_g_conv, d_g_conv, nm_g_conv, nv_g_conv),
     (g_g_attn, d_g_attn, nm_g_attn, nv_g_attn)) = _adamw_small(tot, [
         (b_ada, m_b_ada, v_b_ada, 0), (g_pre, m_g_pre, v_g_pre, 3 * D), (g_post, m_g_post, v_g_post, 4 * D),
         (taps_first(conv_w), taps_first(m_conv_w), taps_first(v_conv_w), g_conv_w),
         (conv_b, m_conv_b, v_conv_b, 5 * D + 3 * C),
         (g_conv, m_g_conv, v_g_conv, 5 * D + 4 * C), (g_attn, m_g_attn, v_g_attn, 5 * D + 5 * C)])
    g_conv_w, d_conv_w, nm_conv_w, nv_conv_w = (taps_first(a) for a in conv_w_results)

    dmod_cols = lax.dynamic_slice_in_dim(small_all[:, :3 * D], me * W, W, axis=1)
    g_w_ada, d_w_ada, nm_w_ada, nv_w_ada = _adamw_ada(c_all.T, dmod_cols, w_ada2, m_w_ada[0], v_w_ada[0], gs_token)

    sum_a, land_gi = _chip_wait("g_attn_chip_wait_b", sum_a, land_gi, cb_send, cb_recv, 4, [g_w_ada], part_b)
    g_w_in, d_w_in, nm_w_in, nv_w_in = _adamw_sharded(
        "adamw_w_in_b", land_gi, sum_c, sum_a, pick_in, w_in2, m_w_in[0], v_w_in[0], rows=part_b, prev=first, tr=256)

    return (loss, grad_x[None],
            g_w_ada[None], g_b_ada, g_g_pre, g_w_in[None], g_conv_w, g_conv_b, g_g_conv, g_g_attn, g_w_out[None], g_g_post,
            d_w_ada[None], d_b_ada, d_g_pre, d_w_in[None], d_conv_w, d_conv_b, d_g_conv, d_g_attn, d_w_out[None], d_g_post,
            nm_w_ada[None], nm_b_ada, nm_g_pre, nm_w_in[None], nm_conv_w, nm_conv_b, nm_g_conv, nm_g_attn, nm_w_out[None], nm_g_post,
            nv_w_ada[None], nv_b_ada, nv_g_pre, nv_w_in[None], nv_conv_w, nv_conv_b, nv_g_conv, nv_g_attn, nv_w_out[None], nv_g_post)
```

```python
import functools
import math

import jax
import jax.numpy as jnp
from jax import lax
from jax.experimental import pallas as pl
from jax.experimental.pallas import tpu as pltpu

f32 = jnp.float32
bf16 = jnp.bfloat16

NDEV = 8
HEAD_DIM = 64
PAIR = 2 * HEAD_DIM
BRANCHES = ((128, 1, 1), (512, 4, 1), (2048, 16, 2))
HALF_WIN = 64
EPS = 1e-6
NEG_INF = -1e30
ADAM_LR, ADAM_B1, ADAM_B2, ADAM_EPS, ADAM_WD, ADAM_STEP = 0.001, 0.9, 0.999, 1e-08, 0.01, 10
MESH = pl.DeviceIdType.MESH
VMEM_LIMIT = 56 * 1024 * 1024
HBM_SPEC = pl.BlockSpec(memory_space=pltpu.HBM)
ANY_SPEC = pl.BlockSpec(memory_space=pl.ANY)
SEM_SPEC = pl.BlockSpec(memory_space=pltpu.SEMAPHORE)


def _params(*sem):
    return pltpu.CompilerParams(dimension_semantics=sem or None, vmem_limit_bytes=VMEM_LIMIT)


def _silu(z):
    return z * jax.nn.sigmoid(z)


def _silu_and_slope(z):
    s = jax.nn.sigmoid(z)
    return z * s, s * (1.0 + z * (1.0 - s))


def _my_place():
    x, y, c = lax.axis_index("x"), lax.axis_index("y"), lax.axis_index("c")
    return x, y, c, 4 * x + 2 * y + c


def _peer(x, y, c, k):
    px, py, pc = x ^ (k >> 2 & 1), y ^ (k >> 1 & 1), c ^ (k & 1)
    return (px, py, pc), 4 * px + 2 * py + pc


def _comm_call(name, arrays, sems, new_sems, body, after=(), token=False):
    na, ns, nn, nf = len(arrays), len(sems), len(new_sems), len(after)

    def kern(*refs):
        ins, outs = refs[:na + ns + nf], refs[na + ns + nf:]
        body(ins[:na], ins[na:na + ns], outs[:nn])
        if token:
            outs[nn + na][...] = jnp.zeros((8, 128), f32)

    out_shape = ([pltpu.SemaphoreType.DMA(s) for s in new_sems] + [pltpu.HBM(a.shape, a.dtype) for a in arrays]
                 + ([jax.ShapeDtypeStruct((8, 128), f32)] if token else []))
    out_specs = [SEM_SPEC] * nn + [HBM_SPEC] * na + ([pl.BlockSpec(memory_space=pltpu.VMEM)] if token else [])
    res = pl.pallas_call(
        kern, name=name, out_shape=tuple(out_shape),
        in_specs=[HBM_SPEC] * na + [SEM_SPEC] * ns + [ANY_SPEC] * nf, out_specs=tuple(out_specs),
        input_output_aliases={t: nn + t for t in range(na)},
        compiler_params=pltpu.CompilerParams(has_side_effects=pltpu.SideEffectType.DATAFLOW_SIDE_EFFECTING),
    )(*[pltpu.with_memory_space_constraint(a, pltpu.HBM) for a in arrays], *sems, *after)
    return list(res[:nn]), list(res[nn:nn + na]), (res[nn + na] if token else None)


def _remote(src, dst, send_sem, recv_sem, device):
    return pltpu.make_async_remote_copy(src_ref=src, dst_ref=dst, send_sem=send_sem, recv_sem=recv_sem,
                                        device_id=device, device_id_type=MESH)


def _gather_start(name, src):
    def body(a, s, new):
        (src, land), (send, recv) = a, new
        x, y, c, me = _my_place()
        pltpu.make_async_copy(src, land.at[me], recv.at[0]).start()
        for k in range(1, NDEV):
            peer, _ = _peer(x, y, c, k)
            _remote(src, land.at[me], send.at[k], recv.at[k], peer).start()

    land = lax.empty((NDEV,) + src.shape, src.dtype)
    (send, recv), (src, land), token = _comm_call(name, [src, land], [], [(NDEV,), (NDEV,)], body, token=True)
    return send, recv, src, land, token


def _gather_wait(name, src, land, send, recv, after):
    def body(a, s, new):
        (src, land), (send, recv) = a, s
        x, y, c, me = _my_place()
        pltpu.make_async_copy(src, land.at[me], recv.at[0]).wait()
        for k in range(1, NDEV):
            peer, slot = _peer(x, y, c, k)
            _remote(src, land.at[slot], send.at[k], recv.at[k], peer).wait_recv()
        for k in range(1, NDEV):
            peer, _ = _peer(x, y, c, k)
            _remote(src, land.at[me], send.at[k], recv.at[k], peer).wait_send()

    return _comm_call(name, [src, land], [send, recv], [], body, after=after)[1][1]


SAME_CORE = (2, 4, 6)
VIA_SIBLING = (3, 5, 7)


def _weights_forward(name, land, recv, after):
    def body(a, s, new):
        (land,), (recv,), (fsend, frecv) = a, s, new
        x, y, c, me = _my_place()
        sibling, _ = _peer(x, y, c, 1)
        for k in SAME_CORE:
            peer, slot = _peer(x, y, c, k)
            _remote(land.at[slot], land.at[slot], fsend.at[k], recv.at[k], peer).wait_recv()
            _remote(land.at[slot], land.at[slot], fsend.at[k], frecv.at[k ^ 1], sibling).start()

    return _comm_call(name, [land], [recv], [(NDEV,), (NDEV,)], body, after=after)


def _weights_wait(name, land, send, recv, fsend, frecv, after):
    def body(a, s, new):
        (land,), (send, recv, fsend, frecv) = a, s
        x, y, c, me = _my_place()
        sibling, sib_slot = _peer(x, y, c, 1)
        _remote(land.at[sib_slot], land.at[sib_slot], send.at[1], recv.at[1], sibling).wait_recv()
        for k in VIA_SIBLING:
            _, slot = _peer(x, y, c, k)
            _remote(land.at[slot], land.at[slot], fsend.at[k ^ 1], frecv.at[k], sibling).wait_recv()
        for k in (1,) + SAME_CORE:
            peer, _ = _peer(x, y, c, k)
            _remote(land.at[me], land.at[me], send.at[k], recv.at[k], peer).wait_send()
        for k in SAME_CORE:
            _, slot = _peer(x, y, c, k)
            _remote(land.at[slot], land.at[slot], fsend.at[k], frecv.at[k ^ 1], sibling).wait_send()

    return _comm_call(name, [land], [send, recv, fsend, frecv], [], body, after=after)[1][0]


def _diag_relay(x, y, c):
    slot = 4 * (x ^ (1 - c)) + 2 * (y ^ c) + c
    return slot, (x ^ c, y ^ (1 - c), c)


def _w_in_start(land, after):
    def body(a, s, new):
        (land,), (send, recv) = a, new
        x, y, c, me = _my_place()
        for k in (1, 2, 4):
            peer, _ = _peer(x, y, c, k)
            _remote(land.at[me], land.at[me], send.at[k], recv.at[k], peer).start()

    (send, recv), (land,), token = _comm_call("w_in_start", [land], [], [(NDEV,), (NDEV,)], body, after=after, token=True)
    return send, recv, land, token


def _w_in_sibling(land, recv, after):
    def body(a, s, new):
        (land,), (recv,) = a, s
        x, y, c, me = _my_place()
        sibling, slot = _peer(x, y, c, 1)
        _remote(land.at[slot], land.at[slot], recv.at[1], recv.at[1], sibling).wait_recv()

    return _comm_call("w_in_sibling", [land], [recv], [], body, after=after)[1][0]


def _w_in_relay(land, recv, after):
    def body(a, s, new):
        (land,), (recv,), (fsend, frecv) = a, s, new
        x, y, c, me = _my_place()
        sibling, _ = _peer(x, y, c, 1)
        for k in (2, 4):
            peer, slot = _peer(x, y, c, k)
            _remote(land.at[slot], land.at[slot], fsend.at[k], recv.at[k], peer).wait_recv()
        slot, target = _diag_relay(x, y, c)
        _remote(land.at[slot], land.at[slot], fsend.at[6], frecv.at[6], target).start()
        for k in (2, 4):
            _, slot = _peer(x, y, c, k)
            _remote(land.at[slot], land.at[slot], fsend.at[k], frecv.at[k ^ 1], sibling).start()

    (fsend, frecv), (land,), _ = _comm_call("w_in_relay", [land], [recv], [(NDEV,), (NDEV,)], body, after=after)
    return fsend, frecv, land


def _w_in_forwarded(land, frecv, after):
    def body(a, s, new):
        (land,), (frecv,) = a, s
        x, y, c, me = _my_place()
        sibling, _ = _peer(x, y, c, 1)
        for k in (3, 5):
            _, slot = _peer(x, y, c, k)
            _remote(land.at[slot], land.at[slot], frecv.at[k], frecv.at[k], sibling).wait_recv()

    return _comm_call("w_in_forwarded", [land], [frecv], [], body, after=after)[1][0]


def _w_in_diag(land, land_o, frecv, after):
    def body(a, s, new):
        (land, land_o), (frecv,), (dsend, drecv, osend, orecv) = a, s, new
        x, y, c, me = _my_place()
        sibling, _ = _peer(x, y, c, 1)
        peer, slot = _peer(x, y, c, 6)
        _remote(land.at[slot], land.at[slot], dsend.at[6], frecv.at[6], peer).wait_recv()
        _remote(land.at[slot], land.at[slot], dsend.at[6], drecv.at[7], sibling).start()
        for k in (1,) + SAME_CORE:
            peer, _ = _peer(x, y, c, k)
            _remote(land_o.at[me], land_o.at[me], osend.at[k], orecv.at[k], peer).start()

    sems, (land, land_o), _ = _comm_call("w_in_diag", [land, land_o], [frecv], [(NDEV,)] * 4, body, after=after)
    return sems, land, land_o


def _w_in_finish(land, send, fsend, dsend, drecv, after):
    def body(a, s, new):
        (land,), (send, fsend, dsend, drecv) = a, s
        x, y, c, me = _my_place()
        sibling, _ = _peer(x, y, c, 1)
        _, slot = _peer(x, y, c, 7)
        _remote(land.at[slot], land.at[slot], dsend.at[6], drecv.at[7], sibling).wait_recv()
        for k in (1, 2, 4):
            peer, _ = _peer(x, y, c, k)
            _remote(land.at[me], land.at[me], send.at[k], send.at[k], peer).wait_send()
        for k in (2, 4, 6):
            _, slot = _peer(x, y, c, k)
            _remote(land.at[slot], land.at[slot], fsend.at[k], fsend.at[k], sibling).wait_send()
        _, slot = _peer(x, y, c, 6)
        _remote(land.at[slot], land.at[slot], dsend.at[6], dsend.at[6], sibling).wait_send()

    return _comm_call("w_in_finish", [land], [send, fsend, dsend, drecv], [], body, after=after)[1][0]


def _in_proj_part(name, h, land, proj, me_arr, k0, kstep, nk, tm=1024):
    S, D = h.shape
    C = land.shape[2]
    tm = min(tm, S)

    def body(me_ref, a_ref, b_ref, *rest):
        rest[-1][...] = jnp.dot(a_ref[...], b_ref[...], preferred_element_type=f32)

    slot = lambda j, me: me[0] ^ (k0 + kstep * j)
    args = [h, land] + ([] if proj is None else [proj])
    grid_spec = pltpu.PrefetchScalarGridSpec(
        num_scalar_prefetch=1, grid=(nk, S // tm),
        in_specs=[pl.BlockSpec((tm, D), lambda j, i, me: (i, 0)),
                  pl.BlockSpec((None, D, C), lambda j, i, me: (slot(j, me), 0, 0))] + [ANY_SPEC] * (len(args) - 2),
        out_specs=pl.BlockSpec((tm, C), lambda j, i, me: (i, slot(j, me))))
    return pl.pallas_call(
        body, name=name, out_shape=jax.ShapeDtypeStruct((S, NDEV * C), f32), grid_spec=grid_spec,
        input_output_aliases={} if proj is None else {3: 0}, compiler_params=_params("arbitrary", "arbitrary"),
    )(me_arr, *args)


NCHIP = NDEV // 2


def _pairs_start(name, srcs):
    n = len(srcs)
    npairs = [src.shape[0] // 2 for src in srcs]

    def body(a, s, new):
        x, y, c, me = _my_place()
        sibling, _ = _peer(x, y, c, 1)
        for t in range(n):
            src, pair, send, recv = a[t], a[n + t], new[t], new[n + t]
            for i in range(npairs[t]):
                _remote(src.at[2 * i + 1 - c], pair.at[i], send.at[i], recv.at[i], sibling).start()

    pairs = [lax.empty((npairs[t],) + srcs[t].shape[1:], srcs[t].dtype) for t in range(n)]
    sems, arrays, token = _comm_call(name, list(srcs) + pairs, [], [(m,) for m in npairs] * 2, body, token=True)
    return [(sems[t], sems[n + t], arrays[t], arrays[n + t]) for t in range(n)], token


def _pairs_wait(name, groups, after):
    n = len(groups)

    def body(a, s, new):
        x, y, c, me = _my_place()
        sibling, _ = _peer(x, y, c, 1)
        for t in range(n):
            src, pair, send, recv = a[t], a[n + t], s[t], s[n + t]
            for i in range(pair.shape[0]):
                cp = _remote(src.at[2 * i + 1 - c], pair.at[i], send.at[i], recv.at[i], sibling)
                cp.wait_recv()
                cp.wait_send()

    arrays = _comm_call(name, [g[2] for g in groups] + [g[3] for g in groups],
                        [g[0] for g in groups] + [g[1] for g in groups], [], body, after=after)[1]
    return [(arrays[t], arrays[n + t]) for t in range(n)]


def _pair_start(name, src):
    ((send, recv, src, pair),), token = _pairs_start(name, [src])
    return send, recv, src, pair, token


def _pair_wait(name, src, pair, send, recv, after):
    return _pairs_wait(name, [(send, recv, src, pair)], after)[0]


def _pair_sum(name, src, pair, core, tr=1024):
    npair, R, Cc = pair.shape
    tr = min(tr, R)

    def body(core_ref, a_ref, b_ref, o_ref):
        o_ref[...] = (a_ref[...].astype(f32) + b_ref[...].astype(f32)).astype(o_ref.dtype)

    grid_spec = pltpu.PrefetchScalarGridSpec(
        num_scalar_prefetch=1, grid=(npair, R // tr),
        in_specs=[pl.BlockSpec((None, tr, Cc), lambda i, r, core: (2 * i + core[0], r, 0)),
                  pl.BlockSpec((None, tr, Cc), lambda i, r, core: (i, r, 0))],
        out_specs=pl.BlockSpec((None, tr, Cc), lambda i, r, core: (i, r, 0)))
    return pl.pallas_call(body, name=name, out_shape=jax.ShapeDtypeStruct(pair.shape, pair.dtype),
                          grid_spec=grid_spec, compiler_params=_params("parallel", "parallel"))(core, src, pair)


def _owner_chip(first, i):
    q = first // 2 + i
    return q >> 1 & 1, q & 1


def _chips_start(name, groups, rows=None, after=()):
    n = len(groups)
    row_of = [pl.ds(*(rows or (0, g[0].shape[1]))) for g in groups]

    def body(a, s, new):
        x, y, c, me = _my_place()
        for t, (_, _, first) in enumerate(groups):
            sums, land, send, recv = a[t], a[n + t], new[t], new[n + t]
            for i in range(sums.shape[0]):
                ox, oy = _owner_chip(first, i)

                @pl.when((x != ox) | (y != oy))
                def _():
                    _remote(sums.at[i, row_of[t]], land.at[2 * x + y, row_of[t]], send.at[i], recv.at[2 * x + y],
                            (ox, oy, c)).start()

    sems, arrays, token = _comm_call(name, [g[0] for g in groups] + [g[1] for g in groups], [],
                                     [(g[0].shape[0],) for g in groups] + [(NCHIP,)] * n, body, after=after, token=True)
    return [(sems[t], sems[n + t], arrays[t], arrays[n + t]) for t in range(n)], token


def _chips_wait(name, groups, firsts, after, rows=None):
    n = len(groups)
    row_of = [pl.ds(*(rows or (0, g[2].shape[1]))) for g in groups]

    def body(a, s, new):
        x, y, c, me = _my_place()
        for t in range(n):
            sums, land, send, recv, first = a[t], a[n + t], s[t], s[n + t], firsts[t]
            npair = sums.shape[0]
            mine = (me >= first) & (me < first + 2 * npair)
            for i in range(npair):
                ox, oy = _owner_chip(first, i)

                @pl.when((x != ox) | (y != oy))
                def _():
                    _remote(sums.at[i, row_of[t]], land.at[2 * x + y, row_of[t]], send.at[i], recv.at[2 * x + y],
                            (ox, oy, c)).wait_send()
            for q in range(NCHIP):
                @pl.when(mine & (2 * x + y != q))
                def _():
                    _remote(sums.at[0, row_of[t]], land.at[q, row_of[t]], send.at[0], recv.at[q],
                            (q >> 1, q & 1, c)).wait_recv()

    arrays = _comm_call(name, [g[2] for g in groups] + [g[3] for g in groups],
                        [g[0] for g in groups] + [g[1] for g in groups], [], body, after=after)[1]
    return [(arrays[t], arrays[n + t]) for t in range(n)]


def _chip_start(name, sums, land, first, rows=None, after=()):
    ((send, recv, sums, land),), token = _chips_start(name, [(sums, land, first)], rows, after)
    return send, recv, sums, land, token


def _chip_wait(name, sums, land, send, recv, first, after, rows=None):
    return _chips_wait(name, [(send, recv, sums, land)], [first], after, rows)[0]


def _matmul(a, b, *, name, out_dtype, ta=False, tb=False, b_slots=False, out_slots=0, b_cols=None,
            tm=1024, tn=1024, tk=2048, dep=None):
    M, K = (a.shape[1], a.shape[0]) if ta else a.shape
    col0 = 0
    if b_slots:
        slab = b.shape[2]
        N = b.shape[1] if tb else b.shape[0] * slab
        assert (K if tb else N) == b.shape[0] * slab
    elif b_cols is not None:
        assert not tb
        col0, N = b_cols
    else:
        N = b.shape[0] if tb else b.shape[1]
    tm, tn, tk = min(tm, M), min(tn, N), min(tk, K)
    if b_slots:
        if tb:
            tk = min(tk, slab)
        else:
            tn = min(tn, slab)
    if out_slots:
        tn = min(tn, N // out_slots)
    nm, nn, nk = M // tm, N // tn, K // tk
    assert (nm * tm, nn * tn, nk * tk) == (M, N, K) and col0 % tn == 0, (name, M, N, K, tm, tn, tk)
    j0 = col0 // tn

    a_spec = pl.BlockSpec((tk, tm), lambda i, j, k: (k, i)) if ta else pl.BlockSpec((tm, tk), lambda i, j, k: (i, k))
    if b_slots and tb:
        per = slab // tk
        b_spec = pl.BlockSpec((None, tn, tk), lambda i, j, k: (k // per, j, k % per))
    elif b_slots:
        per = slab // tn
        b_spec = pl.BlockSpec((None, tk, tn), lambda i, j, k: (j // per, k, j % per))
    elif tb:
        b_spec = pl.BlockSpec((tn, tk), lambda i, j, k: (j, k))
    else:
        b_spec = pl.BlockSpec((tk, tn), lambda i, j, k: (k, j + j0))
    if out_slots:
        per_o = (N // out_slots) // tn
        o_spec = pl.BlockSpec((None, tm, tn), lambda i, j, k: (j // per_o, i, j % per_o))
        out_shape = jax.ShapeDtypeStruct((out_slots, M, N // out_slots), out_dtype)
    else:
        o_spec = pl.BlockSpec((tm, tn), lambda i, j, k: (i, j))
        out_shape = jax.ShapeDtypeStruct((M, N), out_dtype)
    dims = (((0 if ta else 1,), (1 if tb else 0,)), ((), ()))
    deps = [] if dep is None else [dep]

    def body(a_ref, b_ref, *rest):
        o_ref = rest[len(deps)]
        prod = lax.dot_general(a_ref[...], b_ref[...], dims, preferred_element_type=f32)
        if nk == 1:
            o_ref[...] = prod.astype(out_dtype)
            return
        acc_ref = rest[len(deps) + 1]
        k = pl.program_id(2)

        @pl.when(k == 0)
        def _():
            acc_ref[...] = prod

        @pl.when((k > 0) & (k < nk - 1))
        def _():
            acc_ref[...] += prod

        @pl.when(k == nk - 1)
        def _():
            o_ref[...] = (acc_ref[...] + prod).astype(out_dtype)

    return pl.pallas_call(
        body, name=name, out_shape=out_shape, grid=(nm, nn, nk),
        in_specs=[a_spec, b_spec] + [ANY_SPEC] * len(deps), out_specs=o_spec,
        scratch_shapes=[pltpu.VMEM((tm, tn), f32)] if nk > 1 else [],
        compiler_params=_params("parallel", "parallel", "arbitrary"),
    )(a, b, *deps)


def _matmul_slabs_t(a_cols, a_slots, b, *, name, tm=512, tn=512, dep=None):
    M = a_cols.shape[0]
    n_slab, N, slab = b.shape
    n1, n2 = a_cols.shape[1] // slab, a_slots.shape[0]
    assert n1 + n2 == n_slab and a_slots.shape[1:] == (M, slab)
    tm, tn = min(tm, M), min(tn, N)
    deps = [] if dep is None else [dep]

    def body(a1_ref, a2_ref, b_ref, *rest):
        o_ref = rest[len(deps)]
        acc = None
        for s in range(n_slab):
            lhs = a1_ref[:, s * slab:(s + 1) * slab] if s < n1 else a2_ref[s - n1]
            prod = lax.dot_general(lhs, b_ref[s], (((1,), (1,)), ((), ())), preferred_element_type=f32)
            acc = prod if acc is None else acc + prod
        o_ref[...] = acc

    return pl.pallas_call(
        body, name=name, out_shape=jax.ShapeDtypeStruct((M, N), f32), grid=(M // tm, N // tn),
        in_specs=[pl.BlockSpec((tm, n1 * slab), lambda i, j: (i, 0)), pl.BlockSpec((n2, tm, slab), lambda i, j: (0, i, 0)),
                  pl.BlockSpec((n_slab, tn, slab), lambda i, j: (0, j, 0))] + [ANY_SPEC] * len(deps),
        out_specs=pl.BlockSpec((tm, tn), lambda i, j: (i, j)), compiler_params=_params("parallel", "parallel"),
    )(a_cols, a_slots, b, *deps)


def _ada_exchange(c_blk, cw_slab, w_ada, b_cols):
    nblk = c_blk.shape[0]
    D, W = w_ada.shape
    CW = cw_slab.shape[1]

    def body(c_ref, cw_ref, w_ref, b_ref, mod_ref, call_ref, cwg_ref, msend, send_sems, recv_sems):
        x, y, c, me = _my_place()
        call_ref[me] = _silu(c_ref[...])
        cwg_ref[me] = cw_ref[...]
        first = []
        for k in range(1, NDEV):
            peer, _ = _peer(x, y, c, k)
            first.append(_remote(call_ref.at[me], call_ref.at[me], send_sems.at[0, k], recv_sems.at[0, k], peer))
            first.append(_remote(cwg_ref.at[me], cwg_ref.at[me], send_sems.at[1, k], recv_sems.at[1, k], peer))
        for cp in first:
            cp.start()
        for k in range(1, NDEV):
            peer, slot = _peer(x, y, c, k)
            _remote(call_ref.at[slot], call_ref.at[slot], send_sems.at[0, k], recv_sems.at[0, k], peer).wait_recv()
            _remote(cwg_ref.at[slot], cwg_ref.at[slot], send_sems.at[1, k], recv_sems.at[1, k], peer).wait_recv()
        mod = jnp.broadcast_to(b_ref[...], (NDEV, W))
        for r in range(nblk):
            mod = mod + lax.dot_general(call_ref[:, r, :], w_ref[r * 128:(r + 1) * 128, :], (((1,), (0,)), ((), ())),
                                        preferred_element_type=f32, precision=lax.Precision.HIGHEST)
        row = lax.broadcasted_iota(jnp.int32, (NDEV, 1), 0)
        pick = lambda j: jnp.broadcast_to(jnp.sum(jnp.where(row == j, mod, 0.0), axis=0, keepdims=True), (8, W))
        mod_ref[me] = pick(me)
        second = []
        for k in range(1, NDEV):
            peer, slot = _peer(x, y, c, k)
            msend[k] = pick(slot)
            second.append(_remote(msend.at[k], mod_ref.at[me], send_sems.at[2, k], recv_sems.at[2, k], peer))
        for cp in second:
            cp.start()
        for k in range(1, NDEV):
            peer, slot = _peer(x, y, c, k)
            _remote(msend.at[k], mod_ref.at[slot], send_sems.at[2, k], recv_sems.at[2, k], peer).wait_recv()
        for cp in first + second:
            cp.wait_send()

    vmem = pl.BlockSpec(memory_space=pltpu.VMEM)
    return pl.pallas_call(
        body, name="ada_exchange",
        out_shape=(jax.ShapeDtypeStruct((NDEV, 8, W), f32), jax.ShapeDtypeStruct((NDEV, nblk, 128), f32),
                   jax.ShapeDtypeStruct((NDEV, 8, CW), f32)),
        in_specs=[vmem] * 4, out_specs=(vmem, vmem, vmem),
        scratch_shapes=[pltpu.VMEM((NDEV, 8, W), f32), pltpu.SemaphoreType.DMA((3, NDEV)),
                        pltpu.SemaphoreType.DMA((3, NDEV))],
        compiler_params=_params(),
    )(c_blk, cw_slab, w_ada, b_cols)


def _prenorm(x, scale, shift, g_pre, dep, tr=512):
    S, D = x.shape
    tr = min(tr, S)

    def body(x_ref, sc_ref, sh_ref, g_ref, dep_ref, h_ref):
        xv = x_ref[...]
        r = lax.rsqrt(jnp.mean(xv * xv, axis=-1, keepdims=True) + EPS)
        h_ref[...] = ((xv * r) * g_ref[...] * (1.0 + sc_ref[...]) + sh_ref[...]).astype(bf16)

    row = pl.BlockSpec((tr, D), lambda i: (i, 0))
    vec = pl.BlockSpec((1, D), lambda i: (0, 0))
    return pl.pallas_call(body, name="prenorm", out_shape=jax.ShapeDtypeStruct((S, D), bf16), grid=(S // tr,),
                          in_specs=[row, vec, vec, vec, ANY_SPEC], out_specs=row, compiler_params=_params("parallel"))(
                              x, scale, shift, g_pre, dep)


def _ext_rows(i, tr, S):
    g = lax.broadcasted_iota(jnp.int32, (tr + 16, 1), 0) + (i * tr - 8)
    return (g >= 0) & (g < S)


def _halo_specs(tr, S, C, col):
    nb8 = S // 8
    main = pl.BlockSpec((tr, C), lambda i: (i, col))
    prev = pl.BlockSpec((8, C), lambda i: (jnp.maximum(i * (tr // 8) - 1, 0), col))
    nxt = pl.BlockSpec((8, C), lambda i: (jnp.minimum((i + 1) * (tr // 8), nb8 - 1), col))
    return prev, main, nxt


def _conv_fwd(proj, conv_w, conv_b, g_conv, tr=512):
    S, C = proj.shape[0], proj.shape[1] // 8
    tr = min(tr, S)

    def body(up, um, un, cp, cm, cn, bg_ref, zc_ref, w_ref, cb_ref, g_ref, o_ref):
        i = pl.program_id(0)
        exists = _ext_rows(i, tr, S)
        u = jnp.concatenate([up[...], um[...], un[...]], axis=0)
        cg = jnp.concatenate([cp[...], cm[...], cn[...]], axis=0)
        t = jnp.where(exists, cg * u, 0.0)
        t_before = pltpu.roll(t, 1, 0)[8:tr + 8]
        t_after = pltpu.roll(t, tr + 15, 0)[8:tr + 8]
        w = w_ref[...]
        cv = w[0:1] * t_before + w[1:2] * t[8:tr + 8] + w[2:3] * t_after + cb_ref[...]
        yc = bg_ref[...] * cv
        rc = lax.rsqrt(jnp.mean(yc * yc, axis=-1, keepdims=True) + EPS)
        o_ref[...] = ((yc * rc) * g_ref[...] * _silu(zc_ref[...])).astype(bf16)

    u_specs = _halo_specs(tr, S, C, 0)
    c_specs = _halo_specs(tr, S, C, 2)
    vec = pl.BlockSpec((1, C), lambda i: (0, 0))
    return pl.pallas_call(
        body, name="conv_fwd", out_shape=jax.ShapeDtypeStruct((S, 2 * C), bf16), grid=(S // tr,),
        in_specs=[*u_specs, *c_specs, pl.BlockSpec((tr, C), lambda i: (i, 1)), pl.BlockSpec((tr, C), lambda i: (i, 3)),
                  pl.BlockSpec((8, C), lambda i: (0, 0)), vec, vec],
        out_specs=pl.BlockSpec((tr, C), lambda i: (i, 0)), compiler_params=_params("parallel"),
    )(proj, proj, proj, proj, proj, proj, proj, proj, conv_w, conv_b, g_conv)


def _branch_geometry(S, r, inter):
    L = S // r * inter
    nq = min(128, L)
    nk = min(nq + 2 * HALF_WIN * inter, L)
    assert L % nq == 0 and (L == nk or L >= nq + 2 * HALF_WIN * inter)
    return L, nq, nk, L // nq


QUAD = 4


def _to_quad(dst, src, S):
    n = S // QUAD
    for rho in range(QUAD):
        dst[pl.ds(rho * n, n), :] = src[pl.ds(rho, n, stride=QUAD), :]


def _block_rows(idx, r, inter, S, L, nq, nk, nblk):
    rho, qb = (0, idx) if r == 1 else (idx // nblk, idx % nblk)
    i0 = qb * nq
    ws = jnp.clip(i0 - HALF_WIN * inter, 0, L - nk)
    if r == 1:
        return pl.ds(pl.multiple_of(i0, 8), nq), pl.ds(pl.multiple_of(ws, 8), nk), i0 - ws
    assert r % (QUAD * inter) == 0
    step = r // QUAD // inter
    base = (rho % QUAD) * (S // QUAD) + rho // QUAD
    if step == 1:
        return pl.ds(pl.multiple_of(base + i0, 8), nq), pl.ds(pl.multiple_of(base + ws, 8), nk), i0 - ws
    return pl.ds(base + step * i0, nq, stride=step), pl.ds(base + step * ws, nk, stride=step), i0 - ws


N_CASES = 3
SCALE = HEAD_DIM ** -0.5
ATTN_UNROLL = 16


def _bias_shape(S):
    shapes = [_branch_geometry(S, r, inter)[1:3] for _, r, inter in BRANCHES]
    return (len(BRANCHES) * N_CASES * 2, max(nq for nq, _ in shapes), max(nk for _, nk in shapes))


def _bias_index(b, case, head):
    return (b * N_CASES + case) * 2 + head


def _fill_bias(bias_scr, sl_ref, S):
    sl = sl_ref[...]
    slope = (sl[0:1, 0:1], sl[0:1, HEAD_DIM:HEAD_DIM + 1])
    for b, (_, r, inter) in enumerate(BRANCHES):
        L, nq, nk, nblk = _branch_geometry(S, r, inter)
        rel = lax.broadcasted_iota(jnp.int32, (nq, nk), 0) - lax.broadcasted_iota(jnp.int32, (nq, nk), 1)
        for case in range(N_CASES):
            d = jnp.abs(rel + case * HALF_WIN)
            valid = d <= HALF_WIN * inter
            if inter > 1:
                valid = valid & (jnp.bitwise_and(d, inter - 1) == 0)
            dist = d.astype(f32) * float(r // inter)
            for head in range(2):
                bias_scr[_bias_index(b, case, head), 0:nq, 0:nk] = jnp.where(valid, -slope[head] * dist, NEG_INF)


def _bias_tiles(slopes, S, dep):
    npair = slopes.shape[0]
    shape = _bias_shape(S)

    def body(sl_ref, dep_ref, o_ref):
        _fill_bias(o_ref, sl_ref, S)

    return pl.pallas_call(
        body, name="bias_tiles", out_shape=jax.ShapeDtypeStruct((npair,) + shape, f32), grid=(npair,),
        in_specs=[pl.BlockSpec((None, 8, PAIR), lambda p: (p, 0, 0)), ANY_SPEC],
        out_specs=pl.BlockSpec((None,) + shape, lambda p: (p, 0, 0, 0)), compiler_params=_params("parallel"),
    )(slopes, dep)


def _head_slopes(n_heads):
    slopes = 2.0 ** (-8.0 * jnp.arange(1, n_heads + 1, dtype=f32) / n_heads)
    return jnp.broadcast_to(jnp.repeat(slopes.reshape(n_heads // 2, 2), HEAD_DIM, axis=1)[:, None, :],
                            (n_heads // 2, 8, PAIR))


def _attn_fwd(proj, bias):
    S, C = proj.shape[0], proj.shape[1] // 8
    npair = C // PAIR

    def body(q_ref, k_ref, v_ref, bias_scr, o_ref, lse_ref, m_scr, l_scr, a_scr, q4_scr, k4_scr, v4_scr):
        lane = lax.broadcasted_iota(jnp.int32, (1, PAIR), 1)
        first = lane < HEAD_DIM
        for dst, src in ((q4_scr, q_ref), (k4_scr, k_ref), (v4_scr, v_ref)):
            _to_quad(dst, src, S)

        for b, (_, r, inter) in enumerate(BRANCHES):
            L, nq, nk, nblk = _branch_geometry(S, r, inter)
            qs, ks, vs = (q_ref, k_ref, v_ref) if r == 1 else (q4_scr, k4_scr, v4_scr)

            def step(idx, carry, b=b, r=r, L=L, nq=nq, nk=nk, nblk=nblk, qs=qs, ks=ks, vs=vs):
                qrows, krows, off = _block_rows(idx, r, inter, S, L, nq, nk, nblk)
                case = off // HALF_WIN
                q2 = qs[qrows, :] * SCALE
                k2 = ks[krows, :].astype(bf16)
                v2 = vs[krows, :].astype(bf16)
                ms, accs = [], []
                for hh in range(2):
                    mine = first if hh == 0 else ~first
                    qh = jnp.where(mine, q2, 0.0).astype(bf16)
                    s = lax.dot_general(qh, k2, (((1,), (1,)), ((), ())), preferred_element_type=f32)
                    s = s + bias_scr[_bias_index(b, case, hh), 0:nq, 0:nk]
                    m = jnp.max(s, axis=-1, keepdims=True)
                    p = jnp.exp(s - m).astype(bf16)
                    vh = jnp.where(mine, v2, jnp.ones_like(v2))
                    ms.append(m)
                    accs.append(jnp.dot(p, vh, preferred_element_type=f32))
                m_scr[b, qrows, :] = jnp.where(first, ms[0], ms[1])
                a_scr[b, qrows, :] = jnp.where(first, accs[0], accs[1])
                l_scr[b, qrows, :] = jnp.where(first, accs[1], accs[0])
                return carry

            lax.fori_loop(0, S // nq, step, 0, unroll=min(ATTN_UNROLL, S // nq))

        n4 = S // QUAD
        ch = min(256, n4)
        nch = n4 // ch

        def merge(i, carry):
            rho, part = i // nch, i % nch
            sorted_rows = pl.ds(pl.multiple_of(rho * n4 + part * ch, 8), ch)
            token_rows = pl.ds(rho + QUAD * part * ch, ch, stride=QUAD)
            rows = (token_rows,) + (sorted_rows,) * (len(BRANCHES) - 1)
            ms = [m_scr[b, rows[b], :] for b in range(len(BRANCHES))]
            m = functools.reduce(jnp.maximum, ms)
            l = jnp.zeros((ch, PAIR), f32)
            acc = jnp.zeros((ch, PAIR), f32)
            for b in range(len(BRANCHES)):
                w = jnp.exp(ms[b] - m)
                l = l + w * pltpu.roll(l_scr[b, rows[b], :], HEAD_DIM, 1)
                acc = acc + w * a_scr[b, rows[b], :]
            o_ref[token_rows, :] = acc / l
            lse_ref[token_rows, :] = m + jnp.log(l)
            return carry

        lax.fori_loop(0, QUAD * nch, merge, 0, unroll=2)

    blk = lambda part: pl.BlockSpec((S, PAIR), lambda p: (0, part * npair + p))
    out = pl.BlockSpec((S, PAIR), lambda p: (0, p))
    return pl.pallas_call(
        body, name="attn_fwd",
        out_shape=(jax.ShapeDtypeStruct((S, C), f32), jax.ShapeDtypeStruct((S, C), f32)), grid=(npair,),
        in_specs=[blk(4), blk(5), blk(6), pl.BlockSpec((None,) + _bias_shape(S), lambda p: (p, 0, 0, 0))],
        out_specs=(out, out),
        scratch_shapes=[pltpu.VMEM((3, S, PAIR), f32)] * 3 + [pltpu.VMEM((S, PAIR), f32)] * 3,
        compiler_params=_params("parallel"),
    )(proj, proj, proj, bias)


def _attn_post(ycat, o, proj, g_attn, tr=512):
    S, C = o.shape
    tr = min(tr, S)

    def body(y_ref, o_ref, z_ref, g_ref, out_ref):
        del y_ref
        ov = o_ref[...]
        ra = lax.rsqrt(jnp.mean(ov * ov, axis=-1, keepdims=True) + EPS)
        out_ref[...] = ((ov * ra) * g_ref[...] * _silu(z_ref[...])).astype(bf16)

    return pl.pallas_call(
        body, name="attn_post", out_shape=jax.ShapeDtypeStruct(ycat.shape, ycat.dtype), grid=(S // tr,),
        in_specs=[HBM_SPEC, pl.BlockSpec((tr, C), lambda i: (i, 0)), pl.BlockSpec((tr, C), lambda i: (i, 7)),
                  pl.BlockSpec((1, C), lambda i: (0, 0))],
        out_specs=pl.BlockSpec((tr, C), lambda i: (i, 1)), input_output_aliases={0: 0},
        compiler_params=_params("arbitrary"),
    )(ycat, o, proj, g_attn)


def _residual_minus_target(x, target, dep, tr=512):
    S, D = x.shape
    tr = min(tr, S)

    def body(x_ref, t_ref, dep_ref, o_ref):
        o_ref[...] = x_ref[...] - t_ref[...]

    row = pl.BlockSpec((tr, D), lambda i: (i, 0))
    return pl.pallas_call(body, name="residual_minus_target", out_shape=jax.ShapeDtypeStruct((S, D), f32),
                          grid=(S // tr,), in_specs=[row, row, ANY_SPEC], out_specs=row,
                          compiler_params=_params("parallel"))(x, target, dep)


def _sandwich(y, x_minus_t, gate, g_post, tr=256):
    S, D = y.shape
    tr = min(tr, S)

    def body(y_ref, xt_ref, gate_ref, g_ref, dy_ref, dout_ref, sums_ref):
        i = pl.program_id(0)
        gate, g = gate_ref[...], g_ref[...]
        gg = gate * g
        yv = y_ref[...]
        rp = lax.rsqrt(jnp.mean(yv * yv, axis=-1, keepdims=True) + EPS)
        yhat = yv * rp
        err = xt_ref[...] + gg * yhat
        dout = err * (1.0 / D)
        dout_ref[...] = dout
        q = dout * yhat
        w = dout * gg
        dy_ref[...] = (rp * (w - yhat * jnp.sum(q * gg, axis=-1, keepdims=True) * (1.0 / D))).astype(bf16)
        loss = 0.5 * jnp.sum(jnp.mean(err * err, axis=-1, keepdims=True), axis=0, keepdims=True)
        q_sum = jnp.sum(q, axis=0, keepdims=True)
        row = lax.broadcasted_iota(jnp.int32, (8, D), 0)
        upd = jnp.where(row == 0, q_sum * g, jnp.where(row == 1, q_sum * gate, jnp.where(row == 2, loss, 0.0)))

        @pl.when(i == 0)
        def _():
            sums_ref[...] = upd

        @pl.when(i > 0)
        def _():
            sums_ref[...] += upd

    row = pl.BlockSpec((tr, D), lambda i: (i, 0))
    vec = pl.BlockSpec((1, D), lambda i: (0, 0))
    return pl.pallas_call(
        body, name="sandwich",
        out_shape=(jax.ShapeDtypeStruct((S, D), bf16), jax.ShapeDtypeStruct((S, D), f32), jax.ShapeDtypeStruct((8, D), f32)),
        grid=(S // tr,), in_specs=[row, row, vec, vec],
        out_specs=(row, row, pl.BlockSpec((8, D), lambda i: (0, 0))), compiler_params=_params("arbitrary"),
    )(y, x_minus_t, gate, g_post)


def _conv_bwd(proj, dycat, conv_w, conv_b, g_conv, dep, tr=256):
    S, C = proj.shape[0], proj.shape[1] // 8
    tr = min(tr, S)
    n = tr + 16

    def body(*refs):
        ins, (w_ref, cb_ref, g_ref, _, dp_ref, sums_ref) = refs[:15], refs[15:]
        i = pl.program_id(0)
        exists = _ext_rows(i, tr, S)
        u, bg, cg, zc, dyn = (jnp.concatenate([ins[3 * t][...], ins[3 * t + 1][...], ins[3 * t + 2][...]], axis=0)
                              for t in range(5))
        w = w_ref[...]
        t = jnp.where(exists, cg * u, 0.0)
        t_before, t_after = pltpu.roll(t, 1, 0), pltpu.roll(t, n - 1, 0)
        cv = w[0:1] * t_before + w[1:2] * t + w[2:3] * t_after + cb_ref[...]
        yc = bg * cv
        rc = lax.rsqrt(jnp.mean(yc * yc, axis=-1, keepdims=True) + EPS)
        yhat = yc * rc
        sz, dsz = _silu_and_slope(zc)
        wgt = dyn * g_ref[...] * sz
        dyc = rc * (wgt - yhat * jnp.mean(wgt * yhat, axis=-1, keepdims=True))
        dcv = jnp.where(exists, dyc * bg, 0.0)
        dt = w[0:1] * pltpu.roll(dcv, n - 1, 0) + w[1:2] * dcv + w[2:3] * pltpu.roll(dcv, 1, 0)
        mid = slice(8, tr + 8)
        dp_ref[:, 0:C] = (dt * cg)[mid].astype(bf16)
        dp_ref[:, C:2 * C] = (dyc * cv)[mid].astype(bf16)
        dp_ref[:, 2 * C:3 * C] = (dt * u)[mid].astype(bf16)
        dp_ref[:, 3 * C:4 * C] = (dyn * yhat * g_ref[...] * dsz)[mid].astype(bf16)
        colsum = lambda v: jnp.sum(v[mid], axis=0, keepdims=True)
        parts = [colsum(dyn * yhat * sz), colsum(dcv), colsum(dcv * t_before), colsum(dcv * t), colsum(dcv * t_after)]
        row = lax.broadcasted_iota(jnp.int32, (8, C), 0)
        upd = jnp.zeros((8, C), f32)
        for j, pj in enumerate(parts):
            upd = jnp.where(row == j, pj, upd)

        @pl.when(i == 0)
        def _():
            sums_ref[...] = upd

        @pl.when(i > 0)
        def _():
            sums_ref[...] += upd

    specs = []
    for col in range(4):
        specs += _halo_specs(tr, S, C, col)
    specs += _halo_specs(tr, S, C, 0)
    vec = pl.BlockSpec((1, C), lambda i: (0, 0))
    return pl.pallas_call(
        body, name="conv_bwd",
        out_shape=(jax.ShapeDtypeStruct((S, 4 * C), bf16), jax.ShapeDtypeStruct((8, C), f32)), grid=(S // tr,),
        in_specs=[*specs, pl.BlockSpec((8, C), lambda i: (0, 0)), vec, vec, ANY_SPEC],
        out_specs=(pl.BlockSpec((tr, 4 * C), lambda i: (i, 0)), pl.BlockSpec((8, C), lambda i: (0, 0))),
        compiler_params=_params("arbitrary"),
    )(*([proj] * 12), dycat, dycat, dycat, conv_w, conv_b, g_conv, dep)


def _attn_post_bwd(o, proj, dycat, g_attn, dep, tr=512):
    S, C = o.shape
    tr = min(tr, S)

    def body(o_ref, z_ref, dy_ref, g_ref, dep_ref, do_ref, dz_ref, sums_ref):
        i = pl.program_id(0)
        ov, zv, dyn = o_ref[...], z_ref[...], dy_ref[...]
        ra = lax.rsqrt(jnp.mean(ov * ov, axis=-1, keepdims=True) + EPS)
        ohat = ov * ra
        sz, dsz = _silu_and_slope(zv)
        wgt = dyn * g_ref[...] * sz
        do_ref[...] = ra * (wgt - ohat * jnp.mean(wgt * ohat, axis=-1, keepdims=True))
        dz_ref[...] = (dyn * ohat * g_ref[...] * dsz).astype(bf16)
        row = lax.broadcasted_iota(jnp.int32, (8, C), 0)
        upd = jnp.where(row == 0, jnp.sum(dyn * ohat * sz, axis=0, keepdims=True), 0.0)

        @pl.when(i == 0)
        def _():
            sums_ref[...] = upd

        @pl.when(i > 0)
        def _():
            sums_ref[...] += upd

    return pl.pallas_call(
        body, name="attn_post_bwd",
        out_shape=(jax.ShapeDtypeStruct((S, C), f32), jax.ShapeDtypeStruct((4, S, C), bf16),
                   jax.ShapeDtypeStruct((8, C), f32)),
        grid=(S // tr,),
        in_specs=[pl.BlockSpec((tr, C), lambda i: (i, 0)), pl.BlockSpec((tr, C), lambda i: (i, 7)),
                  pl.BlockSpec((tr, C), lambda i: (i, 1)), pl.BlockSpec((1, C), lambda i: (0, 0)), ANY_SPEC],
        out_specs=(pl.BlockSpec((tr, C), lambda i: (i, 0)), pl.BlockSpec((None, tr, C), lambda i: (3, i, 0)),
                   pl.BlockSpec((8, C), lambda i: (0, 0))),
        compiler_params=_params("arbitrary"),
    )(o, proj, dycat, g_attn, dep)


def _attn_bwd(proj, o, do, lse, bias, dqkvz, dep):
    S, C = o.shape
    npair = C // PAIR

    def body(q_ref, k_ref, v_ref, o_ref, do_ref, lse_ref, bias_scr, old_ref, dep_ref, dqkv_ref,
             acc_scr, dl_scr, quad_scr):
        lane = lax.broadcasted_iota(jnp.int32, (1, PAIR), 1)
        first = lane < HEAD_DIM
        ch = min(256, S)

        def prep(i, carry):
            rows = pl.ds(pl.multiple_of(i * ch, 8), ch)
            prod = do_ref[rows, :] * o_ref[rows, :]
            d0 = jnp.sum(jnp.where(first, prod, 0.0), axis=-1, keepdims=True)
            d1 = jnp.sum(jnp.where(first, 0.0, prod), axis=-1, keepdims=True)
            dl_scr[rows, :] = jnp.where(first, d0, d1)
            zero = jnp.zeros((ch, PAIR), f32)
            for order in range(2):
                for t in range(3):
                    acc_scr[order, t, rows, :] = zero
            return carry

        lax.fori_loop(0, S // ch, prep, 0, unroll=2)
        token_srcs = (q_ref, k_ref, v_ref, do_ref, lse_ref, dl_scr)
        for j, src in enumerate(token_srcs):
            _to_quad(quad_scr.at[j], src, S)

        for b, (_, r, inter) in enumerate(BRANCHES):
            L, nq, nk, nblk = _branch_geometry(S, r, inter)
            order = 0 if r == 1 else 1
            srcs = token_srcs if r == 1 else tuple(quad_scr.at[j] for j in range(6))

            def step(idx, carry, b=b, r=r, L=L, nq=nq, nk=nk, nblk=nblk, order=order, srcs=srcs):
                qs, ks, vs, dos, lses, dls = srcs
                dq_scr, dk_scr, dv_scr = (acc_scr.at[order, t] for t in range(3))
                qrows, krows, off = _block_rows(idx, r, inter, S, L, nq, nk, nblk)
                case = off // HALF_WIN
                q2 = qs[qrows, :] * SCALE
                k2 = ks[krows, :].astype(bf16)
                v2 = vs[krows, :].astype(bf16)
                do2 = dos[qrows, :]
                lse2 = lses[qrows, :]
                dl2 = dls[qrows, :]
                dq2 = jnp.zeros((nq, PAIR), f32)
                dk2 = jnp.zeros((nk, PAIR), f32)
                dv2 = jnp.zeros((nk, PAIR), f32)
                for hh in range(2):
                    mine = first if hh == 0 else ~first
                    lo = hh * HEAD_DIM
                    qh = jnp.where(mine, q2, 0.0).astype(bf16)
                    doh = jnp.where(mine, do2, 0.0).astype(bf16)
                    s = lax.dot_general(qh, k2, (((1,), (1,)), ((), ())), preferred_element_type=f32)
                    s = s + bias_scr[_bias_index(b, case, hh), 0:nq, 0:nk]
                    p = jnp.exp(s - lse2[:, lo:lo + 1])
                    dv2 = dv2 + lax.dot_general(p.astype(bf16), doh, (((0,), (0,)), ((), ())), preferred_element_type=f32)
                    dp = lax.dot_general(doh, v2, (((1,), (1,)), ((), ())), preferred_element_type=f32)
                    ds = (p * (dp - dl2[:, lo:lo + 1])).astype(bf16)
                    dq2 = dq2 + jnp.where(mine, jnp.dot(ds, k2, preferred_element_type=f32), 0.0)
                    dk2 = dk2 + lax.dot_general(ds, qh, (((0,), (0,)), ((), ())), preferred_element_type=f32)
                dq_scr[qrows, :] = dq_scr[qrows, :] + dq2
                dk_scr[krows, :] = dk_scr[krows, :] + dk2
                dv_scr[krows, :] = dv_scr[krows, :] + dv2
                return carry

            lax.fori_loop(0, S // nq, step, 0, unroll=min(ATTN_UNROLL, S // nq))

        n4 = S // QUAD
        for t in range(3):
            for rho in range(QUAD):
                token_rows = pl.ds(rho, n4, stride=QUAD)
                acc_scr[0, t, token_rows, :] = acc_scr[0, t, token_rows, :] + acc_scr[1, t, pl.ds(rho * n4, n4), :]
        dqkv_ref[0] = (acc_scr[0, 0] * SCALE).astype(bf16)
        dqkv_ref[1] = acc_scr[0, 1].astype(bf16)
        dqkv_ref[2] = acc_scr[0, 2].astype(bf16)

    blk = lambda part: pl.BlockSpec((S, PAIR), lambda p: (0, part * npair + p))
    own = pl.BlockSpec((S, PAIR), lambda p: (0, p))
    return pl.pallas_call(
        body, name="attn_bwd", out_shape=jax.ShapeDtypeStruct(dqkvz.shape, dqkvz.dtype), grid=(npair,),
        in_specs=[blk(4), blk(5), blk(6), own, own, own,
                  pl.BlockSpec((None,) + _bias_shape(S), lambda p: (p, 0, 0, 0)), ANY_SPEC, ANY_SPEC],
        out_specs=pl.BlockSpec((3, S, PAIR), lambda p: (0, 0, p)), input_output_aliases={7: 0},
        scratch_shapes=[pltpu.VMEM((2, 3, S, PAIR), f32), pltpu.VMEM((S, PAIR), f32), pltpu.VMEM((6, S, PAIR), f32)],
        compiler_params=_params("arbitrary"),
    )(proj, proj, proj, o, do, lse, bias, dqkvz, dep)


def _prenorm_bwd(dh, x, dout, scale, g_pre, tr=256):
    S, D = x.shape
    tr = min(tr, S)

    def body(dh_ref, x_ref, dout_ref, sc_ref, g_ref, gx_ref, sums_ref):
        i = pl.program_id(0)
        xv, dhv = x_ref[...], dh_ref[...]
        r = lax.rsqrt(jnp.mean(xv * xv, axis=-1, keepdims=True) + EPS)
        xn = xv * r
        dxn = dhv * (g_ref[...] * (1.0 + sc_ref[...]))
        gx_ref[...] = dout_ref[...] + r * (dxn - xn * jnp.mean(dxn * xn, axis=-1, keepdims=True))
        dhx = dhv * xn
        row = lax.broadcasted_iota(jnp.int32, (8, D), 0)
        upd = jnp.where(row == 0, jnp.sum(dhv, axis=0, keepdims=True),
                        jnp.where(row == 1, jnp.sum(dhx, axis=0, keepdims=True) * g_ref[...],
                                  jnp.where(row == 2, jnp.sum(dhx, axis=0, keepdims=True) * (1.0 + sc_ref[...]), 0.0)))

        @pl.when(i == 0)
        def _():
            sums_ref[...] = upd

        @pl.when(i > 0)
        def _():
            sums_ref[...] += upd

    row = pl.BlockSpec((tr, D), lambda i: (i, 0))
    vec = pl.BlockSpec((1, D), lambda i: (0, 0))
    return pl.pallas_call(
        body, name="prenorm_bwd",
        out_shape=(jax.ShapeDtypeStruct((S, D), f32), jax.ShapeDtypeStruct((8, D), f32)), grid=(S // tr,),
        in_specs=[row, row, row, vec, vec], out_specs=(row, pl.BlockSpec((8, D), lambda i: (0, 0))),
        compiler_params=_params("arbitrary"),
    )(dh, x, dout, scale, g_pre)


def _adamw(w, g, m, v):
    m = ADAM_B1 * m + (1.0 - ADAM_B1) * g
    v = ADAM_B2 * v + (1.0 - ADAM_B2) * (g * g)
    m_hat = m / (1.0 - ADAM_B1 ** ADAM_STEP)
    v_hat = v / (1.0 - ADAM_B2 ** ADAM_STEP)
    delta = -ADAM_LR * (m_hat / (jnp.sqrt(v_hat) + ADAM_EPS) + ADAM_WD * w)
    return delta, m, v


def _sum_rows(parts, dep):
    P = parts.shape[1]

    def body(p_ref, dep_ref, o_ref):
        acc = p_ref[0:1, :]
        for j in range(1, NDEV):
            acc = acc + p_ref[j:j + 1, :]
        o_ref[...] = jnp.broadcast_to(acc, (8, P))

    vmem = pl.BlockSpec(memory_space=pltpu.VMEM)
    return pl.pallas_call(body, name="sum_small", out_shape=jax.ShapeDtypeStruct((8, P), f32),
                          in_specs=[vmem, ANY_SPEC], out_specs=vmem, compiler_params=_params())(parts, dep)


def _adamw_small(tot, params):
    given = [p[3] for p in params if not isinstance(p[3], int)]

    def body(tot_ref, *refs):
        given_refs = list(refs[:len(given)])
        ins = refs[len(given):len(given) + 3 * len(params)]
        outs = refs[len(given) + 3 * len(params):]
        for t, (w, _, _, where) in enumerate(params):
            w_ref, m_ref, v_ref = ins[3 * t:3 * t + 3]
            g = tot_ref[0:1, where:where + w.size] if isinstance(where, int) else given_refs.pop(0)[...]
            outs[4 * t][...] = g
            outs[4 * t + 1][...], outs[4 * t + 2][...], outs[4 * t + 3][...] = _adamw(w_ref[...], g, m_ref[...], v_ref[...])

    out_shape = tuple(jax.ShapeDtypeStruct(p[0].shape, f32) for p in params for _ in range(4))
    res = pl.pallas_call(body, name="adamw_small", out_shape=out_shape, compiler_params=_params())(
        tot, *given, *[a for p in params for a in p[:3]])
    return [res[4 * t:4 * t + 4] for t in range(len(params))]


def _adamw_sharded(name, parts, sums_a, sums_b, pick, w, m, v, rows=None, prev=None, tr=128):
    R, Cc = w.shape
    r0, nr = rows or (0, R)
    tr = math.gcd(tr, r0, nr)
    n, b0 = parts.shape[0], r0 // tr

    def body(pick_ref, p_ref, a_ref, b_ref, w_ref, m_ref, v_ref, *rest):
        g_ref, d_ref, nm_ref, nv_ref = rest[-4:]
        g = jnp.where(pick_ref[0] == 1, b_ref[...], a_ref[...]).astype(f32)
        for j in range(n):
            g = g + p_ref[j].astype(f32)
        g_ref[...] = g
        d_ref[...], nm_ref[...], nv_ref[...] = _adamw(w_ref[...], g, m_ref[...], v_ref[...])

    row = pl.BlockSpec((tr, Cc), lambda i, pick: (i + b0, 0))
    mine = pl.BlockSpec((None, tr, Cc), lambda i, pick: (pick[1], i + b0, 0))
    out = jax.ShapeDtypeStruct((R, Cc), f32)
    prev = list(prev or [])
    grid_spec = pltpu.PrefetchScalarGridSpec(
        num_scalar_prefetch=1, grid=(nr // tr,),
        in_specs=[pl.BlockSpec((n, tr, Cc), lambda i, pick: (0, i + b0, 0)), mine, mine, row, row, row]
        + [ANY_SPEC] * len(prev),
        out_specs=(row, row, row, row))
    return pl.pallas_call(
        body, name=name, out_shape=(out, out, out, out), grid_spec=grid_spec,
        input_output_aliases={7 + t: t for t in range(len(prev))}, compiler_params=_params("arbitrary"),
    )(pick, parts, sums_a, sums_b, w, m, v, *prev)


def _adamw_ada(c_t, dmod_cols, w, m, v, dep, tr=512):
    D, W = w.shape
    tr = min(tr, D)

    def body(c_ref, dm_ref, w_ref, m_ref, v_ref, dep_ref, g_ref, d_ref, nm_ref, nv_ref):
        g = lax.dot_general(c_ref[...], dm_ref[...], (((1,), (0,)), ((), ())), preferred_element_type=f32,
                            precision=lax.Precision.HIGHEST)
        g_ref[...] = g
        d_ref[...], nm_ref[...], nv_ref[...] = _adamw(w_ref[...], g, m_ref[...], v_ref[...])

    row = pl.BlockSpec((tr, W), lambda i: (i, 0))
    out = jax.ShapeDtypeStruct((D, W), f32)
    return pl.pallas_call(
        body, name="adamw_ada", out_shape=(out, out, out, out), grid=(D // tr,),
        in_specs=[pl.BlockSpec((tr, NDEV), lambda i: (i, 0)), pl.BlockSpec((NDEV, W), lambda i: (0, 0)), row, row, row,
                  ANY_SPEC],
        out_specs=(row, row, row, row), compiler_params=_params("parallel"),
    )(c_t, dmod_cols, w, m, v, dep)


def kernel(x, c, w_ada, b_ada, g_pre, w_in, conv_w, conv_b, g_conv, g_attn, w_out, g_post, loss_target, m_w_ada, m_b_ada, m_g_pre, m_w_in, m_conv_w, m_conv_b, m_g_conv, m_g_attn, m_w_out, m_g_post, v_w_ada, v_b_ada, v_g_pre, v_w_in, v_conv_w, v_conv_b, v_g_conv, v_g_attn, v_w_out, v_g_post):
    S, D = x.shape[1], x.shape[2]
    C = D // 2
    W = w_ada.shape[2]
    CW = conv_w.shape[2]
    me = 4 * lax.axis_index("x") + 2 * lax.axis_index("y") + lax.axis_index("c")
    x2, tgt = x[0], loss_target[0]
    w_ada2, w_in2, w_out2 = w_ada[0], w_in[0], w_out[0]

    R = D // NDEV
    core = lax.axis_index("c").astype(jnp.int32).reshape(1)

    cw_slab = jnp.zeros((8, CW), f32).at[:3].set(conv_w[0])
    b_cols = lax.dynamic_slice_in_dim(b_ada, me * W, W, axis=1)
    mod_slabs, c_blocks, cw_g = _ada_exchange(c.reshape(D // 128, 128), cw_slab, w_ada2, b_cols)
    c_all = c_blocks.reshape(NDEV, D)
    conv_w_full = jnp.transpose(cw_g, (1, 0, 2)).reshape(8, C)
    mod = mod_slabs[:, 0, :].reshape(1, 3 * D)
    shift, scale, gate = mod[:, :D], mod[:, D:2 * D], mod[:, 2 * D:]

    land_i = lax.dynamic_update_slice(lax.empty((NDEV, D, C), bf16), w_in2.astype(bf16)[None], (me, 0, 0))
    land_o = lax.dynamic_update_slice(lax.empty((NDEV, R, D), bf16), w_out2.astype(bf16)[None], (me, 0, 0))
    wi_send, wi_recv, land_i, w_token = _w_in_start(land_i, [mod_slabs])

    me_arr = me.astype(jnp.int32).reshape(1)
    h = _prenorm(x2, scale, shift, g_pre, w_token)
    land_i = _w_in_sibling(land_i, wi_recv, after=[h])
    proj = _in_proj_part("in_proj_a", h, land_i, None, me_arr, 0, 1, 2)
    x_minus_t = _residual_minus_target(x2, tgt, proj)
    bias = _bias_tiles(_head_slopes(C // HEAD_DIM), S, x_minus_t)

    def landing(rows, cols):
        return lax.dynamic_update_slice(lax.empty((NCHIP, rows, cols), bf16), jnp.zeros((1, rows, cols), bf16),
                                        (me // 2, 0, 0))

    land_go, land_gi = landing(R, D), landing(D, C)
    fi_send, fi_recv, land_i = _w_in_relay(land_i, wi_recv, after=[proj, bias, land_go, land_gi])
    proj = _in_proj_part("in_proj_b", h, land_i, proj, me_arr, 2, 2, 2)
    land_i = _w_in_forwarded(land_i, fi_recv, after=[proj])
    proj = _in_proj_part("in_proj_c", h, land_i, proj, me_arr, 3, 2, 2)
    (di_send, di_recv, wo_send, wo_recv), land_i, land_o = _w_in_diag(land_i, land_o, fi_recv, after=[proj])
    proj = _in_proj_part("in_proj_d", h, land_i, proj, me_arr, 6, 1, 1)
    win_g = _w_in_finish(land_i, wi_send, fi_send, di_send, di_recv, after=[proj])
    proj = _in_proj_part("in_proj_e", h, win_g, proj, me_arr, 7, 1, 1)
    ycat = _conv_fwd(proj, conv_w_full, conv_b, g_conv)
    o, lse = _attn_fwd(proj, bias)
    (fo_send, fo_recv), (land_o,), _ = _weights_forward("w_out_forward", land_o, wo_recv, after=[o])
    ycat = _attn_post(ycat, o, proj, g_attn)
    wout_g = _weights_wait("w_out_wait", land_o, wo_send, wo_recv, fo_send, fo_recv, after=[ycat])
    wout_full = wout_g.reshape(D, D)
    y = _matmul(ycat, wout_full, name="out_proj", out_dtype=f32)
    dy, dout, post_sums = _sandwich(y, x_minus_t, gate, g_post)

    gw_out = _matmul(ycat, dy, name="out_proj_dw", out_dtype=bf16, ta=True).reshape(NDEV, R, D)
    dycat = _matmul(dy, wout_full, name="out_proj_dx", out_dtype=f32, tb=True)
    dpc, conv_sums = _conv_bwd(proj, dycat, conv_w_full, conv_b, g_conv, gw_out)
    gw_c = _matmul(h, dpc, name="in_proj_dw_conv", out_dtype=bf16, ta=True, out_slots=4)
    first_pairs, p1_token = _pairs_start("g_first_pair_start", [gw_out, gw_c])
    do, dpa, attn_sums = _attn_post_bwd(o, proj, dycat, g_attn, p1_token)
    (gw_out, pair_o), (gw_c, pair_c) = _pairs_wait("g_first_pair_wait", first_pairs, after=[do])
    sum_o = _pair_sum("g_out_pair_sum", gw_out, pair_o, core)
    sum_c = _pair_sum("g_conv_pair_sum", gw_c, pair_c, core)
    ((co_send, co_recv, sum_o, land_go), (cc_send, cc_recv, sum_c, land_gi)), cc_token = _chips_start(
        "g_first_chip_start", [(sum_o, land_go, 0), (sum_c, land_gi, 0)])
    dpa = _attn_bwd(proj, o, do, lse, bias, dpa, cc_token)
    gw_a = _matmul(h, dpa, name="in_proj_dw_attn", out_dtype=bf16, ta=True, b_slots=True, out_slots=4)
    pa_send, pa_recv, gw_a, pair_a, pa_token = _pair_start("g_attn_pair_start", gw_a)
    sum_o, land_go = _chip_wait("g_out_chip_wait", sum_o, land_go, co_send, co_recv, 0, after=[pa_token])
    pick_out = jnp.stack([jnp.int32(0), me // 2]).astype(jnp.int32)
    g_w_out, d_w_out, nm_w_out, nv_w_out = _adamw_sharded(
        "adamw_w_out", land_go, sum_o, sum_o, pick_out, w_out2, m_w_out[0], v_w_out[0])
    gw_a, pair_a = _pair_wait("g_attn_pair_wait", gw_a, pair_a, pa_send, pa_recv, after=[g_w_out])
    sum_a = _pair_sum("g_attn_pair_sum", gw_a, pair_a, core)
    part_a, part_b = (0, 3 * D // 4), (3 * D // 4, D // 4)
    ca_send, ca_recv, sum_a, land_gi, ca_token = _chip_start("g_attn_chip_start_a", sum_a, land_gi, 4, part_a)
    dh = _matmul_slabs_t(dpc, dpa, win_g, name="in_proj_dx", dep=ca_token)
    grad_x, pre_sums = _prenorm_bwd(dh, x2, dout, scale, g_pre)

    small = jnp.concatenate([pre_sums[0:1], pre_sums[1:2], post_sums[0:1],
                             pre_sums[2:3], post_sums[1:2],
                             conv_sums[2:3], conv_sums[3:4], conv_sums[4:5],
                             conv_sums[1:2], conv_sums[0:1], attn_sums[0:1]], axis=1)
    small = jnp.concatenate([small.reshape(8 * D // 128, 128), jnp.broadcast_to(post_sums[2:3, :128], (8, 128))])
    gs_send, gs_recv, small, small_all, gs_token = _gather_start("gather_small_start", small)

    cb_send, cb_recv, sum_a, land_gi, cb_token = _chip_start("g_attn_chip_start_b", sum_a, land_gi, 4, part_b,
                                                             after=[gs_token])

    pick_in = jnp.stack([me // 4, (me % 4) // 2]).astype(jnp.int32)
    sum_c, land_gi = _chip_wait("g_conv_chip_wait", sum_c, land_gi, cc_send, cc_recv, 0, after=[cb_token])
    sum_a, land_gi = _chip_wait("g_attn_chip_wait_a", sum_a, land_gi, ca_send, ca_recv, 4, [cb_token], part_a)
    first = _adamw_sharded("adamw_w_in_a", land_gi, sum_c, sum_a, pick_in, w_in2, m_w_in[0], v_w_in[0], rows=part_a,
                           tr=256)

    small_all = _gather_wait("gather_small_wait", small, small_all, gs_send, gs_recv, after=[first[0]])
    small_all = small_all.reshape(NDEV, small.size)
    tot = _sum_rows(small_all, gs_token)
    loss = tot[0, 8 * D]
    taps_first = lambda a: jnp.transpose(a, (1, 0, 2))
    g_conv_w = lax.dynamic_slice_in_dim(tot[0:1, 5 * D:5 * D + 3 * C].reshape(3, 1, C), me * CW, CW, axis=2)
    ((g_b_ada, d_b_ada, nm_b_ada, nv_b_ada), (g_g_pre, d_g_pre, nm_g_pre, nv_g_pre),
     (g_g_post, d_g_post, nm_g_post, nv_g_post), conv_w_results,
     (g_conv_b, d_conv_b, nm_conv_b, nv_conv_b), (g_g_conv, d_g_conv, nm_g_conv, nv_g_conv),
     (g_g_attn, d_g_attn, nm_g_attn, nv_g_attn)) = _adamw_small(tot, [
         (b_ada, m_b_ada, v_b_ada, 0), (g_pre, m_g_pre, v_g_pre, 3 * D), (g_post, m_g_post, v_g_post, 4 * D),
         (taps_first(conv_w), taps_first(m_conv_w), taps_first(v_conv_w), g_conv_w),
         (conv_b, m_conv_b, v_conv_b, 5 * D + 3 * C),
         (g_conv, m_g_conv, v_g_conv, 5 * D + 4 * C), (g_attn, m_g_attn, v_g_attn, 5 * D + 5 * C)])
    g_conv_w, d_conv_w, nm_conv_w, nv_conv_w = (taps_first(a) for a in conv_w_results)

    dmod_cols = lax.dynamic_slice_in_dim(small_all[:, :3 * D], me * W, W, axis=1)
    g_w_ada, d_w_ada, nm_w_ada, nv_w_ada = _adamw_ada(c_all.T, dmod_cols, w_ada2, m_w_ada[0], v_w_ada[0], gs_token)

    sum_a, land_gi = _chip_wait("g_attn_chip_wait_b", sum_a, land_gi, cb_send, cb_recv, 4, [g_w_ada], part_b)
    g_w_in, d_w_in, nm_w_in, nv_w_in = _adamw_sharded(
        "adamw_w_in_b", land_gi, sum_c, sum_a, pick_in, w_in2, m_w_in[0], v_w_in[0], rows=part_b, prev=first, tr=256)

    return (loss, grad_x[None],
            g_w_ada[None], g_b_ada, g_g_pre, g_w_in[None], g_conv_w, g_conv_b, g_g_conv, g_g_attn, g_w_out[None], g_g_post,
            d_w_ada[None], d_b_ada, d_g_pre, d_w_in[None], d_conv_w, d_conv_b, d_g_conv, d_g_attn, d_w_out[None], d_g_post,
            nm_w_ada[None], nm_b_ada, nm_g_pre, nm_w_in[None], nm_conv_w, nm_conv_b, nm_g_conv, nm_g_attn, nm_w_out[None], nm_g_post,
            nv_w_ada[None], nv_b_ada, nv_g_pre, nv_w_in[None], nv_conv_w, nv_conv_b, nv_g_conv, nv_g_attn, nv_w_out[None], nv_g_post)
```

```python
import functools
import math

import jax
import jax.numpy as jnp
from jax import lax
from jax.experimental import pallas as pl
from jax.experimental.pallas import tpu as pltpu

f32 = jnp.float32
bf16 = jnp.bfloat16

NDEV = 8
HEAD_DIM = 64
PAIR = 2 * HEAD_DIM
BRANCHES = ((128, 1, 1), (512, 4, 1), (2048, 16, 2))
HALF_WIN = 64
EPS = 1e-6
NEG_INF = -1e30
ADAM_LR, ADAM_B1, ADAM_B2, ADAM_EPS, ADAM_WD, ADAM_STEP = 0.001, 0.9, 0.999, 1e-08, 0.01, 10
MESH = pl.DeviceIdType.MESH
VMEM_LIMIT = 56 * 1024 * 1024
HBM_SPEC = pl.BlockSpec(memory_space=pltpu.HBM)
ANY_SPEC = pl.BlockSpec(memory_space=pl.ANY)
SEM_SPEC = pl.BlockSpec(memory_space=pltpu.SEMAPHORE)


def _params(*sem):
    return pltpu.CompilerParams(dimension_semantics=sem or None, vmem_limit_bytes=VMEM_LIMIT)


def _silu(z):
    return z * jax.nn.sigmoid(z)


def _silu_and_slope(z):
    s = jax.nn.sigmoid(z)
    return z * s, s * (1.0 + z * (1.0 - s))


def _my_place():
    x, y, c = lax.axis_index("x"), lax.axis_index("y"), lax.axis_index("c")
    return x, y, c, 4 * x + 2 * y + c


def _peer(x, y, c, k):
    px, py, pc = x ^ (k >> 2 & 1), y ^ (k >> 1 & 1), c ^ (k & 1)
    return (px, py, pc), 4 * px + 2 * py + pc


def _comm_call(name, arrays, sems, new_sems, body, after=(), token=False):
    na, ns, nn, nf = len(arrays), len(sems), len(new_sems), len(after)

    def kern(*refs):
        ins, outs = refs[:na + ns + nf], refs[na + ns + nf:]
        body(ins[:na], ins[na:na + ns], outs[:nn])
        if token:
            outs[nn + na][...] = jnp.zeros((8, 128), f32)

    out_shape = ([pltpu.SemaphoreType.DMA(s) for s in new_sems] + [pltpu.HBM(a.shape, a.dtype) for a in arrays]
                 + ([jax.ShapeDtypeStruct((8, 128), f32)] if token else []))
    out_specs = [SEM_SPEC] * nn + [HBM_SPEC] * na + ([pl.BlockSpec(memory_space=pltpu.VMEM)] if token else [])
    res = pl.pallas_call(
        kern, name=name, out_shape=tuple(out_shape),
        in_specs=[HBM_SPEC] * na + [SEM_SPEC] * ns + [ANY_SPEC] * nf, out_specs=tuple(out_specs),
        input_output_aliases={t: nn + t for t in range(na)},
        compiler_params=pltpu.CompilerParams(has_side_effects=pltpu.SideEffectType.DATAFLOW_SIDE_EFFECTING),
    )(*[pltpu.with_memory_space_constraint(a, pltpu.HBM) for a in arrays], *sems, *after)
    return list(res[:nn]), list(res[nn:nn + na]), (res[nn + na] if token else None)


def _remote(src, dst, send_sem, recv_sem, device):
    return pltpu.make_async_remote_copy(src_ref=src, dst_ref=dst, send_sem=send_sem, recv_sem=recv_sem,
                                        device_id=device, device_id_type=MESH)


def _gather_start(name, src):
    def body(a, s, new):
        (src, land), (send, recv) = a, new
        x, y, c, me = _my_place()
        pltpu.make_async_copy(src, land.at[me], recv.at[0]).start()
        for k in range(1, NDEV):
            peer, _ = _peer(x, y, c, k)
            _remote(src, land.at[me], send.at[k], recv.at[k], peer).start()

    land = lax.empty((NDEV,) + src.shape, src.dtype)
    (send, recv), (src, land), token = _comm_call(name, [src, land], [], [(NDEV,), (NDEV,)], body, token=True)
    return send, recv, src, land, token


def _gather_wait(name, src, land, send, recv, after):
    def body(a, s, new):
        (src, land), (send, recv) = a, s
        x, y, c, me = _my_place()
        pltpu.make_async_copy(src, land.at[me], recv.at[0]).wait()
        for k in range(1, NDEV):
            peer, slot = _peer(x, y, c, k)
            _remote(src, land.at[slot], send.at[k], recv.at[k], peer).wait_recv()
        for k in range(1, NDEV):
            peer, _ = _peer(x, y, c, k)
            _remote(src, land.at[me], send.at[k], recv.at[k], peer).wait_send()

    return _comm_call(name, [src, land], [send, recv], [], body, after=after)[1][1]


SAME_CORE = (2, 4, 6)
VIA_SIBLING = (3, 5, 7)


def _weights_forward(name, land, recv, after):
    def body(a, s, new):
        (land,), (recv,), (fsend, frecv) = a, s, new
        x, y, c, me = _my_place()
        sibling, _ = _peer(x, y, c, 1)
        for k in SAME_CORE:
            peer, slot = _peer(x, y, c, k)
            _remote(land.at[slot], land.at[slot], fsend.at[k], recv.at[k], peer).wait_recv()
            _remote(land.at[slot], land.at[slot], fsend.at[k], frecv.at[k ^ 1], sibling).start()

    return _comm_call(name, [land], [recv], [(NDEV,), (NDEV,)], body, after=after)


def _weights_wait(name, land, send, recv, fsend, frecv, after):
    def body(a, s, new):
        (land,), (send, recv, fsend, frecv) = a, s
        x, y, c, me = _my_place()
        sibling, sib_slot = _peer(x, y, c, 1)
        _remote(land.at[sib_slot], land.at[sib_slot], send.at[1], recv.at[1], sibling).wait_recv()
        for k in VIA_SIBLING:
            _, slot = _peer(x, y, c, k)
            _remote(land.at[slot], land.at[slot], fsend.at[k ^ 1], frecv.at[k], sibling).wait_recv()
        for k in (1,) + SAME_CORE:
            peer, _ = _peer(x, y, c, k)
            _remote(land.at[me], land.at[me], send.at[k], recv.at[k], peer).wait_send()
        for k in SAME_CORE:
            _, slot = _peer(x, y, c, k)
            _remote(land.at[slot], land.at[slot], fsend.at[k], frecv.at[k ^ 1], sibling).wait_send()

    return _comm_call(name, [land], [send, recv, fsend, frecv], [], body, after=after)[1][0]


def _diag_relay(x, y, c):
    slot = 4 * (x ^ (1 - c)) + 2 * (y ^ c) + c
    return slot, (x ^ c, y ^ (1 - c), c)


def _w_in_start(land, after):
    def body(a, s, new):
        (land,), (send, recv) = a, new
        x, y, c, me = _my_place()
        for k in (1, 2, 4):
            peer, _ = _peer(x, y, c, k)
            _remote(land.at[me], land.at[me], send.at[k], recv.at[k], peer).start()

    (send, recv), (land,), token = _comm_call("w_in_start", [land], [], [(NDEV,), (NDEV,)], body, after=after, token=True)
    return send, recv, land, token


def _w_in_sibling(land, recv, after):
    def body(a, s, new):
        (land,), (recv,) = a, s
        x, y, c, me = _my_place()
        sibling, slot = _peer(x, y, c, 1)
        _remote(land.at[slot], land.at[slot], recv.at[1], recv.at[1], sibling).wait_recv()

    return _comm_call("w_in_sibling", [land], [recv], [], body, after=after)[1][0]


def _w_in_relay(land, recv, after):
    def body(a, s, new):
        (land,), (recv,), (fsend, frecv) = a, s, new
        x, y, c, me = _my_place()
        sibling, _ = _peer(x, y, c, 1)
        for k in (2, 4):
            peer, slot = _peer(x, y, c, k)
            _remote(land.at[slot], land.at[slot], fsend.at[k], recv.at[k], peer).wait_recv()
        slot, target = _diag_relay(x, y, c)
        _remote(land.at[slot], land.at[slot], fsend.at[6], frecv.at[6], target).start()
        for k in (2, 4):
            _, slot = _peer(x, y, c, k)
            _remote(land.at[slot], land.at[slot], fsend.at[k], frecv.at[k ^ 1], sibling).start()

    (fsend, frecv), (land,), _ = _comm_call("w_in_relay", [land], [recv], [(NDEV,), (NDEV,)], body, after=after)
    return fsend, frecv, land


def _w_in_forwarded(land, frecv, after):
    def body(a, s, new):
        (land,), (frecv,) = a, s
        x, y, c, me = _my_place()
        sibling, _ = _peer(x, y, c, 1)
        for k in (3, 5):
            _, slot = _peer(x, y, c, k)
            _remote(land.at[slot], land.at[slot], frecv.at[k], frecv.at[k], sibling).wait_recv()

    return _comm_call("w_in_forwarded", [land], [frecv], [], body, after=after)[1][0]


def _w_in_diag(land, land_o, frecv, after):
    def body(a, s, new):
        (land, land_o), (frecv,), (dsend, drecv, osend, orecv) = a, s, new
        x, y, c, me = _my_place()
        sibling, _ = _peer(x, y, c, 1)
        peer, slot = _peer(x, y, c, 6)
        _remote(land.at[slot], land.at[slot], dsend.at[6], frecv.at[6], peer).wait_recv()
        _remote(land.at[slot], land.at[slot], dsend.at[6], drecv.at[7], sibling).start()
        for k in (1,) + SAME_CORE:
            peer, _ = _peer(x, y, c, k)
            _remote(land_o.at[me], land_o.at[me], osend.at[k], orecv.at[k], peer).start()

    sems, (land, land_o), _ = _comm_call("w_in_diag", [land, land_o], [frecv], [(NDEV,)] * 4, body, after=after)
    return sems, land, land_o


def _w_in_finish(land, send, fsend, dsend, drecv, after):
    def body(a, s, new):
        (land,), (send, fsend, dsend, drecv) = a, s
        x, y, c, me = _my_place()
        sibling, _ = _peer(x, y, c, 1)
        _, slot = _peer(x, y, c, 7)
        _remote(land.at[slot], land.at[slot], dsend.at[6], drecv.at[7], sibling).wait_recv()
        for k in (1, 2, 4):
            peer, _ = _peer(x, y, c, k)
            _remote(land.at[me], land.at[me], send.at[k], send.at[k], peer).wait_send()
        for k in (2, 4, 6):
            _, slot = _peer(x, y, c, k)
            _remote(land.at[slot], land.at[slot], fsend.at[k], fsend.at[k], sibling).wait_send()
        _, slot = _peer(x, y, c, 6)
        _remote(land.at[slot], land.at[slot], dsend.at[6], dsend.at[6], sibling).wait_send()

    return _comm_call("w_in_finish", [land], [send, fsend, dsend, drecv], [], body, after=after)[1][0]


def _cast_to_slot(name, w, me_arr, tr=512):
    R, Cc = w.shape
    tr = min(tr, R)

    def body(me_ref, w_ref, o_ref):
        o_ref[...] = w_ref[...].astype(bf16)

    grid_spec = pltpu.PrefetchScalarGridSpec(
        num_scalar_prefetch=1, grid=(R // tr,), in_specs=[pl.BlockSpec((tr, Cc), lambda i, me: (i, 0))],
        out_specs=pl.BlockSpec((None, tr, Cc), lambda i, me: (me[0], i, 0)))
    return pl.pallas_call(body, name=name, out_shape=jax.ShapeDtypeStruct((NDEV, R, Cc), bf16), grid_spec=grid_spec,
                          compiler_params=_params("parallel"))(me_arr, w)


def _in_proj_part(name, h, land, proj, me_arr, k0, kstep, nk, tm=512):
    S, D = h.shape
    C = land.shape[2]
    tm = min(tm, S)

    def body(me_ref, a_ref, b_ref, *rest):
        rest[-1][...] = jnp.dot(a_ref[...], b_ref[...], preferred_element_type=f32)

    slot = lambda j, me: me[0] ^ (k0 + kstep * j)
    args = [h, land] + ([] if proj is None else [proj])
    grid_spec = pltpu.PrefetchScalarGridSpec(
        num_scalar_prefetch=1, grid=(nk, S // tm),
        in_specs=[pl.BlockSpec((tm, D), lambda j, i, me: (i, 0)),
                  pl.BlockSpec((None, D, C), lambda j, i, me: (slot(j, me), 0, 0))] + [ANY_SPEC] * (len(args) - 2),
        out_specs=pl.BlockSpec((tm, C), lambda j, i, me: (i, slot(j, me))))
    return pl.pallas_call(
        body, name=name, out_shape=jax.ShapeDtypeStruct((S, NDEV * C), f32), grid_spec=grid_spec,
        input_output_aliases={} if proj is None else {3: 0}, compiler_params=_params("arbitrary", "arbitrary"),
    )(me_arr, *args)


NCHIP = NDEV // 2


def _pairs_start(name, srcs):
    n = len(srcs)
    npairs = [src.shape[0] // 2 for src in srcs]

    def body(a, s, new):
        x, y, c, me = _my_place()
        sibling, _ = _peer(x, y, c, 1)
        for t in range(n):
            src, pair, send, recv = a[t], a[n + t], new[t], new[n + t]
            for i in range(npairs[t]):
                _remote(src.at[2 * i + 1 - c], pair.at[i], send.at[i], recv.at[i], sibling).start()

    pairs = [lax.empty((npairs[t],) + srcs[t].shape[1:], srcs[t].dtype) for t in range(n)]
    sems, arrays, token = _comm_call(name, list(srcs) + pairs, [], [(m,) for m in npairs] * 2, body, token=True)
    return [(sems[t], sems[n + t], arrays[t], arrays[n + t]) for t in range(n)], token


def _pairs_wait(name, groups, after):
    n = len(groups)

    def body(a, s, new):
        x, y, c, me = _my_place()
        sibling, _ = _peer(x, y, c, 1)
        for t in range(n):
            src, pair, send, recv = a[t], a[n + t], s[t], s[n + t]
            for i in range(pair.shape[0]):
                cp = _remote(src.at[2 * i + 1 - c], pair.at[i], send.at[i], recv.at[i], sibling)
                cp.wait_recv()
                cp.wait_send()

    arrays = _comm_call(name, [g[2] for g in groups] + [g[3] for g in groups],
                        [g[0] for g in groups] + [g[1] for g in groups], [], body, after=after)[1]
    return [(arrays[t], arrays[n + t]) for t in range(n)]


def _pair_start(name, src):
    ((send, recv, src, pair),), token = _pairs_start(name, [src])
    return send, recv, src, pair, token


def _pair_wait(name, src, pair, send, recv, after):
    return _pairs_wait(name, [(send, recv, src, pair)], after)[0]


def _pair_sum(name, src, pair, core, tr=1024):
    npair, R, Cc = pair.shape
    tr = min(tr, R)

    def body(core_ref, a_ref, b_ref, o_ref):
        o_ref[...] = (a_ref[...].astype(f32) + b_ref[...].astype(f32)).astype(o_ref.dtype)

    grid_spec = pltpu.PrefetchScalarGridSpec(
        num_scalar_prefetch=1, grid=(npair, R // tr),
        in_specs=[pl.BlockSpec((None, tr, Cc), lambda i, r, core: (2 * i + core[0], r, 0)),
                  pl.BlockSpec((None, tr, Cc), lambda i, r, core: (i, r, 0))],
        out_specs=pl.BlockSpec((None, tr, Cc), lambda i, r, core: (i, r, 0)))
    return pl.pallas_call(body, name=name, out_shape=jax.ShapeDtypeStruct(pair.shape, pair.dtype),
                          grid_spec=grid_spec, compiler_params=_params("parallel", "parallel"))(core, src, pair)


def _owner_chip(first, i):
    q = first // 2 + i
    return q >> 1 & 1, q & 1


def _chips_start(name, groups, rows=None, after=()):
    n = len(groups)
    row_of = [pl.ds(*(rows or (0, g[0].shape[1]))) for g in groups]

    def body(a, s, new):
        x, y, c, me = _my_place()
        for t, (_, _, first) in enumerate(groups):
            sums, land, send, recv = a[t], a[n + t], new[t], new[n + t]
            for i in range(sums.shape[0]):
                ox, oy = _owner_chip(first, i)

                @pl.when((x != ox) | (y != oy))
                def _():
                    _remote(sums.at[i, row_of[t]], land.at[2 * x + y, row_of[t]], send.at[i], recv.at[2 * x + y],
                            (ox, oy, c)).start()

    sems, arrays, token = _comm_call(name, [g[0] for g in groups] + [g[1] for g in groups], [],
                                     [(g[0].shape[0],) for g in groups] + [(NCHIP,)] * n, body, after=after, token=True)
    return [(sems[t], sems[n + t], arrays[t], arrays[n + t]) for t in range(n)], token


def _chips_wait(name, groups, firsts, after, rows=None):
    n = len(groups)
    row_of = [pl.ds(*(rows or (0, g[2].shape[1]))) for g in groups]

    def body(a, s, new):
        x, y, c, me = _my_place()
        for t in range(n):
            sums, land, send, recv, first = a[t], a[n + t], s[t], s[n + t], firsts[t]
            npair = sums.shape[0]
            mine = (me >= first) & (me < first + 2 * npair)
            for i in range(npair):
                ox, oy = _owner_chip(first, i)

                @pl.when((x != ox) | (y != oy))
                def _():
                    _remote(sums.at[i, row_of[t]], land.at[2 * x + y, row_of[t]], send.at[i], recv.at[2 * x + y],
                            (ox, oy, c)).wait_send()
            for q in range(NCHIP):
                @pl.when(mine & (2 * x + y != q))
                def _():
                    _remote(sums.at[0, row_of[t]], land.at[q, row_of[t]], send.at[0], recv.at[q],
                            (q >> 1, q & 1, c)).wait_recv()

    arrays = _comm_call(name, [g[2] for g in groups] + [g[3] for g in groups],
                        [g[0] for g in groups] + [g[1] for g in groups], [], body, after=after)[1]
    return [(arrays[t], arrays[n + t]) for t in range(n)]


def _chip_start(name, sums, land, first, rows=None, after=()):
    ((send, recv, sums, land),), token = _chips_start(name, [(sums, land, first)], rows, after)
    return send, recv, sums, land, token


def _chip_wait(name, sums, land, send, recv, first, after, rows=None):
    return _chips_wait(name, [(send, recv, sums, land)], [first], after, rows)[0]


def _matmul(a, b, *, name, out_dtype, ta=False, tb=False, b_slots=False, out_slots=0, b_cols=None,
            tm=1024, tn=1024, tk=2048, dep=None):
    M, K = (a.shape[1], a.shape[0]) if ta else a.shape
    col0 = 0
    if b_slots:
        slab = b.shape[2]
        N = b.shape[1] if tb else b.shape[0] * slab
        assert (K if tb else N) == b.shape[0] * slab
    elif b_cols is not None:
        assert not tb
        col0, N = b_cols
    else:
        N = b.shape[0] if tb else b.shape[1]
    tm, tn, tk = min(tm, M), min(tn, N), min(tk, K)
    if b_slots:
        if tb:
            tk = min(tk, slab)
        else:
            tn = min(tn, slab)
    if out_slots:
        tn = min(tn, N // out_slots)
    nm, nn, nk = M // tm, N // tn, K // tk
    assert (nm * tm, nn * tn, nk * tk) == (M, N, K) and col0 % tn == 0, (name, M, N, K, tm, tn, tk)
    j0 = col0 // tn

    a_spec = pl.BlockSpec((tk, tm), lambda i, j, k: (k, i)) if ta else pl.BlockSpec((tm, tk), lambda i, j, k: (i, k))
    if b_slots and tb:
        per = slab // tk
        b_spec = pl.BlockSpec((None, tn, tk), lambda i, j, k: (k // per, j, k % per))
    elif b_slots:
        per = slab // tn
        b_spec = pl.BlockSpec((None, tk, tn), lambda i, j, k: (j // per, k, j % per))
    elif tb:
        b_spec = pl.BlockSpec((tn, tk), lambda i, j, k: (j, k))
    else:
        b_spec = pl.BlockSpec((tk, tn), lambda i, j, k: (k, j + j0))
    if out_slots:
        per_o = (N // out_slots) // tn
        o_spec = pl.BlockSpec((None, tm, tn), lambda i, j, k: (j // per_o, i, j % per_o))
        out_shape = jax.ShapeDtypeStruct((out_slots, M, N // out_slots), out_dtype)
    else:
        o_spec = pl.BlockSpec((tm, tn), lambda i, j, k: (i, j))
        out_shape = jax.ShapeDtypeStruct((M, N), out_dtype)
    dims = (((0 if ta else 1,), (1 if tb else 0,)), ((), ()))
    deps = [] if dep is None else [dep]

    def body(a_ref, b_ref, *rest):
        o_ref = rest[len(deps)]
        prod = lax.dot_general(a_ref[...], b_ref[...], dims, preferred_element_type=f32)
        if nk == 1:
            o_ref[...] = prod.astype(out_dtype)
            return
        acc_ref = rest[len(deps) + 1]
        k = pl.program_id(2)

        @pl.when(k == 0)
        def _():
            acc_ref[...] = prod

        @pl.when((k > 0) & (k < nk - 1))
        def _():
            acc_ref[...] += prod

        @pl.when(k == nk - 1)
        def _():
            o_ref[...] = (acc_ref[...] + prod).astype(out_dtype)

    return pl.pallas_call(
        body, name=name, out_shape=out_shape, grid=(nm, nn, nk),
        in_specs=[a_spec, b_spec] + [ANY_SPEC] * len(deps), out_specs=o_spec,
        scratch_shapes=[pltpu.VMEM((tm, tn), f32)] if nk > 1 else [],
        compiler_params=_params("parallel", "parallel", "arbitrary"),
    )(a, b, *deps)


def _matmul_slabs_t(a_cols, a_slots, b, *, name, tm=512, tn=512, dep=None):
    M = a_cols.shape[0]
    n_slab, N, slab = b.shape
    n1, n2 = a_cols.shape[1] // slab, a_slots.shape[0]
    assert n1 + n2 == n_slab and a_slots.shape[1:] == (M, slab)
    tm, tn = min(tm, M), min(tn, N)
    deps = [] if dep is None else [dep]

    def body(a1_ref, a2_ref, b_ref, *rest):
        o_ref = rest[len(deps)]
        acc = None
        for s in range(n_slab):
            lhs = a1_ref[:, s * slab:(s + 1) * slab] if s < n1 else a2_ref[s - n1]
            prod = lax.dot_general(lhs, b_ref[s], (((1,), (1,)), ((), ())), preferred_element_type=f32)
            acc = prod if acc is None else acc + prod
        o_ref[...] = acc

    return pl.pallas_call(
        body, name=name, out_shape=jax.ShapeDtypeStruct((M, N), f32), grid=(M // tm, N // tn),
        in_specs=[pl.BlockSpec((tm, n1 * slab), lambda i, j: (i, 0)), pl.BlockSpec((n2, tm, slab), lambda i, j: (0, i, 0)),
                  pl.BlockSpec((n_slab, tn, slab), lambda i, j: (0, j, 0))] + [ANY_SPEC] * len(deps),
        out_specs=pl.BlockSpec((tm, tn), lambda i, j: (i, j)), compiler_params=_params("parallel", "parallel"),
    )(a_cols, a_slots, b, *deps)


def _ada_exchange(c_blk, cw_slab, w_ada, b_cols):
    nblk = c_blk.shape[0]
    D, W = w_ada.shape
    CW = cw_slab.shape[1]

    def body(c_ref, cw_ref, w_ref, b_ref, mod_ref, call_ref, cwg_ref, msend, send_sems, recv_sems):
        x, y, c, me = _my_place()
        call_ref[me] = _silu(c_ref[...])
        cwg_ref[me] = cw_ref[...]
        first = []
        for k in range(1, NDEV):
            peer, _ = _peer(x, y, c, k)
            first.append(_remote(call_ref.at[me], call_ref.at[me], send_sems.at[0, k], recv_sems.at[0, k], peer))
            first.append(_remote(cwg_ref.at[me], cwg_ref.at[me], send_sems.at[1, k], recv_sems.at[1, k], peer))
        for cp in first:
            cp.start()
        for k in range(1, NDEV):
            peer, slot = _peer(x, y, c, k)
            _remote(call_ref.at[slot], call_ref.at[slot], send_sems.at[0, k], recv_sems.at[0, k], peer).wait_recv()
            _remote(cwg_ref.at[slot], cwg_ref.at[slot], send_sems.at[1, k], recv_sems.at[1, k], peer).wait_recv()
        mod = jnp.broadcast_to(b_ref[...], (NDEV, W))
        for r in range(nblk):
            mod = mod + lax.dot_general(call_ref[:, r, :], w_ref[r * 128:(r + 1) * 128, :], (((1,), (0,)), ((), ())),
                                        preferred_element_type=f32, precision=lax.Precision.HIGHEST)
        row = lax.broadcasted_iota(jnp.int32, (NDEV, 1), 0)
        pick = lambda j: jnp.broadcast_to(jnp.sum(jnp.where(row == j, mod, 0.0), axis=0, keepdims=True), (8, W))
        mod_ref[me] = pick(me)
        second = []
        for k in range(1, NDEV):
            peer, slot = _peer(x, y, c, k)
            msend[k] = pick(slot)
            second.append(_remote(msend.at[k], mod_ref.at[me], send_sems.at[2, k], recv_sems.at[2, k], peer))
        for cp in second:
            cp.start()
        for k in range(1, NDEV):
            peer, slot = _peer(x, y, c, k)
            _remote(msend.at[k], mod_ref.at[slot], send_sems.at[2, k], recv_sems.at[2, k], peer).wait_recv()
        for cp in first + second:
            cp.wait_send()

    vmem = pl.BlockSpec(memory_space=pltpu.VMEM)
    return pl.pallas_call(
        body, name="ada_exchange",
        out_shape=(jax.ShapeDtypeStruct((NDEV, 8, W), f32), jax.ShapeDtypeStruct((NDEV, nblk, 128), f32),
                   jax.ShapeDtypeStruct((NDEV, 8, CW), f32)),
        in_specs=[vmem] * 4, out_specs=(vmem, vmem, vmem),
        scratch_shapes=[pltpu.VMEM((NDEV, 8, W), f32), pltpu.SemaphoreType.DMA((3, NDEV)),
                        pltpu.SemaphoreType.DMA((3, NDEV))],
        compiler_params=_params(),
    )(c_blk, cw_slab, w_ada, b_cols)


def _prenorm(x, scale, shift, g_pre, dep, tr=512):
    S, D = x.shape
    tr = min(tr, S)

    def body(x_ref, sc_ref, sh_ref, g_ref, dep_ref, h_ref):
        xv = x_ref[...]
        r = lax.rsqrt(jnp.mean(xv * xv, axis=-1, keepdims=True) + EPS)
        h_ref[...] = ((xv * r) * g_ref[...] * (1.0 + sc_ref[...]) + sh_ref[...]).astype(bf16)

    row = pl.BlockSpec((tr, D), lambda i: (i, 0))
    vec = pl.BlockSpec((1, D), lambda i: (0, 0))
    return pl.pallas_call(body, name="prenorm", out_shape=jax.ShapeDtypeStruct((S, D), bf16), grid=(S // tr,),
                          in_specs=[row, vec, vec, vec, ANY_SPEC], out_specs=row, compiler_params=_params("parallel"))(
                              x, scale, shift, g_pre, dep)


def _ext_rows(i, tr, S):
    g = lax.broadcasted_iota(jnp.int32, (tr + 16, 1), 0) + (i * tr - 8)
    return (g >= 0) & (g < S)


def _halo_specs(tr, S, C, col):
    nb8 = S // 8
    main = pl.BlockSpec((tr, C), lambda i: (i, col))
    prev = pl.BlockSpec((8, C), lambda i: (jnp.maximum(i * (tr // 8) - 1, 0), col))
    nxt = pl.BlockSpec((8, C), lambda i: (jnp.minimum((i + 1) * (tr // 8), nb8 - 1), col))
    return prev, main, nxt


def _conv_fwd(proj, conv_w, conv_b, g_conv, tr=512):
    S, C = proj.shape[0], proj.shape[1] // 8
    tr = min(tr, S)

    def body(up, um, un, cp, cm, cn, bg_ref, zc_ref, w_ref, cb_ref, g_ref, o_ref):
        i = pl.program_id(0)
        exists = _ext_rows(i, tr, S)
        u = jnp.concatenate([up[...], um[...], un[...]], axis=0)
        cg = jnp.concatenate([cp[...], cm[...], cn[...]], axis=0)
        t = jnp.where(exists, cg * u, 0.0)
        t_before = pltpu.roll(t, 1, 0)[8:tr + 8]
        t_after = pltpu.roll(t, tr + 15, 0)[8:tr + 8]
        w = w_ref[...]
        cv = w[0:1] * t_before + w[1:2] * t[8:tr + 8] + w[2:3] * t_after + cb_ref[...]
        yc = bg_ref[...] * cv
        rc = lax.rsqrt(jnp.mean(yc * yc, axis=-1, keepdims=True) + EPS)
        o_ref[...] = ((yc * rc) * g_ref[...] * _silu(zc_ref[...])).astype(bf16)

    u_specs = _halo_specs(tr, S, C, 0)
    c_specs = _halo_specs(tr, S, C, 2)
    vec = pl.BlockSpec((1, C), lambda i: (0, 0))
    return pl.pallas_call(
        body, name="conv_fwd", out_shape=jax.ShapeDtypeStruct((S, 2 * C), bf16), grid=(S // tr,),
        in_specs=[*u_specs, *c_specs, pl.BlockSpec((tr, C), lambda i: (i, 1)), pl.BlockSpec((tr, C), lambda i: (i, 3)),
                  pl.BlockSpec((8, C), lambda i: (0, 0)), vec, vec],
        out_specs=pl.BlockSpec((tr, C), lambda i: (i, 0)), compiler_params=_params("parallel"),
    )(proj, proj, proj, proj, proj, proj, proj, proj, conv_w, conv_b, g_conv)


def _branch_geometry(S, r, inter):
    L = S // r * inter
    nq = min(128, L)
    nk = min(nq + 2 * HALF_WIN * inter, L)
    assert L % nq == 0 and (L == nk or L >= nq + 2 * HALF_WIN * inter)
    return L, nq, nk, L // nq


QUAD = 4


def _to_quad(dst, src, S):
    n = S // QUAD
    for rho in range(QUAD):
        dst[pl.ds(rho * n, n), :] = src[pl.ds(rho, n, stride=QUAD), :]


def _block_rows(idx, r, inter, S, L, nq, nk, nblk):
    rho, qb = (0, idx) if r == 1 else (idx // nblk, idx % nblk)
    i0 = qb * nq
    ws = jnp.clip(i0 - HALF_WIN * inter, 0, L - nk)
    if r == 1:
        return pl.ds(pl.multiple_of(i0, 8), nq), pl.ds(pl.multiple_of(ws, 8), nk), i0 - ws
    assert r % (QUAD * inter) == 0
    step = r // QUAD // inter
    base = (rho % QUAD) * (S // QUAD) + rho // QUAD
    if step == 1:
        return pl.ds(pl.multiple_of(base + i0, 8), nq), pl.ds(pl.multiple_of(base + ws, 8), nk), i0 - ws
    return pl.ds(base + step * i0, nq, stride=step), pl.ds(base + step * ws, nk, stride=step), i0 - ws


N_CASES = 3
SCALE = HEAD_DIM ** -0.5
ATTN_UNROLL = 16


def _bias_shape(S):
    shapes = [_branch_geometry(S, r, inter)[1:3] for _, r, inter in BRANCHES]
    return (len(BRANCHES) * N_CASES * 2, max(nq for nq, _ in shapes), max(nk for _, nk in shapes))


def _bias_index(b, case, head):
    return (b * N_CASES + case) * 2 + head


def _fill_bias(bias_scr, sl_ref, S):
    sl = sl_ref[...]
    slope = (sl[0:1, 0:1], sl[0:1, HEAD_DIM:HEAD_DIM + 1])
    for b, (_, r, inter) in enumerate(BRANCHES):
        L, nq, nk, nblk = _branch_geometry(S, r, inter)
        rel = lax.broadcasted_iota(jnp.int32, (nq, nk), 0) - lax.broadcasted_iota(jnp.int32, (nq, nk), 1)
        for case in range(N_CASES):
            d = jnp.abs(rel + case * HALF_WIN)
            valid = d <= HALF_WIN * inter
            if inter > 1:
                valid = valid & (jnp.bitwise_and(d, inter - 1) == 0)
            dist = d.astype(f32) * float(r // inter)
            for head in range(2):
                bias_scr[_bias_index(b, case, head), 0:nq, 0:nk] = jnp.where(valid, -slope[head] * dist, NEG_INF)


def _bias_tiles(slopes, S, dep):
    npair = slopes.shape[0]
    shape = _bias_shape(S)

    def body(sl_ref, dep_ref, o_ref):
        _fill_bias(o_ref, sl_ref, S)

    return pl.pallas_call(
        body, name="bias_tiles", out_shape=jax.ShapeDtypeStruct((npair,) + shape, f32), grid=(npair,),
        in_specs=[pl.BlockSpec((None, 8, PAIR), lambda p: (p, 0, 0)), ANY_SPEC],
        out_specs=pl.BlockSpec((None,) + shape, lambda p: (p, 0, 0, 0)), compiler_params=_params("parallel"),
    )(slopes, dep)


def _head_slopes(n_heads):
    slopes = 2.0 ** (-8.0 * jnp.arange(1, n_heads + 1, dtype=f32) / n_heads)
    return jnp.broadcast_to(jnp.repeat(slopes.reshape(n_heads // 2, 2), HEAD_DIM, axis=1)[:, None, :],
                            (n_heads // 2, 8, PAIR))


def _attn_fwd(proj, bias):
    S, C = proj.shape[0], proj.shape[1] // 8
    npair = C // PAIR

    def body(q_ref, k_ref, v_ref, bias_scr, o_ref, lse_ref, m_scr, l_scr, a_scr, q4_scr, k4_scr, v4_scr):
        lane = lax.broadcasted_iota(jnp.int32, (1, PAIR), 1)
        first = lane < HEAD_DIM
        for dst, src in ((q4_scr, q_ref), (k4_scr, k_ref), (v4_scr, v_ref)):
            _to_quad(dst, src, S)

        for b, (_, r, inter) in enumerate(BRANCHES):
            L, nq, nk, nblk = _branch_geometry(S, r, inter)
            qs, ks, vs = (q_ref, k_ref, v_ref) if r == 1 else (q4_scr, k4_scr, v4_scr)

            def step(idx, carry, b=b, r=r, L=L, nq=nq, nk=nk, nblk=nblk, qs=qs, ks=ks, vs=vs):
                qrows, krows, off = _block_rows(idx, r, inter, S, L, nq, nk, nblk)
                case = off // HALF_WIN
                q2 = qs[qrows, :] * SCALE
                k2 = ks[krows, :].astype(bf16)
                v2 = vs[krows, :].astype(bf16)
                ms, accs = [], []
                for hh in range(2):
                    mine = first if hh == 0 else ~first
                    qh = jnp.where(mine, q2, 0.0).astype(bf16)
                    s = lax.dot_general(qh, k2, (((1,), (1,)), ((), ())), preferred_element_type=f32)
                    s = s + bias_scr[_bias_index(b, case, hh), 0:nq, 0:nk]
                    m = jnp.max(s, axis=-1, keepdims=True)
                    p = jnp.exp(s - m).astype(bf16)
                    vh = jnp.where(mine, v2, jnp.ones_like(v2))
                    ms.append(m)
                    accs.append(jnp.dot(p, vh, preferred_element_type=f32))
                m_scr[b, qrows, :] = jnp.where(first, ms[0], ms[1])
                a_scr[b, qrows, :] = jnp.where(first, accs[0], accs[1])
                l_scr[b, qrows, :] = jnp.where(first, accs[1], accs[0])
                return carry

            lax.fori_loop(0, S // nq, step, 0, unroll=min(ATTN_UNROLL, S // nq))

        n4 = S // QUAD
        ch = min(256, n4)
        nch = n4 // ch

        def merge(i, carry):
            rho, part = i // nch, i % nch
            sorted_rows = pl.ds(pl.multiple_of(rho * n4 + part * ch, 8), ch)
            token_rows = pl.ds(rho + QUAD * part * ch, ch, stride=QUAD)
            rows = (token_rows,) + (sorted_rows,) * (len(BRANCHES) - 1)
            ms = [m_scr[b, rows[b], :] for b in range(len(BRANCHES))]
            m = functools.reduce(jnp.maximum, ms)
            l = jnp.zeros((ch, PAIR), f32)
            acc = jnp.zeros((ch, PAIR), f32)
            for b in range(len(BRANCHES)):
                w = jnp.exp(ms[b] - m)
                l = l + w * pltpu.roll(l_scr[b, rows[b], :], HEAD_DIM, 1)
                acc = acc + w * a_scr[b, rows[b], :]
            o_ref[token_rows, :] = acc / l
            lse_ref[token_rows, :] = m + jnp.log(l)
            return carry

        lax.fori_loop(0, QUAD * nch, merge, 0, unroll=2)

    blk = lambda part: pl.BlockSpec((S, PAIR), lambda p: (0, part * npair + p))
    out = pl.BlockSpec((S, PAIR), lambda p: (0, p))
    return pl.pallas_call(
        body, name="attn_fwd",
        out_shape=(jax.ShapeDtypeStruct((S, C), f32), jax.ShapeDtypeStruct((S, C), f32)), grid=(npair,),
        in_specs=[blk(4), blk(5), blk(6), pl.BlockSpec((None,) + _bias_shape(S), lambda p: (p, 0, 0, 0))],
        out_specs=(out, out),
        scratch_shapes=[pltpu.VMEM((3, S, PAIR), f32)] * 3 + [pltpu.VMEM((S, PAIR), f32)] * 3,
        compiler_params=_params("parallel"),
    )(proj, proj, proj, bias)


def _attn_post(ycat, o, proj, g_attn, tr=512):
    S, C = o.shape
    tr = min(tr, S)

    def body(y_ref, o_ref, z_ref, g_ref, out_ref):
        del y_ref
        ov = o_ref[...]
        ra = lax.rsqrt(jnp.mean(ov * ov, axis=-1, keepdims=True) + EPS)
        out_ref[...] = ((ov * ra) * g_ref[...] * _silu(z_ref[...])).astype(bf16)

    return pl.pallas_call(
        body, name="attn_post", out_shape=jax.ShapeDtypeStruct(ycat.shape, ycat.dtype), grid=(S // tr,),
        in_specs=[HBM_SPEC, pl.BlockSpec((tr, C), lambda i: (i, 0)), pl.BlockSpec((tr, C), lambda i: (i, 7)),
                  pl.BlockSpec((1, C), lambda i: (0, 0))],
        out_specs=pl.BlockSpec((tr, C), lambda i: (i, 1)), input_output_aliases={0: 0},
        compiler_params=_params("arbitrary"),
    )(ycat, o, proj, g_attn)


def _residual_minus_target(x, target, dep, tr=512):
    S, D = x.shape
    tr = min(tr, S)

    def body(x_ref, t_ref, dep_ref, o_ref):
        o_ref[...] = x_ref[...] - t_ref[...]

    row = pl.BlockSpec((tr, D), lambda i: (i, 0))
    return pl.pallas_call(body, name="residual_minus_target", out_shape=jax.ShapeDtypeStruct((S, D), f32),
                          grid=(S // tr,), in_specs=[row, row, ANY_SPEC], out_specs=row,
                          compiler_params=_params("parallel"))(x, target, dep)


def _sandwich(y, x_minus_t, gate, g_post, tr=256):
    S, D = y.shape
    tr = min(tr, S)

    def body(y_ref, xt_ref, gate_ref, g_ref, dy_ref, dout_ref, sums_ref):
        i = pl.program_id(0)
        gate, g = gate_ref[...], g_ref[...]
        gg = gate * g
        yv = y_ref[...]
        rp = lax.rsqrt(jnp.mean(yv * yv, axis=-1, keepdims=True) + EPS)
        yhat = yv * rp
        err = xt_ref[...] + gg * yhat
        dout = err * (1.0 / D)
        dout_ref[...] = dout
        q = dout * yhat
        w = dout * gg
        dy_ref[...] = (rp * (w - yhat * jnp.sum(q * gg, axis=-1, keepdims=True) * (1.0 / D))).astype(bf16)
        loss = 0.5 * jnp.sum(jnp.mean(err * err, axis=-1, keepdims=True), axis=0, keepdims=True)
        q_sum = jnp.sum(q, axis=0, keepdims=True)
        row = lax.broadcasted_iota(jnp.int32, (8, D), 0)
        upd = jnp.where(row == 0, q_sum * g, jnp.where(row == 1, q_sum * gate, jnp.where(row == 2, loss, 0.0)))

        @pl.when(i == 0)
        def _():
            sums_ref[...] = upd

        @pl.when(i > 0)
        def _():
            sums_ref[...] += upd

    row = pl.BlockSpec((tr, D), lambda i: (i, 0))
    vec = pl.BlockSpec((1, D), lambda i: (0, 0))
    return pl.pallas_call(
        body, name="sandwich",
        out_shape=(jax.ShapeDtypeStruct((S, D), bf16), jax.ShapeDtypeStruct((S, D), f32), jax.ShapeDtypeStruct((8, D), f32)),
        grid=(S // tr,), in_specs=[row, row, vec, vec],
        out_specs=(row, row, pl.BlockSpec((8, D), lambda i: (0, 0))), compiler_params=_params("arbitrary"),
    )(y, x_minus_t, gate, g_post)


def _conv_bwd(proj, dycat, conv_w, conv_b, g_conv, dep, tr=256):
    S, C = proj.shape[0], proj.shape[1] // 8
    tr = min(tr, S)
    n = tr + 16

    def body(*refs):
        ins, (w_ref, cb_ref, g_ref, _, dp_ref, sums_ref) = refs[:15], refs[15:]
        i = pl.program_id(0)
        exists = _ext_rows(i, tr, S)
        u, bg, cg, zc, dyn = (jnp.concatenate([ins[3 * t][...], ins[3 * t + 1][...], ins[3 * t + 2][...]], axis=0)
                              for t in range(5))
        w = w_ref[...]
        t = jnp.where(exists, cg * u, 0.0)
        t_before, t_after = pltpu.roll(t, 1, 0), pltpu.roll(t, n - 1, 0)
        cv = w[0:1] * t_before + w[1:2] * t + w[2:3] * t_after + cb_ref[...]
        yc = bg * cv
        rc = lax.rsqrt(jnp.mean(yc * yc, axis=-1, keepdims=True) + EPS)
        yhat = yc * rc
        sz, dsz = _silu_and_slope(zc)
        wgt = dyn * g_ref[...] * sz
        dyc = rc * (wgt - yhat * jnp.mean(wgt * yhat, axis=-1, keepdims=True))
        dcv = jnp.where(exists, dyc * bg, 0.0)
        dt = w[0:1] * pltpu.roll(dcv, n - 1, 0) + w[1:2] * dcv + w[2:3] * pltpu.roll(dcv, 1, 0)
        mid = slice(8, tr + 8)
        dp_ref[:, 0:C] = (dt * cg)[mid].astype(bf16)
        dp_ref[:, C:2 * C] = (dyc * cv)[mid].astype(bf16)
        dp_ref[:, 2 * C:3 * C] = (dt * u)[mid].astype(bf16)
        dp_ref[:, 3 * C:4 * C] = (dyn * yhat * g_ref[...] * dsz)[mid].astype(bf16)
        colsum = lambda v: jnp.sum(v[mid], axis=0, keepdims=True)
        parts = [colsum(dyn * yhat * sz), colsum(dcv), colsum(dcv * t_before), colsum(dcv * t), colsum(dcv * t_after)]
        row = lax.broadcasted_iota(jnp.int32, (8, C), 0)
        upd = jnp.zeros((8, C), f32)
        for j, pj in enumerate(parts):
            upd = jnp.where(row == j, pj, upd)

        @pl.when(i == 0)
        def _():
            sums_ref[...] = upd

        @pl.when(i > 0)
        def _():
            sums_ref[...] += upd

    specs = []
    for col in range(4):
        specs += _halo_specs(tr, S, C, col)
    specs += _halo_specs(tr, S, C, 0)
    vec = pl.BlockSpec((1, C), lambda i: (0, 0))
    return pl.pallas_call(
        body, name="conv_bwd",
        out_shape=(jax.ShapeDtypeStruct((S, 4 * C), bf16), jax.ShapeDtypeStruct((8, C), f32)), grid=(S // tr,),
        in_specs=[*specs, pl.BlockSpec((8, C), lambda i: (0, 0)), vec, vec, ANY_SPEC],
        out_specs=(pl.BlockSpec((tr, 4 * C), lambda i: (i, 0)), pl.BlockSpec((8, C), lambda i: (0, 0))),
        compiler_params=_params("arbitrary"),
    )(*([proj] * 12), dycat, dycat, dycat, conv_w, conv_b, g_conv, dep)


def _attn_post_bwd(o, proj, dycat, g_attn, dep, tr=512):
    S, C = o.shape
    tr = min(tr, S)

    def body(o_ref, z_ref, dy_ref, g_ref, dep_ref, do_ref, dz_ref, sums_ref):
        i = pl.program_id(0)
        ov, zv, dyn = o_ref[...], z_ref[...], dy_ref[...]
        ra = lax.rsqrt(jnp.mean(ov * ov, axis=-1, keepdims=True) + EPS)
        ohat = ov * ra
        sz, dsz = _silu_and_slope(zv)
        wgt = dyn * g_ref[...] * sz
        do_ref[...] = ra * (wgt - ohat * jnp.mean(wgt * ohat, axis=-1, keepdims=True))
        dz_ref[...] = (dyn * ohat * g_ref[...] * dsz).astype(bf16)
        row = lax.broadcasted_iota(jnp.int32, (8, C), 0)
        upd = jnp.where(row == 0, jnp.sum(dyn * ohat * sz, axis=0, keepdims=True), 0.0)

        @pl.when(i == 0)
        def _():
            sums_ref[...] = upd

        @pl.when(i > 0)
        def _():
            sums_ref[...] += upd

    return pl.pallas_call(
        body, name="attn_post_bwd",
        out_shape=(jax.ShapeDtypeStruct((S, C), f32), jax.ShapeDtypeStruct((4, S, C), bf16),
                   jax.ShapeDtypeStruct((8, C), f32)),
        grid=(S // tr,),
        in_specs=[pl.BlockSpec((tr, C), lambda i: (i, 0)), pl.BlockSpec((tr, C), lambda i: (i, 7)),
                  pl.BlockSpec((tr, C), lambda i: (i, 1)), pl.BlockSpec((1, C), lambda i: (0, 0)), ANY_SPEC],
        out_specs=(pl.BlockSpec((tr, C), lambda i: (i, 0)), pl.BlockSpec((None, tr, C), lambda i: (3, i, 0)),
                   pl.BlockSpec((8, C), lambda i: (0, 0))),
        compiler_params=_params("arbitrary"),
    )(o, proj, dycat, g_attn, dep)


def _attn_bwd(proj, o, do, lse, bias, dqkvz, dep):
    S, C = o.shape
    npair = C // PAIR

    def body(q_ref, k_ref, v_ref, o_ref, do_ref, lse_ref, bias_scr, old_ref, dep_ref, dqkv_ref,
             acc_scr, dl_scr, quad_scr):
        lane = lax.broadcasted_iota(jnp.int32, (1, PAIR), 1)
        first = lane < HEAD_DIM
        ch = min(256, S)

        def prep(i, carry):
            rows = pl.ds(pl.multiple_of(i * ch, 8), ch)
            prod = do_ref[rows, :] * o_ref[rows, :]
            d0 = jnp.sum(jnp.where(first, prod, 0.0), axis=-1, keepdims=True)
            d1 = jnp.sum(jnp.where(first, 0.0, prod), axis=-1, keepdims=True)
            dl_scr[rows, :] = jnp.where(first, d0, d1)
            zero = jnp.zeros((ch, PAIR), f32)
            for order in range(2):
                for t in range(3):
                    acc_scr[order, t, rows, :] = zero
            return carry

        lax.fori_loop(0, S // ch, prep, 0, unroll=2)
        token_srcs = (q_ref, k_ref, v_ref, do_ref, lse_ref, dl_scr)
        for j, src in enumerate(token_srcs):
            _to_quad(quad_scr.at[j], src, S)

        for b, (_, r, inter) in enumerate(BRANCHES):
            L, nq, nk, nblk = _branch_geometry(S, r, inter)
            order = 0 if r == 1 else 1
            srcs = token_srcs if r == 1 else tuple(quad_scr.at[j] for j in range(6))

            def step(idx, carry, b=b, r=r, L=L, nq=nq, nk=nk, nblk=nblk, order=order, srcs=srcs):
                qs, ks, vs, dos, lses, dls = srcs
                dq_scr, dk_scr, dv_scr = (acc_scr.at[order, t] for t in range(3))
                qrows, krows, off = _block_rows(idx, r, inter, S, L, nq, nk, nblk)
                case = off // HALF_WIN
                q2 = qs[qrows, :] * SCALE
                k2 = ks[krows, :].astype(bf16)
                v2 = vs[krows, :].astype(bf16)
                do2 = dos[qrows, :]
                lse2 = lses[qrows, :]
                dl2 = dls[qrows, :]
                dq2 = jnp.zeros((nq, PAIR), f32)
                dk2 = jnp.zeros((nk, PAIR), f32)
                dv2 = jnp.zeros((nk, PAIR), f32)
                for hh in range(2):
                    mine = first if hh == 0 else ~first
                    lo = hh * HEAD_DIM
                    qh = jnp.where(mine, q2, 0.0).astype(bf16)
                    doh = jnp.where(mine, do2, 0.0).astype(bf16)
                    s = lax.dot_general(qh, k2, (((1,), (1,)), ((), ())), preferred_element_type=f32)
                    s = s + bias_scr[_bias_index(b, case, hh), 0:nq, 0:nk]
                    p = jnp.exp(s - lse2[:, lo:lo + 1])
                    dv2 = dv2 + lax.dot_general(p.astype(bf16), doh, (((0,), (0,)), ((), ())), preferred_element_type=f32)
                    dp = lax.dot_general(doh, v2, (((1,), (1,)), ((), ())), preferred_element_type=f32)
                    ds = (p * (dp - dl2[:, lo:lo + 1])).astype(bf16)
                    dq2 = dq2 + jnp.where(mine, jnp.dot(ds, k2, preferred_element_type=f32), 0.0)
                    dk2 = dk2 + lax.dot_general(ds, qh, (((0,), (0,)), ((), ())), preferred_element_type=f32)
                dq_scr[qrows, :] = dq_scr[qrows, :] + dq2
                dk_scr[krows, :] = dk_scr[krows, :] + dk2
                dv_scr[krows, :] = dv_scr[krows, :] + dv2
                return carry

            lax.fori_loop(0, S // nq, step, 0, unroll=min(ATTN_UNROLL, S // nq))

        n4 = S // QUAD
        for t in range(3):
            for rho in range(QUAD):
                token_rows = pl.ds(rho, n4, stride=QUAD)
                acc_scr[0, t, token_rows, :] = acc_scr[0, t, token_rows, :] + acc_scr[1, t, pl.ds(rho * n4, n4), :]
        dqkv_ref[0] = (acc_scr[0, 0] * SCALE).astype(bf16)
        dqkv_ref[1] = acc_scr[0, 1].astype(bf16)
        dqkv_ref[2] = acc_scr[0, 2].astype(bf16)

    blk = lambda part: pl.BlockSpec((S, PAIR), lambda p: (0, part * npair + p))
    own = pl.BlockSpec((S, PAIR), lambda p: (0, p))
    return pl.pallas_call(
        body, name="attn_bwd", out_shape=jax.ShapeDtypeStruct(dqkvz.shape, dqkvz.dtype), grid=(npair,),
        in_specs=[blk(4), blk(5), blk(6), own, own, own,
                  pl.BlockSpec((None,) + _bias_shape(S), lambda p: (p, 0, 0, 0)), ANY_SPEC, ANY_SPEC],
        out_specs=pl.BlockSpec((3, S, PAIR), lambda p: (0, 0, p)), input_output_aliases={7: 0},
        scratch_shapes=[pltpu.VMEM((2, 3, S, PAIR), f32), pltpu.VMEM((S, PAIR), f32), pltpu.VMEM((6, S, PAIR), f32)],
        compiler_params=_params("arbitrary"),
    )(proj, proj, proj, o, do, lse, bias, dqkvz, dep)


def _prenorm_bwd(dh, x, dout, scale, g_pre, tr=256):
    S, D = x.shape
    tr = min(tr, S)

    def body(dh_ref, x_ref, dout_ref, sc_ref, g_ref, gx_ref, sums_ref):
        i = pl.program_id(0)
        xv, dhv = x_ref[...], dh_ref[...]
        r = lax.rsqrt(jnp.mean(xv * xv, axis=-1, keepdims=True) + EPS)
        xn = xv * r
        dxn = dhv * (g_ref[...] * (1.0 + sc_ref[...]))
        gx_ref[...] = dout_ref[...] + r * (dxn - xn * jnp.mean(dxn * xn, axis=-1, keepdims=True))
        dhx = dhv * xn
        row = lax.broadcasted_iota(jnp.int32, (8, D), 0)
        upd = jnp.where(row == 0, jnp.sum(dhv, axis=0, keepdims=True),
                        jnp.where(row == 1, jnp.sum(dhx, axis=0, keepdims=True) * g_ref[...],
                                  jnp.where(row == 2, jnp.sum(dhx, axis=0, keepdims=True) * (1.0 + sc_ref[...]), 0.0)))

        @pl.when(i == 0)
        def _():
            sums_ref[...] = upd

        @pl.when(i > 0)
        def _():
            sums_ref[...] += upd

    row = pl.BlockSpec((tr, D), lambda i: (i, 0))
    vec = pl.BlockSpec((1, D), lambda i: (0, 0))
    return pl.pallas_call(
        body, name="prenorm_bwd",
        out_shape=(jax.ShapeDtypeStruct((S, D), f32), jax.ShapeDtypeStruct((8, D), f32)), grid=(S // tr,),
        in_specs=[row, row, row, vec, vec], out_specs=(row, pl.BlockSpec((8, D), lambda i: (0, 0))),
        compiler_params=_params("arbitrary"),
    )(dh, x, dout, scale, g_pre)


def _adamw(w, g, m, v):
    m = ADAM_B1 * m + (1.0 - ADAM_B1) * g
    v = ADAM_B2 * v + (1.0 - ADAM_B2) * (g * g)
    m_hat = m / (1.0 - ADAM_B1 ** ADAM_STEP)
    v_hat = v / (1.0 - ADAM_B2 ** ADAM_STEP)
    delta = -ADAM_LR * (m_hat / (jnp.sqrt(v_hat) + ADAM_EPS) + ADAM_WD * w)
    return delta, m, v


def _sum_rows(parts, dep):
    P = parts.shape[1]

    def body(p_ref, dep_ref, o_ref):
        acc = p_ref[0:1, :]
        for j in range(1, NDEV):
            acc = acc + p_ref[j:j + 1, :]
        o_ref[...] = jnp.broadcast_to(acc, (8, P))

    vmem = pl.BlockSpec(memory_space=pltpu.VMEM)
    return pl.pallas_call(body, name="sum_small", out_shape=jax.ShapeDtypeStruct((8, P), f32),
                          in_specs=[vmem, ANY_SPEC], out_specs=vmem, compiler_params=_params())(parts, dep)


def _adamw_small(tot, params):
    given = [p[3] for p in params if not isinstance(p[3], int)]

    def body(tot_ref, *refs):
        given_refs = list(refs[:len(given)])
        ins = refs[len(given):len(given) + 3 * len(params)]
        outs = refs[len(given) + 3 * len(params):]
        for t, (w, _, _, where) in enumerate(params):
            w_ref, m_ref, v_ref = ins[3 * t:3 * t + 3]
            g = tot_ref[0:1, where:where + w.size] if isinstance(where, int) else given_refs.pop(0)[...]
            outs[4 * t][...] = g
            outs[4 * t + 1][...], outs[4 * t + 2][...], outs[4 * t + 3][...] = _adamw(w_ref[...], g, m_ref[...], v_ref[...])

    out_shape = tuple(jax.ShapeDtypeStruct(p[0].shape, f32) for p in params for _ in range(4))
    res = pl.pallas_call(body, name="adamw_small", out_shape=out_shape, compiler_params=_params())(
        tot, *given, *[a for p in params for a in p[:3]])
    return [res[4 * t:4 * t + 4] for t in range(len(params))]


def _adamw_sharded(name, parts, sums_a, sums_b, pick, w, m, v, rows=None, prev=None, tr=128):
    R, Cc = w.shape
    r0, nr = rows or (0, R)
    tr = math.gcd(tr, r0, nr)
    n, b0 = parts.shape[0], r0 // tr

    def body(pick_ref, p_ref, a_ref, b_ref, w_ref, m_ref, v_ref, *rest):
        g_ref, d_ref, nm_ref, nv_ref = rest[-4:]
        g = jnp.where(pick_ref[0] == 1, b_ref[...], a_ref[...]).astype(f32)
        for j in range(n):
            g = g + p_ref[j].astype(f32)
        g_ref[...] = g
        d_ref[...], nm_ref[...], nv_ref[...] = _adamw(w_ref[...], g, m_ref[...], v_ref[...])

    row = pl.BlockSpec((tr, Cc), lambda i, pick: (i + b0, 0))
    mine = pl.BlockSpec((None, tr, Cc), lambda i, pick: (pick[1], i + b0, 0))
    out = jax.ShapeDtypeStruct((R, Cc), f32)
    prev = list(prev or [])
    grid_spec = pltpu.PrefetchScalarGridSpec(
        num_scalar_prefetch=1, grid=(nr // tr,),
        in_specs=[pl.BlockSpec((n, tr, Cc), lambda i, pick: (0, i + b0, 0)), mine, mine, row, row, row]
        + [ANY_SPEC] * len(prev),
        out_specs=(row, row, row, row))
    return pl.pallas_call(
        body, name=name, out_shape=(out, out, out, out), grid_spec=grid_spec,
        input_output_aliases={7 + t: t for t in range(len(prev))}, compiler_params=_params("arbitrary"),
    )(pick, parts, sums_a, sums_b, w, m, v, *prev)


def _adamw_ada(c_t, dmod_cols, w, m, v, dep, tr=512):
    D, W = w.shape
    tr = min(tr, D)

    def body(c_ref, dm_ref, w_ref, m_ref, v_ref, dep_ref, g_ref, d_ref, nm_ref, nv_ref):
        g = lax.dot_general(c_ref[...], dm_ref[...], (((1,), (0,)), ((), ())), preferred_element_type=f32,
                            precision=lax.Precision.HIGHEST)
        g_ref[...] = g
        d_ref[...], nm_ref[...], nv_ref[...] = _adamw(w_ref[...], g, m_ref[...], v_ref[...])

    row = pl.BlockSpec((tr, W), lambda i: (i, 0))
    out = jax.ShapeDtypeStruct((D, W), f32)
    return pl.pallas_call(
        body, name="adamw_ada", out_shape=(out, out, out, out), grid=(D // tr,),
        in_specs=[pl.BlockSpec((tr, NDEV), lambda i: (i, 0)), pl.BlockSpec((NDEV, W), lambda i: (0, 0)), row, row, row,
                  ANY_SPEC],
        out_specs=(row, row, row, row), compiler_params=_params("parallel"),
    )(c_t, dmod_cols, w, m, v, dep)


def kernel(x, c, w_ada, b_ada, g_pre, w_in, conv_w, conv_b, g_conv, g_attn, w_out, g_post, loss_target, m_w_ada, m_b_ada, m_g_pre, m_w_in, m_conv_w, m_conv_b, m_g_conv, m_g_attn, m_w_out, m_g_post, v_w_ada, v_b_ada, v_g_pre, v_w_in, v_conv_w, v_conv_b, v_g_conv, v_g_attn, v_w_out, v_g_post):
    S, D = x.shape[1], x.shape[2]
    C = D // 2
    W = w_ada.shape[2]
    CW = conv_w.shape[2]
    me = 4 * lax.axis_index("x") + 2 * lax.axis_index("y") + lax.axis_index("c")
    x2, tgt = x[0], loss_target[0]
    w_ada2, w_in2, w_out2 = w_ada[0], w_in[0], w_out[0]

    R = D // NDEV
    core = lax.axis_index("c").astype(jnp.int32).reshape(1)

    cw_slab = jnp.zeros((8, CW), f32).at[:3].set(conv_w[0])
    b_cols = lax.dynamic_slice_in_dim(b_ada, me * W, W, axis=1)
    mod_slabs, c_blocks, cw_g = _ada_exchange(c.reshape(D // 128, 128), cw_slab, w_ada2, b_cols)
    c_all = c_blocks.reshape(NDEV, D)
    conv_w_full = jnp.transpose(cw_g, (1, 0, 2)).reshape(8, C)
    mod = mod_slabs[:, 0, :].reshape(1, 3 * D)
    shift, scale, gate = mod[:, :D], mod[:, D:2 * D], mod[:, 2 * D:]

    me_arr = me.astype(jnp.int32).reshape(1)
    land_i = _cast_to_slot("w_in_cast", w_in2, me_arr)
    land_o = _cast_to_slot("w_out_cast", w_out2, me_arr)
    wi_send, wi_recv, land_i, w_token = _w_in_start(land_i, [mod_slabs])

    h = _prenorm(x2, scale, shift, g_pre, w_token)
    land_i = _w_in_sibling(land_i, wi_recv, after=[h])
    proj = _in_proj_part("in_proj_a", h, land_i, None, me_arr, 0, 1, 2)
    x_minus_t = _residual_minus_target(x2, tgt, proj)
    bias = _bias_tiles(_head_slopes(C // HEAD_DIM), S, x_minus_t)

    def landing(rows, cols):
        return lax.dynamic_update_slice(lax.empty((NCHIP, rows, cols), bf16), jnp.zeros((1, rows, cols), bf16),
                                        (me // 2, 0, 0))

    land_go, land_gi = landing(R, D), landing(D, C)
    fi_send, fi_recv, land_i = _w_in_relay(land_i, wi_recv, after=[proj, bias, land_go, land_gi])
    proj = _in_proj_part("in_proj_b", h, land_i, proj, me_arr, 2, 2, 2)
    land_i = _w_in_forwarded(land_i, fi_recv, after=[proj])
    proj = _in_proj_part("in_proj_c", h, land_i, proj, me_arr, 3, 2, 2)
    (di_send, di_recv, wo_send, wo_recv), land_i, land_o = _w_in_diag(land_i, land_o, fi_recv, after=[proj])
    proj = _in_proj_part("in_proj_d", h, land_i, proj, me_arr, 6, 1, 1)
    win_g = _w_in_finish(land_i, wi_send, fi_send, di_send, di_recv, after=[proj])
    proj = _in_proj_part("in_proj_e", h, win_g, proj, me_arr, 7, 1, 1)
    ycat = _conv_fwd(proj, conv_w_full, conv_b, g_conv)
    o, lse = _attn_fwd(proj, bias)
    (fo_send, fo_recv), (land_o,), _ = _weights_forward("w_out_forward", land_o, wo_recv, after=[o])
    ycat = _attn_post(ycat, o, proj, g_attn)
    wout_g = _weights_wait("w_out_wait", land_o, wo_send, wo_recv, fo_send, fo_recv, after=[ycat])
    wout_full = wout_g.reshape(D, D)
    y = _matmul(ycat, wout_full, name="out_proj", out_dtype=f32)
    dy, dout, post_sums = _sandwich(y, x_minus_t, gate, g_post)

    gw_out = _matmul(ycat, dy, name="out_proj_dw", out_dtype=bf16, ta=True).reshape(NDEV, R, D)
    dycat = _matmul(dy, wout_full, name="out_proj_dx", out_dtype=f32, tb=True)
    dpc, conv_sums = _conv_bwd(proj, dycat, conv_w_full, conv_b, g_conv, gw_out)
    gw_c = _matmul(h, dpc, name="in_proj_dw_conv", out_dtype=bf16, ta=True, out_slots=4)
    first_pairs, p1_token = _pairs_start("g_first_pair_start", [gw_out, gw_c])
    do, dpa, attn_sums = _attn_post_bwd(o, proj, dycat, g_attn, p1_token)
    (gw_out, pair_o), (gw_c, pair_c) = _pairs_wait("g_first_pair_wait", first_pairs, after=[do])
    sum_o = _pair_sum("g_out_pair_sum", gw_out, pair_o, core)
    sum_c = _pair_sum("g_conv_pair_sum", gw_c, pair_c, core)
    ((co_send, co_recv, sum_o, land_go), (cc_send, cc_recv, sum_c, land_gi)), cc_token = _chips_start(
        "g_first_chip_start", [(sum_o, land_go, 0), (sum_c, land_gi, 0)])
    dpa = _attn_bwd(proj, o, do, lse, bias, dpa, cc_token)
    gw_a = _matmul(h, dpa, name="in_proj_dw_attn", out_dtype=bf16, ta=True, b_slots=True, out_slots=4)
    pa_send, pa_recv, gw_a, pair_a, pa_token = _pair_start("g_attn_pair_start", gw_a)
    sum_o, land_go = _chip_wait("g_out_chip_wait", sum_o, land_go, co_send, co_recv, 0, after=[pa_token])
    pick_out = jnp.stack([jnp.int32(0), me // 2]).astype(jnp.int32)
    g_w_out, d_w_out, nm_w_out, nv_w_out = _adamw_sharded(
        "adamw_w_out", land_go, sum_o, sum_o, pick_out, w_out2, m_w_out[0], v_w_out[0])
    gw_a, pair_a = _pair_wait("g_attn_pair_wait", gw_a, pair_a, pa_send, pa_recv, after=[g_w_out])
    sum_a = _pair_sum("g_attn_pair_sum", gw_a, pair_a, core)
    part_a, part_b = (0, 3 * D // 4), (3 * D // 4, D // 4)
    ca_send, ca_recv, sum_a, land_gi, ca_token = _chip_start("g_attn_chip_start_a", sum_a, land_gi, 4, part_a)
    dh = _matmul_slabs_t(dpc, dpa, win_g, name="in_proj_dx", dep=ca_token)
    grad_x, pre_sums = _prenorm_bwd(dh, x2, dout, scale, g_pre)

    small = jnp.concatenate([pre_sums[0:1], pre_sums[1:2], post_sums[0:1],
                             pre_sums[2:3], post_sums[1:2],
                             conv_sums[2:3], conv_sums[3:4], conv_sums[4:5],
                             conv_sums[1:2], conv_sums[0:1], attn_sums[0:1]], axis=1)
    small = jnp.concatenate([small.reshape(8 * D // 128, 128), jnp.broadcast_to(post_sums[2:3, :128], (8, 128))])
    gs_send, gs_recv, small, small_all, gs_token = _gather_start("gather_small_start", small)

    cb_send, cb_recv, sum_a, land_gi, cb_token = _chip_start("g_attn_chip_start_b", sum_a, land_gi, 4, part_b,
                                                             after=[gs_token])

    pick_in = jnp.stack([me // 4, (me % 4) // 2]).astype(jnp.int32)
    sum_c, land_gi = _chip_wait("g_conv_chip_wait", sum_c, land_gi, cc_send, cc_recv, 0, after=[cb_token])
    sum_a, land_gi = _chip_wait("g_attn_chip_wait_a", sum_a, land_gi, ca_send, ca_recv, 4, [cb_token], part_a)
    first = _adamw_sharded("adamw_w_in_a", land_gi, sum_c, sum_a, pick_in, w_in2, m_w_in[0], v_w_in[0], rows=part_a,
                           tr=256)

    small_all = _gather_wait("gather_small_wait", small, small_all, gs_send, gs_recv, after=[first[0]])
    small_all = small_all.reshape(NDEV, small.size)
    tot = _sum_rows(small_all, gs_token)
    loss = tot[0, 8 * D]
    taps_first = lambda a: jnp.transpose(a, (1, 0, 2))
    g_conv_w = lax.dynamic_slice_in_dim(tot[0:1, 5 * D:5 * D + 3 * C].reshape(3, 1, C), me * CW, CW, axis=2)
    ((g_b_ada, d_b_ada, nm_b_ada, nv_b_ada), (g_g_pre, d_g_pre, nm_g_pre, nv_g_pre),
     (g_g_post, d_g_post, nm_g_post, nv_g_post), conv_w_results,
     (g_conv_b, d_conv_b, nm_conv_b, nv_conv_b), (g_g_conv, d_g_conv, nm_g_conv, nv_g_conv),
     (g_g_attn, d_g_attn, nm_g_attn, nv_g_attn)) = _adamw_small(tot, [
         (b_ada, m_b_ada, v_b_ada, 0), (g_pre, m_g_pre, v_g_pre, 3 * D), (g_post, m_g_post, v_g_post, 4 * D),
         (taps_first(conv_w), taps_first(m_conv_w), taps_first(v_conv_w), g_conv_w),
         (conv_b, m_conv_b, v_conv_b, 5 * D + 3 * C),
         (g_conv, m_g_conv, v_g_conv, 5 * D + 4 * C), (g_attn, m_g_attn, v_g_attn, 5 * D + 5 * C)])
    g_conv_w, d_conv_w, nm_conv_w, nv_conv_w = (taps_first(a) for a in conv_w_results)

    dmod_cols = lax.dynamic_slice_in_dim(small_all[:, :3 * D], me * W, W, axis=1)
    g_w_ada, d_w_ada, nm_w_ada, nv_w_ada = _adamw_ada(c_all.T, dmod_cols, w_ada2, m_w_ada[0], v_w_ada[0], gs_token)

    sum_a, land_gi = _chip_wait("g_attn_chip_wait_b", sum_a, land_gi, cb_send, cb_recv, 4, [g_w_ada], part_b)
    g_w_in, d_w_in, nm_w_in, nv_w_in = _adamw_sharded(
        "adamw_w_in_b", land_gi, sum_c, sum_a, pick_in, w_in2, m_w_in[0], v_w_in[0], rows=part_b, prev=first, tr=256)

    return (loss, grad_x[None],
            g_w_ada[None], g_b_ada, g_g_pre, g_w_in[None], g_conv_w, g_conv_b, g_g_conv, g_g_attn, g_w_out[None], g_g_post,
            d_w_ada[None], d_b_ada, d_g_pre, d_w_in[None], d_conv_w, d_conv_b, d_g_conv, d_g_attn, d_w_out[None], d_g_post,
            nm_w_ada[None], nm_b_ada, nm_g_pre, nm_w_in[None], nm_conv_w, nm_conv_b, nm_g_conv, nm_g_attn, nm_w_out[None], nm_g_post,
            nv_w_ada[None], nv_b_ada, nv_g_pre, nv_w_in[None], nv_conv_w, nv_conv_b, nv_g_conv, nv_g_attn, nv_w_out[None], nv_g_post)
```
